```python
import jax, jax.numpy as jnp
from jax import lax
import numpy as np

D_MODEL = 1024
BATCH = 8
SEQ = 4096
DEPTH = 4

N_MIXERS = 2
N_MLA_LAYERS = (DEPTH + 1) // 2
N_SGU_LAYERS = DEPTH // 2
MLA_HEADS = 8
QK_NOPE_DIM = 128
QK_ROPE_DIM = 64
QK_HEAD_DIM = QK_NOPE_DIM + QK_ROPE_DIM
V_HEAD_DIM = 128
Q_LORA_RANK = 256
KV_LORA_RANK = 128
ROPE_THETA = 10000.0
Q_BLOCK = 128
SGU_CHUNK = 128
SGU_WIDTH = 2 * D_MODEL
SGU_GROUPS = 8
SGU_GROUP_DIM = SGU_WIDTH // SGU_GROUPS
FFN_HIDDEN = 4 * D_MODEL
NORM_EPS = 1e-6
LN_EPS = 1e-5

kernel_name = "hybrid_mla_chunked_sgu_trunk"


def rms_norm(x, g):
    xf = x.astype(jnp.float32)
    y = xf * lax.rsqrt(jnp.mean(xf * xf, axis=-1, keepdims=True) + NORM_EPS)
    return (y * g.astype(jnp.float32)).astype(x.dtype)


def layer_norm(x, g, b):
    xf = x.astype(jnp.float32)
    mu = jnp.mean(xf, axis=-1, keepdims=True)
    var = jnp.mean(jnp.square(xf - mu), axis=-1, keepdims=True)
    y = (xf - mu) * lax.rsqrt(var + LN_EPS)
    return (y * g.astype(jnp.float32) + b.astype(jnp.float32)).astype(x.dtype)


def apply_rope(x, cos, sin):
    x1, x2 = jnp.split(x.astype(jnp.float32), 2, axis=-1)
    out = jnp.concatenate([x1 * cos - x2 * sin, x2 * cos + x1 * sin], axis=-1)
    return out.astype(x.dtype)


def mla_mixer(h, positions, w_dkv, q_norm, kv_norm, w_uq, w_ukv, w_o):
    B, S, _ = h.shape
    lat = h @ w_dkv
    c_q, c_kv, k_rope = jnp.split(lat, [Q_LORA_RANK, Q_LORA_RANK + KV_LORA_RANK], axis=-1)
    c_q = rms_norm(c_q, q_norm)
    c_kv = rms_norm(c_kv, kv_norm)
    q = (c_q @ w_uq).reshape(B, S, MLA_HEADS, QK_HEAD_DIM)
    q_nope, q_rope = jnp.split(q, [QK_NOPE_DIM], axis=-1)
    kv = (c_kv @ w_ukv).reshape(B, S, MLA_HEADS, QK_NOPE_DIM + V_HEAD_DIM)
    k_nope, v = jnp.split(kv, [QK_NOPE_DIM], axis=-1)

    inv_freq = ROPE_THETA ** (-jnp.arange(0, QK_ROPE_DIM, 2, dtype=jnp.float32) / QK_ROPE_DIM)
    ang = positions.astype(jnp.float32)[..., None] * inv_freq
    cos, sin = jnp.cos(ang), jnp.sin(ang)
    q_rope = apply_rope(q_rope, cos[:, :, None, :], sin[:, :, None, :])
    k_rope = apply_rope(k_rope, cos, sin)

    nb = S // Q_BLOCK
    qn_b = q_nope.reshape(B, nb, Q_BLOCK, MLA_HEADS, QK_NOPE_DIM).transpose(1, 0, 2, 3, 4)
    qr_b = q_rope.reshape(B, nb, Q_BLOCK, MLA_HEADS, QK_ROPE_DIM).transpose(1, 0, 2, 3, 4)
    key_idx = jnp.arange(S)
    scale = QK_HEAD_DIM ** -0.5

    def attend(args):
        qn, qr, blk = args
        s = (jnp.einsum('bqhd,bkhd->bhqk', qn, k_nope)
             + jnp.einsum('bqhr,bkr->bhqk', qr, k_rope))
        s = s.astype(jnp.float32) * scale
        q_idx = blk * Q_BLOCK + jnp.arange(Q_BLOCK)
        causal = key_idx[None, :] <= q_idx[:, None]
        s = jnp.where(causal[None, None], s, -jnp.inf)
        p = jax.nn.softmax(s, axis=-1).astype(v.dtype)
        return jnp.einsum('bhqk,bkhd->bqhd', p, v)

    o = lax.map(attend, (qn_b, qr_b, jnp.arange(nb)))
    o = o.transpose(1, 0, 2, 3, 4).reshape(B, S, MLA_HEADS * V_HEAD_DIM)
    return o @ w_o


def chunked_sgu_mixer(h, w_in, ln_g, ln_b, w_spatial, b_spatial, w_out):
    B, S, _ = h.shape
    z = jax.nn.gelu(h @ w_in, approximate=False)
    u, v = jnp.split(z, 2, axis=-1)
    v = layer_norm(v, ln_g, ln_b)
    nc = S // SGU_CHUNK
    vg = v.reshape(B, nc, SGU_CHUNK, SGU_GROUPS, SGU_GROUP_DIM)
    w_causal = jnp.tril(w_spatial)
    mixed = (jnp.einsum('gts,bcsgd->bctgd', w_causal, vg)
             + b_spatial.T[None, None, :, :, None])
    v = mixed.reshape(B, S, SGU_WIDTH)
    return (u * v) @ w_out


def sq_relu_mlp(h, w_up, w_down):
    return jnp.square(jax.nn.relu(h @ w_up)) @ w_down


def _fwd_setup_inputs(seed: int = 0) -> dict:
    key = jax.random.key(seed)
    ks = jax.random.split(key, 24)
    f32 = jnp.float32

    def nrm(k, shape, fan_in, gain=1.0):
        return jax.random.normal(k, shape, f32) * (gain * fan_in ** -0.5)

    def gain(k, shape):
        return 1.0 + 0.02 * jax.random.normal(k, shape, f32)

    x = jax.random.normal(ks[0], (BATCH, SEQ, D_MODEL), f32)
    offset = jax.random.randint(ks[1], (BATCH, 1), 0, 1024, dtype=jnp.int32)
    positions = offset + jnp.arange(SEQ, dtype=jnp.int32)[None, :]

    return {
        "x": x,
        "positions": positions,
        "norm_mix": gain(ks[2], (DEPTH, D_MODEL)),
        "norm_ffn": gain(ks[3], (DEPTH, D_MODEL)),
        "final_norm": gain(ks[4], (D_MODEL,)),
        "mla_w_dkv": nrm(ks[5], (N_MLA_LAYERS, D_MODEL, Q_LORA_RANK + KV_LORA_RANK + QK_ROPE_DIM), D_MODEL),
        "mla_q_norm": gain(ks[6], (N_MLA_LAYERS, Q_LORA_RANK)),
        "mla_kv_norm": gain(ks[7], (N_MLA_LAYERS, KV_LORA_RANK)),
        "mla_w_uq": nrm(ks[8], (N_MLA_LAYERS, Q_LORA_RANK, MLA_HEADS * QK_HEAD_DIM), Q_LORA_RANK),
        "mla_w_ukv": nrm(ks[9], (N_MLA_LAYERS, KV_LORA_RANK, MLA_HEADS * (QK_NOPE_DIM + V_HEAD_DIM)), KV_LORA_RANK),
        "mla_w_o": nrm(ks[10], (N_MLA_LAYERS, MLA_HEADS * V_HEAD_DIM, D_MODEL), MLA_HEADS * V_HEAD_DIM),
        "sgu_w_in": nrm(ks[11], (N_SGU_LAYERS, D_MODEL, 2 * SGU_WIDTH), D_MODEL),
        "sgu_ln_g": gain(ks[12], (N_SGU_LAYERS, SGU_WIDTH)),
        "sgu_ln_b": 0.02 * jax.random.normal(ks[13], (N_SGU_LAYERS, SGU_WIDTH), f32),
        "sgu_w_spatial": nrm(ks[14], (N_SGU_LAYERS, SGU_GROUPS, SGU_CHUNK, SGU_CHUNK), SGU_CHUNK, 0.5),
        "sgu_b_spatial": gain(ks[15], (N_SGU_LAYERS, SGU_GROUPS, SGU_CHUNK)),
        "sgu_w_out": nrm(ks[16], (N_SGU_LAYERS, SGU_WIDTH, D_MODEL), SGU_WIDTH),
        "ffn_w_up": nrm(ks[17], (DEPTH, D_MODEL, FFN_HIDDEN), D_MODEL),
        "ffn_w_down": nrm(ks[18], (DEPTH, FFN_HIDDEN, D_MODEL), FFN_HIDDEN),
    }


def _fwd_reference(x, positions, norm_mix, norm_ffn, final_norm,
              mla_w_dkv, mla_q_norm, mla_kv_norm, mla_w_uq, mla_w_ukv, mla_w_o,
              sgu_w_in, sgu_ln_g, sgu_ln_b, sgu_w_spatial, sgu_b_spatial, sgu_w_out,
              ffn_w_up, ffn_w_down):
    for i in range(DEPTH):
        h = rms_norm(x, norm_mix[i])
        j = i // N_MIXERS
        if i % N_MIXERS == 0:
            x = x + mla_mixer(h, positions, mla_w_dkv[j], mla_q_norm[j], mla_kv_norm[j],
                              mla_w_uq[j], mla_w_ukv[j], mla_w_o[j])
        else:
            x = x + chunked_sgu_mixer(h, sgu_w_in[j], sgu_ln_g[j], sgu_ln_b[j],
                                      sgu_w_spatial[j], sgu_b_spatial[j], sgu_w_out[j])
        h = rms_norm(x, norm_ffn[i])
        x = x + sq_relu_mlp(h, ffn_w_up[i], ffn_w_down[i])
    return rms_norm(x, final_norm)


import jax as _jax
import jax.numpy as _jnp

TWIN_FORMAT = 'train_step'
FWD_PARAMS = ['x', 'positions', 'norm_mix', 'norm_ffn', 'final_norm', 'mla_w_dkv', 'mla_q_norm', 'mla_kv_norm', 'mla_w_uq', 'mla_w_ukv', 'mla_w_o', 'sgu_w_in', 'sgu_ln_g', 'sgu_ln_b', 'sgu_w_spatial', 'sgu_b_spatial', 'sgu_w_out', 'ffn_w_up', 'ffn_w_down']
TWIN_WEIGHTS = ['norm_mix', 'norm_ffn', 'final_norm', 'mla_w_dkv', 'mla_q_norm', 'mla_kv_norm', 'mla_w_uq', 'mla_w_ukv', 'mla_w_o', 'sgu_w_in', 'sgu_ln_g', 'sgu_ln_b', 'sgu_w_spatial', 'sgu_b_spatial', 'sgu_w_out', 'ffn_w_up', 'ffn_w_down']
TWIN_DIFF_INPUT = 'x'
TWIN_INPUTS = ['x', 'positions', 'norm_mix', 'norm_ffn', 'final_norm', 'mla_w_dkv', 'mla_q_norm', 'mla_kv_norm', 'mla_w_uq', 'mla_w_ukv', 'mla_w_o', 'sgu_w_in', 'sgu_ln_g', 'sgu_ln_b', 'sgu_w_spatial', 'sgu_b_spatial', 'sgu_w_out', 'ffn_w_up', 'ffn_w_down', 'loss_target', 'm_norm_mix', 'm_norm_ffn', 'm_final_norm', 'm_mla_w_dkv', 'm_mla_q_norm', 'm_mla_kv_norm', 'm_mla_w_uq', 'm_mla_w_ukv', 'm_mla_w_o', 'm_sgu_w_in', 'm_sgu_ln_g', 'm_sgu_ln_b', 'm_sgu_w_spatial', 'm_sgu_b_spatial', 'm_sgu_w_out', 'm_ffn_w_up', 'm_ffn_w_down', 'v_norm_mix', 'v_norm_ffn', 'v_final_norm', 'v_mla_w_dkv', 'v_mla_q_norm', 'v_mla_kv_norm', 'v_mla_w_uq', 'v_mla_w_ukv', 'v_mla_w_o', 'v_sgu_w_in', 'v_sgu_ln_g', 'v_sgu_ln_b', 'v_sgu_w_spatial', 'v_sgu_b_spatial', 'v_sgu_w_out', 'v_ffn_w_up', 'v_ffn_w_down']
TWIN_OUTPUTS = ['loss', 'grad_x', 'grad_norm_mix', 'grad_norm_ffn', 'grad_final_norm', 'grad_mla_w_dkv', 'grad_mla_q_norm', 'grad_mla_kv_norm', 'grad_mla_w_uq', 'grad_mla_w_ukv', 'grad_mla_w_o', 'grad_sgu_w_in', 'grad_sgu_ln_g', 'grad_sgu_ln_b', 'grad_sgu_w_spatial', 'grad_sgu_b_spatial', 'grad_sgu_w_out', 'grad_ffn_w_up', 'grad_ffn_w_down', 'delta_norm_mix', 'delta_norm_ffn', 'delta_final_norm', 'delta_mla_w_dkv', 'delta_mla_q_norm', 'delta_mla_kv_norm', 'delta_mla_w_uq', 'delta_mla_w_ukv', 'delta_mla_w_o', 'delta_sgu_w_in', 'delta_sgu_ln_g', 'delta_sgu_ln_b', 'delta_sgu_w_spatial', 'delta_sgu_b_spatial', 'delta_sgu_w_out', 'delta_ffn_w_up', 'delta_ffn_w_down', 'new_m_norm_mix', 'new_m_norm_ffn', 'new_m_final_norm', 'new_m_mla_w_dkv', 'new_m_mla_q_norm', 'new_m_mla_kv_norm', 'new_m_mla_w_uq', 'new_m_mla_w_ukv', 'new_m_mla_w_o', 'new_m_sgu_w_in', 'new_m_sgu_ln_g', 'new_m_sgu_ln_b', 'new_m_sgu_w_spatial', 'new_m_sgu_b_spatial', 'new_m_sgu_w_out', 'new_m_ffn_w_up', 'new_m_ffn_w_down', 'new_v_norm_mix', 'new_v_norm_ffn', 'new_v_final_norm', 'new_v_mla_w_dkv', 'new_v_mla_q_norm', 'new_v_mla_kv_norm', 'new_v_mla_w_uq', 'new_v_mla_w_ukv', 'new_v_mla_w_o', 'new_v_sgu_w_in', 'new_v_sgu_ln_g', 'new_v_sgu_ln_b', 'new_v_sgu_w_spatial', 'new_v_sgu_b_spatial', 'new_v_sgu_w_out', 'new_v_ffn_w_up', 'new_v_ffn_w_down']
TWIN_LEAF_KINDS = {'loss': 'loss', 'grad_x': 'grad_x', 'grad_norm_mix': 'grad_w', 'grad_norm_ffn': 'grad_w', 'grad_final_norm': 'grad_w', 'grad_mla_w_dkv': 'grad_w', 'grad_mla_q_norm': 'grad_w', 'grad_mla_kv_norm': 'grad_w', 'grad_mla_w_uq': 'grad_w', 'grad_mla_w_ukv': 'grad_w', 'grad_mla_w_o': 'grad_w', 'grad_sgu_w_in': 'grad_w', 'grad_sgu_ln_g': 'grad_w', 'grad_sgu_ln_b': 'grad_w', 'grad_sgu_w_spatial': 'grad_w', 'grad_sgu_b_spatial': 'grad_w', 'grad_sgu_w_out': 'grad_w', 'grad_ffn_w_up': 'grad_w', 'grad_ffn_w_down': 'grad_w', 'delta_norm_mix': 'delta_w', 'delta_norm_ffn': 'delta_w', 'delta_final_norm': 'delta_w', 'delta_mla_w_dkv': 'delta_w', 'delta_mla_q_norm': 'delta_w', 'delta_mla_kv_norm': 'delta_w', 'delta_mla_w_uq': 'delta_w', 'delta_mla_w_ukv': 'delta_w', 'delta_mla_w_o': 'delta_w', 'delta_sgu_w_in': 'delta_w', 'delta_sgu_ln_g': 'delta_w', 'delta_sgu_ln_b': 'delta_w', 'delta_sgu_w_spatial': 'delta_w', 'delta_sgu_b_spatial': 'delta_w', 'delta_sgu_w_out': 'delta_w', 'delta_ffn_w_up': 'delta_w', 'delta_ffn_w_down': 'delta_w', 'new_m_norm_mix': 'new_m', 'new_m_norm_ffn': 'new_m', 'new_m_final_norm': 'new_m', 'new_m_mla_w_dkv': 'new_m', 'new_m_mla_q_norm': 'new_m', 'new_m_mla_kv_norm': 'new_m', 'new_m_mla_w_uq': 'new_m', 'new_m_mla_w_ukv': 'new_m', 'new_m_mla_w_o': 'new_m', 'new_m_sgu_w_in': 'new_m', 'new_m_sgu_ln_g': 'new_m', 'new_m_sgu_ln_b': 'new_m', 'new_m_sgu_w_spatial': 'new_m', 'new_m_sgu_b_spatial': 'new_m', 'new_m_sgu_w_out': 'new_m', 'new_m_ffn_w_up': 'new_m', 'new_m_ffn_w_down': 'new_m', 'new_v_norm_mix': 'new_v', 'new_v_norm_ffn': 'new_v', 'new_v_final_norm': 'new_v', 'new_v_mla_w_dkv': 'new_v', 'new_v_mla_q_norm': 'new_v', 'new_v_mla_kv_norm': 'new_v', 'new_v_mla_w_uq': 'new_v', 'new_v_mla_w_ukv': 'new_v', 'new_v_mla_w_o': 'new_v', 'new_v_sgu_w_in': 'new_v', 'new_v_sgu_ln_g': 'new_v', 'new_v_sgu_ln_b': 'new_v', 'new_v_sgu_w_spatial': 'new_v', 'new_v_sgu_b_spatial': 'new_v', 'new_v_sgu_w_out': 'new_v', 'new_v_ffn_w_up': 'new_v', 'new_v_ffn_w_down': 'new_v'}


def _forward(args):
    return _fwd_reference(*[args[k] for k in FWD_PARAMS])


def _output_shape():
    def fwd():
        inp = _fwd_setup_inputs(0)
        return _fwd_reference(*[inp[k] for k in FWD_PARAMS])
    out = _jax.eval_shape(fwd)
    return out.shape, out.dtype

N_MICROBATCH = 1
ADAM_LR = 0.001
ADAM_B1 = 0.9
ADAM_B2 = 0.999
ADAM_EPS = 1e-08
ADAM_WD = 0.01
ADAM_STEP = 10
PER_EXAMPLE_BATCH_AXIS = {'x': 0, 'positions': 0, 'loss_target': 0}
SHARED_INPUTS = []
_WEIGHT_DTYPES = {'norm_mix': _jnp.float32, 'norm_ffn': _jnp.float32, 'final_norm': _jnp.float32, 'mla_w_dkv': _jnp.float32, 'mla_q_norm': _jnp.float32, 'mla_kv_norm': _jnp.float32, 'mla_w_uq': _jnp.float32, 'mla_w_ukv': _jnp.float32, 'mla_w_o': _jnp.float32, 'sgu_w_in': _jnp.float32, 'sgu_ln_g': _jnp.float32, 'sgu_ln_b': _jnp.float32, 'sgu_w_spatial': _jnp.float32, 'sgu_b_spatial': _jnp.float32, 'sgu_w_out': _jnp.float32, 'ffn_w_up': _jnp.float32, 'ffn_w_down': _jnp.float32}
MOMENT_SCALE = {'norm_mix': 6.858138e-02, 'norm_ffn': 1.398832e-01, 'final_norm': 3.374961e+01, 'mla_w_dkv': 9.056148e-02, 'mla_q_norm': 5.754625e-02, 'mla_kv_norm': 1.499460e-01, 'mla_w_uq': 2.386956e-02, 'mla_w_ukv': 3.838065e-02, 'mla_w_o': 4.920052e-02, 'sgu_w_in': 4.121727e-02, 'sgu_ln_g': 1.548235e-02, 'sgu_ln_b': 1.487150e-02, 'sgu_w_spatial': 4.372329e-02, 'sgu_b_spatial': 6.110035e-02, 'sgu_w_out': 9.821652e-02, 'ffn_w_up': 7.068752e-02, 'ffn_w_down': 1.607675e-01}


def _to_microbatches(a, axis):
    t = _jnp.moveaxis(a, axis, 0)
    t = t.reshape((N_MICROBATCH, t.shape[0] // N_MICROBATCH) + t.shape[1:])
    return _jnp.moveaxis(t, 1, axis + 1)


def setup_inputs(seed: int = 0) -> dict:
    inp = _fwd_setup_inputs(seed)
    key = _jax.random.fold_in(_jax.random.key(seed), 7919)
    shape, _ = _output_shape()
    out = dict(inp)
    out["loss_target"] = _jax.random.normal(_jax.random.fold_in(key, 0), shape, _jnp.float32)
    for i, name in enumerate(TWIN_WEIGHTS):
        w = inp[name].astype(_jnp.float32)
        if MOMENT_SCALE is None:
            s = _jnp.sqrt(_jnp.mean(_jnp.square(w)) + 1e-30)
        else:
            s = MOMENT_SCALE[name]
        km, kv = _jax.random.split(_jax.random.fold_in(key, i + 1))
        out[name] = w
        out["m_" + name] = s * _jax.random.normal(km, w.shape, _jnp.float32)
        out["v_" + name] = (s * s) * _jax.random.uniform(kv, w.shape, _jnp.float32, 0.5, 1.5)
    if N_MICROBATCH > 1:
        for name, axis in PER_EXAMPLE_BATCH_AXIS.items():
            out[name] = _to_microbatches(out[name], axis)
    return {'x': out['x'], 'positions': out['positions'], 'norm_mix': out['norm_mix'], 'norm_ffn': out['norm_ffn'], 'final_norm': out['final_norm'], 'mla_w_dkv': out['mla_w_dkv'], 'mla_q_norm': out['mla_q_norm'], 'mla_kv_norm': out['mla_kv_norm'], 'mla_w_uq': out['mla_w_uq'], 'mla_w_ukv': out['mla_w_ukv'], 'mla_w_o': out['mla_w_o'], 'sgu_w_in': out['sgu_w_in'], 'sgu_ln_g': out['sgu_ln_g'], 'sgu_ln_b': out['sgu_ln_b'], 'sgu_w_spatial': out['sgu_w_spatial'], 'sgu_b_spatial': out['sgu_b_spatial'], 'sgu_w_out': out['sgu_w_out'], 'ffn_w_up': out['ffn_w_up'], 'ffn_w_down': out['ffn_w_down'], 'loss_target': out['loss_target'], 'm_norm_mix': out['m_norm_mix'], 'm_norm_ffn': out['m_norm_ffn'], 'm_final_norm': out['m_final_norm'], 'm_mla_w_dkv': out['m_mla_w_dkv'], 'm_mla_q_norm': out['m_mla_q_norm'], 'm_mla_kv_norm': out['m_mla_kv_norm'], 'm_mla_w_uq': out['m_mla_w_uq'], 'm_mla_w_ukv': out['m_mla_w_ukv'], 'm_mla_w_o': out['m_mla_w_o'], 'm_sgu_w_in': out['m_sgu_w_in'], 'm_sgu_ln_g': out['m_sgu_ln_g'], 'm_sgu_ln_b': out['m_sgu_ln_b'], 'm_sgu_w_spatial': out['m_sgu_w_spatial'], 'm_sgu_b_spatial': out['m_sgu_b_spatial'], 'm_sgu_w_out': out['m_sgu_w_out'], 'm_ffn_w_up': out['m_ffn_w_up'], 'm_ffn_w_down': out['m_ffn_w_down'], 'v_norm_mix': out['v_norm_mix'], 'v_norm_ffn': out['v_norm_ffn'], 'v_final_norm': out['v_final_norm'], 'v_mla_w_dkv': out['v_mla_w_dkv'], 'v_mla_q_norm': out['v_mla_q_norm'], 'v_mla_kv_norm': out['v_mla_kv_norm'], 'v_mla_w_uq': out['v_mla_w_uq'], 'v_mla_w_ukv': out['v_mla_w_ukv'], 'v_mla_w_o': out['v_mla_w_o'], 'v_sgu_w_in': out['v_sgu_w_in'], 'v_sgu_ln_g': out['v_sgu_ln_g'], 'v_sgu_ln_b': out['v_sgu_ln_b'], 'v_sgu_w_spatial': out['v_sgu_w_spatial'], 'v_sgu_b_spatial': out['v_sgu_b_spatial'], 'v_sgu_w_out': out['v_sgu_w_out'], 'v_ffn_w_up': out['v_ffn_w_up'], 'v_ffn_w_down': out['v_ffn_w_down']}


def _loss(weights, diff, rest, loss_target):
    with _jax.named_scope("forward"):
        args = {**rest, TWIN_DIFF_INPUT: diff, **{k: w.astype(_WEIGHT_DTYPES[k]) for k, w in weights.items()}}
        y = _forward(args)
    with _jax.named_scope("loss_head"):
        err = _jnp.square(y.astype(_jnp.float32) - loss_target)
        return 0.5 * _jnp.sum(_jnp.mean(err, axis=-1)) if err.ndim else 0.5 * err


def _adamw(w, g, m, v):
    m = ADAM_B1 * m + (1.0 - ADAM_B1) * g
    v = ADAM_B2 * v + (1.0 - ADAM_B2) * _jnp.square(g)
    m_hat = m / (1.0 - ADAM_B1 ** ADAM_STEP)
    v_hat = v / (1.0 - ADAM_B2 ** ADAM_STEP)
    delta = -ADAM_LR * (m_hat / (_jnp.sqrt(v_hat) + ADAM_EPS) + ADAM_WD * w)
    return delta, m, v


def reference(x, positions, norm_mix, norm_ffn, final_norm, mla_w_dkv, mla_q_norm, mla_kv_norm, mla_w_uq, mla_w_ukv, mla_w_o, sgu_w_in, sgu_ln_g, sgu_ln_b, sgu_w_spatial, sgu_b_spatial, sgu_w_out, ffn_w_up, ffn_w_down, loss_target, m_norm_mix, m_norm_ffn, m_final_norm, m_mla_w_dkv, m_mla_q_norm, m_mla_kv_norm, m_mla_w_uq, m_mla_w_ukv, m_mla_w_o, m_sgu_w_in, m_sgu_ln_g, m_sgu_ln_b, m_sgu_w_spatial, m_sgu_b_spatial, m_sgu_w_out, m_ffn_w_up, m_ffn_w_down, v_norm_mix, v_norm_ffn, v_final_norm, v_mla_w_dkv, v_mla_q_norm, v_mla_kv_norm, v_mla_w_uq, v_mla_w_ukv, v_mla_w_o, v_sgu_w_in, v_sgu_ln_g, v_sgu_ln_b, v_sgu_w_spatial, v_sgu_b_spatial, v_sgu_w_out, v_ffn_w_up, v_ffn_w_down):
    given = dict(x=x, positions=positions, norm_mix=norm_mix, norm_ffn=norm_ffn, final_norm=final_norm, mla_w_dkv=mla_w_dkv, mla_q_norm=mla_q_norm, mla_kv_norm=mla_kv_norm, mla_w_uq=mla_w_uq, mla_w_ukv=mla_w_ukv, mla_w_o=mla_w_o, sgu_w_in=sgu_w_in, sgu_ln_g=sgu_ln_g, sgu_ln_b=sgu_ln_b, sgu_w_spatial=sgu_w_spatial, sgu_b_spatial=sgu_b_spatial, sgu_w_out=sgu_w_out, ffn_w_up=ffn_w_up, ffn_w_down=ffn_w_down, loss_target=loss_target, m_norm_mix=m_norm_mix, m_norm_ffn=m_norm_ffn, m_final_norm=m_final_norm, m_mla_w_dkv=m_mla_w_dkv, m_mla_q_norm=m_mla_q_norm, m_mla_kv_norm=m_mla_kv_norm, m_mla_w_uq=m_mla_w_uq, m_mla_w_ukv=m_mla_w_ukv, m_mla_w_o=m_mla_w_o, m_sgu_w_in=m_sgu_w_in, m_sgu_ln_g=m_sgu_ln_g, m_sgu_ln_b=m_sgu_ln_b, m_sgu_w_spatial=m_sgu_w_spatial, m_sgu_b_spatial=m_sgu_b_spatial, m_sgu_w_out=m_sgu_w_out, m_ffn_w_up=m_ffn_w_up, m_ffn_w_down=m_ffn_w_down, v_norm_mix=v_norm_mix, v_norm_ffn=v_norm_ffn, v_final_norm=v_final_norm, v_mla_w_dkv=v_mla_w_dkv, v_mla_q_norm=v_mla_q_norm, v_mla_kv_norm=v_mla_kv_norm, v_mla_w_uq=v_mla_w_uq, v_mla_w_ukv=v_mla_w_ukv, v_mla_w_o=v_mla_w_o, v_sgu_w_in=v_sgu_w_in, v_sgu_ln_g=v_sgu_ln_g, v_sgu_ln_b=v_sgu_ln_b, v_sgu_w_spatial=v_sgu_w_spatial, v_sgu_b_spatial=v_sgu_b_spatial, v_sgu_w_out=v_sgu_w_out, v_ffn_w_up=v_ffn_w_up, v_ffn_w_down=v_ffn_w_down)
    weights = {n: given[n] for n in TWIN_WEIGHTS}
    shared = {n: given[n] for n in SHARED_INPUTS}
    per_example = {n: given[n] for n in ['x', 'positions']}
    grad_fn = _jax.value_and_grad(_loss, argnums=(0, 1))

    def one_microbatch(ex, loss_target):
        ex = dict(ex)
        diff = ex.pop(TWIN_DIFF_INPUT)
        return grad_fn(weights, diff, {**shared, **ex}, loss_target)

    if N_MICROBATCH == 1:
        loss, (grad_w, grad_x) = one_microbatch(per_example, given["loss_target"])
    else:
        def body(carry, xs):
            loss_sum, grad_sum = carry
            l_k, (gw_k, gx_k) = one_microbatch(xs[0], xs[1])
            with _jax.named_scope("update"):
                return (loss_sum + l_k, _jax.tree.map(_jnp.add, grad_sum, gw_k)), gx_k

        init = (_jnp.zeros((), _jnp.float32), _jax.tree.map(_jnp.zeros_like, weights))
        (loss, grad_w), grad_x = _jax.lax.scan(body, init, (per_example, given["loss_target"]))
    with _jax.named_scope("update"):
        delta_w, new_m, new_v = {}, {}, {}
        for n in TWIN_WEIGHTS:
            delta_w[n], new_m[n], new_v[n] = _adamw(weights[n], grad_w[n], given["m_" + n], given["v_" + n])
    return (loss, grad_x, *[grad_w[n] for n in TWIN_WEIGHTS], *[delta_w[n] for n in TWIN_WEIGHTS],
            *[new_m[n] for n in TWIN_WEIGHTS], *[new_v[n] for n in TWIN_WEIGHTS])
```

```python
import functools
import math

import jax
import jax.numpy as jnp
from jax import lax
from jax.experimental import pallas as pl
from jax.experimental.pallas import tpu as pltpu

F32 = jnp.float32
BF16 = jnp.bfloat16
MESH = pl.DeviceIdType.MESH

N_DEV = 8
N_CHIP = 4
HEADS = 8
NOPE = 128
ROPE = 64
VDIM = 128
QPAD = 256
Q_RANK = 256
KV_RANK = 128
LAT = Q_RANK + KV_RANK + ROPE
LAT_PAD = 512
ROPE_THETA = 10000.0
SGU_CHUNK = 128
SGU_GROUPS = 8
NORM_EPS = 1e-6
LN_EPS = 1e-5
ADAM_LR = 0.001
ADAM_B1 = 0.9
ADAM_B2 = 0.999
ADAM_EPS = 1e-08
ADAM_WD = 0.01
ADAM_STEP = 10
ATTN_SCALE = (NOPE + ROPE) ** -0.5
NEG = -1e30
VMEM_LIMIT = 56 * 1024 * 1024
SMALL_ROWS = 256

NN = (((1,), (0,)), ((), ()))
NT = (((1,), (1,)), ((), ()))
TN = (((0,), (0,)), ((), ()))


def _pcall(body, **kw):
    return pl.pallas_call(body, **kw)


def _params(n_grid):
    return pltpu.CompilerParams(dimension_semantics=("arbitrary",) * n_grid, vmem_limit_bytes=VMEM_LIMIT)


def _sds(shape, dtype):
    return jax.ShapeDtypeStruct(tuple(shape), dtype)


def _tile(n, want):
    t = min(n, want)
    assert n % t == 0, (n, want)
    return t


def _matmul(name, a, b, extras, *, grid, a_spec, b_spec, extra_specs, out_shapes, out_specs, dims, k_axis=None, nk=1,
            acc_shape=None, epilogue=None, aliases=None):
    n_extra = len(extras)
    n_out = len(out_shapes)
    n_alias = 0 if aliases is None else len(aliases[0])

    def body(*refs):
        a_ref, b_ref = refs[0], refs[1]
        ex = refs[2:2 + n_extra]
        outs = refs[2 + n_extra + n_alias:2 + n_extra + n_alias + n_out]
        prod = lax.dot_general(a_ref[...], b_ref[...], dims, preferred_element_type=F32)

        def finish(acc):
            res = epilogue(acc, *[e[...] for e in ex]) if epilogue is not None else (acc,)
            for o, r in zip(outs, res):
                o[...] = r.astype(o.dtype)

        if k_axis is None:
            finish(prod)
        else:
            acc_ref = refs[-1]
            k = pl.program_id(k_axis)

            @pl.when(k == 0)
            def _():
                acc_ref[...] = prod

            @pl.when(k > 0)
            def _():
                acc_ref[...] += prod

            @pl.when(k == nk - 1)
            def _():
                finish(acc_ref[...])

    in_specs = [a_spec, b_spec, *extra_specs]
    operands = [a, b, *extras]
    io_alias = {}
    if aliases is not None:
        for j, (buf, out_idx) in enumerate(zip(*aliases)):
            io_alias[len(operands)] = out_idx
            operands.append(buf)
            in_specs.append(pl.BlockSpec(memory_space=pl.ANY))
    scratch = [] if k_axis is None else [pltpu.VMEM(acc_shape, F32)]
    res = _pcall(body, name=name, grid=grid, in_specs=in_specs, out_specs=out_specs, out_shape=out_shapes,
                 scratch_shapes=scratch, input_output_aliases=io_alias, compiler_params=_params(len(grid)))(*operands)
    return res


def _rowwise(name, fn, operands, *, grid, in_specs, out_shapes, out_specs, n_acc=0, grid_spec_prefetch=None):
    n_in = len(operands)
    n_out = len(out_shapes)
    n_pre = 0 if grid_spec_prefetch is None else 1

    def body(*refs):
        refs = refs[n_pre:]
        ins = refs[:n_in]
        outs = refs[n_in:n_in + n_out]
        res = fn(*[r[...] for r in ins])
        if not isinstance(res, (tuple, list)):
            res = (res,)
        first = pl.program_id(0) == 0
        for d in range(1, len(grid)):
            first = jnp.logical_and(first, pl.program_id(d) == 0)
        for idx, (o, r) in enumerate(zip(outs, res)):
            if idx < n_out - n_acc:
                o[...] = r.astype(o.dtype)
            else:
                @pl.when(first)
                def _(o=o, r=r):
                    o[...] = r.astype(o.dtype)

                @pl.when(jnp.logical_not(first))
                def _(o=o, r=r):
                    o[...] += r.astype(o.dtype)

    if grid_spec_prefetch is None:
        return _pcall(body, name=name, grid=grid, in_specs=in_specs, out_specs=out_specs, out_shape=out_shapes,
                      compiler_params=_params(len(grid)))(*operands)
    gs = pltpu.PrefetchScalarGridSpec(num_scalar_prefetch=1, grid=grid, in_specs=in_specs, out_specs=out_specs)
    return _pcall(body, name=name, grid_spec=gs, out_shape=out_shapes,
                  compiler_params=_params(len(grid)))(grid_spec_prefetch, *operands)


def _row_spec(tm, w):
    return pl.BlockSpec((tm, w), lambda i: (i, 0))


def _const_spec(shape):
    nd = len(shape)
    return pl.BlockSpec(tuple(shape), lambda *_: (0,) * nd)


def _rms_fwd(x, g):
    r = lax.rsqrt(jnp.mean(x * x, axis=-1, keepdims=True) + NORM_EPS)
    return x * r * g


def _rms_bwd(x, g, dy):
    r = lax.rsqrt(jnp.mean(x * x, axis=-1, keepdims=True) + NORM_EPS)
    xh = x * r
    u = dy * g
    dx = r * (u - xh * jnp.mean(u * xh, axis=-1, keepdims=True))
    dg = jnp.sum(dy * xh, axis=0, keepdims=True)
    return dx, dg


def _gelu(z):
    return 0.5 * z * (1.0 + lax.erf(z * (2.0 ** -0.5)))


def _gelu_grad(z):
    return 0.5 * (1.0 + lax.erf(z * (2.0 ** -0.5))) + z * jnp.exp(-0.5 * z * z) * ((2.0 * math.pi) ** -0.5)


def _rope_fwd(x, cc, sa, sb):
    return x * cc + pltpu.roll(x, 96, 1) * sa + pltpu.roll(x, 32, 1) * sb


def _rope_bwd(d, cc, sa, sb):
    return d * cc + pltpu.roll(d * sa, 32, 1) + pltpu.roll(d * sb, 96, 1)


def _adam(w, g, m, v):
    m = ADAM_B1 * m + (1.0 - ADAM_B1) * g
    v = ADAM_B2 * v + (1.0 - ADAM_B2) * (g * g)
    m_hat = m / (1.0 - ADAM_B1 ** ADAM_STEP)
    v_hat = v / (1.0 - ADAM_B2 ** ADAM_STEP)
    delta = -ADAM_LR * (m_hat / (jnp.sqrt(v_hat) + ADAM_EPS) + ADAM_WD * w)
    return delta, m, v


def _place():
    return lax.axis_index("x"), lax.axis_index("y"), lax.axis_index("c")


def _all_gather(name, arrays):
    n = len(arrays)

    def body(*refs):
        ins = refs[:n]
        outs = refs[n:2 * n]
        send_sems, recv_sems, local_sems = refs[2 * n:]
        x, y, c = _place()
        me, sibling = (x, y, c), (x, y, 1 - c)
        chips = [(1 - x, y), (x, 1 - y), (1 - x, 1 - y)]

        def slot(a, dev):
            return outs[a].at[4 * dev[0] + 2 * dev[1] + dev[2]]

        def copy(a, k, block, to, src=None):
            return pltpu.make_async_remote_copy(
                src_ref=slot(a, block) if src is None else src, dst_ref=slot(a, block),
                send_sem=send_sems.at[a, k], recv_sem=recv_sems.at[a, k], device_id=to, device_id_type=MESH)

        mine = [pltpu.make_async_copy(ins[a], slot(a, me), local_sems.at[a]) for a in range(n)]
        for cp in mine:
            cp.start()
        first = []
        for j, chip in enumerate(chips):
            first += [copy(a, 1 + j, me, (*chip, c), src=ins[a]) for a in range(n)]
        first += [copy(a, 0, me, sibling, src=ins[a]) for a in range(n)]
        for cp in first:
            cp.start()
        passed = []
        for j, chip in enumerate(chips):
            for a in range(n):
                copy(a, 1 + j, (*chip, c), me).wait_recv()
                fwd = copy(a, 4 + j, (*chip, c), sibling)
                fwd.start()
                passed.append(fwd)
        for a in range(n):
            copy(a, 0, sibling, me).wait_recv()
            for j, chip in enumerate(chips):
                copy(a, 4 + j, (*chip, 1 - c), me).wait_recv()
        for cp in first + passed:
            cp.wait_send()
        for cp in mine:
            cp.wait()

    any_spec = pl.BlockSpec(memory_space=pl.ANY)
    return _pcall(
        body, name=name, in_specs=[any_spec] * n, out_specs=[any_spec] * n,
        out_shape=[_sds((N_DEV, *a.shape), a.dtype) for a in arrays],
        scratch_shapes=[pltpu.SemaphoreType.DMA((n, 7)), pltpu.SemaphoreType.DMA((n, 7)), pltpu.SemaphoreType.DMA((n,))],
        compiler_params=pltpu.CompilerParams(has_side_effects=True),
    )(*arrays)


def _sibling_exchange(name, grads):
    n = len(grads)

    def body(*refs):
        ins = refs[:n]
        outs = refs[n:2 * n]
        send_sems, recv_sems = refs[2 * n:]
        x, y, c = _place()
        sibling = (x, y, 1 - c)
        started = []
        for a in range(n):
            for l in range(grads[a].shape[0]):
                for ch in range(N_CHIP):
                    cp = pltpu.make_async_remote_copy(
                        src_ref=ins[a].at[l, 2 * ch + 1 - c], dst_ref=outs[a].at[l, ch],
                        send_sem=send_sems.at[a], recv_sem=recv_sems.at[a], device_id=sibling, device_id_type=MESH)
                    cp.start()
        for a in range(n):
            pltpu.make_async_remote_copy(src_ref=outs[a], dst_ref=outs[a], send_sem=send_sems.at[a], recv_sem=recv_sems.at[a],
                                         device_id=sibling, device_id_type=MESH).wait()

    any_spec = pl.BlockSpec(memory_space=pl.ANY)
    return _pcall(
        body, name=name, in_specs=[any_spec] * n, out_specs=[any_spec] * n,
        out_shape=[_sds((g.shape[0], N_CHIP, *g.shape[2:]), g.dtype) for g in grads],
        scratch_shapes=[pltpu.SemaphoreType.DMA((n,)), pltpu.SemaphoreType.DMA((n,))],
        compiler_params=pltpu.CompilerParams(has_side_effects=True),
    )(*grads)


def _chip_exchange(name, parts):
    n = len(parts)

    def body(*refs):
        ins = refs[:n]
        outs = refs[n:2 * n]
        send_sems, recv_sems, local_sems = refs[2 * n:]
        x, y, c = _place()
        mine = 2 * x + y
        chips = [(1 - x, y), (x, 1 - y), (1 - x, 1 - y)]
        for a in range(n):
            for l in range(parts[a].shape[0]):
                pltpu.make_async_copy(ins[a].at[l, mine], outs[a].at[mine, l], local_sems.at[a]).start()
        for j, chip in enumerate(chips):
            for a in range(n):
                for l in range(parts[a].shape[0]):
                    pltpu.make_async_remote_copy(
                        src_ref=ins[a].at[l, 2 * chip[0] + chip[1]], dst_ref=outs[a].at[mine, l],
                        send_sem=send_sems.at[a, j], recv_sem=recv_sems.at[a, j], device_id=(*chip, c), device_id_type=MESH).start()
        for j, chip in enumerate(chips):
            for a in range(n):
                theirs = outs[a].at[2 * chip[0] + chip[1]]
                pltpu.make_async_remote_copy(src_ref=theirs, dst_ref=theirs, send_sem=send_sems.at[a, j], recv_sem=recv_sems.at[a, j],
                                             device_id=(*chip, c), device_id_type=MESH).wait()
        for a in range(n):
            pltpu.make_async_copy(outs[a].at[mine], outs[a].at[mine], local_sems.at[a]).wait()

    any_spec = pl.BlockSpec(memory_space=pl.ANY)
    return _pcall(
        body, name=name, in_specs=[any_spec] * n, out_specs=[any_spec] * n,
        out_shape=[_sds((N_CHIP, p.shape[0], *p.shape[2:]), p.dtype) for p in parts],
        scratch_shapes=[pltpu.SemaphoreType.DMA((n, 3)), pltpu.SemaphoreType.DMA((n, 3)), pltpu.SemaphoreType.DMA((n,))],
        compiler_params=pltpu.CompilerParams(has_side_effects=True),
    )(*parts)


def _flash_fwd(q, k, v, tq):
    h, t, _ = q.shape
    nq = t // tq

    def body(q_ref, k_ref, v_ref, o_ref, lse_ref, m_ref, l_ref, acc_ref):
        qi = pl.program_id(1)
        qb = q_ref[...]
        m_ref[...] = jnp.full((tq, 1), NEG, F32)
        l_ref[...] = jnp.zeros((tq, 1), F32)
        acc_ref[...] = jnp.zeros((tq, VDIM), F32)

        def step(kj, masked):
            k0 = pl.multiple_of(kj * tq, tq)
            kb = k_ref[pl.ds(k0, tq), :]
            vb = v_ref[pl.ds(k0, tq), :]
            s = lax.dot_general(qb, kb, NT, preferred_element_type=F32) * ATTN_SCALE
            if masked:
                row = lax.broadcasted_iota(jnp.int32, (tq, tq), 0)
                col = lax.broadcasted_iota(jnp.int32, (tq, tq), 1)
                s = jnp.where(col <= row, s, NEG)
            m_old = m_ref[...]
            m_new = jnp.maximum(m_old, jnp.max(s, axis=-1, keepdims=True))
            alpha = jnp.exp(m_old - m_new)
            p = jnp.exp(s - m_new)
            l_ref[...] = alpha * l_ref[...] + jnp.sum(p, axis=-1, keepdims=True)
            acc_ref[...] = alpha * acc_ref[...] + lax.dot_general(p.astype(BF16), vb, NN, preferred_element_type=F32)
            m_ref[...] = m_new

        def loop_body(kj, carry):
            step(kj, False)
            return carry

        lax.fori_loop(0, qi, loop_body, 0)
        step(qi, True)
        l = l_ref[...]
        o_ref[...] = (acc_ref[...] / l).astype(o_ref.dtype)
        lse_ref[...] = m_ref[...] + jnp.log(l)

    return _pcall(
        body, name="flash_fwd", grid=(h, nq),
        in_specs=[pl.BlockSpec((None, tq, QPAD), lambda hh, i: (hh, i, 0)),
                  pl.BlockSpec((None, t, QPAD), lambda hh, i: (hh, 0, 0)),
                  pl.BlockSpec((None, t, VDIM), lambda hh, i: (hh, 0, 0))],
        out_specs=[pl.BlockSpec((tq, VDIM), lambda hh, i: (i, hh)),
                   pl.BlockSpec((None, tq, 1), lambda hh, i: (hh, i, 0))],
        out_shape=[_sds((t, h * VDIM), BF16), _sds((h, t, 1), F32)],
        scratch_shapes=[pltpu.VMEM((tq, 1), F32), pltpu.VMEM((tq, 1), F32), pltpu.VMEM((tq, VDIM), F32)],
        compiler_params=_params(2),
    )(q, k, v)


def _flash_bwd(q, k, v, o, do, lse, tq):
    h, t, _ = q.shape
    nq = t // tq

    def body(q_ref, k_ref, v_ref, o_ref, do_ref, lse_ref, dq_ref, dk_ref, dv_ref):
        kj = pl.program_id(1)

        @pl.when(kj == 0)
        def _():
            dq_ref[...] = jnp.zeros_like(dq_ref)

        kb = k_ref[...]
        vb = v_ref[...]
        dk_ref[...] = jnp.zeros_like(dk_ref)
        dv_ref[...] = jnp.zeros_like(dv_ref)

        def step(qi, masked):
            q0 = pl.multiple_of(qi * tq, tq)
            rows = pl.ds(q0, tq)
            qb = q_ref[rows, :]
            dob = do_ref[rows, :]
            ob = o_ref[rows, :]
            s = lax.dot_general(qb, kb, NT, preferred_element_type=F32) * ATTN_SCALE
            p = jnp.exp(s - lse_ref[rows, :])
            if masked:
                row = lax.broadcasted_iota(jnp.int32, (tq, tq), 0)
                col = lax.broadcasted_iota(jnp.int32, (tq, tq), 1)
                p = jnp.where(col <= row, p, 0.0)
            pb = p.astype(BF16)
            dv_ref[...] += lax.dot_general(pb, dob, TN, preferred_element_type=F32)
            dp = lax.dot_general(dob, vb, NT, preferred_element_type=F32)
            delta = jnp.sum(dob.astype(F32) * ob.astype(F32), axis=-1, keepdims=True)
            ds = (p * (dp - delta) * ATTN_SCALE).astype(BF16)
            dk_ref[...] += lax.dot_general(ds, qb, TN, preferred_element_type=F32)
            dq_ref[rows, :] += lax.dot_general(ds, kb, NN, preferred_element_type=F32)

        step(kj, True)

        def loop_body(qi, carry):
            step(qi, False)
            return carry

        lax.fori_loop(kj + 1, nq, loop_body, 0)

    return _pcall(
        body, name="flash_bwd", grid=(h, nq),
        in_specs=[pl.BlockSpec((None, t, QPAD), lambda hh, j: (hh, 0, 0)),
                  pl.BlockSpec((None, tq, QPAD), lambda hh, j: (hh, j, 0)),
                  pl.BlockSpec((None, tq, VDIM), lambda hh, j: (hh, j, 0)),
                  pl.BlockSpec((t, VDIM), lambda hh, j: (0, hh)),
                  pl.BlockSpec((t, VDIM), lambda hh, j: (0, hh)),
                  pl.BlockSpec((None, t, 1), lambda hh, j: (hh, 0, 0))],
        out_specs=[pl.BlockSpec((None, t, QPAD), lambda hh, j: (hh, 0, 0)),
                   pl.BlockSpec((None, tq, QPAD), lambda hh, j: (hh, j, 0)),
                   pl.BlockSpec((None, tq, VDIM), lambda hh, j: (hh, j, 0))],
        out_shape=[_sds((h, t, QPAD), F32), _sds((h, t, QPAD), F32), _sds((h, t, VDIM), F32)],
        compiler_params=_params(2),
    )(q, k, v, o, do, lse)


def _tril_bf16(w):
    row = lax.broadcasted_iota(jnp.int32, w.shape, 0)
    col = lax.broadcasted_iota(jnp.int32, w.shape, 1)
    return jnp.where(col <= row, w, 0.0).astype(BF16)


def _layer_norm_parts(v0):
    mu = jnp.mean(v0, axis=-1, keepdims=True)
    vc = v0 - mu
    rstd = lax.rsqrt(jnp.mean(vc * vc, axis=-1, keepdims=True) + LN_EPS)
    return vc * rstd, rstd


def _sgu_mid_fwd(ge, ln_g, ln_b, w_sp, b_sp, chunks_per_step):
    t, e2 = ge.shape
    e = e2 // 2
    gd = e // SGU_GROUPS
    rows = SGU_CHUNK * chunks_per_step

    def body(u_ref, v_ref, g_ref, b_ref, w_ref, bs_ref, gate_ref):
        for ck in range(chunks_per_step):
            r = pl.ds(ck * SGU_CHUNK, SGU_CHUNK)
            xh, _ = _layer_norm_parts(v_ref[r, :].astype(F32))
            v1 = (xh * g_ref[...] + b_ref[...]).astype(BF16)
            for g in range(SGU_GROUPS):
                cols = pl.ds(g * gd, gd)
                mixed = lax.dot_general(_tril_bf16(w_ref[g]), v1[:, g * gd:(g + 1) * gd], NN, preferred_element_type=F32) + bs_ref[g]
                gate_ref[r, cols] = (u_ref[r, cols].astype(F32) * mixed).astype(BF16)

    return _pcall(
        body, name="sgu_mid_fwd", grid=(t // rows,),
        in_specs=[pl.BlockSpec((rows, e), lambda i: (i, 0)), pl.BlockSpec((rows, e), lambda i: (i, 1)),
                  _const_spec((1, e)), _const_spec((1, e)), _const_spec(w_sp.shape), _const_spec(b_sp.shape)],
        out_specs=pl.BlockSpec((rows, e), lambda i: (i, 0)),
        out_shape=_sds((t, e), BF16), compiler_params=_params(1),
    )(ge, ge, ln_g, ln_b, w_sp, b_sp)


def _sgu_mid_bwd(ge, z, dgate, ln_g, ln_b, w_sp, b_sp, chunks_per_step):
    t, e2 = ge.shape
    e = e2 // 2
    gd = e // SGU_GROUPS
    rows = SGU_CHUNK * chunks_per_step

    def body(u_ref, v_ref, zu_ref, zv_ref, dg_ref, g_ref, b_ref, w_ref, bs_ref, dz_ref, dw_ref, dbs_ref, dlg_ref, dlb_ref):
        @pl.when(pl.program_id(0) == 0)
        def _():
            dw_ref[...] = jnp.zeros_like(dw_ref)
            dbs_ref[...] = jnp.zeros_like(dbs_ref)
            dlg_ref[...] = jnp.zeros_like(dlg_ref)
            dlb_ref[...] = jnp.zeros_like(dlb_ref)

        for ck in range(chunks_per_step):
            r = pl.ds(ck * SGU_CHUNK, SGU_CHUNK)
            xh, rstd = _layer_norm_parts(v_ref[r, :].astype(F32))
            v1 = (xh * g_ref[...] + b_ref[...]).astype(BF16)
            dv1_parts = []
            for g in range(SGU_GROUPS):
                cols = pl.ds(g * gd, gd)
                wc = _tril_bf16(w_ref[g])
                v1g = v1[:, g * gd:(g + 1) * gd]
                mixed = lax.dot_general(wc, v1g, NN, preferred_element_type=F32) + bs_ref[g]
                dgate = dg_ref[r, cols].astype(F32)
                dmixed = dgate * u_ref[r, cols].astype(F32)
                du = dgate * mixed
                dz_ref[r, cols] = (du * _gelu_grad(zu_ref[r, cols].astype(F32))).astype(BF16)
                dbs_ref[g] += jnp.sum(dmixed, axis=1, keepdims=True)
                dmb = dmixed.astype(BF16)
                dwg = lax.dot_general(dmb, v1g, NT, preferred_element_type=F32)
                row = lax.broadcasted_iota(jnp.int32, dwg.shape, 0)
                col = lax.broadcasted_iota(jnp.int32, dwg.shape, 1)
                dw_ref[g] += jnp.where(col <= row, dwg, 0.0)
                dv1_parts.append(lax.dot_general(wc, dmb, TN, preferred_element_type=F32))
            dv1 = jnp.concatenate(dv1_parts, axis=1)
            dlg_ref[...] += jnp.sum(dv1 * xh, axis=0, keepdims=True)
            dlb_ref[...] += jnp.sum(dv1, axis=0, keepdims=True)
            dxh = dv1 * g_ref[...]
            dv0 = rstd * (dxh - jnp.mean(dxh, axis=-1, keepdims=True) - xh * jnp.mean(dxh * xh, axis=-1, keepdims=True))
            dz_ref[r, pl.ds(e, e)] = (dv0 * _gelu_grad(zv_ref[r, :].astype(F32))).astype(BF16)

    half0 = pl.BlockSpec((rows, e), lambda i: (i, 0))
    half1 = pl.BlockSpec((rows, e), lambda i: (i, 1))
    return _pcall(
        body, name="sgu_mid_bwd", grid=(t // rows,),
        in_specs=[half0, half1, half0, half1, half0, _const_spec((1, e)), _const_spec((1, e)), _const_spec(w_sp.shape), _const_spec(b_sp.shape)],
        out_specs=[pl.BlockSpec((rows, e2), lambda i: (i, 0)), _const_spec(w_sp.shape), _const_spec(b_sp.shape), _const_spec((1, e)), _const_spec((1, e))],
        out_shape=[_sds((t, e2), BF16), _sds(w_sp.shape, F32), _sds(b_sp.shape, F32), _sds((1, e), F32), _sds((1, e), F32)],
        compiler_params=_params(1),
    )(ge, ge, z, z, dgate, ln_g, ln_b, w_sp, b_sp)


def kernel(x, positions, norm_mix, norm_ffn, final_norm, mla_w_dkv, mla_q_norm, mla_kv_norm, mla_w_uq, mla_w_ukv, mla_w_o, sgu_w_in, sgu_ln_g, sgu_ln_b, sgu_w_spatial, sgu_b_spatial, sgu_w_out, ffn_w_up, ffn_w_down, loss_target, m_norm_mix, m_norm_ffn, m_final_norm, m_mla_w_dkv, m_mla_q_norm, m_mla_kv_norm, m_mla_w_uq, m_mla_w_ukv, m_mla_w_o, m_sgu_w_in, m_sgu_ln_g, m_sgu_ln_b, m_sgu_w_spatial, m_sgu_b_spatial, m_sgu_w_out, m_ffn_w_up, m_ffn_w_down, v_norm_mix, v_norm_ffn, v_final_norm, v_mla_w_dkv, v_mla_q_norm, v_mla_kv_norm, v_mla_w_uq, v_mla_w_ukv, v_mla_w_o, v_sgu_w_in, v_sgu_ln_g, v_sgu_ln_b, v_sgu_w_spatial, v_sgu_b_spatial, v_sgu_w_out, v_ffn_w_up, v_ffn_w_down):
    _, T, D = x.shape
    depth = norm_mix.shape[0]
    n_mla, n_sgu = mla_w_dkv.shape[0], sgu_w_in.shape[0]
    FF = ffn_w_up.shape[2] * N_DEV
    E = sgu_w_out.shape[1] * N_DEV
    ffc, ec, e2c = FF // N_DEV, E // N_DEV, 2 * E // N_DEV
    dc = D // N_DEV
    OW = HEADS * VDIM
    owc = OW // N_DEV
    tm = _tile(T, 1024)
    tq = _tile(T, 512)
    ts = _tile(T, 256)
    nt = T // tm
    x2 = x.reshape(T, D)
    tgt = loss_target.reshape(T, D)
    cidx = lax.axis_index("c").astype(jnp.int32).reshape(1)

    ln_local = jnp.concatenate([sgu_ln_g, sgu_ln_b, jnp.zeros((8 - 2 * n_sgu, ec), F32)], axis=0)
    shards = [w.astype(BF16) for w in (mla_w_dkv, mla_w_uq, mla_w_ukv, mla_w_o, sgu_w_in, sgu_w_out, ffn_w_up, ffn_w_down)]
    g_dkv, g_uq, g_ukv, g_o, g_in, g_out, g_up, g_down, g_ln = _all_gather("gather_weights", shards + [ln_local])
    w_dkv = jnp.pad(g_dkv.transpose(1, 0, 2, 3).reshape(n_mla, D, LAT), ((0, 0), (0, 0), (0, LAT_PAD - LAT)))
    w_uq = jnp.pad(g_uq.transpose(1, 0, 2, 3), ((0, 0), (0, 0), (0, 0), (0, QPAD - NOPE - ROPE)))
    w_ukv = g_ukv.transpose(1, 0, 2, 3)
    w_o = g_o.transpose(1, 0, 2, 3).reshape(n_mla, OW, D)
    ln_g_full = [g_ln[:, l, :].reshape(1, E) for l in range(n_sgu)]
    ln_b_full = [g_ln[:, n_sgu + l, :].reshape(1, E) for l in range(n_sgu)]
    b_sp = sgu_b_spatial.reshape(n_sgu, SGU_GROUPS, SGU_CHUNK, 1)

    inv_freq = ROPE_THETA ** (-jnp.arange(0, ROPE, 2, dtype=F32) / ROPE)
    zeros32 = jnp.zeros((ROPE // 2,), F32)
    inv128 = jnp.concatenate([inv_freq, inv_freq, zeros32, zeros32]).reshape(1, 128)
    sel_a = jnp.concatenate([-jnp.ones((32,), F32), zeros32, zeros32, zeros32]).reshape(1, 128)
    sel_b = jnp.concatenate([zeros32, jnp.ones((32,), F32), zeros32, zeros32]).reshape(1, 128)
    sel_c = jnp.concatenate([jnp.ones((64,), F32), zeros32, zeros32]).reshape(1, 128)

    def rope_tables(pos, inv, sa, sb, sc):
        ang = pos.astype(F32) * inv
        cs, sn = jnp.cos(ang), jnp.sin(ang)
        return cs * sc, sn * sa, sn * sb

    t_cc, t_sa, t_sb = _rowwise(
        "rope_tables", rope_tables, [positions.reshape(T, 1), inv128, sel_a, sel_b, sel_c], grid=(nt,),
        in_specs=[_row_spec(tm, 1)] + [_const_spec((1, 128))] * 4,
        out_shapes=[_sds((T, 128), F32)] * 3, out_specs=[_row_spec(tm, 128)] * 3)
    tab_specs = [_row_spec(tm, 128)] * 3

    def gbuf(w):
        return lax.empty((w.shape[0], N_DEV, *w.shape[1:]), BF16)

    G = {"dkv": lax.empty((n_mla, D, LAT_PAD), BF16), "uq": lax.empty((n_mla, HEADS, Q_RANK, QPAD), BF16), "ukv": gbuf(mla_w_ukv),
         "o": lax.empty((n_mla, OW, D), BF16), "in": gbuf(sgu_w_in), "out": lax.empty((n_sgu, E, D), BF16),
         "up": gbuf(ffn_w_up), "down": lax.empty((depth, FF, D), BF16)}

    def rmsnorm(xv, g):
        return _rowwise("rmsnorm", lambda a, gg: _rms_fwd(a, gg), [xv, g.reshape(1, D)], grid=(nt,),
                        in_specs=[_row_spec(tm, D), _const_spec((1, D))], out_shapes=_sds((T, D), BF16), out_specs=_row_spec(tm, D))

    def rmsnorm_bwd(xv, g, dh, dx_in):
        def fn(a, gg, d, dxi):
            dx, dg = _rms_bwd(a, gg, d)
            return dxi + dx, dg
        return _rowwise("rmsnorm_bwd", fn, [xv, g.reshape(1, D), dh, dx_in], grid=(nt,),
                        in_specs=[_row_spec(tm, D), _const_spec((1, D)), _row_spec(tm, D), _row_spec(tm, D)],
                        out_shapes=[_sds((T, D), F32), _sds((1, D), F32)], out_specs=[_row_spec(tm, D), _const_spec((1, D))], n_acc=1)

    def proj_cols(name, h, gw, l, nc, epilogue, n_out):
        n = nc * N_DEV
        return _matmul(name, h, gw, [], grid=(N_DEV, nt),
                       a_spec=pl.BlockSpec((tm, D), lambda j, i: (i, 0)),
                       b_spec=pl.BlockSpec((None, None, D, nc), lambda j, i: (j, l, 0, 0)), extra_specs=[],
                       out_shapes=[_sds((T, n), BF16)] * n_out, out_specs=[pl.BlockSpec((tm, nc), lambda j, i: (i, j))] * n_out,
                       dims=NN, epilogue=epilogue)

    def proj_rows_residual(name, a, gw, l, kc, xres):
        return _matmul(name, a, gw, [xres], grid=(nt, N_DEV),
                       a_spec=pl.BlockSpec((tm, kc), lambda i, k: (i, k)),
                       b_spec=pl.BlockSpec((None, None, kc, D), lambda i, k: (k, l, 0, 0)),
                       extra_specs=[pl.BlockSpec((tm, D), lambda i, k: (i, 0))],
                       out_shapes=[_sds((T, D), F32)], out_specs=[pl.BlockSpec((tm, D), lambda i, k: (i, 0))],
                       dims=NN, k_axis=1, nk=N_DEV, acc_shape=(tm, D), epilogue=lambda acc, xr: (acc + xr,))[0]

    def back_rows(name, dy, gw, l, kc, extras, epilogue):
        return _matmul(name, dy, gw, extras, grid=(N_DEV, nt),
                       a_spec=pl.BlockSpec((tm, D), lambda j, i: (i, 0)),
                       b_spec=pl.BlockSpec((None, None, kc, D), lambda j, i: (j, l, 0, 0)),
                       extra_specs=[pl.BlockSpec((tm, kc), lambda j, i: (i, j))] * len(extras),
                       out_shapes=[_sds((T, kc * N_DEV), BF16)], out_specs=[pl.BlockSpec((tm, kc), lambda j, i: (i, j))],
                       dims=NT, epilogue=epilogue)[0]

    def back_cols(name, da, gw, l, nc):
        return _matmul(name, da, gw, [], grid=(nt, N_DEV),
                       a_spec=pl.BlockSpec((tm, nc), lambda i, k: (i, k)),
                       b_spec=pl.BlockSpec((None, None, D, nc), lambda i, k: (k, l, 0, 0)), extra_specs=[],
                       out_shapes=[_sds((T, D), F32)], out_specs=[pl.BlockSpec((tm, D), lambda i, k: (i, 0))],
                       dims=NT, k_axis=1, nk=N_DEV, acc_shape=(tm, D))[0]

    def wgrad_cols(name, h, da, key, l, nc):
        G[key] = _matmul(name, h, da, [], grid=(N_DEV, nt),
                         a_spec=pl.BlockSpec((tm, D), lambda j, t: (t, 0)), b_spec=pl.BlockSpec((tm, nc), lambda j, t: (t, j)),
                         extra_specs=[], out_shapes=[_sds(G[key].shape, BF16)],
                         out_specs=[pl.BlockSpec((None, None, D, nc), lambda j, t: (l, j, 0, 0))],
                         dims=TN, k_axis=1, nk=nt, acc_shape=(D, nc), aliases=([G[key]], [0]))[0]

    def wgrad_rows(name, a, dy, key, l, kc, ncols):
        G[key] = _matmul(name, a, dy, [], grid=(a.shape[1] // kc, nt),
                         a_spec=pl.BlockSpec((tm, kc), lambda j, t: (t, j)), b_spec=pl.BlockSpec((tm, ncols), lambda j, t: (t, 0)),
                         extra_specs=[], out_shapes=[_sds(G[key].shape, BF16)],
                         out_specs=[pl.BlockSpec((None, kc, ncols), lambda j, t: (l, j, 0))],
                         dims=TN, k_axis=1, nk=nt, acc_shape=(kc, ncols), aliases=([G[key]], [0]))[0]

    saved = []
    xs = x2
    for i in range(depth):
        l = i // 2
        h = rmsnorm(xs, norm_mix[i])
        if i % 2 == 0:
            lat = _matmul("mla_down", h, w_dkv, [], grid=(nt,), a_spec=_row_spec(tm, D),
                          b_spec=pl.BlockSpec((None, D, LAT_PAD), lambda i_: (l, 0, 0)), extra_specs=[],
                          out_shapes=[_sds((T, LAT_PAD), F32)], out_specs=[_row_spec(tm, LAT_PAD)], dims=NN)[0]

            def latent_post(la, qn, kvn, cc, sa, sb):
                cq = _rms_fwd(la[:, :Q_RANK], qn)
                ckv = _rms_fwd(la[:, Q_RANK:Q_RANK + KV_RANK], kvn)
                kr = _rope_fwd(la[:, Q_RANK + KV_RANK:], cc, sa, sb)
                return cq, ckv, kr

            cq, ckv, kr = _rowwise(
                "mla_latent", latent_post, [lat, mla_q_norm[l].reshape(1, Q_RANK), mla_kv_norm[l].reshape(1, KV_RANK), t_cc, t_sa, t_sb],
                grid=(nt,), in_specs=[_row_spec(tm, LAT_PAD), _const_spec((1, Q_RANK)), _const_spec((1, KV_RANK))] + tab_specs,
                out_shapes=[_sds((T, Q_RANK), BF16), _sds((T, KV_RANK), BF16), _sds((T, 128), BF16)],
                out_specs=[_row_spec(tm, Q_RANK), _row_spec(tm, KV_RANK), _row_spec(tm, 128)])

            def q_epilogue(acc, cc, sa, sb):
                return (jnp.concatenate([acc[:, :NOPE], _rope_fwd(acc[:, NOPE:], cc, sa, sb)], axis=1),)

            q = _matmul("mla_q", cq, w_uq, [t_cc, t_sa, t_sb], grid=(HEADS, nt),
                        a_spec=pl.BlockSpec((tm, Q_RANK), lambda b, i_: (i_, 0)),
                        b_spec=pl.BlockSpec((None, None, Q_RANK, QPAD), lambda b, i_: (l, b, 0, 0)),
                        extra_specs=[pl.BlockSpec((tm, 128), lambda b, i_: (i_, 0))] * 3,
                        out_shapes=[_sds((HEADS, T, QPAD), BF16)], out_specs=[pl.BlockSpec((None, tm, QPAD), lambda b, i_: (b, i_, 0))],
                        dims=NN, epilogue=q_epilogue)[0]

            def kv_epilogue(acc, krb):
                return jnp.concatenate([acc[:, :NOPE], krb.astype(F32)], axis=1), acc[:, NOPE:]

            kk, vv = _matmul("mla_kv", ckv, w_ukv, [kr], grid=(HEADS, nt),
                             a_spec=pl.BlockSpec((tm, KV_RANK), lambda b, i_: (i_, 0)),
                             b_spec=pl.BlockSpec((None, None, KV_RANK, NOPE + VDIM), lambda b, i_: (l, b, 0, 0)),
                             extra_specs=[pl.BlockSpec((tm, 128), lambda b, i_: (i_, 0))],
                             out_shapes=[_sds((HEADS, T, QPAD), BF16), _sds((HEADS, T, VDIM), BF16)],
                             out_specs=[pl.BlockSpec((None, tm, QPAD), lambda b, i_: (b, i_, 0)), pl.BlockSpec((None, tm, VDIM), lambda b, i_: (b, i_, 0))],
                             dims=NN, epilogue=kv_epilogue)
            o, lse = _flash_fwd(q, kk, vv, tq)
            xm = _matmul("mla_out", o, w_o, [xs], grid=(nt,), a_spec=_row_spec(tm, OW),
                         b_spec=pl.BlockSpec((None, OW, D), lambda i_: (l, 0, 0)), extra_specs=[_row_spec(tm, D)],
                         out_shapes=[_sds((T, D), F32)], out_specs=[_row_spec(tm, D)], dims=NN, epilogue=lambda acc, xr: (acc + xr,))[0]
            mix_saved = (h, lat, cq, ckv, q, kk, vv, o, lse)
        else:
            z, ge = proj_cols("sgu_in", h, g_in, l, e2c, lambda acc: (acc, _gelu(acc)), 2)
            gate = _sgu_mid_fwd(ge, ln_g_full[l], ln_b_full[l], sgu_w_spatial[l], b_sp[l], 4)
            xm = proj_rows_residual("sgu_out", gate, g_out, l, ec, xs)
            mix_saved = (h, z, ge, gate)
        h2 = rmsnorm(xm, norm_ffn[i])
        r, s = proj_cols("ffn_up", h2, g_up, i, ffc, lambda acc: (jnp.maximum(acc, 0.0), jnp.square(jnp.maximum(acc, 0.0))), 2)
        xo = proj_rows_residual("ffn_down", s, g_down, i, ffc, xm)
        saved.append((xs, xm, mix_saved, h2, r, s))
        xs = xo

    def loss_head(xv, tg, g):
        y = _rms_fwd(xv, g)
        err = y - tg
        part = 0.5 * jnp.sum(jnp.sum(err * err, axis=-1, keepdims=True), axis=0, keepdims=True) / D
        dx, dg = _rms_bwd(xv, g, err / D)
        return dx, jnp.broadcast_to(part, (1, 128)), dg

    dx, loss_part, d_final = _rowwise(
        "loss_head", loss_head, [xs, tgt, final_norm.reshape(1, D)], grid=(nt,),
        in_specs=[_row_spec(tm, D), _row_spec(tm, D), _const_spec((1, D))],
        out_shapes=[_sds((T, D), F32), _sds((1, 128), F32), _sds((1, D), F32)],
        out_specs=[_row_spec(tm, D), _const_spec((1, 128)), _const_spec((1, D))], n_acc=2)
    loss = lax.psum(loss_part[0, 0], ("x", "y", "c"))

    d_norm_mix, d_norm_ffn = [None] * depth, [None] * depth
    d_qn, d_kvn = [None] * n_mla, [None] * n_mla
    d_wsp, d_bsp, d_lng, d_lnb = [None] * n_sgu, [None] * n_sgu, [None] * n_sgu, [None] * n_sgu

    def to_bf16(a):
        return _rowwise("to_bf16", lambda v_: v_, [a], grid=(nt,), in_specs=[_row_spec(tm, a.shape[1])],
                        out_shapes=_sds(a.shape, BF16), out_specs=_row_spec(tm, a.shape[1]))

    for i in reversed(range(depth)):
        l = i // 2
        xs_i, xm, mix_saved, h2, r, s = saved[i]
        dyb = to_bf16(dx)
        da = back_rows("ffn_down_bwd", dyb, g_down, i, ffc, [r], lambda acc, rr: (acc * (2.0 * rr.astype(F32)),))
        wgrad_rows("ffn_down_wgrad", s, dyb, "down", i, ffc, D)
        wgrad_cols("ffn_up_wgrad", h2, da, "up", i, ffc)
        dh2 = back_cols("ffn_up_bwd", da, g_up, i, ffc)
        dx, d_norm_ffn[i] = rmsnorm_bwd(xm, norm_ffn[i], dh2, dx)
        dyb = to_bf16(dx)
        if i % 2 == 0:
            h, lat, cq, ckv, q, kk, vv, o, lse = mix_saved
            do = _matmul("mla_out_bwd", dyb, w_o, [], grid=(nt,), a_spec=_row_spec(tm, D),
                         b_spec=pl.BlockSpec((None, OW, D), lambda i_: (l, 0, 0)), extra_specs=[],
                         out_shapes=[_sds((T, OW), BF16)], out_specs=[_row_spec(tm, OW)], dims=NT)[0]
            wgrad_rows("mla_out_wgrad", o, dyb, "o", l, OW, D)
            dq, dk, dv = _flash_bwd(q, kk, vv, o, do, lse, tq)

            def q_pre(d, cc, sa, sb):
                return jnp.concatenate([d[:, :NOPE], _rope_bwd(d[:, NOPE:], cc, sa, sb)], axis=1)

            dq_pre = _rowwise("mla_dq_rope", q_pre, [dq.reshape(HEADS * T, QPAD), t_cc, t_sa, t_sb], grid=(HEADS * nt,),
                              in_specs=[_row_spec(tm, QPAD)] + [pl.BlockSpec((tm, 128), lambda i_: (i_ % nt, 0))] * 3,
                              out_shapes=_sds((HEADS * T, QPAD), BF16), out_specs=_row_spec(tm, QPAD)).reshape(HEADS, T, QPAD)

            def kv_pre(dkb, dvb, cc, sa, sb):
                dkv = jnp.concatenate([dkb[:, :, :NOPE], dvb], axis=2)
                dkr = _rope_bwd(jnp.sum(dkb[:, :, NOPE:], axis=0), cc, sa, sb)
                return dkv, dkr

            dkv, dkr = _rowwise("mla_dkv_rope", kv_pre, [dk, dv, t_cc, t_sa, t_sb], grid=(T // ts,),
                                in_specs=[pl.BlockSpec((HEADS, ts, QPAD), lambda i_: (0, i_, 0)), pl.BlockSpec((HEADS, ts, VDIM), lambda i_: (0, i_, 0))] + [_row_spec(ts, 128)] * 3,
                                out_shapes=[_sds((HEADS, T, NOPE + VDIM), BF16), _sds((T, 128), F32)],
                                out_specs=[pl.BlockSpec((HEADS, ts, NOPE + VDIM), lambda i_: (0, i_, 0)), _row_spec(ts, 128)])
            G["uq"] = _matmul("mla_q_wgrad", cq, dq_pre, [], grid=(HEADS, nt),
                              a_spec=pl.BlockSpec((tm, Q_RANK), lambda b, t_: (t_, 0)), b_spec=pl.BlockSpec((None, tm, QPAD), lambda b, t_: (b, t_, 0)),
                              extra_specs=[], out_shapes=[_sds(G["uq"].shape, BF16)],
                              out_specs=[pl.BlockSpec((None, None, Q_RANK, QPAD), lambda b, t_: (l, b, 0, 0))],
                              dims=TN, k_axis=1, nk=nt, acc_shape=(Q_RANK, QPAD), aliases=([G["uq"]], [0]))[0]
            G["ukv"] = _matmul("mla_kv_wgrad", ckv, dkv, [], grid=(HEADS, nt),
                               a_spec=pl.BlockSpec((tm, KV_RANK), lambda b, t_: (t_, 0)), b_spec=pl.BlockSpec((None, tm, NOPE + VDIM), lambda b, t_: (b, t_, 0)),
                               extra_specs=[], out_shapes=[_sds(G["ukv"].shape, BF16)],
                               out_specs=[pl.BlockSpec((None, None, KV_RANK, NOPE + VDIM), lambda b, t_: (l, b, 0, 0))],
                               dims=TN, k_axis=1, nk=nt, acc_shape=(KV_RANK, NOPE + VDIM), aliases=([G["ukv"]], [0]))[0]
            dcq = _matmul("mla_q_bwd", dq_pre, w_uq, [], grid=(nt, HEADS),
                          a_spec=pl.BlockSpec((None, tm, QPAD), lambda i_, b: (b, i_, 0)),
                          b_spec=pl.BlockSpec((None, None, Q_RANK, QPAD), lambda i_, b: (l, b, 0, 0)), extra_specs=[],
                          out_shapes=[_sds((T, Q_RANK), F32)], out_specs=[pl.BlockSpec((tm, Q_RANK), lambda i_, b: (i_, 0))],
                          dims=NT, k_axis=1, nk=HEADS, acc_shape=(tm, Q_RANK))[0]
            dckv = _matmul("mla_kv_bwd", dkv, w_ukv, [], grid=(nt, HEADS),
                           a_spec=pl.BlockSpec((None, tm, NOPE + VDIM), lambda i_, b: (b, i_, 0)),
                           b_spec=pl.BlockSpec((None, None, KV_RANK, NOPE + VDIM), lambda i_, b: (l, b, 0, 0)), extra_specs=[],
                           out_shapes=[_sds((T, KV_RANK), F32)], out_specs=[pl.BlockSpec((tm, KV_RANK), lambda i_, b: (i_, 0))],
                           dims=NT, k_axis=1, nk=HEADS, acc_shape=(tm, KV_RANK))[0]

            def latent_bwd(la, qn, kvn, dq_, dkv_, dkr_):
                dcq_raw, dqn = _rms_bwd(la[:, :Q_RANK], qn, dq_)
                dckv_raw, dkvn = _rms_bwd(la[:, Q_RANK:Q_RANK + KV_RANK], kvn, dkv_)
                return jnp.concatenate([dcq_raw, dckv_raw, dkr_], axis=1), dqn, dkvn

            dlat, d_qn[l], d_kvn[l] = _rowwise(
                "mla_latent_bwd", latent_bwd, [lat, mla_q_norm[l].reshape(1, Q_RANK), mla_kv_norm[l].reshape(1, KV_RANK), dcq, dckv, dkr],
                grid=(nt,), in_specs=[_row_spec(tm, LAT_PAD), _const_spec((1, Q_RANK)), _const_spec((1, KV_RANK)),
                                      _row_spec(tm, Q_RANK), _row_spec(tm, KV_RANK), _row_spec(tm, 128)],
                out_shapes=[_sds((T, LAT_PAD), BF16), _sds((1, Q_RANK), F32), _sds((1, KV_RANK), F32)],
                out_specs=[_row_spec(tm, LAT_PAD), _const_spec((1, Q_RANK)), _const_spec((1, KV_RANK))], n_acc=2)
            wgrad_rows("mla_down_wgrad", h, dlat, "dkv", l, D, LAT_PAD)
            dh = _matmul("mla_down_bwd", dlat, w_dkv, [], grid=(nt,), a_spec=_row_spec(tm, LAT_PAD),
                         b_spec=pl.BlockSpec((None, D, LAT_PAD), lambda i_: (l, 0, 0)), extra_specs=[],
                         out_shapes=[_sds((T, D), F32)], out_specs=[_row_spec(tm, D)], dims=NT)[0]
        else:
            h, z, ge, gate = mix_saved
            dgate = back_rows("sgu_out_bwd", dyb, g_out, l, ec, [], None)
            wgrad_rows("sgu_out_wgrad", gate, dyb, "out", l, ec, D)
            dz, d_wsp[l], d_bsp[l], d_lng[l], d_lnb[l] = _sgu_mid_bwd(ge, z, dgate, ln_g_full[l], ln_b_full[l], sgu_w_spatial[l], b_sp[l], 2)
            wgrad_cols("sgu_in_wgrad", h, dz, "in", l, e2c)
            dh = back_cols("sgu_in_bwd", dz, g_in, l, e2c)
        dx, d_norm_mix[i] = rmsnorm_bwd(xs_i, norm_mix[i], dh, dx)
    grad_x = dx.reshape(1, T, D)

    g_big = [G["dkv"][:, :, :LAT].reshape(n_mla, N_DEV, dc, LAT), G["uq"][:, :, :, :NOPE + ROPE], G["ukv"],
             G["o"].reshape(n_mla, N_DEV, owc, D), G["in"], G["out"].reshape(n_sgu, N_DEV, ec, D), G["up"],
             G["down"].reshape(depth, N_DEV, ffc, D)]
    from_sibling = _sibling_exchange("grad_sibling_exchange", g_big)

    def add_pair(name, g, rcv):
        lyr, _, rws, cls = g.shape
        g5 = g.reshape(lyr, N_CHIP, 2, rws, cls)
        rt = _tile(rws, 512)
        return _rowwise(name, lambda a, b_: a.astype(F32) + b_.astype(F32), [g5, rcv], grid=(lyr, N_CHIP, rws // rt),
                        in_specs=[pl.BlockSpec((None, None, None, rt, cls), lambda l_, ch, i_, cr: (l_, ch, cr[0], i_, 0)),
                                  pl.BlockSpec((None, None, rt, cls), lambda l_, ch, i_, cr: (l_, ch, i_, 0))],
                        out_shapes=_sds(rcv.shape, BF16), out_specs=pl.BlockSpec((None, None, rt, cls), lambda l_, ch, i_, cr: (l_, ch, i_, 0)),
                        grid_spec_prefetch=cidx)

    chip_parts = [add_pair("grad_pair_sum", g, rcv) for g, rcv in zip(g_big, from_sibling)]
    from_chips = _chip_exchange("grad_chip_exchange", chip_parts)

    def adam_big(name, parts, w, m, v):
        lyr, rws, cls = w.shape
        rt = _tile(rws, 256)

        def fn(p, w_, m_, v_):
            g = (p[0].astype(F32) + p[1].astype(F32)) + (p[2].astype(F32) + p[3].astype(F32))
            return (g, *_adam(w_, g, m_, v_))

        spec = pl.BlockSpec((None, rt, cls), lambda l_, i_: (l_, i_, 0))
        return _rowwise(name, fn, [parts, w, m, v], grid=(lyr, rws // rt),
                        in_specs=[pl.BlockSpec((N_CHIP, None, rt, cls), lambda l_, i_: (0, l_, i_, 0)), spec, spec, spec],
                        out_shapes=[_sds(w.shape, F32)] * 4, out_specs=[spec] * 4)

    big_w = [mla_w_dkv, mla_w_uq, mla_w_ukv, mla_w_o, sgu_w_in, sgu_w_out, ffn_w_up, ffn_w_down]
    big_m = [m_mla_w_dkv, m_mla_w_uq, m_mla_w_ukv, m_mla_w_o, m_sgu_w_in, m_sgu_w_out, m_ffn_w_up, m_ffn_w_down]
    big_v = [v_mla_w_dkv, v_mla_w_uq, v_mla_w_ukv, v_mla_w_o, v_sgu_w_in, v_sgu_w_out, v_ffn_w_up, v_ffn_w_down]
    big_res = [adam_big("adam_large", p, w, m, v) for p, w, m, v in zip(from_chips, big_w, big_m, big_v)]

    def rows128(a, rows):
        flat = a.reshape(-1, 128)
        return jnp.pad(flat, ((0, rows - flat.shape[0]), (0, 0)))

    def pad8(n):
        return -(-n // 8) * 8

    small_names = ["norm_mix", "norm_ffn", "final_norm", "q_norm", "kv_norm", "w_spatial", "b_spatial"]
    small_g = [jnp.concatenate(d_norm_mix, 0), jnp.concatenate(d_norm_ffn, 0), d_final, jnp.concatenate(d_qn, 0), jnp.concatenate(d_kvn, 0),
               jnp.stack(d_wsp, 0), jnp.stack(d_bsp, 0), jnp.concatenate(d_lng, 0), jnp.concatenate(d_lnb, 0)]
    small_w = [norm_mix, norm_ffn, final_norm, mla_q_norm, mla_kv_norm, sgu_w_spatial, sgu_b_spatial]
    small_m = [m_norm_mix, m_norm_ffn, m_final_norm, m_mla_q_norm, m_mla_kv_norm, m_sgu_w_spatial, m_sgu_b_spatial]
    small_v = [v_norm_mix, v_norm_ffn, v_final_norm, v_mla_q_norm, v_mla_kv_norm, v_sgu_w_spatial, v_sgu_b_spatial]
    sizes = [pad8(g.size // 128) for g in small_g]
    n_rep = len(small_w)
    sizes[n_rep - 1] += -sum(sizes[:n_rep]) % SMALL_ROWS
    offs = [sum(sizes[:k]) for k in range(len(sizes) + 1)]
    rep_rows = offs[n_rep]
    pack_g = jnp.concatenate([rows128(g, sz) for g, sz in zip(small_g, sizes)], axis=0)
    (gathered_small,) = _all_gather("gather_small_grads", [pack_g])

    def pack(arrs):
        return jnp.concatenate([rows128(a, sz) for a, sz in zip(arrs, sizes[:n_rep])], axis=0)

    def sum8(p):
        return ((p[0] + p[1]) + (p[2] + p[3])) + ((p[4] + p[5]) + (p[6] + p[7]))

    sspec = _row_spec(SMALL_ROWS, 128)
    rep_g, rep_d, rep_m, rep_v = _rowwise(
        "adam_small", lambda p, w_, m_, v_: (sum8(p), *_adam(w_, sum8(p), m_, v_)),
        [gathered_small, pack(small_w), pack(small_m), pack(small_v)], grid=(rep_rows // SMALL_ROWS,),
        in_specs=[pl.BlockSpec((N_DEV, SMALL_ROWS, 128), lambda i_: (0, i_, 0)), sspec, sspec, sspec],
        out_shapes=[_sds((rep_rows, 128), F32)] * 4, out_specs=[sspec] * 4)

    def unpack(packed, k, like):
        return packed[offs[k]:offs[k] + like.size // 128].reshape(like.shape)

    my_b = 4 * lax.axis_index("x") + 2 * lax.axis_index("y") + lax.axis_index("c")
    ln_w = jnp.concatenate([sgu_ln_g, sgu_ln_b], 0)
    ln_m = jnp.concatenate([m_sgu_ln_g, m_sgu_ln_b], 0)
    ln_v = jnp.concatenate([v_sgu_ln_g, v_sgu_ln_b], 0)
    ln_all = gathered_small[:, rep_rows:, :]
    ln_mine = lax.dynamic_slice_in_dim(ln_all[:, :2 * n_sgu * E // 128].reshape(N_DEV, 2 * n_sgu, N_DEV, ec), my_b, 1, axis=2).reshape(N_DEV, 2 * n_sgu, ec)
    ln_g_, ln_d, ln_m2, ln_v2 = _rowwise(
        "adam_ln", lambda p, w_, m_, v_: (sum8(p), *_adam(w_, sum8(p), m_, v_)), [ln_mine, ln_w, ln_m, ln_v], grid=(1,),
        in_specs=[_const_spec(ln_mine.shape), _const_spec(ln_w.shape), _const_spec(ln_w.shape), _const_spec(ln_w.shape)],
        out_shapes=[_sds(ln_w.shape, F32)] * 4, out_specs=[_const_spec(ln_w.shape)] * 4)

    def family(pos):
        rep = [rep_g, rep_d, rep_m, rep_v][pos]
        ln = [ln_g_, ln_d, ln_m2, ln_v2][pos]
        small = {nm: unpack(rep, k, w_) for k, (nm, w_) in enumerate(zip(small_names, small_w))}
        big = [res[pos] for res in big_res]
        return [small["norm_mix"], small["norm_ffn"], small["final_norm"], big[0], small["q_norm"], small["kv_norm"], big[1], big[2], big[3],
                big[4], ln[:n_sgu], ln[n_sgu:], small["w_spatial"], small["b_spatial"], big[5], big[6], big[7]]

    return (loss, grad_x, *family(0), *family(1), *family(2), *family(3))
```

```python
import functools
import math

import jax
import jax.numpy as jnp
from jax import lax
from jax.experimental import pallas as pl
from jax.experimental.pallas import tpu as pltpu

F32 = jnp.float32
BF16 = jnp.bfloat16
MESH = pl.DeviceIdType.MESH

N_DEV = 8
N_CHIP = 4
HEADS = 8
NOPE = 128
ROPE = 64
VDIM = 128
QPAD = 256
Q_RANK = 256
KV_RANK = 128
LAT = Q_RANK + KV_RANK + ROPE
LAT_PAD = 512
ROPE_THETA = 10000.0
SGU_CHUNK = 128
SGU_GROUPS = 8
NORM_EPS = 1e-6
LN_EPS = 1e-5
ADAM_LR = 0.001
ADAM_B1 = 0.9
ADAM_B2 = 0.999
ADAM_EPS = 1e-08
ADAM_WD = 0.01
ADAM_STEP = 10
ATTN_SCALE = (NOPE + ROPE) ** -0.5
NEG = -1e30
EXP2_SCALE = ATTN_SCALE * math.log2(math.e)
FLASH_SUB = 1
VMEM_LIMIT = 56 * 1024 * 1024
SMALL_ROWS = 256

NN = (((1,), (0,)), ((), ()))
NT = (((1,), (1,)), ((), ()))
TN = (((0,), (0,)), ((), ()))


def _pcall(body, **kw):
    return pl.pallas_call(body, **kw)


def _params(n_grid):
    return pltpu.CompilerParams(dimension_semantics=("arbitrary",) * n_grid, vmem_limit_bytes=VMEM_LIMIT)


def _sds(shape, dtype):
    return jax.ShapeDtypeStruct(tuple(shape), dtype)


def _tile(n, want):
    t = min(n, want)
    assert n % t == 0, (n, want)
    return t


def _matmul(name, a, b, extras, *, grid, a_spec, b_spec, extra_specs, out_shapes, out_specs, dims, k_axis=None, nk=1,
            acc_shape=None, epilogue=None, aliases=None):
    n_extra = len(extras)
    n_out = len(out_shapes)
    n_alias = 0 if aliases is None else len(aliases[0])

    def body(*refs):
        a_ref, b_ref = refs[0], refs[1]
        ex = refs[2:2 + n_extra]
        outs = refs[2 + n_extra + n_alias:2 + n_extra + n_alias + n_out]
        prod = lax.dot_general(a_ref[...], b_ref[...], dims, preferred_element_type=F32)

        def finish(acc):
            res = epilogue(acc, *[e[...] for e in ex]) if epilogue is not None else (acc,)
            for o, r in zip(outs, res):
                o[...] = r.astype(o.dtype)

        if k_axis is None:
            finish(prod)
        else:
            acc_ref = refs[-1]
            k = pl.program_id(k_axis)

            @pl.when(k == 0)
            def _():
                acc_ref[...] = prod

            @pl.when(k > 0)
            def _():
                acc_ref[...] += prod

            @pl.when(k == nk - 1)
            def _():
                finish(acc_ref[...])

    in_specs = [a_spec, b_spec, *extra_specs]
    operands = [a, b, *extras]
    io_alias = {}
    if aliases is not None:
        for j, (buf, out_idx) in enumerate(zip(*aliases)):
            io_alias[len(operands)] = out_idx
            operands.append(buf)
            in_specs.append(pl.BlockSpec(memory_space=pl.ANY))
    scratch = [] if k_axis is None else [pltpu.VMEM(acc_shape, F32)]
    res = _pcall(body, name=name, grid=grid, in_specs=in_specs, out_specs=out_specs, out_shape=out_shapes,
                 scratch_shapes=scratch, input_output_aliases=io_alias, compiler_params=_params(len(grid)))(*operands)
    return res


def _rowwise(name, fn, operands, *, grid, in_specs, out_shapes, out_specs, n_acc=0, grid_spec_prefetch=None):
    n_in = len(operands)
    n_out = len(out_shapes)
    n_pre = 0 if grid_spec_prefetch is None else 1

    def body(*refs):
        refs = refs[n_pre:]
        ins = refs[:n_in]
        outs = refs[n_in:n_in + n_out]
        res = fn(*[r[...] for r in ins])
        if not isinstance(res, (tuple, list)):
            res = (res,)
        first = pl.program_id(0) == 0
        for d in range(1, len(grid)):
            first = jnp.logical_and(first, pl.program_id(d) == 0)
        for idx, (o, r) in enumerate(zip(outs, res)):
            if idx < n_out - n_acc:
                o[...] = r.astype(o.dtype)
            else:
                @pl.when(first)
                def _(o=o, r=r):
                    o[...] = r.astype(o.dtype)

                @pl.when(jnp.logical_not(first))
                def _(o=o, r=r):
                    o[...] += r.astype(o.dtype)

    if grid_spec_prefetch is None:
        return _pcall(body, name=name, grid=grid, in_specs=in_specs, out_specs=out_specs, out_shape=out_shapes,
                      compiler_params=_params(len(grid)))(*operands)
    gs = pltpu.PrefetchScalarGridSpec(num_scalar_prefetch=1, grid=grid, in_specs=in_specs, out_specs=out_specs)
    return _pcall(body, name=name, grid_spec=gs, out_shape=out_shapes,
                  compiler_params=_params(len(grid)))(grid_spec_prefetch, *operands)


def _row_spec(tm, w):
    return pl.BlockSpec((tm, w), lambda i: (i, 0))


def _const_spec(shape):
    nd = len(shape)
    return pl.BlockSpec(tuple(shape), lambda *_: (0,) * nd)


def _rms_fwd(x, g):
    r = lax.rsqrt(jnp.mean(x * x, axis=-1, keepdims=True) + NORM_EPS)
    return x * r * g


def _rms_bwd(x, g, dy):
    r = lax.rsqrt(jnp.mean(x * x, axis=-1, keepdims=True) + NORM_EPS)
    xh = x * r
    u = dy * g
    dx = r * (u - xh * jnp.mean(u * xh, axis=-1, keepdims=True))
    dg = jnp.sum(dy * xh, axis=0, keepdims=True)
    return dx, dg


def _gelu(z):
    return 0.5 * z * (1.0 + lax.erf(z * (2.0 ** -0.5)))


def _gelu_grad(z):
    return 0.5 * (1.0 + lax.erf(z * (2.0 ** -0.5))) + z * jnp.exp(-0.5 * z * z) * ((2.0 * math.pi) ** -0.5)


def _rope_fwd(x, cc, sa, sb):
    return x * cc + pltpu.roll(x, 96, 1) * sa + pltpu.roll(x, 32, 1) * sb


def _rope_bwd(d, cc, sa, sb):
    return d * cc + pltpu.roll(d * sa, 32, 1) + pltpu.roll(d * sb, 96, 1)


def _adam(w, g, m, v):
    m = ADAM_B1 * m + (1.0 - ADAM_B1) * g
    v = ADAM_B2 * v + (1.0 - ADAM_B2) * (g * g)
    m_hat = m / (1.0 - ADAM_B1 ** ADAM_STEP)
    v_hat = v / (1.0 - ADAM_B2 ** ADAM_STEP)
    delta = -ADAM_LR * (m_hat / (jnp.sqrt(v_hat) + ADAM_EPS) + ADAM_WD * w)
    return delta, m, v


def _place():
    return lax.axis_index("x"), lax.axis_index("y"), lax.axis_index("c")


def _all_gather(name, arrays):
    n = len(arrays)

    def body(*refs):
        ins = refs[:n]
        outs = refs[n:2 * n]
        send_sems, recv_sems, local_sems = refs[2 * n:]
        x, y, c = _place()
        me, sibling = (x, y, c), (x, y, 1 - c)
        chips = [(1 - x, y), (x, 1 - y), (1 - x, 1 - y)]

        def slot(a, dev):
            return outs[a].at[4 * dev[0] + 2 * dev[1] + dev[2]]

        def copy(a, k, block, to, src=None):
            return pltpu.make_async_remote_copy(
                src_ref=slot(a, block) if src is None else src, dst_ref=slot(a, block),
                send_sem=send_sems.at[a, k], recv_sem=recv_sems.at[a, k], device_id=to, device_id_type=MESH)

        mine = [pltpu.make_async_copy(ins[a], slot(a, me), local_sems.at[a]) for a in range(n)]
        for cp in mine:
            cp.start()
        first = []
        for j, chip in enumerate(chips):
            first += [copy(a, 1 + j, me, (*chip, c), src=ins[a]) for a in range(n)]
        first += [copy(a, 0, me, sibling, src=ins[a]) for a in range(n)]
        for cp in first:
            cp.start()
        passed = []
        for j, chip in enumerate(chips):
            for a in range(n):
                copy(a, 1 + j, (*chip, c), me).wait_recv()
                fwd = copy(a, 4 + j, (*chip, c), sibling)
                fwd.start()
                passed.append(fwd)
        for a in range(n):
            copy(a, 0, sibling, me).wait_recv()
            for j, chip in enumerate(chips):
                copy(a, 4 + j, (*chip, 1 - c), me).wait_recv()
        for cp in first + passed:
            cp.wait_send()
        for cp in mine:
            cp.wait()

    any_spec = pl.BlockSpec(memory_space=pl.ANY)
    return _pcall(
        body, name=name, in_specs=[any_spec] * n, out_specs=[any_spec] * n,
        out_shape=[_sds((N_DEV, *a.shape), a.dtype) for a in arrays],
        scratch_shapes=[pltpu.SemaphoreType.DMA((n, 7)), pltpu.SemaphoreType.DMA((n, 7)), pltpu.SemaphoreType.DMA((n,))],
        compiler_params=pltpu.CompilerParams(has_side_effects=True),
    )(*arrays)


def _sibling_exchange(name, grads):
    n = len(grads)

    def body(*refs):
        ins = refs[:n]
        outs = refs[n:2 * n]
        send_sems, recv_sems = refs[2 * n:]
        x, y, c = _place()
        sibling = (x, y, 1 - c)
        started = []
        for a in range(n):
            for l in range(grads[a].shape[0]):
                for ch in range(N_CHIP):
                    cp = pltpu.make_async_remote_copy(
                        src_ref=ins[a].at[l, 2 * ch + 1 - c], dst_ref=outs[a].at[l, ch],
                        send_sem=send_sems.at[a], recv_sem=recv_sems.at[a], device_id=sibling, device_id_type=MESH)
                    cp.start()
        for a in range(n):
            pltpu.make_async_remote_copy(src_ref=outs[a], dst_ref=outs[a], send_sem=send_sems.at[a], recv_sem=recv_sems.at[a],
                                         device_id=sibling, device_id_type=MESH).wait()

    any_spec = pl.BlockSpec(memory_space=pl.ANY)
    return _pcall(
        body, name=name, in_specs=[any_spec] * n, out_specs=[any_spec] * n,
        out_shape=[_sds((g.shape[0], N_CHIP, *g.shape[2:]), g.dtype) for g in grads],
        scratch_shapes=[pltpu.SemaphoreType.DMA((n,)), pltpu.SemaphoreType.DMA((n,))],
        compiler_params=pltpu.CompilerParams(has_side_effects=True),
    )(*grads)


def _chip_exchange(name, parts):
    n = len(parts)

    def body(*refs):
        ins = refs[:n]
        outs = refs[n:2 * n]
        send_sems, recv_sems, local_sems = refs[2 * n:]
        x, y, c = _place()
        mine = 2 * x + y
        chips = [(1 - x, y), (x, 1 - y), (1 - x, 1 - y)]
        for a in range(n):
            for l in range(parts[a].shape[0]):
                pltpu.make_async_copy(ins[a].at[l, mine], outs[a].at[mine, l], local_sems.at[a]).start()
        for j, chip in enumerate(chips):
            for a in range(n):
                for l in range(parts[a].shape[0]):
                    pltpu.make_async_remote_copy(
                        src_ref=ins[a].at[l, 2 * chip[0] + chip[1]], dst_ref=outs[a].at[mine, l],
                        send_sem=send_sems.at[a, j], recv_sem=recv_sems.at[a, j], device_id=(*chip, c), device_id_type=MESH).start()
        for j, chip in enumerate(chips):
            for a in range(n):
                theirs = outs[a].at[2 * chip[0] + chip[1]]
                pltpu.make_async_remote_copy(src_ref=theirs, dst_ref=theirs, send_sem=send_sems.at[a, j], recv_sem=recv_sems.at[a, j],
                                             device_id=(*chip, c), device_id_type=MESH).wait()
        for a in range(n):
            pltpu.make_async_copy(outs[a].at[mine], outs[a].at[mine], local_sems.at[a]).wait()

    any_spec = pl.BlockSpec(memory_space=pl.ANY)
    return _pcall(
        body, name=name, in_specs=[any_spec] * n, out_specs=[any_spec] * n,
        out_shape=[_sds((N_CHIP, p.shape[0], *p.shape[2:]), p.dtype) for p in parts],
        scratch_shapes=[pltpu.SemaphoreType.DMA((n, 3)), pltpu.SemaphoreType.DMA((n, 3)), pltpu.SemaphoreType.DMA((n,))],
        compiler_params=pltpu.CompilerParams(has_side_effects=True),
    )(*parts)


def _flash_fwd(q, k, vt, tq):
    h, t, _ = q.shape
    nq = t // tq
    rs = tq // FLASH_SUB

    def body(q_ref, k_ref, vt_ref, o_ref, lse_ref):
        qi = pl.program_id(1)

        def step(kj, state, masked):
            kb = k_ref[pl.ds(pl.multiple_of(kj * tq, tq), tq), :]
            vtb = vt_ref[kj]
            new_state = []
            for u, (m_old, l_old, acc_old) in enumerate(state):
                st = lax.dot_general(kb, q_ref[pl.ds(u * rs, rs), :], NT, preferred_element_type=F32)
                if masked:
                    key = lax.broadcasted_iota(jnp.int32, (tq, rs), 0)
                    qry = lax.broadcasted_iota(jnp.int32, (tq, rs), 1) + u * rs
                    st = jnp.where(key <= qry, st, NEG)
                m_new = jnp.maximum(m_old, jnp.max(st, axis=0, keepdims=True))
                alpha = jnp.exp2((m_old - m_new) * EXP2_SCALE)
                pt = jnp.exp2((st - m_new) * EXP2_SCALE)
                l_new = alpha * l_old + jnp.sum(pt, axis=0, keepdims=True)
                acc_new = alpha * acc_old + lax.dot_general(vtb, pt.astype(BF16), NN, preferred_element_type=F32)
                new_state.append((m_new, l_new, acc_new))
            return tuple(new_state)

        init = tuple((jnp.full((1, rs), NEG, F32), jnp.zeros((1, rs), F32), jnp.zeros((VDIM, rs), F32)) for _ in range(FLASH_SUB))
        state = lax.fori_loop(0, qi, lambda kj, st_: step(kj, st_, False), init)
        state = step(qi, state, True)
        for u, (m, l, acc) in enumerate(state):
            o_ref[pl.ds(u * rs, rs), :] = (acc / l).T.astype(o_ref.dtype)
            lse_ref[:, pl.ds(u * rs, rs)] = m * EXP2_SCALE + jnp.log2(l)

    return _pcall(
        body, name="flash_fwd", grid=(h, nq),
        in_specs=[pl.BlockSpec((None, tq, QPAD), lambda hh, i: (hh, i, 0)),
                  pl.BlockSpec((None, t, QPAD), lambda hh, i: (hh, 0, 0)),
                  pl.BlockSpec((None, nq, VDIM, tq), lambda hh, i: (hh, 0, 0, 0))],
        out_specs=[pl.BlockSpec((tq, VDIM), lambda hh, i: (i, hh)),
                   pl.BlockSpec((None, None, 1, tq), lambda hh, i: (hh, i, 0, 0))],
        out_shape=[_sds((t, h * VDIM), BF16), _sds((h, nq, 1, tq), F32)],
        compiler_params=_params(2),
    )(q, k, vt)


def _flash_bwd(q, k, v, o, do, lse, tq):
    h, t, _ = q.shape
    nq = t // tq

    def body(q_ref, k_ref, v_ref, o_ref, do_ref, lse_ref, dqt_ref, dk_ref, dv_ref, delta_ref):
        kj = pl.program_id(1)

        @pl.when(kj == 0)
        def _():
            dqt_ref[...] = jnp.zeros_like(dqt_ref)
            ones = jnp.ones((8, VDIM), BF16)
            for qi in range(nq):
                rows = pl.ds(qi * tq, tq)
                prod = do_ref[rows, :].astype(F32) * o_ref[rows, :].astype(F32)
                hi = prod.astype(BF16)
                lo = (prod - hi.astype(F32)).astype(BF16)
                delta_ref[qi] = (lax.dot_general(ones, hi, NT, preferred_element_type=F32)
                                 + lax.dot_general(ones, lo, NT, preferred_element_type=F32))

        kb = k_ref[...]
        vb = v_ref[...]
        kbt = kb.astype(F32).T.astype(BF16)
        dk_ref[...] = jnp.zeros_like(dk_ref)
        dv_ref[...] = jnp.zeros_like(dv_ref)

        def step(qi, masked):
            rows = pl.ds(pl.multiple_of(qi * tq, tq), tq)
            qb = q_ref[rows, :]
            dob = do_ref[rows, :]
            st = lax.dot_general(kb, qb, NT, preferred_element_type=F32)
            pt = jnp.exp2(st * EXP2_SCALE - lse_ref[qi])
            if masked:
                key = lax.broadcasted_iota(jnp.int32, (tq, tq), 0)
                qry = lax.broadcasted_iota(jnp.int32, (tq, tq), 1)
                pt = jnp.where(key <= qry, pt, 0.0)
            dv_ref[...] += lax.dot_general(pt.astype(BF16), dob, NN, preferred_element_type=F32)
            dpt = lax.dot_general(vb, dob, NT, preferred_element_type=F32)
            dst = (pt * (dpt - delta_ref[qi, pl.ds(0, 1), :]) * ATTN_SCALE).astype(BF16)
            dk_ref[...] += lax.dot_general(dst, qb, NN, preferred_element_type=F32)
            dqt_ref[qi] += lax.dot_general(kbt, dst, NN, preferred_element_type=F32)

        step(kj, True)

        def loop_body(qi, carry):
            step(qi, False)
            return carry

        lax.fori_loop(kj + 1, nq, loop_body, 0)

    return _pcall(
        body, name="flash_bwd", grid=(h, nq),
        in_specs=[pl.BlockSpec((None, t, QPAD), lambda hh, j: (hh, 0, 0)),
                  pl.BlockSpec((None, tq, QPAD), lambda hh, j: (hh, j, 0)),
                  pl.BlockSpec((None, tq, VDIM), lambda hh, j: (hh, j, 0)),
                  pl.BlockSpec((t, VDIM), lambda hh, j: (0, hh)),
                  pl.BlockSpec((t, VDIM), lambda hh, j: (0, hh)),
                  pl.BlockSpec((None, nq, 1, tq), lambda hh, j: (hh, 0, 0, 0))],
        out_specs=[pl.BlockSpec((None, nq, QPAD, tq), lambda hh, j: (hh, 0, 0, 0)),
                   pl.BlockSpec((None, tq, QPAD), lambda hh, j: (hh, j, 0)),
                   pl.BlockSpec((None, tq, VDIM), lambda hh, j: (hh, j, 0))],
        out_shape=[_sds((h, nq, QPAD, tq), F32), _sds((h, t, QPAD), F32), _sds((h, t, VDIM), F32)],
        scratch_shapes=[pltpu.VMEM((nq, 8, tq), F32)],
        compiler_params=_params(2),
    )(q, k, v, o, do, lse)


def _tril_bf16(w):
    row = lax.broadcasted_iota(jnp.int32, w.shape, 0)
    col = lax.broadcasted_iota(jnp.int32, w.shape, 1)
    return jnp.where(col <= row, w, 0.0).astype(BF16)


def _layer_norm_parts(v0):
    mu = jnp.mean(v0, axis=-1, keepdims=True)
    vc = v0 - mu
    rstd = lax.rsqrt(jnp.mean(vc * vc, axis=-1, keepdims=True) + LN_EPS)
    return vc * rstd, rstd


def _sgu_mid_fwd(ge, ln_g, ln_b, w_sp, b_sp, chunks_per_step):
    t, e2 = ge.shape
    e = e2 // 2
    gd = e // SGU_GROUPS
    rows = SGU_CHUNK * chunks_per_step

    def body(u_ref, v_ref, g_ref, b_ref, w_ref, bs_ref, gate_ref):
        for ck in range(chunks_per_step):
            r = pl.ds(ck * SGU_CHUNK, SGU_CHUNK)
            xh, _ = _layer_norm_parts(v_ref[r, :].astype(F32))
            v1 = (xh * g_ref[...] + b_ref[...]).astype(BF16)
            for g in range(SGU_GROUPS):
                cols = pl.ds(g * gd, gd)
                mixed = lax.dot_general(_tril_bf16(w_ref[g]), v1[:, g * gd:(g + 1) * gd], NN, preferred_element_type=F32) + bs_ref[g]
                gate_ref[r, cols] = (u_ref[r, cols].astype(F32) * mixed).astype(BF16)

    return _pcall(
        body, name="sgu_mid_fwd", grid=(t // rows,),
        in_specs=[pl.BlockSpec((rows, e), lambda i: (i, 0)), pl.BlockSpec((rows, e), lambda i: (i, 1)),
                  _const_spec((1, e)), _const_spec((1, e)), _const_spec(w_sp.shape), _const_spec(b_sp.shape)],
        out_specs=pl.BlockSpec((rows, e), lambda i: (i, 0)),
        out_shape=_sds((t, e), BF16), compiler_params=_params(1),
    )(ge, ge, ln_g, ln_b, w_sp, b_sp)


def _sgu_mid_bwd(ge, z, dgate, ln_g, ln_b, w_sp, b_sp, chunks_per_step):
    t, e2 = ge.shape
    e = e2 // 2
    gd = e // SGU_GROUPS
    rows = SGU_CHUNK * chunks_per_step

    def body(u_ref, v_ref, zu_ref, zv_ref, dg_ref, g_ref, b_ref, w_ref, bs_ref, dz_ref, dw_ref, dbs_ref, dlg_ref, dlb_ref):
        @pl.when(pl.program_id(0) == 0)
        def _():
            dw_ref[...] = jnp.zeros_like(dw_ref)
            dbs_ref[...] = jnp.zeros_like(dbs_ref)
            dlg_ref[...] = jnp.zeros_like(dlg_ref)
            dlb_ref[...] = jnp.zeros_like(dlb_ref)

        for ck in range(chunks_per_step):
            r = pl.ds(ck * SGU_CHUNK, SGU_CHUNK)
            xh, rstd = _layer_norm_parts(v_ref[r, :].astype(F32))
            v1 = (xh * g_ref[...] + b_ref[...]).astype(BF16)
            dv1_parts = []
            for g in range(SGU_GROUPS):
                cols = pl.ds(g * gd, gd)
                wc = _tril_bf16(w_ref[g])
                v1g = v1[:, g * gd:(g + 1) * gd]
                mixed = lax.dot_general(wc, v1g, NN, preferred_element_type=F32) + bs_ref[g]
                dgate = dg_ref[r, cols].astype(F32)
                dmixed = dgate * u_ref[r, cols].astype(F32)
                du = dgate * mixed
                dz_ref[r, cols] = (du * _gelu_grad(zu_ref[r, cols].astype(F32))).astype(BF16)
                dbs_ref[g] += jnp.sum(dmixed, axis=1, keepdims=True)
                dmb = dmixed.astype(BF16)
                dwg = lax.dot_general(dmb, v1g, NT, preferred_element_type=F32)
                row = lax.broadcasted_iota(jnp.int32, dwg.shape, 0)
                col = lax.broadcasted_iota(jnp.int32, dwg.shape, 1)
                dw_ref[g] += jnp.where(col <= row, dwg, 0.0)
                dv1_parts.append(lax.dot_general(wc, dmb, TN, preferred_element_type=F32))
            dv1 = jnp.concatenate(dv1_parts, axis=1)
            dlg_ref[...] += jnp.sum(dv1 * xh, axis=0, keepdims=True)
            dlb_ref[...] += jnp.sum(dv1, axis=0, keepdims=True)
            dxh = dv1 * g_ref[...]
            dv0 = rstd * (dxh - jnp.mean(dxh, axis=-1, keepdims=True) - xh * jnp.mean(dxh * xh, axis=-1, keepdims=True))
            dz_ref[r, pl.ds(e, e)] = (dv0 * _gelu_grad(zv_ref[r, :].astype(F32))).astype(BF16)

    half0 = pl.BlockSpec((rows, e), lambda i: (i, 0))
    half1 = pl.BlockSpec((rows, e), lambda i: (i, 1))
    return _pcall(
        body, name="sgu_mid_bwd", grid=(t // rows,),
        in_specs=[half0, half1, half0, half1, half0, _const_spec((1, e)), _const_spec((1, e)), _const_spec(w_sp.shape), _const_spec(b_sp.shape)],
        out_specs=[pl.BlockSpec((rows, e2), lambda i: (i, 0)), _const_spec(w_sp.shape), _const_spec(b_sp.shape), _const_spec((1, e)), _const_spec((1, e))],
        out_shape=[_sds((t, e2), BF16), _sds(w_sp.shape, F32), _sds(b_sp.shape, F32), _sds((1, e), F32), _sds((1, e), F32)],
        compiler_params=_params(1),
    )(ge, ge, z, z, dgate, ln_g, ln_b, w_sp, b_sp)


def kernel(x, positions, norm_mix, norm_ffn, final_norm, mla_w_dkv, mla_q_norm, mla_kv_norm, mla_w_uq, mla_w_ukv, mla_w_o, sgu_w_in, sgu_ln_g, sgu_ln_b, sgu_w_spatial, sgu_b_spatial, sgu_w_out, ffn_w_up, ffn_w_down, loss_target, m_norm_mix, m_norm_ffn, m_final_norm, m_mla_w_dkv, m_mla_q_norm, m_mla_kv_norm, m_mla_w_uq, m_mla_w_ukv, m_mla_w_o, m_sgu_w_in, m_sgu_ln_g, m_sgu_ln_b, m_sgu_w_spatial, m_sgu_b_spatial, m_sgu_w_out, m_ffn_w_up, m_ffn_w_down, v_norm_mix, v_norm_ffn, v_final_norm, v_mla_w_dkv, v_mla_q_norm, v_mla_kv_norm, v_mla_w_uq, v_mla_w_ukv, v_mla_w_o, v_sgu_w_in, v_sgu_ln_g, v_sgu_ln_b, v_sgu_w_spatial, v_sgu_b_spatial, v_sgu_w_out, v_ffn_w_up, v_ffn_w_down):
    _, T, D = x.shape
    depth = norm_mix.shape[0]
    n_mla, n_sgu = mla_w_dkv.shape[0], sgu_w_in.shape[0]
    FF = ffn_w_up.shape[2] * N_DEV
    E = sgu_w_out.shape[1] * N_DEV
    ffc, ec, e2c = FF // N_DEV, E // N_DEV, 2 * E // N_DEV
    dc = D // N_DEV
    OW = HEADS * VDIM
    owc = OW // N_DEV
    tm = _tile(T, 1024)
    tq = _tile(T, 512)
    ts = _tile(T, 256)
    nt = T // tm
    x2 = x.reshape(T, D)
    tgt = loss_target.reshape(T, D)
    cidx = lax.axis_index("c").astype(jnp.int32).reshape(1)

    ln_local = jnp.concatenate([sgu_ln_g, sgu_ln_b, jnp.zeros((8 - 2 * n_sgu, ec), F32)], axis=0)
    shards = [w.astype(BF16) for w in (mla_w_dkv, mla_w_uq, mla_w_ukv, mla_w_o, sgu_w_in, sgu_w_out, ffn_w_up, ffn_w_down)]
    g_dkv, g_uq, g_ukv, g_o, g_in, g_out, g_up, g_down, g_ln = _all_gather("gather_weights", shards + [ln_local])
    w_dkv = jnp.pad(g_dkv.transpose(1, 0, 2, 3).reshape(n_mla, D, LAT), ((0, 0), (0, 0), (0, LAT_PAD - LAT)))
    w_uq = jnp.pad(g_uq.transpose(1, 0, 2, 3), ((0, 0), (0, 0), (0, 0), (0, QPAD - NOPE - ROPE)))
    w_ukv = g_ukv.transpose(1, 0, 2, 3)
    w_o = g_o.transpose(1, 0, 2, 3).reshape(n_mla, OW, D)
    ln_g_full = [g_ln[:, l, :].reshape(1, E) for l in range(n_sgu)]
    ln_b_full = [g_ln[:, n_sgu + l, :].reshape(1, E) for l in range(n_sgu)]
    b_sp = sgu_b_spatial.reshape(n_sgu, SGU_GROUPS, SGU_CHUNK, 1)

    inv_freq = ROPE_THETA ** (-jnp.arange(0, ROPE, 2, dtype=F32) / ROPE)
    zeros32 = jnp.zeros((ROPE // 2,), F32)
    inv128 = jnp.concatenate([inv_freq, inv_freq, zeros32, zeros32]).reshape(1, 128)
    sel_a = jnp.concatenate([-jnp.ones((32,), F32), zeros32, zeros32, zeros32]).reshape(1, 128)
    sel_b = jnp.concatenate([zeros32, jnp.ones((32,), F32), zeros32, zeros32]).reshape(1, 128)
    sel_c = jnp.concatenate([jnp.ones((64,), F32), zeros32, zeros32]).reshape(1, 128)

    def rope_tables(pos, inv, sa, sb, sc):
        ang = pos.astype(F32) * inv
        cs, sn = jnp.cos(ang), jnp.sin(ang)
        return cs * sc, sn * sa, sn * sb

    t_cc, t_sa, t_sb = _rowwise(
        "rope_tables", rope_tables, [positions.reshape(T, 1), inv128, sel_a, sel_b, sel_c], grid=(nt,),
        in_specs=[_row_spec(tm, 1)] + [_const_spec((1, 128))] * 4,
        out_shapes=[_sds((T, 128), F32)] * 3, out_specs=[_row_spec(tm, 128)] * 3)
    tab_specs = [_row_spec(tm, 128)] * 3

    def gbuf(w):
        return lax.empty((w.shape[0], N_DEV, *w.shape[1:]), BF16)

    G = {"dkv": lax.empty((n_mla, D, LAT_PAD), BF16), "uq": lax.empty((n_mla, HEADS, Q_RANK, QPAD), BF16), "ukv": gbuf(mla_w_ukv),
         "o": lax.empty((n_mla, OW, D), BF16), "in": gbuf(sgu_w_in), "out": lax.empty((n_sgu, E, D), BF16),
         "up": gbuf(ffn_w_up), "down": lax.empty((depth, FF, D), BF16)}

    def rmsnorm(xv, g):
        return _rowwise("rmsnorm", lambda a, gg: _rms_fwd(a, gg), [xv, g.reshape(1, D)], grid=(nt,),
                        in_specs=[_row_spec(tm, D), _const_spec((1, D))], out_shapes=_sds((T, D), BF16), out_specs=_row_spec(tm, D))

    def rmsnorm_bwd(xv, g, dh, dx_in):
        def fn(a, gg, d, dxi):
            dx, dg = _rms_bwd(a, gg, d)
            return dxi + dx, dg
        return _rowwise("rmsnorm_bwd", fn, [xv, g.reshape(1, D), dh, dx_in], grid=(nt,),
                        in_specs=[_row_spec(tm, D), _const_spec((1, D)), _row_spec(tm, D), _row_spec(tm, D)],
                        out_shapes=[_sds((T, D), F32), _sds((1, D), F32)], out_specs=[_row_spec(tm, D), _const_spec((1, D))], n_acc=1)

    def proj_cols(name, h, gw, l, nc, epilogue, n_out):
        n = nc * N_DEV
        return _matmul(name, h, gw, [], grid=(N_DEV, nt),
                       a_spec=pl.BlockSpec((tm, D), lambda j, i: (i, 0)),
                       b_spec=pl.BlockSpec((None, None, D, nc), lambda j, i: (j, l, 0, 0)), extra_specs=[],
                       out_shapes=[_sds((T, n), BF16)] * n_out, out_specs=[pl.BlockSpec((tm, nc), lambda j, i: (i, j))] * n_out,
                       dims=NN, epilogue=epilogue)

    def proj_rows_residual(name, a, gw, l, kc, xres):
        return _matmul(name, a, gw, [xres], grid=(nt, N_DEV),
                       a_spec=pl.BlockSpec((tm, kc), lambda i, k: (i, k)),
                       b_spec=pl.BlockSpec((None, None, kc, D), lambda i, k: (k, l, 0, 0)),
                       extra_specs=[pl.BlockSpec((tm, D), lambda i, k: (i, 0))],
                       out_shapes=[_sds((T, D), F32)], out_specs=[pl.BlockSpec((tm, D), lambda i, k: (i, 0))],
                       dims=NN, k_axis=1, nk=N_DEV, acc_shape=(tm, D), epilogue=lambda acc, xr: (acc + xr,))[0]

    def back_rows(name, dy, gw, l, kc, extras, epilogue):
        return _matmul(name, dy, gw, extras, grid=(N_DEV, nt),
                       a_spec=pl.BlockSpec((tm, D), lambda j, i: (i, 0)),
                       b_spec=pl.BlockSpec((None, None, kc, D), lambda j, i: (j, l, 0, 0)),
                       extra_specs=[pl.BlockSpec((tm, kc), lambda j, i: (i, j))] * len(extras),
                       out_shapes=[_sds((T, kc * N_DEV), BF16)], out_specs=[pl.BlockSpec((tm, kc), lambda j, i: (i, j))],
                       dims=NT, epilogue=epilogue)[0]

    def back_cols(name, da, gw, l, nc):
        return _matmul(name, da, gw, [], grid=(nt, N_DEV),
                       a_spec=pl.BlockSpec((tm, nc), lambda i, k: (i, k)),
                       b_spec=pl.BlockSpec((None, None, D, nc), lambda i, k: (k, l, 0, 0)), extra_specs=[],
                       out_shapes=[_sds((T, D), F32)], out_specs=[pl.BlockSpec((tm, D), lambda i, k: (i, 0))],
                       dims=NT, k_axis=1, nk=N_DEV, acc_shape=(tm, D))[0]

    def wgrad_cols(name, h, da, key, l, nc):
        G[key] = _matmul(name, h, da, [], grid=(N_DEV, nt),
                         a_spec=pl.BlockSpec((tm, D), lambda j, t: (t, 0)), b_spec=pl.BlockSpec((tm, nc), lambda j, t: (t, j)),
                         extra_specs=[], out_shapes=[_sds(G[key].shape, BF16)],
                         out_specs=[pl.BlockSpec((None, None, D, nc), lambda j, t: (l, j, 0, 0))],
                         dims=TN, k_axis=1, nk=nt, acc_shape=(D, nc), aliases=([G[key]], [0]))[0]

    def wgrad_rows(name, a, dy, key, l, kc, ncols):
        G[key] = _matmul(name, a, dy, [], grid=(a.shape[1] // kc, nt),
                         a_spec=pl.BlockSpec((tm, kc), lambda j, t: (t, j)), b_spec=pl.BlockSpec((tm, ncols), lambda j, t: (t, 0)),
                         extra_specs=[], out_shapes=[_sds(G[key].shape, BF16)],
                         out_specs=[pl.BlockSpec((None, kc, ncols), lambda j, t: (l, j, 0))],
                         dims=TN, k_axis=1, nk=nt, acc_shape=(kc, ncols), aliases=([G[key]], [0]))[0]

    saved = []
    xs = x2
    for i in range(depth):
        l = i // 2
        h = rmsnorm(xs, norm_mix[i])
        if i % 2 == 0:
            lat = _matmul("mla_down", h, w_dkv, [], grid=(nt,), a_spec=_row_spec(tm, D),
                          b_spec=pl.BlockSpec((None, D, LAT_PAD), lambda i_: (l, 0, 0)), extra_specs=[],
                          out_shapes=[_sds((T, LAT_PAD), F32)], out_specs=[_row_spec(tm, LAT_PAD)], dims=NN)[0]

            def latent_post(la, qn, kvn, cc, sa, sb):
                cq = _rms_fwd(la[:, :Q_RANK], qn)
                ckv = _rms_fwd(la[:, Q_RANK:Q_RANK + KV_RANK], kvn)
                kr = _rope_fwd(la[:, Q_RANK + KV_RANK:], cc, sa, sb)
                return cq, ckv, kr

            cq, ckv, kr = _rowwise(
                "mla_latent", latent_post, [lat, mla_q_norm[l].reshape(1, Q_RANK), mla_kv_norm[l].reshape(1, KV_RANK), t_cc, t_sa, t_sb],
                grid=(nt,), in_specs=[_row_spec(tm, LAT_PAD), _const_spec((1, Q_RANK)), _const_spec((1, KV_RANK))] + tab_specs,
                out_shapes=[_sds((T, Q_RANK), BF16), _sds((T, KV_RANK), BF16), _sds((T, 128), BF16)],
                out_specs=[_row_spec(tm, Q_RANK), _row_spec(tm, KV_RANK), _row_spec(tm, 128)])

            def q_epilogue(acc, cc, sa, sb):
                return (jnp.concatenate([acc[:, :NOPE], _rope_fwd(acc[:, NOPE:], cc, sa, sb)], axis=1),)

            q = _matmul("mla_q", cq, w_uq, [t_cc, t_sa, t_sb], grid=(HEADS, nt),
                        a_spec=pl.BlockSpec((tm, Q_RANK), lambda b, i_: (i_, 0)),
                        b_spec=pl.BlockSpec((None, None, Q_RANK, QPAD), lambda b, i_: (l, b, 0, 0)),
                        extra_specs=[pl.BlockSpec((tm, 128), lambda b, i_: (i_, 0))] * 3,
                        out_shapes=[_sds((HEADS, T, QPAD), BF16)], out_specs=[pl.BlockSpec((None, tm, QPAD), lambda b, i_: (b, i_, 0))],
                        dims=NN, epilogue=q_epilogue)[0]

            def kv_epilogue(acc, krb):
                return jnp.concatenate([acc[:, :NOPE], krb.astype(F32)], axis=1), acc[:, NOPE:], acc[:, NOPE:].T

            kk, vv, vt = _matmul("mla_kv", ckv, w_ukv, [kr], grid=(HEADS, T // tq),
                                 a_spec=pl.BlockSpec((tq, KV_RANK), lambda b, i_: (i_, 0)),
                                 b_spec=pl.BlockSpec((None, None, KV_RANK, NOPE + VDIM), lambda b, i_: (l, b, 0, 0)),
                                 extra_specs=[pl.BlockSpec((tq, 128), lambda b, i_: (i_, 0))],
                                 out_shapes=[_sds((HEADS, T, QPAD), BF16), _sds((HEADS, T, VDIM), BF16), _sds((HEADS, T // tq, VDIM, tq), BF16)],
                                 out_specs=[pl.BlockSpec((None, tq, QPAD), lambda b, i_: (b, i_, 0)), pl.BlockSpec((None, tq, VDIM), lambda b, i_: (b, i_, 0)),
                                            pl.BlockSpec((None, None, VDIM, tq), lambda b, i_: (b, i_, 0, 0))],
                                 dims=NN, epilogue=kv_epilogue)
            o, lse = _flash_fwd(q, kk, vt, tq)
            xm = _matmul("mla_out", o, w_o, [xs], grid=(nt,), a_spec=_row_spec(tm, OW),
                         b_spec=pl.BlockSpec((None, OW, D), lambda i_: (l, 0, 0)), extra_specs=[_row_spec(tm, D)],
                         out_shapes=[_sds((T, D), F32)], out_specs=[_row_spec(tm, D)], dims=NN, epilogue=lambda acc, xr: (acc + xr,))[0]
            mix_saved = (h, lat, cq, ckv, q, kk, vv, o, lse)
        else:
            z, ge = proj_cols("sgu_in", h, g_in, l, e2c, lambda acc: (acc, _gelu(acc)), 2)
            gate = _sgu_mid_fwd(ge, ln_g_full[l], ln_b_full[l], sgu_w_spatial[l], b_sp[l], 4)
            xm = proj_rows_residual("sgu_out", gate, g_out, l, ec, xs)
            mix_saved = (h, z, ge, gate)
        h2 = rmsnorm(xm, norm_ffn[i])
        r, s = proj_cols("ffn_up", h2, g_up, i, ffc, lambda acc: (jnp.maximum(acc, 0.0), jnp.square(jnp.maximum(acc, 0.0))), 2)
        xo = proj_rows_residual("ffn_down", s, g_down, i, ffc, xm)
        saved.append((xs, xm, mix_saved, h2, r, s))
        xs = xo

    def loss_head(xv, tg, g):
        y = _rms_fwd(xv, g)
        err = y - tg
        part = 0.5 * jnp.sum(jnp.sum(err * err, axis=-1, keepdims=True), axis=0, keepdims=True) / D
        dx, dg = _rms_bwd(xv, g, err / D)
        return dx, jnp.broadcast_to(part, (1, 128)), dg

    dx, loss_part, d_final = _rowwise(
        "loss_head", loss_head, [xs, tgt, final_norm.reshape(1, D)], grid=(nt,),
        in_specs=[_row_spec(tm, D), _row_spec(tm, D), _const_spec((1, D))],
        out_shapes=[_sds((T, D), F32), _sds((1, 128), F32), _sds((1, D), F32)],
        out_specs=[_row_spec(tm, D), _const_spec((1, 128)), _const_spec((1, D))], n_acc=2)
    loss = lax.psum(loss_part[0, 0], ("x", "y", "c"))

    d_norm_mix, d_norm_ffn = [None] * depth, [None] * depth
    d_qn, d_kvn = [None] * n_mla, [None] * n_mla
    d_wsp, d_bsp, d_lng, d_lnb = [None] * n_sgu, [None] * n_sgu, [None] * n_sgu, [None] * n_sgu

    def to_bf16(a):
        return _rowwise("to_bf16", lambda v_: v_, [a], grid=(nt,), in_specs=[_row_spec(tm, a.shape[1])],
                        out_shapes=_sds(a.shape, BF16), out_specs=_row_spec(tm, a.shape[1]))

    for i in reversed(range(depth)):
        l = i // 2
        xs_i, xm, mix_saved, h2, r, s = saved[i]
        dyb = to_bf16(dx)
        da = back_rows("ffn_down_bwd", dyb, g_down, i, ffc, [r], lambda acc, rr: (acc * (2.0 * rr.astype(F32)),))
        wgrad_rows("ffn_down_wgrad", s, dyb, "down", i, ffc, D)
        wgrad_cols("ffn_up_wgrad", h2, da, "up", i, ffc)
        dh2 = back_cols("ffn_up_bwd", da, g_up, i, ffc)
        dx, d_norm_ffn[i] = rmsnorm_bwd(xm, norm_ffn[i], dh2, dx)
        dyb = to_bf16(dx)
        if i % 2 == 0:
            h, lat, cq, ckv, q, kk, vv, o, lse = mix_saved
            do = _matmul("mla_out_bwd", dyb, w_o, [], grid=(nt,), a_spec=_row_spec(tm, D),
                         b_spec=pl.BlockSpec((None, OW, D), lambda i_: (l, 0, 0)), extra_specs=[],
                         out_shapes=[_sds((T, OW), BF16)], out_specs=[_row_spec(tm, OW)], dims=NT)[0]
            wgrad_rows("mla_out_wgrad", o, dyb, "o", l, OW, D)
            dqt, dk, dv = _flash_bwd(q, kk, vv, o, do, lse, tq)

            def q_pre(dt, cc, sa, sb):
                d = dt.T
                return jnp.concatenate([d[:, :NOPE], _rope_bwd(d[:, NOPE:], cc, sa, sb)], axis=1)

            dq_pre = _rowwise("mla_dq_rope", q_pre, [dqt, t_cc, t_sa, t_sb], grid=(HEADS, T // tq),
                              in_specs=[pl.BlockSpec((None, None, QPAD, tq), lambda b, i_: (b, i_, 0, 0))] + [pl.BlockSpec((tq, 128), lambda b, i_: (i_, 0))] * 3,
                              out_shapes=_sds((HEADS, T, QPAD), BF16), out_specs=pl.BlockSpec((None, tq, QPAD), lambda b, i_: (b, i_, 0)))

            def kv_pre(dkb, dvb, cc, sa, sb):
                dkv = jnp.concatenate([dkb[:, :, :NOPE], dvb], axis=2)
                dkr = _rope_bwd(jnp.sum(dkb[:, :, NOPE:], axis=0), cc, sa, sb)
                return dkv, dkr

            dkv, dkr = _rowwise("mla_dkv_rope", kv_pre, [dk, dv, t_cc, t_sa, t_sb], grid=(T // ts,),
                                in_specs=[pl.BlockSpec((HEADS, ts, QPAD), lambda i_: (0, i_, 0)), pl.BlockSpec((HEADS, ts, VDIM), lambda i_: (0, i_, 0))] + [_row_spec(ts, 128)] * 3,
                                out_shapes=[_sds((HEADS, T, NOPE + VDIM), BF16), _sds((T, 128), F32)],
                                out_specs=[pl.BlockSpec((HEADS, ts, NOPE + VDIM), lambda i_: (0, i_, 0)), _row_spec(ts, 128)])
            G["uq"] = _matmul("mla_q_wgrad", cq, dq_pre, [], grid=(HEADS, nt),
                              a_spec=pl.BlockSpec((tm, Q_RANK), lambda b, t_: (t_, 0)), b_spec=pl.BlockSpec((None, tm, QPAD), lambda b, t_: (b, t_, 0)),
                              extra_specs=[], out_shapes=[_sds(G["uq"].shape, BF16)],
                              out_specs=[pl.BlockSpec((None, None, Q_RANK, QPAD), lambda b, t_: (l, b, 0, 0))],
                              dims=TN, k_axis=1, nk=nt, acc_shape=(Q_RANK, QPAD), aliases=([G["uq"]], [0]))[0]
            G["ukv"] = _matmul("mla_kv_wgrad", ckv, dkv, [], grid=(HEADS, nt),
                               a_spec=pl.BlockSpec((tm, KV_RANK), lambda b, t_: (t_, 0)), b_spec=pl.BlockSpec((None, tm, NOPE + VDIM), lambda b, t_: (b, t_, 0)),
                               extra_specs=[], out_shapes=[_sds(G["ukv"].shape, BF16)],
                               out_specs=[pl.BlockSpec((None, None, KV_RANK, NOPE + VDIM), lambda b, t_: (l, b, 0, 0))],
                               dims=TN, k_axis=1, nk=nt, acc_shape=(KV_RANK, NOPE + VDIM), aliases=([G["ukv"]], [0]))[0]
            dcq = _matmul("mla_q_bwd", dq_pre, w_uq, [], grid=(nt, HEADS),
                          a_spec=pl.BlockSpec((None, tm, QPAD), lambda i_, b: (b, i_, 0)),
                          b_spec=pl.BlockSpec((None, None, Q_RANK, QPAD), lambda i_, b: (l, b, 0, 0)), extra_specs=[],
                          out_shapes=[_sds((T, Q_RANK), F32)], out_specs=[pl.BlockSpec((tm, Q_RANK), lambda i_, b: (i_, 0))],
                          dims=NT, k_axis=1, nk=HEADS, acc_shape=(tm, Q_RANK))[0]
            dckv = _matmul("mla_kv_bwd", dkv, w_ukv, [], grid=(nt, HEADS),
                           a_spec=pl.BlockSpec((None, tm, NOPE + VDIM), lambda i_, b: (b, i_, 0)),
                           b_spec=pl.BlockSpec((None, None, KV_RANK, NOPE + VDIM), lambda i_, b: (l, b, 0, 0)), extra_specs=[],
                           out_shapes=[_sds((T, KV_RANK), F32)], out_specs=[pl.BlockSpec((tm, KV_RANK), lambda i_, b: (i_, 0))],
                           dims=NT, k_axis=1, nk=HEADS, acc_shape=(tm, KV_RANK))[0]

            def latent_bwd(la, qn, kvn, dq_, dkv_, dkr_):
                dcq_raw, dqn = _rms_bwd(la[:, :Q_RANK], qn, dq_)
                dckv_raw, dkvn = _rms_bwd(la[:, Q_RANK:Q_RANK + KV_RANK], kvn, dkv_)
                return jnp.concatenate([dcq_raw, dckv_raw, dkr_], axis=1), dqn, dkvn

            dlat, d_qn[l], d_kvn[l] = _rowwise(
                "mla_latent_bwd", latent_bwd, [lat, mla_q_norm[l].reshape(1, Q_RANK), mla_kv_norm[l].reshape(1, KV_RANK), dcq, dckv, dkr],
                grid=(nt,), in_specs=[_row_spec(tm, LAT_PAD), _const_spec((1, Q_RANK)), _const_spec((1, KV_RANK)),
                                      _row_spec(tm, Q_RANK), _row_spec(tm, KV_RANK), _row_spec(tm, 128)],
                out_shapes=[_sds((T, LAT_PAD), BF16), _sds((1, Q_RANK), F32), _sds((1, KV_RANK), F32)],
                out_specs=[_row_spec(tm, LAT_PAD), _const_spec((1, Q_RANK)), _const_spec((1, KV_RANK))], n_acc=2)
            wgrad_rows("mla_down_wgrad", h, dlat, "dkv", l, D, LAT_PAD)
            dh = _matmul("mla_down_bwd", dlat, w_dkv, [], grid=(nt,), a_spec=_row_spec(tm, LAT_PAD),
                         b_spec=pl.BlockSpec((None, D, LAT_PAD), lambda i_: (l, 0, 0)), extra_specs=[],
                         out_shapes=[_sds((T, D), F32)], out_specs=[_row_spec(tm, D)], dims=NT)[0]
        else:
            h, z, ge, gate = mix_saved
            dgate = back_rows("sgu_out_bwd", dyb, g_out, l, ec, [], None)
            wgrad_rows("sgu_out_wgrad", gate, dyb, "out", l, ec, D)
            dz, d_wsp[l], d_bsp[l], d_lng[l], d_lnb[l] = _sgu_mid_bwd(ge, z, dgate, ln_g_full[l], ln_b_full[l], sgu_w_spatial[l], b_sp[l], 2)
            wgrad_cols("sgu_in_wgrad", h, dz, "in", l, e2c)
            dh = back_cols("sgu_in_bwd", dz, g_in, l, e2c)
        dx, d_norm_mix[i] = rmsnorm_bwd(xs_i, norm_mix[i], dh, dx)
    grad_x = dx.reshape(1, T, D)

    g_big = [G["dkv"][:, :, :LAT].reshape(n_mla, N_DEV, dc, LAT), G["uq"][:, :, :, :NOPE + ROPE], G["ukv"],
             G["o"].reshape(n_mla, N_DEV, owc, D), G["in"], G["out"].reshape(n_sgu, N_DEV, ec, D), G["up"],
             G["down"].reshape(depth, N_DEV, ffc, D)]
    from_sibling = _sibling_exchange("grad_sibling_exchange", g_big)

    def add_pair(name, g, rcv):
        lyr, _, rws, cls = g.shape
        g5 = g.reshape(lyr, N_CHIP, 2, rws, cls)
        rt = _tile(rws, 512)
        return _rowwise(name, lambda a, b_: a.astype(F32) + b_.astype(F32), [g5, rcv], grid=(lyr, N_CHIP, rws // rt),
                        in_specs=[pl.BlockSpec((None, None, None, rt, cls), lambda l_, ch, i_, cr: (l_, ch, cr[0], i_, 0)),
                                  pl.BlockSpec((None, None, rt, cls), lambda l_, ch, i_, cr: (l_, ch, i_, 0))],
                        out_shapes=_sds(rcv.shape, BF16), out_specs=pl.BlockSpec((None, None, rt, cls), lambda l_, ch, i_, cr: (l_, ch, i_, 0)),
                        grid_spec_prefetch=cidx)

    chip_parts = [add_pair("grad_pair_sum", g, rcv) for g, rcv in zip(g_big, from_sibling)]
    from_chips = _chip_exchange("grad_chip_exchange", chip_parts)

    def adam_big(name, parts, w, m, v):
        lyr, rws, cls = w.shape
        rt = _tile(rws, 256)

        def fn(p, w_, m_, v_):
            g = (p[0].astype(F32) + p[1].astype(F32)) + (p[2].astype(F32) + p[3].astype(F32))
            return (g, *_adam(w_, g, m_, v_))

        spec = pl.BlockSpec((None, rt, cls), lambda l_, i_: (l_, i_, 0))
        return _rowwise(name, fn, [parts, w, m, v], grid=(lyr, rws // rt),
                        in_specs=[pl.BlockSpec((N_CHIP, None, rt, cls), lambda l_, i_: (0, l_, i_, 0)), spec, spec, spec],
                        out_shapes=[_sds(w.shape, F32)] * 4, out_specs=[spec] * 4)

    big_w = [mla_w_dkv, mla_w_uq, mla_w_ukv, mla_w_o, sgu_w_in, sgu_w_out, ffn_w_up, ffn_w_down]
    big_m = [m_mla_w_dkv, m_mla_w_uq, m_mla_w_ukv, m_mla_w_o, m_sgu_w_in, m_sgu_w_out, m_ffn_w_up, m_ffn_w_down]
    big_v = [v_mla_w_dkv, v_mla_w_uq, v_mla_w_ukv, v_mla_w_o, v_sgu_w_in, v_sgu_w_out, v_ffn_w_up, v_ffn_w_down]
    big_res = [adam_big("adam_large", p, w, m, v) for p, w, m, v in zip(from_chips, big_w, big_m, big_v)]

    def rows128(a, rows):
        flat = a.reshape(-1, 128)
        return jnp.pad(flat, ((0, rows - flat.shape[0]), (0, 0)))

    def pad8(n):
        return -(-n // 8) * 8

    small_names = ["norm_mix", "norm_ffn", "final_norm", "q_norm", "kv_norm", "w_spatial", "b_spatial"]
    small_g = [jnp.concatenate(d_norm_mix, 0), jnp.concatenate(d_norm_ffn, 0), d_final, jnp.concatenate(d_qn, 0), jnp.concatenate(d_kvn, 0),
               jnp.stack(d_wsp, 0), jnp.stack(d_bsp, 0), jnp.concatenate(d_lng, 0), jnp.concatenate(d_lnb, 0)]
    small_w = [norm_mix, norm_ffn, final_norm, mla_q_norm, mla_kv_norm, sgu_w_spatial, sgu_b_spatial]
    small_m = [m_norm_mix, m_norm_ffn, m_final_norm, m_mla_q_norm, m_mla_kv_norm, m_sgu_w_spatial, m_sgu_b_spatial]
    small_v = [v_norm_mix, v_norm_ffn, v_final_norm, v_mla_q_norm, v_mla_kv_norm, v_sgu_w_spatial, v_sgu_b_spatial]
    sizes = [pad8(g.size // 128) for g in small_g]
    n_rep = len(small_w)
    sizes[n_rep - 1] += -sum(sizes[:n_rep]) % SMALL_ROWS
    offs = [sum(sizes[:k]) for k in range(len(sizes) + 1)]
    rep_rows = offs[n_rep]
    pack_g = jnp.concatenate([rows128(g, sz) for g, sz in zip(small_g, sizes)], axis=0)
    (gathered_small,) = _all_gather("gather_small_grads", [pack_g])

    def pack(arrs):
        return jnp.concatenate([rows128(a, sz) for a, sz in zip(arrs, sizes[:n_rep])], axis=0)

    def sum8(p):
        return ((p[0] + p[1]) + (p[2] + p[3])) + ((p[4] + p[5]) + (p[6] + p[7]))

    sspec = _row_spec(SMALL_ROWS, 128)
    rep_g, rep_d, rep_m, rep_v = _rowwise(
        "adam_small", lambda p, w_, m_, v_: (sum8(p), *_adam(w_, sum8(p), m_, v_)),
        [gathered_small, pack(small_w), pack(small_m), pack(small_v)], grid=(rep_rows // SMALL_ROWS,),
        in_specs=[pl.BlockSpec((N_DEV, SMALL_ROWS, 128), lambda i_: (0, i_, 0)), sspec, sspec, sspec],
        out_shapes=[_sds((rep_rows, 128), F32)] * 4, out_specs=[sspec] * 4)

    def unpack(packed, k, like):
        return packed[offs[k]:offs[k] + like.size // 128].reshape(like.shape)

    my_b = 4 * lax.axis_index("x") + 2 * lax.axis_index("y") + lax.axis_index("c")
    ln_w = jnp.concatenate([sgu_ln_g, sgu_ln_b], 0)
    ln_m = jnp.concatenate([m_sgu_ln_g, m_sgu_ln_b], 0)
    ln_v = jnp.concatenate([v_sgu_ln_g, v_sgu_ln_b], 0)
    ln_all = gathered_small[:, rep_rows:, :]
    ln_mine = lax.dynamic_slice_in_dim(ln_all[:, :2 * n_sgu * E // 128].reshape(N_DEV, 2 * n_sgu, N_DEV, ec), my_b, 1, axis=2).reshape(N_DEV, 2 * n_sgu, ec)
    ln_g_, ln_d, ln_m2, ln_v2 = _rowwise(
        "adam_ln", lambda p, w_, m_, v_: (sum8(p), *_adam(w_, sum8(p), m_, v_)), [ln_mine, ln_w, ln_m, ln_v], grid=(1,),
        in_specs=[_const_spec(ln_mine.shape), _const_spec(ln_w.shape), _const_spec(ln_w.shape), _const_spec(ln_w.shape)],
        out_shapes=[_sds(ln_w.shape, F32)] * 4, out_specs=[_const_spec(ln_w.shape)] * 4)

    def family(pos):
        rep = [rep_g, rep_d, rep_m, rep_v][pos]
        ln = [ln_g_, ln_d, ln_m2, ln_v2][pos]
        small = {nm: unpack(rep, k, w_) for k, (nm, w_) in enumerate(zip(small_names, small_w))}
        big = [res[pos] for res in big_res]
        return [small["norm_mix"], small["norm_ffn"], small["final_norm"], big[0], small["q_norm"], small["kv_norm"], big[1], big[2], big[3],
                big[4], ln[:n_sgu], ln[n_sgu:], small["w_spatial"], small["b_spatial"], big[5], big[6], big[7]]

    return (loss, grad_x, *family(0), *family(1), *family(2), *family(3))
```

```python
import math

import jax
import jax.numpy as jnp
from jax import lax
from jax.experimental import pallas as pl
from jax.experimental.pallas import tpu as pltpu

F32 = jnp.float32
BF16 = jnp.bfloat16
MESH = pl.DeviceIdType.MESH

N_DEV = 8
N_CHIP = 4
HEADS = 8
NOPE = 128
ROPE = 64
VDIM = 128
QPAD = 256
Q_RANK = 256
KV_RANK = 128
LAT = Q_RANK + KV_RANK + ROPE
LAT_PAD = 512
ROPE_THETA = 10000.0
SGU_CHUNK = 128
SGU_GROUPS = 8
NORM_EPS = 1e-6
LN_EPS = 1e-5
ADAM_LR = 0.001
ADAM_B1 = 0.9
ADAM_B2 = 0.999
ADAM_EPS = 1e-08
ADAM_WD = 0.01
ADAM_STEP = 10
ATTN_SCALE = (NOPE + ROPE) ** -0.5
NEG = -1e30
EXP2_SCALE = ATTN_SCALE * math.log2(math.e)
VMEM_LIMIT = 56 * 1024 * 1024
SMALL_ROWS = 256

NN = (((1,), (0,)), ((), ()))
NT = (((1,), (1,)), ((), ()))
TN = (((0,), (0,)), ((), ()))
ANY = pl.BlockSpec(memory_space=pl.ANY)


def _pcall(body, **kw):
    return pl.pallas_call(body, **kw)


def _params(n_grid, side_effects=False):
    return pltpu.CompilerParams(dimension_semantics=("arbitrary",) * n_grid, vmem_limit_bytes=VMEM_LIMIT, has_side_effects=side_effects)


def _sds(shape, dtype):
    return jax.ShapeDtypeStruct(tuple(shape), dtype)


def _tile(n, want):
    t = min(n, want)
    assert n % t == 0, (n, want)
    return t


class _Comm:
    def __init__(self, operands, out_shapes, aliases, scratch, start, finish):
        self.operands, self.out_shapes, self.aliases, self.scratch = operands, out_shapes, aliases, scratch
        self.start, self.finish = start, finish


def _place():
    return lax.axis_index("x"), lax.axis_index("y"), lax.axis_index("c")


def _other_chips(x, y):
    return [(1 - x, y), (x, 1 - y), (1 - x, 1 - y)]


def _dev_index(dev):
    return 4 * dev[0] + 2 * dev[1] + dev[2]


def _comm_call(name, comm):
    c_in, c_out = len(comm.operands), len(comm.out_shapes)

    def body(*refs):
        ins, outs, sems = refs[:c_in], refs[c_in:c_in + c_out], refs[c_in + c_out:]
        comm.start(ins, outs, sems)
        comm.finish(ins, outs, sems)

    return _pcall(body, name=name, in_specs=[ANY] * c_in, out_specs=[ANY] * c_out, out_shape=comm.out_shapes,
                  scratch_shapes=comm.scratch, input_output_aliases=dict(comm.aliases),
                  compiler_params=pltpu.CompilerParams(has_side_effects=True))(*comm.operands)


def _call(name, body, operands, in_specs, out_shapes, out_specs, scratch, grid, comm=None):
    if comm is None:
        return _pcall(body, name=name, grid=grid, in_specs=in_specs, out_specs=out_specs, out_shape=out_shapes,
                      scratch_shapes=scratch, compiler_params=_params(len(grid)))(*operands)
    n_in, n_out, n_sc = len(operands), len(out_shapes), len(scratch)
    c_in, c_out = len(comm.operands), len(comm.out_shapes)

    def hosted(*refs):
        ins, cins = refs[:n_in], refs[n_in:n_in + c_in]
        o0 = n_in + c_in
        outs, couts = refs[o0:o0 + n_out], refs[o0 + n_out:o0 + n_out + c_out]
        rest = refs[o0 + n_out + c_out:]
        sc, csems = rest[:n_sc], rest[n_sc:]
        first = pl.program_id(0) == 0
        last = pl.program_id(0) == grid[0] - 1
        for d in range(1, len(grid)):
            first = jnp.logical_and(first, pl.program_id(d) == 0)
            last = jnp.logical_and(last, pl.program_id(d) == grid[d] - 1)

        @pl.when(first)
        def _():
            comm.start(cins, couts, csems)

        body(*ins, *outs, *sc)

        @pl.when(last)
        def _():
            comm.finish(cins, couts, csems)

    return _pcall(hosted, name=name, grid=grid, in_specs=[*in_specs, *[ANY] * c_in], out_specs=[*out_specs, *[ANY] * c_out],
                  out_shape=[*out_shapes, *comm.out_shapes], scratch_shapes=[*scratch, *comm.scratch],
                  input_output_aliases={n_in + k: n_out + v for k, v in comm.aliases.items()},
                  compiler_params=_params(len(grid), side_effects=True))(*operands, *comm.operands)


def _gather_level1(shards):
    n = len(shards)

    def copies(ins, outs, sems):
        send_sems, recv_sems, local_sems = sems
        x, y, c = _place()
        me, sibling = (x, y, c), (x, y, 1 - c)
        chips = _other_chips(x, y)

        def copy(a, k, block, to, src=None):
            slot = outs[a].at[_dev_index(block)]
            return pltpu.make_async_remote_copy(src_ref=slot if src is None else src, dst_ref=slot, send_sem=send_sems.at[a, k],
                                                recv_sem=recv_sems.at[a, k], device_id=to, device_id_type=MESH)

        mine = [pltpu.make_async_copy(ins[a], outs[a].at[_dev_index(me)], local_sems.at[a]) for a in range(n)]
        sends = [copy(a, 1 + j, me, (*chip, c), src=ins[a]) for j, chip in enumerate(chips) for a in range(n)]
        sends += [copy(a, 0, me, sibling, src=ins[a]) for a in range(n)]
        recvs = [copy(a, 1 + j, (*chip, c), me) for j, chip in enumerate(chips) for a in range(n)]
        recvs += [copy(a, 0, sibling, me) for a in range(n)]
        return mine, sends, recvs

    def start(ins, outs, sems):
        mine, sends, _ = copies(ins, outs, sems)
        for cp in mine + sends:
            cp.start()

    def finish(ins, outs, sems):
        mine, sends, recvs = copies(ins, outs, sems)
        for cp in recvs:
            cp.wait_recv()
        for cp in sends:
            cp.wait_send()
        for cp in mine:
            cp.wait()

    return _Comm(shards, [_sds((N_DEV, *a.shape), a.dtype) for a in shards], {},
                 [pltpu.SemaphoreType.DMA((n, 4)), pltpu.SemaphoreType.DMA((n, 4)), pltpu.SemaphoreType.DMA((n,))], start, finish)


def _gather_level2(bufs):
    n = len(bufs)

    def copies(outs, sems):
        send_sems, recv_sems = sems
        x, y, c = _place()
        sibling = (x, y, 1 - c)
        sends, recvs = [], []
        for j, chip in enumerate(_other_chips(x, y)):
            for a in range(n):
                have, want = outs[a].at[_dev_index((*chip, c))], outs[a].at[_dev_index((*chip, 1 - c))]
                sends.append(pltpu.make_async_remote_copy(src_ref=have, dst_ref=have, send_sem=send_sems.at[a, j], recv_sem=recv_sems.at[a, j],
                                                          device_id=sibling, device_id_type=MESH))
                recvs.append(pltpu.make_async_remote_copy(src_ref=want, dst_ref=want, send_sem=send_sems.at[a, j], recv_sem=recv_sems.at[a, j],
                                                          device_id=sibling, device_id_type=MESH))
        return sends, recvs

    def start(ins, outs, sems):
        for cp in copies(outs, sems)[0]:
            cp.start()

    def finish(ins, outs, sems):
        sends, recvs = copies(outs, sems)
        for cp in recvs:
            cp.wait_recv()
        for cp in sends:
            cp.wait_send()

    return _Comm(bufs, [_sds(b.shape, b.dtype) for b in bufs], {a: a for a in range(n)},
                 [pltpu.SemaphoreType.DMA((n, 3)), pltpu.SemaphoreType.DMA((n, 3))], start, finish)


def _all_gather(name, arrays):
    n = len(arrays)

    def body(*refs):
        ins = refs[:n]
        outs = refs[n:2 * n]
        send_sems, recv_sems, local_sems = refs[2 * n:]
        x, y, c = _place()
        me, sibling = (x, y, c), (x, y, 1 - c)
        chips = _other_chips(x, y)

        def copy(a, k, block, to, src=None):
            slot = outs[a].at[_dev_index(block)]
            return pltpu.make_async_remote_copy(src_ref=slot if src is None else src, dst_ref=slot, send_sem=send_sems.at[a, k],
                                                recv_sem=recv_sems.at[a, k], device_id=to, device_id_type=MESH)

        mine = [pltpu.make_async_copy(ins[a], outs[a].at[_dev_index(me)], local_sems.at[a]) for a in range(n)]
        for cp in mine:
            cp.start()
        first = []
        for j, chip in enumerate(chips):
            first += [copy(a, 1 + j, me, (*chip, c), src=ins[a]) for a in range(n)]
        first += [copy(a, 0, me, sibling, src=ins[a]) for a in range(n)]
        for cp in first:
            cp.start()
        passed = []
        for j, chip in enumerate(chips):
            for a in range(n):
                copy(a, 1 + j, (*chip, c), me).wait_recv()
                fwd = copy(a, 4 + j, (*chip, c), sibling)
                fwd.start()
                passed.append(fwd)
        for a in range(n):
            copy(a, 0, sibling, me).wait_recv()
            for j, chip in enumerate(chips):
                copy(a, 4 + j, (*chip, 1 - c), me).wait_recv()
        for cp in first + passed:
            cp.wait_send()
        for cp in mine:
            cp.wait()

    return _pcall(
        body, name=name, in_specs=[ANY] * n, out_specs=[ANY] * n,
        out_shape=[_sds((N_DEV, *a.shape), a.dtype) for a in arrays],
        scratch_shapes=[pltpu.SemaphoreType.DMA((n, 7)), pltpu.SemaphoreType.DMA((n, 7)), pltpu.SemaphoreType.DMA((n,))],
        compiler_params=pltpu.CompilerParams(has_side_effects=True),
    )(*arrays)


def _sibling_exchange(grads):
    n = len(grads)

    def start(ins, outs, sems):
        send_sems, recv_sems = sems
        x, y, c = _place()
        for a in range(n):
            for ch in range(N_CHIP):
                pltpu.make_async_remote_copy(src_ref=ins[a].at[2 * ch + 1 - c], dst_ref=outs[a].at[ch], send_sem=send_sems.at[a],
                                             recv_sem=recv_sems.at[a], device_id=(x, y, 1 - c), device_id_type=MESH).start()

    def finish(ins, outs, sems):
        send_sems, recv_sems = sems
        x, y, c = _place()
        for a in range(n):
            pltpu.make_async_remote_copy(src_ref=outs[a], dst_ref=outs[a], send_sem=send_sems.at[a], recv_sem=recv_sems.at[a],
                                         device_id=(x, y, 1 - c), device_id_type=MESH).wait()

    return _Comm(grads, [_sds((N_CHIP, *g.shape[1:]), g.dtype) for g in grads], {},
                 [pltpu.SemaphoreType.DMA((n,)), pltpu.SemaphoreType.DMA((n,))], start, finish)


def _chip_exchange(parts, slots, layers, stacked):
    n = len(parts)
    names = []
    for nm, _ in slots:
        if nm not in names:
            names.append(nm)
    shapes = {nm: _sds((N_CHIP, layers[nm], *parts[a].shape[1:]), parts[a].dtype) for a, (nm, _) in enumerate(slots)}
    kept = [nm for nm in names if stacked.get(nm) is not None]
    aliases = {n + k: names.index(nm) for k, nm in enumerate(kept)}

    def copies(ins, outs, sems):
        send_sems, recv_sems, local_sems = sems
        x, y, c = _place()
        mine = 2 * x + y
        local, sends, recvs = [], [], []
        for a, (nm, l) in enumerate(slots):
            buf = outs[names.index(nm)]
            local.append(pltpu.make_async_copy(ins[a].at[mine], buf.at[mine, l], local_sems.at[a]))
            for j, chip in enumerate(_other_chips(x, y)):
                theirs = buf.at[2 * chip[0] + chip[1], l]
                sends.append(pltpu.make_async_remote_copy(src_ref=ins[a].at[2 * chip[0] + chip[1]], dst_ref=buf.at[mine, l], send_sem=send_sems.at[a, j],
                                                          recv_sem=recv_sems.at[a, j], device_id=(*chip, c), device_id_type=MESH))
                recvs.append(pltpu.make_async_remote_copy(src_ref=theirs, dst_ref=theirs, send_sem=send_sems.at[a, j],
                                                          recv_sem=recv_sems.at[a, j], device_id=(*chip, c), device_id_type=MESH))
        return local, sends, recvs

    def start(ins, outs, sems):
        local, sends, _ = copies(ins, outs, sems)
        for cp in local + sends:
            cp.start()

    def finish(ins, outs, sems):
        local, sends, recvs = copies(ins, outs, sems)
        for cp in recvs:
            cp.wait_recv()
        for cp in sends:
            cp.wait_send()
        for cp in local:
            cp.wait()

    comm = _Comm([*parts, *[stacked[nm] for nm in kept]], [shapes[nm] for nm in names], aliases,
                 [pltpu.SemaphoreType.DMA((n, 3)), pltpu.SemaphoreType.DMA((n, 3)), pltpu.SemaphoreType.DMA((n,))], start, finish)
    return comm, names


def _matmul(name, a, b, extras, *, grid, a_spec, b_spec, extra_specs, out_shapes, out_specs, dims, k_axis=None, nk=1,
            acc_shape=None, epilogue=None, comm=None):
    n_extra = len(extras)
    n_out = len(out_shapes)

    def body(*refs):
        a_ref, b_ref = refs[0], refs[1]
        ex = refs[2:2 + n_extra]
        outs = refs[2 + n_extra:2 + n_extra + n_out]
        prod = lax.dot_general(a_ref[...], b_ref[...], dims, preferred_element_type=F32)

        def finish(acc):
            res = epilogue(acc, *[e[...] for e in ex]) if epilogue is not None else (acc,)
            for o, r in zip(outs, res):
                o[...] = r.astype(o.dtype)

        if k_axis is None:
            finish(prod)
        else:
            acc_ref = refs[-1]
            k = pl.program_id(k_axis)

            @pl.when(k == 0)
            def _():
                acc_ref[...] = prod

            @pl.when(k > 0)
            def _():
                acc_ref[...] += prod

            @pl.when(k == nk - 1)
            def _():
                finish(acc_ref[...])

    scratch = [] if k_axis is None else [pltpu.VMEM(acc_shape, F32)]
    return _call(name, body, [a, b, *extras], [a_spec, b_spec, *extra_specs], list(out_shapes), list(out_specs), scratch, grid, comm)


def _rowwise(name, fn, operands, *, grid, in_specs, out_shapes, out_specs, n_acc=0, grid_spec_prefetch=None):
    n_in = len(operands)
    n_out = len(out_shapes)
    n_pre = 0 if grid_spec_prefetch is None else 1

    def body(*refs):
        refs = refs[n_pre:]
        ins = refs[:n_in]
        outs = refs[n_in:n_in + n_out]
        res = fn(*[r[...] for r in ins])
        if not isinstance(res, (tuple, list)):
            res = (res,)
        first = pl.program_id(0) == 0
        for d in range(1, len(grid)):
            first = jnp.logical_and(first, pl.program_id(d) == 0)
        for idx, (o, r) in enumerate(zip(outs, res)):
            if idx < n_out - n_acc:
                o[...] = r.astype(o.dtype)
            else:
                @pl.when(first)
                def _(o=o, r=r):
                    o[...] = r.astype(o.dtype)

                @pl.when(jnp.logical_not(first))
                def _(o=o, r=r):
                    o[...] += r.astype(o.dtype)

    if grid_spec_prefetch is None:
        return _pcall(body, name=name, grid=grid, in_specs=in_specs, out_specs=out_specs, out_shape=out_shapes,
                      compiler_params=_params(len(grid)))(*operands)
    gs = pltpu.PrefetchScalarGridSpec(num_scalar_prefetch=1, grid=grid, in_specs=in_specs, out_specs=out_specs)
    return _pcall(body, name=name, grid_spec=gs, out_shape=out_shapes,
                  compiler_params=_params(len(grid)))(grid_spec_prefetch, *operands)


def _row_spec(tm, w):
    return pl.BlockSpec((tm, w), lambda i: (i, 0))


def _const_spec(shape):
    nd = len(shape)
    return pl.BlockSpec(tuple(shape), lambda *_: (0,) * nd)


def _rms_fwd(x, g):
    r = lax.rsqrt(jnp.mean(x * x, axis=-1, keepdims=True) + NORM_EPS)
    return x * r * g


def _rms_bwd(x, g, dy):
    r = lax.rsqrt(jnp.mean(x * x, axis=-1, keepdims=True) + NORM_EPS)
    xh = x * r
    u = dy * g
    dx = r * (u - xh * jnp.mean(u * xh, axis=-1, keepdims=True))
    dg = jnp.sum(dy * xh, axis=0, keepdims=True)
    return dx, dg


def _gelu(z):
    return 0.5 * z * (1.0 + lax.erf(z * (2.0 ** -0.5)))


def _gelu_grad(z):
    return 0.5 * (1.0 + lax.erf(z * (2.0 ** -0.5))) + z * jnp.exp(-0.5 * z * z) * ((2.0 * math.pi) ** -0.5)


def _rope_fwd(x, cc, sa, sb):
    return x * cc + pltpu.roll(x, 96, 1) * sa + pltpu.roll(x, 32, 1) * sb


def _rope_bwd(d, cc, sa, sb):
    return d * cc + pltpu.roll(d * sa, 32, 1) + pltpu.roll(d * sb, 96, 1)


def _adam(w, g, m, v):
    m = ADAM_B1 * m + (1.0 - ADAM_B1) * g
    v = ADAM_B2 * v + (1.0 - ADAM_B2) * (g * g)
    m_hat = m / (1.0 - ADAM_B1 ** ADAM_STEP)
    v_hat = v / (1.0 - ADAM_B2 ** ADAM_STEP)
    delta = -ADAM_LR * (m_hat / (jnp.sqrt(v_hat) + ADAM_EPS) + ADAM_WD * w)
    return delta, m, v


def _flash_fwd(q, k, vt, tq, comm=None):
    h, t, _ = q.shape
    nq = t // tq

    def body(q_ref, k_ref, vt_ref, o_ref, lse_ref):
        qi = pl.program_id(1)

        def step(kj, state, masked):
            m_old, l_old, acc_old = state
            kb = k_ref[pl.ds(pl.multiple_of(kj * tq, tq), tq), :]
            st = lax.dot_general(kb, q_ref[...], NT, preferred_element_type=F32)
            if masked:
                key = lax.broadcasted_iota(jnp.int32, (tq, tq), 0)
                qry = lax.broadcasted_iota(jnp.int32, (tq, tq), 1)
                st = jnp.where(key <= qry, st, NEG)
            m_new = jnp.maximum(m_old, jnp.max(st, axis=0, keepdims=True))
            alpha = jnp.exp2((m_old - m_new) * EXP2_SCALE)
            pt = jnp.exp2((st - m_new) * EXP2_SCALE)
            l_new = alpha * l_old + jnp.sum(pt, axis=0, keepdims=True)
            acc_new = alpha * acc_old + lax.dot_general(vt_ref[kj], pt.astype(BF16), NN, preferred_element_type=F32)
            return m_new, l_new, acc_new

        init = (jnp.full((1, tq), NEG, F32), jnp.zeros((1, tq), F32), jnp.zeros((VDIM, tq), F32))
        state = lax.fori_loop(0, qi, lambda kj, st_: step(kj, st_, False), init)
        m, l, acc = step(qi, state, True)
        o_ref[...] = (acc / l).T.astype(o_ref.dtype)
        lse_ref[...] = m * EXP2_SCALE + jnp.log2(l)

    return _call(
        "flash_fwd", body, [q, k, vt],
        [pl.BlockSpec((None, tq, QPAD), lambda hh, i: (hh, i, 0)),
         pl.BlockSpec((None, t, QPAD), lambda hh, i: (hh, 0, 0)),
         pl.BlockSpec((None, nq, VDIM, tq), lambda hh, i: (hh, 0, 0, 0))],
        [_sds((t, h * VDIM), BF16), _sds((h, nq, 1, tq), F32)],
        [pl.BlockSpec((tq, VDIM), lambda hh, i: (i, hh)),
         pl.BlockSpec((None, None, 1, tq), lambda hh, i: (hh, i, 0, 0))],
        [], (h, nq), comm)


def _flash_bwd(q, k, v, o, do, lse, tq, comm=None):
    h, t, _ = q.shape
    nq = t // tq

    def body(q_ref, k_ref, v_ref, o_ref, do_ref, lse_ref, dqt_ref, dk_ref, dv_ref, delta_ref):
        kj = pl.program_id(1)

        @pl.when(kj == 0)
        def _():
            dqt_ref[...] = jnp.zeros_like(dqt_ref)
            ones = jnp.ones((8, VDIM), BF16)
            for qi in range(nq):
                rows = pl.ds(qi * tq, tq)
                prod = do_ref[rows, :].astype(F32) * o_ref[rows, :].astype(F32)
                hi = prod.astype(BF16)
                lo = (prod - hi.astype(F32)).astype(BF16)
                delta_ref[qi] = (lax.dot_general(ones, hi, NT, preferred_element_type=F32)
                                 + lax.dot_general(ones, lo, NT, preferred_element_type=F32))

        kb = k_ref[...]
        vb = v_ref[...]
        kbt = kb.astype(F32).T.astype(BF16)
        dk_ref[...] = jnp.zeros_like(dk_ref)
        dv_ref[...] = jnp.zeros_like(dv_ref)

        def step(qi, masked):
            rows = pl.ds(pl.multiple_of(qi * tq, tq), tq)
            qb = q_ref[rows, :]
            dob = do_ref[rows, :]
            st = lax.dot_general(kb, qb, NT, preferred_element_type=F32)
            pt = jnp.exp2(st * EXP2_SCALE - lse_ref[qi])
            if masked:
                key = lax.broadcasted_iota(jnp.int32, (tq, tq), 0)
                qry = lax.broadcasted_iota(jnp.int32, (tq, tq), 1)
                pt = jnp.where(key <= qry, pt, 0.0)
            dv_ref[...] += lax.dot_general(pt.astype(BF16), dob, NN, preferred_element_type=F32)
            dpt = lax.dot_general(vb, dob, NT, preferred_element_type=F32)
            dst = (pt * (dpt - delta_ref[qi, pl.ds(0, 1), :]) * ATTN_SCALE).astype(BF16)
            dk_ref[...] += lax.dot_general(dst, qb, NN, preferred_element_type=F32)
            dqt_ref[qi] += lax.dot_general(kbt, dst, NN, preferred_element_type=F32)

        step(kj, True)

        def loop_body(qi, carry):
            step(qi, False)
            return carry

        lax.fori_loop(kj + 1, nq, loop_body, 0)

    return _call(
        "flash_bwd", body, [q, k, v, o, do, lse],
        [pl.BlockSpec((None, t, QPAD), lambda hh, j: (hh, 0, 0)),
         pl.BlockSpec((None, tq, QPAD), lambda hh, j: (hh, j, 0)),
         pl.BlockSpec((None, tq, VDIM), lambda hh, j: (hh, j, 0)),
         pl.BlockSpec((t, VDIM), lambda hh, j: (0, hh)),
         pl.BlockSpec((t, VDIM), lambda hh, j: (0, hh)),
         pl.BlockSpec((None, nq, 1, tq), lambda hh, j: (hh, 0, 0, 0))],
        [_sds((h, nq, QPAD, tq), F32), _sds((h, t, QPAD), F32), _sds((h, t, VDIM), F32)],
        [pl.BlockSpec((None, nq, QPAD, tq), lambda hh, j: (hh, 0, 0, 0)),
         pl.BlockSpec((None, tq, QPAD), lambda hh, j: (hh, j, 0)),
         pl.BlockSpec((None, tq, VDIM), lambda hh, j: (hh, j, 0))],
        [pltpu.VMEM((nq, 8, tq), F32)], (h, nq), comm)


def _tril_bf16(w):
    row = lax.broadcasted_iota(jnp.int32, w.shape, 0)
    col = lax.broadcasted_iota(jnp.int32, w.shape, 1)
    return jnp.where(col <= row, w, 0.0).astype(BF16)


def _layer_norm_parts(v0):
    mu = jnp.mean(v0, axis=-1, keepdims=True)
    vc = v0 - mu
    rstd = lax.rsqrt(jnp.mean(vc * vc, axis=-1, keepdims=True) + LN_EPS)
    return vc * rstd, rstd


def _sgu_mid_fwd(ge, ln_g, ln_b, w_sp, b_sp, chunks_per_step):
    t, e2 = ge.shape
    e = e2 // 2
    gd = e // SGU_GROUPS
    rows = SGU_CHUNK * chunks_per_step

    def body(u_ref, v_ref, g_ref, b_ref, w_ref, bs_ref, gate_ref):
        for ck in range(chunks_per_step):
            r = pl.ds(ck * SGU_CHUNK, SGU_CHUNK)
            xh, _ = _layer_norm_parts(v_ref[r, :].astype(F32))
            v1 = (xh * g_ref[...] + b_ref[...]).astype(BF16)
            for g in range(SGU_GROUPS):
                cols = pl.ds(g * gd, gd)
                mixed = lax.dot_general(_tril_bf16(w_ref[g]), v1[:, g * gd:(g + 1) * gd], NN, preferred_element_type=F32) + bs_ref[g]
                gate_ref[r, cols] = (u_ref[r, cols].astype(F32) * mixed).astype(BF16)

    return _pcall(
        body, name="sgu_mid_fwd", grid=(t // rows,),
        in_specs=[pl.BlockSpec((rows, e), lambda i: (i, 0)), pl.BlockSpec((rows, e), lambda i: (i, 1)),
                  _const_spec((1, e)), _const_spec((1, e)), _const_spec(w_sp.shape), _const_spec(b_sp.shape)],
        out_specs=pl.BlockSpec((rows, e), lambda i: (i, 0)),
        out_shape=_sds((t, e), BF16), compiler_params=_params(1),
    )(ge, ge, ln_g, ln_b, w_sp, b_sp)


def _sgu_mid_bwd(ge, z, dgate, ln_g, ln_b, w_sp, b_sp, chunks_per_step):
    t, e2 = ge.shape
    e = e2 // 2
    gd = e // SGU_GROUPS
    rows = SGU_CHUNK * chunks_per_step

    def body(u_ref, v_ref, zu_ref, zv_ref, dg_ref, g_ref, b_ref, w_ref, bs_ref, dz_ref, dw_ref, dbs_ref, dlg_ref, dlb_ref):
        @pl.when(pl.program_id(0) == 0)
        def _():
            dw_ref[...] = jnp.zeros_like(dw_ref)
            dbs_ref[...] = jnp.zeros_like(dbs_ref)
            dlg_ref[...] = jnp.zeros_like(dlg_ref)
            dlb_ref[...] = jnp.zeros_like(dlb_ref)

        for ck in range(chunks_per_step):
            r = pl.ds(ck * SGU_CHUNK, SGU_CHUNK)
            xh, rstd = _layer_norm_parts(v_ref[r, :].astype(F32))
            v1 = (xh * g_ref[...] + b_ref[...]).astype(BF16)
            dv1_parts = []
            for g in range(SGU_GROUPS):
                cols = pl.ds(g * gd, gd)
                wc = _tril_bf16(w_ref[g])
                v1g = v1[:, g * gd:(g + 1) * gd]
                mixed = lax.dot_general(wc, v1g, NN, preferred_element_type=F32) + bs_ref[g]
                dgate = dg_ref[r, cols].astype(F32)
                dmixed = dgate * u_ref[r, cols].astype(F32)
                du = dgate * mixed
                dz_ref[r, cols] = (du * _gelu_grad(zu_ref[r, cols].astype(F32))).astype(BF16)
                dbs_ref[g] += jnp.sum(dmixed, axis=1, keepdims=True)
                dmb = dmixed.astype(BF16)
                dwg = lax.dot_general(dmb, v1g, NT, preferred_element_type=F32)
                row = lax.broadcasted_iota(jnp.int32, dwg.shape, 0)
                col = lax.broadcasted_iota(jnp.int32, dwg.shape, 1)
                dw_ref[g] += jnp.where(col <= row, dwg, 0.0)
                dv1_parts.append(lax.dot_general(wc, dmb, TN, preferred_element_type=F32))
            dv1 = jnp.concatenate(dv1_parts, axis=1)
            dlg_ref[...] += jnp.sum(dv1 * xh, axis=0, keepdims=True)
            dlb_ref[...] += jnp.sum(dv1, axis=0, keepdims=True)
            dxh = dv1 * g_ref[...]
            dv0 = rstd * (dxh - jnp.mean(dxh, axis=-1, keepdims=True) - xh * jnp.mean(dxh * xh, axis=-1, keepdims=True))
            dz_ref[r, pl.ds(e, e)] = (dv0 * _gelu_grad(zv_ref[r, :].astype(F32))).astype(BF16)

    half0 = pl.BlockSpec((rows, e), lambda i: (i, 0))
    half1 = pl.BlockSpec((rows, e), lambda i: (i, 1))
    return _pcall(
        body, name="sgu_mid_bwd", grid=(t // rows,),
        in_specs=[half0, half1, half0, half1, half0, _const_spec((1, e)), _const_spec((1, e)), _const_spec(w_sp.shape), _const_spec(b_sp.shape)],
        out_specs=[pl.BlockSpec((rows, e2), lambda i: (i, 0)), _const_spec(w_sp.shape), _const_spec(b_sp.shape), _const_spec((1, e)), _const_spec((1, e))],
        out_shape=[_sds((t, e2), BF16), _sds(w_sp.shape, F32), _sds(b_sp.shape, F32), _sds((1, e), F32), _sds((1, e), F32)],
        compiler_params=_params(1),
    )(ge, ge, z, z, dgate, ln_g, ln_b, w_sp, b_sp)


def kernel(x, positions, norm_mix, norm_ffn, final_norm, mla_w_dkv, mla_q_norm, mla_kv_norm, mla_w_uq, mla_w_ukv, mla_w_o, sgu_w_in, sgu_ln_g, sgu_ln_b, sgu_w_spatial, sgu_b_spatial, sgu_w_out, ffn_w_up, ffn_w_down, loss_target, m_norm_mix, m_norm_ffn, m_final_norm, m_mla_w_dkv, m_mla_q_norm, m_mla_kv_norm, m_mla_w_uq, m_mla_w_ukv, m_mla_w_o, m_sgu_w_in, m_sgu_ln_g, m_sgu_ln_b, m_sgu_w_spatial, m_sgu_b_spatial, m_sgu_w_out, m_ffn_w_up, m_ffn_w_down, v_norm_mix, v_norm_ffn, v_final_norm, v_mla_w_dkv, v_mla_q_norm, v_mla_kv_norm, v_mla_w_uq, v_mla_w_ukv, v_mla_w_o, v_sgu_w_in, v_sgu_ln_g, v_sgu_ln_b, v_sgu_w_spatial, v_sgu_b_spatial, v_sgu_w_out, v_ffn_w_up, v_ffn_w_down):
    _, T, D = x.shape
    depth = norm_mix.shape[0]
    n_mla, n_sgu = mla_w_dkv.shape[0], sgu_w_in.shape[0]
    assert depth % 2 == 0
    FF = ffn_w_up.shape[2] * N_DEV
    E = sgu_w_out.shape[1] * N_DEV
    ffc, ec, e2c = FF // N_DEV, E // N_DEV, 2 * E // N_DEV
    dc = D // N_DEV
    OW = HEADS * VDIM
    owc = OW // N_DEV
    tm = _tile(T, 1024)
    tq = _tile(T, 512)
    ts = _tile(T, 256)
    nt = T // tm
    x2 = x.reshape(T, D)
    tgt = loss_target.reshape(T, D)
    cidx = lax.axis_index("c").astype(jnp.int32).reshape(1)

    ln_local = jnp.concatenate([sgu_ln_g, sgu_ln_b, jnp.zeros((8 - 2 * n_sgu, ec), F32)], axis=0)
    g_dkv, g_uq, g_ukv, g_o, g_ln = _all_gather(
        "gather_small_weights", [w.astype(BF16) for w in (mla_w_dkv, mla_w_uq, mla_w_ukv, mla_w_o)] + [ln_local])
    w_dkv = jnp.pad(g_dkv.transpose(1, 0, 2, 3).reshape(n_mla, D, LAT), ((0, 0), (0, 0), (0, LAT_PAD - LAT)))
    w_uq = jnp.pad(g_uq.transpose(1, 0, 2, 3), ((0, 0), (0, 0), (0, 0), (0, QPAD - NOPE - ROPE)))
    w_ukv = g_ukv.transpose(1, 0, 2, 3)
    w_o = g_o.transpose(1, 0, 2, 3).reshape(n_mla, OW, D)
    ln_g_full = [g_ln[:, l, :].reshape(1, E) for l in range(n_sgu)]
    ln_b_full = [g_ln[:, n_sgu + l, :].reshape(1, E) for l in range(n_sgu)]
    b_sp = sgu_b_spatial.reshape(n_sgu, SGU_GROUPS, SGU_CHUNK, 1)
    up_sh = [ffn_w_up[i].astype(BF16) for i in range(depth)]
    down_sh = [ffn_w_down[i].astype(BF16) for i in range(depth)]
    in_sh = [sgu_w_in[l].astype(BF16) for l in range(n_sgu)]
    out_sh = [sgu_w_out[l].astype(BF16) for l in range(n_sgu)]
    g_up, g_down, g_in, g_out = [None] * depth, [None] * depth, [None] * n_sgu, [None] * n_sgu

    inv_freq = ROPE_THETA ** (-jnp.arange(0, ROPE, 2, dtype=F32) / ROPE)
    zeros32 = jnp.zeros((ROPE // 2,), F32)
    inv128 = jnp.concatenate([inv_freq, inv_freq, zeros32, zeros32]).reshape(1, 128)
    sel_a = jnp.concatenate([-jnp.ones((32,), F32), zeros32, zeros32, zeros32]).reshape(1, 128)
    sel_b = jnp.concatenate([zeros32, jnp.ones((32,), F32), zeros32, zeros32]).reshape(1, 128)
    sel_c = jnp.concatenate([jnp.ones((64,), F32), zeros32, zeros32]).reshape(1, 128)

    def rope_tables(pos, inv, sa, sb, sc):
        ang = pos.astype(F32) * inv
        cs, sn = jnp.cos(ang), jnp.sin(ang)
        return cs * sc, sn * sa, sn * sb

    t_cc, t_sa, t_sb = _rowwise(
        "rope_tables", rope_tables, [positions.reshape(T, 1), inv128, sel_a, sel_b, sel_c], grid=(nt,),
        in_specs=[_row_spec(tm, 1)] + [_const_spec((1, 128))] * 4,
        out_shapes=[_sds((T, 128), F32)] * 3, out_specs=[_row_spec(tm, 128)] * 3)
    tab_specs = [_row_spec(tm, 128)] * 3

    def rmsnorm(xv, g):
        return _rowwise("rmsnorm", lambda a, gg: _rms_fwd(a, gg), [xv, g.reshape(1, D)], grid=(nt,),
                        in_specs=[_row_spec(tm, D), _const_spec((1, D))], out_shapes=_sds((T, D), BF16), out_specs=_row_spec(tm, D))

    def rmsnorm_bwd(xv, g, dh, dx_in):
        def fn(a, gg, d, dxi):
            dx, dg = _rms_bwd(a, gg, d)
            return dxi + dx, dxi + dx, dg
        return _rowwise("rmsnorm_bwd", fn, [xv, g.reshape(1, D), dh, dx_in], grid=(nt,),
                        in_specs=[_row_spec(tm, D), _const_spec((1, D)), _row_spec(tm, D), _row_spec(tm, D)],
                        out_shapes=[_sds((T, D), F32), _sds((T, D), BF16), _sds((1, D), F32)],
                        out_specs=[_row_spec(tm, D), _row_spec(tm, D), _const_spec((1, D))], n_acc=1)

    def proj_cols(name, h, gw, nc, epilogue, n_out):
        return _matmul(name, h, gw, [], grid=(N_DEV, nt),
                       a_spec=pl.BlockSpec((tm, D), lambda j, i: (i, 0)),
                       b_spec=pl.BlockSpec((None, D, nc), lambda j, i: (j, 0, 0)), extra_specs=[],
                       out_shapes=[_sds((T, nc * N_DEV), BF16)] * n_out, out_specs=[pl.BlockSpec((tm, nc), lambda j, i: (i, j))] * n_out,
                       dims=NN, epilogue=epilogue)

    def proj_rows_residual(name, a, gw, kc, xres):
        return _matmul(name, a, gw, [xres], grid=(nt, N_DEV),
                       a_spec=pl.BlockSpec((tm, kc), lambda i, k: (i, k)),
                       b_spec=pl.BlockSpec((None, kc, D), lambda i, k: (k, 0, 0)),
                       extra_specs=[pl.BlockSpec((tm, D), lambda i, k: (i, 0))],
                       out_shapes=[_sds((T, D), F32)], out_specs=[pl.BlockSpec((tm, D), lambda i, k: (i, 0))],
                       dims=NN, k_axis=1, nk=N_DEV, acc_shape=(tm, D), epilogue=lambda acc, xr: (acc + xr,))[0]

    def back_rows(name, dy, gw, kc, extras, epilogue):
        return _matmul(name, dy, gw, extras, grid=(N_DEV, nt),
                       a_spec=pl.BlockSpec((tm, D), lambda j, i: (i, 0)),
                       b_spec=pl.BlockSpec((None, kc, D), lambda j, i: (j, 0, 0)),
                       extra_specs=[pl.BlockSpec((tm, kc), lambda j, i: (i, j))] * len(extras),
                       out_shapes=[_sds((T, kc * N_DEV), BF16)], out_specs=[pl.BlockSpec((tm, kc), lambda j, i: (i, j))],
                       dims=NT, epilogue=epilogue)[0]

    def back_cols(name, da, gw, nc):
        return _matmul(name, da, gw, [], grid=(nt, N_DEV),
                       a_spec=pl.BlockSpec((tm, nc), lambda i, k: (i, k)),
                       b_spec=pl.BlockSpec((None, D, nc), lambda i, k: (k, 0, 0)), extra_specs=[],
                       out_shapes=[_sds((T, D), F32)], out_specs=[pl.BlockSpec((tm, D), lambda i, k: (i, 0))],
                       dims=NT, k_axis=1, nk=N_DEV, acc_shape=(tm, D))[0]

    def wgrad_cols(name, h, da, nc):
        return _matmul(name, h, da, [], grid=(N_DEV, nt),
                       a_spec=pl.BlockSpec((tm, D), lambda j, t: (t, 0)), b_spec=pl.BlockSpec((tm, nc), lambda j, t: (t, j)),
                       extra_specs=[], out_shapes=[_sds((N_DEV, D, nc), BF16)],
                       out_specs=[pl.BlockSpec((None, D, nc), lambda j, t: (j, 0, 0))],
                       dims=TN, k_axis=1, nk=nt, acc_shape=(D, nc))[0]

    def wgrad_rows(name, a, dy, kc, ncols):
        return _matmul(name, a, dy, [], grid=(a.shape[1] // kc, nt),
                       a_spec=pl.BlockSpec((tm, kc), lambda j, t: (t, j)), b_spec=pl.BlockSpec((tm, ncols), lambda j, t: (t, 0)),
                       extra_specs=[], out_shapes=[_sds((a.shape[1], ncols), BF16)],
                       out_specs=[pl.BlockSpec((kc, ncols), lambda j, t: (j, 0))],
                       dims=TN, k_axis=1, nk=nt, acc_shape=(kc, ncols))[0]

    saved = []
    xs = x2
    for i in range(depth):
        l = i // 2
        h = rmsnorm(xs, norm_mix[i])
        if i % 2 == 0:
            lat = _matmul("mla_down", h, w_dkv, [], grid=(nt,), a_spec=_row_spec(tm, D),
                          b_spec=pl.BlockSpec((None, D, LAT_PAD), lambda i_: (l, 0, 0)), extra_specs=[],
                          out_shapes=[_sds((T, LAT_PAD), F32)], out_specs=[_row_spec(tm, LAT_PAD)], dims=NN)[0]

            def latent_post(la, qn, kvn, cc, sa, sb):
                cq = _rms_fwd(la[:, :Q_RANK], qn)
                ckv = _rms_fwd(la[:, Q_RANK:Q_RANK + KV_RANK], kvn)
                kr = _rope_fwd(la[:, Q_RANK + KV_RANK:], cc, sa, sb)
                return cq, ckv, kr

            cq, ckv, kr = _rowwise(
                "mla_latent", latent_post, [lat, mla_q_norm[l].reshape(1, Q_RANK), mla_kv_norm[l].reshape(1, KV_RANK), t_cc, t_sa, t_sb],
                grid=(nt,), in_specs=[_row_spec(tm, LAT_PAD), _const_spec((1, Q_RANK)), _const_spec((1, KV_RANK))] + tab_specs,
                out_shapes=[_sds((T, Q_RANK), BF16), _sds((T, KV_RANK), BF16), _sds((T, 128), BF16)],
                out_specs=[_row_spec(tm, Q_RANK), _row_spec(tm, KV_RANK), _row_spec(tm, 128)])

            def q_epilogue(acc, cc, sa, sb):
                return (jnp.concatenate([acc[:, :NOPE], _rope_fwd(acc[:, NOPE:], cc, sa, sb)], axis=1),)

            q = _matmul("mla_q", cq, w_uq, [t_cc, t_sa, t_sb], grid=(HEADS, nt),
                        a_spec=pl.BlockSpec((tm, Q_RANK), lambda b, i_: (i_, 0)),
                        b_spec=pl.BlockSpec((None, None, Q_RANK, QPAD), lambda b, i_: (l, b, 0, 0)),
                        extra_specs=[pl.BlockSpec((tm, 128), lambda b, i_: (i_, 0))] * 3,
                        out_shapes=[_sds((HEADS, T, QPAD), BF16)], out_specs=[pl.BlockSpec((None, tm, QPAD), lambda b, i_: (b, i_, 0))],
                        dims=NN, epilogue=q_epilogue)[0]

            def kv_epilogue(acc, krb):
                return jnp.concatenate([acc[:, :NOPE], krb.astype(F32)], axis=1), acc[:, NOPE:], acc[:, NOPE:].T

            kk, vv, vt = _matmul("mla_kv", ckv, w_ukv, [kr], grid=(HEADS, T // tq),
                                 a_spec=pl.BlockSpec((tq, KV_RANK), lambda b, i_: (i_, 0)),
                                 b_spec=pl.BlockSpec((None, None, KV_RANK, NOPE + VDIM), lambda b, i_: (l, b, 0, 0)),
                                 extra_specs=[pl.BlockSpec((tq, 128), lambda b, i_: (i_, 0))],
                                 out_shapes=[_sds((HEADS, T, QPAD), BF16), _sds((HEADS, T, VDIM), BF16), _sds((HEADS, T // tq, VDIM, tq), BF16)],
                                 out_specs=[pl.BlockSpec((None, tq, QPAD), lambda b, i_: (b, i_, 0)), pl.BlockSpec((None, tq, VDIM), lambda b, i_: (b, i_, 0)),
                                            pl.BlockSpec((None, None, VDIM, tq), lambda b, i_: (b, i_, 0, 0))],
                                 dims=NN, epilogue=kv_epilogue)
            group = [up_sh[i], down_sh[i], in_sh[l], out_sh[l], up_sh[i + 1], down_sh[i + 1]]
            o, lse, *bufs = _flash_fwd(q, kk, vt, tq, comm=_gather_level1(group))
            xm, *bufs = _matmul("mla_out", o, w_o, [xs], grid=(nt,), a_spec=_row_spec(tm, OW),
                                b_spec=pl.BlockSpec((None, OW, D), lambda i_: (l, 0, 0)), extra_specs=[_row_spec(tm, D)],
                                out_shapes=[_sds((T, D), F32)], out_specs=[_row_spec(tm, D)], dims=NN, epilogue=lambda acc, xr: (acc + xr,),
                                comm=_gather_level2(bufs))
            g_up[i], g_down[i], g_in[l], g_out[l], g_up[i + 1], g_down[i + 1] = bufs
            mix_saved = (h, lat, cq, ckv, q, kk, vv, o, lse)
        else:
            z, ge = proj_cols("sgu_in", h, g_in[l], e2c, lambda acc: (acc, _gelu(acc)), 2)
            gate = _sgu_mid_fwd(ge, ln_g_full[l], ln_b_full[l], sgu_w_spatial[l], b_sp[l], 4)
            xm = proj_rows_residual("sgu_out", gate, g_out[l], ec, xs)
            mix_saved = (h, z, ge, gate)
        h2 = rmsnorm(xm, norm_ffn[i])
        r, s = proj_cols("ffn_up", h2, g_up[i], ffc, lambda acc: (jnp.maximum(acc, 0.0), jnp.square(jnp.maximum(acc, 0.0))), 2)
        xo = proj_rows_residual("ffn_down", s, g_down[i], ffc, xm)
        saved.append((xs, xm, mix_saved, h2, r, s))
        xs = xo

    def loss_head(xv, tg, g):
        y = _rms_fwd(xv, g)
        err = y - tg
        part = 0.5 * jnp.sum(jnp.sum(err * err, axis=-1, keepdims=True), axis=0, keepdims=True) / D
        dx, dg = _rms_bwd(xv, g, err / D)
        return dx, dx, jnp.broadcast_to(part, (1, 128)), dg

    dx, dyb, loss_part, d_final = _rowwise(
        "loss_head", loss_head, [xs, tgt, final_norm.reshape(1, D)], grid=(nt,),
        in_specs=[_row_spec(tm, D), _row_spec(tm, D), _const_spec((1, D))],
        out_shapes=[_sds((T, D), F32), _sds((T, D), BF16), _sds((1, 128), F32), _sds((1, D), F32)],
        out_specs=[_row_spec(tm, D), _row_spec(tm, D), _const_spec((1, 128)), _const_spec((1, D))], n_acc=2)
    loss = lax.psum(loss_part[0, 0], ("x", "y", "c"))

    d_norm_mix, d_norm_ffn = [None] * depth, [None] * depth
    d_qn, d_kvn = [None] * n_mla, [None] * n_mla
    d_wsp, d_bsp, d_lng, d_lnb = [None] * n_sgu, [None] * n_sgu, [None] * n_sgu, [None] * n_sgu
    layers = {"dkv": n_mla, "uq": n_mla, "ukv": n_mla, "o": n_mla, "in": n_sgu, "out": n_sgu, "up": depth, "down": depth}
    stacked = {nm: None for nm in layers}
    pending = []

    def add_pair(g, rcv):
        _, rws, cls = g.shape
        g4 = g.reshape(N_CHIP, 2, rws, cls)
        rt = _tile(rws, 512)
        return _rowwise("grad_pair_sum", lambda a, b_: a.astype(F32) + b_.astype(F32), [g4, rcv], grid=(N_CHIP, rws // rt),
                        in_specs=[pl.BlockSpec((None, None, rt, cls), lambda ch, i_, cr: (ch, cr[0], i_, 0)),
                                  pl.BlockSpec((None, rt, cls), lambda ch, i_, cr: (ch, i_, 0))],
                        out_shapes=_sds(rcv.shape, BF16), out_specs=pl.BlockSpec((None, rt, cls), lambda ch, i_, cr: (ch, i_, 0)),
                        grid_spec_prefetch=cidx)

    def chip_comm_of_pending():
        grads = [g for _, _, g in pending]
        from_sibling = _comm_call("grad_sibling_exchange", _sibling_exchange(grads))
        parts = [add_pair(g, rcv) for g, rcv in zip(grads, from_sibling)]
        comm, names = _chip_exchange(parts, [(nm, l_) for nm, l_, _ in pending], layers, stacked)
        pending.clear()
        return comm, names

    for i in reversed(range(depth)):
        l = i // 2
        xs_i, xm, mix_saved, h2, r, s = saved[i]
        da = back_rows("ffn_down_bwd", dyb, g_down[i], ffc, [r], lambda acc, rr: (acc * (2.0 * rr.astype(F32)),))
        pending.append(("down", i, wgrad_rows("ffn_down_wgrad", s, dyb, ffc, D).reshape(N_DEV, ffc, D)))
        pending.append(("up", i, wgrad_cols("ffn_up_wgrad", h2, da, ffc)))
        dh2 = back_cols("ffn_up_bwd", da, g_up[i], ffc)
        dx, dyb, d_norm_ffn[i] = rmsnorm_bwd(xm, norm_ffn[i], dh2, dx)
        if i % 2 == 0:
            h, lat, cq, ckv, q, kk, vv, o, lse = mix_saved
            do = _matmul("mla_out_bwd", dyb, w_o, [], grid=(nt,), a_spec=_row_spec(tm, D),
                         b_spec=pl.BlockSpec((None, OW, D), lambda i_: (l, 0, 0)), extra_specs=[],
                         out_shapes=[_sds((T, OW), BF16)], out_specs=[_row_spec(tm, OW)], dims=NT)[0]
            g_o_l = wgrad_rows("mla_out_wgrad", o, dyb, OW, D).reshape(N_DEV, owc, D)
            comm, names = chip_comm_of_pending()
            dqt, dk, dv, *bufs = _flash_bwd(q, kk, vv, o, do, lse, tq, comm=comm)
            stacked.update(dict(zip(names, bufs)))
            pending.append(("o", l, g_o_l))

            def q_pre(dt, cc, sa, sb):
                d = dt.T
                return jnp.concatenate([d[:, :NOPE], _rope_bwd(d[:, NOPE:], cc, sa, sb)], axis=1)

            dq_pre = _rowwise("mla_dq_rope", q_pre, [dqt, t_cc, t_sa, t_sb], grid=(HEADS, T // tq),
                              in_specs=[pl.BlockSpec((None, None, QPAD, tq), lambda b, i_: (b, i_, 0, 0))] + [pl.BlockSpec((tq, 128), lambda b, i_: (i_, 0))] * 3,
                              out_shapes=_sds((HEADS, T, QPAD), BF16), out_specs=pl.BlockSpec((None, tq, QPAD), lambda b, i_: (b, i_, 0)))

            def kv_pre(dkb, dvb, cc, sa, sb):
                dkv = jnp.concatenate([dkb[:, :, :NOPE], dvb], axis=2)
                dkr = _rope_bwd(jnp.sum(dkb[:, :, NOPE:], axis=0), cc, sa, sb)
                return dkv, dkr

            dkv, dkr = _rowwise("mla_dkv_rope", kv_pre, [dk, dv, t_cc, t_sa, t_sb], grid=(T // ts,),
                                in_specs=[pl.BlockSpec((HEADS, ts, QPAD), lambda i_: (0, i_, 0)), pl.BlockSpec((HEADS, ts, VDIM), lambda i_: (0, i_, 0))] + [_row_spec(ts, 128)] * 3,
                                out_shapes=[_sds((HEADS, T, NOPE + VDIM), BF16), _sds((T, 128), F32)],
                                out_specs=[pl.BlockSpec((HEADS, ts, NOPE + VDIM), lambda i_: (0, i_, 0)), _row_spec(ts, 128)])
            g_uq_l = _matmul("mla_q_wgrad", cq, dq_pre, [], grid=(HEADS, nt),
                             a_spec=pl.BlockSpec((tm, Q_RANK), lambda b, t_: (t_, 0)), b_spec=pl.BlockSpec((None, tm, QPAD), lambda b, t_: (b, t_, 0)),
                             extra_specs=[], out_shapes=[_sds((HEADS, Q_RANK, QPAD), BF16)],
                             out_specs=[pl.BlockSpec((None, Q_RANK, QPAD), lambda b, t_: (b, 0, 0))],
                             dims=TN, k_axis=1, nk=nt, acc_shape=(Q_RANK, QPAD))[0]
            g_ukv_l = _matmul("mla_kv_wgrad", ckv, dkv, [], grid=(HEADS, nt),
                              a_spec=pl.BlockSpec((tm, KV_RANK), lambda b, t_: (t_, 0)), b_spec=pl.BlockSpec((None, tm, NOPE + VDIM), lambda b, t_: (b, t_, 0)),
                              extra_specs=[], out_shapes=[_sds((HEADS, KV_RANK, NOPE + VDIM), BF16)],
                              out_specs=[pl.BlockSpec((None, KV_RANK, NOPE + VDIM), lambda b, t_: (b, 0, 0))],
                              dims=TN, k_axis=1, nk=nt, acc_shape=(KV_RANK, NOPE + VDIM))[0]
            pending.append(("uq", l, g_uq_l[:, :, :NOPE + ROPE]))
            pending.append(("ukv", l, g_ukv_l))
            dcq = _matmul("mla_q_bwd", dq_pre, w_uq, [], grid=(nt, HEADS),
                          a_spec=pl.BlockSpec((None, tm, QPAD), lambda i_, b: (b, i_, 0)),
                          b_spec=pl.BlockSpec((None, None, Q_RANK, QPAD), lambda i_, b: (l, b, 0, 0)), extra_specs=[],
                          out_shapes=[_sds((T, Q_RANK), F32)], out_specs=[pl.BlockSpec((tm, Q_RANK), lambda i_, b: (i_, 0))],
                          dims=NT, k_axis=1, nk=HEADS, acc_shape=(tm, Q_RANK))[0]
            dckv = _matmul("mla_kv_bwd", dkv, w_ukv, [], grid=(nt, HEADS),
                           a_spec=pl.BlockSpec((None, tm, NOPE + VDIM), lambda i_, b: (b, i_, 0)),
                           b_spec=pl.BlockSpec((None, None, KV_RANK, NOPE + VDIM), lambda i_, b: (l, b, 0, 0)), extra_specs=[],
                           out_shapes=[_sds((T, KV_RANK), F32)], out_specs=[pl.BlockSpec((tm, KV_RANK), lambda i_, b: (i_, 0))],
                           dims=NT, k_axis=1, nk=HEADS, acc_shape=(tm, KV_RANK))[0]

            def latent_bwd(la, qn, kvn, dq_, dkv_, dkr_):
                dcq_raw, dqn = _rms_bwd(la[:, :Q_RANK], qn, dq_)
                dckv_raw, dkvn = _rms_bwd(la[:, Q_RANK:Q_RANK + KV_RANK], kvn, dkv_)
                return jnp.concatenate([dcq_raw, dckv_raw, dkr_], axis=1), dqn, dkvn

            dlat, d_qn[l], d_kvn[l] = _rowwise(
                "mla_latent_bwd", latent_bwd, [lat, mla_q_norm[l].reshape(1, Q_RANK), mla_kv_norm[l].reshape(1, KV_RANK), dcq, dckv, dkr],
                grid=(nt,), in_specs=[_row_spec(tm, LAT_PAD), _const_spec((1, Q_RANK)), _const_spec((1, KV_RANK)),
                                      _row_spec(tm, Q_RANK), _row_spec(tm, KV_RANK), _row_spec(tm, 128)],
                out_shapes=[_sds((T, LAT_PAD), BF16), _sds((1, Q_RANK), F32), _sds((1, KV_RANK), F32)],
                out_specs=[_row_spec(tm, LAT_PAD), _const_spec((1, Q_RANK)), _const_spec((1, KV_RANK))], n_acc=2)
            g_dkv_l = wgrad_rows("mla_down_wgrad", h, dlat, D, LAT_PAD)
            pending.append(("dkv", l, g_dkv_l[:, :LAT].reshape(N_DEV, dc, LAT)))
            dh = _matmul("mla_down_bwd", dlat, w_dkv, [], grid=(nt,), a_spec=_row_spec(tm, LAT_PAD),
                         b_spec=pl.BlockSpec((None, D, LAT_PAD), lambda i_: (l, 0, 0)), extra_specs=[],
                         out_shapes=[_sds((T, D), F32)], out_specs=[_row_spec(tm, D)], dims=NT)[0]
        else:
            h, z, ge, gate = mix_saved
            dgate = back_rows("sgu_out_bwd", dyb, g_out[l], ec, [], None)
            pending.append(("out", l, wgrad_rows("sgu_out_wgrad", gate, dyb, ec, D).reshape(N_DEV, ec, D)))
            dz, d_wsp[l], d_bsp[l], d_lng[l], d_lnb[l] = _sgu_mid_bwd(ge, z, dgate, ln_g_full[l], ln_b_full[l], sgu_w_spatial[l], b_sp[l], 2)
            pending.append(("in", l, wgrad_cols("sgu_in_wgrad", h, dz, e2c)))
            dh = back_cols("sgu_in_bwd", dz, g_in[l], e2c)
        dx, dyb, d_norm_mix[i] = rmsnorm_bwd(xs_i, norm_mix[i], dh, dx)
    grad_x = dx.reshape(1, T, D)

    comm, names = chip_comm_of_pending()
    stacked.update(dict(zip(names, _comm_call("grad_chip_exchange", comm))))

    def adam_big(name, parts, w, m, v):
        lyr, rws, cls = w.shape
        rt = _tile(rws, 256)

        def fn(p, w_, m_, v_):
            g = (p[0].astype(F32) + p[1].astype(F32)) + (p[2].astype(F32) + p[3].astype(F32))
            return (g, *_adam(w_, g, m_, v_))

        spec = pl.BlockSpec((None, rt, cls), lambda l_, i_: (l_, i_, 0))
        return _rowwise(name, fn, [parts, w, m, v], grid=(lyr, rws // rt),
                        in_specs=[pl.BlockSpec((N_CHIP, None, rt, cls), lambda l_, i_: (0, l_, i_, 0)), spec, spec, spec],
                        out_shapes=[_sds(w.shape, F32)] * 4, out_specs=[spec] * 4)

    big_names = ["dkv", "uq", "ukv", "o", "in", "out", "up", "down"]
    big_w = [mla_w_dkv, mla_w_uq, mla_w_ukv, mla_w_o, sgu_w_in, sgu_w_out, ffn_w_up, ffn_w_down]
    big_m = [m_mla_w_dkv, m_mla_w_uq, m_mla_w_ukv, m_mla_w_o, m_sgu_w_in, m_sgu_w_out, m_ffn_w_up, m_ffn_w_down]
    big_v = [v_mla_w_dkv, v_mla_w_uq, v_mla_w_ukv, v_mla_w_o, v_sgu_w_in, v_sgu_w_out, v_ffn_w_up, v_ffn_w_down]
    big_res = [adam_big("adam_large", stacked[nm], w, m, v) for nm, w, m, v in zip(big_names, big_w, big_m, big_v)]

    def rows128(a, rows):
        flat = a.reshape(-1, 128)
        return jnp.pad(flat, ((0, rows - flat.shape[0]), (0, 0)))

    def pad8(n):
        return -(-n // 8) * 8

    small_names = ["norm_mix", "norm_ffn", "final_norm", "q_norm", "kv_norm", "w_spatial", "b_spatial"]
    small_g = [jnp.concatenate(d_norm_mix, 0), jnp.concatenate(d_norm_ffn, 0), d_final, jnp.concatenate(d_qn, 0), jnp.concatenate(d_kvn, 0),
               jnp.stack(d_wsp, 0), jnp.stack(d_bsp, 0), jnp.concatenate(d_lng, 0), jnp.concatenate(d_lnb, 0)]
    small_w = [norm_mix, norm_ffn, final_norm, mla_q_norm, mla_kv_norm, sgu_w_spatial, sgu_b_spatial]
    small_m = [m_norm_mix, m_norm_ffn, m_final_norm, m_mla_q_norm, m_mla_kv_norm, m_sgu_w_spatial, m_sgu_b_spatial]
    small_v = [v_norm_mix, v_norm_ffn, v_final_norm, v_mla_q_norm, v_mla_kv_norm, v_sgu_w_spatial, v_sgu_b_spatial]
    sizes = [pad8(g.size // 128) for g in small_g]
    n_rep = len(small_w)
    sizes[n_rep - 1] += -sum(sizes[:n_rep]) % SMALL_ROWS
    offs = [sum(sizes[:k]) for k in range(len(sizes) + 1)]
    rep_rows = offs[n_rep]
    pack_g = jnp.concatenate([rows128(g, sz) for g, sz in zip(small_g, sizes)], axis=0)
    (gathered_small,) = _all_gather("gather_small_grads", [pack_g])

    def pack(arrs):
        return jnp.concatenate([rows128(a, sz) for a, sz in zip(arrs, sizes[:n_rep])], axis=0)

    def sum8(p):
        return ((p[0] + p[1]) + (p[2] + p[3])) + ((p[4] + p[5]) + (p[6] + p[7]))

    sspec = _row_spec(SMALL_ROWS, 128)
    rep_g, rep_d, rep_m, rep_v = _rowwise(
        "adam_small", lambda p, w_, m_, v_: (sum8(p), *_adam(w_, sum8(p), m_, v_)),
        [gathered_small, pack(small_w), pack(small_m), pack(small_v)], grid=(rep_rows // SMALL_ROWS,),
        in_specs=[pl.BlockSpec((N_DEV, SMALL_ROWS, 128), lambda i_: (0, i_, 0)), sspec, sspec, sspec],
        out_shapes=[_sds((rep_rows, 128), F32)] * 4, out_specs=[sspec] * 4)

    def unpack(packed, k, like):
        return packed[offs[k]:offs[k] + like.size // 128].reshape(like.shape)

    my_b = 4 * lax.axis_index("x") + 2 * lax.axis_index("y") + lax.axis_index("c")
    ln_w = jnp.concatenate([sgu_ln_g, sgu_ln_b], 0)
    ln_m = jnp.concatenate([m_sgu_ln_g, m_sgu_ln_b], 0)
    ln_v = jnp.concatenate([v_sgu_ln_g, v_sgu_ln_b], 0)
    ln_all = gathered_small[:, rep_rows:, :]
    ln_mine = lax.dynamic_slice_in_dim(ln_all[:, :2 * n_sgu * E // 128].reshape(N_DEV, 2 * n_sgu, N_DEV, ec), my_b, 1, axis=2).reshape(N_DEV, 2 * n_sgu, ec)
    ln_g_, ln_d, ln_m2, ln_v2 = _rowwise(
        "adam_ln", lambda p, w_, m_, v_: (sum8(p), *_adam(w_, sum8(p), m_, v_)), [ln_mine, ln_w, ln_m, ln_v], grid=(1,),
        in_specs=[_const_spec(ln_mine.shape), _const_spec(ln_w.shape), _const_spec(ln_w.shape), _const_spec(ln_w.shape)],
        out_shapes=[_sds(ln_w.shape, F32)] * 4, out_specs=[_const_spec(ln_w.shape)] * 4)

    def family(pos):
        rep = [rep_g, rep_d, rep_m, rep_v][pos]
        ln = [ln_g_, ln_d, ln_m2, ln_v2][pos]
        small = {nm: unpack(rep, k, w_) for k, (nm, w_) in enumerate(zip(small_names, small_w))}
        big = [res[pos] for res in big_res]
        return [small["norm_mix"], small["norm_ffn"], small["final_norm"], big[0], small["q_norm"], small["kv_norm"], big[1], big[2], big[3],
                big[4], ln[:n_sgu], ln[n_sgu:], small["w_spatial"], small["b_spatial"], big[5], big[6], big[7]]

    return (loss, grad_x, *family(0), *family(1), *family(2), *family(3))
```

```python
import math

import jax
import jax.numpy as jnp
from jax import lax
from jax.experimental import pallas as pl
from jax.experimental.pallas import tpu as pltpu

F32 = jnp.float32
BF16 = jnp.bfloat16
MESH = pl.DeviceIdType.MESH

N_DEV = 8
N_CHIP = 4
HEADS = 8
NOPE = 128
ROPE = 64
VDIM = 128
QPAD = 256
Q_RANK = 256
KV_RANK = 128
LAT = Q_RANK + KV_RANK + ROPE
LAT_PAD = 512
ROPE_THETA = 10000.0
SGU_CHUNK = 128
SGU_GROUPS = 8
NORM_EPS = 1e-6
LN_EPS = 1e-5
ADAM_LR = 0.001
ADAM_B1 = 0.9
ADAM_B2 = 0.999
ADAM_EPS = 1e-08
ADAM_WD = 0.01
ADAM_STEP = 10
ATTN_SCALE = (NOPE + ROPE) ** -0.5
NEG = -1e30
EXP2_SCALE = ATTN_SCALE * math.log2(math.e)
VMEM_LIMIT = 56 * 1024 * 1024
SMALL_ROWS = 256

NN = (((1,), (0,)), ((), ()))
NT = (((1,), (1,)), ((), ()))
TN = (((0,), (0,)), ((), ()))
ANY = pl.BlockSpec(memory_space=pl.ANY)


def _pcall(body, **kw):
    return pl.pallas_call(body, **kw)


def _params(n_grid, side_effects=False):
    return pltpu.CompilerParams(dimension_semantics=("arbitrary",) * n_grid, vmem_limit_bytes=VMEM_LIMIT, has_side_effects=side_effects)


def _sds(shape, dtype):
    return jax.ShapeDtypeStruct(tuple(shape), dtype)


def _tile(n, want):
    t = min(n, want)
    assert n % t == 0, (n, want)
    return t


class _Comm:
    def __init__(self, operands, out_shapes, aliases, scratch, start, finish):
        self.operands, self.out_shapes, self.aliases, self.scratch = operands, out_shapes, aliases, scratch
        self.start, self.finish = start, finish


def _place():
    return lax.axis_index("x"), lax.axis_index("y"), lax.axis_index("c")


def _other_chips(x, y):
    return [(1 - x, y), (x, 1 - y), (1 - x, 1 - y)]


def _dev_index(dev):
    return 4 * dev[0] + 2 * dev[1] + dev[2]


def _comm_call(name, comm):
    c_in, c_out = len(comm.operands), len(comm.out_shapes)

    def body(*refs):
        ins, outs, sems = refs[:c_in], refs[c_in:c_in + c_out], refs[c_in + c_out:]
        comm.start(ins, outs, sems)
        comm.finish(ins, outs, sems)

    return _pcall(body, name=name, in_specs=[ANY] * c_in, out_specs=[ANY] * c_out, out_shape=comm.out_shapes,
                  scratch_shapes=comm.scratch, input_output_aliases=dict(comm.aliases),
                  compiler_params=pltpu.CompilerParams(has_side_effects=True))(*comm.operands)


def _call(name, body, operands, in_specs, out_shapes, out_specs, scratch, grid, comm=None):
    if comm is None:
        return _pcall(body, name=name, grid=grid, in_specs=in_specs, out_specs=out_specs, out_shape=out_shapes,
                      scratch_shapes=scratch, compiler_params=_params(len(grid)))(*operands)
    n_in, n_out, n_sc = len(operands), len(out_shapes), len(scratch)
    c_in, c_out = len(comm.operands), len(comm.out_shapes)

    def hosted(*refs):
        ins, cins = refs[:n_in], refs[n_in:n_in + c_in]
        o0 = n_in + c_in
        outs, couts = refs[o0:o0 + n_out], refs[o0 + n_out:o0 + n_out + c_out]
        rest = refs[o0 + n_out + c_out:]
        sc, csems = rest[:n_sc], rest[n_sc:]
        first = pl.program_id(0) == 0
        last = pl.program_id(0) == grid[0] - 1
        for d in range(1, len(grid)):
            first = jnp.logical_and(first, pl.program_id(d) == 0)
            last = jnp.logical_and(last, pl.program_id(d) == grid[d] - 1)

        @pl.when(first)
        def _():
            comm.start(cins, couts, csems)

        body(*ins, *outs, *sc)

        @pl.when(last)
        def _():
            comm.finish(cins, couts, csems)

    return _pcall(hosted, name=name, grid=grid, in_specs=[*in_specs, *[ANY] * c_in], out_specs=[*out_specs, *[ANY] * c_out],
                  out_shape=[*out_shapes, *comm.out_shapes], scratch_shapes=[*scratch, *comm.scratch],
                  input_output_aliases={n_in + k: n_out + v for k, v in comm.aliases.items()},
                  compiler_params=_params(len(grid), side_effects=True))(*operands, *comm.operands)


def _gather_level1(shards):
    n = len(shards)

    def copies(ins, outs, sems):
        send_sems, recv_sems, local_sems = sems
        x, y, c = _place()
        me, sibling = (x, y, c), (x, y, 1 - c)
        chips = _other_chips(x, y)

        def copy(a, k, block, to, src=None):
            slot = outs[a].at[_dev_index(block)]
            return pltpu.make_async_remote_copy(src_ref=slot if src is None else src, dst_ref=slot, send_sem=send_sems.at[a, k],
                                                recv_sem=recv_sems.at[a, k], device_id=to, device_id_type=MESH)

        mine = [pltpu.make_async_copy(ins[a], outs[a].at[_dev_index(me)], local_sems.at[a]) for a in range(n)]
        sends = [copy(a, 1 + j, me, (*chip, c), src=ins[a]) for j, chip in enumerate(chips) for a in range(n)]
        sends += [copy(a, 0, me, sibling, src=ins[a]) for a in range(n)]
        recvs = [copy(a, 1 + j, (*chip, c), me) for j, chip in enumerate(chips) for a in range(n)]
        recvs += [copy(a, 0, sibling, me) for a in range(n)]
        return mine, sends, recvs

    def start(ins, outs, sems):
        mine, sends, _ = copies(ins, outs, sems)
        for cp in mine + sends:
            cp.start()

    def finish(ins, outs, sems):
        mine, sends, recvs = copies(ins, outs, sems)
        for cp in recvs:
            cp.wait_recv()
        for cp in sends:
            cp.wait_send()
        for cp in mine:
            cp.wait()

    return _Comm(shards, [_sds((N_DEV, *a.shape), a.dtype) for a in shards], {},
                 [pltpu.SemaphoreType.DMA((n, 4)), pltpu.SemaphoreType.DMA((n, 4)), pltpu.SemaphoreType.DMA((n,))], start, finish)


def _gather_level2(bufs):
    n = len(bufs)

    def copies(outs, sems):
        send_sems, recv_sems = sems
        x, y, c = _place()
        sibling = (x, y, 1 - c)
        sends, recvs = [], []
        for j, chip in enumerate(_other_chips(x, y)):
            for a in range(n):
                have, want = outs[a].at[_dev_index((*chip, c))], outs[a].at[_dev_index((*chip, 1 - c))]
                sends.append(pltpu.make_async_remote_copy(src_ref=have, dst_ref=have, send_sem=send_sems.at[a, j], recv_sem=recv_sems.at[a, j],
                                                          device_id=sibling, device_id_type=MESH))
                recvs.append(pltpu.make_async_remote_copy(src_ref=want, dst_ref=want, send_sem=send_sems.at[a, j], recv_sem=recv_sems.at[a, j],
                                                          device_id=sibling, device_id_type=MESH))
        return sends, recvs

    def start(ins, outs, sems):
        for cp in copies(outs, sems)[0]:
            cp.start()

    def finish(ins, outs, sems):
        sends, recvs = copies(outs, sems)
        for cp in recvs:
            cp.wait_recv()
        for cp in sends:
            cp.wait_send()

    return _Comm(bufs, [_sds(b.shape, b.dtype) for b in bufs], {a: a for a in range(n)},
                 [pltpu.SemaphoreType.DMA((n, 3)), pltpu.SemaphoreType.DMA((n, 3))], start, finish)


def _all_gather(name, arrays):
    n = len(arrays)

    def body(*refs):
        ins = refs[:n]
        outs = refs[n:2 * n]
        send_sems, recv_sems, local_sems = refs[2 * n:]
        x, y, c = _place()
        me, sibling = (x, y, c), (x, y, 1 - c)
        chips = _other_chips(x, y)

        def copy(a, k, block, to, src=None):
            slot = outs[a].at[_dev_index(block)]
            return pltpu.make_async_remote_copy(src_ref=slot if src is None else src, dst_ref=slot, send_sem=send_sems.at[a, k],
                                                recv_sem=recv_sems.at[a, k], device_id=to, device_id_type=MESH)

        mine = [pltpu.make_async_copy(ins[a], outs[a].at[_dev_index(me)], local_sems.at[a]) for a in range(n)]
        for cp in mine:
            cp.start()
        first = []
        for j, chip in enumerate(chips):
            first += [copy(a, 1 + j, me, (*chip, c), src=ins[a]) for a in range(n)]
        first += [copy(a, 0, me, sibling, src=ins[a]) for a in range(n)]
        for cp in first:
            cp.start()
        passed = []
        for j, chip in enumerate(chips):
            for a in range(n):
                copy(a, 1 + j, (*chip, c), me).wait_recv()
                fwd = copy(a, 4 + j, (*chip, c), sibling)
                fwd.start()
                passed.append(fwd)
        for a in range(n):
            copy(a, 0, sibling, me).wait_recv()
            for j, chip in enumerate(chips):
                copy(a, 4 + j, (*chip, 1 - c), me).wait_recv()
        for cp in first + passed:
            cp.wait_send()
        for cp in mine:
            cp.wait()

    return _pcall(
        body, name=name, in_specs=[ANY] * n, out_specs=[ANY] * n,
        out_shape=[_sds((N_DEV, *a.shape), a.dtype) for a in arrays],
        scratch_shapes=[pltpu.SemaphoreType.DMA((n, 7)), pltpu.SemaphoreType.DMA((n, 7)), pltpu.SemaphoreType.DMA((n,))],
        compiler_params=pltpu.CompilerParams(has_side_effects=True),
    )(*arrays)


def _sibling_exchange(grads):
    n = len(grads)

    def start(ins, outs, sems):
        send_sems, recv_sems = sems
        x, y, c = _place()
        for a in range(n):
            for ch in range(N_CHIP):
                pltpu.make_async_remote_copy(src_ref=ins[a].at[2 * ch + 1 - c], dst_ref=outs[a].at[ch], send_sem=send_sems.at[a],
                                             recv_sem=recv_sems.at[a], device_id=(x, y, 1 - c), device_id_type=MESH).start()

    def finish(ins, outs, sems):
        send_sems, recv_sems = sems
        x, y, c = _place()
        for a in range(n):
            pltpu.make_async_remote_copy(src_ref=outs[a], dst_ref=outs[a], send_sem=send_sems.at[a], recv_sem=recv_sems.at[a],
                                         device_id=(x, y, 1 - c), device_id_type=MESH).wait()

    return _Comm(grads, [_sds((N_CHIP, *g.shape[1:]), g.dtype) for g in grads], {},
                 [pltpu.SemaphoreType.DMA((n,)), pltpu.SemaphoreType.DMA((n,))], start, finish)


def _chip_exchange(parts, slots, layers, stacked):
    n = len(parts)
    names = []
    for nm, _ in slots:
        if nm not in names:
            names.append(nm)
    shapes = {nm: _sds((N_CHIP, layers[nm], *parts[a].shape[1:]), parts[a].dtype) for a, (nm, _) in enumerate(slots)}
    kept = [nm for nm in names if stacked.get(nm) is not None]
    aliases = {n + k: names.index(nm) for k, nm in enumerate(kept)}

    def copies(ins, outs, sems):
        send_sems, recv_sems, local_sems = sems
        x, y, c = _place()
        mine = 2 * x + y
        local, sends, recvs = [], [], []
        for a, (nm, l) in enumerate(slots):
            buf = outs[names.index(nm)]
            local.append(pltpu.make_async_copy(ins[a].at[mine], buf.at[mine, l], local_sems.at[a]))
            for j, chip in enumerate(_other_chips(x, y)):
                theirs = buf.at[2 * chip[0] + chip[1], l]
                sends.append(pltpu.make_async_remote_copy(src_ref=ins[a].at[2 * chip[0] + chip[1]], dst_ref=buf.at[mine, l], send_sem=send_sems.at[a, j],
                                                          recv_sem=recv_sems.at[a, j], device_id=(*chip, c), device_id_type=MESH))
                recvs.append(pltpu.make_async_remote_copy(src_ref=theirs, dst_ref=theirs, send_sem=send_sems.at[a, j],
                                                          recv_sem=recv_sems.at[a, j], device_id=(*chip, c), device_id_type=MESH))
        return local, sends, recvs

    def start(ins, outs, sems):
        local, sends, _ = copies(ins, outs, sems)
        for cp in local + sends:
            cp.start()

    def finish(ins, outs, sems):
        local, sends, recvs = copies(ins, outs, sems)
        for cp in recvs:
            cp.wait_recv()
        for cp in sends:
            cp.wait_send()
        for cp in local:
            cp.wait()

    comm = _Comm([*parts, *[stacked[nm] for nm in kept]], [shapes[nm] for nm in names], aliases,
                 [pltpu.SemaphoreType.DMA((n, 3)), pltpu.SemaphoreType.DMA((n, 3)), pltpu.SemaphoreType.DMA((n,))], start, finish)
    return comm, names


def _matmul(name, a, b, extras, *, grid, a_spec, b_spec, extra_specs, out_shapes, out_specs, dims, k_axis=None, nk=1,
            acc_shape=None, epilogue=None, comm=None, n_sum=0):
    n_extra = len(extras)
    n_out = len(out_shapes)

    def body(*refs):
        a_ref, b_ref = refs[0], refs[1]
        ex = refs[2:2 + n_extra]
        outs = refs[2 + n_extra:2 + n_extra + n_out]
        prod = lax.dot_general(a_ref[...], b_ref[...], dims, preferred_element_type=F32)

        def finish(acc):
            res = epilogue(acc, *[e[...] for e in ex]) if epilogue is not None else (acc,)
            first = None
            for d in range(len(grid)):
                if d != k_axis:
                    here = pl.program_id(d) == 0
                    first = here if first is None else jnp.logical_and(first, here)
            for idx, (o, r) in enumerate(zip(outs, res)):
                if idx < n_out - n_sum:
                    o[...] = r.astype(o.dtype)
                else:
                    @pl.when(first)
                    def _(o=o, r=r):
                        o[...] = r.astype(o.dtype)

                    @pl.when(jnp.logical_not(first))
                    def _(o=o, r=r):
                        o[...] += r.astype(o.dtype)

        if k_axis is None:
            finish(prod)
        else:
            acc_ref = refs[-1]
            k = pl.program_id(k_axis)

            @pl.when(k == 0)
            def _():
                acc_ref[...] = prod

            @pl.when(k > 0)
            def _():
                acc_ref[...] += prod

            @pl.when(k == nk - 1)
            def _():
                finish(acc_ref[...])

    scratch = [] if k_axis is None else [pltpu.VMEM(acc_shape, F32)]
    return _call(name, body, [a, b, *extras], [a_spec, b_spec, *extra_specs], list(out_shapes), list(out_specs), scratch, grid, comm)


def _rowwise(name, fn, operands, *, grid, in_specs, out_shapes, out_specs, n_acc=0, grid_spec_prefetch=None):
    n_in = len(operands)
    n_out = len(out_shapes)
    n_pre = 0 if grid_spec_prefetch is None else 1

    def body(*refs):
        refs = refs[n_pre:]
        ins = refs[:n_in]
        outs = refs[n_in:n_in + n_out]
        res = fn(*[r[...] for r in ins])
        if not isinstance(res, (tuple, list)):
            res = (res,)
        first = pl.program_id(0) == 0
        for d in range(1, len(grid)):
            first = jnp.logical_and(first, pl.program_id(d) == 0)
        for idx, (o, r) in enumerate(zip(outs, res)):
            if idx < n_out - n_acc:
                o[...] = r.astype(o.dtype)
            else:
                @pl.when(first)
                def _(o=o, r=r):
                    o[...] = r.astype(o.dtype)

                @pl.when(jnp.logical_not(first))
                def _(o=o, r=r):
                    o[...] += r.astype(o.dtype)

    if grid_spec_prefetch is None:
        return _pcall(body, name=name, grid=grid, in_specs=in_specs, out_specs=out_specs, out_shape=out_shapes,
                      compiler_params=_params(len(grid)))(*operands)
    gs = pltpu.PrefetchScalarGridSpec(num_scalar_prefetch=1, grid=grid, in_specs=in_specs, out_specs=out_specs)
    return _pcall(body, name=name, grid_spec=gs, out_shape=out_shapes,
                  compiler_params=_params(len(grid)))(grid_spec_prefetch, *operands)


def _row_spec(tm, w):
    return pl.BlockSpec((tm, w), lambda i: (i, 0))


def _const_spec(shape):
    nd = len(shape)
    return pl.BlockSpec(tuple(shape), lambda *_: (0,) * nd)


def _rms_fwd(x, g):
    r = lax.rsqrt(jnp.mean(x * x, axis=-1, keepdims=True) + NORM_EPS)
    return x * r * g


def _rms_bwd(x, g, dy):
    r = lax.rsqrt(jnp.mean(x * x, axis=-1, keepdims=True) + NORM_EPS)
    xh = x * r
    u = dy * g
    dx = r * (u - xh * jnp.mean(u * xh, axis=-1, keepdims=True))
    dg = jnp.sum(dy * xh, axis=0, keepdims=True)
    return dx, dg


def _gelu_and_grad(z):
    cdf = 0.5 * (1.0 + lax.erf(z * (2.0 ** -0.5)))
    return cdf + z * jnp.exp(-0.5 * z * z) * ((2.0 * math.pi) ** -0.5), z * cdf


def _rope_fwd(x, cc, sa, sb):
    return x * cc + pltpu.roll(x, 96, 1) * sa + pltpu.roll(x, 32, 1) * sb


def _rope_bwd(d, cc, sa, sb):
    return d * cc + pltpu.roll(d * sa, 32, 1) + pltpu.roll(d * sb, 96, 1)


def _adam(w, g, m, v):
    m = ADAM_B1 * m + (1.0 - ADAM_B1) * g
    v = ADAM_B2 * v + (1.0 - ADAM_B2) * (g * g)
    m_hat = m / (1.0 - ADAM_B1 ** ADAM_STEP)
    v_hat = v / (1.0 - ADAM_B2 ** ADAM_STEP)
    delta = -ADAM_LR * (m_hat / (jnp.sqrt(v_hat) + ADAM_EPS) + ADAM_WD * w)
    return delta, m, v


def _flash_fwd(q, k, vt, tq, comm=None):
    h, t, _ = q.shape
    nq = t // tq

    def body(q_ref, k_ref, vt_ref, o_ref, lse_ref):
        qi = pl.program_id(1)

        def step(kj, state, masked):
            m_old, l_old, acc_old = state
            kb = k_ref[pl.ds(pl.multiple_of(kj * tq, tq), tq), :]
            st = lax.dot_general(kb, q_ref[...], NT, preferred_element_type=F32)
            if masked:
                key = lax.broadcasted_iota(jnp.int32, (tq, tq), 0)
                qry = lax.broadcasted_iota(jnp.int32, (tq, tq), 1)
                st = jnp.where(key <= qry, st, NEG)
            m_new = jnp.maximum(m_old, jnp.max(st, axis=0, keepdims=True))
            alpha = jnp.exp2((m_old - m_new) * EXP2_SCALE)
            pt = jnp.exp2((st - m_new) * EXP2_SCALE)
            l_new = alpha * l_old + jnp.sum(pt, axis=0, keepdims=True)
            acc_new = alpha * acc_old + lax.dot_general(vt_ref[kj], pt.astype(BF16), NN, preferred_element_type=F32)
            return m_new, l_new, acc_new

        init = (jnp.full((1, tq), NEG, F32), jnp.zeros((1, tq), F32), jnp.zeros((VDIM, tq), F32))
        state = lax.fori_loop(0, qi, lambda kj, st_: step(kj, st_, False), init)
        m, l, acc = step(qi, state, True)
        o_ref[...] = (acc / l).T.astype(o_ref.dtype)
        lse_ref[...] = m * EXP2_SCALE + jnp.log2(l)

    return _call(
        "flash_fwd", body, [q, k, vt],
        [pl.BlockSpec((None, tq, QPAD), lambda hh, i: (hh, i, 0)),
         pl.BlockSpec((None, t, QPAD), lambda hh, i: (hh, 0, 0)),
         pl.BlockSpec((None, nq, VDIM, tq), lambda hh, i: (hh, 0, 0, 0))],
        [_sds((t, h * VDIM), BF16), _sds((h, nq, 1, tq), F32)],
        [pl.BlockSpec((tq, VDIM), lambda hh, i: (i, hh)),
         pl.BlockSpec((None, None, 1, tq), lambda hh, i: (hh, i, 0, 0))],
        [], (h, nq), comm)


def _flash_bwd(q, k, v, o, do, lse, tq, comm=None):
    h, t, _ = q.shape
    nq = t // tq

    def body(q_ref, k_ref, v_ref, o_ref, do_ref, lse_ref, dqt_ref, dk_ref, dv_ref, delta_ref):
        kj = pl.program_id(1)

        @pl.when(kj == 0)
        def _():
            dqt_ref[...] = jnp.zeros_like(dqt_ref)
            ones = jnp.ones((8, VDIM), BF16)
            for qi in range(nq):
                rows = pl.ds(qi * tq, tq)
                prod = do_ref[rows, :].astype(F32) * o_ref[rows, :].astype(F32)
                hi = prod.astype(BF16)
                lo = (prod - hi.astype(F32)).astype(BF16)
                delta_ref[qi] = (lax.dot_general(ones, hi, NT, preferred_element_type=F32)
                                 + lax.dot_general(ones, lo, NT, preferred_element_type=F32))

        kb = k_ref[...]
        vb = v_ref[...]
        kbt = kb.astype(F32).T.astype(BF16)
        dk_ref[...] = jnp.zeros_like(dk_ref)
        dv_ref[...] = jnp.zeros_like(dv_ref)

        def step(qi, masked):
            rows = pl.ds(pl.multiple_of(qi * tq, tq), tq)
            qb = q_ref[rows, :]
            dob = do_ref[rows, :]
            st = lax.dot_general(kb, qb, NT, preferred_element_type=F32)
            pt = jnp.exp2(st * EXP2_SCALE - lse_ref[qi])
            if masked:
                key = lax.broadcasted_iota(jnp.int32, (tq, tq), 0)
                qry = lax.broadcasted_iota(jnp.int32, (tq, tq), 1)
                pt = jnp.where(key <= qry, pt, 0.0)
            dv_ref[...] += lax.dot_general(pt.astype(BF16), dob, NN, preferred_element_type=F32)
            dpt = lax.dot_general(vb, dob, NT, preferred_element_type=F32)
            dst = (pt * (dpt - delta_ref[qi, pl.ds(0, 1), :]) * ATTN_SCALE).astype(BF16)
            dk_ref[...] += lax.dot_general(dst, qb, NN, preferred_element_type=F32)
            dqt_ref[qi] += lax.dot_general(kbt, dst, NN, preferred_element_type=F32)

        step(kj, True)

        def loop_body(qi, carry):
            step(qi, False)
            return carry

        lax.fori_loop(kj + 1, nq, loop_body, 0)

    return _call(
        "flash_bwd", body, [q, k, v, o, do, lse],
        [pl.BlockSpec((None, t, QPAD), lambda hh, j: (hh, 0, 0)),
         pl.BlockSpec((None, tq, QPAD), lambda hh, j: (hh, j, 0)),
         pl.BlockSpec((None, tq, VDIM), lambda hh, j: (hh, j, 0)),
         pl.BlockSpec((t, VDIM), lambda hh, j: (0, hh)),
         pl.BlockSpec((t, VDIM), lambda hh, j: (0, hh)),
         pl.BlockSpec((None, nq, 1, tq), lambda hh, j: (hh, 0, 0, 0))],
        [_sds((h, nq, QPAD, tq), F32), _sds((h, t, QPAD), F32), _sds((h, t, VDIM), F32)],
        [pl.BlockSpec((None, nq, QPAD, tq), lambda hh, j: (hh, 0, 0, 0)),
         pl.BlockSpec((None, tq, QPAD), lambda hh, j: (hh, j, 0)),
         pl.BlockSpec((None, tq, VDIM), lambda hh, j: (hh, j, 0))],
        [pltpu.VMEM((nq, 8, tq), F32)], (h, nq), comm)


def _tril_bf16(w):
    row = lax.broadcasted_iota(jnp.int32, w.shape, 0)
    col = lax.broadcasted_iota(jnp.int32, w.shape, 1)
    return jnp.where(col <= row, w, 0.0).astype(BF16)


def _layer_norm_parts(v0):
    mu = jnp.mean(v0, axis=-1, keepdims=True)
    vc = v0 - mu
    rstd = lax.rsqrt(jnp.mean(vc * vc, axis=-1, keepdims=True) + LN_EPS)
    return vc * rstd, rstd


def _sgu_mid_fwd(ge, ln_g, ln_b, w_sp, b_sp, chunks_per_step):
    t, e2 = ge.shape
    e = e2 // 2
    gd = e // SGU_GROUPS
    rows = SGU_CHUNK * chunks_per_step

    def body(u_ref, v_ref, g_ref, b_ref, w_ref, bs_ref, gate_ref):
        for ck in range(chunks_per_step):
            r = pl.ds(ck * SGU_CHUNK, SGU_CHUNK)
            xh, _ = _layer_norm_parts(v_ref[r, :].astype(F32))
            v1 = (xh * g_ref[...] + b_ref[...]).astype(BF16)
            for g in range(SGU_GROUPS):
                cols = pl.ds(g * gd, gd)
                mixed = lax.dot_general(_tril_bf16(w_ref[g]), v1[:, g * gd:(g + 1) * gd], NN, preferred_element_type=F32) + bs_ref[g]
                gate_ref[r, cols] = (u_ref[r, cols].astype(F32) * mixed).astype(BF16)

    return _pcall(
        body, name="sgu_mid_fwd", grid=(t // rows,),
        in_specs=[pl.BlockSpec((rows, e), lambda i: (i, 0)), pl.BlockSpec((rows, e), lambda i: (i, 1)),
                  _const_spec((1, e)), _const_spec((1, e)), _const_spec(w_sp.shape), _const_spec(b_sp.shape)],
        out_specs=pl.BlockSpec((rows, e), lambda i: (i, 0)),
        out_shape=_sds((t, e), BF16), compiler_params=_params(1),
    )(ge, ge, ln_g, ln_b, w_sp, b_sp)


def _sgu_mid_bwd(ge, gp, dgate, ln_g, ln_b, w_sp, b_sp, chunks_per_step):
    t, e2 = ge.shape
    e = e2 // 2
    gd = e // SGU_GROUPS
    rows = SGU_CHUNK * chunks_per_step

    def body(u_ref, v_ref, zu_ref, zv_ref, dg_ref, g_ref, b_ref, w_ref, bs_ref, dz_ref, dw_ref, dbs_ref, dlg_ref, dlb_ref):
        @pl.when(pl.program_id(0) == 0)
        def _():
            dw_ref[...] = jnp.zeros_like(dw_ref)
            dbs_ref[...] = jnp.zeros_like(dbs_ref)
            dlg_ref[...] = jnp.zeros_like(dlg_ref)
            dlb_ref[...] = jnp.zeros_like(dlb_ref)

        for ck in range(chunks_per_step):
            r = pl.ds(ck * SGU_CHUNK, SGU_CHUNK)
            xh, rstd = _layer_norm_parts(v_ref[r, :].astype(F32))
            v1 = (xh * g_ref[...] + b_ref[...]).astype(BF16)
            dv1_parts = []
            for g in range(SGU_GROUPS):
                cols = pl.ds(g * gd, gd)
                wc = _tril_bf16(w_ref[g])
                v1g = v1[:, g * gd:(g + 1) * gd]
                mixed = lax.dot_general(wc, v1g, NN, preferred_element_type=F32) + bs_ref[g]
                dgate = dg_ref[r, cols].astype(F32)
                dmixed = dgate * u_ref[r, cols].astype(F32)
                du = dgate * mixed
                dz_ref[r, cols] = (du * zu_ref[r, cols].astype(F32)).astype(BF16)
                dbs_ref[g] += jnp.sum(dmixed, axis=1, keepdims=True)
                dmb = dmixed.astype(BF16)
                dwg = lax.dot_general(dmb, v1g, NT, preferred_element_type=F32)
                row = lax.broadcasted_iota(jnp.int32, dwg.shape, 0)
                col = lax.broadcasted_iota(jnp.int32, dwg.shape, 1)
                dw_ref[g] += jnp.where(col <= row, dwg, 0.0)
                dv1_parts.append(lax.dot_general(wc, dmb, TN, preferred_element_type=F32))
            dv1 = jnp.concatenate(dv1_parts, axis=1)
            dlg_ref[...] += jnp.sum(dv1 * xh, axis=0, keepdims=True)
            dlb_ref[...] += jnp.sum(dv1, axis=0, keepdims=True)
            dxh = dv1 * g_ref[...]
            dv0 = rstd * (dxh - jnp.mean(dxh, axis=-1, keepdims=True) - xh * jnp.mean(dxh * xh, axis=-1, keepdims=True))
            dz_ref[r, pl.ds(e, e)] = (dv0 * zv_ref[r, :].astype(F32)).astype(BF16)

    half0 = pl.BlockSpec((rows, e), lambda i: (i, 0))
    half1 = pl.BlockSpec((rows, e), lambda i: (i, 1))
    return _pcall(
        body, name="sgu_mid_bwd", grid=(t // rows,),
        in_specs=[half0, half1, half0, half1, half0, _const_spec((1, e)), _const_spec((1, e)), _const_spec(w_sp.shape), _const_spec(b_sp.shape)],
        out_specs=[pl.BlockSpec((rows, e2), lambda i: (i, 0)), _const_spec(w_sp.shape), _const_spec(b_sp.shape), _const_spec((1, e)), _const_spec((1, e))],
        out_shape=[_sds((t, e2), BF16), _sds(w_sp.shape, F32), _sds(b_sp.shape, F32), _sds((1, e), F32), _sds((1, e), F32)],
        compiler_params=_params(1),
    )(ge, ge, gp, gp, dgate, ln_g, ln_b, w_sp, b_sp)


def kernel(x, positions, norm_mix, norm_ffn, final_norm, mla_w_dkv, mla_q_norm, mla_kv_norm, mla_w_uq, mla_w_ukv, mla_w_o, sgu_w_in, sgu_ln_g, sgu_ln_b, sgu_w_spatial, sgu_b_spatial, sgu_w_out, ffn_w_up, ffn_w_down, loss_target, m_norm_mix, m_norm_ffn, m_final_norm, m_mla_w_dkv, m_mla_q_norm, m_mla_kv_norm, m_mla_w_uq, m_mla_w_ukv, m_mla_w_o, m_sgu_w_in, m_sgu_ln_g, m_sgu_ln_b, m_sgu_w_spatial, m_sgu_b_spatial, m_sgu_w_out, m_ffn_w_up, m_ffn_w_down, v_norm_mix, v_norm_ffn, v_final_norm, v_mla_w_dkv, v_mla_q_norm, v_mla_kv_norm, v_mla_w_uq, v_mla_w_ukv, v_mla_w_o, v_sgu_w_in, v_sgu_ln_g, v_sgu_ln_b, v_sgu_w_spatial, v_sgu_b_spatial, v_sgu_w_out, v_ffn_w_up, v_ffn_w_down):
    _, T, D = x.shape
    depth = norm_mix.shape[0]
    n_mla, n_sgu = mla_w_dkv.shape[0], sgu_w_in.shape[0]
    assert depth % 2 == 0
    FF = ffn_w_up.shape[2] * N_DEV
    E = sgu_w_out.shape[1] * N_DEV
    ffc, ec, e2c = FF // N_DEV, E // N_DEV, 2 * E // N_DEV
    dc = D // N_DEV
    OW = HEADS * VDIM
    owc = OW // N_DEV
    tm = _tile(T, 1024)
    tq = _tile(T, 512)
    ts = _tile(T, 256)
    nt = T // tm
    x2 = x.reshape(T, D)
    tgt = loss_target.reshape(T, D)
    cidx = lax.axis_index("c").astype(jnp.int32).reshape(1)

    ln_local = jnp.concatenate([sgu_ln_g, sgu_ln_b, jnp.zeros((8 - 2 * n_sgu, ec), F32)], axis=0)
    g_dkv, g_uq, g_ukv, g_o, g_ln = _all_gather(
        "gather_small_weights", [w.astype(BF16) for w in (mla_w_dkv, mla_w_uq, mla_w_ukv, mla_w_o)] + [ln_local])
    w_dkv = jnp.pad(g_dkv.transpose(1, 0, 2, 3).reshape(n_mla, D, LAT), ((0, 0), (0, 0), (0, LAT_PAD - LAT)))
    w_uq = jnp.pad(g_uq.transpose(1, 0, 2, 3), ((0, 0), (0, 0), (0, 0), (0, QPAD - NOPE - ROPE)))
    w_ukv = g_ukv.transpose(1, 0, 2, 3)
    w_o = g_o.transpose(1, 0, 2, 3).reshape(n_mla, OW, D)
    ln_g_full = [g_ln[:, l, :].reshape(1, E) for l in range(n_sgu)]
    ln_b_full = [g_ln[:, n_sgu + l, :].reshape(1, E) for l in range(n_sgu)]
    b_sp = sgu_b_spatial.reshape(n_sgu, SGU_GROUPS, SGU_CHUNK, 1)
    up_sh = [ffn_w_up[i].astype(BF16) for i in range(depth)]
    down_sh = [ffn_w_down[i].astype(BF16) for i in range(depth)]
    in_sh = [sgu_w_in[l].astype(BF16) for l in range(n_sgu)]
    out_sh = [sgu_w_out[l].astype(BF16) for l in range(n_sgu)]
    g_up, g_down, g_in, g_out = [None] * depth, [None] * depth, [None] * n_sgu, [None] * n_sgu

    inv_freq = ROPE_THETA ** (-jnp.arange(0, ROPE, 2, dtype=F32) / ROPE)
    zeros32 = jnp.zeros((ROPE // 2,), F32)
    inv128 = jnp.concatenate([inv_freq, inv_freq, zeros32, zeros32]).reshape(1, 128)
    sel_a = jnp.concatenate([-jnp.ones((32,), F32), zeros32, zeros32, zeros32]).reshape(1, 128)
    sel_b = jnp.concatenate([zeros32, jnp.ones((32,), F32), zeros32, zeros32]).reshape(1, 128)
    sel_c = jnp.concatenate([jnp.ones((64,), F32), zeros32, zeros32]).reshape(1, 128)

    def rope_tables(pos, inv, sa, sb, sc):
        ang = pos.astype(F32) * inv
        cs, sn = jnp.cos(ang), jnp.sin(ang)
        return cs * sc, sn * sa, sn * sb

    t_cc, t_sa, t_sb = _rowwise(
        "rope_tables", rope_tables, [positions.reshape(T, 1), inv128, sel_a, sel_b, sel_c], grid=(nt,),
        in_specs=[_row_spec(tm, 1)] + [_const_spec((1, 128))] * 4,
        out_shapes=[_sds((T, 128), F32)] * 3, out_specs=[_row_spec(tm, 128)] * 3)
    tab_specs = [_row_spec(tm, 128)] * 3

    def rmsnorm(xv, g):
        return _rowwise("rmsnorm", lambda a, gg: _rms_fwd(a, gg), [xv, g.reshape(1, D)], grid=(nt,),
                        in_specs=[_row_spec(tm, D), _const_spec((1, D))], out_shapes=_sds((T, D), BF16), out_specs=_row_spec(tm, D))

    def proj_cols(name, h, gw, nc, epilogue, n_out):
        return _matmul(name, h, gw, [], grid=(N_DEV, nt),
                       a_spec=pl.BlockSpec((tm, D), lambda j, i: (i, 0)),
                       b_spec=pl.BlockSpec((None, D, nc), lambda j, i: (j, 0, 0)), extra_specs=[],
                       out_shapes=[_sds((T, nc * N_DEV), BF16)] * n_out, out_specs=[pl.BlockSpec((tm, nc), lambda j, i: (i, j))] * n_out,
                       dims=NN, epilogue=epilogue)

    def residual_norm(acc, xr, g):
        xn = acc + xr
        return xn, _rms_fwd(xn, g)

    def proj_rows_residual(name, a, gw, kc, xres, g_next):
        return _matmul(name, a, gw, [xres, g_next.reshape(1, D)], grid=(nt, N_DEV),
                       a_spec=pl.BlockSpec((tm, kc), lambda i, k: (i, k)),
                       b_spec=pl.BlockSpec((None, kc, D), lambda i, k: (k, 0, 0)),
                       extra_specs=[pl.BlockSpec((tm, D), lambda i, k: (i, 0)), _const_spec((1, D))],
                       out_shapes=[_sds((T, D), F32), _sds((T, D), BF16)], out_specs=[pl.BlockSpec((tm, D), lambda i, k: (i, 0))] * 2,
                       dims=NN, k_axis=1, nk=N_DEV, acc_shape=(tm, D), epilogue=residual_norm)

    def back_rows(name, dy, gw, kc, extras, epilogue):
        return _matmul(name, dy, gw, extras, grid=(N_DEV, nt),
                       a_spec=pl.BlockSpec((tm, D), lambda j, i: (i, 0)),
                       b_spec=pl.BlockSpec((None, kc, D), lambda j, i: (j, 0, 0)),
                       extra_specs=[pl.BlockSpec((tm, kc), lambda j, i: (i, j))] * len(extras),
                       out_shapes=[_sds((T, kc * N_DEV), BF16)], out_specs=[pl.BlockSpec((tm, kc), lambda j, i: (i, j))],
                       dims=NT, epilogue=epilogue)[0]

    def norm_bwd_epilogue(dh, xv, g, dxi):
        dxn, dg = _rms_bwd(xv, g, dh)
        return dxi + dxn, dxi + dxn, dg

    def back_cols(name, da, gw, nc, xv, g, dx_in):
        row = pl.BlockSpec((tm, D), lambda i, k: (i, 0))
        return _matmul(name, da, gw, [xv, g.reshape(1, D), dx_in], grid=(nt, N_DEV),
                       a_spec=pl.BlockSpec((tm, nc), lambda i, k: (i, k)),
                       b_spec=pl.BlockSpec((None, D, nc), lambda i, k: (k, 0, 0)), extra_specs=[row, _const_spec((1, D)), row],
                       out_shapes=[_sds((T, D), F32), _sds((T, D), BF16), _sds((1, D), F32)], out_specs=[row, row, _const_spec((1, D))],
                       dims=NT, k_axis=1, nk=N_DEV, acc_shape=(tm, D), epilogue=norm_bwd_epilogue, n_sum=1)

    def wgrad_cols(name, h, da, nc):
        return _matmul(name, h, da, [], grid=(N_DEV, nt),
                       a_spec=pl.BlockSpec((tm, D), lambda j, t: (t, 0)), b_spec=pl.BlockSpec((tm, nc), lambda j, t: (t, j)),
                       extra_specs=[], out_shapes=[_sds((N_DEV, D, nc), BF16)],
                       out_specs=[pl.BlockSpec((None, D, nc), lambda j, t: (j, 0, 0))],
                       dims=TN, k_axis=1, nk=nt, acc_shape=(D, nc))[0]

    def wgrad_rows(name, a, dy, kc, ncols):
        return _matmul(name, a, dy, [], grid=(a.shape[1] // kc, nt),
                       a_spec=pl.BlockSpec((tm, kc), lambda j, t: (t, j)), b_spec=pl.BlockSpec((tm, ncols), lambda j, t: (t, 0)),
                       extra_specs=[], out_shapes=[_sds((a.shape[1], ncols), BF16)],
                       out_specs=[pl.BlockSpec((kc, ncols), lambda j, t: (j, 0))],
                       dims=TN, k_axis=1, nk=nt, acc_shape=(kc, ncols))[0]

    saved = []
    xs = x2
    for i in range(depth):
        l = i // 2
        if i == 0:
            h = rmsnorm(xs, norm_mix[0])
        if i % 2 == 0:
            lat = _matmul("mla_down", h, w_dkv, [], grid=(nt,), a_spec=_row_spec(tm, D),
                          b_spec=pl.BlockSpec((None, D, LAT_PAD), lambda i_: (l, 0, 0)), extra_specs=[],
                          out_shapes=[_sds((T, LAT_PAD), F32)], out_specs=[_row_spec(tm, LAT_PAD)], dims=NN)[0]

            def latent_post(la, qn, kvn, cc, sa, sb):
                cq = _rms_fwd(la[:, :Q_RANK], qn)
                ckv = _rms_fwd(la[:, Q_RANK:Q_RANK + KV_RANK], kvn)
                kr = _rope_fwd(la[:, Q_RANK + KV_RANK:], cc, sa, sb)
                return cq, ckv, kr

            cq, ckv, kr = _rowwise(
                "mla_latent", latent_post, [lat, mla_q_norm[l].reshape(1, Q_RANK), mla_kv_norm[l].reshape(1, KV_RANK), t_cc, t_sa, t_sb],
                grid=(nt,), in_specs=[_row_spec(tm, LAT_PAD), _const_spec((1, Q_RANK)), _const_spec((1, KV_RANK))] + tab_specs,
                out_shapes=[_sds((T, Q_RANK), BF16), _sds((T, KV_RANK), BF16), _sds((T, 128), BF16)],
                out_specs=[_row_spec(tm, Q_RANK), _row_spec(tm, KV_RANK), _row_spec(tm, 128)])

            def q_epilogue(acc, cc, sa, sb):
                return (jnp.concatenate([acc[:, :NOPE], _rope_fwd(acc[:, NOPE:], cc, sa, sb)], axis=1),)

            q = _matmul("mla_q", cq, w_uq, [t_cc, t_sa, t_sb], grid=(HEADS, nt),
                        a_spec=pl.BlockSpec((tm, Q_RANK), lambda b, i_: (i_, 0)),
                        b_spec=pl.BlockSpec((None, None, Q_RANK, QPAD), lambda b, i_: (l, b, 0, 0)),
                        extra_specs=[pl.BlockSpec((tm, 128), lambda b, i_: (i_, 0))] * 3,
                        out_shapes=[_sds((HEADS, T, QPAD), BF16)], out_specs=[pl.BlockSpec((None, tm, QPAD), lambda b, i_: (b, i_, 0))],
                        dims=NN, epilogue=q_epilogue)[0]

            def kv_epilogue(acc, krb):
                return jnp.concatenate([acc[:, :NOPE], krb.astype(F32)], axis=1), acc[:, NOPE:], acc[:, NOPE:].T

            kk, vv, vt = _matmul("mla_kv", ckv, w_ukv, [kr], grid=(HEADS, T // tq),
                                 a_spec=pl.BlockSpec((tq, KV_RANK), lambda b, i_: (i_, 0)),
                                 b_spec=pl.BlockSpec((None, None, KV_RANK, NOPE + VDIM), lambda b, i_: (l, b, 0, 0)),
                                 extra_specs=[pl.BlockSpec((tq, 128), lambda b, i_: (i_, 0))],
                                 out_shapes=[_sds((HEADS, T, QPAD), BF16), _sds((HEADS, T, VDIM), BF16), _sds((HEADS, T // tq, VDIM, tq), BF16)],
                                 out_specs=[pl.BlockSpec((None, tq, QPAD), lambda b, i_: (b, i_, 0)), pl.BlockSpec((None, tq, VDIM), lambda b, i_: (b, i_, 0)),
                                            pl.BlockSpec((None, None, VDIM, tq), lambda b, i_: (b, i_, 0, 0))],
                                 dims=NN, epilogue=kv_epilogue)
            group = [up_sh[i], down_sh[i], in_sh[l], out_sh[l], up_sh[i + 1], down_sh[i + 1]]
            o, lse, *bufs = _flash_fwd(q, kk, vt, tq, comm=_gather_level1(group))
            xm, h2, *bufs = _matmul("mla_out", o, w_o, [xs, norm_ffn[i].reshape(1, D)], grid=(nt,), a_spec=_row_spec(tm, OW),
                                    b_spec=pl.BlockSpec((None, OW, D), lambda i_: (l, 0, 0)), extra_specs=[_row_spec(tm, D), _const_spec((1, D))],
                                    out_shapes=[_sds((T, D), F32), _sds((T, D), BF16)], out_specs=[_row_spec(tm, D)] * 2, dims=NN,
                                    epilogue=residual_norm, comm=_gather_level2(bufs))
            g_up[i], g_down[i], g_in[l], g_out[l], g_up[i + 1], g_down[i + 1] = bufs
            mix_saved = (h, lat, cq, ckv, q, kk, vv, o, lse)
        else:
            gp, ge = proj_cols("sgu_in", h, g_in[l], e2c, _gelu_and_grad, 2)
            gate = _sgu_mid_fwd(ge, ln_g_full[l], ln_b_full[l], sgu_w_spatial[l], b_sp[l], 4)
            xm, h2 = proj_rows_residual("sgu_out", gate, g_out[l], ec, xs, norm_ffn[i])
            mix_saved = (h, gp, ge, gate)
        r, s = proj_cols("ffn_up", h2, g_up[i], ffc, lambda acc: (jnp.maximum(acc, 0.0), jnp.square(jnp.maximum(acc, 0.0))), 2)
        xo, h_next = proj_rows_residual("ffn_down", s, g_down[i], ffc, xm, norm_mix[i + 1] if i + 1 < depth else final_norm)
        saved.append((xs, xm, mix_saved, h2, r, s))
        xs, h = xo, h_next

    def loss_head(xv, tg, g):
        y = _rms_fwd(xv, g)
        err = y - tg
        part = 0.5 * jnp.sum(jnp.sum(err * err, axis=-1, keepdims=True), axis=0, keepdims=True) / D
        dx, dg = _rms_bwd(xv, g, err / D)
        return dx, dx, jnp.broadcast_to(part, (1, 128)), dg

    dx, dyb, loss_part, d_final = _rowwise(
        "loss_head", loss_head, [xs, tgt, final_norm.reshape(1, D)], grid=(nt,),
        in_specs=[_row_spec(tm, D), _row_spec(tm, D), _const_spec((1, D))],
        out_shapes=[_sds((T, D), F32), _sds((T, D), BF16), _sds((1, 128), F32), _sds((1, D), F32)],
        out_specs=[_row_spec(tm, D), _row_spec(tm, D), _const_spec((1, 128)), _const_spec((1, D))], n_acc=2)
    loss = lax.psum(loss_part[0, 0], ("x", "y", "c"))

    d_norm_mix, d_norm_ffn = [None] * depth, [None] * depth
    d_qn, d_kvn = [None] * n_mla, [None] * n_mla
    d_wsp, d_bsp, d_lng, d_lnb = [None] * n_sgu, [None] * n_sgu, [None] * n_sgu, [None] * n_sgu
    layers = {"dkv": n_mla, "uq": n_mla, "ukv": n_mla, "o": n_mla, "in": n_sgu, "out": n_sgu, "up": depth, "down": depth}
    stacked = {nm: None for nm in layers}
    pending = []

    def add_pair(g, rcv):
        _, rws, cls = g.shape
        g4 = g.reshape(N_CHIP, 2, rws, cls)
        rt = _tile(rws, 512)
        return _rowwise("grad_pair_sum", lambda a, b_: a.astype(F32) + b_.astype(F32), [g4, rcv], grid=(N_CHIP, rws // rt),
                        in_specs=[pl.BlockSpec((None, None, rt, cls), lambda ch, i_, cr: (ch, cr[0], i_, 0)),
                                  pl.BlockSpec((None, rt, cls), lambda ch, i_, cr: (ch, i_, 0))],
                        out_shapes=_sds(rcv.shape, BF16), out_specs=pl.BlockSpec((None, rt, cls), lambda ch, i_, cr: (ch, i_, 0)),
                        grid_spec_prefetch=cidx)

    def chip_comm_of_pending():
        grads = [g for _, _, g in pending]
        from_sibling = _comm_call("grad_sibling_exchange", _sibling_exchange(grads))
        parts = [add_pair(g, rcv) for g, rcv in zip(grads, from_sibling)]
        comm, names = _chip_exchange(parts, [(nm, l_) for nm, l_, _ in pending], layers, stacked)
        pending.clear()
        return comm, names

    for i in reversed(range(depth)):
        l = i // 2
        xs_i, xm, mix_saved, h2, r, s = saved[i]
        da = back_rows("ffn_down_bwd", dyb, g_down[i], ffc, [r], lambda acc, rr: (acc * (2.0 * rr.astype(F32)),))
        pending.append(("down", i, wgrad_rows("ffn_down_wgrad", s, dyb, ffc, D).reshape(N_DEV, ffc, D)))
        pending.append(("up", i, wgrad_cols("ffn_up_wgrad", h2, da, ffc)))
        dx, dyb, d_norm_ffn[i] = back_cols("ffn_up_bwd", da, g_up[i], ffc, xm, norm_ffn[i], dx)
        if i % 2 == 0:
            h, lat, cq, ckv, q, kk, vv, o, lse = mix_saved
            do = _matmul("mla_out_bwd", dyb, w_o, [], grid=(nt,), a_spec=_row_spec(tm, D),
                         b_spec=pl.BlockSpec((None, OW, D), lambda i_: (l, 0, 0)), extra_specs=[],
                         out_shapes=[_sds((T, OW), BF16)], out_specs=[_row_spec(tm, OW)], dims=NT)[0]
            g_o_l = wgrad_rows("mla_out_wgrad", o, dyb, OW, D).reshape(N_DEV, owc, D)
            comm, names = chip_comm_of_pending()
            dqt, dk, dv, *bufs = _flash_bwd(q, kk, vv, o, do, lse, tq, comm=comm)
            stacked.update(dict(zip(names, bufs)))
            pending.append(("o", l, g_o_l))

            def q_pre(dt, cc, sa, sb):
                d = dt.T
                return jnp.concatenate([d[:, :NOPE], _rope_bwd(d[:, NOPE:], cc, sa, sb)], axis=1)

            dq_pre = _rowwise("mla_dq_rope", q_pre, [dqt, t_cc, t_sa, t_sb], grid=(HEADS, T // tq),
                              in_specs=[pl.BlockSpec((None, None, QPAD, tq), lambda b, i_: (b, i_, 0, 0))] + [pl.BlockSpec((tq, 128), lambda b, i_: (i_, 0))] * 3,
                              out_shapes=_sds((HEADS, T, QPAD), BF16), out_specs=pl.BlockSpec((None, tq, QPAD), lambda b, i_: (b, i_, 0)))

            def kv_pre(dkb, dvb, cc, sa, sb):
                dkv = jnp.concatenate([dkb[:, :, :NOPE], dvb], axis=2)
                dkr = _rope_bwd(jnp.sum(dkb[:, :, NOPE:], axis=0), cc, sa, sb)
                return dkv, dkr

            dkv, dkr = _rowwise("mla_dkv_rope", kv_pre, [dk, dv, t_cc, t_sa, t_sb], grid=(T // ts,),
                                in_specs=[pl.BlockSpec((HEADS, ts, QPAD), lambda i_: (0, i_, 0)), pl.BlockSpec((HEADS, ts, VDIM), lambda i_: (0, i_, 0))] + [_row_spec(ts, 128)] * 3,
                                out_shapes=[_sds((HEADS, T, NOPE + VDIM), BF16), _sds((T, 128), F32)],
                                out_specs=[pl.BlockSpec((HEADS, ts, NOPE + VDIM), lambda i_: (0, i_, 0)), _row_spec(ts, 128)])
            g_uq_l = _matmul("mla_q_wgrad", cq, dq_pre, [], grid=(HEADS, nt),
                             a_spec=pl.BlockSpec((tm, Q_RANK), lambda b, t_: (t_, 0)), b_spec=pl.BlockSpec((None, tm, QPAD), lambda b, t_: (b, t_, 0)),
                             extra_specs=[], out_shapes=[_sds((HEADS, Q_RANK, QPAD), BF16)],
                             out_specs=[pl.BlockSpec((None, Q_RANK, QPAD), lambda b, t_: (b, 0, 0))],
                             dims=TN, k_axis=1, nk=nt, acc_shape=(Q_RANK, QPAD))[0]
            g_ukv_l = _matmul("mla_kv_wgrad", ckv, dkv, [], grid=(HEADS, nt),
                              a_spec=pl.BlockSpec((tm, KV_RANK), lambda b, t_: (t_, 0)), b_spec=pl.BlockSpec((None, tm, NOPE + VDIM), lambda b, t_: (b, t_, 0)),
                              extra_specs=[], out_shapes=[_sds((HEADS, KV_RANK, NOPE + VDIM), BF16)],
                              out_specs=[pl.BlockSpec((None, KV_RANK, NOPE + VDIM), lambda b, t_: (b, 0, 0))],
                              dims=TN, k_axis=1, nk=nt, acc_shape=(KV_RANK, NOPE + VDIM))[0]
            pending.append(("uq", l, g_uq_l[:, :, :NOPE + ROPE]))
            pending.append(("ukv", l, g_ukv_l))
            dcq = _matmul("mla_q_bwd", dq_pre, w_uq, [], grid=(nt, HEADS),
                          a_spec=pl.BlockSpec((None, tm, QPAD), lambda i_, b: (b, i_, 0)),
                          b_spec=pl.BlockSpec((None, None, Q_RANK, QPAD), lambda i_, b: (l, b, 0, 0)), extra_specs=[],
                          out_shapes=[_sds((T, Q_RANK), F32)], out_specs=[pl.BlockSpec((tm, Q_RANK), lambda i_, b: (i_, 0))],
                          dims=NT, k_axis=1, nk=HEADS, acc_shape=(tm, Q_RANK))[0]
            dckv = _matmul("mla_kv_bwd", dkv, w_ukv, [], grid=(nt, HEADS),
                           a_spec=pl.BlockSpec((None, tm, NOPE + VDIM), lambda i_, b: (b, i_, 0)),
                           b_spec=pl.BlockSpec((None, None, KV_RANK, NOPE + VDIM), lambda i_, b: (l, b, 0, 0)), extra_specs=[],
                           out_shapes=[_sds((T, KV_RANK), F32)], out_specs=[pl.BlockSpec((tm, KV_RANK), lambda i_, b: (i_, 0))],
                           dims=NT, k_axis=1, nk=HEADS, acc_shape=(tm, KV_RANK))[0]

            def latent_bwd(la, qn, kvn, dq_, dkv_, dkr_):
                dcq_raw, dqn = _rms_bwd(la[:, :Q_RANK], qn, dq_)
                dckv_raw, dkvn = _rms_bwd(la[:, Q_RANK:Q_RANK + KV_RANK], kvn, dkv_)
                return jnp.concatenate([dcq_raw, dckv_raw, dkr_], axis=1), dqn, dkvn

            dlat, d_qn[l], d_kvn[l] = _rowwise(
                "mla_latent_bwd", latent_bwd, [lat, mla_q_norm[l].reshape(1, Q_RANK), mla_kv_norm[l].reshape(1, KV_RANK), dcq, dckv, dkr],
                grid=(nt,), in_specs=[_row_spec(tm, LAT_PAD), _const_spec((1, Q_RANK)), _const_spec((1, KV_RANK)),
                                      _row_spec(tm, Q_RANK), _row_spec(tm, KV_RANK), _row_spec(tm, 128)],
                out_shapes=[_sds((T, LAT_PAD), BF16), _sds((1, Q_RANK), F32), _sds((1, KV_RANK), F32)],
                out_specs=[_row_spec(tm, LAT_PAD), _const_spec((1, Q_RANK)), _const_spec((1, KV_RANK))], n_acc=2)
            g_dkv_l = wgrad_rows("mla_down_wgrad", h, dlat, D, LAT_PAD)
            pending.append(("dkv", l, g_dkv_l[:, :LAT].reshape(N_DEV, dc, LAT)))
            dx, dyb, d_norm_mix[i] = _matmul(
                "mla_down_bwd", dlat, w_dkv, [xs_i, norm_mix[i].reshape(1, D), dx], grid=(nt,), a_spec=_row_spec(tm, LAT_PAD),
                b_spec=pl.BlockSpec((None, D, LAT_PAD), lambda i_: (l, 0, 0)), extra_specs=[_row_spec(tm, D), _const_spec((1, D)), _row_spec(tm, D)],
                out_shapes=[_sds((T, D), F32), _sds((T, D), BF16), _sds((1, D), F32)],
                out_specs=[_row_spec(tm, D), _row_spec(tm, D), _const_spec((1, D))], dims=NT, epilogue=norm_bwd_epilogue, n_sum=1)
        else:
            h, gp, ge, gate = mix_saved
            dgate = back_rows("sgu_out_bwd", dyb, g_out[l], ec, [], None)
            pending.append(("out", l, wgrad_rows("sgu_out_wgrad", gate, dyb, ec, D).reshape(N_DEV, ec, D)))
            dz, d_wsp[l], d_bsp[l], d_lng[l], d_lnb[l] = _sgu_mid_bwd(ge, gp, dgate, ln_g_full[l], ln_b_full[l], sgu_w_spatial[l], b_sp[l], 2)
            pending.append(("in", l, wgrad_cols("sgu_in_wgrad", h, dz, e2c)))
            dx, dyb, d_norm_mix[i] = back_cols("sgu_in_bwd", dz, g_in[l], e2c, xs_i, norm_mix[i], dx)
    grad_x = dx.reshape(1, T, D)

    comm, names = chip_comm_of_pending()
    stacked.update(dict(zip(names, _comm_call("grad_chip_exchange", comm))))

    def adam_big(name, parts, w, m, v):
        lyr, rws, cls = w.shape
        rt = _tile(rws, 256)

        def fn(p, w_, m_, v_):
            g = (p[0].astype(F32) + p[1].astype(F32)) + (p[2].astype(F32) + p[3].astype(F32))
            return (g, *_adam(w_, g, m_, v_))

        spec = pl.BlockSpec((None, rt, cls), lambda l_, i_: (l_, i_, 0))
        return _rowwise(name, fn, [parts, w, m, v], grid=(lyr, rws // rt),
                        in_specs=[pl.BlockSpec((N_CHIP, None, rt, cls), lambda l_, i_: (0, l_, i_, 0)), spec, spec, spec],
                        out_shapes=[_sds(w.shape, F32)] * 4, out_specs=[spec] * 4)

    big_names = ["dkv", "uq", "ukv", "o", "in", "out", "up", "down"]
    big_w = [mla_w_dkv, mla_w_uq, mla_w_ukv, mla_w_o, sgu_w_in, sgu_w_out, ffn_w_up, ffn_w_down]
    big_m = [m_mla_w_dkv, m_mla_w_uq, m_mla_w_ukv, m_mla_w_o, m_sgu_w_in, m_sgu_w_out, m_ffn_w_up, m_ffn_w_down]
    big_v = [v_mla_w_dkv, v_mla_w_uq, v_mla_w_ukv, v_mla_w_o, v_sgu_w_in, v_sgu_w_out, v_ffn_w_up, v_ffn_w_down]
    big_res = [adam_big("adam_large", stacked[nm], w, m, v) for nm, w, m, v in zip(big_names, big_w, big_m, big_v)]

    def rows128(a, rows):
        flat = a.reshape(-1, 128)
        return jnp.pad(flat, ((0, rows - flat.shape[0]), (0, 0)))

    def pad8(n):
        return -(-n // 8) * 8

    small_names = ["norm_mix", "norm_ffn", "final_norm", "q_norm", "kv_norm", "w_spatial", "b_spatial"]
    small_g = [jnp.concatenate(d_norm_mix, 0), jnp.concatenate(d_norm_ffn, 0), d_final, jnp.concatenate(d_qn, 0), jnp.concatenate(d_kvn, 0),
               jnp.stack(d_wsp, 0), jnp.stack(d_bsp, 0), jnp.concatenate(d_lng, 0), jnp.concatenate(d_lnb, 0)]
    small_w = [norm_mix, norm_ffn, final_norm, mla_q_norm, mla_kv_norm, sgu_w_spatial, sgu_b_spatial]
    small_m = [m_norm_mix, m_norm_ffn, m_final_norm, m_mla_q_norm, m_mla_kv_norm, m_sgu_w_spatial, m_sgu_b_spatial]
    small_v = [v_norm_mix, v_norm_ffn, v_final_norm, v_mla_q_norm, v_mla_kv_norm, v_sgu_w_spatial, v_sgu_b_spatial]
    sizes = [pad8(g.size // 128) for g in small_g]
    n_rep = len(small_w)
    sizes[n_rep - 1] += -sum(sizes[:n_rep]) % SMALL_ROWS
    offs = [sum(sizes[:k]) for k in range(len(sizes) + 1)]
    rep_rows = offs[n_rep]
    pack_g = jnp.concatenate([rows128(g, sz) for g, sz in zip(small_g, sizes)], axis=0)
    (gathered_small,) = _all_gather("gather_small_grads", [pack_g])

    def pack(arrs):
        return jnp.concatenate([rows128(a, sz) for a, sz in zip(arrs, sizes[:n_rep])], axis=0)

    def sum8(p):
        return ((p[0] + p[1]) + (p[2] + p[3])) + ((p[4] + p[5]) + (p[6] + p[7]))

    sspec = _row_spec(SMALL_ROWS, 128)
    rep_g, rep_d, rep_m, rep_v = _rowwise(
        "adam_small", lambda p, w_, m_, v_: (sum8(p), *_adam(w_, sum8(p), m_, v_)),
        [gathered_small, pack(small_w), pack(small_m), pack(small_v)], grid=(rep_rows // SMALL_ROWS,),
        in_specs=[pl.BlockSpec((N_DEV, SMALL_ROWS, 128), lambda i_: (0, i_, 0)), sspec, sspec, sspec],
        out_shapes=[_sds((rep_rows, 128), F32)] * 4, out_specs=[sspec] * 4)

    def unpack(packed, k, like):
        return packed[offs[k]:offs[k] + like.size // 128].reshape(like.shape)

    my_b = 4 * lax.axis_index("x") + 2 * lax.axis_index("y") + lax.axis_index("c")
    ln_w = jnp.concatenate([sgu_ln_g, sgu_ln_b], 0)
    ln_m = jnp.concatenate([m_sgu_ln_g, m_sgu_ln_b], 0)
    ln_v = jnp.concatenate([v_sgu_ln_g, v_sgu_ln_b], 0)
    ln_all = gathered_small[:, rep_rows:, :]
    ln_mine = lax.dynamic_slice_in_dim(ln_all[:, :2 * n_sgu * E // 128].reshape(N_DEV, 2 * n_sgu, N_DEV, ec), my_b, 1, axis=2).reshape(N_DEV, 2 * n_sgu, ec)
    ln_g_, ln_d, ln_m2, ln_v2 = _rowwise(
        "adam_ln", lambda p, w_, m_, v_: (sum8(p), *_adam(w_, sum8(p), m_, v_)), [ln_mine, ln_w, ln_m, ln_v], grid=(1,),
        in_specs=[_const_spec(ln_mine.shape), _const_spec(ln_w.shape), _const_spec(ln_w.shape), _const_spec(ln_w.shape)],
        out_shapes=[_sds(ln_w.shape, F32)] * 4, out_specs=[_const_spec(ln_w.shape)] * 4)

    def family(pos):
        rep = [rep_g, rep_d, rep_m, rep_v][pos]
        ln = [ln_g_, ln_d, ln_m2, ln_v2][pos]
        small = {nm: unpack(rep, k, w_) for k, (nm, w_) in enumerate(zip(small_names, small_w))}
        big = [res[pos] for res in big_res]
        return [small["norm_mix"], small["norm_ffn"], small["final_norm"], big[0], small["q_norm"], small["kv_norm"], big[1], big[2], big[3],
                big[4], ln[:n_sgu], ln[n_sgu:], small["w_spatial"], small["b_spatial"], big[5], big[6], big[7]]

    return (loss, grad_x, *family(0), *family(1), *family(2), *family(3))
```

```python
import math

import jax
import jax.numpy as jnp
from jax import lax
from jax.experimental import pallas as pl
from jax.experimental.pallas import tpu as pltpu

F32 = jnp.float32
BF16 = jnp.bfloat16
MESH = pl.DeviceIdType.MESH

N_DEV = 8
N_CHIP = 4
HEADS = 8
NOPE = 128
ROPE = 64
VDIM = 128
QPAD = 256
Q_RANK = 256
KV_RANK = 128
LAT = Q_RANK + KV_RANK + ROPE
LAT_PAD = 512
ROPE_THETA = 10000.0
SGU_CHUNK = 128
SGU_GROUPS = 8
NORM_EPS = 1e-6
LN_EPS = 1e-5
ADAM_LR = 0.001
ADAM_B1 = 0.9
ADAM_B2 = 0.999
ADAM_EPS = 1e-08
ADAM_WD = 0.01
ADAM_STEP = 10
ATTN_SCALE = (NOPE + ROPE) ** -0.5
NEG = -1e30
EXP2_SCALE = ATTN_SCALE * math.log2(math.e)
VMEM_LIMIT = 56 * 1024 * 1024
SMALL_ROWS = 256

NN = (((1,), (0,)), ((), ()))
NT = (((1,), (1,)), ((), ()))
TN = (((0,), (0,)), ((), ()))
ANY = pl.BlockSpec(memory_space=pl.ANY)


def _pcall(body, **kw):
    return pl.pallas_call(body, **kw)


def _params(n_grid, side_effects=False):
    return pltpu.CompilerParams(dimension_semantics=("arbitrary",) * n_grid, vmem_limit_bytes=VMEM_LIMIT, has_side_effects=side_effects)


def _sds(shape, dtype):
    return jax.ShapeDtypeStruct(tuple(shape), dtype)


def _tile(n, want):
    t = min(n, want)
    assert n % t == 0, (n, want)
    return t


class _Comm:
    def __init__(self, operands, out_shapes, aliases, scratch, start, finish):
        self.operands, self.out_shapes, self.aliases, self.scratch = operands, out_shapes, aliases, scratch
        self.start, self.finish = start, finish


def _place():
    return lax.axis_index("x"), lax.axis_index("y"), lax.axis_index("c")


def _other_chips(x, y):
    return [(1 - x, y), (x, 1 - y), (1 - x, 1 - y)]


def _dev_index(dev):
    return 4 * dev[0] + 2 * dev[1] + dev[2]


def _comm_call(name, comm):
    c_in, c_out = len(comm.operands), len(comm.out_shapes)

    def body(*refs):
        ins, outs, sems = refs[:c_in], refs[c_in:c_in + c_out], refs[c_in + c_out:]
        comm.start(ins, outs, sems)
        comm.finish(ins, outs, sems)

    return _pcall(body, name=name, in_specs=[ANY] * c_in, out_specs=[ANY] * c_out, out_shape=comm.out_shapes,
                  scratch_shapes=comm.scratch, input_output_aliases=dict(comm.aliases),
                  compiler_params=pltpu.CompilerParams(has_side_effects=True))(*comm.operands)


def _call(name, body, operands, in_specs, out_shapes, out_specs, scratch, grid, comm=None):
    if comm is None:
        return _pcall(body, name=name, grid=grid, in_specs=in_specs, out_specs=out_specs, out_shape=out_shapes,
                      scratch_shapes=scratch, compiler_params=_params(len(grid)))(*operands)
    n_in, n_out, n_sc = len(operands), len(out_shapes), len(scratch)
    c_in, c_out = len(comm.operands), len(comm.out_shapes)

    def hosted(*refs):
        ins, cins = refs[:n_in], refs[n_in:n_in + c_in]
        o0 = n_in + c_in
        outs, couts = refs[o0:o0 + n_out], refs[o0 + n_out:o0 + n_out + c_out]
        rest = refs[o0 + n_out + c_out:]
        sc, csems = rest[:n_sc], rest[n_sc:]
        first = pl.program_id(0) == 0
        last = pl.program_id(0) == grid[0] - 1
        for d in range(1, len(grid)):
            first = jnp.logical_and(first, pl.program_id(d) == 0)
            last = jnp.logical_and(last, pl.program_id(d) == grid[d] - 1)

        @pl.when(first)
        def _():
            comm.start(cins, couts, csems)

        body(*ins, *outs, *sc)

        @pl.when(last)
        def _():
            comm.finish(cins, couts, csems)

    return _pcall(hosted, name=name, grid=grid, in_specs=[*in_specs, *[ANY] * c_in], out_specs=[*out_specs, *[ANY] * c_out],
                  out_shape=[*out_shapes, *comm.out_shapes], scratch_shapes=[*scratch, *comm.scratch],
                  input_output_aliases={n_in + k: n_out + v for k, v in comm.aliases.items()},
                  compiler_params=_params(len(grid), side_effects=True))(*operands, *comm.operands)


def _gather_level1(shards):
    n = len(shards)

    def copies(ins, outs, sems):
        send_sems, recv_sems, local_sems = sems
        x, y, c = _place()
        me, sibling = (x, y, c), (x, y, 1 - c)
        chips = _other_chips(x, y)

        def copy(a, k, block, to, src=None):
            slot = outs[a].at[_dev_index(block)]
            return pltpu.make_async_remote_copy(src_ref=slot if src is None else src, dst_ref=slot, send_sem=send_sems.at[a, k],
                                                recv_sem=recv_sems.at[a, k], device_id=to, device_id_type=MESH)

        mine = [pltpu.make_async_copy(ins[a], outs[a].at[_dev_index(me)], local_sems.at[a]) for a in range(n)]
        sends = [copy(a, 1 + j, me, (*chip, c), src=ins[a]) for j, chip in enumerate(chips) for a in range(n)]
        sends += [copy(a, 0, me, sibling, src=ins[a]) for a in range(n)]
        recvs = [copy(a, 1 + j, (*chip, c), me) for j, chip in enumerate(chips) for a in range(n)]
        recvs += [copy(a, 0, sibling, me) for a in range(n)]
        return mine, sends, recvs

    def start(ins, outs, sems):
        mine, sends, _ = copies(ins, outs, sems)
        for cp in mine + sends:
            cp.start()

    def finish(ins, outs, sems):
        mine, sends, recvs = copies(ins, outs, sems)
        for cp in recvs:
            cp.wait_recv()
        for cp in sends:
            cp.wait_send()
        for cp in mine:
            cp.wait()

    return _Comm(shards, [_sds((N_DEV, *a.shape), a.dtype) for a in shards], {},
                 [pltpu.SemaphoreType.DMA((n, 4)), pltpu.SemaphoreType.DMA((n, 4)), pltpu.SemaphoreType.DMA((n,))], start, finish)


def _gather_level2(bufs):
    n = len(bufs)

    def copies(outs, sems):
        send_sems, recv_sems = sems
        x, y, c = _place()
        sibling = (x, y, 1 - c)
        sends, recvs = [], []
        for j, chip in enumerate(_other_chips(x, y)):
            for a in range(n):
                have, want = outs[a].at[_dev_index((*chip, c))], outs[a].at[_dev_index((*chip, 1 - c))]
                sends.append(pltpu.make_async_remote_copy(src_ref=have, dst_ref=have, send_sem=send_sems.at[a, j], recv_sem=recv_sems.at[a, j],
                                                          device_id=sibling, device_id_type=MESH))
                recvs.append(pltpu.make_async_remote_copy(src_ref=want, dst_ref=want, send_sem=send_sems.at[a, j], recv_sem=recv_sems.at[a, j],
                                                          device_id=sibling, device_id_type=MESH))
        return sends, recvs

    def start(ins, outs, sems):
        for cp in copies(outs, sems)[0]:
            cp.start()

    def finish(ins, outs, sems):
        sends, recvs = copies(outs, sems)
        for cp in recvs:
            cp.wait_recv()
        for cp in sends:
            cp.wait_send()

    return _Comm(bufs, [_sds(b.shape, b.dtype) for b in bufs], {a: a for a in range(n)},
                 [pltpu.SemaphoreType.DMA((n, 3)), pltpu.SemaphoreType.DMA((n, 3))], start, finish)


def _all_gather(name, arrays):
    n = len(arrays)

    def body(*refs):
        ins = refs[:n]
        outs = refs[n:2 * n]
        send_sems, recv_sems, local_sems = refs[2 * n:]
        x, y, c = _place()
        me, sibling = (x, y, c), (x, y, 1 - c)
        chips = _other_chips(x, y)

        def copy(a, k, block, to, src=None):
            slot = outs[a].at[_dev_index(block)]
            return pltpu.make_async_remote_copy(src_ref=slot if src is None else src, dst_ref=slot, send_sem=send_sems.at[a, k],
                                                recv_sem=recv_sems.at[a, k], device_id=to, device_id_type=MESH)

        mine = [pltpu.make_async_copy(ins[a], outs[a].at[_dev_index(me)], local_sems.at[a]) for a in range(n)]
        for cp in mine:
            cp.start()
        first = []
        for j, chip in enumerate(chips):
            first += [copy(a, 1 + j, me, (*chip, c), src=ins[a]) for a in range(n)]
        first += [copy(a, 0, me, sibling, src=ins[a]) for a in range(n)]
        for cp in first:
            cp.start()
        passed = []
        for j, chip in enumerate(chips):
            for a in range(n):
                copy(a, 1 + j, (*chip, c), me).wait_recv()
                fwd = copy(a, 4 + j, (*chip, c), sibling)
                fwd.start()
                passed.append(fwd)
        for a in range(n):
            copy(a, 0, sibling, me).wait_recv()
            for j, chip in enumerate(chips):
                copy(a, 4 + j, (*chip, 1 - c), me).wait_recv()
        for cp in first + passed:
            cp.wait_send()
        for cp in mine:
            cp.wait()

    return _pcall(
        body, name=name, in_specs=[ANY] * n, out_specs=[ANY] * n,
        out_shape=[_sds((N_DEV, *a.shape), a.dtype) for a in arrays],
        scratch_shapes=[pltpu.SemaphoreType.DMA((n, 7)), pltpu.SemaphoreType.DMA((n, 7)), pltpu.SemaphoreType.DMA((n,))],
        compiler_params=pltpu.CompilerParams(has_side_effects=True),
    )(*arrays)


def _sibling_exchange(grads):
    n = len(grads)

    def start(ins, outs, sems):
        send_sems, recv_sems = sems
        x, y, c = _place()
        for a in range(n):
            for ch in range(N_CHIP):
                pltpu.make_async_remote_copy(src_ref=ins[a].at[2 * ch + 1 - c], dst_ref=outs[a].at[ch], send_sem=send_sems.at[a],
                                             recv_sem=recv_sems.at[a], device_id=(x, y, 1 - c), device_id_type=MESH).start()

    def finish(ins, outs, sems):
        send_sems, recv_sems = sems
        x, y, c = _place()
        for a in range(n):
            pltpu.make_async_remote_copy(src_ref=outs[a], dst_ref=outs[a], send_sem=send_sems.at[a], recv_sem=recv_sems.at[a],
                                         device_id=(x, y, 1 - c), device_id_type=MESH).wait()

    return _Comm(grads, [_sds((N_CHIP, *g.shape[1:]), g.dtype) for g in grads], {},
                 [pltpu.SemaphoreType.DMA((n,)), pltpu.SemaphoreType.DMA((n,))], start, finish)


def _chip_exchange(parts, slots, layers, stacked):
    n = len(parts)
    names = []
    for nm, _ in slots:
        if nm not in names:
            names.append(nm)
    shapes = {nm: _sds((N_CHIP, layers[nm], *parts[a].shape[1:]), parts[a].dtype) for a, (nm, _) in enumerate(slots)}
    kept = [nm for nm in names if stacked.get(nm) is not None]
    aliases = {n + k: names.index(nm) for k, nm in enumerate(kept)}

    def copies(ins, outs, sems):
        send_sems, recv_sems, local_sems = sems
        x, y, c = _place()
        mine = 2 * x + y
        local, sends, recvs = [], [], []
        for a, (nm, l) in enumerate(slots):
            buf = outs[names.index(nm)]
            local.append(pltpu.make_async_copy(ins[a].at[mine], buf.at[mine, l], local_sems.at[a]))
            for j, chip in enumerate(_other_chips(x, y)):
                theirs = buf.at[2 * chip[0] + chip[1], l]
                sends.append(pltpu.make_async_remote_copy(src_ref=ins[a].at[2 * chip[0] + chip[1]], dst_ref=buf.at[mine, l], send_sem=send_sems.at[a, j],
                                                          recv_sem=recv_sems.at[a, j], device_id=(*chip, c), device_id_type=MESH))
                recvs.append(pltpu.make_async_remote_copy(src_ref=theirs, dst_ref=theirs, send_sem=send_sems.at[a, j],
                                                          recv_sem=recv_sems.at[a, j], device_id=(*chip, c), device_id_type=MESH))
        return local, sends, recvs

    def start(ins, outs, sems):
        local, sends, _ = copies(ins, outs, sems)
        for cp in local + sends:
            cp.start()

    def finish(ins, outs, sems):
        local, sends, recvs = copies(ins, outs, sems)
        for cp in recvs:
            cp.wait_recv()
        for cp in sends:
            cp.wait_send()
        for cp in local:
            cp.wait()

    comm = _Comm([*parts, *[stacked[nm] for nm in kept]], [shapes[nm] for nm in names], aliases,
                 [pltpu.SemaphoreType.DMA((n, 3)), pltpu.SemaphoreType.DMA((n, 3)), pltpu.SemaphoreType.DMA((n,))], start, finish)
    return comm, names


def _matmul(name, a, b, extras, *, grid, a_spec, b_spec, extra_specs, out_shapes, out_specs, dims, k_axis=None, nk=1,
            acc_shape=None, epilogue=None, comm=None, n_sum=0, write=None):
    n_extra = len(extras)
    n_out = len(out_shapes)

    def body(*refs):
        a_ref, b_ref = refs[0], refs[1]
        ex = refs[2:2 + n_extra]
        outs = refs[2 + n_extra:2 + n_extra + n_out]
        prod = lax.dot_general(a_ref[...], b_ref[...], dims, preferred_element_type=F32)

        def finish(acc):
            if write is not None:
                write(outs, acc, *[e[...] for e in ex])
                return
            res = epilogue(acc, *[e[...] for e in ex]) if epilogue is not None else (acc,)
            first = None
            for d in range(len(grid)):
                if d != k_axis:
                    here = pl.program_id(d) == 0
                    first = here if first is None else jnp.logical_and(first, here)
            for idx, (o, r) in enumerate(zip(outs, res)):
                if idx < n_out - n_sum:
                    o[...] = r.astype(o.dtype)
                else:
                    @pl.when(first)
                    def _(o=o, r=r):
                        o[...] = r.astype(o.dtype)

                    @pl.when(jnp.logical_not(first))
                    def _(o=o, r=r):
                        o[...] += r.astype(o.dtype)

        if k_axis is None:
            finish(prod)
        else:
            acc_ref = refs[-1]
            k = pl.program_id(k_axis)

            @pl.when(k == 0)
            def _():
                acc_ref[...] = prod

            @pl.when(k > 0)
            def _():
                acc_ref[...] += prod

            @pl.when(k == nk - 1)
            def _():
                finish(acc_ref[...])

    scratch = [] if k_axis is None else [pltpu.VMEM(acc_shape, F32)]
    return _call(name, body, [a, b, *extras], [a_spec, b_spec, *extra_specs], list(out_shapes), list(out_specs), scratch, grid, comm)


def _rowwise(name, fn, operands, *, grid, in_specs, out_shapes, out_specs, n_acc=0, grid_spec_prefetch=None):
    n_in = len(operands)
    n_out = len(out_shapes)
    n_pre = 0 if grid_spec_prefetch is None else 1

    def body(*refs):
        refs = refs[n_pre:]
        ins = refs[:n_in]
        outs = refs[n_in:n_in + n_out]
        res = fn(*[r[...] for r in ins])
        if not isinstance(res, (tuple, list)):
            res = (res,)
        first = pl.program_id(0) == 0
        for d in range(1, len(grid)):
            first = jnp.logical_and(first, pl.program_id(d) == 0)
        for idx, (o, r) in enumerate(zip(outs, res)):
            if idx < n_out - n_acc:
                o[...] = r.astype(o.dtype)
            else:
                @pl.when(first)
                def _(o=o, r=r):
                    o[...] = r.astype(o.dtype)

                @pl.when(jnp.logical_not(first))
                def _(o=o, r=r):
                    o[...] += r.astype(o.dtype)

    if grid_spec_prefetch is None:
        return _pcall(body, name=name, grid=grid, in_specs=in_specs, out_specs=out_specs, out_shape=out_shapes,
                      compiler_params=_params(len(grid)))(*operands)
    gs = pltpu.PrefetchScalarGridSpec(num_scalar_prefetch=1, grid=grid, in_specs=in_specs, out_specs=out_specs)
    return _pcall(body, name=name, grid_spec=gs, out_shape=out_shapes,
                  compiler_params=_params(len(grid)))(grid_spec_prefetch, *operands)


def _row_spec(tm, w):
    return pl.BlockSpec((tm, w), lambda i: (i, 0))


def _const_spec(shape):
    nd = len(shape)
    return pl.BlockSpec(tuple(shape), lambda *_: (0,) * nd)


def _rms_fwd(x, g):
    r = lax.rsqrt(jnp.mean(x * x, axis=-1, keepdims=True) + NORM_EPS)
    return x * r * g


def _rms_bwd(x, g, dy):
    r = lax.rsqrt(jnp.mean(x * x, axis=-1, keepdims=True) + NORM_EPS)
    xh = x * r
    u = dy * g
    dx = r * (u - xh * jnp.mean(u * xh, axis=-1, keepdims=True))
    dg = jnp.sum(dy * xh, axis=0, keepdims=True)
    return dx, dg


def _gelu_and_grad(z):
    cdf = 0.5 * (1.0 + lax.erf(z * (2.0 ** -0.5)))
    return cdf + z * jnp.exp(-0.5 * z * z) * ((2.0 * math.pi) ** -0.5), z * cdf


def _rope_fwd(x, cc, sa, sb):
    return x * cc + pltpu.roll(x, 96, 1) * sa + pltpu.roll(x, 32, 1) * sb


def _rope_bwd(d, cc, sa, sb):
    return d * cc + pltpu.roll(d * sa, 32, 1) + pltpu.roll(d * sb, 96, 1)


def _adam(w, g, m, v):
    m = ADAM_B1 * m + (1.0 - ADAM_B1) * g
    v = ADAM_B2 * v + (1.0 - ADAM_B2) * (g * g)
    m_hat = m / (1.0 - ADAM_B1 ** ADAM_STEP)
    v_hat = v / (1.0 - ADAM_B2 ** ADAM_STEP)
    delta = -ADAM_LR * (m_hat / (jnp.sqrt(v_hat) + ADAM_EPS) + ADAM_WD * w)
    return delta, m, v


def _flash_fwd(q, k, vt, tq, comm=None):
    h, t = vt.shape[0], q.shape[0]
    nq = t // tq

    def body(q_ref, k_ref, vt_ref, o_ref, lse_ref):
        qi = pl.program_id(1)

        def scores(kj):
            kb = k_ref[pl.ds(pl.multiple_of(kj * tq, tq), tq), :]
            return lax.dot_general(kb, q_ref[...], NT, preferred_element_type=F32)

        def step(kj, st, state, masked):
            m_old, l_old, acc_old = state
            if masked:
                key = lax.broadcasted_iota(jnp.int32, (tq, tq), 0)
                qry = lax.broadcasted_iota(jnp.int32, (tq, tq), 1)
                st = jnp.where(key <= qry, st, NEG)
            m_new = jnp.maximum(m_old, jnp.max(st, axis=0, keepdims=True))
            alpha = jnp.exp2((m_old - m_new) * EXP2_SCALE)
            pt = jnp.exp2((st - m_new) * EXP2_SCALE)
            l_new = alpha * l_old + jnp.sum(pt, axis=0, keepdims=True)
            acc_new = alpha * acc_old + lax.dot_general(vt_ref[kj], pt.astype(BF16), NN, preferred_element_type=F32)
            return m_new, l_new, acc_new

        init = (jnp.full((1, tq), NEG, F32), jnp.zeros((1, tq), F32), jnp.zeros((VDIM, tq), F32))
        state = lax.fori_loop(0, qi, lambda kj, st_: step(kj, scores(kj), st_, False), init)
        m, l, acc = step(qi, scores(qi), state, True)
        o_ref[...] = (acc / l).T.astype(o_ref.dtype)
        lse_ref[...] = m * EXP2_SCALE + jnp.log2(l)

    return _call(
        "flash_fwd", body, [q, k, vt],
        [pl.BlockSpec((tq, QPAD), lambda hh, i: (i, hh)),
         pl.BlockSpec((t, QPAD), lambda hh, i: (0, hh)),
         pl.BlockSpec((None, nq, VDIM, tq), lambda hh, i: (hh, 0, 0, 0))],
        [_sds((t, h * VDIM), BF16), _sds((h, nq, 1, tq), F32)],
        [pl.BlockSpec((tq, VDIM), lambda hh, i: (i, hh)),
         pl.BlockSpec((None, None, 1, tq), lambda hh, i: (hh, i, 0, 0))],
        [], (h, nq), comm)


def _flash_bwd(q, k, v, o, do, lse, tabs, tq, comm=None):
    t = q.shape[0]
    h = q.shape[1] // QPAD
    nq = t // tq

    def body(q_ref, k_ref, v_ref, o_ref, do_ref, lse_ref, cc_ref, sa_ref, sb_ref, dq_ref, dk_ref, dv_ref, delta_ref, dqt_ref):
        kj = pl.program_id(1)

        @pl.when(kj == 0)
        def _():
            dqt_ref[...] = jnp.zeros_like(dqt_ref)
            ones = jnp.ones((8, VDIM), BF16)
            for qi in range(nq):
                rows = pl.ds(qi * tq, tq)
                prod = do_ref[rows, :].astype(F32) * o_ref[rows, :].astype(F32)
                hi = prod.astype(BF16)
                lo = (prod - hi.astype(F32)).astype(BF16)
                delta_ref[qi] = (lax.dot_general(ones, hi, NT, preferred_element_type=F32)
                                 + lax.dot_general(ones, lo, NT, preferred_element_type=F32))

        kb = k_ref[...]
        vb = v_ref[...]
        kbt = kb.astype(F32).T.astype(BF16)
        dk_ref[...] = jnp.zeros_like(dk_ref)
        dv_ref[...] = jnp.zeros_like(dv_ref)

        def step(qi, masked):
            rows = pl.ds(pl.multiple_of(qi * tq, tq), tq)
            qb = q_ref[rows, :]
            dob = do_ref[rows, :]
            st = lax.dot_general(kb, qb, NT, preferred_element_type=F32)
            pt = jnp.exp2(st * EXP2_SCALE - lse_ref[qi])
            if masked:
                key = lax.broadcasted_iota(jnp.int32, (tq, tq), 0)
                qry = lax.broadcasted_iota(jnp.int32, (tq, tq), 1)
                pt = jnp.where(key <= qry, pt, 0.0)
            dv_ref[...] += lax.dot_general(pt.astype(BF16), dob, NN, preferred_element_type=F32)
            dpt = lax.dot_general(vb, dob, NT, preferred_element_type=F32)
            dst = (pt * (dpt - delta_ref[qi, pl.ds(0, 1), :]) * ATTN_SCALE).astype(BF16)
            dk_ref[...] += lax.dot_general(dst, qb, NN, preferred_element_type=F32)
            dqt_ref[qi] += lax.dot_general(kbt, dst, NN, preferred_element_type=F32)

        step(kj, True)

        def loop_body(qi, carry):
            step(qi, False)
            return carry

        lax.fori_loop(kj + 1, nq, loop_body, 0)

        @pl.when(kj == nq - 1)
        def _():
            for qi in range(nq):
                rows = pl.ds(qi * tq, tq)
                d = dqt_ref[qi].T
                roped = _rope_bwd(d[:, NOPE:], cc_ref[rows, :], sa_ref[rows, :], sb_ref[rows, :])
                dq_ref[rows, :] = jnp.concatenate([d[:, :NOPE], roped], axis=1).astype(BF16)

    head_q = pl.BlockSpec((t, QPAD), lambda hh, j: (0, hh))
    head_v = pl.BlockSpec((t, VDIM), lambda hh, j: (0, hh))
    table = pl.BlockSpec((t, 128), lambda hh, j: (0, 0))
    return _call(
        "flash_bwd", body, [q, k, v, o, do, lse, *tabs],
        [head_q, pl.BlockSpec((tq, QPAD), lambda hh, j: (j, hh)), pl.BlockSpec((tq, VDIM), lambda hh, j: (j, hh)), head_v, head_v,
         pl.BlockSpec((None, nq, 1, tq), lambda hh, j: (hh, 0, 0, 0)), table, table, table],
        [_sds((t, h * QPAD), BF16), _sds((t, h * QPAD), F32), _sds((t, h * VDIM), F32)],
        [head_q, pl.BlockSpec((tq, QPAD), lambda hh, j: (j, hh)), pl.BlockSpec((tq, VDIM), lambda hh, j: (j, hh))],
        [pltpu.VMEM((nq, 8, tq), F32), pltpu.VMEM((nq, QPAD, tq), F32)], (h, nq), comm)


def _tril_bf16(w):
    row = lax.broadcasted_iota(jnp.int32, w.shape, 0)
    col = lax.broadcasted_iota(jnp.int32, w.shape, 1)
    return jnp.where(col <= row, w, 0.0).astype(BF16)


def _layer_norm_parts(v0):
    mu = jnp.mean(v0, axis=-1, keepdims=True)
    vc = v0 - mu
    rstd = lax.rsqrt(jnp.mean(vc * vc, axis=-1, keepdims=True) + LN_EPS)
    return vc * rstd, rstd


def _sgu_mid_fwd(ge, ln_g, ln_b, w_sp, b_sp, chunks_per_step):
    t, e2 = ge.shape
    e = e2 // 2
    gd = e // SGU_GROUPS
    rows = SGU_CHUNK * chunks_per_step

    def body(u_ref, v_ref, g_ref, b_ref, w_ref, bs_ref, gate_ref):
        for ck in range(chunks_per_step):
            r = pl.ds(ck * SGU_CHUNK, SGU_CHUNK)
            xh, _ = _layer_norm_parts(v_ref[r, :].astype(F32))
            v1 = (xh * g_ref[...] + b_ref[...]).astype(BF16)
            for g in range(SGU_GROUPS):
                cols = pl.ds(g * gd, gd)
                mixed = lax.dot_general(_tril_bf16(w_ref[g]), v1[:, g * gd:(g + 1) * gd], NN, preferred_element_type=F32) + bs_ref[g]
                gate_ref[r, cols] = (u_ref[r, cols].astype(F32) * mixed).astype(BF16)

    return _pcall(
        body, name="sgu_mid_fwd", grid=(t // rows,),
        in_specs=[pl.BlockSpec((rows, e), lambda i: (i, 0)), pl.BlockSpec((rows, e), lambda i: (i, 1)),
                  _const_spec((1, e)), _const_spec((1, e)), _const_spec(w_sp.shape), _const_spec(b_sp.shape)],
        out_specs=pl.BlockSpec((rows, e), lambda i: (i, 0)),
        out_shape=_sds((t, e), BF16), compiler_params=_params(1),
    )(ge, ge, ln_g, ln_b, w_sp, b_sp)


def _sgu_mid_bwd(ge, gp, dgate, ln_g, ln_b, w_sp, b_sp, chunks_per_step):
    t, e2 = ge.shape
    e = e2 // 2
    gd = e // SGU_GROUPS
    rows = SGU_CHUNK * chunks_per_step

    def body(u_ref, v_ref, zu_ref, zv_ref, dg_ref, g_ref, b_ref, w_ref, bs_ref, dz_ref, dw_ref, dbs_ref, dlg_ref, dlb_ref):
        @pl.when(pl.program_id(0) == 0)
        def _():
            dw_ref[...] = jnp.zeros_like(dw_ref)
            dbs_ref[...] = jnp.zeros_like(dbs_ref)
            dlg_ref[...] = jnp.zeros_like(dlg_ref)
            dlb_ref[...] = jnp.zeros_like(dlb_ref)

        for ck in range(chunks_per_step):
            r = pl.ds(ck * SGU_CHUNK, SGU_CHUNK)
            xh, rstd = _layer_norm_parts(v_ref[r, :].astype(F32))
            v1 = (xh * g_ref[...] + b_ref[...]).astype(BF16)
            dv1_parts = []
            for g in range(SGU_GROUPS):
                cols = pl.ds(g * gd, gd)
                wc = _tril_bf16(w_ref[g])
                v1g = v1[:, g * gd:(g + 1) * gd]
                mixed = lax.dot_general(wc, v1g, NN, preferred_element_type=F32) + bs_ref[g]
                dgate = dg_ref[r, cols].astype(F32)
                dmixed = dgate * u_ref[r, cols].astype(F32)
                du = dgate * mixed
                dz_ref[r, cols] = (du * zu_ref[r, cols].astype(F32)).astype(BF16)
                dbs_ref[g] += jnp.sum(dmixed, axis=1, keepdims=True)
                dmb = dmixed.astype(BF16)
                dwg = lax.dot_general(dmb, v1g, NT, preferred_element_type=F32)
                row = lax.broadcasted_iota(jnp.int32, dwg.shape, 0)
                col = lax.broadcasted_iota(jnp.int32, dwg.shape, 1)
                dw_ref[g] += jnp.where(col <= row, dwg, 0.0)
                dv1_parts.append(lax.dot_general(wc, dmb, TN, preferred_element_type=F32))
            dv1 = jnp.concatenate(dv1_parts, axis=1)
            dlg_ref[...] += jnp.sum(dv1 * xh, axis=0, keepdims=True)
            dlb_ref[...] += jnp.sum(dv1, axis=0, keepdims=True)
            dxh = dv1 * g_ref[...]
            dv0 = rstd * (dxh - jnp.mean(dxh, axis=-1, keepdims=True) - xh * jnp.mean(dxh * xh, axis=-1, keepdims=True))
            dz_ref[r, pl.ds(e, e)] = (dv0 * zv_ref[r, :].astype(F32)).astype(BF16)

    half0 = pl.BlockSpec((rows, e), lambda i: (i, 0))
    half1 = pl.BlockSpec((rows, e), lambda i: (i, 1))
    return _pcall(
        body, name="sgu_mid_bwd", grid=(t // rows,),
        in_specs=[half0, half1, half0, half1, half0, _const_spec((1, e)), _const_spec((1, e)), _const_spec(w_sp.shape), _const_spec(b_sp.shape)],
        out_specs=[pl.BlockSpec((rows, e2), lambda i: (i, 0)), _const_spec(w_sp.shape), _const_spec(b_sp.shape), _const_spec((1, e)), _const_spec((1, e))],
        out_shape=[_sds((t, e2), BF16), _sds(w_sp.shape, F32), _sds(b_sp.shape, F32), _sds((1, e), F32), _sds((1, e), F32)],
        compiler_params=_params(1),
    )(ge, ge, gp, gp, dgate, ln_g, ln_b, w_sp, b_sp)


def kernel(x, positions, norm_mix, norm_ffn, final_norm, mla_w_dkv, mla_q_norm, mla_kv_norm, mla_w_uq, mla_w_ukv, mla_w_o, sgu_w_in, sgu_ln_g, sgu_ln_b, sgu_w_spatial, sgu_b_spatial, sgu_w_out, ffn_w_up, ffn_w_down, loss_target, m_norm_mix, m_norm_ffn, m_final_norm, m_mla_w_dkv, m_mla_q_norm, m_mla_kv_norm, m_mla_w_uq, m_mla_w_ukv, m_mla_w_o, m_sgu_w_in, m_sgu_ln_g, m_sgu_ln_b, m_sgu_w_spatial, m_sgu_b_spatial, m_sgu_w_out, m_ffn_w_up, m_ffn_w_down, v_norm_mix, v_norm_ffn, v_final_norm, v_mla_w_dkv, v_mla_q_norm, v_mla_kv_norm, v_mla_w_uq, v_mla_w_ukv, v_mla_w_o, v_sgu_w_in, v_sgu_ln_g, v_sgu_ln_b, v_sgu_w_spatial, v_sgu_b_spatial, v_sgu_w_out, v_ffn_w_up, v_ffn_w_down):
    _, T, D = x.shape
    depth = norm_mix.shape[0]
    n_mla, n_sgu = mla_w_dkv.shape[0], sgu_w_in.shape[0]
    assert depth % 2 == 0
    FF = ffn_w_up.shape[2] * N_DEV
    E = sgu_w_out.shape[1] * N_DEV
    ffc, ec, e2c = FF // N_DEV, E // N_DEV, 2 * E // N_DEV
    dc = D // N_DEV
    OW = HEADS * VDIM
    HW = HEADS * QPAD
    owc = OW // N_DEV
    tm = _tile(T, 1024)
    tq = _tile(T, 512)
    ts = _tile(T, 256)
    nt = T // tm
    x2 = x.reshape(T, D)
    tgt = loss_target.reshape(T, D)
    cidx = lax.axis_index("c").astype(jnp.int32).reshape(1)

    ln_local = jnp.concatenate([sgu_ln_g, sgu_ln_b, jnp.zeros((8 - 2 * n_sgu, ec), F32)], axis=0)
    g_dkv, g_uq, g_ukv, g_o, g_ln = _all_gather(
        "gather_small_weights", [w.astype(BF16) for w in (mla_w_dkv, mla_w_uq, mla_w_ukv, mla_w_o)] + [ln_local])
    w_dkv = jnp.pad(g_dkv.transpose(1, 0, 2, 3).reshape(n_mla, D, LAT), ((0, 0), (0, 0), (0, LAT_PAD - LAT)))
    w_uq = jnp.pad(g_uq, ((0, 0), (0, 0), (0, 0), (0, QPAD - NOPE - ROPE))).transpose(1, 2, 0, 3).reshape(n_mla, Q_RANK, HEADS * QPAD)
    w_ukv = g_ukv.transpose(1, 2, 0, 3).reshape(n_mla, KV_RANK, HEADS * (NOPE + VDIM))
    w_o = g_o.transpose(1, 0, 2, 3).reshape(n_mla, OW, D)
    ln_g_full = [g_ln[:, l, :].reshape(1, E) for l in range(n_sgu)]
    ln_b_full = [g_ln[:, n_sgu + l, :].reshape(1, E) for l in range(n_sgu)]
    b_sp = sgu_b_spatial.reshape(n_sgu, SGU_GROUPS, SGU_CHUNK, 1)
    up_sh = [ffn_w_up[i].astype(BF16) for i in range(depth)]
    down_sh = [ffn_w_down[i].astype(BF16) for i in range(depth)]
    in_sh = [sgu_w_in[l].astype(BF16) for l in range(n_sgu)]
    out_sh = [sgu_w_out[l].astype(BF16) for l in range(n_sgu)]
    g_up, g_down, g_in, g_out = [None] * depth, [None] * depth, [None] * n_sgu, [None] * n_sgu

    inv_freq = ROPE_THETA ** (-jnp.arange(0, ROPE, 2, dtype=F32) / ROPE)
    zeros32 = jnp.zeros((ROPE // 2,), F32)
    inv128 = jnp.concatenate([inv_freq, inv_freq, zeros32, zeros32]).reshape(1, 128)
    sel_a = jnp.concatenate([-jnp.ones((32,), F32), zeros32, zeros32, zeros32]).reshape(1, 128)
    sel_b = jnp.concatenate([zeros32, jnp.ones((32,), F32), zeros32, zeros32]).reshape(1, 128)
    sel_c = jnp.concatenate([jnp.ones((64,), F32), zeros32, zeros32]).reshape(1, 128)

    def rope_tables(pos, inv, sa, sb, sc):
        ang = pos.astype(F32) * inv
        cs, sn = jnp.cos(ang), jnp.sin(ang)
        return cs * sc, sn * sa, sn * sb

    t_cc, t_sa, t_sb = _rowwise(
        "rope_tables", rope_tables, [positions.reshape(T, 1), inv128, sel_a, sel_b, sel_c], grid=(nt,),
        in_specs=[_row_spec(tm, 1)] + [_const_spec((1, 128))] * 4,
        out_shapes=[_sds((T, 128), F32)] * 3, out_specs=[_row_spec(tm, 128)] * 3)
    tab_specs = [_row_spec(tm, 128)] * 3

    def rmsnorm(xv, g):
        return _rowwise("rmsnorm", lambda a, gg: _rms_fwd(a, gg), [xv, g.reshape(1, D)], grid=(nt,),
                        in_specs=[_row_spec(tm, D), _const_spec((1, D))], out_shapes=_sds((T, D), BF16), out_specs=_row_spec(tm, D))

    def proj_cols(name, h, gw, nc, epilogue, n_out):
        return _matmul(name, h, gw, [], grid=(N_DEV, nt),
                       a_spec=pl.BlockSpec((tm, D), lambda j, i: (i, 0)),
                       b_spec=pl.BlockSpec((None, D, nc), lambda j, i: (j, 0, 0)), extra_specs=[],
                       out_shapes=[_sds((T, nc * N_DEV), BF16)] * n_out, out_specs=[pl.BlockSpec((tm, nc), lambda j, i: (i, j))] * n_out,
                       dims=NN, epilogue=epilogue)

    def residual_norm(acc, xr, g):
        xn = acc + xr
        return xn, _rms_fwd(xn, g)

    def proj_rows_residual(name, a, gw, kc, xres, g_next):
        return _matmul(name, a, gw, [xres, g_next.reshape(1, D)], grid=(nt, N_DEV),
                       a_spec=pl.BlockSpec((tm, kc), lambda i, k: (i, k)),
                       b_spec=pl.BlockSpec((None, kc, D), lambda i, k: (k, 0, 0)),
                       extra_specs=[pl.BlockSpec((tm, D), lambda i, k: (i, 0)), _const_spec((1, D))],
                       out_shapes=[_sds((T, D), F32), _sds((T, D), BF16)], out_specs=[pl.BlockSpec((tm, D), lambda i, k: (i, 0))] * 2,
                       dims=NN, k_axis=1, nk=N_DEV, acc_shape=(tm, D), epilogue=residual_norm)

    def back_rows(name, dy, gw, kc, extras, epilogue):
        return _matmul(name, dy, gw, extras, grid=(N_DEV, nt),
                       a_spec=pl.BlockSpec((tm, D), lambda j, i: (i, 0)),
                       b_spec=pl.BlockSpec((None, kc, D), lambda j, i: (j, 0, 0)),
                       extra_specs=[pl.BlockSpec((tm, kc), lambda j, i: (i, j))] * len(extras),
                       out_shapes=[_sds((T, kc * N_DEV), BF16)], out_specs=[pl.BlockSpec((tm, kc), lambda j, i: (i, j))],
                       dims=NT, epilogue=epilogue)[0]

    def norm_bwd_epilogue(dh, xv, g, dxi):
        dxn, dg = _rms_bwd(xv, g, dh)
        return dxi + dxn, dxi + dxn, dg

    def back_cols(name, da, gw, nc, xv, g, dx_in):
        row = pl.BlockSpec((tm, D), lambda i, k: (i, 0))
        return _matmul(name, da, gw, [xv, g.reshape(1, D), dx_in], grid=(nt, N_DEV),
                       a_spec=pl.BlockSpec((tm, nc), lambda i, k: (i, k)),
                       b_spec=pl.BlockSpec((None, D, nc), lambda i, k: (k, 0, 0)), extra_specs=[row, _const_spec((1, D)), row],
                       out_shapes=[_sds((T, D), F32), _sds((T, D), BF16), _sds((1, D), F32)], out_specs=[row, row, _const_spec((1, D))],
                       dims=NT, k_axis=1, nk=N_DEV, acc_shape=(tm, D), epilogue=norm_bwd_epilogue, n_sum=1)

    def wgrad_cols(name, h, da, nc):
        return _matmul(name, h, da, [], grid=(N_DEV, nt),
                       a_spec=pl.BlockSpec((tm, D), lambda j, t: (t, 0)), b_spec=pl.BlockSpec((tm, nc), lambda j, t: (t, j)),
                       extra_specs=[], out_shapes=[_sds((N_DEV, D, nc), BF16)],
                       out_specs=[pl.BlockSpec((None, D, nc), lambda j, t: (j, 0, 0))],
                       dims=TN, k_axis=1, nk=nt, acc_shape=(D, nc))[0]

    def wgrad_rows(name, a, dy, kc, ncols):
        return _matmul(name, a, dy, [], grid=(a.shape[1] // kc, nt),
                       a_spec=pl.BlockSpec((tm, kc), lambda j, t: (t, j)), b_spec=pl.BlockSpec((tm, ncols), lambda j, t: (t, 0)),
                       extra_specs=[], out_shapes=[_sds((a.shape[1], ncols), BF16)],
                       out_specs=[pl.BlockSpec((kc, ncols), lambda j, t: (j, 0))],
                       dims=TN, k_axis=1, nk=nt, acc_shape=(kc, ncols))[0]

    saved = []
    xs = x2
    for i in range(depth):
        l = i // 2
        if i == 0:
            h = rmsnorm(xs, norm_mix[0])
        if i % 2 == 0:
            lat = _matmul("mla_down", h, w_dkv, [], grid=(nt,), a_spec=_row_spec(tm, D),
                          b_spec=pl.BlockSpec((None, D, LAT_PAD), lambda i_: (l, 0, 0)), extra_specs=[],
                          out_shapes=[_sds((T, LAT_PAD), F32)], out_specs=[_row_spec(tm, LAT_PAD)], dims=NN)[0]

            def latent_post(la, qn, kvn, cc, sa, sb):
                cq = _rms_fwd(la[:, :Q_RANK], qn)
                ckv = _rms_fwd(la[:, Q_RANK:Q_RANK + KV_RANK], kvn)
                kr = _rope_fwd(la[:, Q_RANK + KV_RANK:], cc, sa, sb)
                return cq, ckv, kr

            cq, ckv, kr = _rowwise(
                "mla_latent", latent_post, [lat, mla_q_norm[l].reshape(1, Q_RANK), mla_kv_norm[l].reshape(1, KV_RANK), t_cc, t_sa, t_sb],
                grid=(nt,), in_specs=[_row_spec(tm, LAT_PAD), _const_spec((1, Q_RANK)), _const_spec((1, KV_RANK))] + tab_specs,
                out_shapes=[_sds((T, Q_RANK), BF16), _sds((T, KV_RANK), BF16), _sds((T, 128), BF16)],
                out_specs=[_row_spec(tm, Q_RANK), _row_spec(tm, KV_RANK), _row_spec(tm, 128)])

            def q_epilogue(acc, cc, sa, sb):
                parts = []
                for b in range(HEADS):
                    parts += [acc[:, b * QPAD:b * QPAD + NOPE], _rope_fwd(acc[:, b * QPAD + NOPE:(b + 1) * QPAD], cc, sa, sb)]
                return (jnp.concatenate(parts, axis=1),)

            q = _matmul("mla_q", cq, w_uq, [t_cc, t_sa, t_sb], grid=(nt,), a_spec=_row_spec(tm, Q_RANK),
                        b_spec=pl.BlockSpec((None, Q_RANK, HW), lambda i_: (l, 0, 0)), extra_specs=tab_specs,
                        out_shapes=[_sds((T, HW), BF16)], out_specs=[_row_spec(tm, HW)], dims=NN, epilogue=q_epilogue)[0]

            def kv_write(outs, acc, krb):
                k_ref, v_ref, vt_ref = outs
                for b in range(HEADS):
                    vb = acc[:, b * QPAD + NOPE:(b + 1) * QPAD]
                    k_ref[:, b * QPAD:b * QPAD + NOPE] = acc[:, b * QPAD:b * QPAD + NOPE].astype(BF16)
                    k_ref[:, b * QPAD + NOPE:(b + 1) * QPAD] = krb
                    v_ref[:, b * VDIM:(b + 1) * VDIM] = vb.astype(BF16)
                    vbt = vb.T.astype(BF16)
                    for u in range(tm // tq):
                        vt_ref[b, u] = vbt[:, u * tq:(u + 1) * tq]

            kk, vv, vt = _matmul("mla_kv", ckv, w_ukv, [kr], grid=(nt,), a_spec=_row_spec(tm, KV_RANK),
                                 b_spec=pl.BlockSpec((None, KV_RANK, HW), lambda i_: (l, 0, 0)), extra_specs=[_row_spec(tm, 128)],
                                 out_shapes=[_sds((T, HW), BF16), _sds((T, OW), BF16), _sds((HEADS, T // tq, VDIM, tq), BF16)],
                                 out_specs=[_row_spec(tm, HW), _row_spec(tm, OW), pl.BlockSpec((HEADS, tm // tq, VDIM, tq), lambda i_: (0, i_, 0, 0))],
                                 dims=NN, write=kv_write)
            group = [up_sh[i], down_sh[i], in_sh[l], out_sh[l], up_sh[i + 1], down_sh[i + 1]]
            o, lse, *bufs = _flash_fwd(q, kk, vt, tq, comm=_gather_level1(group))
            xm, h2, *bufs = _matmul("mla_out", o, w_o, [xs, norm_ffn[i].reshape(1, D)], grid=(nt,), a_spec=_row_spec(tm, OW),
                                    b_spec=pl.BlockSpec((None, OW, D), lambda i_: (l, 0, 0)), extra_specs=[_row_spec(tm, D), _const_spec((1, D))],
                                    out_shapes=[_sds((T, D), F32), _sds((T, D), BF16)], out_specs=[_row_spec(tm, D)] * 2, dims=NN,
                                    epilogue=residual_norm, comm=_gather_level2(bufs))
            g_up[i], g_down[i], g_in[l], g_out[l], g_up[i + 1], g_down[i + 1] = bufs
            mix_saved = (h, lat, cq, ckv, q, kk, vv, o, lse)
        else:
            gp, ge = proj_cols("sgu_in", h, g_in[l], e2c, _gelu_and_grad, 2)
            gate = _sgu_mid_fwd(ge, ln_g_full[l], ln_b_full[l], sgu_w_spatial[l], b_sp[l], 4)
            xm, h2 = proj_rows_residual("sgu_out", gate, g_out[l], ec, xs, norm_ffn[i])
            mix_saved = (h, gp, ge, gate)
        r, s = proj_cols("ffn_up", h2, g_up[i], ffc, lambda acc: (jnp.maximum(acc, 0.0), jnp.square(jnp.maximum(acc, 0.0))), 2)
        xo, h_next = proj_rows_residual("ffn_down", s, g_down[i], ffc, xm, norm_mix[i + 1] if i + 1 < depth else final_norm)
        saved.append((xs, xm, mix_saved, h2, r, s))
        xs, h = xo, h_next

    def loss_head(xv, tg, g):
        y = _rms_fwd(xv, g)
        err = y - tg
        part = 0.5 * jnp.sum(jnp.sum(err * err, axis=-1, keepdims=True), axis=0, keepdims=True) / D
        dx, dg = _rms_bwd(xv, g, err / D)
        return dx, dx, jnp.broadcast_to(part, (1, 128)), dg

    dx, dyb, loss_part, d_final = _rowwise(
        "loss_head", loss_head, [xs, tgt, final_norm.reshape(1, D)], grid=(nt,),
        in_specs=[_row_spec(tm, D), _row_spec(tm, D), _const_spec((1, D))],
        out_shapes=[_sds((T, D), F32), _sds((T, D), BF16), _sds((1, 128), F32), _sds((1, D), F32)],
        out_specs=[_row_spec(tm, D), _row_spec(tm, D), _const_spec((1, 128)), _const_spec((1, D))], n_acc=2)
    loss = lax.psum(loss_part[0, 0], ("x", "y", "c"))

    d_norm_mix, d_norm_ffn = [None] * depth, [None] * depth
    d_qn, d_kvn = [None] * n_mla, [None] * n_mla
    d_wsp, d_bsp, d_lng, d_lnb = [None] * n_sgu, [None] * n_sgu, [None] * n_sgu, [None] * n_sgu
    layers = {"dkv": n_mla, "uq": n_mla, "ukv": n_mla, "o": n_mla, "in": n_sgu, "out": n_sgu, "up": depth, "down": depth}
    stacked = {nm: None for nm in layers}
    pending = []

    def add_pair(g, rcv):
        _, rws, cls = g.shape
        g4 = g.reshape(N_CHIP, 2, rws, cls)
        rt = _tile(rws, 512)
        return _rowwise("grad_pair_sum", lambda a, b_: a.astype(F32) + b_.astype(F32), [g4, rcv], grid=(N_CHIP, rws // rt),
                        in_specs=[pl.BlockSpec((None, None, rt, cls), lambda ch, i_, cr: (ch, cr[0], i_, 0)),
                                  pl.BlockSpec((None, rt, cls), lambda ch, i_, cr: (ch, i_, 0))],
                        out_shapes=_sds(rcv.shape, BF16), out_specs=pl.BlockSpec((None, rt, cls), lambda ch, i_, cr: (ch, i_, 0)),
                        grid_spec_prefetch=cidx)

    def chip_comm_of_pending():
        grads = [g for _, _, g in pending]
        from_sibling = _comm_call("grad_sibling_exchange", _sibling_exchange(grads))
        parts = [add_pair(g, rcv) for g, rcv in zip(grads, from_sibling)]
        comm, names = _chip_exchange(parts, [(nm, l_) for nm, l_, _ in pending], layers, stacked)
        pending.clear()
        return comm, names

    for i in reversed(range(depth)):
        l = i // 2
        xs_i, xm, mix_saved, h2, r, s = saved[i]
        da = back_rows("ffn_down_bwd", dyb, g_down[i], ffc, [r], lambda acc, rr: (acc * (2.0 * rr.astype(F32)),))
        pending.append(("down", i, wgrad_rows("ffn_down_wgrad", s, dyb, ffc, D).reshape(N_DEV, ffc, D)))
        pending.append(("up", i, wgrad_cols("ffn_up_wgrad", h2, da, ffc)))
        dx, dyb, d_norm_ffn[i] = back_cols("ffn_up_bwd", da, g_up[i], ffc, xm, norm_ffn[i], dx)
        if i % 2 == 0:
            h, lat, cq, ckv, q, kk, vv, o, lse = mix_saved
            do = _matmul("mla_out_bwd", dyb, w_o, [], grid=(nt,), a_spec=_row_spec(tm, D),
                         b_spec=pl.BlockSpec((None, OW, D), lambda i_: (l, 0, 0)), extra_specs=[],
                         out_shapes=[_sds((T, OW), BF16)], out_specs=[_row_spec(tm, OW)], dims=NT)[0]
            g_o_l = wgrad_rows("mla_out_wgrad", o, dyb, OW, D).reshape(N_DEV, owc, D)
            comm, names = chip_comm_of_pending()
            dq_pre, dk, dv, *bufs = _flash_bwd(q, kk, vv, o, do, lse, (t_cc, t_sa, t_sb), tq, comm=comm)
            stacked.update(dict(zip(names, bufs)))
            pending.append(("o", l, g_o_l))

            def kv_pre(dkb, dvb, cc, sa, sb):
                parts, dkr = [], None
                for b in range(HEADS):
                    parts += [dkb[:, b * QPAD:b * QPAD + NOPE], dvb[:, b * VDIM:(b + 1) * VDIM]]
                    piece = dkb[:, b * QPAD + NOPE:(b + 1) * QPAD]
                    dkr = piece if dkr is None else dkr + piece
                return jnp.concatenate(parts, axis=1), _rope_bwd(dkr, cc, sa, sb)

            dkv, dkr = _rowwise("mla_dkv_rope", kv_pre, [dk, dv, t_cc, t_sa, t_sb], grid=(T // ts,),
                                in_specs=[_row_spec(ts, HW), _row_spec(ts, OW)] + [_row_spec(ts, 128)] * 3,
                                out_shapes=[_sds((T, HW), BF16), _sds((T, 128), F32)], out_specs=[_row_spec(ts, HW), _row_spec(ts, 128)])
            g_uq_l = wgrad_rows("mla_q_wgrad", cq, dq_pre, Q_RANK, HW)
            g_ukv_l = wgrad_rows("mla_kv_wgrad", ckv, dkv, KV_RANK, HW)
            pending.append(("uq", l, g_uq_l.reshape(Q_RANK, HEADS, QPAD)[:, :, :NOPE + ROPE].transpose(1, 0, 2)))
            pending.append(("ukv", l, g_ukv_l.reshape(KV_RANK, HEADS, NOPE + VDIM).transpose(1, 0, 2)))
            dcq = _matmul("mla_q_bwd", dq_pre, w_uq, [], grid=(nt,), a_spec=_row_spec(tm, HW),
                          b_spec=pl.BlockSpec((None, Q_RANK, HW), lambda i_: (l, 0, 0)), extra_specs=[],
                          out_shapes=[_sds((T, Q_RANK), F32)], out_specs=[_row_spec(tm, Q_RANK)], dims=NT)[0]
            dckv = _matmul("mla_kv_bwd", dkv, w_ukv, [], grid=(nt,), a_spec=_row_spec(tm, HW),
                           b_spec=pl.BlockSpec((None, KV_RANK, HW), lambda i_: (l, 0, 0)), extra_specs=[],
                           out_shapes=[_sds((T, KV_RANK), F32)], out_specs=[_row_spec(tm, KV_RANK)], dims=NT)[0]

            def latent_bwd(la, qn, kvn, dq_, dkv_, dkr_):
                dcq_raw, dqn = _rms_bwd(la[:, :Q_RANK], qn, dq_)
                dckv_raw, dkvn = _rms_bwd(la[:, Q_RANK:Q_RANK + KV_RANK], kvn, dkv_)
                return jnp.concatenate([dcq_raw, dckv_raw, dkr_], axis=1), dqn, dkvn

            dlat, d_qn[l], d_kvn[l] = _rowwise(
                "mla_latent_bwd", latent_bwd, [lat, mla_q_norm[l].reshape(1, Q_RANK), mla_kv_norm[l].reshape(1, KV_RANK), dcq, dckv, dkr],
                grid=(nt,), in_specs=[_row_spec(tm, LAT_PAD), _const_spec((1, Q_RANK)), _const_spec((1, KV_RANK)),
                                      _row_spec(tm, Q_RANK), _row_spec(tm, KV_RANK), _row_spec(tm, 128)],
                out_shapes=[_sds((T, LAT_PAD), BF16), _sds((1, Q_RANK), F32), _sds((1, KV_RANK), F32)],
                out_specs=[_row_spec(tm, LAT_PAD), _const_spec((1, Q_RANK)), _const_spec((1, KV_RANK))], n_acc=2)
            g_dkv_l = wgrad_rows("mla_down_wgrad", h, dlat, D, LAT_PAD)
            pending.append(("dkv", l, g_dkv_l[:, :LAT].reshape(N_DEV, dc, LAT)))
            dx, dyb, d_norm_mix[i] = _matmul(
                "mla_down_bwd", dlat, w_dkv, [xs_i, norm_mix[i].reshape(1, D), dx], grid=(nt,), a_spec=_row_spec(tm, LAT_PAD),
                b_spec=pl.BlockSpec((None, D, LAT_PAD), lambda i_: (l, 0, 0)), extra_specs=[_row_spec(tm, D), _const_spec((1, D)), _row_spec(tm, D)],
                out_shapes=[_sds((T, D), F32), _sds((T, D), BF16), _sds((1, D), F32)],
                out_specs=[_row_spec(tm, D), _row_spec(tm, D), _const_spec((1, D))], dims=NT, epilogue=norm_bwd_epilogue, n_sum=1)
        else:
            h, gp, ge, gate = mix_saved
            dgate = back_rows("sgu_out_bwd", dyb, g_out[l], ec, [], None)
            pending.append(("out", l, wgrad_rows("sgu_out_wgrad", gate, dyb, ec, D).reshape(N_DEV, ec, D)))
            dz, d_wsp[l], d_bsp[l], d_lng[l], d_lnb[l] = _sgu_mid_bwd(ge, gp, dgate, ln_g_full[l], ln_b_full[l], sgu_w_spatial[l], b_sp[l], 2)
            pending.append(("in", l, wgrad_cols("sgu_in_wgrad", h, dz, e2c)))
            dx, dyb, d_norm_mix[i] = back_cols("sgu_in_bwd", dz, g_in[l], e2c, xs_i, norm_mix[i], dx)
    grad_x = dx.reshape(1, T, D)

    comm, names = chip_comm_of_pending()
    stacked.update(dict(zip(names, _comm_call("grad_chip_exchange", comm))))

    def adam_big(name, parts, w, m, v):
        lyr, rws, cls = w.shape
        rt = _tile(rws, 256)

        def fn(p, w_, m_, v_):
            g = (p[0].astype(F32) + p[1].astype(F32)) + (p[2].astype(F32) + p[3].astype(F32))
            return (g, *_adam(w_, g, m_, v_))

        spec = pl.BlockSpec((None, rt, cls), lambda l_, i_: (l_, i_, 0))
        return _rowwise(name, fn, [parts, w, m, v], grid=(lyr, rws // rt),
                        in_specs=[pl.BlockSpec((N_CHIP, None, rt, cls), lambda l_, i_: (0, l_, i_, 0)), spec, spec, spec],
                        out_shapes=[_sds(w.shape, F32)] * 4, out_specs=[spec] * 4)

    big_names = ["dkv", "uq", "ukv", "o", "in", "out", "up", "down"]
    big_w = [mla_w_dkv, mla_w_uq, mla_w_ukv, mla_w_o, sgu_w_in, sgu_w_out, ffn_w_up, ffn_w_down]
    big_m = [m_mla_w_dkv, m_mla_w_uq, m_mla_w_ukv, m_mla_w_o, m_sgu_w_in, m_sgu_w_out, m_ffn_w_up, m_ffn_w_down]
    big_v = [v_mla_w_dkv, v_mla_w_uq, v_mla_w_ukv, v_mla_w_o, v_sgu_w_in, v_sgu_w_out, v_ffn_w_up, v_ffn_w_down]
    big_res = [adam_big("adam_large", stacked[nm], w, m, v) for nm, w, m, v in zip(big_names, big_w, big_m, big_v)]

    def rows128(a, rows):
        flat = a.reshape(-1, 128)
        return jnp.pad(flat, ((0, rows - flat.shape[0]), (0, 0)))

    def pad8(n):
        return -(-n // 8) * 8

    small_names = ["norm_mix", "norm_ffn", "final_norm", "q_norm", "kv_norm", "w_spatial", "b_spatial"]
    small_g = [jnp.concatenate(d_norm_mix, 0), jnp.concatenate(d_norm_ffn, 0), d_final, jnp.concatenate(d_qn, 0), jnp.concatenate(d_kvn, 0),
               jnp.stack(d_wsp, 0), jnp.stack(d_bsp, 0), jnp.concatenate(d_lng, 0), jnp.concatenate(d_lnb, 0)]
    small_w = [norm_mix, norm_ffn, final_norm, mla_q_norm, mla_kv_norm, sgu_w_spatial, sgu_b_spatial]
    small_m = [m_norm_mix, m_norm_ffn, m_final_norm, m_mla_q_norm, m_mla_kv_norm, m_sgu_w_spatial, m_sgu_b_spatial]
    small_v = [v_norm_mix, v_norm_ffn, v_final_norm, v_mla_q_norm, v_mla_kv_norm, v_sgu_w_spatial, v_sgu_b_spatial]
    sizes = [pad8(g.size // 128) for g in small_g]
    n_rep = len(small_w)
    sizes[n_rep - 1] += -sum(sizes[:n_rep]) % SMALL_ROWS
    offs = [sum(sizes[:k]) for k in range(len(sizes) + 1)]
    rep_rows = offs[n_rep]
    pack_g = jnp.concatenate([rows128(g, sz) for g, sz in zip(small_g, sizes)], axis=0)
    (gathered_small,) = _all_gather("gather_small_grads", [pack_g])

    def pack(arrs):
        return jnp.concatenate([rows128(a, sz) for a, sz in zip(arrs, sizes[:n_rep])], axis=0)

    def sum8(p):
        return ((p[0] + p[1]) + (p[2] + p[3])) + ((p[4] + p[5]) + (p[6] + p[7]))

    sspec = _row_spec(SMALL_ROWS, 128)
    rep_g, rep_d, rep_m, rep_v = _rowwise(
        "adam_small", lambda p, w_, m_, v_: (sum8(p), *_adam(w_, sum8(p), m_, v_)),
        [gathered_small, pack(small_w), pack(small_m), pack(small_v)], grid=(rep_rows // SMALL_ROWS,),
        in_specs=[pl.BlockSpec((N_DEV, SMALL_ROWS, 128), lambda i_: (0, i_, 0)), sspec, sspec, sspec],
        out_shapes=[_sds((rep_rows, 128), F32)] * 4, out_specs=[sspec] * 4)

    def unpack(packed, k, like):
        return packed[offs[k]:offs[k] + like.size // 128].reshape(like.shape)

    my_b = 4 * lax.axis_index("x") + 2 * lax.axis_index("y") + lax.axis_index("c")
    ln_w = jnp.concatenate([sgu_ln_g, sgu_ln_b], 0)
    ln_m = jnp.concatenate([m_sgu_ln_g, m_sgu_ln_b], 0)
    ln_v = jnp.concatenate([v_sgu_ln_g, v_sgu_ln_b], 0)
    ln_all = gathered_small[:, rep_rows:, :]
    ln_mine = lax.dynamic_slice_in_dim(ln_all[:, :2 * n_sgu * E // 128].reshape(N_DEV, 2 * n_sgu, N_DEV, ec), my_b, 1, axis=2).reshape(N_DEV, 2 * n_sgu, ec)
    ln_g_, ln_d, ln_m2, ln_v2 = _rowwise(
        "adam_ln", lambda p, w_, m_, v_: (sum8(p), *_adam(w_, sum8(p), m_, v_)), [ln_mine, ln_w, ln_m, ln_v], grid=(1,),
        in_specs=[_const_spec(ln_mine.shape), _const_spec(ln_w.shape), _const_spec(ln_w.shape), _const_spec(ln_w.shape)],
        out_shapes=[_sds(ln_w.shape, F32)] * 4, out_specs=[_const_spec(ln_w.shape)] * 4)

    def family(pos):
        rep = [rep_g, rep_d, rep_m, rep_v][pos]
        ln = [ln_g_, ln_d, ln_m2, ln_v2][pos]
        small = {nm: unpack(rep, k, w_) for k, (nm, w_) in enumerate(zip(small_names, small_w))}
        big = [res[pos] for res in big_res]
        return [small["norm_mix"], small["norm_ffn"], small["final_norm"], big[0], small["q_norm"], small["kv_norm"], big[1], big[2], big[3],
                big[4], ln[:n_sgu], ln[n_sgu:], small["w_spatial"], small["b_spatial"], big[5], big[6], big[7]]

    return (loss, grad_x, *family(0), *family(1), *family(2), *family(3))
```

```python
import math

import jax
import jax.numpy as jnp
from jax import lax
from jax.experimental import pallas as pl
from jax.experimental.pallas import tpu as pltpu

F32 = jnp.float32
BF16 = jnp.bfloat16
MESH = pl.DeviceIdType.MESH

N_DEV = 8
N_CHIP = 4
HEADS = 8
NOPE = 128
ROPE = 64
VDIM = 128
QPAD = 256
Q_RANK = 256
KV_RANK = 128
LAT = Q_RANK + KV_RANK + ROPE
LAT_PAD = 512
ROPE_THETA = 10000.0
SGU_CHUNK = 128
SGU_GROUPS = 8
NORM_EPS = 1e-6
LN_EPS = 1e-5
ADAM_LR = 0.001
ADAM_B1 = 0.9
ADAM_B2 = 0.999
ADAM_EPS = 1e-08
ADAM_WD = 0.01
ADAM_STEP = 10
ATTN_SCALE = (NOPE + ROPE) ** -0.5
NEG = -1e30
EXP2_SCALE = ATTN_SCALE * math.log2(math.e)
VMEM_LIMIT = 56 * 1024 * 1024
SMALL_ROWS = 256

NN = (((1,), (0,)), ((), ()))
NT = (((1,), (1,)), ((), ()))
TN = (((0,), (0,)), ((), ()))
ANY = pl.BlockSpec(memory_space=pl.ANY)


def _pcall(body, **kw):
    return pl.pallas_call(body, **kw)


def _params(n_grid, side_effects=False):
    return pltpu.CompilerParams(dimension_semantics=("arbitrary",) * n_grid, vmem_limit_bytes=VMEM_LIMIT, has_side_effects=side_effects)


def _sds(shape, dtype):
    return jax.ShapeDtypeStruct(tuple(shape), dtype)


def _tile(n, want):
    t = min(n, want)
    assert n % t == 0, (n, want)
    return t


class _Comm:
    def __init__(self, operands, out_shapes, aliases, scratch, start, finish):
        self.operands, self.out_shapes, self.aliases, self.scratch = operands, out_shapes, aliases, scratch
        self.start, self.finish = start, finish


def _place():
    return lax.axis_index("x"), lax.axis_index("y"), lax.axis_index("c")


def _other_chips(x, y):
    return [(1 - x, y), (x, 1 - y), (1 - x, 1 - y)]


def _dev_index(dev):
    return 4 * dev[0] + 2 * dev[1] + dev[2]


def _comm_call(name, comm):
    c_in, c_out = len(comm.operands), len(comm.out_shapes)

    def body(*refs):
        ins, outs, sems = refs[:c_in], refs[c_in:c_in + c_out], refs[c_in + c_out:]
        comm.start(ins, outs, sems)
        comm.finish(ins, outs, sems)

    return _pcall(body, name=name, in_specs=[ANY] * c_in, out_specs=[ANY] * c_out, out_shape=comm.out_shapes,
                  scratch_shapes=comm.scratch, input_output_aliases=dict(comm.aliases),
                  compiler_params=pltpu.CompilerParams(has_side_effects=True))(*comm.operands)


def _call(name, body, operands, in_specs, out_shapes, out_specs, scratch, grid, comm=None):
    if comm is None:
        return _pcall(body, name=name, grid=grid, in_specs=in_specs, out_specs=out_specs, out_shape=out_shapes,
                      scratch_shapes=scratch, compiler_params=_params(len(grid)))(*operands)
    n_in, n_out, n_sc = len(operands), len(out_shapes), len(scratch)
    c_in, c_out = len(comm.operands), len(comm.out_shapes)

    def hosted(*refs):
        ins, cins = refs[:n_in], refs[n_in:n_in + c_in]
        o0 = n_in + c_in
        outs, couts = refs[o0:o0 + n_out], refs[o0 + n_out:o0 + n_out + c_out]
        rest = refs[o0 + n_out + c_out:]
        sc, csems = rest[:n_sc], rest[n_sc:]
        first = pl.program_id(0) == 0
        last = pl.program_id(0) == grid[0] - 1
        for d in range(1, len(grid)):
            first = jnp.logical_and(first, pl.program_id(d) == 0)
            last = jnp.logical_and(last, pl.program_id(d) == grid[d] - 1)

        @pl.when(first)
        def _():
            comm.start(cins, couts, csems)

        body(*ins, *outs, *sc)

        @pl.when(last)
        def _():
            comm.finish(cins, couts, csems)

    return _pcall(hosted, name=name, grid=grid, in_specs=[*in_specs, *[ANY] * c_in], out_specs=[*out_specs, *[ANY] * c_out],
                  out_shape=[*out_shapes, *comm.out_shapes], scratch_shapes=[*scratch, *comm.scratch],
                  input_output_aliases={n_in + k: n_out + v for k, v in comm.aliases.items()},
                  compiler_params=_params(len(grid), side_effects=True))(*operands, *comm.operands)


def _gather_level1(shards):
    n = len(shards)

    def copies(ins, outs, sems):
        send_sems, recv_sems, local_sems = sems
        x, y, c = _place()
        me, sibling = (x, y, c), (x, y, 1 - c)
        chips = _other_chips(x, y)

        def copy(a, k, block, to, src=None):
            slot = outs[a].at[_dev_index(block)]
            return pltpu.make_async_remote_copy(src_ref=slot if src is None else src, dst_ref=slot, send_sem=send_sems.at[a, k],
                                                recv_sem=recv_sems.at[a, k], device_id=to, device_id_type=MESH)

        mine = [pltpu.make_async_copy(ins[a], outs[a].at[_dev_index(me)], local_sems.at[a]) for a in range(n)]
        sends = [copy(a, 1 + j, me, (*chip, c), src=ins[a]) for j, chip in enumerate(chips) for a in range(n)]
        sends += [copy(a, 0, me, sibling, src=ins[a]) for a in range(n)]
        recvs = [copy(a, 1 + j, (*chip, c), me) for j, chip in enumerate(chips) for a in range(n)]
        recvs += [copy(a, 0, sibling, me) for a in range(n)]
        return mine, sends, recvs

    def start(ins, outs, sems):
        mine, sends, _ = copies(ins, outs, sems)
        for cp in mine + sends:
            cp.start()

    def finish(ins, outs, sems):
        mine, sends, recvs = copies(ins, outs, sems)
        for cp in recvs:
            cp.wait_recv()
        for cp in sends:
            cp.wait_send()
        for cp in mine:
            cp.wait()

    return _Comm(shards, [_sds((N_DEV, *a.shape), a.dtype) for a in shards], {},
                 [pltpu.SemaphoreType.DMA((n, 4)), pltpu.SemaphoreType.DMA((n, 4)), pltpu.SemaphoreType.DMA((n,))], start, finish)


def _gather_level2(bufs):
    n = len(bufs)

    def copies(outs, sems):
        send_sems, recv_sems = sems
        x, y, c = _place()
        sibling = (x, y, 1 - c)
        sends, recvs = [], []
        for j, chip in enumerate(_other_chips(x, y)):
            for a in range(n):
                have, want = outs[a].at[_dev_index((*chip, c))], outs[a].at[_dev_index((*chip, 1 - c))]
                sends.append(pltpu.make_async_remote_copy(src_ref=have, dst_ref=have, send_sem=send_sems.at[a, j], recv_sem=recv_sems.at[a, j],
                                                          device_id=sibling, device_id_type=MESH))
                recvs.append(pltpu.make_async_remote_copy(src_ref=want, dst_ref=want, send_sem=send_sems.at[a, j], recv_sem=recv_sems.at[a, j],
                                                          device_id=sibling, device_id_type=MESH))
        return sends, recvs

    def start(ins, outs, sems):
        for cp in copies(outs, sems)[0]:
            cp.start()

    def finish(ins, outs, sems):
        sends, recvs = copies(outs, sems)
        for cp in recvs:
            cp.wait_recv()
        for cp in sends:
            cp.wait_send()

    return _Comm(bufs, [_sds(b.shape, b.dtype) for b in bufs], {a: a for a in range(n)},
                 [pltpu.SemaphoreType.DMA((n, 3)), pltpu.SemaphoreType.DMA((n, 3))], start, finish)


def _all_gather(name, arrays):
    n = len(arrays)

    def body(*refs):
        ins = refs[:n]
        outs = refs[n:2 * n]
        send_sems, recv_sems, local_sems = refs[2 * n:]
        x, y, c = _place()
        me, sibling = (x, y, c), (x, y, 1 - c)
        chips = _other_chips(x, y)

        def copy(a, k, block, to, src=None):
            slot = outs[a].at[_dev_index(block)]
            return pltpu.make_async_remote_copy(src_ref=slot if src is None else src, dst_ref=slot, send_sem=send_sems.at[a, k],
                                                recv_sem=recv_sems.at[a, k], device_id=to, device_id_type=MESH)

        mine = [pltpu.make_async_copy(ins[a], outs[a].at[_dev_index(me)], local_sems.at[a]) for a in range(n)]
        for cp in mine:
            cp.start()
        first = []
        for j, chip in enumerate(chips):
            first += [copy(a, 1 + j, me, (*chip, c), src=ins[a]) for a in range(n)]
        first += [copy(a, 0, me, sibling, src=ins[a]) for a in range(n)]
        for cp in first:
            cp.start()
        passed = []
        for j, chip in enumerate(chips):
            for a in range(n):
                copy(a, 1 + j, (*chip, c), me).wait_recv()
                fwd = copy(a, 4 + j, (*chip, c), sibling)
                fwd.start()
                passed.append(fwd)
        for a in range(n):
            copy(a, 0, sibling, me).wait_recv()
            for j, chip in enumerate(chips):
                copy(a, 4 + j, (*chip, 1 - c), me).wait_recv()
        for cp in first + passed:
            cp.wait_send()
        for cp in mine:
            cp.wait()

    return _pcall(
        body, name=name, in_specs=[ANY] * n, out_specs=[ANY] * n,
        out_shape=[_sds((N_DEV, *a.shape), a.dtype) for a in arrays],
        scratch_shapes=[pltpu.SemaphoreType.DMA((n, 7)), pltpu.SemaphoreType.DMA((n, 7)), pltpu.SemaphoreType.DMA((n,))],
        compiler_params=pltpu.CompilerParams(has_side_effects=True),
    )(*arrays)


def _sibling_exchange(grads):
    n = len(grads)

    def start(ins, outs, sems):
        send_sems, recv_sems = sems
        x, y, c = _place()
        for a in range(n):
            for ch in range(N_CHIP):
                pltpu.make_async_remote_copy(src_ref=ins[a].at[2 * ch + 1 - c], dst_ref=outs[a].at[ch], send_sem=send_sems.at[a],
                                             recv_sem=recv_sems.at[a], device_id=(x, y, 1 - c), device_id_type=MESH).start()

    def finish(ins, outs, sems):
        send_sems, recv_sems = sems
        x, y, c = _place()
        for a in range(n):
            pltpu.make_async_remote_copy(src_ref=outs[a], dst_ref=outs[a], send_sem=send_sems.at[a], recv_sem=recv_sems.at[a],
                                         device_id=(x, y, 1 - c), device_id_type=MESH).wait()

    return _Comm(grads, [_sds((N_CHIP, *g.shape[1:]), g.dtype) for g in grads], {},
                 [pltpu.SemaphoreType.DMA((n,)), pltpu.SemaphoreType.DMA((n,))], start, finish)


def _chip_exchange(parts, slots, layers, stacked):
    n = len(parts)
    names = []
    for nm, _ in slots:
        if nm not in names:
            names.append(nm)
    shapes = {nm: _sds((N_CHIP, layers[nm], *parts[a].shape[1:]), parts[a].dtype) for a, (nm, _) in enumerate(slots)}
    kept = [nm for nm in names if stacked.get(nm) is not None]
    aliases = {n + k: names.index(nm) for k, nm in enumerate(kept)}

    def copies(ins, outs, sems):
        send_sems, recv_sems, local_sems = sems
        x, y, c = _place()
        mine = 2 * x + y
        local, sends, recvs = [], [], []
        for a, (nm, l) in enumerate(slots):
            buf = outs[names.index(nm)]
            local.append(pltpu.make_async_copy(ins[a].at[mine], buf.at[mine, l], local_sems.at[a]))
            for j, chip in enumerate(_other_chips(x, y)):
                theirs = buf.at[2 * chip[0] + chip[1], l]
                sends.append(pltpu.make_async_remote_copy(src_ref=ins[a].at[2 * chip[0] + chip[1]], dst_ref=buf.at[mine, l], send_sem=send_sems.at[a, j],
                                                          recv_sem=recv_sems.at[a, j], device_id=(*chip, c), device_id_type=MESH))
                recvs.append(pltpu.make_async_remote_copy(src_ref=theirs, dst_ref=theirs, send_sem=send_sems.at[a, j],
                                                          recv_sem=recv_sems.at[a, j], device_id=(*chip, c), device_id_type=MESH))
        return local, sends, recvs

    def start(ins, outs, sems):
        local, sends, _ = copies(ins, outs, sems)
        for cp in local + sends:
            cp.start()

    def finish(ins, outs, sems):
        local, sends, recvs = copies(ins, outs, sems)
        for cp in recvs:
            cp.wait_recv()
        for cp in sends:
            cp.wait_send()
        for cp in local:
            cp.wait()

    comm = _Comm([*parts, *[stacked[nm] for nm in kept]], [shapes[nm] for nm in names], aliases,
                 [pltpu.SemaphoreType.DMA((n, 3)), pltpu.SemaphoreType.DMA((n, 3)), pltpu.SemaphoreType.DMA((n,))], start, finish)
    return comm, names


def _matmul(name, a, b, extras, *, grid, a_spec, b_spec, extra_specs, out_shapes, out_specs, dims, k_axis=None, nk=1,
            acc_shape=None, epilogue=None, comm=None, n_sum=0, write=None):
    n_extra = len(extras)
    n_out = len(out_shapes)

    def body(*refs):
        a_ref, b_ref = refs[0], refs[1]
        ex = refs[2:2 + n_extra]
        outs = refs[2 + n_extra:2 + n_extra + n_out]
        prod = lax.dot_general(a_ref[...], b_ref[...], dims, preferred_element_type=F32)

        def finish(acc):
            if write is not None:
                write(outs, acc, *[e[...] for e in ex])
                return
            res = epilogue(acc, *[e[...] for e in ex]) if epilogue is not None else (acc,)
            first = None
            for d in range(len(grid)):
                if d != k_axis:
                    here = pl.program_id(d) == 0
                    first = here if first is None else jnp.logical_and(first, here)
            for idx, (o, r) in enumerate(zip(outs, res)):
                if idx < n_out - n_sum:
                    o[...] = r.astype(o.dtype)
                else:
                    @pl.when(first)
                    def _(o=o, r=r):
                        o[...] = r.astype(o.dtype)

                    @pl.when(jnp.logical_not(first))
                    def _(o=o, r=r):
                        o[...] += r.astype(o.dtype)

        if k_axis is None:
            finish(prod)
        else:
            acc_ref = refs[-1]
            k = pl.program_id(k_axis)

            @pl.when(k == 0)
            def _():
                acc_ref[...] = prod

            @pl.when(k > 0)
            def _():
                acc_ref[...] += prod

            @pl.when(k == nk - 1)
            def _():
                finish(acc_ref[...])

    scratch = [] if k_axis is None else [pltpu.VMEM(acc_shape, F32)]
    return _call(name, body, [a, b, *extras], [a_spec, b_spec, *extra_specs], list(out_shapes), list(out_specs), scratch, grid, comm)


def _rowwise(name, fn, operands, *, grid, in_specs, out_shapes, out_specs, n_acc=0, grid_spec_prefetch=None):
    n_in = len(operands)
    n_out = len(out_shapes)
    n_pre = 0 if grid_spec_prefetch is None else 1

    def body(*refs):
        refs = refs[n_pre:]
        ins = refs[:n_in]
        outs = refs[n_in:n_in + n_out]
        res = fn(*[r[...] for r in ins])
        if not isinstance(res, (tuple, list)):
            res = (res,)
        first = pl.program_id(0) == 0
        for d in range(1, len(grid)):
            first = jnp.logical_and(first, pl.program_id(d) == 0)
        for idx, (o, r) in enumerate(zip(outs, res)):
            if idx < n_out - n_acc:
                o[...] = r.astype(o.dtype)
            else:
                @pl.when(first)
                def _(o=o, r=r):
                    o[...] = r.astype(o.dtype)

                @pl.when(jnp.logical_not(first))
                def _(o=o, r=r):
                    o[...] += r.astype(o.dtype)

    if grid_spec_prefetch is None:
        return _pcall(body, name=name, grid=grid, in_specs=in_specs, out_specs=out_specs, out_shape=out_shapes,
                      compiler_params=_params(len(grid)))(*operands)
    gs = pltpu.PrefetchScalarGridSpec(num_scalar_prefetch=1, grid=grid, in_specs=in_specs, out_specs=out_specs)
    return _pcall(body, name=name, grid_spec=gs, out_shape=out_shapes,
                  compiler_params=_params(len(grid)))(grid_spec_prefetch, *operands)


def _row_spec(tm, w):
    return pl.BlockSpec((tm, w), lambda i: (i, 0))


def _const_spec(shape):
    nd = len(shape)
    return pl.BlockSpec(tuple(shape), lambda *_: (0,) * nd)


def _rms_fwd(x, g):
    r = lax.rsqrt(jnp.mean(x * x, axis=-1, keepdims=True) + NORM_EPS)
    return x * r * g


def _rms_bwd(x, g, dy):
    r = lax.rsqrt(jnp.mean(x * x, axis=-1, keepdims=True) + NORM_EPS)
    xh = x * r
    u = dy * g
    dx = r * (u - xh * jnp.mean(u * xh, axis=-1, keepdims=True))
    dg = jnp.sum(dy * xh, axis=0, keepdims=True)
    return dx, dg


def _gelu_and_grad(z):
    cdf = 0.5 * (1.0 + lax.erf(z * (2.0 ** -0.5)))
    return cdf + z * jnp.exp(-0.5 * z * z) * ((2.0 * math.pi) ** -0.5), z * cdf


def _rope_fwd(x, cc, sa, sb):
    return x * cc + pltpu.roll(x, 96, 1) * sa + pltpu.roll(x, 32, 1) * sb


def _rope_bwd(d, cc, sa, sb):
    return d * cc + pltpu.roll(d * sa, 32, 1) + pltpu.roll(d * sb, 96, 1)


def _adam(w, g, m, v):
    m = ADAM_B1 * m + (1.0 - ADAM_B1) * g
    v = ADAM_B2 * v + (1.0 - ADAM_B2) * (g * g)
    m_hat = m / (1.0 - ADAM_B1 ** ADAM_STEP)
    v_hat = v / (1.0 - ADAM_B2 ** ADAM_STEP)
    delta = -ADAM_LR * (m_hat / (jnp.sqrt(v_hat) + ADAM_EPS) + ADAM_WD * w)
    return delta, m, v


def _flash_fwd(q, k, vt, tq, comm=None):
    h, t = vt.shape[0], q.shape[0]
    nq = t // tq

    def body(q_ref, k_ref, vt_ref, o_ref, lse_ref):
        qi = pl.program_id(1)

        def scores(kj):
            kb = k_ref[pl.ds(pl.multiple_of(kj * tq, tq), tq), :]
            return lax.dot_general(kb, q_ref[...], NT, preferred_element_type=F32)

        def step(kj, st, state, masked):
            m_old, l_old, acc_old = state
            if masked:
                key = lax.broadcasted_iota(jnp.int32, (tq, tq), 0)
                qry = lax.broadcasted_iota(jnp.int32, (tq, tq), 1)
                st = jnp.where(key <= qry, st, NEG)
            m_new = jnp.maximum(m_old, jnp.max(st, axis=0, keepdims=True))
            alpha = jnp.exp2((m_old - m_new) * EXP2_SCALE)
            pt = jnp.exp2((st - m_new) * EXP2_SCALE)
            l_new = alpha * l_old + jnp.sum(pt, axis=0, keepdims=True)
            acc_new = alpha * acc_old + lax.dot_general(vt_ref[kj], pt.astype(BF16), NN, preferred_element_type=F32)
            return m_new, l_new, acc_new

        init = (jnp.full((1, tq), NEG, F32), jnp.zeros((1, tq), F32), jnp.zeros((VDIM, tq), F32))
        state = lax.fori_loop(0, qi, lambda kj, st_: step(kj, scores(kj), st_, False), init)
        m, l, acc = step(qi, scores(qi), state, True)
        o_ref[...] = (acc / l).T.astype(o_ref.dtype)
        lse_ref[...] = m * EXP2_SCALE + jnp.log2(l)

    return _call(
        "flash_fwd", body, [q, k, vt],
        [pl.BlockSpec((tq, QPAD), lambda hh, i: (i, hh)),
         pl.BlockSpec((t, QPAD), lambda hh, i: (0, hh)),
         pl.BlockSpec((None, nq, VDIM, tq), lambda hh, i: (hh, 0, 0, 0))],
        [_sds((t, h * VDIM), BF16), _sds((h, nq, 1, tq), F32)],
        [pl.BlockSpec((tq, VDIM), lambda hh, i: (i, hh)),
         pl.BlockSpec((None, None, 1, tq), lambda hh, i: (hh, i, 0, 0))],
        [], (h, nq), comm)


def _flash_bwd(q, k, v, o, do, lse, tabs, tq, comm=None):
    t = q.shape[0]
    h = q.shape[1] // QPAD
    nq = t // tq

    def body(q_ref, k_ref, v_ref, o_ref, do_ref, lse_ref, cc_ref, sa_ref, sb_ref, dq_ref, dk_ref, dv_ref, delta_ref, dqt_ref):
        kj = pl.program_id(1)

        @pl.when(kj == 0)
        def _():
            dqt_ref[...] = jnp.zeros_like(dqt_ref)
            ones = jnp.ones((8, VDIM), BF16)
            for qi in range(nq):
                rows = pl.ds(qi * tq, tq)
                prod = do_ref[rows, :].astype(F32) * o_ref[rows, :].astype(F32)
                hi = prod.astype(BF16)
                lo = (prod - hi.astype(F32)).astype(BF16)
                delta_ref[qi] = (lax.dot_general(ones, hi, NT, preferred_element_type=F32)
                                 + lax.dot_general(ones, lo, NT, preferred_element_type=F32))

        kb = k_ref[...]
        vb = v_ref[...]
        kbt = kb.astype(F32).T.astype(BF16)
        dk_ref[...] = jnp.zeros_like(dk_ref)
        dv_ref[...] = jnp.zeros_like(dv_ref)

        def step(qi, masked):
            rows = pl.ds(pl.multiple_of(qi * tq, tq), tq)
            qb = q_ref[rows, :]
            dob = do_ref[rows, :]
            st = lax.dot_general(kb, qb, NT, preferred_element_type=F32)
            pt = jnp.exp2(st * EXP2_SCALE - lse_ref[qi])
            if masked:
                key = lax.broadcasted_iota(jnp.int32, (tq, tq), 0)
                qry = lax.broadcasted_iota(jnp.int32, (tq, tq), 1)
                pt = jnp.where(key <= qry, pt, 0.0)
            dv_ref[...] += lax.dot_general(pt.astype(BF16), dob, NN, preferred_element_type=F32)
            dpt = lax.dot_general(vb, dob, NT, preferred_element_type=F32)
            dst = (pt * (dpt - delta_ref[qi, pl.ds(0, 1), :]) * ATTN_SCALE).astype(BF16)
            dk_ref[...] += lax.dot_general(dst, qb, NN, preferred_element_type=F32)
            dqt_ref[qi] += lax.dot_general(kbt, dst, NN, preferred_element_type=F32)

        step(kj, True)

        def loop_body(qi, carry):
            step(qi, False)
            return carry

        lax.fori_loop(kj + 1, nq, loop_body, 0)

        @pl.when(kj == nq - 1)
        def _():
            for qi in range(nq):
                rows = pl.ds(qi * tq, tq)
                d = dqt_ref[qi].T
                roped = _rope_bwd(d[:, NOPE:], cc_ref[rows, :], sa_ref[rows, :], sb_ref[rows, :])
                dq_ref[rows, :] = jnp.concatenate([d[:, :NOPE], roped], axis=1).astype(BF16)

    head_q = pl.BlockSpec((t, QPAD), lambda hh, j: (0, hh))
    head_v = pl.BlockSpec((t, VDIM), lambda hh, j: (0, hh))
    table = pl.BlockSpec((t, 128), lambda hh, j: (0, 0))
    return _call(
        "flash_bwd", body, [q, k, v, o, do, lse, *tabs],
        [head_q, pl.BlockSpec((tq, QPAD), lambda hh, j: (j, hh)), pl.BlockSpec((tq, VDIM), lambda hh, j: (j, hh)), head_v, head_v,
         pl.BlockSpec((None, nq, 1, tq), lambda hh, j: (hh, 0, 0, 0)), table, table, table],
        [_sds((t, h * QPAD), BF16), _sds((t, h * QPAD), F32), _sds((t, h * VDIM), F32)],
        [head_q, pl.BlockSpec((tq, QPAD), lambda hh, j: (j, hh)), pl.BlockSpec((tq, VDIM), lambda hh, j: (j, hh))],
        [pltpu.VMEM((nq, 8, tq), F32), pltpu.VMEM((nq, QPAD, tq), F32)], (h, nq), comm)


def _tril_bf16(w):
    row = lax.broadcasted_iota(jnp.int32, w.shape, 0)
    col = lax.broadcasted_iota(jnp.int32, w.shape, 1)
    return jnp.where(col <= row, w, 0.0).astype(BF16)


def _layer_norm_parts(v0):
    mu = jnp.mean(v0, axis=-1, keepdims=True)
    vc = v0 - mu
    rstd = lax.rsqrt(jnp.mean(vc * vc, axis=-1, keepdims=True) + LN_EPS)
    return vc * rstd, rstd


def _sgu_mid_fwd(ge, ln_g, ln_b, w_sp, b_sp, chunks_per_step):
    t, e2 = ge.shape
    e = e2 // 2
    gd = e // SGU_GROUPS
    rows = SGU_CHUNK * chunks_per_step

    def body(u_ref, v_ref, g_ref, b_ref, w_ref, bs_ref, gate_ref):
        for ck in range(chunks_per_step):
            r = pl.ds(ck * SGU_CHUNK, SGU_CHUNK)
            xh, _ = _layer_norm_parts(v_ref[r, :].astype(F32))
            v1 = (xh * g_ref[...] + b_ref[...]).astype(BF16)
            for g in range(SGU_GROUPS):
                cols = pl.ds(g * gd, gd)
                mixed = lax.dot_general(_tril_bf16(w_ref[g]), v1[:, g * gd:(g + 1) * gd], NN, preferred_element_type=F32) + bs_ref[g]
                gate_ref[r, cols] = (u_ref[r, cols].astype(F32) * mixed).astype(BF16)

    return _pcall(
        body, name="sgu_mid_fwd", grid=(t // rows,),
        in_specs=[pl.BlockSpec((rows, e), lambda i: (i, 0)), pl.BlockSpec((rows, e), lambda i: (i, 1)),
                  _const_spec((1, e)), _const_spec((1, e)), _const_spec(w_sp.shape), _const_spec(b_sp.shape)],
        out_specs=pl.BlockSpec((rows, e), lambda i: (i, 0)),
        out_shape=_sds((t, e), BF16), compiler_params=_params(1),
    )(ge, ge, ln_g, ln_b, w_sp, b_sp)


def _sgu_mid_bwd(ge, gp, dgate, ln_g, ln_b, w_sp, b_sp, chunks_per_step):
    t, e2 = ge.shape
    e = e2 // 2
    gd = e // SGU_GROUPS
    rows = SGU_CHUNK * chunks_per_step

    def body(u_ref, v_ref, zu_ref, zv_ref, dg_ref, g_ref, b_ref, w_ref, bs_ref, dz_ref, dw_ref, dbs_ref, dlg_ref, dlb_ref):
        @pl.when(pl.program_id(0) == 0)
        def _():
            dw_ref[...] = jnp.zeros_like(dw_ref)
            dbs_ref[...] = jnp.zeros_like(dbs_ref)
            dlg_ref[...] = jnp.zeros_like(dlg_ref)
            dlb_ref[...] = jnp.zeros_like(dlb_ref)

        for ck in range(chunks_per_step):
            r = pl.ds(ck * SGU_CHUNK, SGU_CHUNK)
            xh, rstd = _layer_norm_parts(v_ref[r, :].astype(F32))
            v1 = (xh * g_ref[...] + b_ref[...]).astype(BF16)
            dv1_parts = []
            for g in range(SGU_GROUPS):
                cols = pl.ds(g * gd, gd)
                wc = _tril_bf16(w_ref[g])
                v1g = v1[:, g * gd:(g + 1) * gd]
                mixed = lax.dot_general(wc, v1g, NN, preferred_element_type=F32) + bs_ref[g]
                dgate = dg_ref[r, cols].astype(F32)
                dmixed = dgate * u_ref[r, cols].astype(F32)
                du = dgate * mixed
                dz_ref[r, cols] = (du * zu_ref[r, cols].astype(F32)).astype(BF16)
                dbs_ref[g] += jnp.sum(dmixed, axis=1, keepdims=True)
                dmb = dmixed.astype(BF16)
                dwg = lax.dot_general(dmb, v1g, NT, preferred_element_type=F32)
                row = lax.broadcasted_iota(jnp.int32, dwg.shape, 0)
                col = lax.broadcasted_iota(jnp.int32, dwg.shape, 1)
                dw_ref[g] += jnp.where(col <= row, dwg, 0.0)
                dv1_parts.append(lax.dot_general(wc, dmb, TN, preferred_element_type=F32))
            dv1 = jnp.concatenate(dv1_parts, axis=1)
            dlg_ref[...] += jnp.sum(dv1 * xh, axis=0, keepdims=True)
            dlb_ref[...] += jnp.sum(dv1, axis=0, keepdims=True)
            dxh = dv1 * g_ref[...]
            dv0 = rstd * (dxh - jnp.mean(dxh, axis=-1, keepdims=True) - xh * jnp.mean(dxh * xh, axis=-1, keepdims=True))
            dz_ref[r, pl.ds(e, e)] = (dv0 * zv_ref[r, :].astype(F32)).astype(BF16)

    half0 = pl.BlockSpec((rows, e), lambda i: (i, 0))
    half1 = pl.BlockSpec((rows, e), lambda i: (i, 1))
    return _pcall(
        body, name="sgu_mid_bwd", grid=(t // rows,),
        in_specs=[half0, half1, half0, half1, half0, _const_spec((1, e)), _const_spec((1, e)), _const_spec(w_sp.shape), _const_spec(b_sp.shape)],
        out_specs=[pl.BlockSpec((rows, e2), lambda i: (i, 0)), _const_spec(w_sp.shape), _const_spec(b_sp.shape), _const_spec((1, e)), _const_spec((1, e))],
        out_shape=[_sds((t, e2), BF16), _sds(w_sp.shape, F32), _sds(b_sp.shape, F32), _sds((1, e), F32), _sds((1, e), F32)],
        compiler_params=_params(1),
    )(ge, ge, gp, gp, dgate, ln_g, ln_b, w_sp, b_sp)


def kernel(x, positions, norm_mix, norm_ffn, final_norm, mla_w_dkv, mla_q_norm, mla_kv_norm, mla_w_uq, mla_w_ukv, mla_w_o, sgu_w_in, sgu_ln_g, sgu_ln_b, sgu_w_spatial, sgu_b_spatial, sgu_w_out, ffn_w_up, ffn_w_down, loss_target, m_norm_mix, m_norm_ffn, m_final_norm, m_mla_w_dkv, m_mla_q_norm, m_mla_kv_norm, m_mla_w_uq, m_mla_w_ukv, m_mla_w_o, m_sgu_w_in, m_sgu_ln_g, m_sgu_ln_b, m_sgu_w_spatial, m_sgu_b_spatial, m_sgu_w_out, m_ffn_w_up, m_ffn_w_down, v_norm_mix, v_norm_ffn, v_final_norm, v_mla_w_dkv, v_mla_q_norm, v_mla_kv_norm, v_mla_w_uq, v_mla_w_ukv, v_mla_w_o, v_sgu_w_in, v_sgu_ln_g, v_sgu_ln_b, v_sgu_w_spatial, v_sgu_b_spatial, v_sgu_w_out, v_ffn_w_up, v_ffn_w_down):
    _, T, D = x.shape
    depth = norm_mix.shape[0]
    n_mla, n_sgu = mla_w_dkv.shape[0], sgu_w_in.shape[0]
    assert depth % 2 == 0
    FF = ffn_w_up.shape[2] * N_DEV
    E = sgu_w_out.shape[1] * N_DEV
    ffc, ec, e2c = FF // N_DEV, E // N_DEV, 2 * E // N_DEV
    dc = D // N_DEV
    OW = HEADS * VDIM
    HW = HEADS * QPAD
    owc = OW // N_DEV
    tm = _tile(T, 1024)
    tb = _tile(T, 4096)
    tq = _tile(T, 512)
    ts = _tile(T, 256)
    nt = T // tm
    x2 = x.reshape(T, D)
    tgt = loss_target.reshape(T, D)
    cidx = lax.axis_index("c").astype(jnp.int32).reshape(1)

    ln_local = jnp.concatenate([sgu_ln_g, sgu_ln_b, jnp.zeros((8 - 2 * n_sgu, ec), F32)], axis=0)
    g_dkv, g_uq, g_ukv, g_o, g_ln = _all_gather(
        "gather_small_weights", [w.astype(BF16) for w in (mla_w_dkv, mla_w_uq, mla_w_ukv, mla_w_o)] + [ln_local])
    w_dkv = jnp.pad(g_dkv.transpose(1, 0, 2, 3).reshape(n_mla, D, LAT), ((0, 0), (0, 0), (0, LAT_PAD - LAT)))
    w_uq = jnp.pad(g_uq, ((0, 0), (0, 0), (0, 0), (0, QPAD - NOPE - ROPE))).transpose(1, 2, 0, 3).reshape(n_mla, Q_RANK, HEADS * QPAD)
    w_ukv = g_ukv.transpose(1, 2, 0, 3).reshape(n_mla, KV_RANK, HEADS * (NOPE + VDIM))
    w_o = g_o.transpose(1, 0, 2, 3).reshape(n_mla, OW, D)
    ln_g_full = [g_ln[:, l, :].reshape(1, E) for l in range(n_sgu)]
    ln_b_full = [g_ln[:, n_sgu + l, :].reshape(1, E) for l in range(n_sgu)]
    b_sp = sgu_b_spatial.reshape(n_sgu, SGU_GROUPS, SGU_CHUNK, 1)
    up_sh = [ffn_w_up[i].astype(BF16) for i in range(depth)]
    down_sh = [ffn_w_down[i].astype(BF16) for i in range(depth)]
    in_sh = [sgu_w_in[l].astype(BF16) for l in range(n_sgu)]
    out_sh = [sgu_w_out[l].astype(BF16) for l in range(n_sgu)]
    g_up, g_down, g_in, g_out = [None] * depth, [None] * depth, [None] * n_sgu, [None] * n_sgu

    inv_freq = ROPE_THETA ** (-jnp.arange(0, ROPE, 2, dtype=F32) / ROPE)
    zeros32 = jnp.zeros((ROPE // 2,), F32)
    inv128 = jnp.concatenate([inv_freq, inv_freq, zeros32, zeros32]).reshape(1, 128)
    sel_a = jnp.concatenate([-jnp.ones((32,), F32), zeros32, zeros32, zeros32]).reshape(1, 128)
    sel_b = jnp.concatenate([zeros32, jnp.ones((32,), F32), zeros32, zeros32]).reshape(1, 128)
    sel_c = jnp.concatenate([jnp.ones((64,), F32), zeros32, zeros32]).reshape(1, 128)

    def rope_tables(pos, inv, sa, sb, sc):
        ang = pos.astype(F32) * inv
        cs, sn = jnp.cos(ang), jnp.sin(ang)
        return cs * sc, sn * sa, sn * sb

    t_cc, t_sa, t_sb = _rowwise(
        "rope_tables", rope_tables, [positions.reshape(T, 1), inv128, sel_a, sel_b, sel_c], grid=(nt,),
        in_specs=[_row_spec(tm, 1)] + [_const_spec((1, 128))] * 4,
        out_shapes=[_sds((T, 128), F32)] * 3, out_specs=[_row_spec(tm, 128)] * 3)
    tab_specs = [_row_spec(tm, 128)] * 3

    def rmsnorm(xv, g):
        return _rowwise("rmsnorm", lambda a, gg: _rms_fwd(a, gg), [xv, g.reshape(1, D)], grid=(nt,),
                        in_specs=[_row_spec(tm, D), _const_spec((1, D))], out_shapes=_sds((T, D), BF16), out_specs=_row_spec(tm, D))

    def proj_cols(name, h, gw, nc, epilogue, n_out):
        return _matmul(name, h, gw, [], grid=(N_DEV, T // tb),
                       a_spec=pl.BlockSpec((tb, D), lambda j, i: (i, 0)),
                       b_spec=pl.BlockSpec((None, D, nc), lambda j, i: (j, 0, 0)), extra_specs=[],
                       out_shapes=[_sds((T, nc * N_DEV), BF16)] * n_out, out_specs=[pl.BlockSpec((tb, nc), lambda j, i: (i, j))] * n_out,
                       dims=NN, epilogue=epilogue)

    def residual_norm(acc, xr, g):
        xn = acc + xr
        return xn, _rms_fwd(xn, g)

    def proj_rows_residual(name, a, gw, kc, xres, g_next):
        return _matmul(name, a, gw, [xres, g_next.reshape(1, D)], grid=(nt, N_DEV),
                       a_spec=pl.BlockSpec((tm, kc), lambda i, k: (i, k)),
                       b_spec=pl.BlockSpec((None, kc, D), lambda i, k: (k, 0, 0)),
                       extra_specs=[pl.BlockSpec((tm, D), lambda i, k: (i, 0)), _const_spec((1, D))],
                       out_shapes=[_sds((T, D), F32), _sds((T, D), BF16)], out_specs=[pl.BlockSpec((tm, D), lambda i, k: (i, 0))] * 2,
                       dims=NN, k_axis=1, nk=N_DEV, acc_shape=(tm, D), epilogue=residual_norm)

    def back_rows(name, dy, gw, kc, extras, epilogue):
        return _matmul(name, dy, gw, extras, grid=(N_DEV, T // tb),
                       a_spec=pl.BlockSpec((tb, D), lambda j, i: (i, 0)),
                       b_spec=pl.BlockSpec((None, kc, D), lambda j, i: (j, 0, 0)),
                       extra_specs=[pl.BlockSpec((tb, kc), lambda j, i: (i, j))] * len(extras),
                       out_shapes=[_sds((T, kc * N_DEV), BF16)], out_specs=[pl.BlockSpec((tb, kc), lambda j, i: (i, j))],
                       dims=NT, epilogue=epilogue)[0]

    def norm_bwd_epilogue(dh, xv, g, dxi):
        dxn, dg = _rms_bwd(xv, g, dh)
        return dxi + dxn, dxi + dxn, dg

    def back_cols(name, da, gw, nc, xv, g, dx_in):
        row = pl.BlockSpec((tm, D), lambda i, k: (i, 0))
        return _matmul(name, da, gw, [xv, g.reshape(1, D), dx_in], grid=(nt, N_DEV),
                       a_spec=pl.BlockSpec((tm, nc), lambda i, k: (i, k)),
                       b_spec=pl.BlockSpec((None, D, nc), lambda i, k: (k, 0, 0)), extra_specs=[row, _const_spec((1, D)), row],
                       out_shapes=[_sds((T, D), F32), _sds((T, D), BF16), _sds((1, D), F32)], out_specs=[row, row, _const_spec((1, D))],
                       dims=NT, k_axis=1, nk=N_DEV, acc_shape=(tm, D), epilogue=norm_bwd_epilogue, n_sum=1)

    def token_sum(tt):
        return dict(k_axis=1, nk=T // tt) if T // tt > 1 else dict(k_axis=None)

    def wgrad_cols(name, h, da, nc):
        return _matmul(name, h, da, [], grid=(N_DEV, T // tb),
                       a_spec=pl.BlockSpec((tb, D), lambda j, t: (t, 0)), b_spec=pl.BlockSpec((tb, nc), lambda j, t: (t, j)),
                       extra_specs=[], out_shapes=[_sds((N_DEV, D, nc), BF16)],
                       out_specs=[pl.BlockSpec((None, D, nc), lambda j, t: (j, 0, 0))],
                       dims=TN, acc_shape=(D, nc), **token_sum(tb))[0]

    def wgrad_rows(name, a, dy, kc, ncols, tt):
        return _matmul(name, a, dy, [], grid=(a.shape[1] // kc, T // tt),
                       a_spec=pl.BlockSpec((tt, kc), lambda j, t: (t, j)), b_spec=pl.BlockSpec((tt, ncols), lambda j, t: (t, 0)),
                       extra_specs=[], out_shapes=[_sds((a.shape[1], ncols), BF16)],
                       out_specs=[pl.BlockSpec((kc, ncols), lambda j, t: (j, 0))],
                       dims=TN, acc_shape=(kc, ncols), **token_sum(tt))[0]

    saved = []
    xs = x2
    for i in range(depth):
        l = i // 2
        if i == 0:
            h = rmsnorm(xs, norm_mix[0])
        if i % 2 == 0:
            lat = _matmul("mla_down", h, w_dkv, [], grid=(nt,), a_spec=_row_spec(tm, D),
                          b_spec=pl.BlockSpec((None, D, LAT_PAD), lambda i_: (l, 0, 0)), extra_specs=[],
                          out_shapes=[_sds((T, LAT_PAD), F32)], out_specs=[_row_spec(tm, LAT_PAD)], dims=NN)[0]

            def latent_post(la, qn, kvn, cc, sa, sb):
                cq = _rms_fwd(la[:, :Q_RANK], qn)
                ckv = _rms_fwd(la[:, Q_RANK:Q_RANK + KV_RANK], kvn)
                kr = _rope_fwd(la[:, Q_RANK + KV_RANK:], cc, sa, sb)
                return cq, ckv, kr

            cq, ckv, kr = _rowwise(
                "mla_latent", latent_post, [lat, mla_q_norm[l].reshape(1, Q_RANK), mla_kv_norm[l].reshape(1, KV_RANK), t_cc, t_sa, t_sb],
                grid=(nt,), in_specs=[_row_spec(tm, LAT_PAD), _const_spec((1, Q_RANK)), _const_spec((1, KV_RANK))] + tab_specs,
                out_shapes=[_sds((T, Q_RANK), BF16), _sds((T, KV_RANK), BF16), _sds((T, 128), BF16)],
                out_specs=[_row_spec(tm, Q_RANK), _row_spec(tm, KV_RANK), _row_spec(tm, 128)])

            def q_epilogue(acc, cc, sa, sb):
                parts = []
                for b in range(HEADS):
                    parts += [acc[:, b * QPAD:b * QPAD + NOPE], _rope_fwd(acc[:, b * QPAD + NOPE:(b + 1) * QPAD], cc, sa, sb)]
                return (jnp.concatenate(parts, axis=1),)

            q = _matmul("mla_q", cq, w_uq, [t_cc, t_sa, t_sb], grid=(nt,), a_spec=_row_spec(tm, Q_RANK),
                        b_spec=pl.BlockSpec((None, Q_RANK, HW), lambda i_: (l, 0, 0)), extra_specs=tab_specs,
                        out_shapes=[_sds((T, HW), BF16)], out_specs=[_row_spec(tm, HW)], dims=NN, epilogue=q_epilogue)[0]

            def kv_write(outs, acc, krb):
                k_ref, v_ref, vt_ref = outs
                for b in range(HEADS):
                    vb = acc[:, b * QPAD + NOPE:(b + 1) * QPAD]
                    k_ref[:, b * QPAD:b * QPAD + NOPE] = acc[:, b * QPAD:b * QPAD + NOPE].astype(BF16)
                    k_ref[:, b * QPAD + NOPE:(b + 1) * QPAD] = krb
                    v_ref[:, b * VDIM:(b + 1) * VDIM] = vb.astype(BF16)
                    vbt = vb.T.astype(BF16)
                    for u in range(tm // tq):
                        vt_ref[b, u] = vbt[:, u * tq:(u + 1) * tq]

            kk, vv, vt = _matmul("mla_kv", ckv, w_ukv, [kr], grid=(nt,), a_spec=_row_spec(tm, KV_RANK),
                                 b_spec=pl.BlockSpec((None, KV_RANK, HW), lambda i_: (l, 0, 0)), extra_specs=[_row_spec(tm, 128)],
                                 out_shapes=[_sds((T, HW), BF16), _sds((T, OW), BF16), _sds((HEADS, T // tq, VDIM, tq), BF16)],
                                 out_specs=[_row_spec(tm, HW), _row_spec(tm, OW), pl.BlockSpec((HEADS, tm // tq, VDIM, tq), lambda i_: (0, i_, 0, 0))],
                                 dims=NN, write=kv_write)
            group = [up_sh[i], down_sh[i], in_sh[l], out_sh[l], up_sh[i + 1], down_sh[i + 1]]
            o, lse, *bufs = _flash_fwd(q, kk, vt, tq, comm=_gather_level1(group))
            xm, h2, *bufs = _matmul("mla_out", o, w_o, [xs, norm_ffn[i].reshape(1, D)], grid=(nt,), a_spec=_row_spec(tm, OW),
                                    b_spec=pl.BlockSpec((None, OW, D), lambda i_: (l, 0, 0)), extra_specs=[_row_spec(tm, D), _const_spec((1, D))],
                                    out_shapes=[_sds((T, D), F32), _sds((T, D), BF16)], out_specs=[_row_spec(tm, D)] * 2, dims=NN,
                                    epilogue=residual_norm, comm=_gather_level2(bufs))
            g_up[i], g_down[i], g_in[l], g_out[l], g_up[i + 1], g_down[i + 1] = bufs
            mix_saved = (h, lat, cq, ckv, q, kk, vv, o, lse)
        else:
            gp, ge = proj_cols("sgu_in", h, g_in[l], e2c, _gelu_and_grad, 2)
            gate = _sgu_mid_fwd(ge, ln_g_full[l], ln_b_full[l], sgu_w_spatial[l], b_sp[l], 4)
            xm, h2 = proj_rows_residual("sgu_out", gate, g_out[l], ec, xs, norm_ffn[i])
            mix_saved = (h, gp, ge, gate)
        r, s = proj_cols("ffn_up", h2, g_up[i], ffc, lambda acc: (jnp.maximum(acc, 0.0), jnp.square(jnp.maximum(acc, 0.0))), 2)
        xo, h_next = proj_rows_residual("ffn_down", s, g_down[i], ffc, xm, norm_mix[i + 1] if i + 1 < depth else final_norm)
        saved.append((xs, xm, mix_saved, h2, r, s))
        xs, h = xo, h_next

    def loss_head(xv, tg, g):
        y = _rms_fwd(xv, g)
        err = y - tg
        part = 0.5 * jnp.sum(jnp.sum(err * err, axis=-1, keepdims=True), axis=0, keepdims=True) / D
        dx, dg = _rms_bwd(xv, g, err / D)
        return dx, dx, jnp.broadcast_to(part, (1, 128)), dg

    dx, dyb, loss_part, d_final = _rowwise(
        "loss_head", loss_head, [xs, tgt, final_norm.reshape(1, D)], grid=(nt,),
        in_specs=[_row_spec(tm, D), _row_spec(tm, D), _const_spec((1, D))],
        out_shapes=[_sds((T, D), F32), _sds((T, D), BF16), _sds((1, 128), F32), _sds((1, D), F32)],
        out_specs=[_row_spec(tm, D), _row_spec(tm, D), _const_spec((1, 128)), _const_spec((1, D))], n_acc=2)
    loss = lax.psum(loss_part[0, 0], ("x", "y", "c"))

    d_norm_mix, d_norm_ffn = [None] * depth, [None] * depth
    d_qn, d_kvn = [None] * n_mla, [None] * n_mla
    d_wsp, d_bsp, d_lng, d_lnb = [None] * n_sgu, [None] * n_sgu, [None] * n_sgu, [None] * n_sgu
    layers = {"dkv": n_mla, "uq": n_mla, "ukv": n_mla, "o": n_mla, "in": n_sgu, "out": n_sgu, "up": depth, "down": depth}
    stacked = {nm: None for nm in layers}
    pending = []

    def add_pair(g, rcv):
        _, rws, cls = g.shape
        g4 = g.reshape(N_CHIP, 2, rws, cls)
        rt = _tile(rws, 512)
        return _rowwise("grad_pair_sum", lambda a, b_: a.astype(F32) + b_.astype(F32), [g4, rcv], grid=(N_CHIP, rws // rt),
                        in_specs=[pl.BlockSpec((None, None, rt, cls), lambda ch, i_, cr: (ch, cr[0], i_, 0)),
                                  pl.BlockSpec((None, rt, cls), lambda ch, i_, cr: (ch, i_, 0))],
                        out_shapes=_sds(rcv.shape, BF16), out_specs=pl.BlockSpec((None, rt, cls), lambda ch, i_, cr: (ch, i_, 0)),
                        grid_spec_prefetch=cidx)

    def chip_comm_of_pending():
        grads = [g for _, _, g in pending]
        from_sibling = _comm_call("grad_sibling_exchange", _sibling_exchange(grads))
        parts = [add_pair(g, rcv) for g, rcv in zip(grads, from_sibling)]
        comm, names = _chip_exchange(parts, [(nm, l_) for nm, l_, _ in pending], layers, stacked)
        pending.clear()
        return comm, names

    for i in reversed(range(depth)):
        l = i // 2
        xs_i, xm, mix_saved, h2, r, s = saved[i]
        da = back_rows("ffn_down_bwd", dyb, g_down[i], ffc, [r], lambda acc, rr: (acc * (2.0 * rr.astype(F32)),))
        pending.append(("down", i, wgrad_rows("ffn_down_wgrad", s, dyb, ffc, D, tb).reshape(N_DEV, ffc, D)))
        pending.append(("up", i, wgrad_cols("ffn_up_wgrad", h2, da, ffc)))
        dx, dyb, d_norm_ffn[i] = back_cols("ffn_up_bwd", da, g_up[i], ffc, xm, norm_ffn[i], dx)
        if i % 2 == 0:
            h, lat, cq, ckv, q, kk, vv, o, lse = mix_saved
            do = _matmul("mla_out_bwd", dyb, w_o, [], grid=(nt,), a_spec=_row_spec(tm, D),
                         b_spec=pl.BlockSpec((None, OW, D), lambda i_: (l, 0, 0)), extra_specs=[],
                         out_shapes=[_sds((T, OW), BF16)], out_specs=[_row_spec(tm, OW)], dims=NT)[0]
            g_o_l = wgrad_rows("mla_out_wgrad", o, dyb, OW, D, tm).reshape(N_DEV, owc, D)
            comm, names = chip_comm_of_pending()
            dq_pre, dk, dv, *bufs = _flash_bwd(q, kk, vv, o, do, lse, (t_cc, t_sa, t_sb), tq, comm=comm)
            stacked.update(dict(zip(names, bufs)))
            pending.append(("o", l, g_o_l))

            def kv_pre(dkb, dvb, cc, sa, sb):
                parts, dkr = [], None
                for b in range(HEADS):
                    parts += [dkb[:, b * QPAD:b * QPAD + NOPE], dvb[:, b * VDIM:(b + 1) * VDIM]]
                    piece = dkb[:, b * QPAD + NOPE:(b + 1) * QPAD]
                    dkr = piece if dkr is None else dkr + piece
                return jnp.concatenate(parts, axis=1), _rope_bwd(dkr, cc, sa, sb)

            dkv, dkr = _rowwise("mla_dkv_rope", kv_pre, [dk, dv, t_cc, t_sa, t_sb], grid=(T // ts,),
                                in_specs=[_row_spec(ts, HW), _row_spec(ts, OW)] + [_row_spec(ts, 128)] * 3,
                                out_shapes=[_sds((T, HW), BF16), _sds((T, 128), F32)], out_specs=[_row_spec(ts, HW), _row_spec(ts, 128)])
            g_uq_l = wgrad_rows("mla_q_wgrad", cq, dq_pre, Q_RANK, HW, tm)
            g_ukv_l = wgrad_rows("mla_kv_wgrad", ckv, dkv, KV_RANK, HW, tm)
            pending.append(("uq", l, g_uq_l.reshape(Q_RANK, HEADS, QPAD)[:, :, :NOPE + ROPE].transpose(1, 0, 2)))
            pending.append(("ukv", l, g_ukv_l.reshape(KV_RANK, HEADS, NOPE + VDIM).transpose(1, 0, 2)))
            dcq = _matmul("mla_q_bwd", dq_pre, w_uq, [], grid=(nt,), a_spec=_row_spec(tm, HW),
                          b_spec=pl.BlockSpec((None, Q_RANK, HW), lambda i_: (l, 0, 0)), extra_specs=[],
                          out_shapes=[_sds((T, Q_RANK), F32)], out_specs=[_row_spec(tm, Q_RANK)], dims=NT)[0]
            dckv = _matmul("mla_kv_bwd", dkv, w_ukv, [], grid=(nt,), a_spec=_row_spec(tm, HW),
                           b_spec=pl.BlockSpec((None, KV_RANK, HW), lambda i_: (l, 0, 0)), extra_specs=[],
                           out_shapes=[_sds((T, KV_RANK), F32)], out_specs=[_row_spec(tm, KV_RANK)], dims=NT)[0]

            def latent_bwd(la, qn, kvn, dq_, dkv_, dkr_):
                dcq_raw, dqn = _rms_bwd(la[:, :Q_RANK], qn, dq_)
                dckv_raw, dkvn = _rms_bwd(la[:, Q_RANK:Q_RANK + KV_RANK], kvn, dkv_)
                return jnp.concatenate([dcq_raw, dckv_raw, dkr_], axis=1), dqn, dkvn

            dlat, d_qn[l], d_kvn[l] = _rowwise(
                "mla_latent_bwd", latent_bwd, [lat, mla_q_norm[l].reshape(1, Q_RANK), mla_kv_norm[l].reshape(1, KV_RANK), dcq, dckv, dkr],
                grid=(nt,), in_specs=[_row_spec(tm, LAT_PAD), _const_spec((1, Q_RANK)), _const_spec((1, KV_RANK)),
                                      _row_spec(tm, Q_RANK), _row_spec(tm, KV_RANK), _row_spec(tm, 128)],
                out_shapes=[_sds((T, LAT_PAD), BF16), _sds((1, Q_RANK), F32), _sds((1, KV_RANK), F32)],
                out_specs=[_row_spec(tm, LAT_PAD), _const_spec((1, Q_RANK)), _const_spec((1, KV_RANK))], n_acc=2)
            g_dkv_l = wgrad_rows("mla_down_wgrad", h, dlat, D, LAT_PAD, tm)
            pending.append(("dkv", l, g_dkv_l[:, :LAT].reshape(N_DEV, dc, LAT)))
            dx, dyb, d_norm_mix[i] = _matmul(
                "mla_down_bwd", dlat, w_dkv, [xs_i, norm_mix[i].reshape(1, D), dx], grid=(nt,), a_spec=_row_spec(tm, LAT_PAD),
                b_spec=pl.BlockSpec((None, D, LAT_PAD), lambda i_: (l, 0, 0)), extra_specs=[_row_spec(tm, D), _const_spec((1, D)), _row_spec(tm, D)],
                out_shapes=[_sds((T, D), F32), _sds((T, D), BF16), _sds((1, D), F32)],
                out_specs=[_row_spec(tm, D), _row_spec(tm, D), _const_spec((1, D))], dims=NT, epilogue=norm_bwd_epilogue, n_sum=1)
        else:
            h, gp, ge, gate = mix_saved
            dgate = back_rows("sgu_out_bwd", dyb, g_out[l], ec, [], None)
            pending.append(("out", l, wgrad_rows("sgu_out_wgrad", gate, dyb, ec, D, tb).reshape(N_DEV, ec, D)))
            dz, d_wsp[l], d_bsp[l], d_lng[l], d_lnb[l] = _sgu_mid_bwd(ge, gp, dgate, ln_g_full[l], ln_b_full[l], sgu_w_spatial[l], b_sp[l], 2)
            pending.append(("in", l, wgrad_cols("sgu_in_wgrad", h, dz, e2c)))
            dx, dyb, d_norm_mix[i] = back_cols("sgu_in_bwd", dz, g_in[l], e2c, xs_i, norm_mix[i], dx)
    grad_x = dx.reshape(1, T, D)

    comm, names = chip_comm_of_pending()
    stacked.update(dict(zip(names, _comm_call("grad_chip_exchange", comm))))

    def adam_big(name, parts, w, m, v):
        lyr, rws, cls = w.shape
        rt = _tile(rws, 256)

        def fn(p, w_, m_, v_):
            g = (p[0].astype(F32) + p[1].astype(F32)) + (p[2].astype(F32) + p[3].astype(F32))
            return (g, *_adam(w_, g, m_, v_))

        spec = pl.BlockSpec((None, rt, cls), lambda l_, i_: (l_, i_, 0))
        return _rowwise(name, fn, [parts, w, m, v], grid=(lyr, rws // rt),
                        in_specs=[pl.BlockSpec((N_CHIP, None, rt, cls), lambda l_, i_: (0, l_, i_, 0)), spec, spec, spec],
                        out_shapes=[_sds(w.shape, F32)] * 4, out_specs=[spec] * 4)

    big_names = ["dkv", "uq", "ukv", "o", "in", "out", "up", "down"]
    big_w = [mla_w_dkv, mla_w_uq, mla_w_ukv, mla_w_o, sgu_w_in, sgu_w_out, ffn_w_up, ffn_w_down]
    big_m = [m_mla_w_dkv, m_mla_w_uq, m_mla_w_ukv, m_mla_w_o, m_sgu_w_in, m_sgu_w_out, m_ffn_w_up, m_ffn_w_down]
    big_v = [v_mla_w_dkv, v_mla_w_uq, v_mla_w_ukv, v_mla_w_o, v_sgu_w_in, v_sgu_w_out, v_ffn_w_up, v_ffn_w_down]
    big_res = [adam_big("adam_large", stacked[nm], w, m, v) for nm, w, m, v in zip(big_names, big_w, big_m, big_v)]

    def rows128(a, rows):
        flat = a.reshape(-1, 128)
        return jnp.pad(flat, ((0, rows - flat.shape[0]), (0, 0)))

    def pad8(n):
        return -(-n // 8) * 8

    small_names = ["norm_mix", "norm_ffn", "final_norm", "q_norm", "kv_norm", "w_spatial", "b_spatial"]
    small_g = [jnp.concatenate(d_norm_mix, 0), jnp.concatenate(d_norm_ffn, 0), d_final, jnp.concatenate(d_qn, 0), jnp.concatenate(d_kvn, 0),
               jnp.stack(d_wsp, 0), jnp.stack(d_bsp, 0), jnp.concatenate(d_lng, 0), jnp.concatenate(d_lnb, 0)]
    small_w = [norm_mix, norm_ffn, final_norm, mla_q_norm, mla_kv_norm, sgu_w_spatial, sgu_b_spatial]
    small_m = [m_norm_mix, m_norm_ffn, m_final_norm, m_mla_q_norm, m_mla_kv_norm, m_sgu_w_spatial, m_sgu_b_spatial]
    small_v = [v_norm_mix, v_norm_ffn, v_final_norm, v_mla_q_norm, v_mla_kv_norm, v_sgu_w_spatial, v_sgu_b_spatial]
    sizes = [pad8(g.size // 128) for g in small_g]
    n_rep = len(small_w)
    sizes[n_rep - 1] += -sum(sizes[:n_rep]) % SMALL_ROWS
    offs = [sum(sizes[:k]) for k in range(len(sizes) + 1)]
    rep_rows = offs[n_rep]
    pack_g = jnp.concatenate([rows128(g, sz) for g, sz in zip(small_g, sizes)], axis=0)
    (gathered_small,) = _all_gather("gather_small_grads", [pack_g])

    def pack(arrs):
        return jnp.concatenate([rows128(a, sz) for a, sz in zip(arrs, sizes[:n_rep])], axis=0)

    def sum8(p):
        return ((p[0] + p[1]) + (p[2] + p[3])) + ((p[4] + p[5]) + (p[6] + p[7]))

    sspec = _row_spec(SMALL_ROWS, 128)
    rep_g, rep_d, rep_m, rep_v = _rowwise(
        "adam_small", lambda p, w_, m_, v_: (sum8(p), *_adam(w_, sum8(p), m_, v_)),
        [gathered_small, pack(small_w), pack(small_m), pack(small_v)], grid=(rep_rows // SMALL_ROWS,),
        in_specs=[pl.BlockSpec((N_DEV, SMALL_ROWS, 128), lambda i_: (0, i_, 0)), sspec, sspec, sspec],
        out_shapes=[_sds((rep_rows, 128), F32)] * 4, out_specs=[sspec] * 4)

    def unpack(packed, k, like):
        return packed[offs[k]:offs[k] + like.size // 128].reshape(like.shape)

    my_b = 4 * lax.axis_index("x") + 2 * lax.axis_index("y") + lax.axis_index("c")
    ln_w = jnp.concatenate([sgu_ln_g, sgu_ln_b], 0)
    ln_m = jnp.concatenate([m_sgu_ln_g, m_sgu_ln_b], 0)
    ln_v = jnp.concatenate([v_sgu_ln_g, v_sgu_ln_b], 0)
    ln_all = gathered_small[:, rep_rows:, :]
    ln_mine = lax.dynamic_slice_in_dim(ln_all[:, :2 * n_sgu * E // 128].reshape(N_DEV, 2 * n_sgu, N_DEV, ec), my_b, 1, axis=2).reshape(N_DEV, 2 * n_sgu, ec)
    ln_g_, ln_d, ln_m2, ln_v2 = _rowwise(
        "adam_ln", lambda p, w_, m_, v_: (sum8(p), *_adam(w_, sum8(p), m_, v_)), [ln_mine, ln_w, ln_m, ln_v], grid=(1,),
        in_specs=[_const_spec(ln_mine.shape), _const_spec(ln_w.shape), _const_spec(ln_w.shape), _const_spec(ln_w.shape)],
        out_shapes=[_sds(ln_w.shape, F32)] * 4, out_specs=[_const_spec(ln_w.shape)] * 4)

    def family(pos):
        rep = [rep_g, rep_d, rep_m, rep_v][pos]
        ln = [ln_g_, ln_d, ln_m2, ln_v2][pos]
        small = {nm: unpack(rep, k, w_) for k, (nm, w_) in enumerate(zip(small_names, small_w))}
        big = [res[pos] for res in big_res]
        return [small["norm_mix"], small["norm_ffn"], small["final_norm"], big[0], small["q_norm"], small["kv_norm"], big[1], big[2], big[3],
                big[4], ln[:n_sgu], ln[n_sgu:], small["w_spatial"], small["b_spatial"], big[5], big[6], big[7]]

    return (loss, grad_x, *family(0), *family(1), *family(2), *family(3))
```

```python
import math

import jax
import jax.numpy as jnp
from jax import lax
from jax.experimental import pallas as pl
from jax.experimental.pallas import tpu as pltpu

F32 = jnp.float32
BF16 = jnp.bfloat16
MESH = pl.DeviceIdType.MESH

N_DEV = 8
N_CHIP = 4
HEADS = 8
NOPE = 128
ROPE = 64
VDIM = 128
QPAD = 256
Q_RANK = 256
KV_RANK = 128
LAT = Q_RANK + KV_RANK + ROPE
LAT_PAD = 512
ROPE_THETA = 10000.0
SGU_CHUNK = 128
SGU_GROUPS = 8
NORM_EPS = 1e-6
LN_EPS = 1e-5
ADAM_LR = 0.001
ADAM_B1 = 0.9
ADAM_B2 = 0.999
ADAM_EPS = 1e-08
ADAM_WD = 0.01
ADAM_STEP = 10
ATTN_SCALE = (NOPE + ROPE) ** -0.5
NEG = -1e30
EXP2_SCALE = ATTN_SCALE * math.log2(math.e)
VMEM_LIMIT = 56 * 1024 * 1024
SMALL_ROWS = 256

NN = (((1,), (0,)), ((), ()))
NT = (((1,), (1,)), ((), ()))
TN = (((0,), (0,)), ((), ()))
ANY = pl.BlockSpec(memory_space=pl.ANY)


def _pcall(body, **kw):
    return pl.pallas_call(body, **kw)


def _params(n_grid, side_effects=False):
    return pltpu.CompilerParams(dimension_semantics=("arbitrary",) * n_grid, vmem_limit_bytes=VMEM_LIMIT, has_side_effects=side_effects)


def _sds(shape, dtype):
    return jax.ShapeDtypeStruct(tuple(shape), dtype)


def _tile(n, want):
    t = min(n, want)
    assert n % t == 0, (n, want)
    return t


class _Comm:
    def __init__(self, operands, out_shapes, aliases, scratch, start, finish):
        self.operands, self.out_shapes, self.aliases, self.scratch = operands, out_shapes, aliases, scratch
        self.start, self.finish = start, finish


def _place():
    return lax.axis_index("x"), lax.axis_index("y"), lax.axis_index("c")


def _other_chips(x, y):
    return [(1 - x, y), (x, 1 - y), (1 - x, 1 - y)]


def _dev_index(dev):
    return 4 * dev[0] + 2 * dev[1] + dev[2]


def _comm_call(name, comm):
    c_in, c_out = len(comm.operands), len(comm.out_shapes)

    def body(*refs):
        ins, outs, sems = refs[:c_in], refs[c_in:c_in + c_out], refs[c_in + c_out:]
        comm.start(ins, outs, sems)
        comm.finish(ins, outs, sems)

    return _pcall(body, name=name, in_specs=[ANY] * c_in, out_specs=[ANY] * c_out, out_shape=comm.out_shapes,
                  scratch_shapes=comm.scratch, input_output_aliases=dict(comm.aliases),
                  compiler_params=pltpu.CompilerParams(has_side_effects=True))(*comm.operands)


def _call(name, body, operands, in_specs, out_shapes, out_specs, scratch, grid, comm=None):
    if comm is None:
        return _pcall(body, name=name, grid=grid, in_specs=in_specs, out_specs=out_specs, out_shape=out_shapes,
                      scratch_shapes=scratch, compiler_params=_params(len(grid)))(*operands)
    n_in, n_out, n_sc = len(operands), len(out_shapes), len(scratch)
    c_in, c_out = len(comm.operands), len(comm.out_shapes)

    def hosted(*refs):
        ins, cins = refs[:n_in], refs[n_in:n_in + c_in]
        o0 = n_in + c_in
        outs, couts = refs[o0:o0 + n_out], refs[o0 + n_out:o0 + n_out + c_out]
        rest = refs[o0 + n_out + c_out:]
        sc, csems = rest[:n_sc], rest[n_sc:]
        first = pl.program_id(0) == 0
        last = pl.program_id(0) == grid[0] - 1
        for d in range(1, len(grid)):
            first = jnp.logical_and(first, pl.program_id(d) == 0)
            last = jnp.logical_and(last, pl.program_id(d) == grid[d] - 1)

        @pl.when(first)
        def _():
            comm.start(cins, couts, csems)

        body(*ins, *outs, *sc)

        @pl.when(last)
        def _():
            comm.finish(cins, couts, csems)

    return _pcall(hosted, name=name, grid=grid, in_specs=[*in_specs, *[ANY] * c_in], out_specs=[*out_specs, *[ANY] * c_out],
                  out_shape=[*out_shapes, *comm.out_shapes], scratch_shapes=[*scratch, *comm.scratch],
                  input_output_aliases={n_in + k: n_out + v for k, v in comm.aliases.items()},
                  compiler_params=_params(len(grid), side_effects=True))(*operands, *comm.operands)


def _gather_level1(shards):
    n = len(shards)

    def copies(ins, outs, sems):
        send_sems, recv_sems, local_sems = sems
        x, y, c = _place()
        me, sibling = (x, y, c), (x, y, 1 - c)
        chips = _other_chips(x, y)

        def copy(a, k, block, to, src=None):
            slot = outs[a].at[_dev_index(block)]
            return pltpu.make_async_remote_copy(src_ref=slot if src is None else src, dst_ref=slot, send_sem=send_sems.at[a, k],
                                                recv_sem=recv_sems.at[a, k], device_id=to, device_id_type=MESH)

        mine = [pltpu.make_async_copy(ins[a], outs[a].at[_dev_index(me)], local_sems.at[a]) for a in range(n)]
        sends = [copy(a, 1 + j, me, (*chip, c), src=ins[a]) for j, chip in enumerate(chips) for a in range(n)]
        sends += [copy(a, 0, me, sibling, src=ins[a]) for a in range(n)]
        recvs = [copy(a, 1 + j, (*chip, c), me) for j, chip in enumerate(chips) for a in range(n)]
        recvs += [copy(a, 0, sibling, me) for a in range(n)]
        return mine, sends, recvs

    def start(ins, outs, sems):
        mine, sends, _ = copies(ins, outs, sems)
        for cp in mine + sends:
            cp.start()

    def finish(ins, outs, sems):
        mine, sends, recvs = copies(ins, outs, sems)
        for cp in recvs:
            cp.wait_recv()
        for cp in sends:
            cp.wait_send()
        for cp in mine:
            cp.wait()

    return _Comm(shards, [_sds((N_DEV, *a.shape), a.dtype) for a in shards], {},
                 [pltpu.SemaphoreType.DMA((n, 4)), pltpu.SemaphoreType.DMA((n, 4)), pltpu.SemaphoreType.DMA((n,))], start, finish)


def _gather_level2(bufs):
    n = len(bufs)

    def copies(outs, sems):
        send_sems, recv_sems = sems
        x, y, c = _place()
        sibling = (x, y, 1 - c)
        sends, recvs = [], []
        for j, chip in enumerate(_other_chips(x, y)):
            for a in range(n):
                have, want = outs[a].at[_dev_index((*chip, c))], outs[a].at[_dev_index((*chip, 1 - c))]
                sends.append(pltpu.make_async_remote_copy(src_ref=have, dst_ref=have, send_sem=send_sems.at[a, j], recv_sem=recv_sems.at[a, j],
                                                          device_id=sibling, device_id_type=MESH))
                recvs.append(pltpu.make_async_remote_copy(src_ref=want, dst_ref=want, send_sem=send_sems.at[a, j], recv_sem=recv_sems.at[a, j],
                                                          device_id=sibling, device_id_type=MESH))
        return sends, recvs

    def start(ins, outs, sems):
        for cp in copies(outs, sems)[0]:
            cp.start()

    def finish(ins, outs, sems):
        sends, recvs = copies(outs, sems)
        for cp in recvs:
            cp.wait_recv()
        for cp in sends:
            cp.wait_send()

    return _Comm(bufs, [_sds(b.shape, b.dtype) for b in bufs], {a: a for a in range(n)},
                 [pltpu.SemaphoreType.DMA((n, 3)), pltpu.SemaphoreType.DMA((n, 3))], start, finish)


def _all_gather(name, arrays):
    n = len(arrays)

    def body(*refs):
        ins = refs[:n]
        outs = refs[n:2 * n]
        send_sems, recv_sems, local_sems = refs[2 * n:]
        x, y, c = _place()
        me, sibling = (x, y, c), (x, y, 1 - c)
        chips = _other_chips(x, y)

        def copy(a, k, block, to, src=None):
            slot = outs[a].at[_dev_index(block)]
            return pltpu.make_async_remote_copy(src_ref=slot if src is None else src, dst_ref=slot, send_sem=send_sems.at[a, k],
                                                recv_sem=recv_sems.at[a, k], device_id=to, device_id_type=MESH)

        mine = [pltpu.make_async_copy(ins[a], outs[a].at[_dev_index(me)], local_sems.at[a]) for a in range(n)]
        for cp in mine:
            cp.start()
        first = []
        for j, chip in enumerate(chips):
            first += [copy(a, 1 + j, me, (*chip, c), src=ins[a]) for a in range(n)]
        first += [copy(a, 0, me, sibling, src=ins[a]) for a in range(n)]
        for cp in first:
            cp.start()
        passed = []
        for j, chip in enumerate(chips):
            for a in range(n):
                copy(a, 1 + j, (*chip, c), me).wait_recv()
                fwd = copy(a, 4 + j, (*chip, c), sibling)
                fwd.start()
                passed.append(fwd)
        for a in range(n):
            copy(a, 0, sibling, me).wait_recv()
            for j, chip in enumerate(chips):
                copy(a, 4 + j, (*chip, 1 - c), me).wait_recv()
        for cp in first + passed:
            cp.wait_send()
        for cp in mine:
            cp.wait()

    return _pcall(
        body, name=name, in_specs=[ANY] * n, out_specs=[ANY] * n,
        out_shape=[_sds((N_DEV, *a.shape), a.dtype) for a in arrays],
        scratch_shapes=[pltpu.SemaphoreType.DMA((n, 7)), pltpu.SemaphoreType.DMA((n, 7)), pltpu.SemaphoreType.DMA((n,))],
        compiler_params=pltpu.CompilerParams(has_side_effects=True),
    )(*arrays)


def _sibling_exchange(grads):
    n = len(grads)

    def start(ins, outs, sems):
        send_sems, recv_sems = sems
        x, y, c = _place()
        for a in range(n):
            for ch in range(N_CHIP):
                pltpu.make_async_remote_copy(src_ref=ins[a].at[2 * ch + 1 - c], dst_ref=outs[a].at[ch], send_sem=send_sems.at[a],
                                             recv_sem=recv_sems.at[a], device_id=(x, y, 1 - c), device_id_type=MESH).start()

    def finish(ins, outs, sems):
        send_sems, recv_sems = sems
        x, y, c = _place()
        for a in range(n):
            pltpu.make_async_remote_copy(src_ref=outs[a], dst_ref=outs[a], send_sem=send_sems.at[a], recv_sem=recv_sems.at[a],
                                         device_id=(x, y, 1 - c), device_id_type=MESH).wait()

    return _Comm(grads, [_sds((N_CHIP, *g.shape[1:]), g.dtype) for g in grads], {},
                 [pltpu.SemaphoreType.DMA((n,)), pltpu.SemaphoreType.DMA((n,))], start, finish)


def _chip_exchange(parts, slots, layers, stacked):
    n = len(parts)
    names = []
    for nm, _ in slots:
        if nm not in names:
            names.append(nm)
    shapes = {nm: _sds((N_CHIP, layers[nm], *parts[a].shape[1:]), parts[a].dtype) for a, (nm, _) in enumerate(slots)}
    kept = [nm for nm in names if stacked.get(nm) is not None]
    aliases = {n + k: names.index(nm) for k, nm in enumerate(kept)}

    def copies(ins, outs, sems):
        send_sems, recv_sems, local_sems = sems
        x, y, c = _place()
        mine = 2 * x + y
        local, sends, recvs = [], [], []
        for a, (nm, l) in enumerate(slots):
            buf = outs[names.index(nm)]
            local.append(pltpu.make_async_copy(ins[a].at[mine], buf.at[mine, l], local_sems.at[a]))
            for j, chip in enumerate(_other_chips(x, y)):
                theirs = buf.at[2 * chip[0] + chip[1], l]
                sends.append(pltpu.make_async_remote_copy(src_ref=ins[a].at[2 * chip[0] + chip[1]], dst_ref=buf.at[mine, l], send_sem=send_sems.at[a, j],
                                                          recv_sem=recv_sems.at[a, j], device_id=(*chip, c), device_id_type=MESH))
                recvs.append(pltpu.make_async_remote_copy(src_ref=theirs, dst_ref=theirs, send_sem=send_sems.at[a, j],
                                                          recv_sem=recv_sems.at[a, j], device_id=(*chip, c), device_id_type=MESH))
        return local, sends, recvs

    def start(ins, outs, sems):
        local, sends, _ = copies(ins, outs, sems)
        for cp in local + sends:
            cp.start()

    def finish(ins, outs, sems):
        local, sends, recvs = copies(ins, outs, sems)
        for cp in recvs:
            cp.wait_recv()
        for cp in sends:
            cp.wait_send()
        for cp in local:
            cp.wait()

    comm = _Comm([*parts, *[stacked[nm] for nm in kept]], [shapes[nm] for nm in names], aliases,
                 [pltpu.SemaphoreType.DMA((n, 3)), pltpu.SemaphoreType.DMA((n, 3)), pltpu.SemaphoreType.DMA((n,))], start, finish)
    return comm, names


def _matmul(name, a, b, extras, *, grid, a_spec, b_spec, extra_specs, out_shapes, out_specs, dims, k_axis=None, nk=1,
            acc_shape=None, epilogue=None, comm=None, n_sum=0, write=None):
    n_extra = len(extras)
    n_out = len(out_shapes)

    def body(*refs):
        a_ref, b_ref = refs[0], refs[1]
        ex = refs[2:2 + n_extra]
        outs = refs[2 + n_extra:2 + n_extra + n_out]
        prod = lax.dot_general(a_ref[...], b_ref[...], dims, preferred_element_type=F32)

        def finish(acc):
            if write is not None:
                write(outs, acc, *[e[...] for e in ex])
                return
            res = epilogue(acc, *[e[...] for e in ex]) if epilogue is not None else (acc,)
            first = None
            for d in range(len(grid)):
                if d != k_axis:
                    here = pl.program_id(d) == 0
                    first = here if first is None else jnp.logical_and(first, here)
            for idx, (o, r) in enumerate(zip(outs, res)):
                if idx < n_out - n_sum:
                    o[...] = r.astype(o.dtype)
                else:
                    @pl.when(first)
                    def _(o=o, r=r):
                        o[...] = r.astype(o.dtype)

                    @pl.when(jnp.logical_not(first))
                    def _(o=o, r=r):
                        o[...] += r.astype(o.dtype)

        if k_axis is None:
            finish(prod)
        else:
            acc_ref = refs[-1]
            k = pl.program_id(k_axis)

            @pl.when(k == 0)
            def _():
                acc_ref[...] = prod

            @pl.when(k > 0)
            def _():
                acc_ref[...] += prod

            @pl.when(k == nk - 1)
            def _():
                finish(acc_ref[...])

    scratch = [] if k_axis is None else [pltpu.VMEM(acc_shape, F32)]
    return _call(name, body, [a, b, *extras], [a_spec, b_spec, *extra_specs], list(out_shapes), list(out_specs), scratch, grid, comm)


def _rowwise(name, fn, operands, *, grid, in_specs, out_shapes, out_specs, n_acc=0, grid_spec_prefetch=None):
    n_in = len(operands)
    n_out = len(out_shapes)
    n_pre = 0 if grid_spec_prefetch is None else 1

    def body(*refs):
        refs = refs[n_pre:]
        ins = refs[:n_in]
        outs = refs[n_in:n_in + n_out]
        res = fn(*[r[...] for r in ins])
        if not isinstance(res, (tuple, list)):
            res = (res,)
        first = pl.program_id(0) == 0
        for d in range(1, len(grid)):
            first = jnp.logical_and(first, pl.program_id(d) == 0)
        for idx, (o, r) in enumerate(zip(outs, res)):
            if idx < n_out - n_acc:
                o[...] = r.astype(o.dtype)
            else:
                @pl.when(first)
                def _(o=o, r=r):
                    o[...] = r.astype(o.dtype)

                @pl.when(jnp.logical_not(first))
                def _(o=o, r=r):
                    o[...] += r.astype(o.dtype)

    if grid_spec_prefetch is None:
        return _pcall(body, name=name, grid=grid, in_specs=in_specs, out_specs=out_specs, out_shape=out_shapes,
                      compiler_params=_params(len(grid)))(*operands)
    gs = pltpu.PrefetchScalarGridSpec(num_scalar_prefetch=1, grid=grid, in_specs=in_specs, out_specs=out_specs)
    return _pcall(body, name=name, grid_spec=gs, out_shape=out_shapes,
                  compiler_params=_params(len(grid)))(grid_spec_prefetch, *operands)


def _row_spec(tm, w):
    return pl.BlockSpec((tm, w), lambda i: (i, 0))


def _const_spec(shape):
    nd = len(shape)
    return pl.BlockSpec(tuple(shape), lambda *_: (0,) * nd)


def _rms_fwd(x, g):
    r = lax.rsqrt(jnp.mean(x * x, axis=-1, keepdims=True) + NORM_EPS)
    return x * r * g


def _rms_bwd(x, g, dy):
    r = lax.rsqrt(jnp.mean(x * x, axis=-1, keepdims=True) + NORM_EPS)
    xh = x * r
    u = dy * g
    dx = r * (u - xh * jnp.mean(u * xh, axis=-1, keepdims=True))
    dg = jnp.sum(dy * xh, axis=0, keepdims=True)
    return dx, dg


def _gelu_and_grad(z):
    cdf = 0.5 * (1.0 + lax.erf(z * (2.0 ** -0.5)))
    return cdf + z * jnp.exp(-0.5 * z * z) * ((2.0 * math.pi) ** -0.5), z * cdf


def _rope_fwd(x, cc, sa, sb):
    return x * cc + pltpu.roll(x, 96, 1) * sa + pltpu.roll(x, 32, 1) * sb


def _rope_bwd(d, cc, sa, sb):
    return d * cc + pltpu.roll(d * sa, 32, 1) + pltpu.roll(d * sb, 96, 1)


def _adam(w, g, m, v):
    m = ADAM_B1 * m + (1.0 - ADAM_B1) * g
    v = ADAM_B2 * v + (1.0 - ADAM_B2) * (g * g)
    m_hat = m / (1.0 - ADAM_B1 ** ADAM_STEP)
    v_hat = v / (1.0 - ADAM_B2 ** ADAM_STEP)
    delta = -ADAM_LR * (m_hat / (jnp.sqrt(v_hat) + ADAM_EPS) + ADAM_WD * w)
    return delta, m, v


def _flash_fwd(q, k, vt, tq, comm=None):
    h, t = vt.shape[0], q.shape[0]
    nq = t // tq

    def body(q_ref, k_ref, vt_ref, o_ref, lse_ref):
        qi = pl.program_id(1)

        def scores(kj):
            kb = k_ref[pl.ds(pl.multiple_of(kj * tq, tq), tq), :]
            return lax.dot_general(kb, q_ref[...], NT, preferred_element_type=F32)

        def step(kj, st, state, masked):
            m_old, l_old, acc_old = state
            if masked:
                key = lax.broadcasted_iota(jnp.int32, (tq, tq), 0)
                qry = lax.broadcasted_iota(jnp.int32, (tq, tq), 1)
                st = jnp.where(key <= qry, st, NEG)
            m_new = jnp.maximum(m_old, jnp.max(st, axis=0, keepdims=True))
            alpha = jnp.exp2((m_old - m_new) * EXP2_SCALE)
            pt = jnp.exp2((st - m_new) * EXP2_SCALE)
            l_new = alpha * l_old + jnp.sum(pt, axis=0, keepdims=True)
            acc_new = alpha * acc_old + lax.dot_general(vt_ref[kj], pt.astype(BF16), NN, preferred_element_type=F32)
            return m_new, l_new, acc_new

        init = (jnp.full((1, tq), NEG, F32), jnp.zeros((1, tq), F32), jnp.zeros((VDIM, tq), F32))
        state = lax.fori_loop(0, qi, lambda kj, st_: step(kj, scores(kj), st_, False), init)
        m, l, acc = step(qi, scores(qi), state, True)
        o_ref[...] = (acc / l).T.astype(o_ref.dtype)
        lse_ref[...] = m * EXP2_SCALE + jnp.log2(l)

    return _call(
        "flash_fwd", body, [q, k, vt],
        [pl.BlockSpec((tq, QPAD), lambda hh, i: (i, hh)),
         pl.BlockSpec((t, QPAD), lambda hh, i: (0, hh)),
         pl.BlockSpec((None, nq, VDIM, tq), lambda hh, i: (hh, 0, 0, 0))],
        [_sds((t, h * VDIM), BF16), _sds((h, nq, 1, tq), F32)],
        [pl.BlockSpec((tq, VDIM), lambda hh, i: (i, hh)),
         pl.BlockSpec((None, None, 1, tq), lambda hh, i: (hh, i, 0, 0))],
        [], (h, nq), comm)


def _flash_bwd(q, k, v, o, do, lse, tabs, tq, comm=None):
    t = q.shape[0]
    h = q.shape[1] // QPAD
    nq = t // tq

    def body(q_ref, k_ref, v_ref, o_ref, do_ref, lse_ref, cc_ref, sa_ref, sb_ref, dq_ref, dk_ref, dv_ref, delta_ref, dqt_ref):
        kj = pl.program_id(1)

        @pl.when(kj == 0)
        def _():
            dqt_ref[...] = jnp.zeros_like(dqt_ref)
            ones = jnp.ones((8, VDIM), BF16)
            for qi in range(nq):
                rows = pl.ds(qi * tq, tq)
                prod = do_ref[rows, :].astype(F32) * o_ref[rows, :].astype(F32)
                hi = prod.astype(BF16)
                lo = (prod - hi.astype(F32)).astype(BF16)
                delta_ref[qi] = (lax.dot_general(ones, hi, NT, preferred_element_type=F32)
                                 + lax.dot_general(ones, lo, NT, preferred_element_type=F32))

        kb = k_ref[...]
        vb = v_ref[...]
        kbt = kb.astype(F32).T.astype(BF16)
        dk_ref[...] = jnp.zeros_like(dk_ref)
        dv_ref[...] = jnp.zeros_like(dv_ref)

        def step(qi, masked):
            rows = pl.ds(pl.multiple_of(qi * tq, tq), tq)
            qb = q_ref[rows, :]
            dob = do_ref[rows, :]
            st = lax.dot_general(kb, qb, NT, preferred_element_type=F32)
            pt = jnp.exp2(st * EXP2_SCALE - lse_ref[qi])
            if masked:
                key = lax.broadcasted_iota(jnp.int32, (tq, tq), 0)
                qry = lax.broadcasted_iota(jnp.int32, (tq, tq), 1)
                pt = jnp.where(key <= qry, pt, 0.0)
            dv_ref[...] += lax.dot_general(pt.astype(BF16), dob, NN, preferred_element_type=F32)
            dpt = lax.dot_general(vb, dob, NT, preferred_element_type=F32)
            dst = (pt * (dpt - delta_ref[qi, pl.ds(0, 1), :]) * ATTN_SCALE).astype(BF16)
            dk_ref[...] += lax.dot_general(dst, qb, NN, preferred_element_type=F32)
            dqt_ref[qi] += lax.dot_general(kbt, dst, NN, preferred_element_type=F32)

        step(kj, True)

        def loop_body(qi, carry):
            step(qi, False)
            return carry

        lax.fori_loop(kj + 1, nq, loop_body, 0)

        @pl.when(kj == nq - 1)
        def _():
            for qi in range(nq):
                rows = pl.ds(qi * tq, tq)
                d = dqt_ref[qi].T
                roped = _rope_bwd(d[:, NOPE:], cc_ref[rows, :], sa_ref[rows, :], sb_ref[rows, :])
                dq_ref[rows, :] = jnp.concatenate([d[:, :NOPE], roped], axis=1).astype(BF16)

    head_q = pl.BlockSpec((t, QPAD), lambda hh, j: (0, hh))
    head_v = pl.BlockSpec((t, VDIM), lambda hh, j: (0, hh))
    table = pl.BlockSpec((t, 128), lambda hh, j: (0, 0))
    return _call(
        "flash_bwd", body, [q, k, v, o, do, lse, *tabs],
        [head_q, pl.BlockSpec((tq, QPAD), lambda hh, j: (j, hh)), pl.BlockSpec((tq, VDIM), lambda hh, j: (j, hh)), head_v, head_v,
         pl.BlockSpec((None, nq, 1, tq), lambda hh, j: (hh, 0, 0, 0)), table, table, table],
        [_sds((t, h * QPAD), BF16), _sds((t, h * QPAD), F32), _sds((t, h * VDIM), F32)],
        [head_q, pl.BlockSpec((tq, QPAD), lambda hh, j: (j, hh)), pl.BlockSpec((tq, VDIM), lambda hh, j: (j, hh))],
        [pltpu.VMEM((nq, 8, tq), F32), pltpu.VMEM((nq, QPAD, tq), F32)], (h, nq), comm)


def _tril_bf16(w):
    row = lax.broadcasted_iota(jnp.int32, w.shape, 0)
    col = lax.broadcasted_iota(jnp.int32, w.shape, 1)
    return jnp.where(col <= row, w, 0.0).astype(BF16)


def _layer_norm_parts(v0):
    mu = jnp.mean(v0, axis=-1, keepdims=True)
    vc = v0 - mu
    rstd = lax.rsqrt(jnp.mean(vc * vc, axis=-1, keepdims=True) + LN_EPS)
    return vc * rstd, rstd


def _sgu_mid_fwd(ge, ln_g, ln_b, w_sp, b_sp, chunks_per_step):
    t, e2 = ge.shape
    e = e2 // 2
    gd = e // SGU_GROUPS
    rows = SGU_CHUNK * chunks_per_step

    def body(u_ref, v_ref, g_ref, b_ref, w_ref, bs_ref, gate_ref):
        for ck in range(chunks_per_step):
            r = pl.ds(ck * SGU_CHUNK, SGU_CHUNK)
            xh, _ = _layer_norm_parts(v_ref[r, :].astype(F32))
            v1 = (xh * g_ref[...] + b_ref[...]).astype(BF16)
            for g in range(SGU_GROUPS):
                cols = pl.ds(g * gd, gd)
                mixed = lax.dot_general(_tril_bf16(w_ref[g]), v1[:, g * gd:(g + 1) * gd], NN, preferred_element_type=F32) + bs_ref[g]
                gate_ref[r, cols] = (u_ref[r, cols].astype(F32) * mixed).astype(BF16)

    return _pcall(
        body, name="sgu_mid_fwd", grid=(t // rows,),
        in_specs=[pl.BlockSpec((rows, e), lambda i: (i, 0)), pl.BlockSpec((rows, e), lambda i: (i, 1)),
                  _const_spec((1, e)), _const_spec((1, e)), _const_spec(w_sp.shape), _const_spec(b_sp.shape)],
        out_specs=pl.BlockSpec((rows, e), lambda i: (i, 0)),
        out_shape=_sds((t, e), BF16), compiler_params=_params(1),
    )(ge, ge, ln_g, ln_b, w_sp, b_sp)


def _sgu_mid_bwd(ge, gp, dgate, ln_g, ln_b, w_sp, b_sp, chunks_per_step):
    t, e2 = ge.shape
    e = e2 // 2
    gd = e // SGU_GROUPS
    rows = SGU_CHUNK * chunks_per_step

    def body(u_ref, v_ref, zu_ref, zv_ref, dg_ref, g_ref, b_ref, w_ref, bs_ref, dz_ref, dw_ref, dbs_ref, dlg_ref, dlb_ref):
        @pl.when(pl.program_id(0) == 0)
        def _():
            dw_ref[...] = jnp.zeros_like(dw_ref)
            dbs_ref[...] = jnp.zeros_like(dbs_ref)
            dlg_ref[...] = jnp.zeros_like(dlg_ref)
            dlb_ref[...] = jnp.zeros_like(dlb_ref)

        for ck in range(chunks_per_step):
            r = pl.ds(ck * SGU_CHUNK, SGU_CHUNK)
            xh, rstd = _layer_norm_parts(v_ref[r, :].astype(F32))
            v1 = (xh * g_ref[...] + b_ref[...]).astype(BF16)
            dv1_parts = []
            for g in range(SGU_GROUPS):
                cols = pl.ds(g * gd, gd)
                wc = _tril_bf16(w_ref[g])
                v1g = v1[:, g * gd:(g + 1) * gd]
                mixed = lax.dot_general(wc, v1g, NN, preferred_element_type=F32) + bs_ref[g]
                dgate = dg_ref[r, cols].astype(F32)
                dmixed = dgate * u_ref[r, cols].astype(F32)
                du = dgate * mixed
                dz_ref[r, cols] = (du * zu_ref[r, cols].astype(F32)).astype(BF16)
                dbs_ref[g] += jnp.sum(dmixed, axis=1, keepdims=True)
                dmb = dmixed.astype(BF16)
                dwg = lax.dot_general(dmb, v1g, NT, preferred_element_type=F32)
                row = lax.broadcasted_iota(jnp.int32, dwg.shape, 0)
                col = lax.broadcasted_iota(jnp.int32, dwg.shape, 1)
                dw_ref[g] += jnp.where(col <= row, dwg, 0.0)
                dv1_parts.append(lax.dot_general(wc, dmb, TN, preferred_element_type=F32))
            dv1 = jnp.concatenate(dv1_parts, axis=1)
            dlg_ref[...] += jnp.sum(dv1 * xh, axis=0, keepdims=True)
            dlb_ref[...] += jnp.sum(dv1, axis=0, keepdims=True)
            dxh = dv1 * g_ref[...]
            dv0 = rstd * (dxh - jnp.mean(dxh, axis=-1, keepdims=True) - xh * jnp.mean(dxh * xh, axis=-1, keepdims=True))
            dz_ref[r, pl.ds(e, e)] = (dv0 * zv_ref[r, :].astype(F32)).astype(BF16)

    half0 = pl.BlockSpec((rows, e), lambda i: (i, 0))
    half1 = pl.BlockSpec((rows, e), lambda i: (i, 1))
    return _pcall(
        body, name="sgu_mid_bwd", grid=(t // rows,),
        in_specs=[half0, half1, half0, half1, half0, _const_spec((1, e)), _const_spec((1, e)), _const_spec(w_sp.shape), _const_spec(b_sp.shape)],
        out_specs=[pl.BlockSpec((rows, e2), lambda i: (i, 0)), _const_spec(w_sp.shape), _const_spec(b_sp.shape), _const_spec((1, e)), _const_spec((1, e))],
        out_shape=[_sds((t, e2), BF16), _sds(w_sp.shape, F32), _sds(b_sp.shape, F32), _sds((1, e), F32), _sds((1, e), F32)],
        compiler_params=_params(1),
    )(ge, ge, gp, gp, dgate, ln_g, ln_b, w_sp, b_sp)


def kernel(x, positions, norm_mix, norm_ffn, final_norm, mla_w_dkv, mla_q_norm, mla_kv_norm, mla_w_uq, mla_w_ukv, mla_w_o, sgu_w_in, sgu_ln_g, sgu_ln_b, sgu_w_spatial, sgu_b_spatial, sgu_w_out, ffn_w_up, ffn_w_down, loss_target, m_norm_mix, m_norm_ffn, m_final_norm, m_mla_w_dkv, m_mla_q_norm, m_mla_kv_norm, m_mla_w_uq, m_mla_w_ukv, m_mla_w_o, m_sgu_w_in, m_sgu_ln_g, m_sgu_ln_b, m_sgu_w_spatial, m_sgu_b_spatial, m_sgu_w_out, m_ffn_w_up, m_ffn_w_down, v_norm_mix, v_norm_ffn, v_final_norm, v_mla_w_dkv, v_mla_q_norm, v_mla_kv_norm, v_mla_w_uq, v_mla_w_ukv, v_mla_w_o, v_sgu_w_in, v_sgu_ln_g, v_sgu_ln_b, v_sgu_w_spatial, v_sgu_b_spatial, v_sgu_w_out, v_ffn_w_up, v_ffn_w_down):
    _, T, D = x.shape
    depth = norm_mix.shape[0]
    n_mla, n_sgu = mla_w_dkv.shape[0], sgu_w_in.shape[0]
    assert depth % 2 == 0
    FF = ffn_w_up.shape[2] * N_DEV
    E = sgu_w_out.shape[1] * N_DEV
    ffc, ec, e2c = FF // N_DEV, E // N_DEV, 2 * E // N_DEV
    dc = D // N_DEV
    OW = HEADS * VDIM
    HW = HEADS * QPAD
    owc = OW // N_DEV
    tm = _tile(T, 1024)
    tb = _tile(T, 4096)
    tk = _tile(T, 512)
    tq = _tile(T, 512)
    ts = _tile(T, 256)
    nt = T // tm
    x2 = x.reshape(T, D)
    tgt = loss_target.reshape(T, D)
    cidx = lax.axis_index("c").astype(jnp.int32).reshape(1)

    ln_local = jnp.concatenate([sgu_ln_g, sgu_ln_b, jnp.zeros((8 - 2 * n_sgu, ec), F32)], axis=0)
    g_dkv, g_uq, g_ukv, g_o, g_ln = _all_gather(
        "gather_small_weights", [w.astype(BF16) for w in (mla_w_dkv, mla_w_uq, mla_w_ukv, mla_w_o)] + [ln_local])
    w_dkv = jnp.pad(g_dkv.transpose(1, 0, 2, 3).reshape(n_mla, D, LAT), ((0, 0), (0, 0), (0, LAT_PAD - LAT)))
    w_uq = jnp.pad(g_uq, ((0, 0), (0, 0), (0, 0), (0, QPAD - NOPE - ROPE))).transpose(1, 2, 0, 3).reshape(n_mla, Q_RANK, HEADS * QPAD)
    w_ukv = g_ukv.transpose(1, 2, 0, 3).reshape(n_mla, KV_RANK, HEADS * (NOPE + VDIM))
    w_o = g_o.transpose(1, 0, 2, 3).reshape(n_mla, OW, D)
    ln_g_full = [g_ln[:, l, :].reshape(1, E) for l in range(n_sgu)]
    ln_b_full = [g_ln[:, n_sgu + l, :].reshape(1, E) for l in range(n_sgu)]
    b_sp = sgu_b_spatial.reshape(n_sgu, SGU_GROUPS, SGU_CHUNK, 1)
    up_sh = [ffn_w_up[i].astype(BF16) for i in range(depth)]
    down_sh = [ffn_w_down[i].astype(BF16) for i in range(depth)]
    in_sh = [sgu_w_in[l].astype(BF16) for l in range(n_sgu)]
    out_sh = [sgu_w_out[l].astype(BF16) for l in range(n_sgu)]
    g_up, g_down, g_in, g_out = [None] * depth, [None] * depth, [None] * n_sgu, [None] * n_sgu

    inv_freq = ROPE_THETA ** (-jnp.arange(0, ROPE, 2, dtype=F32) / ROPE)
    zeros32 = jnp.zeros((ROPE // 2,), F32)
    inv128 = jnp.concatenate([inv_freq, inv_freq, zeros32, zeros32]).reshape(1, 128)
    sel_a = jnp.concatenate([-jnp.ones((32,), F32), zeros32, zeros32, zeros32]).reshape(1, 128)
    sel_b = jnp.concatenate([zeros32, jnp.ones((32,), F32), zeros32, zeros32]).reshape(1, 128)
    sel_c = jnp.concatenate([jnp.ones((64,), F32), zeros32, zeros32]).reshape(1, 128)

    def rope_tables(pos, inv, sa, sb, sc):
        ang = pos.astype(F32) * inv
        cs, sn = jnp.cos(ang), jnp.sin(ang)
        return cs * sc, sn * sa, sn * sb

    t_cc, t_sa, t_sb = _rowwise(
        "rope_tables", rope_tables, [positions.reshape(T, 1), inv128, sel_a, sel_b, sel_c], grid=(nt,),
        in_specs=[_row_spec(tm, 1)] + [_const_spec((1, 128))] * 4,
        out_shapes=[_sds((T, 128), F32)] * 3, out_specs=[_row_spec(tm, 128)] * 3)
    tab_specs = [_row_spec(tm, 128)] * 3

    def rmsnorm(xv, g):
        return _rowwise("rmsnorm", lambda a, gg: _rms_fwd(a, gg), [xv, g.reshape(1, D)], grid=(nt,),
                        in_specs=[_row_spec(tm, D), _const_spec((1, D))], out_shapes=_sds((T, D), BF16), out_specs=_row_spec(tm, D))

    def proj_cols(name, h, gw, nc, epilogue, n_out):
        return _matmul(name, h, gw, [], grid=(N_DEV, T // tb),
                       a_spec=pl.BlockSpec((tb, D), lambda j, i: (i, 0)),
                       b_spec=pl.BlockSpec((None, D, nc), lambda j, i: (j, 0, 0)), extra_specs=[],
                       out_shapes=[_sds((T, nc * N_DEV), BF16)] * n_out, out_specs=[pl.BlockSpec((tb, nc), lambda j, i: (i, j))] * n_out,
                       dims=NN, epilogue=epilogue)

    def residual_norm(acc, xr, g):
        xn = acc + xr
        return xn, _rms_fwd(xn, g)

    def proj_rows_residual(name, a, gw, xres, g_next):
        kk_ = a.shape[1]
        return _matmul(name, a, gw.reshape(kk_, D), [xres, g_next.reshape(1, D)], grid=(T // tk,),
                       a_spec=_row_spec(tk, kk_), b_spec=_const_spec((kk_, D)), extra_specs=[_row_spec(tk, D), _const_spec((1, D))],
                       out_shapes=[_sds((T, D), F32), _sds((T, D), BF16)], out_specs=[_row_spec(tk, D)] * 2,
                       dims=NN, epilogue=residual_norm)

    def back_rows(name, dy, gw, kc, extras, epilogue):
        return _matmul(name, dy, gw, extras, grid=(N_DEV, T // tb),
                       a_spec=pl.BlockSpec((tb, D), lambda j, i: (i, 0)),
                       b_spec=pl.BlockSpec((None, kc, D), lambda j, i: (j, 0, 0)),
                       extra_specs=[pl.BlockSpec((tb, kc), lambda j, i: (i, j))] * len(extras),
                       out_shapes=[_sds((T, kc * N_DEV), BF16)], out_specs=[pl.BlockSpec((tb, kc), lambda j, i: (i, j))],
                       dims=NT, epilogue=epilogue)[0]

    def norm_bwd_epilogue(dh, xv, g, dxi):
        dxn, dg = _rms_bwd(xv, g, dh)
        return dxi + dxn, dxi + dxn, dg

    def transposed(gw):
        return gw.transpose(0, 2, 1).reshape(gw.shape[0] * gw.shape[2], D)

    def back_cols(name, da, gwt, xv, g, dx_in):
        n = da.shape[1]
        row = _row_spec(tk, D)
        return _matmul(name, da, gwt, [xv, g.reshape(1, D), dx_in], grid=(T // tk,),
                       a_spec=_row_spec(tk, n), b_spec=_const_spec((n, D)), extra_specs=[row, _const_spec((1, D)), row],
                       out_shapes=[_sds((T, D), F32), _sds((T, D), BF16), _sds((1, D), F32)], out_specs=[row, row, _const_spec((1, D))],
                       dims=NN, epilogue=norm_bwd_epilogue, n_sum=1)

    def token_sum(tt):
        return dict(k_axis=1, nk=T // tt) if T // tt > 1 else dict(k_axis=None)

    def wgrad_cols(name, h, da, nc):
        return _matmul(name, h, da, [], grid=(N_DEV, T // tb),
                       a_spec=pl.BlockSpec((tb, D), lambda j, t: (t, 0)), b_spec=pl.BlockSpec((tb, nc), lambda j, t: (t, j)),
                       extra_specs=[], out_shapes=[_sds((N_DEV, D, nc), BF16)],
                       out_specs=[pl.BlockSpec((None, D, nc), lambda j, t: (j, 0, 0))],
                       dims=TN, acc_shape=(D, nc), **token_sum(tb))[0]

    def wgrad_rows(name, a, dy, kc, ncols, tt):
        return _matmul(name, a, dy, [], grid=(a.shape[1] // kc, T // tt),
                       a_spec=pl.BlockSpec((tt, kc), lambda j, t: (t, j)), b_spec=pl.BlockSpec((tt, ncols), lambda j, t: (t, 0)),
                       extra_specs=[], out_shapes=[_sds((a.shape[1], ncols), BF16)],
                       out_specs=[pl.BlockSpec((kc, ncols), lambda j, t: (j, 0))],
                       dims=TN, acc_shape=(kc, ncols), **token_sum(tt))[0]

    saved = []
    xs = x2
    for i in range(depth):
        l = i // 2
        if i == 0:
            h = rmsnorm(xs, norm_mix[0])
        if i % 2 == 0:
            lat = _matmul("mla_down", h, w_dkv, [], grid=(nt,), a_spec=_row_spec(tm, D),
                          b_spec=pl.BlockSpec((None, D, LAT_PAD), lambda i_: (l, 0, 0)), extra_specs=[],
                          out_shapes=[_sds((T, LAT_PAD), F32)], out_specs=[_row_spec(tm, LAT_PAD)], dims=NN)[0]

            def latent_post(la, qn, kvn, cc, sa, sb):
                cq = _rms_fwd(la[:, :Q_RANK], qn)
                ckv = _rms_fwd(la[:, Q_RANK:Q_RANK + KV_RANK], kvn)
                kr = _rope_fwd(la[:, Q_RANK + KV_RANK:], cc, sa, sb)
                return cq, ckv, kr

            cq, ckv, kr = _rowwise(
                "mla_latent", latent_post, [lat, mla_q_norm[l].reshape(1, Q_RANK), mla_kv_norm[l].reshape(1, KV_RANK), t_cc, t_sa, t_sb],
                grid=(nt,), in_specs=[_row_spec(tm, LAT_PAD), _const_spec((1, Q_RANK)), _const_spec((1, KV_RANK))] + tab_specs,
                out_shapes=[_sds((T, Q_RANK), BF16), _sds((T, KV_RANK), BF16), _sds((T, 128), BF16)],
                out_specs=[_row_spec(tm, Q_RANK), _row_spec(tm, KV_RANK), _row_spec(tm, 128)])

            def q_epilogue(acc, cc, sa, sb):
                parts = []
                for b in range(HEADS):
                    parts += [acc[:, b * QPAD:b * QPAD + NOPE], _rope_fwd(acc[:, b * QPAD + NOPE:(b + 1) * QPAD], cc, sa, sb)]
                return (jnp.concatenate(parts, axis=1),)

            q = _matmul("mla_q", cq, w_uq, [t_cc, t_sa, t_sb], grid=(nt,), a_spec=_row_spec(tm, Q_RANK),
                        b_spec=pl.BlockSpec((None, Q_RANK, HW), lambda i_: (l, 0, 0)), extra_specs=tab_specs,
                        out_shapes=[_sds((T, HW), BF16)], out_specs=[_row_spec(tm, HW)], dims=NN, epilogue=q_epilogue)[0]

            def kv_write(outs, acc, krb):
                k_ref, v_ref, vt_ref = outs
                for b in range(HEADS):
                    vb = acc[:, b * QPAD + NOPE:(b + 1) * QPAD]
                    k_ref[:, b * QPAD:b * QPAD + NOPE] = acc[:, b * QPAD:b * QPAD + NOPE].astype(BF16)
                    k_ref[:, b * QPAD + NOPE:(b + 1) * QPAD] = krb
                    v_ref[:, b * VDIM:(b + 1) * VDIM] = vb.astype(BF16)
                    vbt = vb.T.astype(BF16)
                    for u in range(tm // tq):
                        vt_ref[b, u] = vbt[:, u * tq:(u + 1) * tq]

            kk, vv, vt = _matmul("mla_kv", ckv, w_ukv, [kr], grid=(nt,), a_spec=_row_spec(tm, KV_RANK),
                                 b_spec=pl.BlockSpec((None, KV_RANK, HW), lambda i_: (l, 0, 0)), extra_specs=[_row_spec(tm, 128)],
                                 out_shapes=[_sds((T, HW), BF16), _sds((T, OW), BF16), _sds((HEADS, T // tq, VDIM, tq), BF16)],
                                 out_specs=[_row_spec(tm, HW), _row_spec(tm, OW), pl.BlockSpec((HEADS, tm // tq, VDIM, tq), lambda i_: (0, i_, 0, 0))],
                                 dims=NN, write=kv_write)
            group = [up_sh[i], down_sh[i], in_sh[l], out_sh[l], up_sh[i + 1], down_sh[i + 1]]
            o, lse, *bufs = _flash_fwd(q, kk, vt, tq, comm=_gather_level1(group))
            xm, h2, *bufs = _matmul("mla_out", o, w_o, [xs, norm_ffn[i].reshape(1, D)], grid=(nt,), a_spec=_row_spec(tm, OW),
                                    b_spec=pl.BlockSpec((None, OW, D), lambda i_: (l, 0, 0)), extra_specs=[_row_spec(tm, D), _const_spec((1, D))],
                                    out_shapes=[_sds((T, D), F32), _sds((T, D), BF16)], out_specs=[_row_spec(tm, D)] * 2, dims=NN,
                                    epilogue=residual_norm, comm=_gather_level2(bufs))
            g_up[i], g_down[i], g_in[l], g_out[l], g_up[i + 1], g_down[i + 1] = bufs
            mix_saved = (h, lat, cq, ckv, q, kk, vv, o, lse)
        else:
            gp, ge = proj_cols("sgu_in", h, g_in[l], e2c, _gelu_and_grad, 2)
            gate = _sgu_mid_fwd(ge, ln_g_full[l], ln_b_full[l], sgu_w_spatial[l], b_sp[l], 4)
            xm, h2 = proj_rows_residual("sgu_out", gate, g_out[l], xs, norm_ffn[i])
            mix_saved = (h, gp, ge, gate)
        r, s = proj_cols("ffn_up", h2, g_up[i], ffc, lambda acc: (jnp.maximum(acc, 0.0), jnp.square(jnp.maximum(acc, 0.0))), 2)
        xo, h_next = proj_rows_residual("ffn_down", s, g_down[i], xm, norm_mix[i + 1] if i + 1 < depth else final_norm)
        saved.append((xs, xm, mix_saved, h2, r, s))
        xs, h = xo, h_next

    def loss_head(xv, tg, g):
        y = _rms_fwd(xv, g)
        err = y - tg
        part = 0.5 * jnp.sum(jnp.sum(err * err, axis=-1, keepdims=True), axis=0, keepdims=True) / D
        dx, dg = _rms_bwd(xv, g, err / D)
        return dx, dx, jnp.broadcast_to(part, (1, 128)), dg

    dx, dyb, loss_part, d_final = _rowwise(
        "loss_head", loss_head, [xs, tgt, final_norm.reshape(1, D)], grid=(nt,),
        in_specs=[_row_spec(tm, D), _row_spec(tm, D), _const_spec((1, D))],
        out_shapes=[_sds((T, D), F32), _sds((T, D), BF16), _sds((1, 128), F32), _sds((1, D), F32)],
        out_specs=[_row_spec(tm, D), _row_spec(tm, D), _const_spec((1, 128)), _const_spec((1, D))], n_acc=2)
    loss = lax.psum(loss_part[0, 0], ("x", "y", "c"))

    d_norm_mix, d_norm_ffn = [None] * depth, [None] * depth
    d_qn, d_kvn = [None] * n_mla, [None] * n_mla
    d_wsp, d_bsp, d_lng, d_lnb = [None] * n_sgu, [None] * n_sgu, [None] * n_sgu, [None] * n_sgu
    layers = {"dkv": n_mla, "uq": n_mla, "ukv": n_mla, "o": n_mla, "in": n_sgu, "out": n_sgu, "up": depth, "down": depth}
    stacked = {nm: None for nm in layers}
    pending = []

    def add_pair(g, rcv):
        _, rws, cls = g.shape
        g4 = g.reshape(N_CHIP, 2, rws, cls)
        rt = _tile(rws, 512)
        return _rowwise("grad_pair_sum", lambda a, b_: a.astype(F32) + b_.astype(F32), [g4, rcv], grid=(N_CHIP, rws // rt),
                        in_specs=[pl.BlockSpec((None, None, rt, cls), lambda ch, i_, cr: (ch, cr[0], i_, 0)),
                                  pl.BlockSpec((None, rt, cls), lambda ch, i_, cr: (ch, i_, 0))],
                        out_shapes=_sds(rcv.shape, BF16), out_specs=pl.BlockSpec((None, rt, cls), lambda ch, i_, cr: (ch, i_, 0)),
                        grid_spec_prefetch=cidx)

    def chip_comm_of_pending():
        grads = [g for _, _, g in pending]
        from_sibling = _comm_call("grad_sibling_exchange", _sibling_exchange(grads))
        parts = [add_pair(g, rcv) for g, rcv in zip(grads, from_sibling)]
        comm, names = _chip_exchange(parts, [(nm, l_) for nm, l_, _ in pending], layers, stacked)
        pending.clear()
        return comm, names

    for i in reversed(range(depth)):
        l = i // 2
        xs_i, xm, mix_saved, h2, r, s = saved[i]
        da = back_rows("ffn_down_bwd", dyb, g_down[i], ffc, [r], lambda acc, rr: (acc * (2.0 * rr.astype(F32)),))
        pending.append(("down", i, wgrad_rows("ffn_down_wgrad", s, dyb, ffc, D, tb).reshape(N_DEV, ffc, D)))
        pending.append(("up", i, wgrad_cols("ffn_up_wgrad", h2, da, ffc)))
        dx, dyb, d_norm_ffn[i] = back_cols("ffn_up_bwd", da, transposed(g_up[i]), xm, norm_ffn[i], dx)
        if i % 2 == 0:
            h, lat, cq, ckv, q, kk, vv, o, lse = mix_saved
            do = _matmul("mla_out_bwd", dyb, w_o, [], grid=(nt,), a_spec=_row_spec(tm, D),
                         b_spec=pl.BlockSpec((None, OW, D), lambda i_: (l, 0, 0)), extra_specs=[],
                         out_shapes=[_sds((T, OW), BF16)], out_specs=[_row_spec(tm, OW)], dims=NT)[0]
            g_o_l = wgrad_rows("mla_out_wgrad", o, dyb, OW, D, tm).reshape(N_DEV, owc, D)
            comm, names = chip_comm_of_pending()
            dq_pre, dk, dv, *bufs = _flash_bwd(q, kk, vv, o, do, lse, (t_cc, t_sa, t_sb), tq, comm=comm)
            stacked.update(dict(zip(names, bufs)))
            pending.append(("o", l, g_o_l))

            def kv_pre(dkb, dvb, cc, sa, sb):
                parts, dkr = [], None
                for b in range(HEADS):
                    parts += [dkb[:, b * QPAD:b * QPAD + NOPE], dvb[:, b * VDIM:(b + 1) * VDIM]]
                    piece = dkb[:, b * QPAD + NOPE:(b + 1) * QPAD]
                    dkr = piece if dkr is None else dkr + piece
                return jnp.concatenate(parts, axis=1), _rope_bwd(dkr, cc, sa, sb)

            dkv, dkr = _rowwise("mla_dkv_rope", kv_pre, [dk, dv, t_cc, t_sa, t_sb], grid=(T // ts,),
                                in_specs=[_row_spec(ts, HW), _row_spec(ts, OW)] + [_row_spec(ts, 128)] * 3,
                                out_shapes=[_sds((T, HW), BF16), _sds((T, 128), F32)], out_specs=[_row_spec(ts, HW), _row_spec(ts, 128)])
            g_uq_l = wgrad_rows("mla_q_wgrad", cq, dq_pre, Q_RANK, HW, tm)
            g_ukv_l = wgrad_rows("mla_kv_wgrad", ckv, dkv, KV_RANK, HW, tm)
            pending.append(("uq", l, g_uq_l.reshape(Q_RANK, HEADS, QPAD)[:, :, :NOPE + ROPE].transpose(1, 0, 2)))
            pending.append(("ukv", l, g_ukv_l.reshape(KV_RANK, HEADS, NOPE + VDIM).transpose(1, 0, 2)))
            dcq = _matmul("mla_q_bwd", dq_pre, w_uq, [], grid=(nt,), a_spec=_row_spec(tm, HW),
                          b_spec=pl.BlockSpec((None, Q_RANK, HW), lambda i_: (l, 0, 0)), extra_specs=[],
                          out_shapes=[_sds((T, Q_RANK), F32)], out_specs=[_row_spec(tm, Q_RANK)], dims=NT)[0]
            dckv = _matmul("mla_kv_bwd", dkv, w_ukv, [], grid=(nt,), a_spec=_row_spec(tm, HW),
                           b_spec=pl.BlockSpec((None, KV_RANK, HW), lambda i_: (l, 0, 0)), extra_specs=[],
                           out_shapes=[_sds((T, KV_RANK), F32)], out_specs=[_row_spec(tm, KV_RANK)], dims=NT)[0]

            def latent_bwd(la, qn, kvn, dq_, dkv_, dkr_):
                dcq_raw, dqn = _rms_bwd(la[:, :Q_RANK], qn, dq_)
                dckv_raw, dkvn = _rms_bwd(la[:, Q_RANK:Q_RANK + KV_RANK], kvn, dkv_)
                return jnp.concatenate([dcq_raw, dckv_raw, dkr_], axis=1), dqn, dkvn

            dlat, d_qn[l], d_kvn[l] = _rowwise(
                "mla_latent_bwd", latent_bwd, [lat, mla_q_norm[l].reshape(1, Q_RANK), mla_kv_norm[l].reshape(1, KV_RANK), dcq, dckv, dkr],
                grid=(nt,), in_specs=[_row_spec(tm, LAT_PAD), _const_spec((1, Q_RANK)), _const_spec((1, KV_RANK)),
                                      _row_spec(tm, Q_RANK), _row_spec(tm, KV_RANK), _row_spec(tm, 128)],
                out_shapes=[_sds((T, LAT_PAD), BF16), _sds((1, Q_RANK), F32), _sds((1, KV_RANK), F32)],
                out_specs=[_row_spec(tm, LAT_PAD), _const_spec((1, Q_RANK)), _const_spec((1, KV_RANK))], n_acc=2)
            g_dkv_l = wgrad_rows("mla_down_wgrad", h, dlat, D, LAT_PAD, tm)
            pending.append(("dkv", l, g_dkv_l[:, :LAT].reshape(N_DEV, dc, LAT)))
            dx, dyb, d_norm_mix[i] = _matmul(
                "mla_down_bwd", dlat, w_dkv, [xs_i, norm_mix[i].reshape(1, D), dx], grid=(nt,), a_spec=_row_spec(tm, LAT_PAD),
                b_spec=pl.BlockSpec((None, D, LAT_PAD), lambda i_: (l, 0, 0)), extra_specs=[_row_spec(tm, D), _const_spec((1, D)), _row_spec(tm, D)],
                out_shapes=[_sds((T, D), F32), _sds((T, D), BF16), _sds((1, D), F32)],
                out_specs=[_row_spec(tm, D), _row_spec(tm, D), _const_spec((1, D))], dims=NT, epilogue=norm_bwd_epilogue, n_sum=1)
        else:
            h, gp, ge, gate = mix_saved
            dgate = back_rows("sgu_out_bwd", dyb, g_out[l], ec, [], None)
            pending.append(("out", l, wgrad_rows("sgu_out_wgrad", gate, dyb, ec, D, tb).reshape(N_DEV, ec, D)))
            dz, d_wsp[l], d_bsp[l], d_lng[l], d_lnb[l] = _sgu_mid_bwd(ge, gp, dgate, ln_g_full[l], ln_b_full[l], sgu_w_spatial[l], b_sp[l], 2)
            pending.append(("in", l, wgrad_cols("sgu_in_wgrad", h, dz, e2c)))
            dx, dyb, d_norm_mix[i] = back_cols("sgu_in_bwd", dz, transposed(g_in[l]), xs_i, norm_mix[i], dx)
    grad_x = dx.reshape(1, T, D)

    comm, names = chip_comm_of_pending()
    stacked.update(dict(zip(names, _comm_call("grad_chip_exchange", comm))))

    def adam_big(name, parts, w, m, v):
        lyr, rws, cls = w.shape
        rt = _tile(rws, 256)

        def fn(p, w_, m_, v_):
            g = (p[0].astype(F32) + p[1].astype(F32)) + (p[2].astype(F32) + p[3].astype(F32))
            return (g, *_adam(w_, g, m_, v_))

        spec = pl.BlockSpec((None, rt, cls), lambda l_, i_: (l_, i_, 0))
        return _rowwise(name, fn, [parts, w, m, v], grid=(lyr, rws // rt),
                        in_specs=[pl.BlockSpec((N_CHIP, None, rt, cls), lambda l_, i_: (0, l_, i_, 0)), spec, spec, spec],
                        out_shapes=[_sds(w.shape, F32)] * 4, out_specs=[spec] * 4)

    big_names = ["dkv", "uq", "ukv", "o", "in", "out", "up", "down"]
    big_w = [mla_w_dkv, mla_w_uq, mla_w_ukv, mla_w_o, sgu_w_in, sgu_w_out, ffn_w_up, ffn_w_down]
    big_m = [m_mla_w_dkv, m_mla_w_uq, m_mla_w_ukv, m_mla_w_o, m_sgu_w_in, m_sgu_w_out, m_ffn_w_up, m_ffn_w_down]
    big_v = [v_mla_w_dkv, v_mla_w_uq, v_mla_w_ukv, v_mla_w_o, v_sgu_w_in, v_sgu_w_out, v_ffn_w_up, v_ffn_w_down]
    big_res = [adam_big("adam_large", stacked[nm], w, m, v) for nm, w, m, v in zip(big_names, big_w, big_m, big_v)]

    def rows128(a, rows):
        flat = a.reshape(-1, 128)
        return jnp.pad(flat, ((0, rows - flat.shape[0]), (0, 0)))

    def pad8(n):
        return -(-n // 8) * 8

    small_names = ["norm_mix", "norm_ffn", "final_norm", "q_norm", "kv_norm", "w_spatial", "b_spatial"]
    small_g = [jnp.concatenate(d_norm_mix, 0), jnp.concatenate(d_norm_ffn, 0), d_final, jnp.concatenate(d_qn, 0), jnp.concatenate(d_kvn, 0),
               jnp.stack(d_wsp, 0), jnp.stack(d_bsp, 0), jnp.concatenate(d_lng, 0), jnp.concatenate(d_lnb, 0)]
    small_w = [norm_mix, norm_ffn, final_norm, mla_q_norm, mla_kv_norm, sgu_w_spatial, sgu_b_spatial]
    small_m = [m_norm_mix, m_norm_ffn, m_final_norm, m_mla_q_norm, m_mla_kv_norm, m_sgu_w_spatial, m_sgu_b_spatial]
    small_v = [v_norm_mix, v_norm_ffn, v_final_norm, v_mla_q_norm, v_mla_kv_norm, v_sgu_w_spatial, v_sgu_b_spatial]
    sizes = [pad8(g.size // 128) for g in small_g]
    n_rep = len(small_w)
    sizes[n_rep - 1] += -sum(sizes[:n_rep]) % SMALL_ROWS
    offs = [sum(sizes[:k]) for k in range(len(sizes) + 1)]
    rep_rows = offs[n_rep]
    pack_g = jnp.concatenate([rows128(g, sz) for g, sz in zip(small_g, sizes)], axis=0)
    (gathered_small,) = _all_gather("gather_small_grads", [pack_g])

    def pack(arrs):
        return jnp.concatenate([rows128(a, sz) for a, sz in zip(arrs, sizes[:n_rep])], axis=0)

    def sum8(p):
        return ((p[0] + p[1]) + (p[2] + p[3])) + ((p[4] + p[5]) + (p[6] + p[7]))

    sspec = _row_spec(SMALL_ROWS, 128)
    rep_g, rep_d, rep_m, rep_v = _rowwise(
        "adam_small", lambda p, w_, m_, v_: (sum8(p), *_adam(w_, sum8(p), m_, v_)),
        [gathered_small, pack(small_w), pack(small_m), pack(small_v)], grid=(rep_rows // SMALL_ROWS,),
        in_specs=[pl.BlockSpec((N_DEV, SMALL_ROWS, 128), lambda i_: (0, i_, 0)), sspec, sspec, sspec],
        out_shapes=[_sds((rep_rows, 128), F32)] * 4, out_specs=[sspec] * 4)

    def unpack(packed, k, like):
        return packed[offs[k]:offs[k] + like.size // 128].reshape(like.shape)

    my_b = 4 * lax.axis_index("x") + 2 * lax.axis_index("y") + lax.axis_index("c")
    ln_w = jnp.concatenate([sgu_ln_g, sgu_ln_b], 0)
    ln_m = jnp.concatenate([m_sgu_ln_g, m_sgu_ln_b], 0)
    ln_v = jnp.concatenate([v_sgu_ln_g, v_sgu_ln_b], 0)
    ln_all = gathered_small[:, rep_rows:, :]
    ln_mine = lax.dynamic_slice_in_dim(ln_all[:, :2 * n_sgu * E // 128].reshape(N_DEV, 2 * n_sgu, N_DEV, ec), my_b, 1, axis=2).reshape(N_DEV, 2 * n_sgu, ec)
    ln_g_, ln_d, ln_m2, ln_v2 = _rowwise(
        "adam_ln", lambda p, w_, m_, v_: (sum8(p), *_adam(w_, sum8(p), m_, v_)), [ln_mine, ln_w, ln_m, ln_v], grid=(1,),
        in_specs=[_const_spec(ln_mine.shape), _const_spec(ln_w.shape), _const_spec(ln_w.shape), _const_spec(ln_w.shape)],
        out_shapes=[_sds(ln_w.shape, F32)] * 4, out_specs=[_const_spec(ln_w.shape)] * 4)

    def family(pos):
        rep = [rep_g, rep_d, rep_m, rep_v][pos]
        ln = [ln_g_, ln_d, ln_m2, ln_v2][pos]
        small = {nm: unpack(rep, k, w_) for k, (nm, w_) in enumerate(zip(small_names, small_w))}
        big = [res[pos] for res in big_res]
        return [small["norm_mix"], small["norm_ffn"], small["final_norm"], big[0], small["q_norm"], small["kv_norm"], big[1], big[2], big[3],
                big[4], ln[:n_sgu], ln[n_sgu:], small["w_spatial"], small["b_spatial"], big[5], big[6], big[7]]

    return (loss, grad_x, *family(0), *family(1), *family(2), *family(3))
```

```python
import math

import jax
import jax.numpy as jnp
from jax import lax
from jax.experimental import pallas as pl
from jax.experimental.pallas import tpu as pltpu

F32 = jnp.float32
BF16 = jnp.bfloat16
MESH = pl.DeviceIdType.MESH

N_DEV = 8
N_CHIP = 4
HEADS = 8
NOPE = 128
ROPE = 64
VDIM = 128
QPAD = 256
Q_RANK = 256
KV_RANK = 128
LAT = Q_RANK + KV_RANK + ROPE
LAT_PAD = 512
ROPE_THETA = 10000.0
SGU_CHUNK = 128
SGU_GROUPS = 8
NORM_EPS = 1e-6
LN_EPS = 1e-5
ADAM_LR = 0.001
ADAM_B1 = 0.9
ADAM_B2 = 0.999
ADAM_EPS = 1e-08
ADAM_WD = 0.01
ADAM_STEP = 10
ATTN_SCALE = (NOPE + ROPE) ** -0.5
NEG = -1e30
EXP2_SCALE = ATTN_SCALE * math.log2(math.e)
VMEM_LIMIT = 56 * 1024 * 1024
SMALL_ROWS = 256

NN = (((1,), (0,)), ((), ()))
NT = (((1,), (1,)), ((), ()))
TN = (((0,), (0,)), ((), ()))
ANY = pl.BlockSpec(memory_space=pl.ANY)


def _pcall(body, **kw):
    return pl.pallas_call(body, **kw)


def _params(n_grid, side_effects=False):
    return pltpu.CompilerParams(dimension_semantics=("arbitrary",) * n_grid, vmem_limit_bytes=VMEM_LIMIT, has_side_effects=side_effects)


def _sds(shape, dtype):
    return jax.ShapeDtypeStruct(tuple(shape), dtype)


def _tile(n, want):
    t = min(n, want)
    assert n % t == 0, (n, want)
    return t


class _Comm:
    def __init__(self, operands, out_shapes, aliases, scratch, start, finish):
        self.operands, self.out_shapes, self.aliases, self.scratch = operands, out_shapes, aliases, scratch
        self.start, self.finish = start, finish


def _merge_comm(first, second):
    n_in, n_out, n_sc = len(first.operands), len(first.out_shapes), len(first.scratch)
    aliases = dict(first.aliases)
    aliases.update({n_in + k: n_out + v for k, v in second.aliases.items()})

    def start(ins, outs, sems):
        first.start(ins[:n_in], outs[:n_out], sems[:n_sc])
        second.start(ins[n_in:], outs[n_out:], sems[n_sc:])

    def finish(ins, outs, sems):
        first.finish(ins[:n_in], outs[:n_out], sems[:n_sc])
        second.finish(ins[n_in:], outs[n_out:], sems[n_sc:])

    return _Comm([*first.operands, *second.operands], [*first.out_shapes, *second.out_shapes], aliases,
                 [*first.scratch, *second.scratch], start, finish)


def _place():
    return lax.axis_index("x"), lax.axis_index("y"), lax.axis_index("c")


def _other_chips(x, y):
    return [(1 - x, y), (x, 1 - y), (1 - x, 1 - y)]


def _dev_index(dev):
    return 4 * dev[0] + 2 * dev[1] + dev[2]


def _comm_call(name, comm):
    c_in, c_out = len(comm.operands), len(comm.out_shapes)

    def body(*refs):
        ins, outs, sems = refs[:c_in], refs[c_in:c_in + c_out], refs[c_in + c_out:]
        comm.start(ins, outs, sems)
        comm.finish(ins, outs, sems)

    return _pcall(body, name=name, in_specs=[ANY] * c_in, out_specs=[ANY] * c_out, out_shape=comm.out_shapes,
                  scratch_shapes=comm.scratch, input_output_aliases=dict(comm.aliases),
                  compiler_params=pltpu.CompilerParams(has_side_effects=True))(*comm.operands)


def _call(name, body, operands, in_specs, out_shapes, out_specs, scratch, grid, comm=None):
    if comm is None:
        return _pcall(body, name=name, grid=grid, in_specs=in_specs, out_specs=out_specs, out_shape=out_shapes,
                      scratch_shapes=scratch, compiler_params=_params(len(grid)))(*operands)
    n_in, n_out, n_sc = len(operands), len(out_shapes), len(scratch)
    c_in, c_out = len(comm.operands), len(comm.out_shapes)

    def hosted(*refs):
        ins, cins = refs[:n_in], refs[n_in:n_in + c_in]
        o0 = n_in + c_in
        outs, couts = refs[o0:o0 + n_out], refs[o0 + n_out:o0 + n_out + c_out]
        rest = refs[o0 + n_out + c_out:]
        sc, csems = rest[:n_sc], rest[n_sc:]
        first = pl.program_id(0) == 0
        last = pl.program_id(0) == grid[0] - 1
        for d in range(1, len(grid)):
            first = jnp.logical_and(first, pl.program_id(d) == 0)
            last = jnp.logical_and(last, pl.program_id(d) == grid[d] - 1)

        @pl.when(first)
        def _():
            comm.start(cins, couts, csems)

        body(*ins, *outs, *sc)

        @pl.when(last)
        def _():
            comm.finish(cins, couts, csems)

    return _pcall(hosted, name=name, grid=grid, in_specs=[*in_specs, *[ANY] * c_in], out_specs=[*out_specs, *[ANY] * c_out],
                  out_shape=[*out_shapes, *comm.out_shapes], scratch_shapes=[*scratch, *comm.scratch],
                  input_output_aliases={n_in + k: n_out + v for k, v in comm.aliases.items()},
                  compiler_params=_params(len(grid), side_effects=True))(*operands, *comm.operands)


def _gather_level1(shards):
    n = len(shards)

    def copies(ins, outs, sems):
        send_sems, recv_sems, local_sems = sems
        x, y, c = _place()
        me, sibling = (x, y, c), (x, y, 1 - c)
        chips = _other_chips(x, y)

        def copy(a, k, block, to, src=None):
            slot = outs[a].at[_dev_index(block)]
            return pltpu.make_async_remote_copy(src_ref=slot if src is None else src, dst_ref=slot, send_sem=send_sems.at[a, k],
                                                recv_sem=recv_sems.at[a, k], device_id=to, device_id_type=MESH)

        mine = [pltpu.make_async_copy(ins[a], outs[a].at[_dev_index(me)], local_sems.at[a]) for a in range(n)]
        sends = [copy(a, 1 + j, me, (*chip, c), src=ins[a]) for j, chip in enumerate(chips) for a in range(n)]
        sends += [copy(a, 0, me, sibling, src=ins[a]) for a in range(n)]
        recvs = [copy(a, 1 + j, (*chip, c), me) for j, chip in enumerate(chips) for a in range(n)]
        recvs += [copy(a, 0, sibling, me) for a in range(n)]
        return mine, sends, recvs

    def start(ins, outs, sems):
        mine, sends, _ = copies(ins, outs, sems)
        for cp in mine + sends:
            cp.start()

    def finish(ins, outs, sems):
        mine, sends, recvs = copies(ins, outs, sems)
        for cp in recvs:
            cp.wait_recv()
        for cp in sends:
            cp.wait_send()
        for cp in mine:
            cp.wait()

    return _Comm(shards, [_sds((N_DEV, *a.shape), a.dtype) for a in shards], {},
                 [pltpu.SemaphoreType.DMA((n, 4)), pltpu.SemaphoreType.DMA((n, 4)), pltpu.SemaphoreType.DMA((n,))], start, finish)


def _gather_level2(bufs):
    n = len(bufs)

    def copies(outs, sems):
        send_sems, recv_sems = sems
        x, y, c = _place()
        sibling = (x, y, 1 - c)
        sends, recvs = [], []
        for j, chip in enumerate(_other_chips(x, y)):
            for a in range(n):
                have, want = outs[a].at[_dev_index((*chip, c))], outs[a].at[_dev_index((*chip, 1 - c))]
                sends.append(pltpu.make_async_remote_copy(src_ref=have, dst_ref=have, send_sem=send_sems.at[a, j], recv_sem=recv_sems.at[a, j],
                                                          device_id=sibling, device_id_type=MESH))
                recvs.append(pltpu.make_async_remote_copy(src_ref=want, dst_ref=want, send_sem=send_sems.at[a, j], recv_sem=recv_sems.at[a, j],
                                                          device_id=sibling, device_id_type=MESH))
        return sends, recvs

    def start(ins, outs, sems):
        for cp in copies(outs, sems)[0]:
            cp.start()

    def finish(ins, outs, sems):
        sends, recvs = copies(outs, sems)
        for cp in recvs:
            cp.wait_recv()
        for cp in sends:
            cp.wait_send()

    return _Comm(bufs, [_sds(b.shape, b.dtype) for b in bufs], {a: a for a in range(n)},
                 [pltpu.SemaphoreType.DMA((n, 3)), pltpu.SemaphoreType.DMA((n, 3))], start, finish)


def _all_gather(name, arrays):
    n = len(arrays)

    def body(*refs):
        ins = refs[:n]
        outs = refs[n:2 * n]
        send_sems, recv_sems, local_sems = refs[2 * n:]
        x, y, c = _place()
        me, sibling = (x, y, c), (x, y, 1 - c)
        chips = _other_chips(x, y)

        def copy(a, k, block, to, src=None):
            slot = outs[a].at[_dev_index(block)]
            return pltpu.make_async_remote_copy(src_ref=slot if src is None else src, dst_ref=slot, send_sem=send_sems.at[a, k],
                                                recv_sem=recv_sems.at[a, k], device_id=to, device_id_type=MESH)

        mine = [pltpu.make_async_copy(ins[a], outs[a].at[_dev_index(me)], local_sems.at[a]) for a in range(n)]
        for cp in mine:
            cp.start()
        first = []
        for j, chip in enumerate(chips):
            first += [copy(a, 1 + j, me, (*chip, c), src=ins[a]) for a in range(n)]
        first += [copy(a, 0, me, sibling, src=ins[a]) for a in range(n)]
        for cp in first:
            cp.start()
        passed = []
        for j, chip in enumerate(chips):
            for a in range(n):
                copy(a, 1 + j, (*chip, c), me).wait_recv()
                fwd = copy(a, 4 + j, (*chip, c), sibling)
                fwd.start()
                passed.append(fwd)
        for a in range(n):
            copy(a, 0, sibling, me).wait_recv()
            for j, chip in enumerate(chips):
                copy(a, 4 + j, (*chip, 1 - c), me).wait_recv()
        for cp in first + passed:
            cp.wait_send()
        for cp in mine:
            cp.wait()

    return _pcall(
        body, name=name, in_specs=[ANY] * n, out_specs=[ANY] * n,
        out_shape=[_sds((N_DEV, *a.shape), a.dtype) for a in arrays],
        scratch_shapes=[pltpu.SemaphoreType.DMA((n, 7)), pltpu.SemaphoreType.DMA((n, 7)), pltpu.SemaphoreType.DMA((n,))],
        compiler_params=pltpu.CompilerParams(has_side_effects=True),
    )(*arrays)


def _sibling_exchange(grads):
    n = len(grads)

    def start(ins, outs, sems):
        send_sems, recv_sems = sems
        x, y, c = _place()
        for a in range(n):
            for ch in range(N_CHIP):
                pltpu.make_async_remote_copy(src_ref=ins[a].at[2 * ch + 1 - c], dst_ref=outs[a].at[ch], send_sem=send_sems.at[a],
                                             recv_sem=recv_sems.at[a], device_id=(x, y, 1 - c), device_id_type=MESH).start()

    def finish(ins, outs, sems):
        send_sems, recv_sems = sems
        x, y, c = _place()
        for a in range(n):
            pltpu.make_async_remote_copy(src_ref=outs[a], dst_ref=outs[a], send_sem=send_sems.at[a], recv_sem=recv_sems.at[a],
                                         device_id=(x, y, 1 - c), device_id_type=MESH).wait()

    return _Comm(grads, [_sds((N_CHIP, *g.shape[1:]), g.dtype) for g in grads], {},
                 [pltpu.SemaphoreType.DMA((n,)), pltpu.SemaphoreType.DMA((n,))], start, finish)


def _chip_exchange(parts, slots, layers, stacked):
    n = len(parts)
    names = []
    for nm, _ in slots:
        if nm not in names:
            names.append(nm)
    shapes = {nm: _sds((N_CHIP, layers[nm], *parts[a].shape[1:]), parts[a].dtype) for a, (nm, _) in enumerate(slots)}
    kept = [nm for nm in names if stacked.get(nm) is not None]
    aliases = {n + k: names.index(nm) for k, nm in enumerate(kept)}

    def copies(ins, outs, sems):
        send_sems, recv_sems, local_sems = sems
        x, y, c = _place()
        mine = 2 * x + y
        local, sends, recvs = [], [], []
        for a, (nm, l) in enumerate(slots):
            buf = outs[names.index(nm)]
            local.append(pltpu.make_async_copy(ins[a].at[mine], buf.at[mine, l], local_sems.at[a]))
            for j, chip in enumerate(_other_chips(x, y)):
                theirs = buf.at[2 * chip[0] + chip[1], l]
                sends.append(pltpu.make_async_remote_copy(src_ref=ins[a].at[2 * chip[0] + chip[1]], dst_ref=buf.at[mine, l], send_sem=send_sems.at[a, j],
                                                          recv_sem=recv_sems.at[a, j], device_id=(*chip, c), device_id_type=MESH))
                recvs.append(pltpu.make_async_remote_copy(src_ref=theirs, dst_ref=theirs, send_sem=send_sems.at[a, j],
                                                          recv_sem=recv_sems.at[a, j], device_id=(*chip, c), device_id_type=MESH))
        return local, sends, recvs

    def start(ins, outs, sems):
        local, sends, _ = copies(ins, outs, sems)
        for cp in local + sends:
            cp.start()

    def finish(ins, outs, sems):
        local, sends, recvs = copies(ins, outs, sems)
        for cp in recvs:
            cp.wait_recv()
        for cp in sends:
            cp.wait_send()
        for cp in local:
            cp.wait()

    comm = _Comm([*parts, *[stacked[nm] for nm in kept]], [shapes[nm] for nm in names], aliases,
                 [pltpu.SemaphoreType.DMA((n, 3)), pltpu.SemaphoreType.DMA((n, 3)), pltpu.SemaphoreType.DMA((n,))], start, finish)
    return comm, names


def _matmul(name, a, b, extras, *, grid, a_spec, b_spec, extra_specs, out_shapes, out_specs, dims, k_axis=None, nk=1,
            acc_shape=None, epilogue=None, comm=None, n_sum=0, write=None):
    n_extra = len(extras)
    n_out = len(out_shapes)

    def body(*refs):
        a_ref, b_ref = refs[0], refs[1]
        ex = refs[2:2 + n_extra]
        outs = refs[2 + n_extra:2 + n_extra + n_out]
        prod = lax.dot_general(a_ref[...], b_ref[...], dims, preferred_element_type=F32)

        def finish(acc):
            if write is not None:
                write(outs, acc, *[e[...] for e in ex])
                return
            res = epilogue(acc, *[e[...] for e in ex]) if epilogue is not None else (acc,)
            first = None
            for d in range(len(grid)):
                if d != k_axis:
                    here = pl.program_id(d) == 0
                    first = here if first is None else jnp.logical_and(first, here)
            for idx, (o, r) in enumerate(zip(outs, res)):
                if idx < n_out - n_sum:
                    o[...] = r.astype(o.dtype)
                else:
                    @pl.when(first)
                    def _(o=o, r=r):
                        o[...] = r.astype(o.dtype)

                    @pl.when(jnp.logical_not(first))
                    def _(o=o, r=r):
                        o[...] += r.astype(o.dtype)

        if k_axis is None:
            finish(prod)
        else:
            acc_ref = refs[-1]
            k = pl.program_id(k_axis)

            @pl.when(k == 0)
            def _():
                acc_ref[...] = prod

            @pl.when(k > 0)
            def _():
                acc_ref[...] += prod

            @pl.when(k == nk - 1)
            def _():
                finish(acc_ref[...])

    scratch = [] if k_axis is None else [pltpu.VMEM(acc_shape, F32)]
    return _call(name, body, [a, b, *extras], [a_spec, b_spec, *extra_specs], list(out_shapes), list(out_specs), scratch, grid, comm)


def _rowwise(name, fn, operands, *, grid, in_specs, out_shapes, out_specs, n_acc=0, grid_spec_prefetch=None, comm=None):
    n_in = len(operands)
    n_out = len(out_shapes)
    n_pre = 0 if grid_spec_prefetch is None else 1

    def body(*refs):
        refs = refs[n_pre:]
        ins = refs[:n_in]
        outs = refs[n_in:n_in + n_out]
        res = fn(*[r[...] for r in ins])
        if not isinstance(res, (tuple, list)):
            res = (res,)
        first = pl.program_id(0) == 0
        for d in range(1, len(grid)):
            first = jnp.logical_and(first, pl.program_id(d) == 0)
        for idx, (o, r) in enumerate(zip(outs, res)):
            if idx < n_out - n_acc:
                o[...] = r.astype(o.dtype)
            else:
                @pl.when(first)
                def _(o=o, r=r):
                    o[...] = r.astype(o.dtype)

                @pl.when(jnp.logical_not(first))
                def _(o=o, r=r):
                    o[...] += r.astype(o.dtype)

    if comm is not None:
        return _call(name, body, list(operands), list(in_specs), list(out_shapes), list(out_specs), [], grid, comm)
    if grid_spec_prefetch is None:
        return _pcall(body, name=name, grid=grid, in_specs=in_specs, out_specs=out_specs, out_shape=out_shapes,
                      compiler_params=_params(len(grid)))(*operands)
    gs = pltpu.PrefetchScalarGridSpec(num_scalar_prefetch=1, grid=grid, in_specs=in_specs, out_specs=out_specs)
    return _pcall(body, name=name, grid_spec=gs, out_shape=out_shapes,
                  compiler_params=_params(len(grid)))(grid_spec_prefetch, *operands)


def _row_spec(tm, w):
    return pl.BlockSpec((tm, w), lambda i: (i, 0))


def _const_spec(shape):
    nd = len(shape)
    return pl.BlockSpec(tuple(shape), lambda *_: (0,) * nd)


def _rms_fwd(x, g):
    r = lax.rsqrt(jnp.mean(x * x, axis=-1, keepdims=True) + NORM_EPS)
    return x * r * g


def _rms_bwd(x, g, dy):
    r = lax.rsqrt(jnp.mean(x * x, axis=-1, keepdims=True) + NORM_EPS)
    xh = x * r
    u = dy * g
    dx = r * (u - xh * jnp.mean(u * xh, axis=-1, keepdims=True))
    dg = jnp.sum(dy * xh, axis=0, keepdims=True)
    return dx, dg


def _gelu_and_grad(z):
    cdf = 0.5 * (1.0 + lax.erf(z * (2.0 ** -0.5)))
    return cdf + z * jnp.exp(-0.5 * z * z) * ((2.0 * math.pi) ** -0.5), z * cdf


def _rope_fwd(x, cc, sa, sb):
    return x * cc + pltpu.roll(x, 96, 1) * sa + pltpu.roll(x, 32, 1) * sb


def _rope_bwd(d, cc, sa, sb):
    return d * cc + pltpu.roll(d * sa, 32, 1) + pltpu.roll(d * sb, 96, 1)


def _adam(w, g, m, v):
    m = ADAM_B1 * m + (1.0 - ADAM_B1) * g
    v = ADAM_B2 * v + (1.0 - ADAM_B2) * (g * g)
    m_hat = m / (1.0 - ADAM_B1 ** ADAM_STEP)
    v_hat = v / (1.0 - ADAM_B2 ** ADAM_STEP)
    delta = -ADAM_LR * (m_hat / (jnp.sqrt(v_hat) + ADAM_EPS) + ADAM_WD * w)
    return delta, m, v


def _flash_fwd(q, k, vt, tq, comm=None):
    h, t = vt.shape[0], q.shape[0]
    nq = t // tq

    def body(q_ref, k_ref, vt_ref, o_ref, lse_ref):
        qi = pl.program_id(1)

        def scores(kj):
            kb = k_ref[pl.ds(pl.multiple_of(kj * tq, tq), tq), :]
            return lax.dot_general(kb, q_ref[...], NT, preferred_element_type=F32)

        def step(kj, st, state, masked):
            m_old, l_old, acc_old = state
            if masked:
                key = lax.broadcasted_iota(jnp.int32, (tq, tq), 0)
                qry = lax.broadcasted_iota(jnp.int32, (tq, tq), 1)
                st = jnp.where(key <= qry, st, NEG)
            m_new = jnp.maximum(m_old, jnp.max(st, axis=0, keepdims=True))
            alpha = jnp.exp2((m_old - m_new) * EXP2_SCALE)
            pt = jnp.exp2((st - m_new) * EXP2_SCALE)
            l_new = alpha * l_old + jnp.sum(pt, axis=0, keepdims=True)
            acc_new = alpha * acc_old + lax.dot_general(vt_ref[kj], pt.astype(BF16), NN, preferred_element_type=F32)
            return m_new, l_new, acc_new

        init = (jnp.full((1, tq), NEG, F32), jnp.zeros((1, tq), F32), jnp.zeros((VDIM, tq), F32))
        state = lax.fori_loop(0, qi, lambda kj, st_: step(kj, scores(kj), st_, False), init)
        m, l, acc = step(qi, scores(qi), state, True)
        o_ref[...] = (acc / l).T.astype(o_ref.dtype)
        lse_ref[...] = m * EXP2_SCALE + jnp.log2(l)

    return _call(
        "flash_fwd", body, [q, k, vt],
        [pl.BlockSpec((tq, QPAD), lambda hh, i: (i, hh)),
         pl.BlockSpec((t, QPAD), lambda hh, i: (0, hh)),
         pl.BlockSpec((None, nq, VDIM, tq), lambda hh, i: (hh, 0, 0, 0))],
        [_sds((t, h * VDIM), BF16), _sds((h, nq, 1, tq), F32)],
        [pl.BlockSpec((tq, VDIM), lambda hh, i: (i, hh)),
         pl.BlockSpec((None, None, 1, tq), lambda hh, i: (hh, i, 0, 0))],
        [], (h, nq), comm)


def _flash_bwd(q, k, v, o, do, lse, tabs, tq, comm=None):
    t = q.shape[0]
    h = q.shape[1] // QPAD
    nq = t // tq

    def body(q_ref, k_ref, v_ref, o_ref, do_ref, lse_ref, cc_ref, sa_ref, sb_ref, dq_ref, dk_ref, dv_ref, delta_ref, dqt_ref):
        kj = pl.program_id(1)

        @pl.when(kj == 0)
        def _():
            dqt_ref[...] = jnp.zeros_like(dqt_ref)
            ones = jnp.ones((8, VDIM), BF16)
            for qi in range(nq):
                rows = pl.ds(qi * tq, tq)
                prod = do_ref[rows, :].astype(F32) * o_ref[rows, :].astype(F32)
                hi = prod.astype(BF16)
                lo = (prod - hi.astype(F32)).astype(BF16)
                delta_ref[qi] = (lax.dot_general(ones, hi, NT, preferred_element_type=F32)
                                 + lax.dot_general(ones, lo, NT, preferred_element_type=F32))

        kb = k_ref[...]
        vb = v_ref[...]
        kbt = kb.astype(F32).T.astype(BF16)
        dk_ref[...] = jnp.zeros_like(dk_ref)
        dv_ref[...] = jnp.zeros_like(dv_ref)

        def step(qi, masked):
            rows = pl.ds(pl.multiple_of(qi * tq, tq), tq)
            qb = q_ref[rows, :]
            dob = do_ref[rows, :]
            st = lax.dot_general(kb, qb, NT, preferred_element_type=F32)
            pt = jnp.exp2(st * EXP2_SCALE - lse_ref[qi])
            if masked:
                key = lax.broadcasted_iota(jnp.int32, (tq, tq), 0)
                qry = lax.broadcasted_iota(jnp.int32, (tq, tq), 1)
                pt = jnp.where(key <= qry, pt, 0.0)
            dv_ref[...] += lax.dot_general(pt.astype(BF16), dob, NN, preferred_element_type=F32)
            dpt = lax.dot_general(vb, dob, NT, preferred_element_type=F32)
            dst = (pt * (dpt - delta_ref[qi, pl.ds(0, 1), :]) * ATTN_SCALE).astype(BF16)
            dk_ref[...] += lax.dot_general(dst, qb, NN, preferred_element_type=F32)
            dqt_ref[qi] += lax.dot_general(kbt, dst, NN, preferred_element_type=F32)

        step(kj, True)

        def loop_body(qi, carry):
            step(qi, False)
            return carry

        lax.fori_loop(kj + 1, nq, loop_body, 0)

        @pl.when(kj == nq - 1)
        def _():
            for qi in range(nq):
                rows = pl.ds(qi * tq, tq)
                d = dqt_ref[qi].T
                roped = _rope_bwd(d[:, NOPE:], cc_ref[rows, :], sa_ref[rows, :], sb_ref[rows, :])
                dq_ref[rows, :] = jnp.concatenate([d[:, :NOPE], roped], axis=1).astype(BF16)

    head_q = pl.BlockSpec((t, QPAD), lambda hh, j: (0, hh))
    head_v = pl.BlockSpec((t, VDIM), lambda hh, j: (0, hh))
    table = pl.BlockSpec((t, 128), lambda hh, j: (0, 0))
    return _call(
        "flash_bwd", body, [q, k, v, o, do, lse, *tabs],
        [head_q, pl.BlockSpec((tq, QPAD), lambda hh, j: (j, hh)), pl.BlockSpec((tq, VDIM), lambda hh, j: (j, hh)), head_v, head_v,
         pl.BlockSpec((None, nq, 1, tq), lambda hh, j: (hh, 0, 0, 0)), table, table, table],
        [_sds((t, h * QPAD), BF16), _sds((t, h * QPAD), F32), _sds((t, h * VDIM), F32)],
        [head_q, pl.BlockSpec((tq, QPAD), lambda hh, j: (j, hh)), pl.BlockSpec((tq, VDIM), lambda hh, j: (j, hh))],
        [pltpu.VMEM((nq, 8, tq), F32), pltpu.VMEM((nq, QPAD, tq), F32)], (h, nq), comm)


def _tril_bf16(w):
    row = lax.broadcasted_iota(jnp.int32, w.shape, 0)
    col = lax.broadcasted_iota(jnp.int32, w.shape, 1)
    return jnp.where(col <= row, w, 0.0).astype(BF16)


def _layer_norm_parts(v0):
    mu = jnp.mean(v0, axis=-1, keepdims=True)
    vc = v0 - mu
    rstd = lax.rsqrt(jnp.mean(vc * vc, axis=-1, keepdims=True) + LN_EPS)
    return vc * rstd, rstd


def _sgu_mid_fwd(ge, ln_g, ln_b, w_sp, b_sp, chunks_per_step):
    t, e2 = ge.shape
    e = e2 // 2
    gd = e // SGU_GROUPS
    rows = SGU_CHUNK * chunks_per_step

    def body(u_ref, v_ref, g_ref, b_ref, w_ref, bs_ref, gate_ref):
        for ck in range(chunks_per_step):
            r = pl.ds(ck * SGU_CHUNK, SGU_CHUNK)
            xh, _ = _layer_norm_parts(v_ref[r, :].astype(F32))
            v1 = (xh * g_ref[...] + b_ref[...]).astype(BF16)
            for g in range(SGU_GROUPS):
                cols = pl.ds(g * gd, gd)
                mixed = lax.dot_general(_tril_bf16(w_ref[g]), v1[:, g * gd:(g + 1) * gd], NN, preferred_element_type=F32) + bs_ref[g]
                gate_ref[r, cols] = (u_ref[r, cols].astype(F32) * mixed).astype(BF16)

    return _pcall(
        body, name="sgu_mid_fwd", grid=(t // rows,),
        in_specs=[pl.BlockSpec((rows, e), lambda i: (i, 0)), pl.BlockSpec((rows, e), lambda i: (i, 1)),
                  _const_spec((1, e)), _const_spec((1, e)), _const_spec(w_sp.shape), _const_spec(b_sp.shape)],
        out_specs=pl.BlockSpec((rows, e), lambda i: (i, 0)),
        out_shape=_sds((t, e), BF16), compiler_params=_params(1),
    )(ge, ge, ln_g, ln_b, w_sp, b_sp)


def _sgu_mid_bwd(ge, gp, dgate, ln_g, ln_b, w_sp, b_sp, chunks_per_step):
    t, e2 = ge.shape
    e = e2 // 2
    gd = e // SGU_GROUPS
    rows = SGU_CHUNK * chunks_per_step

    def body(u_ref, v_ref, zu_ref, zv_ref, dg_ref, g_ref, b_ref, w_ref, bs_ref, dz_ref, dw_ref, dbs_ref, dlg_ref, dlb_ref):
        @pl.when(pl.program_id(0) == 0)
        def _():
            dw_ref[...] = jnp.zeros_like(dw_ref)
            dbs_ref[...] = jnp.zeros_like(dbs_ref)
            dlg_ref[...] = jnp.zeros_like(dlg_ref)
            dlb_ref[...] = jnp.zeros_like(dlb_ref)

        for ck in range(chunks_per_step):
            r = pl.ds(ck * SGU_CHUNK, SGU_CHUNK)
            xh, rstd = _layer_norm_parts(v_ref[r, :].astype(F32))
            v1 = (xh * g_ref[...] + b_ref[...]).astype(BF16)
            dv1_parts = []
            for g in range(SGU_GROUPS):
                cols = pl.ds(g * gd, gd)
                wc = _tril_bf16(w_ref[g])
                v1g = v1[:, g * gd:(g + 1) * gd]
                mixed = lax.dot_general(wc, v1g, NN, preferred_element_type=F32) + bs_ref[g]
                dgate = dg_ref[r, cols].astype(F32)
                dmixed = dgate * u_ref[r, cols].astype(F32)
                du = dgate * mixed
                dz_ref[r, cols] = (du * zu_ref[r, cols].astype(F32)).astype(BF16)
                dbs_ref[g] += jnp.sum(dmixed, axis=1, keepdims=True)
                dmb = dmixed.astype(BF16)
                dwg = lax.dot_general(dmb, v1g, NT, preferred_element_type=F32)
                row = lax.broadcasted_iota(jnp.int32, dwg.shape, 0)
                col = lax.broadcasted_iota(jnp.int32, dwg.shape, 1)
                dw_ref[g] += jnp.where(col <= row, dwg, 0.0)
                dv1_parts.append(lax.dot_general(wc, dmb, TN, preferred_element_type=F32))
            dv1 = jnp.concatenate(dv1_parts, axis=1)
            dlg_ref[...] += jnp.sum(dv1 * xh, axis=0, keepdims=True)
            dlb_ref[...] += jnp.sum(dv1, axis=0, keepdims=True)
            dxh = dv1 * g_ref[...]
            dv0 = rstd * (dxh - jnp.mean(dxh, axis=-1, keepdims=True) - xh * jnp.mean(dxh * xh, axis=-1, keepdims=True))
            dz_ref[r, pl.ds(e, e)] = (dv0 * zv_ref[r, :].astype(F32)).astype(BF16)

    half0 = pl.BlockSpec((rows, e), lambda i: (i, 0))
    half1 = pl.BlockSpec((rows, e), lambda i: (i, 1))
    return _pcall(
        body, name="sgu_mid_bwd", grid=(t // rows,),
        in_specs=[half0, half1, half0, half1, half0, _const_spec((1, e)), _const_spec((1, e)), _const_spec(w_sp.shape), _const_spec(b_sp.shape)],
        out_specs=[pl.BlockSpec((rows, e2), lambda i: (i, 0)), _const_spec(w_sp.shape), _const_spec(b_sp.shape), _const_spec((1, e)), _const_spec((1, e))],
        out_shape=[_sds((t, e2), BF16), _sds(w_sp.shape, F32), _sds(b_sp.shape, F32), _sds((1, e), F32), _sds((1, e), F32)],
        compiler_params=_params(1),
    )(ge, ge, gp, gp, dgate, ln_g, ln_b, w_sp, b_sp)


def kernel(x, positions, norm_mix, norm_ffn, final_norm, mla_w_dkv, mla_q_norm, mla_kv_norm, mla_w_uq, mla_w_ukv, mla_w_o, sgu_w_in, sgu_ln_g, sgu_ln_b, sgu_w_spatial, sgu_b_spatial, sgu_w_out, ffn_w_up, ffn_w_down, loss_target, m_norm_mix, m_norm_ffn, m_final_norm, m_mla_w_dkv, m_mla_q_norm, m_mla_kv_norm, m_mla_w_uq, m_mla_w_ukv, m_mla_w_o, m_sgu_w_in, m_sgu_ln_g, m_sgu_ln_b, m_sgu_w_spatial, m_sgu_b_spatial, m_sgu_w_out, m_ffn_w_up, m_ffn_w_down, v_norm_mix, v_norm_ffn, v_final_norm, v_mla_w_dkv, v_mla_q_norm, v_mla_kv_norm, v_mla_w_uq, v_mla_w_ukv, v_mla_w_o, v_sgu_w_in, v_sgu_ln_g, v_sgu_ln_b, v_sgu_w_spatial, v_sgu_b_spatial, v_sgu_w_out, v_ffn_w_up, v_ffn_w_down):
    _, T, D = x.shape
    depth = norm_mix.shape[0]
    n_mla, n_sgu = mla_w_dkv.shape[0], sgu_w_in.shape[0]
    assert depth % 2 == 0
    FF = ffn_w_up.shape[2] * N_DEV
    E = sgu_w_out.shape[1] * N_DEV
    ffc, ec, e2c = FF // N_DEV, E // N_DEV, 2 * E // N_DEV
    dc = D // N_DEV
    OW = HEADS * VDIM
    HW = HEADS * QPAD
    owc = OW // N_DEV
    tm = _tile(T, 1024)
    tb = _tile(T, 4096)
    tk = _tile(T, 512)
    tq = _tile(T, 512)
    ts = _tile(T, 256)
    nt = T // tm
    x2 = x.reshape(T, D)
    tgt = loss_target.reshape(T, D)
    cidx = lax.axis_index("c").astype(jnp.int32).reshape(1)

    ln_local = jnp.concatenate([sgu_ln_g, sgu_ln_b, jnp.zeros((8 - 2 * n_sgu, ec), F32)], axis=0)
    g_dkv, g_uq, g_ukv, g_o, g_ln = _all_gather(
        "gather_small_weights", [w.astype(BF16) for w in (mla_w_dkv, mla_w_uq, mla_w_ukv, mla_w_o)] + [ln_local])
    w_dkv = jnp.pad(g_dkv.transpose(1, 0, 2, 3).reshape(n_mla, D, LAT), ((0, 0), (0, 0), (0, LAT_PAD - LAT)))
    w_uq = jnp.pad(g_uq, ((0, 0), (0, 0), (0, 0), (0, QPAD - NOPE - ROPE))).transpose(1, 2, 0, 3).reshape(n_mla, Q_RANK, HEADS * QPAD)
    w_ukv = g_ukv.transpose(1, 2, 0, 3).reshape(n_mla, KV_RANK, HEADS * (NOPE + VDIM))
    w_o = g_o.transpose(1, 0, 2, 3).reshape(n_mla, OW, D)
    ln_g_full = [g_ln[:, l, :].reshape(1, E) for l in range(n_sgu)]
    ln_b_full = [g_ln[:, n_sgu + l, :].reshape(1, E) for l in range(n_sgu)]
    b_sp = sgu_b_spatial.reshape(n_sgu, SGU_GROUPS, SGU_CHUNK, 1)
    up_sh = [ffn_w_up[i].astype(BF16) for i in range(depth)]
    down_sh = [ffn_w_down[i].astype(BF16) for i in range(depth)]
    in_sh = [sgu_w_in[l].astype(BF16) for l in range(n_sgu)]
    out_sh = [sgu_w_out[l].astype(BF16) for l in range(n_sgu)]
    g_up, g_down, g_in, g_out = [None] * depth, [None] * depth, [None] * n_sgu, [None] * n_sgu

    inv_freq = ROPE_THETA ** (-jnp.arange(0, ROPE, 2, dtype=F32) / ROPE)
    zeros32 = jnp.zeros((ROPE // 2,), F32)
    inv128 = jnp.concatenate([inv_freq, inv_freq, zeros32, zeros32]).reshape(1, 128)
    sel_a = jnp.concatenate([-jnp.ones((32,), F32), zeros32, zeros32, zeros32]).reshape(1, 128)
    sel_b = jnp.concatenate([zeros32, jnp.ones((32,), F32), zeros32, zeros32]).reshape(1, 128)
    sel_c = jnp.concatenate([jnp.ones((64,), F32), zeros32, zeros32]).reshape(1, 128)

    def rope_tables(pos, inv, sa, sb, sc):
        ang = pos.astype(F32) * inv
        cs, sn = jnp.cos(ang), jnp.sin(ang)
        return cs * sc, sn * sa, sn * sb

    t_cc, t_sa, t_sb = _rowwise(
        "rope_tables", rope_tables, [positions.reshape(T, 1), inv128, sel_a, sel_b, sel_c], grid=(nt,),
        in_specs=[_row_spec(tm, 1)] + [_const_spec((1, 128))] * 4,
        out_shapes=[_sds((T, 128), F32)] * 3, out_specs=[_row_spec(tm, 128)] * 3)
    tab_specs = [_row_spec(tm, 128)] * 3

    def rmsnorm(xv, g):
        return _rowwise("rmsnorm", lambda a, gg: _rms_fwd(a, gg), [xv, g.reshape(1, D)], grid=(nt,),
                        in_specs=[_row_spec(tm, D), _const_spec((1, D))], out_shapes=_sds((T, D), BF16), out_specs=_row_spec(tm, D))

    def proj_cols(name, h, gw, nc, epilogue, n_out, comm=None):
        return _matmul(name, h, gw, [], grid=(N_DEV, T // tb),
                       a_spec=pl.BlockSpec((tb, D), lambda j, i: (i, 0)),
                       b_spec=pl.BlockSpec((None, D, nc), lambda j, i: (j, 0, 0)), extra_specs=[],
                       out_shapes=[_sds((T, nc * N_DEV), BF16)] * n_out, out_specs=[pl.BlockSpec((tb, nc), lambda j, i: (i, j))] * n_out,
                       dims=NN, epilogue=epilogue, comm=comm)

    def residual_norm(acc, xr, g):
        xn = acc + xr
        return xn, _rms_fwd(xn, g)

    def proj_rows_residual(name, a, gw, xres, g_next):
        kk_ = a.shape[1]
        return _matmul(name, a, gw.reshape(kk_, D), [xres, g_next.reshape(1, D)], grid=(T // tk,),
                       a_spec=_row_spec(tk, kk_), b_spec=_const_spec((kk_, D)), extra_specs=[_row_spec(tk, D), _const_spec((1, D))],
                       out_shapes=[_sds((T, D), F32), _sds((T, D), BF16)], out_specs=[_row_spec(tk, D)] * 2,
                       dims=NN, epilogue=residual_norm)

    def back_rows(name, dy, gw, kc, extras, epilogue):
        return _matmul(name, dy, gw, extras, grid=(N_DEV, T // tb),
                       a_spec=pl.BlockSpec((tb, D), lambda j, i: (i, 0)),
                       b_spec=pl.BlockSpec((None, kc, D), lambda j, i: (j, 0, 0)),
                       extra_specs=[pl.BlockSpec((tb, kc), lambda j, i: (i, j))] * len(extras),
                       out_shapes=[_sds((T, kc * N_DEV), BF16)], out_specs=[pl.BlockSpec((tb, kc), lambda j, i: (i, j))],
                       dims=NT, epilogue=epilogue)[0]

    def norm_bwd_epilogue(dh, xv, g, dxi):
        dxn, dg = _rms_bwd(xv, g, dh)
        return dxi + dxn, dxi + dxn, dg

    def transposed(gw):
        return gw.transpose(0, 2, 1).reshape(gw.shape[0] * gw.shape[2], D)

    def back_cols(name, da, gwt, xv, g, dx_in):
        n = da.shape[1]
        row = _row_spec(tk, D)
        return _matmul(name, da, gwt, [xv, g.reshape(1, D), dx_in], grid=(T // tk,),
                       a_spec=_row_spec(tk, n), b_spec=_const_spec((n, D)), extra_specs=[row, _const_spec((1, D)), row],
                       out_shapes=[_sds((T, D), F32), _sds((T, D), BF16), _sds((1, D), F32)], out_specs=[row, row, _const_spec((1, D))],
                       dims=NN, epilogue=norm_bwd_epilogue, n_sum=1)

    def token_sum(tt):
        return dict(k_axis=1, nk=T // tt) if T // tt > 1 else dict(k_axis=None)

    def wgrad_cols(name, h, da, nc):
        return _matmul(name, h, da, [], grid=(N_DEV, T // tb),
                       a_spec=pl.BlockSpec((tb, D), lambda j, t: (t, 0)), b_spec=pl.BlockSpec((tb, nc), lambda j, t: (t, j)),
                       extra_specs=[], out_shapes=[_sds((N_DEV, D, nc), BF16)],
                       out_specs=[pl.BlockSpec((None, D, nc), lambda j, t: (j, 0, 0))],
                       dims=TN, acc_shape=(D, nc), **token_sum(tb))[0]

    def wgrad_rows(name, a, dy, kc, ncols, tt):
        return _matmul(name, a, dy, [], grid=(a.shape[1] // kc, T // tt),
                       a_spec=pl.BlockSpec((tt, kc), lambda j, t: (t, j)), b_spec=pl.BlockSpec((tt, ncols), lambda j, t: (t, 0)),
                       extra_specs=[], out_shapes=[_sds((a.shape[1], ncols), BF16)],
                       out_specs=[pl.BlockSpec((kc, ncols), lambda j, t: (j, 0))],
                       dims=TN, acc_shape=(kc, ncols), **token_sum(tt))[0]

    saved = []
    xs = x2
    for i in range(depth):
        l = i // 2
        if i == 0:
            h = rmsnorm(xs, norm_mix[0])
        if i % 2 == 0:
            lat = _matmul("mla_down", h, w_dkv, [], grid=(nt,), a_spec=_row_spec(tm, D),
                          b_spec=pl.BlockSpec((None, D, LAT_PAD), lambda i_: (l, 0, 0)), extra_specs=[],
                          out_shapes=[_sds((T, LAT_PAD), F32)], out_specs=[_row_spec(tm, LAT_PAD)], dims=NN)[0]

            def latent_post(la, qn, kvn, cc, sa, sb):
                cq = _rms_fwd(la[:, :Q_RANK], qn)
                ckv = _rms_fwd(la[:, Q_RANK:Q_RANK + KV_RANK], kvn)
                kr = _rope_fwd(la[:, Q_RANK + KV_RANK:], cc, sa, sb)
                return cq, ckv, kr

            cq, ckv, kr = _rowwise(
                "mla_latent", latent_post, [lat, mla_q_norm[l].reshape(1, Q_RANK), mla_kv_norm[l].reshape(1, KV_RANK), t_cc, t_sa, t_sb],
                grid=(nt,), in_specs=[_row_spec(tm, LAT_PAD), _const_spec((1, Q_RANK)), _const_spec((1, KV_RANK))] + tab_specs,
                out_shapes=[_sds((T, Q_RANK), BF16), _sds((T, KV_RANK), BF16), _sds((T, 128), BF16)],
                out_specs=[_row_spec(tm, Q_RANK), _row_spec(tm, KV_RANK), _row_spec(tm, 128)])

            def q_epilogue(acc, cc, sa, sb):
                parts = []
                for b in range(HEADS):
                    parts += [acc[:, b * QPAD:b * QPAD + NOPE], _rope_fwd(acc[:, b * QPAD + NOPE:(b + 1) * QPAD], cc, sa, sb)]
                return (jnp.concatenate(parts, axis=1),)

            q = _matmul("mla_q", cq, w_uq, [t_cc, t_sa, t_sb], grid=(nt,), a_spec=_row_spec(tm, Q_RANK),
                        b_spec=pl.BlockSpec((None, Q_RANK, HW), lambda i_: (l, 0, 0)), extra_specs=tab_specs,
                        out_shapes=[_sds((T, HW), BF16)], out_specs=[_row_spec(tm, HW)], dims=NN, epilogue=q_epilogue)[0]

            def kv_write(outs, acc, krb):
                k_ref, v_ref, vt_ref = outs
                for b in range(HEADS):
                    vb = acc[:, b * QPAD + NOPE:(b + 1) * QPAD]
                    k_ref[:, b * QPAD:b * QPAD + NOPE] = acc[:, b * QPAD:b * QPAD + NOPE].astype(BF16)
                    k_ref[:, b * QPAD + NOPE:(b + 1) * QPAD] = krb
                    v_ref[:, b * VDIM:(b + 1) * VDIM] = vb.astype(BF16)
                    vbt = vb.T.astype(BF16)
                    for u in range(tm // tq):
                        vt_ref[b, u] = vbt[:, u * tq:(u + 1) * tq]

            kk, vv, vt = _matmul("mla_kv", ckv, w_ukv, [kr], grid=(nt,), a_spec=_row_spec(tm, KV_RANK),
                                 b_spec=pl.BlockSpec((None, KV_RANK, HW), lambda i_: (l, 0, 0)), extra_specs=[_row_spec(tm, 128)],
                                 out_shapes=[_sds((T, HW), BF16), _sds((T, OW), BF16), _sds((HEADS, T // tq, VDIM, tq), BF16)],
                                 out_specs=[_row_spec(tm, HW), _row_spec(tm, OW), pl.BlockSpec((HEADS, tm // tq, VDIM, tq), lambda i_: (0, i_, 0, 0))],
                                 dims=NN, write=kv_write)
            group = [up_sh[i], down_sh[i], in_sh[l], out_sh[l], up_sh[i + 1], down_sh[i + 1]]
            o, lse, *bufs = _flash_fwd(q, kk, vt, tq, comm=_gather_level1(group))
            xm, h2, g_up[i], g_down[i] = _matmul(
                "mla_out", o, w_o, [xs, norm_ffn[i].reshape(1, D)], grid=(nt,), a_spec=_row_spec(tm, OW),
                b_spec=pl.BlockSpec((None, OW, D), lambda i_: (l, 0, 0)), extra_specs=[_row_spec(tm, D), _const_spec((1, D))],
                out_shapes=[_sds((T, D), F32), _sds((T, D), BF16)], out_specs=[_row_spec(tm, D)] * 2, dims=NN,
                epilogue=residual_norm, comm=_gather_level2(bufs[:2]))
            half_gathered = bufs[2:]
            mix_saved = (h, lat, cq, ckv, q, kk, vv, o, lse)
        else:
            gp, ge = proj_cols("sgu_in", h, g_in[l], e2c, _gelu_and_grad, 2)
            gate = _sgu_mid_fwd(ge, ln_g_full[l], ln_b_full[l], sgu_w_spatial[l], b_sp[l], 4)
            xm, h2 = proj_rows_residual("sgu_out", gate, g_out[l], xs, norm_ffn[i])
            mix_saved = (h, gp, ge, gate)
        r, s, *rest = proj_cols("ffn_up", h2, g_up[i], ffc, lambda acc: (jnp.maximum(acc, 0.0), jnp.square(jnp.maximum(acc, 0.0))), 2,
                                comm=_gather_level2(half_gathered) if i % 2 == 0 else None)
        if i % 2 == 0:
            g_in[l], g_out[l], g_up[i + 1], g_down[i + 1] = rest
        xo, h_next = proj_rows_residual("ffn_down", s, g_down[i], xm, norm_mix[i + 1] if i + 1 < depth else final_norm)
        saved.append((xs, xm, mix_saved, h2, r, s))
        xs, h = xo, h_next

    def loss_head(xv, tg, g):
        y = _rms_fwd(xv, g)
        err = y - tg
        part = 0.5 * jnp.sum(jnp.sum(err * err, axis=-1, keepdims=True), axis=0, keepdims=True) / D
        dx, dg = _rms_bwd(xv, g, err / D)
        return dx, dx, jnp.broadcast_to(part, (1, 128)), dg

    dx, dyb, loss_part, d_final = _rowwise(
        "loss_head", loss_head, [xs, tgt, final_norm.reshape(1, D)], grid=(nt,),
        in_specs=[_row_spec(tm, D), _row_spec(tm, D), _const_spec((1, D))],
        out_shapes=[_sds((T, D), F32), _sds((T, D), BF16), _sds((1, 128), F32), _sds((1, D), F32)],
        out_specs=[_row_spec(tm, D), _row_spec(tm, D), _const_spec((1, 128)), _const_spec((1, D))], n_acc=2)
    loss = lax.psum(loss_part[0, 0], ("x", "y", "c"))

    d_norm_mix, d_norm_ffn = [None] * depth, [None] * depth
    d_qn, d_kvn = [None] * n_mla, [None] * n_mla
    d_wsp, d_bsp, d_lng, d_lnb = [None] * n_sgu, [None] * n_sgu, [None] * n_sgu, [None] * n_sgu
    layers = {"dkv": n_mla, "uq": n_mla, "ukv": n_mla, "o": n_mla, "in": n_sgu, "out": n_sgu, "up": depth, "down": depth}
    stacked = {nm: None for nm in layers}
    pending = []

    def add_pair(g, rcv):
        _, rws, cls = g.shape
        g4 = g.reshape(N_CHIP, 2, rws, cls)
        rt = _tile(rws, 512)
        return _rowwise("grad_pair_sum", lambda a, b_: a.astype(F32) + b_.astype(F32), [g4, rcv], grid=(N_CHIP, rws // rt),
                        in_specs=[pl.BlockSpec((None, None, rt, cls), lambda ch, i_, cr: (ch, cr[0], i_, 0)),
                                  pl.BlockSpec((None, rt, cls), lambda ch, i_, cr: (ch, i_, 0))],
                        out_shapes=_sds(rcv.shape, BF16), out_specs=pl.BlockSpec((None, rt, cls), lambda ch, i_, cr: (ch, i_, 0)),
                        grid_spec_prefetch=cidx)

    def chip_comm_of_pending():
        grads = [g for _, _, g in pending]
        from_sibling = _comm_call("grad_sibling_exchange", _sibling_exchange(grads))
        parts = [add_pair(g, rcv) for g, rcv in zip(grads, from_sibling)]
        comm, names = _chip_exchange(parts, [(nm, l_) for nm, l_, _ in pending], layers, stacked)
        pending.clear()
        return comm, names

    def rows128(a, rows):
        flat = a.reshape(-1, 128)
        return jnp.pad(flat, ((0, rows - flat.shape[0]), (0, 0)))

    def pad_to(n, mult):
        return -(-n // mult) * mult

    def packed(arrs, sizes):
        return jnp.concatenate([rows128(a, sz) for a, sz in zip(arrs, sizes)], axis=0)

    n_wsp, n_bsp, n_ln = sgu_w_spatial.size // 128, pad_to(sgu_b_spatial.size // 128, 8), pad_to(n_sgu * E // 128, 8)
    early_sizes = [n_wsp, pad_to(n_wsp + n_bsp, SMALL_ROWS) - n_wsp, n_ln, n_ln]
    early_rep = early_sizes[0] + early_sizes[1]
    gathered_early = None

    for i in reversed(range(depth)):
        l = i // 2
        xs_i, xm, mix_saved, h2, r, s = saved[i]
        da = back_rows("ffn_down_bwd", dyb, g_down[i], ffc, [r], lambda acc, rr: (acc * (2.0 * rr.astype(F32)),))
        pending.append(("down", i, wgrad_rows("ffn_down_wgrad", s, dyb, ffc, D, tb).reshape(N_DEV, ffc, D)))
        pending.append(("up", i, wgrad_cols("ffn_up_wgrad", h2, da, ffc)))
        dx, dyb, d_norm_ffn[i] = back_cols("ffn_up_bwd", da, transposed(g_up[i]), xm, norm_ffn[i], dx)
        if i % 2 == 0:
            h, lat, cq, ckv, q, kk, vv, o, lse = mix_saved
            do = _matmul("mla_out_bwd", dyb, w_o, [], grid=(nt,), a_spec=_row_spec(tm, D),
                         b_spec=pl.BlockSpec((None, OW, D), lambda i_: (l, 0, 0)), extra_specs=[],
                         out_shapes=[_sds((T, OW), BF16)], out_specs=[_row_spec(tm, OW)], dims=NT)[0]
            g_o_l = wgrad_rows("mla_out_wgrad", o, dyb, OW, D, tm).reshape(N_DEV, owc, D)
            comm, names = chip_comm_of_pending()
            if i == 0:
                early = packed([jnp.stack(d_wsp, 0), jnp.stack(d_bsp, 0), jnp.concatenate(d_lng, 0), jnp.concatenate(d_lnb, 0)], early_sizes)
                comm = _merge_comm(comm, _gather_level1([early]))
            dq_pre, dk, dv, *bufs = _flash_bwd(q, kk, vv, o, do, lse, (t_cc, t_sa, t_sb), tq, comm=comm)
            stacked.update(dict(zip(names, bufs)))
            pending.append(("o", l, g_o_l))

            def kv_pre(dkb, dvb, cc, sa, sb):
                parts, dkr = [], None
                for b in range(HEADS):
                    parts += [dkb[:, b * QPAD:b * QPAD + NOPE], dvb[:, b * VDIM:(b + 1) * VDIM]]
                    piece = dkb[:, b * QPAD + NOPE:(b + 1) * QPAD]
                    dkr = piece if dkr is None else dkr + piece
                return jnp.concatenate(parts, axis=1), _rope_bwd(dkr, cc, sa, sb)

            dkv, dkr, *rest = _rowwise("mla_dkv_rope", kv_pre, [dk, dv, t_cc, t_sa, t_sb], grid=(T // ts,),
                                       in_specs=[_row_spec(ts, HW), _row_spec(ts, OW)] + [_row_spec(ts, 128)] * 3,
                                       out_shapes=[_sds((T, HW), BF16), _sds((T, 128), F32)], out_specs=[_row_spec(ts, HW), _row_spec(ts, 128)],
                                       comm=_gather_level2(bufs[len(names):]) if i == 0 else None)
            if i == 0:
                (gathered_early,) = rest
            g_uq_l = wgrad_rows("mla_q_wgrad", cq, dq_pre, Q_RANK, HW, tm)
            g_ukv_l = wgrad_rows("mla_kv_wgrad", ckv, dkv, KV_RANK, HW, tm)
            pending.append(("uq", l, g_uq_l.reshape(Q_RANK, HEADS, QPAD)[:, :, :NOPE + ROPE].transpose(1, 0, 2)))
            pending.append(("ukv", l, g_ukv_l.reshape(KV_RANK, HEADS, NOPE + VDIM).transpose(1, 0, 2)))
            dcq = _matmul("mla_q_bwd", dq_pre, w_uq, [], grid=(nt,), a_spec=_row_spec(tm, HW),
                          b_spec=pl.BlockSpec((None, Q_RANK, HW), lambda i_: (l, 0, 0)), extra_specs=[],
                          out_shapes=[_sds((T, Q_RANK), F32)], out_specs=[_row_spec(tm, Q_RANK)], dims=NT)[0]
            dckv = _matmul("mla_kv_bwd", dkv, w_ukv, [], grid=(nt,), a_spec=_row_spec(tm, HW),
                           b_spec=pl.BlockSpec((None, KV_RANK, HW), lambda i_: (l, 0, 0)), extra_specs=[],
                           out_shapes=[_sds((T, KV_RANK), F32)], out_specs=[_row_spec(tm, KV_RANK)], dims=NT)[0]

            def latent_bwd(la, qn, kvn, dq_, dkv_, dkr_):
                dcq_raw, dqn = _rms_bwd(la[:, :Q_RANK], qn, dq_)
                dckv_raw, dkvn = _rms_bwd(la[:, Q_RANK:Q_RANK + KV_RANK], kvn, dkv_)
                return jnp.concatenate([dcq_raw, dckv_raw, dkr_], axis=1), dqn, dkvn

            dlat, d_qn[l], d_kvn[l] = _rowwise(
                "mla_latent_bwd", latent_bwd, [lat, mla_q_norm[l].reshape(1, Q_RANK), mla_kv_norm[l].reshape(1, KV_RANK), dcq, dckv, dkr],
                grid=(nt,), in_specs=[_row_spec(tm, LAT_PAD), _const_spec((1, Q_RANK)), _const_spec((1, KV_RANK)),
                                      _row_spec(tm, Q_RANK), _row_spec(tm, KV_RANK), _row_spec(tm, 128)],
                out_shapes=[_sds((T, LAT_PAD), BF16), _sds((1, Q_RANK), F32), _sds((1, KV_RANK), F32)],
                out_specs=[_row_spec(tm, LAT_PAD), _const_spec((1, Q_RANK)), _const_spec((1, KV_RANK))], n_acc=2)
            g_dkv_l = wgrad_rows("mla_down_wgrad", h, dlat, D, LAT_PAD, tm)
            pending.append(("dkv", l, g_dkv_l[:, :LAT].reshape(N_DEV, dc, LAT)))
            dx, dyb, d_norm_mix[i] = _matmul(
                "mla_down_bwd", dlat, w_dkv, [xs_i, norm_mix[i].reshape(1, D), dx], grid=(nt,), a_spec=_row_spec(tm, LAT_PAD),
                b_spec=pl.BlockSpec((None, D, LAT_PAD), lambda i_: (l, 0, 0)), extra_specs=[_row_spec(tm, D), _const_spec((1, D)), _row_spec(tm, D)],
                out_shapes=[_sds((T, D), F32), _sds((T, D), BF16), _sds((1, D), F32)],
                out_specs=[_row_spec(tm, D), _row_spec(tm, D), _const_spec((1, D))], dims=NT, epilogue=norm_bwd_epilogue, n_sum=1)
        else:
            h, gp, ge, gate = mix_saved
            dgate = back_rows("sgu_out_bwd", dyb, g_out[l], ec, [], None)
            pending.append(("out", l, wgrad_rows("sgu_out_wgrad", gate, dyb, ec, D, tb).reshape(N_DEV, ec, D)))
            dz, d_wsp[l], d_bsp[l], d_lng[l], d_lnb[l] = _sgu_mid_bwd(ge, gp, dgate, ln_g_full[l], ln_b_full[l], sgu_w_spatial[l], b_sp[l], 2)
            pending.append(("in", l, wgrad_cols("sgu_in_wgrad", h, dz, e2c)))
            dx, dyb, d_norm_mix[i] = back_cols("sgu_in_bwd", dz, transposed(g_in[l]), xs_i, norm_mix[i], dx)
    grad_x = dx.reshape(1, T, D)

    last_comm, last_names = chip_comm_of_pending()
    late_g = [jnp.concatenate(d_norm_mix, 0), jnp.concatenate(d_norm_ffn, 0), d_final, jnp.concatenate(d_qn, 0), jnp.concatenate(d_kvn, 0)]
    late_w = [norm_mix, norm_ffn, final_norm, mla_q_norm, mla_kv_norm]
    late_m = [m_norm_mix, m_norm_ffn, m_final_norm, m_mla_q_norm, m_mla_kv_norm]
    late_v = [v_norm_mix, v_norm_ffn, v_final_norm, v_mla_q_norm, v_mla_kv_norm]
    late_sizes = [pad_to(g.size // 128, 8) for g in late_g]
    late_rows = sum(late_sizes)

    def adam_big(parts, w, m, v, comm=None):
        lyr, rws, cls = w.shape
        rt = _tile(rws, 256)

        def fn(p, w_, m_, v_):
            g = (p[0].astype(F32) + p[1].astype(F32)) + (p[2].astype(F32) + p[3].astype(F32))
            return (g, *_adam(w_, g, m_, v_))

        spec = pl.BlockSpec((None, rt, cls), lambda l_, i_: (l_, i_, 0))
        return _rowwise("adam_large", fn, [parts, w, m, v], grid=(lyr, rws // rt),
                        in_specs=[pl.BlockSpec((N_CHIP, None, rt, cls), lambda l_, i_: (0, l_, i_, 0)), spec, spec, spec],
                        out_shapes=[_sds(w.shape, F32)] * 4, out_specs=[spec] * 4, comm=comm)

    big = {}
    big["in"] = adam_big(stacked["in"], sgu_w_in, m_sgu_w_in, v_sgu_w_in, comm=last_comm)
    stacked.update(dict(zip(last_names, big["in"][4:])))
    *big["up"], late_half = adam_big(stacked["up"], ffn_w_up, m_ffn_w_up, v_ffn_w_up, comm=_gather_level1([packed(late_g, late_sizes)]))
    *big["down"], gathered_late = adam_big(stacked["down"], ffn_w_down, m_ffn_w_down, v_ffn_w_down, comm=_gather_level2([late_half]))
    big["out"] = adam_big(stacked["out"], sgu_w_out, m_sgu_w_out, v_sgu_w_out)
    big["dkv"] = adam_big(stacked["dkv"], mla_w_dkv, m_mla_w_dkv, v_mla_w_dkv)
    big["uq"] = adam_big(stacked["uq"], mla_w_uq, m_mla_w_uq, v_mla_w_uq)
    big["ukv"] = adam_big(stacked["ukv"], mla_w_ukv, m_mla_w_ukv, v_mla_w_ukv)
    big["o"] = adam_big(stacked["o"], mla_w_o, m_mla_w_o, v_mla_w_o)
    big_res = [big[nm][:4] for nm in ("dkv", "uq", "ukv", "o", "in", "out", "up", "down")]

    def sum8(p):
        return ((p[0] + p[1]) + (p[2] + p[3])) + ((p[4] + p[5]) + (p[6] + p[7]))

    def adam_packed(name, gathered, ws, ms, vs, sizes, rows, tile):
        spec = _row_spec(tile, 128)
        return _rowwise(name, lambda p, w_, m_, v_: (sum8(p), *_adam(w_, sum8(p), m_, v_)),
                        [gathered, packed(ws, sizes), packed(ms, sizes), packed(vs, sizes)], grid=(rows // tile,),
                        in_specs=[pl.BlockSpec((N_DEV, tile, 128), lambda i_: (0, i_, 0)), spec, spec, spec],
                        out_shapes=[_sds((rows, 128), F32)] * 4, out_specs=[spec] * 4)

    late_res = adam_packed("adam_small", gathered_late, late_w, late_m, late_v, late_sizes, late_rows, late_rows)
    early_res = adam_packed("adam_spatial", gathered_early, [sgu_w_spatial, sgu_b_spatial], [m_sgu_w_spatial, m_sgu_b_spatial],
                            [v_sgu_w_spatial, v_sgu_b_spatial], early_sizes[:2], early_rep, SMALL_ROWS)

    def unpack(res, sizes, k, like):
        off = sum(sizes[:k])
        return res[off:off + like.size // 128].reshape(like.shape)

    my_b = 4 * lax.axis_index("x") + 2 * lax.axis_index("y") + lax.axis_index("c")
    ln_w = jnp.concatenate([sgu_ln_g, sgu_ln_b], 0)
    ln_m = jnp.concatenate([m_sgu_ln_g, m_sgu_ln_b], 0)
    ln_v = jnp.concatenate([v_sgu_ln_g, v_sgu_ln_b], 0)
    ln_all = jnp.concatenate([gathered_early[:, early_rep:early_rep + n_sgu * E // 128], gathered_early[:, early_rep + n_ln:early_rep + n_ln + n_sgu * E // 128]], axis=1)
    ln_mine = lax.dynamic_slice_in_dim(ln_all.reshape(N_DEV, 2 * n_sgu, N_DEV, ec), my_b, 1, axis=2).reshape(N_DEV, 2 * n_sgu, ec)
    ln_g_, ln_d, ln_m2, ln_v2 = _rowwise(
        "adam_ln", lambda p, w_, m_, v_: (sum8(p), *_adam(w_, sum8(p), m_, v_)), [ln_mine, ln_w, ln_m, ln_v], grid=(1,),
        in_specs=[_const_spec(ln_mine.shape), _const_spec(ln_w.shape), _const_spec(ln_w.shape), _const_spec(ln_w.shape)],
        out_shapes=[_sds(ln_w.shape, F32)] * 4, out_specs=[_const_spec(ln_w.shape)] * 4)

    def family(pos):
        ln = [ln_g_, ln_d, ln_m2, ln_v2][pos]
        late = [unpack(late_res[pos], late_sizes, k, w_) for k, w_ in enumerate(late_w)]
        w_sp_, b_sp_ = unpack(early_res[pos], early_sizes, 0, sgu_w_spatial), unpack(early_res[pos], early_sizes, 1, sgu_b_spatial)
        bigs = [res[pos] for res in big_res]
        return [late[0], late[1], late[2], bigs[0], late[3], late[4], bigs[1], bigs[2], bigs[3],
                bigs[4], ln[:n_sgu], ln[n_sgu:], w_sp_, b_sp_, bigs[5], bigs[6], bigs[7]]

    return (loss, grad_x, *family(0), *family(1), *family(2), *family(3))
```

```python
import math

import jax
import jax.numpy as jnp
from jax import lax
from jax.experimental import pallas as pl
from jax.experimental.pallas import tpu as pltpu

F32 = jnp.float32
BF16 = jnp.bfloat16
MESH = pl.DeviceIdType.MESH

N_DEV = 8
N_CHIP = 4
HEADS = 8
NOPE = 128
ROPE = 64
VDIM = 128
QPAD = 256
Q_RANK = 256
KV_RANK = 128
LAT = Q_RANK + KV_RANK + ROPE
LAT_PAD = 512
ROPE_THETA = 10000.0
SGU_CHUNK = 128
SGU_GROUPS = 8
NORM_EPS = 1e-6
LN_EPS = 1e-5
ADAM_LR = 0.001
ADAM_B1 = 0.9
ADAM_B2 = 0.999
ADAM_EPS = 1e-08
ADAM_WD = 0.01
ADAM_STEP = 10
ATTN_SCALE = (NOPE + ROPE) ** -0.5
NEG = -1e30
EXP2_SCALE = ATTN_SCALE * math.log2(math.e)
VMEM_LIMIT = 56 * 1024 * 1024
SMALL_ROWS = 256

NN = (((1,), (0,)), ((), ()))
NT = (((1,), (1,)), ((), ()))
TN = (((0,), (0,)), ((), ()))
ANY = pl.BlockSpec(memory_space=pl.ANY)


def _pcall(body, **kw):
    return pl.pallas_call(body, **kw)


def _params(n_grid, side_effects=False):
    return pltpu.CompilerParams(dimension_semantics=("arbitrary",) * n_grid, vmem_limit_bytes=VMEM_LIMIT, has_side_effects=side_effects)


def _sds(shape, dtype):
    return jax.ShapeDtypeStruct(tuple(shape), dtype)


def _tile(n, want):
    t = min(n, want)
    assert n % t == 0, (n, want)
    return t


class _Comm:
    def __init__(self, operands, out_shapes, aliases, scratch, start, finish):
        self.operands, self.out_shapes, self.aliases, self.scratch = operands, out_shapes, aliases, scratch
        self.start, self.finish = start, finish


def _merge_comm(first, second):
    n_in, n_out, n_sc = len(first.operands), len(first.out_shapes), len(first.scratch)
    aliases = dict(first.aliases)
    aliases.update({n_in + k: n_out + v for k, v in second.aliases.items()})

    def start(ins, outs, sems):
        first.start(ins[:n_in], outs[:n_out], sems[:n_sc])
        second.start(ins[n_in:], outs[n_out:], sems[n_sc:])

    def finish(ins, outs, sems):
        first.finish(ins[:n_in], outs[:n_out], sems[:n_sc])
        second.finish(ins[n_in:], outs[n_out:], sems[n_sc:])

    return _Comm([*first.operands, *second.operands], [*first.out_shapes, *second.out_shapes], aliases,
                 [*first.scratch, *second.scratch], start, finish)


def _place():
    return lax.axis_index("x"), lax.axis_index("y"), lax.axis_index("c")


def _other_chips(x, y):
    return [(1 - x, y), (x, 1 - y), (1 - x, 1 - y)]


def _dev_index(dev):
    return 4 * dev[0] + 2 * dev[1] + dev[2]


def _comm_call(name, comm):
    c_in, c_out = len(comm.operands), len(comm.out_shapes)

    def body(*refs):
        ins, outs, sems = refs[:c_in], refs[c_in:c_in + c_out], refs[c_in + c_out:]
        comm.start(ins, outs, sems)
        comm.finish(ins, outs, sems)

    return _pcall(body, name=name, in_specs=[ANY] * c_in, out_specs=[ANY] * c_out, out_shape=comm.out_shapes,
                  scratch_shapes=comm.scratch, input_output_aliases=dict(comm.aliases),
                  compiler_params=pltpu.CompilerParams(has_side_effects=True))(*comm.operands)


def _call(name, body, operands, in_specs, out_shapes, out_specs, scratch, grid, comm=None):
    if comm is None:
        return _pcall(body, name=name, grid=grid, in_specs=in_specs, out_specs=out_specs, out_shape=out_shapes,
                      scratch_shapes=scratch, compiler_params=_params(len(grid)))(*operands)
    n_in, n_out, n_sc = len(operands), len(out_shapes), len(scratch)
    c_in, c_out = len(comm.operands), len(comm.out_shapes)

    def hosted(*refs):
        ins, cins = refs[:n_in], refs[n_in:n_in + c_in]
        o0 = n_in + c_in
        outs, couts = refs[o0:o0 + n_out], refs[o0 + n_out:o0 + n_out + c_out]
        rest = refs[o0 + n_out + c_out:]
        sc, csems = rest[:n_sc], rest[n_sc:]
        first = pl.program_id(0) == 0
        last = pl.program_id(0) == grid[0] - 1
        for d in range(1, len(grid)):
            first = jnp.logical_and(first, pl.program_id(d) == 0)
            last = jnp.logical_and(last, pl.program_id(d) == grid[d] - 1)

        @pl.when(first)
        def _():
            comm.start(cins, couts, csems)

        body(*ins, *outs, *sc)

        @pl.when(last)
        def _():
            comm.finish(cins, couts, csems)

    return _pcall(hosted, name=name, grid=grid, in_specs=[*in_specs, *[ANY] * c_in], out_specs=[*out_specs, *[ANY] * c_out],
                  out_shape=[*out_shapes, *comm.out_shapes], scratch_shapes=[*scratch, *comm.scratch],
                  input_output_aliases={n_in + k: n_out + v for k, v in comm.aliases.items()},
                  compiler_params=_params(len(grid), side_effects=True))(*operands, *comm.operands)


def _gather_level1(shards):
    n = len(shards)

    def copies(ins, outs, sems):
        send_sems, recv_sems, local_sems = sems
        x, y, c = _place()
        me, sibling = (x, y, c), (x, y, 1 - c)
        chips = _other_chips(x, y)

        def copy(a, k, block, to, src=None):
            slot = outs[a].at[_dev_index(block)]
            return pltpu.make_async_remote_copy(src_ref=slot if src is None else src, dst_ref=slot, send_sem=send_sems.at[a, k],
                                                recv_sem=recv_sems.at[a, k], device_id=to, device_id_type=MESH)

        mine = [pltpu.make_async_copy(ins[a], outs[a].at[_dev_index(me)], local_sems.at[a]) for a in range(n)]
        sends = [copy(a, 1 + j, me, (*chip, c), src=ins[a]) for j, chip in enumerate(chips) for a in range(n)]
        sends += [copy(a, 0, me, sibling, src=ins[a]) for a in range(n)]
        recvs = [copy(a, 1 + j, (*chip, c), me) for j, chip in enumerate(chips) for a in range(n)]
        recvs += [copy(a, 0, sibling, me) for a in range(n)]
        return mine, sends, recvs

    def start(ins, outs, sems):
        mine, sends, _ = copies(ins, outs, sems)
        for cp in mine + sends:
            cp.start()

    def finish(ins, outs, sems):
        mine, sends, recvs = copies(ins, outs, sems)
        for cp in recvs:
            cp.wait_recv()
        for cp in sends:
            cp.wait_send()
        for cp in mine:
            cp.wait()

    return _Comm(shards, [_sds((N_DEV, *a.shape), a.dtype) for a in shards], {},
                 [pltpu.SemaphoreType.DMA((n, 4)), pltpu.SemaphoreType.DMA((n, 4)), pltpu.SemaphoreType.DMA((n,))], start, finish)


def _gather_level2(bufs):
    n = len(bufs)

    def copies(outs, sems):
        send_sems, recv_sems = sems
        x, y, c = _place()
        sibling = (x, y, 1 - c)
        sends, recvs = [], []
        for j, chip in enumerate(_other_chips(x, y)):
            for a in range(n):
                have, want = outs[a].at[_dev_index((*chip, c))], outs[a].at[_dev_index((*chip, 1 - c))]
                sends.append(pltpu.make_async_remote_copy(src_ref=have, dst_ref=have, send_sem=send_sems.at[a, j], recv_sem=recv_sems.at[a, j],
                                                          device_id=sibling, device_id_type=MESH))
                recvs.append(pltpu.make_async_remote_copy(src_ref=want, dst_ref=want, send_sem=send_sems.at[a, j], recv_sem=recv_sems.at[a, j],
                                                          device_id=sibling, device_id_type=MESH))
        return sends, recvs

    def start(ins, outs, sems):
        for cp in copies(outs, sems)[0]:
            cp.start()

    def finish(ins, outs, sems):
        sends, recvs = copies(outs, sems)
        for cp in recvs:
            cp.wait_recv()
        for cp in sends:
            cp.wait_send()

    return _Comm(bufs, [_sds(b.shape, b.dtype) for b in bufs], {a: a for a in range(n)},
                 [pltpu.SemaphoreType.DMA((n, 3)), pltpu.SemaphoreType.DMA((n, 3))], start, finish)


def _all_gather(name, arrays):
    n = len(arrays)

    def body(*refs):
        ins = refs[:n]
        outs = refs[n:2 * n]
        send_sems, recv_sems, local_sems = refs[2 * n:]
        x, y, c = _place()
        me, sibling = (x, y, c), (x, y, 1 - c)
        chips = _other_chips(x, y)

        def copy(a, k, block, to, src=None):
            slot = outs[a].at[_dev_index(block)]
            return pltpu.make_async_remote_copy(src_ref=slot if src is None else src, dst_ref=slot, send_sem=send_sems.at[a, k],
                                                recv_sem=recv_sems.at[a, k], device_id=to, device_id_type=MESH)

        mine = [pltpu.make_async_copy(ins[a], outs[a].at[_dev_index(me)], local_sems.at[a]) for a in range(n)]
        for cp in mine:
            cp.start()
        first = []
        for j, chip in enumerate(chips):
            first += [copy(a, 1 + j, me, (*chip, c), src=ins[a]) for a in range(n)]
        first += [copy(a, 0, me, sibling, src=ins[a]) for a in range(n)]
        for cp in first:
            cp.start()
        passed = []
        for j, chip in enumerate(chips):
            for a in range(n):
                copy(a, 1 + j, (*chip, c), me).wait_recv()
                fwd = copy(a, 4 + j, (*chip, c), sibling)
                fwd.start()
                passed.append(fwd)
        for a in range(n):
            copy(a, 0, sibling, me).wait_recv()
            for j, chip in enumerate(chips):
                copy(a, 4 + j, (*chip, 1 - c), me).wait_recv()
        for cp in first + passed:
            cp.wait_send()
        for cp in mine:
            cp.wait()

    return _pcall(
        body, name=name, in_specs=[ANY] * n, out_specs=[ANY] * n,
        out_shape=[_sds((N_DEV, *a.shape), a.dtype) for a in arrays],
        scratch_shapes=[pltpu.SemaphoreType.DMA((n, 7)), pltpu.SemaphoreType.DMA((n, 7)), pltpu.SemaphoreType.DMA((n,))],
        compiler_params=pltpu.CompilerParams(has_side_effects=True),
    )(*arrays)


def _sibling_exchange(grads):
    n = len(grads)

    def start(ins, outs, sems):
        send_sems, recv_sems = sems
        x, y, c = _place()
        for a in range(n):
            for ch in range(N_CHIP):
                pltpu.make_async_remote_copy(src_ref=ins[a].at[2 * ch + 1 - c], dst_ref=outs[a].at[ch], send_sem=send_sems.at[a],
                                             recv_sem=recv_sems.at[a], device_id=(x, y, 1 - c), device_id_type=MESH).start()

    def finish(ins, outs, sems):
        send_sems, recv_sems = sems
        x, y, c = _place()
        for a in range(n):
            pltpu.make_async_remote_copy(src_ref=outs[a], dst_ref=outs[a], send_sem=send_sems.at[a], recv_sem=recv_sems.at[a],
                                         device_id=(x, y, 1 - c), device_id_type=MESH).wait()

    return _Comm(grads, [_sds((N_CHIP, *g.shape[1:]), g.dtype) for g in grads], {},
                 [pltpu.SemaphoreType.DMA((n,)), pltpu.SemaphoreType.DMA((n,))], start, finish)


def _chip_exchange(parts, slots, layers, stacked):
    n = len(parts)
    names = []
    for nm, _ in slots:
        if nm not in names:
            names.append(nm)
    shapes = {nm: _sds((N_CHIP, layers[nm], *parts[a].shape[1:]), parts[a].dtype) for a, (nm, _) in enumerate(slots)}
    kept = [nm for nm in names if stacked.get(nm) is not None]
    aliases = {n + k: names.index(nm) for k, nm in enumerate(kept)}

    def copies(ins, outs, sems):
        send_sems, recv_sems, local_sems = sems
        x, y, c = _place()
        mine = 2 * x + y
        local, sends, recvs = [], [], []
        for a, (nm, l) in enumerate(slots):
            buf = outs[names.index(nm)]
            local.append(pltpu.make_async_copy(ins[a].at[mine], buf.at[mine, l], local_sems.at[a]))
            for j, chip in enumerate(_other_chips(x, y)):
                theirs = buf.at[2 * chip[0] + chip[1], l]
                sends.append(pltpu.make_async_remote_copy(src_ref=ins[a].at[2 * chip[0] + chip[1]], dst_ref=buf.at[mine, l], send_sem=send_sems.at[a, j],
                                                          recv_sem=recv_sems.at[a, j], device_id=(*chip, c), device_id_type=MESH))
                recvs.append(pltpu.make_async_remote_copy(src_ref=theirs, dst_ref=theirs, send_sem=send_sems.at[a, j],
                                                          recv_sem=recv_sems.at[a, j], device_id=(*chip, c), device_id_type=MESH))
        return local, sends, recvs

    def start(ins, outs, sems):
        local, sends, _ = copies(ins, outs, sems)
        for cp in local + sends:
            cp.start()

    def finish(ins, outs, sems):
        local, sends, recvs = copies(ins, outs, sems)
        for cp in recvs:
            cp.wait_recv()
        for cp in sends:
            cp.wait_send()
        for cp in local:
            cp.wait()

    comm = _Comm([*parts, *[stacked[nm] for nm in kept]], [shapes[nm] for nm in names], aliases,
                 [pltpu.SemaphoreType.DMA((n, 3)), pltpu.SemaphoreType.DMA((n, 3)), pltpu.SemaphoreType.DMA((n,))], start, finish)
    return comm, names


def _matmul(name, a, b, extras, *, grid, a_spec, b_spec, extra_specs, out_shapes, out_specs, dims, k_axis=None, nk=1,
            acc_shape=None, epilogue=None, comm=None, n_sum=0, write=None):
    n_extra = len(extras)
    n_out = len(out_shapes)

    def body(*refs):
        a_ref, b_ref = refs[0], refs[1]
        ex = refs[2:2 + n_extra]
        outs = refs[2 + n_extra:2 + n_extra + n_out]
        prod = lax.dot_general(a_ref[...], b_ref[...], dims, preferred_element_type=F32)

        def finish(acc):
            if write is not None:
                write(outs, acc, *[e[...] for e in ex])
                return
            res = epilogue(acc, *[e[...] for e in ex]) if epilogue is not None else (acc,)
            first = None
            for d in range(len(grid)):
                if d != k_axis:
                    here = pl.program_id(d) == 0
                    first = here if first is None else jnp.logical_and(first, here)
            for idx, (o, r) in enumerate(zip(outs, res)):
                if idx < n_out - n_sum:
                    o[...] = r.astype(o.dtype)
                else:
                    @pl.when(first)
                    def _(o=o, r=r):
                        o[...] = r.astype(o.dtype)

                    @pl.when(jnp.logical_not(first))
                    def _(o=o, r=r):
                        o[...] += r.astype(o.dtype)

        if k_axis is None:
            finish(prod)
        else:
            acc_ref = refs[-1]
            k = pl.program_id(k_axis)

            @pl.when(k == 0)
            def _():
                acc_ref[...] = prod

            @pl.when(k > 0)
            def _():
                acc_ref[...] += prod

            @pl.when(k == nk - 1)
            def _():
                finish(acc_ref[...])

    scratch = [] if k_axis is None else [pltpu.VMEM(acc_shape, F32)]
    return _call(name, body, [a, b, *extras], [a_spec, b_spec, *extra_specs], list(out_shapes), list(out_specs), scratch, grid, comm)


def _rowwise(name, fn, operands, *, grid, in_specs, out_shapes, out_specs, n_acc=0, grid_spec_prefetch=None, comm=None):
    n_in = len(operands)
    n_out = len(out_shapes)
    n_pre = 0 if grid_spec_prefetch is None else 1

    def body(*refs):
        refs = refs[n_pre:]
        ins = refs[:n_in]
        outs = refs[n_in:n_in + n_out]
        res = fn(*[r[...] for r in ins])
        if not isinstance(res, (tuple, list)):
            res = (res,)
        first = pl.program_id(0) == 0
        for d in range(1, len(grid)):
            first = jnp.logical_and(first, pl.program_id(d) == 0)
        for idx, (o, r) in enumerate(zip(outs, res)):
            if idx < n_out - n_acc:
                o[...] = r.astype(o.dtype)
            else:
                @pl.when(first)
                def _(o=o, r=r):
                    o[...] = r.astype(o.dtype)

                @pl.when(jnp.logical_not(first))
                def _(o=o, r=r):
                    o[...] += r.astype(o.dtype)

    if comm is not None:
        return _call(name, body, list(operands), list(in_specs), list(out_shapes), list(out_specs), [], grid, comm)
    if grid_spec_prefetch is None:
        return _pcall(body, name=name, grid=grid, in_specs=in_specs, out_specs=out_specs, out_shape=out_shapes,
                      compiler_params=_params(len(grid)))(*operands)
    gs = pltpu.PrefetchScalarGridSpec(num_scalar_prefetch=1, grid=grid, in_specs=in_specs, out_specs=out_specs)
    return _pcall(body, name=name, grid_spec=gs, out_shape=out_shapes,
                  compiler_params=_params(len(grid)))(grid_spec_prefetch, *operands)


def _row_spec(tm, w):
    return pl.BlockSpec((tm, w), lambda i: (i, 0))


def _const_spec(shape):
    nd = len(shape)
    return pl.BlockSpec(tuple(shape), lambda *_: (0,) * nd)


def _rms_fwd(x, g):
    r = lax.rsqrt(jnp.mean(x * x, axis=-1, keepdims=True) + NORM_EPS)
    return x * r * g


def _rms_bwd(x, g, dy):
    r = lax.rsqrt(jnp.mean(x * x, axis=-1, keepdims=True) + NORM_EPS)
    xh = x * r
    u = dy * g
    dx = r * (u - xh * jnp.mean(u * xh, axis=-1, keepdims=True))
    dg = jnp.sum(dy * xh, axis=0, keepdims=True)
    return dx, dg


def _gelu_and_grad(z):
    cdf = 0.5 * (1.0 + lax.erf(z * (2.0 ** -0.5)))
    return cdf + z * jnp.exp(-0.5 * z * z) * ((2.0 * math.pi) ** -0.5), z * cdf


def _rope_fwd(x, cc, sa, sb):
    return x * cc + pltpu.roll(x, 96, 1) * sa + pltpu.roll(x, 32, 1) * sb


def _rope_bwd(d, cc, sa, sb):
    return d * cc + pltpu.roll(d * sa, 32, 1) + pltpu.roll(d * sb, 96, 1)


def _adam(w, g, m, v):
    m = ADAM_B1 * m + (1.0 - ADAM_B1) * g
    v = ADAM_B2 * v + (1.0 - ADAM_B2) * (g * g)
    m_hat = m / (1.0 - ADAM_B1 ** ADAM_STEP)
    v_hat = v / (1.0 - ADAM_B2 ** ADAM_STEP)
    delta = -ADAM_LR * (m_hat / (jnp.sqrt(v_hat) + ADAM_EPS) + ADAM_WD * w)
    return delta, m, v


def _flash_fwd(q, k, vt, tq, comm=None):
    h, t = vt.shape[0], q.shape[0]
    nq = t // tq

    def body(q_ref, k_ref, vt_ref, o_ref, lse_ref):
        qi = pl.program_id(1)

        def scores(kj):
            kb = k_ref[pl.ds(pl.multiple_of(kj * tq, tq), tq), :]
            return lax.dot_general(kb, q_ref[...], NT, preferred_element_type=F32)

        def step(kj, st, state, masked):
            m_old, l_old, acc_old = state
            if masked:
                key = lax.broadcasted_iota(jnp.int32, (tq, tq), 0)
                qry = lax.broadcasted_iota(jnp.int32, (tq, tq), 1)
                st = jnp.where(key <= qry, st, NEG)
            m_new = jnp.maximum(m_old, jnp.max(st, axis=0, keepdims=True))
            alpha = jnp.exp2((m_old - m_new) * EXP2_SCALE)
            pt = jnp.exp2((st - m_new) * EXP2_SCALE)
            l_new = alpha * l_old + jnp.sum(pt, axis=0, keepdims=True)
            acc_new = alpha * acc_old + lax.dot_general(vt_ref[kj], pt.astype(BF16), NN, preferred_element_type=F32)
            return m_new, l_new, acc_new

        init = (jnp.full((1, tq), NEG, F32), jnp.zeros((1, tq), F32), jnp.zeros((VDIM, tq), F32))
        state = lax.fori_loop(0, qi, lambda kj, st_: step(kj, scores(kj), st_, False), init)
        m, l, acc = step(qi, scores(qi), state, True)
        o_ref[...] = (acc / l).T.astype(o_ref.dtype)
        lse_ref[...] = m * EXP2_SCALE + jnp.log2(l)

    return _call(
        "flash_fwd", body, [q, k, vt],
        [pl.BlockSpec((tq, QPAD), lambda hh, i: (i, hh)),
         pl.BlockSpec((t, QPAD), lambda hh, i: (0, hh)),
         pl.BlockSpec((None, nq, VDIM, tq), lambda hh, i: (hh, 0, 0, 0))],
        [_sds((t, h * VDIM), BF16), _sds((h, nq, 1, tq), F32)],
        [pl.BlockSpec((tq, VDIM), lambda hh, i: (i, hh)),
         pl.BlockSpec((None, None, 1, tq), lambda hh, i: (hh, i, 0, 0))],
        [], (h, nq), comm)


def _flash_bwd(q, k, v, o, do, lse, tabs, tq, comm=None):
    t = q.shape[0]
    h = q.shape[1] // QPAD
    nq = t // tq

    def body(q_ref, k_ref, v_ref, o_ref, do_ref, lse_ref, cc_ref, sa_ref, sb_ref, dq_ref, dk_ref, dv_ref, delta_ref, dqt_ref):
        kj = pl.program_id(1)

        @pl.when(kj == 0)
        def _():
            dqt_ref[...] = jnp.zeros_like(dqt_ref)
            ones = jnp.ones((8, VDIM), BF16)
            for qi in range(nq):
                rows = pl.ds(qi * tq, tq)
                prod = do_ref[rows, :].astype(F32) * o_ref[rows, :].astype(F32)
                hi = prod.astype(BF16)
                lo = (prod - hi.astype(F32)).astype(BF16)
                delta_ref[qi] = (lax.dot_general(ones, hi, NT, preferred_element_type=F32)
                                 + lax.dot_general(ones, lo, NT, preferred_element_type=F32))

        kb = k_ref[...]
        vb = v_ref[...]
        kbt = kb.astype(F32).T.astype(BF16)
        dk_ref[...] = jnp.zeros_like(dk_ref)
        dv_ref[...] = jnp.zeros_like(dv_ref)

        def step(qi, masked):
            rows = pl.ds(pl.multiple_of(qi * tq, tq), tq)
            qb = q_ref[rows, :]
            dob = do_ref[rows, :]
            st = lax.dot_general(kb, qb, NT, preferred_element_type=F32)
            pt = jnp.exp2(st * EXP2_SCALE - lse_ref[qi])
            if masked:
                key = lax.broadcasted_iota(jnp.int32, (tq, tq), 0)
                qry = lax.broadcasted_iota(jnp.int32, (tq, tq), 1)
                pt = jnp.where(key <= qry, pt, 0.0)
            dv_ref[...] += lax.dot_general(pt.astype(BF16), dob, NN, preferred_element_type=F32)
            dpt = lax.dot_general(vb, dob, NT, preferred_element_type=F32)
            dst = (pt * (dpt - delta_ref[qi, pl.ds(0, 1), :]) * ATTN_SCALE).astype(BF16)
            dk_ref[...] += lax.dot_general(dst, qb, NN, preferred_element_type=F32)
            dqt_ref[qi] += lax.dot_general(kbt, dst, NN, preferred_element_type=F32)

        step(kj, True)

        def loop_body(qi, carry):
            step(qi, False)
            return carry

        lax.fori_loop(kj + 1, nq, loop_body, 0)

        @pl.when(kj == nq - 1)
        def _():
            for qi in range(nq):
                rows = pl.ds(qi * tq, tq)
                d = dqt_ref[qi].T
                roped = _rope_bwd(d[:, NOPE:], cc_ref[rows, :], sa_ref[rows, :], sb_ref[rows, :])
                dq_ref[rows, :] = jnp.concatenate([d[:, :NOPE], roped], axis=1).astype(BF16)

    head_q = pl.BlockSpec((t, QPAD), lambda hh, j: (0, hh))
    head_v = pl.BlockSpec((t, VDIM), lambda hh, j: (0, hh))
    table = pl.BlockSpec((t, 128), lambda hh, j: (0, 0))
    return _call(
        "flash_bwd", body, [q, k, v, o, do, lse, *tabs],
        [head_q, pl.BlockSpec((tq, QPAD), lambda hh, j: (j, hh)), pl.BlockSpec((tq, VDIM), lambda hh, j: (j, hh)), head_v, head_v,
         pl.BlockSpec((None, nq, 1, tq), lambda hh, j: (hh, 0, 0, 0)), table, table, table],
        [_sds((t, h * QPAD), BF16), _sds((t, h * QPAD), F32), _sds((t, h * VDIM), F32)],
        [head_q, pl.BlockSpec((tq, QPAD), lambda hh, j: (j, hh)), pl.BlockSpec((tq, VDIM), lambda hh, j: (j, hh))],
        [pltpu.VMEM((nq, 8, tq), F32), pltpu.VMEM((nq, QPAD, tq), F32)], (h, nq), comm)


def _tril_bf16(w):
    row = lax.broadcasted_iota(jnp.int32, w.shape, 0)
    col = lax.broadcasted_iota(jnp.int32, w.shape, 1)
    return jnp.where(col <= row, w, 0.0).astype(BF16)


def _layer_norm_parts(v0):
    mu = jnp.mean(v0, axis=-1, keepdims=True)
    vc = v0 - mu
    rstd = lax.rsqrt(jnp.mean(vc * vc, axis=-1, keepdims=True) + LN_EPS)
    return vc * rstd, rstd


def _sgu_mid_fwd(ge, ln_g, ln_b, w_sp, b_sp, chunks_per_step):
    t, e2 = ge.shape
    e = e2 // 2
    gd = e // SGU_GROUPS
    rows = SGU_CHUNK * chunks_per_step

    def body(u_ref, v_ref, g_ref, b_ref, w_ref, bs_ref, gate_ref):
        for ck in range(chunks_per_step):
            r = pl.ds(ck * SGU_CHUNK, SGU_CHUNK)
            xh, _ = _layer_norm_parts(v_ref[r, :].astype(F32))
            v1 = (xh * g_ref[...] + b_ref[...]).astype(BF16)
            for g in range(SGU_GROUPS):
                cols = pl.ds(g * gd, gd)
                mixed = lax.dot_general(_tril_bf16(w_ref[g]), v1[:, g * gd:(g + 1) * gd], NN, preferred_element_type=F32) + bs_ref[g]
                gate_ref[r, cols] = (u_ref[r, cols].astype(F32) * mixed).astype(BF16)

    return _pcall(
        body, name="sgu_mid_fwd", grid=(t // rows,),
        in_specs=[pl.BlockSpec((rows, e), lambda i: (i, 0)), pl.BlockSpec((rows, e), lambda i: (i, 1)),
                  _const_spec((1, e)), _const_spec((1, e)), _const_spec(w_sp.shape), _const_spec(b_sp.shape)],
        out_specs=pl.BlockSpec((rows, e), lambda i: (i, 0)),
        out_shape=_sds((t, e), BF16), compiler_params=_params(1),
    )(ge, ge, ln_g, ln_b, w_sp, b_sp)


def _sgu_mid_bwd(ge, gp, dgate, ln_g, ln_b, w_sp, b_sp, chunks_per_step):
    t, e2 = ge.shape
    e = e2 // 2
    gd = e // SGU_GROUPS
    rows = SGU_CHUNK * chunks_per_step

    def body(u_ref, v_ref, zu_ref, zv_ref, dg_ref, g_ref, b_ref, w_ref, bs_ref, dz_ref, dw_ref, dbs_ref, dlg_ref, dlb_ref):
        @pl.when(pl.program_id(0) == 0)
        def _():
            dw_ref[...] = jnp.zeros_like(dw_ref)
            dbs_ref[...] = jnp.zeros_like(dbs_ref)
            dlg_ref[...] = jnp.zeros_like(dlg_ref)
            dlb_ref[...] = jnp.zeros_like(dlb_ref)

        for ck in range(chunks_per_step):
            r = pl.ds(ck * SGU_CHUNK, SGU_CHUNK)
            xh, rstd = _layer_norm_parts(v_ref[r, :].astype(F32))
            v1 = (xh * g_ref[...] + b_ref[...]).astype(BF16)
            dv1_parts = []
            for g in range(SGU_GROUPS):
                cols = pl.ds(g * gd, gd)
                wc = _tril_bf16(w_ref[g])
                v1g = v1[:, g * gd:(g + 1) * gd]
                mixed = lax.dot_general(wc, v1g, NN, preferred_element_type=F32) + bs_ref[g]
                dgate = dg_ref[r, cols].astype(F32)
                dmixed = dgate * u_ref[r, cols].astype(F32)
                du = dgate * mixed
                dz_ref[r, cols] = (du * zu_ref[r, cols].astype(F32)).astype(BF16)
                dbs_ref[g] += jnp.sum(dmixed, axis=1, keepdims=True)
                dmb = dmixed.astype(BF16)
                dwg = lax.dot_general(dmb, v1g, NT, preferred_element_type=F32)
                row = lax.broadcasted_iota(jnp.int32, dwg.shape, 0)
                col = lax.broadcasted_iota(jnp.int32, dwg.shape, 1)
                dw_ref[g] += jnp.where(col <= row, dwg, 0.0)
                dv1_parts.append(lax.dot_general(wc, dmb, TN, preferred_element_type=F32))
            dv1 = jnp.concatenate(dv1_parts, axis=1)
            dlg_ref[...] += jnp.sum(dv1 * xh, axis=0, keepdims=True)
            dlb_ref[...] += jnp.sum(dv1, axis=0, keepdims=True)
            dxh = dv1 * g_ref[...]
            dv0 = rstd * (dxh - jnp.mean(dxh, axis=-1, keepdims=True) - xh * jnp.mean(dxh * xh, axis=-1, keepdims=True))
            dz_ref[r, pl.ds(e, e)] = (dv0 * zv_ref[r, :].astype(F32)).astype(BF16)

    half0 = pl.BlockSpec((rows, e), lambda i: (i, 0))
    half1 = pl.BlockSpec((rows, e), lambda i: (i, 1))
    return _pcall(
        body, name="sgu_mid_bwd", grid=(t // rows,),
        in_specs=[half0, half1, half0, half1, half0, _const_spec((1, e)), _const_spec((1, e)), _const_spec(w_sp.shape), _const_spec(b_sp.shape)],
        out_specs=[pl.BlockSpec((rows, e2), lambda i: (i, 0)), _const_spec(w_sp.shape), _const_spec(b_sp.shape), _const_spec((1, e)), _const_spec((1, e))],
        out_shape=[_sds((t, e2), BF16), _sds(w_sp.shape, F32), _sds(b_sp.shape, F32), _sds((1, e), F32), _sds((1, e), F32)],
        compiler_params=_params(1),
    )(ge, ge, gp, gp, dgate, ln_g, ln_b, w_sp, b_sp)


def kernel(x, positions, norm_mix, norm_ffn, final_norm, mla_w_dkv, mla_q_norm, mla_kv_norm, mla_w_uq, mla_w_ukv, mla_w_o, sgu_w_in, sgu_ln_g, sgu_ln_b, sgu_w_spatial, sgu_b_spatial, sgu_w_out, ffn_w_up, ffn_w_down, loss_target, m_norm_mix, m_norm_ffn, m_final_norm, m_mla_w_dkv, m_mla_q_norm, m_mla_kv_norm, m_mla_w_uq, m_mla_w_ukv, m_mla_w_o, m_sgu_w_in, m_sgu_ln_g, m_sgu_ln_b, m_sgu_w_spatial, m_sgu_b_spatial, m_sgu_w_out, m_ffn_w_up, m_ffn_w_down, v_norm_mix, v_norm_ffn, v_final_norm, v_mla_w_dkv, v_mla_q_norm, v_mla_kv_norm, v_mla_w_uq, v_mla_w_ukv, v_mla_w_o, v_sgu_w_in, v_sgu_ln_g, v_sgu_ln_b, v_sgu_w_spatial, v_sgu_b_spatial, v_sgu_w_out, v_ffn_w_up, v_ffn_w_down):
    _, T, D = x.shape
    depth = norm_mix.shape[0]
    n_mla, n_sgu = mla_w_dkv.shape[0], sgu_w_in.shape[0]
    assert depth % 2 == 0
    FF = ffn_w_up.shape[2] * N_DEV
    E = sgu_w_out.shape[1] * N_DEV
    ffc, ec, e2c = FF // N_DEV, E // N_DEV, 2 * E // N_DEV
    dc = D // N_DEV
    OW = HEADS * VDIM
    HW = HEADS * QPAD
    owc = OW // N_DEV
    tm = _tile(T, 1024)
    tb = _tile(T, 4096)
    tk = _tile(T, 512)
    tq = _tile(T, 512)
    ts = _tile(T, 256)
    nt = T // tm
    x2 = x.reshape(T, D)
    tgt = loss_target.reshape(T, D)
    cidx = lax.axis_index("c").astype(jnp.int32).reshape(1)

    ln_local = jnp.concatenate([sgu_ln_g, sgu_ln_b, jnp.zeros((8 - 2 * n_sgu, ec), F32)], axis=0)
    g_dkv, g_uq, g_ukv, g_o, g_ln = _all_gather(
        "gather_small_weights", [w.astype(BF16) for w in (mla_w_dkv, mla_w_uq, mla_w_ukv, mla_w_o)] + [ln_local])
    w_dkv = jnp.pad(g_dkv.transpose(1, 0, 2, 3).reshape(n_mla, D, LAT), ((0, 0), (0, 0), (0, LAT_PAD - LAT)))
    w_uq = jnp.pad(g_uq, ((0, 0), (0, 0), (0, 0), (0, QPAD - NOPE - ROPE))).transpose(1, 2, 0, 3).reshape(n_mla, Q_RANK, HEADS * QPAD)
    w_ukv = g_ukv.transpose(1, 2, 0, 3).reshape(n_mla, KV_RANK, HEADS * (NOPE + VDIM))
    w_o = g_o.transpose(1, 0, 2, 3).reshape(n_mla, OW, D)
    ln_g_full = [g_ln[:, l, :].reshape(1, E) for l in range(n_sgu)]
    ln_b_full = [g_ln[:, n_sgu + l, :].reshape(1, E) for l in range(n_sgu)]
    b_sp = sgu_b_spatial.reshape(n_sgu, SGU_GROUPS, SGU_CHUNK, 1)
    up_sh = [ffn_w_up[i].astype(BF16) for i in range(depth)]
    down_sh = [ffn_w_down[i].astype(BF16) for i in range(depth)]
    in_sh = [sgu_w_in[l].astype(BF16) for l in range(n_sgu)]
    out_sh = [sgu_w_out[l].astype(BF16) for l in range(n_sgu)]
    g_up, g_down, g_in, g_out = [None] * depth, [None] * depth, [None] * n_sgu, [None] * n_sgu

    inv_freq = ROPE_THETA ** (-jnp.arange(0, ROPE, 2, dtype=F32) / ROPE)
    zeros32 = jnp.zeros((ROPE // 2,), F32)
    inv128 = jnp.concatenate([inv_freq, inv_freq, zeros32, zeros32]).reshape(1, 128)
    sel_a = jnp.concatenate([-jnp.ones((32,), F32), zeros32, zeros32, zeros32]).reshape(1, 128)
    sel_b = jnp.concatenate([zeros32, jnp.ones((32,), F32), zeros32, zeros32]).reshape(1, 128)
    sel_c = jnp.concatenate([jnp.ones((64,), F32), zeros32, zeros32]).reshape(1, 128)

    def rope_tables(pos, inv, sa, sb, sc):
        ang = pos.astype(F32) * inv
        cs, sn = jnp.cos(ang), jnp.sin(ang)
        return cs * sc, sn * sa, sn * sb

    t_cc, t_sa, t_sb = _rowwise(
        "rope_tables", rope_tables, [positions.reshape(T, 1), inv128, sel_a, sel_b, sel_c], grid=(nt,),
        in_specs=[_row_spec(tm, 1)] + [_const_spec((1, 128))] * 4,
        out_shapes=[_sds((T, 128), F32)] * 3, out_specs=[_row_spec(tm, 128)] * 3)
    tab_specs = [_row_spec(tm, 128)] * 3

    def rmsnorm(xv, g):
        return _rowwise("rmsnorm", lambda a, gg: _rms_fwd(a, gg), [xv, g.reshape(1, D)], grid=(nt,),
                        in_specs=[_row_spec(tm, D), _const_spec((1, D))], out_shapes=_sds((T, D), BF16), out_specs=_row_spec(tm, D))

    def proj_cols(name, h, gw, nc, epilogue, n_out, comm=None):
        return _matmul(name, h, gw, [], grid=(N_DEV, T // tb),
                       a_spec=pl.BlockSpec((tb, D), lambda j, i: (i, 0)),
                       b_spec=pl.BlockSpec((None, D, nc), lambda j, i: (j, 0, 0)), extra_specs=[],
                       out_shapes=[_sds((T, nc * N_DEV), BF16)] * n_out, out_specs=[pl.BlockSpec((tb, nc), lambda j, i: (i, j))] * n_out,
                       dims=NN, epilogue=epilogue, comm=comm)

    def residual_norm(acc, xr, g):
        xn = acc + xr
        return xn, _rms_fwd(xn, g)

    def proj_rows_residual(name, a, gw, xres, g_next):
        kk_ = a.shape[1]
        return _matmul(name, a, gw.reshape(kk_, D), [xres, g_next.reshape(1, D)], grid=(T // tk,),
                       a_spec=_row_spec(tk, kk_), b_spec=_const_spec((kk_, D)), extra_specs=[_row_spec(tk, D), _const_spec((1, D))],
                       out_shapes=[_sds((T, D), F32), _sds((T, D), BF16)], out_specs=[_row_spec(tk, D)] * 2,
                       dims=NN, epilogue=residual_norm)

    def back_rows(name, dy, gw, kc, extras, epilogue):
        return _matmul(name, dy, gw, extras, grid=(N_DEV, T // tb),
                       a_spec=pl.BlockSpec((tb, D), lambda j, i: (i, 0)),
                       b_spec=pl.BlockSpec((None, kc, D), lambda j, i: (j, 0, 0)),
                       extra_specs=[pl.BlockSpec((tb, kc), lambda j, i: (i, j))] * len(extras),
                       out_shapes=[_sds((T, kc * N_DEV), BF16)], out_specs=[pl.BlockSpec((tb, kc), lambda j, i: (i, j))],
                       dims=NT, epilogue=epilogue)[0]

    def norm_bwd_epilogue(dh, xv, g, dxi):
        dxn, dg = _rms_bwd(xv, g, dh)
        return dxi + dxn, dxi + dxn, dg

    def transposed(gw):
        return gw.transpose(0, 2, 1).reshape(gw.shape[0] * gw.shape[2], D)

    def back_cols(name, da, gwt, xv, g, dx_in):
        n = da.shape[1]
        row = _row_spec(tk, D)
        return _matmul(name, da, gwt, [xv, g.reshape(1, D), dx_in], grid=(T // tk,),
                       a_spec=_row_spec(tk, n), b_spec=_const_spec((n, D)), extra_specs=[row, _const_spec((1, D)), row],
                       out_shapes=[_sds((T, D), F32), _sds((T, D), BF16), _sds((1, D), F32)], out_specs=[row, row, _const_spec((1, D))],
                       dims=NN, epilogue=norm_bwd_epilogue, n_sum=1)

    def token_sum(tt):
        return dict(k_axis=1, nk=T // tt) if T // tt > 1 else dict(k_axis=None)

    def wgrad_cols(name, h, da, nc):
        return _matmul(name, h, da, [], grid=(N_DEV, T // tb),
                       a_spec=pl.BlockSpec((tb, D), lambda j, t: (t, 0)), b_spec=pl.BlockSpec((tb, nc), lambda j, t: (t, j)),
                       extra_specs=[], out_shapes=[_sds((N_DEV, D, nc), BF16)],
                       out_specs=[pl.BlockSpec((None, D, nc), lambda j, t: (j, 0, 0))],
                       dims=TN, acc_shape=(D, nc), **token_sum(tb))[0]

    def wgrad_rows(name, a, dy, kc, ncols, tt):
        return _matmul(name, a, dy, [], grid=(a.shape[1] // kc, T // tt),
                       a_spec=pl.BlockSpec((tt, kc), lambda j, t: (t, j)), b_spec=pl.BlockSpec((tt, ncols), lambda j, t: (t, 0)),
                       extra_specs=[], out_shapes=[_sds((a.shape[1], ncols), BF16)],
                       out_specs=[pl.BlockSpec((kc, ncols), lambda j, t: (j, 0))],
                       dims=TN, acc_shape=(kc, ncols), **token_sum(tt))[0]

    saved = []
    xs = x2
    for i in range(depth):
        l = i // 2
        if i == 0:
            h = rmsnorm(xs, norm_mix[0])
        if i % 2 == 0:
            lat = _matmul("mla_down", h, w_dkv, [], grid=(nt,), a_spec=_row_spec(tm, D),
                          b_spec=pl.BlockSpec((None, D, LAT_PAD), lambda i_: (l, 0, 0)), extra_specs=[],
                          out_shapes=[_sds((T, LAT_PAD), F32)], out_specs=[_row_spec(tm, LAT_PAD)], dims=NN)[0]

            def latent_post(la, qn, kvn, cc, sa, sb):
                cq = _rms_fwd(la[:, :Q_RANK], qn)
                ckv = _rms_fwd(la[:, Q_RANK:Q_RANK + KV_RANK], kvn)
                kr = _rope_fwd(la[:, Q_RANK + KV_RANK:], cc, sa, sb)
                return cq, ckv, kr

            cq, ckv, kr = _rowwise(
                "mla_latent", latent_post, [lat, mla_q_norm[l].reshape(1, Q_RANK), mla_kv_norm[l].reshape(1, KV_RANK), t_cc, t_sa, t_sb],
                grid=(nt,), in_specs=[_row_spec(tm, LAT_PAD), _const_spec((1, Q_RANK)), _const_spec((1, KV_RANK))] + tab_specs,
                out_shapes=[_sds((T, Q_RANK), BF16), _sds((T, KV_RANK), BF16), _sds((T, 128), BF16)],
                out_specs=[_row_spec(tm, Q_RANK), _row_spec(tm, KV_RANK), _row_spec(tm, 128)])

            def q_epilogue(acc, cc, sa, sb):
                parts = []
                for b in range(HEADS):
                    parts += [acc[:, b * QPAD:b * QPAD + NOPE], _rope_fwd(acc[:, b * QPAD + NOPE:(b + 1) * QPAD], cc, sa, sb)]
                return (jnp.concatenate(parts, axis=1),)

            q = _matmul("mla_q", cq, w_uq, [t_cc, t_sa, t_sb], grid=(nt,), a_spec=_row_spec(tm, Q_RANK),
                        b_spec=pl.BlockSpec((None, Q_RANK, HW), lambda i_: (l, 0, 0)), extra_specs=tab_specs,
                        out_shapes=[_sds((T, HW), BF16)], out_specs=[_row_spec(tm, HW)], dims=NN, epilogue=q_epilogue)[0]

            def kv_write(outs, acc, krb):
                k_ref, v_ref, vt_ref = outs
                for b in range(HEADS):
                    vb = acc[:, b * QPAD + NOPE:(b + 1) * QPAD]
                    k_ref[:, b * QPAD:b * QPAD + NOPE] = acc[:, b * QPAD:b * QPAD + NOPE].astype(BF16)
                    k_ref[:, b * QPAD + NOPE:(b + 1) * QPAD] = krb
                    v_ref[:, b * VDIM:(b + 1) * VDIM] = vb.astype(BF16)
                    vbt = vb.T.astype(BF16)
                    for u in range(tm // tq):
                        vt_ref[b, u] = vbt[:, u * tq:(u + 1) * tq]

            kk, vv, vt = _matmul("mla_kv", ckv, w_ukv, [kr], grid=(nt,), a_spec=_row_spec(tm, KV_RANK),
                                 b_spec=pl.BlockSpec((None, KV_RANK, HW), lambda i_: (l, 0, 0)), extra_specs=[_row_spec(tm, 128)],
                                 out_shapes=[_sds((T, HW), BF16), _sds((T, OW), BF16), _sds((HEADS, T // tq, VDIM, tq), BF16)],
                                 out_specs=[_row_spec(tm, HW), _row_spec(tm, OW), pl.BlockSpec((HEADS, tm // tq, VDIM, tq), lambda i_: (0, i_, 0, 0))],
                                 dims=NN, write=kv_write)
            group = [up_sh[i], down_sh[i], in_sh[l], out_sh[l], up_sh[i + 1], down_sh[i + 1]]
            o, lse, *bufs = _flash_fwd(q, kk, vt, tq, comm=_gather_level1(group))
            xm, h2, g_up[i], g_down[i] = _matmul(
                "mla_out", o, w_o, [xs, norm_ffn[i].reshape(1, D)], grid=(nt,), a_spec=_row_spec(tm, OW),
                b_spec=pl.BlockSpec((None, OW, D), lambda i_: (l, 0, 0)), extra_specs=[_row_spec(tm, D), _const_spec((1, D))],
                out_shapes=[_sds((T, D), F32), _sds((T, D), BF16)], out_specs=[_row_spec(tm, D)] * 2, dims=NN,
                epilogue=residual_norm, comm=_gather_level2(bufs[:2]))
            half_gathered = bufs[2:]
            mix_saved = (h, lat, cq, ckv, q, kk, vv, o, lse)
        else:
            gp, ge = proj_cols("sgu_in", h, g_in[l], e2c, _gelu_and_grad, 2)
            gate = _sgu_mid_fwd(ge, ln_g_full[l], ln_b_full[l], sgu_w_spatial[l], b_sp[l], 4)
            xm, h2 = proj_rows_residual("sgu_out", gate, g_out[l], xs, norm_ffn[i])
            mix_saved = (h, gp, ge, gate)
        r, s, *rest = proj_cols("ffn_up", h2, g_up[i], ffc, lambda acc: (jnp.maximum(acc, 0.0), jnp.square(jnp.maximum(acc, 0.0))), 2,
                                comm=_gather_level2(half_gathered) if i % 2 == 0 else None)
        if i % 2 == 0:
            g_in[l], g_out[l], g_up[i + 1], g_down[i + 1] = rest
        xo, h_next = proj_rows_residual("ffn_down", s, g_down[i], xm, norm_mix[i + 1] if i + 1 < depth else final_norm)
        saved.append((xs, xm, mix_saved, h2, r, s))
        xs, h = xo, h_next

    def loss_head(xv, tg, g):
        y = _rms_fwd(xv, g)
        err = y - tg
        part = 0.5 * jnp.sum(jnp.sum(err * err, axis=-1, keepdims=True), axis=0, keepdims=True) / D
        dx, dg = _rms_bwd(xv, g, err / D)
        return dx, dx, jnp.broadcast_to(part, (1, 128)), dg

    dx, dyb, loss_part, d_final = _rowwise(
        "loss_head", loss_head, [xs, tgt, final_norm.reshape(1, D)], grid=(nt,),
        in_specs=[_row_spec(tm, D), _row_spec(tm, D), _const_spec((1, D))],
        out_shapes=[_sds((T, D), F32), _sds((T, D), BF16), _sds((1, 128), F32), _sds((1, D), F32)],
        out_specs=[_row_spec(tm, D), _row_spec(tm, D), _const_spec((1, 128)), _const_spec((1, D))], n_acc=2)
    loss = lax.psum(loss_part[0, 0], ("x", "y", "c"))

    d_norm_mix, d_norm_ffn = [None] * depth, [None] * depth
    d_qn, d_kvn = [None] * n_mla, [None] * n_mla
    d_wsp, d_bsp, d_lng, d_lnb = [None] * n_sgu, [None] * n_sgu, [None] * n_sgu, [None] * n_sgu
    layers = {"dkv": n_mla, "uq": n_mla, "ukv": n_mla, "o": n_mla, "in": n_sgu, "out": n_sgu, "up": depth, "down": depth}
    stacked = {nm: None for nm in layers}
    pending = []

    def add_pair(g, rcv):
        _, rws, cls = g.shape
        g4 = g.reshape(N_CHIP, 2, rws, cls)
        rt = _tile(rws, 512)
        return _rowwise("grad_pair_sum", lambda a, b_: a.astype(F32) + b_.astype(F32), [g4, rcv], grid=(N_CHIP, rws // rt),
                        in_specs=[pl.BlockSpec((None, None, rt, cls), lambda ch, i_, cr: (ch, cr[0], i_, 0)),
                                  pl.BlockSpec((None, rt, cls), lambda ch, i_, cr: (ch, i_, 0))],
                        out_shapes=_sds(rcv.shape, BF16), out_specs=pl.BlockSpec((None, rt, cls), lambda ch, i_, cr: (ch, i_, 0)),
                        grid_spec_prefetch=cidx)

    def chip_comm_of_pending():
        grads = [g for _, _, g in pending]
        from_sibling = _comm_call("grad_sibling_exchange", _sibling_exchange(grads))
        parts = [add_pair(g, rcv) for g, rcv in zip(grads, from_sibling)]
        comm, names = _chip_exchange(parts, [(nm, l_) for nm, l_, _ in pending], layers, stacked)
        pending.clear()
        return comm, names

    def rows128(a, rows):
        flat = a.reshape(-1, 128)
        return jnp.pad(flat, ((0, rows - flat.shape[0]), (0, 0)))

    def pad_to(n, mult):
        return -(-n // mult) * mult

    def packed(arrs, sizes):
        return jnp.concatenate([rows128(a, sz) for a, sz in zip(arrs, sizes)], axis=0)

    n_wsp, n_bsp, n_ln = sgu_w_spatial.size // 128, pad_to(sgu_b_spatial.size // 128, 8), pad_to(n_sgu * E // 128, 8)
    early_sizes = [n_wsp, pad_to(n_wsp + n_bsp, SMALL_ROWS) - n_wsp, n_ln, n_ln]
    early_rep = early_sizes[0] + early_sizes[1]
    gathered_early = None

    for i in reversed(range(depth)):
        l = i // 2
        xs_i, xm, mix_saved, h2, r, s = saved[i]
        da = back_rows("ffn_down_bwd", dyb, g_down[i], ffc, [r], lambda acc, rr: (acc * (2.0 * rr.astype(F32)),))
        pending.append(("down", i, wgrad_rows("ffn_down_wgrad", s, dyb, ffc, D, tb).reshape(N_DEV, ffc, D)))
        pending.append(("up", i, wgrad_cols("ffn_up_wgrad", h2, da, ffc)))
        dx, dyb, d_norm_ffn[i] = back_cols("ffn_up_bwd", da, transposed(g_up[i]), xm, norm_ffn[i], dx)
        if i % 2 == 0:
            h, lat, cq, ckv, q, kk, vv, o, lse = mix_saved
            do = _matmul("mla_out_bwd", dyb, w_o, [], grid=(nt,), a_spec=_row_spec(tm, D),
                         b_spec=pl.BlockSpec((None, OW, D), lambda i_: (l, 0, 0)), extra_specs=[],
                         out_shapes=[_sds((T, OW), BF16)], out_specs=[_row_spec(tm, OW)], dims=NT)[0]
            g_o_l = wgrad_rows("mla_out_wgrad", o, dyb, OW, D, tm).reshape(N_DEV, owc, D)
            comm, names = chip_comm_of_pending()
            if i == 0:
                early = packed([jnp.stack(d_wsp, 0), jnp.stack(d_bsp, 0), jnp.concatenate(d_lng, 0), jnp.concatenate(d_lnb, 0)], early_sizes)
                comm = _merge_comm(comm, _gather_level1([early]))
            dq_pre, dk, dv, *bufs = _flash_bwd(q, kk, vv, o, do, lse, (t_cc, t_sa, t_sb), tq, comm=comm)
            stacked.update(dict(zip(names, bufs)))
            pending.append(("o", l, g_o_l))

            def kv_pre(dkb, dvb, cc, sa, sb):
                parts, dkr = [], None
                for b in range(HEADS):
                    parts += [dkb[:, b * QPAD:b * QPAD + NOPE], dvb[:, b * VDIM:(b + 1) * VDIM]]
                    piece = dkb[:, b * QPAD + NOPE:(b + 1) * QPAD]
                    dkr = piece if dkr is None else dkr + piece
                return jnp.concatenate(parts, axis=1), _rope_bwd(dkr, cc, sa, sb)

            dkv, dkr, *rest = _rowwise("mla_dkv_rope", kv_pre, [dk, dv, t_cc, t_sa, t_sb], grid=(T // ts,),
                                       in_specs=[_row_spec(ts, HW), _row_spec(ts, OW)] + [_row_spec(ts, 128)] * 3,
                                       out_shapes=[_sds((T, HW), BF16), _sds((T, 128), F32)], out_specs=[_row_spec(ts, HW), _row_spec(ts, 128)],
                                       comm=_gather_level2(bufs[len(names):]) if i == 0 else None)
            if i == 0:
                (gathered_early,) = rest
            g_uq_l = wgrad_rows("mla_q_wgrad", cq, dq_pre, Q_RANK, HW, tm)
            g_ukv_l = wgrad_rows("mla_kv_wgrad", ckv, dkv, KV_RANK, HW, tm)
            pending.append(("uq", l, g_uq_l.reshape(Q_RANK, HEADS, QPAD)[:, :, :NOPE + ROPE].transpose(1, 0, 2)))
            pending.append(("ukv", l, g_ukv_l.reshape(KV_RANK, HEADS, NOPE + VDIM).transpose(1, 0, 2)))
            dcq = _matmul("mla_q_bwd", dq_pre, w_uq, [], grid=(nt,), a_spec=_row_spec(tm, HW),
                          b_spec=pl.BlockSpec((None, Q_RANK, HW), lambda i_: (l, 0, 0)), extra_specs=[],
                          out_shapes=[_sds((T, Q_RANK), F32)], out_specs=[_row_spec(tm, Q_RANK)], dims=NT)[0]
            dckv = _matmul("mla_kv_bwd", dkv, w_ukv, [], grid=(nt,), a_spec=_row_spec(tm, HW),
                           b_spec=pl.BlockSpec((None, KV_RANK, HW), lambda i_: (l, 0, 0)), extra_specs=[],
                           out_shapes=[_sds((T, KV_RANK), F32)], out_specs=[_row_spec(tm, KV_RANK)], dims=NT)[0]

            def latent_bwd(la, qn, kvn, dq_, dkv_, dkr_):
                dcq_raw, dqn = _rms_bwd(la[:, :Q_RANK], qn, dq_)
                dckv_raw, dkvn = _rms_bwd(la[:, Q_RANK:Q_RANK + KV_RANK], kvn, dkv_)
                return jnp.concatenate([dcq_raw, dckv_raw, dkr_], axis=1), dqn, dkvn

            dlat, d_qn[l], d_kvn[l] = _rowwise(
                "mla_latent_bwd", latent_bwd, [lat, mla_q_norm[l].reshape(1, Q_RANK), mla_kv_norm[l].reshape(1, KV_RANK), dcq, dckv, dkr],
                grid=(nt,), in_specs=[_row_spec(tm, LAT_PAD), _const_spec((1, Q_RANK)), _const_spec((1, KV_RANK)),
                                      _row_spec(tm, Q_RANK), _row_spec(tm, KV_RANK), _row_spec(tm, 128)],
                out_shapes=[_sds((T, LAT_PAD), BF16), _sds((1, Q_RANK), F32), _sds((1, KV_RANK), F32)],
                out_specs=[_row_spec(tm, LAT_PAD), _const_spec((1, Q_RANK)), _const_spec((1, KV_RANK))], n_acc=2)
            g_dkv_l = wgrad_rows("mla_down_wgrad", h, dlat, D, LAT_PAD, tm)
            pending.append(("dkv", l, g_dkv_l[:, :LAT].reshape(N_DEV, dc, LAT)))
            dx, dyb, d_norm_mix[i] = _matmul(
                "mla_down_bwd", dlat, w_dkv, [xs_i, norm_mix[i].reshape(1, D), dx], grid=(nt,), a_spec=_row_spec(tm, LAT_PAD),
                b_spec=pl.BlockSpec((None, D, LAT_PAD), lambda i_: (l, 0, 0)), extra_specs=[_row_spec(tm, D), _const_spec((1, D)), _row_spec(tm, D)],
                out_shapes=[_sds((T, D), F32), _sds((T, D), BF16), _sds((1, D), F32)],
                out_specs=[_row_spec(tm, D), _row_spec(tm, D), _const_spec((1, D))], dims=NT, epilogue=norm_bwd_epilogue, n_sum=1)
        else:
            h, gp, ge, gate = mix_saved
            dgate = back_rows("sgu_out_bwd", dyb, g_out[l], ec, [], None)
            pending.append(("out", l, wgrad_rows("sgu_out_wgrad", gate, dyb, ec, D, tb).reshape(N_DEV, ec, D)))
            dz, d_wsp[l], d_bsp[l], d_lng[l], d_lnb[l] = _sgu_mid_bwd(ge, gp, dgate, ln_g_full[l], ln_b_full[l], sgu_w_spatial[l], b_sp[l], 2)
            pending.append(("in", l, wgrad_cols("sgu_in_wgrad", h, dz, e2c)))
            dx, dyb, d_norm_mix[i] = back_cols("sgu_in_bwd", dz, transposed(g_in[l]), xs_i, norm_mix[i], dx)
    grad_x = dx.reshape(1, T, D)

    last_comm, last_names = chip_comm_of_pending()
    late_g = [jnp.concatenate(d_norm_mix, 0), jnp.concatenate(d_norm_ffn, 0), d_final, jnp.concatenate(d_qn, 0), jnp.concatenate(d_kvn, 0)]
    late_w = [norm_mix, norm_ffn, final_norm, mla_q_norm, mla_kv_norm]
    late_m = [m_norm_mix, m_norm_ffn, m_final_norm, m_mla_q_norm, m_mla_kv_norm]
    late_v = [v_norm_mix, v_norm_ffn, v_final_norm, v_mla_q_norm, v_mla_kv_norm]
    late_sizes = [pad_to(g.size // 128, 8) for g in late_g]
    late_rows = sum(late_sizes)

    def adam_big(parts, w, m, v):
        lyr, rws, cls = w.shape
        rt = _tile(rws, 256)

        def fn(p, w_, m_, v_):
            g = (p[0].astype(F32) + p[1].astype(F32)) + (p[2].astype(F32) + p[3].astype(F32))
            return (g, *_adam(w_, g, m_, v_))

        spec = pl.BlockSpec((None, rt, cls), lambda l_, i_: (l_, i_, 0))
        return _rowwise("adam_large", fn, [parts, w, m, v], grid=(lyr, rws // rt),
                        in_specs=[pl.BlockSpec((N_CHIP, None, rt, cls), lambda l_, i_: (0, l_, i_, 0)), spec, spec, spec],
                        out_shapes=[_sds(w.shape, F32)] * 4, out_specs=[spec] * 4)

    stacked.update(dict(zip(last_names, _comm_call("grad_chip_exchange", last_comm))))
    (gathered_late,) = _all_gather("gather_small_grads", [packed(late_g, late_sizes)])
    big = {}
    big["in"] = adam_big(stacked["in"], sgu_w_in, m_sgu_w_in, v_sgu_w_in)
    big["up"] = adam_big(stacked["up"], ffn_w_up, m_ffn_w_up, v_ffn_w_up)
    big["down"] = adam_big(stacked["down"], ffn_w_down, m_ffn_w_down, v_ffn_w_down)
    big["out"] = adam_big(stacked["out"], sgu_w_out, m_sgu_w_out, v_sgu_w_out)
    big["dkv"] = adam_big(stacked["dkv"], mla_w_dkv, m_mla_w_dkv, v_mla_w_dkv)
    big["uq"] = adam_big(stacked["uq"], mla_w_uq, m_mla_w_uq, v_mla_w_uq)
    big["ukv"] = adam_big(stacked["ukv"], mla_w_ukv, m_mla_w_ukv, v_mla_w_ukv)
    big["o"] = adam_big(stacked["o"], mla_w_o, m_mla_w_o, v_mla_w_o)
    big_res = [big[nm][:4] for nm in ("dkv", "uq", "ukv", "o", "in", "out", "up", "down")]

    def sum8(p):
        return ((p[0] + p[1]) + (p[2] + p[3])) + ((p[4] + p[5]) + (p[6] + p[7]))

    def adam_packed(name, gathered, ws, ms, vs, sizes, rows, tile):
        spec = _row_spec(tile, 128)
        return _rowwise(name, lambda p, w_, m_, v_: (sum8(p), *_adam(w_, sum8(p), m_, v_)),
                        [gathered, packed(ws, sizes), packed(ms, sizes), packed(vs, sizes)], grid=(rows // tile,),
                        in_specs=[pl.BlockSpec((N_DEV, tile, 128), lambda i_: (0, i_, 0)), spec, spec, spec],
                        out_shapes=[_sds((rows, 128), F32)] * 4, out_specs=[spec] * 4)

    late_res = adam_packed("adam_small", gathered_late, late_w, late_m, late_v, late_sizes, late_rows, late_rows)
    early_res = adam_packed("adam_spatial", gathered_early, [sgu_w_spatial, sgu_b_spatial], [m_sgu_w_spatial, m_sgu_b_spatial],
                            [v_sgu_w_spatial, v_sgu_b_spatial], early_sizes[:2], early_rep, SMALL_ROWS)

    def unpack(res, sizes, k, like):
        off = sum(sizes[:k])
        return res[off:off + like.size // 128].reshape(like.shape)

    my_b = 4 * lax.axis_index("x") + 2 * lax.axis_index("y") + lax.axis_index("c")
    ln_w = jnp.concatenate([sgu_ln_g, sgu_ln_b], 0)
    ln_m = jnp.concatenate([m_sgu_ln_g, m_sgu_ln_b], 0)
    ln_v = jnp.concatenate([v_sgu_ln_g, v_sgu_ln_b], 0)
    ln_all = jnp.concatenate([gathered_early[:, early_rep:early_rep + n_sgu * E // 128], gathered_early[:, early_rep + n_ln:early_rep + n_ln + n_sgu * E // 128]], axis=1)
    ln_mine = lax.dynamic_slice_in_dim(ln_all.reshape(N_DEV, 2 * n_sgu, N_DEV, ec), my_b, 1, axis=2).reshape(N_DEV, 2 * n_sgu, ec)
    ln_g_, ln_d, ln_m2, ln_v2 = _rowwise(
        "adam_ln", lambda p, w_, m_, v_: (sum8(p), *_adam(w_, sum8(p), m_, v_)), [ln_mine, ln_w, ln_m, ln_v], grid=(1,),
        in_specs=[_const_spec(ln_mine.shape), _const_spec(ln_w.shape), _const_spec(ln_w.shape), _const_spec(ln_w.shape)],
        out_shapes=[_sds(ln_w.shape, F32)] * 4, out_specs=[_const_spec(ln_w.shape)] * 4)

    def family(pos):
        ln = [ln_g_, ln_d, ln_m2, ln_v2][pos]
        late = [unpack(late_res[pos], late_sizes, k, w_) for k, w_ in enumerate(late_w)]
        w_sp_, b_sp_ = unpack(early_res[pos], early_sizes, 0, sgu_w_spatial), unpack(early_res[pos], early_sizes, 1, sgu_b_spatial)
        bigs = [res[pos] for res in big_res]
        return [late[0], late[1], late[2], bigs[0], late[3], late[4], bigs[1], bigs[2], bigs[3],
                bigs[4], ln[:n_sgu], ln[n_sgu:], w_sp_, b_sp_, bigs[5], bigs[6], bigs[7]]

    return (loss, grad_x, *family(0), *family(1), *family(2), *family(3))
```

```python
import math

import jax
import jax.numpy as jnp
from jax import lax
from jax.experimental import pallas as pl
from jax.experimental.pallas import tpu as pltpu

F32 = jnp.float32
BF16 = jnp.bfloat16
MESH = pl.DeviceIdType.MESH

N_DEV = 8
N_CHIP = 4
HEADS = 8
NOPE = 128
ROPE = 64
VDIM = 128
QPAD = 256
Q_RANK = 256
KV_RANK = 128
LAT = Q_RANK + KV_RANK + ROPE
LAT_PAD = 512
ROPE_THETA = 10000.0
SGU_CHUNK = 128
SGU_GROUPS = 8
NORM_EPS = 1e-6
LN_EPS = 1e-5
ADAM_LR = 0.001
ADAM_B1 = 0.9
ADAM_B2 = 0.999
ADAM_EPS = 1e-08
ADAM_WD = 0.01
ADAM_STEP = 10
ATTN_SCALE = (NOPE + ROPE) ** -0.5
NEG = -1e30
EXP2_SCALE = ATTN_SCALE * math.log2(math.e)
VMEM_LIMIT = 56 * 1024 * 1024
SMALL_ROWS = 256

NN = (((1,), (0,)), ((), ()))
NT = (((1,), (1,)), ((), ()))
TN = (((0,), (0,)), ((), ()))
ANY = pl.BlockSpec(memory_space=pl.ANY)


def _pcall(body, **kw):
    return pl.pallas_call(body, **kw)


def _params(n_grid, side_effects=False):
    return pltpu.CompilerParams(dimension_semantics=("arbitrary",) * n_grid, vmem_limit_bytes=VMEM_LIMIT, has_side_effects=side_effects)


def _sds(shape, dtype):
    return jax.ShapeDtypeStruct(tuple(shape), dtype)


def _tile(n, want):
    t = min(n, want)
    assert n % t == 0, (n, want)
    return t


class _Comm:
    def __init__(self, operands, out_shapes, aliases, scratch, start, finish):
        self.operands, self.out_shapes, self.aliases, self.scratch = operands, out_shapes, aliases, scratch
        self.start, self.finish = start, finish


def _merge_comm(first, second):
    n_in, n_out, n_sc = len(first.operands), len(first.out_shapes), len(first.scratch)
    aliases = dict(first.aliases)
    aliases.update({n_in + k: n_out + v for k, v in second.aliases.items()})

    def start(ins, outs, sems):
        first.start(ins[:n_in], outs[:n_out], sems[:n_sc])
        second.start(ins[n_in:], outs[n_out:], sems[n_sc:])

    def finish(ins, outs, sems):
        first.finish(ins[:n_in], outs[:n_out], sems[:n_sc])
        second.finish(ins[n_in:], outs[n_out:], sems[n_sc:])

    return _Comm([*first.operands, *second.operands], [*first.out_shapes, *second.out_shapes], aliases,
                 [*first.scratch, *second.scratch], start, finish)


def _place():
    return lax.axis_index("x"), lax.axis_index("y"), lax.axis_index("c")


def _other_chips(x, y):
    return [(1 - x, y), (x, 1 - y), (1 - x, 1 - y)]


def _dev_index(dev):
    return 4 * dev[0] + 2 * dev[1] + dev[2]


def _comm_call(name, comm):
    c_in, c_out = len(comm.operands), len(comm.out_shapes)

    def body(*refs):
        ins, outs, sems = refs[:c_in], refs[c_in:c_in + c_out], refs[c_in + c_out:]
        comm.start(ins, outs, sems)
        comm.finish(ins, outs, sems)

    return _pcall(body, name=name, in_specs=[ANY] * c_in, out_specs=[ANY] * c_out, out_shape=comm.out_shapes,
                  scratch_shapes=comm.scratch, input_output_aliases=dict(comm.aliases),
                  compiler_params=pltpu.CompilerParams(has_side_effects=True))(*comm.operands)


def _call(name, body, operands, in_specs, out_shapes, out_specs, scratch, grid, comm=None):
    if comm is None:
        return _pcall(body, name=name, grid=grid, in_specs=in_specs, out_specs=out_specs, out_shape=out_shapes,
                      scratch_shapes=scratch, compiler_params=_params(len(grid)))(*operands)
    n_in, n_out, n_sc = len(operands), len(out_shapes), len(scratch)
    c_in, c_out = len(comm.operands), len(comm.out_shapes)

    def hosted(*refs):
        ins, cins = refs[:n_in], refs[n_in:n_in + c_in]
        o0 = n_in + c_in
        outs, couts = refs[o0:o0 + n_out], refs[o0 + n_out:o0 + n_out + c_out]
        rest = refs[o0 + n_out + c_out:]
        sc, csems = rest[:n_sc], rest[n_sc:]
        first = pl.program_id(0) == 0
        last = pl.program_id(0) == grid[0] - 1
        for d in range(1, len(grid)):
            first = jnp.logical_and(first, pl.program_id(d) == 0)
            last = jnp.logical_and(last, pl.program_id(d) == grid[d] - 1)

        @pl.when(first)
        def _():
            comm.start(cins, couts, csems)

        body(*ins, *outs, *sc)

        @pl.when(last)
        def _():
            comm.finish(cins, couts, csems)

    return _pcall(hosted, name=name, grid=grid, in_specs=[*in_specs, *[ANY] * c_in], out_specs=[*out_specs, *[ANY] * c_out],
                  out_shape=[*out_shapes, *comm.out_shapes], scratch_shapes=[*scratch, *comm.scratch],
                  input_output_aliases={n_in + k: n_out + v for k, v in comm.aliases.items()},
                  compiler_params=_params(len(grid), side_effects=True))(*operands, *comm.operands)


def _gather_level1(shards):
    n = len(shards)

    def copies(ins, outs, sems):
        send_sems, recv_sems, local_sems = sems
        x, y, c = _place()
        me, sibling = (x, y, c), (x, y, 1 - c)
        chips = _other_chips(x, y)

        def copy(a, k, block, to, src=None):
            slot = outs[a].at[_dev_index(block)]
            return pltpu.make_async_remote_copy(src_ref=slot if src is None else src, dst_ref=slot, send_sem=send_sems.at[a, k],
                                                recv_sem=recv_sems.at[a, k], device_id=to, device_id_type=MESH)

        mine = [pltpu.make_async_copy(ins[a], outs[a].at[_dev_index(me)], local_sems.at[a]) for a in range(n)]
        sends = [copy(a, 1 + j, me, (*chip, c), src=ins[a]) for j, chip in enumerate(chips) for a in range(n)]
        sends += [copy(a, 0, me, sibling, src=ins[a]) for a in range(n)]
        recvs = [copy(a, 1 + j, (*chip, c), me) for j, chip in enumerate(chips) for a in range(n)]
        recvs += [copy(a, 0, sibling, me) for a in range(n)]
        return mine, sends, recvs

    def start(ins, outs, sems):
        mine, sends, _ = copies(ins, outs, sems)
        for cp in mine + sends:
            cp.start()

    def finish(ins, outs, sems):
        mine, sends, recvs = copies(ins, outs, sems)
        for cp in recvs:
            cp.wait_recv()
        for cp in sends:
            cp.wait_send()
        for cp in mine:
            cp.wait()

    return _Comm(shards, [_sds((N_DEV, *a.shape), a.dtype) for a in shards], {},
                 [pltpu.SemaphoreType.DMA((n, 4)), pltpu.SemaphoreType.DMA((n, 4)), pltpu.SemaphoreType.DMA((n,))], start, finish)


def _gather_level2(bufs):
    n = len(bufs)

    def copies(outs, sems):
        send_sems, recv_sems = sems
        x, y, c = _place()
        sibling = (x, y, 1 - c)
        sends, recvs = [], []
        for j, chip in enumerate(_other_chips(x, y)):
            for a in range(n):
                have, want = outs[a].at[_dev_index((*chip, c))], outs[a].at[_dev_index((*chip, 1 - c))]
                sends.append(pltpu.make_async_remote_copy(src_ref=have, dst_ref=have, send_sem=send_sems.at[a, j], recv_sem=recv_sems.at[a, j],
                                                          device_id=sibling, device_id_type=MESH))
                recvs.append(pltpu.make_async_remote_copy(src_ref=want, dst_ref=want, send_sem=send_sems.at[a, j], recv_sem=recv_sems.at[a, j],
                                                          device_id=sibling, device_id_type=MESH))
        return sends, recvs

    def start(ins, outs, sems):
        for cp in copies(outs, sems)[0]:
            cp.start()

    def finish(ins, outs, sems):
        sends, recvs = copies(outs, sems)
        for cp in recvs:
            cp.wait_recv()
        for cp in sends:
            cp.wait_send()

    return _Comm(bufs, [_sds(b.shape, b.dtype) for b in bufs], {a: a for a in range(n)},
                 [pltpu.SemaphoreType.DMA((n, 3)), pltpu.SemaphoreType.DMA((n, 3))], start, finish)


def _all_gather(name, arrays):
    n = len(arrays)

    def body(*refs):
        ins = refs[:n]
        outs = refs[n:2 * n]
        send_sems, recv_sems, local_sems = refs[2 * n:]
        x, y, c = _place()
        me, sibling = (x, y, c), (x, y, 1 - c)
        chips = _other_chips(x, y)

        def copy(a, k, block, to, src=None):
            slot = outs[a].at[_dev_index(block)]
            return pltpu.make_async_remote_copy(src_ref=slot if src is None else src, dst_ref=slot, send_sem=send_sems.at[a, k],
                                                recv_sem=recv_sems.at[a, k], device_id=to, device_id_type=MESH)

        mine = [pltpu.make_async_copy(ins[a], outs[a].at[_dev_index(me)], local_sems.at[a]) for a in range(n)]
        for cp in mine:
            cp.start()
        first = []
        for j, chip in enumerate(chips):
            first += [copy(a, 1 + j, me, (*chip, c), src=ins[a]) for a in range(n)]
        first += [copy(a, 0, me, sibling, src=ins[a]) for a in range(n)]
        for cp in first:
            cp.start()
        passed = []
        for j, chip in enumerate(chips):
            for a in range(n):
                copy(a, 1 + j, (*chip, c), me).wait_recv()
                fwd = copy(a, 4 + j, (*chip, c), sibling)
                fwd.start()
                passed.append(fwd)
        for a in range(n):
            copy(a, 0, sibling, me).wait_recv()
            for j, chip in enumerate(chips):
                copy(a, 4 + j, (*chip, 1 - c), me).wait_recv()
        for cp in first + passed:
            cp.wait_send()
        for cp in mine:
            cp.wait()

    return _pcall(
        body, name=name, in_specs=[ANY] * n, out_specs=[ANY] * n,
        out_shape=[_sds((N_DEV, *a.shape), a.dtype) for a in arrays],
        scratch_shapes=[pltpu.SemaphoreType.DMA((n, 7)), pltpu.SemaphoreType.DMA((n, 7)), pltpu.SemaphoreType.DMA((n,))],
        compiler_params=pltpu.CompilerParams(has_side_effects=True),
    )(*arrays)


def _sibling_exchange(grads):
    n = len(grads)

    def start(ins, outs, sems):
        send_sems, recv_sems = sems
        x, y, c = _place()
        for a in range(n):
            for ch in range(N_CHIP):
                pltpu.make_async_remote_copy(src_ref=ins[a].at[2 * ch + 1 - c], dst_ref=outs[a].at[ch], send_sem=send_sems.at[a],
                                             recv_sem=recv_sems.at[a], device_id=(x, y, 1 - c), device_id_type=MESH).start()

    def finish(ins, outs, sems):
        send_sems, recv_sems = sems
        x, y, c = _place()
        for a in range(n):
            pltpu.make_async_remote_copy(src_ref=outs[a], dst_ref=outs[a], send_sem=send_sems.at[a], recv_sem=recv_sems.at[a],
                                         device_id=(x, y, 1 - c), device_id_type=MESH).wait()

    return _Comm(grads, [_sds((N_CHIP, *g.shape[1:]), g.dtype) for g in grads], {},
                 [pltpu.SemaphoreType.DMA((n,)), pltpu.SemaphoreType.DMA((n,))], start, finish)


def _chip_exchange(parts, slots, layers, stacked):
    n = len(parts)
    names = []
    for nm, _ in slots:
        if nm not in names:
            names.append(nm)
    shapes = {nm: _sds((N_CHIP, layers[nm], *parts[a].shape[1:]), parts[a].dtype) for a, (nm, _) in enumerate(slots)}
    kept = [nm for nm in names if stacked.get(nm) is not None]
    aliases = {n + k: names.index(nm) for k, nm in enumerate(kept)}

    def copies(ins, outs, sems):
        send_sems, recv_sems, local_sems = sems
        x, y, c = _place()
        mine = 2 * x + y
        local, sends, recvs = [], [], []
        for a, (nm, l) in enumerate(slots):
            buf = outs[names.index(nm)]
            local.append(pltpu.make_async_copy(ins[a].at[mine], buf.at[mine, l], local_sems.at[a]))
            for j, chip in enumerate(_other_chips(x, y)):
                theirs = buf.at[2 * chip[0] + chip[1], l]
                sends.append(pltpu.make_async_remote_copy(src_ref=ins[a].at[2 * chip[0] + chip[1]], dst_ref=buf.at[mine, l], send_sem=send_sems.at[a, j],
                                                          recv_sem=recv_sems.at[a, j], device_id=(*chip, c), device_id_type=MESH))
                recvs.append(pltpu.make_async_remote_copy(src_ref=theirs, dst_ref=theirs, send_sem=send_sems.at[a, j],
                                                          recv_sem=recv_sems.at[a, j], device_id=(*chip, c), device_id_type=MESH))
        return local, sends, recvs

    def start(ins, outs, sems):
        local, sends, _ = copies(ins, outs, sems)
        for cp in local + sends:
            cp.start()

    def finish(ins, outs, sems):
        local, sends, recvs = copies(ins, outs, sems)
        for cp in recvs:
            cp.wait_recv()
        for cp in sends:
            cp.wait_send()
        for cp in local:
            cp.wait()

    comm = _Comm([*parts, *[stacked[nm] for nm in kept]], [shapes[nm] for nm in names], aliases,
                 [pltpu.SemaphoreType.DMA((n, 3)), pltpu.SemaphoreType.DMA((n, 3)), pltpu.SemaphoreType.DMA((n,))], start, finish)
    return comm, names


def _matmul(name, a, b, extras, *, grid, a_spec, b_spec, extra_specs, out_shapes, out_specs, dims, k_axis=None, nk=1,
            acc_shape=None, epilogue=None, comm=None, n_sum=0, write=None):
    n_extra = len(extras)
    n_out = len(out_shapes)

    def body(*refs):
        a_ref, b_ref = refs[0], refs[1]
        ex = refs[2:2 + n_extra]
        outs = refs[2 + n_extra:2 + n_extra + n_out]
        prod = lax.dot_general(a_ref[...], b_ref[...], dims, preferred_element_type=F32)

        def finish(acc):
            if write is not None:
                write(outs, acc, *[e[...] for e in ex])
                return
            res = epilogue(acc, *[e[...] for e in ex]) if epilogue is not None else (acc,)
            first = None
            for d in range(len(grid)):
                if d != k_axis:
                    here = pl.program_id(d) == 0
                    first = here if first is None else jnp.logical_and(first, here)
            for idx, (o, r) in enumerate(zip(outs, res)):
                if idx < n_out - n_sum:
                    o[...] = r.astype(o.dtype)
                else:
                    @pl.when(first)
                    def _(o=o, r=r):
                        o[...] = r.astype(o.dtype)

                    @pl.when(jnp.logical_not(first))
                    def _(o=o, r=r):
                        o[...] += r.astype(o.dtype)

        if k_axis is None:
            finish(prod)
        else:
            acc_ref = refs[-1]
            k = pl.program_id(k_axis)

            @pl.when(k == 0)
            def _():
                acc_ref[...] = prod

            @pl.when(k > 0)
            def _():
                acc_ref[...] += prod

            @pl.when(k == nk - 1)
            def _():
                finish(acc_ref[...])

    scratch = [] if k_axis is None else [pltpu.VMEM(acc_shape, F32)]
    return _call(name, body, [a, b, *extras], [a_spec, b_spec, *extra_specs], list(out_shapes), list(out_specs), scratch, grid, comm)


def _rowwise(name, fn, operands, *, grid, in_specs, out_shapes, out_specs, n_acc=0, grid_spec_prefetch=None, comm=None):
    n_in = len(operands)
    n_out = len(out_shapes)
    n_pre = 0 if grid_spec_prefetch is None else 1

    def body(*refs):
        refs = refs[n_pre:]
        ins = refs[:n_in]
        outs = refs[n_in:n_in + n_out]
        res = fn(*[r[...] for r in ins])
        if not isinstance(res, (tuple, list)):
            res = (res,)
        first = pl.program_id(0) == 0
        for d in range(1, len(grid)):
            first = jnp.logical_and(first, pl.program_id(d) == 0)
        for idx, (o, r) in enumerate(zip(outs, res)):
            if idx < n_out - n_acc:
                o[...] = r.astype(o.dtype)
            else:
                @pl.when(first)
                def _(o=o, r=r):
                    o[...] = r.astype(o.dtype)

                @pl.when(jnp.logical_not(first))
                def _(o=o, r=r):
                    o[...] += r.astype(o.dtype)

    if comm is not None:
        return _call(name, body, list(operands), list(in_specs), list(out_shapes), list(out_specs), [], grid, comm)
    if grid_spec_prefetch is None:
        return _pcall(body, name=name, grid=grid, in_specs=in_specs, out_specs=out_specs, out_shape=out_shapes,
                      compiler_params=_params(len(grid)))(*operands)
    gs = pltpu.PrefetchScalarGridSpec(num_scalar_prefetch=1, grid=grid, in_specs=in_specs, out_specs=out_specs)
    return _pcall(body, name=name, grid_spec=gs, out_shape=out_shapes,
                  compiler_params=_params(len(grid)))(grid_spec_prefetch, *operands)


def _row_spec(tm, w):
    return pl.BlockSpec((tm, w), lambda i: (i, 0))


def _const_spec(shape):
    nd = len(shape)
    return pl.BlockSpec(tuple(shape), lambda *_: (0,) * nd)


def _rms_fwd(x, g):
    r = lax.rsqrt(jnp.mean(x * x, axis=-1, keepdims=True) + NORM_EPS)
    return x * r * g


def _rms_bwd(x, g, dy):
    r = lax.rsqrt(jnp.mean(x * x, axis=-1, keepdims=True) + NORM_EPS)
    xh = x * r
    u = dy * g
    dx = r * (u - xh * jnp.mean(u * xh, axis=-1, keepdims=True))
    dg = jnp.sum(dy * xh, axis=0, keepdims=True)
    return dx, dg


def _gelu_and_grad(z):
    cdf = 0.5 * (1.0 + lax.erf(z * (2.0 ** -0.5)))
    return cdf + z * jnp.exp(-0.5 * z * z) * ((2.0 * math.pi) ** -0.5), z * cdf


def _rope_fwd(x, cc, sa, sb):
    return x * cc + pltpu.roll(x, 96, 1) * sa + pltpu.roll(x, 32, 1) * sb


def _rope_bwd(d, cc, sa, sb):
    return d * cc + pltpu.roll(d * sa, 32, 1) + pltpu.roll(d * sb, 96, 1)


def _adam(w, g, m, v):
    m = ADAM_B1 * m + (1.0 - ADAM_B1) * g
    v = ADAM_B2 * v + (1.0 - ADAM_B2) * (g * g)
    m_hat = m / (1.0 - ADAM_B1 ** ADAM_STEP)
    v_hat = v / (1.0 - ADAM_B2 ** ADAM_STEP)
    delta = -ADAM_LR * (m_hat / (jnp.sqrt(v_hat) + ADAM_EPS) + ADAM_WD * w)
    return delta, m, v


def _flash_fwd(q, k, vt, tq, comm=None):
    h, t = vt.shape[0], q.shape[0]
    nq = t // tq

    chunk_blocks = [c for c in (4, 2, 1) if c < nq]

    def body(q_ref, k_ref, vt_ref, o_ref, lse_ref, m_ref, l_ref, acc_ref):
        qi = pl.program_id(1)
        m_ref[...] = jnp.full((1, tq), NEG, F32)
        l_ref[...] = jnp.zeros((1, tq), F32)
        acc_ref[...] = jnp.zeros((VDIM, tq), F32)

        def update(kb0, nblk, masked):
            kb = k_ref[pl.ds(pl.multiple_of(kb0 * tq, tq), nblk * tq), :]
            st = lax.dot_general(kb, q_ref[...], NT, preferred_element_type=F32)
            if masked:
                key = lax.broadcasted_iota(jnp.int32, (tq, tq), 0)
                qry = lax.broadcasted_iota(jnp.int32, (tq, tq), 1)
                st = jnp.where(key <= qry, st, NEG)
            m_old = m_ref[...]
            m_new = jnp.maximum(m_old, jnp.max(st, axis=0, keepdims=True))
            alpha = jnp.exp2((m_old - m_new) * EXP2_SCALE)
            pt = jnp.exp2((st - m_new) * EXP2_SCALE)
            l_ref[...] = alpha * l_ref[...] + jnp.sum(pt, axis=0, keepdims=True)
            ptb = pt.astype(BF16)
            pv = lax.dot_general(vt_ref[kb0], ptb[:tq], NN, preferred_element_type=F32)
            for j in range(1, nblk):
                pv += lax.dot_general(vt_ref[kb0 + j], ptb[j * tq:(j + 1) * tq], NN, preferred_element_type=F32)
            acc_ref[...] = alpha * acc_ref[...] + pv
            m_ref[...] = m_new

        start = jnp.int32(0)
        for c in chunk_blocks:
            take = (qi & c) != 0

            @pl.when(take)
            def _(start=start, c=c):
                update(start, c, False)

            start = start + jnp.where(take, c, 0)
        update(qi, 1, True)
        l = l_ref[...]
        o_ref[...] = (acc_ref[...] / l).T.astype(o_ref.dtype)
        lse_ref[...] = m_ref[...] * EXP2_SCALE + jnp.log2(l)

    return _call(
        "flash_fwd", body, [q, k, vt],
        [pl.BlockSpec((tq, QPAD), lambda hh, i: (i, hh)),
         pl.BlockSpec((t, QPAD), lambda hh, i: (0, hh)),
         pl.BlockSpec((None, nq, VDIM, tq), lambda hh, i: (hh, 0, 0, 0))],
        [_sds((t, h * VDIM), BF16), _sds((h, nq, 1, tq), F32)],
        [pl.BlockSpec((tq, VDIM), lambda hh, i: (i, hh)),
         pl.BlockSpec((None, None, 1, tq), lambda hh, i: (hh, i, 0, 0))],
        [pltpu.VMEM((1, tq), F32), pltpu.VMEM((1, tq), F32), pltpu.VMEM((VDIM, tq), F32)], (h, nq), comm)


def _flash_bwd(q, k, v, o, do, lse, tabs, tq, comm=None):
    t = q.shape[0]
    h = q.shape[1] // QPAD
    nq = t // tq

    def body(q_ref, k_ref, v_ref, o_ref, do_ref, lse_ref, cc_ref, sa_ref, sb_ref, dq_ref, dk_ref, dv_ref, delta_ref, dqt_ref):
        kj = pl.program_id(1)

        @pl.when(kj == 0)
        def _():
            dqt_ref[...] = jnp.zeros_like(dqt_ref)
            ones = jnp.ones((8, VDIM), BF16)
            for qi in range(nq):
                rows = pl.ds(qi * tq, tq)
                prod = do_ref[rows, :].astype(F32) * o_ref[rows, :].astype(F32)
                hi = prod.astype(BF16)
                lo = (prod - hi.astype(F32)).astype(BF16)
                delta_ref[qi] = (lax.dot_general(ones, hi, NT, preferred_element_type=F32)
                                 + lax.dot_general(ones, lo, NT, preferred_element_type=F32))

        kb = k_ref[...]
        vb = v_ref[...]
        kbt = kb.astype(F32).T.astype(BF16)
        dk_ref[...] = jnp.zeros_like(dk_ref)
        dv_ref[...] = jnp.zeros_like(dv_ref)

        def step(qi, masked):
            rows = pl.ds(pl.multiple_of(qi * tq, tq), tq)
            qb = q_ref[rows, :]
            dob = do_ref[rows, :]
            st = lax.dot_general(kb, qb, NT, preferred_element_type=F32)
            pt = jnp.exp2(st * EXP2_SCALE - lse_ref[qi])
            if masked:
                key = lax.broadcasted_iota(jnp.int32, (tq, tq), 0)
                qry = lax.broadcasted_iota(jnp.int32, (tq, tq), 1)
                pt = jnp.where(key <= qry, pt, 0.0)
            dv_ref[...] += lax.dot_general(pt.astype(BF16), dob, NN, preferred_element_type=F32)
            dpt = lax.dot_general(vb, dob, NT, preferred_element_type=F32)
            dst = (pt * (dpt - delta_ref[qi, pl.ds(0, 1), :]) * ATTN_SCALE).astype(BF16)
            dk_ref[...] += lax.dot_general(dst, qb, NN, preferred_element_type=F32)
            dqt_ref[qi] += lax.dot_general(kbt, dst, NN, preferred_element_type=F32)

        step(kj, True)

        def loop_body(qi, carry):
            step(qi, False)
            return carry

        lax.fori_loop(kj + 1, nq, loop_body, 0)

        @pl.when(kj == nq - 1)
        def _():
            for qi in range(nq):
                rows = pl.ds(qi * tq, tq)
                d = dqt_ref[qi].T
                roped = _rope_bwd(d[:, NOPE:], cc_ref[rows, :], sa_ref[rows, :], sb_ref[rows, :])
                dq_ref[rows, :] = jnp.concatenate([d[:, :NOPE], roped], axis=1).astype(BF16)

    head_q = pl.BlockSpec((t, QPAD), lambda hh, j: (0, hh))
    head_v = pl.BlockSpec((t, VDIM), lambda hh, j: (0, hh))
    table = pl.BlockSpec((t, 128), lambda hh, j: (0, 0))
    return _call(
        "flash_bwd", body, [q, k, v, o, do, lse, *tabs],
        [head_q, pl.BlockSpec((tq, QPAD), lambda hh, j: (j, hh)), pl.BlockSpec((tq, VDIM), lambda hh, j: (j, hh)), head_v, head_v,
         pl.BlockSpec((None, nq, 1, tq), lambda hh, j: (hh, 0, 0, 0)), table, table, table],
        [_sds((t, h * QPAD), BF16), _sds((t, h * QPAD), F32), _sds((t, h * VDIM), F32)],
        [head_q, pl.BlockSpec((tq, QPAD), lambda hh, j: (j, hh)), pl.BlockSpec((tq, VDIM), lambda hh, j: (j, hh))],
        [pltpu.VMEM((nq, 8, tq), F32), pltpu.VMEM((nq, QPAD, tq), F32)], (h, nq), comm)


def _tril_bf16(w):
    row = lax.broadcasted_iota(jnp.int32, w.shape, 0)
    col = lax.broadcasted_iota(jnp.int32, w.shape, 1)
    return jnp.where(col <= row, w, 0.0).astype(BF16)


def _layer_norm_parts(v0):
    mu = jnp.mean(v0, axis=-1, keepdims=True)
    vc = v0 - mu
    rstd = lax.rsqrt(jnp.mean(vc * vc, axis=-1, keepdims=True) + LN_EPS)
    return vc * rstd, rstd


def _sgu_mid_fwd(ge, ln_g, ln_b, w_sp, b_sp, chunks_per_step):
    t, e2 = ge.shape
    e = e2 // 2
    gd = e // SGU_GROUPS
    rows = SGU_CHUNK * chunks_per_step

    def body(u_ref, v_ref, g_ref, b_ref, w_ref, bs_ref, gate_ref):
        for ck in range(chunks_per_step):
            r = pl.ds(ck * SGU_CHUNK, SGU_CHUNK)
            xh, _ = _layer_norm_parts(v_ref[r, :].astype(F32))
            v1 = (xh * g_ref[...] + b_ref[...]).astype(BF16)
            for g in range(SGU_GROUPS):
                cols = pl.ds(g * gd, gd)
                mixed = lax.dot_general(_tril_bf16(w_ref[g]), v1[:, g * gd:(g + 1) * gd], NN, preferred_element_type=F32) + bs_ref[g]
                gate_ref[r, cols] = (u_ref[r, cols].astype(F32) * mixed).astype(BF16)

    return _pcall(
        body, name="sgu_mid_fwd", grid=(t // rows,),
        in_specs=[pl.BlockSpec((rows, e), lambda i: (i, 0)), pl.BlockSpec((rows, e), lambda i: (i, 1)),
                  _const_spec((1, e)), _const_spec((1, e)), _const_spec(w_sp.shape), _const_spec(b_sp.shape)],
        out_specs=pl.BlockSpec((rows, e), lambda i: (i, 0)),
        out_shape=_sds((t, e), BF16), compiler_params=_params(1),
    )(ge, ge, ln_g, ln_b, w_sp, b_sp)


def _sgu_mid_bwd(ge, gp, dgate, ln_g, ln_b, w_sp, b_sp, chunks_per_step):
    t, e2 = ge.shape
    e = e2 // 2
    gd = e // SGU_GROUPS
    rows = SGU_CHUNK * chunks_per_step

    def body(u_ref, v_ref, zu_ref, zv_ref, dg_ref, g_ref, b_ref, w_ref, bs_ref, dz_ref, dw_ref, dbs_ref, dlg_ref, dlb_ref):
        @pl.when(pl.program_id(0) == 0)
        def _():
            dw_ref[...] = jnp.zeros_like(dw_ref)
            dbs_ref[...] = jnp.zeros_like(dbs_ref)
            dlg_ref[...] = jnp.zeros_like(dlg_ref)
            dlb_ref[...] = jnp.zeros_like(dlb_ref)

        for ck in range(chunks_per_step):
            r = pl.ds(ck * SGU_CHUNK, SGU_CHUNK)
            xh, rstd = _layer_norm_parts(v_ref[r, :].astype(F32))
            v1 = (xh * g_ref[...] + b_ref[...]).astype(BF16)
            dv1_parts = []
            for g in range(SGU_GROUPS):
                cols = pl.ds(g * gd, gd)
                wc = _tril_bf16(w_ref[g])
                v1g = v1[:, g * gd:(g + 1) * gd]
                mixed = lax.dot_general(wc, v1g, NN, preferred_element_type=F32) + bs_ref[g]
                dgate = dg_ref[r, cols].astype(F32)
                dmixed = dgate * u_ref[r, cols].astype(F32)
                du = dgate * mixed
                dz_ref[r, cols] = (du * zu_ref[r, cols].astype(F32)).astype(BF16)
                dbs_ref[g] += jnp.sum(dmixed, axis=1, keepdims=True)
                dmb = dmixed.astype(BF16)
                dwg = lax.dot_general(dmb, v1g, NT, preferred_element_type=F32)
                row = lax.broadcasted_iota(jnp.int32, dwg.shape, 0)
                col = lax.broadcasted_iota(jnp.int32, dwg.shape, 1)
                dw_ref[g] += jnp.where(col <= row, dwg, 0.0)
                dv1_parts.append(lax.dot_general(wc, dmb, TN, preferred_element_type=F32))
            dv1 = jnp.concatenate(dv1_parts, axis=1)
            dlg_ref[...] += jnp.sum(dv1 * xh, axis=0, keepdims=True)
            dlb_ref[...] += jnp.sum(dv1, axis=0, keepdims=True)
            dxh = dv1 * g_ref[...]
            dv0 = rstd * (dxh - jnp.mean(dxh, axis=-1, keepdims=True) - xh * jnp.mean(dxh * xh, axis=-1, keepdims=True))
            dz_ref[r, pl.ds(e, e)] = (dv0 * zv_ref[r, :].astype(F32)).astype(BF16)

    half0 = pl.BlockSpec((rows, e), lambda i: (i, 0))
    half1 = pl.BlockSpec((rows, e), lambda i: (i, 1))
    return _pcall(
        body, name="sgu_mid_bwd", grid=(t // rows,),
        in_specs=[half0, half1, half0, half1, half0, _const_spec((1, e)), _const_spec((1, e)), _const_spec(w_sp.shape), _const_spec(b_sp.shape)],
        out_specs=[pl.BlockSpec((rows, e2), lambda i: (i, 0)), _const_spec(w_sp.shape), _const_spec(b_sp.shape), _const_spec((1, e)), _const_spec((1, e))],
        out_shape=[_sds((t, e2), BF16), _sds(w_sp.shape, F32), _sds(b_sp.shape, F32), _sds((1, e), F32), _sds((1, e), F32)],
        compiler_params=_params(1),
    )(ge, ge, gp, gp, dgate, ln_g, ln_b, w_sp, b_sp)


def kernel(x, positions, norm_mix, norm_ffn, final_norm, mla_w_dkv, mla_q_norm, mla_kv_norm, mla_w_uq, mla_w_ukv, mla_w_o, sgu_w_in, sgu_ln_g, sgu_ln_b, sgu_w_spatial, sgu_b_spatial, sgu_w_out, ffn_w_up, ffn_w_down, loss_target, m_norm_mix, m_norm_ffn, m_final_norm, m_mla_w_dkv, m_mla_q_norm, m_mla_kv_norm, m_mla_w_uq, m_mla_w_ukv, m_mla_w_o, m_sgu_w_in, m_sgu_ln_g, m_sgu_ln_b, m_sgu_w_spatial, m_sgu_b_spatial, m_sgu_w_out, m_ffn_w_up, m_ffn_w_down, v_norm_mix, v_norm_ffn, v_final_norm, v_mla_w_dkv, v_mla_q_norm, v_mla_kv_norm, v_mla_w_uq, v_mla_w_ukv, v_mla_w_o, v_sgu_w_in, v_sgu_ln_g, v_sgu_ln_b, v_sgu_w_spatial, v_sgu_b_spatial, v_sgu_w_out, v_ffn_w_up, v_ffn_w_down):
    _, T, D = x.shape
    depth = norm_mix.shape[0]
    n_mla, n_sgu = mla_w_dkv.shape[0], sgu_w_in.shape[0]
    assert depth % 2 == 0
    FF = ffn_w_up.shape[2] * N_DEV
    E = sgu_w_out.shape[1] * N_DEV
    ffc, ec, e2c = FF // N_DEV, E // N_DEV, 2 * E // N_DEV
    dc = D // N_DEV
    OW = HEADS * VDIM
    HW = HEADS * QPAD
    owc = OW // N_DEV
    tm = _tile(T, 1024)
    tb = _tile(T, 4096)
    tk = _tile(T, 512)
    tq = _tile(T, 512)
    ts = _tile(T, 256)
    nt = T // tm
    x2 = x.reshape(T, D)
    tgt = loss_target.reshape(T, D)
    cidx = lax.axis_index("c").astype(jnp.int32).reshape(1)

    ln_local = jnp.concatenate([sgu_ln_g, sgu_ln_b, jnp.zeros((8 - 2 * n_sgu, ec), F32)], axis=0)
    g_dkv, g_uq, g_ukv, g_o, g_ln = _all_gather(
        "gather_small_weights", [w.astype(BF16) for w in (mla_w_dkv, mla_w_uq, mla_w_ukv, mla_w_o)] + [ln_local])
    w_dkv = jnp.pad(g_dkv.transpose(1, 0, 2, 3).reshape(n_mla, D, LAT), ((0, 0), (0, 0), (0, LAT_PAD - LAT)))
    w_uq = jnp.pad(g_uq, ((0, 0), (0, 0), (0, 0), (0, QPAD - NOPE - ROPE))).transpose(1, 2, 0, 3).reshape(n_mla, Q_RANK, HEADS * QPAD)
    w_ukv = g_ukv.transpose(1, 2, 0, 3).reshape(n_mla, KV_RANK, HEADS * (NOPE + VDIM))
    w_o = g_o.transpose(1, 0, 2, 3).reshape(n_mla, OW, D)
    ln_g_full = [g_ln[:, l, :].reshape(1, E) for l in range(n_sgu)]
    ln_b_full = [g_ln[:, n_sgu + l, :].reshape(1, E) for l in range(n_sgu)]
    b_sp = sgu_b_spatial.reshape(n_sgu, SGU_GROUPS, SGU_CHUNK, 1)
    up_sh = [ffn_w_up[i].astype(BF16) for i in range(depth)]
    down_sh = [ffn_w_down[i].astype(BF16) for i in range(depth)]
    in_sh = [sgu_w_in[l].astype(BF16) for l in range(n_sgu)]
    out_sh = [sgu_w_out[l].astype(BF16) for l in range(n_sgu)]
    g_up, g_down, g_in, g_out = [None] * depth, [None] * depth, [None] * n_sgu, [None] * n_sgu

    inv_freq = ROPE_THETA ** (-jnp.arange(0, ROPE, 2, dtype=F32) / ROPE)
    zeros32 = jnp.zeros((ROPE // 2,), F32)
    inv128 = jnp.concatenate([inv_freq, inv_freq, zeros32, zeros32]).reshape(1, 128)
    sel_a = jnp.concatenate([-jnp.ones((32,), F32), zeros32, zeros32, zeros32]).reshape(1, 128)
    sel_b = jnp.concatenate([zeros32, jnp.ones((32,), F32), zeros32, zeros32]).reshape(1, 128)
    sel_c = jnp.concatenate([jnp.ones((64,), F32), zeros32, zeros32]).reshape(1, 128)

    def rope_tables(pos, inv, sa, sb, sc):
        ang = pos.astype(F32) * inv
        cs, sn = jnp.cos(ang), jnp.sin(ang)
        return cs * sc, sn * sa, sn * sb

    t_cc, t_sa, t_sb = _rowwise(
        "rope_tables", rope_tables, [positions.reshape(T, 1), inv128, sel_a, sel_b, sel_c], grid=(nt,),
        in_specs=[_row_spec(tm, 1)] + [_const_spec((1, 128))] * 4,
        out_shapes=[_sds((T, 128), F32)] * 3, out_specs=[_row_spec(tm, 128)] * 3)
    tab_specs = [_row_spec(tm, 128)] * 3

    def rmsnorm(xv, g):
        return _rowwise("rmsnorm", lambda a, gg: _rms_fwd(a, gg), [xv, g.reshape(1, D)], grid=(nt,),
                        in_specs=[_row_spec(tm, D), _const_spec((1, D))], out_shapes=_sds((T, D), BF16), out_specs=_row_spec(tm, D))

    def proj_cols(name, h, gw, nc, epilogue, n_out, comm=None):
        return _matmul(name, h, gw, [], grid=(N_DEV, T // tb),
                       a_spec=pl.BlockSpec((tb, D), lambda j, i: (i, 0)),
                       b_spec=pl.BlockSpec((None, D, nc), lambda j, i: (j, 0, 0)), extra_specs=[],
                       out_shapes=[_sds((T, nc * N_DEV), BF16)] * n_out, out_specs=[pl.BlockSpec((tb, nc), lambda j, i: (i, j))] * n_out,
                       dims=NN, epilogue=epilogue, comm=comm)

    def residual_norm(acc, xr, g):
        xn = acc + xr
        return xn, _rms_fwd(xn, g)

    def proj_rows_residual(name, a, gw, xres, g_next):
        kk_ = a.shape[1]
        return _matmul(name, a, gw.reshape(kk_, D), [xres, g_next.reshape(1, D)], grid=(T // tk,),
                       a_spec=_row_spec(tk, kk_), b_spec=_const_spec((kk_, D)), extra_specs=[_row_spec(tk, D), _const_spec((1, D))],
                       out_shapes=[_sds((T, D), F32), _sds((T, D), BF16)], out_specs=[_row_spec(tk, D)] * 2,
                       dims=NN, epilogue=residual_norm)

    def back_rows(name, dy, gw, kc, extras, epilogue, comm=None):
        return _matmul(name, dy, gw, extras, grid=(N_DEV, T // tb),
                       a_spec=pl.BlockSpec((tb, D), lambda j, i: (i, 0)),
                       b_spec=pl.BlockSpec((None, kc, D), lambda j, i: (j, 0, 0)),
                       extra_specs=[pl.BlockSpec((tb, kc), lambda j, i: (i, j))] * len(extras),
                       out_shapes=[_sds((T, kc * N_DEV), BF16)], out_specs=[pl.BlockSpec((tb, kc), lambda j, i: (i, j))],
                       dims=NT, epilogue=epilogue, comm=comm)

    def norm_bwd_epilogue(dh, xv, g, dxi):
        dxn, dg = _rms_bwd(xv, g, dh)
        return dxi + dxn, dxi + dxn, dg

    def transposed(gw):
        return gw.transpose(0, 2, 1).reshape(gw.shape[0] * gw.shape[2], D)

    def back_cols(name, da, gwt, xv, g, dx_in, comm=None):
        n = da.shape[1]
        row = _row_spec(tk, D)
        return _matmul(name, da, gwt, [xv, g.reshape(1, D), dx_in], grid=(T // tk,),
                       a_spec=_row_spec(tk, n), b_spec=_const_spec((n, D)), extra_specs=[row, _const_spec((1, D)), row],
                       out_shapes=[_sds((T, D), F32), _sds((T, D), BF16), _sds((1, D), F32)], out_specs=[row, row, _const_spec((1, D))],
                       dims=NN, epilogue=norm_bwd_epilogue, n_sum=1, comm=comm)

    def token_sum(tt):
        return dict(k_axis=1, nk=T // tt) if T // tt > 1 else dict(k_axis=None)

    def wgrad_cols(name, h, da, nc):
        return _matmul(name, h, da, [], grid=(N_DEV, T // tb),
                       a_spec=pl.BlockSpec((tb, D), lambda j, t: (t, 0)), b_spec=pl.BlockSpec((tb, nc), lambda j, t: (t, j)),
                       extra_specs=[], out_shapes=[_sds((N_DEV, D, nc), BF16)],
                       out_specs=[pl.BlockSpec((None, D, nc), lambda j, t: (j, 0, 0))],
                       dims=TN, acc_shape=(D, nc), **token_sum(tb))[0]

    def wgrad_rows(name, a, dy, kc, ncols, tt):
        return _matmul(name, a, dy, [], grid=(a.shape[1] // kc, T // tt),
                       a_spec=pl.BlockSpec((tt, kc), lambda j, t: (t, j)), b_spec=pl.BlockSpec((tt, ncols), lambda j, t: (t, 0)),
                       extra_specs=[], out_shapes=[_sds((a.shape[1], ncols), BF16)],
                       out_specs=[pl.BlockSpec((kc, ncols), lambda j, t: (j, 0))],
                       dims=TN, acc_shape=(kc, ncols), **token_sum(tt))[0]

    saved = []
    xs = x2
    for i in range(depth):
        l = i // 2
        if i == 0:
            h = rmsnorm(xs, norm_mix[0])
        if i % 2 == 0:
            lat = _matmul("mla_down", h, w_dkv, [], grid=(nt,), a_spec=_row_spec(tm, D),
                          b_spec=pl.BlockSpec((None, D, LAT_PAD), lambda i_: (l, 0, 0)), extra_specs=[],
                          out_shapes=[_sds((T, LAT_PAD), F32)], out_specs=[_row_spec(tm, LAT_PAD)], dims=NN)[0]

            def latent_post(la, qn, kvn, cc, sa, sb):
                cq = _rms_fwd(la[:, :Q_RANK], qn)
                ckv = _rms_fwd(la[:, Q_RANK:Q_RANK + KV_RANK], kvn)
                kr = _rope_fwd(la[:, Q_RANK + KV_RANK:], cc, sa, sb)
                return cq, ckv, kr

            cq, ckv, kr = _rowwise(
                "mla_latent", latent_post, [lat, mla_q_norm[l].reshape(1, Q_RANK), mla_kv_norm[l].reshape(1, KV_RANK), t_cc, t_sa, t_sb],
                grid=(nt,), in_specs=[_row_spec(tm, LAT_PAD), _const_spec((1, Q_RANK)), _const_spec((1, KV_RANK))] + tab_specs,
                out_shapes=[_sds((T, Q_RANK), BF16), _sds((T, KV_RANK), BF16), _sds((T, 128), BF16)],
                out_specs=[_row_spec(tm, Q_RANK), _row_spec(tm, KV_RANK), _row_spec(tm, 128)])

            def q_epilogue(acc, cc, sa, sb):
                parts = []
                for b in range(HEADS):
                    parts += [acc[:, b * QPAD:b * QPAD + NOPE], _rope_fwd(acc[:, b * QPAD + NOPE:(b + 1) * QPAD], cc, sa, sb)]
                return (jnp.concatenate(parts, axis=1),)

            q = _matmul("mla_q", cq, w_uq, [t_cc, t_sa, t_sb], grid=(nt,), a_spec=_row_spec(tm, Q_RANK),
                        b_spec=pl.BlockSpec((None, Q_RANK, HW), lambda i_: (l, 0, 0)), extra_specs=tab_specs,
                        out_shapes=[_sds((T, HW), BF16)], out_specs=[_row_spec(tm, HW)], dims=NN, epilogue=q_epilogue)[0]

            def kv_write(outs, acc, krb):
                k_ref, v_ref, vt_ref = outs
                for b in range(HEADS):
                    vb = acc[:, b * QPAD + NOPE:(b + 1) * QPAD]
                    k_ref[:, b * QPAD:b * QPAD + NOPE] = acc[:, b * QPAD:b * QPAD + NOPE].astype(BF16)
                    k_ref[:, b * QPAD + NOPE:(b + 1) * QPAD] = krb
                    v_ref[:, b * VDIM:(b + 1) * VDIM] = vb.astype(BF16)
                    vbt = vb.T.astype(BF16)
                    for u in range(tm // tq):
                        vt_ref[b, u] = vbt[:, u * tq:(u + 1) * tq]

            kk, vv, vt = _matmul("mla_kv", ckv, w_ukv, [kr], grid=(nt,), a_spec=_row_spec(tm, KV_RANK),
                                 b_spec=pl.BlockSpec((None, KV_RANK, HW), lambda i_: (l, 0, 0)), extra_specs=[_row_spec(tm, 128)],
                                 out_shapes=[_sds((T, HW), BF16), _sds((T, OW), BF16), _sds((HEADS, T // tq, VDIM, tq), BF16)],
                                 out_specs=[_row_spec(tm, HW), _row_spec(tm, OW), pl.BlockSpec((HEADS, tm // tq, VDIM, tq), lambda i_: (0, i_, 0, 0))],
                                 dims=NN, write=kv_write)
            group = [up_sh[i], down_sh[i], in_sh[l], out_sh[l], up_sh[i + 1], down_sh[i + 1]]
            o, lse, *bufs = _flash_fwd(q, kk, vt, tq, comm=_gather_level1(group))
            xm, h2, g_up[i], g_down[i] = _matmul(
                "mla_out", o, w_o, [xs, norm_ffn[i].reshape(1, D)], grid=(nt,), a_spec=_row_spec(tm, OW),
                b_spec=pl.BlockSpec((None, OW, D), lambda i_: (l, 0, 0)), extra_specs=[_row_spec(tm, D), _const_spec((1, D))],
                out_shapes=[_sds((T, D), F32), _sds((T, D), BF16)], out_specs=[_row_spec(tm, D)] * 2, dims=NN,
                epilogue=residual_norm, comm=_gather_level2(bufs[:2]))
            half_gathered = bufs[2:]
            mix_saved = (h, lat, cq, ckv, q, kk, vv, o, lse)
        else:
            gp, ge = proj_cols("sgu_in", h, g_in[l], e2c, _gelu_and_grad, 2)
            gate = _sgu_mid_fwd(ge, ln_g_full[l], ln_b_full[l], sgu_w_spatial[l], b_sp[l], 4)
            xm, h2 = proj_rows_residual("sgu_out", gate, g_out[l], xs, norm_ffn[i])
            mix_saved = (h, gp, ge, gate)
        r, s, *rest = proj_cols("ffn_up", h2, g_up[i], ffc, lambda acc: (jnp.maximum(acc, 0.0), jnp.square(jnp.maximum(acc, 0.0))), 2,
                                comm=_gather_level2(half_gathered) if i % 2 == 0 else None)
        if i % 2 == 0:
            g_in[l], g_out[l], g_up[i + 1], g_down[i + 1] = rest
        xo, h_next = proj_rows_residual("ffn_down", s, g_down[i], xm, norm_mix[i + 1] if i + 1 < depth else final_norm)
        saved.append((xs, xm, mix_saved, h2, r, s))
        xs, h = xo, h_next

    def loss_head(xv, tg, g):
        y = _rms_fwd(xv, g)
        err = y - tg
        part = 0.5 * jnp.sum(jnp.sum(err * err, axis=-1, keepdims=True), axis=0, keepdims=True) / D
        dx, dg = _rms_bwd(xv, g, err / D)
        return dx, dx, jnp.broadcast_to(part, (1, 128)), dg

    dx, dyb, loss_part, d_final = _rowwise(
        "loss_head", loss_head, [xs, tgt, final_norm.reshape(1, D)], grid=(nt,),
        in_specs=[_row_spec(tm, D), _row_spec(tm, D), _const_spec((1, D))],
        out_shapes=[_sds((T, D), F32), _sds((T, D), BF16), _sds((1, 128), F32), _sds((1, D), F32)],
        out_specs=[_row_spec(tm, D), _row_spec(tm, D), _const_spec((1, 128)), _const_spec((1, D))], n_acc=2)
    loss = lax.psum(loss_part[0, 0], ("x", "y", "c"))

    d_norm_mix, d_norm_ffn = [None] * depth, [None] * depth
    d_qn, d_kvn = [None] * n_mla, [None] * n_mla
    d_wsp, d_bsp, d_lng, d_lnb = [None] * n_sgu, [None] * n_sgu, [None] * n_sgu, [None] * n_sgu
    layers = {"dkv": n_mla, "uq": n_mla, "ukv": n_mla, "o": n_mla, "in": n_sgu, "out": n_sgu, "up": depth, "down": depth}
    stacked = {nm: None for nm in layers}
    pending = []
    summed = []

    def add_pair(g, rcv):
        _, rws, cls = g.shape
        g4 = g.reshape(N_CHIP, 2, rws, cls)
        rt = _tile(rws, 512)
        return _rowwise("grad_pair_sum", lambda a, b_: a.astype(F32) + b_.astype(F32), [g4, rcv], grid=(N_CHIP, rws // rt),
                        in_specs=[pl.BlockSpec((None, None, rt, cls), lambda ch, i_, cr: (ch, cr[0], i_, 0)),
                                  pl.BlockSpec((None, rt, cls), lambda ch, i_, cr: (ch, i_, 0))],
                        out_shapes=_sds(rcv.shape, BF16), out_specs=pl.BlockSpec((None, rt, cls), lambda ch, i_, cr: (ch, i_, 0)),
                        grid_spec_prefetch=cidx)

    def sibling_comm():
        return _sibling_exchange([g for _, _, g in pending]) if pending else None

    def absorb(from_sibling):
        for (nm, l_, g), rcv in zip(pending, from_sibling):
            summed.append((nm, l_, add_pair(g, rcv)))
        pending.clear()

    def chip_comm():
        if pending:
            absorb(_comm_call("grad_sibling_exchange", sibling_comm()))
        comm, names = _chip_exchange([p for _, _, p in summed], [(nm, l_) for nm, l_, _ in summed], layers, stacked)
        summed.clear()
        return comm, names

    def rows128(a, rows):
        flat = a.reshape(-1, 128)
        return jnp.pad(flat, ((0, rows - flat.shape[0]), (0, 0)))

    def pad_to(n, mult):
        return -(-n // mult) * mult

    def packed(arrs, sizes):
        return jnp.concatenate([rows128(a, sz) for a, sz in zip(arrs, sizes)], axis=0)

    n_wsp, n_bsp, n_ln = sgu_w_spatial.size // 128, pad_to(sgu_b_spatial.size // 128, 8), pad_to(n_sgu * E // 128, 8)
    early_sizes = [n_wsp, pad_to(n_wsp + n_bsp, SMALL_ROWS) - n_wsp, n_ln, n_ln]
    early_rep = early_sizes[0] + early_sizes[1]
    gathered_early = None

    for i in reversed(range(depth)):
        l = i // 2
        xs_i, xm, mix_saved, h2, r, s = saved[i]
        da, *rcv = back_rows("ffn_down_bwd", dyb, g_down[i], ffc, [r], lambda acc, rr: (acc * (2.0 * rr.astype(F32)),), comm=sibling_comm())
        absorb(rcv)
        pending.append(("down", i, wgrad_rows("ffn_down_wgrad", s, dyb, ffc, D, tb).reshape(N_DEV, ffc, D)))
        pending.append(("up", i, wgrad_cols("ffn_up_wgrad", h2, da, ffc)))
        dx, dyb, d_norm_ffn[i], *rcv = back_cols("ffn_up_bwd", da, transposed(g_up[i]), xm, norm_ffn[i], dx, comm=sibling_comm())
        absorb(rcv)
        if i % 2 == 0:
            h, lat, cq, ckv, q, kk, vv, o, lse = mix_saved
            do = _matmul("mla_out_bwd", dyb, w_o, [], grid=(nt,), a_spec=_row_spec(tm, D),
                         b_spec=pl.BlockSpec((None, OW, D), lambda i_: (l, 0, 0)), extra_specs=[],
                         out_shapes=[_sds((T, OW), BF16)], out_specs=[_row_spec(tm, OW)], dims=NT)[0]
            g_o_l = wgrad_rows("mla_out_wgrad", o, dyb, OW, D, tm).reshape(N_DEV, owc, D)
            comm, names = chip_comm()
            if i == 0:
                early = packed([jnp.stack(d_wsp, 0), jnp.stack(d_bsp, 0), jnp.concatenate(d_lng, 0), jnp.concatenate(d_lnb, 0)], early_sizes)
                comm = _merge_comm(comm, _gather_level1([early]))
            dq_pre, dk, dv, *bufs = _flash_bwd(q, kk, vv, o, do, lse, (t_cc, t_sa, t_sb), tq, comm=comm)
            stacked.update(dict(zip(names, bufs)))
            pending.append(("o", l, g_o_l))

            def kv_pre(dkb, dvb, cc, sa, sb):
                parts, dkr = [], None
                for b in range(HEADS):
                    parts += [dkb[:, b * QPAD:b * QPAD + NOPE], dvb[:, b * VDIM:(b + 1) * VDIM]]
                    piece = dkb[:, b * QPAD + NOPE:(b + 1) * QPAD]
                    dkr = piece if dkr is None else dkr + piece
                return jnp.concatenate(parts, axis=1), _rope_bwd(dkr, cc, sa, sb)

            dkv, dkr, *rest = _rowwise("mla_dkv_rope", kv_pre, [dk, dv, t_cc, t_sa, t_sb], grid=(T // ts,),
                                       in_specs=[_row_spec(ts, HW), _row_spec(ts, OW)] + [_row_spec(ts, 128)] * 3,
                                       out_shapes=[_sds((T, HW), BF16), _sds((T, 128), F32)], out_specs=[_row_spec(ts, HW), _row_spec(ts, 128)],
                                       comm=_gather_level2(bufs[len(names):]) if i == 0 else None)
            if i == 0:
                (gathered_early,) = rest
            g_uq_l = wgrad_rows("mla_q_wgrad", cq, dq_pre, Q_RANK, HW, tm)
            g_ukv_l = wgrad_rows("mla_kv_wgrad", ckv, dkv, KV_RANK, HW, tm)
            pending.append(("uq", l, g_uq_l.reshape(Q_RANK, HEADS, QPAD)[:, :, :NOPE + ROPE].transpose(1, 0, 2)))
            pending.append(("ukv", l, g_ukv_l.reshape(KV_RANK, HEADS, NOPE + VDIM).transpose(1, 0, 2)))
            dcq = _matmul("mla_q_bwd", dq_pre, w_uq, [], grid=(nt,), a_spec=_row_spec(tm, HW),
                          b_spec=pl.BlockSpec((None, Q_RANK, HW), lambda i_: (l, 0, 0)), extra_specs=[],
                          out_shapes=[_sds((T, Q_RANK), F32)], out_specs=[_row_spec(tm, Q_RANK)], dims=NT)[0]
            dckv = _matmul("mla_kv_bwd", dkv, w_ukv, [], grid=(nt,), a_spec=_row_spec(tm, HW),
                           b_spec=pl.BlockSpec((None, KV_RANK, HW), lambda i_: (l, 0, 0)), extra_specs=[],
                           out_shapes=[_sds((T, KV_RANK), F32)], out_specs=[_row_spec(tm, KV_RANK)], dims=NT)[0]

            def latent_bwd(la, qn, kvn, dq_, dkv_, dkr_):
                dcq_raw, dqn = _rms_bwd(la[:, :Q_RANK], qn, dq_)
                dckv_raw, dkvn = _rms_bwd(la[:, Q_RANK:Q_RANK + KV_RANK], kvn, dkv_)
                return jnp.concatenate([dcq_raw, dckv_raw, dkr_], axis=1), dqn, dkvn

            dlat, d_qn[l], d_kvn[l] = _rowwise(
                "mla_latent_bwd", latent_bwd, [lat, mla_q_norm[l].reshape(1, Q_RANK), mla_kv_norm[l].reshape(1, KV_RANK), dcq, dckv, dkr],
                grid=(nt,), in_specs=[_row_spec(tm, LAT_PAD), _const_spec((1, Q_RANK)), _const_spec((1, KV_RANK)),
                                      _row_spec(tm, Q_RANK), _row_spec(tm, KV_RANK), _row_spec(tm, 128)],
                out_shapes=[_sds((T, LAT_PAD), BF16), _sds((1, Q_RANK), F32), _sds((1, KV_RANK), F32)],
                out_specs=[_row_spec(tm, LAT_PAD), _const_spec((1, Q_RANK)), _const_spec((1, KV_RANK))], n_acc=2)
            g_dkv_l = wgrad_rows("mla_down_wgrad", h, dlat, D, LAT_PAD, tm)
            pending.append(("dkv", l, g_dkv_l[:, :LAT].reshape(N_DEV, dc, LAT)))
            dx, dyb, d_norm_mix[i] = _matmul(
                "mla_down_bwd", dlat, w_dkv, [xs_i, norm_mix[i].reshape(1, D), dx], grid=(nt,), a_spec=_row_spec(tm, LAT_PAD),
                b_spec=pl.BlockSpec((None, D, LAT_PAD), lambda i_: (l, 0, 0)), extra_specs=[_row_spec(tm, D), _const_spec((1, D)), _row_spec(tm, D)],
                out_shapes=[_sds((T, D), F32), _sds((T, D), BF16), _sds((1, D), F32)],
                out_specs=[_row_spec(tm, D), _row_spec(tm, D), _const_spec((1, D))], dims=NT, epilogue=norm_bwd_epilogue, n_sum=1)
        else:
            h, gp, ge, gate = mix_saved
            (dgate,) = back_rows("sgu_out_bwd", dyb, g_out[l], ec, [], None)
            pending.append(("out", l, wgrad_rows("sgu_out_wgrad", gate, dyb, ec, D, tb).reshape(N_DEV, ec, D)))
            dz, d_wsp[l], d_bsp[l], d_lng[l], d_lnb[l] = _sgu_mid_bwd(ge, gp, dgate, ln_g_full[l], ln_b_full[l], sgu_w_spatial[l], b_sp[l], 2)
            pending.append(("in", l, wgrad_cols("sgu_in_wgrad", h, dz, e2c)))
            dx, dyb, d_norm_mix[i], *rcv = back_cols("sgu_in_bwd", dz, transposed(g_in[l]), xs_i, norm_mix[i], dx, comm=sibling_comm())
            absorb(rcv)
    grad_x = dx.reshape(1, T, D)

    last_comm, last_names = chip_comm()
    late_g = [jnp.concatenate(d_norm_mix, 0), jnp.concatenate(d_norm_ffn, 0), d_final, jnp.concatenate(d_qn, 0), jnp.concatenate(d_kvn, 0)]
    late_w = [norm_mix, norm_ffn, final_norm, mla_q_norm, mla_kv_norm]
    late_m = [m_norm_mix, m_norm_ffn, m_final_norm, m_mla_q_norm, m_mla_kv_norm]
    late_v = [v_norm_mix, v_norm_ffn, v_final_norm, v_mla_q_norm, v_mla_kv_norm]
    late_sizes = [pad_to(g.size // 128, 8) for g in late_g]
    late_rows = sum(late_sizes)

    def adam_big(parts, w, m, v):
        lyr, rws, cls = w.shape
        rt = _tile(rws, 256)

        def fn(p, w_, m_, v_):
            g = (p[0].astype(F32) + p[1].astype(F32)) + (p[2].astype(F32) + p[3].astype(F32))
            return (g, *_adam(w_, g, m_, v_))

        spec = pl.BlockSpec((None, rt, cls), lambda l_, i_: (l_, i_, 0))
        return _rowwise("adam_large", fn, [parts, w, m, v], grid=(lyr, rws // rt),
                        in_specs=[pl.BlockSpec((N_CHIP, None, rt, cls), lambda l_, i_: (0, l_, i_, 0)), spec, spec, spec],
                        out_shapes=[_sds(w.shape, F32)] * 4, out_specs=[spec] * 4)

    stacked.update(dict(zip(last_names, _comm_call("grad_chip_exchange", last_comm))))
    (gathered_late,) = _all_gather("gather_small_grads", [packed(late_g, late_sizes)])
    big = {}
    big["in"] = adam_big(stacked["in"], sgu_w_in, m_sgu_w_in, v_sgu_w_in)
    big["up"] = adam_big(stacked["up"], ffn_w_up, m_ffn_w_up, v_ffn_w_up)
    big["down"] = adam_big(stacked["down"], ffn_w_down, m_ffn_w_down, v_ffn_w_down)
    big["out"] = adam_big(stacked["out"], sgu_w_out, m_sgu_w_out, v_sgu_w_out)
    big["dkv"] = adam_big(stacked["dkv"], mla_w_dkv, m_mla_w_dkv, v_mla_w_dkv)
    big["uq"] = adam_big(stacked["uq"], mla_w_uq, m_mla_w_uq, v_mla_w_uq)
    big["ukv"] = adam_big(stacked["ukv"], mla_w_ukv, m_mla_w_ukv, v_mla_w_ukv)
    big["o"] = adam_big(stacked["o"], mla_w_o, m_mla_w_o, v_mla_w_o)
    big_res = [big[nm][:4] for nm in ("dkv", "uq", "ukv", "o", "in", "out", "up", "down")]

    def sum8(p):
        return ((p[0] + p[1]) + (p[2] + p[3])) + ((p[4] + p[5]) + (p[6] + p[7]))

    def adam_packed(name, gathered, ws, ms, vs, sizes, rows, tile):
        spec = _row_spec(tile, 128)
        return _rowwise(name, lambda p, w_, m_, v_: (sum8(p), *_adam(w_, sum8(p), m_, v_)),
                        [gathered, packed(ws, sizes), packed(ms, sizes), packed(vs, sizes)], grid=(rows // tile,),
                        in_specs=[pl.BlockSpec((N_DEV, tile, 128), lambda i_: (0, i_, 0)), spec, spec, spec],
                        out_shapes=[_sds((rows, 128), F32)] * 4, out_specs=[spec] * 4)

    late_res = adam_packed("adam_small", gathered_late, late_w, late_m, late_v, late_sizes, late_rows, late_rows)
    early_res = adam_packed("adam_spatial", gathered_early, [sgu_w_spatial, sgu_b_spatial], [m_sgu_w_spatial, m_sgu_b_spatial],
                            [v_sgu_w_spatial, v_sgu_b_spatial], early_sizes[:2], early_rep, SMALL_ROWS)

    def unpack(res, sizes, k, like):
        off = sum(sizes[:k])
        return res[off:off + like.size // 128].reshape(like.shape)

    my_b = 4 * lax.axis_index("x") + 2 * lax.axis_index("y") + lax.axis_index("c")
    ln_w = jnp.concatenate([sgu_ln_g, sgu_ln_b], 0)
    ln_m = jnp.concatenate([m_sgu_ln_g, m_sgu_ln_b], 0)
    ln_v = jnp.concatenate([v_sgu_ln_g, v_sgu_ln_b], 0)
    ln_all = jnp.concatenate([gathered_early[:, early_rep:early_rep + n_sgu * E // 128], gathered_early[:, early_rep + n_ln:early_rep + n_ln + n_sgu * E // 128]], axis=1)
    ln_mine = lax.dynamic_slice_in_dim(ln_all.reshape(N_DEV, 2 * n_sgu, N_DEV, ec), my_b, 1, axis=2).reshape(N_DEV, 2 * n_sgu, ec)
    ln_g_, ln_d, ln_m2, ln_v2 = _rowwise(
        "adam_ln", lambda p, w_, m_, v_: (sum8(p), *_adam(w_, sum8(p), m_, v_)), [ln_mine, ln_w, ln_m, ln_v], grid=(1,),
        in_specs=[_const_spec(ln_mine.shape), _const_spec(ln_w.shape), _const_spec(ln_w.shape), _const_spec(ln_w.shape)],
        out_shapes=[_sds(ln_w.shape, F32)] * 4, out_specs=[_const_spec(ln_w.shape)] * 4)

    def family(pos):
        ln = [ln_g_, ln_d, ln_m2, ln_v2][pos]
        late = [unpack(late_res[pos], late_sizes, k, w_) for k, w_ in enumerate(late_w)]
        w_sp_, b_sp_ = unpack(early_res[pos], early_sizes, 0, sgu_w_spatial), unpack(early_res[pos], early_sizes, 1, sgu_b_spatial)
        bigs = [res[pos] for res in big_res]
        return [late[0], late[1], late[2], bigs[0], late[3], late[4], bigs[1], bigs[2], bigs[3],
                bigs[4], ln[:n_sgu], ln[n_sgu:], w_sp_, b_sp_, bigs[5], bigs[6], bigs[7]]

    return (loss, grad_x, *family(0), *family(1), *family(2), *family(3))
```

```python
import math

import jax
import jax.numpy as jnp
from jax import lax
from jax.experimental import pallas as pl
from jax.experimental.pallas import tpu as pltpu

F32 = jnp.float32
BF16 = jnp.bfloat16
MESH = pl.DeviceIdType.MESH

N_DEV = 8
N_CHIP = 4
HEADS = 8
NOPE = 128
ROPE = 64
VDIM = 128
QPAD = 256
Q_RANK = 256
KV_RANK = 128
LAT = Q_RANK + KV_RANK + ROPE
LAT_PAD = 512
ROPE_THETA = 10000.0
SGU_CHUNK = 128
SGU_GROUPS = 8
NORM_EPS = 1e-6
LN_EPS = 1e-5
ADAM_LR = 0.001
ADAM_B1 = 0.9
ADAM_B2 = 0.999
ADAM_EPS = 1e-08
ADAM_WD = 0.01
ADAM_STEP = 10
ATTN_SCALE = (NOPE + ROPE) ** -0.5
NEG = -1e30
EXP2_SCALE = ATTN_SCALE * math.log2(math.e)
VMEM_LIMIT = 56 * 1024 * 1024
SMALL_ROWS = 256

NN = (((1,), (0,)), ((), ()))
NT = (((1,), (1,)), ((), ()))
TN = (((0,), (0,)), ((), ()))
ANY = pl.BlockSpec(memory_space=pl.ANY)


def _pcall(body, **kw):
    return pl.pallas_call(body, **kw)


def _params(n_grid, side_effects=False):
    return pltpu.CompilerParams(dimension_semantics=("arbitrary",) * n_grid, vmem_limit_bytes=VMEM_LIMIT, has_side_effects=side_effects)


def _sds(shape, dtype):
    return jax.ShapeDtypeStruct(tuple(shape), dtype)


def _tile(n, want):
    t = min(n, want)
    assert n % t == 0, (n, want)
    return t


class _Comm:
    def __init__(self, operands, out_shapes, aliases, scratch, start, finish):
        self.operands, self.out_shapes, self.aliases, self.scratch = operands, out_shapes, aliases, scratch
        self.start, self.finish = start, finish


def _merge_comm(first, second):
    n_in, n_out, n_sc = len(first.operands), len(first.out_shapes), len(first.scratch)
    aliases = dict(first.aliases)
    aliases.update({n_in + k: n_out + v for k, v in second.aliases.items()})

    def start(ins, outs, sems):
        first.start(ins[:n_in], outs[:n_out], sems[:n_sc])
        second.start(ins[n_in:], outs[n_out:], sems[n_sc:])

    def finish(ins, outs, sems):
        first.finish(ins[:n_in], outs[:n_out], sems[:n_sc])
        second.finish(ins[n_in:], outs[n_out:], sems[n_sc:])

    return _Comm([*first.operands, *second.operands], [*first.out_shapes, *second.out_shapes], aliases,
                 [*first.scratch, *second.scratch], start, finish)


def _place():
    return lax.axis_index("x"), lax.axis_index("y"), lax.axis_index("c")


def _other_chips(x, y):
    return [(1 - x, y), (x, 1 - y), (1 - x, 1 - y)]


def _dev_index(dev):
    return 4 * dev[0] + 2 * dev[1] + dev[2]


def _comm_call(name, comm):
    c_in, c_out = len(comm.operands), len(comm.out_shapes)

    def body(*refs):
        ins, outs, sems = refs[:c_in], refs[c_in:c_in + c_out], refs[c_in + c_out:]
        comm.start(ins, outs, sems)
        comm.finish(ins, outs, sems)

    return _pcall(body, name=name, in_specs=[ANY] * c_in, out_specs=[ANY] * c_out, out_shape=comm.out_shapes,
                  scratch_shapes=comm.scratch, input_output_aliases=dict(comm.aliases),
                  compiler_params=pltpu.CompilerParams(has_side_effects=True))(*comm.operands)


def _call(name, body, operands, in_specs, out_shapes, out_specs, scratch, grid, comm=None):
    if comm is None:
        return _pcall(body, name=name, grid=grid, in_specs=in_specs, out_specs=out_specs, out_shape=out_shapes,
                      scratch_shapes=scratch, compiler_params=_params(len(grid)))(*operands)
    n_in, n_out, n_sc = len(operands), len(out_shapes), len(scratch)
    c_in, c_out = len(comm.operands), len(comm.out_shapes)

    def hosted(*refs):
        ins, cins = refs[:n_in], refs[n_in:n_in + c_in]
        o0 = n_in + c_in
        outs, couts = refs[o0:o0 + n_out], refs[o0 + n_out:o0 + n_out + c_out]
        rest = refs[o0 + n_out + c_out:]
        sc, csems = rest[:n_sc], rest[n_sc:]
        first = pl.program_id(0) == 0
        last = pl.program_id(0) == grid[0] - 1
        for d in range(1, len(grid)):
            first = jnp.logical_and(first, pl.program_id(d) == 0)
            last = jnp.logical_and(last, pl.program_id(d) == grid[d] - 1)

        @pl.when(first)
        def _():
            comm.start(cins, couts, csems)

        body(*ins, *outs, *sc)

        @pl.when(last)
        def _():
            comm.finish(cins, couts, csems)

    return _pcall(hosted, name=name, grid=grid, in_specs=[*in_specs, *[ANY] * c_in], out_specs=[*out_specs, *[ANY] * c_out],
                  out_shape=[*out_shapes, *comm.out_shapes], scratch_shapes=[*scratch, *comm.scratch],
                  input_output_aliases={n_in + k: n_out + v for k, v in comm.aliases.items()},
                  compiler_params=_params(len(grid), side_effects=True))(*operands, *comm.operands)


def _gather_level1(shards):
    n = len(shards)

    def copies(ins, outs, sems):
        send_sems, recv_sems, local_sems = sems
        x, y, c = _place()
        me, sibling = (x, y, c), (x, y, 1 - c)
        chips = _other_chips(x, y)

        def copy(a, k, block, to, src=None):
            slot = outs[a].at[_dev_index(block)]
            return pltpu.make_async_remote_copy(src_ref=slot if src is None else src, dst_ref=slot, send_sem=send_sems.at[a, k],
                                                recv_sem=recv_sems.at[a, k], device_id=to, device_id_type=MESH)

        mine = [pltpu.make_async_copy(ins[a], outs[a].at[_dev_index(me)], local_sems.at[a]) for a in range(n)]
        sends = [copy(a, 1 + j, me, (*chip, c), src=ins[a]) for j, chip in enumerate(chips) for a in range(n)]
        sends += [copy(a, 0, me, sibling, src=ins[a]) for a in range(n)]
        recvs = [copy(a, 1 + j, (*chip, c), me) for j, chip in enumerate(chips) for a in range(n)]
        recvs += [copy(a, 0, sibling, me) for a in range(n)]
        return mine, sends, recvs

    def start(ins, outs, sems):
        mine, sends, _ = copies(ins, outs, sems)
        for cp in mine + sends:
            cp.start()

    def finish(ins, outs, sems):
        mine, sends, recvs = copies(ins, outs, sems)
        for cp in recvs:
            cp.wait_recv()
        for cp in sends:
            cp.wait_send()
        for cp in mine:
            cp.wait()

    return _Comm(shards, [_sds((N_DEV, *a.shape), a.dtype) for a in shards], {},
                 [pltpu.SemaphoreType.DMA((n, 4)), pltpu.SemaphoreType.DMA((n, 4)), pltpu.SemaphoreType.DMA((n,))], start, finish)


def _gather_level2(bufs):
    n = len(bufs)

    def copies(outs, sems):
        send_sems, recv_sems = sems
        x, y, c = _place()
        sibling = (x, y, 1 - c)
        sends, recvs = [], []
        for j, chip in enumerate(_other_chips(x, y)):
            for a in range(n):
                have, want = outs[a].at[_dev_index((*chip, c))], outs[a].at[_dev_index((*chip, 1 - c))]
                sends.append(pltpu.make_async_remote_copy(src_ref=have, dst_ref=have, send_sem=send_sems.at[a, j], recv_sem=recv_sems.at[a, j],
                                                          device_id=sibling, device_id_type=MESH))
                recvs.append(pltpu.make_async_remote_copy(src_ref=want, dst_ref=want, send_sem=send_sems.at[a, j], recv_sem=recv_sems.at[a, j],
                                                          device_id=sibling, device_id_type=MESH))
        return sends, recvs

    def start(ins, outs, sems):
        for cp in copies(outs, sems)[0]:
            cp.start()

    def finish(ins, outs, sems):
        sends, recvs = copies(outs, sems)
        for cp in recvs:
            cp.wait_recv()
        for cp in sends:
            cp.wait_send()

    return _Comm(bufs, [_sds(b.shape, b.dtype) for b in bufs], {a: a for a in range(n)},
                 [pltpu.SemaphoreType.DMA((n, 3)), pltpu.SemaphoreType.DMA((n, 3))], start, finish)


def _all_gather(name, arrays):
    n = len(arrays)

    def body(*refs):
        ins = refs[:n]
        outs = refs[n:2 * n]
        send_sems, recv_sems, local_sems = refs[2 * n:]
        x, y, c = _place()
        me, sibling = (x, y, c), (x, y, 1 - c)
        chips = _other_chips(x, y)

        def copy(a, k, block, to, src=None):
            slot = outs[a].at[_dev_index(block)]
            return pltpu.make_async_remote_copy(src_ref=slot if src is None else src, dst_ref=slot, send_sem=send_sems.at[a, k],
                                                recv_sem=recv_sems.at[a, k], device_id=to, device_id_type=MESH)

        mine = [pltpu.make_async_copy(ins[a], outs[a].at[_dev_index(me)], local_sems.at[a]) for a in range(n)]
        for cp in mine:
            cp.start()
        first = []
        for j, chip in enumerate(chips):
            first += [copy(a, 1 + j, me, (*chip, c), src=ins[a]) for a in range(n)]
        first += [copy(a, 0, me, sibling, src=ins[a]) for a in range(n)]
        for cp in first:
            cp.start()
        passed = []
        for j, chip in enumerate(chips):
            for a in range(n):
                copy(a, 1 + j, (*chip, c), me).wait_recv()
                fwd = copy(a, 4 + j, (*chip, c), sibling)
                fwd.start()
                passed.append(fwd)
        for a in range(n):
            copy(a, 0, sibling, me).wait_recv()
            for j, chip in enumerate(chips):
                copy(a, 4 + j, (*chip, 1 - c), me).wait_recv()
        for cp in first + passed:
            cp.wait_send()
        for cp in mine:
            cp.wait()

    return _pcall(
        body, name=name, in_specs=[ANY] * n, out_specs=[ANY] * n,
        out_shape=[_sds((N_DEV, *a.shape), a.dtype) for a in arrays],
        scratch_shapes=[pltpu.SemaphoreType.DMA((n, 7)), pltpu.SemaphoreType.DMA((n, 7)), pltpu.SemaphoreType.DMA((n,))],
        compiler_params=pltpu.CompilerParams(has_side_effects=True),
    )(*arrays)


def _sibling_exchange(grads):
    n = len(grads)

    def start(ins, outs, sems):
        send_sems, recv_sems = sems
        x, y, c = _place()
        for a in range(n):
            for ch in range(N_CHIP):
                pltpu.make_async_remote_copy(src_ref=ins[a].at[2 * ch + 1 - c], dst_ref=outs[a].at[ch], send_sem=send_sems.at[a],
                                             recv_sem=recv_sems.at[a], device_id=(x, y, 1 - c), device_id_type=MESH).start()

    def finish(ins, outs, sems):
        send_sems, recv_sems = sems
        x, y, c = _place()
        for a in range(n):
            pltpu.make_async_remote_copy(src_ref=outs[a], dst_ref=outs[a], send_sem=send_sems.at[a], recv_sem=recv_sems.at[a],
                                         device_id=(x, y, 1 - c), device_id_type=MESH).wait()

    return _Comm(grads, [_sds((N_CHIP, *g.shape[1:]), g.dtype) for g in grads], {},
                 [pltpu.SemaphoreType.DMA((n,)), pltpu.SemaphoreType.DMA((n,))], start, finish)


def _chip_exchange(parts, slots, layers, stacked):
    n = len(parts)
    names = []
    for nm, _ in slots:
        if nm not in names:
            names.append(nm)
    shapes = {nm: _sds((N_CHIP, layers[nm], *parts[a].shape[1:]), parts[a].dtype) for a, (nm, _) in enumerate(slots)}
    kept = [nm for nm in names if stacked.get(nm) is not None]
    aliases = {n + k: names.index(nm) for k, nm in enumerate(kept)}

    def copies(ins, outs, sems):
        send_sems, recv_sems, local_sems = sems
        x, y, c = _place()
        mine = 2 * x + y
        local, sends, recvs = [], [], []
        for a, (nm, l) in enumerate(slots):
            buf = outs[names.index(nm)]
            local.append(pltpu.make_async_copy(ins[a].at[mine], buf.at[mine, l], local_sems.at[a]))
            for j, chip in enumerate(_other_chips(x, y)):
                theirs = buf.at[2 * chip[0] + chip[1], l]
                sends.append(pltpu.make_async_remote_copy(src_ref=ins[a].at[2 * chip[0] + chip[1]], dst_ref=buf.at[mine, l], send_sem=send_sems.at[a, j],
                                                          recv_sem=recv_sems.at[a, j], device_id=(*chip, c), device_id_type=MESH))
                recvs.append(pltpu.make_async_remote_copy(src_ref=theirs, dst_ref=theirs, send_sem=send_sems.at[a, j],
                                                          recv_sem=recv_sems.at[a, j], device_id=(*chip, c), device_id_type=MESH))
        return local, sends, recvs

    def start(ins, outs, sems):
        local, sends, _ = copies(ins, outs, sems)
        for cp in local + sends:
            cp.start()

    def finish(ins, outs, sems):
        local, sends, recvs = copies(ins, outs, sems)
        for cp in recvs:
            cp.wait_recv()
        for cp in sends:
            cp.wait_send()
        for cp in local:
            cp.wait()

    comm = _Comm([*parts, *[stacked[nm] for nm in kept]], [shapes[nm] for nm in names], aliases,
                 [pltpu.SemaphoreType.DMA((n, 3)), pltpu.SemaphoreType.DMA((n, 3)), pltpu.SemaphoreType.DMA((n,))], start, finish)
    return comm, names


def _matmul(name, a, b, extras, *, grid, a_spec, b_spec, extra_specs, out_shapes, out_specs, dims, k_axis=None, nk=1,
            acc_shape=None, epilogue=None, comm=None, n_sum=0, write=None):
    n_extra = len(extras)
    n_out = len(out_shapes)

    def body(*refs):
        a_ref, b_ref = refs[0], refs[1]
        ex = refs[2:2 + n_extra]
        outs = refs[2 + n_extra:2 + n_extra + n_out]
        prod = lax.dot_general(a_ref[...], b_ref[...], dims, preferred_element_type=F32)

        def finish(acc):
            if write is not None:
                write(outs, acc, *[e[...] for e in ex])
                return
            res = epilogue(acc, *[e[...] for e in ex]) if epilogue is not None else (acc,)
            first = None
            for d in range(len(grid)):
                if d != k_axis:
                    here = pl.program_id(d) == 0
                    first = here if first is None else jnp.logical_and(first, here)
            for idx, (o, r) in enumerate(zip(outs, res)):
                if idx < n_out - n_sum:
                    o[...] = r.astype(o.dtype)
                else:
                    @pl.when(first)
                    def _(o=o, r=r):
                        o[...] = r.astype(o.dtype)

                    @pl.when(jnp.logical_not(first))
                    def _(o=o, r=r):
                        o[...] += r.astype(o.dtype)

        if k_axis is None:
            finish(prod)
        else:
            acc_ref = refs[-1]
            k = pl.program_id(k_axis)

            @pl.when(k == 0)
            def _():
                acc_ref[...] = prod

            @pl.when(k > 0)
            def _():
                acc_ref[...] += prod

            @pl.when(k == nk - 1)
            def _():
                finish(acc_ref[...])

    scratch = [] if k_axis is None else [pltpu.VMEM(acc_shape, F32)]
    return _call(name, body, [a, b, *extras], [a_spec, b_spec, *extra_specs], list(out_shapes), list(out_specs), scratch, grid, comm)


def _rowwise(name, fn, operands, *, grid, in_specs, out_shapes, out_specs, n_acc=0, grid_spec_prefetch=None, comm=None):
    n_in = len(operands)
    n_out = len(out_shapes)
    n_pre = 0 if grid_spec_prefetch is None else 1

    def body(*refs):
        refs = refs[n_pre:]
        ins = refs[:n_in]
        outs = refs[n_in:n_in + n_out]
        res = fn(*[r[...] for r in ins])
        if not isinstance(res, (tuple, list)):
            res = (res,)
        first = pl.program_id(0) == 0
        for d in range(1, len(grid)):
            first = jnp.logical_and(first, pl.program_id(d) == 0)
        for idx, (o, r) in enumerate(zip(outs, res)):
            if idx < n_out - n_acc:
                o[...] = r.astype(o.dtype)
            else:
                @pl.when(first)
                def _(o=o, r=r):
                    o[...] = r.astype(o.dtype)

                @pl.when(jnp.logical_not(first))
                def _(o=o, r=r):
                    o[...] += r.astype(o.dtype)

    if comm is not None:
        return _call(name, body, list(operands), list(in_specs), list(out_shapes), list(out_specs), [], grid, comm)
    if grid_spec_prefetch is None:
        return _pcall(body, name=name, grid=grid, in_specs=in_specs, out_specs=out_specs, out_shape=out_shapes,
                      compiler_params=_params(len(grid)))(*operands)
    gs = pltpu.PrefetchScalarGridSpec(num_scalar_prefetch=1, grid=grid, in_specs=in_specs, out_specs=out_specs)
    return _pcall(body, name=name, grid_spec=gs, out_shape=out_shapes,
                  compiler_params=_params(len(grid)))(grid_spec_prefetch, *operands)


def _row_spec(tm, w):
    return pl.BlockSpec((tm, w), lambda i: (i, 0))


def _const_spec(shape):
    nd = len(shape)
    return pl.BlockSpec(tuple(shape), lambda *_: (0,) * nd)


def _rms_fwd(x, g):
    r = lax.rsqrt(jnp.mean(x * x, axis=-1, keepdims=True) + NORM_EPS)
    return x * r * g


def _rms_bwd(x, g, dy):
    r = lax.rsqrt(jnp.mean(x * x, axis=-1, keepdims=True) + NORM_EPS)
    xh = x * r
    u = dy * g
    dx = r * (u - xh * jnp.mean(u * xh, axis=-1, keepdims=True))
    dg = jnp.sum(dy * xh, axis=0, keepdims=True)
    return dx, dg


def _gelu_and_grad(z):
    cdf = 0.5 * (1.0 + lax.erf(z * (2.0 ** -0.5)))
    return cdf + z * jnp.exp(-0.5 * z * z) * ((2.0 * math.pi) ** -0.5), z * cdf


def _rope_fwd(x, cc, sa, sb):
    return x * cc + pltpu.roll(x, 96, 1) * sa + pltpu.roll(x, 32, 1) * sb


def _rope_bwd(d, cc, sa, sb):
    return d * cc + pltpu.roll(d * sa, 32, 1) + pltpu.roll(d * sb, 96, 1)


def _adam(w, g, m, v):
    m = ADAM_B1 * m + (1.0 - ADAM_B1) * g
    v = ADAM_B2 * v + (1.0 - ADAM_B2) * (g * g)
    m_hat = m / (1.0 - ADAM_B1 ** ADAM_STEP)
    v_hat = v / (1.0 - ADAM_B2 ** ADAM_STEP)
    delta = -ADAM_LR * (m_hat / (jnp.sqrt(v_hat) + ADAM_EPS) + ADAM_WD * w)
    return delta, m, v


def _flash_fwd(q, k, vt, tq, comm=None):
    h, t = vt.shape[0], q.shape[0]
    nq = t // tq

    chunk_blocks = [c for c in (4, 2, 1) if c < nq]

    def body(q_ref, k_ref, vt_ref, o_ref, lse_ref, m_ref, l_ref, acc_ref):
        qi = pl.program_id(1)
        m_ref[...] = jnp.full((1, tq), NEG, F32)
        l_ref[...] = jnp.zeros((1, tq), F32)
        acc_ref[...] = jnp.zeros((VDIM, tq), F32)

        def update(kb0, nblk, masked):
            kb = k_ref[pl.ds(pl.multiple_of(kb0 * tq, tq), nblk * tq), :]
            st = lax.dot_general(kb, q_ref[...], NT, preferred_element_type=F32)
            if masked:
                key = lax.broadcasted_iota(jnp.int32, (tq, tq), 0)
                qry = lax.broadcasted_iota(jnp.int32, (tq, tq), 1)
                st = jnp.where(key <= qry, st, NEG)
            m_old = m_ref[...]
            m_new = jnp.maximum(m_old, jnp.max(st, axis=0, keepdims=True))
            alpha = jnp.exp2((m_old - m_new) * EXP2_SCALE)
            pt = jnp.exp2((st - m_new) * EXP2_SCALE)
            l_ref[...] = alpha * l_ref[...] + jnp.sum(pt, axis=0, keepdims=True)
            ptb = pt.astype(BF16)
            pv = lax.dot_general(vt_ref[kb0], ptb[:tq], NN, preferred_element_type=F32)
            for j in range(1, nblk):
                pv += lax.dot_general(vt_ref[kb0 + j], ptb[j * tq:(j + 1) * tq], NN, preferred_element_type=F32)
            acc_ref[...] = alpha * acc_ref[...] + pv
            m_ref[...] = m_new

        start = jnp.int32(0)
        for c in chunk_blocks:
            take = (qi & c) != 0

            @pl.when(take)
            def _(start=start, c=c):
                update(start, c, False)

            start = start + jnp.where(take, c, 0)
        update(qi, 1, True)
        l = l_ref[...]
        o_ref[...] = (acc_ref[...] / l).T.astype(o_ref.dtype)
        lse_ref[...] = m_ref[...] * EXP2_SCALE + jnp.log2(l)

    return _call(
        "flash_fwd", body, [q, k, vt],
        [pl.BlockSpec((tq, QPAD), lambda hh, i: (i, hh)),
         pl.BlockSpec((t, QPAD), lambda hh, i: (0, hh)),
         pl.BlockSpec((None, nq, VDIM, tq), lambda hh, i: (hh, 0, 0, 0))],
        [_sds((t, h * VDIM), BF16), _sds((h, nq, 1, tq), F32)],
        [pl.BlockSpec((tq, VDIM), lambda hh, i: (i, hh)),
         pl.BlockSpec((None, None, 1, tq), lambda hh, i: (hh, i, 0, 0))],
        [pltpu.VMEM((1, tq), F32), pltpu.VMEM((1, tq), F32), pltpu.VMEM((VDIM, tq), F32)], (h, nq), comm)


def _flash_bwd(q, k, v, o, do, lse, tabs, tq, comm=None):
    t = q.shape[0]
    h = q.shape[1] // QPAD
    nq = t // tq

    def body(q_ref, k_ref, v_ref, o_ref, do_ref, lse_ref, cc_ref, sa_ref, sb_ref, dq_ref, dk_ref, dv_ref, delta_ref, dqt_ref):
        kj = pl.program_id(1)

        @pl.when(kj == 0)
        def _():
            dqt_ref[...] = jnp.zeros_like(dqt_ref)
            ones = jnp.ones((8, VDIM), BF16)
            for qi in range(nq):
                rows = pl.ds(qi * tq, tq)
                prod = do_ref[rows, :].astype(F32) * o_ref[rows, :].astype(F32)
                hi = prod.astype(BF16)
                lo = (prod - hi.astype(F32)).astype(BF16)
                delta_ref[qi] = (lax.dot_general(ones, hi, NT, preferred_element_type=F32)
                                 + lax.dot_general(ones, lo, NT, preferred_element_type=F32))

        kb = k_ref[...]
        vb = v_ref[...]
        kbt = kb.astype(F32).T.astype(BF16)
        dk_ref[...] = jnp.zeros_like(dk_ref)
        dv_ref[...] = jnp.zeros_like(dv_ref)

        def step(q0, nblk, masked):
            rows = pl.ds(pl.multiple_of(q0 * tq, tq), nblk * tq)
            qb = q_ref[rows, :]
            dob = do_ref[rows, :]
            lse = jnp.concatenate([lse_ref[q0 + j] for j in range(nblk)], axis=1)
            delta = jnp.concatenate([delta_ref[q0 + j, pl.ds(0, 1), :] for j in range(nblk)], axis=1)
            st = lax.dot_general(kb, qb, NT, preferred_element_type=F32)
            pt = jnp.exp2(st * EXP2_SCALE - lse)
            if masked:
                key = lax.broadcasted_iota(jnp.int32, (tq, tq), 0)
                qry = lax.broadcasted_iota(jnp.int32, (tq, tq), 1)
                pt = jnp.where(key <= qry, pt, 0.0)
            dv_ref[...] += lax.dot_general(pt.astype(BF16), dob, NN, preferred_element_type=F32)
            dpt = lax.dot_general(vb, dob, NT, preferred_element_type=F32)
            dst = (pt * (dpt - delta) * ATTN_SCALE).astype(BF16)
            dk_ref[...] += lax.dot_general(dst, qb, NN, preferred_element_type=F32)
            dqt = lax.dot_general(kbt, dst, NN, preferred_element_type=F32)
            for j in range(nblk):
                dqt_ref[q0 + j] += dqt[:, j * tq:(j + 1) * tq]

        later = nq - 1 - kj
        step(kj, 1, True)

        @pl.when((later & 1) != 0)
        def _():
            step(kj + 1, 1, False)

        first_pair = kj + 1 + (later & 1)

        def loop_body(p, carry):
            step(first_pair + 2 * p, 2, False)
            return carry

        lax.fori_loop(0, later // 2, loop_body, 0)

        @pl.when(kj == nq - 1)
        def _():
            for qi in range(nq):
                rows = pl.ds(qi * tq, tq)
                d = dqt_ref[qi].T
                roped = _rope_bwd(d[:, NOPE:], cc_ref[rows, :], sa_ref[rows, :], sb_ref[rows, :])
                dq_ref[rows, :] = jnp.concatenate([d[:, :NOPE], roped], axis=1).astype(BF16)

    head_q = pl.BlockSpec((t, QPAD), lambda hh, j: (0, hh))
    head_v = pl.BlockSpec((t, VDIM), lambda hh, j: (0, hh))
    table = pl.BlockSpec((t, 128), lambda hh, j: (0, 0))
    return _call(
        "flash_bwd", body, [q, k, v, o, do, lse, *tabs],
        [head_q, pl.BlockSpec((tq, QPAD), lambda hh, j: (j, hh)), pl.BlockSpec((tq, VDIM), lambda hh, j: (j, hh)), head_v, head_v,
         pl.BlockSpec((None, nq, 1, tq), lambda hh, j: (hh, 0, 0, 0)), table, table, table],
        [_sds((t, h * QPAD), BF16), _sds((t, h * QPAD), F32), _sds((t, h * VDIM), F32)],
        [head_q, pl.BlockSpec((tq, QPAD), lambda hh, j: (j, hh)), pl.BlockSpec((tq, VDIM), lambda hh, j: (j, hh))],
        [pltpu.VMEM((nq, 8, tq), F32), pltpu.VMEM((nq, QPAD, tq), F32)], (h, nq), comm)


def _tril_bf16(w):
    row = lax.broadcasted_iota(jnp.int32, w.shape, 0)
    col = lax.broadcasted_iota(jnp.int32, w.shape, 1)
    return jnp.where(col <= row, w, 0.0).astype(BF16)


def _layer_norm_parts(v0):
    mu = jnp.mean(v0, axis=-1, keepdims=True)
    vc = v0 - mu
    rstd = lax.rsqrt(jnp.mean(vc * vc, axis=-1, keepdims=True) + LN_EPS)
    return vc * rstd, rstd


def _sgu_mid_fwd(ge, ln_g, ln_b, w_sp, b_sp, chunks_per_step):
    t, e2 = ge.shape
    e = e2 // 2
    gd = e // SGU_GROUPS
    rows = SGU_CHUNK * chunks_per_step

    def body(u_ref, v_ref, g_ref, b_ref, w_ref, bs_ref, gate_ref):
        for ck in range(chunks_per_step):
            r = pl.ds(ck * SGU_CHUNK, SGU_CHUNK)
            xh, _ = _layer_norm_parts(v_ref[r, :].astype(F32))
            v1 = (xh * g_ref[...] + b_ref[...]).astype(BF16)
            for g in range(SGU_GROUPS):
                cols = pl.ds(g * gd, gd)
                mixed = lax.dot_general(_tril_bf16(w_ref[g]), v1[:, g * gd:(g + 1) * gd], NN, preferred_element_type=F32) + bs_ref[g]
                gate_ref[r, cols] = (u_ref[r, cols].astype(F32) * mixed).astype(BF16)

    return _pcall(
        body, name="sgu_mid_fwd", grid=(t // rows,),
        in_specs=[pl.BlockSpec((rows, e), lambda i: (i, 0)), pl.BlockSpec((rows, e), lambda i: (i, 1)),
                  _const_spec((1, e)), _const_spec((1, e)), _const_spec(w_sp.shape), _const_spec(b_sp.shape)],
        out_specs=pl.BlockSpec((rows, e), lambda i: (i, 0)),
        out_shape=_sds((t, e), BF16), compiler_params=_params(1),
    )(ge, ge, ln_g, ln_b, w_sp, b_sp)


def _sgu_mid_bwd(ge, gp, dgate, ln_g, ln_b, w_sp, b_sp, chunks_per_step):
    t, e2 = ge.shape
    e = e2 // 2
    gd = e // SGU_GROUPS
    rows = SGU_CHUNK * chunks_per_step

    def body(u_ref, v_ref, zu_ref, zv_ref, dg_ref, g_ref, b_ref, w_ref, bs_ref, dz_ref, dw_ref, dbs_ref, dlg_ref, dlb_ref):
        @pl.when(pl.program_id(0) == 0)
        def _():
            dw_ref[...] = jnp.zeros_like(dw_ref)
            dbs_ref[...] = jnp.zeros_like(dbs_ref)
            dlg_ref[...] = jnp.zeros_like(dlg_ref)
            dlb_ref[...] = jnp.zeros_like(dlb_ref)

        for ck in range(chunks_per_step):
            r = pl.ds(ck * SGU_CHUNK, SGU_CHUNK)
            xh, rstd = _layer_norm_parts(v_ref[r, :].astype(F32))
            v1 = (xh * g_ref[...] + b_ref[...]).astype(BF16)
            dv1_parts = []
            for g in range(SGU_GROUPS):
                cols = pl.ds(g * gd, gd)
                wc = _tril_bf16(w_ref[g])
                v1g = v1[:, g * gd:(g + 1) * gd]
                mixed = lax.dot_general(wc, v1g, NN, preferred_element_type=F32) + bs_ref[g]
                dgate = dg_ref[r, cols].astype(F32)
                dmixed = dgate * u_ref[r, cols].astype(F32)
                du = dgate * mixed
                dz_ref[r, cols] = (du * zu_ref[r, cols].astype(F32)).astype(BF16)
                dbs_ref[g] += jnp.sum(dmixed, axis=1, keepdims=True)
                dmb = dmixed.astype(BF16)
                dwg = lax.dot_general(dmb, v1g, NT, preferred_element_type=F32)
                row = lax.broadcasted_iota(jnp.int32, dwg.shape, 0)
                col = lax.broadcasted_iota(jnp.int32, dwg.shape, 1)
                dw_ref[g] += jnp.where(col <= row, dwg, 0.0)
                dv1_parts.append(lax.dot_general(wc, dmb, TN, preferred_element_type=F32))
            dv1 = jnp.concatenate(dv1_parts, axis=1)
            dlg_ref[...] += jnp.sum(dv1 * xh, axis=0, keepdims=True)
            dlb_ref[...] += jnp.sum(dv1, axis=0, keepdims=True)
            dxh = dv1 * g_ref[...]
            dv0 = rstd * (dxh - jnp.mean(dxh, axis=-1, keepdims=True) - xh * jnp.mean(dxh * xh, axis=-1, keepdims=True))
            dz_ref[r, pl.ds(e, e)] = (dv0 * zv_ref[r, :].astype(F32)).astype(BF16)

    half0 = pl.BlockSpec((rows, e), lambda i: (i, 0))
    half1 = pl.BlockSpec((rows, e), lambda i: (i, 1))
    return _pcall(
        body, name="sgu_mid_bwd", grid=(t // rows,),
        in_specs=[half0, half1, half0, half1, half0, _const_spec((1, e)), _const_spec((1, e)), _const_spec(w_sp.shape), _const_spec(b_sp.shape)],
        out_specs=[pl.BlockSpec((rows, e2), lambda i: (i, 0)), _const_spec(w_sp.shape), _const_spec(b_sp.shape), _const_spec((1, e)), _const_spec((1, e))],
        out_shape=[_sds((t, e2), BF16), _sds(w_sp.shape, F32), _sds(b_sp.shape, F32), _sds((1, e), F32), _sds((1, e), F32)],
        compiler_params=_params(1),
    )(ge, ge, gp, gp, dgate, ln_g, ln_b, w_sp, b_sp)


def kernel(x, positions, norm_mix, norm_ffn, final_norm, mla_w_dkv, mla_q_norm, mla_kv_norm, mla_w_uq, mla_w_ukv, mla_w_o, sgu_w_in, sgu_ln_g, sgu_ln_b, sgu_w_spatial, sgu_b_spatial, sgu_w_out, ffn_w_up, ffn_w_down, loss_target, m_norm_mix, m_norm_ffn, m_final_norm, m_mla_w_dkv, m_mla_q_norm, m_mla_kv_norm, m_mla_w_uq, m_mla_w_ukv, m_mla_w_o, m_sgu_w_in, m_sgu_ln_g, m_sgu_ln_b, m_sgu_w_spatial, m_sgu_b_spatial, m_sgu_w_out, m_ffn_w_up, m_ffn_w_down, v_norm_mix, v_norm_ffn, v_final_norm, v_mla_w_dkv, v_mla_q_norm, v_mla_kv_norm, v_mla_w_uq, v_mla_w_ukv, v_mla_w_o, v_sgu_w_in, v_sgu_ln_g, v_sgu_ln_b, v_sgu_w_spatial, v_sgu_b_spatial, v_sgu_w_out, v_ffn_w_up, v_ffn_w_down):
    _, T, D = x.shape
    depth = norm_mix.shape[0]
    n_mla, n_sgu = mla_w_dkv.shape[0], sgu_w_in.shape[0]
    assert depth % 2 == 0
    FF = ffn_w_up.shape[2] * N_DEV
    E = sgu_w_out.shape[1] * N_DEV
    ffc, ec, e2c = FF // N_DEV, E // N_DEV, 2 * E // N_DEV
    dc = D // N_DEV
    OW = HEADS * VDIM
    HW = HEADS * QPAD
    owc = OW // N_DEV
    tm = _tile(T, 1024)
    tb = _tile(T, 4096)
    tk = _tile(T, 512)
    tq = _tile(T, 512)
    ts = _tile(T, 256)
    nt = T // tm
    x2 = x.reshape(T, D)
    tgt = loss_target.reshape(T, D)
    cidx = lax.axis_index("c").astype(jnp.int32).reshape(1)

    ln_local = jnp.concatenate([sgu_ln_g, sgu_ln_b, jnp.zeros((8 - 2 * n_sgu, ec), F32)], axis=0)
    g_dkv, g_uq, g_ukv, g_o, g_ln = _all_gather(
        "gather_small_weights", [w.astype(BF16) for w in (mla_w_dkv, mla_w_uq, mla_w_ukv, mla_w_o)] + [ln_local])
    w_dkv = jnp.pad(g_dkv.transpose(1, 0, 2, 3).reshape(n_mla, D, LAT), ((0, 0), (0, 0), (0, LAT_PAD - LAT)))
    w_uq = jnp.pad(g_uq, ((0, 0), (0, 0), (0, 0), (0, QPAD - NOPE - ROPE))).transpose(1, 2, 0, 3).reshape(n_mla, Q_RANK, HEADS * QPAD)
    w_ukv = g_ukv.transpose(1, 2, 0, 3).reshape(n_mla, KV_RANK, HEADS * (NOPE + VDIM))
    w_o = g_o.transpose(1, 0, 2, 3).reshape(n_mla, OW, D)
    ln_g_full = [g_ln[:, l, :].reshape(1, E) for l in range(n_sgu)]
    ln_b_full = [g_ln[:, n_sgu + l, :].reshape(1, E) for l in range(n_sgu)]
    b_sp = sgu_b_spatial.reshape(n_sgu, SGU_GROUPS, SGU_CHUNK, 1)
    up_sh = [ffn_w_up[i].astype(BF16) for i in range(depth)]
    down_sh = [ffn_w_down[i].astype(BF16) for i in range(depth)]
    in_sh = [sgu_w_in[l].astype(BF16) for l in range(n_sgu)]
    out_sh = [sgu_w_out[l].astype(BF16) for l in range(n_sgu)]
    g_up, g_down, g_in, g_out = [None] * depth, [None] * depth, [None] * n_sgu, [None] * n_sgu

    inv_freq = ROPE_THETA ** (-jnp.arange(0, ROPE, 2, dtype=F32) / ROPE)
    zeros32 = jnp.zeros((ROPE // 2,), F32)
    inv128 = jnp.concatenate([inv_freq, inv_freq, zeros32, zeros32]).reshape(1, 128)
    sel_a = jnp.concatenate([-jnp.ones((32,), F32), zeros32, zeros32, zeros32]).reshape(1, 128)
    sel_b = jnp.concatenate([zeros32, jnp.ones((32,), F32), zeros32, zeros32]).reshape(1, 128)
    sel_c = jnp.concatenate([jnp.ones((64,), F32), zeros32, zeros32]).reshape(1, 128)

    def rope_tables(pos, inv, sa, sb, sc):
        ang = pos.astype(F32) * inv
        cs, sn = jnp.cos(ang), jnp.sin(ang)
        return cs * sc, sn * sa, sn * sb

    t_cc, t_sa, t_sb = _rowwise(
        "rope_tables", rope_tables, [positions.reshape(T, 1), inv128, sel_a, sel_b, sel_c], grid=(nt,),
        in_specs=[_row_spec(tm, 1)] + [_const_spec((1, 128))] * 4,
        out_shapes=[_sds((T, 128), F32)] * 3, out_specs=[_row_spec(tm, 128)] * 3)
    tab_specs = [_row_spec(tm, 128)] * 3

    def rmsnorm(xv, g):
        return _rowwise("rmsnorm", lambda a, gg: _rms_fwd(a, gg), [xv, g.reshape(1, D)], grid=(nt,),
                        in_specs=[_row_spec(tm, D), _const_spec((1, D))], out_shapes=_sds((T, D), BF16), out_specs=_row_spec(tm, D))

    def proj_cols(name, h, gw, nc, epilogue, n_out, comm=None):
        return _matmul(name, h, gw, [], grid=(N_DEV, T // tb),
                       a_spec=pl.BlockSpec((tb, D), lambda j, i: (i, 0)),
                       b_spec=pl.BlockSpec((None, D, nc), lambda j, i: (j, 0, 0)), extra_specs=[],
                       out_shapes=[_sds((T, nc * N_DEV), BF16)] * n_out, out_specs=[pl.BlockSpec((tb, nc), lambda j, i: (i, j))] * n_out,
                       dims=NN, epilogue=epilogue, comm=comm)

    def residual_norm(acc, xr, g):
        xn = acc + xr
        return xn, _rms_fwd(xn, g)

    def proj_rows_residual(name, a, gw, xres, g_next):
        kk_ = a.shape[1]
        return _matmul(name, a, gw.reshape(kk_, D), [xres, g_next.reshape(1, D)], grid=(T // tk,),
                       a_spec=_row_spec(tk, kk_), b_spec=_const_spec((kk_, D)), extra_specs=[_row_spec(tk, D), _const_spec((1, D))],
                       out_shapes=[_sds((T, D), F32), _sds((T, D), BF16)], out_specs=[_row_spec(tk, D)] * 2,
                       dims=NN, epilogue=residual_norm)

    def back_rows(name, dy, gw, kc, extras, epilogue, comm=None):
        return _matmul(name, dy, gw, extras, grid=(N_DEV, T // tb),
                       a_spec=pl.BlockSpec((tb, D), lambda j, i: (i, 0)),
                       b_spec=pl.BlockSpec((None, kc, D), lambda j, i: (j, 0, 0)),
                       extra_specs=[pl.BlockSpec((tb, kc), lambda j, i: (i, j))] * len(extras),
                       out_shapes=[_sds((T, kc * N_DEV), BF16)], out_specs=[pl.BlockSpec((tb, kc), lambda j, i: (i, j))],
                       dims=NT, epilogue=epilogue, comm=comm)

    def norm_bwd_epilogue(dh, xv, g, dxi):
        dxn, dg = _rms_bwd(xv, g, dh)
        return dxi + dxn, dxi + dxn, dg

    def transposed(gw):
        return gw.transpose(0, 2, 1).reshape(gw.shape[0] * gw.shape[2], D)

    def back_cols(name, da, gwt, xv, g, dx_in, comm=None):
        n = da.shape[1]
        row = _row_spec(tk, D)
        return _matmul(name, da, gwt, [xv, g.reshape(1, D), dx_in], grid=(T // tk,),
                       a_spec=_row_spec(tk, n), b_spec=_const_spec((n, D)), extra_specs=[row, _const_spec((1, D)), row],
                       out_shapes=[_sds((T, D), F32), _sds((T, D), BF16), _sds((1, D), F32)], out_specs=[row, row, _const_spec((1, D))],
                       dims=NN, epilogue=norm_bwd_epilogue, n_sum=1, comm=comm)

    def token_sum(tt):
        return dict(k_axis=1, nk=T // tt) if T // tt > 1 else dict(k_axis=None)

    def wgrad_cols(name, h, da, nc):
        return _matmul(name, h, da, [], grid=(N_DEV, T // tb),
                       a_spec=pl.BlockSpec((tb, D), lambda j, t: (t, 0)), b_spec=pl.BlockSpec((tb, nc), lambda j, t: (t, j)),
                       extra_specs=[], out_shapes=[_sds((N_DEV, D, nc), BF16)],
                       out_specs=[pl.BlockSpec((None, D, nc), lambda j, t: (j, 0, 0))],
                       dims=TN, acc_shape=(D, nc), **token_sum(tb))[0]

    def wgrad_rows(name, a, dy, kc, ncols, tt):
        return _matmul(name, a, dy, [], grid=(a.shape[1] // kc, T // tt),
                       a_spec=pl.BlockSpec((tt, kc), lambda j, t: (t, j)), b_spec=pl.BlockSpec((tt, ncols), lambda j, t: (t, 0)),
                       extra_specs=[], out_shapes=[_sds((a.shape[1], ncols), BF16)],
                       out_specs=[pl.BlockSpec((kc, ncols), lambda j, t: (j, 0))],
                       dims=TN, acc_shape=(kc, ncols), **token_sum(tt))[0]

    saved = []
    xs = x2
    for i in range(depth):
        l = i // 2
        if i == 0:
            h = rmsnorm(xs, norm_mix[0])
        if i % 2 == 0:
            lat = _matmul("mla_down", h, w_dkv, [], grid=(nt,), a_spec=_row_spec(tm, D),
                          b_spec=pl.BlockSpec((None, D, LAT_PAD), lambda i_: (l, 0, 0)), extra_specs=[],
                          out_shapes=[_sds((T, LAT_PAD), F32)], out_specs=[_row_spec(tm, LAT_PAD)], dims=NN)[0]

            def latent_post(la, qn, kvn, cc, sa, sb):
                cq = _rms_fwd(la[:, :Q_RANK], qn)
                ckv = _rms_fwd(la[:, Q_RANK:Q_RANK + KV_RANK], kvn)
                kr = _rope_fwd(la[:, Q_RANK + KV_RANK:], cc, sa, sb)
                return cq, ckv, kr

            cq, ckv, kr = _rowwise(
                "mla_latent", latent_post, [lat, mla_q_norm[l].reshape(1, Q_RANK), mla_kv_norm[l].reshape(1, KV_RANK), t_cc, t_sa, t_sb],
                grid=(nt,), in_specs=[_row_spec(tm, LAT_PAD), _const_spec((1, Q_RANK)), _const_spec((1, KV_RANK))] + tab_specs,
                out_shapes=[_sds((T, Q_RANK), BF16), _sds((T, KV_RANK), BF16), _sds((T, 128), BF16)],
                out_specs=[_row_spec(tm, Q_RANK), _row_spec(tm, KV_RANK), _row_spec(tm, 128)])

            def q_epilogue(acc, cc, sa, sb):
                parts = []
                for b in range(HEADS):
                    parts += [acc[:, b * QPAD:b * QPAD + NOPE], _rope_fwd(acc[:, b * QPAD + NOPE:(b + 1) * QPAD], cc, sa, sb)]
                return (jnp.concatenate(parts, axis=1),)

            q = _matmul("mla_q", cq, w_uq, [t_cc, t_sa, t_sb], grid=(nt,), a_spec=_row_spec(tm, Q_RANK),
                        b_spec=pl.BlockSpec((None, Q_RANK, HW), lambda i_: (l, 0, 0)), extra_specs=tab_specs,
                        out_shapes=[_sds((T, HW), BF16)], out_specs=[_row_spec(tm, HW)], dims=NN, epilogue=q_epilogue)[0]

            def kv_write(outs, acc, krb):
                k_ref, v_ref, vt_ref = outs
                for b in range(HEADS):
                    vb = acc[:, b * QPAD + NOPE:(b + 1) * QPAD]
                    k_ref[:, b * QPAD:b * QPAD + NOPE] = acc[:, b * QPAD:b * QPAD + NOPE].astype(BF16)
                    k_ref[:, b * QPAD + NOPE:(b + 1) * QPAD] = krb
                    v_ref[:, b * VDIM:(b + 1) * VDIM] = vb.astype(BF16)
                    vbt = vb.T.astype(BF16)
                    for u in range(tm // tq):
                        vt_ref[b, u] = vbt[:, u * tq:(u + 1) * tq]

            kk, vv, vt = _matmul("mla_kv", ckv, w_ukv, [kr], grid=(nt,), a_spec=_row_spec(tm, KV_RANK),
                                 b_spec=pl.BlockSpec((None, KV_RANK, HW), lambda i_: (l, 0, 0)), extra_specs=[_row_spec(tm, 128)],
                                 out_shapes=[_sds((T, HW), BF16), _sds((T, OW), BF16), _sds((HEADS, T // tq, VDIM, tq), BF16)],
                                 out_specs=[_row_spec(tm, HW), _row_spec(tm, OW), pl.BlockSpec((HEADS, tm // tq, VDIM, tq), lambda i_: (0, i_, 0, 0))],
                                 dims=NN, write=kv_write)
            group = [up_sh[i], down_sh[i], in_sh[l], out_sh[l], up_sh[i + 1], down_sh[i + 1]]
            o, lse, *bufs = _flash_fwd(q, kk, vt, tq, comm=_gather_level1(group))
            xm, h2, g_up[i], g_down[i] = _matmul(
                "mla_out", o, w_o, [xs, norm_ffn[i].reshape(1, D)], grid=(nt,), a_spec=_row_spec(tm, OW),
                b_spec=pl.BlockSpec((None, OW, D), lambda i_: (l, 0, 0)), extra_specs=[_row_spec(tm, D), _const_spec((1, D))],
                out_shapes=[_sds((T, D), F32), _sds((T, D), BF16)], out_specs=[_row_spec(tm, D)] * 2, dims=NN,
                epilogue=residual_norm, comm=_gather_level2(bufs[:2]))
            half_gathered = bufs[2:]
            mix_saved = (h, lat, cq, ckv, q, kk, vv, o, lse)
        else:
            gp, ge = proj_cols("sgu_in", h, g_in[l], e2c, _gelu_and_grad, 2)
            gate = _sgu_mid_fwd(ge, ln_g_full[l], ln_b_full[l], sgu_w_spatial[l], b_sp[l], 4)
            xm, h2 = proj_rows_residual("sgu_out", gate, g_out[l], xs, norm_ffn[i])
            mix_saved = (h, gp, ge, gate)
        r, s, *rest = proj_cols("ffn_up", h2, g_up[i], ffc, lambda acc: (jnp.maximum(acc, 0.0), jnp.square(jnp.maximum(acc, 0.0))), 2,
                                comm=_gather_level2(half_gathered) if i % 2 == 0 else None)
        if i % 2 == 0:
            g_in[l], g_out[l], g_up[i + 1], g_down[i + 1] = rest
        xo, h_next = proj_rows_residual("ffn_down", s, g_down[i], xm, norm_mix[i + 1] if i + 1 < depth else final_norm)
        saved.append((xs, xm, mix_saved, h2, r, s))
        xs, h = xo, h_next

    def loss_head(xv, tg, g):
        y = _rms_fwd(xv, g)
        err = y - tg
        part = 0.5 * jnp.sum(jnp.sum(err * err, axis=-1, keepdims=True), axis=0, keepdims=True) / D
        dx, dg = _rms_bwd(xv, g, err / D)
        return dx, dx, jnp.broadcast_to(part, (1, 128)), dg

    dx, dyb, loss_part, d_final = _rowwise(
        "loss_head", loss_head, [xs, tgt, final_norm.reshape(1, D)], grid=(nt,),
        in_specs=[_row_spec(tm, D), _row_spec(tm, D), _const_spec((1, D))],
        out_shapes=[_sds((T, D), F32), _sds((T, D), BF16), _sds((1, 128), F32), _sds((1, D), F32)],
        out_specs=[_row_spec(tm, D), _row_spec(tm, D), _const_spec((1, 128)), _const_spec((1, D))], n_acc=2)
    loss = lax.psum(loss_part[0, 0], ("x", "y", "c"))

    d_norm_mix, d_norm_ffn = [None] * depth, [None] * depth
    d_qn, d_kvn = [None] * n_mla, [None] * n_mla
    d_wsp, d_bsp, d_lng, d_lnb = [None] * n_sgu, [None] * n_sgu, [None] * n_sgu, [None] * n_sgu
    layers = {"dkv": n_mla, "uq": n_mla, "ukv": n_mla, "o": n_mla, "in": n_sgu, "out": n_sgu, "up": depth, "down": depth}
    stacked = {nm: None for nm in layers}
    pending = []
    summed = []

    def add_pair(g, rcv):
        _, rws, cls = g.shape
        g4 = g.reshape(N_CHIP, 2, rws, cls)
        rt = _tile(rws, 1024)
        return _rowwise("grad_pair_sum", lambda a, b_: a.astype(F32) + b_.astype(F32), [g4, rcv], grid=(N_CHIP, rws // rt),
                        in_specs=[pl.BlockSpec((None, None, rt, cls), lambda ch, i_, cr: (ch, cr[0], i_, 0)),
                                  pl.BlockSpec((None, rt, cls), lambda ch, i_, cr: (ch, i_, 0))],
                        out_shapes=_sds(rcv.shape, BF16), out_specs=pl.BlockSpec((None, rt, cls), lambda ch, i_, cr: (ch, i_, 0)),
                        grid_spec_prefetch=cidx)

    def sibling_comm():
        return _sibling_exchange([g for _, _, g in pending]) if pending else None

    def absorb(from_sibling):
        for (nm, l_, g), rcv in zip(pending, from_sibling):
            summed.append((nm, l_, add_pair(g, rcv)))
        pending.clear()

    def chip_comm():
        if pending:
            absorb(_comm_call("grad_sibling_exchange", sibling_comm()))
        comm, names = _chip_exchange([p for _, _, p in summed], [(nm, l_) for nm, l_, _ in summed], layers, stacked)
        summed.clear()
        return comm, names

    def rows128(a, rows):
        flat = a.reshape(-1, 128)
        return jnp.pad(flat, ((0, rows - flat.shape[0]), (0, 0)))

    def pad_to(n, mult):
        return -(-n // mult) * mult

    def packed(arrs, sizes):
        return jnp.concatenate([rows128(a, sz) for a, sz in zip(arrs, sizes)], axis=0)

    n_wsp, n_bsp, n_ln = sgu_w_spatial.size // 128, pad_to(sgu_b_spatial.size // 128, 8), pad_to(n_sgu * E // 128, 8)
    early_sizes = [n_wsp, pad_to(n_wsp + n_bsp, SMALL_ROWS) - n_wsp, n_ln, n_ln]
    early_rep = early_sizes[0] + early_sizes[1]
    gathered_early = None

    for i in reversed(range(depth)):
        l = i // 2
        xs_i, xm, mix_saved, h2, r, s = saved[i]
        da, *rcv = back_rows("ffn_down_bwd", dyb, g_down[i], ffc, [r], lambda acc, rr: (acc * (2.0 * rr.astype(F32)),), comm=sibling_comm())
        absorb(rcv)
        pending.append(("down", i, wgrad_rows("ffn_down_wgrad", s, dyb, ffc, D, tb).reshape(N_DEV, ffc, D)))
        pending.append(("up", i, wgrad_cols("ffn_up_wgrad", h2, da, ffc)))
        dx, dyb, d_norm_ffn[i], *rcv = back_cols("ffn_up_bwd", da, transposed(g_up[i]), xm, norm_ffn[i], dx, comm=sibling_comm())
        absorb(rcv)
        if i % 2 == 0:
            h, lat, cq, ckv, q, kk, vv, o, lse = mix_saved
            do = _matmul("mla_out_bwd", dyb, w_o, [], grid=(nt,), a_spec=_row_spec(tm, D),
                         b_spec=pl.BlockSpec((None, OW, D), lambda i_: (l, 0, 0)), extra_specs=[],
                         out_shapes=[_sds((T, OW), BF16)], out_specs=[_row_spec(tm, OW)], dims=NT)[0]
            g_o_l = wgrad_rows("mla_out_wgrad", o, dyb, OW, D, tm).reshape(N_DEV, owc, D)
            comm, names = chip_comm()
            if i == 0:
                early = packed([jnp.stack(d_wsp, 0), jnp.stack(d_bsp, 0), jnp.concatenate(d_lng, 0), jnp.concatenate(d_lnb, 0)], early_sizes)
                comm = _merge_comm(comm, _gather_level1([early]))
            dq_pre, dk, dv, *bufs = _flash_bwd(q, kk, vv, o, do, lse, (t_cc, t_sa, t_sb), tq, comm=comm)
            stacked.update(dict(zip(names, bufs)))
            pending.append(("o", l, g_o_l))

            def kv_pre(dkb, dvb, cc, sa, sb):
                parts, dkr = [], None
                for b in range(HEADS):
                    parts += [dkb[:, b * QPAD:b * QPAD + NOPE], dvb[:, b * VDIM:(b + 1) * VDIM]]
                    piece = dkb[:, b * QPAD + NOPE:(b + 1) * QPAD]
                    dkr = piece if dkr is None else dkr + piece
                return jnp.concatenate(parts, axis=1), _rope_bwd(dkr, cc, sa, sb)

            dkv, dkr, *rest = _rowwise("mla_dkv_rope", kv_pre, [dk, dv, t_cc, t_sa, t_sb], grid=(T // ts,),
                                       in_specs=[_row_spec(ts, HW), _row_spec(ts, OW)] + [_row_spec(ts, 128)] * 3,
                                       out_shapes=[_sds((T, HW), BF16), _sds((T, 128), F32)], out_specs=[_row_spec(ts, HW), _row_spec(ts, 128)],
                                       comm=_gather_level2(bufs[len(names):]) if i == 0 else None)
            if i == 0:
                (gathered_early,) = rest
            g_uq_l = wgrad_rows("mla_q_wgrad", cq, dq_pre, Q_RANK, HW, tm)
            g_ukv_l = wgrad_rows("mla_kv_wgrad", ckv, dkv, KV_RANK, HW, tm)
            pending.append(("uq", l, g_uq_l.reshape(Q_RANK, HEADS, QPAD)[:, :, :NOPE + ROPE].transpose(1, 0, 2)))
            pending.append(("ukv", l, g_ukv_l.reshape(KV_RANK, HEADS, NOPE + VDIM).transpose(1, 0, 2)))
            dcq = _matmul("mla_q_bwd", dq_pre, w_uq, [], grid=(nt,), a_spec=_row_spec(tm, HW),
                          b_spec=pl.BlockSpec((None, Q_RANK, HW), lambda i_: (l, 0, 0)), extra_specs=[],
                          out_shapes=[_sds((T, Q_RANK), F32)], out_specs=[_row_spec(tm, Q_RANK)], dims=NT)[0]
            dckv = _matmul("mla_kv_bwd", dkv, w_ukv, [], grid=(nt,), a_spec=_row_spec(tm, HW),
                           b_spec=pl.BlockSpec((None, KV_RANK, HW), lambda i_: (l, 0, 0)), extra_specs=[],
                           out_shapes=[_sds((T, KV_RANK), F32)], out_specs=[_row_spec(tm, KV_RANK)], dims=NT)[0]

            def latent_bwd(la, qn, kvn, dq_, dkv_, dkr_):
                dcq_raw, dqn = _rms_bwd(la[:, :Q_RANK], qn, dq_)
                dckv_raw, dkvn = _rms_bwd(la[:, Q_RANK:Q_RANK + KV_RANK], kvn, dkv_)
                return jnp.concatenate([dcq_raw, dckv_raw, dkr_], axis=1), dqn, dkvn

            dlat, d_qn[l], d_kvn[l] = _rowwise(
                "mla_latent_bwd", latent_bwd, [lat, mla_q_norm[l].reshape(1, Q_RANK), mla_kv_norm[l].reshape(1, KV_RANK), dcq, dckv, dkr],
                grid=(nt,), in_specs=[_row_spec(tm, LAT_PAD), _const_spec((1, Q_RANK)), _const_spec((1, KV_RANK)),
                                      _row_spec(tm, Q_RANK), _row_spec(tm, KV_RANK), _row_spec(tm, 128)],
                out_shapes=[_sds((T, LAT_PAD), BF16), _sds((1, Q_RANK), F32), _sds((1, KV_RANK), F32)],
                out_specs=[_row_spec(tm, LAT_PAD), _const_spec((1, Q_RANK)), _const_spec((1, KV_RANK))], n_acc=2)
            g_dkv_l = wgrad_rows("mla_down_wgrad", h, dlat, D, LAT_PAD, tm)
            pending.append(("dkv", l, g_dkv_l[:, :LAT].reshape(N_DEV, dc, LAT)))
            dx, dyb, d_norm_mix[i] = _matmul(
                "mla_down_bwd", dlat, w_dkv, [xs_i, norm_mix[i].reshape(1, D), dx], grid=(nt,), a_spec=_row_spec(tm, LAT_PAD),
                b_spec=pl.BlockSpec((None, D, LAT_PAD), lambda i_: (l, 0, 0)), extra_specs=[_row_spec(tm, D), _const_spec((1, D)), _row_spec(tm, D)],
                out_shapes=[_sds((T, D), F32), _sds((T, D), BF16), _sds((1, D), F32)],
                out_specs=[_row_spec(tm, D), _row_spec(tm, D), _const_spec((1, D))], dims=NT, epilogue=norm_bwd_epilogue, n_sum=1)
        else:
            h, gp, ge, gate = mix_saved
            (dgate,) = back_rows("sgu_out_bwd", dyb, g_out[l], ec, [], None)
            pending.append(("out", l, wgrad_rows("sgu_out_wgrad", gate, dyb, ec, D, tb).reshape(N_DEV, ec, D)))
            dz, d_wsp[l], d_bsp[l], d_lng[l], d_lnb[l] = _sgu_mid_bwd(ge, gp, dgate, ln_g_full[l], ln_b_full[l], sgu_w_spatial[l], b_sp[l], 2)
            pending.append(("in", l, wgrad_cols("sgu_in_wgrad", h, dz, e2c)))
            dx, dyb, d_norm_mix[i], *rcv = back_cols("sgu_in_bwd", dz, transposed(g_in[l]), xs_i, norm_mix[i], dx, comm=sibling_comm())
            absorb(rcv)
    grad_x = dx.reshape(1, T, D)

    last_comm, last_names = chip_comm()
    late_g = [jnp.concatenate(d_norm_mix, 0), jnp.concatenate(d_norm_ffn, 0), d_final, jnp.concatenate(d_qn, 0), jnp.concatenate(d_kvn, 0)]
    late_w = [norm_mix, norm_ffn, final_norm, mla_q_norm, mla_kv_norm]
    late_m = [m_norm_mix, m_norm_ffn, m_final_norm, m_mla_q_norm, m_mla_kv_norm]
    late_v = [v_norm_mix, v_norm_ffn, v_final_norm, v_mla_q_norm, v_mla_kv_norm]
    late_sizes = [pad_to(g.size // 128, 8) for g in late_g]
    late_rows = sum(late_sizes)

    def adam_big(parts, w, m, v):
        lyr, rws, cls = w.shape
        rt = _tile(rws, 256)

        def fn(p, w_, m_, v_):
            g = (p[0].astype(F32) + p[1].astype(F32)) + (p[2].astype(F32) + p[3].astype(F32))
            return (g, *_adam(w_, g, m_, v_))

        spec = pl.BlockSpec((None, rt, cls), lambda l_, i_: (l_, i_, 0))
        return _rowwise("adam_large", fn, [parts, w, m, v], grid=(lyr, rws // rt),
                        in_specs=[pl.BlockSpec((N_CHIP, None, rt, cls), lambda l_, i_: (0, l_, i_, 0)), spec, spec, spec],
                        out_shapes=[_sds(w.shape, F32)] * 4, out_specs=[spec] * 4)

    stacked.update(dict(zip(last_names, _comm_call("grad_chip_exchange", last_comm))))
    (gathered_late,) = _all_gather("gather_small_grads", [packed(late_g, late_sizes)])
    big = {}
    big["in"] = adam_big(stacked["in"], sgu_w_in, m_sgu_w_in, v_sgu_w_in)
    big["up"] = adam_big(stacked["up"], ffn_w_up, m_ffn_w_up, v_ffn_w_up)
    big["down"] = adam_big(stacked["down"], ffn_w_down, m_ffn_w_down, v_ffn_w_down)
    big["out"] = adam_big(stacked["out"], sgu_w_out, m_sgu_w_out, v_sgu_w_out)
    big["dkv"] = adam_big(stacked["dkv"], mla_w_dkv, m_mla_w_dkv, v_mla_w_dkv)
    big["uq"] = adam_big(stacked["uq"], mla_w_uq, m_mla_w_uq, v_mla_w_uq)
    big["ukv"] = adam_big(stacked["ukv"], mla_w_ukv, m_mla_w_ukv, v_mla_w_ukv)
    big["o"] = adam_big(stacked["o"], mla_w_o, m_mla_w_o, v_mla_w_o)
    big_res = [big[nm][:4] for nm in ("dkv", "uq", "ukv", "o", "in", "out", "up", "down")]

    def sum8(p):
        return ((p[0] + p[1]) + (p[2] + p[3])) + ((p[4] + p[5]) + (p[6] + p[7]))

    def adam_packed(name, gathered, ws, ms, vs, sizes, rows, tile):
        spec = _row_spec(tile, 128)
        return _rowwise(name, lambda p, w_, m_, v_: (sum8(p), *_adam(w_, sum8(p), m_, v_)),
                        [gathered, packed(ws, sizes), packed(ms, sizes), packed(vs, sizes)], grid=(rows // tile,),
                        in_specs=[pl.BlockSpec((N_DEV, tile, 128), lambda i_: (0, i_, 0)), spec, spec, spec],
                        out_shapes=[_sds((rows, 128), F32)] * 4, out_specs=[spec] * 4)

    late_res = adam_packed("adam_small", gathered_late, late_w, late_m, late_v, late_sizes, late_rows, late_rows)
    early_res = adam_packed("adam_spatial", gathered_early, [sgu_w_spatial, sgu_b_spatial], [m_sgu_w_spatial, m_sgu_b_spatial],
                            [v_sgu_w_spatial, v_sgu_b_spatial], early_sizes[:2], early_rep, SMALL_ROWS)

    def unpack(res, sizes, k, like):
        off = sum(sizes[:k])
        return res[off:off + like.size // 128].reshape(like.shape)

    my_b = 4 * lax.axis_index("x") + 2 * lax.axis_index("y") + lax.axis_index("c")
    ln_w = jnp.concatenate([sgu_ln_g, sgu_ln_b], 0)
    ln_m = jnp.concatenate([m_sgu_ln_g, m_sgu_ln_b], 0)
    ln_v = jnp.concatenate([v_sgu_ln_g, v_sgu_ln_b], 0)
    ln_all = jnp.concatenate([gathered_early[:, early_rep:early_rep + n_sgu * E // 128], gathered_early[:, early_rep + n_ln:early_rep + n_ln + n_sgu * E // 128]], axis=1)
    ln_mine = lax.dynamic_slice_in_dim(ln_all.reshape(N_DEV, 2 * n_sgu, N_DEV, ec), my_b, 1, axis=2).reshape(N_DEV, 2 * n_sgu, ec)
    ln_g_, ln_d, ln_m2, ln_v2 = _rowwise(
        "adam_ln", lambda p, w_, m_, v_: (sum8(p), *_adam(w_, sum8(p), m_, v_)), [ln_mine, ln_w, ln_m, ln_v], grid=(1,),
        in_specs=[_const_spec(ln_mine.shape), _const_spec(ln_w.shape), _const_spec(ln_w.shape), _const_spec(ln_w.shape)],
        out_shapes=[_sds(ln_w.shape, F32)] * 4, out_specs=[_const_spec(ln_w.shape)] * 4)

    def family(pos):
        ln = [ln_g_, ln_d, ln_m2, ln_v2][pos]
        late = [unpack(late_res[pos], late_sizes, k, w_) for k, w_ in enumerate(late_w)]
        w_sp_, b_sp_ = unpack(early_res[pos], early_sizes, 0, sgu_w_spatial), unpack(early_res[pos], early_sizes, 1, sgu_b_spatial)
        bigs = [res[pos] for res in big_res]
        return [late[0], late[1], late[2], bigs[0], late[3], late[4], bigs[1], bigs[2], bigs[3],
                bigs[4], ln[:n_sgu], ln[n_sgu:], w_sp_, b_sp_, bigs[5], bigs[6], bigs[7]]

    return (loss, grad_x, *family(0), *family(1), *family(2), *family(3))
```

```python
import math

import jax
import jax.numpy as jnp
from jax import lax
from jax.experimental import pallas as pl
from jax.experimental.pallas import tpu as pltpu

F32 = jnp.float32
BF16 = jnp.bfloat16
MESH = pl.DeviceIdType.MESH

N_DEV = 8
N_CHIP = 4
HEADS = 8
NOPE = 128
ROPE = 64
VDIM = 128
QPAD = 256
Q_RANK = 256
KV_RANK = 128
LAT = Q_RANK + KV_RANK + ROPE
LAT_PAD = 512
ROPE_THETA = 10000.0
SGU_CHUNK = 128
SGU_GROUPS = 8
NORM_EPS = 1e-6
LN_EPS = 1e-5
ADAM_LR = 0.001
ADAM_B1 = 0.9
ADAM_B2 = 0.999
ADAM_EPS = 1e-08
ADAM_WD = 0.01
ADAM_STEP = 10
ATTN_SCALE = (NOPE + ROPE) ** -0.5
NEG = -1e30
EXP2_SCALE = ATTN_SCALE * math.log2(math.e)
VMEM_LIMIT = 56 * 1024 * 1024
SMALL_ROWS = 256

NN = (((1,), (0,)), ((), ()))
NT = (((1,), (1,)), ((), ()))
TN = (((0,), (0,)), ((), ()))
ANY = pl.BlockSpec(memory_space=pl.ANY)


def _pcall(body, **kw):
    return pl.pallas_call(body, **kw)


def _params(n_grid, side_effects=False):
    return pltpu.CompilerParams(dimension_semantics=("arbitrary",) * n_grid, vmem_limit_bytes=VMEM_LIMIT, has_side_effects=side_effects)


def _sds(shape, dtype):
    return jax.ShapeDtypeStruct(tuple(shape), dtype)


def _tile(n, want):
    t = min(n, want)
    assert n % t == 0, (n, want)
    return t


class _Comm:
    def __init__(self, operands, out_shapes, aliases, scratch, start, finish):
        self.operands, self.out_shapes, self.aliases, self.scratch = operands, out_shapes, aliases, scratch
        self.start, self.finish = start, finish


def _merge_comm(first, second):
    n_in, n_out, n_sc = len(first.operands), len(first.out_shapes), len(first.scratch)
    aliases = dict(first.aliases)
    aliases.update({n_in + k: n_out + v for k, v in second.aliases.items()})

    def start(ins, outs, sems):
        first.start(ins[:n_in], outs[:n_out], sems[:n_sc])
        second.start(ins[n_in:], outs[n_out:], sems[n_sc:])

    def finish(ins, outs, sems):
        first.finish(ins[:n_in], outs[:n_out], sems[:n_sc])
        second.finish(ins[n_in:], outs[n_out:], sems[n_sc:])

    return _Comm([*first.operands, *second.operands], [*first.out_shapes, *second.out_shapes], aliases,
                 [*first.scratch, *second.scratch], start, finish)


def _place():
    return lax.axis_index("x"), lax.axis_index("y"), lax.axis_index("c")


def _other_chips(x, y):
    return [(1 - x, y), (x, 1 - y), (1 - x, 1 - y)]


def _dev_index(dev):
    return 4 * dev[0] + 2 * dev[1] + dev[2]


def _comm_call(name, comm):
    c_in, c_out = len(comm.operands), len(comm.out_shapes)

    def body(*refs):
        ins, outs, sems = refs[:c_in], refs[c_in:c_in + c_out], refs[c_in + c_out:]
        comm.start(ins, outs, sems)
        comm.finish(ins, outs, sems)

    return _pcall(body, name=name, in_specs=[ANY] * c_in, out_specs=[ANY] * c_out, out_shape=comm.out_shapes,
                  scratch_shapes=comm.scratch, input_output_aliases=dict(comm.aliases),
                  compiler_params=pltpu.CompilerParams(has_side_effects=True))(*comm.operands)


def _call(name, body, operands, in_specs, out_shapes, out_specs, scratch, grid, comm=None):
    if comm is None:
        return _pcall(body, name=name, grid=grid, in_specs=in_specs, out_specs=out_specs, out_shape=out_shapes,
                      scratch_shapes=scratch, compiler_params=_params(len(grid)))(*operands)
    n_in, n_out, n_sc = len(operands), len(out_shapes), len(scratch)
    c_in, c_out = len(comm.operands), len(comm.out_shapes)

    def hosted(*refs):
        ins, cins = refs[:n_in], refs[n_in:n_in + c_in]
        o0 = n_in + c_in
        outs, couts = refs[o0:o0 + n_out], refs[o0 + n_out:o0 + n_out + c_out]
        rest = refs[o0 + n_out + c_out:]
        sc, csems = rest[:n_sc], rest[n_sc:]
        first = pl.program_id(0) == 0
        last = pl.program_id(0) == grid[0] - 1
        for d in range(1, len(grid)):
            first = jnp.logical_and(first, pl.program_id(d) == 0)
            last = jnp.logical_and(last, pl.program_id(d) == grid[d] - 1)

        @pl.when(first)
        def _():
            comm.start(cins, couts, csems)

        body(*ins, *outs, *sc)

        @pl.when(last)
        def _():
            comm.finish(cins, couts, csems)

    return _pcall(hosted, name=name, grid=grid, in_specs=[*in_specs, *[ANY] * c_in], out_specs=[*out_specs, *[ANY] * c_out],
                  out_shape=[*out_shapes, *comm.out_shapes], scratch_shapes=[*scratch, *comm.scratch],
                  input_output_aliases={n_in + k: n_out + v for k, v in comm.aliases.items()},
                  compiler_params=_params(len(grid), side_effects=True))(*operands, *comm.operands)


def _gather_level1(shards):
    n = len(shards)

    def copies(ins, outs, sems):
        send_sems, recv_sems, local_sems = sems
        x, y, c = _place()
        me, sibling = (x, y, c), (x, y, 1 - c)
        chips = _other_chips(x, y)

        def copy(a, k, block, to, src=None):
            slot = outs[a].at[_dev_index(block)]
            return pltpu.make_async_remote_copy(src_ref=slot if src is None else src, dst_ref=slot, send_sem=send_sems.at[a, k],
                                                recv_sem=recv_sems.at[a, k], device_id=to, device_id_type=MESH)

        mine = [pltpu.make_async_copy(ins[a], outs[a].at[_dev_index(me)], local_sems.at[a]) for a in range(n)]
        sends = [copy(a, 1 + j, me, (*chip, c), src=ins[a]) for j, chip in enumerate(chips) for a in range(n)]
        sends += [copy(a, 0, me, sibling, src=ins[a]) for a in range(n)]
        recvs = [copy(a, 1 + j, (*chip, c), me) for j, chip in enumerate(chips) for a in range(n)]
        recvs += [copy(a, 0, sibling, me) for a in range(n)]
        return mine, sends, recvs

    def start(ins, outs, sems):
        mine, sends, _ = copies(ins, outs, sems)
        for cp in mine + sends:
            cp.start()

    def finish(ins, outs, sems):
        mine, sends, recvs = copies(ins, outs, sems)
        for cp in recvs:
            cp.wait_recv()
        for cp in sends:
            cp.wait_send()
        for cp in mine:
            cp.wait()

    return _Comm(shards, [_sds((N_DEV, *a.shape), a.dtype) for a in shards], {},
                 [pltpu.SemaphoreType.DMA((n, 4)), pltpu.SemaphoreType.DMA((n, 4)), pltpu.SemaphoreType.DMA((n,))], start, finish)


def _gather_level2(bufs):
    n = len(bufs)

    def copies(outs, sems):
        send_sems, recv_sems = sems
        x, y, c = _place()
        sibling = (x, y, 1 - c)
        sends, recvs = [], []
        for j, chip in enumerate(_other_chips(x, y)):
            for a in range(n):
                have, want = outs[a].at[_dev_index((*chip, c))], outs[a].at[_dev_index((*chip, 1 - c))]
                sends.append(pltpu.make_async_remote_copy(src_ref=have, dst_ref=have, send_sem=send_sems.at[a, j], recv_sem=recv_sems.at[a, j],
                                                          device_id=sibling, device_id_type=MESH))
                recvs.append(pltpu.make_async_remote_copy(src_ref=want, dst_ref=want, send_sem=send_sems.at[a, j], recv_sem=recv_sems.at[a, j],
                                                          device_id=sibling, device_id_type=MESH))
        return sends, recvs

    def start(ins, outs, sems):
        for cp in copies(outs, sems)[0]:
            cp.start()

    def finish(ins, outs, sems):
        sends, recvs = copies(outs, sems)
        for cp in recvs:
            cp.wait_recv()
        for cp in sends:
            cp.wait_send()

    return _Comm(bufs, [_sds(b.shape, b.dtype) for b in bufs], {a: a for a in range(n)},
                 [pltpu.SemaphoreType.DMA((n, 3)), pltpu.SemaphoreType.DMA((n, 3))], start, finish)


def _all_gather(name, arrays):
    n = len(arrays)

    def body(*refs):
        ins = refs[:n]
        outs = refs[n:2 * n]
        send_sems, recv_sems, local_sems = refs[2 * n:]
        x, y, c = _place()
        me, sibling = (x, y, c), (x, y, 1 - c)
        chips = _other_chips(x, y)

        def copy(a, k, block, to, src=None):
            slot = outs[a].at[_dev_index(block)]
            return pltpu.make_async_remote_copy(src_ref=slot if src is None else src, dst_ref=slot, send_sem=send_sems.at[a, k],
                                                recv_sem=recv_sems.at[a, k], device_id=to, device_id_type=MESH)

        mine = [pltpu.make_async_copy(ins[a], outs[a].at[_dev_index(me)], local_sems.at[a]) for a in range(n)]
        for cp in mine:
            cp.start()
        first = []
        for j, chip in enumerate(chips):
            first += [copy(a, 1 + j, me, (*chip, c), src=ins[a]) for a in range(n)]
        first += [copy(a, 0, me, sibling, src=ins[a]) for a in range(n)]
        for cp in first:
            cp.start()
        passed = []
        for j, chip in enumerate(chips):
            for a in range(n):
                copy(a, 1 + j, (*chip, c), me).wait_recv()
                fwd = copy(a, 4 + j, (*chip, c), sibling)
                fwd.start()
                passed.append(fwd)
        for a in range(n):
            copy(a, 0, sibling, me).wait_recv()
            for j, chip in enumerate(chips):
                copy(a, 4 + j, (*chip, 1 - c), me).wait_recv()
        for cp in first + passed:
            cp.wait_send()
        for cp in mine:
            cp.wait()

    return _pcall(
        body, name=name, in_specs=[ANY] * n, out_specs=[ANY] * n,
        out_shape=[_sds((N_DEV, *a.shape), a.dtype) for a in arrays],
        scratch_shapes=[pltpu.SemaphoreType.DMA((n, 7)), pltpu.SemaphoreType.DMA((n, 7)), pltpu.SemaphoreType.DMA((n,))],
        compiler_params=pltpu.CompilerParams(has_side_effects=True),
    )(*arrays)


def _sibling_exchange(grads):
    n = len(grads)

    def start(ins, outs, sems):
        send_sems, recv_sems = sems
        x, y, c = _place()
        for a in range(n):
            for ch in range(N_CHIP):
                pltpu.make_async_remote_copy(src_ref=ins[a].at[2 * ch + 1 - c], dst_ref=outs[a].at[ch], send_sem=send_sems.at[a],
                                             recv_sem=recv_sems.at[a], device_id=(x, y, 1 - c), device_id_type=MESH).start()

    def finish(ins, outs, sems):
        send_sems, recv_sems = sems
        x, y, c = _place()
        for a in range(n):
            pltpu.make_async_remote_copy(src_ref=outs[a], dst_ref=outs[a], send_sem=send_sems.at[a], recv_sem=recv_sems.at[a],
                                         device_id=(x, y, 1 - c), device_id_type=MESH).wait()

    return _Comm(grads, [_sds((N_CHIP, *g.shape[1:]), g.dtype) for g in grads], {},
                 [pltpu.SemaphoreType.DMA((n,)), pltpu.SemaphoreType.DMA((n,))], start, finish)


def _chip_exchange(parts, slots, layers, stacked):
    n = len(parts)
    names = []
    for nm, _ in slots:
        if nm not in names:
            names.append(nm)
    shapes = {nm: _sds((N_CHIP, layers[nm], *parts[a].shape[1:]), parts[a].dtype) for a, (nm, _) in enumerate(slots)}
    kept = [nm for nm in names if stacked.get(nm) is not None]
    aliases = {n + k: names.index(nm) for k, nm in enumerate(kept)}

    def copies(ins, outs, sems):
        send_sems, recv_sems, local_sems = sems
        x, y, c = _place()
        mine = 2 * x + y
        local, sends, recvs = [], [], []
        for a, (nm, l) in enumerate(slots):
            buf = outs[names.index(nm)]
            local.append(pltpu.make_async_copy(ins[a].at[mine], buf.at[mine, l], local_sems.at[a]))
            for j, chip in enumerate(_other_chips(x, y)):
                theirs = buf.at[2 * chip[0] + chip[1], l]
                sends.append(pltpu.make_async_remote_copy(src_ref=ins[a].at[2 * chip[0] + chip[1]], dst_ref=buf.at[mine, l], send_sem=send_sems.at[a, j],
                                                          recv_sem=recv_sems.at[a, j], device_id=(*chip, c), device_id_type=MESH))
                recvs.append(pltpu.make_async_remote_copy(src_ref=theirs, dst_ref=theirs, send_sem=send_sems.at[a, j],
                                                          recv_sem=recv_sems.at[a, j], device_id=(*chip, c), device_id_type=MESH))
        return local, sends, recvs

    def start(ins, outs, sems):
        local, sends, _ = copies(ins, outs, sems)
        for cp in local + sends:
            cp.start()

    def finish(ins, outs, sems):
        local, sends, recvs = copies(ins, outs, sems)
        for cp in recvs:
            cp.wait_recv()
        for cp in sends:
            cp.wait_send()
        for cp in local:
            cp.wait()

    comm = _Comm([*parts, *[stacked[nm] for nm in kept]], [shapes[nm] for nm in names], aliases,
                 [pltpu.SemaphoreType.DMA((n, 3)), pltpu.SemaphoreType.DMA((n, 3)), pltpu.SemaphoreType.DMA((n,))], start, finish)
    return comm, names


def _matmul(name, a, b, extras, *, grid, a_spec, b_spec, extra_specs, out_shapes, out_specs, dims, k_axis=None, nk=1,
            acc_shape=None, epilogue=None, comm=None, n_sum=0, write=None):
    n_extra = len(extras)
    n_out = len(out_shapes)

    def body(*refs):
        a_ref, b_ref = refs[0], refs[1]
        ex = refs[2:2 + n_extra]
        outs = refs[2 + n_extra:2 + n_extra + n_out]
        prod = lax.dot_general(a_ref[...], b_ref[...], dims, preferred_element_type=F32)

        def finish(acc):
            if write is not None:
                write(outs, acc, *[e[...] for e in ex])
                return
            res = epilogue(acc, *[e[...] for e in ex]) if epilogue is not None else (acc,)
            first = None
            for d in range(len(grid)):
                if d != k_axis:
                    here = pl.program_id(d) == 0
                    first = here if first is None else jnp.logical_and(first, here)
            for idx, (o, r) in enumerate(zip(outs, res)):
                if idx < n_out - n_sum:
                    o[...] = r.astype(o.dtype)
                else:
                    @pl.when(first)
                    def _(o=o, r=r):
                        o[...] = r.astype(o.dtype)

                    @pl.when(jnp.logical_not(first))
                    def _(o=o, r=r):
                        o[...] += r.astype(o.dtype)

        if k_axis is None:
            finish(prod)
        else:
            acc_ref = refs[-1]
            k = pl.program_id(k_axis)

            @pl.when(k == 0)
            def _():
                acc_ref[...] = prod

            @pl.when(k > 0)
            def _():
                acc_ref[...] += prod

            @pl.when(k == nk - 1)
            def _():
                finish(acc_ref[...])

    scratch = [] if k_axis is None else [pltpu.VMEM(acc_shape, F32)]
    return _call(name, body, [a, b, *extras], [a_spec, b_spec, *extra_specs], list(out_shapes), list(out_specs), scratch, grid, comm)


def _rowwise(name, fn, operands, *, grid, in_specs, out_shapes, out_specs, n_acc=0, grid_spec_prefetch=None, comm=None):
    n_in = len(operands)
    n_out = len(out_shapes)
    n_pre = 0 if grid_spec_prefetch is None else 1

    def body(*refs):
        refs = refs[n_pre:]
        ins = refs[:n_in]
        outs = refs[n_in:n_in + n_out]
        res = fn(*[r[...] for r in ins])
        if not isinstance(res, (tuple, list)):
            res = (res,)
        first = pl.program_id(0) == 0
        for d in range(1, len(grid)):
            first = jnp.logical_and(first, pl.program_id(d) == 0)
        for idx, (o, r) in enumerate(zip(outs, res)):
            if idx < n_out - n_acc:
                o[...] = r.astype(o.dtype)
            else:
                @pl.when(first)
                def _(o=o, r=r):
                    o[...] = r.astype(o.dtype)

                @pl.when(jnp.logical_not(first))
                def _(o=o, r=r):
                    o[...] += r.astype(o.dtype)

    if comm is not None:
        return _call(name, body, list(operands), list(in_specs), list(out_shapes), list(out_specs), [], grid, comm)
    if grid_spec_prefetch is None:
        return _pcall(body, name=name, grid=grid, in_specs=in_specs, out_specs=out_specs, out_shape=out_shapes,
                      compiler_params=_params(len(grid)))(*operands)
    gs = pltpu.PrefetchScalarGridSpec(num_scalar_prefetch=1, grid=grid, in_specs=in_specs, out_specs=out_specs)
    return _pcall(body, name=name, grid_spec=gs, out_shape=out_shapes,
                  compiler_params=_params(len(grid)))(grid_spec_prefetch, *operands)


def _row_spec(tm, w):
    return pl.BlockSpec((tm, w), lambda i: (i, 0))


def _const_spec(shape):
    nd = len(shape)
    return pl.BlockSpec(tuple(shape), lambda *_: (0,) * nd)


def _rms_fwd(x, g):
    r = lax.rsqrt(jnp.mean(x * x, axis=-1, keepdims=True) + NORM_EPS)
    return x * r * g


def _rms_bwd(x, g, dy):
    r = lax.rsqrt(jnp.mean(x * x, axis=-1, keepdims=True) + NORM_EPS)
    xh = x * r
    u = dy * g
    dx = r * (u - xh * jnp.mean(u * xh, axis=-1, keepdims=True))
    dg = jnp.sum(dy * xh, axis=0, keepdims=True)
    return dx, dg


def _gelu_and_grad(z):
    cdf = 0.5 * (1.0 + lax.erf(z * (2.0 ** -0.5)))
    return cdf + z * jnp.exp(-0.5 * z * z) * ((2.0 * math.pi) ** -0.5), z * cdf


def _rope_fwd(x, cc, sa, sb):
    return x * cc + pltpu.roll(x, 96, 1) * sa + pltpu.roll(x, 32, 1) * sb


def _rope_bwd(d, cc, sa, sb):
    return d * cc + pltpu.roll(d * sa, 32, 1) + pltpu.roll(d * sb, 96, 1)


def _adam(w, g, m, v):
    m = ADAM_B1 * m + (1.0 - ADAM_B1) * g
    v = ADAM_B2 * v + (1.0 - ADAM_B2) * (g * g)
    m_hat = m / (1.0 - ADAM_B1 ** ADAM_STEP)
    v_hat = v / (1.0 - ADAM_B2 ** ADAM_STEP)
    delta = -ADAM_LR * (m_hat / (jnp.sqrt(v_hat) + ADAM_EPS) + ADAM_WD * w)
    return delta, m, v


def _flash_fwd(q, k, vt, tq, comm=None):
    h, t = vt.shape[0], q.shape[0]
    nq = t // tq

    chunk_blocks = [c for c in (4, 2, 1) if c < nq]

    def body(q_ref, k_ref, vt_ref, o_ref, lse_ref, m_ref, l_ref, acc_ref):
        qi = pl.program_id(1)
        m_ref[...] = jnp.full((1, tq), NEG, F32)
        l_ref[...] = jnp.zeros((1, tq), F32)
        acc_ref[...] = jnp.zeros((VDIM, tq), F32)

        def update(kb0, nblk, masked):
            kb = k_ref[pl.ds(pl.multiple_of(kb0 * tq, tq), nblk * tq), :]
            st = lax.dot_general(kb, q_ref[...], NT, preferred_element_type=F32)
            if masked:
                key = lax.broadcasted_iota(jnp.int32, (tq, tq), 0)
                qry = lax.broadcasted_iota(jnp.int32, (tq, tq), 1)
                st = jnp.where(key <= qry, st, NEG)
            m_old = m_ref[...]
            m_new = jnp.maximum(m_old, jnp.max(st, axis=0, keepdims=True))
            alpha = jnp.exp2((m_old - m_new) * EXP2_SCALE)
            pt = jnp.exp2((st - m_new) * EXP2_SCALE)
            l_ref[...] = alpha * l_ref[...] + jnp.sum(pt, axis=0, keepdims=True)
            ptb = pt.astype(BF16)
            pv = lax.dot_general(vt_ref[kb0], ptb[:tq], NN, preferred_element_type=F32)
            for j in range(1, nblk):
                pv += lax.dot_general(vt_ref[kb0 + j], ptb[j * tq:(j + 1) * tq], NN, preferred_element_type=F32)
            acc_ref[...] = alpha * acc_ref[...] + pv
            m_ref[...] = m_new

        start = jnp.int32(0)
        for c in chunk_blocks:
            take = (qi & c) != 0

            @pl.when(take)
            def _(start=start, c=c):
                update(start, c, False)

            start = start + jnp.where(take, c, 0)
        update(qi, 1, True)
        l = l_ref[...]
        o_ref[...] = (acc_ref[...] / l).T.astype(o_ref.dtype)
        lse_ref[...] = m_ref[...] * EXP2_SCALE + jnp.log2(l)

    return _call(
        "flash_fwd", body, [q, k, vt],
        [pl.BlockSpec((tq, QPAD), lambda hh, i: (i, hh)),
         pl.BlockSpec((t, QPAD), lambda hh, i: (0, hh)),
         pl.BlockSpec((None, nq, VDIM, tq), lambda hh, i: (hh, 0, 0, 0))],
        [_sds((t, h * VDIM), BF16), _sds((h, nq, 1, tq), F32)],
        [pl.BlockSpec((tq, VDIM), lambda hh, i: (i, hh)),
         pl.BlockSpec((None, None, 1, tq), lambda hh, i: (hh, i, 0, 0))],
        [pltpu.VMEM((1, tq), F32), pltpu.VMEM((1, tq), F32), pltpu.VMEM((VDIM, tq), F32)], (h, nq), comm)


def _flash_bwd(q, k, v, o, do, lse, tabs, tq, comm=None):
    t = q.shape[0]
    h = q.shape[1] // QPAD
    nq = t // tq

    def body(q_ref, k_ref, v_ref, o_ref, do_ref, lse_ref, cc_ref, sa_ref, sb_ref, dq_ref, dk_ref, dv_ref, delta_ref, dqt_ref):
        kj = pl.program_id(1)

        @pl.when(kj == 0)
        def _():
            dqt_ref[...] = jnp.zeros_like(dqt_ref)
            ones = jnp.ones((8, VDIM), BF16)
            for qi in range(nq):
                rows = pl.ds(qi * tq, tq)
                prod = do_ref[rows, :].astype(F32) * o_ref[rows, :].astype(F32)
                hi = prod.astype(BF16)
                lo = (prod - hi.astype(F32)).astype(BF16)
                delta_ref[qi] = (lax.dot_general(ones, hi, NT, preferred_element_type=F32)
                                 + lax.dot_general(ones, lo, NT, preferred_element_type=F32))

        kb = k_ref[...]
        vb = v_ref[...]
        kbt = kb.astype(F32).T.astype(BF16)
        dk_ref[...] = jnp.zeros_like(dk_ref)
        dv_ref[...] = jnp.zeros_like(dv_ref)

        def step(q0, nblk, masked):
            rows = pl.ds(pl.multiple_of(q0 * tq, tq), nblk * tq)
            qb = q_ref[rows, :]
            dob = do_ref[rows, :]
            lse = jnp.concatenate([lse_ref[q0 + j] for j in range(nblk)], axis=1)
            delta = jnp.concatenate([delta_ref[q0 + j, pl.ds(0, 1), :] for j in range(nblk)], axis=1)
            st = lax.dot_general(kb, qb, NT, preferred_element_type=F32)
            pt = jnp.exp2(st * EXP2_SCALE - lse)
            if masked:
                key = lax.broadcasted_iota(jnp.int32, (tq, tq), 0)
                qry = lax.broadcasted_iota(jnp.int32, (tq, tq), 1)
                pt = jnp.where(key <= qry, pt, 0.0)
            dv_ref[...] += lax.dot_general(pt.astype(BF16), dob, NN, preferred_element_type=F32)
            dpt = lax.dot_general(vb, dob, NT, preferred_element_type=F32)
            dst = (pt * (dpt - delta) * ATTN_SCALE).astype(BF16)
            dk_ref[...] += lax.dot_general(dst, qb, NN, preferred_element_type=F32)
            dqt = lax.dot_general(kbt, dst, NN, preferred_element_type=F32)
            for j in range(nblk):
                dqt_ref[q0 + j] += dqt[:, j * tq:(j + 1) * tq]

        later = nq - 1 - kj
        step(kj, 1, True)
        start = kj + 1
        for c in [c for c in (1, 2, 4) if c < nq]:
            take = (later & c) != 0

            @pl.when(take)
            def _(start=start, c=c):
                step(start, c, False)

            start = start + jnp.where(take, c, 0)

        @pl.when(kj == nq - 1)
        def _():
            for qi in range(nq):
                rows = pl.ds(qi * tq, tq)
                d = dqt_ref[qi].T
                roped = _rope_bwd(d[:, NOPE:], cc_ref[rows, :], sa_ref[rows, :], sb_ref[rows, :])
                dq_ref[rows, :] = jnp.concatenate([d[:, :NOPE], roped], axis=1).astype(BF16)

    head_q = pl.BlockSpec((t, QPAD), lambda hh, j: (0, hh))
    head_v = pl.BlockSpec((t, VDIM), lambda hh, j: (0, hh))
    table = pl.BlockSpec((t, 128), lambda hh, j: (0, 0))
    return _call(
        "flash_bwd", body, [q, k, v, o, do, lse, *tabs],
        [head_q, pl.BlockSpec((tq, QPAD), lambda hh, j: (j, hh)), pl.BlockSpec((tq, VDIM), lambda hh, j: (j, hh)), head_v, head_v,
         pl.BlockSpec((None, nq, 1, tq), lambda hh, j: (hh, 0, 0, 0)), table, table, table],
        [_sds((t, h * QPAD), BF16), _sds((t, h * QPAD), F32), _sds((t, h * VDIM), F32)],
        [head_q, pl.BlockSpec((tq, QPAD), lambda hh, j: (j, hh)), pl.BlockSpec((tq, VDIM), lambda hh, j: (j, hh))],
        [pltpu.VMEM((nq, 8, tq), F32), pltpu.VMEM((nq, QPAD, tq), F32)], (h, nq), comm)


def _tril_bf16(w):
    row = lax.broadcasted_iota(jnp.int32, w.shape, 0)
    col = lax.broadcasted_iota(jnp.int32, w.shape, 1)
    return jnp.where(col <= row, w, 0.0).astype(BF16)


def _layer_norm_parts(v0):
    mu = jnp.mean(v0, axis=-1, keepdims=True)
    vc = v0 - mu
    rstd = lax.rsqrt(jnp.mean(vc * vc, axis=-1, keepdims=True) + LN_EPS)
    return vc * rstd, rstd


def _sgu_mid_fwd(ge, ln_g, ln_b, w_sp, b_sp, chunks_per_step):
    t, e2 = ge.shape
    e = e2 // 2
    gd = e // SGU_GROUPS
    rows = SGU_CHUNK * chunks_per_step

    def body(u_ref, v_ref, g_ref, b_ref, w_ref, bs_ref, gate_ref):
        for ck in range(chunks_per_step):
            r = pl.ds(ck * SGU_CHUNK, SGU_CHUNK)
            xh, _ = _layer_norm_parts(v_ref[r, :].astype(F32))
            v1 = (xh * g_ref[...] + b_ref[...]).astype(BF16)
            for g in range(SGU_GROUPS):
                cols = pl.ds(g * gd, gd)
                mixed = lax.dot_general(_tril_bf16(w_ref[g]), v1[:, g * gd:(g + 1) * gd], NN, preferred_element_type=F32) + bs_ref[g]
                gate_ref[r, cols] = (u_ref[r, cols].astype(F32) * mixed).astype(BF16)

    return _pcall(
        body, name="sgu_mid_fwd", grid=(t // rows,),
        in_specs=[pl.BlockSpec((rows, e), lambda i: (i, 0)), pl.BlockSpec((rows, e), lambda i: (i, 1)),
                  _const_spec((1, e)), _const_spec((1, e)), _const_spec(w_sp.shape), _const_spec(b_sp.shape)],
        out_specs=pl.BlockSpec((rows, e), lambda i: (i, 0)),
        out_shape=_sds((t, e), BF16), compiler_params=_params(1),
    )(ge, ge, ln_g, ln_b, w_sp, b_sp)


def _sgu_mid_bwd(ge, gp, dgate, ln_g, ln_b, w_sp, b_sp, chunks_per_step):
    t, e2 = ge.shape
    e = e2 // 2
    gd = e // SGU_GROUPS
    rows = SGU_CHUNK * chunks_per_step

    def body(u_ref, v_ref, zu_ref, zv_ref, dg_ref, g_ref, b_ref, w_ref, bs_ref, dz_ref, dw_ref, dbs_ref, dlg_ref, dlb_ref):
        @pl.when(pl.program_id(0) == 0)
        def _():
            dw_ref[...] = jnp.zeros_like(dw_ref)
            dbs_ref[...] = jnp.zeros_like(dbs_ref)
            dlg_ref[...] = jnp.zeros_like(dlg_ref)
            dlb_ref[...] = jnp.zeros_like(dlb_ref)

        for ck in range(chunks_per_step):
            r = pl.ds(ck * SGU_CHUNK, SGU_CHUNK)
            xh, rstd = _layer_norm_parts(v_ref[r, :].astype(F32))
            v1 = (xh * g_ref[...] + b_ref[...]).astype(BF16)
            dv1_parts = []
            for g in range(SGU_GROUPS):
                cols = pl.ds(g * gd, gd)
                wc = _tril_bf16(w_ref[g])
                v1g = v1[:, g * gd:(g + 1) * gd]
                mixed = lax.dot_general(wc, v1g, NN, preferred_element_type=F32) + bs_ref[g]
                dgate = dg_ref[r, cols].astype(F32)
                dmixed = dgate * u_ref[r, cols].astype(F32)
                du = dgate * mixed
                dz_ref[r, cols] = (du * zu_ref[r, cols].astype(F32)).astype(BF16)
                dbs_ref[g] += jnp.sum(dmixed, axis=1, keepdims=True)
                dmb = dmixed.astype(BF16)
                dwg = lax.dot_general(dmb, v1g, NT, preferred_element_type=F32)
                row = lax.broadcasted_iota(jnp.int32, dwg.shape, 0)
                col = lax.broadcasted_iota(jnp.int32, dwg.shape, 1)
                dw_ref[g] += jnp.where(col <= row, dwg, 0.0)
                dv1_parts.append(lax.dot_general(wc, dmb, TN, preferred_element_type=F32))
            dv1 = jnp.concatenate(dv1_parts, axis=1)
            dlg_ref[...] += jnp.sum(dv1 * xh, axis=0, keepdims=True)
            dlb_ref[...] += jnp.sum(dv1, axis=0, keepdims=True)
            dxh = dv1 * g_ref[...]
            dv0 = rstd * (dxh - jnp.mean(dxh, axis=-1, keepdims=True) - xh * jnp.mean(dxh * xh, axis=-1, keepdims=True))
            dz_ref[r, pl.ds(e, e)] = (dv0 * zv_ref[r, :].astype(F32)).astype(BF16)

    half0 = pl.BlockSpec((rows, e), lambda i: (i, 0))
    half1 = pl.BlockSpec((rows, e), lambda i: (i, 1))
    return _pcall(
        body, name="sgu_mid_bwd", grid=(t // rows,),
        in_specs=[half0, half1, half0, half1, half0, _const_spec((1, e)), _const_spec((1, e)), _const_spec(w_sp.shape), _const_spec(b_sp.shape)],
        out_specs=[pl.BlockSpec((rows, e2), lambda i: (i, 0)), _const_spec(w_sp.shape), _const_spec(b_sp.shape), _const_spec((1, e)), _const_spec((1, e))],
        out_shape=[_sds((t, e2), BF16), _sds(w_sp.shape, F32), _sds(b_sp.shape, F32), _sds((1, e), F32), _sds((1, e), F32)],
        compiler_params=_params(1),
    )(ge, ge, gp, gp, dgate, ln_g, ln_b, w_sp, b_sp)


def kernel(x, positions, norm_mix, norm_ffn, final_norm, mla_w_dkv, mla_q_norm, mla_kv_norm, mla_w_uq, mla_w_ukv, mla_w_o, sgu_w_in, sgu_ln_g, sgu_ln_b, sgu_w_spatial, sgu_b_spatial, sgu_w_out, ffn_w_up, ffn_w_down, loss_target, m_norm_mix, m_norm_ffn, m_final_norm, m_mla_w_dkv, m_mla_q_norm, m_mla_kv_norm, m_mla_w_uq, m_mla_w_ukv, m_mla_w_o, m_sgu_w_in, m_sgu_ln_g, m_sgu_ln_b, m_sgu_w_spatial, m_sgu_b_spatial, m_sgu_w_out, m_ffn_w_up, m_ffn_w_down, v_norm_mix, v_norm_ffn, v_final_norm, v_mla_w_dkv, v_mla_q_norm, v_mla_kv_norm, v_mla_w_uq, v_mla_w_ukv, v_mla_w_o, v_sgu_w_in, v_sgu_ln_g, v_sgu_ln_b, v_sgu_w_spatial, v_sgu_b_spatial, v_sgu_w_out, v_ffn_w_up, v_ffn_w_down):
    _, T, D = x.shape
    depth = norm_mix.shape[0]
    n_mla, n_sgu = mla_w_dkv.shape[0], sgu_w_in.shape[0]
    assert depth % 2 == 0
    FF = ffn_w_up.shape[2] * N_DEV
    E = sgu_w_out.shape[1] * N_DEV
    ffc, ec, e2c = FF // N_DEV, E // N_DEV, 2 * E // N_DEV
    dc = D // N_DEV
    OW = HEADS * VDIM
    HW = HEADS * QPAD
    owc = OW // N_DEV
    tm = _tile(T, 1024)
    tb = _tile(T, 4096)
    tk = _tile(T, 512)
    tq = _tile(T, 512)
    ts = _tile(T, 256)
    nt = T // tm
    x2 = x.reshape(T, D)
    tgt = loss_target.reshape(T, D)
    cidx = lax.axis_index("c").astype(jnp.int32).reshape(1)

    ln_local = jnp.concatenate([sgu_ln_g, sgu_ln_b, jnp.zeros((8 - 2 * n_sgu, ec), F32)], axis=0)
    mla_sh = [[w[l].astype(BF16) for w in (mla_w_dkv, mla_w_uq, mla_w_ukv, mla_w_o)] for l in range(n_mla)]

    def mla_layouts(g_dkv, g_uq, g_ukv, g_o):
        w_dkv = jnp.pad(g_dkv.reshape(1, D, LAT), ((0, 0), (0, 0), (0, LAT_PAD - LAT)))
        w_uq = jnp.pad(g_uq, ((0, 0), (0, 0), (0, QPAD - NOPE - ROPE))).transpose(1, 0, 2).reshape(1, Q_RANK, HEADS * QPAD)
        w_ukv = g_ukv.transpose(1, 0, 2).reshape(1, KV_RANK, HEADS * (NOPE + VDIM))
        return w_dkv, w_uq, w_ukv, g_o.reshape(1, HEADS * VDIM, D)

    mla_w = [None] * n_mla
    mla_w[0] = mla_layouts(*_all_gather("gather_first_weights", mla_sh[0]))
    small_later = [a for l in range(1, n_mla) for a in mla_sh[l]] + [ln_local]
    ln_g_full, ln_b_full = [None] * n_sgu, [None] * n_sgu
    b_sp = sgu_b_spatial.reshape(n_sgu, SGU_GROUPS, SGU_CHUNK, 1)
    up_sh = [ffn_w_up[i].astype(BF16) for i in range(depth)]
    down_sh = [ffn_w_down[i].astype(BF16) for i in range(depth)]
    in_sh = [sgu_w_in[l].astype(BF16) for l in range(n_sgu)]
    out_sh = [sgu_w_out[l].astype(BF16) for l in range(n_sgu)]
    g_up, g_down, g_in, g_out = [None] * depth, [None] * depth, [None] * n_sgu, [None] * n_sgu

    inv_freq = ROPE_THETA ** (-jnp.arange(0, ROPE, 2, dtype=F32) / ROPE)
    zeros32 = jnp.zeros((ROPE // 2,), F32)
    inv128 = jnp.concatenate([inv_freq, inv_freq, zeros32, zeros32]).reshape(1, 128)
    sel_a = jnp.concatenate([-jnp.ones((32,), F32), zeros32, zeros32, zeros32]).reshape(1, 128)
    sel_b = jnp.concatenate([zeros32, jnp.ones((32,), F32), zeros32, zeros32]).reshape(1, 128)
    sel_c = jnp.concatenate([jnp.ones((64,), F32), zeros32, zeros32]).reshape(1, 128)

    def rope_tables(pos, inv, sa, sb, sc):
        ang = pos.astype(F32) * inv
        cs, sn = jnp.cos(ang), jnp.sin(ang)
        return cs * sc, sn * sa, sn * sb

    t_cc, t_sa, t_sb = _rowwise(
        "rope_tables", rope_tables, [positions.reshape(T, 1), inv128, sel_a, sel_b, sel_c], grid=(nt,),
        in_specs=[_row_spec(tm, 1)] + [_const_spec((1, 128))] * 4,
        out_shapes=[_sds((T, 128), F32)] * 3, out_specs=[_row_spec(tm, 128)] * 3)
    tab_specs = [_row_spec(tm, 128)] * 3

    def rmsnorm(xv, g):
        return _rowwise("rmsnorm", lambda a, gg: _rms_fwd(a, gg), [xv, g.reshape(1, D)], grid=(nt,),
                        in_specs=[_row_spec(tm, D), _const_spec((1, D))], out_shapes=_sds((T, D), BF16), out_specs=_row_spec(tm, D))

    def proj_cols(name, h, gw, nc, epilogue, n_out, comm=None):
        return _matmul(name, h, gw, [], grid=(N_DEV, T // tb),
                       a_spec=pl.BlockSpec((tb, D), lambda j, i: (i, 0)),
                       b_spec=pl.BlockSpec((None, D, nc), lambda j, i: (j, 0, 0)), extra_specs=[],
                       out_shapes=[_sds((T, nc * N_DEV), BF16)] * n_out, out_specs=[pl.BlockSpec((tb, nc), lambda j, i: (i, j))] * n_out,
                       dims=NN, epilogue=epilogue, comm=comm)

    def residual_norm(acc, xr, g):
        xn = acc + xr
        return xn, _rms_fwd(xn, g)

    def proj_rows_residual(name, a, gw, xres, g_next):
        kk_ = a.shape[1]
        return _matmul(name, a, gw.reshape(kk_, D), [xres, g_next.reshape(1, D)], grid=(T // tk,),
                       a_spec=_row_spec(tk, kk_), b_spec=_const_spec((kk_, D)), extra_specs=[_row_spec(tk, D), _const_spec((1, D))],
                       out_shapes=[_sds((T, D), F32), _sds((T, D), BF16)], out_specs=[_row_spec(tk, D)] * 2,
                       dims=NN, epilogue=residual_norm)

    def back_rows(name, dy, gw, kc, extras, epilogue, comm=None):
        return _matmul(name, dy, gw, extras, grid=(N_DEV, T // tb),
                       a_spec=pl.BlockSpec((tb, D), lambda j, i: (i, 0)),
                       b_spec=pl.BlockSpec((None, kc, D), lambda j, i: (j, 0, 0)),
                       extra_specs=[pl.BlockSpec((tb, kc), lambda j, i: (i, j))] * len(extras),
                       out_shapes=[_sds((T, kc * N_DEV), BF16)], out_specs=[pl.BlockSpec((tb, kc), lambda j, i: (i, j))],
                       dims=NT, epilogue=epilogue, comm=comm)

    def norm_bwd_epilogue(dh, xv, g, dxi):
        dxn, dg = _rms_bwd(xv, g, dh)
        return dxi + dxn, dxi + dxn, dg

    def transposed(gw):
        return gw.transpose(0, 2, 1).reshape(gw.shape[0] * gw.shape[2], D)

    def back_cols(name, da, gwt, xv, g, dx_in, comm=None):
        n = da.shape[1]
        row = _row_spec(tk, D)
        return _matmul(name, da, gwt, [xv, g.reshape(1, D), dx_in], grid=(T // tk,),
                       a_spec=_row_spec(tk, n), b_spec=_const_spec((n, D)), extra_specs=[row, _const_spec((1, D)), row],
                       out_shapes=[_sds((T, D), F32), _sds((T, D), BF16), _sds((1, D), F32)], out_specs=[row, row, _const_spec((1, D))],
                       dims=NN, epilogue=norm_bwd_epilogue, n_sum=1, comm=comm)

    def token_sum(tt):
        return dict(k_axis=1, nk=T // tt) if T // tt > 1 else dict(k_axis=None)

    def wgrad_cols(name, h, da, nc):
        return _matmul(name, h, da, [], grid=(N_DEV, T // tb),
                       a_spec=pl.BlockSpec((tb, D), lambda j, t: (t, 0)), b_spec=pl.BlockSpec((tb, nc), lambda j, t: (t, j)),
                       extra_specs=[], out_shapes=[_sds((N_DEV, D, nc), BF16)],
                       out_specs=[pl.BlockSpec((None, D, nc), lambda j, t: (j, 0, 0))],
                       dims=TN, acc_shape=(D, nc), **token_sum(tb))[0]

    def wgrad_rows(name, a, dy, kc, ncols, tt):
        return _matmul(name, a, dy, [], grid=(a.shape[1] // kc, T // tt),
                       a_spec=pl.BlockSpec((tt, kc), lambda j, t: (t, j)), b_spec=pl.BlockSpec((tt, ncols), lambda j, t: (t, 0)),
                       extra_specs=[], out_shapes=[_sds((a.shape[1], ncols), BF16)],
                       out_specs=[pl.BlockSpec((kc, ncols), lambda j, t: (j, 0))],
                       dims=TN, acc_shape=(kc, ncols), **token_sum(tt))[0]

    saved = []
    xs = x2
    for i in range(depth):
        l = i // 2
        if i == 0:
            h = rmsnorm(xs, norm_mix[0])
        if i % 2 == 0:
            w_dkv, w_uq, w_ukv, w_o = mla_w[l]
            lat = _matmul("mla_down", h, w_dkv, [], grid=(nt,), a_spec=_row_spec(tm, D),
                          b_spec=pl.BlockSpec((None, D, LAT_PAD), lambda i_: (0, 0, 0)), extra_specs=[],
                          out_shapes=[_sds((T, LAT_PAD), F32)], out_specs=[_row_spec(tm, LAT_PAD)], dims=NN)[0]

            def latent_post(la, qn, kvn, cc, sa, sb):
                cq = _rms_fwd(la[:, :Q_RANK], qn)
                ckv = _rms_fwd(la[:, Q_RANK:Q_RANK + KV_RANK], kvn)
                kr = _rope_fwd(la[:, Q_RANK + KV_RANK:], cc, sa, sb)
                return cq, ckv, kr

            cq, ckv, kr = _rowwise(
                "mla_latent", latent_post, [lat, mla_q_norm[l].reshape(1, Q_RANK), mla_kv_norm[l].reshape(1, KV_RANK), t_cc, t_sa, t_sb],
                grid=(nt,), in_specs=[_row_spec(tm, LAT_PAD), _const_spec((1, Q_RANK)), _const_spec((1, KV_RANK))] + tab_specs,
                out_shapes=[_sds((T, Q_RANK), BF16), _sds((T, KV_RANK), BF16), _sds((T, 128), BF16)],
                out_specs=[_row_spec(tm, Q_RANK), _row_spec(tm, KV_RANK), _row_spec(tm, 128)])

            def q_epilogue(acc, cc, sa, sb):
                parts = []
                for b in range(HEADS):
                    parts += [acc[:, b * QPAD:b * QPAD + NOPE], _rope_fwd(acc[:, b * QPAD + NOPE:(b + 1) * QPAD], cc, sa, sb)]
                return (jnp.concatenate(parts, axis=1),)

            q = _matmul("mla_q", cq, w_uq, [t_cc, t_sa, t_sb], grid=(nt,), a_spec=_row_spec(tm, Q_RANK),
                        b_spec=pl.BlockSpec((None, Q_RANK, HW), lambda i_: (0, 0, 0)), extra_specs=tab_specs,
                        out_shapes=[_sds((T, HW), BF16)], out_specs=[_row_spec(tm, HW)], dims=NN, epilogue=q_epilogue)[0]

            def kv_write(outs, acc, krb):
                k_ref, v_ref, vt_ref = outs
                for b in range(HEADS):
                    vb = acc[:, b * QPAD + NOPE:(b + 1) * QPAD]
                    k_ref[:, b * QPAD:b * QPAD + NOPE] = acc[:, b * QPAD:b * QPAD + NOPE].astype(BF16)
                    k_ref[:, b * QPAD + NOPE:(b + 1) * QPAD] = krb
                    v_ref[:, b * VDIM:(b + 1) * VDIM] = vb.astype(BF16)
                    vbt = vb.T.astype(BF16)
                    for u in range(tm // tq):
                        vt_ref[b, u] = vbt[:, u * tq:(u + 1) * tq]

            kk, vv, vt = _matmul("mla_kv", ckv, w_ukv, [kr], grid=(nt,), a_spec=_row_spec(tm, KV_RANK),
                                 b_spec=pl.BlockSpec((None, KV_RANK, HW), lambda i_: (0, 0, 0)), extra_specs=[_row_spec(tm, 128)],
                                 out_shapes=[_sds((T, HW), BF16), _sds((T, OW), BF16), _sds((HEADS, T // tq, VDIM, tq), BF16)],
                                 out_specs=[_row_spec(tm, HW), _row_spec(tm, OW), pl.BlockSpec((HEADS, tm // tq, VDIM, tq), lambda i_: (0, i_, 0, 0))],
                                 dims=NN, write=kv_write)
            group = [up_sh[i], down_sh[i], in_sh[l], out_sh[l], up_sh[i + 1], down_sh[i + 1]] + (small_later if i == 0 else [])
            o, lse, *bufs = _flash_fwd(q, kk, vt, tq, comm=_gather_level1(group))
            xm, h2, g_up[i], g_down[i] = _matmul(
                "mla_out", o, w_o, [xs, norm_ffn[i].reshape(1, D)], grid=(nt,), a_spec=_row_spec(tm, OW),
                b_spec=pl.BlockSpec((None, OW, D), lambda i_: (0, 0, 0)), extra_specs=[_row_spec(tm, D), _const_spec((1, D))],
                out_shapes=[_sds((T, D), F32), _sds((T, D), BF16)], out_specs=[_row_spec(tm, D)] * 2, dims=NN,
                epilogue=residual_norm, comm=_gather_level2(bufs[:2]))
            half_gathered = bufs[2:]
            mix_saved = (h, lat, cq, ckv, q, kk, vv, o, lse)
        else:
            gp, ge = proj_cols("sgu_in", h, g_in[l], e2c, _gelu_and_grad, 2)
            gate = _sgu_mid_fwd(ge, ln_g_full[l], ln_b_full[l], sgu_w_spatial[l], b_sp[l], 4)
            xm, h2 = proj_rows_residual("sgu_out", gate, g_out[l], xs, norm_ffn[i])
            mix_saved = (h, gp, ge, gate)
        r, s, *rest = proj_cols("ffn_up", h2, g_up[i], ffc, lambda acc: (jnp.maximum(acc, 0.0), jnp.square(jnp.maximum(acc, 0.0))), 2,
                                comm=_gather_level2(half_gathered) if i % 2 == 0 else None)
        if i % 2 == 0:
            g_in[l], g_out[l], g_up[i + 1], g_down[i + 1], *small_gathered = rest
        if i == 0:
            for l_ in range(1, n_mla):
                mla_w[l_] = mla_layouts(*small_gathered[4 * (l_ - 1):4 * l_])
            g_ln = small_gathered[-1]
            ln_g_full = [g_ln[:, l_, :].reshape(1, E) for l_ in range(n_sgu)]
            ln_b_full = [g_ln[:, n_sgu + l_, :].reshape(1, E) for l_ in range(n_sgu)]
        xo, h_next = proj_rows_residual("ffn_down", s, g_down[i], xm, norm_mix[i + 1] if i + 1 < depth else final_norm)
        saved.append((xs, xm, mix_saved, h2, r, s))
        xs, h = xo, h_next

    def loss_head(xv, tg, g):
        y = _rms_fwd(xv, g)
        err = y - tg
        part = 0.5 * jnp.sum(jnp.sum(err * err, axis=-1, keepdims=True), axis=0, keepdims=True) / D
        dx, dg = _rms_bwd(xv, g, err / D)
        return dx, dx, jnp.broadcast_to(part, (1, 128)), dg

    dx, dyb, loss_part, d_final = _rowwise(
        "loss_head", loss_head, [xs, tgt, final_norm.reshape(1, D)], grid=(nt,),
        in_specs=[_row_spec(tm, D), _row_spec(tm, D), _const_spec((1, D))],
        out_shapes=[_sds((T, D), F32), _sds((T, D), BF16), _sds((1, 128), F32), _sds((1, D), F32)],
        out_specs=[_row_spec(tm, D), _row_spec(tm, D), _const_spec((1, 128)), _const_spec((1, D))], n_acc=2)
    loss = lax.psum(loss_part[0, 0], ("x", "y", "c"))

    d_norm_mix, d_norm_ffn = [None] * depth, [None] * depth
    d_qn, d_kvn = [None] * n_mla, [None] * n_mla
    d_wsp, d_bsp, d_lng, d_lnb = [None] * n_sgu, [None] * n_sgu, [None] * n_sgu, [None] * n_sgu
    layers = {"dkv": n_mla, "uq": n_mla, "ukv": n_mla, "o": n_mla, "in": n_sgu, "out": n_sgu, "up": depth, "down": depth}
    stacked = {nm: None for nm in layers}
    pending = []
    summed = []

    def add_pair(g, rcv):
        _, rws, cls = g.shape
        g4 = g.reshape(N_CHIP, 2, rws, cls)
        rt = _tile(rws, 1024)
        return _rowwise("grad_pair_sum", lambda a, b_: a.astype(F32) + b_.astype(F32), [g4, rcv], grid=(N_CHIP, rws // rt),
                        in_specs=[pl.BlockSpec((None, None, rt, cls), lambda ch, i_, cr: (ch, cr[0], i_, 0)),
                                  pl.BlockSpec((None, rt, cls), lambda ch, i_, cr: (ch, i_, 0))],
                        out_shapes=_sds(rcv.shape, BF16), out_specs=pl.BlockSpec((None, rt, cls), lambda ch, i_, cr: (ch, i_, 0)),
                        grid_spec_prefetch=cidx)

    def sibling_comm():
        return _sibling_exchange([g for _, _, g in pending]) if pending else None

    def absorb(from_sibling):
        for (nm, l_, g), rcv in zip(pending, from_sibling):
            summed.append((nm, l_, add_pair(g, rcv)))
        pending.clear()

    def chip_comm():
        if pending:
            absorb(_comm_call("grad_sibling_exchange", sibling_comm()))
        comm, names = _chip_exchange([p for _, _, p in summed], [(nm, l_) for nm, l_, _ in summed], layers, stacked)
        summed.clear()
        return comm, names

    def rows128(a, rows):
        flat = a.reshape(-1, 128)
        return jnp.pad(flat, ((0, rows - flat.shape[0]), (0, 0)))

    def pad_to(n, mult):
        return -(-n // mult) * mult

    def packed(arrs, sizes):
        return jnp.concatenate([rows128(a, sz) for a, sz in zip(arrs, sizes)], axis=0)

    n_wsp, n_bsp, n_ln = sgu_w_spatial.size // 128, pad_to(sgu_b_spatial.size // 128, 8), pad_to(n_sgu * E // 128, 8)
    early_sizes = [n_wsp, pad_to(n_wsp + n_bsp, SMALL_ROWS) - n_wsp, n_ln, n_ln]
    early_rep = early_sizes[0] + early_sizes[1]
    gathered_early = None

    for i in reversed(range(depth)):
        l = i // 2
        xs_i, xm, mix_saved, h2, r, s = saved[i]
        da, *rcv = back_rows("ffn_down_bwd", dyb, g_down[i], ffc, [r], lambda acc, rr: (acc * (2.0 * rr.astype(F32)),), comm=sibling_comm())
        absorb(rcv)
        pending.append(("down", i, wgrad_rows("ffn_down_wgrad", s, dyb, ffc, D, tb).reshape(N_DEV, ffc, D)))
        pending.append(("up", i, wgrad_cols("ffn_up_wgrad", h2, da, ffc)))
        dx, dyb, d_norm_ffn[i], *rcv = back_cols("ffn_up_bwd", da, transposed(g_up[i]), xm, norm_ffn[i], dx, comm=sibling_comm())
        absorb(rcv)
        if i % 2 == 0:
            h, lat, cq, ckv, q, kk, vv, o, lse = mix_saved
            w_dkv, w_uq, w_ukv, w_o = mla_w[l]
            do = _matmul("mla_out_bwd", dyb, w_o, [], grid=(nt,), a_spec=_row_spec(tm, D),
                         b_spec=pl.BlockSpec((None, OW, D), lambda i_: (0, 0, 0)), extra_specs=[],
                         out_shapes=[_sds((T, OW), BF16)], out_specs=[_row_spec(tm, OW)], dims=NT)[0]
            g_o_l = wgrad_rows("mla_out_wgrad", o, dyb, OW, D, tm).reshape(N_DEV, owc, D)
            comm, names = chip_comm()
            if i == 0:
                early = packed([jnp.stack(d_wsp, 0), jnp.stack(d_bsp, 0), jnp.concatenate(d_lng, 0), jnp.concatenate(d_lnb, 0)], early_sizes)
                comm = _merge_comm(comm, _gather_level1([early]))
            dq_pre, dk, dv, *bufs = _flash_bwd(q, kk, vv, o, do, lse, (t_cc, t_sa, t_sb), tq, comm=comm)
            stacked.update(dict(zip(names, bufs)))
            pending.append(("o", l, g_o_l))

            def kv_pre(dkb, dvb, cc, sa, sb):
                parts, dkr = [], None
                for b in range(HEADS):
                    parts += [dkb[:, b * QPAD:b * QPAD + NOPE], dvb[:, b * VDIM:(b + 1) * VDIM]]
                    piece = dkb[:, b * QPAD + NOPE:(b + 1) * QPAD]
                    dkr = piece if dkr is None else dkr + piece
                return jnp.concatenate(parts, axis=1), _rope_bwd(dkr, cc, sa, sb)

            dkv, dkr, *rest = _rowwise("mla_dkv_rope", kv_pre, [dk, dv, t_cc, t_sa, t_sb], grid=(T // ts,),
                                       in_specs=[_row_spec(ts, HW), _row_spec(ts, OW)] + [_row_spec(ts, 128)] * 3,
                                       out_shapes=[_sds((T, HW), BF16), _sds((T, 128), F32)], out_specs=[_row_spec(ts, HW), _row_spec(ts, 128)],
                                       comm=_gather_level2(bufs[len(names):]) if i == 0 else None)
            if i == 0:
                (gathered_early,) = rest
            g_uq_l = wgrad_rows("mla_q_wgrad", cq, dq_pre, Q_RANK, HW, tm)
            g_ukv_l = wgrad_rows("mla_kv_wgrad", ckv, dkv, KV_RANK, HW, tm)
            pending.append(("uq", l, g_uq_l.reshape(Q_RANK, HEADS, QPAD)[:, :, :NOPE + ROPE].transpose(1, 0, 2)))
            pending.append(("ukv", l, g_ukv_l.reshape(KV_RANK, HEADS, NOPE + VDIM).transpose(1, 0, 2)))
            dcq = _matmul("mla_q_bwd", dq_pre, w_uq, [], grid=(nt,), a_spec=_row_spec(tm, HW),
                          b_spec=pl.BlockSpec((None, Q_RANK, HW), lambda i_: (0, 0, 0)), extra_specs=[],
                          out_shapes=[_sds((T, Q_RANK), F32)], out_specs=[_row_spec(tm, Q_RANK)], dims=NT)[0]
            dckv = _matmul("mla_kv_bwd", dkv, w_ukv, [], grid=(nt,), a_spec=_row_spec(tm, HW),
                           b_spec=pl.BlockSpec((None, KV_RANK, HW), lambda i_: (0, 0, 0)), extra_specs=[],
                           out_shapes=[_sds((T, KV_RANK), F32)], out_specs=[_row_spec(tm, KV_RANK)], dims=NT)[0]

            def latent_bwd(la, qn, kvn, dq_, dkv_, dkr_):
                dcq_raw, dqn = _rms_bwd(la[:, :Q_RANK], qn, dq_)
                dckv_raw, dkvn = _rms_bwd(la[:, Q_RANK:Q_RANK + KV_RANK], kvn, dkv_)
                return jnp.concatenate([dcq_raw, dckv_raw, dkr_], axis=1), dqn, dkvn

            dlat, d_qn[l], d_kvn[l] = _rowwise(
                "mla_latent_bwd", latent_bwd, [lat, mla_q_norm[l].reshape(1, Q_RANK), mla_kv_norm[l].reshape(1, KV_RANK), dcq, dckv, dkr],
                grid=(nt,), in_specs=[_row_spec(tm, LAT_PAD), _const_spec((1, Q_RANK)), _const_spec((1, KV_RANK)),
                                      _row_spec(tm, Q_RANK), _row_spec(tm, KV_RANK), _row_spec(tm, 128)],
                out_shapes=[_sds((T, LAT_PAD), BF16), _sds((1, Q_RANK), F32), _sds((1, KV_RANK), F32)],
                out_specs=[_row_spec(tm, LAT_PAD), _const_spec((1, Q_RANK)), _const_spec((1, KV_RANK))], n_acc=2)
            g_dkv_l = wgrad_rows("mla_down_wgrad", h, dlat, D, LAT_PAD, tm)
            pending.append(("dkv", l, g_dkv_l[:, :LAT].reshape(N_DEV, dc, LAT)))
            dx, dyb, d_norm_mix[i] = _matmul(
                "mla_down_bwd", dlat, w_dkv, [xs_i, norm_mix[i].reshape(1, D), dx], grid=(nt,), a_spec=_row_spec(tm, LAT_PAD),
                b_spec=pl.BlockSpec((None, D, LAT_PAD), lambda i_: (0, 0, 0)), extra_specs=[_row_spec(tm, D), _const_spec((1, D)), _row_spec(tm, D)],
                out_shapes=[_sds((T, D), F32), _sds((T, D), BF16), _sds((1, D), F32)],
                out_specs=[_row_spec(tm, D), _row_spec(tm, D), _const_spec((1, D))], dims=NT, epilogue=norm_bwd_epilogue, n_sum=1)
        else:
            h, gp, ge, gate = mix_saved
            (dgate,) = back_rows("sgu_out_bwd", dyb, g_out[l], ec, [], None)
            pending.append(("out", l, wgrad_rows("sgu_out_wgrad", gate, dyb, ec, D, tb).reshape(N_DEV, ec, D)))
            dz, d_wsp[l], d_bsp[l], d_lng[l], d_lnb[l] = _sgu_mid_bwd(ge, gp, dgate, ln_g_full[l], ln_b_full[l], sgu_w_spatial[l], b_sp[l], 2)
            pending.append(("in", l, wgrad_cols("sgu_in_wgrad", h, dz, e2c)))
            dx, dyb, d_norm_mix[i], *rcv = back_cols("sgu_in_bwd", dz, transposed(g_in[l]), xs_i, norm_mix[i], dx, comm=sibling_comm())
            absorb(rcv)
    grad_x = dx.reshape(1, T, D)

    last_comm, last_names = chip_comm()
    late_g = [jnp.concatenate(d_norm_mix, 0), jnp.concatenate(d_norm_ffn, 0), d_final, jnp.concatenate(d_qn, 0), jnp.concatenate(d_kvn, 0)]
    late_w = [norm_mix, norm_ffn, final_norm, mla_q_norm, mla_kv_norm]
    late_m = [m_norm_mix, m_norm_ffn, m_final_norm, m_mla_q_norm, m_mla_kv_norm]
    late_v = [v_norm_mix, v_norm_ffn, v_final_norm, v_mla_q_norm, v_mla_kv_norm]
    late_sizes = [pad_to(g.size // 128, 8) for g in late_g]
    late_rows = sum(late_sizes)

    def adam_big(parts, w, m, v):
        lyr, rws, cls = w.shape
        rt = _tile(rws, 256)

        def fn(p, w_, m_, v_):
            g = (p[0].astype(F32) + p[1].astype(F32)) + (p[2].astype(F32) + p[3].astype(F32))
            return (g, *_adam(w_, g, m_, v_))

        spec = pl.BlockSpec((None, rt, cls), lambda l_, i_: (l_, i_, 0))
        return _rowwise("adam_large", fn, [parts, w, m, v], grid=(lyr, rws // rt),
                        in_specs=[pl.BlockSpec((N_CHIP, None, rt, cls), lambda l_, i_: (0, l_, i_, 0)), spec, spec, spec],
                        out_shapes=[_sds(w.shape, F32)] * 4, out_specs=[spec] * 4)

    stacked.update(dict(zip(last_names, _comm_call("grad_chip_exchange", last_comm))))
    (gathered_late,) = _all_gather("gather_small_grads", [packed(late_g, late_sizes)])
    big = {}
    big["in"] = adam_big(stacked["in"], sgu_w_in, m_sgu_w_in, v_sgu_w_in)
    big["up"] = adam_big(stacked["up"], ffn_w_up, m_ffn_w_up, v_ffn_w_up)
    big["down"] = adam_big(stacked["down"], ffn_w_down, m_ffn_w_down, v_ffn_w_down)
    big["out"] = adam_big(stacked["out"], sgu_w_out, m_sgu_w_out, v_sgu_w_out)
    big["dkv"] = adam_big(stacked["dkv"], mla_w_dkv, m_mla_w_dkv, v_mla_w_dkv)
    big["uq"] = adam_big(stacked["uq"], mla_w_uq, m_mla_w_uq, v_mla_w_uq)
    big["ukv"] = adam_big(stacked["ukv"], mla_w_ukv, m_mla_w_ukv, v_mla_w_ukv)
    big["o"] = adam_big(stacked["o"], mla_w_o, m_mla_w_o, v_mla_w_o)
    big_res = [big[nm][:4] for nm in ("dkv", "uq", "ukv", "o", "in", "out", "up", "down")]

    def sum8(p):
        return ((p[0] + p[1]) + (p[2] + p[3])) + ((p[4] + p[5]) + (p[6] + p[7]))

    def adam_packed(name, gathered, ws, ms, vs, sizes, rows, tile):
        spec = _row_spec(tile, 128)
        return _rowwise(name, lambda p, w_, m_, v_: (sum8(p), *_adam(w_, sum8(p), m_, v_)),
                        [gathered, packed(ws, sizes), packed(ms, sizes), packed(vs, sizes)], grid=(rows // tile,),
                        in_specs=[pl.BlockSpec((N_DEV, tile, 128), lambda i_: (0, i_, 0)), spec, spec, spec],
                        out_shapes=[_sds((rows, 128), F32)] * 4, out_specs=[spec] * 4)

    late_res = adam_packed("adam_small", gathered_late, late_w, late_m, late_v, late_sizes, late_rows, late_rows)
    early_res = adam_packed("adam_spatial", gathered_early, [sgu_w_spatial, sgu_b_spatial], [m_sgu_w_spatial, m_sgu_b_spatial],
                            [v_sgu_w_spatial, v_sgu_b_spatial], early_sizes[:2], early_rep, SMALL_ROWS)

    def unpack(res, sizes, k, like):
        off = sum(sizes[:k])
        return res[off:off + like.size // 128].reshape(like.shape)

    my_b = 4 * lax.axis_index("x") + 2 * lax.axis_index("y") + lax.axis_index("c")
    ln_w = jnp.concatenate([sgu_ln_g, sgu_ln_b], 0)
    ln_m = jnp.concatenate([m_sgu_ln_g, m_sgu_ln_b], 0)
    ln_v = jnp.concatenate([v_sgu_ln_g, v_sgu_ln_b], 0)
    ln_all = jnp.concatenate([gathered_early[:, early_rep:early_rep + n_sgu * E // 128], gathered_early[:, early_rep + n_ln:early_rep + n_ln + n_sgu * E // 128]], axis=1)
    ln_mine = lax.dynamic_slice_in_dim(ln_all.reshape(N_DEV, 2 * n_sgu, N_DEV, ec), my_b, 1, axis=2).reshape(N_DEV, 2 * n_sgu, ec)
    ln_g_, ln_d, ln_m2, ln_v2 = _rowwise(
        "adam_ln", lambda p, w_, m_, v_: (sum8(p), *_adam(w_, sum8(p), m_, v_)), [ln_mine, ln_w, ln_m, ln_v], grid=(1,),
        in_specs=[_const_spec(ln_mine.shape), _const_spec(ln_w.shape), _const_spec(ln_w.shape), _const_spec(ln_w.shape)],
        out_shapes=[_sds(ln_w.shape, F32)] * 4, out_specs=[_const_spec(ln_w.shape)] * 4)

    def family(pos):
        ln = [ln_g_, ln_d, ln_m2, ln_v2][pos]
        late = [unpack(late_res[pos], late_sizes, k, w_) for k, w_ in enumerate(late_w)]
        w_sp_, b_sp_ = unpack(early_res[pos], early_sizes, 0, sgu_w_spatial), unpack(early_res[pos], early_sizes, 1, sgu_b_spatial)
        bigs = [res[pos] for res in big_res]
        return [late[0], late[1], late[2], bigs[0], late[3], late[4], bigs[1], bigs[2], bigs[3],
                bigs[4], ln[:n_sgu], ln[n_sgu:], w_sp_, b_sp_, bigs[5], bigs[6], bigs[7]]

    return (loss, grad_x, *family(0), *family(1), *family(2), *family(3))
```

```python
import math

import jax
import jax.numpy as jnp
from jax import lax
from jax.experimental import pallas as pl
from jax.experimental.pallas import tpu as pltpu

F32 = jnp.float32
BF16 = jnp.bfloat16
MESH = pl.DeviceIdType.MESH

N_DEV = 8
N_CHIP = 4
HEADS = 8
NOPE = 128
ROPE = 64
VDIM = 128
QPAD = 256
Q_RANK = 256
KV_RANK = 128
LAT = Q_RANK + KV_RANK + ROPE
LAT_PAD = 512
ROPE_THETA = 10000.0
SGU_CHUNK = 128
SGU_GROUPS = 8
NORM_EPS = 1e-6
LN_EPS = 1e-5
ADAM_LR = 0.001
ADAM_B1 = 0.9
ADAM_B2 = 0.999
ADAM_EPS = 1e-08
ADAM_WD = 0.01
ADAM_STEP = 10
ATTN_SCALE = (NOPE + ROPE) ** -0.5
NEG = -1e30
EXP2_SCALE = ATTN_SCALE * math.log2(math.e)
VMEM_LIMIT = 56 * 1024 * 1024
SMALL_ROWS = 256

NN = (((1,), (0,)), ((), ()))
NT = (((1,), (1,)), ((), ()))
TN = (((0,), (0,)), ((), ()))
ANY = pl.BlockSpec(memory_space=pl.ANY)


def _pcall(body, **kw):
    return pl.pallas_call(body, **kw)


def _params(n_grid, side_effects=False):
    return pltpu.CompilerParams(dimension_semantics=("arbitrary",) * n_grid, vmem_limit_bytes=VMEM_LIMIT, has_side_effects=side_effects)


def _sds(shape, dtype):
    return jax.ShapeDtypeStruct(tuple(shape), dtype)


def _tile(n, want):
    t = min(n, want)
    assert n % t == 0, (n, want)
    return t


class _Comm:
    def __init__(self, operands, out_shapes, aliases, scratch, start, finish):
        self.operands, self.out_shapes, self.aliases, self.scratch = operands, out_shapes, aliases, scratch
        self.start, self.finish = start, finish


def _merge_comm(first, second):
    n_in, n_out, n_sc = len(first.operands), len(first.out_shapes), len(first.scratch)
    aliases = dict(first.aliases)
    aliases.update({n_in + k: n_out + v for k, v in second.aliases.items()})

    def start(ins, outs, sems):
        first.start(ins[:n_in], outs[:n_out], sems[:n_sc])
        second.start(ins[n_in:], outs[n_out:], sems[n_sc:])

    def finish(ins, outs, sems):
        first.finish(ins[:n_in], outs[:n_out], sems[:n_sc])
        second.finish(ins[n_in:], outs[n_out:], sems[n_sc:])

    return _Comm([*first.operands, *second.operands], [*first.out_shapes, *second.out_shapes], aliases,
                 [*first.scratch, *second.scratch], start, finish)


def _place():
    return lax.axis_index("x"), lax.axis_index("y"), lax.axis_index("c")


def _other_chips(x, y):
    return [(1 - x, y), (x, 1 - y), (1 - x, 1 - y)]


def _dev_index(dev):
    return 4 * dev[0] + 2 * dev[1] + dev[2]


def _comm_call(name, comm):
    c_in, c_out = len(comm.operands), len(comm.out_shapes)

    def body(*refs):
        ins, outs, sems = refs[:c_in], refs[c_in:c_in + c_out], refs[c_in + c_out:]
        comm.start(ins, outs, sems)
        comm.finish(ins, outs, sems)

    return _pcall(body, name=name, in_specs=[ANY] * c_in, out_specs=[ANY] * c_out, out_shape=comm.out_shapes,
                  scratch_shapes=comm.scratch, input_output_aliases=dict(comm.aliases),
                  compiler_params=pltpu.CompilerParams(has_side_effects=True))(*comm.operands)


def _call(name, body, operands, in_specs, out_shapes, out_specs, scratch, grid, comm=None):
    if comm is None:
        return _pcall(body, name=name, grid=grid, in_specs=in_specs, out_specs=out_specs, out_shape=out_shapes,
                      scratch_shapes=scratch, compiler_params=_params(len(grid)))(*operands)
    n_in, n_out, n_sc = len(operands), len(out_shapes), len(scratch)
    c_in, c_out = len(comm.operands), len(comm.out_shapes)

    def hosted(*refs):
        ins, cins = refs[:n_in], refs[n_in:n_in + c_in]
        o0 = n_in + c_in
        outs, couts = refs[o0:o0 + n_out], refs[o0 + n_out:o0 + n_out + c_out]
        rest = refs[o0 + n_out + c_out:]
        sc, csems = rest[:n_sc], rest[n_sc:]
        first = pl.program_id(0) == 0
        last = pl.program_id(0) == grid[0] - 1
        for d in range(1, len(grid)):
            first = jnp.logical_and(first, pl.program_id(d) == 0)
            last = jnp.logical_and(last, pl.program_id(d) == grid[d] - 1)

        @pl.when(first)
        def _():
            comm.start(cins, couts, csems)

        body(*ins, *outs, *sc)

        @pl.when(last)
        def _():
            comm.finish(cins, couts, csems)

    return _pcall(hosted, name=name, grid=grid, in_specs=[*in_specs, *[ANY] * c_in], out_specs=[*out_specs, *[ANY] * c_out],
                  out_shape=[*out_shapes, *comm.out_shapes], scratch_shapes=[*scratch, *comm.scratch],
                  input_output_aliases={n_in + k: n_out + v for k, v in comm.aliases.items()},
                  compiler_params=_params(len(grid), side_effects=True))(*operands, *comm.operands)


def _gather_level1(shards):
    n = len(shards)

    def copies(ins, outs, sems):
        send_sems, recv_sems, local_sems = sems
        x, y, c = _place()
        me, sibling = (x, y, c), (x, y, 1 - c)
        chips = _other_chips(x, y)

        def copy(a, k, block, to, src=None):
            slot = outs[a].at[_dev_index(block)]
            return pltpu.make_async_remote_copy(src_ref=slot if src is None else src, dst_ref=slot, send_sem=send_sems.at[a, k],
                                                recv_sem=recv_sems.at[a, k], device_id=to, device_id_type=MESH)

        mine = [pltpu.make_async_copy(ins[a], outs[a].at[_dev_index(me)], local_sems.at[a]) for a in range(n)]
        sends = [copy(a, 1 + j, me, (*chip, c), src=ins[a]) for j, chip in enumerate(chips) for a in range(n)]
        sends += [copy(a, 0, me, sibling, src=ins[a]) for a in range(n)]
        recvs = [copy(a, 1 + j, (*chip, c), me) for j, chip in enumerate(chips) for a in range(n)]
        recvs += [copy(a, 0, sibling, me) for a in range(n)]
        return mine, sends, recvs

    def start(ins, outs, sems):
        mine, sends, _ = copies(ins, outs, sems)
        for cp in mine + sends:
            cp.start()

    def finish(ins, outs, sems):
        mine, sends, recvs = copies(ins, outs, sems)
        for cp in recvs:
            cp.wait_recv()
        for cp in sends:
            cp.wait_send()
        for cp in mine:
            cp.wait()

    return _Comm(shards, [_sds((N_DEV, *a.shape), a.dtype) for a in shards], {},
                 [pltpu.SemaphoreType.DMA((n, 4)), pltpu.SemaphoreType.DMA((n, 4)), pltpu.SemaphoreType.DMA((n,))], start, finish)


def _gather_level2(bufs):
    n = len(bufs)

    def copies(outs, sems):
        send_sems, recv_sems = sems
        x, y, c = _place()
        sibling = (x, y, 1 - c)
        sends, recvs = [], []
        for j, chip in enumerate(_other_chips(x, y)):
            for a in range(n):
                have, want = outs[a].at[_dev_index((*chip, c))], outs[a].at[_dev_index((*chip, 1 - c))]
                sends.append(pltpu.make_async_remote_copy(src_ref=have, dst_ref=have, send_sem=send_sems.at[a, j], recv_sem=recv_sems.at[a, j],
                                                          device_id=sibling, device_id_type=MESH))
                recvs.append(pltpu.make_async_remote_copy(src_ref=want, dst_ref=want, send_sem=send_sems.at[a, j], recv_sem=recv_sems.at[a, j],
                                                          device_id=sibling, device_id_type=MESH))
        return sends, recvs

    def start(ins, outs, sems):
        for cp in copies(outs, sems)[0]:
            cp.start()

    def finish(ins, outs, sems):
        sends, recvs = copies(outs, sems)
        for cp in recvs:
            cp.wait_recv()
        for cp in sends:
            cp.wait_send()

    return _Comm(bufs, [_sds(b.shape, b.dtype) for b in bufs], {a: a for a in range(n)},
                 [pltpu.SemaphoreType.DMA((n, 3)), pltpu.SemaphoreType.DMA((n, 3))], start, finish)


def _all_gather(name, arrays):
    n = len(arrays)

    def body(*refs):
        ins = refs[:n]
        outs = refs[n:2 * n]
        send_sems, recv_sems, local_sems = refs[2 * n:]
        x, y, c = _place()
        me, sibling = (x, y, c), (x, y, 1 - c)
        chips = _other_chips(x, y)

        def copy(a, k, block, to, src=None):
            slot = outs[a].at[_dev_index(block)]
            return pltpu.make_async_remote_copy(src_ref=slot if src is None else src, dst_ref=slot, send_sem=send_sems.at[a, k],
                                                recv_sem=recv_sems.at[a, k], device_id=to, device_id_type=MESH)

        mine = [pltpu.make_async_copy(ins[a], outs[a].at[_dev_index(me)], local_sems.at[a]) for a in range(n)]
        for cp in mine:
            cp.start()
        first = []
        for j, chip in enumerate(chips):
            first += [copy(a, 1 + j, me, (*chip, c), src=ins[a]) for a in range(n)]
        first += [copy(a, 0, me, sibling, src=ins[a]) for a in range(n)]
        for cp in first:
            cp.start()
        passed = []
        for j, chip in enumerate(chips):
            for a in range(n):
                copy(a, 1 + j, (*chip, c), me).wait_recv()
                fwd = copy(a, 4 + j, (*chip, c), sibling)
                fwd.start()
                passed.append(fwd)
        for a in range(n):
            copy(a, 0, sibling, me).wait_recv()
            for j, chip in enumerate(chips):
                copy(a, 4 + j, (*chip, 1 - c), me).wait_recv()
        for cp in first + passed:
            cp.wait_send()
        for cp in mine:
            cp.wait()

    return _pcall(
        body, name=name, in_specs=[ANY] * n, out_specs=[ANY] * n,
        out_shape=[_sds((N_DEV, *a.shape), a.dtype) for a in arrays],
        scratch_shapes=[pltpu.SemaphoreType.DMA((n, 7)), pltpu.SemaphoreType.DMA((n, 7)), pltpu.SemaphoreType.DMA((n,))],
        compiler_params=pltpu.CompilerParams(has_side_effects=True),
    )(*arrays)


def _sibling_exchange(grads):
    n = len(grads)

    def start(ins, outs, sems):
        send_sems, recv_sems = sems
        x, y, c = _place()
        for a in range(n):
            for ch in range(N_CHIP):
                pltpu.make_async_remote_copy(src_ref=ins[a].at[2 * ch + 1 - c], dst_ref=outs[a].at[ch], send_sem=send_sems.at[a],
                                             recv_sem=recv_sems.at[a], device_id=(x, y, 1 - c), device_id_type=MESH).start()

    def finish(ins, outs, sems):
        send_sems, recv_sems = sems
        x, y, c = _place()
        for a in range(n):
            pltpu.make_async_remote_copy(src_ref=outs[a], dst_ref=outs[a], send_sem=send_sems.at[a], recv_sem=recv_sems.at[a],
                                         device_id=(x, y, 1 - c), device_id_type=MESH).wait()

    return _Comm(grads, [_sds((N_CHIP, *g.shape[1:]), g.dtype) for g in grads], {},
                 [pltpu.SemaphoreType.DMA((n,)), pltpu.SemaphoreType.DMA((n,))], start, finish)


def _chip_exchange(parts, slots, layers, stacked):
    n = len(parts)
    names = []
    for nm, _ in slots:
        if nm not in names:
            names.append(nm)
    shapes = {nm: _sds((N_CHIP, layers[nm], *parts[a].shape[1:]), parts[a].dtype) for a, (nm, _) in enumerate(slots)}
    kept = [nm for nm in names if stacked.get(nm) is not None]
    aliases = {n + k: names.index(nm) for k, nm in enumerate(kept)}

    def copies(ins, outs, sems):
        send_sems, recv_sems, local_sems = sems
        x, y, c = _place()
        mine = 2 * x + y
        local, sends, recvs = [], [], []
        for a, (nm, l) in enumerate(slots):
            buf = outs[names.index(nm)]
            local.append(pltpu.make_async_copy(ins[a].at[mine], buf.at[mine, l], local_sems.at[a]))
            for j, chip in enumerate(_other_chips(x, y)):
                theirs = buf.at[2 * chip[0] + chip[1], l]
                sends.append(pltpu.make_async_remote_copy(src_ref=ins[a].at[2 * chip[0] + chip[1]], dst_ref=buf.at[mine, l], send_sem=send_sems.at[a, j],
                                                          recv_sem=recv_sems.at[a, j], device_id=(*chip, c), device_id_type=MESH))
                recvs.append(pltpu.make_async_remote_copy(src_ref=theirs, dst_ref=theirs, send_sem=send_sems.at[a, j],
                                                          recv_sem=recv_sems.at[a, j], device_id=(*chip, c), device_id_type=MESH))
        return local, sends, recvs

    def start(ins, outs, sems):
        local, sends, _ = copies(ins, outs, sems)
        for cp in local + sends:
            cp.start()

    def finish(ins, outs, sems):
        local, sends, recvs = copies(ins, outs, sems)
        for cp in recvs:
            cp.wait_recv()
        for cp in sends:
            cp.wait_send()
        for cp in local:
            cp.wait()

    comm = _Comm([*parts, *[stacked[nm] for nm in kept]], [shapes[nm] for nm in names], aliases,
                 [pltpu.SemaphoreType.DMA((n, 3)), pltpu.SemaphoreType.DMA((n, 3)), pltpu.SemaphoreType.DMA((n,))], start, finish)
    return comm, names


def _matmul(name, a, b, extras, *, grid, a_spec, b_spec, extra_specs, out_shapes, out_specs, dims, k_axis=None, nk=1,
            acc_shape=None, epilogue=None, comm=None, n_sum=0, write=None):
    n_extra = len(extras)
    n_out = len(out_shapes)

    def body(*refs):
        a_ref, b_ref = refs[0], refs[1]
        ex = refs[2:2 + n_extra]
        outs = refs[2 + n_extra:2 + n_extra + n_out]
        prod = lax.dot_general(a_ref[...], b_ref[...], dims, preferred_element_type=F32)

        def finish(acc):
            if write is not None:
                write(outs, acc, *[e[...] for e in ex])
                return
            res = epilogue(acc, *[e[...] for e in ex]) if epilogue is not None else (acc,)
            first = None
            for d in range(len(grid)):
                if d != k_axis:
                    here = pl.program_id(d) == 0
                    first = here if first is None else jnp.logical_and(first, here)
            for idx, (o, r) in enumerate(zip(outs, res)):
                if idx < n_out - n_sum:
                    o[...] = r.astype(o.dtype)
                else:
                    @pl.when(first)
                    def _(o=o, r=r):
                        o[...] = r.astype(o.dtype)

                    @pl.when(jnp.logical_not(first))
                    def _(o=o, r=r):
                        o[...] += r.astype(o.dtype)

        if k_axis is None:
            finish(prod)
        else:
            acc_ref = refs[-1]
            k = pl.program_id(k_axis)

            @pl.when(k == 0)
            def _():
                acc_ref[...] = prod

            @pl.when(k > 0)
            def _():
                acc_ref[...] += prod

            @pl.when(k == nk - 1)
            def _():
                finish(acc_ref[...])

    scratch = [] if k_axis is None else [pltpu.VMEM(acc_shape, F32)]
    return _call(name, body, [a, b, *extras], [a_spec, b_spec, *extra_specs], list(out_shapes), list(out_specs), scratch, grid, comm)


def _rowwise(name, fn, operands, *, grid, in_specs, out_shapes, out_specs, n_acc=0, grid_spec_prefetch=None, comm=None):
    n_in = len(operands)
    n_out = len(out_shapes)
    n_pre = 0 if grid_spec_prefetch is None else 1

    def body(*refs):
        refs = refs[n_pre:]
        ins = refs[:n_in]
        outs = refs[n_in:n_in + n_out]
        res = fn(*[r[...] for r in ins])
        if not isinstance(res, (tuple, list)):
            res = (res,)
        first = pl.program_id(0) == 0
        for d in range(1, len(grid)):
            first = jnp.logical_and(first, pl.program_id(d) == 0)
        for idx, (o, r) in enumerate(zip(outs, res)):
            if idx < n_out - n_acc:
                o[...] = r.astype(o.dtype)
            else:
                @pl.when(first)
                def _(o=o, r=r):
                    o[...] = r.astype(o.dtype)

                @pl.when(jnp.logical_not(first))
                def _(o=o, r=r):
                    o[...] += r.astype(o.dtype)

    if comm is not None:
        return _call(name, body, list(operands), list(in_specs), list(out_shapes), list(out_specs), [], grid, comm)
    if grid_spec_prefetch is None:
        return _pcall(body, name=name, grid=grid, in_specs=in_specs, out_specs=out_specs, out_shape=out_shapes,
                      compiler_params=_params(len(grid)))(*operands)
    gs = pltpu.PrefetchScalarGridSpec(num_scalar_prefetch=1, grid=grid, in_specs=in_specs, out_specs=out_specs)
    return _pcall(body, name=name, grid_spec=gs, out_shape=out_shapes,
                  compiler_params=_params(len(grid)))(grid_spec_prefetch, *operands)


def _row_spec(tm, w):
    return pl.BlockSpec((tm, w), lambda i: (i, 0))


def _const_spec(shape):
    nd = len(shape)
    return pl.BlockSpec(tuple(shape), lambda *_: (0,) * nd)


def _rms_fwd(x, g):
    r = lax.rsqrt(jnp.mean(x * x, axis=-1, keepdims=True) + NORM_EPS)
    return x * r * g


def _rms_bwd(x, g, dy):
    r = lax.rsqrt(jnp.mean(x * x, axis=-1, keepdims=True) + NORM_EPS)
    xh = x * r
    u = dy * g
    dx = r * (u - xh * jnp.mean(u * xh, axis=-1, keepdims=True))
    dg = jnp.sum(dy * xh, axis=0, keepdims=True)
    return dx, dg


def _gelu_and_grad(z):
    cdf = 0.5 * (1.0 + lax.erf(z * (2.0 ** -0.5)))
    return cdf + z * jnp.exp(-0.5 * z * z) * ((2.0 * math.pi) ** -0.5), z * cdf


def _rope_fwd(x, cc, sa, sb):
    return x * cc + pltpu.roll(x, 96, 1) * sa + pltpu.roll(x, 32, 1) * sb


def _rope_bwd(d, cc, sa, sb):
    return d * cc + pltpu.roll(d * sa, 32, 1) + pltpu.roll(d * sb, 96, 1)


def _adam(w, g, m, v):
    m = ADAM_B1 * m + (1.0 - ADAM_B1) * g
    v = ADAM_B2 * v + (1.0 - ADAM_B2) * (g * g)
    m_hat = m / (1.0 - ADAM_B1 ** ADAM_STEP)
    v_hat = v / (1.0 - ADAM_B2 ** ADAM_STEP)
    delta = -ADAM_LR * (m_hat / (jnp.sqrt(v_hat) + ADAM_EPS) + ADAM_WD * w)
    return delta, m, v


def _flash_fwd(q, k, vt, tq, comm=None):
    h, t = vt.shape[0], q.shape[0]
    nq = t // tq

    chunk_blocks = [c for c in (4, 2, 1) if c < nq]

    def body(q_ref, k_ref, vt_ref, o_ref, lse_ref, m_ref, l_ref, acc_ref):
        qi = pl.program_id(1)
        m_ref[...] = jnp.full((1, tq), NEG, F32)
        l_ref[...] = jnp.zeros((1, tq), F32)
        acc_ref[...] = jnp.zeros((VDIM, tq), F32)

        def update(kb0, nblk, masked):
            kb = k_ref[pl.ds(pl.multiple_of(kb0 * tq, tq), nblk * tq), :]
            st = lax.dot_general(kb, q_ref[...], NT, preferred_element_type=F32)
            if masked:
                key = lax.broadcasted_iota(jnp.int32, (tq, tq), 0)
                qry = lax.broadcasted_iota(jnp.int32, (tq, tq), 1)
                st = jnp.where(key <= qry, st, NEG)
            m_old = m_ref[...]
            m_new = jnp.maximum(m_old, jnp.max(st, axis=0, keepdims=True))
            alpha = jnp.exp2((m_old - m_new) * EXP2_SCALE)
            pt = jnp.exp2((st - m_new) * EXP2_SCALE)
            l_ref[...] = alpha * l_ref[...] + jnp.sum(pt, axis=0, keepdims=True)
            ptb = pt.astype(BF16)
            pv = lax.dot_general(vt_ref[kb0], ptb[:tq], NN, preferred_element_type=F32)
            for j in range(1, nblk):
                pv += lax.dot_general(vt_ref[kb0 + j], ptb[j * tq:(j + 1) * tq], NN, preferred_element_type=F32)
            acc_ref[...] = alpha * acc_ref[...] + pv
            m_ref[...] = m_new

        start = jnp.int32(0)
        for c in chunk_blocks:
            take = (qi & c) != 0

            @pl.when(take)
            def _(start=start, c=c):
                update(start, c, False)

            start = start + jnp.where(take, c, 0)
        update(qi, 1, True)
        l = l_ref[...]
        o_ref[...] = (acc_ref[...] / l).T.astype(o_ref.dtype)
        lse_ref[...] = m_ref[...] * EXP2_SCALE + jnp.log2(l)

    return _call(
        "flash_fwd", body, [q, k, vt],
        [pl.BlockSpec((tq, QPAD), lambda hh, i: (i, hh)),
         pl.BlockSpec((t, QPAD), lambda hh, i: (0, hh)),
         pl.BlockSpec((None, nq, VDIM, tq), lambda hh, i: (hh, 0, 0, 0))],
        [_sds((t, h * VDIM), BF16), _sds((h, nq, 1, tq), F32)],
        [pl.BlockSpec((tq, VDIM), lambda hh, i: (i, hh)),
         pl.BlockSpec((None, None, 1, tq), lambda hh, i: (hh, i, 0, 0))],
        [pltpu.VMEM((1, tq), F32), pltpu.VMEM((1, tq), F32), pltpu.VMEM((VDIM, tq), F32)], (h, nq), comm)


def _flash_bwd(q, k, v, o, do, lse, tabs, tq, comm=None):
    t = q.shape[0]
    h = q.shape[1] // QPAD
    nq = t // tq

    def body(q_ref, k_ref, v_ref, o_ref, do_ref, lse_ref, cc_ref, sa_ref, sb_ref, dq_ref, dk_ref, dv_ref, delta_ref, dqt_ref):
        kj = pl.program_id(1)

        @pl.when(kj == 0)
        def _():
            dqt_ref[...] = jnp.zeros_like(dqt_ref)
            ones = jnp.ones((8, VDIM), BF16)
            for qi in range(nq):
                rows = pl.ds(qi * tq, tq)
                prod = do_ref[rows, :].astype(F32) * o_ref[rows, :].astype(F32)
                hi = prod.astype(BF16)
                lo = (prod - hi.astype(F32)).astype(BF16)
                delta_ref[qi] = (lax.dot_general(ones, hi, NT, preferred_element_type=F32)
                                 + lax.dot_general(ones, lo, NT, preferred_element_type=F32))

        kb = k_ref[...]
        vb = v_ref[...]
        kbt = kb.astype(F32).T.astype(BF16)
        dk_ref[...] = jnp.zeros_like(dk_ref)
        dv_ref[...] = jnp.zeros_like(dv_ref)

        def step(q0, nblk, masked):
            rows = pl.ds(pl.multiple_of(q0 * tq, tq), nblk * tq)
            qb = q_ref[rows, :]
            dob = do_ref[rows, :]
            lse = jnp.concatenate([lse_ref[q0 + j] for j in range(nblk)], axis=1)
            delta = jnp.concatenate([delta_ref[q0 + j, pl.ds(0, 1), :] for j in range(nblk)], axis=1)
            st = lax.dot_general(kb, qb, NT, preferred_element_type=F32)
            pt = jnp.exp2(st * EXP2_SCALE - lse)
            if masked:
                key = lax.broadcasted_iota(jnp.int32, (tq, tq), 0)
                qry = lax.broadcasted_iota(jnp.int32, (tq, tq), 1)
                pt = jnp.where(key <= qry, pt, 0.0)
            dv_ref[...] += lax.dot_general(pt.astype(BF16), dob, NN, preferred_element_type=F32)
            dpt = lax.dot_general(vb, dob, NT, preferred_element_type=F32)
            dst = (pt * (dpt - delta) * ATTN_SCALE).astype(BF16)
            dk_ref[...] += lax.dot_general(dst, qb, NN, preferred_element_type=F32)
            dqt = lax.dot_general(kbt, dst, NN, preferred_element_type=F32)
            for j in range(nblk):
                dqt_ref[q0 + j] += dqt[:, j * tq:(j + 1) * tq]

        later = nq - 1 - kj
        step(kj, 1, True)
        start = kj + 1
        for c in [c for c in (1, 2, 4) if c < nq]:
            take = (later & c) != 0

            @pl.when(take)
            def _(start=start, c=c):
                step(start, c, False)

            start = start + jnp.where(take, c, 0)

        @pl.when(kj == nq - 1)
        def _():
            for qi in range(nq):
                rows = pl.ds(qi * tq, tq)
                d = dqt_ref[qi].T
                roped = _rope_bwd(d[:, NOPE:], cc_ref[rows, :], sa_ref[rows, :], sb_ref[rows, :])
                dq_ref[rows, :] = jnp.concatenate([d[:, :NOPE], roped], axis=1).astype(BF16)

    head_q = pl.BlockSpec((t, QPAD), lambda hh, j: (0, hh))
    head_v = pl.BlockSpec((t, VDIM), lambda hh, j: (0, hh))
    table = pl.BlockSpec((t, 128), lambda hh, j: (0, 0))
    return _call(
        "flash_bwd", body, [q, k, v, o, do, lse, *tabs],
        [head_q, pl.BlockSpec((tq, QPAD), lambda hh, j: (j, hh)), pl.BlockSpec((tq, VDIM), lambda hh, j: (j, hh)), head_v, head_v,
         pl.BlockSpec((None, nq, 1, tq), lambda hh, j: (hh, 0, 0, 0)), table, table, table],
        [_sds((t, h * QPAD), BF16), _sds((t, h * QPAD), F32), _sds((t, h * VDIM), F32)],
        [head_q, pl.BlockSpec((tq, QPAD), lambda hh, j: (j, hh)), pl.BlockSpec((tq, VDIM), lambda hh, j: (j, hh))],
        [pltpu.VMEM((nq, 8, tq), F32), pltpu.VMEM((nq, QPAD, tq), F32)], (h, nq), comm)


def _tril_bf16(w):
    row = lax.broadcasted_iota(jnp.int32, w.shape, 0)
    col = lax.broadcasted_iota(jnp.int32, w.shape, 1)
    return jnp.where(col <= row, w, 0.0).astype(BF16)


def _layer_norm_parts(v0):
    mu = jnp.mean(v0, axis=-1, keepdims=True)
    vc = v0 - mu
    rstd = lax.rsqrt(jnp.mean(vc * vc, axis=-1, keepdims=True) + LN_EPS)
    return vc * rstd, rstd


def _sgu_mid_fwd(ge, ln_g, ln_b, w_sp, b_sp, chunks_per_step):
    t, e2 = ge.shape
    e = e2 // 2
    gd = e // SGU_GROUPS
    rows = SGU_CHUNK * chunks_per_step

    def body(u_ref, v_ref, g_ref, b_ref, w_ref, bs_ref, gate_ref):
        for ck in range(chunks_per_step):
            r = pl.ds(ck * SGU_CHUNK, SGU_CHUNK)
            xh, _ = _layer_norm_parts(v_ref[r, :].astype(F32))
            v1 = (xh * g_ref[...] + b_ref[...]).astype(BF16)
            for g in range(SGU_GROUPS):
                cols = pl.ds(g * gd, gd)
                mixed = lax.dot_general(_tril_bf16(w_ref[g]), v1[:, g * gd:(g + 1) * gd], NN, preferred_element_type=F32) + bs_ref[g]
                gate_ref[r, cols] = (u_ref[r, cols].astype(F32) * mixed).astype(BF16)

    return _pcall(
        body, name="sgu_mid_fwd", grid=(t // rows,),
        in_specs=[pl.BlockSpec((rows, e), lambda i: (i, 0)), pl.BlockSpec((rows, e), lambda i: (i, 1)),
                  _const_spec((1, e)), _const_spec((1, e)), _const_spec(w_sp.shape), _const_spec(b_sp.shape)],
        out_specs=pl.BlockSpec((rows, e), lambda i: (i, 0)),
        out_shape=_sds((t, e), BF16), compiler_params=_params(1),
    )(ge, ge, ln_g, ln_b, w_sp, b_sp)


def _sgu_mid_bwd(ge, gp, dgate, ln_g, ln_b, w_sp, b_sp, chunks_per_step):
    t, e2 = ge.shape
    e = e2 // 2
    gd = e // SGU_GROUPS
    rows = SGU_CHUNK * chunks_per_step

    def body(u_ref, v_ref, zu_ref, zv_ref, dg_ref, g_ref, b_ref, w_ref, bs_ref, dz_ref, dw_ref, dbs_ref, dlg_ref, dlb_ref):
        @pl.when(pl.program_id(0) == 0)
        def _():
            dw_ref[...] = jnp.zeros_like(dw_ref)
            dbs_ref[...] = jnp.zeros_like(dbs_ref)
            dlg_ref[...] = jnp.zeros_like(dlg_ref)
            dlb_ref[...] = jnp.zeros_like(dlb_ref)

        for ck in range(chunks_per_step):
            r = pl.ds(ck * SGU_CHUNK, SGU_CHUNK)
            xh, rstd = _layer_norm_parts(v_ref[r, :].astype(F32))
            v1 = (xh * g_ref[...] + b_ref[...]).astype(BF16)
            dv1_parts = []
            for g in range(SGU_GROUPS):
                cols = pl.ds(g * gd, gd)
                wc = _tril_bf16(w_ref[g])
                v1g = v1[:, g * gd:(g + 1) * gd]
                mixed = lax.dot_general(wc, v1g, NN, preferred_element_type=F32) + bs_ref[g]
                dgate = dg_ref[r, cols].astype(F32)
                dmixed = dgate * u_ref[r, cols].astype(F32)
                du = dgate * mixed
                dz_ref[r, cols] = (du * zu_ref[r, cols].astype(F32)).astype(BF16)
                dbs_ref[g] += jnp.sum(dmixed, axis=1, keepdims=True)
                dmb = dmixed.astype(BF16)
                dwg = lax.dot_general(dmb, v1g, NT, preferred_element_type=F32)
                row = lax.broadcasted_iota(jnp.int32, dwg.shape, 0)
                col = lax.broadcasted_iota(jnp.int32, dwg.shape, 1)
                dw_ref[g] += jnp.where(col <= row, dwg, 0.0)
                dv1_parts.append(lax.dot_general(wc, dmb, TN, preferred_element_type=F32))
            dv1 = jnp.concatenate(dv1_parts, axis=1)
            dlg_ref[...] += jnp.sum(dv1 * xh, axis=0, keepdims=True)
            dlb_ref[...] += jnp.sum(dv1, axis=0, keepdims=True)
            dxh = dv1 * g_ref[...]
            dv0 = rstd * (dxh - jnp.mean(dxh, axis=-1, keepdims=True) - xh * jnp.mean(dxh * xh, axis=-1, keepdims=True))
            dz_ref[r, pl.ds(e, e)] = (dv0 * zv_ref[r, :].astype(F32)).astype(BF16)

    half0 = pl.BlockSpec((rows, e), lambda i: (i, 0))
    half1 = pl.BlockSpec((rows, e), lambda i: (i, 1))
    return _pcall(
        body, name="sgu_mid_bwd", grid=(t // rows,),
        in_specs=[half0, half1, half0, half1, half0, _const_spec((1, e)), _const_spec((1, e)), _const_spec(w_sp.shape), _const_spec(b_sp.shape)],
        out_specs=[pl.BlockSpec((rows, e2), lambda i: (i, 0)), _const_spec(w_sp.shape), _const_spec(b_sp.shape), _const_spec((1, e)), _const_spec((1, e))],
        out_shape=[_sds((t, e2), BF16), _sds(w_sp.shape, F32), _sds(b_sp.shape, F32), _sds((1, e), F32), _sds((1, e), F32)],
        compiler_params=_params(1),
    )(ge, ge, gp, gp, dgate, ln_g, ln_b, w_sp, b_sp)


def kernel(x, positions, norm_mix, norm_ffn, final_norm, mla_w_dkv, mla_q_norm, mla_kv_norm, mla_w_uq, mla_w_ukv, mla_w_o, sgu_w_in, sgu_ln_g, sgu_ln_b, sgu_w_spatial, sgu_b_spatial, sgu_w_out, ffn_w_up, ffn_w_down, loss_target, m_norm_mix, m_norm_ffn, m_final_norm, m_mla_w_dkv, m_mla_q_norm, m_mla_kv_norm, m_mla_w_uq, m_mla_w_ukv, m_mla_w_o, m_sgu_w_in, m_sgu_ln_g, m_sgu_ln_b, m_sgu_w_spatial, m_sgu_b_spatial, m_sgu_w_out, m_ffn_w_up, m_ffn_w_down, v_norm_mix, v_norm_ffn, v_final_norm, v_mla_w_dkv, v_mla_q_norm, v_mla_kv_norm, v_mla_w_uq, v_mla_w_ukv, v_mla_w_o, v_sgu_w_in, v_sgu_ln_g, v_sgu_ln_b, v_sgu_w_spatial, v_sgu_b_spatial, v_sgu_w_out, v_ffn_w_up, v_ffn_w_down):
    _, T, D = x.shape
    depth = norm_mix.shape[0]
    n_mla, n_sgu = mla_w_dkv.shape[0], sgu_w_in.shape[0]
    assert depth % 2 == 0
    FF = ffn_w_up.shape[2] * N_DEV
    E = sgu_w_out.shape[1] * N_DEV
    ffc, ec, e2c = FF // N_DEV, E // N_DEV, 2 * E // N_DEV
    dc = D // N_DEV
    OW = HEADS * VDIM
    HW = HEADS * QPAD
    owc = OW // N_DEV
    tm = _tile(T, 1024)
    tb = _tile(T, 4096)
    tk = _tile(T, 512)
    tq = _tile(T, 512)
    ts = _tile(T, 256)
    nt = T // tm
    x2 = x.reshape(T, D)
    tgt = loss_target.reshape(T, D)
    cidx = lax.axis_index("c").astype(jnp.int32).reshape(1)

    ln_local = jnp.concatenate([sgu_ln_g, sgu_ln_b, jnp.zeros((8 - 2 * n_sgu, ec), F32)], axis=0)
    mla_sh = [[w[l].astype(BF16) for w in (mla_w_dkv, mla_w_uq, mla_w_ukv, mla_w_o)] for l in range(n_mla)]

    def mla_layouts(g_dkv, g_uq, g_ukv, g_o):
        w_dkv = jnp.pad(g_dkv.reshape(1, D, LAT), ((0, 0), (0, 0), (0, LAT_PAD - LAT)))
        w_uq = jnp.pad(g_uq, ((0, 0), (0, 0), (0, QPAD - NOPE - ROPE))).transpose(1, 0, 2).reshape(1, Q_RANK, HEADS * QPAD)
        w_ukv = g_ukv.transpose(1, 0, 2).reshape(1, KV_RANK, HEADS * (NOPE + VDIM))
        return w_dkv, w_uq, w_ukv, g_o.reshape(1, HEADS * VDIM, D)

    mla_w = [None] * n_mla
    mla_w[0] = mla_layouts(*_all_gather("gather_first_weights", mla_sh[0]))
    small_later = [a for l in range(1, n_mla) for a in mla_sh[l]] + [ln_local]
    ln_g_full, ln_b_full = [None] * n_sgu, [None] * n_sgu
    b_sp = sgu_b_spatial.reshape(n_sgu, SGU_GROUPS, SGU_CHUNK, 1)
    up_sh = [ffn_w_up[i].astype(BF16) for i in range(depth)]
    down_sh = [ffn_w_down[i].astype(BF16) for i in range(depth)]
    in_sh = [sgu_w_in[l].astype(BF16) for l in range(n_sgu)]
    out_sh = [sgu_w_out[l].astype(BF16) for l in range(n_sgu)]
    g_up, g_down, g_in, g_out = [None] * depth, [None] * depth, [None] * n_sgu, [None] * n_sgu

    inv_freq = ROPE_THETA ** (-jnp.arange(0, ROPE, 2, dtype=F32) / ROPE)
    zeros32 = jnp.zeros((ROPE // 2,), F32)
    inv128 = jnp.concatenate([inv_freq, inv_freq, zeros32, zeros32]).reshape(1, 128)
    sel_a = jnp.concatenate([-jnp.ones((32,), F32), zeros32, zeros32, zeros32]).reshape(1, 128)
    sel_b = jnp.concatenate([zeros32, jnp.ones((32,), F32), zeros32, zeros32]).reshape(1, 128)
    sel_c = jnp.concatenate([jnp.ones((64,), F32), zeros32, zeros32]).reshape(1, 128)

    def rope_tables(pos, inv, sa, sb, sc):
        ang = pos.astype(F32) * inv
        cs, sn = jnp.cos(ang), jnp.sin(ang)
        return cs * sc, sn * sa, sn * sb

    t_cc, t_sa, t_sb = _rowwise(
        "rope_tables", rope_tables, [positions.reshape(T, 1), inv128, sel_a, sel_b, sel_c], grid=(nt,),
        in_specs=[_row_spec(tm, 1)] + [_const_spec((1, 128))] * 4,
        out_shapes=[_sds((T, 128), F32)] * 3, out_specs=[_row_spec(tm, 128)] * 3)
    tab_specs = [_row_spec(tm, 128)] * 3

    def rmsnorm(xv, g):
        return _rowwise("rmsnorm", lambda a, gg: _rms_fwd(a, gg), [xv, g.reshape(1, D)], grid=(nt,),
                        in_specs=[_row_spec(tm, D), _const_spec((1, D))], out_shapes=_sds((T, D), BF16), out_specs=_row_spec(tm, D))

    def proj_cols(name, h, gw, nc, epilogue, n_out, comm=None):
        return _matmul(name, h, gw, [], grid=(N_DEV, T // tb),
                       a_spec=pl.BlockSpec((tb, D), lambda j, i: (i, 0)),
                       b_spec=pl.BlockSpec((None, D, nc), lambda j, i: (j, 0, 0)), extra_specs=[],
                       out_shapes=[_sds((T, nc * N_DEV), BF16)] * n_out, out_specs=[pl.BlockSpec((tb, nc), lambda j, i: (i, j))] * n_out,
                       dims=NN, epilogue=epilogue, comm=comm)

    def residual_norm(acc, xr, g):
        xn = acc + xr
        return xn, _rms_fwd(xn, g)

    def proj_rows_residual(name, a, gw, xres, g_next, comm=None):
        kk_ = a.shape[1]
        return _matmul(name, a, gw.reshape(kk_, D), [xres, g_next.reshape(1, D)], grid=(T // tk,),
                       a_spec=_row_spec(tk, kk_), b_spec=_const_spec((kk_, D)), extra_specs=[_row_spec(tk, D), _const_spec((1, D))],
                       out_shapes=[_sds((T, D), F32), _sds((T, D), BF16)], out_specs=[_row_spec(tk, D)] * 2,
                       dims=NN, epilogue=residual_norm, comm=comm)

    def back_rows(name, dy, gw, kc, extras, epilogue, comm=None):
        return _matmul(name, dy, gw, extras, grid=(N_DEV, T // tb),
                       a_spec=pl.BlockSpec((tb, D), lambda j, i: (i, 0)),
                       b_spec=pl.BlockSpec((None, kc, D), lambda j, i: (j, 0, 0)),
                       extra_specs=[pl.BlockSpec((tb, kc), lambda j, i: (i, j))] * len(extras),
                       out_shapes=[_sds((T, kc * N_DEV), BF16)], out_specs=[pl.BlockSpec((tb, kc), lambda j, i: (i, j))],
                       dims=NT, epilogue=epilogue, comm=comm)

    def norm_bwd_epilogue(dh, xv, g, dxi):
        dxn, dg = _rms_bwd(xv, g, dh)
        return dxi + dxn, dxi + dxn, dg

    def transposed(gw):
        return gw.transpose(0, 2, 1).reshape(gw.shape[0] * gw.shape[2], D)

    def back_cols(name, da, gwt, xv, g, dx_in, comm=None):
        n = da.shape[1]
        row = _row_spec(tk, D)
        return _matmul(name, da, gwt, [xv, g.reshape(1, D), dx_in], grid=(T // tk,),
                       a_spec=_row_spec(tk, n), b_spec=_const_spec((n, D)), extra_specs=[row, _const_spec((1, D)), row],
                       out_shapes=[_sds((T, D), F32), _sds((T, D), BF16), _sds((1, D), F32)], out_specs=[row, row, _const_spec((1, D))],
                       dims=NN, epilogue=norm_bwd_epilogue, n_sum=1, comm=comm)

    def token_sum(tt):
        return dict(k_axis=1, nk=T // tt) if T // tt > 1 else dict(k_axis=None)

    def wgrad_cols(name, h, da, nc):
        return _matmul(name, h, da, [], grid=(N_DEV, T // tb),
                       a_spec=pl.BlockSpec((tb, D), lambda j, t: (t, 0)), b_spec=pl.BlockSpec((tb, nc), lambda j, t: (t, j)),
                       extra_specs=[], out_shapes=[_sds((N_DEV, D, nc), BF16)],
                       out_specs=[pl.BlockSpec((None, D, nc), lambda j, t: (j, 0, 0))],
                       dims=TN, acc_shape=(D, nc), **token_sum(tb))[0]

    def wgrad_rows(name, a, dy, kc, ncols, tt):
        return _matmul(name, a, dy, [], grid=(a.shape[1] // kc, T // tt),
                       a_spec=pl.BlockSpec((tt, kc), lambda j, t: (t, j)), b_spec=pl.BlockSpec((tt, ncols), lambda j, t: (t, 0)),
                       extra_specs=[], out_shapes=[_sds((a.shape[1], ncols), BF16)],
                       out_specs=[pl.BlockSpec((kc, ncols), lambda j, t: (j, 0))],
                       dims=TN, acc_shape=(kc, ncols), **token_sum(tt))[0]

    saved = []
    xs = x2
    for i in range(depth):
        l = i // 2
        if i == 0:
            h = rmsnorm(xs, norm_mix[0])
        if i % 2 == 0:
            w_dkv, w_uq, w_ukv, w_o = mla_w[l]
            lat = _matmul("mla_down", h, w_dkv, [], grid=(nt,), a_spec=_row_spec(tm, D),
                          b_spec=pl.BlockSpec((None, D, LAT_PAD), lambda i_: (0, 0, 0)), extra_specs=[],
                          out_shapes=[_sds((T, LAT_PAD), F32)], out_specs=[_row_spec(tm, LAT_PAD)], dims=NN)[0]

            def latent_post(la, qn, kvn, cc, sa, sb):
                cq = _rms_fwd(la[:, :Q_RANK], qn)
                ckv = _rms_fwd(la[:, Q_RANK:Q_RANK + KV_RANK], kvn)
                kr = _rope_fwd(la[:, Q_RANK + KV_RANK:], cc, sa, sb)
                return cq, ckv, kr

            cq, ckv, kr = _rowwise(
                "mla_latent", latent_post, [lat, mla_q_norm[l].reshape(1, Q_RANK), mla_kv_norm[l].reshape(1, KV_RANK), t_cc, t_sa, t_sb],
                grid=(nt,), in_specs=[_row_spec(tm, LAT_PAD), _const_spec((1, Q_RANK)), _const_spec((1, KV_RANK))] + tab_specs,
                out_shapes=[_sds((T, Q_RANK), BF16), _sds((T, KV_RANK), BF16), _sds((T, 128), BF16)],
                out_specs=[_row_spec(tm, Q_RANK), _row_spec(tm, KV_RANK), _row_spec(tm, 128)])

            def q_epilogue(acc, cc, sa, sb):
                parts = []
                for b in range(HEADS):
                    parts += [acc[:, b * QPAD:b * QPAD + NOPE], _rope_fwd(acc[:, b * QPAD + NOPE:(b + 1) * QPAD], cc, sa, sb)]
                return (jnp.concatenate(parts, axis=1),)

            q = _matmul("mla_q", cq, w_uq, [t_cc, t_sa, t_sb], grid=(nt,), a_spec=_row_spec(tm, Q_RANK),
                        b_spec=pl.BlockSpec((None, Q_RANK, HW), lambda i_: (0, 0, 0)), extra_specs=tab_specs,
                        out_shapes=[_sds((T, HW), BF16)], out_specs=[_row_spec(tm, HW)], dims=NN, epilogue=q_epilogue)[0]

            def kv_write(outs, acc, krb):
                k_ref, v_ref, vt_ref = outs
                for b in range(HEADS):
                    vb = acc[:, b * QPAD + NOPE:(b + 1) * QPAD]
                    k_ref[:, b * QPAD:b * QPAD + NOPE] = acc[:, b * QPAD:b * QPAD + NOPE].astype(BF16)
                    k_ref[:, b * QPAD + NOPE:(b + 1) * QPAD] = krb
                    v_ref[:, b * VDIM:(b + 1) * VDIM] = vb.astype(BF16)
                    vbt = vb.T.astype(BF16)
                    for u in range(tm // tq):
                        vt_ref[b, u] = vbt[:, u * tq:(u + 1) * tq]

            kk, vv, vt = _matmul("mla_kv", ckv, w_ukv, [kr], grid=(nt,), a_spec=_row_spec(tm, KV_RANK),
                                 b_spec=pl.BlockSpec((None, KV_RANK, HW), lambda i_: (0, 0, 0)), extra_specs=[_row_spec(tm, 128)],
                                 out_shapes=[_sds((T, HW), BF16), _sds((T, OW), BF16), _sds((HEADS, T // tq, VDIM, tq), BF16)],
                                 out_specs=[_row_spec(tm, HW), _row_spec(tm, OW), pl.BlockSpec((HEADS, tm // tq, VDIM, tq), lambda i_: (0, i_, 0, 0))],
                                 dims=NN, write=kv_write)
            group = [up_sh[i], down_sh[i], in_sh[l], out_sh[l]] + (small_later if i == 0 else [])
            o, lse, *bufs = _flash_fwd(q, kk, vt, tq, comm=_gather_level1(group))
            xm, h2, g_up[i], g_down[i] = _matmul(
                "mla_out", o, w_o, [xs, norm_ffn[i].reshape(1, D)], grid=(nt,), a_spec=_row_spec(tm, OW),
                b_spec=pl.BlockSpec((None, OW, D), lambda i_: (0, 0, 0)), extra_specs=[_row_spec(tm, D), _const_spec((1, D))],
                out_shapes=[_sds((T, D), F32), _sds((T, D), BF16)], out_specs=[_row_spec(tm, D)] * 2, dims=NN,
                epilogue=residual_norm, comm=_gather_level2(bufs[:2]))
            half_gathered = bufs[2:]
            mix_saved = (h, lat, cq, ckv, q, kk, vv, o, lse)
        else:
            gp, ge, g_up[i], g_down[i] = proj_cols("sgu_in", h, g_in[l], e2c, _gelu_and_grad, 2, comm=_gather_level2(next_mlp))
            gate = _sgu_mid_fwd(ge, ln_g_full[l], ln_b_full[l], sgu_w_spatial[l], b_sp[l], 4)
            xm, h2 = proj_rows_residual("sgu_out", gate, g_out[l], xs, norm_ffn[i])
            mix_saved = (h, gp, ge, gate)
        r, s, *rest = proj_cols("ffn_up", h2, g_up[i], ffc, lambda acc: (jnp.maximum(acc, 0.0), jnp.square(jnp.maximum(acc, 0.0))), 2,
                                comm=_merge_comm(_gather_level2(half_gathered), _gather_level1([up_sh[i + 1]])) if i % 2 == 0 else None)
        if i % 2 == 0:
            g_in[l], g_out[l], *small_gathered, next_up = rest
        if i == 0:
            for l_ in range(1, n_mla):
                mla_w[l_] = mla_layouts(*small_gathered[4 * (l_ - 1):4 * l_])
            g_ln = small_gathered[-1]
            ln_g_full = [g_ln[:, l_, :].reshape(1, E) for l_ in range(n_sgu)]
            ln_b_full = [g_ln[:, n_sgu + l_, :].reshape(1, E) for l_ in range(n_sgu)]
        xo, h_next, *rest = proj_rows_residual("ffn_down", s, g_down[i], xm, norm_mix[i + 1] if i + 1 < depth else final_norm,
                                               comm=_gather_level1([down_sh[i + 1]]) if i % 2 == 0 else None)
        if i % 2 == 0:
            next_mlp = [next_up, rest[0]]
        saved.append((xs, xm, mix_saved, h2, r, s))
        xs, h = xo, h_next

    def loss_head(xv, tg, g):
        y = _rms_fwd(xv, g)
        err = y - tg
        part = 0.5 * jnp.sum(jnp.sum(err * err, axis=-1, keepdims=True), axis=0, keepdims=True) / D
        dx, dg = _rms_bwd(xv, g, err / D)
        return dx, dx, jnp.broadcast_to(part, (1, 128)), dg

    dx, dyb, loss_part, d_final = _rowwise(
        "loss_head", loss_head, [xs, tgt, final_norm.reshape(1, D)], grid=(nt,),
        in_specs=[_row_spec(tm, D), _row_spec(tm, D), _const_spec((1, D))],
        out_shapes=[_sds((T, D), F32), _sds((T, D), BF16), _sds((1, 128), F32), _sds((1, D), F32)],
        out_specs=[_row_spec(tm, D), _row_spec(tm, D), _const_spec((1, 128)), _const_spec((1, D))], n_acc=2)
    loss = lax.psum(loss_part[0, 0], ("x", "y", "c"))

    d_norm_mix, d_norm_ffn = [None] * depth, [None] * depth
    d_qn, d_kvn = [None] * n_mla, [None] * n_mla
    d_wsp, d_bsp, d_lng, d_lnb = [None] * n_sgu, [None] * n_sgu, [None] * n_sgu, [None] * n_sgu
    layers = {"dkv": n_mla, "uq": n_mla, "ukv": n_mla, "o": n_mla, "in": n_sgu, "out": n_sgu, "up": depth, "down": depth}
    stacked = {nm: None for nm in layers}
    pending = []
    summed = []

    def add_pair(g, rcv):
        _, rws, cls = g.shape
        g4 = g.reshape(N_CHIP, 2, rws, cls)
        rt = _tile(rws, 1024)
        return _rowwise("grad_pair_sum", lambda a, b_: a.astype(F32) + b_.astype(F32), [g4, rcv], grid=(N_CHIP, rws // rt),
                        in_specs=[pl.BlockSpec((None, None, rt, cls), lambda ch, i_, cr: (ch, cr[0], i_, 0)),
                                  pl.BlockSpec((None, rt, cls), lambda ch, i_, cr: (ch, i_, 0))],
                        out_shapes=_sds(rcv.shape, BF16), out_specs=pl.BlockSpec((None, rt, cls), lambda ch, i_, cr: (ch, i_, 0)),
                        grid_spec_prefetch=cidx)

    def sibling_comm():
        return _sibling_exchange([g for _, _, g in pending]) if pending else None

    def absorb(from_sibling):
        for (nm, l_, g), rcv in zip(pending, from_sibling):
            summed.append((nm, l_, add_pair(g, rcv)))
        pending.clear()

    def chip_comm():
        if pending:
            absorb(_comm_call("grad_sibling_exchange", sibling_comm()))
        comm, names = _chip_exchange([p for _, _, p in summed], [(nm, l_) for nm, l_, _ in summed], layers, stacked)
        summed.clear()
        return comm, names

    def rows128(a, rows):
        flat = a.reshape(-1, 128)
        return jnp.pad(flat, ((0, rows - flat.shape[0]), (0, 0)))

    def pad_to(n, mult):
        return -(-n // mult) * mult

    def packed(arrs, sizes):
        return jnp.concatenate([rows128(a, sz) for a, sz in zip(arrs, sizes)], axis=0)

    n_wsp, n_bsp, n_ln = sgu_w_spatial.size // 128, pad_to(sgu_b_spatial.size // 128, 8), pad_to(n_sgu * E // 128, 8)
    early_sizes = [n_wsp, pad_to(n_wsp + n_bsp, SMALL_ROWS) - n_wsp, n_ln, n_ln]
    early_rep = early_sizes[0] + early_sizes[1]
    gathered_early = None

    for i in reversed(range(depth)):
        l = i // 2
        xs_i, xm, mix_saved, h2, r, s = saved[i]
        da, *rcv = back_rows("ffn_down_bwd", dyb, g_down[i], ffc, [r], lambda acc, rr: (acc * (2.0 * rr.astype(F32)),), comm=sibling_comm())
        absorb(rcv)
        pending.append(("down", i, wgrad_rows("ffn_down_wgrad", s, dyb, ffc, D, tb).reshape(N_DEV, ffc, D)))
        pending.append(("up", i, wgrad_cols("ffn_up_wgrad", h2, da, ffc)))
        dx, dyb, d_norm_ffn[i], *rcv = back_cols("ffn_up_bwd", da, transposed(g_up[i]), xm, norm_ffn[i], dx, comm=sibling_comm())
        absorb(rcv)
        if i % 2 == 0:
            h, lat, cq, ckv, q, kk, vv, o, lse = mix_saved
            w_dkv, w_uq, w_ukv, w_o = mla_w[l]
            do = _matmul("mla_out_bwd", dyb, w_o, [], grid=(nt,), a_spec=_row_spec(tm, D),
                         b_spec=pl.BlockSpec((None, OW, D), lambda i_: (0, 0, 0)), extra_specs=[],
                         out_shapes=[_sds((T, OW), BF16)], out_specs=[_row_spec(tm, OW)], dims=NT)[0]
            g_o_l = wgrad_rows("mla_out_wgrad", o, dyb, OW, D, tm).reshape(N_DEV, owc, D)
            comm, names = chip_comm()
            if i == 0:
                early = packed([jnp.stack(d_wsp, 0), jnp.stack(d_bsp, 0), jnp.concatenate(d_lng, 0), jnp.concatenate(d_lnb, 0)], early_sizes)
                comm = _merge_comm(comm, _gather_level1([early]))
            dq_pre, dk, dv, *bufs = _flash_bwd(q, kk, vv, o, do, lse, (t_cc, t_sa, t_sb), tq, comm=comm)
            stacked.update(dict(zip(names, bufs)))
            pending.append(("o", l, g_o_l))

            def kv_pre(dkb, dvb, cc, sa, sb):
                parts, dkr = [], None
                for b in range(HEADS):
                    parts += [dkb[:, b * QPAD:b * QPAD + NOPE], dvb[:, b * VDIM:(b + 1) * VDIM]]
                    piece = dkb[:, b * QPAD + NOPE:(b + 1) * QPAD]
                    dkr = piece if dkr is None else dkr + piece
                return jnp.concatenate(parts, axis=1), _rope_bwd(dkr, cc, sa, sb)

            dkv, dkr, *rest = _rowwise("mla_dkv_rope", kv_pre, [dk, dv, t_cc, t_sa, t_sb], grid=(T // ts,),
                                       in_specs=[_row_spec(ts, HW), _row_spec(ts, OW)] + [_row_spec(ts, 128)] * 3,
                                       out_shapes=[_sds((T, HW), BF16), _sds((T, 128), F32)], out_specs=[_row_spec(ts, HW), _row_spec(ts, 128)],
                                       comm=_gather_level2(bufs[len(names):]) if i == 0 else None)
            if i == 0:
                (gathered_early,) = rest
            g_uq_l = wgrad_rows("mla_q_wgrad", cq, dq_pre, Q_RANK, HW, tm)
            g_ukv_l = wgrad_rows("mla_kv_wgrad", ckv, dkv, KV_RANK, HW, tm)
            pending.append(("uq", l, g_uq_l.reshape(Q_RANK, HEADS, QPAD)[:, :, :NOPE + ROPE].transpose(1, 0, 2)))
            pending.append(("ukv", l, g_ukv_l.reshape(KV_RANK, HEADS, NOPE + VDIM).transpose(1, 0, 2)))
            dcq = _matmul("mla_q_bwd", dq_pre, w_uq, [], grid=(nt,), a_spec=_row_spec(tm, HW),
                          b_spec=pl.BlockSpec((None, Q_RANK, HW), lambda i_: (0, 0, 0)), extra_specs=[],
                          out_shapes=[_sds((T, Q_RANK), F32)], out_specs=[_row_spec(tm, Q_RANK)], dims=NT)[0]
            dckv = _matmul("mla_kv_bwd", dkv, w_ukv, [], grid=(nt,), a_spec=_row_spec(tm, HW),
                           b_spec=pl.BlockSpec((None, KV_RANK, HW), lambda i_: (0, 0, 0)), extra_specs=[],
                           out_shapes=[_sds((T, KV_RANK), F32)], out_specs=[_row_spec(tm, KV_RANK)], dims=NT)[0]

            def latent_bwd(la, qn, kvn, dq_, dkv_, dkr_):
                dcq_raw, dqn = _rms_bwd(la[:, :Q_RANK], qn, dq_)
                dckv_raw, dkvn = _rms_bwd(la[:, Q_RANK:Q_RANK + KV_RANK], kvn, dkv_)
                return jnp.concatenate([dcq_raw, dckv_raw, dkr_], axis=1), dqn, dkvn

            dlat, d_qn[l], d_kvn[l] = _rowwise(
                "mla_latent_bwd", latent_bwd, [lat, mla_q_norm[l].reshape(1, Q_RANK), mla_kv_norm[l].reshape(1, KV_RANK), dcq, dckv, dkr],
                grid=(nt,), in_specs=[_row_spec(tm, LAT_PAD), _const_spec((1, Q_RANK)), _const_spec((1, KV_RANK)),
                                      _row_spec(tm, Q_RANK), _row_spec(tm, KV_RANK), _row_spec(tm, 128)],
                out_shapes=[_sds((T, LAT_PAD), BF16), _sds((1, Q_RANK), F32), _sds((1, KV_RANK), F32)],
                out_specs=[_row_spec(tm, LAT_PAD), _const_spec((1, Q_RANK)), _const_spec((1, KV_RANK))], n_acc=2)
            g_dkv_l = wgrad_rows("mla_down_wgrad", h, dlat, D, LAT_PAD, tm)
            pending.append(("dkv", l, g_dkv_l[:, :LAT].reshape(N_DEV, dc, LAT)))
            dx, dyb, d_norm_mix[i] = _matmul(
                "mla_down_bwd", dlat, w_dkv, [xs_i, norm_mix[i].reshape(1, D), dx], grid=(nt,), a_spec=_row_spec(tm, LAT_PAD),
                b_spec=pl.BlockSpec((None, D, LAT_PAD), lambda i_: (0, 0, 0)), extra_specs=[_row_spec(tm, D), _const_spec((1, D)), _row_spec(tm, D)],
                out_shapes=[_sds((T, D), F32), _sds((T, D), BF16), _sds((1, D), F32)],
                out_specs=[_row_spec(tm, D), _row_spec(tm, D), _const_spec((1, D))], dims=NT, epilogue=norm_bwd_epilogue, n_sum=1)
        else:
            h, gp, ge, gate = mix_saved
            (dgate,) = back_rows("sgu_out_bwd", dyb, g_out[l], ec, [], None)
            pending.append(("out", l, wgrad_rows("sgu_out_wgrad", gate, dyb, ec, D, tb).reshape(N_DEV, ec, D)))
            dz, d_wsp[l], d_bsp[l], d_lng[l], d_lnb[l] = _sgu_mid_bwd(ge, gp, dgate, ln_g_full[l], ln_b_full[l], sgu_w_spatial[l], b_sp[l], 2)
            pending.append(("in", l, wgrad_cols("sgu_in_wgrad", h, dz, e2c)))
            dx, dyb, d_norm_mix[i], *rcv = back_cols("sgu_in_bwd", dz, transposed(g_in[l]), xs_i, norm_mix[i], dx, comm=sibling_comm())
            absorb(rcv)
    grad_x = dx.reshape(1, T, D)

    last_comm, last_names = chip_comm()
    late_g = [jnp.concatenate(d_norm_mix, 0), jnp.concatenate(d_norm_ffn, 0), d_final, jnp.concatenate(d_qn, 0), jnp.concatenate(d_kvn, 0)]
    late_w = [norm_mix, norm_ffn, final_norm, mla_q_norm, mla_kv_norm]
    late_m = [m_norm_mix, m_norm_ffn, m_final_norm, m_mla_q_norm, m_mla_kv_norm]
    late_v = [v_norm_mix, v_norm_ffn, v_final_norm, v_mla_q_norm, v_mla_kv_norm]
    late_sizes = [pad_to(g.size // 128, 8) for g in late_g]
    late_rows = sum(late_sizes)

    def adam_big(parts, w, m, v):
        lyr, rws, cls = w.shape
        rt = _tile(rws, 256)

        def fn(p, w_, m_, v_):
            g = (p[0].astype(F32) + p[1].astype(F32)) + (p[2].astype(F32) + p[3].astype(F32))
            return (g, *_adam(w_, g, m_, v_))

        spec = pl.BlockSpec((None, rt, cls), lambda l_, i_: (l_, i_, 0))
        return _rowwise("adam_large", fn, [parts, w, m, v], grid=(lyr, rws // rt),
                        in_specs=[pl.BlockSpec((N_CHIP, None, rt, cls), lambda l_, i_: (0, l_, i_, 0)), spec, spec, spec],
                        out_shapes=[_sds(w.shape, F32)] * 4, out_specs=[spec] * 4)

    stacked.update(dict(zip(last_names, _comm_call("grad_chip_exchange", last_comm))))
    (gathered_late,) = _all_gather("gather_small_grads", [packed(late_g, late_sizes)])
    big = {}
    big["in"] = adam_big(stacked["in"], sgu_w_in, m_sgu_w_in, v_sgu_w_in)
    big["up"] = adam_big(stacked["up"], ffn_w_up, m_ffn_w_up, v_ffn_w_up)
    big["down"] = adam_big(stacked["down"], ffn_w_down, m_ffn_w_down, v_ffn_w_down)
    big["out"] = adam_big(stacked["out"], sgu_w_out, m_sgu_w_out, v_sgu_w_out)
    big["dkv"] = adam_big(stacked["dkv"], mla_w_dkv, m_mla_w_dkv, v_mla_w_dkv)
    big["uq"] = adam_big(stacked["uq"], mla_w_uq, m_mla_w_uq, v_mla_w_uq)
    big["ukv"] = adam_big(stacked["ukv"], mla_w_ukv, m_mla_w_ukv, v_mla_w_ukv)
    big["o"] = adam_big(stacked["o"], mla_w_o, m_mla_w_o, v_mla_w_o)
    big_res = [big[nm][:4] for nm in ("dkv", "uq", "ukv", "o", "in", "out", "up", "down")]

    def sum8(p):
        return ((p[0] + p[1]) + (p[2] + p[3])) + ((p[4] + p[5]) + (p[6] + p[7]))

    def adam_packed(name, gathered, ws, ms, vs, sizes, rows, tile):
        spec = _row_spec(tile, 128)
        return _rowwise(name, lambda p, w_, m_, v_: (sum8(p), *_adam(w_, sum8(p), m_, v_)),
                        [gathered, packed(ws, sizes), packed(ms, sizes), packed(vs, sizes)], grid=(rows // tile,),
                        in_specs=[pl.BlockSpec((N_DEV, tile, 128), lambda i_: (0, i_, 0)), spec, spec, spec],
                        out_shapes=[_sds((rows, 128), F32)] * 4, out_specs=[spec] * 4)

    late_res = adam_packed("adam_small", gathered_late, late_w, late_m, late_v, late_sizes, late_rows, late_rows)
    early_res = adam_packed("adam_spatial", gathered_early, [sgu_w_spatial, sgu_b_spatial], [m_sgu_w_spatial, m_sgu_b_spatial],
                            [v_sgu_w_spatial, v_sgu_b_spatial], early_sizes[:2], early_rep, SMALL_ROWS)

    def unpack(res, sizes, k, like):
        off = sum(sizes[:k])
        return res[off:off + like.size // 128].reshape(like.shape)

    my_b = 4 * lax.axis_index("x") + 2 * lax.axis_index("y") + lax.axis_index("c")
    ln_w = jnp.concatenate([sgu_ln_g, sgu_ln_b], 0)
    ln_m = jnp.concatenate([m_sgu_ln_g, m_sgu_ln_b], 0)
    ln_v = jnp.concatenate([v_sgu_ln_g, v_sgu_ln_b], 0)
    ln_all = jnp.concatenate([gathered_early[:, early_rep:early_rep + n_sgu * E // 128], gathered_early[:, early_rep + n_ln:early_rep + n_ln + n_sgu * E // 128]], axis=1)
    ln_mine = lax.dynamic_slice_in_dim(ln_all.reshape(N_DEV, 2 * n_sgu, N_DEV, ec), my_b, 1, axis=2).reshape(N_DEV, 2 * n_sgu, ec)
    ln_g_, ln_d, ln_m2, ln_v2 = _rowwise(
        "adam_ln", lambda p, w_, m_, v_: (sum8(p), *_adam(w_, sum8(p), m_, v_)), [ln_mine, ln_w, ln_m, ln_v], grid=(1,),
        in_specs=[_const_spec(ln_mine.shape), _const_spec(ln_w.shape), _const_spec(ln_w.shape), _const_spec(ln_w.shape)],
        out_shapes=[_sds(ln_w.shape, F32)] * 4, out_specs=[_const_spec(ln_w.shape)] * 4)

    def family(pos):
        ln = [ln_g_, ln_d, ln_m2, ln_v2][pos]
        late = [unpack(late_res[pos], late_sizes, k, w_) for k, w_ in enumerate(late_w)]
        w_sp_, b_sp_ = unpack(early_res[pos], early_sizes, 0, sgu_w_spatial), unpack(early_res[pos], early_sizes, 1, sgu_b_spatial)
        bigs = [res[pos] for res in big_res]
        return [late[0], late[1], late[2], bigs[0], late[3], late[4], bigs[1], bigs[2], bigs[3],
                bigs[4], ln[:n_sgu], ln[n_sgu:], w_sp_, b_sp_, bigs[5], bigs[6], bigs[7]]

    return (loss, grad_x, *family(0), *family(1), *family(2), *family(3))
```

```python
import math

import jax
import jax.numpy as jnp
from jax import lax
from jax.experimental import pallas as pl
from jax.experimental.pallas import tpu as pltpu

F32 = jnp.float32
BF16 = jnp.bfloat16
MESH = pl.DeviceIdType.MESH

N_DEV = 8
N_CHIP = 4
HEADS = 8
NOPE = 128
ROPE = 64
VDIM = 128
QPAD = 256
Q_RANK = 256
KV_RANK = 128
LAT = Q_RANK + KV_RANK + ROPE
LAT_PAD = 512
ROPE_THETA = 10000.0
SGU_CHUNK = 128
SGU_GROUPS = 8
NORM_EPS = 1e-6
LN_EPS = 1e-5
ADAM_LR = 0.001
ADAM_B1 = 0.9
ADAM_B2 = 0.999
ADAM_EPS = 1e-08
ADAM_WD = 0.01
ADAM_STEP = 10
ATTN_SCALE = (NOPE + ROPE) ** -0.5
NEG = -1e30
EXP2_SCALE = ATTN_SCALE * math.log2(math.e)
VMEM_LIMIT = 56 * 1024 * 1024
SMALL_ROWS = 256

NN = (((1,), (0,)), ((), ()))
NT = (((1,), (1,)), ((), ()))
TN = (((0,), (0,)), ((), ()))
ANY = pl.BlockSpec(memory_space=pl.ANY)


def _pcall(body, **kw):
    return pl.pallas_call(body, **kw)


def _params(n_grid, side_effects=False):
    return pltpu.CompilerParams(dimension_semantics=("arbitrary",) * n_grid, vmem_limit_bytes=VMEM_LIMIT, has_side_effects=side_effects)


def _sds(shape, dtype):
    return jax.ShapeDtypeStruct(tuple(shape), dtype)


def _tile(n, want):
    t = min(n, want)
    assert n % t == 0, (n, want)
    return t


class _Comm:
    def __init__(self, operands, out_shapes, aliases, scratch, start, finish):
        self.operands, self.out_shapes, self.aliases, self.scratch = operands, out_shapes, aliases, scratch
        self.start, self.finish = start, finish


def _merge_comm(first, second):
    n_in, n_out, n_sc = len(first.operands), len(first.out_shapes), len(first.scratch)
    aliases = dict(first.aliases)
    aliases.update({n_in + k: n_out + v for k, v in second.aliases.items()})

    def start(ins, outs, sems):
        first.start(ins[:n_in], outs[:n_out], sems[:n_sc])
        second.start(ins[n_in:], outs[n_out:], sems[n_sc:])

    def finish(ins, outs, sems):
        first.finish(ins[:n_in], outs[:n_out], sems[:n_sc])
        second.finish(ins[n_in:], outs[n_out:], sems[n_sc:])

    return _Comm([*first.operands, *second.operands], [*first.out_shapes, *second.out_shapes], aliases,
                 [*first.scratch, *second.scratch], start, finish)


def _place():
    return lax.axis_index("x"), lax.axis_index("y"), lax.axis_index("c")


def _other_chips(x, y):
    return [(1 - x, y), (x, 1 - y), (1 - x, 1 - y)]


def _dev_index(dev):
    return 4 * dev[0] + 2 * dev[1] + dev[2]


def _comm_call(name, comm):
    c_in, c_out = len(comm.operands), len(comm.out_shapes)

    def body(*refs):
        ins, outs, sems = refs[:c_in], refs[c_in:c_in + c_out], refs[c_in + c_out:]
        comm.start(ins, outs, sems)
        comm.finish(ins, outs, sems)

    return _pcall(body, name=name, in_specs=[ANY] * c_in, out_specs=[ANY] * c_out, out_shape=comm.out_shapes,
                  scratch_shapes=comm.scratch, input_output_aliases=dict(comm.aliases),
                  compiler_params=pltpu.CompilerParams(has_side_effects=True))(*comm.operands)


def _call(name, body, operands, in_specs, out_shapes, out_specs, scratch, grid, comm=None):
    if comm is None:
        return _pcall(body, name=name, grid=grid, in_specs=in_specs, out_specs=out_specs, out_shape=out_shapes,
                      scratch_shapes=scratch, compiler_params=_params(len(grid)))(*operands)
    n_in, n_out, n_sc = len(operands), len(out_shapes), len(scratch)
    c_in, c_out = len(comm.operands), len(comm.out_shapes)

    def hosted(*refs):
        ins, cins = refs[:n_in], refs[n_in:n_in + c_in]
        o0 = n_in + c_in
        outs, couts = refs[o0:o0 + n_out], refs[o0 + n_out:o0 + n_out + c_out]
        rest = refs[o0 + n_out + c_out:]
        sc, csems = rest[:n_sc], rest[n_sc:]
        first = pl.program_id(0) == 0
        last = pl.program_id(0) == grid[0] - 1
        for d in range(1, len(grid)):
            first = jnp.logical_and(first, pl.program_id(d) == 0)
            last = jnp.logical_and(last, pl.program_id(d) == grid[d] - 1)

        @pl.when(first)
        def _():
            comm.start(cins, couts, csems)

        body(*ins, *outs, *sc)

        @pl.when(last)
        def _():
            comm.finish(cins, couts, csems)

    return _pcall(hosted, name=name, grid=grid, in_specs=[*in_specs, *[ANY] * c_in], out_specs=[*out_specs, *[ANY] * c_out],
                  out_shape=[*out_shapes, *comm.out_shapes], scratch_shapes=[*scratch, *comm.scratch],
                  input_output_aliases={n_in + k: n_out + v for k, v in comm.aliases.items()},
                  compiler_params=_params(len(grid), side_effects=True))(*operands, *comm.operands)


def _gather_level1(shards):
    n = len(shards)

    def copies(ins, outs, sems):
        send_sems, recv_sems, local_sems = sems
        x, y, c = _place()
        me, sibling = (x, y, c), (x, y, 1 - c)
        chips = _other_chips(x, y)

        def copy(a, k, block, to, src=None):
            slot = outs[a].at[_dev_index(block)]
            return pltpu.make_async_remote_copy(src_ref=slot if src is None else src, dst_ref=slot, send_sem=send_sems.at[a, k],
                                                recv_sem=recv_sems.at[a, k], device_id=to, device_id_type=MESH)

        mine = [pltpu.make_async_copy(ins[a], outs[a].at[_dev_index(me)], local_sems.at[a]) for a in range(n)]
        sends = [copy(a, 1 + j, me, (*chip, c), src=ins[a]) for j, chip in enumerate(chips) for a in range(n)]
        sends += [copy(a, 0, me, sibling, src=ins[a]) for a in range(n)]
        recvs = [copy(a, 1 + j, (*chip, c), me) for j, chip in enumerate(chips) for a in range(n)]
        recvs += [copy(a, 0, sibling, me) for a in range(n)]
        return mine, sends, recvs

    def start(ins, outs, sems):
        mine, sends, _ = copies(ins, outs, sems)
        for cp in mine + sends:
            cp.start()

    def finish(ins, outs, sems):
        mine, sends, recvs = copies(ins, outs, sems)
        for cp in recvs:
            cp.wait_recv()
        for cp in sends:
            cp.wait_send()
        for cp in mine:
            cp.wait()

    return _Comm(shards, [_sds((N_DEV, *a.shape), a.dtype) for a in shards], {},
                 [pltpu.SemaphoreType.DMA((n, 4)), pltpu.SemaphoreType.DMA((n, 4)), pltpu.SemaphoreType.DMA((n,))], start, finish)


def _gather_level2(bufs):
    n = len(bufs)

    def copies(outs, sems):
        send_sems, recv_sems = sems
        x, y, c = _place()
        sibling = (x, y, 1 - c)
        sends, recvs = [], []
        for j, chip in enumerate(_other_chips(x, y)):
            for a in range(n):
                have, want = outs[a].at[_dev_index((*chip, c))], outs[a].at[_dev_index((*chip, 1 - c))]
                sends.append(pltpu.make_async_remote_copy(src_ref=have, dst_ref=have, send_sem=send_sems.at[a, j], recv_sem=recv_sems.at[a, j],
                                                          device_id=sibling, device_id_type=MESH))
                recvs.append(pltpu.make_async_remote_copy(src_ref=want, dst_ref=want, send_sem=send_sems.at[a, j], recv_sem=recv_sems.at[a, j],
                                                          device_id=sibling, device_id_type=MESH))
        return sends, recvs

    def start(ins, outs, sems):
        for cp in copies(outs, sems)[0]:
            cp.start()

    def finish(ins, outs, sems):
        sends, recvs = copies(outs, sems)
        for cp in recvs:
            cp.wait_recv()
        for cp in sends:
            cp.wait_send()

    return _Comm(bufs, [_sds(b.shape, b.dtype) for b in bufs], {a: a for a in range(n)},
                 [pltpu.SemaphoreType.DMA((n, 3)), pltpu.SemaphoreType.DMA((n, 3))], start, finish)


def _all_gather(name, arrays):
    n = len(arrays)

    def body(*refs):
        ins = refs[:n]
        outs = refs[n:2 * n]
        send_sems, recv_sems, local_sems = refs[2 * n:]
        x, y, c = _place()
        me, sibling = (x, y, c), (x, y, 1 - c)
        chips = _other_chips(x, y)

        def copy(a, k, block, to, src=None):
            slot = outs[a].at[_dev_index(block)]
            return pltpu.make_async_remote_copy(src_ref=slot if src is None else src, dst_ref=slot, send_sem=send_sems.at[a, k],
                                                recv_sem=recv_sems.at[a, k], device_id=to, device_id_type=MESH)

        mine = [pltpu.make_async_copy(ins[a], outs[a].at[_dev_index(me)], local_sems.at[a]) for a in range(n)]
        for cp in mine:
            cp.start()
        first = []
        for j, chip in enumerate(chips):
            first += [copy(a, 1 + j, me, (*chip, c), src=ins[a]) for a in range(n)]
        first += [copy(a, 0, me, sibling, src=ins[a]) for a in range(n)]
        for cp in first:
            cp.start()
        passed = []
        for j, chip in enumerate(chips):
            for a in range(n):
                copy(a, 1 + j, (*chip, c), me).wait_recv()
                fwd = copy(a, 4 + j, (*chip, c), sibling)
                fwd.start()
                passed.append(fwd)
        for a in range(n):
            copy(a, 0, sibling, me).wait_recv()
            for j, chip in enumerate(chips):
                copy(a, 4 + j, (*chip, 1 - c), me).wait_recv()
        for cp in first + passed:
            cp.wait_send()
        for cp in mine:
            cp.wait()

    return _pcall(
        body, name=name, in_specs=[ANY] * n, out_specs=[ANY] * n,
        out_shape=[_sds((N_DEV, *a.shape), a.dtype) for a in arrays],
        scratch_shapes=[pltpu.SemaphoreType.DMA((n, 7)), pltpu.SemaphoreType.DMA((n, 7)), pltpu.SemaphoreType.DMA((n,))],
        compiler_params=pltpu.CompilerParams(has_side_effects=True),
    )(*arrays)


def _sibling_exchange(grads):
    n = len(grads)

    def start(ins, outs, sems):
        send_sems, recv_sems = sems
        x, y, c = _place()
        for a in range(n):
            for ch in range(N_CHIP):
                pltpu.make_async_remote_copy(src_ref=ins[a].at[2 * ch + 1 - c], dst_ref=outs[a].at[ch], send_sem=send_sems.at[a],
                                             recv_sem=recv_sems.at[a], device_id=(x, y, 1 - c), device_id_type=MESH).start()

    def finish(ins, outs, sems):
        send_sems, recv_sems = sems
        x, y, c = _place()
        for a in range(n):
            pltpu.make_async_remote_copy(src_ref=outs[a], dst_ref=outs[a], send_sem=send_sems.at[a], recv_sem=recv_sems.at[a],
                                         device_id=(x, y, 1 - c), device_id_type=MESH).wait()

    return _Comm(grads, [_sds((N_CHIP, *g.shape[1:]), g.dtype) for g in grads], {},
                 [pltpu.SemaphoreType.DMA((n,)), pltpu.SemaphoreType.DMA((n,))], start, finish)


def _chip_exchange(parts, slots, layers, stacked):
    n = len(parts)
    names = []
    for nm, _ in slots:
        if nm not in names:
            names.append(nm)
    shapes = {nm: _sds((N_CHIP, layers[nm], *parts[a].shape[1:]), parts[a].dtype) for a, (nm, _) in enumerate(slots)}
    kept = [nm for nm in names if stacked.get(nm) is not None]
    aliases = {n + k: names.index(nm) for k, nm in enumerate(kept)}

    def copies(ins, outs, sems):
        send_sems, recv_sems, local_sems = sems
        x, y, c = _place()
        mine = 2 * x + y
        local, sends, recvs = [], [], []
        for a, (nm, l) in enumerate(slots):
            buf = outs[names.index(nm)]
            local.append(pltpu.make_async_copy(ins[a].at[mine], buf.at[mine, l], local_sems.at[a]))
            for j, chip in enumerate(_other_chips(x, y)):
                theirs = buf.at[2 * chip[0] + chip[1], l]
                sends.append(pltpu.make_async_remote_copy(src_ref=ins[a].at[2 * chip[0] + chip[1]], dst_ref=buf.at[mine, l], send_sem=send_sems.at[a, j],
                                                          recv_sem=recv_sems.at[a, j], device_id=(*chip, c), device_id_type=MESH))
                recvs.append(pltpu.make_async_remote_copy(src_ref=theirs, dst_ref=theirs, send_sem=send_sems.at[a, j],
                                                          recv_sem=recv_sems.at[a, j], device_id=(*chip, c), device_id_type=MESH))
        return local, sends, recvs

    def start(ins, outs, sems):
        local, sends, _ = copies(ins, outs, sems)
        for cp in local + sends:
            cp.start()

    def finish(ins, outs, sems):
        local, sends, recvs = copies(ins, outs, sems)
        for cp in recvs:
            cp.wait_recv()
        for cp in sends:
            cp.wait_send()
        for cp in local:
            cp.wait()

    comm = _Comm([*parts, *[stacked[nm] for nm in kept]], [shapes[nm] for nm in names], aliases,
                 [pltpu.SemaphoreType.DMA((n, 3)), pltpu.SemaphoreType.DMA((n, 3)), pltpu.SemaphoreType.DMA((n,))], start, finish)
    return comm, names


def _matmul(name, a, b, extras, *, grid, a_spec, b_spec, extra_specs, out_shapes, out_specs, dims, k_axis=None, nk=1,
            acc_shape=None, epilogue=None, comm=None, n_sum=0, write=None):
    n_extra = len(extras)
    n_out = len(out_shapes)

    def body(*refs):
        a_ref, b_ref = refs[0], refs[1]
        ex = refs[2:2 + n_extra]
        outs = refs[2 + n_extra:2 + n_extra + n_out]
        prod = lax.dot_general(a_ref[...], b_ref[...], dims, preferred_element_type=F32)

        def finish(acc):
            if write is not None:
                write(outs, acc, *[e[...] for e in ex])
                return
            res = epilogue(acc, *[e[...] for e in ex]) if epilogue is not None else (acc,)
            first = None
            for d in range(len(grid)):
                if d != k_axis:
                    here = pl.program_id(d) == 0
                    first = here if first is None else jnp.logical_and(first, here)
            for idx, (o, r) in enumerate(zip(outs, res)):
                if idx < n_out - n_sum:
                    o[...] = r.astype(o.dtype)
                else:
                    @pl.when(first)
                    def _(o=o, r=r):
                        o[...] = r.astype(o.dtype)

                    @pl.when(jnp.logical_not(first))
                    def _(o=o, r=r):
                        o[...] += r.astype(o.dtype)

        if k_axis is None:
            finish(prod)
        else:
            acc_ref = refs[-1]
            k = pl.program_id(k_axis)

            @pl.when(k == 0)
            def _():
                acc_ref[...] = prod

            @pl.when(k > 0)
            def _():
                acc_ref[...] += prod

            @pl.when(k == nk - 1)
            def _():
                finish(acc_ref[...])

    scratch = [] if k_axis is None else [pltpu.VMEM(acc_shape, F32)]
    return _call(name, body, [a, b, *extras], [a_spec, b_spec, *extra_specs], list(out_shapes), list(out_specs), scratch, grid, comm)


def _rowwise(name, fn, operands, *, grid, in_specs, out_shapes, out_specs, n_acc=0, grid_spec_prefetch=None, comm=None):
    n_in = len(operands)
    n_out = len(out_shapes)
    n_pre = 0 if grid_spec_prefetch is None else 1

    def body(*refs):
        refs = refs[n_pre:]
        ins = refs[:n_in]
        outs = refs[n_in:n_in + n_out]
        res = fn(*[r[...] for r in ins])
        if not isinstance(res, (tuple, list)):
            res = (res,)
        first = pl.program_id(0) == 0
        for d in range(1, len(grid)):
            first = jnp.logical_and(first, pl.program_id(d) == 0)
        for idx, (o, r) in enumerate(zip(outs, res)):
            if idx < n_out - n_acc:
                o[...] = r.astype(o.dtype)
            else:
                @pl.when(first)
                def _(o=o, r=r):
                    o[...] = r.astype(o.dtype)

                @pl.when(jnp.logical_not(first))
                def _(o=o, r=r):
                    o[...] += r.astype(o.dtype)

    if comm is not None:
        return _call(name, body, list(operands), list(in_specs), list(out_shapes), list(out_specs), [], grid, comm)
    if grid_spec_prefetch is None:
        return _pcall(body, name=name, grid=grid, in_specs=in_specs, out_specs=out_specs, out_shape=out_shapes,
                      compiler_params=_params(len(grid)))(*operands)
    gs = pltpu.PrefetchScalarGridSpec(num_scalar_prefetch=1, grid=grid, in_specs=in_specs, out_specs=out_specs)
    return _pcall(body, name=name, grid_spec=gs, out_shape=out_shapes,
                  compiler_params=_params(len(grid)))(grid_spec_prefetch, *operands)


def _row_spec(tm, w):
    return pl.BlockSpec((tm, w), lambda i: (i, 0))


def _const_spec(shape):
    nd = len(shape)
    return pl.BlockSpec(tuple(shape), lambda *_: (0,) * nd)


def _rms_fwd(x, g):
    r = lax.rsqrt(jnp.mean(x * x, axis=-1, keepdims=True) + NORM_EPS)
    return x * r * g


def _rms_bwd(x, g, dy):
    r = lax.rsqrt(jnp.mean(x * x, axis=-1, keepdims=True) + NORM_EPS)
    xh = x * r
    u = dy * g
    dx = r * (u - xh * jnp.mean(u * xh, axis=-1, keepdims=True))
    dg = jnp.sum(dy * xh, axis=0, keepdims=True)
    return dx, dg


def _gelu_and_grad(z):
    cdf = 0.5 * (1.0 + lax.erf(z * (2.0 ** -0.5)))
    return cdf + z * jnp.exp(-0.5 * z * z) * ((2.0 * math.pi) ** -0.5), z * cdf


def _rope_fwd(x, cc, sa, sb):
    return x * cc + pltpu.roll(x, 96, 1) * sa + pltpu.roll(x, 32, 1) * sb


def _rope_bwd(d, cc, sa, sb):
    return d * cc + pltpu.roll(d * sa, 32, 1) + pltpu.roll(d * sb, 96, 1)


def _adam(w, g, m, v):
    m = ADAM_B1 * m + (1.0 - ADAM_B1) * g
    v = ADAM_B2 * v + (1.0 - ADAM_B2) * (g * g)
    m_hat = m / (1.0 - ADAM_B1 ** ADAM_STEP)
    v_hat = v / (1.0 - ADAM_B2 ** ADAM_STEP)
    delta = -ADAM_LR * (m_hat / (jnp.sqrt(v_hat) + ADAM_EPS) + ADAM_WD * w)
    return delta, m, v


def _flash_fwd(q, k, vt, tq, comm=None):
    h, t = vt.shape[0], q.shape[0]
    nq = t // tq

    chunk_blocks = [c for c in (4, 2, 1) if c < nq]

    def body(q_ref, k_ref, vt_ref, o_ref, lse_ref, m_ref, l_ref, acc_ref):
        qi = pl.program_id(1)
        m_ref[...] = jnp.full((1, tq), NEG, F32)
        l_ref[...] = jnp.zeros((1, tq), F32)
        acc_ref[...] = jnp.zeros((VDIM, tq), F32)

        def update(kb0, nblk, masked):
            kb = k_ref[pl.ds(pl.multiple_of(kb0 * tq, tq), nblk * tq), :]
            st = lax.dot_general(kb, q_ref[...], NT, preferred_element_type=F32)
            if masked:
                key = lax.broadcasted_iota(jnp.int32, (tq, tq), 0)
                qry = lax.broadcasted_iota(jnp.int32, (tq, tq), 1)
                st = jnp.where(key <= qry, st, NEG)
            m_old = m_ref[...]
            m_new = jnp.maximum(m_old, jnp.max(st, axis=0, keepdims=True))
            alpha = jnp.exp2((m_old - m_new) * EXP2_SCALE)
            pt = jnp.exp2((st - m_new) * EXP2_SCALE)
            l_ref[...] = alpha * l_ref[...] + jnp.sum(pt, axis=0, keepdims=True)
            ptb = pt.astype(BF16)
            pv = lax.dot_general(vt_ref[kb0], ptb[:tq], NN, preferred_element_type=F32)
            for j in range(1, nblk):
                pv += lax.dot_general(vt_ref[kb0 + j], ptb[j * tq:(j + 1) * tq], NN, preferred_element_type=F32)
            acc_ref[...] = alpha * acc_ref[...] + pv
            m_ref[...] = m_new

        start = jnp.int32(0)
        for c in chunk_blocks:
            take = (qi & c) != 0

            @pl.when(take)
            def _(start=start, c=c):
                update(start, c, False)

            start = start + jnp.where(take, c, 0)
        update(qi, 1, True)
        l = l_ref[...]
        o_ref[...] = (acc_ref[...] / l).T.astype(o_ref.dtype)
        lse_ref[...] = m_ref[...] * EXP2_SCALE + jnp.log2(l)

    return _call(
        "flash_fwd", body, [q, k, vt],
        [pl.BlockSpec((tq, QPAD), lambda hh, i: (i, hh)),
         pl.BlockSpec((t, QPAD), lambda hh, i: (0, hh)),
         pl.BlockSpec((None, nq, VDIM, tq), lambda hh, i: (hh, 0, 0, 0))],
        [_sds((t, h * VDIM), BF16), _sds((h, nq, 1, tq), F32)],
        [pl.BlockSpec((tq, VDIM), lambda hh, i: (i, hh)),
         pl.BlockSpec((None, None, 1, tq), lambda hh, i: (hh, i, 0, 0))],
        [pltpu.VMEM((1, tq), F32), pltpu.VMEM((1, tq), F32), pltpu.VMEM((VDIM, tq), F32)], (h, nq), comm)


def _flash_bwd(q, k, v, o, do, lse, tabs, tq, comm=None):
    t = q.shape[0]
    h = q.shape[1] // QPAD
    nq = t // tq

    def body(q_ref, k_ref, v_ref, o_ref, do_ref, lse_ref, cc_ref, sa_ref, sb_ref, dq_ref, dk_out, dv_out, delta_ref, dqt_ref, dk_ref, dv_ref):
        kj = pl.program_id(1)

        @pl.when(kj == 0)
        def _():
            dqt_ref[...] = jnp.zeros_like(dqt_ref)
            ones = jnp.ones((8, VDIM), BF16)
            for qi in range(nq):
                rows = pl.ds(qi * tq, tq)
                prod = do_ref[rows, :].astype(F32) * o_ref[rows, :].astype(F32)
                hi = prod.astype(BF16)
                lo = (prod - hi.astype(F32)).astype(BF16)
                delta_ref[qi] = (lax.dot_general(ones, hi, NT, preferred_element_type=F32)
                                 + lax.dot_general(ones, lo, NT, preferred_element_type=F32))

        kb = k_ref[...]
        vb = v_ref[...]
        kbt = kb.astype(F32).T.astype(BF16)
        dk_ref[...] = jnp.zeros_like(dk_ref)
        dv_ref[...] = jnp.zeros_like(dv_ref)

        def step(q0, nblk, masked):
            rows = pl.ds(pl.multiple_of(q0 * tq, tq), nblk * tq)
            qb = q_ref[rows, :]
            dob = do_ref[rows, :]
            lse = jnp.concatenate([lse_ref[q0 + j] for j in range(nblk)], axis=1)
            delta = jnp.concatenate([delta_ref[q0 + j, pl.ds(0, 1), :] for j in range(nblk)], axis=1)
            st = lax.dot_general(kb, qb, NT, preferred_element_type=F32)
            pt = jnp.exp2(st * EXP2_SCALE - lse)
            if masked:
                key = lax.broadcasted_iota(jnp.int32, (tq, tq), 0)
                qry = lax.broadcasted_iota(jnp.int32, (tq, tq), 1)
                pt = jnp.where(key <= qry, pt, 0.0)
            dv_ref[...] += lax.dot_general(pt.astype(BF16), dob, NN, preferred_element_type=F32)
            dpt = lax.dot_general(vb, dob, NT, preferred_element_type=F32)
            dst = (pt * (dpt - delta) * ATTN_SCALE).astype(BF16)
            dk_ref[...] += lax.dot_general(dst, qb, NN, preferred_element_type=F32)
            dqt = lax.dot_general(kbt, dst, NN, preferred_element_type=F32)
            for j in range(nblk):
                dqt_ref[q0 + j] += dqt[:, j * tq:(j + 1) * tq]

        later = nq - 1 - kj
        step(kj, 1, True)
        start = kj + 1
        for c in [c for c in (1, 2, 4) if c < nq]:
            take = (later & c) != 0

            @pl.when(take)
            def _(start=start, c=c):
                step(start, c, False)

            start = start + jnp.where(take, c, 0)
        dk_out[...] = dk_ref[...].astype(BF16)
        dv_out[...] = dv_ref[...].astype(BF16)

        @pl.when(kj == nq - 1)
        def _():
            for qi in range(nq):
                rows = pl.ds(qi * tq, tq)
                d = dqt_ref[qi].T
                roped = _rope_bwd(d[:, NOPE:], cc_ref[rows, :], sa_ref[rows, :], sb_ref[rows, :])
                dq_ref[rows, :] = jnp.concatenate([d[:, :NOPE], roped], axis=1).astype(BF16)

    head_q = pl.BlockSpec((t, QPAD), lambda hh, j: (0, hh))
    head_v = pl.BlockSpec((t, VDIM), lambda hh, j: (0, hh))
    table = pl.BlockSpec((t, 128), lambda hh, j: (0, 0))
    return _call(
        "flash_bwd", body, [q, k, v, o, do, lse, *tabs],
        [head_q, pl.BlockSpec((tq, QPAD), lambda hh, j: (j, hh)), pl.BlockSpec((tq, VDIM), lambda hh, j: (j, hh)), head_v, head_v,
         pl.BlockSpec((None, nq, 1, tq), lambda hh, j: (hh, 0, 0, 0)), table, table, table],
        [_sds((t, h * QPAD), BF16), _sds((t, h * QPAD), BF16), _sds((t, h * VDIM), BF16)],
        [head_q, pl.BlockSpec((tq, QPAD), lambda hh, j: (j, hh)), pl.BlockSpec((tq, VDIM), lambda hh, j: (j, hh))],
        [pltpu.VMEM((nq, 8, tq), F32), pltpu.VMEM((nq, QPAD, tq), F32), pltpu.VMEM((tq, QPAD), F32), pltpu.VMEM((tq, VDIM), F32)], (h, nq), comm)


def _tril_bf16(w):
    row = lax.broadcasted_iota(jnp.int32, w.shape, 0)
    col = lax.broadcasted_iota(jnp.int32, w.shape, 1)
    return jnp.where(col <= row, w, 0.0).astype(BF16)


def _layer_norm_parts(v0):
    mu = jnp.mean(v0, axis=-1, keepdims=True)
    vc = v0 - mu
    rstd = lax.rsqrt(jnp.mean(vc * vc, axis=-1, keepdims=True) + LN_EPS)
    return vc * rstd, rstd


def _sgu_mid_fwd(ge, ln_g, ln_b, w_sp, b_sp, chunks_per_step):
    t, e2 = ge.shape
    e = e2 // 2
    gd = e // SGU_GROUPS
    rows = SGU_CHUNK * chunks_per_step

    def body(u_ref, v_ref, g_ref, b_ref, w_ref, bs_ref, gate_ref):
        for ck in range(chunks_per_step):
            r = pl.ds(ck * SGU_CHUNK, SGU_CHUNK)
            xh, _ = _layer_norm_parts(v_ref[r, :].astype(F32))
            v1 = (xh * g_ref[...] + b_ref[...]).astype(BF16)
            for g in range(SGU_GROUPS):
                cols = pl.ds(g * gd, gd)
                mixed = lax.dot_general(_tril_bf16(w_ref[g]), v1[:, g * gd:(g + 1) * gd], NN, preferred_element_type=F32) + bs_ref[g]
                gate_ref[r, cols] = (u_ref[r, cols].astype(F32) * mixed).astype(BF16)

    return _pcall(
        body, name="sgu_mid_fwd", grid=(t // rows,),
        in_specs=[pl.BlockSpec((rows, e), lambda i: (i, 0)), pl.BlockSpec((rows, e), lambda i: (i, 1)),
                  _const_spec((1, e)), _const_spec((1, e)), _const_spec(w_sp.shape), _const_spec(b_sp.shape)],
        out_specs=pl.BlockSpec((rows, e), lambda i: (i, 0)),
        out_shape=_sds((t, e), BF16), compiler_params=_params(1),
    )(ge, ge, ln_g, ln_b, w_sp, b_sp)


def _sgu_mid_bwd(ge, gp, dgate, ln_g, ln_b, w_sp, b_sp, chunks_per_step):
    t, e2 = ge.shape
    e = e2 // 2
    gd = e // SGU_GROUPS
    rows = SGU_CHUNK * chunks_per_step

    def body(u_ref, v_ref, zu_ref, zv_ref, dg_ref, g_ref, b_ref, w_ref, bs_ref, dz_ref, dw_ref, dbs_ref, dlg_ref, dlb_ref):
        @pl.when(pl.program_id(0) == 0)
        def _():
            dw_ref[...] = jnp.zeros_like(dw_ref)
            dbs_ref[...] = jnp.zeros_like(dbs_ref)
            dlg_ref[...] = jnp.zeros_like(dlg_ref)
            dlb_ref[...] = jnp.zeros_like(dlb_ref)

        for ck in range(chunks_per_step):
            r = pl.ds(ck * SGU_CHUNK, SGU_CHUNK)
            xh, rstd = _layer_norm_parts(v_ref[r, :].astype(F32))
            v1 = (xh * g_ref[...] + b_ref[...]).astype(BF16)
            dv1_parts = []
            for g in range(SGU_GROUPS):
                cols = pl.ds(g * gd, gd)
                wc = _tril_bf16(w_ref[g])
                v1g = v1[:, g * gd:(g + 1) * gd]
                mixed = lax.dot_general(wc, v1g, NN, preferred_element_type=F32) + bs_ref[g]
                dgate = dg_ref[r, cols].astype(F32)
                dmixed = dgate * u_ref[r, cols].astype(F32)
                du = dgate * mixed
                dz_ref[r, cols] = (du * zu_ref[r, cols].astype(F32)).astype(BF16)
                dbs_ref[g] += jnp.sum(dmixed, axis=1, keepdims=True)
                dmb = dmixed.astype(BF16)
                dwg = lax.dot_general(dmb, v1g, NT, preferred_element_type=F32)
                row = lax.broadcasted_iota(jnp.int32, dwg.shape, 0)
                col = lax.broadcasted_iota(jnp.int32, dwg.shape, 1)
                dw_ref[g] += jnp.where(col <= row, dwg, 0.0)
                dv1_parts.append(lax.dot_general(wc, dmb, TN, preferred_element_type=F32))
            dv1 = jnp.concatenate(dv1_parts, axis=1)
            dlg_ref[...] += jnp.sum(dv1 * xh, axis=0, keepdims=True)
            dlb_ref[...] += jnp.sum(dv1, axis=0, keepdims=True)
            dxh = dv1 * g_ref[...]
            dv0 = rstd * (dxh - jnp.mean(dxh, axis=-1, keepdims=True) - xh * jnp.mean(dxh * xh, axis=-1, keepdims=True))
            dz_ref[r, pl.ds(e, e)] = (dv0 * zv_ref[r, :].astype(F32)).astype(BF16)

    half0 = pl.BlockSpec((rows, e), lambda i: (i, 0))
    half1 = pl.BlockSpec((rows, e), lambda i: (i, 1))
    return _pcall(
        body, name="sgu_mid_bwd", grid=(t // rows,),
        in_specs=[half0, half1, half0, half1, half0, _const_spec((1, e)), _const_spec((1, e)), _const_spec(w_sp.shape), _const_spec(b_sp.shape)],
        out_specs=[pl.BlockSpec((rows, e2), lambda i: (i, 0)), _const_spec(w_sp.shape), _const_spec(b_sp.shape), _const_spec((1, e)), _const_spec((1, e))],
        out_shape=[_sds((t, e2), BF16), _sds(w_sp.shape, F32), _sds(b_sp.shape, F32), _sds((1, e), F32), _sds((1, e), F32)],
        compiler_params=_params(1),
    )(ge, ge, gp, gp, dgate, ln_g, ln_b, w_sp, b_sp)


def kernel(x, positions, norm_mix, norm_ffn, final_norm, mla_w_dkv, mla_q_norm, mla_kv_norm, mla_w_uq, mla_w_ukv, mla_w_o, sgu_w_in, sgu_ln_g, sgu_ln_b, sgu_w_spatial, sgu_b_spatial, sgu_w_out, ffn_w_up, ffn_w_down, loss_target, m_norm_mix, m_norm_ffn, m_final_norm, m_mla_w_dkv, m_mla_q_norm, m_mla_kv_norm, m_mla_w_uq, m_mla_w_ukv, m_mla_w_o, m_sgu_w_in, m_sgu_ln_g, m_sgu_ln_b, m_sgu_w_spatial, m_sgu_b_spatial, m_sgu_w_out, m_ffn_w_up, m_ffn_w_down, v_norm_mix, v_norm_ffn, v_final_norm, v_mla_w_dkv, v_mla_q_norm, v_mla_kv_norm, v_mla_w_uq, v_mla_w_ukv, v_mla_w_o, v_sgu_w_in, v_sgu_ln_g, v_sgu_ln_b, v_sgu_w_spatial, v_sgu_b_spatial, v_sgu_w_out, v_ffn_w_up, v_ffn_w_down):
    _, T, D = x.shape
    depth = norm_mix.shape[0]
    n_mla, n_sgu = mla_w_dkv.shape[0], sgu_w_in.shape[0]
    assert depth % 2 == 0
    FF = ffn_w_up.shape[2] * N_DEV
    E = sgu_w_out.shape[1] * N_DEV
    ffc, ec, e2c = FF // N_DEV, E // N_DEV, 2 * E // N_DEV
    dc = D // N_DEV
    OW = HEADS * VDIM
    HW = HEADS * QPAD
    owc = OW // N_DEV
    tm = _tile(T, 1024)
    tb = _tile(T, 4096)
    tk = _tile(T, 512)
    tq = _tile(T, 512)
    ts = _tile(T, 256)
    nt = T // tm
    x2 = x.reshape(T, D)
    tgt = loss_target.reshape(T, D)
    cidx = lax.axis_index("c").astype(jnp.int32).reshape(1)

    ln_local = jnp.concatenate([sgu_ln_g, sgu_ln_b, jnp.zeros((8 - 2 * n_sgu, ec), F32)], axis=0)
    mla_sh = [[w[l].astype(BF16) for w in (mla_w_dkv, mla_w_uq, mla_w_ukv, mla_w_o)] for l in range(n_mla)]

    def mla_layouts(g_dkv, g_uq, g_ukv, g_o):
        w_dkv = jnp.pad(g_dkv.reshape(1, D, LAT), ((0, 0), (0, 0), (0, LAT_PAD - LAT)))
        w_uq = jnp.pad(g_uq, ((0, 0), (0, 0), (0, QPAD - NOPE - ROPE))).transpose(1, 0, 2).reshape(1, Q_RANK, HEADS * QPAD)
        w_ukv = g_ukv.transpose(1, 0, 2).reshape(1, KV_RANK, HEADS * (NOPE + VDIM))
        return w_dkv, w_uq, w_ukv, g_o.reshape(1, HEADS * VDIM, D)

    mla_w = [None] * n_mla
    mla_w[0] = mla_layouts(*_all_gather("gather_first_weights", mla_sh[0]))
    small_later = [a for l in range(1, n_mla) for a in mla_sh[l]] + [ln_local]
    ln_g_full, ln_b_full = [None] * n_sgu, [None] * n_sgu
    b_sp = sgu_b_spatial.reshape(n_sgu, SGU_GROUPS, SGU_CHUNK, 1)
    up_sh = [ffn_w_up[i].astype(BF16) for i in range(depth)]
    down_sh = [ffn_w_down[i].astype(BF16) for i in range(depth)]
    in_sh = [sgu_w_in[l].astype(BF16) for l in range(n_sgu)]
    out_sh = [sgu_w_out[l].astype(BF16) for l in range(n_sgu)]
    g_up, g_down, g_in, g_out = [None] * depth, [None] * depth, [None] * n_sgu, [None] * n_sgu

    inv_freq = ROPE_THETA ** (-jnp.arange(0, ROPE, 2, dtype=F32) / ROPE)
    zeros32 = jnp.zeros((ROPE // 2,), F32)
    inv128 = jnp.concatenate([inv_freq, inv_freq, zeros32, zeros32]).reshape(1, 128)
    sel_a = jnp.concatenate([-jnp.ones((32,), F32), zeros32, zeros32, zeros32]).reshape(1, 128)
    sel_b = jnp.concatenate([zeros32, jnp.ones((32,), F32), zeros32, zeros32]).reshape(1, 128)
    sel_c = jnp.concatenate([jnp.ones((64,), F32), zeros32, zeros32]).reshape(1, 128)

    def rope_tables(pos, inv, sa, sb, sc):
        ang = pos.astype(F32) * inv
        cs, sn = jnp.cos(ang), jnp.sin(ang)
        return cs * sc, sn * sa, sn * sb

    t_cc, t_sa, t_sb = _rowwise(
        "rope_tables", rope_tables, [positions.reshape(T, 1), inv128, sel_a, sel_b, sel_c], grid=(nt,),
        in_specs=[_row_spec(tm, 1)] + [_const_spec((1, 128))] * 4,
        out_shapes=[_sds((T, 128), F32)] * 3, out_specs=[_row_spec(tm, 128)] * 3)
    tab_specs = [_row_spec(tm, 128)] * 3

    def rmsnorm(xv, g):
        return _rowwise("rmsnorm", lambda a, gg: _rms_fwd(a, gg), [xv, g.reshape(1, D)], grid=(nt,),
                        in_specs=[_row_spec(tm, D), _const_spec((1, D))], out_shapes=_sds((T, D), BF16), out_specs=_row_spec(tm, D))

    def proj_cols(name, h, gw, nc, epilogue, n_out, comm=None):
        return _matmul(name, h, gw, [], grid=(N_DEV, T // tb),
                       a_spec=pl.BlockSpec((tb, D), lambda j, i: (i, 0)),
                       b_spec=pl.BlockSpec((None, D, nc), lambda j, i: (j, 0, 0)), extra_specs=[],
                       out_shapes=[_sds((T, nc * N_DEV), BF16)] * n_out, out_specs=[pl.BlockSpec((tb, nc), lambda j, i: (i, j))] * n_out,
                       dims=NN, epilogue=epilogue, comm=comm)

    def residual_norm(acc, xr, g):
        xn = acc + xr
        return xn, _rms_fwd(xn, g)

    def proj_rows_residual(name, a, gw, xres, g_next, comm=None):
        kk_ = a.shape[1]
        return _matmul(name, a, gw.reshape(kk_, D), [xres, g_next.reshape(1, D)], grid=(T // tk,),
                       a_spec=_row_spec(tk, kk_), b_spec=_const_spec((kk_, D)), extra_specs=[_row_spec(tk, D), _const_spec((1, D))],
                       out_shapes=[_sds((T, D), F32), _sds((T, D), BF16)], out_specs=[_row_spec(tk, D)] * 2,
                       dims=NN, epilogue=residual_norm, comm=comm)

    def back_rows(name, dy, gw, kc, extras, epilogue, comm=None):
        return _matmul(name, dy, gw, extras, grid=(N_DEV, T // tb),
                       a_spec=pl.BlockSpec((tb, D), lambda j, i: (i, 0)),
                       b_spec=pl.BlockSpec((None, kc, D), lambda j, i: (j, 0, 0)),
                       extra_specs=[pl.BlockSpec((tb, kc), lambda j, i: (i, j))] * len(extras),
                       out_shapes=[_sds((T, kc * N_DEV), BF16)], out_specs=[pl.BlockSpec((tb, kc), lambda j, i: (i, j))],
                       dims=NT, epilogue=epilogue, comm=comm)

    def norm_bwd_epilogue(dh, xv, g, dxi):
        dxn, dg = _rms_bwd(xv, g, dh)
        return dxi + dxn, dxi + dxn, dg

    def transposed(gw):
        return gw.transpose(0, 2, 1).reshape(gw.shape[0] * gw.shape[2], D)

    def back_cols(name, da, gwt, xv, g, dx_in, comm=None):
        n = da.shape[1]
        row = _row_spec(tk, D)
        return _matmul(name, da, gwt, [xv, g.reshape(1, D), dx_in], grid=(T // tk,),
                       a_spec=_row_spec(tk, n), b_spec=_const_spec((n, D)), extra_specs=[row, _const_spec((1, D)), row],
                       out_shapes=[_sds((T, D), F32), _sds((T, D), BF16), _sds((1, D), F32)], out_specs=[row, row, _const_spec((1, D))],
                       dims=NN, epilogue=norm_bwd_epilogue, n_sum=1, comm=comm)

    def token_sum(tt):
        return dict(k_axis=1, nk=T // tt) if T // tt > 1 else dict(k_axis=None)

    def wgrad_cols(name, h, da, nc):
        return _matmul(name, h, da, [], grid=(N_DEV, T // tb),
                       a_spec=pl.BlockSpec((tb, D), lambda j, t: (t, 0)), b_spec=pl.BlockSpec((tb, nc), lambda j, t: (t, j)),
                       extra_specs=[], out_shapes=[_sds((N_DEV, D, nc), BF16)],
                       out_specs=[pl.BlockSpec((None, D, nc), lambda j, t: (j, 0, 0))],
                       dims=TN, acc_shape=(D, nc), **token_sum(tb))[0]

    def wgrad_rows(name, a, dy, kc, ncols, tt):
        return _matmul(name, a, dy, [], grid=(a.shape[1] // kc, T // tt),
                       a_spec=pl.BlockSpec((tt, kc), lambda j, t: (t, j)), b_spec=pl.BlockSpec((tt, ncols), lambda j, t: (t, 0)),
                       extra_specs=[], out_shapes=[_sds((a.shape[1], ncols), BF16)],
                       out_specs=[pl.BlockSpec((kc, ncols), lambda j, t: (j, 0))],
                       dims=TN, acc_shape=(kc, ncols), **token_sum(tt))[0]

    saved = []
    xs = x2
    for i in range(depth):
        l = i // 2
        if i == 0:
            h = rmsnorm(xs, norm_mix[0])
        if i % 2 == 0:
            w_dkv, w_uq, w_ukv, w_o = mla_w[l]
            lat = _matmul("mla_down", h, w_dkv, [], grid=(nt,), a_spec=_row_spec(tm, D),
                          b_spec=pl.BlockSpec((None, D, LAT_PAD), lambda i_: (0, 0, 0)), extra_specs=[],
                          out_shapes=[_sds((T, LAT_PAD), F32)], out_specs=[_row_spec(tm, LAT_PAD)], dims=NN)[0]

            def latent_post(la, qn, kvn, cc, sa, sb):
                cq = _rms_fwd(la[:, :Q_RANK], qn)
                ckv = _rms_fwd(la[:, Q_RANK:Q_RANK + KV_RANK], kvn)
                kr = _rope_fwd(la[:, Q_RANK + KV_RANK:], cc, sa, sb)
                return cq, ckv, kr

            cq, ckv, kr = _rowwise(
                "mla_latent", latent_post, [lat, mla_q_norm[l].reshape(1, Q_RANK), mla_kv_norm[l].reshape(1, KV_RANK), t_cc, t_sa, t_sb],
                grid=(nt,), in_specs=[_row_spec(tm, LAT_PAD), _const_spec((1, Q_RANK)), _const_spec((1, KV_RANK))] + tab_specs,
                out_shapes=[_sds((T, Q_RANK), BF16), _sds((T, KV_RANK), BF16), _sds((T, 128), BF16)],
                out_specs=[_row_spec(tm, Q_RANK), _row_spec(tm, KV_RANK), _row_spec(tm, 128)])

            def q_epilogue(acc, cc, sa, sb):
                parts = []
                for b in range(HEADS):
                    parts += [acc[:, b * QPAD:b * QPAD + NOPE], _rope_fwd(acc[:, b * QPAD + NOPE:(b + 1) * QPAD], cc, sa, sb)]
                return (jnp.concatenate(parts, axis=1),)

            q = _matmul("mla_q", cq, w_uq, [t_cc, t_sa, t_sb], grid=(nt,), a_spec=_row_spec(tm, Q_RANK),
                        b_spec=pl.BlockSpec((None, Q_RANK, HW), lambda i_: (0, 0, 0)), extra_specs=tab_specs,
                        out_shapes=[_sds((T, HW), BF16)], out_specs=[_row_spec(tm, HW)], dims=NN, epilogue=q_epilogue)[0]

            def kv_write(outs, acc, krb):
                k_ref, v_ref, vt_ref = outs
                for b in range(HEADS):
                    vb = acc[:, b * QPAD + NOPE:(b + 1) * QPAD]
                    k_ref[:, b * QPAD:b * QPAD + NOPE] = acc[:, b * QPAD:b * QPAD + NOPE].astype(BF16)
                    k_ref[:, b * QPAD + NOPE:(b + 1) * QPAD] = krb
                    v_ref[:, b * VDIM:(b + 1) * VDIM] = vb.astype(BF16)
                    vbt = vb.T.astype(BF16)
                    for u in range(tm // tq):
                        vt_ref[b, u] = vbt[:, u * tq:(u + 1) * tq]

            kk, vv, vt = _matmul("mla_kv", ckv, w_ukv, [kr], grid=(nt,), a_spec=_row_spec(tm, KV_RANK),
                                 b_spec=pl.BlockSpec((None, KV_RANK, HW), lambda i_: (0, 0, 0)), extra_specs=[_row_spec(tm, 128)],
                                 out_shapes=[_sds((T, HW), BF16), _sds((T, OW), BF16), _sds((HEADS, T // tq, VDIM, tq), BF16)],
                                 out_specs=[_row_spec(tm, HW), _row_spec(tm, OW), pl.BlockSpec((HEADS, tm // tq, VDIM, tq), lambda i_: (0, i_, 0, 0))],
                                 dims=NN, write=kv_write)
            group = [up_sh[i], down_sh[i], in_sh[l], out_sh[l]] + (small_later if i == 0 else [])
            o, lse, *bufs = _flash_fwd(q, kk, vt, tq, comm=_gather_level1(group))
            xm, h2, g_up[i], g_down[i] = _matmul(
                "mla_out", o, w_o, [xs, norm_ffn[i].reshape(1, D)], grid=(nt,), a_spec=_row_spec(tm, OW),
                b_spec=pl.BlockSpec((None, OW, D), lambda i_: (0, 0, 0)), extra_specs=[_row_spec(tm, D), _const_spec((1, D))],
                out_shapes=[_sds((T, D), F32), _sds((T, D), BF16)], out_specs=[_row_spec(tm, D)] * 2, dims=NN,
                epilogue=residual_norm, comm=_gather_level2(bufs[:2]))
            half_gathered = bufs[2:]
            mix_saved = (h, lat, cq, ckv, q, kk, vv, o, lse)
        else:
            gp, ge, g_up[i], g_down[i] = proj_cols("sgu_in", h, g_in[l], e2c, _gelu_and_grad, 2, comm=_gather_level2(next_mlp))
            gate = _sgu_mid_fwd(ge, ln_g_full[l], ln_b_full[l], sgu_w_spatial[l], b_sp[l], 4)
            xm, h2 = proj_rows_residual("sgu_out", gate, g_out[l], xs, norm_ffn[i])
            mix_saved = (h, gp, ge, gate)
        r, s, *rest = proj_cols("ffn_up", h2, g_up[i], ffc, lambda acc: (jnp.maximum(acc, 0.0), jnp.square(jnp.maximum(acc, 0.0))), 2,
                                comm=_merge_comm(_gather_level2(half_gathered), _gather_level1([up_sh[i + 1]])) if i % 2 == 0 else None)
        if i % 2 == 0:
            g_in[l], g_out[l], *small_gathered, next_up = rest
        if i == 0:
            for l_ in range(1, n_mla):
                mla_w[l_] = mla_layouts(*small_gathered[4 * (l_ - 1):4 * l_])
            g_ln = small_gathered[-1]
            ln_g_full = [g_ln[:, l_, :].reshape(1, E) for l_ in range(n_sgu)]
            ln_b_full = [g_ln[:, n_sgu + l_, :].reshape(1, E) for l_ in range(n_sgu)]
        xo, h_next, *rest = proj_rows_residual("ffn_down", s, g_down[i], xm, norm_mix[i + 1] if i + 1 < depth else final_norm,
                                               comm=_gather_level1([down_sh[i + 1]]) if i % 2 == 0 else None)
        if i % 2 == 0:
            next_mlp = [next_up, rest[0]]
        saved.append((xs, xm, mix_saved, h2, r, s))
        xs, h = xo, h_next

    def loss_head(xv, tg, g):
        y = _rms_fwd(xv, g)
        err = y - tg
        part = 0.5 * jnp.sum(jnp.sum(err * err, axis=-1, keepdims=True), axis=0, keepdims=True) / D
        dx, dg = _rms_bwd(xv, g, err / D)
        return dx, dx, jnp.broadcast_to(part, (1, 128)), dg

    dx, dyb, loss_part, d_final = _rowwise(
        "loss_head", loss_head, [xs, tgt, final_norm.reshape(1, D)], grid=(nt,),
        in_specs=[_row_spec(tm, D), _row_spec(tm, D), _const_spec((1, D))],
        out_shapes=[_sds((T, D), F32), _sds((T, D), BF16), _sds((1, 128), F32), _sds((1, D), F32)],
        out_specs=[_row_spec(tm, D), _row_spec(tm, D), _const_spec((1, 128)), _const_spec((1, D))], n_acc=2)
    loss = lax.psum(loss_part[0, 0], ("x", "y", "c"))

    d_norm_mix, d_norm_ffn = [None] * depth, [None] * depth
    d_qn, d_kvn = [None] * n_mla, [None] * n_mla
    d_wsp, d_bsp, d_lng, d_lnb = [None] * n_sgu, [None] * n_sgu, [None] * n_sgu, [None] * n_sgu
    layers = {"dkv": n_mla, "uq": n_mla, "ukv": n_mla, "o": n_mla, "in": n_sgu, "out": n_sgu, "up": depth, "down": depth}
    stacked = {nm: None for nm in layers}
    pending = []
    summed = []

    def add_pair(g, rcv):
        _, rws, cls = g.shape
        g4 = g.reshape(N_CHIP, 2, rws, cls)
        rt = _tile(rws, 1024)
        return _rowwise("grad_pair_sum", lambda a, b_: a.astype(F32) + b_.astype(F32), [g4, rcv], grid=(N_CHIP, rws // rt),
                        in_specs=[pl.BlockSpec((None, None, rt, cls), lambda ch, i_, cr: (ch, cr[0], i_, 0)),
                                  pl.BlockSpec((None, rt, cls), lambda ch, i_, cr: (ch, i_, 0))],
                        out_shapes=_sds(rcv.shape, BF16), out_specs=pl.BlockSpec((None, rt, cls), lambda ch, i_, cr: (ch, i_, 0)),
                        grid_spec_prefetch=cidx)

    def sibling_comm():
        return _sibling_exchange([g for _, _, g in pending]) if pending else None

    def absorb(from_sibling):
        for (nm, l_, g), rcv in zip(pending, from_sibling):
            summed.append((nm, l_, add_pair(g, rcv)))
        pending.clear()

    def chip_comm():
        if pending:
            absorb(_comm_call("grad_sibling_exchange", sibling_comm()))
        comm, names = _chip_exchange([p for _, _, p in summed], [(nm, l_) for nm, l_, _ in summed], layers, stacked)
        summed.clear()
        return comm, names

    def rows128(a, rows):
        flat = a.reshape(-1, 128)
        return jnp.pad(flat, ((0, rows - flat.shape[0]), (0, 0)))

    def pad_to(n, mult):
        return -(-n // mult) * mult

    def packed(arrs, sizes):
        return jnp.concatenate([rows128(a, sz) for a, sz in zip(arrs, sizes)], axis=0)

    n_wsp, n_bsp, n_ln = sgu_w_spatial.size // 128, pad_to(sgu_b_spatial.size // 128, 8), pad_to(n_sgu * E // 128, 8)
    early_sizes = [n_wsp, pad_to(n_wsp + n_bsp, SMALL_ROWS) - n_wsp, n_ln, n_ln]
    early_rep = early_sizes[0] + early_sizes[1]
    gathered_early = None

    for i in reversed(range(depth)):
        l = i // 2
        xs_i, xm, mix_saved, h2, r, s = saved[i]
        da, *rcv = back_rows("ffn_down_bwd", dyb, g_down[i], ffc, [r], lambda acc, rr: (acc * (2.0 * rr.astype(F32)),), comm=sibling_comm())
        absorb(rcv)
        pending.append(("down", i, wgrad_rows("ffn_down_wgrad", s, dyb, ffc, D, tb).reshape(N_DEV, ffc, D)))
        pending.append(("up", i, wgrad_cols("ffn_up_wgrad", h2, da, ffc)))
        dx, dyb, d_norm_ffn[i], *rcv = back_cols("ffn_up_bwd", da, transposed(g_up[i]), xm, norm_ffn[i], dx, comm=sibling_comm())
        absorb(rcv)
        if i % 2 == 0:
            h, lat, cq, ckv, q, kk, vv, o, lse = mix_saved
            w_dkv, w_uq, w_ukv, w_o = mla_w[l]
            do = _matmul("mla_out_bwd", dyb, w_o, [], grid=(nt,), a_spec=_row_spec(tm, D),
                         b_spec=pl.BlockSpec((None, OW, D), lambda i_: (0, 0, 0)), extra_specs=[],
                         out_shapes=[_sds((T, OW), BF16)], out_specs=[_row_spec(tm, OW)], dims=NT)[0]
            g_o_l = wgrad_rows("mla_out_wgrad", o, dyb, OW, D, tm).reshape(N_DEV, owc, D)
            comm, names = chip_comm()
            if i == 0:
                early = packed([jnp.stack(d_wsp, 0), jnp.stack(d_bsp, 0), jnp.concatenate(d_lng, 0), jnp.concatenate(d_lnb, 0)], early_sizes)
                comm = _merge_comm(comm, _gather_level1([early]))
            dq_pre, dk, dv, *bufs = _flash_bwd(q, kk, vv, o, do, lse, (t_cc, t_sa, t_sb), tq, comm=comm)
            stacked.update(dict(zip(names, bufs)))
            pending.append(("o", l, g_o_l))

            def kv_pre(dkb, dvb, cc, sa, sb):
                parts, dkr = [], None
                for b in range(HEADS):
                    parts += [dkb[:, b * QPAD:b * QPAD + NOPE], dvb[:, b * VDIM:(b + 1) * VDIM]]
                    piece = dkb[:, b * QPAD + NOPE:(b + 1) * QPAD].astype(F32)
                    dkr = piece if dkr is None else dkr + piece
                return jnp.concatenate(parts, axis=1), _rope_bwd(dkr, cc, sa, sb)

            dkv, dkr, *rest = _rowwise("mla_dkv_rope", kv_pre, [dk, dv, t_cc, t_sa, t_sb], grid=(T // ts,),
                                       in_specs=[_row_spec(ts, HW), _row_spec(ts, OW)] + [_row_spec(ts, 128)] * 3,
                                       out_shapes=[_sds((T, HW), BF16), _sds((T, 128), F32)], out_specs=[_row_spec(ts, HW), _row_spec(ts, 128)],
                                       comm=_gather_level2(bufs[len(names):]) if i == 0 else None)
            if i == 0:
                (gathered_early,) = rest
            g_uq_l = wgrad_rows("mla_q_wgrad", cq, dq_pre, Q_RANK, HW, tm)
            g_ukv_l = wgrad_rows("mla_kv_wgrad", ckv, dkv, KV_RANK, HW, tm)
            pending.append(("uq", l, g_uq_l.reshape(Q_RANK, HEADS, QPAD)[:, :, :NOPE + ROPE].transpose(1, 0, 2)))
            pending.append(("ukv", l, g_ukv_l.reshape(KV_RANK, HEADS, NOPE + VDIM).transpose(1, 0, 2)))
            dcq = _matmul("mla_q_bwd", dq_pre, w_uq, [], grid=(nt,), a_spec=_row_spec(tm, HW),
                          b_spec=pl.BlockSpec((None, Q_RANK, HW), lambda i_: (0, 0, 0)), extra_specs=[],
                          out_shapes=[_sds((T, Q_RANK), F32)], out_specs=[_row_spec(tm, Q_RANK)], dims=NT)[0]
            dckv = _matmul("mla_kv_bwd", dkv, w_ukv, [], grid=(nt,), a_spec=_row_spec(tm, HW),
                           b_spec=pl.BlockSpec((None, KV_RANK, HW), lambda i_: (0, 0, 0)), extra_specs=[],
                           out_shapes=[_sds((T, KV_RANK), F32)], out_specs=[_row_spec(tm, KV_RANK)], dims=NT)[0]

            def latent_bwd(la, qn, kvn, dq_, dkv_, dkr_):
                dcq_raw, dqn = _rms_bwd(la[:, :Q_RANK], qn, dq_)
                dckv_raw, dkvn = _rms_bwd(la[:, Q_RANK:Q_RANK + KV_RANK], kvn, dkv_)
                return jnp.concatenate([dcq_raw, dckv_raw, dkr_], axis=1), dqn, dkvn

            dlat, d_qn[l], d_kvn[l] = _rowwise(
                "mla_latent_bwd", latent_bwd, [lat, mla_q_norm[l].reshape(1, Q_RANK), mla_kv_norm[l].reshape(1, KV_RANK), dcq, dckv, dkr],
                grid=(nt,), in_specs=[_row_spec(tm, LAT_PAD), _const_spec((1, Q_RANK)), _const_spec((1, KV_RANK)),
                                      _row_spec(tm, Q_RANK), _row_spec(tm, KV_RANK), _row_spec(tm, 128)],
                out_shapes=[_sds((T, LAT_PAD), BF16), _sds((1, Q_RANK), F32), _sds((1, KV_RANK), F32)],
                out_specs=[_row_spec(tm, LAT_PAD), _const_spec((1, Q_RANK)), _const_spec((1, KV_RANK))], n_acc=2)
            g_dkv_l = wgrad_rows("mla_down_wgrad", h, dlat, D, LAT_PAD, tm)
            pending.append(("dkv", l, g_dkv_l[:, :LAT].reshape(N_DEV, dc, LAT)))
            dx, dyb, d_norm_mix[i] = _matmul(
                "mla_down_bwd", dlat, w_dkv, [xs_i, norm_mix[i].reshape(1, D), dx], grid=(nt,), a_spec=_row_spec(tm, LAT_PAD),
                b_spec=pl.BlockSpec((None, D, LAT_PAD), lambda i_: (0, 0, 0)), extra_specs=[_row_spec(tm, D), _const_spec((1, D)), _row_spec(tm, D)],
                out_shapes=[_sds((T, D), F32), _sds((T, D), BF16), _sds((1, D), F32)],
                out_specs=[_row_spec(tm, D), _row_spec(tm, D), _const_spec((1, D))], dims=NT, epilogue=norm_bwd_epilogue, n_sum=1)
        else:
            h, gp, ge, gate = mix_saved
            (dgate,) = back_rows("sgu_out_bwd", dyb, g_out[l], ec, [], None)
            pending.append(("out", l, wgrad_rows("sgu_out_wgrad", gate, dyb, ec, D, tb).reshape(N_DEV, ec, D)))
            dz, d_wsp[l], d_bsp[l], d_lng[l], d_lnb[l] = _sgu_mid_bwd(ge, gp, dgate, ln_g_full[l], ln_b_full[l], sgu_w_spatial[l], b_sp[l], 2)
            pending.append(("in", l, wgrad_cols("sgu_in_wgrad", h, dz, e2c)))
            dx, dyb, d_norm_mix[i], *rcv = back_cols("sgu_in_bwd", dz, transposed(g_in[l]), xs_i, norm_mix[i], dx, comm=sibling_comm())
            absorb(rcv)
    grad_x = dx.reshape(1, T, D)

    last_comm, last_names = chip_comm()
    late_g = [jnp.concatenate(d_norm_mix, 0), jnp.concatenate(d_norm_ffn, 0), d_final, jnp.concatenate(d_qn, 0), jnp.concatenate(d_kvn, 0)]
    late_w = [norm_mix, norm_ffn, final_norm, mla_q_norm, mla_kv_norm]
    late_m = [m_norm_mix, m_norm_ffn, m_final_norm, m_mla_q_norm, m_mla_kv_norm]
    late_v = [v_norm_mix, v_norm_ffn, v_final_norm, v_mla_q_norm, v_mla_kv_norm]
    late_sizes = [pad_to(g.size // 128, 8) for g in late_g]
    late_rows = sum(late_sizes)

    def adam_big(parts, w, m, v):
        lyr, rws, cls = w.shape
        rt = _tile(rws, 512)

        def fn(p, w_, m_, v_):
            g = (p[0].astype(F32) + p[1].astype(F32)) + (p[2].astype(F32) + p[3].astype(F32))
            return (g, *_adam(w_, g, m_, v_))

        spec = pl.BlockSpec((None, rt, cls), lambda l_, i_: (l_, i_, 0))
        return _rowwise("adam_large", fn, [parts, w, m, v], grid=(lyr, rws // rt),
                        in_specs=[pl.BlockSpec((N_CHIP, None, rt, cls), lambda l_, i_: (0, l_, i_, 0)), spec, spec, spec],
                        out_shapes=[_sds(w.shape, F32)] * 4, out_specs=[spec] * 4)

    stacked.update(dict(zip(last_names, _comm_call("grad_chip_exchange", last_comm))))
    (gathered_late,) = _all_gather("gather_small_grads", [packed(late_g, late_sizes)])
    big = {}
    big["in"] = adam_big(stacked["in"], sgu_w_in, m_sgu_w_in, v_sgu_w_in)
    big["up"] = adam_big(stacked["up"], ffn_w_up, m_ffn_w_up, v_ffn_w_up)
    big["down"] = adam_big(stacked["down"], ffn_w_down, m_ffn_w_down, v_ffn_w_down)
    big["out"] = adam_big(stacked["out"], sgu_w_out, m_sgu_w_out, v_sgu_w_out)
    big["dkv"] = adam_big(stacked["dkv"], mla_w_dkv, m_mla_w_dkv, v_mla_w_dkv)
    big["uq"] = adam_big(stacked["uq"], mla_w_uq, m_mla_w_uq, v_mla_w_uq)
    big["ukv"] = adam_big(stacked["ukv"], mla_w_ukv, m_mla_w_ukv, v_mla_w_ukv)
    big["o"] = adam_big(stacked["o"], mla_w_o, m_mla_w_o, v_mla_w_o)
    big_res = [big[nm][:4] for nm in ("dkv", "uq", "ukv", "o", "in", "out", "up", "down")]

    def sum8(p):
        return ((p[0] + p[1]) + (p[2] + p[3])) + ((p[4] + p[5]) + (p[6] + p[7]))

    def adam_packed(name, gathered, ws, ms, vs, sizes, rows, tile):
        spec = _row_spec(tile, 128)
        return _rowwise(name, lambda p, w_, m_, v_: (sum8(p), *_adam(w_, sum8(p), m_, v_)),
                        [gathered, packed(ws, sizes), packed(ms, sizes), packed(vs, sizes)], grid=(rows // tile,),
                        in_specs=[pl.BlockSpec((N_DEV, tile, 128), lambda i_: (0, i_, 0)), spec, spec, spec],
                        out_shapes=[_sds((rows, 128), F32)] * 4, out_specs=[spec] * 4)

    late_res = adam_packed("adam_small", gathered_late, late_w, late_m, late_v, late_sizes, late_rows, late_rows)
    early_res = adam_packed("adam_spatial", gathered_early, [sgu_w_spatial, sgu_b_spatial], [m_sgu_w_spatial, m_sgu_b_spatial],
                            [v_sgu_w_spatial, v_sgu_b_spatial], early_sizes[:2], early_rep, SMALL_ROWS)

    def unpack(res, sizes, k, like):
        off = sum(sizes[:k])
        return res[off:off + like.size // 128].reshape(like.shape)

    my_b = 4 * lax.axis_index("x") + 2 * lax.axis_index("y") + lax.axis_index("c")
    ln_w = jnp.concatenate([sgu_ln_g, sgu_ln_b], 0)
    ln_m = jnp.concatenate([m_sgu_ln_g, m_sgu_ln_b], 0)
    ln_v = jnp.concatenate([v_sgu_ln_g, v_sgu_ln_b], 0)
    ln_all = jnp.concatenate([gathered_early[:, early_rep:early_rep + n_sgu * E // 128], gathered_early[:, early_rep + n_ln:early_rep + n_ln + n_sgu * E // 128]], axis=1)
    ln_mine = lax.dynamic_slice_in_dim(ln_all.reshape(N_DEV, 2 * n_sgu, N_DEV, ec), my_b, 1, axis=2).reshape(N_DEV, 2 * n_sgu, ec)
    ln_g_, ln_d, ln_m2, ln_v2 = _rowwise(
        "adam_ln", lambda p, w_, m_, v_: (sum8(p), *_adam(w_, sum8(p), m_, v_)), [ln_mine, ln_w, ln_m, ln_v], grid=(1,),
        in_specs=[_const_spec(ln_mine.shape), _const_spec(ln_w.shape), _const_spec(ln_w.shape), _const_spec(ln_w.shape)],
        out_shapes=[_sds(ln_w.shape, F32)] * 4, out_specs=[_const_spec(ln_w.shape)] * 4)

    def family(pos):
        ln = [ln_g_, ln_d, ln_m2, ln_v2][pos]
        late = [unpack(late_res[pos], late_sizes, k, w_) for k, w_ in enumerate(late_w)]
        w_sp_, b_sp_ = unpack(early_res[pos], early_sizes, 0, sgu_w_spatial), unpack(early_res[pos], early_sizes, 1, sgu_b_spatial)
        bigs = [res[pos] for res in big_res]
        return [late[0], late[1], late[2], bigs[0], late[3], late[4], bigs[1], bigs[2], bigs[3],
                bigs[4], ln[:n_sgu], ln[n_sgu:], w_sp_, b_sp_, bigs[5], bigs[6], bigs[7]]

    return (loss, grad_x, *family(0), *family(1), *family(2), *family(3))
```

```python
import math

import jax
import jax.numpy as jnp
from jax import lax
from jax.experimental import pallas as pl
from jax.experimental.pallas import tpu as pltpu

F32 = jnp.float32
BF16 = jnp.bfloat16
MESH = pl.DeviceIdType.MESH

N_DEV = 8
N_CHIP = 4
HEADS = 8
NOPE = 128
ROPE = 64
VDIM = 128
QPAD = 256
Q_RANK = 256
KV_RANK = 128
LAT = Q_RANK + KV_RANK + ROPE
LAT_PAD = 512
ROPE_THETA = 10000.0
SGU_CHUNK = 128
SGU_GROUPS = 8
NORM_EPS = 1e-6
LN_EPS = 1e-5
ADAM_LR = 0.001
ADAM_B1 = 0.9
ADAM_B2 = 0.999
ADAM_EPS = 1e-08
ADAM_WD = 0.01
ADAM_STEP = 10
ATTN_SCALE = (NOPE + ROPE) ** -0.5
NEG = -1e30
EXP2_SCALE = ATTN_SCALE * math.log2(math.e)
VMEM_LIMIT = 56 * 1024 * 1024
SMALL_ROWS = 256

NN = (((1,), (0,)), ((), ()))
NT = (((1,), (1,)), ((), ()))
TN = (((0,), (0,)), ((), ()))
ANY = pl.BlockSpec(memory_space=pl.ANY)


def _pcall(body, **kw):
    return pl.pallas_call(body, **kw)


def _params(n_grid, side_effects=False):
    return pltpu.CompilerParams(dimension_semantics=("arbitrary",) * n_grid, vmem_limit_bytes=VMEM_LIMIT, has_side_effects=side_effects)


def _sds(shape, dtype):
    return jax.ShapeDtypeStruct(tuple(shape), dtype)


def _tile(n, want):
    t = min(n, want)
    assert n % t == 0, (n, want)
    return t


class _Comm:
    def __init__(self, operands, out_shapes, aliases, scratch, start, finish):
        self.operands, self.out_shapes, self.aliases, self.scratch = operands, out_shapes, aliases, scratch
        self.start, self.finish = start, finish


def _merge_comm(first, second):
    n_in, n_out, n_sc = len(first.operands), len(first.out_shapes), len(first.scratch)
    aliases = dict(first.aliases)
    aliases.update({n_in + k: n_out + v for k, v in second.aliases.items()})

    def start(ins, outs, sems):
        first.start(ins[:n_in], outs[:n_out], sems[:n_sc])
        second.start(ins[n_in:], outs[n_out:], sems[n_sc:])

    def finish(ins, outs, sems):
        first.finish(ins[:n_in], outs[:n_out], sems[:n_sc])
        second.finish(ins[n_in:], outs[n_out:], sems[n_sc:])

    return _Comm([*first.operands, *second.operands], [*first.out_shapes, *second.out_shapes], aliases,
                 [*first.scratch, *second.scratch], start, finish)


def _place():
    return lax.axis_index("x"), lax.axis_index("y"), lax.axis_index("c")


def _other_chips(x, y):
    return [(1 - x, y), (x, 1 - y), (1 - x, 1 - y)]


def _dev_index(dev):
    return 4 * dev[0] + 2 * dev[1] + dev[2]


def _comm_call(name, comm):
    c_in, c_out = len(comm.operands), len(comm.out_shapes)

    def body(*refs):
        ins, outs, sems = refs[:c_in], refs[c_in:c_in + c_out], refs[c_in + c_out:]
        comm.start(ins, outs, sems)
        comm.finish(ins, outs, sems)

    return _pcall(body, name=name, in_specs=[ANY] * c_in, out_specs=[ANY] * c_out, out_shape=comm.out_shapes,
                  scratch_shapes=comm.scratch, input_output_aliases=dict(comm.aliases),
                  compiler_params=pltpu.CompilerParams(has_side_effects=True))(*comm.operands)


def _call(name, body, operands, in_specs, out_shapes, out_specs, scratch, grid, comm=None):
    if comm is None:
        return _pcall(body, name=name, grid=grid, in_specs=in_specs, out_specs=out_specs, out_shape=out_shapes,
                      scratch_shapes=scratch, compiler_params=_params(len(grid)))(*operands)
    n_in, n_out, n_sc = len(operands), len(out_shapes), len(scratch)
    c_in, c_out = len(comm.operands), len(comm.out_shapes)

    def hosted(*refs):
        ins, cins = refs[:n_in], refs[n_in:n_in + c_in]
        o0 = n_in + c_in
        outs, couts = refs[o0:o0 + n_out], refs[o0 + n_out:o0 + n_out + c_out]
        rest = refs[o0 + n_out + c_out:]
        sc, csems = rest[:n_sc], rest[n_sc:]
        first = pl.program_id(0) == 0
        last = pl.program_id(0) == grid[0] - 1
        for d in range(1, len(grid)):
            first = jnp.logical_and(first, pl.program_id(d) == 0)
            last = jnp.logical_and(last, pl.program_id(d) == grid[d] - 1)

        @pl.when(first)
        def _():
            comm.start(cins, couts, csems)

        body(*ins, *outs, *sc)

        @pl.when(last)
        def _():
            comm.finish(cins, couts, csems)

    return _pcall(hosted, name=name, grid=grid, in_specs=[*in_specs, *[ANY] * c_in], out_specs=[*out_specs, *[ANY] * c_out],
                  out_shape=[*out_shapes, *comm.out_shapes], scratch_shapes=[*scratch, *comm.scratch],
                  input_output_aliases={n_in + k: n_out + v for k, v in comm.aliases.items()},
                  compiler_params=_params(len(grid), side_effects=True))(*operands, *comm.operands)


def _gather_level1(shards):
    n = len(shards)

    def copies(ins, outs, sems):
        send_sems, recv_sems, local_sems = sems
        x, y, c = _place()
        me, sibling = (x, y, c), (x, y, 1 - c)
        chips = _other_chips(x, y)

        def copy(a, k, block, to, src=None):
            slot = outs[a].at[_dev_index(block)]
            return pltpu.make_async_remote_copy(src_ref=slot if src is None else src, dst_ref=slot, send_sem=send_sems.at[a, k],
                                                recv_sem=recv_sems.at[a, k], device_id=to, device_id_type=MESH)

        mine = [pltpu.make_async_copy(ins[a], outs[a].at[_dev_index(me)], local_sems.at[a]) for a in range(n)]
        sends = [copy(a, 1 + j, me, (*chip, c), src=ins[a]) for j, chip in enumerate(chips) for a in range(n)]
        sends += [copy(a, 0, me, sibling, src=ins[a]) for a in range(n)]
        recvs = [copy(a, 1 + j, (*chip, c), me) for j, chip in enumerate(chips) for a in range(n)]
        recvs += [copy(a, 0, sibling, me) for a in range(n)]
        return mine, sends, recvs

    def start(ins, outs, sems):
        mine, sends, _ = copies(ins, outs, sems)
        for cp in mine + sends:
            cp.start()

    def finish(ins, outs, sems):
        mine, sends, recvs = copies(ins, outs, sems)
        for cp in recvs:
            cp.wait_recv()
        for cp in sends:
            cp.wait_send()
        for cp in mine:
            cp.wait()

    return _Comm(shards, [_sds((N_DEV, *a.shape), a.dtype) for a in shards], {},
                 [pltpu.SemaphoreType.DMA((n, 4)), pltpu.SemaphoreType.DMA((n, 4)), pltpu.SemaphoreType.DMA((n,))], start, finish)


def _gather_level2(bufs):
    n = len(bufs)

    def copies(outs, sems):
        send_sems, recv_sems = sems
        x, y, c = _place()
        sibling = (x, y, 1 - c)
        sends, recvs = [], []
        for j, chip in enumerate(_other_chips(x, y)):
            for a in range(n):
                have, want = outs[a].at[_dev_index((*chip, c))], outs[a].at[_dev_index((*chip, 1 - c))]
                sends.append(pltpu.make_async_remote_copy(src_ref=have, dst_ref=have, send_sem=send_sems.at[a, j], recv_sem=recv_sems.at[a, j],
                                                          device_id=sibling, device_id_type=MESH))
                recvs.append(pltpu.make_async_remote_copy(src_ref=want, dst_ref=want, send_sem=send_sems.at[a, j], recv_sem=recv_sems.at[a, j],
                                                          device_id=sibling, device_id_type=MESH))
        return sends, recvs

    def start(ins, outs, sems):
        for cp in copies(outs, sems)[0]:
            cp.start()

    def finish(ins, outs, sems):
        sends, recvs = copies(outs, sems)
        for cp in recvs:
            cp.wait_recv()
        for cp in sends:
            cp.wait_send()

    return _Comm(bufs, [_sds(b.shape, b.dtype) for b in bufs], {a: a for a in range(n)},
                 [pltpu.SemaphoreType.DMA((n, 3)), pltpu.SemaphoreType.DMA((n, 3))], start, finish)


def _all_gather(name, arrays):
    n = len(arrays)

    def body(*refs):
        ins = refs[:n]
        outs = refs[n:2 * n]
        send_sems, recv_sems, local_sems = refs[2 * n:]
        x, y, c = _place()
        me, sibling = (x, y, c), (x, y, 1 - c)
        chips = _other_chips(x, y)

        def copy(a, k, block, to, src=None):
            slot = outs[a].at[_dev_index(block)]
            return pltpu.make_async_remote_copy(src_ref=slot if src is None else src, dst_ref=slot, send_sem=send_sems.at[a, k],
                                                recv_sem=recv_sems.at[a, k], device_id=to, device_id_type=MESH)

        mine = [pltpu.make_async_copy(ins[a], outs[a].at[_dev_index(me)], local_sems.at[a]) for a in range(n)]
        for cp in mine:
            cp.start()
        first = []
        for j, chip in enumerate(chips):
            first += [copy(a, 1 + j, me, (*chip, c), src=ins[a]) for a in range(n)]
        first += [copy(a, 0, me, sibling, src=ins[a]) for a in range(n)]
        for cp in first:
            cp.start()
        passed = []
        for j, chip in enumerate(chips):
            for a in range(n):
                copy(a, 1 + j, (*chip, c), me).wait_recv()
                fwd = copy(a, 4 + j, (*chip, c), sibling)
                fwd.start()
                passed.append(fwd)
        for a in range(n):
            copy(a, 0, sibling, me).wait_recv()
            for j, chip in enumerate(chips):
                copy(a, 4 + j, (*chip, 1 - c), me).wait_recv()
        for cp in first + passed:
            cp.wait_send()
        for cp in mine:
            cp.wait()

    return _pcall(
        body, name=name, in_specs=[ANY] * n, out_specs=[ANY] * n,
        out_shape=[_sds((N_DEV, *a.shape), a.dtype) for a in arrays],
        scratch_shapes=[pltpu.SemaphoreType.DMA((n, 7)), pltpu.SemaphoreType.DMA((n, 7)), pltpu.SemaphoreType.DMA((n,))],
        compiler_params=pltpu.CompilerParams(has_side_effects=True),
    )(*arrays)


def _sibling_exchange(grads):
    n = len(grads)

    def start(ins, outs, sems):
        send_sems, recv_sems = sems
        x, y, c = _place()
        for a in range(n):
            for ch in range(N_CHIP):
                pltpu.make_async_remote_copy(src_ref=ins[a].at[2 * ch + 1 - c], dst_ref=outs[a].at[ch], send_sem=send_sems.at[a],
                                             recv_sem=recv_sems.at[a], device_id=(x, y, 1 - c), device_id_type=MESH).start()

    def finish(ins, outs, sems):
        send_sems, recv_sems = sems
        x, y, c = _place()
        for a in range(n):
            pltpu.make_async_remote_copy(src_ref=outs[a], dst_ref=outs[a], send_sem=send_sems.at[a], recv_sem=recv_sems.at[a],
                                         device_id=(x, y, 1 - c), device_id_type=MESH).wait()

    return _Comm(grads, [_sds((N_CHIP, *g.shape[1:]), g.dtype) for g in grads], {},
                 [pltpu.SemaphoreType.DMA((n,)), pltpu.SemaphoreType.DMA((n,))], start, finish)


def _chip_exchange(parts, slots, layers, stacked):
    n = len(parts)
    names = []
    for nm, _ in slots:
        if nm not in names:
            names.append(nm)
    shapes = {nm: _sds((N_CHIP, layers[nm], *parts[a].shape[1:]), parts[a].dtype) for a, (nm, _) in enumerate(slots)}
    kept = [nm for nm in names if stacked.get(nm) is not None]
    aliases = {n + k: names.index(nm) for k, nm in enumerate(kept)}

    def copies(ins, outs, sems):
        send_sems, recv_sems, local_sems = sems
        x, y, c = _place()
        mine = 2 * x + y
        local, sends, recvs = [], [], []
        for a, (nm, l) in enumerate(slots):
            buf = outs[names.index(nm)]
            local.append(pltpu.make_async_copy(ins[a].at[mine], buf.at[mine, l], local_sems.at[a]))
            for j, chip in enumerate(_other_chips(x, y)):
                theirs = buf.at[2 * chip[0] + chip[1], l]
                sends.append(pltpu.make_async_remote_copy(src_ref=ins[a].at[2 * chip[0] + chip[1]], dst_ref=buf.at[mine, l], send_sem=send_sems.at[a, j],
                                                          recv_sem=recv_sems.at[a, j], device_id=(*chip, c), device_id_type=MESH))
                recvs.append(pltpu.make_async_remote_copy(src_ref=theirs, dst_ref=theirs, send_sem=send_sems.at[a, j],
                                                          recv_sem=recv_sems.at[a, j], device_id=(*chip, c), device_id_type=MESH))
        return local, sends, recvs

    def start(ins, outs, sems):
        local, sends, _ = copies(ins, outs, sems)
        for cp in local + sends:
            cp.start()

    def finish(ins, outs, sems):
        local, sends, recvs = copies(ins, outs, sems)
        for cp in recvs:
            cp.wait_recv()
        for cp in sends:
            cp.wait_send()
        for cp in local:
            cp.wait()

    comm = _Comm([*parts, *[stacked[nm] for nm in kept]], [shapes[nm] for nm in names], aliases,
                 [pltpu.SemaphoreType.DMA((n, 3)), pltpu.SemaphoreType.DMA((n, 3)), pltpu.SemaphoreType.DMA((n,))], start, finish)
    return comm, names


def _matmul(name, a, b, extras, *, grid, a_spec, b_spec, extra_specs, out_shapes, out_specs, dims, k_axis=None, nk=1,
            acc_shape=None, epilogue=None, comm=None, n_sum=0, write=None):
    n_extra = len(extras)
    n_out = len(out_shapes)

    def body(*refs):
        a_ref, b_ref = refs[0], refs[1]
        ex = refs[2:2 + n_extra]
        outs = refs[2 + n_extra:2 + n_extra + n_out]
        prod = lax.dot_general(a_ref[...], b_ref[...], dims, preferred_element_type=F32)

        def finish(acc):
            if write is not None:
                write(outs, acc, *[e[...] for e in ex])
                return
            res = epilogue(acc, *[e[...] for e in ex]) if epilogue is not None else (acc,)
            first = None
            for d in range(len(grid)):
                if d != k_axis:
                    here = pl.program_id(d) == 0
                    first = here if first is None else jnp.logical_and(first, here)
            for idx, (o, r) in enumerate(zip(outs, res)):
                if idx < n_out - n_sum:
                    o[...] = r.astype(o.dtype)
                else:
                    @pl.when(first)
                    def _(o=o, r=r):
                        o[...] = r.astype(o.dtype)

                    @pl.when(jnp.logical_not(first))
                    def _(o=o, r=r):
                        o[...] += r.astype(o.dtype)

        if k_axis is None:
            finish(prod)
        else:
            acc_ref = refs[-1]
            k = pl.program_id(k_axis)

            @pl.when(k == 0)
            def _():
                acc_ref[...] = prod

            @pl.when(k > 0)
            def _():
                acc_ref[...] += prod

            @pl.when(k == nk - 1)
            def _():
                finish(acc_ref[...])

    scratch = [] if k_axis is None else [pltpu.VMEM(acc_shape, F32)]
    return _call(name, body, [a, b, *extras], [a_spec, b_spec, *extra_specs], list(out_shapes), list(out_specs), scratch, grid, comm)


def _rowwise(name, fn, operands, *, grid, in_specs, out_shapes, out_specs, n_acc=0, grid_spec_prefetch=None, comm=None):
    n_in = len(operands)
    n_out = len(out_shapes)
    n_pre = 0 if grid_spec_prefetch is None else 1

    def body(*refs):
        refs = refs[n_pre:]
        ins = refs[:n_in]
        outs = refs[n_in:n_in + n_out]
        res = fn(*[r[...] for r in ins])
        if not isinstance(res, (tuple, list)):
            res = (res,)
        first = pl.program_id(0) == 0
        for d in range(1, len(grid)):
            first = jnp.logical_and(first, pl.program_id(d) == 0)
        for idx, (o, r) in enumerate(zip(outs, res)):
            if idx < n_out - n_acc:
                o[...] = r.astype(o.dtype)
            else:
                @pl.when(first)
                def _(o=o, r=r):
                    o[...] = r.astype(o.dtype)

                @pl.when(jnp.logical_not(first))
                def _(o=o, r=r):
                    o[...] += r.astype(o.dtype)

    if comm is not None:
        return _call(name, body, list(operands), list(in_specs), list(out_shapes), list(out_specs), [], grid, comm)
    if grid_spec_prefetch is None:
        return _pcall(body, name=name, grid=grid, in_specs=in_specs, out_specs=out_specs, out_shape=out_shapes,
                      compiler_params=_params(len(grid)))(*operands)
    gs = pltpu.PrefetchScalarGridSpec(num_scalar_prefetch=1, grid=grid, in_specs=in_specs, out_specs=out_specs)
    return _pcall(body, name=name, grid_spec=gs, out_shape=out_shapes,
                  compiler_params=_params(len(grid)))(grid_spec_prefetch, *operands)


def _row_spec(tm, w):
    return pl.BlockSpec((tm, w), lambda i: (i, 0))


def _const_spec(shape):
    nd = len(shape)
    return pl.BlockSpec(tuple(shape), lambda *_: (0,) * nd)


def _rms_fwd(x, g):
    r = lax.rsqrt(jnp.mean(x * x, axis=-1, keepdims=True) + NORM_EPS)
    return x * r * g


def _rms_bwd(x, g, dy):
    r = lax.rsqrt(jnp.mean(x * x, axis=-1, keepdims=True) + NORM_EPS)
    xh = x * r
    u = dy * g
    dx = r * (u - xh * jnp.mean(u * xh, axis=-1, keepdims=True))
    dg = jnp.sum(dy * xh, axis=0, keepdims=True)
    return dx, dg


def _gelu_and_grad(z):
    cdf = 0.5 * (1.0 + lax.erf(z * (2.0 ** -0.5)))
    return cdf + z * jnp.exp(-0.5 * z * z) * ((2.0 * math.pi) ** -0.5), z * cdf


def _rope_fwd(x, cc, sa, sb):
    return x * cc + pltpu.roll(x, 96, 1) * sa + pltpu.roll(x, 32, 1) * sb


def _rope_bwd(d, cc, sa, sb):
    return d * cc + pltpu.roll(d * sa, 32, 1) + pltpu.roll(d * sb, 96, 1)


def _adam(w, g, m, v):
    m = ADAM_B1 * m + (1.0 - ADAM_B1) * g
    v = ADAM_B2 * v + (1.0 - ADAM_B2) * (g * g)
    m_hat = m / (1.0 - ADAM_B1 ** ADAM_STEP)
    v_hat = v / (1.0 - ADAM_B2 ** ADAM_STEP)
    delta = -ADAM_LR * (m_hat / (jnp.sqrt(v_hat) + ADAM_EPS) + ADAM_WD * w)
    return delta, m, v


def _flash_fwd(q, k, vt, tq, comm=None):
    h, t = vt.shape[0], q.shape[0]
    nq = t // tq

    chunk_blocks = [c for c in (4, 2, 1) if c < nq]

    def body(q_ref, k_ref, vt_ref, o_ref, lse_ref, m_ref, l_ref, acc_ref):
        qi = pl.program_id(1)
        m_ref[...] = jnp.full((1, tq), NEG, F32)
        l_ref[...] = jnp.zeros((1, tq), F32)
        acc_ref[...] = jnp.zeros((VDIM, tq), F32)

        def update(kb0, nblk, masked):
            kb = k_ref[pl.ds(pl.multiple_of(kb0 * tq, tq), nblk * tq), :]
            st = lax.dot_general(kb, q_ref[...], NT, preferred_element_type=F32)
            if masked:
                key = lax.broadcasted_iota(jnp.int32, (tq, tq), 0)
                qry = lax.broadcasted_iota(jnp.int32, (tq, tq), 1)
                st = jnp.where(key <= qry, st, NEG)
            m_old = m_ref[...]
            m_new = jnp.maximum(m_old, jnp.max(st, axis=0, keepdims=True))
            alpha = jnp.exp2((m_old - m_new) * EXP2_SCALE)
            pt = jnp.exp2((st - m_new) * EXP2_SCALE)
            l_ref[...] = alpha * l_ref[...] + jnp.sum(pt, axis=0, keepdims=True)
            ptb = pt.astype(BF16)
            pv = lax.dot_general(vt_ref[kb0], ptb[:tq], NN, preferred_element_type=F32)
            for j in range(1, nblk):
                pv += lax.dot_general(vt_ref[kb0 + j], ptb[j * tq:(j + 1) * tq], NN, preferred_element_type=F32)
            acc_ref[...] = alpha * acc_ref[...] + pv
            m_ref[...] = m_new

        start = jnp.int32(0)
        for c in chunk_blocks:
            take = (qi & c) != 0

            @pl.when(take)
            def _(start=start, c=c):
                update(start, c, False)

            start = start + jnp.where(take, c, 0)
        update(qi, 1, True)
        l = l_ref[...]
        o_ref[...] = (acc_ref[...] / l).T.astype(o_ref.dtype)
        lse_ref[...] = m_ref[...] * EXP2_SCALE + jnp.log2(l)

    return _call(
        "flash_fwd", body, [q, k, vt],
        [pl.BlockSpec((tq, QPAD), lambda hh, i: (i, hh)),
         pl.BlockSpec((t, QPAD), lambda hh, i: (0, hh)),
         pl.BlockSpec((None, nq, VDIM, tq), lambda hh, i: (hh, 0, 0, 0))],
        [_sds((t, h * VDIM), BF16), _sds((h, nq, 1, tq), F32)],
        [pl.BlockSpec((tq, VDIM), lambda hh, i: (i, hh)),
         pl.BlockSpec((None, None, 1, tq), lambda hh, i: (hh, i, 0, 0))],
        [pltpu.VMEM((1, tq), F32), pltpu.VMEM((1, tq), F32), pltpu.VMEM((VDIM, tq), F32)], (h, nq), comm)


def _flash_bwd(q, k, v, o, do, lse, tabs, tq, comm=None):
    t = q.shape[0]
    h = q.shape[1] // QPAD
    nq = t // tq

    def body(q_ref, k_ref, v_ref, o_ref, do_ref, lse_ref, cc_ref, sa_ref, sb_ref, dq_ref, dk_out, dv_out, delta_ref, dqt_ref, dk_ref, dv_ref):
        kj = pl.program_id(1)

        @pl.when(kj == 0)
        def _():
            dqt_ref[...] = jnp.zeros_like(dqt_ref)
            ones = jnp.ones((8, VDIM), BF16)
            for qi in range(nq):
                rows = pl.ds(qi * tq, tq)
                prod = do_ref[rows, :].astype(F32) * o_ref[rows, :].astype(F32)
                hi = prod.astype(BF16)
                lo = (prod - hi.astype(F32)).astype(BF16)
                delta_ref[qi] = (lax.dot_general(ones, hi, NT, preferred_element_type=F32)
                                 + lax.dot_general(ones, lo, NT, preferred_element_type=F32))

        kb = k_ref[...]
        vb = v_ref[...]
        kbt = kb.astype(F32).T.astype(BF16)
        dk_ref[...] = jnp.zeros_like(dk_ref)
        dv_ref[...] = jnp.zeros_like(dv_ref)

        def step(q0, nblk, masked):
            rows = pl.ds(pl.multiple_of(q0 * tq, tq), nblk * tq)
            qb = q_ref[rows, :]
            dob = do_ref[rows, :]
            lse = jnp.concatenate([lse_ref[q0 + j] for j in range(nblk)], axis=1)
            delta = jnp.concatenate([delta_ref[q0 + j, pl.ds(0, 1), :] for j in range(nblk)], axis=1)
            st = lax.dot_general(kb, qb, NT, preferred_element_type=F32)
            pt = jnp.exp2(st * EXP2_SCALE - lse)
            if masked:
                key = lax.broadcasted_iota(jnp.int32, (tq, tq), 0)
                qry = lax.broadcasted_iota(jnp.int32, (tq, tq), 1)
                pt = jnp.where(key <= qry, pt, 0.0)
            dv_ref[...] += lax.dot_general(pt.astype(BF16), dob, NN, preferred_element_type=F32)
            dpt = lax.dot_general(vb, dob, NT, preferred_element_type=F32)
            dst = (pt * (dpt - delta) * ATTN_SCALE).astype(BF16)
            dk_ref[...] += lax.dot_general(dst, qb, NN, preferred_element_type=F32)
            dqt = lax.dot_general(kbt, dst, NN, preferred_element_type=F32)
            for j in range(nblk):
                dqt_ref[q0 + j] += dqt[:, j * tq:(j + 1) * tq]

        later = nq - 1 - kj
        step(kj, 1, True)
        start = kj + 1
        for c in [c for c in (1, 2, 4) if c < nq]:
            take = (later & c) != 0

            @pl.when(take)
            def _(start=start, c=c):
                step(start, c, False)

            start = start + jnp.where(take, c, 0)
        dk_out[...] = dk_ref[...].astype(BF16)
        dv_out[...] = dv_ref[...].astype(BF16)

        @pl.when(kj == nq - 1)
        def _():
            for qi in range(nq):
                rows = pl.ds(qi * tq, tq)
                d = dqt_ref[qi].T
                roped = _rope_bwd(d[:, NOPE:], cc_ref[rows, :], sa_ref[rows, :], sb_ref[rows, :])
                dq_ref[rows, :] = jnp.concatenate([d[:, :NOPE], roped], axis=1).astype(BF16)

    head_q = pl.BlockSpec((t, QPAD), lambda hh, j: (0, hh))
    head_v = pl.BlockSpec((t, VDIM), lambda hh, j: (0, hh))
    table = pl.BlockSpec((t, 128), lambda hh, j: (0, 0))
    return _call(
        "flash_bwd", body, [q, k, v, o, do, lse, *tabs],
        [head_q, pl.BlockSpec((tq, QPAD), lambda hh, j: (j, hh)), pl.BlockSpec((tq, VDIM), lambda hh, j: (j, hh)), head_v, head_v,
         pl.BlockSpec((None, nq, 1, tq), lambda hh, j: (hh, 0, 0, 0)), table, table, table],
        [_sds((t, h * QPAD), BF16), _sds((t, h * QPAD), BF16), _sds((t, h * VDIM), BF16)],
        [head_q, pl.BlockSpec((tq, QPAD), lambda hh, j: (j, hh)), pl.BlockSpec((tq, VDIM), lambda hh, j: (j, hh))],
        [pltpu.VMEM((nq, 8, tq), F32), pltpu.VMEM((nq, QPAD, tq), F32), pltpu.VMEM((tq, QPAD), F32), pltpu.VMEM((tq, VDIM), F32)], (h, nq), comm)


def _tril_bf16(w):
    row = lax.broadcasted_iota(jnp.int32, w.shape, 0)
    col = lax.broadcasted_iota(jnp.int32, w.shape, 1)
    return jnp.where(col <= row, w, 0.0).astype(BF16)


def _layer_norm_parts(v0):
    mu = jnp.mean(v0, axis=-1, keepdims=True)
    vc = v0 - mu
    rstd = lax.rsqrt(jnp.mean(vc * vc, axis=-1, keepdims=True) + LN_EPS)
    return vc * rstd, rstd


def _sgu_mid_fwd(ge, ln_g, ln_b, w_sp, b_sp, chunks_per_step):
    t, e2 = ge.shape
    e = e2 // 2
    gd = e // SGU_GROUPS
    rows = SGU_CHUNK * chunks_per_step

    def body(u_ref, v_ref, g_ref, b_ref, w_ref, bs_ref, gate_ref):
        for ck in range(chunks_per_step):
            r = pl.ds(ck * SGU_CHUNK, SGU_CHUNK)
            xh, _ = _layer_norm_parts(v_ref[r, :].astype(F32))
            v1 = (xh * g_ref[...] + b_ref[...]).astype(BF16)
            for g in range(SGU_GROUPS):
                cols = pl.ds(g * gd, gd)
                mixed = lax.dot_general(_tril_bf16(w_ref[g]), v1[:, g * gd:(g + 1) * gd], NN, preferred_element_type=F32) + bs_ref[g]
                gate_ref[r, cols] = (u_ref[r, cols].astype(F32) * mixed).astype(BF16)

    return _pcall(
        body, name="sgu_mid_fwd", grid=(t // rows,),
        in_specs=[pl.BlockSpec((rows, e), lambda i: (i, 0)), pl.BlockSpec((rows, e), lambda i: (i, 1)),
                  _const_spec((1, e)), _const_spec((1, e)), _const_spec(w_sp.shape), _const_spec(b_sp.shape)],
        out_specs=pl.BlockSpec((rows, e), lambda i: (i, 0)),
        out_shape=_sds((t, e), BF16), compiler_params=_params(1),
    )(ge, ge, ln_g, ln_b, w_sp, b_sp)


def _sgu_mid_bwd(ge, gp, dgate, ln_g, ln_b, w_sp, b_sp, chunks_per_step):
    t, e2 = ge.shape
    e = e2 // 2
    gd = e // SGU_GROUPS
    rows = SGU_CHUNK * chunks_per_step

    def body(u_ref, v_ref, zu_ref, zv_ref, dg_ref, g_ref, b_ref, w_ref, bs_ref, dz_ref, dw_ref, dbs_ref, dlg_ref, dlb_ref):
        @pl.when(pl.program_id(0) == 0)
        def _():
            dw_ref[...] = jnp.zeros_like(dw_ref)
            dbs_ref[...] = jnp.zeros_like(dbs_ref)
            dlg_ref[...] = jnp.zeros_like(dlg_ref)
            dlb_ref[...] = jnp.zeros_like(dlb_ref)

        for ck in range(chunks_per_step):
            r = pl.ds(ck * SGU_CHUNK, SGU_CHUNK)
            xh, rstd = _layer_norm_parts(v_ref[r, :].astype(F32))
            v1 = (xh * g_ref[...] + b_ref[...]).astype(BF16)
            dv1_parts = []
            for g in range(SGU_GROUPS):
                cols = pl.ds(g * gd, gd)
                wc = _tril_bf16(w_ref[g])
                v1g = v1[:, g * gd:(g + 1) * gd]
                mixed = lax.dot_general(wc, v1g, NN, preferred_element_type=F32) + bs_ref[g]
                dgate = dg_ref[r, cols].astype(F32)
                dmixed = dgate * u_ref[r, cols].astype(F32)
                du = dgate * mixed
                dz_ref[r, cols] = (du * zu_ref[r, cols].astype(F32)).astype(BF16)
                dbs_ref[g] += jnp.sum(dmixed, axis=1, keepdims=True)
                dmb = dmixed.astype(BF16)
                dwg = lax.dot_general(dmb, v1g, NT, preferred_element_type=F32)
                row = lax.broadcasted_iota(jnp.int32, dwg.shape, 0)
                col = lax.broadcasted_iota(jnp.int32, dwg.shape, 1)
                dw_ref[g] += jnp.where(col <= row, dwg, 0.0)
                dv1_parts.append(lax.dot_general(wc, dmb, TN, preferred_element_type=F32))
            dv1 = jnp.concatenate(dv1_parts, axis=1)
            dlg_ref[...] += jnp.sum(dv1 * xh, axis=0, keepdims=True)
            dlb_ref[...] += jnp.sum(dv1, axis=0, keepdims=True)
            dxh = dv1 * g_ref[...]
            dv0 = rstd * (dxh - jnp.mean(dxh, axis=-1, keepdims=True) - xh * jnp.mean(dxh * xh, axis=-1, keepdims=True))
            dz_ref[r, pl.ds(e, e)] = (dv0 * zv_ref[r, :].astype(F32)).astype(BF16)

    half0 = pl.BlockSpec((rows, e), lambda i: (i, 0))
    half1 = pl.BlockSpec((rows, e), lambda i: (i, 1))
    return _pcall(
        body, name="sgu_mid_bwd", grid=(t // rows,),
        in_specs=[half0, half1, half0, half1, half0, _const_spec((1, e)), _const_spec((1, e)), _const_spec(w_sp.shape), _const_spec(b_sp.shape)],
        out_specs=[pl.BlockSpec((rows, e2), lambda i: (i, 0)), _const_spec(w_sp.shape), _const_spec(b_sp.shape), _const_spec((1, e)), _const_spec((1, e))],
        out_shape=[_sds((t, e2), BF16), _sds(w_sp.shape, F32), _sds(b_sp.shape, F32), _sds((1, e), F32), _sds((1, e), F32)],
        compiler_params=_params(1),
    )(ge, ge, gp, gp, dgate, ln_g, ln_b, w_sp, b_sp)


def kernel(x, positions, norm_mix, norm_ffn, final_norm, mla_w_dkv, mla_q_norm, mla_kv_norm, mla_w_uq, mla_w_ukv, mla_w_o, sgu_w_in, sgu_ln_g, sgu_ln_b, sgu_w_spatial, sgu_b_spatial, sgu_w_out, ffn_w_up, ffn_w_down, loss_target, m_norm_mix, m_norm_ffn, m_final_norm, m_mla_w_dkv, m_mla_q_norm, m_mla_kv_norm, m_mla_w_uq, m_mla_w_ukv, m_mla_w_o, m_sgu_w_in, m_sgu_ln_g, m_sgu_ln_b, m_sgu_w_spatial, m_sgu_b_spatial, m_sgu_w_out, m_ffn_w_up, m_ffn_w_down, v_norm_mix, v_norm_ffn, v_final_norm, v_mla_w_dkv, v_mla_q_norm, v_mla_kv_norm, v_mla_w_uq, v_mla_w_ukv, v_mla_w_o, v_sgu_w_in, v_sgu_ln_g, v_sgu_ln_b, v_sgu_w_spatial, v_sgu_b_spatial, v_sgu_w_out, v_ffn_w_up, v_ffn_w_down):
    _, T, D = x.shape
    depth = norm_mix.shape[0]
    n_mla, n_sgu = mla_w_dkv.shape[0], sgu_w_in.shape[0]
    assert depth % 2 == 0
    FF = ffn_w_up.shape[2] * N_DEV
    E = sgu_w_out.shape[1] * N_DEV
    ffc, ec, e2c = FF // N_DEV, E // N_DEV, 2 * E // N_DEV
    dc = D // N_DEV
    OW = HEADS * VDIM
    HW = HEADS * QPAD
    owc = OW // N_DEV
    tm = _tile(T, 1024)
    tb = _tile(T, 4096)
    tk = _tile(T, 512)
    tq = _tile(T, 512)
    ts = _tile(T, 256)
    nt = T // tm
    x2 = x.reshape(T, D)
    tgt = loss_target.reshape(T, D)
    cidx = lax.axis_index("c").astype(jnp.int32).reshape(1)

    ln_local = jnp.concatenate([sgu_ln_g, sgu_ln_b, jnp.zeros((8 - 2 * n_sgu, ec), F32)], axis=0)
    mla_sh = [[w[l].astype(BF16) for w in (mla_w_dkv, mla_w_uq, mla_w_ukv, mla_w_o)] for l in range(n_mla)]

    def mla_layouts(g_dkv, g_uq, g_ukv, g_o):
        w_dkv = jnp.pad(g_dkv.reshape(1, D, LAT), ((0, 0), (0, 0), (0, LAT_PAD - LAT)))
        w_uq = jnp.pad(g_uq, ((0, 0), (0, 0), (0, QPAD - NOPE - ROPE))).transpose(1, 0, 2).reshape(1, Q_RANK, HEADS * QPAD)
        w_ukv = g_ukv.transpose(1, 0, 2).reshape(1, KV_RANK, HEADS * (NOPE + VDIM))
        return w_dkv, w_uq, w_ukv, g_o.reshape(1, HEADS * VDIM, D)

    mla_w = [None] * n_mla
    mla_w[0] = mla_layouts(*_all_gather("gather_first_weights", mla_sh[0]))
    small_later = [a for l in range(1, n_mla) for a in mla_sh[l]] + [ln_local]
    ln_g_full, ln_b_full = [None] * n_sgu, [None] * n_sgu
    b_sp = sgu_b_spatial.reshape(n_sgu, SGU_GROUPS, SGU_CHUNK, 1)
    up_sh = [ffn_w_up[i].astype(BF16) for i in range(depth)]
    down_sh = [ffn_w_down[i].astype(BF16) for i in range(depth)]
    in_sh = [sgu_w_in[l].astype(BF16) for l in range(n_sgu)]
    out_sh = [sgu_w_out[l].astype(BF16) for l in range(n_sgu)]
    g_up, g_down, g_in, g_out = [None] * depth, [None] * depth, [None] * n_sgu, [None] * n_sgu

    inv_freq = ROPE_THETA ** (-jnp.arange(0, ROPE, 2, dtype=F32) / ROPE)
    zeros32 = jnp.zeros((ROPE // 2,), F32)
    inv128 = jnp.concatenate([inv_freq, inv_freq, zeros32, zeros32]).reshape(1, 128)
    sel_a = jnp.concatenate([-jnp.ones((32,), F32), zeros32, zeros32, zeros32]).reshape(1, 128)
    sel_b = jnp.concatenate([zeros32, jnp.ones((32,), F32), zeros32, zeros32]).reshape(1, 128)
    sel_c = jnp.concatenate([jnp.ones((64,), F32), zeros32, zeros32]).reshape(1, 128)

    def rope_tables(pos, inv, sa, sb, sc):
        ang = pos.astype(F32) * inv
        cs, sn = jnp.cos(ang), jnp.sin(ang)
        return cs * sc, sn * sa, sn * sb

    t_cc, t_sa, t_sb = _rowwise(
        "rope_tables", rope_tables, [positions.reshape(T, 1), inv128, sel_a, sel_b, sel_c], grid=(nt,),
        in_specs=[_row_spec(tm, 1)] + [_const_spec((1, 128))] * 4,
        out_shapes=[_sds((T, 128), F32)] * 3, out_specs=[_row_spec(tm, 128)] * 3)
    tab_specs = [_row_spec(tm, 128)] * 3

    def rmsnorm(xv, g):
        return _rowwise("rmsnorm", lambda a, gg: _rms_fwd(a, gg), [xv, g.reshape(1, D)], grid=(nt,),
                        in_specs=[_row_spec(tm, D), _const_spec((1, D))], out_shapes=_sds((T, D), BF16), out_specs=_row_spec(tm, D))

    def proj_cols(name, h, gw, nc, epilogue, n_out, comm=None):
        return _matmul(name, h, gw, [], grid=(N_DEV, T // tb),
                       a_spec=pl.BlockSpec((tb, D), lambda j, i: (i, 0)),
                       b_spec=pl.BlockSpec((None, D, nc), lambda j, i: (j, 0, 0)), extra_specs=[],
                       out_shapes=[_sds((T, nc * N_DEV), BF16)] * n_out, out_specs=[pl.BlockSpec((tb, nc), lambda j, i: (i, j))] * n_out,
                       dims=NN, epilogue=epilogue, comm=comm)

    def residual_norm(acc, xr, g):
        xn = acc + xr
        return xn, _rms_fwd(xn, g)

    def proj_rows_residual(name, a, gw, xres, g_next, comm=None):
        kk_ = a.shape[1]
        return _matmul(name, a, gw.reshape(kk_, D), [xres, g_next.reshape(1, D)], grid=(T // tk,),
                       a_spec=_row_spec(tk, kk_), b_spec=_const_spec((kk_, D)), extra_specs=[_row_spec(tk, D), _const_spec((1, D))],
                       out_shapes=[_sds((T, D), F32), _sds((T, D), BF16)], out_specs=[_row_spec(tk, D)] * 2,
                       dims=NN, epilogue=residual_norm, comm=comm)

    def back_rows(name, dy, gw, kc, extras, epilogue, comm=None):
        return _matmul(name, dy, gw, extras, grid=(N_DEV, T // tb),
                       a_spec=pl.BlockSpec((tb, D), lambda j, i: (i, 0)),
                       b_spec=pl.BlockSpec((None, kc, D), lambda j, i: (j, 0, 0)),
                       extra_specs=[pl.BlockSpec((tb, kc), lambda j, i: (i, j))] * len(extras),
                       out_shapes=[_sds((T, kc * N_DEV), BF16)], out_specs=[pl.BlockSpec((tb, kc), lambda j, i: (i, j))],
                       dims=NT, epilogue=epilogue, comm=comm)

    def norm_bwd_epilogue(dh, xv, g, dxi):
        dxn, dg = _rms_bwd(xv, g, dh)
        return dxi + dxn, dxi + dxn, dg

    def transposed(gw):
        return gw.transpose(0, 2, 1).reshape(gw.shape[0] * gw.shape[2], D)

    def back_cols(name, da, gwt, xv, g, dx_in, comm=None):
        n = da.shape[1]
        row = _row_spec(tk, D)
        return _matmul(name, da, gwt, [xv, g.reshape(1, D), dx_in], grid=(T // tk,),
                       a_spec=_row_spec(tk, n), b_spec=_const_spec((n, D)), extra_specs=[row, _const_spec((1, D)), row],
                       out_shapes=[_sds((T, D), F32), _sds((T, D), BF16), _sds((1, D), F32)], out_specs=[row, row, _const_spec((1, D))],
                       dims=NN, epilogue=norm_bwd_epilogue, n_sum=1, comm=comm)

    def token_sum(tt):
        return dict(k_axis=1, nk=T // tt) if T // tt > 1 else dict(k_axis=None)

    def wgrad_cols(name, h, da, nc):
        return _matmul(name, h, da, [], grid=(N_DEV, T // tb),
                       a_spec=pl.BlockSpec((tb, D), lambda j, t: (t, 0)), b_spec=pl.BlockSpec((tb, nc), lambda j, t: (t, j)),
                       extra_specs=[], out_shapes=[_sds((N_DEV, D, nc), BF16)],
                       out_specs=[pl.BlockSpec((None, D, nc), lambda j, t: (j, 0, 0))],
                       dims=TN, acc_shape=(D, nc), **token_sum(tb))[0]

    def wgrad_rows(name, a, dy, kc, ncols, tt):
        return _matmul(name, a, dy, [], grid=(a.shape[1] // kc, T // tt),
                       a_spec=pl.BlockSpec((tt, kc), lambda j, t: (t, j)), b_spec=pl.BlockSpec((tt, ncols), lambda j, t: (t, 0)),
                       extra_specs=[], out_shapes=[_sds((a.shape[1], ncols), BF16)],
                       out_specs=[pl.BlockSpec((kc, ncols), lambda j, t: (j, 0))],
                       dims=TN, acc_shape=(kc, ncols), **token_sum(tt))[0]

    saved = []
    xs = x2
    for i in range(depth):
        l = i // 2
        if i == 0:
            h = rmsnorm(xs, norm_mix[0])
        if i % 2 == 0:
            w_dkv, w_uq, w_ukv, w_o = mla_w[l]
            lat = _matmul("mla_down", h, w_dkv, [], grid=(nt,), a_spec=_row_spec(tm, D),
                          b_spec=pl.BlockSpec((None, D, LAT_PAD), lambda i_: (0, 0, 0)), extra_specs=[],
                          out_shapes=[_sds((T, LAT_PAD), F32)], out_specs=[_row_spec(tm, LAT_PAD)], dims=NN)[0]

            def latent_post(la, qn, kvn, cc, sa, sb):
                cq = _rms_fwd(la[:, :Q_RANK], qn)
                ckv = _rms_fwd(la[:, Q_RANK:Q_RANK + KV_RANK], kvn)
                kr = _rope_fwd(la[:, Q_RANK + KV_RANK:], cc, sa, sb)
                return cq, ckv, kr

            cq, ckv, kr = _rowwise(
                "mla_latent", latent_post, [lat, mla_q_norm[l].reshape(1, Q_RANK), mla_kv_norm[l].reshape(1, KV_RANK), t_cc, t_sa, t_sb],
                grid=(nt,), in_specs=[_row_spec(tm, LAT_PAD), _const_spec((1, Q_RANK)), _const_spec((1, KV_RANK))] + tab_specs,
                out_shapes=[_sds((T, Q_RANK), BF16), _sds((T, KV_RANK), BF16), _sds((T, 128), BF16)],
                out_specs=[_row_spec(tm, Q_RANK), _row_spec(tm, KV_RANK), _row_spec(tm, 128)])

            def q_epilogue(acc, cc, sa, sb):
                parts = []
                for b in range(HEADS):
                    parts += [acc[:, b * QPAD:b * QPAD + NOPE], _rope_fwd(acc[:, b * QPAD + NOPE:(b + 1) * QPAD], cc, sa, sb)]
                return (jnp.concatenate(parts, axis=1),)

            q = _matmul("mla_q", cq, w_uq, [t_cc, t_sa, t_sb], grid=(nt,), a_spec=_row_spec(tm, Q_RANK),
                        b_spec=pl.BlockSpec((None, Q_RANK, HW), lambda i_: (0, 0, 0)), extra_specs=tab_specs,
                        out_shapes=[_sds((T, HW), BF16)], out_specs=[_row_spec(tm, HW)], dims=NN, epilogue=q_epilogue)[0]

            def kv_write(outs, acc, krb):
                k_ref, v_ref, vt_ref = outs
                for b in range(HEADS):
                    vb = acc[:, b * QPAD + NOPE:(b + 1) * QPAD]
                    k_ref[:, b * QPAD:b * QPAD + NOPE] = acc[:, b * QPAD:b * QPAD + NOPE].astype(BF16)
                    k_ref[:, b * QPAD + NOPE:(b + 1) * QPAD] = krb
                    v_ref[:, b * VDIM:(b + 1) * VDIM] = vb.astype(BF16)
                    vbt = vb.T.astype(BF16)
                    for u in range(tm // tq):
                        vt_ref[b, u] = vbt[:, u * tq:(u + 1) * tq]

            kk, vv, vt = _matmul("mla_kv", ckv, w_ukv, [kr], grid=(nt,), a_spec=_row_spec(tm, KV_RANK),
                                 b_spec=pl.BlockSpec((None, KV_RANK, HW), lambda i_: (0, 0, 0)), extra_specs=[_row_spec(tm, 128)],
                                 out_shapes=[_sds((T, HW), BF16), _sds((T, OW), BF16), _sds((HEADS, T // tq, VDIM, tq), BF16)],
                                 out_specs=[_row_spec(tm, HW), _row_spec(tm, OW), pl.BlockSpec((HEADS, tm // tq, VDIM, tq), lambda i_: (0, i_, 0, 0))],
                                 dims=NN, write=kv_write)
            group = [up_sh[i], down_sh[i], in_sh[l], out_sh[l]] + (small_later if i == 0 else [])
            o, lse, *bufs = _flash_fwd(q, kk, vt, tq, comm=_gather_level1(group))
            xm, h2, g_up[i], g_down[i] = _matmul(
                "mla_out", o, w_o, [xs, norm_ffn[i].reshape(1, D)], grid=(nt,), a_spec=_row_spec(tm, OW),
                b_spec=pl.BlockSpec((None, OW, D), lambda i_: (0, 0, 0)), extra_specs=[_row_spec(tm, D), _const_spec((1, D))],
                out_shapes=[_sds((T, D), F32), _sds((T, D), BF16)], out_specs=[_row_spec(tm, D)] * 2, dims=NN,
                epilogue=residual_norm, comm=_gather_level2(bufs[:2]))
            half_gathered = bufs[2:]
            mix_saved = (h, lat, cq, ckv, q, kk, vv, o, lse)
        else:
            gp, ge, g_up[i], g_down[i] = proj_cols("sgu_in", h, g_in[l], e2c, _gelu_and_grad, 2, comm=_gather_level2(next_mlp))
            gate = _sgu_mid_fwd(ge, ln_g_full[l], ln_b_full[l], sgu_w_spatial[l], b_sp[l], 4)
            xm, h2 = proj_rows_residual("sgu_out", gate, g_out[l], xs, norm_ffn[i])
            mix_saved = (h, gp, ge, gate)
        r, s, *rest = proj_cols("ffn_up", h2, g_up[i], ffc, lambda acc: (jnp.maximum(acc, 0.0), jnp.square(jnp.maximum(acc, 0.0))), 2,
                                comm=_merge_comm(_gather_level2(half_gathered), _gather_level1([up_sh[i + 1]])) if i % 2 == 0 else None)
        if i % 2 == 0:
            g_in[l], g_out[l], *small_gathered, next_up = rest
        if i == 0:
            for l_ in range(1, n_mla):
                mla_w[l_] = mla_layouts(*small_gathered[4 * (l_ - 1):4 * l_])
            g_ln = small_gathered[-1]
            ln_g_full = [g_ln[:, l_, :].reshape(1, E) for l_ in range(n_sgu)]
            ln_b_full = [g_ln[:, n_sgu + l_, :].reshape(1, E) for l_ in range(n_sgu)]
        xo, h_next, *rest = proj_rows_residual("ffn_down", s, g_down[i], xm, norm_mix[i + 1] if i + 1 < depth else final_norm,
                                               comm=_gather_level1([down_sh[i + 1]]) if i % 2 == 0 else None)
        if i % 2 == 0:
            next_mlp = [next_up, rest[0]]
        saved.append((xs, xm, mix_saved, h2, r, s))
        xs, h = xo, h_next

    def loss_head(xv, tg, g):
        y = _rms_fwd(xv, g)
        err = y - tg
        part = 0.5 * jnp.sum(jnp.sum(err * err, axis=-1, keepdims=True), axis=0, keepdims=True) / D
        dx, dg = _rms_bwd(xv, g, err / D)
        return dx, dx, jnp.broadcast_to(part, (1, 128)), dg

    dx, dyb, loss_part, d_final = _rowwise(
        "loss_head", loss_head, [xs, tgt, final_norm.reshape(1, D)], grid=(nt,),
        in_specs=[_row_spec(tm, D), _row_spec(tm, D), _const_spec((1, D))],
        out_shapes=[_sds((T, D), F32), _sds((T, D), BF16), _sds((1, 128), F32), _sds((1, D), F32)],
        out_specs=[_row_spec(tm, D), _row_spec(tm, D), _const_spec((1, 128)), _const_spec((1, D))], n_acc=2)
    loss = lax.psum(loss_part[0, 0], ("x", "y", "c"))

    d_norm_mix, d_norm_ffn = [None] * depth, [None] * depth
    d_qn, d_kvn = [None] * n_mla, [None] * n_mla
    d_wsp, d_bsp, d_lng, d_lnb = [None] * n_sgu, [None] * n_sgu, [None] * n_sgu, [None] * n_sgu
    layers = {"dkv": n_mla, "uq": n_mla, "ukv": n_mla, "o": n_mla, "in": n_sgu, "out": n_sgu, "up": depth, "down": depth}
    stacked = {nm: None for nm in layers}
    pending = []
    summed = []

    def add_pairs(gs, rcvs):
        operands, in_specs, out_shapes, out_specs = [], [], [], []
        for g, rcv in zip(gs, rcvs):
            _, rws, cls = g.shape
            slab = pl.BlockSpec((None, rws, cls), lambda ch, cr: (ch, 0, 0))
            operands += [g.reshape(N_CHIP, 2, rws, cls), rcv]
            in_specs += [pl.BlockSpec((None, None, rws, cls), lambda ch, cr: (ch, cr[0], 0, 0)), slab]
            out_shapes.append(_sds(rcv.shape, BF16))
            out_specs.append(slab)

        def fn(*blocks):
            return tuple(blocks[2 * k].astype(F32) + blocks[2 * k + 1].astype(F32) for k in range(len(gs)))

        return _rowwise("grad_pair_sum", fn, operands, grid=(N_CHIP,), in_specs=in_specs, out_shapes=out_shapes, out_specs=out_specs,
                        grid_spec_prefetch=cidx)

    def sibling_comm():
        return _sibling_exchange([g for _, _, g in pending]) if pending else None

    def absorb(from_sibling):
        if pending:
            parts = add_pairs([g for _, _, g in pending], list(from_sibling))
            summed.extend((nm, l_, p) for (nm, l_, _), p in zip(pending, parts))
            pending.clear()

    def chip_comm():
        if pending:
            absorb(_comm_call("grad_sibling_exchange", sibling_comm()))
        comm, names = _chip_exchange([p for _, _, p in summed], [(nm, l_) for nm, l_, _ in summed], layers, stacked)
        summed.clear()
        return comm, names

    def rows128(a, rows):
        flat = a.reshape(-1, 128)
        return jnp.pad(flat, ((0, rows - flat.shape[0]), (0, 0)))

    def pad_to(n, mult):
        return -(-n // mult) * mult

    def packed(arrs, sizes):
        return jnp.concatenate([rows128(a, sz) for a, sz in zip(arrs, sizes)], axis=0)

    n_wsp, n_bsp, n_ln = sgu_w_spatial.size // 128, pad_to(sgu_b_spatial.size // 128, 8), pad_to(n_sgu * E // 128, 8)
    early_sizes = [n_wsp, pad_to(n_wsp + n_bsp, SMALL_ROWS) - n_wsp, n_ln, n_ln]
    early_rep = early_sizes[0] + early_sizes[1]
    gathered_early = None

    for i in reversed(range(depth)):
        l = i // 2
        xs_i, xm, mix_saved, h2, r, s = saved[i]
        da, *rcv = back_rows("ffn_down_bwd", dyb, g_down[i], ffc, [r], lambda acc, rr: (acc * (2.0 * rr.astype(F32)),), comm=sibling_comm())
        absorb(rcv)
        pending.append(("down", i, wgrad_rows("ffn_down_wgrad", s, dyb, ffc, D, tb).reshape(N_DEV, ffc, D)))
        pending.append(("up", i, wgrad_cols("ffn_up_wgrad", h2, da, ffc)))
        dx, dyb, d_norm_ffn[i], *rcv = back_cols("ffn_up_bwd", da, transposed(g_up[i]), xm, norm_ffn[i], dx, comm=sibling_comm())
        absorb(rcv)
        if i % 2 == 0:
            h, lat, cq, ckv, q, kk, vv, o, lse = mix_saved
            w_dkv, w_uq, w_ukv, w_o = mla_w[l]
            do = _matmul("mla_out_bwd", dyb, w_o, [], grid=(nt,), a_spec=_row_spec(tm, D),
                         b_spec=pl.BlockSpec((None, OW, D), lambda i_: (0, 0, 0)), extra_specs=[],
                         out_shapes=[_sds((T, OW), BF16)], out_specs=[_row_spec(tm, OW)], dims=NT)[0]
            g_o_l = wgrad_rows("mla_out_wgrad", o, dyb, OW, D, tm).reshape(N_DEV, owc, D)
            comm, names = chip_comm()
            if i == 0:
                early = packed([jnp.stack(d_wsp, 0), jnp.stack(d_bsp, 0), jnp.concatenate(d_lng, 0), jnp.concatenate(d_lnb, 0)], early_sizes)
                comm = _merge_comm(comm, _gather_level1([early]))
            dq_pre, dk, dv, *bufs = _flash_bwd(q, kk, vv, o, do, lse, (t_cc, t_sa, t_sb), tq, comm=comm)
            stacked.update(dict(zip(names, bufs)))
            pending.append(("o", l, g_o_l))

            def kv_pre(dkb, dvb, cc, sa, sb):
                parts, dkr = [], None
                for b in range(HEADS):
                    parts += [dkb[:, b * QPAD:b * QPAD + NOPE], dvb[:, b * VDIM:(b + 1) * VDIM]]
                    piece = dkb[:, b * QPAD + NOPE:(b + 1) * QPAD].astype(F32)
                    dkr = piece if dkr is None else dkr + piece
                return jnp.concatenate(parts, axis=1), _rope_bwd(dkr, cc, sa, sb)

            dkv, dkr, *rest = _rowwise("mla_dkv_rope", kv_pre, [dk, dv, t_cc, t_sa, t_sb], grid=(T // ts,),
                                       in_specs=[_row_spec(ts, HW), _row_spec(ts, OW)] + [_row_spec(ts, 128)] * 3,
                                       out_shapes=[_sds((T, HW), BF16), _sds((T, 128), F32)], out_specs=[_row_spec(ts, HW), _row_spec(ts, 128)],
                                       comm=_gather_level2(bufs[len(names):]) if i == 0 else None)
            if i == 0:
                (gathered_early,) = rest
            g_uq_l = wgrad_rows("mla_q_wgrad", cq, dq_pre, Q_RANK, HW, tm)
            g_ukv_l = wgrad_rows("mla_kv_wgrad", ckv, dkv, KV_RANK, HW, tm)
            pending.append(("uq", l, g_uq_l.reshape(Q_RANK, HEADS, QPAD)[:, :, :NOPE + ROPE].transpose(1, 0, 2)))
            pending.append(("ukv", l, g_ukv_l.reshape(KV_RANK, HEADS, NOPE + VDIM).transpose(1, 0, 2)))
            dcq = _matmul("mla_q_bwd", dq_pre, w_uq, [], grid=(nt,), a_spec=_row_spec(tm, HW),
                          b_spec=pl.BlockSpec((None, Q_RANK, HW), lambda i_: (0, 0, 0)), extra_specs=[],
                          out_shapes=[_sds((T, Q_RANK), F32)], out_specs=[_row_spec(tm, Q_RANK)], dims=NT)[0]
            dckv = _matmul("mla_kv_bwd", dkv, w_ukv, [], grid=(nt,), a_spec=_row_spec(tm, HW),
                           b_spec=pl.BlockSpec((None, KV_RANK, HW), lambda i_: (0, 0, 0)), extra_specs=[],
                           out_shapes=[_sds((T, KV_RANK), F32)], out_specs=[_row_spec(tm, KV_RANK)], dims=NT)[0]

            def latent_bwd(la, qn, kvn, dq_, dkv_, dkr_):
                dcq_raw, dqn = _rms_bwd(la[:, :Q_RANK], qn, dq_)
                dckv_raw, dkvn = _rms_bwd(la[:, Q_RANK:Q_RANK + KV_RANK], kvn, dkv_)
                return jnp.concatenate([dcq_raw, dckv_raw, dkr_], axis=1), dqn, dkvn

            dlat, d_qn[l], d_kvn[l] = _rowwise(
                "mla_latent_bwd", latent_bwd, [lat, mla_q_norm[l].reshape(1, Q_RANK), mla_kv_norm[l].reshape(1, KV_RANK), dcq, dckv, dkr],
                grid=(nt,), in_specs=[_row_spec(tm, LAT_PAD), _const_spec((1, Q_RANK)), _const_spec((1, KV_RANK)),
                                      _row_spec(tm, Q_RANK), _row_spec(tm, KV_RANK), _row_spec(tm, 128)],
                out_shapes=[_sds((T, LAT_PAD), BF16), _sds((1, Q_RANK), F32), _sds((1, KV_RANK), F32)],
                out_specs=[_row_spec(tm, LAT_PAD), _const_spec((1, Q_RANK)), _const_spec((1, KV_RANK))], n_acc=2)
            g_dkv_l = wgrad_rows("mla_down_wgrad", h, dlat, D, LAT_PAD, tm)
            pending.append(("dkv", l, g_dkv_l[:, :LAT].reshape(N_DEV, dc, LAT)))
            dx, dyb, d_norm_mix[i] = _matmul(
                "mla_down_bwd", dlat, w_dkv, [xs_i, norm_mix[i].reshape(1, D), dx], grid=(nt,), a_spec=_row_spec(tm, LAT_PAD),
                b_spec=pl.BlockSpec((None, D, LAT_PAD), lambda i_: (0, 0, 0)), extra_specs=[_row_spec(tm, D), _const_spec((1, D)), _row_spec(tm, D)],
                out_shapes=[_sds((T, D), F32), _sds((T, D), BF16), _sds((1, D), F32)],
                out_specs=[_row_spec(tm, D), _row_spec(tm, D), _const_spec((1, D))], dims=NT, epilogue=norm_bwd_epilogue, n_sum=1)
        else:
            h, gp, ge, gate = mix_saved
            (dgate,) = back_rows("sgu_out_bwd", dyb, g_out[l], ec, [], None)
            pending.append(("out", l, wgrad_rows("sgu_out_wgrad", gate, dyb, ec, D, tb).reshape(N_DEV, ec, D)))
            dz, d_wsp[l], d_bsp[l], d_lng[l], d_lnb[l] = _sgu_mid_bwd(ge, gp, dgate, ln_g_full[l], ln_b_full[l], sgu_w_spatial[l], b_sp[l], 2)
            pending.append(("in", l, wgrad_cols("sgu_in_wgrad", h, dz, e2c)))
            dx, dyb, d_norm_mix[i], *rcv = back_cols("sgu_in_bwd", dz, transposed(g_in[l]), xs_i, norm_mix[i], dx, comm=sibling_comm())
            absorb(rcv)
    grad_x = dx.reshape(1, T, D)

    last_comm, last_names = chip_comm()
    late_g = [jnp.concatenate(d_norm_mix, 0), jnp.concatenate(d_norm_ffn, 0), d_final, jnp.concatenate(d_qn, 0), jnp.concatenate(d_kvn, 0)]
    late_w = [norm_mix, norm_ffn, final_norm, mla_q_norm, mla_kv_norm]
    late_m = [m_norm_mix, m_norm_ffn, m_final_norm, m_mla_q_norm, m_mla_kv_norm]
    late_v = [v_norm_mix, v_norm_ffn, v_final_norm, v_mla_q_norm, v_mla_kv_norm]
    late_sizes = [pad_to(g.size // 128, 8) for g in late_g]
    late_rows = sum(late_sizes)

    def adam_big(parts, w, m, v):
        lyr, rws, cls = w.shape
        rt = _tile(rws, 512)

        def fn(p, w_, m_, v_):
            g = (p[0].astype(F32) + p[1].astype(F32)) + (p[2].astype(F32) + p[3].astype(F32))
            return (g, *_adam(w_, g, m_, v_))

        spec = pl.BlockSpec((None, rt, cls), lambda l_, i_: (l_, i_, 0))
        return _rowwise("adam_large", fn, [parts, w, m, v], grid=(lyr, rws // rt),
                        in_specs=[pl.BlockSpec((N_CHIP, None, rt, cls), lambda l_, i_: (0, l_, i_, 0)), spec, spec, spec],
                        out_shapes=[_sds(w.shape, F32)] * 4, out_specs=[spec] * 4)

    stacked.update(dict(zip(last_names, _comm_call("grad_chip_exchange", last_comm))))
    (gathered_late,) = _all_gather("gather_small_grads", [packed(late_g, late_sizes)])
    big = {}
    big["in"] = adam_big(stacked["in"], sgu_w_in, m_sgu_w_in, v_sgu_w_in)
    big["up"] = adam_big(stacked["up"], ffn_w_up, m_ffn_w_up, v_ffn_w_up)
    big["down"] = adam_big(stacked["down"], ffn_w_down, m_ffn_w_down, v_ffn_w_down)
    big["out"] = adam_big(stacked["out"], sgu_w_out, m_sgu_w_out, v_sgu_w_out)
    big["dkv"] = adam_big(stacked["dkv"], mla_w_dkv, m_mla_w_dkv, v_mla_w_dkv)
    big["uq"] = adam_big(stacked["uq"], mla_w_uq, m_mla_w_uq, v_mla_w_uq)
    big["ukv"] = adam_big(stacked["ukv"], mla_w_ukv, m_mla_w_ukv, v_mla_w_ukv)
    big["o"] = adam_big(stacked["o"], mla_w_o, m_mla_w_o, v_mla_w_o)
    big_res = [big[nm][:4] for nm in ("dkv", "uq", "ukv", "o", "in", "out", "up", "down")]

    def sum8(p):
        return ((p[0] + p[1]) + (p[2] + p[3])) + ((p[4] + p[5]) + (p[6] + p[7]))

    def adam_packed(name, gathered, ws, ms, vs, sizes, rows, tile):
        spec = _row_spec(tile, 128)
        return _rowwise(name, lambda p, w_, m_, v_: (sum8(p), *_adam(w_, sum8(p), m_, v_)),
                        [gathered, packed(ws, sizes), packed(ms, sizes), packed(vs, sizes)], grid=(rows // tile,),
                        in_specs=[pl.BlockSpec((N_DEV, tile, 128), lambda i_: (0, i_, 0)), spec, spec, spec],
                        out_shapes=[_sds((rows, 128), F32)] * 4, out_specs=[spec] * 4)

    late_res = adam_packed("adam_small", gathered_late, late_w, late_m, late_v, late_sizes, late_rows, late_rows)
    early_res = adam_packed("adam_spatial", gathered_early, [sgu_w_spatial, sgu_b_spatial], [m_sgu_w_spatial, m_sgu_b_spatial],
                            [v_sgu_w_spatial, v_sgu_b_spatial], early_sizes[:2], early_rep, SMALL_ROWS)

    def unpack(res, sizes, k, like):
        off = sum(sizes[:k])
        return res[off:off + like.size // 128].reshape(like.shape)

    my_b = 4 * lax.axis_index("x") + 2 * lax.axis_index("y") + lax.axis_index("c")
    ln_w = jnp.concatenate([sgu_ln_g, sgu_ln_b], 0)
    ln_m = jnp.concatenate([m_sgu_ln_g, m_sgu_ln_b], 0)
    ln_v = jnp.concatenate([v_sgu_ln_g, v_sgu_ln_b], 0)
    ln_all = jnp.concatenate([gathered_early[:, early_rep:early_rep + n_sgu * E // 128], gathered_early[:, early_rep + n_ln:early_rep + n_ln + n_sgu * E // 128]], axis=1)
    ln_mine = lax.dynamic_slice_in_dim(ln_all.reshape(N_DEV, 2 * n_sgu, N_DEV, ec), my_b, 1, axis=2).reshape(N_DEV, 2 * n_sgu, ec)
    ln_g_, ln_d, ln_m2, ln_v2 = _rowwise(
        "adam_ln", lambda p, w_, m_, v_: (sum8(p), *_adam(w_, sum8(p), m_, v_)), [ln_mine, ln_w, ln_m, ln_v], grid=(1,),
        in_specs=[_const_spec(ln_mine.shape), _const_spec(ln_w.shape), _const_spec(ln_w.shape), _const_spec(ln_w.shape)],
        out_shapes=[_sds(ln_w.shape, F32)] * 4, out_specs=[_const_spec(ln_w.shape)] * 4)

    def family(pos):
        ln = [ln_g_, ln_d, ln_m2, ln_v2][pos]
        late = [unpack(late_res[pos], late_sizes, k, w_) for k, w_ in enumerate(late_w)]
        w_sp_, b_sp_ = unpack(early_res[pos], early_sizes, 0, sgu_w_spatial), unpack(early_res[pos], early_sizes, 1, sgu_b_spatial)
        bigs = [res[pos] for res in big_res]
        return [late[0], late[1], late[2], bigs[0], late[3], late[4], bigs[1], bigs[2], bigs[3],
                bigs[4], ln[:n_sgu], ln[n_sgu:], w_sp_, b_sp_, bigs[5], bigs[6], bigs[7]]

    return (loss, grad_x, *family(0), *family(1), *family(2), *family(3))
```

```python
import math

import jax
import jax.numpy as jnp
from jax import lax
from jax.experimental import pallas as pl
from jax.experimental.pallas import tpu as pltpu

F32 = jnp.float32
BF16 = jnp.bfloat16
MESH = pl.DeviceIdType.MESH

N_DEV = 8
N_CHIP = 4
HEADS = 8
NOPE = 128
ROPE = 64
VDIM = 128
QPAD = 256
Q_RANK = 256
KV_RANK = 128
LAT = Q_RANK + KV_RANK + ROPE
LAT_PAD = 512
ROPE_THETA = 10000.0
SGU_CHUNK = 128
SGU_GROUPS = 8
NORM_EPS = 1e-6
LN_EPS = 1e-5
ADAM_LR = 0.001
ADAM_B1 = 0.9
ADAM_B2 = 0.999
ADAM_EPS = 1e-08
ADAM_WD = 0.01
ADAM_STEP = 10
ATTN_SCALE = (NOPE + ROPE) ** -0.5
NEG = -1e30
EXP2_SCALE = ATTN_SCALE * math.log2(math.e)
VMEM_LIMIT = 56 * 1024 * 1024
SMALL_ROWS = 256

NN = (((1,), (0,)), ((), ()))
NT = (((1,), (1,)), ((), ()))
TN = (((0,), (0,)), ((), ()))
ANY = pl.BlockSpec(memory_space=pl.ANY)


def _pcall(body, **kw):
    return pl.pallas_call(body, **kw)


def _params(n_grid, side_effects=False):
    return pltpu.CompilerParams(dimension_semantics=("arbitrary",) * n_grid, vmem_limit_bytes=VMEM_LIMIT, has_side_effects=side_effects)


def _sds(shape, dtype):
    return jax.ShapeDtypeStruct(tuple(shape), dtype)


def _tile(n, want):
    t = min(n, want)
    assert n % t == 0, (n, want)
    return t


class _Comm:
    def __init__(self, operands, out_shapes, aliases, scratch, start, finish):
        self.operands, self.out_shapes, self.aliases, self.scratch = operands, out_shapes, aliases, scratch
        self.start, self.finish = start, finish


def _merge_comm(first, second):
    n_in, n_out, n_sc = len(first.operands), len(first.out_shapes), len(first.scratch)
    aliases = dict(first.aliases)
    aliases.update({n_in + k: n_out + v for k, v in second.aliases.items()})

    def start(ins, outs, sems):
        first.start(ins[:n_in], outs[:n_out], sems[:n_sc])
        second.start(ins[n_in:], outs[n_out:], sems[n_sc:])

    def finish(ins, outs, sems):
        first.finish(ins[:n_in], outs[:n_out], sems[:n_sc])
        second.finish(ins[n_in:], outs[n_out:], sems[n_sc:])

    return _Comm([*first.operands, *second.operands], [*first.out_shapes, *second.out_shapes], aliases,
                 [*first.scratch, *second.scratch], start, finish)


def _place():
    return lax.axis_index("x"), lax.axis_index("y"), lax.axis_index("c")


def _other_chips(x, y):
    return [(1 - x, y), (x, 1 - y), (1 - x, 1 - y)]


def _dev_index(dev):
    return 4 * dev[0] + 2 * dev[1] + dev[2]


def _comm_call(name, comm):
    c_in, c_out = len(comm.operands), len(comm.out_shapes)

    def body(*refs):
        ins, outs, sems = refs[:c_in], refs[c_in:c_in + c_out], refs[c_in + c_out:]
        comm.start(ins, outs, sems)
        comm.finish(ins, outs, sems)

    return _pcall(body, name=name, in_specs=[ANY] * c_in, out_specs=[ANY] * c_out, out_shape=comm.out_shapes,
                  scratch_shapes=comm.scratch, input_output_aliases=dict(comm.aliases),
                  compiler_params=pltpu.CompilerParams(has_side_effects=True))(*comm.operands)


def _call(name, body, operands, in_specs, out_shapes, out_specs, scratch, grid, comm=None):
    if comm is None:
        return _pcall(body, name=name, grid=grid, in_specs=in_specs, out_specs=out_specs, out_shape=out_shapes,
                      scratch_shapes=scratch, compiler_params=_params(len(grid)))(*operands)
    n_in, n_out, n_sc = len(operands), len(out_shapes), len(scratch)
    c_in, c_out = len(comm.operands), len(comm.out_shapes)

    def hosted(*refs):
        ins, cins = refs[:n_in], refs[n_in:n_in + c_in]
        o0 = n_in + c_in
        outs, couts = refs[o0:o0 + n_out], refs[o0 + n_out:o0 + n_out + c_out]
        rest = refs[o0 + n_out + c_out:]
        sc, csems = rest[:n_sc], rest[n_sc:]
        first = pl.program_id(0) == 0
        last = pl.program_id(0) == grid[0] - 1
        for d in range(1, len(grid)):
            first = jnp.logical_and(first, pl.program_id(d) == 0)
            last = jnp.logical_and(last, pl.program_id(d) == grid[d] - 1)

        @pl.when(first)
        def _():
            comm.start(cins, couts, csems)

        body(*ins, *outs, *sc)

        @pl.when(last)
        def _():
            comm.finish(cins, couts, csems)

    return _pcall(hosted, name=name, grid=grid, in_specs=[*in_specs, *[ANY] * c_in], out_specs=[*out_specs, *[ANY] * c_out],
                  out_shape=[*out_shapes, *comm.out_shapes], scratch_shapes=[*scratch, *comm.scratch],
                  input_output_aliases={n_in + k: n_out + v for k, v in comm.aliases.items()},
                  compiler_params=_params(len(grid), side_effects=True))(*operands, *comm.operands)


def _gather_level1(shards):
    n = len(shards)

    def copies(ins, outs, sems):
        send_sems, recv_sems, local_sems = sems
        x, y, c = _place()
        me, sibling = (x, y, c), (x, y, 1 - c)
        chips = _other_chips(x, y)

        def copy(a, k, block, to, src=None):
            slot = outs[a].at[_dev_index(block)]
            return pltpu.make_async_remote_copy(src_ref=slot if src is None else src, dst_ref=slot, send_sem=send_sems.at[a, k],
                                                recv_sem=recv_sems.at[a, k], device_id=to, device_id_type=MESH)

        mine = [pltpu.make_async_copy(ins[a], outs[a].at[_dev_index(me)], local_sems.at[a]) for a in range(n)]
        sends = [copy(a, 1 + j, me, (*chip, c), src=ins[a]) for j, chip in enumerate(chips) for a in range(n)]
        sends += [copy(a, 0, me, sibling, src=ins[a]) for a in range(n)]
        recvs = [copy(a, 1 + j, (*chip, c), me) for j, chip in enumerate(chips) for a in range(n)]
        recvs += [copy(a, 0, sibling, me) for a in range(n)]
        return mine, sends, recvs

    def start(ins, outs, sems):
        mine, sends, _ = copies(ins, outs, sems)
        for cp in mine + sends:
            cp.start()

    def finish(ins, outs, sems):
        mine, sends, recvs = copies(ins, outs, sems)
        for cp in recvs:
            cp.wait_recv()
        for cp in sends:
            cp.wait_send()
        for cp in mine:
            cp.wait()

    return _Comm(shards, [_sds((N_DEV, *a.shape), a.dtype) for a in shards], {},
                 [pltpu.SemaphoreType.DMA((n, 4)), pltpu.SemaphoreType.DMA((n, 4)), pltpu.SemaphoreType.DMA((n,))], start, finish)


def _gather_level2(bufs):
    n = len(bufs)

    def copies(outs, sems):
        send_sems, recv_sems = sems
        x, y, c = _place()
        sibling = (x, y, 1 - c)
        sends, recvs = [], []
        for j, chip in enumerate(_other_chips(x, y)):
            for a in range(n):
                have, want = outs[a].at[_dev_index((*chip, c))], outs[a].at[_dev_index((*chip, 1 - c))]
                sends.append(pltpu.make_async_remote_copy(src_ref=have, dst_ref=have, send_sem=send_sems.at[a, j], recv_sem=recv_sems.at[a, j],
                                                          device_id=sibling, device_id_type=MESH))
                recvs.append(pltpu.make_async_remote_copy(src_ref=want, dst_ref=want, send_sem=send_sems.at[a, j], recv_sem=recv_sems.at[a, j],
                                                          device_id=sibling, device_id_type=MESH))
        return sends, recvs

    def start(ins, outs, sems):
        for cp in copies(outs, sems)[0]:
            cp.start()

    def finish(ins, outs, sems):
        sends, recvs = copies(outs, sems)
        for cp in recvs:
            cp.wait_recv()
        for cp in sends:
            cp.wait_send()

    return _Comm(bufs, [_sds(b.shape, b.dtype) for b in bufs], {a: a for a in range(n)},
                 [pltpu.SemaphoreType.DMA((n, 3)), pltpu.SemaphoreType.DMA((n, 3))], start, finish)


def _all_gather(name, arrays):
    n = len(arrays)

    def body(*refs):
        ins = refs[:n]
        outs = refs[n:2 * n]
        send_sems, recv_sems, local_sems = refs[2 * n:]
        x, y, c = _place()
        me, sibling = (x, y, c), (x, y, 1 - c)
        chips = _other_chips(x, y)

        def copy(a, k, block, to, src=None):
            slot = outs[a].at[_dev_index(block)]
            return pltpu.make_async_remote_copy(src_ref=slot if src is None else src, dst_ref=slot, send_sem=send_sems.at[a, k],
                                                recv_sem=recv_sems.at[a, k], device_id=to, device_id_type=MESH)

        mine = [pltpu.make_async_copy(ins[a], outs[a].at[_dev_index(me)], local_sems.at[a]) for a in range(n)]
        for cp in mine:
            cp.start()
        first = []
        for j, chip in enumerate(chips):
            first += [copy(a, 1 + j, me, (*chip, c), src=ins[a]) for a in range(n)]
        first += [copy(a, 0, me, sibling, src=ins[a]) for a in range(n)]
        for cp in first:
            cp.start()
        passed = []
        for j, chip in enumerate(chips):
            for a in range(n):
                copy(a, 1 + j, (*chip, c), me).wait_recv()
                fwd = copy(a, 4 + j, (*chip, c), sibling)
                fwd.start()
                passed.append(fwd)
        for a in range(n):
            copy(a, 0, sibling, me).wait_recv()
            for j, chip in enumerate(chips):
                copy(a, 4 + j, (*chip, 1 - c), me).wait_recv()
        for cp in first + passed:
            cp.wait_send()
        for cp in mine:
            cp.wait()

    return _pcall(
        body, name=name, in_specs=[ANY] * n, out_specs=[ANY] * n,
        out_shape=[_sds((N_DEV, *a.shape), a.dtype) for a in arrays],
        scratch_shapes=[pltpu.SemaphoreType.DMA((n, 7)), pltpu.SemaphoreType.DMA((n, 7)), pltpu.SemaphoreType.DMA((n,))],
        compiler_params=pltpu.CompilerParams(has_side_effects=True),
    )(*arrays)


def _sibling_exchange(grads):
    n = len(grads)

    def start(ins, outs, sems):
        send_sems, recv_sems = sems
        x, y, c = _place()
        for a in range(n):
            for ch in range(N_CHIP):
                pltpu.make_async_remote_copy(src_ref=ins[a].at[2 * ch + 1 - c], dst_ref=outs[a].at[ch], send_sem=send_sems.at[a],
                                             recv_sem=recv_sems.at[a], device_id=(x, y, 1 - c), device_id_type=MESH).start()

    def finish(ins, outs, sems):
        send_sems, recv_sems = sems
        x, y, c = _place()
        for a in range(n):
            pltpu.make_async_remote_copy(src_ref=outs[a], dst_ref=outs[a], send_sem=send_sems.at[a], recv_sem=recv_sems.at[a],
                                         device_id=(x, y, 1 - c), device_id_type=MESH).wait()

    return _Comm(grads, [_sds((N_CHIP, *g.shape[1:]), g.dtype) for g in grads], {},
                 [pltpu.SemaphoreType.DMA((n,)), pltpu.SemaphoreType.DMA((n,))], start, finish)


def _chip_exchange(parts, slots, layers, stacked):
    n = len(parts)
    names = []
    for nm, _ in slots:
        if nm not in names:
            names.append(nm)
    shapes = {nm: _sds((N_CHIP, layers[nm], *parts[a].shape[1:]), parts[a].dtype) for a, (nm, _) in enumerate(slots)}
    kept = [nm for nm in names if stacked.get(nm) is not None]
    aliases = {n + k: names.index(nm) for k, nm in enumerate(kept)}

    def copies(ins, outs, sems):
        send_sems, recv_sems, local_sems = sems
        x, y, c = _place()
        mine = 2 * x + y
        local, sends, recvs = [], [], []
        for a, (nm, l) in enumerate(slots):
            buf = outs[names.index(nm)]
            local.append(pltpu.make_async_copy(ins[a].at[mine], buf.at[mine, l], local_sems.at[a]))
            for j, chip in enumerate(_other_chips(x, y)):
                theirs = buf.at[2 * chip[0] + chip[1], l]
                sends.append(pltpu.make_async_remote_copy(src_ref=ins[a].at[2 * chip[0] + chip[1]], dst_ref=buf.at[mine, l], send_sem=send_sems.at[a, j],
                                                          recv_sem=recv_sems.at[a, j], device_id=(*chip, c), device_id_type=MESH))
                recvs.append(pltpu.make_async_remote_copy(src_ref=theirs, dst_ref=theirs, send_sem=send_sems.at[a, j],
                                                          recv_sem=recv_sems.at[a, j], device_id=(*chip, c), device_id_type=MESH))
        return local, sends, recvs

    def start(ins, outs, sems):
        local, sends, _ = copies(ins, outs, sems)
        for cp in local + sends:
            cp.start()

    def finish(ins, outs, sems):
        local, sends, recvs = copies(ins, outs, sems)
        for cp in recvs:
            cp.wait_recv()
        for cp in sends:
            cp.wait_send()
        for cp in local:
            cp.wait()

    comm = _Comm([*parts, *[stacked[nm] for nm in kept]], [shapes[nm] for nm in names], aliases,
                 [pltpu.SemaphoreType.DMA((n, 3)), pltpu.SemaphoreType.DMA((n, 3)), pltpu.SemaphoreType.DMA((n,))], start, finish)
    return comm, names


def _matmul(name, a, b, extras, *, grid, a_spec, b_spec, extra_specs, out_shapes, out_specs, dims, k_axis=None, nk=1,
            acc_shape=None, epilogue=None, comm=None, n_sum=0, write=None):
    n_extra = len(extras)
    n_out = len(out_shapes)

    def body(*refs):
        a_ref, b_ref = refs[0], refs[1]
        ex = refs[2:2 + n_extra]
        outs = refs[2 + n_extra:2 + n_extra + n_out]
        prod = lax.dot_general(a_ref[...], b_ref[...], dims, preferred_element_type=F32)

        def finish(acc):
            if write is not None:
                write(outs, acc, *[e[...] for e in ex])
                return
            res = epilogue(acc, *[e[...] for e in ex]) if epilogue is not None else (acc,)
            first = None
            for d in range(len(grid)):
                if d != k_axis:
                    here = pl.program_id(d) == 0
                    first = here if first is None else jnp.logical_and(first, here)
            for idx, (o, r) in enumerate(zip(outs, res)):
                if idx < n_out - n_sum:
                    o[...] = r.astype(o.dtype)
                else:
                    @pl.when(first)
                    def _(o=o, r=r):
                        o[...] = r.astype(o.dtype)

                    @pl.when(jnp.logical_not(first))
                    def _(o=o, r=r):
                        o[...] += r.astype(o.dtype)

        if k_axis is None:
            finish(prod)
        else:
            acc_ref = refs[-1]
            k = pl.program_id(k_axis)

            @pl.when(k == 0)
            def _():
                acc_ref[...] = prod

            @pl.when(k > 0)
            def _():
                acc_ref[...] += prod

            @pl.when(k == nk - 1)
            def _():
                finish(acc_ref[...])

    scratch = [] if k_axis is None else [pltpu.VMEM(acc_shape, F32)]
    return _call(name, body, [a, b, *extras], [a_spec, b_spec, *extra_specs], list(out_shapes), list(out_specs), scratch, grid, comm)


def _rowwise(name, fn, operands, *, grid, in_specs, out_shapes, out_specs, n_acc=0, grid_spec_prefetch=None, comm=None):
    n_in = len(operands)
    n_out = len(out_shapes)
    n_pre = 0 if grid_spec_prefetch is None else 1

    def body(*refs):
        refs = refs[n_pre:]
        ins = refs[:n_in]
        outs = refs[n_in:n_in + n_out]
        res = fn(*[r[...] for r in ins])
        if not isinstance(res, (tuple, list)):
            res = (res,)
        first = pl.program_id(0) == 0
        for d in range(1, len(grid)):
            first = jnp.logical_and(first, pl.program_id(d) == 0)
        for idx, (o, r) in enumerate(zip(outs, res)):
            if idx < n_out - n_acc:
                o[...] = r.astype(o.dtype)
            else:
                @pl.when(first)
                def _(o=o, r=r):
                    o[...] = r.astype(o.dtype)

                @pl.when(jnp.logical_not(first))
                def _(o=o, r=r):
                    o[...] += r.astype(o.dtype)

    if comm is not None:
        return _call(name, body, list(operands), list(in_specs), list(out_shapes), list(out_specs), [], grid, comm)
    if grid_spec_prefetch is None:
        return _pcall(body, name=name, grid=grid, in_specs=in_specs, out_specs=out_specs, out_shape=out_shapes,
                      compiler_params=_params(len(grid)))(*operands)
    gs = pltpu.PrefetchScalarGridSpec(num_scalar_prefetch=1, grid=grid, in_specs=in_specs, out_specs=out_specs)
    return _pcall(body, name=name, grid_spec=gs, out_shape=out_shapes,
                  compiler_params=_params(len(grid)))(grid_spec_prefetch, *operands)


def _row_spec(tm, w):
    return pl.BlockSpec((tm, w), lambda i: (i, 0))


def _const_spec(shape):
    nd = len(shape)
    return pl.BlockSpec(tuple(shape), lambda *_: (0,) * nd)


def _rms_fwd(x, g):
    r = lax.rsqrt(jnp.mean(x * x, axis=-1, keepdims=True) + NORM_EPS)
    return x * r * g


def _rms_bwd(x, g, dy):
    r = lax.rsqrt(jnp.mean(x * x, axis=-1, keepdims=True) + NORM_EPS)
    xh = x * r
    u = dy * g
    dx = r * (u - xh * jnp.mean(u * xh, axis=-1, keepdims=True))
    dg = jnp.sum(dy * xh, axis=0, keepdims=True)
    return dx, dg


def _gelu_and_grad(z):
    cdf = 0.5 * (1.0 + lax.erf(z * (2.0 ** -0.5)))
    return cdf + z * jnp.exp(-0.5 * z * z) * ((2.0 * math.pi) ** -0.5), z * cdf


def _rope_fwd(x, cc, sa, sb):
    return x * cc + pltpu.roll(x, 96, 1) * sa + pltpu.roll(x, 32, 1) * sb


def _rope_bwd(d, cc, sa, sb):
    return d * cc + pltpu.roll(d * sa, 32, 1) + pltpu.roll(d * sb, 96, 1)


def _adam(w, g, m, v):
    m = ADAM_B1 * m + (1.0 - ADAM_B1) * g
    v = ADAM_B2 * v + (1.0 - ADAM_B2) * (g * g)
    m_hat = m / (1.0 - ADAM_B1 ** ADAM_STEP)
    v_hat = v / (1.0 - ADAM_B2 ** ADAM_STEP)
    delta = -ADAM_LR * (m_hat / (jnp.sqrt(v_hat) + ADAM_EPS) + ADAM_WD * w)
    return delta, m, v


def _flash_fwd(q, k, vt, tq, comm=None):
    h, t = vt.shape[0], q.shape[0]
    nq = t // tq

    chunk_blocks = [c for c in (4, 2, 1) if c < nq]

    def body(q_ref, k_ref, vt_ref, o_ref, lse_ref, m_ref, l_ref, acc_ref):
        qi = pl.program_id(1)
        m_ref[...] = jnp.full((1, tq), NEG, F32)
        l_ref[...] = jnp.zeros((1, tq), F32)
        acc_ref[...] = jnp.zeros((VDIM, tq), F32)

        def update(kb0, nblk, masked):
            kb = k_ref[pl.ds(pl.multiple_of(kb0 * tq, tq), nblk * tq), :]
            st = lax.dot_general(kb, q_ref[...], NT, preferred_element_type=F32)
            if masked:
                key = lax.broadcasted_iota(jnp.int32, (tq, tq), 0)
                qry = lax.broadcasted_iota(jnp.int32, (tq, tq), 1)
                st = jnp.where(key <= qry, st, NEG)
            m_old = m_ref[...]
            m_new = jnp.maximum(m_old, jnp.max(st, axis=0, keepdims=True))
            alpha = jnp.exp2((m_old - m_new) * EXP2_SCALE)
            pt = jnp.exp2((st - m_new) * EXP2_SCALE)
            l_ref[...] = alpha * l_ref[...] + jnp.sum(pt, axis=0, keepdims=True)
            ptb = pt.astype(BF16)
            pv = lax.dot_general(vt_ref[kb0], ptb[:tq], NN, preferred_element_type=F32)
            for j in range(1, nblk):
                pv += lax.dot_general(vt_ref[kb0 + j], ptb[j * tq:(j + 1) * tq], NN, preferred_element_type=F32)
            acc_ref[...] = alpha * acc_ref[...] + pv
            m_ref[...] = m_new

        start = jnp.int32(0)
        for c in chunk_blocks:
            take = (qi & c) != 0

            @pl.when(take)
            def _(start=start, c=c):
                update(start, c, False)

            start = start + jnp.where(take, c, 0)
        update(qi, 1, True)
        l = l_ref[...]
        o_ref[...] = (acc_ref[...] / l).T.astype(o_ref.dtype)
        lse_ref[...] = m_ref[...] * EXP2_SCALE + jnp.log2(l)

    return _call(
        "flash_fwd", body, [q, k, vt],
        [pl.BlockSpec((tq, QPAD), lambda hh, i: (i, hh)),
         pl.BlockSpec((t, QPAD), lambda hh, i: (0, hh)),
         pl.BlockSpec((None, nq, VDIM, tq), lambda hh, i: (hh, 0, 0, 0))],
        [_sds((t, h * VDIM), BF16), _sds((h, nq, 1, tq), F32)],
        [pl.BlockSpec((tq, VDIM), lambda hh, i: (i, hh)),
         pl.BlockSpec((None, None, 1, tq), lambda hh, i: (hh, i, 0, 0))],
        [pltpu.VMEM((1, tq), F32), pltpu.VMEM((1, tq), F32), pltpu.VMEM((VDIM, tq), F32)], (h, nq), comm)


def _flash_bwd(q, k, v, o, do, lse, tabs, tq, comm=None):
    t = q.shape[0]
    h = q.shape[1] // QPAD
    nq = t // tq

    def body(q_ref, k_ref, v_ref, o_ref, do_ref, lse_ref, cc_ref, sa_ref, sb_ref, dq_ref, dk_out, dv_out, delta_ref, dqt_ref, dk_ref, dv_ref):
        kj = pl.program_id(1)

        @pl.when(kj == 0)
        def _():
            dqt_ref[...] = jnp.zeros_like(dqt_ref)
            ones = jnp.ones((8, VDIM), BF16)
            for qi in range(nq):
                rows = pl.ds(qi * tq, tq)
                prod = do_ref[rows, :].astype(F32) * o_ref[rows, :].astype(F32)
                hi = prod.astype(BF16)
                lo = (prod - hi.astype(F32)).astype(BF16)
                delta_ref[qi] = (lax.dot_general(ones, hi, NT, preferred_element_type=F32)
                                 + lax.dot_general(ones, lo, NT, preferred_element_type=F32))

        kb = k_ref[...]
        vb = v_ref[...]
        kbt = kb.astype(F32).T.astype(BF16)
        dk_ref[...] = jnp.zeros_like(dk_ref)
        dv_ref[...] = jnp.zeros_like(dv_ref)

        def step(q0, nblk, masked):
            rows = pl.ds(pl.multiple_of(q0 * tq, tq), nblk * tq)
            qb = q_ref[rows, :]
            dob = do_ref[rows, :]
            lse = jnp.concatenate([lse_ref[q0 + j] for j in range(nblk)], axis=1)
            delta = jnp.concatenate([delta_ref[q0 + j, pl.ds(0, 1), :] for j in range(nblk)], axis=1)
            st = lax.dot_general(kb, qb, NT, preferred_element_type=F32)
            pt = jnp.exp2(st * EXP2_SCALE - lse)
            if masked:
                key = lax.broadcasted_iota(jnp.int32, (tq, tq), 0)
                qry = lax.broadcasted_iota(jnp.int32, (tq, tq), 1)
                pt = jnp.where(key <= qry, pt, 0.0)
            dv_ref[...] += lax.dot_general(pt.astype(BF16), dob, NN, preferred_element_type=F32)
            dpt = lax.dot_general(vb, dob, NT, preferred_element_type=F32)
            dst = (pt * (dpt - delta) * ATTN_SCALE).astype(BF16)
            dk_ref[...] += lax.dot_general(dst, qb, NN, preferred_element_type=F32)
            dqt = lax.dot_general(kbt, dst, NN, preferred_element_type=F32)
            for j in range(nblk):
                dqt_ref[q0 + j] += dqt[:, j * tq:(j + 1) * tq]

        later = nq - 1 - kj
        step(kj, 1, True)
        start = kj + 1
        for c in [c for c in (1, 2, 4) if c < nq]:
            take = (later & c) != 0

            @pl.when(take)
            def _(start=start, c=c):
                step(start, c, False)

            start = start + jnp.where(take, c, 0)
        dk_out[...] = dk_ref[...].astype(BF16)
        dv_out[...] = dv_ref[...].astype(BF16)

        @pl.when(kj == nq - 1)
        def _():
            for qi in range(nq):
                rows = pl.ds(qi * tq, tq)
                d = dqt_ref[qi].T
                roped = _rope_bwd(d[:, NOPE:], cc_ref[rows, :], sa_ref[rows, :], sb_ref[rows, :])
                dq_ref[rows, :] = jnp.concatenate([d[:, :NOPE], roped], axis=1).astype(BF16)

    head_q = pl.BlockSpec((t, QPAD), lambda hh, j: (0, hh))
    head_v = pl.BlockSpec((t, VDIM), lambda hh, j: (0, hh))
    table = pl.BlockSpec((t, 128), lambda hh, j: (0, 0))
    return _call(
        "flash_bwd", body, [q, k, v, o, do, lse, *tabs],
        [head_q, pl.BlockSpec((tq, QPAD), lambda hh, j: (j, hh)), pl.BlockSpec((tq, VDIM), lambda hh, j: (j, hh)), head_v, head_v,
         pl.BlockSpec((None, nq, 1, tq), lambda hh, j: (hh, 0, 0, 0)), table, table, table],
        [_sds((t, h * QPAD), BF16), _sds((t, h * QPAD), BF16), _sds((t, h * VDIM), BF16)],
        [head_q, pl.BlockSpec((tq, QPAD), lambda hh, j: (j, hh)), pl.BlockSpec((tq, VDIM), lambda hh, j: (j, hh))],
        [pltpu.VMEM((nq, 8, tq), F32), pltpu.VMEM((nq, QPAD, tq), F32), pltpu.VMEM((tq, QPAD), F32), pltpu.VMEM((tq, VDIM), F32)], (h, nq), comm)


def _tril_bf16(w):
    row = lax.broadcasted_iota(jnp.int32, w.shape, 0)
    col = lax.broadcasted_iota(jnp.int32, w.shape, 1)
    return jnp.where(col <= row, w, 0.0).astype(BF16)


def _layer_norm_parts(v0):
    mu = jnp.mean(v0, axis=-1, keepdims=True)
    vc = v0 - mu
    rstd = lax.rsqrt(jnp.mean(vc * vc, axis=-1, keepdims=True) + LN_EPS)
    return vc * rstd, rstd


def _sgu_mid_fwd(ge, ln_g, ln_b, w_sp, b_sp, chunks_per_step):
    t, e2 = ge.shape
    e = e2 // 2
    gd = e // SGU_GROUPS
    rows = SGU_CHUNK * chunks_per_step

    def body(u_ref, v_ref, g_ref, b_ref, w_ref, bs_ref, gate_ref):
        for ck in range(chunks_per_step):
            r = pl.ds(ck * SGU_CHUNK, SGU_CHUNK)
            xh, _ = _layer_norm_parts(v_ref[r, :].astype(F32))
            v1 = (xh * g_ref[...] + b_ref[...]).astype(BF16)
            for g in range(SGU_GROUPS):
                cols = pl.ds(g * gd, gd)
                mixed = lax.dot_general(_tril_bf16(w_ref[g]), v1[:, g * gd:(g + 1) * gd], NN, preferred_element_type=F32) + bs_ref[g]
                gate_ref[r, cols] = (u_ref[r, cols].astype(F32) * mixed).astype(BF16)

    return _pcall(
        body, name="sgu_mid_fwd", grid=(t // rows,),
        in_specs=[pl.BlockSpec((rows, e), lambda i: (i, 0)), pl.BlockSpec((rows, e), lambda i: (i, 1)),
                  _const_spec((1, e)), _const_spec((1, e)), _const_spec(w_sp.shape), _const_spec(b_sp.shape)],
        out_specs=pl.BlockSpec((rows, e), lambda i: (i, 0)),
        out_shape=_sds((t, e), BF16), compiler_params=_params(1),
    )(ge, ge, ln_g, ln_b, w_sp, b_sp)


def _sgu_mid_bwd(ge, gp, dgate, ln_g, ln_b, w_sp, b_sp, chunks_per_step):
    t, e2 = ge.shape
    e = e2 // 2
    gd = e // SGU_GROUPS
    rows = SGU_CHUNK * chunks_per_step

    def body(u_ref, v_ref, zu_ref, zv_ref, dg_ref, g_ref, b_ref, w_ref, bs_ref, dz_ref, dw_ref, dbs_ref, dlg_ref, dlb_ref):
        @pl.when(pl.program_id(0) == 0)
        def _():
            dw_ref[...] = jnp.zeros_like(dw_ref)
            dbs_ref[...] = jnp.zeros_like(dbs_ref)
            dlg_ref[...] = jnp.zeros_like(dlg_ref)
            dlb_ref[...] = jnp.zeros_like(dlb_ref)

        for ck in range(chunks_per_step):
            r = pl.ds(ck * SGU_CHUNK, SGU_CHUNK)
            xh, rstd = _layer_norm_parts(v_ref[r, :].astype(F32))
            v1 = (xh * g_ref[...] + b_ref[...]).astype(BF16)
            dv1_parts = []
            for g in range(SGU_GROUPS):
                cols = pl.ds(g * gd, gd)
                wc = _tril_bf16(w_ref[g])
                v1g = v1[:, g * gd:(g + 1) * gd]
                mixed = lax.dot_general(wc, v1g, NN, preferred_element_type=F32) + bs_ref[g]
                dgate = dg_ref[r, cols].astype(F32)
                dmixed = dgate * u_ref[r, cols].astype(F32)
                du = dgate * mixed
                dz_ref[r, cols] = (du * zu_ref[r, cols].astype(F32)).astype(BF16)
                dbs_ref[g] += jnp.sum(dmixed, axis=1, keepdims=True)
                dmb = dmixed.astype(BF16)
                dwg = lax.dot_general(dmb, v1g, NT, preferred_element_type=F32)
                row = lax.broadcasted_iota(jnp.int32, dwg.shape, 0)
                col = lax.broadcasted_iota(jnp.int32, dwg.shape, 1)
                dw_ref[g] += jnp.where(col <= row, dwg, 0.0)
                dv1_parts.append(lax.dot_general(wc, dmb, TN, preferred_element_type=F32))
            dv1 = jnp.concatenate(dv1_parts, axis=1)
            dlg_ref[...] += jnp.sum(dv1 * xh, axis=0, keepdims=True)
            dlb_ref[...] += jnp.sum(dv1, axis=0, keepdims=True)
            dxh = dv1 * g_ref[...]
            dv0 = rstd * (dxh - jnp.mean(dxh, axis=-1, keepdims=True) - xh * jnp.mean(dxh * xh, axis=-1, keepdims=True))
            dz_ref[r, pl.ds(e, e)] = (dv0 * zv_ref[r, :].astype(F32)).astype(BF16)

    half0 = pl.BlockSpec((rows, e), lambda i: (i, 0))
    half1 = pl.BlockSpec((rows, e), lambda i: (i, 1))
    return _pcall(
        body, name="sgu_mid_bwd", grid=(t // rows,),
        in_specs=[half0, half1, half0, half1, half0, _const_spec((1, e)), _const_spec((1, e)), _const_spec(w_sp.shape), _const_spec(b_sp.shape)],
        out_specs=[pl.BlockSpec((rows, e2), lambda i: (i, 0)), _const_spec(w_sp.shape), _const_spec(b_sp.shape), _const_spec((1, e)), _const_spec((1, e))],
        out_shape=[_sds((t, e2), BF16), _sds(w_sp.shape, F32), _sds(b_sp.shape, F32), _sds((1, e), F32), _sds((1, e), F32)],
        compiler_params=_params(1),
    )(ge, ge, gp, gp, dgate, ln_g, ln_b, w_sp, b_sp)


def kernel(x, positions, norm_mix, norm_ffn, final_norm, mla_w_dkv, mla_q_norm, mla_kv_norm, mla_w_uq, mla_w_ukv, mla_w_o, sgu_w_in, sgu_ln_g, sgu_ln_b, sgu_w_spatial, sgu_b_spatial, sgu_w_out, ffn_w_up, ffn_w_down, loss_target, m_norm_mix, m_norm_ffn, m_final_norm, m_mla_w_dkv, m_mla_q_norm, m_mla_kv_norm, m_mla_w_uq, m_mla_w_ukv, m_mla_w_o, m_sgu_w_in, m_sgu_ln_g, m_sgu_ln_b, m_sgu_w_spatial, m_sgu_b_spatial, m_sgu_w_out, m_ffn_w_up, m_ffn_w_down, v_norm_mix, v_norm_ffn, v_final_norm, v_mla_w_dkv, v_mla_q_norm, v_mla_kv_norm, v_mla_w_uq, v_mla_w_ukv, v_mla_w_o, v_sgu_w_in, v_sgu_ln_g, v_sgu_ln_b, v_sgu_w_spatial, v_sgu_b_spatial, v_sgu_w_out, v_ffn_w_up, v_ffn_w_down):
    _, T, D = x.shape
    depth = norm_mix.shape[0]
    n_mla, n_sgu = mla_w_dkv.shape[0], sgu_w_in.shape[0]
    assert depth % 2 == 0
    FF = ffn_w_up.shape[2] * N_DEV
    E = sgu_w_out.shape[1] * N_DEV
    ffc, ec, e2c = FF // N_DEV, E // N_DEV, 2 * E // N_DEV
    dc = D // N_DEV
    OW = HEADS * VDIM
    HW = HEADS * QPAD
    owc = OW // N_DEV
    tm = _tile(T, 1024)
    tb = _tile(T, 4096)
    tk = _tile(T, 512)
    tq = _tile(T, 512)
    ts = _tile(T, 256)
    nt = T // tm
    x2 = x.reshape(T, D)
    tgt = loss_target.reshape(T, D)
    cidx = lax.axis_index("c").astype(jnp.int32).reshape(1)

    ln_local = jnp.concatenate([sgu_ln_g, sgu_ln_b, jnp.zeros((8 - 2 * n_sgu, ec), F32)], axis=0)
    mla_sh = [[w[l].astype(BF16) for w in (mla_w_dkv, mla_w_uq, mla_w_ukv, mla_w_o)] for l in range(n_mla)]

    def mla_layouts(g_dkv, g_uq, g_ukv, g_o):
        w_dkv = jnp.pad(g_dkv.reshape(1, D, LAT), ((0, 0), (0, 0), (0, LAT_PAD - LAT)))
        w_uq = jnp.pad(g_uq, ((0, 0), (0, 0), (0, QPAD - NOPE - ROPE))).transpose(1, 0, 2).reshape(1, Q_RANK, HEADS * QPAD)
        w_ukv = g_ukv.transpose(1, 0, 2).reshape(1, KV_RANK, HEADS * (NOPE + VDIM))
        return w_dkv, w_uq, w_ukv, g_o.reshape(1, HEADS * VDIM, D)

    mla_w = [None] * n_mla
    small_later = [a for l in range(1, n_mla) for a in mla_sh[l]] + [ln_local]
    ln_g_full, ln_b_full = [None] * n_sgu, [None] * n_sgu
    b_sp = sgu_b_spatial.reshape(n_sgu, SGU_GROUPS, SGU_CHUNK, 1)
    up_sh = [ffn_w_up[i].astype(BF16) for i in range(depth)]
    down_sh = [ffn_w_down[i].astype(BF16) for i in range(depth)]
    in_sh = [sgu_w_in[l].astype(BF16) for l in range(n_sgu)]
    out_sh = [sgu_w_out[l].astype(BF16) for l in range(n_sgu)]
    g_up, g_down, g_in, g_out = [None] * depth, [None] * depth, [None] * n_sgu, [None] * n_sgu

    inv_freq = ROPE_THETA ** (-jnp.arange(0, ROPE, 2, dtype=F32) / ROPE)
    zeros32 = jnp.zeros((ROPE // 2,), F32)
    inv128 = jnp.concatenate([inv_freq, inv_freq, zeros32, zeros32]).reshape(1, 128)
    sel_a = jnp.concatenate([-jnp.ones((32,), F32), zeros32, zeros32, zeros32]).reshape(1, 128)
    sel_b = jnp.concatenate([zeros32, jnp.ones((32,), F32), zeros32, zeros32]).reshape(1, 128)
    sel_c = jnp.concatenate([jnp.ones((64,), F32), zeros32, zeros32]).reshape(1, 128)

    def rope_tables(pos, inv, sa, sb, sc):
        ang = pos.astype(F32) * inv
        cs, sn = jnp.cos(ang), jnp.sin(ang)
        return cs * sc, sn * sa, sn * sb

    t_cc, t_sa, t_sb, *first_half = _rowwise(
        "rope_tables", rope_tables, [positions.reshape(T, 1), inv128, sel_a, sel_b, sel_c], grid=(nt,),
        in_specs=[_row_spec(tm, 1)] + [_const_spec((1, 128))] * 4,
        out_shapes=[_sds((T, 128), F32)] * 3, out_specs=[_row_spec(tm, 128)] * 3, comm=_gather_level1(mla_sh[0]))
    tab_specs = [_row_spec(tm, 128)] * 3

    def rmsnorm(xv, g, comm):
        return _rowwise("rmsnorm", lambda a, gg: _rms_fwd(a, gg), [xv, g.reshape(1, D)], grid=(nt,),
                        in_specs=[_row_spec(tm, D), _const_spec((1, D))], out_shapes=[_sds((T, D), BF16)], out_specs=[_row_spec(tm, D)], comm=comm)

    def proj_cols(name, h, gw, nc, epilogue, n_out, comm=None):
        return _matmul(name, h, gw, [], grid=(N_DEV, T // tb),
                       a_spec=pl.BlockSpec((tb, D), lambda j, i: (i, 0)),
                       b_spec=pl.BlockSpec((None, D, nc), lambda j, i: (j, 0, 0)), extra_specs=[],
                       out_shapes=[_sds((T, nc * N_DEV), BF16)] * n_out, out_specs=[pl.BlockSpec((tb, nc), lambda j, i: (i, j))] * n_out,
                       dims=NN, epilogue=epilogue, comm=comm)

    def residual_norm(acc, xr, g):
        xn = acc + xr
        return xn, _rms_fwd(xn, g)

    def proj_rows_residual(name, a, gw, xres, g_next, comm=None):
        kk_ = a.shape[1]
        return _matmul(name, a, gw.reshape(kk_, D), [xres, g_next.reshape(1, D)], grid=(T // tk,),
                       a_spec=_row_spec(tk, kk_), b_spec=_const_spec((kk_, D)), extra_specs=[_row_spec(tk, D), _const_spec((1, D))],
                       out_shapes=[_sds((T, D), F32), _sds((T, D), BF16)], out_specs=[_row_spec(tk, D)] * 2,
                       dims=NN, epilogue=residual_norm, comm=comm)

    def back_rows(name, dy, gw, kc, extras, epilogue, comm=None):
        return _matmul(name, dy, gw, extras, grid=(N_DEV, T // tb),
                       a_spec=pl.BlockSpec((tb, D), lambda j, i: (i, 0)),
                       b_spec=pl.BlockSpec((None, kc, D), lambda j, i: (j, 0, 0)),
                       extra_specs=[pl.BlockSpec((tb, kc), lambda j, i: (i, j))] * len(extras),
                       out_shapes=[_sds((T, kc * N_DEV), BF16)], out_specs=[pl.BlockSpec((tb, kc), lambda j, i: (i, j))],
                       dims=NT, epilogue=epilogue, comm=comm)

    def norm_bwd_epilogue(dh, xv, g, dxi):
        dxn, dg = _rms_bwd(xv, g, dh)
        return dxi + dxn, dxi + dxn, dg

    def transposed(gw):
        return gw.transpose(0, 2, 1).reshape(gw.shape[0] * gw.shape[2], D)

    def back_cols(name, da, gwt, xv, g, dx_in, comm=None):
        n = da.shape[1]
        row = _row_spec(tk, D)
        return _matmul(name, da, gwt, [xv, g.reshape(1, D), dx_in], grid=(T // tk,),
                       a_spec=_row_spec(tk, n), b_spec=_const_spec((n, D)), extra_specs=[row, _const_spec((1, D)), row],
                       out_shapes=[_sds((T, D), F32), _sds((T, D), BF16), _sds((1, D), F32)], out_specs=[row, row, _const_spec((1, D))],
                       dims=NN, epilogue=norm_bwd_epilogue, n_sum=1, comm=comm)

    def token_sum(tt):
        return dict(k_axis=1, nk=T // tt) if T // tt > 1 else dict(k_axis=None)

    def wgrad_cols(name, h, da, nc):
        return _matmul(name, h, da, [], grid=(N_DEV, T // tb),
                       a_spec=pl.BlockSpec((tb, D), lambda j, t: (t, 0)), b_spec=pl.BlockSpec((tb, nc), lambda j, t: (t, j)),
                       extra_specs=[], out_shapes=[_sds((N_DEV, D, nc), BF16)],
                       out_specs=[pl.BlockSpec((None, D, nc), lambda j, t: (j, 0, 0))],
                       dims=TN, acc_shape=(D, nc), **token_sum(tb))[0]

    def wgrad_rows(name, a, dy, kc, ncols, tt):
        return _matmul(name, a, dy, [], grid=(a.shape[1] // kc, T // tt),
                       a_spec=pl.BlockSpec((tt, kc), lambda j, t: (t, j)), b_spec=pl.BlockSpec((tt, ncols), lambda j, t: (t, 0)),
                       extra_specs=[], out_shapes=[_sds((a.shape[1], ncols), BF16)],
                       out_specs=[pl.BlockSpec((kc, ncols), lambda j, t: (j, 0))],
                       dims=TN, acc_shape=(kc, ncols), **token_sum(tt))[0]

    saved = []
    xs = x2
    for i in range(depth):
        l = i // 2
        if i == 0:
            h, *first_w = rmsnorm(xs, norm_mix[0], _gather_level2(first_half))
            mla_w[0] = mla_layouts(*first_w)
        if i % 2 == 0:
            w_dkv, w_uq, w_ukv, w_o = mla_w[l]
            lat = _matmul("mla_down", h, w_dkv, [], grid=(nt,), a_spec=_row_spec(tm, D),
                          b_spec=pl.BlockSpec((None, D, LAT_PAD), lambda i_: (0, 0, 0)), extra_specs=[],
                          out_shapes=[_sds((T, LAT_PAD), F32)], out_specs=[_row_spec(tm, LAT_PAD)], dims=NN)[0]

            def latent_post(la, qn, kvn, cc, sa, sb):
                cq = _rms_fwd(la[:, :Q_RANK], qn)
                ckv = _rms_fwd(la[:, Q_RANK:Q_RANK + KV_RANK], kvn)
                kr = _rope_fwd(la[:, Q_RANK + KV_RANK:], cc, sa, sb)
                return cq, ckv, kr

            cq, ckv, kr = _rowwise(
                "mla_latent", latent_post, [lat, mla_q_norm[l].reshape(1, Q_RANK), mla_kv_norm[l].reshape(1, KV_RANK), t_cc, t_sa, t_sb],
                grid=(nt,), in_specs=[_row_spec(tm, LAT_PAD), _const_spec((1, Q_RANK)), _const_spec((1, KV_RANK))] + tab_specs,
                out_shapes=[_sds((T, Q_RANK), BF16), _sds((T, KV_RANK), BF16), _sds((T, 128), BF16)],
                out_specs=[_row_spec(tm, Q_RANK), _row_spec(tm, KV_RANK), _row_spec(tm, 128)])

            def q_epilogue(acc, cc, sa, sb):
                parts = []
                for b in range(HEADS):
                    parts += [acc[:, b * QPAD:b * QPAD + NOPE], _rope_fwd(acc[:, b * QPAD + NOPE:(b + 1) * QPAD], cc, sa, sb)]
                return (jnp.concatenate(parts, axis=1),)

            q = _matmul("mla_q", cq, w_uq, [t_cc, t_sa, t_sb], grid=(nt,), a_spec=_row_spec(tm, Q_RANK),
                        b_spec=pl.BlockSpec((None, Q_RANK, HW), lambda i_: (0, 0, 0)), extra_specs=tab_specs,
                        out_shapes=[_sds((T, HW), BF16)], out_specs=[_row_spec(tm, HW)], dims=NN, epilogue=q_epilogue)[0]

            def kv_write(outs, acc, krb):
                k_ref, v_ref, vt_ref = outs
                for b in range(HEADS):
                    vb = acc[:, b * QPAD + NOPE:(b + 1) * QPAD]
                    k_ref[:, b * QPAD:b * QPAD + NOPE] = acc[:, b * QPAD:b * QPAD + NOPE].astype(BF16)
                    k_ref[:, b * QPAD + NOPE:(b + 1) * QPAD] = krb
                    v_ref[:, b * VDIM:(b + 1) * VDIM] = vb.astype(BF16)
                    vbt = vb.T.astype(BF16)
                    for u in range(tm // tq):
                        vt_ref[b, u] = vbt[:, u * tq:(u + 1) * tq]

            kk, vv, vt = _matmul("mla_kv", ckv, w_ukv, [kr], grid=(nt,), a_spec=_row_spec(tm, KV_RANK),
                                 b_spec=pl.BlockSpec((None, KV_RANK, HW), lambda i_: (0, 0, 0)), extra_specs=[_row_spec(tm, 128)],
                                 out_shapes=[_sds((T, HW), BF16), _sds((T, OW), BF16), _sds((HEADS, T // tq, VDIM, tq), BF16)],
                                 out_specs=[_row_spec(tm, HW), _row_spec(tm, OW), pl.BlockSpec((HEADS, tm // tq, VDIM, tq), lambda i_: (0, i_, 0, 0))],
                                 dims=NN, write=kv_write)
            group = [up_sh[i], down_sh[i], in_sh[l], out_sh[l], up_sh[i + 1]] + (small_later if i == 0 else [])
            o, lse, *bufs = _flash_fwd(q, kk, vt, tq, comm=_gather_level1(group))
            xm, h2, g_up[i], g_down[i] = _matmul(
                "mla_out", o, w_o, [xs, norm_ffn[i].reshape(1, D)], grid=(nt,), a_spec=_row_spec(tm, OW),
                b_spec=pl.BlockSpec((None, OW, D), lambda i_: (0, 0, 0)), extra_specs=[_row_spec(tm, D), _const_spec((1, D))],
                out_shapes=[_sds((T, D), F32), _sds((T, D), BF16)], out_specs=[_row_spec(tm, D)] * 2, dims=NN,
                epilogue=residual_norm, comm=_gather_level2(bufs[:2]))
            half_gathered = bufs[2:]
            mix_saved = (h, lat, cq, ckv, q, kk, vv, o, lse)
        else:
            gp, ge, g_down[i] = proj_cols("sgu_in", h, g_in[l], e2c, _gelu_and_grad, 2, comm=_gather_level2(next_mlp))
            gate = _sgu_mid_fwd(ge, ln_g_full[l], ln_b_full[l], sgu_w_spatial[l], b_sp[l], 4)
            xm, h2 = proj_rows_residual("sgu_out", gate, g_out[l], xs, norm_ffn[i])
            mix_saved = (h, gp, ge, gate)
        r, s, *rest = proj_cols("ffn_up", h2, g_up[i], ffc, lambda acc: (jnp.maximum(acc, 0.0), jnp.square(jnp.maximum(acc, 0.0))), 2,
                                comm=_gather_level2(half_gathered) if i % 2 == 0 else None)
        if i % 2 == 0:
            g_in[l], g_out[l], g_up[i + 1], *small_gathered = rest
        if i == 0:
            for l_ in range(1, n_mla):
                mla_w[l_] = mla_layouts(*small_gathered[4 * (l_ - 1):4 * l_])
            g_ln = small_gathered[-1]
            ln_g_full = [g_ln[:, l_, :].reshape(1, E) for l_ in range(n_sgu)]
            ln_b_full = [g_ln[:, n_sgu + l_, :].reshape(1, E) for l_ in range(n_sgu)]
        xo, h_next, *rest = proj_rows_residual("ffn_down", s, g_down[i], xm, norm_mix[i + 1] if i + 1 < depth else final_norm,
                                               comm=_gather_level1([down_sh[i + 1]]) if i % 2 == 0 else None)
        if i % 2 == 0:
            next_mlp = [rest[0]]
        saved.append((xs, xm, mix_saved, h2, r, s))
        xs, h = xo, h_next

    def loss_head(xv, tg, g):
        y = _rms_fwd(xv, g)
        err = y - tg
        part = 0.5 * jnp.sum(jnp.sum(err * err, axis=-1, keepdims=True), axis=0, keepdims=True) / D
        dx, dg = _rms_bwd(xv, g, err / D)
        return dx, dx, jnp.broadcast_to(part, (1, 128)), dg

    dx, dyb, loss_part, d_final = _rowwise(
        "loss_head", loss_head, [xs, tgt, final_norm.reshape(1, D)], grid=(nt,),
        in_specs=[_row_spec(tm, D), _row_spec(tm, D), _const_spec((1, D))],
        out_shapes=[_sds((T, D), F32), _sds((T, D), BF16), _sds((1, 128), F32), _sds((1, D), F32)],
        out_specs=[_row_spec(tm, D), _row_spec(tm, D), _const_spec((1, 128)), _const_spec((1, D))], n_acc=2)
    loss = lax.psum(loss_part[0, 0], ("x", "y", "c"))

    d_norm_mix, d_norm_ffn = [None] * depth, [None] * depth
    d_qn, d_kvn = [None] * n_mla, [None] * n_mla
    d_wsp, d_bsp, d_lng, d_lnb = [None] * n_sgu, [None] * n_sgu, [None] * n_sgu, [None] * n_sgu
    layers = {"dkv": n_mla, "uq": n_mla, "ukv": n_mla, "o": n_mla, "in": n_sgu, "out": n_sgu, "up": depth, "down": depth}
    stacked = {nm: None for nm in layers}
    pending = []
    summed = []

    def add_pairs(gs, rcvs):
        operands, in_specs, out_shapes, out_specs = [], [], [], []
        for g, rcv in zip(gs, rcvs):
            _, rws, cls = g.shape
            slab = pl.BlockSpec((None, rws, cls), lambda ch, cr: (ch, 0, 0))
            operands += [g.reshape(N_CHIP, 2, rws, cls), rcv]
            in_specs += [pl.BlockSpec((None, None, rws, cls), lambda ch, cr: (ch, cr[0], 0, 0)), slab]
            out_shapes.append(_sds(rcv.shape, BF16))
            out_specs.append(slab)

        def fn(*blocks):
            return tuple(blocks[2 * k].astype(F32) + blocks[2 * k + 1].astype(F32) for k in range(len(gs)))

        return _rowwise("grad_pair_sum", fn, operands, grid=(N_CHIP,), in_specs=in_specs, out_shapes=out_shapes, out_specs=out_specs,
                        grid_spec_prefetch=cidx)

    def sibling_comm():
        return _sibling_exchange([g for _, _, g in pending]) if pending else None

    def absorb(from_sibling):
        if pending:
            parts = add_pairs([g for _, _, g in pending], list(from_sibling))
            summed.extend((nm, l_, p) for (nm, l_, _), p in zip(pending, parts))
            pending.clear()

    def chip_comm():
        if pending:
            absorb(_comm_call("grad_sibling_exchange", sibling_comm()))
        comm, names = _chip_exchange([p for _, _, p in summed], [(nm, l_) for nm, l_, _ in summed], layers, stacked)
        summed.clear()
        return comm, names

    def rows128(a, rows):
        flat = a.reshape(-1, 128)
        return jnp.pad(flat, ((0, rows - flat.shape[0]), (0, 0)))

    def pad_to(n, mult):
        return -(-n // mult) * mult

    def packed(arrs, sizes):
        return jnp.concatenate([rows128(a, sz) for a, sz in zip(arrs, sizes)], axis=0)

    n_wsp, n_bsp, n_ln = sgu_w_spatial.size // 128, pad_to(sgu_b_spatial.size // 128, 8), pad_to(n_sgu * E // 128, 8)
    early_sizes = [n_wsp, pad_to(n_wsp + n_bsp, SMALL_ROWS) - n_wsp, n_ln, n_ln]
    early_rep = early_sizes[0] + early_sizes[1]
    gathered_early = None

    for i in reversed(range(depth)):
        l = i // 2
        xs_i, xm, mix_saved, h2, r, s = saved[i]
        da, *rcv = back_rows("ffn_down_bwd", dyb, g_down[i], ffc, [r], lambda acc, rr: (acc * (2.0 * rr.astype(F32)),), comm=sibling_comm())
        absorb(rcv)
        pending.append(("down", i, wgrad_rows("ffn_down_wgrad", s, dyb, ffc, D, tb).reshape(N_DEV, ffc, D)))
        pending.append(("up", i, wgrad_cols("ffn_up_wgrad", h2, da, ffc)))
        dx, dyb, d_norm_ffn[i], *rcv = back_cols("ffn_up_bwd", da, transposed(g_up[i]), xm, norm_ffn[i], dx, comm=sibling_comm())
        absorb(rcv)
        if i % 2 == 0:
            h, lat, cq, ckv, q, kk, vv, o, lse = mix_saved
            w_dkv, w_uq, w_ukv, w_o = mla_w[l]
            do = _matmul("mla_out_bwd", dyb, w_o, [], grid=(nt,), a_spec=_row_spec(tm, D),
                         b_spec=pl.BlockSpec((None, OW, D), lambda i_: (0, 0, 0)), extra_specs=[],
                         out_shapes=[_sds((T, OW), BF16)], out_specs=[_row_spec(tm, OW)], dims=NT)[0]
            g_o_l = wgrad_rows("mla_out_wgrad", o, dyb, OW, D, tm).reshape(N_DEV, owc, D)
            comm, names = chip_comm()
            if i == 0:
                early = packed([jnp.stack(d_wsp, 0), jnp.stack(d_bsp, 0), jnp.concatenate(d_lng, 0), jnp.concatenate(d_lnb, 0)], early_sizes)
                comm = _merge_comm(comm, _gather_level1([early]))
            dq_pre, dk, dv, *bufs = _flash_bwd(q, kk, vv, o, do, lse, (t_cc, t_sa, t_sb), tq, comm=comm)
            stacked.update(dict(zip(names, bufs)))
            pending.append(("o", l, g_o_l))

            def kv_pre(dkb, dvb, cc, sa, sb):
                parts, dkr = [], None
                for b in range(HEADS):
                    parts += [dkb[:, b * QPAD:b * QPAD + NOPE], dvb[:, b * VDIM:(b + 1) * VDIM]]
                    piece = dkb[:, b * QPAD + NOPE:(b + 1) * QPAD].astype(F32)
                    dkr = piece if dkr is None else dkr + piece
                return jnp.concatenate(parts, axis=1), _rope_bwd(dkr, cc, sa, sb)

            dkv, dkr, *rest = _rowwise("mla_dkv_rope", kv_pre, [dk, dv, t_cc, t_sa, t_sb], grid=(T // ts,),
                                       in_specs=[_row_spec(ts, HW), _row_spec(ts, OW)] + [_row_spec(ts, 128)] * 3,
                                       out_shapes=[_sds((T, HW), BF16), _sds((T, 128), F32)], out_specs=[_row_spec(ts, HW), _row_spec(ts, 128)],
                                       comm=_gather_level2(bufs[len(names):]) if i == 0 else None)
            if i == 0:
                (gathered_early,) = rest
            g_uq_l = wgrad_rows("mla_q_wgrad", cq, dq_pre, Q_RANK, HW, tm)
            g_ukv_l = wgrad_rows("mla_kv_wgrad", ckv, dkv, KV_RANK, HW, tm)
            pending.append(("uq", l, g_uq_l.reshape(Q_RANK, HEADS, QPAD)[:, :, :NOPE + ROPE].transpose(1, 0, 2)))
            pending.append(("ukv", l, g_ukv_l.reshape(KV_RANK, HEADS, NOPE + VDIM).transpose(1, 0, 2)))
            dcq = _matmul("mla_q_bwd", dq_pre, w_uq, [], grid=(nt,), a_spec=_row_spec(tm, HW),
                          b_spec=pl.BlockSpec((None, Q_RANK, HW), lambda i_: (0, 0, 0)), extra_specs=[],
                          out_shapes=[_sds((T, Q_RANK), F32)], out_specs=[_row_spec(tm, Q_RANK)], dims=NT)[0]
            dckv = _matmul("mla_kv_bwd", dkv, w_ukv, [], grid=(nt,), a_spec=_row_spec(tm, HW),
                           b_spec=pl.BlockSpec((None, KV_RANK, HW), lambda i_: (0, 0, 0)), extra_specs=[],
                           out_shapes=[_sds((T, KV_RANK), F32)], out_specs=[_row_spec(tm, KV_RANK)], dims=NT)[0]

            def latent_bwd(la, qn, kvn, dq_, dkv_, dkr_):
                dcq_raw, dqn = _rms_bwd(la[:, :Q_RANK], qn, dq_)
                dckv_raw, dkvn = _rms_bwd(la[:, Q_RANK:Q_RANK + KV_RANK], kvn, dkv_)
                return jnp.concatenate([dcq_raw, dckv_raw, dkr_], axis=1), dqn, dkvn

            dlat, d_qn[l], d_kvn[l] = _rowwise(
                "mla_latent_bwd", latent_bwd, [lat, mla_q_norm[l].reshape(1, Q_RANK), mla_kv_norm[l].reshape(1, KV_RANK), dcq, dckv, dkr],
                grid=(nt,), in_specs=[_row_spec(tm, LAT_PAD), _const_spec((1, Q_RANK)), _const_spec((1, KV_RANK)),
                                      _row_spec(tm, Q_RANK), _row_spec(tm, KV_RANK), _row_spec(tm, 128)],
                out_shapes=[_sds((T, LAT_PAD), BF16), _sds((1, Q_RANK), F32), _sds((1, KV_RANK), F32)],
                out_specs=[_row_spec(tm, LAT_PAD), _const_spec((1, Q_RANK)), _const_spec((1, KV_RANK))], n_acc=2)
            g_dkv_l = wgrad_rows("mla_down_wgrad", h, dlat, D, LAT_PAD, tm)
            pending.append(("dkv", l, g_dkv_l[:, :LAT].reshape(N_DEV, dc, LAT)))
            dx, dyb, d_norm_mix[i] = _matmul(
                "mla_down_bwd", dlat, w_dkv, [xs_i, norm_mix[i].reshape(1, D), dx], grid=(nt,), a_spec=_row_spec(tm, LAT_PAD),
                b_spec=pl.BlockSpec((None, D, LAT_PAD), lambda i_: (0, 0, 0)), extra_specs=[_row_spec(tm, D), _const_spec((1, D)), _row_spec(tm, D)],
                out_shapes=[_sds((T, D), F32), _sds((T, D), BF16), _sds((1, D), F32)],
                out_specs=[_row_spec(tm, D), _row_spec(tm, D), _const_spec((1, D))], dims=NT, epilogue=norm_bwd_epilogue, n_sum=1)
        else:
            h, gp, ge, gate = mix_saved
            (dgate,) = back_rows("sgu_out_bwd", dyb, g_out[l], ec, [], None)
            pending.append(("out", l, wgrad_rows("sgu_out_wgrad", gate, dyb, ec, D, tb).reshape(N_DEV, ec, D)))
            dz, d_wsp[l], d_bsp[l], d_lng[l], d_lnb[l] = _sgu_mid_bwd(ge, gp, dgate, ln_g_full[l], ln_b_full[l], sgu_w_spatial[l], b_sp[l], 2)
            pending.append(("in", l, wgrad_cols("sgu_in_wgrad", h, dz, e2c)))
            dx, dyb, d_norm_mix[i], *rcv = back_cols("sgu_in_bwd", dz, transposed(g_in[l]), xs_i, norm_mix[i], dx, comm=sibling_comm())
            absorb(rcv)
    grad_x = dx.reshape(1, T, D)

    last_comm, last_names = chip_comm()
    late_g = [jnp.concatenate(d_norm_mix, 0), jnp.concatenate(d_norm_ffn, 0), d_final, jnp.concatenate(d_qn, 0), jnp.concatenate(d_kvn, 0)]
    late_w = [norm_mix, norm_ffn, final_norm, mla_q_norm, mla_kv_norm]
    late_m = [m_norm_mix, m_norm_ffn, m_final_norm, m_mla_q_norm, m_mla_kv_norm]
    late_v = [v_norm_mix, v_norm_ffn, v_final_norm, v_mla_q_norm, v_mla_kv_norm]
    late_sizes = [pad_to(g.size // 128, 8) for g in late_g]
    late_rows = sum(late_sizes)

    def adam_big(parts, w, m, v):
        lyr, rws, cls = w.shape
        rt = _tile(rws, 512)

        def fn(p, w_, m_, v_):
            g = (p[0].astype(F32) + p[1].astype(F32)) + (p[2].astype(F32) + p[3].astype(F32))
            return (g, *_adam(w_, g, m_, v_))

        spec = pl.BlockSpec((None, rt, cls), lambda l_, i_: (l_, i_, 0))
        return _rowwise("adam_large", fn, [parts, w, m, v], grid=(lyr, rws // rt),
                        in_specs=[pl.BlockSpec((N_CHIP, None, rt, cls), lambda l_, i_: (0, l_, i_, 0)), spec, spec, spec],
                        out_shapes=[_sds(w.shape, F32)] * 4, out_specs=[spec] * 4)

    stacked.update(dict(zip(last_names, _comm_call("grad_chip_exchange", last_comm))))
    (gathered_late,) = _all_gather("gather_small_grads", [packed(late_g, late_sizes)])
    big = {}
    big["in"] = adam_big(stacked["in"], sgu_w_in, m_sgu_w_in, v_sgu_w_in)
    big["up"] = adam_big(stacked["up"], ffn_w_up, m_ffn_w_up, v_ffn_w_up)
    big["down"] = adam_big(stacked["down"], ffn_w_down, m_ffn_w_down, v_ffn_w_down)
    big["out"] = adam_big(stacked["out"], sgu_w_out, m_sgu_w_out, v_sgu_w_out)
    big["dkv"] = adam_big(stacked["dkv"], mla_w_dkv, m_mla_w_dkv, v_mla_w_dkv)
    big["uq"] = adam_big(stacked["uq"], mla_w_uq, m_mla_w_uq, v_mla_w_uq)
    big["ukv"] = adam_big(stacked["ukv"], mla_w_ukv, m_mla_w_ukv, v_mla_w_ukv)
    big["o"] = adam_big(stacked["o"], mla_w_o, m_mla_w_o, v_mla_w_o)
    big_res = [big[nm][:4] for nm in ("dkv", "uq", "ukv", "o", "in", "out", "up", "down")]

    def sum8(p):
        return ((p[0] + p[1]) + (p[2] + p[3])) + ((p[4] + p[5]) + (p[6] + p[7]))

    def adam_packed(name, gathered, ws, ms, vs, sizes, rows, tile):
        spec = _row_spec(tile, 128)
        return _rowwise(name, lambda p, w_, m_, v_: (sum8(p), *_adam(w_, sum8(p), m_, v_)),
                        [gathered, packed(ws, sizes), packed(ms, sizes), packed(vs, sizes)], grid=(rows // tile,),
                        in_specs=[pl.BlockSpec((N_DEV, tile, 128), lambda i_: (0, i_, 0)), spec, spec, spec],
                        out_shapes=[_sds((rows, 128), F32)] * 4, out_specs=[spec] * 4)

    late_res = adam_packed("adam_small", gathered_late, late_w, late_m, late_v, late_sizes, late_rows, late_rows)
    early_res = adam_packed("adam_spatial", gathered_early, [sgu_w_spatial, sgu_b_spatial], [m_sgu_w_spatial, m_sgu_b_spatial],
                            [v_sgu_w_spatial, v_sgu_b_spatial], early_sizes[:2], early_rep, SMALL_ROWS)

    def unpack(res, sizes, k, like):
        off = sum(sizes[:k])
        return res[off:off + like.size // 128].reshape(like.shape)

    my_b = 4 * lax.axis_index("x") + 2 * lax.axis_index("y") + lax.axis_index("c")
    ln_w = jnp.concatenate([sgu_ln_g, sgu_ln_b], 0)
    ln_m = jnp.concatenate([m_sgu_ln_g, m_sgu_ln_b], 0)
    ln_v = jnp.concatenate([v_sgu_ln_g, v_sgu_ln_b], 0)
    ln_all = jnp.concatenate([gathered_early[:, early_rep:early_rep + n_sgu * E // 128], gathered_early[:, early_rep + n_ln:early_rep + n_ln + n_sgu * E // 128]], axis=1)
    ln_mine = lax.dynamic_slice_in_dim(ln_all.reshape(N_DEV, 2 * n_sgu, N_DEV, ec), my_b, 1, axis=2).reshape(N_DEV, 2 * n_sgu, ec)
    ln_g_, ln_d, ln_m2, ln_v2 = _rowwise(
        "adam_ln", lambda p, w_, m_, v_: (sum8(p), *_adam(w_, sum8(p), m_, v_)), [ln_mine, ln_w, ln_m, ln_v], grid=(1,),
        in_specs=[_const_spec(ln_mine.shape), _const_spec(ln_w.shape), _const_spec(ln_w.shape), _const_spec(ln_w.shape)],
        out_shapes=[_sds(ln_w.shape, F32)] * 4, out_specs=[_const_spec(ln_w.shape)] * 4)

    def family(pos):
        ln = [ln_g_, ln_d, ln_m2, ln_v2][pos]
        late = [unpack(late_res[pos], late_sizes, k, w_) for k, w_ in enumerate(late_w)]
        w_sp_, b_sp_ = unpack(early_res[pos], early_sizes, 0, sgu_w_spatial), unpack(early_res[pos], early_sizes, 1, sgu_b_spatial)
        bigs = [res[pos] for res in big_res]
        return [late[0], late[1], late[2], bigs[0], late[3], late[4], bigs[1], bigs[2], bigs[3],
                bigs[4], ln[:n_sgu], ln[n_sgu:], w_sp_, b_sp_, bigs[5], bigs[6], bigs[7]]

    return (loss, grad_x, *family(0), *family(1), *family(2), *family(3))
```

```python
import math

import jax
import jax.numpy as jnp
from jax import lax
from jax.experimental import pallas as pl
from jax.experimental.pallas import tpu as pltpu

F32 = jnp.float32
BF16 = jnp.bfloat16
MESH = pl.DeviceIdType.MESH

N_DEV = 8
N_CHIP = 4
HEADS = 8
NOPE = 128
ROPE = 64
VDIM = 128
QPAD = 256
Q_RANK = 256
KV_RANK = 128
LAT = Q_RANK + KV_RANK + ROPE
LAT_PAD = 512
ROPE_THETA = 10000.0
SGU_CHUNK = 128
SGU_GROUPS = 8
NORM_EPS = 1e-6
LN_EPS = 1e-5
ADAM_LR = 0.001
ADAM_B1 = 0.9
ADAM_B2 = 0.999
ADAM_EPS = 1e-08
ADAM_WD = 0.01
ADAM_STEP = 10
ATTN_SCALE = (NOPE + ROPE) ** -0.5
NEG = -1e30
EXP2_SCALE = ATTN_SCALE * math.log2(math.e)
VMEM_LIMIT = 56 * 1024 * 1024
SMALL_ROWS = 256

NN = (((1,), (0,)), ((), ()))
NT = (((1,), (1,)), ((), ()))
TN = (((0,), (0,)), ((), ()))
ANY = pl.BlockSpec(memory_space=pl.ANY)


def _pcall(body, **kw):
    return pl.pallas_call(body, **kw)


def _params(n_grid, side_effects=False):
    return pltpu.CompilerParams(dimension_semantics=("arbitrary",) * n_grid, vmem_limit_bytes=VMEM_LIMIT, has_side_effects=side_effects)


def _sds(shape, dtype):
    return jax.ShapeDtypeStruct(tuple(shape), dtype)


def _tile(n, want):
    t = min(n, want)
    assert n % t == 0, (n, want)
    return t


class _Comm:
    def __init__(self, operands, out_shapes, aliases, scratch, start, finish):
        self.operands, self.out_shapes, self.aliases, self.scratch = operands, out_shapes, aliases, scratch
        self.start, self.finish = start, finish


def _merge_comm(first, second):
    n_in, n_out, n_sc = len(first.operands), len(first.out_shapes), len(first.scratch)
    aliases = dict(first.aliases)
    aliases.update({n_in + k: n_out + v for k, v in second.aliases.items()})

    def start(ins, outs, sems):
        first.start(ins[:n_in], outs[:n_out], sems[:n_sc])
        second.start(ins[n_in:], outs[n_out:], sems[n_sc:])

    def finish(ins, outs, sems):
        first.finish(ins[:n_in], outs[:n_out], sems[:n_sc])
        second.finish(ins[n_in:], outs[n_out:], sems[n_sc:])

    return _Comm([*first.operands, *second.operands], [*first.out_shapes, *second.out_shapes], aliases,
                 [*first.scratch, *second.scratch], start, finish)


def _place():
    return lax.axis_index("x"), lax.axis_index("y"), lax.axis_index("c")


def _other_chips(x, y):
    return [(1 - x, y), (x, 1 - y), (1 - x, 1 - y)]


def _dev_index(dev):
    return 4 * dev[0] + 2 * dev[1] + dev[2]


def _comm_call(name, comm):
    c_in, c_out = len(comm.operands), len(comm.out_shapes)

    def body(*refs):
        ins, outs, sems = refs[:c_in], refs[c_in:c_in + c_out], refs[c_in + c_out:]
        comm.start(ins, outs, sems)
        comm.finish(ins, outs, sems)

    return _pcall(body, name=name, in_specs=[ANY] * c_in, out_specs=[ANY] * c_out, out_shape=comm.out_shapes,
                  scratch_shapes=comm.scratch, input_output_aliases=dict(comm.aliases),
                  compiler_params=pltpu.CompilerParams(has_side_effects=True))(*comm.operands)


def _call(name, body, operands, in_specs, out_shapes, out_specs, scratch, grid, comm=None):
    if comm is None:
        return _pcall(body, name=name, grid=grid, in_specs=in_specs, out_specs=out_specs, out_shape=out_shapes,
                      scratch_shapes=scratch, compiler_params=_params(len(grid)))(*operands)
    n_in, n_out, n_sc = len(operands), len(out_shapes), len(scratch)
    c_in, c_out = len(comm.operands), len(comm.out_shapes)

    def hosted(*refs):
        ins, cins = refs[:n_in], refs[n_in:n_in + c_in]
        o0 = n_in + c_in
        outs, couts = refs[o0:o0 + n_out], refs[o0 + n_out:o0 + n_out + c_out]
        rest = refs[o0 + n_out + c_out:]
        sc, csems = rest[:n_sc], rest[n_sc:]
        first = pl.program_id(0) == 0
        last = pl.program_id(0) == grid[0] - 1
        for d in range(1, len(grid)):
            first = jnp.logical_and(first, pl.program_id(d) == 0)
            last = jnp.logical_and(last, pl.program_id(d) == grid[d] - 1)

        @pl.when(first)
        def _():
            comm.start(cins, couts, csems)

        body(*ins, *outs, *sc)

        @pl.when(last)
        def _():
            comm.finish(cins, couts, csems)

    return _pcall(hosted, name=name, grid=grid, in_specs=[*in_specs, *[ANY] * c_in], out_specs=[*out_specs, *[ANY] * c_out],
                  out_shape=[*out_shapes, *comm.out_shapes], scratch_shapes=[*scratch, *comm.scratch],
                  input_output_aliases={n_in + k: n_out + v for k, v in comm.aliases.items()},
                  compiler_params=_params(len(grid), side_effects=True))(*operands, *comm.operands)


def _gather_level1(shards):
    n = len(shards)

    def copies(ins, outs, sems):
        send_sems, recv_sems, local_sems = sems
        x, y, c = _place()
        me, sibling = (x, y, c), (x, y, 1 - c)
        chips = _other_chips(x, y)

        def copy(a, k, block, to, src=None):
            slot = outs[a].at[_dev_index(block)]
            return pltpu.make_async_remote_copy(src_ref=slot if src is None else src, dst_ref=slot, send_sem=send_sems.at[a, k],
                                                recv_sem=recv_sems.at[a, k], device_id=to, device_id_type=MESH)

        mine = [pltpu.make_async_copy(ins[a], outs[a].at[_dev_index(me)], local_sems.at[a]) for a in range(n)]
        sends = [copy(a, 1 + j, me, (*chip, c), src=ins[a]) for j, chip in enumerate(chips) for a in range(n)]
        sends += [copy(a, 0, me, sibling, src=ins[a]) for a in range(n)]
        recvs = [copy(a, 1 + j, (*chip, c), me) for j, chip in enumerate(chips) for a in range(n)]
        recvs += [copy(a, 0, sibling, me) for a in range(n)]
        return mine, sends, recvs

    def start(ins, outs, sems):
        mine, sends, _ = copies(ins, outs, sems)
        for cp in mine + sends:
            cp.start()

    def finish(ins, outs, sems):
        mine, sends, recvs = copies(ins, outs, sems)
        for cp in recvs:
            cp.wait_recv()
        for cp in sends:
            cp.wait_send()
        for cp in mine:
            cp.wait()

    return _Comm(shards, [_sds((N_DEV, *a.shape), a.dtype) for a in shards], {},
                 [pltpu.SemaphoreType.DMA((n, 4)), pltpu.SemaphoreType.DMA((n, 4)), pltpu.SemaphoreType.DMA((n,))], start, finish)


def _gather_level2(bufs):
    n = len(bufs)

    def copies(outs, sems):
        send_sems, recv_sems = sems
        x, y, c = _place()
        sibling = (x, y, 1 - c)
        sends, recvs = [], []
        for j, chip in enumerate(_other_chips(x, y)):
            for a in range(n):
                have, want = outs[a].at[_dev_index((*chip, c))], outs[a].at[_dev_index((*chip, 1 - c))]
                sends.append(pltpu.make_async_remote_copy(src_ref=have, dst_ref=have, send_sem=send_sems.at[a, j], recv_sem=recv_sems.at[a, j],
                                                          device_id=sibling, device_id_type=MESH))
                recvs.append(pltpu.make_async_remote_copy(src_ref=want, dst_ref=want, send_sem=send_sems.at[a, j], recv_sem=recv_sems.at[a, j],
                                                          device_id=sibling, device_id_type=MESH))
        return sends, recvs

    def start(ins, outs, sems):
        for cp in copies(outs, sems)[0]:
            cp.start()

    def finish(ins, outs, sems):
        sends, recvs = copies(outs, sems)
        for cp in recvs:
            cp.wait_recv()
        for cp in sends:
            cp.wait_send()

    return _Comm(bufs, [_sds(b.shape, b.dtype) for b in bufs], {a: a for a in range(n)},
                 [pltpu.SemaphoreType.DMA((n, 3)), pltpu.SemaphoreType.DMA((n, 3))], start, finish)


def _all_gather(name, arrays):
    n = len(arrays)

    def body(*refs):
        ins = refs[:n]
        outs = refs[n:2 * n]
        send_sems, recv_sems, local_sems = refs[2 * n:]
        x, y, c = _place()
        me, sibling = (x, y, c), (x, y, 1 - c)
        chips = _other_chips(x, y)

        def copy(a, k, block, to, src=None):
            slot = outs[a].at[_dev_index(block)]
            return pltpu.make_async_remote_copy(src_ref=slot if src is None else src, dst_ref=slot, send_sem=send_sems.at[a, k],
                                                recv_sem=recv_sems.at[a, k], device_id=to, device_id_type=MESH)

        mine = [pltpu.make_async_copy(ins[a], outs[a].at[_dev_index(me)], local_sems.at[a]) for a in range(n)]
        for cp in mine:
            cp.start()
        first = []
        for j, chip in enumerate(chips):
            first += [copy(a, 1 + j, me, (*chip, c), src=ins[a]) for a in range(n)]
        first += [copy(a, 0, me, sibling, src=ins[a]) for a in range(n)]
        for cp in first:
            cp.start()
        passed = []
        for j, chip in enumerate(chips):
            for a in range(n):
                copy(a, 1 + j, (*chip, c), me).wait_recv()
                fwd = copy(a, 4 + j, (*chip, c), sibling)
                fwd.start()
                passed.append(fwd)
        for a in range(n):
            copy(a, 0, sibling, me).wait_recv()
            for j, chip in enumerate(chips):
                copy(a, 4 + j, (*chip, 1 - c), me).wait_recv()
        for cp in first + passed:
            cp.wait_send()
        for cp in mine:
            cp.wait()

    return _pcall(
        body, name=name, in_specs=[ANY] * n, out_specs=[ANY] * n,
        out_shape=[_sds((N_DEV, *a.shape), a.dtype) for a in arrays],
        scratch_shapes=[pltpu.SemaphoreType.DMA((n, 7)), pltpu.SemaphoreType.DMA((n, 7)), pltpu.SemaphoreType.DMA((n,))],
        compiler_params=pltpu.CompilerParams(has_side_effects=True),
    )(*arrays)


def _sibling_exchange(grads):
    n = len(grads)

    def start(ins, outs, sems):
        send_sems, recv_sems = sems
        x, y, c = _place()
        for a in range(n):
            for ch in range(N_CHIP):
                pltpu.make_async_remote_copy(src_ref=ins[a].at[2 * ch + 1 - c], dst_ref=outs[a].at[ch], send_sem=send_sems.at[a],
                                             recv_sem=recv_sems.at[a], device_id=(x, y, 1 - c), device_id_type=MESH).start()

    def finish(ins, outs, sems):
        send_sems, recv_sems = sems
        x, y, c = _place()
        for a in range(n):
            pltpu.make_async_remote_copy(src_ref=outs[a], dst_ref=outs[a], send_sem=send_sems.at[a], recv_sem=recv_sems.at[a],
                                         device_id=(x, y, 1 - c), device_id_type=MESH).wait()

    return _Comm(grads, [_sds((N_CHIP, *g.shape[1:]), g.dtype) for g in grads], {},
                 [pltpu.SemaphoreType.DMA((n,)), pltpu.SemaphoreType.DMA((n,))], start, finish)


def _chip_exchange(parts, slots, layers, stacked):
    n = len(parts)
    names = []
    for nm, _ in slots:
        if nm not in names:
            names.append(nm)
    shapes = {nm: _sds((N_CHIP, layers[nm], *parts[a].shape[1:]), parts[a].dtype) for a, (nm, _) in enumerate(slots)}
    kept = [nm for nm in names if stacked.get(nm) is not None]
    aliases = {n + k: names.index(nm) for k, nm in enumerate(kept)}

    def copies(ins, outs, sems):
        send_sems, recv_sems, local_sems = sems
        x, y, c = _place()
        mine = 2 * x + y
        local, sends, recvs = [], [], []
        for a, (nm, l) in enumerate(slots):
            buf = outs[names.index(nm)]
            local.append(pltpu.make_async_copy(ins[a].at[mine], buf.at[mine, l], local_sems.at[a]))
            for j, chip in enumerate(_other_chips(x, y)):
                theirs = buf.at[2 * chip[0] + chip[1], l]
                sends.append(pltpu.make_async_remote_copy(src_ref=ins[a].at[2 * chip[0] + chip[1]], dst_ref=buf.at[mine, l], send_sem=send_sems.at[a, j],
                                                          recv_sem=recv_sems.at[a, j], device_id=(*chip, c), device_id_type=MESH))
                recvs.append(pltpu.make_async_remote_copy(src_ref=theirs, dst_ref=theirs, send_sem=send_sems.at[a, j],
                                                          recv_sem=recv_sems.at[a, j], device_id=(*chip, c), device_id_type=MESH))
        return local, sends, recvs

    def start(ins, outs, sems):
        local, sends, _ = copies(ins, outs, sems)
        for cp in local + sends:
            cp.start()

    def finish(ins, outs, sems):
        local, sends, recvs = copies(ins, outs, sems)
        for cp in recvs:
            cp.wait_recv()
        for cp in sends:
            cp.wait_send()
        for cp in local:
            cp.wait()

    comm = _Comm([*parts, *[stacked[nm] for nm in kept]], [shapes[nm] for nm in names], aliases,
                 [pltpu.SemaphoreType.DMA((n, 3)), pltpu.SemaphoreType.DMA((n, 3)), pltpu.SemaphoreType.DMA((n,))], start, finish)
    return comm, names


def _matmul(name, a, b, extras, *, grid, a_spec, b_spec, extra_specs, out_shapes, out_specs, dims, k_axis=None, nk=1,
            acc_shape=None, epilogue=None, comm=None, n_sum=0, write=None):
    n_extra = len(extras)
    n_out = len(out_shapes)

    def body(*refs):
        a_ref, b_ref = refs[0], refs[1]
        ex = refs[2:2 + n_extra]
        outs = refs[2 + n_extra:2 + n_extra + n_out]
        prod = lax.dot_general(a_ref[...], b_ref[...], dims, preferred_element_type=F32)

        def finish(acc):
            if write is not None:
                write(outs, acc, *[e[...] for e in ex])
                return
            res = epilogue(acc, *[e[...] for e in ex]) if epilogue is not None else (acc,)
            first = None
            for d in range(len(grid)):
                if d != k_axis:
                    here = pl.program_id(d) == 0
                    first = here if first is None else jnp.logical_and(first, here)
            for idx, (o, r) in enumerate(zip(outs, res)):
                if idx < n_out - n_sum:
                    o[...] = r.astype(o.dtype)
                else:
                    @pl.when(first)
                    def _(o=o, r=r):
                        o[...] = r.astype(o.dtype)

                    @pl.when(jnp.logical_not(first))
                    def _(o=o, r=r):
                        o[...] += r.astype(o.dtype)

        if k_axis is None:
            finish(prod)
        else:
            acc_ref = refs[-1]
            k = pl.program_id(k_axis)

            @pl.when(k == 0)
            def _():
                acc_ref[...] = prod

            @pl.when(k > 0)
            def _():
                acc_ref[...] += prod

            @pl.when(k == nk - 1)
            def _():
                finish(acc_ref[...])

    scratch = [] if k_axis is None else [pltpu.VMEM(acc_shape, F32)]
    return _call(name, body, [a, b, *extras], [a_spec, b_spec, *extra_specs], list(out_shapes), list(out_specs), scratch, grid, comm)


def _rowwise(name, fn, operands, *, grid, in_specs, out_shapes, out_specs, n_acc=0, grid_spec_prefetch=None, comm=None):
    n_in = len(operands)
    n_out = len(out_shapes)
    n_pre = 0 if grid_spec_prefetch is None else 1

    def body(*refs):
        refs = refs[n_pre:]
        ins = refs[:n_in]
        outs = refs[n_in:n_in + n_out]
        res = fn(*[r[...] for r in ins])
        if not isinstance(res, (tuple, list)):
            res = (res,)
        first = pl.program_id(0) == 0
        for d in range(1, len(grid)):
            first = jnp.logical_and(first, pl.program_id(d) == 0)
        for idx, (o, r) in enumerate(zip(outs, res)):
            if idx < n_out - n_acc:
                o[...] = r.astype(o.dtype)
            else:
                @pl.when(first)
                def _(o=o, r=r):
                    o[...] = r.astype(o.dtype)

                @pl.when(jnp.logical_not(first))
                def _(o=o, r=r):
                    o[...] += r.astype(o.dtype)

    if comm is not None:
        return _call(name, body, list(operands), list(in_specs), list(out_shapes), list(out_specs), [], grid, comm)
    if grid_spec_prefetch is None:
        return _pcall(body, name=name, grid=grid, in_specs=in_specs, out_specs=out_specs, out_shape=out_shapes,
                      compiler_params=_params(len(grid)))(*operands)
    gs = pltpu.PrefetchScalarGridSpec(num_scalar_prefetch=1, grid=grid, in_specs=in_specs, out_specs=out_specs)
    return _pcall(body, name=name, grid_spec=gs, out_shape=out_shapes,
                  compiler_params=_params(len(grid)))(grid_spec_prefetch, *operands)


def _row_spec(tm, w):
    return pl.BlockSpec((tm, w), lambda i: (i, 0))


def _const_spec(shape):
    nd = len(shape)
    return pl.BlockSpec(tuple(shape), lambda *_: (0,) * nd)


def _rms_fwd(x, g):
    r = lax.rsqrt(jnp.mean(x * x, axis=-1, keepdims=True) + NORM_EPS)
    return x * r * g


def _rms_bwd(x, g, dy):
    r = lax.rsqrt(jnp.mean(x * x, axis=-1, keepdims=True) + NORM_EPS)
    xh = x * r
    u = dy * g
    dx = r * (u - xh * jnp.mean(u * xh, axis=-1, keepdims=True))
    dg = jnp.sum(dy * xh, axis=0, keepdims=True)
    return dx, dg


def _gelu_and_grad(z):
    cdf = 0.5 * (1.0 + lax.erf(z * (2.0 ** -0.5)))
    return cdf + z * jnp.exp(-0.5 * z * z) * ((2.0 * math.pi) ** -0.5), z * cdf


def _rope_fwd(x, cc, sa, sb):
    return x * cc + pltpu.roll(x, 96, 1) * sa + pltpu.roll(x, 32, 1) * sb


def _rope_bwd(d, cc, sa, sb):
    return d * cc + pltpu.roll(d * sa, 32, 1) + pltpu.roll(d * sb, 96, 1)


def _adam(w, g, m, v):
    m = ADAM_B1 * m + (1.0 - ADAM_B1) * g
    v = ADAM_B2 * v + (1.0 - ADAM_B2) * (g * g)
    m_hat = m / (1.0 - ADAM_B1 ** ADAM_STEP)
    v_hat = v / (1.0 - ADAM_B2 ** ADAM_STEP)
    delta = -ADAM_LR * (m_hat / (jnp.sqrt(v_hat) + ADAM_EPS) + ADAM_WD * w)
    return delta, m, v


def _flash_fwd(q, k, vt, tq, comm=None):
    h, t = vt.shape[0], q.shape[0]
    nq = t // tq

    chunk_blocks = [c for c in (4, 2, 1) if c < nq]

    def body(q_ref, k_ref, vt_ref, o_ref, lse_ref, m_ref, l_ref, acc_ref):
        qi = pl.program_id(1)
        m_ref[...] = jnp.full((1, tq), NEG, F32)
        l_ref[...] = jnp.zeros((1, tq), F32)
        acc_ref[...] = jnp.zeros((VDIM, tq), F32)

        def update(kb0, nblk, masked):
            kb = k_ref[pl.ds(pl.multiple_of(kb0 * tq, tq), nblk * tq), :]
            st = lax.dot_general(kb, q_ref[...], NT, preferred_element_type=F32)
            if masked:
                key = lax.broadcasted_iota(jnp.int32, (tq, tq), 0)
                qry = lax.broadcasted_iota(jnp.int32, (tq, tq), 1)
                st = jnp.where(key <= qry, st, NEG)
            m_old = m_ref[...]
            m_new = jnp.maximum(m_old, jnp.max(st, axis=0, keepdims=True))
            alpha = jnp.exp2((m_old - m_new) * EXP2_SCALE)
            pt = jnp.exp2((st - m_new) * EXP2_SCALE)
            l_ref[...] = alpha * l_ref[...] + jnp.sum(pt, axis=0, keepdims=True)
            ptb = pt.astype(BF16)
            pv = lax.dot_general(vt_ref[kb0], ptb[:tq], NN, preferred_element_type=F32)
            for j in range(1, nblk):
                pv += lax.dot_general(vt_ref[kb0 + j], ptb[j * tq:(j + 1) * tq], NN, preferred_element_type=F32)
            acc_ref[...] = alpha * acc_ref[...] + pv
            m_ref[...] = m_new

        start = jnp.int32(0)
        for c in chunk_blocks:
            take = (qi & c) != 0

            @pl.when(take)
            def _(start=start, c=c):
                update(start, c, False)

            start = start + jnp.where(take, c, 0)
        update(qi, 1, True)
        l = l_ref[...]
        o_ref[...] = (acc_ref[...] / l).T.astype(o_ref.dtype)
        lse_ref[...] = m_ref[...] * EXP2_SCALE + jnp.log2(l)

    return _call(
        "flash_fwd", body, [q, k, vt],
        [pl.BlockSpec((tq, QPAD), lambda hh, i: (i, hh)),
         pl.BlockSpec((t, QPAD), lambda hh, i: (0, hh)),
         pl.BlockSpec((None, nq, VDIM, tq), lambda hh, i: (hh, 0, 0, 0))],
        [_sds((t, h * VDIM), BF16), _sds((h, nq, 1, tq), F32)],
        [pl.BlockSpec((tq, VDIM), lambda hh, i: (i, hh)),
         pl.BlockSpec((None, None, 1, tq), lambda hh, i: (hh, i, 0, 0))],
        [pltpu.VMEM((1, tq), F32), pltpu.VMEM((1, tq), F32), pltpu.VMEM((VDIM, tq), F32)], (h, nq), comm)


def _flash_bwd(q, k, v, o, do, lse, tabs, tq, comm=None):
    t = q.shape[0]
    h = q.shape[1] // QPAD
    nq = t // tq

    def body(q_ref, k_ref, v_ref, o_ref, do_ref, lse_ref, cc_ref, sa_ref, sb_ref, dq_ref, dk_out, dv_out, delta_ref, dqt_ref, dk_ref, dv_ref):
        kj = pl.program_id(1)

        @pl.when(kj == 0)
        def _():
            dqt_ref[...] = jnp.zeros_like(dqt_ref)
            ones = jnp.ones((8, VDIM), BF16)
            for qi in range(nq):
                rows = pl.ds(qi * tq, tq)
                prod = do_ref[rows, :].astype(F32) * o_ref[rows, :].astype(F32)
                hi = prod.astype(BF16)
                lo = (prod - hi.astype(F32)).astype(BF16)
                delta_ref[qi] = (lax.dot_general(ones, hi, NT, preferred_element_type=F32)
                                 + lax.dot_general(ones, lo, NT, preferred_element_type=F32))

        kb = k_ref[...]
        vb = v_ref[...]
        kbt = kb.astype(F32).T.astype(BF16)
        dk_ref[...] = jnp.zeros_like(dk_ref)
        dv_ref[...] = jnp.zeros_like(dv_ref)

        def step(q0, nblk, masked):
            rows = pl.ds(pl.multiple_of(q0 * tq, tq), nblk * tq)
            qb = q_ref[rows, :]
            dob = do_ref[rows, :]
            lse = jnp.concatenate([lse_ref[q0 + j] for j in range(nblk)], axis=1)
            delta = jnp.concatenate([delta_ref[q0 + j, pl.ds(0, 1), :] for j in range(nblk)], axis=1)
            st = lax.dot_general(kb, qb, NT, preferred_element_type=F32)
            pt = jnp.exp2(st * EXP2_SCALE - lse)
            if masked:
                key = lax.broadcasted_iota(jnp.int32, (tq, tq), 0)
                qry = lax.broadcasted_iota(jnp.int32, (tq, tq), 1)
                pt = jnp.where(key <= qry, pt, 0.0)
            dv_ref[...] += lax.dot_general(pt.astype(BF16), dob, NN, preferred_element_type=F32)
            dpt = lax.dot_general(vb, dob, NT, preferred_element_type=F32)
            dst = (pt * (dpt - delta) * ATTN_SCALE).astype(BF16)
            dk_ref[...] += lax.dot_general(dst, qb, NN, preferred_element_type=F32)
            dqt = lax.dot_general(kbt, dst, NN, preferred_element_type=F32)
            for j in range(nblk):
                dqt_ref[q0 + j] += dqt[:, j * tq:(j + 1) * tq]

        later = nq - 1 - kj
        step(kj, 1, True)
        start = kj + 1
        for c in [c for c in (1, 2, 4) if c < nq]:
            take = (later & c) != 0

            @pl.when(take)
            def _(start=start, c=c):
                step(start, c, False)

            start = start + jnp.where(take, c, 0)
        dk_out[...] = dk_ref[...].astype(BF16)
        dv_out[...] = dv_ref[...].astype(BF16)

        @pl.when(kj == nq - 1)
        def _():
            for qi in range(nq):
                rows = pl.ds(qi * tq, tq)
                d = dqt_ref[qi].T
                roped = _rope_bwd(d[:, NOPE:], cc_ref[rows, :], sa_ref[rows, :], sb_ref[rows, :])
                dq_ref[rows, :] = jnp.concatenate([d[:, :NOPE], roped], axis=1).astype(BF16)

    head_q = pl.BlockSpec((t, QPAD), lambda hh, j: (0, hh))
    head_v = pl.BlockSpec((t, VDIM), lambda hh, j: (0, hh))
    table = pl.BlockSpec((t, 128), lambda hh, j: (0, 0))
    return _call(
        "flash_bwd", body, [q, k, v, o, do, lse, *tabs],
        [head_q, pl.BlockSpec((tq, QPAD), lambda hh, j: (j, hh)), pl.BlockSpec((tq, VDIM), lambda hh, j: (j, hh)), head_v, head_v,
         pl.BlockSpec((None, nq, 1, tq), lambda hh, j: (hh, 0, 0, 0)), table, table, table],
        [_sds((t, h * QPAD), BF16), _sds((t, h * QPAD), BF16), _sds((t, h * VDIM), BF16)],
        [head_q, pl.BlockSpec((tq, QPAD), lambda hh, j: (j, hh)), pl.BlockSpec((tq, VDIM), lambda hh, j: (j, hh))],
        [pltpu.VMEM((nq, 8, tq), F32), pltpu.VMEM((nq, QPAD, tq), F32), pltpu.VMEM((tq, QPAD), F32), pltpu.VMEM((tq, VDIM), F32)], (h, nq), comm)


def _tril_bf16(w):
    row = lax.broadcasted_iota(jnp.int32, w.shape, 0)
    col = lax.broadcasted_iota(jnp.int32, w.shape, 1)
    return jnp.where(col <= row, w, 0.0).astype(BF16)


def _layer_norm_parts(v0):
    mu = jnp.mean(v0, axis=-1, keepdims=True)
    vc = v0 - mu
    rstd = lax.rsqrt(jnp.mean(vc * vc, axis=-1, keepdims=True) + LN_EPS)
    return vc * rstd, rstd


def _sgu_mid_fwd(ge, ln_g, ln_b, w_sp, b_sp, chunks_per_step):
    t, e2 = ge.shape
    e = e2 // 2
    gd = e // SGU_GROUPS
    rows = SGU_CHUNK * chunks_per_step

    def body(u_ref, v_ref, g_ref, b_ref, w_ref, bs_ref, gate_ref):
        for ck in range(chunks_per_step):
            r = pl.ds(ck * SGU_CHUNK, SGU_CHUNK)
            xh, _ = _layer_norm_parts(v_ref[r, :].astype(F32))
            v1 = (xh * g_ref[...] + b_ref[...]).astype(BF16)
            for g in range(SGU_GROUPS):
                cols = pl.ds(g * gd, gd)
                mixed = lax.dot_general(_tril_bf16(w_ref[g]), v1[:, g * gd:(g + 1) * gd], NN, preferred_element_type=F32) + bs_ref[g]
                gate_ref[r, cols] = (u_ref[r, cols].astype(F32) * mixed).astype(BF16)

    return _pcall(
        body, name="sgu_mid_fwd", grid=(t // rows,),
        in_specs=[pl.BlockSpec((rows, e), lambda i: (i, 0)), pl.BlockSpec((rows, e), lambda i: (i, 1)),
                  _const_spec((1, e)), _const_spec((1, e)), _const_spec(w_sp.shape), _const_spec(b_sp.shape)],
        out_specs=pl.BlockSpec((rows, e), lambda i: (i, 0)),
        out_shape=_sds((t, e), BF16), compiler_params=_params(1),
    )(ge, ge, ln_g, ln_b, w_sp, b_sp)


def _sgu_mid_bwd(ge, gp, dgate, ln_g, ln_b, w_sp, b_sp, chunks_per_step):
    t, e2 = ge.shape
    e = e2 // 2
    gd = e // SGU_GROUPS
    rows = SGU_CHUNK * chunks_per_step

    def body(u_ref, v_ref, zu_ref, zv_ref, dg_ref, g_ref, b_ref, w_ref, bs_ref, dz_ref, dw_ref, dbs_ref, dlg_ref, dlb_ref):
        @pl.when(pl.program_id(0) == 0)
        def _():
            dw_ref[...] = jnp.zeros_like(dw_ref)
            dbs_ref[...] = jnp.zeros_like(dbs_ref)
            dlg_ref[...] = jnp.zeros_like(dlg_ref)
            dlb_ref[...] = jnp.zeros_like(dlb_ref)

        for ck in range(chunks_per_step):
            r = pl.ds(ck * SGU_CHUNK, SGU_CHUNK)
            xh, rstd = _layer_norm_parts(v_ref[r, :].astype(F32))
            v1 = (xh * g_ref[...] + b_ref[...]).astype(BF16)
            dv1_parts = []
            for g in range(SGU_GROUPS):
                cols = pl.ds(g * gd, gd)
                wc = _tril_bf16(w_ref[g])
                v1g = v1[:, g * gd:(g + 1) * gd]
                mixed = lax.dot_general(wc, v1g, NN, preferred_element_type=F32) + bs_ref[g]
                dgate = dg_ref[r, cols].astype(F32)
                dmixed = dgate * u_ref[r, cols].astype(F32)
                du = dgate * mixed
                dz_ref[r, cols] = (du * zu_ref[r, cols].astype(F32)).astype(BF16)
                dbs_ref[g] += jnp.sum(dmixed, axis=1, keepdims=True)
                dmb = dmixed.astype(BF16)
                dwg = lax.dot_general(dmb, v1g, NT, preferred_element_type=F32)
                row = lax.broadcasted_iota(jnp.int32, dwg.shape, 0)
                col = lax.broadcasted_iota(jnp.int32, dwg.shape, 1)
                dw_ref[g] += jnp.where(col <= row, dwg, 0.0)
                dv1_parts.append(lax.dot_general(wc, dmb, TN, preferred_element_type=F32))
            dv1 = jnp.concatenate(dv1_parts, axis=1)
            dlg_ref[...] += jnp.sum(dv1 * xh, axis=0, keepdims=True)
            dlb_ref[...] += jnp.sum(dv1, axis=0, keepdims=True)
            dxh = dv1 * g_ref[...]
            dv0 = rstd * (dxh - jnp.mean(dxh, axis=-1, keepdims=True) - xh * jnp.mean(dxh * xh, axis=-1, keepdims=True))
            dz_ref[r, pl.ds(e, e)] = (dv0 * zv_ref[r, :].astype(F32)).astype(BF16)

    half0 = pl.BlockSpec((rows, e), lambda i: (i, 0))
    half1 = pl.BlockSpec((rows, e), lambda i: (i, 1))
    return _pcall(
        body, name="sgu_mid_bwd", grid=(t // rows,),
        in_specs=[half0, half1, half0, half1, half0, _const_spec((1, e)), _const_spec((1, e)), _const_spec(w_sp.shape), _const_spec(b_sp.shape)],
        out_specs=[pl.BlockSpec((rows, e2), lambda i: (i, 0)), _const_spec(w_sp.shape), _const_spec(b_sp.shape), _const_spec((1, e)), _const_spec((1, e))],
        out_shape=[_sds((t, e2), BF16), _sds(w_sp.shape, F32), _sds(b_sp.shape, F32), _sds((1, e), F32), _sds((1, e), F32)],
        compiler_params=_params(1),
    )(ge, ge, gp, gp, dgate, ln_g, ln_b, w_sp, b_sp)


def kernel(x, positions, norm_mix, norm_ffn, final_norm, mla_w_dkv, mla_q_norm, mla_kv_norm, mla_w_uq, mla_w_ukv, mla_w_o, sgu_w_in, sgu_ln_g, sgu_ln_b, sgu_w_spatial, sgu_b_spatial, sgu_w_out, ffn_w_up, ffn_w_down, loss_target, m_norm_mix, m_norm_ffn, m_final_norm, m_mla_w_dkv, m_mla_q_norm, m_mla_kv_norm, m_mla_w_uq, m_mla_w_ukv, m_mla_w_o, m_sgu_w_in, m_sgu_ln_g, m_sgu_ln_b, m_sgu_w_spatial, m_sgu_b_spatial, m_sgu_w_out, m_ffn_w_up, m_ffn_w_down, v_norm_mix, v_norm_ffn, v_final_norm, v_mla_w_dkv, v_mla_q_norm, v_mla_kv_norm, v_mla_w_uq, v_mla_w_ukv, v_mla_w_o, v_sgu_w_in, v_sgu_ln_g, v_sgu_ln_b, v_sgu_w_spatial, v_sgu_b_spatial, v_sgu_w_out, v_ffn_w_up, v_ffn_w_down):
    _, T, D = x.shape
    depth = norm_mix.shape[0]
    n_mla, n_sgu = mla_w_dkv.shape[0], sgu_w_in.shape[0]
    assert depth % 2 == 0
    FF = ffn_w_up.shape[2] * N_DEV
    E = sgu_w_out.shape[1] * N_DEV
    ffc, ec, e2c = FF // N_DEV, E // N_DEV, 2 * E // N_DEV
    dc = D // N_DEV
    OW = HEADS * VDIM
    HW = HEADS * QPAD
    owc = OW // N_DEV
    tm = _tile(T, 1024)
    tb = _tile(T, 4096)
    tk = _tile(T, 512)
    tq = _tile(T, 512)
    ts = _tile(T, 256)
    nt = T // tm
    x2 = x.reshape(T, D)
    tgt = loss_target.reshape(T, D)
    cidx = lax.axis_index("c").astype(jnp.int32).reshape(1)

    ln_local = jnp.concatenate([sgu_ln_g, sgu_ln_b, jnp.zeros((8 - 2 * n_sgu, ec), F32)], axis=0)
    mla_sh = [[w[l].astype(BF16) for w in (mla_w_dkv, mla_w_uq, mla_w_ukv, mla_w_o)] for l in range(n_mla)]

    def mla_layouts(g_dkv, g_uq, g_ukv, g_o):
        w_dkv = jnp.pad(g_dkv.reshape(1, D, LAT), ((0, 0), (0, 0), (0, LAT_PAD - LAT)))
        w_uq = jnp.pad(g_uq, ((0, 0), (0, 0), (0, QPAD - NOPE - ROPE))).transpose(1, 0, 2).reshape(1, Q_RANK, HEADS * QPAD)
        w_ukv = g_ukv.transpose(1, 0, 2).reshape(1, KV_RANK, HEADS * (NOPE + VDIM))
        return w_dkv, w_uq, w_ukv, g_o.reshape(1, HEADS * VDIM, D)

    mla_w = [None] * n_mla
    small_later = [a for l in range(1, n_mla) for a in mla_sh[l]] + [ln_local]
    ln_g_full, ln_b_full = [None] * n_sgu, [None] * n_sgu
    b_sp = sgu_b_spatial.reshape(n_sgu, SGU_GROUPS, SGU_CHUNK, 1)
    up_sh = [ffn_w_up[i].astype(BF16) for i in range(depth)]
    down_sh = [ffn_w_down[i].astype(BF16) for i in range(depth)]
    in_sh = [sgu_w_in[l].astype(BF16) for l in range(n_sgu)]
    out_sh = [sgu_w_out[l].astype(BF16) for l in range(n_sgu)]
    g_up, g_down, g_in, g_out = [None] * depth, [None] * depth, [None] * n_sgu, [None] * n_sgu

    inv_freq = ROPE_THETA ** (-jnp.arange(0, ROPE, 2, dtype=F32) / ROPE)
    zeros32 = jnp.zeros((ROPE // 2,), F32)
    inv128 = jnp.concatenate([inv_freq, inv_freq, zeros32, zeros32]).reshape(1, 128)
    sel_a = jnp.concatenate([-jnp.ones((32,), F32), zeros32, zeros32, zeros32]).reshape(1, 128)
    sel_b = jnp.concatenate([zeros32, jnp.ones((32,), F32), zeros32, zeros32]).reshape(1, 128)
    sel_c = jnp.concatenate([jnp.ones((64,), F32), zeros32, zeros32]).reshape(1, 128)

    def rope_tables(pos, inv, sa, sb, sc):
        ang = pos.astype(F32) * inv
        cs, sn = jnp.cos(ang), jnp.sin(ang)
        return cs * sc, sn * sa, sn * sb

    t_cc, t_sa, t_sb, *first_half = _rowwise(
        "rope_tables", rope_tables, [positions.reshape(T, 1), inv128, sel_a, sel_b, sel_c], grid=(nt,),
        in_specs=[_row_spec(tm, 1)] + [_const_spec((1, 128))] * 4,
        out_shapes=[_sds((T, 128), F32)] * 3, out_specs=[_row_spec(tm, 128)] * 3, comm=_gather_level1(mla_sh[0]))
    tab_specs = [_row_spec(tm, 128)] * 3

    def rmsnorm(xv, g, comm):
        return _rowwise("rmsnorm", lambda a, gg: _rms_fwd(a, gg), [xv, g.reshape(1, D)], grid=(nt,),
                        in_specs=[_row_spec(tm, D), _const_spec((1, D))], out_shapes=[_sds((T, D), BF16)], out_specs=[_row_spec(tm, D)], comm=comm)

    def proj_cols(name, h, gw, nc, epilogue, n_out, comm=None):
        return _matmul(name, h, gw, [], grid=(N_DEV, T // tb),
                       a_spec=pl.BlockSpec((tb, D), lambda j, i: (i, 0)),
                       b_spec=pl.BlockSpec((None, D, nc), lambda j, i: (j, 0, 0)), extra_specs=[],
                       out_shapes=[_sds((T, nc * N_DEV), BF16)] * n_out, out_specs=[pl.BlockSpec((tb, nc), lambda j, i: (i, j))] * n_out,
                       dims=NN, epilogue=epilogue, comm=comm)

    def residual_norm(acc, xr, g):
        xn = acc + xr
        return xn, _rms_fwd(xn, g)

    def proj_rows_residual(name, a, gw, xres, g_next, comm=None):
        kk_ = a.shape[1]
        return _matmul(name, a, gw.reshape(kk_, D), [xres, g_next.reshape(1, D)], grid=(T // tk,),
                       a_spec=_row_spec(tk, kk_), b_spec=_const_spec((kk_, D)), extra_specs=[_row_spec(tk, D), _const_spec((1, D))],
                       out_shapes=[_sds((T, D), F32), _sds((T, D), BF16)], out_specs=[_row_spec(tk, D)] * 2,
                       dims=NN, epilogue=residual_norm, comm=comm)

    def back_rows(name, dy, gw, kc, extras, epilogue, comm=None):
        return _matmul(name, dy, gw, extras, grid=(N_DEV, T // tb),
                       a_spec=pl.BlockSpec((tb, D), lambda j, i: (i, 0)),
                       b_spec=pl.BlockSpec((None, kc, D), lambda j, i: (j, 0, 0)),
                       extra_specs=[pl.BlockSpec((tb, kc), lambda j, i: (i, j))] * len(extras),
                       out_shapes=[_sds((T, kc * N_DEV), BF16)], out_specs=[pl.BlockSpec((tb, kc), lambda j, i: (i, j))],
                       dims=NT, epilogue=epilogue, comm=comm)

    def norm_bwd_epilogue(dh, xv, g, dxi):
        dxn, dg = _rms_bwd(xv, g, dh)
        return dxi + dxn, dxi + dxn, dg

    def transposed(gw):
        return gw.transpose(0, 2, 1).reshape(gw.shape[0] * gw.shape[2], D)

    def back_cols(name, da, gwt, xv, g, dx_in, comm=None):
        n = da.shape[1]
        row = _row_spec(tk, D)
        return _matmul(name, da, gwt, [xv, g.reshape(1, D), dx_in], grid=(T // tk,),
                       a_spec=_row_spec(tk, n), b_spec=_const_spec((n, D)), extra_specs=[row, _const_spec((1, D)), row],
                       out_shapes=[_sds((T, D), F32), _sds((T, D), BF16), _sds((1, D), F32)], out_specs=[row, row, _const_spec((1, D))],
                       dims=NN, epilogue=norm_bwd_epilogue, n_sum=1, comm=comm)

    def token_sum(tt):
        return dict(k_axis=1, nk=T // tt) if T // tt > 1 else dict(k_axis=None)

    def wgrad_cols(name, h, da, nc):
        return _matmul(name, h, da, [], grid=(N_DEV, T // tb),
                       a_spec=pl.BlockSpec((tb, D), lambda j, t: (t, 0)), b_spec=pl.BlockSpec((tb, nc), lambda j, t: (t, j)),
                       extra_specs=[], out_shapes=[_sds((N_DEV, D, nc), BF16)],
                       out_specs=[pl.BlockSpec((None, D, nc), lambda j, t: (j, 0, 0))],
                       dims=TN, acc_shape=(D, nc), **token_sum(tb))[0]

    def wgrad_rows(name, a, dy, kc, ncols, tt):
        return _matmul(name, a, dy, [], grid=(a.shape[1] // kc, T // tt),
                       a_spec=pl.BlockSpec((tt, kc), lambda j, t: (t, j)), b_spec=pl.BlockSpec((tt, ncols), lambda j, t: (t, 0)),
                       extra_specs=[], out_shapes=[_sds((a.shape[1], ncols), BF16)],
                       out_specs=[pl.BlockSpec((kc, ncols), lambda j, t: (j, 0))],
                       dims=TN, acc_shape=(kc, ncols), **token_sum(tt))[0]

    saved = []
    xs = x2
    for i in range(depth):
        l = i // 2
        if i == 0:
            h, *first_w = rmsnorm(xs, norm_mix[0], _gather_level2(first_half))
            mla_w[0] = mla_layouts(*first_w)
        if i % 2 == 0:
            w_dkv, w_uq, w_ukv, w_o = mla_w[l]
            lat = _matmul("mla_down", h, w_dkv, [], grid=(nt,), a_spec=_row_spec(tm, D),
                          b_spec=pl.BlockSpec((None, D, LAT_PAD), lambda i_: (0, 0, 0)), extra_specs=[],
                          out_shapes=[_sds((T, LAT_PAD), F32)], out_specs=[_row_spec(tm, LAT_PAD)], dims=NN)[0]

            def latent_post(la, qn, kvn, cc, sa, sb):
                cq = _rms_fwd(la[:, :Q_RANK], qn)
                ckv = _rms_fwd(la[:, Q_RANK:Q_RANK + KV_RANK], kvn)
                kr = _rope_fwd(la[:, Q_RANK + KV_RANK:], cc, sa, sb)
                return cq, ckv, kr

            cq, ckv, kr = _rowwise(
                "mla_latent", latent_post, [lat, mla_q_norm[l].reshape(1, Q_RANK), mla_kv_norm[l].reshape(1, KV_RANK), t_cc, t_sa, t_sb],
                grid=(nt,), in_specs=[_row_spec(tm, LAT_PAD), _const_spec((1, Q_RANK)), _const_spec((1, KV_RANK))] + tab_specs,
                out_shapes=[_sds((T, Q_RANK), BF16), _sds((T, KV_RANK), BF16), _sds((T, 128), BF16)],
                out_specs=[_row_spec(tm, Q_RANK), _row_spec(tm, KV_RANK), _row_spec(tm, 128)])

            def q_epilogue(acc, cc, sa, sb):
                parts = []
                for b in range(HEADS):
                    parts += [acc[:, b * QPAD:b * QPAD + NOPE], _rope_fwd(acc[:, b * QPAD + NOPE:(b + 1) * QPAD], cc, sa, sb)]
                return (jnp.concatenate(parts, axis=1),)

            q = _matmul("mla_q", cq, w_uq, [t_cc, t_sa, t_sb], grid=(nt,), a_spec=_row_spec(tm, Q_RANK),
                        b_spec=pl.BlockSpec((None, Q_RANK, HW), lambda i_: (0, 0, 0)), extra_specs=tab_specs,
                        out_shapes=[_sds((T, HW), BF16)], out_specs=[_row_spec(tm, HW)], dims=NN, epilogue=q_epilogue)[0]

            def kv_write(outs, acc, krb):
                k_ref, v_ref, vt_ref = outs
                for b in range(HEADS):
                    vb = acc[:, b * QPAD + NOPE:(b + 1) * QPAD]
                    k_ref[:, b * QPAD:b * QPAD + NOPE] = acc[:, b * QPAD:b * QPAD + NOPE].astype(BF16)
                    k_ref[:, b * QPAD + NOPE:(b + 1) * QPAD] = krb
                    v_ref[:, b * VDIM:(b + 1) * VDIM] = vb.astype(BF16)
                    vbt = vb.T.astype(BF16)
                    for u in range(tm // tq):
                        vt_ref[b, u] = vbt[:, u * tq:(u + 1) * tq]

            kk, vv, vt = _matmul("mla_kv", ckv, w_ukv, [kr], grid=(nt,), a_spec=_row_spec(tm, KV_RANK),
                                 b_spec=pl.BlockSpec((None, KV_RANK, HW), lambda i_: (0, 0, 0)), extra_specs=[_row_spec(tm, 128)],
                                 out_shapes=[_sds((T, HW), BF16), _sds((T, OW), BF16), _sds((HEADS, T // tq, VDIM, tq), BF16)],
                                 out_specs=[_row_spec(tm, HW), _row_spec(tm, OW), pl.BlockSpec((HEADS, tm // tq, VDIM, tq), lambda i_: (0, i_, 0, 0))],
                                 dims=NN, write=kv_write)
            group = [up_sh[i], down_sh[i], in_sh[l], out_sh[l]] + (small_later if i == 0 else [])
            o, lse, *bufs = _flash_fwd(q, kk, vt, tq, comm=_gather_level1(group))
            xm, h2, g_up[i], g_down[i] = _matmul(
                "mla_out", o, w_o, [xs, norm_ffn[i].reshape(1, D)], grid=(nt,), a_spec=_row_spec(tm, OW),
                b_spec=pl.BlockSpec((None, OW, D), lambda i_: (0, 0, 0)), extra_specs=[_row_spec(tm, D), _const_spec((1, D))],
                out_shapes=[_sds((T, D), F32), _sds((T, D), BF16)], out_specs=[_row_spec(tm, D)] * 2, dims=NN,
                epilogue=residual_norm, comm=_gather_level2(bufs[:2]))
            half_gathered = bufs[2:]
            mix_saved = (h, lat, cq, ckv, q, kk, vv, o, lse)
        else:
            gp, ge, g_down[i], up_half = proj_cols("sgu_in", h, g_in[l], e2c, _gelu_and_grad, 2,
                                                   comm=_merge_comm(_gather_level2([down_half]), _gather_level1([up_sh[i]])))
            gate = _sgu_mid_fwd(ge, ln_g_full[l], ln_b_full[l], sgu_w_spatial[l], b_sp[l], 4)
            xm, h2, g_up[i] = proj_rows_residual("sgu_out", gate, g_out[l], xs, norm_ffn[i], comm=_gather_level2([up_half]))
            mix_saved = (h, gp, ge, gate)
        r, s, *rest = proj_cols("ffn_up", h2, g_up[i], ffc, lambda acc: (jnp.maximum(acc, 0.0), jnp.square(jnp.maximum(acc, 0.0))), 2,
                                comm=_gather_level2(half_gathered) if i % 2 == 0 else None)
        if i % 2 == 0:
            g_in[l], g_out[l], *small_gathered = rest
        if i == 0:
            for l_ in range(1, n_mla):
                mla_w[l_] = mla_layouts(*small_gathered[4 * (l_ - 1):4 * l_])
            g_ln = small_gathered[-1]
            ln_g_full = [g_ln[:, l_, :].reshape(1, E) for l_ in range(n_sgu)]
            ln_b_full = [g_ln[:, n_sgu + l_, :].reshape(1, E) for l_ in range(n_sgu)]
        xo, h_next, *rest = proj_rows_residual("ffn_down", s, g_down[i], xm, norm_mix[i + 1] if i + 1 < depth else final_norm,
                                               comm=_gather_level1([down_sh[i + 1]]) if i % 2 == 0 else None)
        if i % 2 == 0:
            (down_half,) = rest
        saved.append((xs, xm, mix_saved, h2, r, s))
        xs, h = xo, h_next

    def loss_head(xv, tg, g):
        y = _rms_fwd(xv, g)
        err = y - tg
        part = 0.5 * jnp.sum(jnp.sum(err * err, axis=-1, keepdims=True), axis=0, keepdims=True) / D
        dx, dg = _rms_bwd(xv, g, err / D)
        return dx, dx, jnp.broadcast_to(part, (1, 128)), dg

    dx, dyb, loss_part, d_final = _rowwise(
        "loss_head", loss_head, [xs, tgt, final_norm.reshape(1, D)], grid=(nt,),
        in_specs=[_row_spec(tm, D), _row_spec(tm, D), _const_spec((1, D))],
        out_shapes=[_sds((T, D), F32), _sds((T, D), BF16), _sds((1, 128), F32), _sds((1, D), F32)],
        out_specs=[_row_spec(tm, D), _row_spec(tm, D), _const_spec((1, 128)), _const_spec((1, D))], n_acc=2)
    loss = lax.psum(loss_part[0, 0], ("x", "y", "c"))

    d_norm_mix, d_norm_ffn = [None] * depth, [None] * depth
    d_qn, d_kvn = [None] * n_mla, [None] * n_mla
    d_wsp, d_bsp, d_lng, d_lnb = [None] * n_sgu, [None] * n_sgu, [None] * n_sgu, [None] * n_sgu
    layers = {"dkv": n_mla, "uq": n_mla, "ukv": n_mla, "o": n_mla, "in": n_sgu, "out": n_sgu, "up": depth, "down": depth}
    stacked = {nm: None for nm in layers}
    pending = []
    summed = []

    def add_pairs(gs, rcvs):
        operands, in_specs, out_shapes, out_specs = [], [], [], []
        for g, rcv in zip(gs, rcvs):
            _, rws, cls = g.shape
            slab = pl.BlockSpec((None, rws, cls), lambda ch, cr: (ch, 0, 0))
            operands += [g.reshape(N_CHIP, 2, rws, cls), rcv]
            in_specs += [pl.BlockSpec((None, None, rws, cls), lambda ch, cr: (ch, cr[0], 0, 0)), slab]
            out_shapes.append(_sds(rcv.shape, BF16))
            out_specs.append(slab)

        def fn(*blocks):
            return tuple(blocks[2 * k].astype(F32) + blocks[2 * k + 1].astype(F32) for k in range(len(gs)))

        return _rowwise("grad_pair_sum", fn, operands, grid=(N_CHIP,), in_specs=in_specs, out_shapes=out_shapes, out_specs=out_specs,
                        grid_spec_prefetch=cidx)

    def sibling_comm():
        return _sibling_exchange([g for _, _, g in pending]) if pending else None

    def absorb(from_sibling):
        if pending:
            parts = add_pairs([g for _, _, g in pending], list(from_sibling))
            summed.extend((nm, l_, p) for (nm, l_, _), p in zip(pending, parts))
            pending.clear()

    def chip_comm():
        if pending:
            absorb(_comm_call("grad_sibling_exchange", sibling_comm()))
        comm, names = _chip_exchange([p for _, _, p in summed], [(nm, l_) for nm, l_, _ in summed], layers, stacked)
        summed.clear()
        return comm, names

    def rows128(a, rows):
        flat = a.reshape(-1, 128)
        return jnp.pad(flat, ((0, rows - flat.shape[0]), (0, 0)))

    def pad_to(n, mult):
        return -(-n // mult) * mult

    def packed(arrs, sizes):
        return jnp.concatenate([rows128(a, sz) for a, sz in zip(arrs, sizes)], axis=0)

    n_wsp, n_bsp, n_ln = sgu_w_spatial.size // 128, pad_to(sgu_b_spatial.size // 128, 8), pad_to(n_sgu * E // 128, 8)
    early_sizes = [n_wsp, pad_to(n_wsp + n_bsp, SMALL_ROWS) - n_wsp, n_ln, n_ln]
    early_rep = early_sizes[0] + early_sizes[1]
    gathered_early = None

    for i in reversed(range(depth)):
        l = i // 2
        xs_i, xm, mix_saved, h2, r, s = saved[i]
        da, *rcv = back_rows("ffn_down_bwd", dyb, g_down[i], ffc, [r], lambda acc, rr: (acc * (2.0 * rr.astype(F32)),), comm=sibling_comm())
        absorb(rcv)
        pending.append(("down", i, wgrad_rows("ffn_down_wgrad", s, dyb, ffc, D, tb).reshape(N_DEV, ffc, D)))
        pending.append(("up", i, wgrad_cols("ffn_up_wgrad", h2, da, ffc)))
        dx, dyb, d_norm_ffn[i], *rcv = back_cols("ffn_up_bwd", da, transposed(g_up[i]), xm, norm_ffn[i], dx, comm=sibling_comm())
        absorb(rcv)
        if i % 2 == 0:
            h, lat, cq, ckv, q, kk, vv, o, lse = mix_saved
            w_dkv, w_uq, w_ukv, w_o = mla_w[l]
            do = _matmul("mla_out_bwd", dyb, w_o, [], grid=(nt,), a_spec=_row_spec(tm, D),
                         b_spec=pl.BlockSpec((None, OW, D), lambda i_: (0, 0, 0)), extra_specs=[],
                         out_shapes=[_sds((T, OW), BF16)], out_specs=[_row_spec(tm, OW)], dims=NT)[0]
            g_o_l = wgrad_rows("mla_out_wgrad", o, dyb, OW, D, tm).reshape(N_DEV, owc, D)
            comm, names = chip_comm()
            if i == 0:
                early = packed([jnp.stack(d_wsp, 0), jnp.stack(d_bsp, 0), jnp.concatenate(d_lng, 0), jnp.concatenate(d_lnb, 0)], early_sizes)
                comm = _merge_comm(comm, _gather_level1([early]))
            dq_pre, dk, dv, *bufs = _flash_bwd(q, kk, vv, o, do, lse, (t_cc, t_sa, t_sb), tq, comm=comm)
            stacked.update(dict(zip(names, bufs)))
            pending.append(("o", l, g_o_l))

            def kv_pre(dkb, dvb, cc, sa, sb):
                parts, dkr = [], None
                for b in range(HEADS):
                    parts += [dkb[:, b * QPAD:b * QPAD + NOPE], dvb[:, b * VDIM:(b + 1) * VDIM]]
                    piece = dkb[:, b * QPAD + NOPE:(b + 1) * QPAD].astype(F32)
                    dkr = piece if dkr is None else dkr + piece
                return jnp.concatenate(parts, axis=1), _rope_bwd(dkr, cc, sa, sb)

            dkv, dkr, *rest = _rowwise("mla_dkv_rope", kv_pre, [dk, dv, t_cc, t_sa, t_sb], grid=(T // ts,),
                                       in_specs=[_row_spec(ts, HW), _row_spec(ts, OW)] + [_row_spec(ts, 128)] * 3,
                                       out_shapes=[_sds((T, HW), BF16), _sds((T, 128), F32)], out_specs=[_row_spec(ts, HW), _row_spec(ts, 128)],
                                       comm=_gather_level2(bufs[len(names):]) if i == 0 else None)
            if i == 0:
                (gathered_early,) = rest
            g_uq_l = wgrad_rows("mla_q_wgrad", cq, dq_pre, Q_RANK, HW, tm)
            g_ukv_l = wgrad_rows("mla_kv_wgrad", ckv, dkv, KV_RANK, HW, tm)
            pending.append(("uq", l, g_uq_l.reshape(Q_RANK, HEADS, QPAD)[:, :, :NOPE + ROPE].transpose(1, 0, 2)))
            pending.append(("ukv", l, g_ukv_l.reshape(KV_RANK, HEADS, NOPE + VDIM).transpose(1, 0, 2)))
            dcq = _matmul("mla_q_bwd", dq_pre, w_uq, [], grid=(nt,), a_spec=_row_spec(tm, HW),
                          b_spec=pl.BlockSpec((None, Q_RANK, HW), lambda i_: (0, 0, 0)), extra_specs=[],
                          out_shapes=[_sds((T, Q_RANK), F32)], out_specs=[_row_spec(tm, Q_RANK)], dims=NT)[0]
            dckv = _matmul("mla_kv_bwd", dkv, w_ukv, [], grid=(nt,), a_spec=_row_spec(tm, HW),
                           b_spec=pl.BlockSpec((None, KV_RANK, HW), lambda i_: (0, 0, 0)), extra_specs=[],
                           out_shapes=[_sds((T, KV_RANK), F32)], out_specs=[_row_spec(tm, KV_RANK)], dims=NT)[0]

            def latent_bwd(la, qn, kvn, dq_, dkv_, dkr_):
                dcq_raw, dqn = _rms_bwd(la[:, :Q_RANK], qn, dq_)
                dckv_raw, dkvn = _rms_bwd(la[:, Q_RANK:Q_RANK + KV_RANK], kvn, dkv_)
                return jnp.concatenate([dcq_raw, dckv_raw, dkr_], axis=1), dqn, dkvn

            dlat, d_qn[l], d_kvn[l] = _rowwise(
                "mla_latent_bwd", latent_bwd, [lat, mla_q_norm[l].reshape(1, Q_RANK), mla_kv_norm[l].reshape(1, KV_RANK), dcq, dckv, dkr],
                grid=(nt,), in_specs=[_row_spec(tm, LAT_PAD), _const_spec((1, Q_RANK)), _const_spec((1, KV_RANK)),
                                      _row_spec(tm, Q_RANK), _row_spec(tm, KV_RANK), _row_spec(tm, 128)],
                out_shapes=[_sds((T, LAT_PAD), BF16), _sds((1, Q_RANK), F32), _sds((1, KV_RANK), F32)],
                out_specs=[_row_spec(tm, LAT_PAD), _const_spec((1, Q_RANK)), _const_spec((1, KV_RANK))], n_acc=2)
            g_dkv_l = wgrad_rows("mla_down_wgrad", h, dlat, D, LAT_PAD, tm)
            pending.append(("dkv", l, g_dkv_l[:, :LAT].reshape(N_DEV, dc, LAT)))
            dx, dyb, d_norm_mix[i] = _matmul(
                "mla_down_bwd", dlat, w_dkv, [xs_i, norm_mix[i].reshape(1, D), dx], grid=(nt,), a_spec=_row_spec(tm, LAT_PAD),
                b_spec=pl.BlockSpec((None, D, LAT_PAD), lambda i_: (0, 0, 0)), extra_specs=[_row_spec(tm, D), _const_spec((1, D)), _row_spec(tm, D)],
                out_shapes=[_sds((T, D), F32), _sds((T, D), BF16), _sds((1, D), F32)],
                out_specs=[_row_spec(tm, D), _row_spec(tm, D), _const_spec((1, D))], dims=NT, epilogue=norm_bwd_epilogue, n_sum=1)
        else:
            h, gp, ge, gate = mix_saved
            (dgate,) = back_rows("sgu_out_bwd", dyb, g_out[l], ec, [], None)
            pending.append(("out", l, wgrad_rows("sgu_out_wgrad", gate, dyb, ec, D, tb).reshape(N_DEV, ec, D)))
            dz, d_wsp[l], d_bsp[l], d_lng[l], d_lnb[l] = _sgu_mid_bwd(ge, gp, dgate, ln_g_full[l], ln_b_full[l], sgu_w_spatial[l], b_sp[l], 2)
            pending.append(("in", l, wgrad_cols("sgu_in_wgrad", h, dz, e2c)))
            dx, dyb, d_norm_mix[i], *rcv = back_cols("sgu_in_bwd", dz, transposed(g_in[l]), xs_i, norm_mix[i], dx, comm=sibling_comm())
            absorb(rcv)
    grad_x = dx.reshape(1, T, D)

    last_comm, last_names = chip_comm()
    late_g = [jnp.concatenate(d_norm_mix, 0), jnp.concatenate(d_norm_ffn, 0), d_final, jnp.concatenate(d_qn, 0), jnp.concatenate(d_kvn, 0)]
    late_w = [norm_mix, norm_ffn, final_norm, mla_q_norm, mla_kv_norm]
    late_m = [m_norm_mix, m_norm_ffn, m_final_norm, m_mla_q_norm, m_mla_kv_norm]
    late_v = [v_norm_mix, v_norm_ffn, v_final_norm, v_mla_q_norm, v_mla_kv_norm]
    late_sizes = [pad_to(g.size // 128, 8) for g in late_g]
    late_rows = sum(late_sizes)

    def adam_big(parts, w, m, v):
        lyr, rws, cls = w.shape
        rt = _tile(rws, 512)

        def fn(p, w_, m_, v_):
            g = (p[0].astype(F32) + p[1].astype(F32)) + (p[2].astype(F32) + p[3].astype(F32))
            return (g, *_adam(w_, g, m_, v_))

        spec = pl.BlockSpec((None, rt, cls), lambda l_, i_: (l_, i_, 0))
        return _rowwise("adam_large", fn, [parts, w, m, v], grid=(lyr, rws // rt),
                        in_specs=[pl.BlockSpec((N_CHIP, None, rt, cls), lambda l_, i_: (0, l_, i_, 0)), spec, spec, spec],
                        out_shapes=[_sds(w.shape, F32)] * 4, out_specs=[spec] * 4)

    stacked.update(dict(zip(last_names, _comm_call("grad_chip_exchange", last_comm))))
    (gathered_late,) = _all_gather("gather_small_grads", [packed(late_g, late_sizes)])
    big = {}
    big["in"] = adam_big(stacked["in"], sgu_w_in, m_sgu_w_in, v_sgu_w_in)
    big["up"] = adam_big(stacked["up"], ffn_w_up, m_ffn_w_up, v_ffn_w_up)
    big["down"] = adam_big(stacked["down"], ffn_w_down, m_ffn_w_down, v_ffn_w_down)
    big["out"] = adam_big(stacked["out"], sgu_w_out, m_sgu_w_out, v_sgu_w_out)
    big["dkv"] = adam_big(stacked["dkv"], mla_w_dkv, m_mla_w_dkv, v_mla_w_dkv)
    big["uq"] = adam_big(stacked["uq"], mla_w_uq, m_mla_w_uq, v_mla_w_uq)
    big["ukv"] = adam_big(stacked["ukv"], mla_w_ukv, m_mla_w_ukv, v_mla_w_ukv)
    big["o"] = adam_big(stacked["o"], mla_w_o, m_mla_w_o, v_mla_w_o)
    big_res = [big[nm][:4] for nm in ("dkv", "uq", "ukv", "o", "in", "out", "up", "down")]

    def sum8(p):
        return ((p[0] + p[1]) + (p[2] + p[3])) + ((p[4] + p[5]) + (p[6] + p[7]))

    def adam_packed(name, gathered, ws, ms, vs, sizes, rows, tile):
        spec = _row_spec(tile, 128)
        return _rowwise(name, lambda p, w_, m_, v_: (sum8(p), *_adam(w_, sum8(p), m_, v_)),
                        [gathered, packed(ws, sizes), packed(ms, sizes), packed(vs, sizes)], grid=(rows // tile,),
                        in_specs=[pl.BlockSpec((N_DEV, tile, 128), lambda i_: (0, i_, 0)), spec, spec, spec],
                        out_shapes=[_sds((rows, 128), F32)] * 4, out_specs=[spec] * 4)

    late_res = adam_packed("adam_small", gathered_late, late_w, late_m, late_v, late_sizes, late_rows, late_rows)
    early_res = adam_packed("adam_spatial", gathered_early, [sgu_w_spatial, sgu_b_spatial], [m_sgu_w_spatial, m_sgu_b_spatial],
                            [v_sgu_w_spatial, v_sgu_b_spatial], early_sizes[:2], early_rep, SMALL_ROWS)

    def unpack(res, sizes, k, like):
        off = sum(sizes[:k])
        return res[off:off + like.size // 128].reshape(like.shape)

    my_b = 4 * lax.axis_index("x") + 2 * lax.axis_index("y") + lax.axis_index("c")
    ln_w = jnp.concatenate([sgu_ln_g, sgu_ln_b], 0)
    ln_m = jnp.concatenate([m_sgu_ln_g, m_sgu_ln_b], 0)
    ln_v = jnp.concatenate([v_sgu_ln_g, v_sgu_ln_b], 0)
    ln_all = jnp.concatenate([gathered_early[:, early_rep:early_rep + n_sgu * E // 128], gathered_early[:, early_rep + n_ln:early_rep + n_ln + n_sgu * E // 128]], axis=1)
    ln_mine = lax.dynamic_slice_in_dim(ln_all.reshape(N_DEV, 2 * n_sgu, N_DEV, ec), my_b, 1, axis=2).reshape(N_DEV, 2 * n_sgu, ec)
    ln_g_, ln_d, ln_m2, ln_v2 = _rowwise(
        "adam_ln", lambda p, w_, m_, v_: (sum8(p), *_adam(w_, sum8(p), m_, v_)), [ln_mine, ln_w, ln_m, ln_v], grid=(1,),
        in_specs=[_const_spec(ln_mine.shape), _const_spec(ln_w.shape), _const_spec(ln_w.shape), _const_spec(ln_w.shape)],
        out_shapes=[_sds(ln_w.shape, F32)] * 4, out_specs=[_const_spec(ln_w.shape)] * 4)

    def family(pos):
        ln = [ln_g_, ln_d, ln_m2, ln_v2][pos]
        late = [unpack(late_res[pos], late_sizes, k, w_) for k, w_ in enumerate(late_w)]
        w_sp_, b_sp_ = unpack(early_res[pos], early_sizes, 0, sgu_w_spatial), unpack(early_res[pos], early_sizes, 1, sgu_b_spatial)
        bigs = [res[pos] for res in big_res]
        return [late[0], late[1], late[2], bigs[0], late[3], late[4], bigs[1], bigs[2], bigs[3],
                bigs[4], ln[:n_sgu], ln[n_sgu:], w_sp_, b_sp_, bigs[5], bigs[6], bigs[7]]

    return (loss, grad_x, *family(0), *family(1), *family(2), *family(3))
```

```python
import math

import jax
import jax.numpy as jnp
from jax import lax
from jax.experimental import pallas as pl
from jax.experimental.pallas import tpu as pltpu

F32 = jnp.float32
BF16 = jnp.bfloat16
MESH = pl.DeviceIdType.MESH

N_DEV = 8
N_CHIP = 4
HEADS = 8
NOPE = 128
ROPE = 64
VDIM = 128
QPAD = 256
Q_RANK = 256
KV_RANK = 128
LAT = Q_RANK + KV_RANK + ROPE
LAT_PAD = 512
ROPE_THETA = 10000.0
SGU_CHUNK = 128
SGU_GROUPS = 8
NORM_EPS = 1e-6
LN_EPS = 1e-5
ADAM_LR = 0.001
ADAM_B1 = 0.9
ADAM_B2 = 0.999
ADAM_EPS = 1e-08
ADAM_WD = 0.01
ADAM_STEP = 10
ATTN_SCALE = (NOPE + ROPE) ** -0.5
NEG = -1e30
EXP2_SCALE = ATTN_SCALE * math.log2(math.e)
VMEM_LIMIT = 56 * 1024 * 1024
SMALL_ROWS = 256

NN = (((1,), (0,)), ((), ()))
NT = (((1,), (1,)), ((), ()))
TN = (((0,), (0,)), ((), ()))
ANY = pl.BlockSpec(memory_space=pl.ANY)


def _pcall(body, **kw):
    return pl.pallas_call(body, **kw)


def _params(n_grid, side_effects=False):
    return pltpu.CompilerParams(dimension_semantics=("arbitrary",) * n_grid, vmem_limit_bytes=VMEM_LIMIT, has_side_effects=side_effects)


def _sds(shape, dtype):
    return jax.ShapeDtypeStruct(tuple(shape), dtype)


def _tile(n, want):
    t = min(n, want)
    assert n % t == 0, (n, want)
    return t


class _Comm:
    def __init__(self, operands, out_shapes, aliases, scratch, start, finish):
        self.operands, self.out_shapes, self.aliases, self.scratch = operands, out_shapes, aliases, scratch
        self.start, self.finish = start, finish


def _merge_comm(first, second):
    n_in, n_out, n_sc = len(first.operands), len(first.out_shapes), len(first.scratch)
    aliases = dict(first.aliases)
    aliases.update({n_in + k: n_out + v for k, v in second.aliases.items()})

    def start(ins, outs, sems):
        first.start(ins[:n_in], outs[:n_out], sems[:n_sc])
        second.start(ins[n_in:], outs[n_out:], sems[n_sc:])

    def finish(ins, outs, sems):
        first.finish(ins[:n_in], outs[:n_out], sems[:n_sc])
        second.finish(ins[n_in:], outs[n_out:], sems[n_sc:])

    return _Comm([*first.operands, *second.operands], [*first.out_shapes, *second.out_shapes], aliases,
                 [*first.scratch, *second.scratch], start, finish)


def _place():
    return lax.axis_index("x"), lax.axis_index("y"), lax.axis_index("c")


def _other_chips(x, y):
    return [(1 - x, y), (x, 1 - y), (1 - x, 1 - y)]


def _dev_index(dev):
    return 4 * dev[0] + 2 * dev[1] + dev[2]


def _comm_call(name, comm):
    c_in, c_out = len(comm.operands), len(comm.out_shapes)

    def body(*refs):
        ins, outs, sems = refs[:c_in], refs[c_in:c_in + c_out], refs[c_in + c_out:]
        comm.start(ins, outs, sems)
        comm.finish(ins, outs, sems)

    return _pcall(body, name=name, in_specs=[ANY] * c_in, out_specs=[ANY] * c_out, out_shape=comm.out_shapes,
                  scratch_shapes=comm.scratch, input_output_aliases=dict(comm.aliases),
                  compiler_params=pltpu.CompilerParams(has_side_effects=True))(*comm.operands)


def _call(name, body, operands, in_specs, out_shapes, out_specs, scratch, grid, comm=None):
    if comm is None:
        return _pcall(body, name=name, grid=grid, in_specs=in_specs, out_specs=out_specs, out_shape=out_shapes,
                      scratch_shapes=scratch, compiler_params=_params(len(grid)))(*operands)
    n_in, n_out, n_sc = len(operands), len(out_shapes), len(scratch)
    c_in, c_out = len(comm.operands), len(comm.out_shapes)

    def hosted(*refs):
        ins, cins = refs[:n_in], refs[n_in:n_in + c_in]
        o0 = n_in + c_in
        outs, couts = refs[o0:o0 + n_out], refs[o0 + n_out:o0 + n_out + c_out]
        rest = refs[o0 + n_out + c_out:]
        sc, csems = rest[:n_sc], rest[n_sc:]
        first = pl.program_id(0) == 0
        last = pl.program_id(0) == grid[0] - 1
        for d in range(1, len(grid)):
            first = jnp.logical_and(first, pl.program_id(d) == 0)
            last = jnp.logical_and(last, pl.program_id(d) == grid[d] - 1)

        @pl.when(first)
        def _():
            comm.start(cins, couts, csems)

        body(*ins, *outs, *sc)

        @pl.when(last)
        def _():
            comm.finish(cins, couts, csems)

    return _pcall(hosted, name=name, grid=grid, in_specs=[*in_specs, *[ANY] * c_in], out_specs=[*out_specs, *[ANY] * c_out],
                  out_shape=[*out_shapes, *comm.out_shapes], scratch_shapes=[*scratch, *comm.scratch],
                  input_output_aliases={n_in + k: n_out + v for k, v in comm.aliases.items()},
                  compiler_params=_params(len(grid), side_effects=True))(*operands, *comm.operands)


def _gather_level1(shards):
    n = len(shards)

    def copies(ins, outs, sems):
        send_sems, recv_sems, local_sems = sems
        x, y, c = _place()
        me, sibling = (x, y, c), (x, y, 1 - c)
        chips = _other_chips(x, y)

        def copy(a, k, block, to, src=None):
            slot = outs[a].at[_dev_index(block)]
            return pltpu.make_async_remote_copy(src_ref=slot if src is None else src, dst_ref=slot, send_sem=send_sems.at[a, k],
                                                recv_sem=recv_sems.at[a, k], device_id=to, device_id_type=MESH)

        mine = [pltpu.make_async_copy(ins[a], outs[a].at[_dev_index(me)], local_sems.at[a]) for a in range(n)]
        sends = [copy(a, 1 + j, me, (*chip, c), src=ins[a]) for j, chip in enumerate(chips) for a in range(n)]
        sends += [copy(a, 0, me, sibling, src=ins[a]) for a in range(n)]
        recvs = [copy(a, 1 + j, (*chip, c), me) for j, chip in enumerate(chips) for a in range(n)]
        recvs += [copy(a, 0, sibling, me) for a in range(n)]
        return mine, sends, recvs

    def start(ins, outs, sems):
        mine, sends, _ = copies(ins, outs, sems)
        for cp in mine + sends:
            cp.start()

    def finish(ins, outs, sems):
        mine, sends, recvs = copies(ins, outs, sems)
        for cp in recvs:
            cp.wait_recv()
        for cp in sends:
            cp.wait_send()
        for cp in mine:
            cp.wait()

    return _Comm(shards, [_sds((N_DEV, *a.shape), a.dtype) for a in shards], {},
                 [pltpu.SemaphoreType.DMA((n, 4)), pltpu.SemaphoreType.DMA((n, 4)), pltpu.SemaphoreType.DMA((n,))], start, finish)


def _gather_level2(bufs):
    n = len(bufs)

    def copies(outs, sems):
        send_sems, recv_sems = sems
        x, y, c = _place()
        sibling = (x, y, 1 - c)
        sends, recvs = [], []
        for j, chip in enumerate(_other_chips(x, y)):
            for a in range(n):
                have, want = outs[a].at[_dev_index((*chip, c))], outs[a].at[_dev_index((*chip, 1 - c))]
                sends.append(pltpu.make_async_remote_copy(src_ref=have, dst_ref=have, send_sem=send_sems.at[a, j], recv_sem=recv_sems.at[a, j],
                                                          device_id=sibling, device_id_type=MESH))
                recvs.append(pltpu.make_async_remote_copy(src_ref=want, dst_ref=want, send_sem=send_sems.at[a, j], recv_sem=recv_sems.at[a, j],
                                                          device_id=sibling, device_id_type=MESH))
        return sends, recvs

    def start(ins, outs, sems):
        for cp in copies(outs, sems)[0]:
            cp.start()

    def finish(ins, outs, sems):
        sends, recvs = copies(outs, sems)
        for cp in recvs:
            cp.wait_recv()
        for cp in sends:
            cp.wait_send()

    return _Comm(bufs, [_sds(b.shape, b.dtype) for b in bufs], {a: a for a in range(n)},
                 [pltpu.SemaphoreType.DMA((n, 3)), pltpu.SemaphoreType.DMA((n, 3))], start, finish)


def _all_gather(name, arrays):
    n = len(arrays)

    def body(*refs):
        ins = refs[:n]
        outs = refs[n:2 * n]
        send_sems, recv_sems, local_sems = refs[2 * n:]
        x, y, c = _place()
        me, sibling = (x, y, c), (x, y, 1 - c)
        chips = _other_chips(x, y)

        def copy(a, k, block, to, src=None):
            slot = outs[a].at[_dev_index(block)]
            return pltpu.make_async_remote_copy(src_ref=slot if src is None else src, dst_ref=slot, send_sem=send_sems.at[a, k],
                                                recv_sem=recv_sems.at[a, k], device_id=to, device_id_type=MESH)

        mine = [pltpu.make_async_copy(ins[a], outs[a].at[_dev_index(me)], local_sems.at[a]) for a in range(n)]
        for cp in mine:
            cp.start()
        first = []
        for j, chip in enumerate(chips):
            first += [copy(a, 1 + j, me, (*chip, c), src=ins[a]) for a in range(n)]
        first += [copy(a, 0, me, sibling, src=ins[a]) for a in range(n)]
        for cp in first:
            cp.start()
        passed = []
        for j, chip in enumerate(chips):
            for a in range(n):
                copy(a, 1 + j, (*chip, c), me).wait_recv()
                fwd = copy(a, 4 + j, (*chip, c), sibling)
                fwd.start()
                passed.append(fwd)
        for a in range(n):
            copy(a, 0, sibling, me).wait_recv()
            for j, chip in enumerate(chips):
                copy(a, 4 + j, (*chip, 1 - c), me).wait_recv()
        for cp in first + passed:
            cp.wait_send()
        for cp in mine:
            cp.wait()

    return _pcall(
        body, name=name, in_specs=[ANY] * n, out_specs=[ANY] * n,
        out_shape=[_sds((N_DEV, *a.shape), a.dtype) for a in arrays],
        scratch_shapes=[pltpu.SemaphoreType.DMA((n, 7)), pltpu.SemaphoreType.DMA((n, 7)), pltpu.SemaphoreType.DMA((n,))],
        compiler_params=pltpu.CompilerParams(has_side_effects=True),
    )(*arrays)


def _sibling_exchange(grads):
    n = len(grads)

    def start(ins, outs, sems):
        send_sems, recv_sems = sems
        x, y, c = _place()
        for a in range(n):
            for ch in range(N_CHIP):
                pltpu.make_async_remote_copy(src_ref=ins[a].at[2 * ch + 1 - c], dst_ref=outs[a].at[ch], send_sem=send_sems.at[a],
                                             recv_sem=recv_sems.at[a], device_id=(x, y, 1 - c), device_id_type=MESH).start()

    def finish(ins, outs, sems):
        send_sems, recv_sems = sems
        x, y, c = _place()
        for a in range(n):
            pltpu.make_async_remote_copy(src_ref=outs[a], dst_ref=outs[a], send_sem=send_sems.at[a], recv_sem=recv_sems.at[a],
                                         device_id=(x, y, 1 - c), device_id_type=MESH).wait()

    return _Comm(grads, [_sds((N_CHIP, *g.shape[1:]), g.dtype) for g in grads], {},
                 [pltpu.SemaphoreType.DMA((n,)), pltpu.SemaphoreType.DMA((n,))], start, finish)


def _chip_exchange(parts, slots, layers, stacked):
    n = len(parts)
    names = []
    for nm, _ in slots:
        if nm not in names:
            names.append(nm)
    shapes = {nm: _sds((N_CHIP, layers[nm], *parts[a].shape[1:]), parts[a].dtype) for a, (nm, _) in enumerate(slots)}
    kept = [nm for nm in names if stacked.get(nm) is not None]
    aliases = {n + k: names.index(nm) for k, nm in enumerate(kept)}

    def copies(ins, outs, sems):
        send_sems, recv_sems, local_sems = sems
        x, y, c = _place()
        mine = 2 * x + y
        local, sends, recvs = [], [], []
        for a, (nm, l) in enumerate(slots):
            buf = outs[names.index(nm)]
            local.append(pltpu.make_async_copy(ins[a].at[mine], buf.at[mine, l], local_sems.at[a]))
            for j, chip in enumerate(_other_chips(x, y)):
                theirs = buf.at[2 * chip[0] + chip[1], l]
                sends.append(pltpu.make_async_remote_copy(src_ref=ins[a].at[2 * chip[0] + chip[1]], dst_ref=buf.at[mine, l], send_sem=send_sems.at[a, j],
                                                          recv_sem=recv_sems.at[a, j], device_id=(*chip, c), device_id_type=MESH))
                recvs.append(pltpu.make_async_remote_copy(src_ref=theirs, dst_ref=theirs, send_sem=send_sems.at[a, j],
                                                          recv_sem=recv_sems.at[a, j], device_id=(*chip, c), device_id_type=MESH))
        return local, sends, recvs

    def start(ins, outs, sems):
        local, sends, _ = copies(ins, outs, sems)
        for cp in local + sends:
            cp.start()

    def finish(ins, outs, sems):
        local, sends, recvs = copies(ins, outs, sems)
        for cp in recvs:
            cp.wait_recv()
        for cp in sends:
            cp.wait_send()
        for cp in local:
            cp.wait()

    comm = _Comm([*parts, *[stacked[nm] for nm in kept]], [shapes[nm] for nm in names], aliases,
                 [pltpu.SemaphoreType.DMA((n, 3)), pltpu.SemaphoreType.DMA((n, 3)), pltpu.SemaphoreType.DMA((n,))], start, finish)
    return comm, names


def _matmul(name, a, b, extras, *, grid, a_spec, b_spec, extra_specs, out_shapes, out_specs, dims, k_axis=None, nk=1,
            acc_shape=None, epilogue=None, comm=None, n_sum=0, write=None):
    n_extra = len(extras)
    n_out = len(out_shapes)

    def body(*refs):
        a_ref, b_ref = refs[0], refs[1]
        ex = refs[2:2 + n_extra]
        outs = refs[2 + n_extra:2 + n_extra + n_out]
        prod = lax.dot_general(a_ref[...], b_ref[...], dims, preferred_element_type=F32)

        def finish(acc):
            if write is not None:
                write(outs, acc, *[e[...] for e in ex])
                return
            res = epilogue(acc, *[e[...] for e in ex]) if epilogue is not None else (acc,)
            first = None
            for d in range(len(grid)):
                if d != k_axis:
                    here = pl.program_id(d) == 0
                    first = here if first is None else jnp.logical_and(first, here)
            for idx, (o, r) in enumerate(zip(outs, res)):
                if idx < n_out - n_sum:
                    o[...] = r.astype(o.dtype)
                else:
                    @pl.when(first)
                    def _(o=o, r=r):
                        o[...] = r.astype(o.dtype)

                    @pl.when(jnp.logical_not(first))
                    def _(o=o, r=r):
                        o[...] += r.astype(o.dtype)

        if k_axis is None:
            finish(prod)
        else:
            acc_ref = refs[-1]
            k = pl.program_id(k_axis)

            @pl.when(k == 0)
            def _():
                acc_ref[...] = prod

            @pl.when(k > 0)
            def _():
                acc_ref[...] += prod

            @pl.when(k == nk - 1)
            def _():
                finish(acc_ref[...])

    scratch = [] if k_axis is None else [pltpu.VMEM(acc_shape, F32)]
    return _call(name, body, [a, b, *extras], [a_spec, b_spec, *extra_specs], list(out_shapes), list(out_specs), scratch, grid, comm)


def _rowwise(name, fn, operands, *, grid, in_specs, out_shapes, out_specs, n_acc=0, grid_spec_prefetch=None, comm=None):
    n_in = len(operands)
    n_out = len(out_shapes)
    n_pre = 0 if grid_spec_prefetch is None else 1

    def body(*refs):
        refs = refs[n_pre:]
        ins = refs[:n_in]
        outs = refs[n_in:n_in + n_out]
        res = fn(*[r[...] for r in ins])
        if not isinstance(res, (tuple, list)):
            res = (res,)
        first = pl.program_id(0) == 0
        for d in range(1, len(grid)):
            first = jnp.logical_and(first, pl.program_id(d) == 0)
        for idx, (o, r) in enumerate(zip(outs, res)):
            if idx < n_out - n_acc:
                o[...] = r.astype(o.dtype)
            else:
                @pl.when(first)
                def _(o=o, r=r):
                    o[...] = r.astype(o.dtype)

                @pl.when(jnp.logical_not(first))
                def _(o=o, r=r):
                    o[...] += r.astype(o.dtype)

    if comm is not None:
        return _call(name, body, list(operands), list(in_specs), list(out_shapes), list(out_specs), [], grid, comm)
    if grid_spec_prefetch is None:
        return _pcall(body, name=name, grid=grid, in_specs=in_specs, out_specs=out_specs, out_shape=out_shapes,
                      compiler_params=_params(len(grid)))(*operands)
    gs = pltpu.PrefetchScalarGridSpec(num_scalar_prefetch=1, grid=grid, in_specs=in_specs, out_specs=out_specs)
    return _pcall(body, name=name, grid_spec=gs, out_shape=out_shapes,
                  compiler_params=_params(len(grid)))(grid_spec_prefetch, *operands)


def _row_spec(tm, w):
    return pl.BlockSpec((tm, w), lambda i: (i, 0))


def _const_spec(shape):
    nd = len(shape)
    return pl.BlockSpec(tuple(shape), lambda *_: (0,) * nd)


def _rms_fwd(x, g):
    r = lax.rsqrt(jnp.mean(x * x, axis=-1, keepdims=True) + NORM_EPS)
    return x * r * g


def _rms_bwd(x, g, dy):
    r = lax.rsqrt(jnp.mean(x * x, axis=-1, keepdims=True) + NORM_EPS)
    xh = x * r
    u = dy * g
    dx = r * (u - xh * jnp.mean(u * xh, axis=-1, keepdims=True))
    dg = jnp.sum(dy * xh, axis=0, keepdims=True)
    return dx, dg


def _gelu_and_grad(z):
    cdf = 0.5 * (1.0 + lax.erf(z * (2.0 ** -0.5)))
    return cdf + z * jnp.exp(-0.5 * z * z) * ((2.0 * math.pi) ** -0.5), z * cdf


def _rope_fwd(x, cc, sa, sb):
    return x * cc + pltpu.roll(x, 96, 1) * sa + pltpu.roll(x, 32, 1) * sb


def _rope_bwd(d, cc, sa, sb):
    return d * cc + pltpu.roll(d * sa, 32, 1) + pltpu.roll(d * sb, 96, 1)


def _adam(w, g, m, v):
    m = ADAM_B1 * m + (1.0 - ADAM_B1) * g
    v = ADAM_B2 * v + (1.0 - ADAM_B2) * (g * g)
    m_hat = m / (1.0 - ADAM_B1 ** ADAM_STEP)
    v_hat = v / (1.0 - ADAM_B2 ** ADAM_STEP)
    delta = -ADAM_LR * (m_hat / (jnp.sqrt(v_hat) + ADAM_EPS) + ADAM_WD * w)
    return delta, m, v


def _flash_fwd(q, k, vt, tq, comm=None):
    h, t = vt.shape[0], q.shape[0]
    nq = t // tq

    chunk_blocks = [c for c in (4, 2) if c < nq]

    def body(q_ref, k_ref, vt_ref, o_ref, lse_ref, m_ref, l_ref, acc_ref):
        qi = pl.program_id(1)
        m_ref[...] = jnp.full((1, tq), NEG, F32)
        l_ref[...] = jnp.zeros((1, tq), F32)
        acc_ref[...] = jnp.zeros((VDIM, tq), F32)

        def update(kb0, nblk, masked):
            kb = k_ref[pl.ds(pl.multiple_of(kb0 * tq, tq), nblk * tq), :]
            st = lax.dot_general(kb, q_ref[...], NT, preferred_element_type=F32)
            if masked:
                key = lax.broadcasted_iota(jnp.int32, (nblk * tq, tq), 0) - (nblk - 1) * tq
                qry = lax.broadcasted_iota(jnp.int32, (nblk * tq, tq), 1)
                st = jnp.where(key <= qry, st, NEG)
            m_old = m_ref[...]
            m_new = jnp.maximum(m_old, jnp.max(st, axis=0, keepdims=True))
            alpha = jnp.exp2((m_old - m_new) * EXP2_SCALE)
            pt = jnp.exp2((st - m_new) * EXP2_SCALE)
            l_ref[...] = alpha * l_ref[...] + jnp.sum(pt, axis=0, keepdims=True)
            ptb = pt.astype(BF16)
            pv = lax.dot_general(vt_ref[kb0], ptb[:tq], NN, preferred_element_type=F32)
            for j in range(1, nblk):
                pv += lax.dot_general(vt_ref[kb0 + j], ptb[j * tq:(j + 1) * tq], NN, preferred_element_type=F32)
            acc_ref[...] = alpha * acc_ref[...] + pv
            m_ref[...] = m_new

        start = jnp.int32(0)
        for c in chunk_blocks:
            take = (qi & c) != 0

            @pl.when(take)
            def _(start=start, c=c):
                update(start, c, False)

            start = start + jnp.where(take, c, 0)
        if nq > 1:
            @pl.when((qi & 1) != 0)
            def _():
                update(qi - 1, 2, True)

            @pl.when((qi & 1) == 0)
            def _():
                update(qi, 1, True)
        else:
            update(qi, 1, True)
        l = l_ref[...]
        o_ref[...] = (acc_ref[...] / l).T.astype(o_ref.dtype)
        lse_ref[...] = m_ref[...] * EXP2_SCALE + jnp.log2(l)

    return _call(
        "flash_fwd", body, [q, k, vt],
        [pl.BlockSpec((tq, QPAD), lambda hh, i: (i, hh)),
         pl.BlockSpec((t, QPAD), lambda hh, i: (0, hh)),
         pl.BlockSpec((None, nq, VDIM, tq), lambda hh, i: (hh, 0, 0, 0))],
        [_sds((t, h * VDIM), BF16), _sds((h, nq, 1, tq), F32)],
        [pl.BlockSpec((tq, VDIM), lambda hh, i: (i, hh)),
         pl.BlockSpec((None, None, 1, tq), lambda hh, i: (hh, i, 0, 0))],
        [pltpu.VMEM((1, tq), F32), pltpu.VMEM((1, tq), F32), pltpu.VMEM((VDIM, tq), F32)], (h, nq), comm)


def _flash_bwd(q, k, v, o, do, lse, tabs, tq, comm=None):
    t = q.shape[0]
    h = q.shape[1] // QPAD
    nq = t // tq

    def body(q_ref, k_ref, v_ref, o_ref, do_ref, lse_ref, cc_ref, sa_ref, sb_ref, dq_ref, dk_out, dv_out, delta_ref, dqt_ref, dk_ref, dv_ref):
        kj = pl.program_id(1)

        @pl.when(kj == 0)
        def _():
            dqt_ref[...] = jnp.zeros_like(dqt_ref)
            ones = jnp.ones((8, VDIM), BF16)
            for qi in range(nq):
                rows = pl.ds(qi * tq, tq)
                prod = do_ref[rows, :].astype(F32) * o_ref[rows, :].astype(F32)
                hi = prod.astype(BF16)
                lo = (prod - hi.astype(F32)).astype(BF16)
                delta_ref[qi] = (lax.dot_general(ones, hi, NT, preferred_element_type=F32)
                                 + lax.dot_general(ones, lo, NT, preferred_element_type=F32))

        kb = k_ref[...]
        vb = v_ref[...]
        kbt = kb.astype(F32).T.astype(BF16)
        dk_ref[...] = jnp.zeros_like(dk_ref)
        dv_ref[...] = jnp.zeros_like(dv_ref)

        def step(q0, nblk, masked):
            rows = pl.ds(pl.multiple_of(q0 * tq, tq), nblk * tq)
            qb = q_ref[rows, :]
            dob = do_ref[rows, :]
            lse = jnp.concatenate([lse_ref[q0 + j] for j in range(nblk)], axis=1)
            delta = jnp.concatenate([delta_ref[q0 + j, pl.ds(0, 1), :] for j in range(nblk)], axis=1)
            st = lax.dot_general(kb, qb, NT, preferred_element_type=F32)
            pt = jnp.exp2(st * EXP2_SCALE - lse)
            if masked:
                key = lax.broadcasted_iota(jnp.int32, (tq, nblk * tq), 0)
                qry = lax.broadcasted_iota(jnp.int32, (tq, nblk * tq), 1)
                pt = jnp.where(key <= qry, pt, 0.0)
            dv_ref[...] += lax.dot_general(pt.astype(BF16), dob, NN, preferred_element_type=F32)
            dpt = lax.dot_general(vb, dob, NT, preferred_element_type=F32)
            dst = (pt * (dpt - delta) * ATTN_SCALE).astype(BF16)
            dk_ref[...] += lax.dot_general(dst, qb, NN, preferred_element_type=F32)
            dqt = lax.dot_general(kbt, dst, NN, preferred_element_type=F32)
            for j in range(nblk):
                dqt_ref[q0 + j] += dqt[:, j * tq:(j + 1) * tq]

        later = nq - 1 - kj
        if nq > 1:
            @pl.when((later & 1) != 0)
            def _():
                step(kj, 2, True)

            @pl.when((later & 1) == 0)
            def _():
                step(kj, 1, True)
        else:
            step(kj, 1, True)
        start = kj + 1 + (later & 1)
        for c in [c for c in (2, 4) if c < nq]:
            take = (later & c) != 0

            @pl.when(take)
            def _(start=start, c=c):
                step(start, c, False)

            start = start + jnp.where(take, c, 0)
        dk_out[...] = dk_ref[...].astype(BF16)
        dv_out[...] = dv_ref[...].astype(BF16)

        @pl.when(kj == nq - 1)
        def _():
            for qi in range(nq):
                rows = pl.ds(qi * tq, tq)
                d = dqt_ref[qi].T
                roped = _rope_bwd(d[:, NOPE:], cc_ref[rows, :], sa_ref[rows, :], sb_ref[rows, :])
                dq_ref[rows, :] = jnp.concatenate([d[:, :NOPE], roped], axis=1).astype(BF16)

    head_q = pl.BlockSpec((t, QPAD), lambda hh, j: (0, hh))
    head_v = pl.BlockSpec((t, VDIM), lambda hh, j: (0, hh))
    table = pl.BlockSpec((t, 128), lambda hh, j: (0, 0))
    return _call(
        "flash_bwd", body, [q, k, v, o, do, lse, *tabs],
        [head_q, pl.BlockSpec((tq, QPAD), lambda hh, j: (j, hh)), pl.BlockSpec((tq, VDIM), lambda hh, j: (j, hh)), head_v, head_v,
         pl.BlockSpec((None, nq, 1, tq), lambda hh, j: (hh, 0, 0, 0)), table, table, table],
        [_sds((t, h * QPAD), BF16), _sds((t, h * QPAD), BF16), _sds((t, h * VDIM), BF16)],
        [head_q, pl.BlockSpec((tq, QPAD), lambda hh, j: (j, hh)), pl.BlockSpec((tq, VDIM), lambda hh, j: (j, hh))],
        [pltpu.VMEM((nq, 8, tq), F32), pltpu.VMEM((nq, QPAD, tq), F32), pltpu.VMEM((tq, QPAD), F32), pltpu.VMEM((tq, VDIM), F32)], (h, nq), comm)


def _tril_bf16(w):
    row = lax.broadcasted_iota(jnp.int32, w.shape, 0)
    col = lax.broadcasted_iota(jnp.int32, w.shape, 1)
    return jnp.where(col <= row, w, 0.0).astype(BF16)


def _layer_norm_parts(v0):
    mu = jnp.mean(v0, axis=-1, keepdims=True)
    vc = v0 - mu
    rstd = lax.rsqrt(jnp.mean(vc * vc, axis=-1, keepdims=True) + LN_EPS)
    return vc * rstd, rstd


def _sgu_mid_fwd(ge, ln_g, ln_b, w_sp, b_sp, chunks_per_step):
    t, e2 = ge.shape
    e = e2 // 2
    gd = e // SGU_GROUPS
    rows = SGU_CHUNK * chunks_per_step

    def body(u_ref, v_ref, g_ref, b_ref, w_ref, bs_ref, gate_ref):
        for ck in range(chunks_per_step):
            r = pl.ds(ck * SGU_CHUNK, SGU_CHUNK)
            xh, _ = _layer_norm_parts(v_ref[r, :].astype(F32))
            v1 = (xh * g_ref[...] + b_ref[...]).astype(BF16)
            for g in range(SGU_GROUPS):
                cols = pl.ds(g * gd, gd)
                mixed = lax.dot_general(_tril_bf16(w_ref[g]), v1[:, g * gd:(g + 1) * gd], NN, preferred_element_type=F32) + bs_ref[g]
                gate_ref[r, cols] = (u_ref[r, cols].astype(F32) * mixed).astype(BF16)

    return _pcall(
        body, name="sgu_mid_fwd", grid=(t // rows,),
        in_specs=[pl.BlockSpec((rows, e), lambda i: (i, 0)), pl.BlockSpec((rows, e), lambda i: (i, 1)),
                  _const_spec((1, e)), _const_spec((1, e)), _const_spec(w_sp.shape), _const_spec(b_sp.shape)],
        out_specs=pl.BlockSpec((rows, e), lambda i: (i, 0)),
        out_shape=_sds((t, e), BF16), compiler_params=_params(1),
    )(ge, ge, ln_g, ln_b, w_sp, b_sp)


def _sgu_mid_bwd(ge, gp, dgate, ln_g, ln_b, w_sp, b_sp, chunks_per_step):
    t, e2 = ge.shape
    e = e2 // 2
    gd = e // SGU_GROUPS
    rows = SGU_CHUNK * chunks_per_step

    def body(u_ref, v_ref, zu_ref, zv_ref, dg_ref, g_ref, b_ref, w_ref, bs_ref, dz_ref, dw_ref, dbs_ref, dlg_ref, dlb_ref):
        @pl.when(pl.program_id(0) == 0)
        def _():
            dw_ref[...] = jnp.zeros_like(dw_ref)
            dbs_ref[...] = jnp.zeros_like(dbs_ref)
            dlg_ref[...] = jnp.zeros_like(dlg_ref)
            dlb_ref[...] = jnp.zeros_like(dlb_ref)

        for ck in range(chunks_per_step):
            r = pl.ds(ck * SGU_CHUNK, SGU_CHUNK)
            xh, rstd = _layer_norm_parts(v_ref[r, :].astype(F32))
            v1 = (xh * g_ref[...] + b_ref[...]).astype(BF16)
            dv1_parts = []
            for g in range(SGU_GROUPS):
                cols = pl.ds(g * gd, gd)
                wc = _tril_bf16(w_ref[g])
                v1g = v1[:, g * gd:(g + 1) * gd]
                mixed = lax.dot_general(wc, v1g, NN, preferred_element_type=F32) + bs_ref[g]
                dgate = dg_ref[r, cols].astype(F32)
                dmixed = dgate * u_ref[r, cols].astype(F32)
                du = dgate * mixed
                dz_ref[r, cols] = (du * zu_ref[r, cols].astype(F32)).astype(BF16)
                dbs_ref[g] += jnp.sum(dmixed, axis=1, keepdims=True)
                dmb = dmixed.astype(BF16)
                dwg = lax.dot_general(dmb, v1g, NT, preferred_element_type=F32)
                row = lax.broadcasted_iota(jnp.int32, dwg.shape, 0)
                col = lax.broadcasted_iota(jnp.int32, dwg.shape, 1)
                dw_ref[g] += jnp.where(col <= row, dwg, 0.0)
                dv1_parts.append(lax.dot_general(wc, dmb, TN, preferred_element_type=F32))
            dv1 = jnp.concatenate(dv1_parts, axis=1)
            dlg_ref[...] += jnp.sum(dv1 * xh, axis=0, keepdims=True)
            dlb_ref[...] += jnp.sum(dv1, axis=0, keepdims=True)
            dxh = dv1 * g_ref[...]
            dv0 = rstd * (dxh - jnp.mean(dxh, axis=-1, keepdims=True) - xh * jnp.mean(dxh * xh, axis=-1, keepdims=True))
            dz_ref[r, pl.ds(e, e)] = (dv0 * zv_ref[r, :].astype(F32)).astype(BF16)

    half0 = pl.BlockSpec((rows, e), lambda i: (i, 0))
    half1 = pl.BlockSpec((rows, e), lambda i: (i, 1))
    return _pcall(
        body, name="sgu_mid_bwd", grid=(t // rows,),
        in_specs=[half0, half1, half0, half1, half0, _const_spec((1, e)), _const_spec((1, e)), _const_spec(w_sp.shape), _const_spec(b_sp.shape)],
        out_specs=[pl.BlockSpec((rows, e2), lambda i: (i, 0)), _const_spec(w_sp.shape), _const_spec(b_sp.shape), _const_spec((1, e)), _const_spec((1, e))],
        out_shape=[_sds((t, e2), BF16), _sds(w_sp.shape, F32), _sds(b_sp.shape, F32), _sds((1, e), F32), _sds((1, e), F32)],
        compiler_params=_params(1),
    )(ge, ge, gp, gp, dgate, ln_g, ln_b, w_sp, b_sp)


def kernel(x, positions, norm_mix, norm_ffn, final_norm, mla_w_dkv, mla_q_norm, mla_kv_norm, mla_w_uq, mla_w_ukv, mla_w_o, sgu_w_in, sgu_ln_g, sgu_ln_b, sgu_w_spatial, sgu_b_spatial, sgu_w_out, ffn_w_up, ffn_w_down, loss_target, m_norm_mix, m_norm_ffn, m_final_norm, m_mla_w_dkv, m_mla_q_norm, m_mla_kv_norm, m_mla_w_uq, m_mla_w_ukv, m_mla_w_o, m_sgu_w_in, m_sgu_ln_g, m_sgu_ln_b, m_sgu_w_spatial, m_sgu_b_spatial, m_sgu_w_out, m_ffn_w_up, m_ffn_w_down, v_norm_mix, v_norm_ffn, v_final_norm, v_mla_w_dkv, v_mla_q_norm, v_mla_kv_norm, v_mla_w_uq, v_mla_w_ukv, v_mla_w_o, v_sgu_w_in, v_sgu_ln_g, v_sgu_ln_b, v_sgu_w_spatial, v_sgu_b_spatial, v_sgu_w_out, v_ffn_w_up, v_ffn_w_down):
    _, T, D = x.shape
    depth = norm_mix.shape[0]
    n_mla, n_sgu = mla_w_dkv.shape[0], sgu_w_in.shape[0]
    assert depth % 2 == 0
    FF = ffn_w_up.shape[2] * N_DEV
    E = sgu_w_out.shape[1] * N_DEV
    ffc, ec, e2c = FF // N_DEV, E // N_DEV, 2 * E // N_DEV
    dc = D // N_DEV
    OW = HEADS * VDIM
    HW = HEADS * QPAD
    owc = OW // N_DEV
    tm = _tile(T, 1024)
    tb = _tile(T, 4096)
    tk = _tile(T, 512)
    tq = _tile(T, 512)
    ts = _tile(T, 256)
    nt = T // tm
    x2 = x.reshape(T, D)
    tgt = loss_target.reshape(T, D)
    cidx = lax.axis_index("c").astype(jnp.int32).reshape(1)

    ln_local = jnp.concatenate([sgu_ln_g, sgu_ln_b, jnp.zeros((8 - 2 * n_sgu, ec), F32)], axis=0)
    mla_sh = [[w[l].astype(BF16) for w in (mla_w_dkv, mla_w_uq, mla_w_ukv, mla_w_o)] for l in range(n_mla)]

    def mla_layouts(g_dkv, g_uq, g_ukv, g_o):
        w_dkv = jnp.pad(g_dkv.reshape(1, D, LAT), ((0, 0), (0, 0), (0, LAT_PAD - LAT)))
        w_uq = jnp.pad(g_uq, ((0, 0), (0, 0), (0, QPAD - NOPE - ROPE))).transpose(1, 0, 2).reshape(1, Q_RANK, HEADS * QPAD)
        w_ukv = g_ukv.transpose(1, 0, 2).reshape(1, KV_RANK, HEADS * (NOPE + VDIM))
        return w_dkv, w_uq, w_ukv, g_o.reshape(1, HEADS * VDIM, D)

    mla_w = [None] * n_mla
    small_later = [a for l in range(1, n_mla) for a in mla_sh[l]] + [ln_local]
    ln_g_full, ln_b_full = [None] * n_sgu, [None] * n_sgu
    b_sp = sgu_b_spatial.reshape(n_sgu, SGU_GROUPS, SGU_CHUNK, 1)
    up_sh = [ffn_w_up[i].astype(BF16) for i in range(depth)]
    down_sh = [ffn_w_down[i].astype(BF16) for i in range(depth)]
    in_sh = [sgu_w_in[l].astype(BF16) for l in range(n_sgu)]
    out_sh = [sgu_w_out[l].astype(BF16) for l in range(n_sgu)]
    g_up, g_down, g_in, g_out = [None] * depth, [None] * depth, [None] * n_sgu, [None] * n_sgu

    inv_freq = ROPE_THETA ** (-jnp.arange(0, ROPE, 2, dtype=F32) / ROPE)
    zeros32 = jnp.zeros((ROPE // 2,), F32)
    inv128 = jnp.concatenate([inv_freq, inv_freq, zeros32, zeros32]).reshape(1, 128)
    sel_a = jnp.concatenate([-jnp.ones((32,), F32), zeros32, zeros32, zeros32]).reshape(1, 128)
    sel_b = jnp.concatenate([zeros32, jnp.ones((32,), F32), zeros32, zeros32]).reshape(1, 128)
    sel_c = jnp.concatenate([jnp.ones((64,), F32), zeros32, zeros32]).reshape(1, 128)

    def rope_tables(pos, inv, sa, sb, sc):
        ang = pos.astype(F32) * inv
        cs, sn = jnp.cos(ang), jnp.sin(ang)
        return cs * sc, sn * sa, sn * sb

    t_cc, t_sa, t_sb, *first_half = _rowwise(
        "rope_tables", rope_tables, [positions.reshape(T, 1), inv128, sel_a, sel_b, sel_c], grid=(nt,),
        in_specs=[_row_spec(tm, 1)] + [_const_spec((1, 128))] * 4,
        out_shapes=[_sds((T, 128), F32)] * 3, out_specs=[_row_spec(tm, 128)] * 3, comm=_gather_level1(mla_sh[0]))
    tab_specs = [_row_spec(tm, 128)] * 3

    def rmsnorm(xv, g, comm):
        return _rowwise("rmsnorm", lambda a, gg: _rms_fwd(a, gg), [xv, g.reshape(1, D)], grid=(nt,),
                        in_specs=[_row_spec(tm, D), _const_spec((1, D))], out_shapes=[_sds((T, D), BF16)], out_specs=[_row_spec(tm, D)], comm=comm)

    def proj_cols(name, h, gw, nc, epilogue, n_out, comm=None):
        return _matmul(name, h, gw, [], grid=(N_DEV, T // tb),
                       a_spec=pl.BlockSpec((tb, D), lambda j, i: (i, 0)),
                       b_spec=pl.BlockSpec((None, D, nc), lambda j, i: (j, 0, 0)), extra_specs=[],
                       out_shapes=[_sds((T, nc * N_DEV), BF16)] * n_out, out_specs=[pl.BlockSpec((tb, nc), lambda j, i: (i, j))] * n_out,
                       dims=NN, epilogue=epilogue, comm=comm)

    def residual_norm(acc, xr, g):
        xn = acc + xr
        return xn, _rms_fwd(xn, g)

    def proj_rows_residual(name, a, gw, xres, g_next, comm=None):
        kk_ = a.shape[1]
        return _matmul(name, a, gw.reshape(kk_, D), [xres, g_next.reshape(1, D)], grid=(T // tk,),
                       a_spec=_row_spec(tk, kk_), b_spec=_const_spec((kk_, D)), extra_specs=[_row_spec(tk, D), _const_spec((1, D))],
                       out_shapes=[_sds((T, D), F32), _sds((T, D), BF16)], out_specs=[_row_spec(tk, D)] * 2,
                       dims=NN, epilogue=residual_norm, comm=comm)

    def back_rows(name, dy, gw, kc, extras, epilogue, comm=None):
        return _matmul(name, dy, gw, extras, grid=(N_DEV, T // tb),
                       a_spec=pl.BlockSpec((tb, D), lambda j, i: (i, 0)),
                       b_spec=pl.BlockSpec((None, kc, D), lambda j, i: (j, 0, 0)),
                       extra_specs=[pl.BlockSpec((tb, kc), lambda j, i: (i, j))] * len(extras),
                       out_shapes=[_sds((T, kc * N_DEV), BF16)], out_specs=[pl.BlockSpec((tb, kc), lambda j, i: (i, j))],
                       dims=NT, epilogue=epilogue, comm=comm)

    def norm_bwd_epilogue(dh, xv, g, dxi):
        dxn, dg = _rms_bwd(xv, g, dh)
        return dxi + dxn, dxi + dxn, dg

    def transposed(gw):
        return gw.transpose(0, 2, 1).reshape(gw.shape[0] * gw.shape[2], D)

    def back_cols(name, da, gwt, xv, g, dx_in, comm=None):
        n = da.shape[1]
        row = _row_spec(tk, D)
        return _matmul(name, da, gwt, [xv, g.reshape(1, D), dx_in], grid=(T // tk,),
                       a_spec=_row_spec(tk, n), b_spec=_const_spec((n, D)), extra_specs=[row, _const_spec((1, D)), row],
                       out_shapes=[_sds((T, D), F32), _sds((T, D), BF16), _sds((1, D), F32)], out_specs=[row, row, _const_spec((1, D))],
                       dims=NN, epilogue=norm_bwd_epilogue, n_sum=1, comm=comm)

    def token_sum(tt):
        return dict(k_axis=1, nk=T // tt) if T // tt > 1 else dict(k_axis=None)

    def wgrad_cols(name, h, da, nc):
        return _matmul(name, h, da, [], grid=(N_DEV, T // tb),
                       a_spec=pl.BlockSpec((tb, D), lambda j, t: (t, 0)), b_spec=pl.BlockSpec((tb, nc), lambda j, t: (t, j)),
                       extra_specs=[], out_shapes=[_sds((N_DEV, D, nc), BF16)],
                       out_specs=[pl.BlockSpec((None, D, nc), lambda j, t: (j, 0, 0))],
                       dims=TN, acc_shape=(D, nc), **token_sum(tb))[0]

    def wgrad_rows(name, a, dy, kc, ncols, tt):
        return _matmul(name, a, dy, [], grid=(a.shape[1] // kc, T // tt),
                       a_spec=pl.BlockSpec((tt, kc), lambda j, t: (t, j)), b_spec=pl.BlockSpec((tt, ncols), lambda j, t: (t, 0)),
                       extra_specs=[], out_shapes=[_sds((a.shape[1], ncols), BF16)],
                       out_specs=[pl.BlockSpec((kc, ncols), lambda j, t: (j, 0))],
                       dims=TN, acc_shape=(kc, ncols), **token_sum(tt))[0]

    saved = []
    xs = x2
    for i in range(depth):
        l = i // 2
        if i == 0:
            h, *first_w = rmsnorm(xs, norm_mix[0], _gather_level2(first_half))
            mla_w[0] = mla_layouts(*first_w)
        if i % 2 == 0:
            w_dkv, w_uq, w_ukv, w_o = mla_w[l]
            lat = _matmul("mla_down", h, w_dkv, [], grid=(nt,), a_spec=_row_spec(tm, D),
                          b_spec=pl.BlockSpec((None, D, LAT_PAD), lambda i_: (0, 0, 0)), extra_specs=[],
                          out_shapes=[_sds((T, LAT_PAD), F32)], out_specs=[_row_spec(tm, LAT_PAD)], dims=NN)[0]

            def latent_post(la, qn, kvn, cc, sa, sb):
                cq = _rms_fwd(la[:, :Q_RANK], qn)
                ckv = _rms_fwd(la[:, Q_RANK:Q_RANK + KV_RANK], kvn)
                kr = _rope_fwd(la[:, Q_RANK + KV_RANK:], cc, sa, sb)
                return cq, ckv, kr

            cq, ckv, kr = _rowwise(
                "mla_latent", latent_post, [lat, mla_q_norm[l].reshape(1, Q_RANK), mla_kv_norm[l].reshape(1, KV_RANK), t_cc, t_sa, t_sb],
                grid=(nt,), in_specs=[_row_spec(tm, LAT_PAD), _const_spec((1, Q_RANK)), _const_spec((1, KV_RANK))] + tab_specs,
                out_shapes=[_sds((T, Q_RANK), BF16), _sds((T, KV_RANK), BF16), _sds((T, 128), BF16)],
                out_specs=[_row_spec(tm, Q_RANK), _row_spec(tm, KV_RANK), _row_spec(tm, 128)])

            def q_epilogue(acc, cc, sa, sb):
                parts = []
                for b in range(HEADS):
                    parts += [acc[:, b * QPAD:b * QPAD + NOPE], _rope_fwd(acc[:, b * QPAD + NOPE:(b + 1) * QPAD], cc, sa, sb)]
                return (jnp.concatenate(parts, axis=1),)

            q = _matmul("mla_q", cq, w_uq, [t_cc, t_sa, t_sb], grid=(nt,), a_spec=_row_spec(tm, Q_RANK),
                        b_spec=pl.BlockSpec((None, Q_RANK, HW), lambda i_: (0, 0, 0)), extra_specs=tab_specs,
                        out_shapes=[_sds((T, HW), BF16)], out_specs=[_row_spec(tm, HW)], dims=NN, epilogue=q_epilogue)[0]

            def kv_write(outs, acc, krb):
                k_ref, v_ref, vt_ref = outs
                for b in range(HEADS):
                    vb = acc[:, b * QPAD + NOPE:(b + 1) * QPAD]
                    k_ref[:, b * QPAD:b * QPAD + NOPE] = acc[:, b * QPAD:b * QPAD + NOPE].astype(BF16)
                    k_ref[:, b * QPAD + NOPE:(b + 1) * QPAD] = krb
                    v_ref[:, b * VDIM:(b + 1) * VDIM] = vb.astype(BF16)
                    vbt = vb.T.astype(BF16)
                    for u in range(tm // tq):
                        vt_ref[b, u] = vbt[:, u * tq:(u + 1) * tq]

            kk, vv, vt = _matmul("mla_kv", ckv, w_ukv, [kr], grid=(nt,), a_spec=_row_spec(tm, KV_RANK),
                                 b_spec=pl.BlockSpec((None, KV_RANK, HW), lambda i_: (0, 0, 0)), extra_specs=[_row_spec(tm, 128)],
                                 out_shapes=[_sds((T, HW), BF16), _sds((T, OW), BF16), _sds((HEADS, T // tq, VDIM, tq), BF16)],
                                 out_specs=[_row_spec(tm, HW), _row_spec(tm, OW), pl.BlockSpec((HEADS, tm // tq, VDIM, tq), lambda i_: (0, i_, 0, 0))],
                                 dims=NN, write=kv_write)
            group = [up_sh[i], down_sh[i], in_sh[l], out_sh[l]] + (small_later if i == 0 else [])
            o, lse, *bufs = _flash_fwd(q, kk, vt, tq, comm=_gather_level1(group))
            xm, h2, g_up[i], g_down[i] = _matmul(
                "mla_out", o, w_o, [xs, norm_ffn[i].reshape(1, D)], grid=(nt,), a_spec=_row_spec(tm, OW),
                b_spec=pl.BlockSpec((None, OW, D), lambda i_: (0, 0, 0)), extra_specs=[_row_spec(tm, D), _const_spec((1, D))],
                out_shapes=[_sds((T, D), F32), _sds((T, D), BF16)], out_specs=[_row_spec(tm, D)] * 2, dims=NN,
                epilogue=residual_norm, comm=_gather_level2(bufs[:2]))
            half_gathered = bufs[2:]
            mix_saved = (h, lat, cq, ckv, q, kk, vv, o, lse)
        else:
            gp, ge, g_down[i], up_half = proj_cols("sgu_in", h, g_in[l], e2c, _gelu_and_grad, 2,
                                                   comm=_merge_comm(_gather_level2([down_half]), _gather_level1([up_sh[i]])))
            gate = _sgu_mid_fwd(ge, ln_g_full[l], ln_b_full[l], sgu_w_spatial[l], b_sp[l], 4)
            xm, h2, g_up[i] = proj_rows_residual("sgu_out", gate, g_out[l], xs, norm_ffn[i], comm=_gather_level2([up_half]))
            mix_saved = (h, gp, ge, gate)
        r, s, *rest = proj_cols("ffn_up", h2, g_up[i], ffc, lambda acc: (jnp.maximum(acc, 0.0), jnp.square(jnp.maximum(acc, 0.0))), 2,
                                comm=_gather_level2(half_gathered) if i % 2 == 0 else None)
        if i % 2 == 0:
            g_in[l], g_out[l], *small_gathered = rest
        if i == 0:
            for l_ in range(1, n_mla):
                mla_w[l_] = mla_layouts(*small_gathered[4 * (l_ - 1):4 * l_])
            g_ln = small_gathered[-1]
            ln_g_full = [g_ln[:, l_, :].reshape(1, E) for l_ in range(n_sgu)]
            ln_b_full = [g_ln[:, n_sgu + l_, :].reshape(1, E) for l_ in range(n_sgu)]
        xo, h_next, *rest = proj_rows_residual("ffn_down", s, g_down[i], xm, norm_mix[i + 1] if i + 1 < depth else final_norm,
                                               comm=_gather_level1([down_sh[i + 1]]) if i % 2 == 0 else None)
        if i % 2 == 0:
            (down_half,) = rest
        saved.append((xs, xm, mix_saved, h2, r, s))
        xs, h = xo, h_next

    def loss_head(xv, tg, g):
        y = _rms_fwd(xv, g)
        err = y - tg
        part = 0.5 * jnp.sum(jnp.sum(err * err, axis=-1, keepdims=True), axis=0, keepdims=True) / D
        dx, dg = _rms_bwd(xv, g, err / D)
        return dx, dx, jnp.broadcast_to(part, (1, 128)), dg

    dx, dyb, loss_part, d_final = _rowwise(
        "loss_head", loss_head, [xs, tgt, final_norm.reshape(1, D)], grid=(nt,),
        in_specs=[_row_spec(tm, D), _row_spec(tm, D), _const_spec((1, D))],
        out_shapes=[_sds((T, D), F32), _sds((T, D), BF16), _sds((1, 128), F32), _sds((1, D), F32)],
        out_specs=[_row_spec(tm, D), _row_spec(tm, D), _const_spec((1, 128)), _const_spec((1, D))], n_acc=2)
    loss = lax.psum(loss_part[0, 0], ("x", "y", "c"))

    d_norm_mix, d_norm_ffn = [None] * depth, [None] * depth
    d_qn, d_kvn = [None] * n_mla, [None] * n_mla
    d_wsp, d_bsp, d_lng, d_lnb = [None] * n_sgu, [None] * n_sgu, [None] * n_sgu, [None] * n_sgu
    layers = {"dkv": n_mla, "uq": n_mla, "ukv": n_mla, "o": n_mla, "in": n_sgu, "out": n_sgu, "up": depth, "down": depth}
    stacked = {nm: None for nm in layers}
    pending = []
    summed = []

    def add_pairs(gs, rcvs):
        operands, in_specs, out_shapes, out_specs = [], [], [], []
        for g, rcv in zip(gs, rcvs):
            _, rws, cls = g.shape
            slab = pl.BlockSpec((None, rws, cls), lambda ch, cr: (ch, 0, 0))
            operands += [g.reshape(N_CHIP, 2, rws, cls), rcv]
            in_specs += [pl.BlockSpec((None, None, rws, cls), lambda ch, cr: (ch, cr[0], 0, 0)), slab]
            out_shapes.append(_sds(rcv.shape, BF16))
            out_specs.append(slab)

        def fn(*blocks):
            return tuple(blocks[2 * k].astype(F32) + blocks[2 * k + 1].astype(F32) for k in range(len(gs)))

        return _rowwise("grad_pair_sum", fn, operands, grid=(N_CHIP,), in_specs=in_specs, out_shapes=out_shapes, out_specs=out_specs,
                        grid_spec_prefetch=cidx)

    def sibling_comm():
        return _sibling_exchange([g for _, _, g in pending]) if pending else None

    def absorb(from_sibling):
        if pending:
            parts = add_pairs([g for _, _, g in pending], list(from_sibling))
            summed.extend((nm, l_, p) for (nm, l_, _), p in zip(pending, parts))
            pending.clear()

    def chip_comm():
        if pending:
            absorb(_comm_call("grad_sibling_exchange", sibling_comm()))
        comm, names = _chip_exchange([p for _, _, p in summed], [(nm, l_) for nm, l_, _ in summed], layers, stacked)
        summed.clear()
        return comm, names

    def rows128(a, rows):
        flat = a.reshape(-1, 128)
        return jnp.pad(flat, ((0, rows - flat.shape[0]), (0, 0)))

    def pad_to(n, mult):
        return -(-n // mult) * mult

    def packed(arrs, sizes):
        return jnp.concatenate([rows128(a, sz) for a, sz in zip(arrs, sizes)], axis=0)

    n_wsp, n_bsp, n_ln = sgu_w_spatial.size // 128, pad_to(sgu_b_spatial.size // 128, 8), pad_to(n_sgu * E // 128, 8)
    early_sizes = [n_wsp, pad_to(n_wsp + n_bsp, SMALL_ROWS) - n_wsp, n_ln, n_ln]
    early_rep = early_sizes[0] + early_sizes[1]
    gathered_early = None

    for i in reversed(range(depth)):
        l = i // 2
        xs_i, xm, mix_saved, h2, r, s = saved[i]
        da, *rcv = back_rows("ffn_down_bwd", dyb, g_down[i], ffc, [r], lambda acc, rr: (acc * (2.0 * rr.astype(F32)),), comm=sibling_comm())
        absorb(rcv)
        pending.append(("down", i, wgrad_rows("ffn_down_wgrad", s, dyb, ffc, D, tb).reshape(N_DEV, ffc, D)))
        pending.append(("up", i, wgrad_cols("ffn_up_wgrad", h2, da, ffc)))
        dx, dyb, d_norm_ffn[i], *rcv = back_cols("ffn_up_bwd", da, transposed(g_up[i]), xm, norm_ffn[i], dx, comm=sibling_comm())
        absorb(rcv)
        if i % 2 == 0:
            h, lat, cq, ckv, q, kk, vv, o, lse = mix_saved
            w_dkv, w_uq, w_ukv, w_o = mla_w[l]
            do = _matmul("mla_out_bwd", dyb, w_o, [], grid=(nt,), a_spec=_row_spec(tm, D),
                         b_spec=pl.BlockSpec((None, OW, D), lambda i_: (0, 0, 0)), extra_specs=[],
                         out_shapes=[_sds((T, OW), BF16)], out_specs=[_row_spec(tm, OW)], dims=NT)[0]
            g_o_l = wgrad_rows("mla_out_wgrad", o, dyb, OW, D, tm).reshape(N_DEV, owc, D)
            comm, names = chip_comm()
            if i == 0:
                early = packed([jnp.stack(d_wsp, 0), jnp.stack(d_bsp, 0), jnp.concatenate(d_lng, 0), jnp.concatenate(d_lnb, 0)], early_sizes)
                comm = _merge_comm(comm, _gather_level1([early]))
            dq_pre, dk, dv, *bufs = _flash_bwd(q, kk, vv, o, do, lse, (t_cc, t_sa, t_sb), tq, comm=comm)
            stacked.update(dict(zip(names, bufs)))
            pending.append(("o", l, g_o_l))

            def kv_pre(dkb, dvb, cc, sa, sb):
                parts, dkr = [], None
                for b in range(HEADS):
                    parts += [dkb[:, b * QPAD:b * QPAD + NOPE], dvb[:, b * VDIM:(b + 1) * VDIM]]
                    piece = dkb[:, b * QPAD + NOPE:(b + 1) * QPAD].astype(F32)
                    dkr = piece if dkr is None else dkr + piece
                return jnp.concatenate(parts, axis=1), _rope_bwd(dkr, cc, sa, sb)

            dkv, dkr, *rest = _rowwise("mla_dkv_rope", kv_pre, [dk, dv, t_cc, t_sa, t_sb], grid=(T // ts,),
                                       in_specs=[_row_spec(ts, HW), _row_spec(ts, OW)] + [_row_spec(ts, 128)] * 3,
                                       out_shapes=[_sds((T, HW), BF16), _sds((T, 128), F32)], out_specs=[_row_spec(ts, HW), _row_spec(ts, 128)],
                                       comm=_gather_level2(bufs[len(names):]) if i == 0 else None)
            if i == 0:
                (gathered_early,) = rest
            g_uq_l = wgrad_rows("mla_q_wgrad", cq, dq_pre, Q_RANK, HW, tm)
            g_ukv_l = wgrad_rows("mla_kv_wgrad", ckv, dkv, KV_RANK, HW, tm)
            pending.append(("uq", l, g_uq_l.reshape(Q_RANK, HEADS, QPAD)[:, :, :NOPE + ROPE].transpose(1, 0, 2)))
            pending.append(("ukv", l, g_ukv_l.reshape(KV_RANK, HEADS, NOPE + VDIM).transpose(1, 0, 2)))
            dcq = _matmul("mla_q_bwd", dq_pre, w_uq, [], grid=(nt,), a_spec=_row_spec(tm, HW),
                          b_spec=pl.BlockSpec((None, Q_RANK, HW), lambda i_: (0, 0, 0)), extra_specs=[],
                          out_shapes=[_sds((T, Q_RANK), F32)], out_specs=[_row_spec(tm, Q_RANK)], dims=NT)[0]
            dckv = _matmul("mla_kv_bwd", dkv, w_ukv, [], grid=(nt,), a_spec=_row_spec(tm, HW),
                           b_spec=pl.BlockSpec((None, KV_RANK, HW), lambda i_: (0, 0, 0)), extra_specs=[],
                           out_shapes=[_sds((T, KV_RANK), F32)], out_specs=[_row_spec(tm, KV_RANK)], dims=NT)[0]

            def latent_bwd(la, qn, kvn, dq_, dkv_, dkr_):
                dcq_raw, dqn = _rms_bwd(la[:, :Q_RANK], qn, dq_)
                dckv_raw, dkvn = _rms_bwd(la[:, Q_RANK:Q_RANK + KV_RANK], kvn, dkv_)
                return jnp.concatenate([dcq_raw, dckv_raw, dkr_], axis=1), dqn, dkvn

            dlat, d_qn[l], d_kvn[l] = _rowwise(
                "mla_latent_bwd", latent_bwd, [lat, mla_q_norm[l].reshape(1, Q_RANK), mla_kv_norm[l].reshape(1, KV_RANK), dcq, dckv, dkr],
                grid=(nt,), in_specs=[_row_spec(tm, LAT_PAD), _const_spec((1, Q_RANK)), _const_spec((1, KV_RANK)),
                                      _row_spec(tm, Q_RANK), _row_spec(tm, KV_RANK), _row_spec(tm, 128)],
                out_shapes=[_sds((T, LAT_PAD), BF16), _sds((1, Q_RANK), F32), _sds((1, KV_RANK), F32)],
                out_specs=[_row_spec(tm, LAT_PAD), _const_spec((1, Q_RANK)), _const_spec((1, KV_RANK))], n_acc=2)
            g_dkv_l = wgrad_rows("mla_down_wgrad", h, dlat, D, LAT_PAD, tm)
            pending.append(("dkv", l, g_dkv_l[:, :LAT].reshape(N_DEV, dc, LAT)))
            dx, dyb, d_norm_mix[i] = _matmul(
                "mla_down_bwd", dlat, w_dkv, [xs_i, norm_mix[i].reshape(1, D), dx], grid=(nt,), a_spec=_row_spec(tm, LAT_PAD),
                b_spec=pl.BlockSpec((None, D, LAT_PAD), lambda i_: (0, 0, 0)), extra_specs=[_row_spec(tm, D), _const_spec((1, D)), _row_spec(tm, D)],
                out_shapes=[_sds((T, D), F32), _sds((T, D), BF16), _sds((1, D), F32)],
                out_specs=[_row_spec(tm, D), _row_spec(tm, D), _const_spec((1, D))], dims=NT, epilogue=norm_bwd_epilogue, n_sum=1)
        else:
            h, gp, ge, gate = mix_saved
            (dgate,) = back_rows("sgu_out_bwd", dyb, g_out[l], ec, [], None)
            pending.append(("out", l, wgrad_rows("sgu_out_wgrad", gate, dyb, ec, D, tb).reshape(N_DEV, ec, D)))
            dz, d_wsp[l], d_bsp[l], d_lng[l], d_lnb[l] = _sgu_mid_bwd(ge, gp, dgate, ln_g_full[l], ln_b_full[l], sgu_w_spatial[l], b_sp[l], 2)
            pending.append(("in", l, wgrad_cols("sgu_in_wgrad", h, dz, e2c)))
            dx, dyb, d_norm_mix[i], *rcv = back_cols("sgu_in_bwd", dz, transposed(g_in[l]), xs_i, norm_mix[i], dx, comm=sibling_comm())
            absorb(rcv)
    grad_x = dx.reshape(1, T, D)

    last_comm, last_names = chip_comm()
    late_g = [jnp.concatenate(d_norm_mix, 0), jnp.concatenate(d_norm_ffn, 0), d_final, jnp.concatenate(d_qn, 0), jnp.concatenate(d_kvn, 0)]
    late_w = [norm_mix, norm_ffn, final_norm, mla_q_norm, mla_kv_norm]
    late_m = [m_norm_mix, m_norm_ffn, m_final_norm, m_mla_q_norm, m_mla_kv_norm]
    late_v = [v_norm_mix, v_norm_ffn, v_final_norm, v_mla_q_norm, v_mla_kv_norm]
    late_sizes = [pad_to(g.size // 128, 8) for g in late_g]
    late_rows = sum(late_sizes)

    def adam_big(parts, w, m, v):
        lyr, rws, cls = w.shape
        rt = _tile(rws, 512)

        def fn(p, w_, m_, v_):
            g = (p[0].astype(F32) + p[1].astype(F32)) + (p[2].astype(F32) + p[3].astype(F32))
            return (g, *_adam(w_, g, m_, v_))

        spec = pl.BlockSpec((None, rt, cls), lambda l_, i_: (l_, i_, 0))
        return _rowwise("adam_large", fn, [parts, w, m, v], grid=(lyr, rws // rt),
                        in_specs=[pl.BlockSpec((N_CHIP, None, rt, cls), lambda l_, i_: (0, l_, i_, 0)), spec, spec, spec],
                        out_shapes=[_sds(w.shape, F32)] * 4, out_specs=[spec] * 4)

    stacked.update(dict(zip(last_names, _comm_call("grad_chip_exchange", last_comm))))
    (gathered_late,) = _all_gather("gather_small_grads", [packed(late_g, late_sizes)])
    big = {}
    big["in"] = adam_big(stacked["in"], sgu_w_in, m_sgu_w_in, v_sgu_w_in)
    big["up"] = adam_big(stacked["up"], ffn_w_up, m_ffn_w_up, v_ffn_w_up)
    big["down"] = adam_big(stacked["down"], ffn_w_down, m_ffn_w_down, v_ffn_w_down)
    big["out"] = adam_big(stacked["out"], sgu_w_out, m_sgu_w_out, v_sgu_w_out)
    big["dkv"] = adam_big(stacked["dkv"], mla_w_dkv, m_mla_w_dkv, v_mla_w_dkv)
    big["uq"] = adam_big(stacked["uq"], mla_w_uq, m_mla_w_uq, v_mla_w_uq)
    big["ukv"] = adam_big(stacked["ukv"], mla_w_ukv, m_mla_w_ukv, v_mla_w_ukv)
    big["o"] = adam_big(stacked["o"], mla_w_o, m_mla_w_o, v_mla_w_o)
    big_res = [big[nm][:4] for nm in ("dkv", "uq", "ukv", "o", "in", "out", "up", "down")]

    def sum8(p):
        return ((p[0] + p[1]) + (p[2] + p[3])) + ((p[4] + p[5]) + (p[6] + p[7]))

    def adam_packed(name, gathered, ws, ms, vs, sizes, rows, tile):
        spec = _row_spec(tile, 128)
        return _rowwise(name, lambda p, w_, m_, v_: (sum8(p), *_adam(w_, sum8(p), m_, v_)),
                        [gathered, packed(ws, sizes), packed(ms, sizes), packed(vs, sizes)], grid=(rows // tile,),
                        in_specs=[pl.BlockSpec((N_DEV, tile, 128), lambda i_: (0, i_, 0)), spec, spec, spec],
                        out_shapes=[_sds((rows, 128), F32)] * 4, out_specs=[spec] * 4)

    late_res = adam_packed("adam_small", gathered_late, late_w, late_m, late_v, late_sizes, late_rows, late_rows)
    early_res = adam_packed("adam_spatial", gathered_early, [sgu_w_spatial, sgu_b_spatial], [m_sgu_w_spatial, m_sgu_b_spatial],
                            [v_sgu_w_spatial, v_sgu_b_spatial], early_sizes[:2], early_rep, SMALL_ROWS)

    def unpack(res, sizes, k, like):
        off = sum(sizes[:k])
        return res[off:off + like.size // 128].reshape(like.shape)

    my_b = 4 * lax.axis_index("x") + 2 * lax.axis_index("y") + lax.axis_index("c")
    ln_w = jnp.concatenate([sgu_ln_g, sgu_ln_b], 0)
    ln_m = jnp.concatenate([m_sgu_ln_g, m_sgu_ln_b], 0)
    ln_v = jnp.concatenate([v_sgu_ln_g, v_sgu_ln_b], 0)
    ln_all = jnp.concatenate([gathered_early[:, early_rep:early_rep + n_sgu * E // 128], gathered_early[:, early_rep + n_ln:early_rep + n_ln + n_sgu * E // 128]], axis=1)
    ln_mine = lax.dynamic_slice_in_dim(ln_all.reshape(N_DEV, 2 * n_sgu, N_DEV, ec), my_b, 1, axis=2).reshape(N_DEV, 2 * n_sgu, ec)
    ln_g_, ln_d, ln_m2, ln_v2 = _rowwise(
        "adam_ln", lambda p, w_, m_, v_: (sum8(p), *_adam(w_, sum8(p), m_, v_)), [ln_mine, ln_w, ln_m, ln_v], grid=(1,),
        in_specs=[_const_spec(ln_mine.shape), _const_spec(ln_w.shape), _const_spec(ln_w.shape), _const_spec(ln_w.shape)],
        out_shapes=[_sds(ln_w.shape, F32)] * 4, out_specs=[_const_spec(ln_w.shape)] * 4)

    def family(pos):
        ln = [ln_g_, ln_d, ln_m2, ln_v2][pos]
        late = [unpack(late_res[pos], late_sizes, k, w_) for k, w_ in enumerate(late_w)]
        w_sp_, b_sp_ = unpack(early_res[pos], early_sizes, 0, sgu_w_spatial), unpack(early_res[pos], early_sizes, 1, sgu_b_spatial)
        bigs = [res[pos] for res in big_res]
        return [late[0], late[1], late[2], bigs[0], late[3], late[4], bigs[1], bigs[2], bigs[3],
                bigs[4], ln[:n_sgu], ln[n_sgu:], w_sp_, b_sp_, bigs[5], bigs[6], bigs[7]]

    return (loss, grad_x, *family(0), *family(1), *family(2), *family(3))
```

```python
import math

import jax
import jax.numpy as jnp
from jax import lax
from jax.experimental import pallas as pl
from jax.experimental.pallas import tpu as pltpu

F32 = jnp.float32
BF16 = jnp.bfloat16
MESH = pl.DeviceIdType.MESH

N_DEV = 8
N_CHIP = 4
HEADS = 8
NOPE = 128
ROPE = 64
VDIM = 128
QPAD = 256
Q_RANK = 256
KV_RANK = 128
LAT = Q_RANK + KV_RANK + ROPE
LAT_PAD = 512
ROPE_THETA = 10000.0
SGU_CHUNK = 128
SGU_GROUPS = 8
NORM_EPS = 1e-6
LN_EPS = 1e-5
ADAM_LR = 0.001
ADAM_B1 = 0.9
ADAM_B2 = 0.999
ADAM_EPS = 1e-08
ADAM_WD = 0.01
ADAM_STEP = 10
ATTN_SCALE = (NOPE + ROPE) ** -0.5
NEG = -1e30
EXP2_SCALE = ATTN_SCALE * math.log2(math.e)
VMEM_LIMIT = 56 * 1024 * 1024
SMALL_ROWS = 256

NN = (((1,), (0,)), ((), ()))
NT = (((1,), (1,)), ((), ()))
TN = (((0,), (0,)), ((), ()))
ANY = pl.BlockSpec(memory_space=pl.ANY)


def _pcall(body, **kw):
    return pl.pallas_call(body, **kw)


def _params(n_grid, side_effects=False):
    return pltpu.CompilerParams(dimension_semantics=("arbitrary",) * n_grid, vmem_limit_bytes=VMEM_LIMIT, has_side_effects=side_effects)


def _sds(shape, dtype):
    return jax.ShapeDtypeStruct(tuple(shape), dtype)


def _tile(n, want):
    t = min(n, want)
    assert n % t == 0, (n, want)
    return t


class _Comm:
    def __init__(self, operands, out_shapes, aliases, scratch, start, finish):
        self.operands, self.out_shapes, self.aliases, self.scratch = operands, out_shapes, aliases, scratch
        self.start, self.finish = start, finish


def _merge_comm(first, second):
    n_in, n_out, n_sc = len(first.operands), len(first.out_shapes), len(first.scratch)
    aliases = dict(first.aliases)
    aliases.update({n_in + k: n_out + v for k, v in second.aliases.items()})

    def start(ins, outs, sems):
        first.start(ins[:n_in], outs[:n_out], sems[:n_sc])
        second.start(ins[n_in:], outs[n_out:], sems[n_sc:])

    def finish(ins, outs, sems):
        first.finish(ins[:n_in], outs[:n_out], sems[:n_sc])
        second.finish(ins[n_in:], outs[n_out:], sems[n_sc:])

    return _Comm([*first.operands, *second.operands], [*first.out_shapes, *second.out_shapes], aliases,
                 [*first.scratch, *second.scratch], start, finish)


def _place():
    return lax.axis_index("x"), lax.axis_index("y"), lax.axis_index("c")


def _other_chips(x, y):
    return [(1 - x, y), (x, 1 - y), (1 - x, 1 - y)]


def _dev_index(dev):
    return 4 * dev[0] + 2 * dev[1] + dev[2]


def _comm_call(name, comm):
    c_in, c_out = len(comm.operands), len(comm.out_shapes)

    def body(*refs):
        ins, outs, sems = refs[:c_in], refs[c_in:c_in + c_out], refs[c_in + c_out:]
        comm.start(ins, outs, sems)
        comm.finish(ins, outs, sems)

    return _pcall(body, name=name, in_specs=[ANY] * c_in, out_specs=[ANY] * c_out, out_shape=comm.out_shapes,
                  scratch_shapes=comm.scratch, input_output_aliases=dict(comm.aliases),
                  compiler_params=pltpu.CompilerParams(has_side_effects=True))(*comm.operands)


def _call(name, body, operands, in_specs, out_shapes, out_specs, scratch, grid, comm=None):
    if comm is None:
        return _pcall(body, name=name, grid=grid, in_specs=in_specs, out_specs=out_specs, out_shape=out_shapes,
                      scratch_shapes=scratch, compiler_params=_params(len(grid)))(*operands)
    n_in, n_out, n_sc = len(operands), len(out_shapes), len(scratch)
    c_in, c_out = len(comm.operands), len(comm.out_shapes)

    def hosted(*refs):
        ins, cins = refs[:n_in], refs[n_in:n_in + c_in]
        o0 = n_in + c_in
        outs, couts = refs[o0:o0 + n_out], refs[o0 + n_out:o0 + n_out + c_out]
        rest = refs[o0 + n_out + c_out:]
        sc, csems = rest[:n_sc], rest[n_sc:]
        first = pl.program_id(0) == 0
        last = pl.program_id(0) == grid[0] - 1
        for d in range(1, len(grid)):
            first = jnp.logical_and(first, pl.program_id(d) == 0)
            last = jnp.logical_and(last, pl.program_id(d) == grid[d] - 1)

        @pl.when(first)
        def _():
            comm.start(cins, couts, csems)

        body(*ins, *outs, *sc)

        @pl.when(last)
        def _():
            comm.finish(cins, couts, csems)

    return _pcall(hosted, name=name, grid=grid, in_specs=[*in_specs, *[ANY] * c_in], out_specs=[*out_specs, *[ANY] * c_out],
                  out_shape=[*out_shapes, *comm.out_shapes], scratch_shapes=[*scratch, *comm.scratch],
                  input_output_aliases={n_in + k: n_out + v for k, v in comm.aliases.items()},
                  compiler_params=_params(len(grid), side_effects=True))(*operands, *comm.operands)


def _gather_level1(shards):
    n = len(shards)

    def copies(ins, outs, sems):
        send_sems, recv_sems, local_sems = sems
        x, y, c = _place()
        me, sibling = (x, y, c), (x, y, 1 - c)
        chips = _other_chips(x, y)

        def copy(a, k, block, to, src=None):
            slot = outs[a].at[_dev_index(block)]
            return pltpu.make_async_remote_copy(src_ref=slot if src is None else src, dst_ref=slot, send_sem=send_sems.at[a, k],
                                                recv_sem=recv_sems.at[a, k], device_id=to, device_id_type=MESH)

        mine = [pltpu.make_async_copy(ins[a], outs[a].at[_dev_index(me)], local_sems.at[a]) for a in range(n)]
        sends = [copy(a, 1 + j, me, (*chip, c), src=ins[a]) for j, chip in enumerate(chips) for a in range(n)]
        sends += [copy(a, 0, me, sibling, src=ins[a]) for a in range(n)]
        recvs = [copy(a, 1 + j, (*chip, c), me) for j, chip in enumerate(chips) for a in range(n)]
        recvs += [copy(a, 0, sibling, me) for a in range(n)]
        return mine, sends, recvs

    def start(ins, outs, sems):
        mine, sends, _ = copies(ins, outs, sems)
        for cp in mine + sends:
            cp.start()

    def finish(ins, outs, sems):
        mine, sends, recvs = copies(ins, outs, sems)
        for cp in recvs:
            cp.wait_recv()
        for cp in sends:
            cp.wait_send()
        for cp in mine:
            cp.wait()

    return _Comm(shards, [_sds((N_DEV, *a.shape), a.dtype) for a in shards], {},
                 [pltpu.SemaphoreType.DMA((n, 4)), pltpu.SemaphoreType.DMA((n, 4)), pltpu.SemaphoreType.DMA((n,))], start, finish)


def _gather_level2(bufs):
    n = len(bufs)

    def copies(outs, sems):
        send_sems, recv_sems = sems
        x, y, c = _place()
        sibling = (x, y, 1 - c)
        sends, recvs = [], []
        for j, chip in enumerate(_other_chips(x, y)):
            for a in range(n):
                have, want = outs[a].at[_dev_index((*chip, c))], outs[a].at[_dev_index((*chip, 1 - c))]
                sends.append(pltpu.make_async_remote_copy(src_ref=have, dst_ref=have, send_sem=send_sems.at[a, j], recv_sem=recv_sems.at[a, j],
                                                          device_id=sibling, device_id_type=MESH))
                recvs.append(pltpu.make_async_remote_copy(src_ref=want, dst_ref=want, send_sem=send_sems.at[a, j], recv_sem=recv_sems.at[a, j],
                                                          device_id=sibling, device_id_type=MESH))
        return sends, recvs

    def start(ins, outs, sems):
        for cp in copies(outs, sems)[0]:
            cp.start()

    def finish(ins, outs, sems):
        sends, recvs = copies(outs, sems)
        for cp in recvs:
            cp.wait_recv()
        for cp in sends:
            cp.wait_send()

    return _Comm(bufs, [_sds(b.shape, b.dtype) for b in bufs], {a: a for a in range(n)},
                 [pltpu.SemaphoreType.DMA((n, 3)), pltpu.SemaphoreType.DMA((n, 3))], start, finish)


def _all_gather(name, arrays):
    n = len(arrays)

    def body(*refs):
        ins = refs[:n]
        outs = refs[n:2 * n]
        send_sems, recv_sems, local_sems = refs[2 * n:]
        x, y, c = _place()
        me, sibling = (x, y, c), (x, y, 1 - c)
        chips = _other_chips(x, y)

        def copy(a, k, block, to, src=None):
            slot = outs[a].at[_dev_index(block)]
            return pltpu.make_async_remote_copy(src_ref=slot if src is None else src, dst_ref=slot, send_sem=send_sems.at[a, k],
                                                recv_sem=recv_sems.at[a, k], device_id=to, device_id_type=MESH)

        mine = [pltpu.make_async_copy(ins[a], outs[a].at[_dev_index(me)], local_sems.at[a]) for a in range(n)]
        for cp in mine:
            cp.start()
        first = []
        for j, chip in enumerate(chips):
            first += [copy(a, 1 + j, me, (*chip, c), src=ins[a]) for a in range(n)]
        first += [copy(a, 0, me, sibling, src=ins[a]) for a in range(n)]
        for cp in first:
            cp.start()
        passed = []
        for j, chip in enumerate(chips):
            for a in range(n):
                copy(a, 1 + j, (*chip, c), me).wait_recv()
                fwd = copy(a, 4 + j, (*chip, c), sibling)
                fwd.start()
                passed.append(fwd)
        for a in range(n):
            copy(a, 0, sibling, me).wait_recv()
            for j, chip in enumerate(chips):
                copy(a, 4 + j, (*chip, 1 - c), me).wait_recv()
        for cp in first + passed:
            cp.wait_send()
        for cp in mine:
            cp.wait()

    return _pcall(
        body, name=name, in_specs=[ANY] * n, out_specs=[ANY] * n,
        out_shape=[_sds((N_DEV, *a.shape), a.dtype) for a in arrays],
        scratch_shapes=[pltpu.SemaphoreType.DMA((n, 7)), pltpu.SemaphoreType.DMA((n, 7)), pltpu.SemaphoreType.DMA((n,))],
        compiler_params=pltpu.CompilerParams(has_side_effects=True),
    )(*arrays)


def _sibling_exchange(grads):
    n = len(grads)

    def start(ins, outs, sems):
        send_sems, recv_sems = sems
        x, y, c = _place()
        for a in range(n):
            for ch in range(N_CHIP):
                pltpu.make_async_remote_copy(src_ref=ins[a].at[2 * ch + 1 - c], dst_ref=outs[a].at[ch], send_sem=send_sems.at[a],
                                             recv_sem=recv_sems.at[a], device_id=(x, y, 1 - c), device_id_type=MESH).start()

    def finish(ins, outs, sems):
        send_sems, recv_sems = sems
        x, y, c = _place()
        for a in range(n):
            pltpu.make_async_remote_copy(src_ref=outs[a], dst_ref=outs[a], send_sem=send_sems.at[a], recv_sem=recv_sems.at[a],
                                         device_id=(x, y, 1 - c), device_id_type=MESH).wait()

    return _Comm(grads, [_sds((N_CHIP, *g.shape[1:]), g.dtype) for g in grads], {},
                 [pltpu.SemaphoreType.DMA((n,)), pltpu.SemaphoreType.DMA((n,))], start, finish)


def _chip_exchange(parts, slots, layers, stacked):
    n = len(parts)
    names = []
    for nm, _ in slots:
        if nm not in names:
            names.append(nm)
    shapes = {nm: _sds((N_CHIP, layers[nm], *parts[a].shape[1:]), parts[a].dtype) for a, (nm, _) in enumerate(slots)}
    kept = [nm for nm in names if stacked.get(nm) is not None]
    aliases = {n + k: names.index(nm) for k, nm in enumerate(kept)}

    def copies(ins, outs, sems):
        send_sems, recv_sems, local_sems = sems
        x, y, c = _place()
        mine = 2 * x + y
        local, sends, recvs = [], [], []
        for a, (nm, l) in enumerate(slots):
            buf = outs[names.index(nm)]
            local.append(pltpu.make_async_copy(ins[a].at[mine], buf.at[mine, l], local_sems.at[a]))
            for j, chip in enumerate(_other_chips(x, y)):
                theirs = buf.at[2 * chip[0] + chip[1], l]
                sends.append(pltpu.make_async_remote_copy(src_ref=ins[a].at[2 * chip[0] + chip[1]], dst_ref=buf.at[mine, l], send_sem=send_sems.at[a, j],
                                                          recv_sem=recv_sems.at[a, j], device_id=(*chip, c), device_id_type=MESH))
                recvs.append(pltpu.make_async_remote_copy(src_ref=theirs, dst_ref=theirs, send_sem=send_sems.at[a, j],
                                                          recv_sem=recv_sems.at[a, j], device_id=(*chip, c), device_id_type=MESH))
        return local, sends, recvs

    def start(ins, outs, sems):
        local, sends, _ = copies(ins, outs, sems)
        for cp in local + sends:
            cp.start()

    def finish(ins, outs, sems):
        local, sends, recvs = copies(ins, outs, sems)
        for cp in recvs:
            cp.wait_recv()
        for cp in sends:
            cp.wait_send()
        for cp in local:
            cp.wait()

    comm = _Comm([*parts, *[stacked[nm] for nm in kept]], [shapes[nm] for nm in names], aliases,
                 [pltpu.SemaphoreType.DMA((n, 3)), pltpu.SemaphoreType.DMA((n, 3)), pltpu.SemaphoreType.DMA((n,))], start, finish)
    return comm, names


def _matmul(name, a, b, extras, *, grid, a_spec, b_spec, extra_specs, out_shapes, out_specs, dims, k_axis=None, nk=1,
            acc_shape=None, epilogue=None, comm=None, n_sum=0, write=None):
    n_extra = len(extras)
    n_out = len(out_shapes)

    def body(*refs):
        a_ref, b_ref = refs[0], refs[1]
        ex = refs[2:2 + n_extra]
        outs = refs[2 + n_extra:2 + n_extra + n_out]
        prod = lax.dot_general(a_ref[...], b_ref[...], dims, preferred_element_type=F32)

        def finish(acc):
            if write is not None:
                write(outs, acc, *[e[...] for e in ex])
                return
            res = epilogue(acc, *[e[...] for e in ex]) if epilogue is not None else (acc,)
            first = None
            for d in range(len(grid)):
                if d != k_axis:
                    here = pl.program_id(d) == 0
                    first = here if first is None else jnp.logical_and(first, here)
            for idx, (o, r) in enumerate(zip(outs, res)):
                if idx < n_out - n_sum:
                    o[...] = r.astype(o.dtype)
                else:
                    @pl.when(first)
                    def _(o=o, r=r):
                        o[...] = r.astype(o.dtype)

                    @pl.when(jnp.logical_not(first))
                    def _(o=o, r=r):
                        o[...] += r.astype(o.dtype)

        if k_axis is None:
            finish(prod)
        else:
            acc_ref = refs[-1]
            k = pl.program_id(k_axis)

            @pl.when(k == 0)
            def _():
                acc_ref[...] = prod

            @pl.when(k > 0)
            def _():
                acc_ref[...] += prod

            @pl.when(k == nk - 1)
            def _():
                finish(acc_ref[...])

    scratch = [] if k_axis is None else [pltpu.VMEM(acc_shape, F32)]
    return _call(name, body, [a, b, *extras], [a_spec, b_spec, *extra_specs], list(out_shapes), list(out_specs), scratch, grid, comm)


def _rowwise(name, fn, operands, *, grid, in_specs, out_shapes, out_specs, n_acc=0, grid_spec_prefetch=None, comm=None):
    n_in = len(operands)
    n_out = len(out_shapes)
    n_pre = 0 if grid_spec_prefetch is None else 1

    def body(*refs):
        refs = refs[n_pre:]
        ins = refs[:n_in]
        outs = refs[n_in:n_in + n_out]
        res = fn(*[r[...] for r in ins])
        if not isinstance(res, (tuple, list)):
            res = (res,)
        first = pl.program_id(0) == 0
        for d in range(1, len(grid)):
            first = jnp.logical_and(first, pl.program_id(d) == 0)
        for idx, (o, r) in enumerate(zip(outs, res)):
            if idx < n_out - n_acc:
                o[...] = r.astype(o.dtype)
            else:
                @pl.when(first)
                def _(o=o, r=r):
                    o[...] = r.astype(o.dtype)

                @pl.when(jnp.logical_not(first))
                def _(o=o, r=r):
                    o[...] += r.astype(o.dtype)

    if comm is not None:
        return _call(name, body, list(operands), list(in_specs), list(out_shapes), list(out_specs), [], grid, comm)
    if grid_spec_prefetch is None:
        return _pcall(body, name=name, grid=grid, in_specs=in_specs, out_specs=out_specs, out_shape=out_shapes,
                      compiler_params=_params(len(grid)))(*operands)
    gs = pltpu.PrefetchScalarGridSpec(num_scalar_prefetch=1, grid=grid, in_specs=in_specs, out_specs=out_specs)
    return _pcall(body, name=name, grid_spec=gs, out_shape=out_shapes,
                  compiler_params=_params(len(grid)))(grid_spec_prefetch, *operands)


def _row_spec(tm, w):
    return pl.BlockSpec((tm, w), lambda i: (i, 0))


def _const_spec(shape):
    nd = len(shape)
    return pl.BlockSpec(tuple(shape), lambda *_: (0,) * nd)


def _rms_fwd(x, g):
    r = lax.rsqrt(jnp.mean(x * x, axis=-1, keepdims=True) + NORM_EPS)
    return x * r * g


def _rms_bwd(x, g, dy):
    r = lax.rsqrt(jnp.mean(x * x, axis=-1, keepdims=True) + NORM_EPS)
    xh = x * r
    u = dy * g
    dx = r * (u - xh * jnp.mean(u * xh, axis=-1, keepdims=True))
    dg = jnp.sum(dy * xh, axis=0, keepdims=True)
    return dx, dg


def _gelu_and_grad(z):
    cdf = 0.5 * (1.0 + lax.erf(z * (2.0 ** -0.5)))
    return cdf + z * jnp.exp(-0.5 * z * z) * ((2.0 * math.pi) ** -0.5), z * cdf


def _rope_fwd(x, cc, sa, sb):
    return x * cc + pltpu.roll(x, 96, 1) * sa + pltpu.roll(x, 32, 1) * sb


def _rope_bwd(d, cc, sa, sb):
    return d * cc + pltpu.roll(d * sa, 32, 1) + pltpu.roll(d * sb, 96, 1)


def _adam(w, g, m, v):
    m = ADAM_B1 * m + (1.0 - ADAM_B1) * g
    v = ADAM_B2 * v + (1.0 - ADAM_B2) * (g * g)
    m_hat = m / (1.0 - ADAM_B1 ** ADAM_STEP)
    v_hat = v / (1.0 - ADAM_B2 ** ADAM_STEP)
    delta = -ADAM_LR * (m_hat / (jnp.sqrt(v_hat) + ADAM_EPS) + ADAM_WD * w)
    return delta, m, v


def _flash_fwd(q, k, vt, tq, comm=None):
    h, t = vt.shape[0], q.shape[0]
    nq = t // tq

    chunk_blocks = [c for c in (4, 2) if c < nq]

    def body(q_ref, k_ref, vt_ref, o_ref, lse_ref, m_ref, l_ref, acc_ref):
        qi = pl.program_id(1)
        m_ref[...] = jnp.full((1, tq), NEG, F32)
        l_ref[...] = jnp.zeros((1, tq), F32)
        acc_ref[...] = jnp.zeros((VDIM, tq), F32)

        def update(kb0, nblk, masked):
            kb = k_ref[pl.ds(pl.multiple_of(kb0 * tq, tq), nblk * tq), :]
            st = lax.dot_general(kb, q_ref[...], NT, preferred_element_type=F32)
            if masked:
                key = lax.broadcasted_iota(jnp.int32, (nblk * tq, tq), 0) - (nblk - 1) * tq
                qry = lax.broadcasted_iota(jnp.int32, (nblk * tq, tq), 1)
                st = jnp.where(key <= qry, st, NEG)
            m_old = m_ref[...]
            m_new = jnp.maximum(m_old, jnp.max(st, axis=0, keepdims=True))
            alpha = jnp.exp2((m_old - m_new) * EXP2_SCALE)
            pt = jnp.exp2((st - m_new) * EXP2_SCALE)
            l_ref[...] = alpha * l_ref[...] + jnp.sum(pt, axis=0, keepdims=True)
            ptb = pt.astype(BF16)
            pv = lax.dot_general(vt_ref[kb0], ptb[:tq], NN, preferred_element_type=F32)
            for j in range(1, nblk):
                pv += lax.dot_general(vt_ref[kb0 + j], ptb[j * tq:(j + 1) * tq], NN, preferred_element_type=F32)
            acc_ref[...] = alpha * acc_ref[...] + pv
            m_ref[...] = m_new

        start = jnp.int32(0)
        for c in chunk_blocks:
            take = (qi & c) != 0

            @pl.when(take)
            def _(start=start, c=c):
                update(start, c, False)

            start = start + jnp.where(take, c, 0)
        if nq > 1:
            @pl.when((qi & 1) != 0)
            def _():
                update(qi - 1, 2, True)

            @pl.when((qi & 1) == 0)
            def _():
                update(qi, 1, True)
        else:
            update(qi, 1, True)
        l = l_ref[...]
        o_ref[...] = (acc_ref[...] / l).T.astype(o_ref.dtype)
        lse_ref[...] = m_ref[...] * EXP2_SCALE + jnp.log2(l)

    return _call(
        "flash_fwd", body, [q, k, vt],
        [pl.BlockSpec((tq, QPAD), lambda hh, i: (i, hh)),
         pl.BlockSpec((t, QPAD), lambda hh, i: (0, hh)),
         pl.BlockSpec((None, nq, VDIM, tq), lambda hh, i: (hh, 0, 0, 0))],
        [_sds((t, h * VDIM), BF16), _sds((h, nq, 1, tq), F32)],
        [pl.BlockSpec((tq, VDIM), lambda hh, i: (i, hh)),
         pl.BlockSpec((None, None, 1, tq), lambda hh, i: (hh, i, 0, 0))],
        [pltpu.VMEM((1, tq), F32), pltpu.VMEM((1, tq), F32), pltpu.VMEM((VDIM, tq), F32)], (h, nq), comm)


def _flash_bwd(q, k, v, o, do, lse, tabs, tq, comm=None):
    t = q.shape[0]
    h = q.shape[1] // QPAD
    nq = t // tq

    def body(q_ref, k_ref, v_ref, o_ref, do_ref, lse_ref, cc_ref, sa_ref, sb_ref, dq_ref, dk_out, dv_out, delta_ref, dqt_ref, dk_ref, dv_ref):
        kj = pl.program_id(1)

        @pl.when(kj == 0)
        def _():
            dqt_ref[...] = jnp.zeros_like(dqt_ref)
            ones = jnp.ones((8, VDIM), BF16)
            for qi in range(nq):
                rows = pl.ds(qi * tq, tq)
                prod = do_ref[rows, :].astype(F32) * o_ref[rows, :].astype(F32)
                hi = prod.astype(BF16)
                lo = (prod - hi.astype(F32)).astype(BF16)
                delta_ref[qi] = (lax.dot_general(ones, hi, NT, preferred_element_type=F32)
                                 + lax.dot_general(ones, lo, NT, preferred_element_type=F32))

        kb = k_ref[...]
        vb = v_ref[...]
        kbt = kb.astype(F32).T.astype(BF16)
        dk_ref[...] = jnp.zeros_like(dk_ref)
        dv_ref[...] = jnp.zeros_like(dv_ref)

        def step(q0, nblk, masked):
            rows = pl.ds(pl.multiple_of(q0 * tq, tq), nblk * tq)
            qb = q_ref[rows, :]
            dob = do_ref[rows, :]
            lse = jnp.concatenate([lse_ref[q0 + j] for j in range(nblk)], axis=1)
            delta = jnp.concatenate([delta_ref[q0 + j, pl.ds(0, 1), :] for j in range(nblk)], axis=1)
            st = lax.dot_general(kb, qb, NT, preferred_element_type=F32)
            pt = jnp.exp2(st * EXP2_SCALE - lse)
            if masked:
                key = lax.broadcasted_iota(jnp.int32, (tq, nblk * tq), 0)
                qry = lax.broadcasted_iota(jnp.int32, (tq, nblk * tq), 1)
                pt = jnp.where(key <= qry, pt, 0.0)
            dv_ref[...] += lax.dot_general(pt.astype(BF16), dob, NN, preferred_element_type=F32)
            dpt = lax.dot_general(vb, dob, NT, preferred_element_type=F32)
            dst = (pt * (dpt - delta) * ATTN_SCALE).astype(BF16)
            dk_ref[...] += lax.dot_general(dst, qb, NN, preferred_element_type=F32)
            dqt = lax.dot_general(kbt, dst, NN, preferred_element_type=F32)
            for j in range(nblk):
                dqt_ref[q0 + j] += dqt[:, j * tq:(j + 1) * tq]

        later = nq - 1 - kj
        if nq > 1:
            @pl.when((later & 1) != 0)
            def _():
                step(kj, 2, True)

            @pl.when((later & 1) == 0)
            def _():
                step(kj, 1, True)
        else:
            step(kj, 1, True)
        start = kj + 1 + (later & 1)
        for c in [c for c in (2, 4) if c < nq]:
            take = (later & c) != 0

            @pl.when(take)
            def _(start=start, c=c):
                step(start, c, False)

            start = start + jnp.where(take, c, 0)
        dk_out[...] = dk_ref[...].astype(BF16)
        dv_out[...] = dv_ref[...].astype(BF16)

        @pl.when(kj == nq - 1)
        def _():
            for qi in range(nq):
                rows = pl.ds(qi * tq, tq)
                d = dqt_ref[qi].T
                roped = _rope_bwd(d[:, NOPE:], cc_ref[rows, :], sa_ref[rows, :], sb_ref[rows, :])
                dq_ref[rows, :] = jnp.concatenate([d[:, :NOPE], roped], axis=1).astype(BF16)

    head_q = pl.BlockSpec((t, QPAD), lambda hh, j: (0, hh))
    head_v = pl.BlockSpec((t, VDIM), lambda hh, j: (0, hh))
    table = pl.BlockSpec((t, 128), lambda hh, j: (0, 0))
    return _call(
        "flash_bwd", body, [q, k, v, o, do, lse, *tabs],
        [head_q, pl.BlockSpec((tq, QPAD), lambda hh, j: (j, hh)), pl.BlockSpec((tq, VDIM), lambda hh, j: (j, hh)), head_v, head_v,
         pl.BlockSpec((None, nq, 1, tq), lambda hh, j: (hh, 0, 0, 0)), table, table, table],
        [_sds((t, h * QPAD), BF16), _sds((t, h * QPAD), BF16), _sds((t, h * VDIM), BF16)],
        [head_q, pl.BlockSpec((tq, QPAD), lambda hh, j: (j, hh)), pl.BlockSpec((tq, VDIM), lambda hh, j: (j, hh))],
        [pltpu.VMEM((nq, 8, tq), F32), pltpu.VMEM((nq, QPAD, tq), F32), pltpu.VMEM((tq, QPAD), F32), pltpu.VMEM((tq, VDIM), F32)], (h, nq), comm)


def _tril_bf16(w):
    row = lax.broadcasted_iota(jnp.int32, w.shape, 0)
    col = lax.broadcasted_iota(jnp.int32, w.shape, 1)
    return jnp.where(col <= row, w, 0.0).astype(BF16)


def _layer_norm_parts(v0):
    mu = jnp.mean(v0, axis=-1, keepdims=True)
    vc = v0 - mu
    rstd = lax.rsqrt(jnp.mean(vc * vc, axis=-1, keepdims=True) + LN_EPS)
    return vc * rstd, rstd


def _sgu_mid_fwd(ge, ln_g, ln_b, w_sp, b_sp, chunks_per_step):
    t, e2 = ge.shape
    e = e2 // 2
    gd = e // SGU_GROUPS
    rows = SGU_CHUNK * chunks_per_step

    def body(u_ref, v_ref, g_ref, b_ref, w_ref, bs_ref, gate_ref):
        for ck in range(chunks_per_step):
            r = pl.ds(ck * SGU_CHUNK, SGU_CHUNK)
            xh, _ = _layer_norm_parts(v_ref[r, :].astype(F32))
            v1 = (xh * g_ref[...] + b_ref[...]).astype(BF16)
            for g in range(SGU_GROUPS):
                cols = pl.ds(g * gd, gd)
                mixed = lax.dot_general(_tril_bf16(w_ref[g]), v1[:, g * gd:(g + 1) * gd], NN, preferred_element_type=F32) + bs_ref[g]
                gate_ref[r, cols] = (u_ref[r, cols].astype(F32) * mixed).astype(BF16)

    return _pcall(
        body, name="sgu_mid_fwd", grid=(t // rows,),
        in_specs=[pl.BlockSpec((rows, e), lambda i: (i, 0)), pl.BlockSpec((rows, e), lambda i: (i, 1)),
                  _const_spec((1, e)), _const_spec((1, e)), _const_spec(w_sp.shape), _const_spec(b_sp.shape)],
        out_specs=pl.BlockSpec((rows, e), lambda i: (i, 0)),
        out_shape=_sds((t, e), BF16), compiler_params=_params(1),
    )(ge, ge, ln_g, ln_b, w_sp, b_sp)


def _sgu_mid_bwd(ge, gp, dgate, ln_g, ln_b, w_sp, b_sp, chunks_per_step):
    t, e2 = ge.shape
    e = e2 // 2
    gd = e // SGU_GROUPS
    rows = SGU_CHUNK * chunks_per_step

    def body(u_ref, v_ref, zu_ref, zv_ref, dg_ref, g_ref, b_ref, w_ref, bs_ref, dz_ref, dw_ref, dbs_ref, dlg_ref, dlb_ref):
        @pl.when(pl.program_id(0) == 0)
        def _():
            dw_ref[...] = jnp.zeros_like(dw_ref)
            dbs_ref[...] = jnp.zeros_like(dbs_ref)
            dlg_ref[...] = jnp.zeros_like(dlg_ref)
            dlb_ref[...] = jnp.zeros_like(dlb_ref)

        for ck in range(chunks_per_step):
            r = pl.ds(ck * SGU_CHUNK, SGU_CHUNK)
            xh, rstd = _layer_norm_parts(v_ref[r, :].astype(F32))
            v1 = (xh * g_ref[...] + b_ref[...]).astype(BF16)
            dv1_parts = []
            for g in range(SGU_GROUPS):
                cols = pl.ds(g * gd, gd)
                wc = _tril_bf16(w_ref[g])
                v1g = v1[:, g * gd:(g + 1) * gd]
                mixed = lax.dot_general(wc, v1g, NN, preferred_element_type=F32) + bs_ref[g]
                dgate = dg_ref[r, cols].astype(F32)
                dmixed = dgate * u_ref[r, cols].astype(F32)
                du = dgate * mixed
                dz_ref[r, cols] = (du * zu_ref[r, cols].astype(F32)).astype(BF16)
                dbs_ref[g] += jnp.sum(dmixed, axis=1, keepdims=True)
                dmb = dmixed.astype(BF16)
                dwg = lax.dot_general(dmb, v1g, NT, preferred_element_type=F32)
                row = lax.broadcasted_iota(jnp.int32, dwg.shape, 0)
                col = lax.broadcasted_iota(jnp.int32, dwg.shape, 1)
                dw_ref[g] += jnp.where(col <= row, dwg, 0.0)
                dv1_parts.append(lax.dot_general(wc, dmb, TN, preferred_element_type=F32))
            dv1 = jnp.concatenate(dv1_parts, axis=1)
            dlg_ref[...] += jnp.sum(dv1 * xh, axis=0, keepdims=True)
            dlb_ref[...] += jnp.sum(dv1, axis=0, keepdims=True)
            dxh = dv1 * g_ref[...]
            dv0 = rstd * (dxh - jnp.mean(dxh, axis=-1, keepdims=True) - xh * jnp.mean(dxh * xh, axis=-1, keepdims=True))
            dz_ref[r, pl.ds(e, e)] = (dv0 * zv_ref[r, :].astype(F32)).astype(BF16)

    half0 = pl.BlockSpec((rows, e), lambda i: (i, 0))
    half1 = pl.BlockSpec((rows, e), lambda i: (i, 1))
    return _pcall(
        body, name="sgu_mid_bwd", grid=(t // rows,),
        in_specs=[half0, half1, half0, half1, half0, _const_spec((1, e)), _const_spec((1, e)), _const_spec(w_sp.shape), _const_spec(b_sp.shape)],
        out_specs=[pl.BlockSpec((rows, e2), lambda i: (i, 0)), _const_spec(w_sp.shape), _const_spec(b_sp.shape), _const_spec((1, e)), _const_spec((1, e))],
        out_shape=[_sds((t, e2), BF16), _sds(w_sp.shape, F32), _sds(b_sp.shape, F32), _sds((1, e), F32), _sds((1, e), F32)],
        compiler_params=_params(1),
    )(ge, ge, gp, gp, dgate, ln_g, ln_b, w_sp, b_sp)


def kernel(x, positions, norm_mix, norm_ffn, final_norm, mla_w_dkv, mla_q_norm, mla_kv_norm, mla_w_uq, mla_w_ukv, mla_w_o, sgu_w_in, sgu_ln_g, sgu_ln_b, sgu_w_spatial, sgu_b_spatial, sgu_w_out, ffn_w_up, ffn_w_down, loss_target, m_norm_mix, m_norm_ffn, m_final_norm, m_mla_w_dkv, m_mla_q_norm, m_mla_kv_norm, m_mla_w_uq, m_mla_w_ukv, m_mla_w_o, m_sgu_w_in, m_sgu_ln_g, m_sgu_ln_b, m_sgu_w_spatial, m_sgu_b_spatial, m_sgu_w_out, m_ffn_w_up, m_ffn_w_down, v_norm_mix, v_norm_ffn, v_final_norm, v_mla_w_dkv, v_mla_q_norm, v_mla_kv_norm, v_mla_w_uq, v_mla_w_ukv, v_mla_w_o, v_sgu_w_in, v_sgu_ln_g, v_sgu_ln_b, v_sgu_w_spatial, v_sgu_b_spatial, v_sgu_w_out, v_ffn_w_up, v_ffn_w_down):
    _, T, D = x.shape
    depth = norm_mix.shape[0]
    n_mla, n_sgu = mla_w_dkv.shape[0], sgu_w_in.shape[0]
    assert depth % 2 == 0
    FF = ffn_w_up.shape[2] * N_DEV
    E = sgu_w_out.shape[1] * N_DEV
    ffc, ec, e2c = FF // N_DEV, E // N_DEV, 2 * E // N_DEV
    dc = D // N_DEV
    OW = HEADS * VDIM
    HW = HEADS * QPAD
    owc = OW // N_DEV
    tm = _tile(T, 1024)
    tb = _tile(T, 4096)
    tk = _tile(T, 512)
    tq = _tile(T, 512)
    ts = _tile(T, 256)
    nt = T // tm
    x2 = x.reshape(T, D)
    tgt = loss_target.reshape(T, D)
    cidx = lax.axis_index("c").astype(jnp.int32).reshape(1)

    ln_local = jnp.concatenate([sgu_ln_g, sgu_ln_b, jnp.zeros((8 - 2 * n_sgu, ec), F32)], axis=0)
    mla_sh = [[w[l].astype(BF16) for w in (mla_w_dkv, mla_w_uq, mla_w_ukv, mla_w_o)] for l in range(n_mla)]

    def mla_layouts(g_dkv, g_uq, g_ukv, g_o):
        w_dkv = jnp.pad(g_dkv.reshape(1, D, LAT), ((0, 0), (0, 0), (0, LAT_PAD - LAT)))
        w_uq = jnp.pad(g_uq, ((0, 0), (0, 0), (0, QPAD - NOPE - ROPE))).transpose(1, 0, 2).reshape(1, Q_RANK, HEADS * QPAD)
        w_ukv = g_ukv.transpose(1, 0, 2).reshape(1, KV_RANK, HEADS * (NOPE + VDIM))
        return w_dkv, w_uq, w_ukv, g_o.reshape(1, HEADS * VDIM, D)

    mla_w = [None] * n_mla
    small_later = [a for l in range(1, n_mla) for a in mla_sh[l]] + [ln_local]
    ln_g_full, ln_b_full = [None] * n_sgu, [None] * n_sgu
    b_sp = sgu_b_spatial.reshape(n_sgu, SGU_GROUPS, SGU_CHUNK, 1)
    up_sh = [ffn_w_up[i].astype(BF16) for i in range(depth)]
    down_sh = [ffn_w_down[i].astype(BF16) for i in range(depth)]
    in_sh = [sgu_w_in[l].astype(BF16) for l in range(n_sgu)]
    out_sh = [sgu_w_out[l].astype(BF16) for l in range(n_sgu)]
    g_up, g_down, g_in, g_out = [None] * depth, [None] * depth, [None] * n_sgu, [None] * n_sgu

    inv_freq = ROPE_THETA ** (-jnp.arange(0, ROPE, 2, dtype=F32) / ROPE)
    zeros32 = jnp.zeros((ROPE // 2,), F32)
    inv128 = jnp.concatenate([inv_freq, inv_freq, zeros32, zeros32]).reshape(1, 128)
    sel_a = jnp.concatenate([-jnp.ones((32,), F32), zeros32, zeros32, zeros32]).reshape(1, 128)
    sel_b = jnp.concatenate([zeros32, jnp.ones((32,), F32), zeros32, zeros32]).reshape(1, 128)
    sel_c = jnp.concatenate([jnp.ones((64,), F32), zeros32, zeros32]).reshape(1, 128)

    def rope_tables(pos, inv, sa, sb, sc):
        ang = pos.astype(F32) * inv
        cs, sn = jnp.cos(ang), jnp.sin(ang)
        return cs * sc, sn * sa, sn * sb

    t_cc, t_sa, t_sb, *first_half = _rowwise(
        "rope_tables", rope_tables, [positions.reshape(T, 1), inv128, sel_a, sel_b, sel_c], grid=(nt,),
        in_specs=[_row_spec(tm, 1)] + [_const_spec((1, 128))] * 4,
        out_shapes=[_sds((T, 128), F32)] * 3, out_specs=[_row_spec(tm, 128)] * 3, comm=_gather_level1(mla_sh[0]))
    tab_specs = [_row_spec(tm, 128)] * 3

    def rmsnorm(xv, g, comm):
        return _rowwise("rmsnorm", lambda a, gg: _rms_fwd(a, gg), [xv, g.reshape(1, D)], grid=(nt,),
                        in_specs=[_row_spec(tm, D), _const_spec((1, D))], out_shapes=[_sds((T, D), BF16)], out_specs=[_row_spec(tm, D)], comm=comm)

    def proj_cols(name, h, gw, nc, epilogue, n_out, comm=None):
        return _matmul(name, h, gw, [], grid=(N_DEV, T // tb),
                       a_spec=pl.BlockSpec((tb, D), lambda j, i: (i, 0)),
                       b_spec=pl.BlockSpec((None, D, nc), lambda j, i: (j, 0, 0)), extra_specs=[],
                       out_shapes=[_sds((T, nc * N_DEV), BF16)] * n_out, out_specs=[pl.BlockSpec((tb, nc), lambda j, i: (i, j))] * n_out,
                       dims=NN, epilogue=epilogue, comm=comm)

    def residual_norm(acc, xr, g):
        xn = acc + xr
        return xn, _rms_fwd(xn, g)

    def proj_rows_residual(name, a, gw, xres, g_next, comm=None):
        kk_ = a.shape[1]
        return _matmul(name, a, gw.reshape(kk_, D), [xres, g_next.reshape(1, D)], grid=(T // tk,),
                       a_spec=_row_spec(tk, kk_), b_spec=_const_spec((kk_, D)), extra_specs=[_row_spec(tk, D), _const_spec((1, D))],
                       out_shapes=[_sds((T, D), F32), _sds((T, D), BF16)], out_specs=[_row_spec(tk, D)] * 2,
                       dims=NN, epilogue=residual_norm, comm=comm)

    def back_rows(name, dy, gw, kc, extras, epilogue, comm=None):
        return _matmul(name, dy, gw, extras, grid=(N_DEV, T // tb),
                       a_spec=pl.BlockSpec((tb, D), lambda j, i: (i, 0)),
                       b_spec=pl.BlockSpec((None, kc, D), lambda j, i: (j, 0, 0)),
                       extra_specs=[pl.BlockSpec((tb, kc), lambda j, i: (i, j))] * len(extras),
                       out_shapes=[_sds((T, kc * N_DEV), BF16)], out_specs=[pl.BlockSpec((tb, kc), lambda j, i: (i, j))],
                       dims=NT, epilogue=epilogue, comm=comm)

    def norm_bwd_epilogue(dh, xv, g, dxi):
        dxn, dg = _rms_bwd(xv, g, dh)
        return dxi + dxn, dxi + dxn, dg

    def transposed(gw):
        return gw.transpose(0, 2, 1).reshape(gw.shape[0] * gw.shape[2], D)

    def back_cols(name, da, gwt, xv, g, dx_in, comm=None):
        n = da.shape[1]
        row = _row_spec(tk, D)
        return _matmul(name, da, gwt, [xv, g.reshape(1, D), dx_in], grid=(T // tk,),
                       a_spec=_row_spec(tk, n), b_spec=_const_spec((n, D)), extra_specs=[row, _const_spec((1, D)), row],
                       out_shapes=[_sds((T, D), F32), _sds((T, D), BF16), _sds((1, D), F32)], out_specs=[row, row, _const_spec((1, D))],
                       dims=NN, epilogue=norm_bwd_epilogue, n_sum=1, comm=comm)

    def token_sum(tt):
        return dict(k_axis=1, nk=T // tt) if T // tt > 1 else dict(k_axis=None)

    def wgrad_cols(name, h, da, nc):
        return _matmul(name, h, da, [], grid=(N_DEV, T // tb),
                       a_spec=pl.BlockSpec((tb, D), lambda j, t: (t, 0)), b_spec=pl.BlockSpec((tb, nc), lambda j, t: (t, j)),
                       extra_specs=[], out_shapes=[_sds((N_DEV, D, nc), BF16)],
                       out_specs=[pl.BlockSpec((None, D, nc), lambda j, t: (j, 0, 0))],
                       dims=TN, acc_shape=(D, nc), **token_sum(tb))[0]

    def wgrad_rows(name, a, dy, kc, ncols, tt, comm=None):
        res = _matmul(name, a, dy, [], grid=(a.shape[1] // kc, T // tt),
                      a_spec=pl.BlockSpec((tt, kc), lambda j, t: (t, j)), b_spec=pl.BlockSpec((tt, ncols), lambda j, t: (t, 0)),
                      extra_specs=[], out_shapes=[_sds((a.shape[1], ncols), BF16)],
                      out_specs=[pl.BlockSpec((kc, ncols), lambda j, t: (j, 0))],
                      dims=TN, acc_shape=(kc, ncols), comm=comm, **token_sum(tt))
        return res[0] if comm is None else res

    saved = []
    xs = x2
    for i in range(depth):
        l = i // 2
        if i == 0:
            h, *first_w = rmsnorm(xs, norm_mix[0], _gather_level2(first_half))
            mla_w[0] = mla_layouts(*first_w)
        if i % 2 == 0:
            w_dkv, w_uq, w_ukv, w_o = mla_w[l]
            lat = _matmul("mla_down", h, w_dkv, [], grid=(nt,), a_spec=_row_spec(tm, D),
                          b_spec=pl.BlockSpec((None, D, LAT_PAD), lambda i_: (0, 0, 0)), extra_specs=[],
                          out_shapes=[_sds((T, LAT_PAD), F32)], out_specs=[_row_spec(tm, LAT_PAD)], dims=NN)[0]

            def latent_post(la, qn, kvn, cc, sa, sb):
                cq = _rms_fwd(la[:, :Q_RANK], qn)
                ckv = _rms_fwd(la[:, Q_RANK:Q_RANK + KV_RANK], kvn)
                kr = _rope_fwd(la[:, Q_RANK + KV_RANK:], cc, sa, sb)
                return cq, ckv, kr

            cq, ckv, kr = _rowwise(
                "mla_latent", latent_post, [lat, mla_q_norm[l].reshape(1, Q_RANK), mla_kv_norm[l].reshape(1, KV_RANK), t_cc, t_sa, t_sb],
                grid=(nt,), in_specs=[_row_spec(tm, LAT_PAD), _const_spec((1, Q_RANK)), _const_spec((1, KV_RANK))] + tab_specs,
                out_shapes=[_sds((T, Q_RANK), BF16), _sds((T, KV_RANK), BF16), _sds((T, 128), BF16)],
                out_specs=[_row_spec(tm, Q_RANK), _row_spec(tm, KV_RANK), _row_spec(tm, 128)])

            def q_epilogue(acc, cc, sa, sb):
                parts = []
                for b in range(HEADS):
                    parts += [acc[:, b * QPAD:b * QPAD + NOPE], _rope_fwd(acc[:, b * QPAD + NOPE:(b + 1) * QPAD], cc, sa, sb)]
                return (jnp.concatenate(parts, axis=1),)

            q = _matmul("mla_q", cq, w_uq, [t_cc, t_sa, t_sb], grid=(nt,), a_spec=_row_spec(tm, Q_RANK),
                        b_spec=pl.BlockSpec((None, Q_RANK, HW), lambda i_: (0, 0, 0)), extra_specs=tab_specs,
                        out_shapes=[_sds((T, HW), BF16)], out_specs=[_row_spec(tm, HW)], dims=NN, epilogue=q_epilogue)[0]

            def kv_write(outs, acc, krb):
                k_ref, v_ref, vt_ref = outs
                for b in range(HEADS):
                    vb = acc[:, b * QPAD + NOPE:(b + 1) * QPAD]
                    k_ref[:, b * QPAD:b * QPAD + NOPE] = acc[:, b * QPAD:b * QPAD + NOPE].astype(BF16)
                    k_ref[:, b * QPAD + NOPE:(b + 1) * QPAD] = krb
                    v_ref[:, b * VDIM:(b + 1) * VDIM] = vb.astype(BF16)
                    vbt = vb.T.astype(BF16)
                    for u in range(tm // tq):
                        vt_ref[b, u] = vbt[:, u * tq:(u + 1) * tq]

            kk, vv, vt = _matmul("mla_kv", ckv, w_ukv, [kr], grid=(nt,), a_spec=_row_spec(tm, KV_RANK),
                                 b_spec=pl.BlockSpec((None, KV_RANK, HW), lambda i_: (0, 0, 0)), extra_specs=[_row_spec(tm, 128)],
                                 out_shapes=[_sds((T, HW), BF16), _sds((T, OW), BF16), _sds((HEADS, T // tq, VDIM, tq), BF16)],
                                 out_specs=[_row_spec(tm, HW), _row_spec(tm, OW), pl.BlockSpec((HEADS, tm // tq, VDIM, tq), lambda i_: (0, i_, 0, 0))],
                                 dims=NN, write=kv_write)
            group = [up_sh[i], down_sh[i], in_sh[l], out_sh[l]] + (small_later if i == 0 else [])
            o, lse, *bufs = _flash_fwd(q, kk, vt, tq, comm=_gather_level1(group))
            xm, h2, g_up[i], g_down[i] = _matmul(
                "mla_out", o, w_o, [xs, norm_ffn[i].reshape(1, D)], grid=(nt,), a_spec=_row_spec(tm, OW),
                b_spec=pl.BlockSpec((None, OW, D), lambda i_: (0, 0, 0)), extra_specs=[_row_spec(tm, D), _const_spec((1, D))],
                out_shapes=[_sds((T, D), F32), _sds((T, D), BF16)], out_specs=[_row_spec(tm, D)] * 2, dims=NN,
                epilogue=residual_norm, comm=_gather_level2(bufs[:2]))
            half_gathered = bufs[2:]
            mix_saved = (h, lat, cq, ckv, q, kk, vv, o, lse)
        else:
            gp, ge, g_down[i], up_half = proj_cols("sgu_in", h, g_in[l], e2c, _gelu_and_grad, 2,
                                                   comm=_merge_comm(_gather_level2([down_half]), _gather_level1([up_sh[i]])))
            gate = _sgu_mid_fwd(ge, ln_g_full[l], ln_b_full[l], sgu_w_spatial[l], b_sp[l], 4)
            xm, h2, g_up[i] = proj_rows_residual("sgu_out", gate, g_out[l], xs, norm_ffn[i], comm=_gather_level2([up_half]))
            mix_saved = (h, gp, ge, gate)
        r, s, *rest = proj_cols("ffn_up", h2, g_up[i], ffc, lambda acc: (jnp.maximum(acc, 0.0), jnp.square(jnp.maximum(acc, 0.0))), 2,
                                comm=_gather_level2(half_gathered) if i % 2 == 0 else None)
        if i % 2 == 0:
            g_in[l], g_out[l], *small_gathered = rest
        if i == 0:
            for l_ in range(1, n_mla):
                mla_w[l_] = mla_layouts(*small_gathered[4 * (l_ - 1):4 * l_])
            g_ln = small_gathered[-1]
            ln_g_full = [g_ln[:, l_, :].reshape(1, E) for l_ in range(n_sgu)]
            ln_b_full = [g_ln[:, n_sgu + l_, :].reshape(1, E) for l_ in range(n_sgu)]
        xo, h_next, *rest = proj_rows_residual("ffn_down", s, g_down[i], xm, norm_mix[i + 1] if i + 1 < depth else final_norm,
                                               comm=_gather_level1([down_sh[i + 1]]) if i % 2 == 0 else None)
        if i % 2 == 0:
            (down_half,) = rest
        saved.append((xs, xm, mix_saved, h2, r, s))
        xs, h = xo, h_next

    def loss_head(xv, tg, g):
        y = _rms_fwd(xv, g)
        err = y - tg
        part = 0.5 * jnp.sum(jnp.sum(err * err, axis=-1, keepdims=True), axis=0, keepdims=True) / D
        dx, dg = _rms_bwd(xv, g, err / D)
        return dx, dx, jnp.broadcast_to(part, (1, 128)), dg

    dx, dyb, loss_part, d_final = _rowwise(
        "loss_head", loss_head, [xs, tgt, final_norm.reshape(1, D)], grid=(nt,),
        in_specs=[_row_spec(tm, D), _row_spec(tm, D), _const_spec((1, D))],
        out_shapes=[_sds((T, D), F32), _sds((T, D), BF16), _sds((1, 128), F32), _sds((1, D), F32)],
        out_specs=[_row_spec(tm, D), _row_spec(tm, D), _const_spec((1, 128)), _const_spec((1, D))], n_acc=2)
    loss = lax.psum(loss_part[0, 0], ("x", "y", "c"))

    d_norm_mix, d_norm_ffn = [None] * depth, [None] * depth
    d_qn, d_kvn = [None] * n_mla, [None] * n_mla
    d_wsp, d_bsp, d_lng, d_lnb = [None] * n_sgu, [None] * n_sgu, [None] * n_sgu, [None] * n_sgu
    layers = {"dkv": n_mla, "uq": n_mla, "ukv": n_mla, "o": n_mla, "in": n_sgu, "out": n_sgu, "up": depth, "down": depth}
    stacked = {nm: None for nm in layers}
    pending = []
    summed = []

    def add_pairs(gs, rcvs):
        operands, in_specs, out_shapes, out_specs = [], [], [], []
        for g, rcv in zip(gs, rcvs):
            _, rws, cls = g.shape
            slab = pl.BlockSpec((None, rws, cls), lambda ch, cr: (ch, 0, 0))
            operands += [g.reshape(N_CHIP, 2, rws, cls), rcv]
            in_specs += [pl.BlockSpec((None, None, rws, cls), lambda ch, cr: (ch, cr[0], 0, 0)), slab]
            out_shapes.append(_sds(rcv.shape, BF16))
            out_specs.append(slab)

        def fn(*blocks):
            return tuple(blocks[2 * k].astype(F32) + blocks[2 * k + 1].astype(F32) for k in range(len(gs)))

        return _rowwise("grad_pair_sum", fn, operands, grid=(N_CHIP,), in_specs=in_specs, out_shapes=out_shapes, out_specs=out_specs,
                        grid_spec_prefetch=cidx)

    def sibling_comm():
        return _sibling_exchange([g for _, _, g in pending]) if pending else None

    def absorb(from_sibling):
        if pending:
            parts = add_pairs([g for _, _, g in pending], list(from_sibling))
            summed.extend((nm, l_, p) for (nm, l_, _), p in zip(pending, parts))
            pending.clear()

    def chip_comm():
        if pending:
            absorb(_comm_call("grad_sibling_exchange", sibling_comm()))
        comm, names = _chip_exchange([p for _, _, p in summed], [(nm, l_) for nm, l_, _ in summed], layers, stacked)
        summed.clear()
        return comm, names

    def rows128(a, rows):
        flat = a.reshape(-1, 128)
        return jnp.pad(flat, ((0, rows - flat.shape[0]), (0, 0)))

    def pad_to(n, mult):
        return -(-n // mult) * mult

    def packed(arrs, sizes):
        return jnp.concatenate([rows128(a, sz) for a, sz in zip(arrs, sizes)], axis=0)

    n_wsp, n_bsp, n_ln = sgu_w_spatial.size // 128, pad_to(sgu_b_spatial.size // 128, 8), pad_to(n_sgu * E // 128, 8)
    early_sizes = [n_wsp, pad_to(n_wsp + n_bsp, SMALL_ROWS) - n_wsp, n_ln, n_ln]
    early_rep = early_sizes[0] + early_sizes[1]
    gathered_early = None

    for i in reversed(range(depth)):
        l = i // 2
        xs_i, xm, mix_saved, h2, r, s = saved[i]
        da, *rcv = back_rows("ffn_down_bwd", dyb, g_down[i], ffc, [r], lambda acc, rr: (acc * (2.0 * rr.astype(F32)),), comm=sibling_comm())
        absorb(rcv)
        if i % 2 == 0 and i > 0 and any(nm == "down" for nm, _, _ in summed):
            nm_, l_, part = summed.pop([nm for nm, _, _ in summed].index("down"))
            comm, names = _chip_exchange([part], [(nm_, l_)], layers, stacked)
            g_down_i, *bufs = wgrad_rows("ffn_down_wgrad", s, dyb, ffc, D, tb, comm=comm)
            stacked.update(dict(zip(names, bufs)))
        else:
            g_down_i = wgrad_rows("ffn_down_wgrad", s, dyb, ffc, D, tb)
        pending.append(("down", i, g_down_i.reshape(N_DEV, ffc, D)))
        pending.append(("up", i, wgrad_cols("ffn_up_wgrad", h2, da, ffc)))
        dx, dyb, d_norm_ffn[i], *rcv = back_cols("ffn_up_bwd", da, transposed(g_up[i]), xm, norm_ffn[i], dx, comm=sibling_comm())
        absorb(rcv)
        if i % 2 == 0:
            h, lat, cq, ckv, q, kk, vv, o, lse = mix_saved
            w_dkv, w_uq, w_ukv, w_o = mla_w[l]
            do = _matmul("mla_out_bwd", dyb, w_o, [], grid=(nt,), a_spec=_row_spec(tm, D),
                         b_spec=pl.BlockSpec((None, OW, D), lambda i_: (0, 0, 0)), extra_specs=[],
                         out_shapes=[_sds((T, OW), BF16)], out_specs=[_row_spec(tm, OW)], dims=NT)[0]
            g_o_l = wgrad_rows("mla_out_wgrad", o, dyb, OW, D, tm).reshape(N_DEV, owc, D)
            comm, names = chip_comm()
            if i == 0:
                early = packed([jnp.stack(d_wsp, 0), jnp.stack(d_bsp, 0), jnp.concatenate(d_lng, 0), jnp.concatenate(d_lnb, 0)], early_sizes)
                comm = _merge_comm(comm, _gather_level1([early]))
            dq_pre, dk, dv, *bufs = _flash_bwd(q, kk, vv, o, do, lse, (t_cc, t_sa, t_sb), tq, comm=comm)
            stacked.update(dict(zip(names, bufs)))
            pending.append(("o", l, g_o_l))

            def kv_pre(dkb, dvb, cc, sa, sb):
                parts, dkr = [], None
                for b in range(HEADS):
                    parts += [dkb[:, b * QPAD:b * QPAD + NOPE], dvb[:, b * VDIM:(b + 1) * VDIM]]
                    piece = dkb[:, b * QPAD + NOPE:(b + 1) * QPAD].astype(F32)
                    dkr = piece if dkr is None else dkr + piece
                return jnp.concatenate(parts, axis=1), _rope_bwd(dkr, cc, sa, sb)

            dkv, dkr, *rest = _rowwise("mla_dkv_rope", kv_pre, [dk, dv, t_cc, t_sa, t_sb], grid=(T // ts,),
                                       in_specs=[_row_spec(ts, HW), _row_spec(ts, OW)] + [_row_spec(ts, 128)] * 3,
                                       out_shapes=[_sds((T, HW), BF16), _sds((T, 128), F32)], out_specs=[_row_spec(ts, HW), _row_spec(ts, 128)],
                                       comm=_gather_level2(bufs[len(names):]) if i == 0 else None)
            if i == 0:
                (gathered_early,) = rest
            g_uq_l = wgrad_rows("mla_q_wgrad", cq, dq_pre, Q_RANK, HW, tm)
            g_ukv_l = wgrad_rows("mla_kv_wgrad", ckv, dkv, KV_RANK, HW, tm)
            pending.append(("uq", l, g_uq_l.reshape(Q_RANK, HEADS, QPAD)[:, :, :NOPE + ROPE].transpose(1, 0, 2)))
            pending.append(("ukv", l, g_ukv_l.reshape(KV_RANK, HEADS, NOPE + VDIM).transpose(1, 0, 2)))
            dcq = _matmul("mla_q_bwd", dq_pre, w_uq, [], grid=(nt,), a_spec=_row_spec(tm, HW),
                          b_spec=pl.BlockSpec((None, Q_RANK, HW), lambda i_: (0, 0, 0)), extra_specs=[],
                          out_shapes=[_sds((T, Q_RANK), F32)], out_specs=[_row_spec(tm, Q_RANK)], dims=NT)[0]
            dckv = _matmul("mla_kv_bwd", dkv, w_ukv, [], grid=(nt,), a_spec=_row_spec(tm, HW),
                           b_spec=pl.BlockSpec((None, KV_RANK, HW), lambda i_: (0, 0, 0)), extra_specs=[],
                           out_shapes=[_sds((T, KV_RANK), F32)], out_specs=[_row_spec(tm, KV_RANK)], dims=NT)[0]

            def latent_bwd(la, qn, kvn, dq_, dkv_, dkr_):
                dcq_raw, dqn = _rms_bwd(la[:, :Q_RANK], qn, dq_)
                dckv_raw, dkvn = _rms_bwd(la[:, Q_RANK:Q_RANK + KV_RANK], kvn, dkv_)
                return jnp.concatenate([dcq_raw, dckv_raw, dkr_], axis=1), dqn, dkvn

            dlat, d_qn[l], d_kvn[l] = _rowwise(
                "mla_latent_bwd", latent_bwd, [lat, mla_q_norm[l].reshape(1, Q_RANK), mla_kv_norm[l].reshape(1, KV_RANK), dcq, dckv, dkr],
                grid=(nt,), in_specs=[_row_spec(tm, LAT_PAD), _const_spec((1, Q_RANK)), _const_spec((1, KV_RANK)),
                                      _row_spec(tm, Q_RANK), _row_spec(tm, KV_RANK), _row_spec(tm, 128)],
                out_shapes=[_sds((T, LAT_PAD), BF16), _sds((1, Q_RANK), F32), _sds((1, KV_RANK), F32)],
                out_specs=[_row_spec(tm, LAT_PAD), _const_spec((1, Q_RANK)), _const_spec((1, KV_RANK))], n_acc=2)
            g_dkv_l = wgrad_rows("mla_down_wgrad", h, dlat, D, LAT_PAD, tm)
            pending.append(("dkv", l, g_dkv_l[:, :LAT].reshape(N_DEV, dc, LAT)))
            dx, dyb, d_norm_mix[i] = _matmul(
                "mla_down_bwd", dlat, w_dkv, [xs_i, norm_mix[i].reshape(1, D), dx], grid=(nt,), a_spec=_row_spec(tm, LAT_PAD),
                b_spec=pl.BlockSpec((None, D, LAT_PAD), lambda i_: (0, 0, 0)), extra_specs=[_row_spec(tm, D), _const_spec((1, D)), _row_spec(tm, D)],
                out_shapes=[_sds((T, D), F32), _sds((T, D), BF16), _sds((1, D), F32)],
                out_specs=[_row_spec(tm, D), _row_spec(tm, D), _const_spec((1, D))], dims=NT, epilogue=norm_bwd_epilogue, n_sum=1)
        else:
            h, gp, ge, gate = mix_saved
            (dgate,) = back_rows("sgu_out_bwd", dyb, g_out[l], ec, [], None)
            pending.append(("out", l, wgrad_rows("sgu_out_wgrad", gate, dyb, ec, D, tb).reshape(N_DEV, ec, D)))
            dz, d_wsp[l], d_bsp[l], d_lng[l], d_lnb[l] = _sgu_mid_bwd(ge, gp, dgate, ln_g_full[l], ln_b_full[l], sgu_w_spatial[l], b_sp[l], 2)
            pending.append(("in", l, wgrad_cols("sgu_in_wgrad", h, dz, e2c)))
            dx, dyb, d_norm_mix[i], *rcv = back_cols("sgu_in_bwd", dz, transposed(g_in[l]), xs_i, norm_mix[i], dx, comm=sibling_comm())
            absorb(rcv)
    grad_x = dx.reshape(1, T, D)

    last_comm, last_names = chip_comm()
    late_g = [jnp.concatenate(d_norm_mix, 0), jnp.concatenate(d_norm_ffn, 0), d_final, jnp.concatenate(d_qn, 0), jnp.concatenate(d_kvn, 0)]
    late_w = [norm_mix, norm_ffn, final_norm, mla_q_norm, mla_kv_norm]
    late_m = [m_norm_mix, m_norm_ffn, m_final_norm, m_mla_q_norm, m_mla_kv_norm]
    late_v = [v_norm_mix, v_norm_ffn, v_final_norm, v_mla_q_norm, v_mla_kv_norm]
    late_sizes = [pad_to(g.size // 128, 8) for g in late_g]
    late_rows = sum(late_sizes)

    def adam_big(parts, w, m, v):
        lyr, rws, cls = w.shape
        rt = _tile(rws, 512)

        def fn(p, w_, m_, v_):
            g = (p[0].astype(F32) + p[1].astype(F32)) + (p[2].astype(F32) + p[3].astype(F32))
            return (g, *_adam(w_, g, m_, v_))

        spec = pl.BlockSpec((None, rt, cls), lambda l_, i_: (l_, i_, 0))
        return _rowwise("adam_large", fn, [parts, w, m, v], grid=(lyr, rws // rt),
                        in_specs=[pl.BlockSpec((N_CHIP, None, rt, cls), lambda l_, i_: (0, l_, i_, 0)), spec, spec, spec],
                        out_shapes=[_sds(w.shape, F32)] * 4, out_specs=[spec] * 4)

    stacked.update(dict(zip(last_names, _comm_call("grad_chip_exchange", last_comm))))
    (gathered_late,) = _all_gather("gather_small_grads", [packed(late_g, late_sizes)])
    big = {}
    big["in"] = adam_big(stacked["in"], sgu_w_in, m_sgu_w_in, v_sgu_w_in)
    big["up"] = adam_big(stacked["up"], ffn_w_up, m_ffn_w_up, v_ffn_w_up)
    big["down"] = adam_big(stacked["down"], ffn_w_down, m_ffn_w_down, v_ffn_w_down)
    big["out"] = adam_big(stacked["out"], sgu_w_out, m_sgu_w_out, v_sgu_w_out)
    big["dkv"] = adam_big(stacked["dkv"], mla_w_dkv, m_mla_w_dkv, v_mla_w_dkv)
    big["uq"] = adam_big(stacked["uq"], mla_w_uq, m_mla_w_uq, v_mla_w_uq)
    big["ukv"] = adam_big(stacked["ukv"], mla_w_ukv, m_mla_w_ukv, v_mla_w_ukv)
    big["o"] = adam_big(stacked["o"], mla_w_o, m_mla_w_o, v_mla_w_o)
    big_res = [big[nm][:4] for nm in ("dkv", "uq", "ukv", "o", "in", "out", "up", "down")]

    def sum8(p):
        return ((p[0] + p[1]) + (p[2] + p[3])) + ((p[4] + p[5]) + (p[6] + p[7]))

    def adam_packed(name, gathered, ws, ms, vs, sizes, rows, tile):
        spec = _row_spec(tile, 128)
        return _rowwise(name, lambda p, w_, m_, v_: (sum8(p), *_adam(w_, sum8(p), m_, v_)),
                        [gathered, packed(ws, sizes), packed(ms, sizes), packed(vs, sizes)], grid=(rows // tile,),
                        in_specs=[pl.BlockSpec((N_DEV, tile, 128), lambda i_: (0, i_, 0)), spec, spec, spec],
                        out_shapes=[_sds((rows, 128), F32)] * 4, out_specs=[spec] * 4)

    late_res = adam_packed("adam_small", gathered_late, late_w, late_m, late_v, late_sizes, late_rows, late_rows)
    early_res = adam_packed("adam_spatial", gathered_early, [sgu_w_spatial, sgu_b_spatial], [m_sgu_w_spatial, m_sgu_b_spatial],
                            [v_sgu_w_spatial, v_sgu_b_spatial], early_sizes[:2], early_rep, SMALL_ROWS)

    def unpack(res, sizes, k, like):
        off = sum(sizes[:k])
        return res[off:off + like.size // 128].reshape(like.shape)

    my_b = 4 * lax.axis_index("x") + 2 * lax.axis_index("y") + lax.axis_index("c")
    ln_w = jnp.concatenate([sgu_ln_g, sgu_ln_b], 0)
    ln_m = jnp.concatenate([m_sgu_ln_g, m_sgu_ln_b], 0)
    ln_v = jnp.concatenate([v_sgu_ln_g, v_sgu_ln_b], 0)
    ln_all = jnp.concatenate([gathered_early[:, early_rep:early_rep + n_sgu * E // 128], gathered_early[:, early_rep + n_ln:early_rep + n_ln + n_sgu * E // 128]], axis=1)
    ln_mine = lax.dynamic_slice_in_dim(ln_all.reshape(N_DEV, 2 * n_sgu, N_DEV, ec), my_b, 1, axis=2).reshape(N_DEV, 2 * n_sgu, ec)
    ln_g_, ln_d, ln_m2, ln_v2 = _rowwise(
        "adam_ln", lambda p, w_, m_, v_: (sum8(p), *_adam(w_, sum8(p), m_, v_)), [ln_mine, ln_w, ln_m, ln_v], grid=(1,),
        in_specs=[_const_spec(ln_mine.shape), _const_spec(ln_w.shape), _const_spec(ln_w.shape), _const_spec(ln_w.shape)],
        out_shapes=[_sds(ln_w.shape, F32)] * 4, out_specs=[_const_spec(ln_w.shape)] * 4)

    def family(pos):
        ln = [ln_g_, ln_d, ln_m2, ln_v2][pos]
        late = [unpack(late_res[pos], late_sizes, k, w_) for k, w_ in enumerate(late_w)]
        w_sp_, b_sp_ = unpack(early_res[pos], early_sizes, 0, sgu_w_spatial), unpack(early_res[pos], early_sizes, 1, sgu_b_spatial)
        bigs = [res[pos] for res in big_res]
        return [late[0], late[1], late[2], bigs[0], late[3], late[4], bigs[1], bigs[2], bigs[3],
                bigs[4], ln[:n_sgu], ln[n_sgu:], w_sp_, b_sp_, bigs[5], bigs[6], bigs[7]]

    return (loss, grad_x, *family(0), *family(1), *family(2), *family(3))
```

```python
import math

import jax
import jax.numpy as jnp
from jax import lax
from jax.experimental import pallas as pl
from jax.experimental.pallas import tpu as pltpu

F32 = jnp.float32
BF16 = jnp.bfloat16
MESH = pl.DeviceIdType.MESH

N_DEV = 8
N_CHIP = 4
HEADS = 8
NOPE = 128
ROPE = 64
VDIM = 128
QPAD = 256
Q_RANK = 256
KV_RANK = 128
LAT = Q_RANK + KV_RANK + ROPE
LAT_PAD = 512
ROPE_THETA = 10000.0
SGU_CHUNK = 128
SGU_GROUPS = 8
NORM_EPS = 1e-6
LN_EPS = 1e-5
ADAM_LR = 0.001
ADAM_B1 = 0.9
ADAM_B2 = 0.999
ADAM_EPS = 1e-08
ADAM_WD = 0.01
ADAM_STEP = 10
ATTN_SCALE = (NOPE + ROPE) ** -0.5
NEG = -1e30
EXP2_SCALE = ATTN_SCALE * math.log2(math.e)
VMEM_LIMIT = 56 * 1024 * 1024
SMALL_ROWS = 256

NN = (((1,), (0,)), ((), ()))
NT = (((1,), (1,)), ((), ()))
TN = (((0,), (0,)), ((), ()))
ANY = pl.BlockSpec(memory_space=pl.ANY)


def _pcall(body, **kw):
    return pl.pallas_call(body, **kw)


def _params(n_grid, side_effects=False):
    return pltpu.CompilerParams(dimension_semantics=("arbitrary",) * n_grid, vmem_limit_bytes=VMEM_LIMIT, has_side_effects=side_effects)


def _sds(shape, dtype):
    return jax.ShapeDtypeStruct(tuple(shape), dtype)


def _tile(n, want):
    t = min(n, want)
    assert n % t == 0, (n, want)
    return t


class _Comm:
    def __init__(self, operands, out_shapes, aliases, scratch, start, finish):
        self.operands, self.out_shapes, self.aliases, self.scratch = operands, out_shapes, aliases, scratch
        self.start, self.finish = start, finish


def _merge_comm(first, second):
    n_in, n_out, n_sc = len(first.operands), len(first.out_shapes), len(first.scratch)
    aliases = dict(first.aliases)
    aliases.update({n_in + k: n_out + v for k, v in second.aliases.items()})

    def start(ins, outs, sems):
        first.start(ins[:n_in], outs[:n_out], sems[:n_sc])
        second.start(ins[n_in:], outs[n_out:], sems[n_sc:])

    def finish(ins, outs, sems):
        first.finish(ins[:n_in], outs[:n_out], sems[:n_sc])
        second.finish(ins[n_in:], outs[n_out:], sems[n_sc:])

    return _Comm([*first.operands, *second.operands], [*first.out_shapes, *second.out_shapes], aliases,
                 [*first.scratch, *second.scratch], start, finish)


def _place():
    return lax.axis_index("x"), lax.axis_index("y"), lax.axis_index("c")


def _other_chips(x, y):
    return [(1 - x, y), (x, 1 - y), (1 - x, 1 - y)]


def _dev_index(dev):
    return 4 * dev[0] + 2 * dev[1] + dev[2]


def _comm_call(name, comm):
    c_in, c_out = len(comm.operands), len(comm.out_shapes)

    def body(*refs):
        ins, outs, sems = refs[:c_in], refs[c_in:c_in + c_out], refs[c_in + c_out:]
        comm.start(ins, outs, sems)
        comm.finish(ins, outs, sems)

    return _pcall(body, name=name, in_specs=[ANY] * c_in, out_specs=[ANY] * c_out, out_shape=comm.out_shapes,
                  scratch_shapes=comm.scratch, input_output_aliases=dict(comm.aliases),
                  compiler_params=pltpu.CompilerParams(has_side_effects=True))(*comm.operands)


def _call(name, body, operands, in_specs, out_shapes, out_specs, scratch, grid, comm=None):
    if comm is None:
        return _pcall(body, name=name, grid=grid, in_specs=in_specs, out_specs=out_specs, out_shape=out_shapes,
                      scratch_shapes=scratch, compiler_params=_params(len(grid)))(*operands)
    n_in, n_out, n_sc = len(operands), len(out_shapes), len(scratch)
    c_in, c_out = len(comm.operands), len(comm.out_shapes)

    def hosted(*refs):
        ins, cins = refs[:n_in], refs[n_in:n_in + c_in]
        o0 = n_in + c_in
        outs, couts = refs[o0:o0 + n_out], refs[o0 + n_out:o0 + n_out + c_out]
        rest = refs[o0 + n_out + c_out:]
        sc, csems = rest[:n_sc], rest[n_sc:]
        first = pl.program_id(0) == 0
        last = pl.program_id(0) == grid[0] - 1
        for d in range(1, len(grid)):
            first = jnp.logical_and(first, pl.program_id(d) == 0)
            last = jnp.logical_and(last, pl.program_id(d) == grid[d] - 1)

        @pl.when(first)
        def _():
            comm.start(cins, couts, csems)

        body(*ins, *outs, *sc)

        @pl.when(last)
        def _():
            comm.finish(cins, couts, csems)

    return _pcall(hosted, name=name, grid=grid, in_specs=[*in_specs, *[ANY] * c_in], out_specs=[*out_specs, *[ANY] * c_out],
                  out_shape=[*out_shapes, *comm.out_shapes], scratch_shapes=[*scratch, *comm.scratch],
                  input_output_aliases={n_in + k: n_out + v for k, v in comm.aliases.items()},
                  compiler_params=_params(len(grid), side_effects=True))(*operands, *comm.operands)


def _gather_level1(shards):
    n = len(shards)

    def copies(ins, outs, sems):
        send_sems, recv_sems, local_sems = sems
        x, y, c = _place()
        me, sibling = (x, y, c), (x, y, 1 - c)
        chips = _other_chips(x, y)

        def copy(a, k, block, to, src=None):
            slot = outs[a].at[_dev_index(block)]
            return pltpu.make_async_remote_copy(src_ref=slot if src is None else src, dst_ref=slot, send_sem=send_sems.at[a, k],
                                                recv_sem=recv_sems.at[a, k], device_id=to, device_id_type=MESH)

        mine = [pltpu.make_async_copy(ins[a], outs[a].at[_dev_index(me)], local_sems.at[a]) for a in range(n)]
        sends = [copy(a, 1 + j, me, (*chip, c), src=ins[a]) for j, chip in enumerate(chips) for a in range(n)]
        sends += [copy(a, 0, me, sibling, src=ins[a]) for a in range(n)]
        recvs = [copy(a, 1 + j, (*chip, c), me) for j, chip in enumerate(chips) for a in range(n)]
        recvs += [copy(a, 0, sibling, me) for a in range(n)]
        return mine, sends, recvs

    def start(ins, outs, sems):
        mine, sends, _ = copies(ins, outs, sems)
        for cp in mine + sends:
            cp.start()

    def finish(ins, outs, sems):
        mine, sends, recvs = copies(ins, outs, sems)
        for cp in recvs:
            cp.wait_recv()
        for cp in sends:
            cp.wait_send()
        for cp in mine:
            cp.wait()

    return _Comm(shards, [_sds((N_DEV, *a.shape), a.dtype) for a in shards], {},
                 [pltpu.SemaphoreType.DMA((n, 4)), pltpu.SemaphoreType.DMA((n, 4)), pltpu.SemaphoreType.DMA((n,))], start, finish)


def _gather_level2(bufs):
    n = len(bufs)

    def copies(outs, sems):
        send_sems, recv_sems = sems
        x, y, c = _place()
        sibling = (x, y, 1 - c)
        sends, recvs = [], []
        for j, chip in enumerate(_other_chips(x, y)):
            for a in range(n):
                have, want = outs[a].at[_dev_index((*chip, c))], outs[a].at[_dev_index((*chip, 1 - c))]
                sends.append(pltpu.make_async_remote_copy(src_ref=have, dst_ref=have, send_sem=send_sems.at[a, j], recv_sem=recv_sems.at[a, j],
                                                          device_id=sibling, device_id_type=MESH))
                recvs.append(pltpu.make_async_remote_copy(src_ref=want, dst_ref=want, send_sem=send_sems.at[a, j], recv_sem=recv_sems.at[a, j],
                                                          device_id=sibling, device_id_type=MESH))
        return sends, recvs

    def start(ins, outs, sems):
        for cp in copies(outs, sems)[0]:
            cp.start()

    def finish(ins, outs, sems):
        sends, recvs = copies(outs, sems)
        for cp in recvs:
            cp.wait_recv()
        for cp in sends:
            cp.wait_send()

    return _Comm(bufs, [_sds(b.shape, b.dtype) for b in bufs], {a: a for a in range(n)},
                 [pltpu.SemaphoreType.DMA((n, 3)), pltpu.SemaphoreType.DMA((n, 3))], start, finish)


def _all_gather(name, arrays):
    n = len(arrays)

    def body(*refs):
        ins = refs[:n]
        outs = refs[n:2 * n]
        send_sems, recv_sems, local_sems = refs[2 * n:]
        x, y, c = _place()
        me, sibling = (x, y, c), (x, y, 1 - c)
        chips = _other_chips(x, y)

        def copy(a, k, block, to, src=None):
            slot = outs[a].at[_dev_index(block)]
            return pltpu.make_async_remote_copy(src_ref=slot if src is None else src, dst_ref=slot, send_sem=send_sems.at[a, k],
                                                recv_sem=recv_sems.at[a, k], device_id=to, device_id_type=MESH)

        mine = [pltpu.make_async_copy(ins[a], outs[a].at[_dev_index(me)], local_sems.at[a]) for a in range(n)]
        for cp in mine:
            cp.start()
        first = []
        for j, chip in enumerate(chips):
            first += [copy(a, 1 + j, me, (*chip, c), src=ins[a]) for a in range(n)]
        first += [copy(a, 0, me, sibling, src=ins[a]) for a in range(n)]
        for cp in first:
            cp.start()
        passed = []
        for j, chip in enumerate(chips):
            for a in range(n):
                copy(a, 1 + j, (*chip, c), me).wait_recv()
                fwd = copy(a, 4 + j, (*chip, c), sibling)
                fwd.start()
                passed.append(fwd)
        for a in range(n):
            copy(a, 0, sibling, me).wait_recv()
            for j, chip in enumerate(chips):
                copy(a, 4 + j, (*chip, 1 - c), me).wait_recv()
        for cp in first + passed:
            cp.wait_send()
        for cp in mine:
            cp.wait()

    return _pcall(
        body, name=name, in_specs=[ANY] * n, out_specs=[ANY] * n,
        out_shape=[_sds((N_DEV, *a.shape), a.dtype) for a in arrays],
        scratch_shapes=[pltpu.SemaphoreType.DMA((n, 7)), pltpu.SemaphoreType.DMA((n, 7)), pltpu.SemaphoreType.DMA((n,))],
        compiler_params=pltpu.CompilerParams(has_side_effects=True),
    )(*arrays)


def _sibling_exchange(grads):
    n = len(grads)

    def start(ins, outs, sems):
        send_sems, recv_sems = sems
        x, y, c = _place()
        for a in range(n):
            for ch in range(N_CHIP):
                pltpu.make_async_remote_copy(src_ref=ins[a].at[2 * ch + 1 - c], dst_ref=outs[a].at[ch], send_sem=send_sems.at[a],
                                             recv_sem=recv_sems.at[a], device_id=(x, y, 1 - c), device_id_type=MESH).start()

    def finish(ins, outs, sems):
        send_sems, recv_sems = sems
        x, y, c = _place()
        for a in range(n):
            pltpu.make_async_remote_copy(src_ref=outs[a], dst_ref=outs[a], send_sem=send_sems.at[a], recv_sem=recv_sems.at[a],
                                         device_id=(x, y, 1 - c), device_id_type=MESH).wait()

    return _Comm(grads, [_sds((N_CHIP, *g.shape[1:]), g.dtype) for g in grads], {},
                 [pltpu.SemaphoreType.DMA((n,)), pltpu.SemaphoreType.DMA((n,))], start, finish)


def _chip_exchange(parts, slots, layers, stacked):
    n = len(parts)
    names = []
    for nm, _ in slots:
        if nm not in names:
            names.append(nm)
    shapes = {nm: _sds((N_CHIP, layers[nm], *parts[a].shape[1:]), parts[a].dtype) for a, (nm, _) in enumerate(slots)}
    kept = [nm for nm in names if stacked.get(nm) is not None]
    aliases = {n + k: names.index(nm) for k, nm in enumerate(kept)}

    def copies(ins, outs, sems):
        send_sems, recv_sems, local_sems = sems
        x, y, c = _place()
        mine = 2 * x + y
        local, sends, recvs = [], [], []
        for a, (nm, l) in enumerate(slots):
            buf = outs[names.index(nm)]
            local.append(pltpu.make_async_copy(ins[a].at[mine], buf.at[mine, l], local_sems.at[a]))
            for j, chip in enumerate(_other_chips(x, y)):
                theirs = buf.at[2 * chip[0] + chip[1], l]
                sends.append(pltpu.make_async_remote_copy(src_ref=ins[a].at[2 * chip[0] + chip[1]], dst_ref=buf.at[mine, l], send_sem=send_sems.at[a, j],
                                                          recv_sem=recv_sems.at[a, j], device_id=(*chip, c), device_id_type=MESH))
                recvs.append(pltpu.make_async_remote_copy(src_ref=theirs, dst_ref=theirs, send_sem=send_sems.at[a, j],
                                                          recv_sem=recv_sems.at[a, j], device_id=(*chip, c), device_id_type=MESH))
        return local, sends, recvs

    def start(ins, outs, sems):
        local, sends, _ = copies(ins, outs, sems)
        for cp in local + sends:
            cp.start()

    def finish(ins, outs, sems):
        local, sends, recvs = copies(ins, outs, sems)
        for cp in recvs:
            cp.wait_recv()
        for cp in sends:
            cp.wait_send()
        for cp in local:
            cp.wait()

    comm = _Comm([*parts, *[stacked[nm] for nm in kept]], [shapes[nm] for nm in names], aliases,
                 [pltpu.SemaphoreType.DMA((n, 3)), pltpu.SemaphoreType.DMA((n, 3)), pltpu.SemaphoreType.DMA((n,))], start, finish)
    return comm, names


def _matmul(name, a, b, extras, *, grid, a_spec, b_spec, extra_specs, out_shapes, out_specs, dims, k_axis=None, nk=1,
            acc_shape=None, epilogue=None, comm=None, n_sum=0, write=None):
    n_extra = len(extras)
    n_out = len(out_shapes)

    def body(*refs):
        a_ref, b_ref = refs[0], refs[1]
        ex = refs[2:2 + n_extra]
        outs = refs[2 + n_extra:2 + n_extra + n_out]
        prod = lax.dot_general(a_ref[...], b_ref[...], dims, preferred_element_type=F32)

        def finish(acc):
            if write is not None:
                write(outs, acc, *[e[...] for e in ex])
                return
            res = epilogue(acc, *[e[...] for e in ex]) if epilogue is not None else (acc,)
            first = None
            for d in range(len(grid)):
                if d != k_axis:
                    here = pl.program_id(d) == 0
                    first = here if first is None else jnp.logical_and(first, here)
            for idx, (o, r) in enumerate(zip(outs, res)):
                if idx < n_out - n_sum:
                    o[...] = r.astype(o.dtype)
                else:
                    @pl.when(first)
                    def _(o=o, r=r):
                        o[...] = r.astype(o.dtype)

                    @pl.when(jnp.logical_not(first))
                    def _(o=o, r=r):
                        o[...] += r.astype(o.dtype)

        if k_axis is None:
            finish(prod)
        else:
            acc_ref = refs[-1]
            k = pl.program_id(k_axis)

            @pl.when(k == 0)
            def _():
                acc_ref[...] = prod

            @pl.when(k > 0)
            def _():
                acc_ref[...] += prod

            @pl.when(k == nk - 1)
            def _():
                finish(acc_ref[...])

    scratch = [] if k_axis is None else [pltpu.VMEM(acc_shape, F32)]
    return _call(name, body, [a, b, *extras], [a_spec, b_spec, *extra_specs], list(out_shapes), list(out_specs), scratch, grid, comm)


def _rowwise(name, fn, operands, *, grid, in_specs, out_shapes, out_specs, n_acc=0, grid_spec_prefetch=None, comm=None):
    n_in = len(operands)
    n_out = len(out_shapes)
    n_pre = 0 if grid_spec_prefetch is None else 1

    def body(*refs):
        refs = refs[n_pre:]
        ins = refs[:n_in]
        outs = refs[n_in:n_in + n_out]
        res = fn(*[r[...] for r in ins])
        if not isinstance(res, (tuple, list)):
            res = (res,)
        first = pl.program_id(0) == 0
        for d in range(1, len(grid)):
            first = jnp.logical_and(first, pl.program_id(d) == 0)
        for idx, (o, r) in enumerate(zip(outs, res)):
            if idx < n_out - n_acc:
                o[...] = r.astype(o.dtype)
            else:
                @pl.when(first)
                def _(o=o, r=r):
                    o[...] = r.astype(o.dtype)

                @pl.when(jnp.logical_not(first))
                def _(o=o, r=r):
                    o[...] += r.astype(o.dtype)

    if comm is not None:
        return _call(name, body, list(operands), list(in_specs), list(out_shapes), list(out_specs), [], grid, comm)
    if grid_spec_prefetch is None:
        return _pcall(body, name=name, grid=grid, in_specs=in_specs, out_specs=out_specs, out_shape=out_shapes,
                      compiler_params=_params(len(grid)))(*operands)
    gs = pltpu.PrefetchScalarGridSpec(num_scalar_prefetch=1, grid=grid, in_specs=in_specs, out_specs=out_specs)
    return _pcall(body, name=name, grid_spec=gs, out_shape=out_shapes,
                  compiler_params=_params(len(grid)))(grid_spec_prefetch, *operands)


def _row_spec(tm, w):
    return pl.BlockSpec((tm, w), lambda i: (i, 0))


def _const_spec(shape):
    nd = len(shape)
    return pl.BlockSpec(tuple(shape), lambda *_: (0,) * nd)


def _rms_fwd(x, g):
    r = lax.rsqrt(jnp.mean(x * x, axis=-1, keepdims=True) + NORM_EPS)
    return x * r * g


def _rms_bwd(x, g, dy):
    r = lax.rsqrt(jnp.mean(x * x, axis=-1, keepdims=True) + NORM_EPS)
    xh = x * r
    u = dy * g
    dx = r * (u - xh * jnp.mean(u * xh, axis=-1, keepdims=True))
    dg = jnp.sum(dy * xh, axis=0, keepdims=True)
    return dx, dg


def _gelu_and_grad(z):
    cdf = 0.5 * (1.0 + lax.erf(z * (2.0 ** -0.5)))
    return cdf + z * jnp.exp(-0.5 * z * z) * ((2.0 * math.pi) ** -0.5), z * cdf


def _rope_fwd(x, cc, sa, sb):
    return x * cc + pltpu.roll(x, 96, 1) * sa + pltpu.roll(x, 32, 1) * sb


def _rope_bwd(d, cc, sa, sb):
    return d * cc + pltpu.roll(d * sa, 32, 1) + pltpu.roll(d * sb, 96, 1)


def _adam(w, g, m, v):
    m = ADAM_B1 * m + (1.0 - ADAM_B1) * g
    v = ADAM_B2 * v + (1.0 - ADAM_B2) * (g * g)
    m_hat = m / (1.0 - ADAM_B1 ** ADAM_STEP)
    v_hat = v / (1.0 - ADAM_B2 ** ADAM_STEP)
    delta = -ADAM_LR * (m_hat / (jnp.sqrt(v_hat) + ADAM_EPS) + ADAM_WD * w)
    return delta, m, v


def _flash_fwd(q, k, vt, tq, comm=None):
    h, t = vt.shape[0], q.shape[0]
    nq = t // tq

    chunk_blocks = [c for c in (4, 2) if c < nq]

    def body(q_ref, k_ref, vt_ref, o_ref, lse_ref, m_ref, l_ref, acc_ref):
        qi = pl.program_id(1)
        m_ref[...] = jnp.full((1, tq), NEG, F32)
        l_ref[...] = jnp.zeros((1, tq), F32)
        acc_ref[...] = jnp.zeros((VDIM, tq), F32)

        def update(kb0, nblk, masked):
            kb = k_ref[pl.ds(pl.multiple_of(kb0 * tq, tq), nblk * tq), :]
            st = lax.dot_general(kb, q_ref[...], NT, preferred_element_type=F32)
            if masked:
                key = lax.broadcasted_iota(jnp.int32, (nblk * tq, tq), 0) - (nblk - 1) * tq
                qry = lax.broadcasted_iota(jnp.int32, (nblk * tq, tq), 1)
                st = jnp.where(key <= qry, st, NEG)
            m_old = m_ref[...]
            m_new = jnp.maximum(m_old, jnp.max(st, axis=0, keepdims=True))
            alpha = jnp.exp2((m_old - m_new) * EXP2_SCALE)
            pt = jnp.exp2((st - m_new) * EXP2_SCALE)
            l_ref[...] = alpha * l_ref[...] + jnp.sum(pt, axis=0, keepdims=True)
            ptb = pt.astype(BF16)
            pv = lax.dot_general(vt_ref[kb0], ptb[:tq], NN, preferred_element_type=F32)
            for j in range(1, nblk):
                pv += lax.dot_general(vt_ref[kb0 + j], ptb[j * tq:(j + 1) * tq], NN, preferred_element_type=F32)
            acc_ref[...] = alpha * acc_ref[...] + pv
            m_ref[...] = m_new

        start = jnp.int32(0)
        for c in chunk_blocks:
            take = (qi & c) != 0

            @pl.when(take)
            def _(start=start, c=c):
                update(start, c, False)

            start = start + jnp.where(take, c, 0)
        if nq > 1:
            @pl.when((qi & 1) != 0)
            def _():
                update(qi - 1, 2, True)

            @pl.when((qi & 1) == 0)
            def _():
                update(qi, 1, True)
        else:
            update(qi, 1, True)
        l = l_ref[...]
        o_ref[...] = (acc_ref[...] / l).T.astype(o_ref.dtype)
        lse_ref[...] = m_ref[...] * EXP2_SCALE + jnp.log2(l)

    return _call(
        "flash_fwd", body, [q, k, vt],
        [pl.BlockSpec((tq, QPAD), lambda hh, i: (i, hh)),
         pl.BlockSpec((t, QPAD), lambda hh, i: (0, hh)),
         pl.BlockSpec((None, nq, VDIM, tq), lambda hh, i: (hh, 0, 0, 0))],
        [_sds((t, h * VDIM), BF16), _sds((h, nq, 1, tq), F32)],
        [pl.BlockSpec((tq, VDIM), lambda hh, i: (i, hh)),
         pl.BlockSpec((None, None, 1, tq), lambda hh, i: (hh, i, 0, 0))],
        [pltpu.VMEM((1, tq), F32), pltpu.VMEM((1, tq), F32), pltpu.VMEM((VDIM, tq), F32)], (h, nq), comm)


def _flash_bwd(q, k, v, o, do, lse, tabs, tq, comm=None):
    t = q.shape[0]
    h = q.shape[1] // QPAD
    nq = t // tq

    def body(q_ref, k_ref, v_ref, o_ref, do_ref, lse_ref, cc_ref, sa_ref, sb_ref, dq_ref, dk_out, dv_out, delta_ref, dqt_ref, dk_ref, dv_ref):
        kj = pl.program_id(1)

        @pl.when(kj == 0)
        def _():
            dqt_ref[...] = jnp.zeros_like(dqt_ref)
            ones = jnp.ones((8, VDIM), BF16)
            for qi in range(nq):
                rows = pl.ds(qi * tq, tq)
                prod = do_ref[rows, :].astype(F32) * o_ref[rows, :].astype(F32)
                hi = prod.astype(BF16)
                lo = (prod - hi.astype(F32)).astype(BF16)
                delta_ref[qi] = (lax.dot_general(ones, hi, NT, preferred_element_type=F32)
                                 + lax.dot_general(ones, lo, NT, preferred_element_type=F32))

        kb = k_ref[...]
        vb = v_ref[...]
        kbt = kb.astype(F32).T.astype(BF16)
        dk_ref[...] = jnp.zeros_like(dk_ref)
        dv_ref[...] = jnp.zeros_like(dv_ref)

        def step(q0, nblk, masked):
            rows = pl.ds(pl.multiple_of(q0 * tq, tq), nblk * tq)
            qb = q_ref[rows, :]
            dob = do_ref[rows, :]
            lse = jnp.concatenate([lse_ref[q0 + j] for j in range(nblk)], axis=1)
            delta = jnp.concatenate([delta_ref[q0 + j, pl.ds(0, 1), :] for j in range(nblk)], axis=1)
            st = lax.dot_general(kb, qb, NT, preferred_element_type=F32)
            pt = jnp.exp2(st * EXP2_SCALE - lse)
            if masked:
                key = lax.broadcasted_iota(jnp.int32, (tq, nblk * tq), 0)
                qry = lax.broadcasted_iota(jnp.int32, (tq, nblk * tq), 1)
                pt = jnp.where(key <= qry, pt, 0.0)
            dv_ref[...] += lax.dot_general(pt.astype(BF16), dob, NN, preferred_element_type=F32)
            dpt = lax.dot_general(vb, dob, NT, preferred_element_type=F32)
            dst = (pt * (dpt - delta) * ATTN_SCALE).astype(BF16)
            dk_ref[...] += lax.dot_general(dst, qb, NN, preferred_element_type=F32)
            dqt = lax.dot_general(kbt, dst, NN, preferred_element_type=F32)
            for j in range(nblk):
                dqt_ref[q0 + j] += dqt[:, j * tq:(j + 1) * tq]

        later = nq - 1 - kj
        if nq > 1:
            @pl.when((later & 1) != 0)
            def _():
                step(kj, 2, True)

            @pl.when((later & 1) == 0)
            def _():
                step(kj, 1, True)
        else:
            step(kj, 1, True)
        start = kj + 1 + (later & 1)
        for c in [c for c in (2, 4) if c < nq]:
            take = (later & c) != 0

            @pl.when(take)
            def _(start=start, c=c):
                step(start, c, False)

            start = start + jnp.where(take, c, 0)
        dk_out[...] = dk_ref[...].astype(BF16)
        dv_out[...] = dv_ref[...].astype(BF16)

        @pl.when(kj == nq - 1)
        def _():
            for qi in range(nq):
                rows = pl.ds(qi * tq, tq)
                d = dqt_ref[qi].T
                roped = _rope_bwd(d[:, NOPE:], cc_ref[rows, :], sa_ref[rows, :], sb_ref[rows, :])
                dq_ref[rows, :] = jnp.concatenate([d[:, :NOPE], roped], axis=1).astype(BF16)

    head_q = pl.BlockSpec((t, QPAD), lambda hh, j: (0, hh))
    head_v = pl.BlockSpec((t, VDIM), lambda hh, j: (0, hh))
    table = pl.BlockSpec((t, 128), lambda hh, j: (0, 0))
    return _call(
        "flash_bwd", body, [q, k, v, o, do, lse, *tabs],
        [head_q, pl.BlockSpec((tq, QPAD), lambda hh, j: (j, hh)), pl.BlockSpec((tq, VDIM), lambda hh, j: (j, hh)), head_v, head_v,
         pl.BlockSpec((None, nq, 1, tq), lambda hh, j: (hh, 0, 0, 0)), table, table, table],
        [_sds((t, h * QPAD), BF16), _sds((t, h * QPAD), BF16), _sds((t, h * VDIM), BF16)],
        [head_q, pl.BlockSpec((tq, QPAD), lambda hh, j: (j, hh)), pl.BlockSpec((tq, VDIM), lambda hh, j: (j, hh))],
        [pltpu.VMEM((nq, 8, tq), F32), pltpu.VMEM((nq, QPAD, tq), F32), pltpu.VMEM((tq, QPAD), F32), pltpu.VMEM((tq, VDIM), F32)], (h, nq), comm)


def _tril_bf16(w):
    row = lax.broadcasted_iota(jnp.int32, w.shape, 0)
    col = lax.broadcasted_iota(jnp.int32, w.shape, 1)
    return jnp.where(col <= row, w, 0.0).astype(BF16)


def _layer_norm_parts(v0):
    mu = jnp.mean(v0, axis=-1, keepdims=True)
    vc = v0 - mu
    rstd = lax.rsqrt(jnp.mean(vc * vc, axis=-1, keepdims=True) + LN_EPS)
    return vc * rstd, rstd


def _sgu_mid_fwd(ge, ln_g, ln_b, w_sp, b_sp, chunks_per_step):
    t, e2 = ge.shape
    e = e2 // 2
    gd = e // SGU_GROUPS
    rows = SGU_CHUNK * chunks_per_step

    def body(u_ref, v_ref, g_ref, b_ref, w_ref, bs_ref, gate_ref):
        for ck in range(chunks_per_step):
            r = pl.ds(ck * SGU_CHUNK, SGU_CHUNK)
            xh, _ = _layer_norm_parts(v_ref[r, :].astype(F32))
            v1 = (xh * g_ref[...] + b_ref[...]).astype(BF16)
            for g in range(SGU_GROUPS):
                cols = pl.ds(g * gd, gd)
                mixed = lax.dot_general(_tril_bf16(w_ref[g]), v1[:, g * gd:(g + 1) * gd], NN, preferred_element_type=F32) + bs_ref[g]
                gate_ref[r, cols] = (u_ref[r, cols].astype(F32) * mixed).astype(BF16)

    return _pcall(
        body, name="sgu_mid_fwd", grid=(t // rows,),
        in_specs=[pl.BlockSpec((rows, e), lambda i: (i, 0)), pl.BlockSpec((rows, e), lambda i: (i, 1)),
                  _const_spec((1, e)), _const_spec((1, e)), _const_spec(w_sp.shape), _const_spec(b_sp.shape)],
        out_specs=pl.BlockSpec((rows, e), lambda i: (i, 0)),
        out_shape=_sds((t, e), BF16), compiler_params=_params(1),
    )(ge, ge, ln_g, ln_b, w_sp, b_sp)


def _sgu_mid_bwd(ge, gp, dgate, ln_g, ln_b, w_sp, b_sp, chunks_per_step):
    t, e2 = ge.shape
    e = e2 // 2
    gd = e // SGU_GROUPS
    rows = SGU_CHUNK * chunks_per_step

    def body(u_ref, v_ref, zu_ref, zv_ref, dg_ref, g_ref, b_ref, w_ref, bs_ref, dz_ref, dw_ref, dbs_ref, dlg_ref, dlb_ref):
        @pl.when(pl.program_id(0) == 0)
        def _():
            dw_ref[...] = jnp.zeros_like(dw_ref)
            dbs_ref[...] = jnp.zeros_like(dbs_ref)
            dlg_ref[...] = jnp.zeros_like(dlg_ref)
            dlb_ref[...] = jnp.zeros_like(dlb_ref)

        for ck in range(chunks_per_step):
            r = pl.ds(ck * SGU_CHUNK, SGU_CHUNK)
            xh, rstd = _layer_norm_parts(v_ref[r, :].astype(F32))
            v1 = (xh * g_ref[...] + b_ref[...]).astype(BF16)
            dv1_parts = []
            for g in range(SGU_GROUPS):
                cols = pl.ds(g * gd, gd)
                wc = _tril_bf16(w_ref[g])
                v1g = v1[:, g * gd:(g + 1) * gd]
                mixed = lax.dot_general(wc, v1g, NN, preferred_element_type=F32) + bs_ref[g]
                dgate = dg_ref[r, cols].astype(F32)
                dmixed = dgate * u_ref[r, cols].astype(F32)
                du = dgate * mixed
                dz_ref[r, cols] = (du * zu_ref[r, cols].astype(F32)).astype(BF16)
                dbs_ref[g] += jnp.sum(dmixed, axis=1, keepdims=True)
                dmb = dmixed.astype(BF16)
                dwg = lax.dot_general(dmb, v1g, NT, preferred_element_type=F32)
                row = lax.broadcasted_iota(jnp.int32, dwg.shape, 0)
                col = lax.broadcasted_iota(jnp.int32, dwg.shape, 1)
                dw_ref[g] += jnp.where(col <= row, dwg, 0.0)
                dv1_parts.append(lax.dot_general(wc, dmb, TN, preferred_element_type=F32))
            dv1 = jnp.concatenate(dv1_parts, axis=1)
            dlg_ref[...] += jnp.sum(dv1 * xh, axis=0, keepdims=True)
            dlb_ref[...] += jnp.sum(dv1, axis=0, keepdims=True)
            dxh = dv1 * g_ref[...]
            dv0 = rstd * (dxh - jnp.mean(dxh, axis=-1, keepdims=True) - xh * jnp.mean(dxh * xh, axis=-1, keepdims=True))
            dz_ref[r, pl.ds(e, e)] = (dv0 * zv_ref[r, :].astype(F32)).astype(BF16)

    half0 = pl.BlockSpec((rows, e), lambda i: (i, 0))
    half1 = pl.BlockSpec((rows, e), lambda i: (i, 1))
    return _pcall(
        body, name="sgu_mid_bwd", grid=(t // rows,),
        in_specs=[half0, half1, half0, half1, half0, _const_spec((1, e)), _const_spec((1, e)), _const_spec(w_sp.shape), _const_spec(b_sp.shape)],
        out_specs=[pl.BlockSpec((rows, e2), lambda i: (i, 0)), _const_spec(w_sp.shape), _const_spec(b_sp.shape), _const_spec((1, e)), _const_spec((1, e))],
        out_shape=[_sds((t, e2), BF16), _sds(w_sp.shape, F32), _sds(b_sp.shape, F32), _sds((1, e), F32), _sds((1, e), F32)],
        compiler_params=_params(1),
    )(ge, ge, gp, gp, dgate, ln_g, ln_b, w_sp, b_sp)


def kernel(x, positions, norm_mix, norm_ffn, final_norm, mla_w_dkv, mla_q_norm, mla_kv_norm, mla_w_uq, mla_w_ukv, mla_w_o, sgu_w_in, sgu_ln_g, sgu_ln_b, sgu_w_spatial, sgu_b_spatial, sgu_w_out, ffn_w_up, ffn_w_down, loss_target, m_norm_mix, m_norm_ffn, m_final_norm, m_mla_w_dkv, m_mla_q_norm, m_mla_kv_norm, m_mla_w_uq, m_mla_w_ukv, m_mla_w_o, m_sgu_w_in, m_sgu_ln_g, m_sgu_ln_b, m_sgu_w_spatial, m_sgu_b_spatial, m_sgu_w_out, m_ffn_w_up, m_ffn_w_down, v_norm_mix, v_norm_ffn, v_final_norm, v_mla_w_dkv, v_mla_q_norm, v_mla_kv_norm, v_mla_w_uq, v_mla_w_ukv, v_mla_w_o, v_sgu_w_in, v_sgu_ln_g, v_sgu_ln_b, v_sgu_w_spatial, v_sgu_b_spatial, v_sgu_w_out, v_ffn_w_up, v_ffn_w_down):
    _, T, D = x.shape
    depth = norm_mix.shape[0]
    n_mla, n_sgu = mla_w_dkv.shape[0], sgu_w_in.shape[0]
    assert depth % 2 == 0
    FF = ffn_w_up.shape[2] * N_DEV
    E = sgu_w_out.shape[1] * N_DEV
    ffc, ec, e2c = FF // N_DEV, E // N_DEV, 2 * E // N_DEV
    dc = D // N_DEV
    OW = HEADS * VDIM
    HW = HEADS * QPAD
    owc = OW // N_DEV
    tm = _tile(T, 1024)
    tb = _tile(T, 4096)
    tk = _tile(T, 512)
    tq = _tile(T, 512)
    ts = _tile(T, 256)
    nt = T // tm
    x2 = x.reshape(T, D)
    tgt = loss_target.reshape(T, D)
    cidx = lax.axis_index("c").astype(jnp.int32).reshape(1)

    ln_local = jnp.concatenate([sgu_ln_g, sgu_ln_b, jnp.zeros((8 - 2 * n_sgu, ec), F32)], axis=0)
    mla_sh = [[w[l].astype(BF16) for w in (mla_w_dkv, mla_w_uq, mla_w_ukv, mla_w_o)] for l in range(n_mla)]

    def mla_layouts(g_dkv, g_uq, g_ukv, g_o):
        w_dkv = jnp.pad(g_dkv.reshape(1, D, LAT), ((0, 0), (0, 0), (0, LAT_PAD - LAT)))
        w_uq = jnp.pad(g_uq, ((0, 0), (0, 0), (0, QPAD - NOPE - ROPE))).transpose(1, 0, 2).reshape(1, Q_RANK, HEADS * QPAD)
        w_ukv = g_ukv.transpose(1, 0, 2).reshape(1, KV_RANK, HEADS * (NOPE + VDIM))
        return w_dkv, w_uq, w_ukv, g_o.reshape(1, HEADS * VDIM, D)

    mla_w = [None] * n_mla
    small_later = [a for l in range(1, n_mla) for a in mla_sh[l]] + [ln_local]
    ln_g_full, ln_b_full = [None] * n_sgu, [None] * n_sgu
    b_sp = sgu_b_spatial.reshape(n_sgu, SGU_GROUPS, SGU_CHUNK, 1)
    up_sh = [ffn_w_up[i].astype(BF16) for i in range(depth)]
    down_sh = [ffn_w_down[i].astype(BF16) for i in range(depth)]
    in_sh = [sgu_w_in[l].astype(BF16) for l in range(n_sgu)]
    out_sh = [sgu_w_out[l].astype(BF16) for l in range(n_sgu)]
    g_up, g_down, g_in, g_out = [None] * depth, [None] * depth, [None] * n_sgu, [None] * n_sgu

    inv_freq = ROPE_THETA ** (-jnp.arange(0, ROPE, 2, dtype=F32) / ROPE)
    zeros32 = jnp.zeros((ROPE // 2,), F32)
    inv128 = jnp.concatenate([inv_freq, inv_freq, zeros32, zeros32]).reshape(1, 128)
    sel_a = jnp.concatenate([-jnp.ones((32,), F32), zeros32, zeros32, zeros32]).reshape(1, 128)
    sel_b = jnp.concatenate([zeros32, jnp.ones((32,), F32), zeros32, zeros32]).reshape(1, 128)
    sel_c = jnp.concatenate([jnp.ones((64,), F32), zeros32, zeros32]).reshape(1, 128)

    def rope_tables(pos, inv, sa, sb, sc):
        ang = pos.astype(F32) * inv
        cs, sn = jnp.cos(ang), jnp.sin(ang)
        return cs * sc, sn * sa, sn * sb

    t_cc, t_sa, t_sb, *first_half = _rowwise(
        "rope_tables", rope_tables, [positions.reshape(T, 1), inv128, sel_a, sel_b, sel_c], grid=(nt,),
        in_specs=[_row_spec(tm, 1)] + [_const_spec((1, 128))] * 4,
        out_shapes=[_sds((T, 128), F32)] * 3, out_specs=[_row_spec(tm, 128)] * 3, comm=_gather_level1(mla_sh[0]))
    tab_specs = [_row_spec(tm, 128)] * 3

    def rmsnorm(xv, g, comm):
        return _rowwise("rmsnorm", lambda a, gg: _rms_fwd(a, gg), [xv, g.reshape(1, D)], grid=(nt,),
                        in_specs=[_row_spec(tm, D), _const_spec((1, D))], out_shapes=[_sds((T, D), BF16)], out_specs=[_row_spec(tm, D)], comm=comm)

    def proj_cols(name, h, gw, nc, epilogue, n_out, comm=None):
        return _matmul(name, h, gw, [], grid=(N_DEV, T // tb),
                       a_spec=pl.BlockSpec((tb, D), lambda j, i: (i, 0)),
                       b_spec=pl.BlockSpec((None, D, nc), lambda j, i: (j, 0, 0)), extra_specs=[],
                       out_shapes=[_sds((T, nc * N_DEV), BF16)] * n_out, out_specs=[pl.BlockSpec((tb, nc), lambda j, i: (i, j))] * n_out,
                       dims=NN, epilogue=epilogue, comm=comm)

    def residual_norm(acc, xr, g):
        xn = acc + xr
        return xn, _rms_fwd(xn, g)

    def proj_rows_residual(name, a, gw, xres, g_next, comm=None):
        kk_ = a.shape[1]
        return _matmul(name, a, gw.reshape(kk_, D), [xres, g_next.reshape(1, D)], grid=(T // tk,),
                       a_spec=_row_spec(tk, kk_), b_spec=_const_spec((kk_, D)), extra_specs=[_row_spec(tk, D), _const_spec((1, D))],
                       out_shapes=[_sds((T, D), F32), _sds((T, D), BF16)], out_specs=[_row_spec(tk, D)] * 2,
                       dims=NN, epilogue=residual_norm, comm=comm)

    def back_rows(name, dy, gw, kc, extras, epilogue, comm=None):
        return _matmul(name, dy, gw, extras, grid=(N_DEV, T // tb),
                       a_spec=pl.BlockSpec((tb, D), lambda j, i: (i, 0)),
                       b_spec=pl.BlockSpec((None, kc, D), lambda j, i: (j, 0, 0)),
                       extra_specs=[pl.BlockSpec((tb, kc), lambda j, i: (i, j))] * len(extras),
                       out_shapes=[_sds((T, kc * N_DEV), BF16)], out_specs=[pl.BlockSpec((tb, kc), lambda j, i: (i, j))],
                       dims=NT, epilogue=epilogue, comm=comm)

    def norm_bwd_epilogue(dh, xv, g, dxi):
        dxn, dg = _rms_bwd(xv, g, dh)
        return dxi + dxn, dxi + dxn, dg

    def transposed(gw):
        return gw.transpose(0, 2, 1).reshape(gw.shape[0] * gw.shape[2], D)

    def back_cols(name, da, gwt, xv, g, dx_in, comm=None):
        n = da.shape[1]
        row = _row_spec(tk, D)
        return _matmul(name, da, gwt, [xv, g.reshape(1, D), dx_in], grid=(T // tk,),
                       a_spec=_row_spec(tk, n), b_spec=_const_spec((n, D)), extra_specs=[row, _const_spec((1, D)), row],
                       out_shapes=[_sds((T, D), F32), _sds((T, D), BF16), _sds((1, D), F32)], out_specs=[row, row, _const_spec((1, D))],
                       dims=NN, epilogue=norm_bwd_epilogue, n_sum=1, comm=comm)

    def token_sum(tt):
        return dict(k_axis=1, nk=T // tt) if T // tt > 1 else dict(k_axis=None)

    def wgrad_cols(name, h, da, nc):
        return _matmul(name, h, da, [], grid=(N_DEV, T // tb),
                       a_spec=pl.BlockSpec((tb, D), lambda j, t: (t, 0)), b_spec=pl.BlockSpec((tb, nc), lambda j, t: (t, j)),
                       extra_specs=[], out_shapes=[_sds((N_DEV, D, nc), BF16)],
                       out_specs=[pl.BlockSpec((None, D, nc), lambda j, t: (j, 0, 0))],
                       dims=TN, acc_shape=(D, nc), **token_sum(tb))[0]

    def wgrad_rows(name, a, dy, kc, ncols, tt, comm=None):
        res = _matmul(name, a, dy, [], grid=(a.shape[1] // kc, T // tt),
                      a_spec=pl.BlockSpec((tt, kc), lambda j, t: (t, j)), b_spec=pl.BlockSpec((tt, ncols), lambda j, t: (t, 0)),
                      extra_specs=[], out_shapes=[_sds((a.shape[1], ncols), BF16)],
                      out_specs=[pl.BlockSpec((kc, ncols), lambda j, t: (j, 0))],
                      dims=TN, acc_shape=(kc, ncols), comm=comm, **token_sum(tt))
        return res[0] if comm is None else res

    saved = []
    xs = x2
    for i in range(depth):
        l = i // 2
        if i == 0:
            h, *first_w = rmsnorm(xs, norm_mix[0], _gather_level2(first_half))
            mla_w[0] = mla_layouts(*first_w)
        if i % 2 == 0:
            w_dkv, w_uq, w_ukv, w_o = mla_w[l]
            lat = _matmul("mla_down", h, w_dkv, [], grid=(nt,), a_spec=_row_spec(tm, D),
                          b_spec=pl.BlockSpec((None, D, LAT_PAD), lambda i_: (0, 0, 0)), extra_specs=[],
                          out_shapes=[_sds((T, LAT_PAD), F32)], out_specs=[_row_spec(tm, LAT_PAD)], dims=NN)[0]

            def latent_post(la, qn, kvn, cc, sa, sb):
                cq = _rms_fwd(la[:, :Q_RANK], qn)
                ckv = _rms_fwd(la[:, Q_RANK:Q_RANK + KV_RANK], kvn)
                kr = _rope_fwd(la[:, Q_RANK + KV_RANK:], cc, sa, sb)
                return cq, ckv, kr

            cq, ckv, kr = _rowwise(
                "mla_latent", latent_post, [lat, mla_q_norm[l].reshape(1, Q_RANK), mla_kv_norm[l].reshape(1, KV_RANK), t_cc, t_sa, t_sb],
                grid=(nt,), in_specs=[_row_spec(tm, LAT_PAD), _const_spec((1, Q_RANK)), _const_spec((1, KV_RANK))] + tab_specs,
                out_shapes=[_sds((T, Q_RANK), BF16), _sds((T, KV_RANK), BF16), _sds((T, 128), BF16)],
                out_specs=[_row_spec(tm, Q_RANK), _row_spec(tm, KV_RANK), _row_spec(tm, 128)])

            def q_epilogue(acc, cc, sa, sb):
                parts = []
                for b in range(HEADS):
                    parts += [acc[:, b * QPAD:b * QPAD + NOPE], _rope_fwd(acc[:, b * QPAD + NOPE:(b + 1) * QPAD], cc, sa, sb)]
                return (jnp.concatenate(parts, axis=1),)

            q = _matmul("mla_q", cq, w_uq, [t_cc, t_sa, t_sb], grid=(nt,), a_spec=_row_spec(tm, Q_RANK),
                        b_spec=pl.BlockSpec((None, Q_RANK, HW), lambda i_: (0, 0, 0)), extra_specs=tab_specs,
                        out_shapes=[_sds((T, HW), BF16)], out_specs=[_row_spec(tm, HW)], dims=NN, epilogue=q_epilogue)[0]

            def kv_write(outs, acc, krb):
                k_ref, v_ref, vt_ref = outs
                for b in range(HEADS):
                    vb = acc[:, b * QPAD + NOPE:(b + 1) * QPAD]
                    k_ref[:, b * QPAD:b * QPAD + NOPE] = acc[:, b * QPAD:b * QPAD + NOPE].astype(BF16)
                    k_ref[:, b * QPAD + NOPE:(b + 1) * QPAD] = krb
                    v_ref[:, b * VDIM:(b + 1) * VDIM] = vb.astype(BF16)
                    vbt = vb.T.astype(BF16)
                    for u in range(tm // tq):
                        vt_ref[b, u] = vbt[:, u * tq:(u + 1) * tq]

            kk, vv, vt = _matmul("mla_kv", ckv, w_ukv, [kr], grid=(nt,), a_spec=_row_spec(tm, KV_RANK),
                                 b_spec=pl.BlockSpec((None, KV_RANK, HW), lambda i_: (0, 0, 0)), extra_specs=[_row_spec(tm, 128)],
                                 out_shapes=[_sds((T, HW), BF16), _sds((T, OW), BF16), _sds((HEADS, T // tq, VDIM, tq), BF16)],
                                 out_specs=[_row_spec(tm, HW), _row_spec(tm, OW), pl.BlockSpec((HEADS, tm // tq, VDIM, tq), lambda i_: (0, i_, 0, 0))],
                                 dims=NN, write=kv_write)
            group = [up_sh[i], down_sh[i], in_sh[l], out_sh[l]] + (small_later if i == 0 else [])
            o, lse, *bufs = _flash_fwd(q, kk, vt, tq, comm=_gather_level1(group))
            xm, h2, g_up[i], g_down[i] = _matmul(
                "mla_out", o, w_o, [xs, norm_ffn[i].reshape(1, D)], grid=(nt,), a_spec=_row_spec(tm, OW),
                b_spec=pl.BlockSpec((None, OW, D), lambda i_: (0, 0, 0)), extra_specs=[_row_spec(tm, D), _const_spec((1, D))],
                out_shapes=[_sds((T, D), F32), _sds((T, D), BF16)], out_specs=[_row_spec(tm, D)] * 2, dims=NN,
                epilogue=residual_norm, comm=_gather_level2(bufs[:2]))
            half_gathered = bufs[2:]
            mix_saved = (h, lat, cq, ckv, q, kk, vv, o, lse)
        else:
            gp, ge, g_down[i], up_half = proj_cols("sgu_in", h, g_in[l], e2c, _gelu_and_grad, 2,
                                                   comm=_merge_comm(_gather_level2([down_half]), _gather_level1([up_sh[i]])))
            gate = _sgu_mid_fwd(ge, ln_g_full[l], ln_b_full[l], sgu_w_spatial[l], b_sp[l], 4)
            xm, h2, g_up[i] = proj_rows_residual("sgu_out", gate, g_out[l], xs, norm_ffn[i], comm=_gather_level2([up_half]))
            mix_saved = (h, gp, ge, gate)
        r, s, *rest = proj_cols("ffn_up", h2, g_up[i], ffc, lambda acc: (jnp.maximum(acc, 0.0), jnp.square(jnp.maximum(acc, 0.0))), 2,
                                comm=_gather_level2(half_gathered) if i % 2 == 0 else None)
        if i % 2 == 0:
            g_in[l], g_out[l], *small_gathered = rest
        if i == 0:
            for l_ in range(1, n_mla):
                mla_w[l_] = mla_layouts(*small_gathered[4 * (l_ - 1):4 * l_])
            g_ln = small_gathered[-1]
            ln_g_full = [g_ln[:, l_, :].reshape(1, E) for l_ in range(n_sgu)]
            ln_b_full = [g_ln[:, n_sgu + l_, :].reshape(1, E) for l_ in range(n_sgu)]
        xo, h_next, *rest = proj_rows_residual("ffn_down", s, g_down[i], xm, norm_mix[i + 1] if i + 1 < depth else final_norm,
                                               comm=_gather_level1([down_sh[i + 1]]) if i % 2 == 0 else None)
        if i % 2 == 0:
            (down_half,) = rest
        saved.append((xs, xm, mix_saved, h2, r, s))
        xs, h = xo, h_next

    def loss_head(xv, tg, g):
        y = _rms_fwd(xv, g)
        err = y - tg
        part = 0.5 * jnp.sum(jnp.sum(err * err, axis=-1, keepdims=True), axis=0, keepdims=True) / D
        dx, dg = _rms_bwd(xv, g, err / D)
        return dx, dx, jnp.broadcast_to(part, (1, 128)), dg

    dx, dyb, loss_part, d_final = _rowwise(
        "loss_head", loss_head, [xs, tgt, final_norm.reshape(1, D)], grid=(nt,),
        in_specs=[_row_spec(tm, D), _row_spec(tm, D), _const_spec((1, D))],
        out_shapes=[_sds((T, D), F32), _sds((T, D), BF16), _sds((1, 128), F32), _sds((1, D), F32)],
        out_specs=[_row_spec(tm, D), _row_spec(tm, D), _const_spec((1, 128)), _const_spec((1, D))], n_acc=2)
    loss = lax.psum(loss_part[0, 0], ("x", "y", "c"))

    d_norm_mix, d_norm_ffn = [None] * depth, [None] * depth
    d_qn, d_kvn = [None] * n_mla, [None] * n_mla
    d_wsp, d_bsp, d_lng, d_lnb = [None] * n_sgu, [None] * n_sgu, [None] * n_sgu, [None] * n_sgu
    layers = {"dkv": n_mla, "uq": n_mla, "ukv": n_mla, "o": n_mla, "in": n_sgu, "out": n_sgu, "up": depth, "down": depth}
    stacked = {nm: None for nm in layers}
    pending = []
    summed = []

    def add_pairs(gs, rcvs):
        operands, in_specs, out_shapes, out_specs = [], [], [], []
        for g, rcv in zip(gs, rcvs):
            _, rws, cls = g.shape
            slab = pl.BlockSpec((None, rws, cls), lambda ch, cr: (ch, 0, 0))
            operands += [g.reshape(N_CHIP, 2, rws, cls), rcv]
            in_specs += [pl.BlockSpec((None, None, rws, cls), lambda ch, cr: (ch, cr[0], 0, 0)), slab]
            out_shapes.append(_sds(rcv.shape, BF16))
            out_specs.append(slab)

        def fn(*blocks):
            return tuple(blocks[2 * k].astype(F32) + blocks[2 * k + 1].astype(F32) for k in range(len(gs)))

        return _rowwise("grad_pair_sum", fn, operands, grid=(N_CHIP,), in_specs=in_specs, out_shapes=out_shapes, out_specs=out_specs,
                        grid_spec_prefetch=cidx)

    def sibling_comm():
        return _sibling_exchange([g for _, _, g in pending]) if pending else None

    def absorb(from_sibling):
        if pending:
            parts = add_pairs([g for _, _, g in pending], list(from_sibling))
            summed.extend((nm, l_, p) for (nm, l_, _), p in zip(pending, parts))
            pending.clear()

    def chip_comm():
        if pending:
            absorb(_comm_call("grad_sibling_exchange", sibling_comm()))
        comm, names = _chip_exchange([p for _, _, p in summed], [(nm, l_) for nm, l_, _ in summed], layers, stacked)
        summed.clear()
        return comm, names

    def rows128(a, rows):
        flat = a.reshape(-1, 128)
        return jnp.pad(flat, ((0, rows - flat.shape[0]), (0, 0)))

    def pad_to(n, mult):
        return -(-n // mult) * mult

    def packed(arrs, sizes):
        return jnp.concatenate([rows128(a, sz) for a, sz in zip(arrs, sizes)], axis=0)

    n_wsp, n_bsp, n_ln = sgu_w_spatial.size // 128, pad_to(sgu_b_spatial.size // 128, 8), pad_to(n_sgu * E // 128, 8)
    early_sizes = [n_wsp, pad_to(n_wsp + n_bsp, SMALL_ROWS) - n_wsp, n_ln, n_ln]
    early_rep = early_sizes[0] + early_sizes[1]
    gathered_early = None

    for i in reversed(range(depth)):
        l = i // 2
        xs_i, xm, mix_saved, h2, r, s = saved[i]
        comm = sibling_comm()
        if i == 0:
            comm = _gather_level2([early_half]) if comm is None else _merge_comm(comm, _gather_level2([early_half]))
        da, *rcv = back_rows("ffn_down_bwd", dyb, g_down[i], ffc, [r], lambda acc, rr: (acc * (2.0 * rr.astype(F32)),), comm=comm)
        if i == 0:
            *rcv, gathered_early = rcv
        absorb(rcv)
        if i == 0 and any(nm == "down" for nm, _, _ in summed):
            nm_, l_, part = summed.pop([nm for nm, _, _ in summed].index("down"))
            comm, names = _chip_exchange([part], [(nm_, l_)], layers, stacked)
            g_down_i, *bufs = wgrad_rows("ffn_down_wgrad", s, dyb, ffc, D, tb, comm=comm)
            stacked.update(dict(zip(names, bufs)))
        else:
            g_down_i = wgrad_rows("ffn_down_wgrad", s, dyb, ffc, D, tb)
        pending.append(("down", i, g_down_i.reshape(N_DEV, ffc, D)))
        pending.append(("up", i, wgrad_cols("ffn_up_wgrad", h2, da, ffc)))
        dx, dyb, d_norm_ffn[i], *rcv = back_cols("ffn_up_bwd", da, transposed(g_up[i]), xm, norm_ffn[i], dx, comm=sibling_comm())
        absorb(rcv)
        if i % 2 == 0:
            h, lat, cq, ckv, q, kk, vv, o, lse = mix_saved
            w_dkv, w_uq, w_ukv, w_o = mla_w[l]
            do = _matmul("mla_out_bwd", dyb, w_o, [], grid=(nt,), a_spec=_row_spec(tm, D),
                         b_spec=pl.BlockSpec((None, OW, D), lambda i_: (0, 0, 0)), extra_specs=[],
                         out_shapes=[_sds((T, OW), BF16)], out_specs=[_row_spec(tm, OW)], dims=NT)[0]
            g_o_l = wgrad_rows("mla_out_wgrad", o, dyb, OW, D, tm).reshape(N_DEV, owc, D)
            comm, names = chip_comm()
            dq_pre, dk, dv, *bufs = _flash_bwd(q, kk, vv, o, do, lse, (t_cc, t_sa, t_sb), tq, comm=comm)
            stacked.update(dict(zip(names, bufs)))
            pending.append(("o", l, g_o_l))

            def kv_pre(dkb, dvb, cc, sa, sb):
                parts, dkr = [], None
                for b in range(HEADS):
                    parts += [dkb[:, b * QPAD:b * QPAD + NOPE], dvb[:, b * VDIM:(b + 1) * VDIM]]
                    piece = dkb[:, b * QPAD + NOPE:(b + 1) * QPAD].astype(F32)
                    dkr = piece if dkr is None else dkr + piece
                return jnp.concatenate(parts, axis=1), _rope_bwd(dkr, cc, sa, sb)

            dkv, dkr = _rowwise("mla_dkv_rope", kv_pre, [dk, dv, t_cc, t_sa, t_sb], grid=(T // ts,),
                                in_specs=[_row_spec(ts, HW), _row_spec(ts, OW)] + [_row_spec(ts, 128)] * 3,
                                out_shapes=[_sds((T, HW), BF16), _sds((T, 128), F32)], out_specs=[_row_spec(ts, HW), _row_spec(ts, 128)])
            g_uq_l = wgrad_rows("mla_q_wgrad", cq, dq_pre, Q_RANK, HW, tm)
            g_ukv_l = wgrad_rows("mla_kv_wgrad", ckv, dkv, KV_RANK, HW, tm)
            pending.append(("uq", l, g_uq_l.reshape(Q_RANK, HEADS, QPAD)[:, :, :NOPE + ROPE].transpose(1, 0, 2)))
            pending.append(("ukv", l, g_ukv_l.reshape(KV_RANK, HEADS, NOPE + VDIM).transpose(1, 0, 2)))
            dcq = _matmul("mla_q_bwd", dq_pre, w_uq, [], grid=(nt,), a_spec=_row_spec(tm, HW),
                          b_spec=pl.BlockSpec((None, Q_RANK, HW), lambda i_: (0, 0, 0)), extra_specs=[],
                          out_shapes=[_sds((T, Q_RANK), F32)], out_specs=[_row_spec(tm, Q_RANK)], dims=NT)[0]
            dckv = _matmul("mla_kv_bwd", dkv, w_ukv, [], grid=(nt,), a_spec=_row_spec(tm, HW),
                           b_spec=pl.BlockSpec((None, KV_RANK, HW), lambda i_: (0, 0, 0)), extra_specs=[],
                           out_shapes=[_sds((T, KV_RANK), F32)], out_specs=[_row_spec(tm, KV_RANK)], dims=NT)[0]

            def latent_bwd(la, qn, kvn, dq_, dkv_, dkr_):
                dcq_raw, dqn = _rms_bwd(la[:, :Q_RANK], qn, dq_)
                dckv_raw, dkvn = _rms_bwd(la[:, Q_RANK:Q_RANK + KV_RANK], kvn, dkv_)
                return jnp.concatenate([dcq_raw, dckv_raw, dkr_], axis=1), dqn, dkvn

            dlat, d_qn[l], d_kvn[l] = _rowwise(
                "mla_latent_bwd", latent_bwd, [lat, mla_q_norm[l].reshape(1, Q_RANK), mla_kv_norm[l].reshape(1, KV_RANK), dcq, dckv, dkr],
                grid=(nt,), in_specs=[_row_spec(tm, LAT_PAD), _const_spec((1, Q_RANK)), _const_spec((1, KV_RANK)),
                                      _row_spec(tm, Q_RANK), _row_spec(tm, KV_RANK), _row_spec(tm, 128)],
                out_shapes=[_sds((T, LAT_PAD), BF16), _sds((1, Q_RANK), F32), _sds((1, KV_RANK), F32)],
                out_specs=[_row_spec(tm, LAT_PAD), _const_spec((1, Q_RANK)), _const_spec((1, KV_RANK))], n_acc=2)
            g_dkv_l = wgrad_rows("mla_down_wgrad", h, dlat, D, LAT_PAD, tm)
            pending.append(("dkv", l, g_dkv_l[:, :LAT].reshape(N_DEV, dc, LAT)))
            dx, dyb, d_norm_mix[i] = _matmul(
                "mla_down_bwd", dlat, w_dkv, [xs_i, norm_mix[i].reshape(1, D), dx], grid=(nt,), a_spec=_row_spec(tm, LAT_PAD),
                b_spec=pl.BlockSpec((None, D, LAT_PAD), lambda i_: (0, 0, 0)), extra_specs=[_row_spec(tm, D), _const_spec((1, D)), _row_spec(tm, D)],
                out_shapes=[_sds((T, D), F32), _sds((T, D), BF16), _sds((1, D), F32)],
                out_specs=[_row_spec(tm, D), _row_spec(tm, D), _const_spec((1, D))], dims=NT, epilogue=norm_bwd_epilogue, n_sum=1)
        else:
            h, gp, ge, gate = mix_saved
            (dgate,) = back_rows("sgu_out_bwd", dyb, g_out[l], ec, [], None)
            pending.append(("out", l, wgrad_rows("sgu_out_wgrad", gate, dyb, ec, D, tb).reshape(N_DEV, ec, D)))
            dz, d_wsp[l], d_bsp[l], d_lng[l], d_lnb[l] = _sgu_mid_bwd(ge, gp, dgate, ln_g_full[l], ln_b_full[l], sgu_w_spatial[l], b_sp[l], 2)
            pending.append(("in", l, wgrad_cols("sgu_in_wgrad", h, dz, e2c)))
            comm = sibling_comm()
            if i == 1:
                early = packed([jnp.stack(d_wsp, 0), jnp.stack(d_bsp, 0), jnp.concatenate(d_lng, 0), jnp.concatenate(d_lnb, 0)], early_sizes)
                comm = _merge_comm(comm, _gather_level1([early]))
            dx, dyb, d_norm_mix[i], *rcv = back_cols("sgu_in_bwd", dz, transposed(g_in[l]), xs_i, norm_mix[i], dx, comm=comm)
            if i == 1:
                *rcv, early_half = rcv
            absorb(rcv)
    grad_x = dx.reshape(1, T, D)

    last_comm, last_names = chip_comm()
    late_g = [jnp.concatenate(d_norm_mix, 0), jnp.concatenate(d_norm_ffn, 0), d_final, jnp.concatenate(d_qn, 0), jnp.concatenate(d_kvn, 0)]
    late_w = [norm_mix, norm_ffn, final_norm, mla_q_norm, mla_kv_norm]
    late_m = [m_norm_mix, m_norm_ffn, m_final_norm, m_mla_q_norm, m_mla_kv_norm]
    late_v = [v_norm_mix, v_norm_ffn, v_final_norm, v_mla_q_norm, v_mla_kv_norm]
    late_sizes = [pad_to(g.size // 128, 8) for g in late_g]
    late_rows = sum(late_sizes)

    def adam_big(parts, w, m, v):
        lyr, rws, cls = w.shape
        rt = _tile(rws, 512)

        def fn(p, w_, m_, v_):
            g = (p[0].astype(F32) + p[1].astype(F32)) + (p[2].astype(F32) + p[3].astype(F32))
            return (g, *_adam(w_, g, m_, v_))

        spec = pl.BlockSpec((None, rt, cls), lambda l_, i_: (l_, i_, 0))
        return _rowwise("adam_large", fn, [parts, w, m, v], grid=(lyr, rws // rt),
                        in_specs=[pl.BlockSpec((N_CHIP, None, rt, cls), lambda l_, i_: (0, l_, i_, 0)), spec, spec, spec],
                        out_shapes=[_sds(w.shape, F32)] * 4, out_specs=[spec] * 4)

    stacked.update(dict(zip(last_names, _comm_call("grad_chip_exchange", last_comm))))
    (gathered_late,) = _all_gather("gather_small_grads", [packed(late_g, late_sizes)])
    big = {}
    big["in"] = adam_big(stacked["in"], sgu_w_in, m_sgu_w_in, v_sgu_w_in)
    big["up"] = adam_big(stacked["up"], ffn_w_up, m_ffn_w_up, v_ffn_w_up)
    big["down"] = adam_big(stacked["down"], ffn_w_down, m_ffn_w_down, v_ffn_w_down)
    big["out"] = adam_big(stacked["out"], sgu_w_out, m_sgu_w_out, v_sgu_w_out)
    big["dkv"] = adam_big(stacked["dkv"], mla_w_dkv, m_mla_w_dkv, v_mla_w_dkv)
    big["uq"] = adam_big(stacked["uq"], mla_w_uq, m_mla_w_uq, v_mla_w_uq)
    big["ukv"] = adam_big(stacked["ukv"], mla_w_ukv, m_mla_w_ukv, v_mla_w_ukv)
    big["o"] = adam_big(stacked["o"], mla_w_o, m_mla_w_o, v_mla_w_o)
    big_res = [big[nm][:4] for nm in ("dkv", "uq", "ukv", "o", "in", "out", "up", "down")]

    def sum8(p):
        return ((p[0] + p[1]) + (p[2] + p[3])) + ((p[4] + p[5]) + (p[6] + p[7]))

    def adam_packed(name, gathered, ws, ms, vs, sizes, rows, tile):
        spec = _row_spec(tile, 128)
        return _rowwise(name, lambda p, w_, m_, v_: (sum8(p), *_adam(w_, sum8(p), m_, v_)),
                        [gathered, packed(ws, sizes), packed(ms, sizes), packed(vs, sizes)], grid=(rows // tile,),
                        in_specs=[pl.BlockSpec((N_DEV, tile, 128), lambda i_: (0, i_, 0)), spec, spec, spec],
                        out_shapes=[_sds((rows, 128), F32)] * 4, out_specs=[spec] * 4)

    late_res = adam_packed("adam_small", gathered_late, late_w, late_m, late_v, late_sizes, late_rows, late_rows)
    early_res = adam_packed("adam_spatial", gathered_early, [sgu_w_spatial, sgu_b_spatial], [m_sgu_w_spatial, m_sgu_b_spatial],
                            [v_sgu_w_spatial, v_sgu_b_spatial], early_sizes[:2], early_rep, SMALL_ROWS)

    def unpack(res, sizes, k, like):
        off = sum(sizes[:k])
        return res[off:off + like.size // 128].reshape(like.shape)

    my_b = 4 * lax.axis_index("x") + 2 * lax.axis_index("y") + lax.axis_index("c")
    ln_w = jnp.concatenate([sgu_ln_g, sgu_ln_b], 0)
    ln_m = jnp.concatenate([m_sgu_ln_g, m_sgu_ln_b], 0)
    ln_v = jnp.concatenate([v_sgu_ln_g, v_sgu_ln_b], 0)
    ln_all = jnp.concatenate([gathered_early[:, early_rep:early_rep + n_sgu * E // 128], gathered_early[:, early_rep + n_ln:early_rep + n_ln + n_sgu * E // 128]], axis=1)
    ln_mine = lax.dynamic_slice_in_dim(ln_all.reshape(N_DEV, 2 * n_sgu, N_DEV, ec), my_b, 1, axis=2).reshape(N_DEV, 2 * n_sgu, ec)
    ln_g_, ln_d, ln_m2, ln_v2 = _rowwise(
        "adam_ln", lambda p, w_, m_, v_: (sum8(p), *_adam(w_, sum8(p), m_, v_)), [ln_mine, ln_w, ln_m, ln_v], grid=(1,),
        in_specs=[_const_spec(ln_mine.shape), _const_spec(ln_w.shape), _const_spec(ln_w.shape), _const_spec(ln_w.shape)],
        out_shapes=[_sds(ln_w.shape, F32)] * 4, out_specs=[_const_spec(ln_w.shape)] * 4)

    def family(pos):
        ln = [ln_g_, ln_d, ln_m2, ln_v2][pos]
        late = [unpack(late_res[pos], late_sizes, k, w_) for k, w_ in enumerate(late_w)]
        w_sp_, b_sp_ = unpack(early_res[pos], early_sizes, 0, sgu_w_spatial), unpack(early_res[pos], early_sizes, 1, sgu_b_spatial)
        bigs = [res[pos] for res in big_res]
        return [late[0], late[1], late[2], bigs[0], late[3], late[4], bigs[1], bigs[2], bigs[3],
                bigs[4], ln[:n_sgu], ln[n_sgu:], w_sp_, b_sp_, bigs[5], bigs[6], bigs[7]]

    return (loss, grad_x, *family(0), *family(1), *family(2), *family(3))
```

```python
import math

import jax
import jax.numpy as jnp
from jax import lax
from jax.experimental import pallas as pl
from jax.experimental.pallas import tpu as pltpu

F32 = jnp.float32
BF16 = jnp.bfloat16
MESH = pl.DeviceIdType.MESH

N_DEV = 8
N_CHIP = 4
HEADS = 8
NOPE = 128
ROPE = 64
VDIM = 128
QPAD = 256
Q_RANK = 256
KV_RANK = 128
LAT = Q_RANK + KV_RANK + ROPE
LAT_PAD = 512
ROPE_THETA = 10000.0
SGU_CHUNK = 128
SGU_GROUPS = 8
NORM_EPS = 1e-6
LN_EPS = 1e-5
ADAM_LR = 0.001
ADAM_B1 = 0.9
ADAM_B2 = 0.999
ADAM_EPS = 1e-08
ADAM_WD = 0.01
ADAM_STEP = 10
ATTN_SCALE = (NOPE + ROPE) ** -0.5
NEG = -1e30
EXP2_SCALE = ATTN_SCALE * math.log2(math.e)
VMEM_LIMIT = 56 * 1024 * 1024
SMALL_ROWS = 256

NN = (((1,), (0,)), ((), ()))
NT = (((1,), (1,)), ((), ()))
TN = (((0,), (0,)), ((), ()))
ANY = pl.BlockSpec(memory_space=pl.ANY)


def _pcall(body, **kw):
    return pl.pallas_call(body, **kw)


def _params(n_grid, side_effects=False):
    return pltpu.CompilerParams(dimension_semantics=("arbitrary",) * n_grid, vmem_limit_bytes=VMEM_LIMIT, has_side_effects=side_effects)


def _sds(shape, dtype):
    return jax.ShapeDtypeStruct(tuple(shape), dtype)


def _tile(n, want):
    t = min(n, want)
    assert n % t == 0, (n, want)
    return t


class _Comm:
    def __init__(self, operands, out_shapes, aliases, scratch, start, finish):
        self.operands, self.out_shapes, self.aliases, self.scratch = operands, out_shapes, aliases, scratch
        self.start, self.finish = start, finish


def _merge_comm(first, second):
    n_in, n_out, n_sc = len(first.operands), len(first.out_shapes), len(first.scratch)
    aliases = dict(first.aliases)
    aliases.update({n_in + k: n_out + v for k, v in second.aliases.items()})

    def start(ins, outs, sems):
        first.start(ins[:n_in], outs[:n_out], sems[:n_sc])
        second.start(ins[n_in:], outs[n_out:], sems[n_sc:])

    def finish(ins, outs, sems):
        first.finish(ins[:n_in], outs[:n_out], sems[:n_sc])
        second.finish(ins[n_in:], outs[n_out:], sems[n_sc:])

    return _Comm([*first.operands, *second.operands], [*first.out_shapes, *second.out_shapes], aliases,
                 [*first.scratch, *second.scratch], start, finish)


def _place():
    return lax.axis_index("x"), lax.axis_index("y"), lax.axis_index("c")


def _other_chips(x, y):
    return [(1 - x, y), (x, 1 - y), (1 - x, 1 - y)]


def _dev_index(dev):
    return 4 * dev[0] + 2 * dev[1] + dev[2]


def _comm_call(name, comm):
    c_in, c_out = len(comm.operands), len(comm.out_shapes)

    def body(*refs):
        ins, outs, sems = refs[:c_in], refs[c_in:c_in + c_out], refs[c_in + c_out:]
        comm.start(ins, outs, sems)
        comm.finish(ins, outs, sems)

    return _pcall(body, name=name, in_specs=[ANY] * c_in, out_specs=[ANY] * c_out, out_shape=comm.out_shapes,
                  scratch_shapes=comm.scratch, input_output_aliases=dict(comm.aliases),
                  compiler_params=pltpu.CompilerParams(has_side_effects=True))(*comm.operands)


def _call(name, body, operands, in_specs, out_shapes, out_specs, scratch, grid, comm=None):
    if comm is None:
        return _pcall(body, name=name, grid=grid, in_specs=in_specs, out_specs=out_specs, out_shape=out_shapes,
                      scratch_shapes=scratch, compiler_params=_params(len(grid)))(*operands)
    n_in, n_out, n_sc = len(operands), len(out_shapes), len(scratch)
    c_in, c_out = len(comm.operands), len(comm.out_shapes)

    def hosted(*refs):
        ins, cins = refs[:n_in], refs[n_in:n_in + c_in]
        o0 = n_in + c_in
        outs, couts = refs[o0:o0 + n_out], refs[o0 + n_out:o0 + n_out + c_out]
        rest = refs[o0 + n_out + c_out:]
        sc, csems = rest[:n_sc], rest[n_sc:]
        first = pl.program_id(0) == 0
        last = pl.program_id(0) == grid[0] - 1
        for d in range(1, len(grid)):
            first = jnp.logical_and(first, pl.program_id(d) == 0)
            last = jnp.logical_and(last, pl.program_id(d) == grid[d] - 1)

        @pl.when(first)
        def _():
            comm.start(cins, couts, csems)

        body(*ins, *outs, *sc)

        @pl.when(last)
        def _():
            comm.finish(cins, couts, csems)

    return _pcall(hosted, name=name, grid=grid, in_specs=[*in_specs, *[ANY] * c_in], out_specs=[*out_specs, *[ANY] * c_out],
                  out_shape=[*out_shapes, *comm.out_shapes], scratch_shapes=[*scratch, *comm.scratch],
                  input_output_aliases={n_in + k: n_out + v for k, v in comm.aliases.items()},
                  compiler_params=_params(len(grid), side_effects=True))(*operands, *comm.operands)


def _gather_level1(shards):
    n = len(shards)

    def copies(ins, outs, sems):
        send_sems, recv_sems, local_sems = sems
        x, y, c = _place()
        me, sibling = (x, y, c), (x, y, 1 - c)
        chips = _other_chips(x, y)

        def copy(a, k, block, to, src=None):
            slot = outs[a].at[_dev_index(block)]
            return pltpu.make_async_remote_copy(src_ref=slot if src is None else src, dst_ref=slot, send_sem=send_sems.at[a, k],
                                                recv_sem=recv_sems.at[a, k], device_id=to, device_id_type=MESH)

        mine = [pltpu.make_async_copy(ins[a], outs[a].at[_dev_index(me)], local_sems.at[a]) for a in range(n)]
        sends = [copy(a, 1 + j, me, (*chip, c), src=ins[a]) for j, chip in enumerate(chips) for a in range(n)]
        sends += [copy(a, 0, me, sibling, src=ins[a]) for a in range(n)]
        recvs = [copy(a, 1 + j, (*chip, c), me) for j, chip in enumerate(chips) for a in range(n)]
        recvs += [copy(a, 0, sibling, me) for a in range(n)]
        return mine, sends, recvs

    def start(ins, outs, sems):
        mine, sends, _ = copies(ins, outs, sems)
        for cp in mine + sends:
            cp.start()

    def finish(ins, outs, sems):
        mine, sends, recvs = copies(ins, outs, sems)
        for cp in recvs:
            cp.wait_recv()
        for cp in sends:
            cp.wait_send()
        for cp in mine:
            cp.wait()

    return _Comm(shards, [_sds((N_DEV, *a.shape), a.dtype) for a in shards], {},
                 [pltpu.SemaphoreType.DMA((n, 4)), pltpu.SemaphoreType.DMA((n, 4)), pltpu.SemaphoreType.DMA((n,))], start, finish)


def _gather_level2(bufs):
    n = len(bufs)

    def copies(outs, sems):
        send_sems, recv_sems = sems
        x, y, c = _place()
        sibling = (x, y, 1 - c)
        sends, recvs = [], []
        for j, chip in enumerate(_other_chips(x, y)):
            for a in range(n):
                have, want = outs[a].at[_dev_index((*chip, c))], outs[a].at[_dev_index((*chip, 1 - c))]
                sends.append(pltpu.make_async_remote_copy(src_ref=have, dst_ref=have, send_sem=send_sems.at[a, j], recv_sem=recv_sems.at[a, j],
                                                          device_id=sibling, device_id_type=MESH))
                recvs.append(pltpu.make_async_remote_copy(src_ref=want, dst_ref=want, send_sem=send_sems.at[a, j], recv_sem=recv_sems.at[a, j],
                                                          device_id=sibling, device_id_type=MESH))
        return sends, recvs

    def start(ins, outs, sems):
        for cp in copies(outs, sems)[0]:
            cp.start()

    def finish(ins, outs, sems):
        sends, recvs = copies(outs, sems)
        for cp in recvs:
            cp.wait_recv()
        for cp in sends:
            cp.wait_send()

    return _Comm(bufs, [_sds(b.shape, b.dtype) for b in bufs], {a: a for a in range(n)},
                 [pltpu.SemaphoreType.DMA((n, 3)), pltpu.SemaphoreType.DMA((n, 3))], start, finish)


def _all_gather(name, arrays):
    n = len(arrays)

    def body(*refs):
        ins = refs[:n]
        outs = refs[n:2 * n]
        send_sems, recv_sems, local_sems = refs[2 * n:]
        x, y, c = _place()
        me, sibling = (x, y, c), (x, y, 1 - c)
        chips = _other_chips(x, y)

        def copy(a, k, block, to, src=None):
            slot = outs[a].at[_dev_index(block)]
            return pltpu.make_async_remote_copy(src_ref=slot if src is None else src, dst_ref=slot, send_sem=send_sems.at[a, k],
                                                recv_sem=recv_sems.at[a, k], device_id=to, device_id_type=MESH)

        mine = [pltpu.make_async_copy(ins[a], outs[a].at[_dev_index(me)], local_sems.at[a]) for a in range(n)]
        for cp in mine:
            cp.start()
        first = []
        for j, chip in enumerate(chips):
            first += [copy(a, 1 + j, me, (*chip, c), src=ins[a]) for a in range(n)]
        first += [copy(a, 0, me, sibling, src=ins[a]) for a in range(n)]
        for cp in first:
            cp.start()
        passed = []
        for j, chip in enumerate(chips):
            for a in range(n):
                copy(a, 1 + j, (*chip, c), me).wait_recv()
                fwd = copy(a, 4 + j, (*chip, c), sibling)
                fwd.start()
                passed.append(fwd)
        for a in range(n):
            copy(a, 0, sibling, me).wait_recv()
            for j, chip in enumerate(chips):
                copy(a, 4 + j, (*chip, 1 - c), me).wait_recv()
        for cp in first + passed:
            cp.wait_send()
        for cp in mine:
            cp.wait()

    return _pcall(
        body, name=name, in_specs=[ANY] * n, out_specs=[ANY] * n,
        out_shape=[_sds((N_DEV, *a.shape), a.dtype) for a in arrays],
        scratch_shapes=[pltpu.SemaphoreType.DMA((n, 7)), pltpu.SemaphoreType.DMA((n, 7)), pltpu.SemaphoreType.DMA((n,))],
        compiler_params=pltpu.CompilerParams(has_side_effects=True),
    )(*arrays)


def _sibling_exchange(grads):
    n = len(grads)

    def start(ins, outs, sems):
        send_sems, recv_sems = sems
        x, y, c = _place()
        for a in range(n):
            for ch in range(N_CHIP):
                pltpu.make_async_remote_copy(src_ref=ins[a].at[2 * ch + 1 - c], dst_ref=outs[a].at[ch], send_sem=send_sems.at[a],
                                             recv_sem=recv_sems.at[a], device_id=(x, y, 1 - c), device_id_type=MESH).start()

    def finish(ins, outs, sems):
        send_sems, recv_sems = sems
        x, y, c = _place()
        for a in range(n):
            pltpu.make_async_remote_copy(src_ref=outs[a], dst_ref=outs[a], send_sem=send_sems.at[a], recv_sem=recv_sems.at[a],
                                         device_id=(x, y, 1 - c), device_id_type=MESH).wait()

    return _Comm(grads, [_sds((N_CHIP, *g.shape[1:]), g.dtype) for g in grads], {},
                 [pltpu.SemaphoreType.DMA((n,)), pltpu.SemaphoreType.DMA((n,))], start, finish)


def _chip_exchange(parts, slots, layers, stacked):
    n = len(parts)
    names = []
    for nm, _ in slots:
        if nm not in names:
            names.append(nm)
    shapes = {nm: _sds((N_CHIP, layers[nm], *parts[a].shape[1:]), parts[a].dtype) for a, (nm, _) in enumerate(slots)}
    kept = [nm for nm in names if stacked.get(nm) is not None]
    aliases = {n + k: names.index(nm) for k, nm in enumerate(kept)}

    def copies(ins, outs, sems):
        send_sems, recv_sems, local_sems = sems
        x, y, c = _place()
        mine = 2 * x + y
        local, sends, recvs = [], [], []
        for a, (nm, l) in enumerate(slots):
            buf = outs[names.index(nm)]
            local.append(pltpu.make_async_copy(ins[a].at[mine], buf.at[mine, l], local_sems.at[a]))
            for j, chip in enumerate(_other_chips(x, y)):
                theirs = buf.at[2 * chip[0] + chip[1], l]
                sends.append(pltpu.make_async_remote_copy(src_ref=ins[a].at[2 * chip[0] + chip[1]], dst_ref=buf.at[mine, l], send_sem=send_sems.at[a, j],
                                                          recv_sem=recv_sems.at[a, j], device_id=(*chip, c), device_id_type=MESH))
                recvs.append(pltpu.make_async_remote_copy(src_ref=theirs, dst_ref=theirs, send_sem=send_sems.at[a, j],
                                                          recv_sem=recv_sems.at[a, j], device_id=(*chip, c), device_id_type=MESH))
        return local, sends, recvs

    def start(ins, outs, sems):
        local, sends, _ = copies(ins, outs, sems)
        for cp in local + sends:
            cp.start()

    def finish(ins, outs, sems):
        local, sends, recvs = copies(ins, outs, sems)
        for cp in recvs:
            cp.wait_recv()
        for cp in sends:
            cp.wait_send()
        for cp in local:
            cp.wait()

    comm = _Comm([*parts, *[stacked[nm] for nm in kept]], [shapes[nm] for nm in names], aliases,
                 [pltpu.SemaphoreType.DMA((n, 3)), pltpu.SemaphoreType.DMA((n, 3)), pltpu.SemaphoreType.DMA((n,))], start, finish)
    return comm, names


def _matmul(name, a, b, extras, *, grid, a_spec, b_spec, extra_specs, out_shapes, out_specs, dims, k_axis=None, nk=1,
            acc_shape=None, epilogue=None, comm=None, n_sum=0, write=None):
    n_extra = len(extras)
    n_out = len(out_shapes)

    def body(*refs):
        a_ref, b_ref = refs[0], refs[1]
        ex = refs[2:2 + n_extra]
        outs = refs[2 + n_extra:2 + n_extra + n_out]
        prod = lax.dot_general(a_ref[...], b_ref[...], dims, preferred_element_type=F32)

        def finish(acc):
            if write is not None:
                write(outs, acc, *[e[...] for e in ex])
                return
            res = epilogue(acc, *[e[...] for e in ex]) if epilogue is not None else (acc,)
            first = None
            for d in range(len(grid)):
                if d != k_axis:
                    here = pl.program_id(d) == 0
                    first = here if first is None else jnp.logical_and(first, here)
            for idx, (o, r) in enumerate(zip(outs, res)):
                if idx < n_out - n_sum:
                    o[...] = r.astype(o.dtype)
                else:
                    @pl.when(first)
                    def _(o=o, r=r):
                        o[...] = r.astype(o.dtype)

                    @pl.when(jnp.logical_not(first))
                    def _(o=o, r=r):
                        o[...] += r.astype(o.dtype)

        if k_axis is None:
            finish(prod)
        else:
            acc_ref = refs[-1]
            k = pl.program_id(k_axis)

            @pl.when(k == 0)
            def _():
                acc_ref[...] = prod

            @pl.when(k > 0)
            def _():
                acc_ref[...] += prod

            @pl.when(k == nk - 1)
            def _():
                finish(acc_ref[...])

    scratch = [] if k_axis is None else [pltpu.VMEM(acc_shape, F32)]
    return _call(name, body, [a, b, *extras], [a_spec, b_spec, *extra_specs], list(out_shapes), list(out_specs), scratch, grid, comm)


def _rowwise(name, fn, operands, *, grid, in_specs, out_shapes, out_specs, n_acc=0, grid_spec_prefetch=None, comm=None):
    n_in = len(operands)
    n_out = len(out_shapes)
    n_pre = 0 if grid_spec_prefetch is None else 1

    def body(*refs):
        refs = refs[n_pre:]
        ins = refs[:n_in]
        outs = refs[n_in:n_in + n_out]
        res = fn(*[r[...] for r in ins])
        if not isinstance(res, (tuple, list)):
            res = (res,)
        first = pl.program_id(0) == 0
        for d in range(1, len(grid)):
            first = jnp.logical_and(first, pl.program_id(d) == 0)
        for idx, (o, r) in enumerate(zip(outs, res)):
            if idx < n_out - n_acc:
                o[...] = r.astype(o.dtype)
            else:
                @pl.when(first)
                def _(o=o, r=r):
                    o[...] = r.astype(o.dtype)

                @pl.when(jnp.logical_not(first))
                def _(o=o, r=r):
                    o[...] += r.astype(o.dtype)

    if comm is not None:
        return _call(name, body, list(operands), list(in_specs), list(out_shapes), list(out_specs), [], grid, comm)
    if grid_spec_prefetch is None:
        return _pcall(body, name=name, grid=grid, in_specs=in_specs, out_specs=out_specs, out_shape=out_shapes,
                      compiler_params=_params(len(grid)))(*operands)
    gs = pltpu.PrefetchScalarGridSpec(num_scalar_prefetch=1, grid=grid, in_specs=in_specs, out_specs=out_specs)
    return _pcall(body, name=name, grid_spec=gs, out_shape=out_shapes,
                  compiler_params=_params(len(grid)))(grid_spec_prefetch, *operands)


def _row_spec(tm, w):
    return pl.BlockSpec((tm, w), lambda i: (i, 0))


def _const_spec(shape):
    nd = len(shape)
    return pl.BlockSpec(tuple(shape), lambda *_: (0,) * nd)


def _rms_fwd(x, g):
    r = lax.rsqrt(jnp.mean(x * x, axis=-1, keepdims=True) + NORM_EPS)
    return x * r * g


def _rms_bwd(x, g, dy):
    r = lax.rsqrt(jnp.mean(x * x, axis=-1, keepdims=True) + NORM_EPS)
    xh = x * r
    u = dy * g
    dx = r * (u - xh * jnp.mean(u * xh, axis=-1, keepdims=True))
    dg = jnp.sum(dy * xh, axis=0, keepdims=True)
    return dx, dg


def _gelu_and_grad(z):
    cdf = 0.5 * (1.0 + lax.erf(z * (2.0 ** -0.5)))
    return cdf + z * jnp.exp(-0.5 * z * z) * ((2.0 * math.pi) ** -0.5), z * cdf


def _rope_fwd(x, cc, sa, sb):
    return x * cc + pltpu.roll(x, 96, 1) * sa + pltpu.roll(x, 32, 1) * sb


def _rope_bwd(d, cc, sa, sb):
    return d * cc + pltpu.roll(d * sa, 32, 1) + pltpu.roll(d * sb, 96, 1)


def _adam(w, g, m, v):
    m = ADAM_B1 * m + (1.0 - ADAM_B1) * g
    v = ADAM_B2 * v + (1.0 - ADAM_B2) * (g * g)
    m_hat = m / (1.0 - ADAM_B1 ** ADAM_STEP)
    v_hat = v / (1.0 - ADAM_B2 ** ADAM_STEP)
    delta = -ADAM_LR * (m_hat / (jnp.sqrt(v_hat) + ADAM_EPS) + ADAM_WD * w)
    return delta, m, v


def _flash_fwd(q, k, vt, tq, comm=None):
    h, t = vt.shape[0], q.shape[0]
    nq = t // tq

    chunk_blocks = [c for c in (4, 2) if c < nq]

    def body(q_ref, k_ref, vt_ref, o_ref, lse_ref, m_ref, l_ref, acc_ref):
        qi = pl.program_id(1)
        m_ref[...] = jnp.full((1, tq), NEG, F32)
        l_ref[...] = jnp.zeros((1, tq), F32)
        acc_ref[...] = jnp.zeros((VDIM, tq), F32)

        def update(kb0, nblk, masked):
            kb = k_ref[pl.ds(pl.multiple_of(kb0 * tq, tq), nblk * tq), :]
            st = lax.dot_general(kb, q_ref[...], NT, preferred_element_type=F32)
            if masked:
                key = lax.broadcasted_iota(jnp.int32, (nblk * tq, tq), 0) - (nblk - 1) * tq
                qry = lax.broadcasted_iota(jnp.int32, (nblk * tq, tq), 1)
                st = jnp.where(key <= qry, st, NEG)
            m_old = m_ref[...]
            m_new = jnp.maximum(m_old, jnp.max(st, axis=0, keepdims=True))
            alpha = jnp.exp2((m_old - m_new) * EXP2_SCALE)
            pt = jnp.exp2((st - m_new) * EXP2_SCALE)
            l_ref[...] = alpha * l_ref[...] + jnp.sum(pt, axis=0, keepdims=True)
            ptb = pt.astype(BF16)
            pv = lax.dot_general(vt_ref[kb0], ptb[:tq], NN, preferred_element_type=F32)
            for j in range(1, nblk):
                pv += lax.dot_general(vt_ref[kb0 + j], ptb[j * tq:(j + 1) * tq], NN, preferred_element_type=F32)
            acc_ref[...] = alpha * acc_ref[...] + pv
            m_ref[...] = m_new

        start = jnp.int32(0)
        for c in chunk_blocks:
            take = (qi & c) != 0

            @pl.when(take)
            def _(start=start, c=c):
                update(start, c, False)

            start = start + jnp.where(take, c, 0)
        if nq > 1:
            @pl.when((qi & 1) != 0)
            def _():
                update(qi - 1, 2, True)

            @pl.when((qi & 1) == 0)
            def _():
                update(qi, 1, True)
        else:
            update(qi, 1, True)
        l = l_ref[...]
        o_ref[...] = (acc_ref[...] / l).T.astype(o_ref.dtype)
        lse_ref[...] = m_ref[...] * EXP2_SCALE + jnp.log2(l)

    return _call(
        "flash_fwd", body, [q, k, vt],
        [pl.BlockSpec((tq, QPAD), lambda hh, i: (i, hh)),
         pl.BlockSpec((t, QPAD), lambda hh, i: (0, hh)),
         pl.BlockSpec((None, nq, VDIM, tq), lambda hh, i: (hh, 0, 0, 0))],
        [_sds((t, h * VDIM), BF16), _sds((h, nq, 1, tq), F32)],
        [pl.BlockSpec((tq, VDIM), lambda hh, i: (i, hh)),
         pl.BlockSpec((None, None, 1, tq), lambda hh, i: (hh, i, 0, 0))],
        [pltpu.VMEM((1, tq), F32), pltpu.VMEM((1, tq), F32), pltpu.VMEM((VDIM, tq), F32)], (h, nq), comm)


def _flash_bwd(q, k, v, o, do, lse, tabs, tq, comm=None):
    t = q.shape[0]
    h = q.shape[1] // QPAD
    nq = t // tq

    def body(q_ref, k_ref, v_ref, o_ref, do_ref, lse_ref, cc_ref, sa_ref, sb_ref, dq_ref, dk_out, dv_out, delta_ref, dqt_ref, dk_ref, dv_ref):
        kj = pl.program_id(1)

        @pl.when(kj == 0)
        def _():
            dqt_ref[...] = jnp.zeros_like(dqt_ref)
            ones = jnp.ones((8, VDIM), BF16)
            for qi in range(nq):
                rows = pl.ds(qi * tq, tq)
                prod = do_ref[rows, :].astype(F32) * o_ref[rows, :].astype(F32)
                hi = prod.astype(BF16)
                lo = (prod - hi.astype(F32)).astype(BF16)
                delta_ref[qi] = (lax.dot_general(ones, hi, NT, preferred_element_type=F32)
                                 + lax.dot_general(ones, lo, NT, preferred_element_type=F32))

        kb = k_ref[...]
        vb = v_ref[...]
        kbt = kb.astype(F32).T.astype(BF16)
        dk_ref[...] = jnp.zeros_like(dk_ref)
        dv_ref[...] = jnp.zeros_like(dv_ref)

        def step(q0, nblk, masked):
            rows = pl.ds(pl.multiple_of(q0 * tq, tq), nblk * tq)
            qb = q_ref[rows, :]
            dob = do_ref[rows, :]
            lse = jnp.concatenate([lse_ref[q0 + j] for j in range(nblk)], axis=1)
            delta = jnp.concatenate([delta_ref[q0 + j, pl.ds(0, 1), :] for j in range(nblk)], axis=1)
            st = lax.dot_general(kb, qb, NT, preferred_element_type=F32)
            pt = jnp.exp2(st * EXP2_SCALE - lse)
            if masked:
                key = lax.broadcasted_iota(jnp.int32, (tq, nblk * tq), 0)
                qry = lax.broadcasted_iota(jnp.int32, (tq, nblk * tq), 1)
                pt = jnp.where(key <= qry, pt, 0.0)
            dv_ref[...] += lax.dot_general(pt.astype(BF16), dob, NN, preferred_element_type=F32)
            dpt = lax.dot_general(vb, dob, NT, preferred_element_type=F32)
            dst = (pt * (dpt - delta) * ATTN_SCALE).astype(BF16)
            dk_ref[...] += lax.dot_general(dst, qb, NN, preferred_element_type=F32)
            dqt = lax.dot_general(kbt, dst, NN, preferred_element_type=F32)
            for j in range(nblk):
                dqt_ref[q0 + j] += dqt[:, j * tq:(j + 1) * tq]

        later = nq - 1 - kj
        if nq > 1:
            @pl.when((later & 1) != 0)
            def _():
                step(kj, 2, True)

            @pl.when((later & 1) == 0)
            def _():
                step(kj, 1, True)
        else:
            step(kj, 1, True)
        start = kj + 1 + (later & 1)
        for c in [c for c in (2, 4) if c < nq]:
            take = (later & c) != 0

            @pl.when(take)
            def _(start=start, c=c):
                step(start, c, False)

            start = start + jnp.where(take, c, 0)
        dk_out[...] = dk_ref[...].astype(BF16)
        dv_out[...] = dv_ref[...].astype(BF16)

        @pl.when(kj == nq - 1)
        def _():
            for qi in range(nq):
                rows = pl.ds(qi * tq, tq)
                d = dqt_ref[qi].T
                roped = _rope_bwd(d[:, NOPE:], cc_ref[rows, :], sa_ref[rows, :], sb_ref[rows, :])
                dq_ref[rows, :] = jnp.concatenate([d[:, :NOPE], roped], axis=1).astype(BF16)

    head_q = pl.BlockSpec((t, QPAD), lambda hh, j: (0, hh))
    head_v = pl.BlockSpec((t, VDIM), lambda hh, j: (0, hh))
    table = pl.BlockSpec((t, 128), lambda hh, j: (0, 0))
    return _call(
        "flash_bwd", body, [q, k, v, o, do, lse, *tabs],
        [head_q, pl.BlockSpec((tq, QPAD), lambda hh, j: (j, hh)), pl.BlockSpec((tq, VDIM), lambda hh, j: (j, hh)), head_v, head_v,
         pl.BlockSpec((None, nq, 1, tq), lambda hh, j: (hh, 0, 0, 0)), table, table, table],
        [_sds((t, h * QPAD), BF16), _sds((t, h * QPAD), BF16), _sds((t, h * VDIM), BF16)],
        [head_q, pl.BlockSpec((tq, QPAD), lambda hh, j: (j, hh)), pl.BlockSpec((tq, VDIM), lambda hh, j: (j, hh))],
        [pltpu.VMEM((nq, 8, tq), F32), pltpu.VMEM((nq, QPAD, tq), F32), pltpu.VMEM((tq, QPAD), F32), pltpu.VMEM((tq, VDIM), F32)], (h, nq), comm)


def _tril_bf16(w):
    row = lax.broadcasted_iota(jnp.int32, w.shape, 0)
    col = lax.broadcasted_iota(jnp.int32, w.shape, 1)
    return jnp.where(col <= row, w, 0.0).astype(BF16)


def _layer_norm_parts(v0):
    mu = jnp.mean(v0, axis=-1, keepdims=True)
    vc = v0 - mu
    rstd = lax.rsqrt(jnp.mean(vc * vc, axis=-1, keepdims=True) + LN_EPS)
    return vc * rstd, rstd


def _sgu_mid_fwd(ge, ln_g, ln_b, w_sp, b_sp, chunks_per_step):
    t, e2 = ge.shape
    e = e2 // 2
    gd = e // SGU_GROUPS
    rows = SGU_CHUNK * chunks_per_step

    def body(u_ref, v_ref, g_ref, b_ref, w_ref, bs_ref, gate_ref):
        for ck in range(chunks_per_step):
            r = pl.ds(ck * SGU_CHUNK, SGU_CHUNK)
            xh, _ = _layer_norm_parts(v_ref[r, :].astype(F32))
            v1 = (xh * g_ref[...] + b_ref[...]).astype(BF16)
            for g in range(SGU_GROUPS):
                cols = pl.ds(g * gd, gd)
                mixed = lax.dot_general(_tril_bf16(w_ref[g]), v1[:, g * gd:(g + 1) * gd], NN, preferred_element_type=F32) + bs_ref[g]
                gate_ref[r, cols] = (u_ref[r, cols].astype(F32) * mixed).astype(BF16)

    return _pcall(
        body, name="sgu_mid_fwd", grid=(t // rows,),
        in_specs=[pl.BlockSpec((rows, e), lambda i: (i, 0)), pl.BlockSpec((rows, e), lambda i: (i, 1)),
                  _const_spec((1, e)), _const_spec((1, e)), _const_spec(w_sp.shape), _const_spec(b_sp.shape)],
        out_specs=pl.BlockSpec((rows, e), lambda i: (i, 0)),
        out_shape=_sds((t, e), BF16), compiler_params=_params(1),
    )(ge, ge, ln_g, ln_b, w_sp, b_sp)


def _sgu_mid_bwd(ge, gp, dgate, ln_g, ln_b, w_sp, b_sp, chunks_per_step):
    t, e2 = ge.shape
    e = e2 // 2
    gd = e // SGU_GROUPS
    rows = SGU_CHUNK * chunks_per_step

    def body(u_ref, v_ref, zu_ref, zv_ref, dg_ref, g_ref, b_ref, w_ref, bs_ref, dz_ref, dw_ref, dbs_ref, dlg_ref, dlb_ref):
        @pl.when(pl.program_id(0) == 0)
        def _():
            dw_ref[...] = jnp.zeros_like(dw_ref)
            dbs_ref[...] = jnp.zeros_like(dbs_ref)
            dlg_ref[...] = jnp.zeros_like(dlg_ref)
            dlb_ref[...] = jnp.zeros_like(dlb_ref)

        for ck in range(chunks_per_step):
            r = pl.ds(ck * SGU_CHUNK, SGU_CHUNK)
            xh, rstd = _layer_norm_parts(v_ref[r, :].astype(F32))
            v1 = (xh * g_ref[...] + b_ref[...]).astype(BF16)
            dv1_parts = []
            for g in range(SGU_GROUPS):
                cols = pl.ds(g * gd, gd)
                wc = _tril_bf16(w_ref[g])
                v1g = v1[:, g * gd:(g + 1) * gd]
                mixed = lax.dot_general(wc, v1g, NN, preferred_element_type=F32) + bs_ref[g]
                dgate = dg_ref[r, cols].astype(F32)
                dmixed = dgate * u_ref[r, cols].astype(F32)
                du = dgate * mixed
                dz_ref[r, cols] = (du * zu_ref[r, cols].astype(F32)).astype(BF16)
                dbs_ref[g] += jnp.sum(dmixed, axis=1, keepdims=True)
                dmb = dmixed.astype(BF16)
                dwg = lax.dot_general(dmb, v1g, NT, preferred_element_type=F32)
                row = lax.broadcasted_iota(jnp.int32, dwg.shape, 0)
                col = lax.broadcasted_iota(jnp.int32, dwg.shape, 1)
                dw_ref[g] += jnp.where(col <= row, dwg, 0.0)
                dv1_parts.append(lax.dot_general(wc, dmb, TN, preferred_element_type=F32))
            dv1 = jnp.concatenate(dv1_parts, axis=1)
            dlg_ref[...] += jnp.sum(dv1 * xh, axis=0, keepdims=True)
            dlb_ref[...] += jnp.sum(dv1, axis=0, keepdims=True)
            dxh = dv1 * g_ref[...]
            dv0 = rstd * (dxh - jnp.mean(dxh, axis=-1, keepdims=True) - xh * jnp.mean(dxh * xh, axis=-1, keepdims=True))
            dz_ref[r, pl.ds(e, e)] = (dv0 * zv_ref[r, :].astype(F32)).astype(BF16)

    half0 = pl.BlockSpec((rows, e), lambda i: (i, 0))
    half1 = pl.BlockSpec((rows, e), lambda i: (i, 1))
    return _pcall(
        body, name="sgu_mid_bwd", grid=(t // rows,),
        in_specs=[half0, half1, half0, half1, half0, _const_spec((1, e)), _const_spec((1, e)), _const_spec(w_sp.shape), _const_spec(b_sp.shape)],
        out_specs=[pl.BlockSpec((rows, e2), lambda i: (i, 0)), _const_spec(w_sp.shape), _const_spec(b_sp.shape), _const_spec((1, e)), _const_spec((1, e))],
        out_shape=[_sds((t, e2), BF16), _sds(w_sp.shape, F32), _sds(b_sp.shape, F32), _sds((1, e), F32), _sds((1, e), F32)],
        compiler_params=_params(1),
    )(ge, ge, gp, gp, dgate, ln_g, ln_b, w_sp, b_sp)


def kernel(x, positions, norm_mix, norm_ffn, final_norm, mla_w_dkv, mla_q_norm, mla_kv_norm, mla_w_uq, mla_w_ukv, mla_w_o, sgu_w_in, sgu_ln_g, sgu_ln_b, sgu_w_spatial, sgu_b_spatial, sgu_w_out, ffn_w_up, ffn_w_down, loss_target, m_norm_mix, m_norm_ffn, m_final_norm, m_mla_w_dkv, m_mla_q_norm, m_mla_kv_norm, m_mla_w_uq, m_mla_w_ukv, m_mla_w_o, m_sgu_w_in, m_sgu_ln_g, m_sgu_ln_b, m_sgu_w_spatial, m_sgu_b_spatial, m_sgu_w_out, m_ffn_w_up, m_ffn_w_down, v_norm_mix, v_norm_ffn, v_final_norm, v_mla_w_dkv, v_mla_q_norm, v_mla_kv_norm, v_mla_w_uq, v_mla_w_ukv, v_mla_w_o, v_sgu_w_in, v_sgu_ln_g, v_sgu_ln_b, v_sgu_w_spatial, v_sgu_b_spatial, v_sgu_w_out, v_ffn_w_up, v_ffn_w_down):
    _, T, D = x.shape
    depth = norm_mix.shape[0]
    n_mla, n_sgu = mla_w_dkv.shape[0], sgu_w_in.shape[0]
    assert depth % 2 == 0
    FF = ffn_w_up.shape[2] * N_DEV
    E = sgu_w_out.shape[1] * N_DEV
    ffc, ec, e2c = FF // N_DEV, E // N_DEV, 2 * E // N_DEV
    dc = D // N_DEV
    OW = HEADS * VDIM
    HW = HEADS * QPAD
    owc = OW // N_DEV
    tm = _tile(T, 1024)
    tb = _tile(T, 4096)
    tk = _tile(T, 512)
    tq = _tile(T, 512)
    ts = _tile(T, 256)
    nt = T // tm
    x2 = x.reshape(T, D)
    tgt = loss_target.reshape(T, D)
    cidx = lax.axis_index("c").astype(jnp.int32).reshape(1)

    ln_local = jnp.concatenate([sgu_ln_g, sgu_ln_b, jnp.zeros((8 - 2 * n_sgu, ec), F32)], axis=0)
    mla_sh = [[w[l].astype(BF16) for w in (mla_w_dkv, mla_w_uq, mla_w_ukv, mla_w_o)] for l in range(n_mla)]

    def mla_layouts(g_dkv, g_uq, g_ukv, g_o):
        w_dkv = jnp.pad(g_dkv.reshape(1, D, LAT), ((0, 0), (0, 0), (0, LAT_PAD - LAT)))
        w_uq = jnp.pad(g_uq, ((0, 0), (0, 0), (0, QPAD - NOPE - ROPE))).transpose(1, 0, 2).reshape(1, Q_RANK, HEADS * QPAD)
        w_ukv = g_ukv.transpose(1, 0, 2).reshape(1, KV_RANK, HEADS * (NOPE + VDIM))
        return w_dkv, w_uq, w_ukv, g_o.reshape(1, HEADS * VDIM, D)

    mla_w = [None] * n_mla
    small_later = [a for l in range(1, n_mla) for a in mla_sh[l]] + [ln_local]
    ln_g_full, ln_b_full = [None] * n_sgu, [None] * n_sgu
    b_sp = sgu_b_spatial.reshape(n_sgu, SGU_GROUPS, SGU_CHUNK, 1)
    up_sh = [ffn_w_up[i].astype(BF16) for i in range(depth)]
    down_sh = [ffn_w_down[i].astype(BF16) for i in range(depth)]
    in_sh = [sgu_w_in[l].astype(BF16) for l in range(n_sgu)]
    out_sh = [sgu_w_out[l].astype(BF16) for l in range(n_sgu)]
    g_up, g_down, g_in, g_out = [None] * depth, [None] * depth, [None] * n_sgu, [None] * n_sgu

    inv_freq = ROPE_THETA ** (-jnp.arange(0, ROPE, 2, dtype=F32) / ROPE)
    zeros32 = jnp.zeros((ROPE // 2,), F32)
    inv128 = jnp.concatenate([inv_freq, inv_freq, zeros32, zeros32]).reshape(1, 128)
    sel_a = jnp.concatenate([-jnp.ones((32,), F32), zeros32, zeros32, zeros32]).reshape(1, 128)
    sel_b = jnp.concatenate([zeros32, jnp.ones((32,), F32), zeros32, zeros32]).reshape(1, 128)
    sel_c = jnp.concatenate([jnp.ones((64,), F32), zeros32, zeros32]).reshape(1, 128)

    def rope_tables(pos, inv, sa, sb, sc):
        ang = pos.astype(F32) * inv
        cs, sn = jnp.cos(ang), jnp.sin(ang)
        return cs * sc, sn * sa, sn * sb

    t_cc, t_sa, t_sb, *first_half = _rowwise(
        "rope_tables", rope_tables, [positions.reshape(T, 1), inv128, sel_a, sel_b, sel_c], grid=(nt,),
        in_specs=[_row_spec(tm, 1)] + [_const_spec((1, 128))] * 4,
        out_shapes=[_sds((T, 128), F32)] * 3, out_specs=[_row_spec(tm, 128)] * 3, comm=_gather_level1(mla_sh[0]))
    tab_specs = [_row_spec(tm, 128)] * 3

    def rmsnorm(xv, g, comm):
        return _rowwise("rmsnorm", lambda a, gg: _rms_fwd(a, gg), [xv, g.reshape(1, D)], grid=(nt,),
                        in_specs=[_row_spec(tm, D), _const_spec((1, D))], out_shapes=[_sds((T, D), BF16)], out_specs=[_row_spec(tm, D)], comm=comm)

    def proj_cols(name, h, gw, nc, epilogue, n_out, comm=None):
        return _matmul(name, h, gw, [], grid=(N_DEV, T // tb),
                       a_spec=pl.BlockSpec((tb, D), lambda j, i: (i, 0)),
                       b_spec=pl.BlockSpec((None, D, nc), lambda j, i: (j, 0, 0)), extra_specs=[],
                       out_shapes=[_sds((T, nc * N_DEV), BF16)] * n_out, out_specs=[pl.BlockSpec((tb, nc), lambda j, i: (i, j))] * n_out,
                       dims=NN, epilogue=epilogue, comm=comm)

    def residual_norm(acc, xr, g):
        xn = acc + xr
        return xn, _rms_fwd(xn, g)

    def proj_rows_residual(name, a, gw, xres, g_next, comm=None):
        kk_ = a.shape[1]
        return _matmul(name, a, gw.reshape(kk_, D), [xres, g_next.reshape(1, D)], grid=(T // tk,),
                       a_spec=_row_spec(tk, kk_), b_spec=_const_spec((kk_, D)), extra_specs=[_row_spec(tk, D), _const_spec((1, D))],
                       out_shapes=[_sds((T, D), F32), _sds((T, D), BF16)], out_specs=[_row_spec(tk, D)] * 2,
                       dims=NN, epilogue=residual_norm, comm=comm)

    def back_rows(name, dy, gw, kc, extras, epilogue, comm=None):
        return _matmul(name, dy, gw, extras, grid=(N_DEV, T // tb),
                       a_spec=pl.BlockSpec((tb, D), lambda j, i: (i, 0)),
                       b_spec=pl.BlockSpec((None, kc, D), lambda j, i: (j, 0, 0)),
                       extra_specs=[pl.BlockSpec((tb, kc), lambda j, i: (i, j))] * len(extras),
                       out_shapes=[_sds((T, kc * N_DEV), BF16)], out_specs=[pl.BlockSpec((tb, kc), lambda j, i: (i, j))],
                       dims=NT, epilogue=epilogue, comm=comm)

    def norm_bwd_epilogue(dh, xv, g, dxi):
        dxn, dg = _rms_bwd(xv, g, dh)
        return dxi + dxn, dxi + dxn, dg

    def transposed(gw):
        return gw.transpose(0, 2, 1).reshape(gw.shape[0] * gw.shape[2], D)

    def back_cols(name, da, gwt, xv, g, dx_in, comm=None):
        n = da.shape[1]
        row = _row_spec(tk, D)
        return _matmul(name, da, gwt, [xv, g.reshape(1, D), dx_in], grid=(T // tk,),
                       a_spec=_row_spec(tk, n), b_spec=_const_spec((n, D)), extra_specs=[row, _const_spec((1, D)), row],
                       out_shapes=[_sds((T, D), F32), _sds((T, D), BF16), _sds((1, D), F32)], out_specs=[row, row, _const_spec((1, D))],
                       dims=NN, epilogue=norm_bwd_epilogue, n_sum=1, comm=comm)

    def token_sum(tt):
        return dict(k_axis=1, nk=T // tt) if T // tt > 1 else dict(k_axis=None)

    def wgrad_cols(name, h, da, nc, comm=None):
        res = _matmul(name, h, da, [], grid=(N_DEV, T // tb),
                       a_spec=pl.BlockSpec((tb, D), lambda j, t: (t, 0)), b_spec=pl.BlockSpec((tb, nc), lambda j, t: (t, j)),
                       extra_specs=[], out_shapes=[_sds((N_DEV, D, nc), BF16)],
                       out_specs=[pl.BlockSpec((None, D, nc), lambda j, t: (j, 0, 0))],
                       dims=TN, acc_shape=(D, nc), comm=comm, **token_sum(tb))
        return res[0] if comm is None else res

    def wgrad_rows(name, a, dy, kc, ncols, tt, comm=None):
        res = _matmul(name, a, dy, [], grid=(a.shape[1] // kc, T // tt),
                      a_spec=pl.BlockSpec((tt, kc), lambda j, t: (t, j)), b_spec=pl.BlockSpec((tt, ncols), lambda j, t: (t, 0)),
                      extra_specs=[], out_shapes=[_sds((a.shape[1], ncols), BF16)],
                      out_specs=[pl.BlockSpec((kc, ncols), lambda j, t: (j, 0))],
                      dims=TN, acc_shape=(kc, ncols), comm=comm, **token_sum(tt))
        return res[0] if comm is None else res

    saved = []
    xs = x2
    for i in range(depth):
        l = i // 2
        if i == 0:
            h, *first_w = rmsnorm(xs, norm_mix[0], _gather_level2(first_half))
            mla_w[0] = mla_layouts(*first_w)
        if i % 2 == 0:
            w_dkv, w_uq, w_ukv, w_o = mla_w[l]
            def latent_post(la, qn, kvn, cc, sa, sb):
                cq = _rms_fwd(la[:, :Q_RANK], qn)
                ckv = _rms_fwd(la[:, Q_RANK:Q_RANK + KV_RANK], kvn)
                kr = _rope_fwd(la[:, Q_RANK + KV_RANK:], cc, sa, sb)
                return la, cq, ckv, kr

            lat, cq, ckv, kr = _matmul(
                "mla_down", h, w_dkv, [mla_q_norm[l].reshape(1, Q_RANK), mla_kv_norm[l].reshape(1, KV_RANK), t_cc, t_sa, t_sb],
                grid=(nt,), a_spec=_row_spec(tm, D), b_spec=pl.BlockSpec((None, D, LAT_PAD), lambda i_: (0, 0, 0)),
                extra_specs=[_const_spec((1, Q_RANK)), _const_spec((1, KV_RANK))] + tab_specs,
                out_shapes=[_sds((T, LAT_PAD), F32), _sds((T, Q_RANK), BF16), _sds((T, KV_RANK), BF16), _sds((T, 128), BF16)],
                out_specs=[_row_spec(tm, LAT_PAD), _row_spec(tm, Q_RANK), _row_spec(tm, KV_RANK), _row_spec(tm, 128)], dims=NN,
                epilogue=latent_post)

            def q_epilogue(acc, cc, sa, sb):
                parts = []
                for b in range(HEADS):
                    parts += [acc[:, b * QPAD:b * QPAD + NOPE], _rope_fwd(acc[:, b * QPAD + NOPE:(b + 1) * QPAD], cc, sa, sb)]
                return (jnp.concatenate(parts, axis=1),)

            q = _matmul("mla_q", cq, w_uq, [t_cc, t_sa, t_sb], grid=(nt,), a_spec=_row_spec(tm, Q_RANK),
                        b_spec=pl.BlockSpec((None, Q_RANK, HW), lambda i_: (0, 0, 0)), extra_specs=tab_specs,
                        out_shapes=[_sds((T, HW), BF16)], out_specs=[_row_spec(tm, HW)], dims=NN, epilogue=q_epilogue)[0]

            def kv_write(outs, acc, krb):
                k_ref, v_ref, vt_ref = outs
                for b in range(HEADS):
                    vb = acc[:, b * QPAD + NOPE:(b + 1) * QPAD]
                    k_ref[:, b * QPAD:b * QPAD + NOPE] = acc[:, b * QPAD:b * QPAD + NOPE].astype(BF16)
                    k_ref[:, b * QPAD + NOPE:(b + 1) * QPAD] = krb
                    v_ref[:, b * VDIM:(b + 1) * VDIM] = vb.astype(BF16)
                    vbt = vb.T.astype(BF16)
                    for u in range(tm // tq):
                        vt_ref[b, u] = vbt[:, u * tq:(u + 1) * tq]

            kk, vv, vt = _matmul("mla_kv", ckv, w_ukv, [kr], grid=(nt,), a_spec=_row_spec(tm, KV_RANK),
                                 b_spec=pl.BlockSpec((None, KV_RANK, HW), lambda i_: (0, 0, 0)), extra_specs=[_row_spec(tm, 128)],
                                 out_shapes=[_sds((T, HW), BF16), _sds((T, OW), BF16), _sds((HEADS, T // tq, VDIM, tq), BF16)],
                                 out_specs=[_row_spec(tm, HW), _row_spec(tm, OW), pl.BlockSpec((HEADS, tm // tq, VDIM, tq), lambda i_: (0, i_, 0, 0))],
                                 dims=NN, write=kv_write)
            group = [up_sh[i], down_sh[i], in_sh[l], out_sh[l]] + (small_later if i == 0 else [])
            o, lse, *bufs = _flash_fwd(q, kk, vt, tq, comm=_gather_level1(group))
            xm, h2, g_up[i], g_down[i] = _matmul(
                "mla_out", o, w_o, [xs, norm_ffn[i].reshape(1, D)], grid=(nt,), a_spec=_row_spec(tm, OW),
                b_spec=pl.BlockSpec((None, OW, D), lambda i_: (0, 0, 0)), extra_specs=[_row_spec(tm, D), _const_spec((1, D))],
                out_shapes=[_sds((T, D), F32), _sds((T, D), BF16)], out_specs=[_row_spec(tm, D)] * 2, dims=NN,
                epilogue=residual_norm, comm=_gather_level2(bufs[:2]))
            half_gathered = bufs[2:]
            mix_saved = (h, lat, cq, ckv, q, kk, vv, o, lse)
        else:
            gp, ge, g_down[i], up_half = proj_cols("sgu_in", h, g_in[l], e2c, _gelu_and_grad, 2,
                                                   comm=_merge_comm(_gather_level2([down_half]), _gather_level1([up_sh[i]])))
            gate = _sgu_mid_fwd(ge, ln_g_full[l], ln_b_full[l], sgu_w_spatial[l], b_sp[l], 4)
            xm, h2, g_up[i] = proj_rows_residual("sgu_out", gate, g_out[l], xs, norm_ffn[i], comm=_gather_level2([up_half]))
            mix_saved = (h, gp, ge, gate)
        r, s, *rest = proj_cols("ffn_up", h2, g_up[i], ffc, lambda acc: (jnp.maximum(acc, 0.0), jnp.square(jnp.maximum(acc, 0.0))), 2,
                                comm=_gather_level2(half_gathered) if i % 2 == 0 else None)
        if i % 2 == 0:
            g_in[l], g_out[l], *small_gathered = rest
        if i == 0:
            for l_ in range(1, n_mla):
                mla_w[l_] = mla_layouts(*small_gathered[4 * (l_ - 1):4 * l_])
            g_ln = small_gathered[-1]
            ln_g_full = [g_ln[:, l_, :].reshape(1, E) for l_ in range(n_sgu)]
            ln_b_full = [g_ln[:, n_sgu + l_, :].reshape(1, E) for l_ in range(n_sgu)]
        xo, h_next, *rest = proj_rows_residual("ffn_down", s, g_down[i], xm, norm_mix[i + 1] if i + 1 < depth else final_norm,
                                               comm=_gather_level1([down_sh[i + 1]]) if i % 2 == 0 else None)
        if i % 2 == 0:
            (down_half,) = rest
        saved.append((xs, xm, mix_saved, h2, r, s))
        xs, h = xo, h_next

    def loss_head(xv, tg, g):
        y = _rms_fwd(xv, g)
        err = y - tg
        part = 0.5 * jnp.sum(jnp.sum(err * err, axis=-1, keepdims=True), axis=0, keepdims=True) / D
        dx, dg = _rms_bwd(xv, g, err / D)
        return dx, dx, jnp.broadcast_to(part, (1, 128)), dg

    dx, dyb, loss_part, d_final = _rowwise(
        "loss_head", loss_head, [xs, tgt, final_norm.reshape(1, D)], grid=(nt,),
        in_specs=[_row_spec(tm, D), _row_spec(tm, D), _const_spec((1, D))],
        out_shapes=[_sds((T, D), F32), _sds((T, D), BF16), _sds((1, 128), F32), _sds((1, D), F32)],
        out_specs=[_row_spec(tm, D), _row_spec(tm, D), _const_spec((1, 128)), _const_spec((1, D))], n_acc=2)
    loss = lax.psum(loss_part[0, 0], ("x", "y", "c"))

    d_norm_mix, d_norm_ffn = [None] * depth, [None] * depth
    d_qn, d_kvn = [None] * n_mla, [None] * n_mla
    d_wsp, d_bsp, d_lng, d_lnb = [None] * n_sgu, [None] * n_sgu, [None] * n_sgu, [None] * n_sgu
    layers = {"dkv": n_mla, "uq": n_mla, "ukv": n_mla, "o": n_mla, "in": n_sgu, "out": n_sgu, "up": depth, "down": depth}
    stacked = {nm: None for nm in layers}
    pending = []
    summed = []

    def add_pairs(gs, rcvs):
        operands, in_specs, out_shapes, out_specs = [], [], [], []
        for g, rcv in zip(gs, rcvs):
            _, rws, cls = g.shape
            slab = pl.BlockSpec((None, rws, cls), lambda ch, cr: (ch, 0, 0))
            operands += [g.reshape(N_CHIP, 2, rws, cls), rcv]
            in_specs += [pl.BlockSpec((None, None, rws, cls), lambda ch, cr: (ch, cr[0], 0, 0)), slab]
            out_shapes.append(_sds(rcv.shape, BF16))
            out_specs.append(slab)

        def fn(*blocks):
            return tuple(blocks[2 * k].astype(F32) + blocks[2 * k + 1].astype(F32) for k in range(len(gs)))

        return _rowwise("grad_pair_sum", fn, operands, grid=(N_CHIP,), in_specs=in_specs, out_shapes=out_shapes, out_specs=out_specs,
                        grid_spec_prefetch=cidx)

    def sibling_comm():
        return _sibling_exchange([g for _, _, g in pending]) if pending else None

    def absorb(from_sibling):
        if pending:
            parts = add_pairs([g for _, _, g in pending], list(from_sibling))
            summed.extend((nm, l_, p) for (nm, l_, _), p in zip(pending, parts))
            pending.clear()

    def chip_comm():
        if pending:
            absorb(_comm_call("grad_sibling_exchange", sibling_comm()))
        comm, names = _chip_exchange([p for _, _, p in summed], [(nm, l_) for nm, l_, _ in summed], layers, stacked)
        summed.clear()
        return comm, names

    def rows128(a, rows):
        flat = a.reshape(-1, 128)
        return jnp.pad(flat, ((0, rows - flat.shape[0]), (0, 0)))

    def pad_to(n, mult):
        return -(-n // mult) * mult

    def packed(arrs, sizes):
        return jnp.concatenate([rows128(a, sz) for a, sz in zip(arrs, sizes)], axis=0)

    n_wsp, n_bsp, n_ln = sgu_w_spatial.size // 128, pad_to(sgu_b_spatial.size // 128, 8), pad_to(n_sgu * E // 128, 8)
    early_sizes = [n_wsp, pad_to(n_wsp + n_bsp, SMALL_ROWS) - n_wsp, n_ln, n_ln]
    early_rep = early_sizes[0] + early_sizes[1]
    gathered_early = None

    for i in reversed(range(depth)):
        l = i // 2
        xs_i, xm, mix_saved, h2, r, s = saved[i]
        comm = sibling_comm()
        if i == 0:
            comm = _gather_level2([early_half]) if comm is None else _merge_comm(comm, _gather_level2([early_half]))
        da, *rcv = back_rows("ffn_down_bwd", dyb, g_down[i], ffc, [r], lambda acc, rr: (acc * (2.0 * rr.astype(F32)),), comm=comm)
        if i == 0:
            *rcv, gathered_early = rcv
        absorb(rcv)
        if i == 0 and any(nm == "down" for nm, _, _ in summed):
            nm_, l_, part = summed.pop([nm for nm, _, _ in summed].index("down"))
            comm, names = _chip_exchange([part], [(nm_, l_)], layers, stacked)
            g_down_i, *bufs = wgrad_rows("ffn_down_wgrad", s, dyb, ffc, D, tb, comm=comm)
            stacked.update(dict(zip(names, bufs)))
        else:
            g_down_i = wgrad_rows("ffn_down_wgrad", s, dyb, ffc, D, tb)
        pending.append(("down", i, g_down_i.reshape(N_DEV, ffc, D)))
        pending.append(("up", i, wgrad_cols("ffn_up_wgrad", h2, da, ffc)))
        if i % 2 == 0:
            dx, dyb, d_norm_ffn[i], *rcv = back_cols("ffn_up_bwd", da, transposed(g_up[i]), xm, norm_ffn[i], dx, comm=sibling_comm())
            absorb(rcv)
        else:
            dx, dyb, d_norm_ffn[i] = back_cols("ffn_up_bwd", da, transposed(g_up[i]), xm, norm_ffn[i], dx)
        if i % 2 == 0:
            h, lat, cq, ckv, q, kk, vv, o, lse = mix_saved
            w_dkv, w_uq, w_ukv, w_o = mla_w[l]
            do = _matmul("mla_out_bwd", dyb, w_o, [], grid=(nt,), a_spec=_row_spec(tm, D),
                         b_spec=pl.BlockSpec((None, OW, D), lambda i_: (0, 0, 0)), extra_specs=[],
                         out_shapes=[_sds((T, OW), BF16)], out_specs=[_row_spec(tm, OW)], dims=NT)[0]
            g_o_l = wgrad_rows("mla_out_wgrad", o, dyb, OW, D, tm).reshape(N_DEV, owc, D)
            comm, names = chip_comm()
            dq_pre, dk, dv, *bufs = _flash_bwd(q, kk, vv, o, do, lse, (t_cc, t_sa, t_sb), tq, comm=comm)
            stacked.update(dict(zip(names, bufs)))
            pending.append(("o", l, g_o_l))

            def kv_pre(dkb, dvb, cc, sa, sb):
                parts, dkr = [], None
                for b in range(HEADS):
                    parts += [dkb[:, b * QPAD:b * QPAD + NOPE], dvb[:, b * VDIM:(b + 1) * VDIM]]
                    piece = dkb[:, b * QPAD + NOPE:(b + 1) * QPAD].astype(F32)
                    dkr = piece if dkr is None else dkr + piece
                return jnp.concatenate(parts, axis=1), _rope_bwd(dkr, cc, sa, sb)

            dkv, dkr = _rowwise("mla_dkv_rope", kv_pre, [dk, dv, t_cc, t_sa, t_sb], grid=(T // ts,),
                                in_specs=[_row_spec(ts, HW), _row_spec(ts, OW)] + [_row_spec(ts, 128)] * 3,
                                out_shapes=[_sds((T, HW), BF16), _sds((T, 128), F32)], out_specs=[_row_spec(ts, HW), _row_spec(ts, 128)])
            g_uq_l = wgrad_rows("mla_q_wgrad", cq, dq_pre, Q_RANK, HW, tm)
            g_ukv_l = wgrad_rows("mla_kv_wgrad", ckv, dkv, KV_RANK, HW, tm)
            pending.append(("uq", l, g_uq_l.reshape(Q_RANK, HEADS, QPAD)[:, :, :NOPE + ROPE].transpose(1, 0, 2)))
            pending.append(("ukv", l, g_ukv_l.reshape(KV_RANK, HEADS, NOPE + VDIM).transpose(1, 0, 2)))
            dcq = _matmul("mla_q_bwd", dq_pre, w_uq, [], grid=(nt,), a_spec=_row_spec(tm, HW),
                          b_spec=pl.BlockSpec((None, Q_RANK, HW), lambda i_: (0, 0, 0)), extra_specs=[],
                          out_shapes=[_sds((T, Q_RANK), F32)], out_specs=[_row_spec(tm, Q_RANK)], dims=NT)[0]
            def latent_bwd(dkv_, la, qn, kvn, dq_, dkr_):
                dcq_raw, dqn = _rms_bwd(la[:, :Q_RANK], qn, dq_)
                dckv_raw, dkvn = _rms_bwd(la[:, Q_RANK:Q_RANK + KV_RANK], kvn, dkv_)
                return jnp.concatenate([dcq_raw, dckv_raw, dkr_], axis=1), dqn, dkvn

            dlat, d_qn[l], d_kvn[l] = _matmul(
                "mla_kv_bwd", dkv, w_ukv, [lat, mla_q_norm[l].reshape(1, Q_RANK), mla_kv_norm[l].reshape(1, KV_RANK), dcq, dkr],
                grid=(nt,), a_spec=_row_spec(tm, HW), b_spec=pl.BlockSpec((None, KV_RANK, HW), lambda i_: (0, 0, 0)),
                extra_specs=[_row_spec(tm, LAT_PAD), _const_spec((1, Q_RANK)), _const_spec((1, KV_RANK)), _row_spec(tm, Q_RANK), _row_spec(tm, 128)],
                out_shapes=[_sds((T, LAT_PAD), BF16), _sds((1, Q_RANK), F32), _sds((1, KV_RANK), F32)],
                out_specs=[_row_spec(tm, LAT_PAD), _const_spec((1, Q_RANK)), _const_spec((1, KV_RANK))], dims=NT,
                epilogue=latent_bwd, n_sum=2)
            g_dkv_l = wgrad_rows("mla_down_wgrad", h, dlat, D, LAT_PAD, tm)
            pending.append(("dkv", l, g_dkv_l[:, :LAT].reshape(N_DEV, dc, LAT)))
            dx, dyb, d_norm_mix[i] = _matmul(
                "mla_down_bwd", dlat, w_dkv, [xs_i, norm_mix[i].reshape(1, D), dx], grid=(nt,), a_spec=_row_spec(tm, LAT_PAD),
                b_spec=pl.BlockSpec((None, D, LAT_PAD), lambda i_: (0, 0, 0)), extra_specs=[_row_spec(tm, D), _const_spec((1, D)), _row_spec(tm, D)],
                out_shapes=[_sds((T, D), F32), _sds((T, D), BF16), _sds((1, D), F32)],
                out_specs=[_row_spec(tm, D), _row_spec(tm, D), _const_spec((1, D))], dims=NT, epilogue=norm_bwd_epilogue, n_sum=1)
        else:
            h, gp, ge, gate = mix_saved
            (dgate,) = back_rows("sgu_out_bwd", dyb, g_out[l], ec, [], None)
            pending.append(("out", l, wgrad_rows("sgu_out_wgrad", gate, dyb, ec, D, tb).reshape(N_DEV, ec, D)))
            dz, d_wsp[l], d_bsp[l], d_lng[l], d_lnb[l] = _sgu_mid_bwd(ge, gp, dgate, ln_g_full[l], ln_b_full[l], sgu_w_spatial[l], b_sp[l], 2)
            g_in_l, *rcv = wgrad_cols("sgu_in_wgrad", h, dz, e2c, comm=sibling_comm())
            absorb(rcv)
            pending.append(("in", l, g_in_l))
            comm = sibling_comm()
            if i == 1:
                early = packed([jnp.stack(d_wsp, 0), jnp.stack(d_bsp, 0), jnp.concatenate(d_lng, 0), jnp.concatenate(d_lnb, 0)], early_sizes)
                comm = _merge_comm(comm, _gather_level1([early]))
            dx, dyb, d_norm_mix[i], *rcv = back_cols("sgu_in_bwd", dz, transposed(g_in[l]), xs_i, norm_mix[i], dx, comm=comm)
            if i == 1:
                *rcv, early_half = rcv
            absorb(rcv)
    grad_x = dx.reshape(1, T, D)

    last_comm, last_names = chip_comm()
    late_g = [jnp.concatenate(d_norm_mix, 0), jnp.concatenate(d_norm_ffn, 0), d_final, jnp.concatenate(d_qn, 0), jnp.concatenate(d_kvn, 0)]
    late_w = [norm_mix, norm_ffn, final_norm, mla_q_norm, mla_kv_norm]
    late_m = [m_norm_mix, m_norm_ffn, m_final_norm, m_mla_q_norm, m_mla_kv_norm]
    late_v = [v_norm_mix, v_norm_ffn, v_final_norm, v_mla_q_norm, v_mla_kv_norm]
    late_sizes = [pad_to(g.size // 128, 8) for g in late_g]
    late_rows = sum(late_sizes)

    def adam_big(parts, w, m, v):
        lyr, rws, cls = w.shape
        rt = _tile(rws, 512)

        def fn(p, w_, m_, v_):
            g = (p[0].astype(F32) + p[1].astype(F32)) + (p[2].astype(F32) + p[3].astype(F32))
            return (g, *_adam(w_, g, m_, v_))

        spec = pl.BlockSpec((None, rt, cls), lambda l_, i_: (l_, i_, 0))
        return _rowwise("adam_large", fn, [parts, w, m, v], grid=(lyr, rws // rt),
                        in_specs=[pl.BlockSpec((N_CHIP, None, rt, cls), lambda l_, i_: (0, l_, i_, 0)), spec, spec, spec],
                        out_shapes=[_sds(w.shape, F32)] * 4, out_specs=[spec] * 4)

    stacked.update(dict(zip(last_names, _comm_call("grad_chip_exchange", last_comm))))
    (gathered_late,) = _all_gather("gather_small_grads", [packed(late_g, late_sizes)])
    big = {}
    big["in"] = adam_big(stacked["in"], sgu_w_in, m_sgu_w_in, v_sgu_w_in)
    big["up"] = adam_big(stacked["up"], ffn_w_up, m_ffn_w_up, v_ffn_w_up)
    big["down"] = adam_big(stacked["down"], ffn_w_down, m_ffn_w_down, v_ffn_w_down)
    big["out"] = adam_big(stacked["out"], sgu_w_out, m_sgu_w_out, v_sgu_w_out)
    big["dkv"] = adam_big(stacked["dkv"], mla_w_dkv, m_mla_w_dkv, v_mla_w_dkv)
    big["uq"] = adam_big(stacked["uq"], mla_w_uq, m_mla_w_uq, v_mla_w_uq)
    big["ukv"] = adam_big(stacked["ukv"], mla_w_ukv, m_mla_w_ukv, v_mla_w_ukv)
    big["o"] = adam_big(stacked["o"], mla_w_o, m_mla_w_o, v_mla_w_o)
    big_res = [big[nm][:4] for nm in ("dkv", "uq", "ukv", "o", "in", "out", "up", "down")]

    def sum8(p):
        return ((p[0] + p[1]) + (p[2] + p[3])) + ((p[4] + p[5]) + (p[6] + p[7]))

    def adam_packed(name, gathered, ws, ms, vs, sizes, rows, tile):
        spec = _row_spec(tile, 128)
        return _rowwise(name, lambda p, w_, m_, v_: (sum8(p), *_adam(w_, sum8(p), m_, v_)),
                        [gathered, packed(ws, sizes), packed(ms, sizes), packed(vs, sizes)], grid=(rows // tile,),
                        in_specs=[pl.BlockSpec((N_DEV, tile, 128), lambda i_: (0, i_, 0)), spec, spec, spec],
                        out_shapes=[_sds((rows, 128), F32)] * 4, out_specs=[spec] * 4)

    late_res = adam_packed("adam_small", gathered_late, late_w, late_m, late_v, late_sizes, late_rows, late_rows)
    early_res = adam_packed("adam_spatial", gathered_early, [sgu_w_spatial, sgu_b_spatial], [m_sgu_w_spatial, m_sgu_b_spatial],
                            [v_sgu_w_spatial, v_sgu_b_spatial], early_sizes[:2], early_rep, SMALL_ROWS)

    def unpack(res, sizes, k, like):
        off = sum(sizes[:k])
        return res[off:off + like.size // 128].reshape(like.shape)

    my_b = 4 * lax.axis_index("x") + 2 * lax.axis_index("y") + lax.axis_index("c")
    ln_w = jnp.concatenate([sgu_ln_g, sgu_ln_b], 0)
    ln_m = jnp.concatenate([m_sgu_ln_g, m_sgu_ln_b], 0)
    ln_v = jnp.concatenate([v_sgu_ln_g, v_sgu_ln_b], 0)
    ln_all = jnp.concatenate([gathered_early[:, early_rep:early_rep + n_sgu * E // 128], gathered_early[:, early_rep + n_ln:early_rep + n_ln + n_sgu * E // 128]], axis=1)
    ln_mine = lax.dynamic_slice_in_dim(ln_all.reshape(N_DEV, 2 * n_sgu, N_DEV, ec), my_b, 1, axis=2).reshape(N_DEV, 2 * n_sgu, ec)
    ln_g_, ln_d, ln_m2, ln_v2 = _rowwise(
        "adam_ln", lambda p, w_, m_, v_: (sum8(p), *_adam(w_, sum8(p), m_, v_)), [ln_mine, ln_w, ln_m, ln_v], grid=(1,),
        in_specs=[_const_spec(ln_mine.shape), _const_spec(ln_w.shape), _const_spec(ln_w.shape), _const_spec(ln_w.shape)],
        out_shapes=[_sds(ln_w.shape, F32)] * 4, out_specs=[_const_spec(ln_w.shape)] * 4)

    def family(pos):
        ln = [ln_g_, ln_d, ln_m2, ln_v2][pos]
        late = [unpack(late_res[pos], late_sizes, k, w_) for k, w_ in enumerate(late_w)]
        w_sp_, b_sp_ = unpack(early_res[pos], early_sizes, 0, sgu_w_spatial), unpack(early_res[pos], early_sizes, 1, sgu_b_spatial)
        bigs = [res[pos] for res in big_res]
        return [late[0], late[1], late[2], bigs[0], late[3], late[4], bigs[1], bigs[2], bigs[3],
                bigs[4], ln[:n_sgu], ln[n_sgu:], w_sp_, b_sp_, bigs[5], bigs[6], bigs[7]]

    return (loss, grad_x, *family(0), *family(1), *family(2), *family(3))
```

```python
import math

import jax
import jax.numpy as jnp
from jax import lax
from jax.experimental import pallas as pl
from jax.experimental.pallas import tpu as pltpu

F32 = jnp.float32
BF16 = jnp.bfloat16
MESH = pl.DeviceIdType.MESH

N_DEV = 8
N_CHIP = 4
HEADS = 8
NOPE = 128
ROPE = 64
VDIM = 128
QPAD = 256
Q_RANK = 256
KV_RANK = 128
LAT = Q_RANK + KV_RANK + ROPE
LAT_PAD = 512
ROPE_THETA = 10000.0
SGU_CHUNK = 128
SGU_GROUPS = 8
NORM_EPS = 1e-6
LN_EPS = 1e-5
ADAM_LR = 0.001
ADAM_B1 = 0.9
ADAM_B2 = 0.999
ADAM_EPS = 1e-08
ADAM_WD = 0.01
ADAM_STEP = 10
ATTN_SCALE = (NOPE + ROPE) ** -0.5
NEG = -1e30
EXP2_SCALE = ATTN_SCALE * math.log2(math.e)
VMEM_LIMIT = 56 * 1024 * 1024
SMALL_ROWS = 256

NN = (((1,), (0,)), ((), ()))
NT = (((1,), (1,)), ((), ()))
TN = (((0,), (0,)), ((), ()))
ANY = pl.BlockSpec(memory_space=pl.ANY)


def _pcall(body, **kw):
    return pl.pallas_call(body, **kw)


def _params(n_grid, side_effects=False):
    return pltpu.CompilerParams(dimension_semantics=("arbitrary",) * n_grid, vmem_limit_bytes=VMEM_LIMIT, has_side_effects=side_effects)


def _sds(shape, dtype):
    return jax.ShapeDtypeStruct(tuple(shape), dtype)


def _tile(n, want):
    t = min(n, want)
    assert n % t == 0, (n, want)
    return t


class _Comm:
    def __init__(self, operands, out_shapes, aliases, scratch, start, finish):
        self.operands, self.out_shapes, self.aliases, self.scratch = operands, out_shapes, aliases, scratch
        self.start, self.finish = start, finish


def _merge_comm(first, second):
    n_in, n_out, n_sc = len(first.operands), len(first.out_shapes), len(first.scratch)
    aliases = dict(first.aliases)
    aliases.update({n_in + k: n_out + v for k, v in second.aliases.items()})

    def start(ins, outs, sems):
        first.start(ins[:n_in], outs[:n_out], sems[:n_sc])
        second.start(ins[n_in:], outs[n_out:], sems[n_sc:])

    def finish(ins, outs, sems):
        first.finish(ins[:n_in], outs[:n_out], sems[:n_sc])
        second.finish(ins[n_in:], outs[n_out:], sems[n_sc:])

    return _Comm([*first.operands, *second.operands], [*first.out_shapes, *second.out_shapes], aliases,
                 [*first.scratch, *second.scratch], start, finish)


def _place():
    return lax.axis_index("x"), lax.axis_index("y"), lax.axis_index("c")


def _other_chips(x, y):
    return [(1 - x, y), (x, 1 - y), (1 - x, 1 - y)]


def _dev_index(dev):
    return 4 * dev[0] + 2 * dev[1] + dev[2]


def _comm_call(name, comm):
    c_in, c_out = len(comm.operands), len(comm.out_shapes)

    def body(*refs):
        ins, outs, sems = refs[:c_in], refs[c_in:c_in + c_out], refs[c_in + c_out:]
        comm.start(ins, outs, sems)
        comm.finish(ins, outs, sems)

    return _pcall(body, name=name, in_specs=[ANY] * c_in, out_specs=[ANY] * c_out, out_shape=comm.out_shapes,
                  scratch_shapes=comm.scratch, input_output_aliases=dict(comm.aliases),
                  compiler_params=pltpu.CompilerParams(has_side_effects=True))(*comm.operands)


def _call(name, body, operands, in_specs, out_shapes, out_specs, scratch, grid, comm=None):
    if comm is None:
        return _pcall(body, name=name, grid=grid, in_specs=in_specs, out_specs=out_specs, out_shape=out_shapes,
                      scratch_shapes=scratch, compiler_params=_params(len(grid)))(*operands)
    n_in, n_out, n_sc = len(operands), len(out_shapes), len(scratch)
    c_in, c_out = len(comm.operands), len(comm.out_shapes)

    def hosted(*refs):
        ins, cins = refs[:n_in], refs[n_in:n_in + c_in]
        o0 = n_in + c_in
        outs, couts = refs[o0:o0 + n_out], refs[o0 + n_out:o0 + n_out + c_out]
        rest = refs[o0 + n_out + c_out:]
        sc, csems = rest[:n_sc], rest[n_sc:]
        first = pl.program_id(0) == 0
        last = pl.program_id(0) == grid[0] - 1
        for d in range(1, len(grid)):
            first = jnp.logical_and(first, pl.program_id(d) == 0)
            last = jnp.logical_and(last, pl.program_id(d) == grid[d] - 1)

        @pl.when(first)
        def _():
            comm.start(cins, couts, csems)

        body(*ins, *outs, *sc)

        @pl.when(last)
        def _():
            comm.finish(cins, couts, csems)

    return _pcall(hosted, name=name, grid=grid, in_specs=[*in_specs, *[ANY] * c_in], out_specs=[*out_specs, *[ANY] * c_out],
                  out_shape=[*out_shapes, *comm.out_shapes], scratch_shapes=[*scratch, *comm.scratch],
                  input_output_aliases={n_in + k: n_out + v for k, v in comm.aliases.items()},
                  compiler_params=_params(len(grid), side_effects=True))(*operands, *comm.operands)


def _gather_level1(shards):
    n = len(shards)

    def copies(ins, outs, sems):
        send_sems, recv_sems, local_sems = sems
        x, y, c = _place()
        me, sibling = (x, y, c), (x, y, 1 - c)
        chips = _other_chips(x, y)

        def copy(a, k, block, to, src=None):
            slot = outs[a].at[_dev_index(block)]
            return pltpu.make_async_remote_copy(src_ref=slot if src is None else src, dst_ref=slot, send_sem=send_sems.at[a, k],
                                                recv_sem=recv_sems.at[a, k], device_id=to, device_id_type=MESH)

        mine = [pltpu.make_async_copy(ins[a], outs[a].at[_dev_index(me)], local_sems.at[a]) for a in range(n)]
        sends = [copy(a, 1 + j, me, (*chip, c), src=ins[a]) for j, chip in enumerate(chips) for a in range(n)]
        sends += [copy(a, 0, me, sibling, src=ins[a]) for a in range(n)]
        recvs = [copy(a, 1 + j, (*chip, c), me) for j, chip in enumerate(chips) for a in range(n)]
        recvs += [copy(a, 0, sibling, me) for a in range(n)]
        return mine, sends, recvs

    def start(ins, outs, sems):
        mine, sends, _ = copies(ins, outs, sems)
        for cp in mine + sends:
            cp.start()

    def finish(ins, outs, sems):
        mine, sends, recvs = copies(ins, outs, sems)
        for cp in recvs:
            cp.wait_recv()
        for cp in sends:
            cp.wait_send()
        for cp in mine:
            cp.wait()

    return _Comm(shards, [_sds((N_DEV, *a.shape), a.dtype) for a in shards], {},
                 [pltpu.SemaphoreType.DMA((n, 4)), pltpu.SemaphoreType.DMA((n, 4)), pltpu.SemaphoreType.DMA((n,))], start, finish)


def _gather_level2(bufs):
    n = len(bufs)

    def copies(outs, sems):
        send_sems, recv_sems = sems
        x, y, c = _place()
        sibling = (x, y, 1 - c)
        sends, recvs = [], []
        for j, chip in enumerate(_other_chips(x, y)):
            for a in range(n):
                have, want = outs[a].at[_dev_index((*chip, c))], outs[a].at[_dev_index((*chip, 1 - c))]
                sends.append(pltpu.make_async_remote_copy(src_ref=have, dst_ref=have, send_sem=send_sems.at[a, j], recv_sem=recv_sems.at[a, j],
                                                          device_id=sibling, device_id_type=MESH))
                recvs.append(pltpu.make_async_remote_copy(src_ref=want, dst_ref=want, send_sem=send_sems.at[a, j], recv_sem=recv_sems.at[a, j],
                                                          device_id=sibling, device_id_type=MESH))
        return sends, recvs

    def start(ins, outs, sems):
        for cp in copies(outs, sems)[0]:
            cp.start()

    def finish(ins, outs, sems):
        sends, recvs = copies(outs, sems)
        for cp in recvs:
            cp.wait_recv()
        for cp in sends:
            cp.wait_send()

    return _Comm(bufs, [_sds(b.shape, b.dtype) for b in bufs], {a: a for a in range(n)},
                 [pltpu.SemaphoreType.DMA((n, 3)), pltpu.SemaphoreType.DMA((n, 3))], start, finish)


def _all_gather(name, arrays):
    n = len(arrays)

    def body(*refs):
        ins = refs[:n]
        outs = refs[n:2 * n]
        send_sems, recv_sems, local_sems = refs[2 * n:]
        x, y, c = _place()
        me, sibling = (x, y, c), (x, y, 1 - c)
        chips = _other_chips(x, y)

        def copy(a, k, block, to, src=None):
            slot = outs[a].at[_dev_index(block)]
            return pltpu.make_async_remote_copy(src_ref=slot if src is None else src, dst_ref=slot, send_sem=send_sems.at[a, k],
                                                recv_sem=recv_sems.at[a, k], device_id=to, device_id_type=MESH)

        mine = [pltpu.make_async_copy(ins[a], outs[a].at[_dev_index(me)], local_sems.at[a]) for a in range(n)]
        for cp in mine:
            cp.start()
        first = []
        for j, chip in enumerate(chips):
            first += [copy(a, 1 + j, me, (*chip, c), src=ins[a]) for a in range(n)]
        first += [copy(a, 0, me, sibling, src=ins[a]) for a in range(n)]
        for cp in first:
            cp.start()
        passed = []
        for j, chip in enumerate(chips):
            for a in range(n):
                copy(a, 1 + j, (*chip, c), me).wait_recv()
                fwd = copy(a, 4 + j, (*chip, c), sibling)
                fwd.start()
                passed.append(fwd)
        for a in range(n):
            copy(a, 0, sibling, me).wait_recv()
            for j, chip in enumerate(chips):
                copy(a, 4 + j, (*chip, 1 - c), me).wait_recv()
        for cp in first + passed:
            cp.wait_send()
        for cp in mine:
            cp.wait()

    return _pcall(
        body, name=name, in_specs=[ANY] * n, out_specs=[ANY] * n,
        out_shape=[_sds((N_DEV, *a.shape), a.dtype) for a in arrays],
        scratch_shapes=[pltpu.SemaphoreType.DMA((n, 7)), pltpu.SemaphoreType.DMA((n, 7)), pltpu.SemaphoreType.DMA((n,))],
        compiler_params=pltpu.CompilerParams(has_side_effects=True),
    )(*arrays)


def _sibling_exchange(grads):
    n = len(grads)

    def start(ins, outs, sems):
        send_sems, recv_sems = sems
        x, y, c = _place()
        for a in range(n):
            for ch in range(N_CHIP):
                pltpu.make_async_remote_copy(src_ref=ins[a].at[2 * ch + 1 - c], dst_ref=outs[a].at[ch], send_sem=send_sems.at[a],
                                             recv_sem=recv_sems.at[a], device_id=(x, y, 1 - c), device_id_type=MESH).start()

    def finish(ins, outs, sems):
        send_sems, recv_sems = sems
        x, y, c = _place()
        for a in range(n):
            pltpu.make_async_remote_copy(src_ref=outs[a], dst_ref=outs[a], send_sem=send_sems.at[a], recv_sem=recv_sems.at[a],
                                         device_id=(x, y, 1 - c), device_id_type=MESH).wait()

    return _Comm(grads, [_sds((N_CHIP, *g.shape[1:]), g.dtype) for g in grads], {},
                 [pltpu.SemaphoreType.DMA((n,)), pltpu.SemaphoreType.DMA((n,))], start, finish)


def _chip_exchange(parts, slots, layers, stacked):
    n = len(parts)
    names = []
    for nm, _ in slots:
        if nm not in names:
            names.append(nm)
    shapes = {nm: _sds((N_CHIP, layers[nm], *parts[a].shape[1:]), parts[a].dtype) for a, (nm, _) in enumerate(slots)}
    kept = [nm for nm in names if stacked.get(nm) is not None]
    aliases = {n + k: names.index(nm) for k, nm in enumerate(kept)}

    def copies(ins, outs, sems):
        send_sems, recv_sems, local_sems = sems
        x, y, c = _place()
        mine = 2 * x + y
        local, sends, recvs = [], [], []
        for a, (nm, l) in enumerate(slots):
            buf = outs[names.index(nm)]
            local.append(pltpu.make_async_copy(ins[a].at[mine], buf.at[mine, l], local_sems.at[a]))
            for j, chip in enumerate(_other_chips(x, y)):
                theirs = buf.at[2 * chip[0] + chip[1], l]
                sends.append(pltpu.make_async_remote_copy(src_ref=ins[a].at[2 * chip[0] + chip[1]], dst_ref=buf.at[mine, l], send_sem=send_sems.at[a, j],
                                                          recv_sem=recv_sems.at[a, j], device_id=(*chip, c), device_id_type=MESH))
                recvs.append(pltpu.make_async_remote_copy(src_ref=theirs, dst_ref=theirs, send_sem=send_sems.at[a, j],
                                                          recv_sem=recv_sems.at[a, j], device_id=(*chip, c), device_id_type=MESH))
        return local, sends, recvs

    def start(ins, outs, sems):
        local, sends, _ = copies(ins, outs, sems)
        for cp in local + sends:
            cp.start()

    def finish(ins, outs, sems):
        local, sends, recvs = copies(ins, outs, sems)
        for cp in recvs:
            cp.wait_recv()
        for cp in sends:
            cp.wait_send()
        for cp in local:
            cp.wait()

    comm = _Comm([*parts, *[stacked[nm] for nm in kept]], [shapes[nm] for nm in names], aliases,
                 [pltpu.SemaphoreType.DMA((n, 3)), pltpu.SemaphoreType.DMA((n, 3)), pltpu.SemaphoreType.DMA((n,))], start, finish)
    return comm, names


def _matmul(name, a, b, extras, *, grid, a_spec, b_spec, extra_specs, out_shapes, out_specs, dims, k_axis=None, nk=1,
            acc_shape=None, epilogue=None, comm=None, n_sum=0, write=None):
    n_extra = len(extras)
    n_out = len(out_shapes)

    def body(*refs):
        a_ref, b_ref = refs[0], refs[1]
        ex = refs[2:2 + n_extra]
        outs = refs[2 + n_extra:2 + n_extra + n_out]
        prod = lax.dot_general(a_ref[...], b_ref[...], dims, preferred_element_type=F32)

        def finish(acc):
            if write is not None:
                write(outs, acc, *[e[...] for e in ex])
                return
            res = epilogue(acc, *[e[...] for e in ex]) if epilogue is not None else (acc,)
            first = None
            for d in range(len(grid)):
                if d != k_axis:
                    here = pl.program_id(d) == 0
                    first = here if first is None else jnp.logical_and(first, here)
            for idx, (o, r) in enumerate(zip(outs, res)):
                if idx < n_out - n_sum:
                    o[...] = r.astype(o.dtype)
                else:
                    @pl.when(first)
                    def _(o=o, r=r):
                        o[...] = r.astype(o.dtype)

                    @pl.when(jnp.logical_not(first))
                    def _(o=o, r=r):
                        o[...] += r.astype(o.dtype)

        if k_axis is None:
            finish(prod)
        else:
            acc_ref = refs[-1]
            k = pl.program_id(k_axis)

            @pl.when(k == 0)
            def _():
                acc_ref[...] = prod

            @pl.when(k > 0)
            def _():
                acc_ref[...] += prod

            @pl.when(k == nk - 1)
            def _():
                finish(acc_ref[...])

    scratch = [] if k_axis is None else [pltpu.VMEM(acc_shape, F32)]
    return _call(name, body, [a, b, *extras], [a_spec, b_spec, *extra_specs], list(out_shapes), list(out_specs), scratch, grid, comm)


def _rowwise(name, fn, operands, *, grid, in_specs, out_shapes, out_specs, n_acc=0, grid_spec_prefetch=None, comm=None):
    n_in = len(operands)
    n_out = len(out_shapes)
    n_pre = 0 if grid_spec_prefetch is None else 1

    def body(*refs):
        refs = refs[n_pre:]
        ins = refs[:n_in]
        outs = refs[n_in:n_in + n_out]
        res = fn(*[r[...] for r in ins])
        if not isinstance(res, (tuple, list)):
            res = (res,)
        first = pl.program_id(0) == 0
        for d in range(1, len(grid)):
            first = jnp.logical_and(first, pl.program_id(d) == 0)
        for idx, (o, r) in enumerate(zip(outs, res)):
            if idx < n_out - n_acc:
                o[...] = r.astype(o.dtype)
            else:
                @pl.when(first)
                def _(o=o, r=r):
                    o[...] = r.astype(o.dtype)

                @pl.when(jnp.logical_not(first))
                def _(o=o, r=r):
                    o[...] += r.astype(o.dtype)

    if comm is not None:
        return _call(name, body, list(operands), list(in_specs), list(out_shapes), list(out_specs), [], grid, comm)
    if grid_spec_prefetch is None:
        return _pcall(body, name=name, grid=grid, in_specs=in_specs, out_specs=out_specs, out_shape=out_shapes,
                      compiler_params=_params(len(grid)))(*operands)
    gs = pltpu.PrefetchScalarGridSpec(num_scalar_prefetch=1, grid=grid, in_specs=in_specs, out_specs=out_specs)
    return _pcall(body, name=name, grid_spec=gs, out_shape=out_shapes,
                  compiler_params=_params(len(grid)))(grid_spec_prefetch, *operands)


def _row_spec(tm, w):
    return pl.BlockSpec((tm, w), lambda i: (i, 0))


def _const_spec(shape):
    nd = len(shape)
    return pl.BlockSpec(tuple(shape), lambda *_: (0,) * nd)


def _rms_fwd(x, g):
    r = lax.rsqrt(jnp.mean(x * x, axis=-1, keepdims=True) + NORM_EPS)
    return x * r * g


def _rms_bwd(x, g, dy):
    r = lax.rsqrt(jnp.mean(x * x, axis=-1, keepdims=True) + NORM_EPS)
    xh = x * r
    u = dy * g
    dx = r * (u - xh * jnp.mean(u * xh, axis=-1, keepdims=True))
    dg = jnp.sum(dy * xh, axis=0, keepdims=True)
    return dx, dg


def _gelu_and_grad(z):
    cdf = 0.5 * (1.0 + lax.erf(z * (2.0 ** -0.5)))
    return cdf + z * jnp.exp(-0.5 * z * z) * ((2.0 * math.pi) ** -0.5), z * cdf


def _rope_fwd(x, cc, sa, sb):
    return x * cc + pltpu.roll(x, 96, 1) * sa + pltpu.roll(x, 32, 1) * sb


def _rope_bwd(d, cc, sa, sb):
    return d * cc + pltpu.roll(d * sa, 32, 1) + pltpu.roll(d * sb, 96, 1)


def _adam(w, g, m, v):
    m = ADAM_B1 * m + (1.0 - ADAM_B1) * g
    v = ADAM_B2 * v + (1.0 - ADAM_B2) * (g * g)
    m_hat = m / (1.0 - ADAM_B1 ** ADAM_STEP)
    v_hat = v / (1.0 - ADAM_B2 ** ADAM_STEP)
    delta = -ADAM_LR * (m_hat / (jnp.sqrt(v_hat) + ADAM_EPS) + ADAM_WD * w)
    return delta, m, v


def _flash_fwd(q, k, vt, tq, comm=None):
    h, t = vt.shape[0], q.shape[0]
    nq = t // tq

    chunk_blocks = [c for c in (4, 2) if c < nq]

    def body(q_ref, k_ref, vt_ref, o_ref, lse_ref, m_ref, l_ref, acc_ref):
        qi = pl.program_id(1)
        m_ref[...] = jnp.full((1, tq), NEG, F32)
        l_ref[...] = jnp.zeros((1, tq), F32)
        acc_ref[...] = jnp.zeros((VDIM, tq), F32)

        def update(kb0, nblk, masked):
            kb = k_ref[pl.ds(pl.multiple_of(kb0 * tq, tq), nblk * tq), :]
            st = lax.dot_general(kb, q_ref[...], NT, preferred_element_type=F32)
            if masked:
                key = lax.broadcasted_iota(jnp.int32, (nblk * tq, tq), 0) - (nblk - 1) * tq
                qry = lax.broadcasted_iota(jnp.int32, (nblk * tq, tq), 1)
                st = jnp.where(key <= qry, st, NEG)
            m_old = m_ref[...]
            m_new = jnp.maximum(m_old, jnp.max(st, axis=0, keepdims=True))
            alpha = jnp.exp2((m_old - m_new) * EXP2_SCALE)
            pt = jnp.exp2((st - m_new) * EXP2_SCALE)
            l_ref[...] = alpha * l_ref[...] + jnp.sum(pt, axis=0, keepdims=True)
            ptb = pt.astype(BF16)
            pv = lax.dot_general(vt_ref[kb0], ptb[:tq], NN, preferred_element_type=F32)
            for j in range(1, nblk):
                pv += lax.dot_general(vt_ref[kb0 + j], ptb[j * tq:(j + 1) * tq], NN, preferred_element_type=F32)
            acc_ref[...] = alpha * acc_ref[...] + pv
            m_ref[...] = m_new

        start = jnp.int32(0)
        for c in chunk_blocks:
            take = (qi & c) != 0

            @pl.when(take)
            def _(start=start, c=c):
                update(start, c, False)

            start = start + jnp.where(take, c, 0)
        if nq > 1:
            @pl.when((qi & 1) != 0)
            def _():
                update(qi - 1, 2, True)

            @pl.when((qi & 1) == 0)
            def _():
                update(qi, 1, True)
        else:
            update(qi, 1, True)
        l = l_ref[...]
        o_ref[...] = (acc_ref[...] / l).T.astype(o_ref.dtype)
        lse_ref[...] = m_ref[...] * EXP2_SCALE + jnp.log2(l)

    return _call(
        "flash_fwd", body, [q, k, vt],
        [pl.BlockSpec((tq, QPAD), lambda hh, i: (i, hh)),
         pl.BlockSpec((t, QPAD), lambda hh, i: (0, hh)),
         pl.BlockSpec((None, nq, VDIM, tq), lambda hh, i: (hh, 0, 0, 0))],
        [_sds((t, h * VDIM), BF16), _sds((h, nq, 1, tq), F32)],
        [pl.BlockSpec((tq, VDIM), lambda hh, i: (i, hh)),
         pl.BlockSpec((None, None, 1, tq), lambda hh, i: (hh, i, 0, 0))],
        [pltpu.VMEM((1, tq), F32), pltpu.VMEM((1, tq), F32), pltpu.VMEM((VDIM, tq), F32)], (h, nq), comm)


def _flash_bwd(q, k, v, o, do, lse, tabs, tq, comm=None):
    t = q.shape[0]
    h = q.shape[1] // QPAD
    nq = t // tq

    def body(q_ref, k_ref, v_ref, o_ref, do_ref, lse_ref, cc_ref, sa_ref, sb_ref, dq_ref, dk_out, dv_out, delta_ref, dqt_ref, dk_ref, dv_ref):
        kj = pl.program_id(1)

        @pl.when(kj == 0)
        def _():
            dqt_ref[...] = jnp.zeros_like(dqt_ref)
            ones = jnp.ones((8, VDIM), BF16)
            for qi in range(nq):
                rows = pl.ds(qi * tq, tq)
                prod = do_ref[rows, :].astype(F32) * o_ref[rows, :].astype(F32)
                hi = prod.astype(BF16)
                lo = (prod - hi.astype(F32)).astype(BF16)
                delta_ref[qi] = (lax.dot_general(ones, hi, NT, preferred_element_type=F32)
                                 + lax.dot_general(ones, lo, NT, preferred_element_type=F32))

        kb = k_ref[...]
        vb = v_ref[...]
        kbt = kb.astype(F32).T.astype(BF16)
        dk_ref[...] = jnp.zeros_like(dk_ref)
        dv_ref[...] = jnp.zeros_like(dv_ref)

        def step(q0, nblk, masked):
            rows = pl.ds(pl.multiple_of(q0 * tq, tq), nblk * tq)
            qb = q_ref[rows, :]
            dob = do_ref[rows, :]
            lse = jnp.concatenate([lse_ref[q0 + j] for j in range(nblk)], axis=1)
            delta = jnp.concatenate([delta_ref[q0 + j, pl.ds(0, 1), :] for j in range(nblk)], axis=1)
            st = lax.dot_general(kb, qb, NT, preferred_element_type=F32)
            pt = jnp.exp2(st * EXP2_SCALE - lse)
            if masked:
                key = lax.broadcasted_iota(jnp.int32, (tq, nblk * tq), 0)
                qry = lax.broadcasted_iota(jnp.int32, (tq, nblk * tq), 1)
                pt = jnp.where(key <= qry, pt, 0.0)
            dv_ref[...] += lax.dot_general(pt.astype(BF16), dob, NN, preferred_element_type=F32)
            dpt = lax.dot_general(vb, dob, NT, preferred_element_type=F32)
            dst = (pt * (dpt - delta) * ATTN_SCALE).astype(BF16)
            dk_ref[...] += lax.dot_general(dst, qb, NN, preferred_element_type=F32)
            dqt = lax.dot_general(kbt, dst, NN, preferred_element_type=F32)
            for j in range(nblk):
                dqt_ref[q0 + j] += dqt[:, j * tq:(j + 1) * tq]

        later = nq - 1 - kj
        if nq > 1:
            @pl.when((later & 1) != 0)
            def _():
                step(kj, 2, True)

            @pl.when((later & 1) == 0)
            def _():
                step(kj, 1, True)
        else:
            step(kj, 1, True)
        start = kj + 1 + (later & 1)
        for c in [c for c in (2, 4) if c < nq]:
            take = (later & c) != 0

            @pl.when(take)
            def _(start=start, c=c):
                step(start, c, False)

            start = start + jnp.where(take, c, 0)
        dk_out[...] = dk_ref[...].astype(BF16)
        dv_out[...] = dv_ref[...].astype(BF16)

        @pl.when(kj == nq - 1)
        def _():
            for qi in range(nq):
                rows = pl.ds(qi * tq, tq)
                d = dqt_ref[qi].T
                roped = _rope_bwd(d[:, NOPE:], cc_ref[rows, :], sa_ref[rows, :], sb_ref[rows, :])
                dq_ref[rows, :] = jnp.concatenate([d[:, :NOPE], roped], axis=1).astype(BF16)

    head_q = pl.BlockSpec((t, QPAD), lambda hh, j: (0, hh))
    head_v = pl.BlockSpec((t, VDIM), lambda hh, j: (0, hh))
    table = pl.BlockSpec((t, 128), lambda hh, j: (0, 0))
    return _call(
        "flash_bwd", body, [q, k, v, o, do, lse, *tabs],
        [head_q, pl.BlockSpec((tq, QPAD), lambda hh, j: (j, hh)), pl.BlockSpec((tq, VDIM), lambda hh, j: (j, hh)), head_v, head_v,
         pl.BlockSpec((None, nq, 1, tq), lambda hh, j: (hh, 0, 0, 0)), table, table, table],
        [_sds((t, h * QPAD), BF16), _sds((t, h * QPAD), BF16), _sds((t, h * VDIM), BF16)],
        [head_q, pl.BlockSpec((tq, QPAD), lambda hh, j: (j, hh)), pl.BlockSpec((tq, VDIM), lambda hh, j: (j, hh))],
        [pltpu.VMEM((nq, 8, tq), F32), pltpu.VMEM((nq, QPAD, tq), F32), pltpu.VMEM((tq, QPAD), F32), pltpu.VMEM((tq, VDIM), F32)], (h, nq), comm)


def _tril_bf16(w):
    row = lax.broadcasted_iota(jnp.int32, w.shape, 0)
    col = lax.broadcasted_iota(jnp.int32, w.shape, 1)
    return jnp.where(col <= row, w, 0.0).astype(BF16)


def _layer_norm_parts(v0):
    mu = jnp.mean(v0, axis=-1, keepdims=True)
    vc = v0 - mu
    rstd = lax.rsqrt(jnp.mean(vc * vc, axis=-1, keepdims=True) + LN_EPS)
    return vc * rstd, rstd


def _sgu_mid_fwd(ge, ln_g, ln_b, w_sp, b_sp, chunks_per_step):
    t, e2 = ge.shape
    e = e2 // 2
    gd = e // SGU_GROUPS
    rows = SGU_CHUNK * chunks_per_step

    def body(u_ref, v_ref, g_ref, b_ref, w_ref, bs_ref, gate_ref):
        for ck in range(chunks_per_step):
            r = pl.ds(ck * SGU_CHUNK, SGU_CHUNK)
            xh, _ = _layer_norm_parts(v_ref[r, :].astype(F32))
            v1 = (xh * g_ref[...] + b_ref[...]).astype(BF16)
            for g in range(SGU_GROUPS):
                cols = pl.ds(g * gd, gd)
                mixed = lax.dot_general(_tril_bf16(w_ref[g]), v1[:, g * gd:(g + 1) * gd], NN, preferred_element_type=F32) + bs_ref[g]
                gate_ref[r, cols] = (u_ref[r, cols].astype(F32) * mixed).astype(BF16)

    return _pcall(
        body, name="sgu_mid_fwd", grid=(t // rows,),
        in_specs=[pl.BlockSpec((rows, e), lambda i: (i, 0)), pl.BlockSpec((rows, e), lambda i: (i, 1)),
                  _const_spec((1, e)), _const_spec((1, e)), _const_spec(w_sp.shape), _const_spec(b_sp.shape)],
        out_specs=pl.BlockSpec((rows, e), lambda i: (i, 0)),
        out_shape=_sds((t, e), BF16), compiler_params=_params(1),
    )(ge, ge, ln_g, ln_b, w_sp, b_sp)


def _sgu_mid_bwd(ge, gp, dgate, ln_g, ln_b, w_sp, b_sp, chunks_per_step):
    t, e2 = ge.shape
    e = e2 // 2
    gd = e // SGU_GROUPS
    rows = SGU_CHUNK * chunks_per_step

    def body(u_ref, v_ref, zu_ref, zv_ref, dg_ref, g_ref, b_ref, w_ref, bs_ref, dz_ref, dw_ref, dbs_ref, dlg_ref, dlb_ref):
        @pl.when(pl.program_id(0) == 0)
        def _():
            dw_ref[...] = jnp.zeros_like(dw_ref)
            dbs_ref[...] = jnp.zeros_like(dbs_ref)
            dlg_ref[...] = jnp.zeros_like(dlg_ref)
            dlb_ref[...] = jnp.zeros_like(dlb_ref)

        for ck in range(chunks_per_step):
            r = pl.ds(ck * SGU_CHUNK, SGU_CHUNK)
            xh, rstd = _layer_norm_parts(v_ref[r, :].astype(F32))
            v1 = (xh * g_ref[...] + b_ref[...]).astype(BF16)
            dv1_parts = []
            for g in range(SGU_GROUPS):
                cols = pl.ds(g * gd, gd)
                wc = _tril_bf16(w_ref[g])
                v1g = v1[:, g * gd:(g + 1) * gd]
                mixed = lax.dot_general(wc, v1g, NN, preferred_element_type=F32) + bs_ref[g]
                dgate = dg_ref[r, cols].astype(F32)
                dmixed = dgate * u_ref[r, cols].astype(F32)
                du = dgate * mixed
                dz_ref[r, cols] = (du * zu_ref[r, cols].astype(F32)).astype(BF16)
                dbs_ref[g] += jnp.sum(dmixed, axis=1, keepdims=True)
                dmb = dmixed.astype(BF16)
                dwg = lax.dot_general(dmb, v1g, NT, preferred_element_type=F32)
                row = lax.broadcasted_iota(jnp.int32, dwg.shape, 0)
                col = lax.broadcasted_iota(jnp.int32, dwg.shape, 1)
                dw_ref[g] += jnp.where(col <= row, dwg, 0.0)
                dv1_parts.append(lax.dot_general(wc, dmb, TN, preferred_element_type=F32))
            dv1 = jnp.concatenate(dv1_parts, axis=1)
            dlg_ref[...] += jnp.sum(dv1 * xh, axis=0, keepdims=True)
            dlb_ref[...] += jnp.sum(dv1, axis=0, keepdims=True)
            dxh = dv1 * g_ref[...]
            dv0 = rstd * (dxh - jnp.mean(dxh, axis=-1, keepdims=True) - xh * jnp.mean(dxh * xh, axis=-1, keepdims=True))
            dz_ref[r, pl.ds(e, e)] = (dv0 * zv_ref[r, :].astype(F32)).astype(BF16)

    half0 = pl.BlockSpec((rows, e), lambda i: (i, 0))
    half1 = pl.BlockSpec((rows, e), lambda i: (i, 1))
    return _pcall(
        body, name="sgu_mid_bwd", grid=(t // rows,),
        in_specs=[half0, half1, half0, half1, half0, _const_spec((1, e)), _const_spec((1, e)), _const_spec(w_sp.shape), _const_spec(b_sp.shape)],
        out_specs=[pl.BlockSpec((rows, e2), lambda i: (i, 0)), _const_spec(w_sp.shape), _const_spec(b_sp.shape), _const_spec((1, e)), _const_spec((1, e))],
        out_shape=[_sds((t, e2), BF16), _sds(w_sp.shape, F32), _sds(b_sp.shape, F32), _sds((1, e), F32), _sds((1, e), F32)],
        compiler_params=_params(1),
    )(ge, ge, gp, gp, dgate, ln_g, ln_b, w_sp, b_sp)


def kernel(x, positions, norm_mix, norm_ffn, final_norm, mla_w_dkv, mla_q_norm, mla_kv_norm, mla_w_uq, mla_w_ukv, mla_w_o, sgu_w_in, sgu_ln_g, sgu_ln_b, sgu_w_spatial, sgu_b_spatial, sgu_w_out, ffn_w_up, ffn_w_down, loss_target, m_norm_mix, m_norm_ffn, m_final_norm, m_mla_w_dkv, m_mla_q_norm, m_mla_kv_norm, m_mla_w_uq, m_mla_w_ukv, m_mla_w_o, m_sgu_w_in, m_sgu_ln_g, m_sgu_ln_b, m_sgu_w_spatial, m_sgu_b_spatial, m_sgu_w_out, m_ffn_w_up, m_ffn_w_down, v_norm_mix, v_norm_ffn, v_final_norm, v_mla_w_dkv, v_mla_q_norm, v_mla_kv_norm, v_mla_w_uq, v_mla_w_ukv, v_mla_w_o, v_sgu_w_in, v_sgu_ln_g, v_sgu_ln_b, v_sgu_w_spatial, v_sgu_b_spatial, v_sgu_w_out, v_ffn_w_up, v_ffn_w_down):
    _, T, D = x.shape
    depth = norm_mix.shape[0]
    n_mla, n_sgu = mla_w_dkv.shape[0], sgu_w_in.shape[0]
    assert depth % 2 == 0
    FF = ffn_w_up.shape[2] * N_DEV
    E = sgu_w_out.shape[1] * N_DEV
    ffc, ec, e2c = FF // N_DEV, E // N_DEV, 2 * E // N_DEV
    dc = D // N_DEV
    OW = HEADS * VDIM
    HW = HEADS * QPAD
    owc = OW // N_DEV
    tm = _tile(T, 1024)
    tb = _tile(T, 4096)
    tk = _tile(T, 512)
    tq = _tile(T, 512)
    ts = _tile(T, 256)
    nt = T // tm
    x2 = x.reshape(T, D)
    tgt = loss_target.reshape(T, D)
    cidx = lax.axis_index("c").astype(jnp.int32).reshape(1)

    ln_local = jnp.concatenate([sgu_ln_g, sgu_ln_b, jnp.zeros((8 - 2 * n_sgu, ec), F32)], axis=0)
    mla_sh = [[w[l].astype(BF16) for w in (mla_w_dkv, mla_w_uq, mla_w_ukv, mla_w_o)] for l in range(n_mla)]

    def mla_layouts(g_dkv, g_uq, g_ukv, g_o):
        w_dkv = jnp.pad(g_dkv.reshape(1, D, LAT), ((0, 0), (0, 0), (0, LAT_PAD - LAT)))
        w_uq = jnp.pad(g_uq, ((0, 0), (0, 0), (0, QPAD - NOPE - ROPE))).transpose(1, 0, 2).reshape(1, Q_RANK, HEADS * QPAD)
        w_ukv = g_ukv.transpose(1, 0, 2).reshape(1, KV_RANK, HEADS * (NOPE + VDIM))
        return w_dkv, w_uq, w_ukv, g_o.reshape(1, HEADS * VDIM, D)

    mla_w = [None] * n_mla
    small_later = [a for l in range(1, n_mla) for a in mla_sh[l]] + [ln_local]
    ln_g_full, ln_b_full = [None] * n_sgu, [None] * n_sgu
    b_sp = sgu_b_spatial.reshape(n_sgu, SGU_GROUPS, SGU_CHUNK, 1)
    up_sh = [ffn_w_up[i].astype(BF16) for i in range(depth)]
    down_sh = [ffn_w_down[i].astype(BF16) for i in range(depth)]
    in_sh = [sgu_w_in[l].astype(BF16) for l in range(n_sgu)]
    out_sh = [sgu_w_out[l].astype(BF16) for l in range(n_sgu)]
    g_up, g_down, g_in, g_out = [None] * depth, [None] * depth, [None] * n_sgu, [None] * n_sgu

    inv_freq = ROPE_THETA ** (-jnp.arange(0, ROPE, 2, dtype=F32) / ROPE)
    zeros32 = jnp.zeros((ROPE // 2,), F32)
    inv128 = jnp.concatenate([inv_freq, inv_freq, zeros32, zeros32]).reshape(1, 128)
    sel_a = jnp.concatenate([-jnp.ones((32,), F32), zeros32, zeros32, zeros32]).reshape(1, 128)
    sel_b = jnp.concatenate([zeros32, jnp.ones((32,), F32), zeros32, zeros32]).reshape(1, 128)
    sel_c = jnp.concatenate([jnp.ones((64,), F32), zeros32, zeros32]).reshape(1, 128)

    def rope_tables(pos, inv, sa, sb, sc):
        ang = pos.astype(F32) * inv
        cs, sn = jnp.cos(ang), jnp.sin(ang)
        return cs * sc, sn * sa, sn * sb

    t_cc, t_sa, t_sb, *first_half = _rowwise(
        "rope_tables", rope_tables, [positions.reshape(T, 1), inv128, sel_a, sel_b, sel_c], grid=(nt,),
        in_specs=[_row_spec(tm, 1)] + [_const_spec((1, 128))] * 4,
        out_shapes=[_sds((T, 128), F32)] * 3, out_specs=[_row_spec(tm, 128)] * 3, comm=_gather_level1(mla_sh[0]))
    tab_specs = [_row_spec(tm, 128)] * 3

    def rmsnorm(xv, g, comm):
        return _rowwise("rmsnorm", lambda a, gg: _rms_fwd(a, gg), [xv, g.reshape(1, D)], grid=(nt,),
                        in_specs=[_row_spec(tm, D), _const_spec((1, D))], out_shapes=[_sds((T, D), BF16)], out_specs=[_row_spec(tm, D)], comm=comm)

    def proj_cols(name, h, gw, nc, epilogue, n_out, comm=None):
        return _matmul(name, h, gw, [], grid=(N_DEV, T // tb),
                       a_spec=pl.BlockSpec((tb, D), lambda j, i: (i, 0)),
                       b_spec=pl.BlockSpec((None, D, nc), lambda j, i: (j, 0, 0)), extra_specs=[],
                       out_shapes=[_sds((T, nc * N_DEV), BF16)] * n_out, out_specs=[pl.BlockSpec((tb, nc), lambda j, i: (i, j))] * n_out,
                       dims=NN, epilogue=epilogue, comm=comm)

    def residual_norm(acc, xr, g):
        xn = acc + xr
        return xn, _rms_fwd(xn, g)

    def proj_rows_residual(name, a, gw, xres, g_next, comm=None):
        kk_ = a.shape[1]
        return _matmul(name, a, gw.reshape(kk_, D), [xres, g_next.reshape(1, D)], grid=(T // tk,),
                       a_spec=_row_spec(tk, kk_), b_spec=_const_spec((kk_, D)), extra_specs=[_row_spec(tk, D), _const_spec((1, D))],
                       out_shapes=[_sds((T, D), F32), _sds((T, D), BF16)], out_specs=[_row_spec(tk, D)] * 2,
                       dims=NN, epilogue=residual_norm, comm=comm)

    def back_rows(name, dy, gw, kc, extras, epilogue, comm=None):
        return _matmul(name, dy, gw, extras, grid=(N_DEV, T // tb),
                       a_spec=pl.BlockSpec((tb, D), lambda j, i: (i, 0)),
                       b_spec=pl.BlockSpec((None, kc, D), lambda j, i: (j, 0, 0)),
                       extra_specs=[pl.BlockSpec((tb, kc), lambda j, i: (i, j))] * len(extras),
                       out_shapes=[_sds((T, kc * N_DEV), BF16)], out_specs=[pl.BlockSpec((tb, kc), lambda j, i: (i, j))],
                       dims=NT, epilogue=epilogue, comm=comm)

    def norm_bwd_epilogue(dh, xv, g, dxi):
        dxn, dg = _rms_bwd(xv, g, dh)
        return dxi + dxn, dxi + dxn, dg

    def transposed(gw):
        return gw.transpose(0, 2, 1).reshape(gw.shape[0] * gw.shape[2], D)

    def back_cols(name, da, gwt, xv, g, dx_in, comm=None):
        n = da.shape[1]
        row = _row_spec(tk, D)
        return _matmul(name, da, gwt, [xv, g.reshape(1, D), dx_in], grid=(T // tk,),
                       a_spec=_row_spec(tk, n), b_spec=_const_spec((n, D)), extra_specs=[row, _const_spec((1, D)), row],
                       out_shapes=[_sds((T, D), F32), _sds((T, D), BF16), _sds((1, D), F32)], out_specs=[row, row, _const_spec((1, D))],
                       dims=NN, epilogue=norm_bwd_epilogue, n_sum=1, comm=comm)

    def token_sum(tt):
        return dict(k_axis=1, nk=T // tt) if T // tt > 1 else dict(k_axis=None)

    def wgrad_cols(name, h, da, nc):
        return _matmul(name, h, da, [], grid=(N_DEV, T // tb),
                       a_spec=pl.BlockSpec((tb, D), lambda j, t: (t, 0)), b_spec=pl.BlockSpec((tb, nc), lambda j, t: (t, j)),
                       extra_specs=[], out_shapes=[_sds((N_DEV, D, nc), BF16)],
                       out_specs=[pl.BlockSpec((None, D, nc), lambda j, t: (j, 0, 0))],
                       dims=TN, acc_shape=(D, nc), **token_sum(tb))[0]

    def wgrad_rows(name, a, dy, kc, ncols, tt, comm=None):
        res = _matmul(name, a, dy, [], grid=(a.shape[1] // kc, T // tt),
                      a_spec=pl.BlockSpec((tt, kc), lambda j, t: (t, j)), b_spec=pl.BlockSpec((tt, ncols), lambda j, t: (t, 0)),
                      extra_specs=[], out_shapes=[_sds((a.shape[1], ncols), BF16)],
                      out_specs=[pl.BlockSpec((kc, ncols), lambda j, t: (j, 0))],
                      dims=TN, acc_shape=(kc, ncols), comm=comm, **token_sum(tt))
        return res[0] if comm is None else res

    saved = []
    xs = x2
    for i in range(depth):
        l = i // 2
        if i == 0:
            h, *first_w = rmsnorm(xs, norm_mix[0], _gather_level2(first_half))
            mla_w[0] = mla_layouts(*first_w)
        if i % 2 == 0:
            w_dkv, w_uq, w_ukv, w_o = mla_w[l]
            lat = _matmul("mla_down", h, w_dkv, [], grid=(nt,), a_spec=_row_spec(tm, D),
                          b_spec=pl.BlockSpec((None, D, LAT_PAD), lambda i_: (0, 0, 0)), extra_specs=[],
                          out_shapes=[_sds((T, LAT_PAD), F32)], out_specs=[_row_spec(tm, LAT_PAD)], dims=NN)[0]

            def latent_post(la, qn, kvn, cc, sa, sb):
                cq = _rms_fwd(la[:, :Q_RANK], qn)
                ckv = _rms_fwd(la[:, Q_RANK:Q_RANK + KV_RANK], kvn)
                kr = _rope_fwd(la[:, Q_RANK + KV_RANK:], cc, sa, sb)
                return cq, ckv, kr

            cq, ckv, kr = _rowwise(
                "mla_latent", latent_post, [lat, mla_q_norm[l].reshape(1, Q_RANK), mla_kv_norm[l].reshape(1, KV_RANK), t_cc, t_sa, t_sb],
                grid=(nt,), in_specs=[_row_spec(tm, LAT_PAD), _const_spec((1, Q_RANK)), _const_spec((1, KV_RANK))] + tab_specs,
                out_shapes=[_sds((T, Q_RANK), BF16), _sds((T, KV_RANK), BF16), _sds((T, 128), BF16)],
                out_specs=[_row_spec(tm, Q_RANK), _row_spec(tm, KV_RANK), _row_spec(tm, 128)])

            def q_epilogue(acc, cc, sa, sb):
                parts = []
                for b in range(HEADS):
                    parts += [acc[:, b * QPAD:b * QPAD + NOPE], _rope_fwd(acc[:, b * QPAD + NOPE:(b + 1) * QPAD], cc, sa, sb)]
                return (jnp.concatenate(parts, axis=1),)

            q = _matmul("mla_q", cq, w_uq, [t_cc, t_sa, t_sb], grid=(nt,), a_spec=_row_spec(tm, Q_RANK),
                        b_spec=pl.BlockSpec((None, Q_RANK, HW), lambda i_: (0, 0, 0)), extra_specs=tab_specs,
                        out_shapes=[_sds((T, HW), BF16)], out_specs=[_row_spec(tm, HW)], dims=NN, epilogue=q_epilogue)[0]

            def kv_write(outs, acc, krb):
                k_ref, v_ref, vt_ref = outs
                for b in range(HEADS):
                    vb = acc[:, b * QPAD + NOPE:(b + 1) * QPAD]
                    k_ref[:, b * QPAD:b * QPAD + NOPE] = acc[:, b * QPAD:b * QPAD + NOPE].astype(BF16)
                    k_ref[:, b * QPAD + NOPE:(b + 1) * QPAD] = krb
                    v_ref[:, b * VDIM:(b + 1) * VDIM] = vb.astype(BF16)
                    vbt = vb.T.astype(BF16)
                    for u in range(tm // tq):
                        vt_ref[b, u] = vbt[:, u * tq:(u + 1) * tq]

            kk, vv, vt = _matmul("mla_kv", ckv, w_ukv, [kr], grid=(nt,), a_spec=_row_spec(tm, KV_RANK),
                                 b_spec=pl.BlockSpec((None, KV_RANK, HW), lambda i_: (0, 0, 0)), extra_specs=[_row_spec(tm, 128)],
                                 out_shapes=[_sds((T, HW), BF16), _sds((T, OW), BF16), _sds((HEADS, T // tq, VDIM, tq), BF16)],
                                 out_specs=[_row_spec(tm, HW), _row_spec(tm, OW), pl.BlockSpec((HEADS, tm // tq, VDIM, tq), lambda i_: (0, i_, 0, 0))],
                                 dims=NN, write=kv_write)
            group = [up_sh[i], down_sh[i], in_sh[l], out_sh[l]] + (small_later if i == 0 else [])
            o, lse, *bufs = _flash_fwd(q, kk, vt, tq, comm=_gather_level1(group))
            xm, h2, g_up[i], g_down[i] = _matmul(
                "mla_out", o, w_o, [xs, norm_ffn[i].reshape(1, D)], grid=(nt,), a_spec=_row_spec(tm, OW),
                b_spec=pl.BlockSpec((None, OW, D), lambda i_: (0, 0, 0)), extra_specs=[_row_spec(tm, D), _const_spec((1, D))],
                out_shapes=[_sds((T, D), F32), _sds((T, D), BF16)], out_specs=[_row_spec(tm, D)] * 2, dims=NN,
                epilogue=residual_norm, comm=_gather_level2(bufs[:2]))
            half_gathered = bufs[2:]
            mix_saved = (h, lat, cq, ckv, q, kk, vv, o, lse)
        else:
            gp, ge, g_down[i], up_half = proj_cols("sgu_in", h, g_in[l], e2c, _gelu_and_grad, 2,
                                                   comm=_merge_comm(_gather_level2([down_half]), _gather_level1([up_sh[i]])))
            gate = _sgu_mid_fwd(ge, ln_g_full[l], ln_b_full[l], sgu_w_spatial[l], b_sp[l], 8)
            xm, h2, g_up[i] = proj_rows_residual("sgu_out", gate, g_out[l], xs, norm_ffn[i], comm=_gather_level2([up_half]))
            mix_saved = (h, gp, ge, gate)
        r, s, *rest = proj_cols("ffn_up", h2, g_up[i], ffc, lambda acc: (jnp.maximum(acc, 0.0), jnp.square(jnp.maximum(acc, 0.0))), 2,
                                comm=_gather_level2(half_gathered) if i % 2 == 0 else None)
        if i % 2 == 0:
            g_in[l], g_out[l], *small_gathered = rest
        if i == 0:
            for l_ in range(1, n_mla):
                mla_w[l_] = mla_layouts(*small_gathered[4 * (l_ - 1):4 * l_])
            g_ln = small_gathered[-1]
            ln_g_full = [g_ln[:, l_, :].reshape(1, E) for l_ in range(n_sgu)]
            ln_b_full = [g_ln[:, n_sgu + l_, :].reshape(1, E) for l_ in range(n_sgu)]
        saved.append((xs, xm, mix_saved, h2, r, s))
        if i + 1 < depth:
            xs, h, *rest = proj_rows_residual("ffn_down", s, g_down[i], xm, norm_mix[i + 1],
                                              comm=_gather_level1([down_sh[i + 1]]) if i % 2 == 0 else None)
            if i % 2 == 0:
                (down_half,) = rest

    def loss_head(acc, xr, tg, g):
        xv = acc + xr
        y = _rms_fwd(xv, g)
        err = y - tg
        part = 0.5 * jnp.sum(jnp.sum(err * err, axis=-1, keepdims=True), axis=0, keepdims=True) / D
        dxv, dg = _rms_bwd(xv, g, err / D)
        return dxv, dxv, jnp.broadcast_to(part, (1, 128)), dg

    dx, dyb, loss_part, d_final = _matmul(
        "ffn_down_loss", s, g_down[depth - 1].reshape(FF, D), [xm, tgt, final_norm.reshape(1, D)], grid=(T // tk,),
        a_spec=_row_spec(tk, FF), b_spec=_const_spec((FF, D)), extra_specs=[_row_spec(tk, D), _row_spec(tk, D), _const_spec((1, D))],
        out_shapes=[_sds((T, D), F32), _sds((T, D), BF16), _sds((1, 128), F32), _sds((1, D), F32)],
        out_specs=[_row_spec(tk, D), _row_spec(tk, D), _const_spec((1, 128)), _const_spec((1, D))], dims=NN, epilogue=loss_head, n_sum=2)
    loss = lax.psum(loss_part[0, 0], ("x", "y", "c"))

    d_norm_mix, d_norm_ffn = [None] * depth, [None] * depth
    d_qn, d_kvn = [None] * n_mla, [None] * n_mla
    d_wsp, d_bsp, d_lng, d_lnb = [None] * n_sgu, [None] * n_sgu, [None] * n_sgu, [None] * n_sgu
    layers = {"dkv": n_mla, "uq": n_mla, "ukv": n_mla, "o": n_mla, "in": n_sgu, "out": n_sgu, "up": depth, "down": depth}
    stacked = {nm: None for nm in layers}
    pending = []
    summed = []

    def add_pairs(gs, rcvs):
        operands, in_specs, out_shapes, out_specs = [], [], [], []
        for g, rcv in zip(gs, rcvs):
            _, rws, cls = g.shape
            slab = pl.BlockSpec((None, rws, cls), lambda ch, cr: (ch, 0, 0))
            operands += [g.reshape(N_CHIP, 2, rws, cls), rcv]
            in_specs += [pl.BlockSpec((None, None, rws, cls), lambda ch, cr: (ch, cr[0], 0, 0)), slab]
            out_shapes.append(_sds(rcv.shape, BF16))
            out_specs.append(slab)

        def fn(*blocks):
            return tuple(blocks[2 * k].astype(F32) + blocks[2 * k + 1].astype(F32) for k in range(len(gs)))

        return _rowwise("grad_pair_sum", fn, operands, grid=(N_CHIP,), in_specs=in_specs, out_shapes=out_shapes, out_specs=out_specs,
                        grid_spec_prefetch=cidx)

    def sibling_comm():
        return _sibling_exchange([g for _, _, g in pending]) if pending else None

    def absorb(from_sibling):
        if pending:
            parts = add_pairs([g for _, _, g in pending], list(from_sibling))
            summed.extend((nm, l_, p) for (nm, l_, _), p in zip(pending, parts))
            pending.clear()

    def chip_comm():
        if pending:
            absorb(_comm_call("grad_sibling_exchange", sibling_comm()))
        comm, names = _chip_exchange([p for _, _, p in summed], [(nm, l_) for nm, l_, _ in summed], layers, stacked)
        summed.clear()
        return comm, names

    def rows128(a, rows):
        flat = a.reshape(-1, 128)
        return jnp.pad(flat, ((0, rows - flat.shape[0]), (0, 0)))

    def pad_to(n, mult):
        return -(-n // mult) * mult

    def packed(arrs, sizes):
        return jnp.concatenate([rows128(a, sz) for a, sz in zip(arrs, sizes)], axis=0)

    n_wsp, n_bsp, n_ln = sgu_w_spatial.size // 128, pad_to(sgu_b_spatial.size // 128, 8), pad_to(n_sgu * E // 128, 8)
    early_sizes = [n_wsp, pad_to(n_wsp + n_bsp, SMALL_ROWS) - n_wsp, n_ln, n_ln]
    early_rep = early_sizes[0] + early_sizes[1]
    gathered_early = None

    for i in reversed(range(depth)):
        l = i // 2
        xs_i, xm, mix_saved, h2, r, s = saved[i]
        comm = sibling_comm()
        if i == 0:
            comm = _gather_level2([early_half]) if comm is None else _merge_comm(comm, _gather_level2([early_half]))
        da, *rcv = back_rows("ffn_down_bwd", dyb, g_down[i], ffc, [r], lambda acc, rr: (acc * (2.0 * rr.astype(F32)),), comm=comm)
        if i == 0:
            *rcv, gathered_early = rcv
        absorb(rcv)
        if i == 0 and any(nm == "down" for nm, _, _ in summed):
            nm_, l_, part = summed.pop([nm for nm, _, _ in summed].index("down"))
            comm, names = _chip_exchange([part], [(nm_, l_)], layers, stacked)
            g_down_i, *bufs = wgrad_rows("ffn_down_wgrad", s, dyb, ffc, D, tb, comm=comm)
            stacked.update(dict(zip(names, bufs)))
        else:
            g_down_i = wgrad_rows("ffn_down_wgrad", s, dyb, ffc, D, tb)
        pending.append(("down", i, g_down_i.reshape(N_DEV, ffc, D)))
        pending.append(("up", i, wgrad_cols("ffn_up_wgrad", h2, da, ffc)))
        dx, dyb, d_norm_ffn[i], *rcv = back_cols("ffn_up_bwd", da, transposed(g_up[i]), xm, norm_ffn[i], dx, comm=sibling_comm())
        absorb(rcv)
        if i % 2 == 0:
            h, lat, cq, ckv, q, kk, vv, o, lse = mix_saved
            w_dkv, w_uq, w_ukv, w_o = mla_w[l]
            do = _matmul("mla_out_bwd", dyb, w_o, [], grid=(nt,), a_spec=_row_spec(tm, D),
                         b_spec=pl.BlockSpec((None, OW, D), lambda i_: (0, 0, 0)), extra_specs=[],
                         out_shapes=[_sds((T, OW), BF16)], out_specs=[_row_spec(tm, OW)], dims=NT)[0]
            g_o_l = wgrad_rows("mla_out_wgrad", o, dyb, OW, D, tm).reshape(N_DEV, owc, D)
            comm, names = chip_comm()
            dq_pre, dk, dv, *bufs = _flash_bwd(q, kk, vv, o, do, lse, (t_cc, t_sa, t_sb), tq, comm=comm)
            stacked.update(dict(zip(names, bufs)))
            pending.append(("o", l, g_o_l))

            def kv_pre(dkb, dvb, cc, sa, sb):
                parts, dkr = [], None
                for b in range(HEADS):
                    parts += [dkb[:, b * QPAD:b * QPAD + NOPE], dvb[:, b * VDIM:(b + 1) * VDIM]]
                    piece = dkb[:, b * QPAD + NOPE:(b + 1) * QPAD].astype(F32)
                    dkr = piece if dkr is None else dkr + piece
                return jnp.concatenate(parts, axis=1), _rope_bwd(dkr, cc, sa, sb)

            dkv, dkr = _rowwise("mla_dkv_rope", kv_pre, [dk, dv, t_cc, t_sa, t_sb], grid=(T // ts,),
                                in_specs=[_row_spec(ts, HW), _row_spec(ts, OW)] + [_row_spec(ts, 128)] * 3,
                                out_shapes=[_sds((T, HW), BF16), _sds((T, 128), F32)], out_specs=[_row_spec(ts, HW), _row_spec(ts, 128)])
            g_uq_l = wgrad_rows("mla_q_wgrad", cq, dq_pre, Q_RANK, HW, tm)
            g_ukv_l = wgrad_rows("mla_kv_wgrad", ckv, dkv, KV_RANK, HW, tm)
            pending.append(("uq", l, g_uq_l.reshape(Q_RANK, HEADS, QPAD)[:, :, :NOPE + ROPE].transpose(1, 0, 2)))
            pending.append(("ukv", l, g_ukv_l.reshape(KV_RANK, HEADS, NOPE + VDIM).transpose(1, 0, 2)))
            dcq = _matmul("mla_q_bwd", dq_pre, w_uq, [], grid=(nt,), a_spec=_row_spec(tm, HW),
                          b_spec=pl.BlockSpec((None, Q_RANK, HW), lambda i_: (0, 0, 0)), extra_specs=[],
                          out_shapes=[_sds((T, Q_RANK), F32)], out_specs=[_row_spec(tm, Q_RANK)], dims=NT)[0]
            dckv = _matmul("mla_kv_bwd", dkv, w_ukv, [], grid=(nt,), a_spec=_row_spec(tm, HW),
                           b_spec=pl.BlockSpec((None, KV_RANK, HW), lambda i_: (0, 0, 0)), extra_specs=[],
                           out_shapes=[_sds((T, KV_RANK), F32)], out_specs=[_row_spec(tm, KV_RANK)], dims=NT)[0]

            def latent_bwd(la, qn, kvn, dq_, dkv_, dkr_):
                dcq_raw, dqn = _rms_bwd(la[:, :Q_RANK], qn, dq_)
                dckv_raw, dkvn = _rms_bwd(la[:, Q_RANK:Q_RANK + KV_RANK], kvn, dkv_)
                return jnp.concatenate([dcq_raw, dckv_raw, dkr_], axis=1), dqn, dkvn

            dlat, d_qn[l], d_kvn[l] = _rowwise(
                "mla_latent_bwd", latent_bwd, [lat, mla_q_norm[l].reshape(1, Q_RANK), mla_kv_norm[l].reshape(1, KV_RANK), dcq, dckv, dkr],
                grid=(nt,), in_specs=[_row_spec(tm, LAT_PAD), _const_spec((1, Q_RANK)), _const_spec((1, KV_RANK)),
                                      _row_spec(tm, Q_RANK), _row_spec(tm, KV_RANK), _row_spec(tm, 128)],
                out_shapes=[_sds((T, LAT_PAD), BF16), _sds((1, Q_RANK), F32), _sds((1, KV_RANK), F32)],
                out_specs=[_row_spec(tm, LAT_PAD), _const_spec((1, Q_RANK)), _const_spec((1, KV_RANK))], n_acc=2)
            g_dkv_l = wgrad_rows("mla_down_wgrad", h, dlat, D, LAT_PAD, tm)
            pending.append(("dkv", l, g_dkv_l[:, :LAT].reshape(N_DEV, dc, LAT)))
            dx, dyb, d_norm_mix[i] = _matmul(
                "mla_down_bwd", dlat, w_dkv, [xs_i, norm_mix[i].reshape(1, D), dx], grid=(nt,), a_spec=_row_spec(tm, LAT_PAD),
                b_spec=pl.BlockSpec((None, D, LAT_PAD), lambda i_: (0, 0, 0)), extra_specs=[_row_spec(tm, D), _const_spec((1, D)), _row_spec(tm, D)],
                out_shapes=[_sds((T, D), F32), _sds((T, D), BF16), _sds((1, D), F32)],
                out_specs=[_row_spec(tm, D), _row_spec(tm, D), _const_spec((1, D))], dims=NT, epilogue=norm_bwd_epilogue, n_sum=1)
        else:
            h, gp, ge, gate = mix_saved
            (dgate,) = back_rows("sgu_out_bwd", dyb, g_out[l], ec, [], None)
            pending.append(("out", l, wgrad_rows("sgu_out_wgrad", gate, dyb, ec, D, tb).reshape(N_DEV, ec, D)))
            dz, d_wsp[l], d_bsp[l], d_lng[l], d_lnb[l] = _sgu_mid_bwd(ge, gp, dgate, ln_g_full[l], ln_b_full[l], sgu_w_spatial[l], b_sp[l], 4)
            pending.append(("in", l, wgrad_cols("sgu_in_wgrad", h, dz, e2c)))
            comm = sibling_comm()
            if i == 1:
                early = packed([jnp.stack(d_wsp, 0), jnp.stack(d_bsp, 0), jnp.concatenate(d_lng, 0), jnp.concatenate(d_lnb, 0)], early_sizes)
                comm = _merge_comm(comm, _gather_level1([early]))
            dx, dyb, d_norm_mix[i], *rcv = back_cols("sgu_in_bwd", dz, transposed(g_in[l]), xs_i, norm_mix[i], dx, comm=comm)
            if i == 1:
                *rcv, early_half = rcv
            absorb(rcv)
    grad_x = dx.reshape(1, T, D)

    last_comm, last_names = chip_comm()
    late_g = [jnp.concatenate(d_norm_mix, 0), jnp.concatenate(d_norm_ffn, 0), d_final, jnp.concatenate(d_qn, 0), jnp.concatenate(d_kvn, 0)]
    late_w = [norm_mix, norm_ffn, final_norm, mla_q_norm, mla_kv_norm]
    late_m = [m_norm_mix, m_norm_ffn, m_final_norm, m_mla_q_norm, m_mla_kv_norm]
    late_v = [v_norm_mix, v_norm_ffn, v_final_norm, v_mla_q_norm, v_mla_kv_norm]
    late_sizes = [pad_to(g.size // 128, 8) for g in late_g]
    late_rows = sum(late_sizes)

    def adam_big(parts, w, m, v):
        lyr, rws, cls = w.shape
        rt = _tile(rws, 512)

        def fn(p, w_, m_, v_):
            g = (p[0].astype(F32) + p[1].astype(F32)) + (p[2].astype(F32) + p[3].astype(F32))
            return (g, *_adam(w_, g, m_, v_))

        spec = pl.BlockSpec((None, rt, cls), lambda l_, i_: (l_, i_, 0))
        return _rowwise("adam_large", fn, [parts, w, m, v], grid=(lyr, rws // rt),
                        in_specs=[pl.BlockSpec((N_CHIP, None, rt, cls), lambda l_, i_: (0, l_, i_, 0)), spec, spec, spec],
                        out_shapes=[_sds(w.shape, F32)] * 4, out_specs=[spec] * 4)

    stacked.update(dict(zip(last_names, _comm_call("grad_chip_exchange", last_comm))))
    (gathered_late,) = _all_gather("gather_small_grads", [packed(late_g, late_sizes)])
    big = {}
    big["in"] = adam_big(stacked["in"], sgu_w_in, m_sgu_w_in, v_sgu_w_in)
    big["up"] = adam_big(stacked["up"], ffn_w_up, m_ffn_w_up, v_ffn_w_up)
    big["down"] = adam_big(stacked["down"], ffn_w_down, m_ffn_w_down, v_ffn_w_down)
    big["out"] = adam_big(stacked["out"], sgu_w_out, m_sgu_w_out, v_sgu_w_out)
    big["dkv"] = adam_big(stacked["dkv"], mla_w_dkv, m_mla_w_dkv, v_mla_w_dkv)
    big["uq"] = adam_big(stacked["uq"], mla_w_uq, m_mla_w_uq, v_mla_w_uq)
    big["ukv"] = adam_big(stacked["ukv"], mla_w_ukv, m_mla_w_ukv, v_mla_w_ukv)
    big["o"] = adam_big(stacked["o"], mla_w_o, m_mla_w_o, v_mla_w_o)
    big_res = [big[nm][:4] for nm in ("dkv", "uq", "ukv", "o", "in", "out", "up", "down")]

    def sum8(p):
        return ((p[0] + p[1]) + (p[2] + p[3])) + ((p[4] + p[5]) + (p[6] + p[7]))

    def adam_packed(name, gathered, ws, ms, vs, sizes, rows, tile):
        spec = _row_spec(tile, 128)
        return _rowwise(name, lambda p, w_, m_, v_: (sum8(p), *_adam(w_, sum8(p), m_, v_)),
                        [gathered, packed(ws, sizes), packed(ms, sizes), packed(vs, sizes)], grid=(rows // tile,),
                        in_specs=[pl.BlockSpec((N_DEV, tile, 128), lambda i_: (0, i_, 0)), spec, spec, spec],
                        out_shapes=[_sds((rows, 128), F32)] * 4, out_specs=[spec] * 4)

    late_res = adam_packed("adam_small", gathered_late, late_w, late_m, late_v, late_sizes, late_rows, late_rows)
    early_res = adam_packed("adam_spatial", gathered_early, [sgu_w_spatial, sgu_b_spatial], [m_sgu_w_spatial, m_sgu_b_spatial],
                            [v_sgu_w_spatial, v_sgu_b_spatial], early_sizes[:2], early_rep, SMALL_ROWS)

    def unpack(res, sizes, k, like):
        off = sum(sizes[:k])
        return res[off:off + like.size // 128].reshape(like.shape)

    my_b = 4 * lax.axis_index("x") + 2 * lax.axis_index("y") + lax.axis_index("c")
    ln_w = jnp.concatenate([sgu_ln_g, sgu_ln_b], 0)
    ln_m = jnp.concatenate([m_sgu_ln_g, m_sgu_ln_b], 0)
    ln_v = jnp.concatenate([v_sgu_ln_g, v_sgu_ln_b], 0)
    ln_all = jnp.concatenate([gathered_early[:, early_rep:early_rep + n_sgu * E // 128], gathered_early[:, early_rep + n_ln:early_rep + n_ln + n_sgu * E // 128]], axis=1)
    ln_mine = lax.dynamic_slice_in_dim(ln_all.reshape(N_DEV, 2 * n_sgu, N_DEV, ec), my_b, 1, axis=2).reshape(N_DEV, 2 * n_sgu, ec)
    ln_g_, ln_d, ln_m2, ln_v2 = _rowwise(
        "adam_ln", lambda p, w_, m_, v_: (sum8(p), *_adam(w_, sum8(p), m_, v_)), [ln_mine, ln_w, ln_m, ln_v], grid=(1,),
        in_specs=[_const_spec(ln_mine.shape), _const_spec(ln_w.shape), _const_spec(ln_w.shape), _const_spec(ln_w.shape)],
        out_shapes=[_sds(ln_w.shape, F32)] * 4, out_specs=[_const_spec(ln_w.shape)] * 4)

    def family(pos):
        ln = [ln_g_, ln_d, ln_m2, ln_v2][pos]
        late = [unpack(late_res[pos], late_sizes, k, w_) for k, w_ in enumerate(late_w)]
        w_sp_, b_sp_ = unpack(early_res[pos], early_sizes, 0, sgu_w_spatial), unpack(early_res[pos], early_sizes, 1, sgu_b_spatial)
        bigs = [res[pos] for res in big_res]
        return [late[0], late[1], late[2], bigs[0], late[3], late[4], bigs[1], bigs[2], bigs[3],
                bigs[4], ln[:n_sgu], ln[n_sgu:], w_sp_, b_sp_, bigs[5], bigs[6], bigs[7]]

    return (loss, grad_x, *family(0), *family(1), *family(2), *family(3))
```

```python
import math

import jax
import jax.numpy as jnp
from jax import lax
from jax.experimental import pallas as pl
from jax.experimental.pallas import tpu as pltpu

F32 = jnp.float32
BF16 = jnp.bfloat16
MESH = pl.DeviceIdType.MESH

N_DEV = 8
N_CHIP = 4
HEADS = 8
NOPE = 128
ROPE = 64
VDIM = 128
QPAD = 256
Q_RANK = 256
KV_RANK = 128
LAT = Q_RANK + KV_RANK + ROPE
LAT_PAD = 512
ROPE_THETA = 10000.0
SGU_CHUNK = 128
SGU_GROUPS = 8
NORM_EPS = 1e-6
LN_EPS = 1e-5
ADAM_LR = 0.001
ADAM_B1 = 0.9
ADAM_B2 = 0.999
ADAM_EPS = 1e-08
ADAM_WD = 0.01
ADAM_STEP = 10
ATTN_SCALE = (NOPE + ROPE) ** -0.5
NEG = -1e30
EXP2_SCALE = ATTN_SCALE * math.log2(math.e)
VMEM_LIMIT = 56 * 1024 * 1024
SMALL_ROWS = 256

NN = (((1,), (0,)), ((), ()))
NT = (((1,), (1,)), ((), ()))
TN = (((0,), (0,)), ((), ()))
ANY = pl.BlockSpec(memory_space=pl.ANY)


def _pcall(body, **kw):
    return pl.pallas_call(body, **kw)


def _params(n_grid, side_effects=False):
    return pltpu.CompilerParams(dimension_semantics=("arbitrary",) * n_grid, vmem_limit_bytes=VMEM_LIMIT, has_side_effects=side_effects)


def _sds(shape, dtype):
    return jax.ShapeDtypeStruct(tuple(shape), dtype)


def _tile(n, want):
    t = min(n, want)
    assert n % t == 0, (n, want)
    return t


class _Comm:
    def __init__(self, operands, out_shapes, aliases, scratch, start, finish):
        self.operands, self.out_shapes, self.aliases, self.scratch = operands, out_shapes, aliases, scratch
        self.start, self.finish = start, finish


def _merge_comm(first, second):
    n_in, n_out, n_sc = len(first.operands), len(first.out_shapes), len(first.scratch)
    aliases = dict(first.aliases)
    aliases.update({n_in + k: n_out + v for k, v in second.aliases.items()})

    def start(ins, outs, sems):
        first.start(ins[:n_in], outs[:n_out], sems[:n_sc])
        second.start(ins[n_in:], outs[n_out:], sems[n_sc:])

    def finish(ins, outs, sems):
        first.finish(ins[:n_in], outs[:n_out], sems[:n_sc])
        second.finish(ins[n_in:], outs[n_out:], sems[n_sc:])

    return _Comm([*first.operands, *second.operands], [*first.out_shapes, *second.out_shapes], aliases,
                 [*first.scratch, *second.scratch], start, finish)


def _place():
    return lax.axis_index("x"), lax.axis_index("y"), lax.axis_index("c")


def _other_chips(x, y):
    return [(1 - x, y), (x, 1 - y), (1 - x, 1 - y)]


def _dev_index(dev):
    return 4 * dev[0] + 2 * dev[1] + dev[2]


def _comm_call(name, comm):
    c_in, c_out = len(comm.operands), len(comm.out_shapes)

    def body(*refs):
        ins, outs, sems = refs[:c_in], refs[c_in:c_in + c_out], refs[c_in + c_out:]
        comm.start(ins, outs, sems)
        comm.finish(ins, outs, sems)

    return _pcall(body, name=name, in_specs=[ANY] * c_in, out_specs=[ANY] * c_out, out_shape=comm.out_shapes,
                  scratch_shapes=comm.scratch, input_output_aliases=dict(comm.aliases),
                  compiler_params=pltpu.CompilerParams(has_side_effects=True))(*comm.operands)


def _call(name, body, operands, in_specs, out_shapes, out_specs, scratch, grid, comm=None):
    if comm is None:
        return _pcall(body, name=name, grid=grid, in_specs=in_specs, out_specs=out_specs, out_shape=out_shapes,
                      scratch_shapes=scratch, compiler_params=_params(len(grid)))(*operands)
    n_in, n_out, n_sc = len(operands), len(out_shapes), len(scratch)
    c_in, c_out = len(comm.operands), len(comm.out_shapes)

    def hosted(*refs):
        ins, cins = refs[:n_in], refs[n_in:n_in + c_in]
        o0 = n_in + c_in
        outs, couts = refs[o0:o0 + n_out], refs[o0 + n_out:o0 + n_out + c_out]
        rest = refs[o0 + n_out + c_out:]
        sc, csems = rest[:n_sc], rest[n_sc:]
        first = pl.program_id(0) == 0
        last = pl.program_id(0) == grid[0] - 1
        for d in range(1, len(grid)):
            first = jnp.logical_and(first, pl.program_id(d) == 0)
            last = jnp.logical_and(last, pl.program_id(d) == grid[d] - 1)

        @pl.when(first)
        def _():
            comm.start(cins, couts, csems)

        body(*ins, *outs, *sc)

        @pl.when(last)
        def _():
            comm.finish(cins, couts, csems)

    return _pcall(hosted, name=name, grid=grid, in_specs=[*in_specs, *[ANY] * c_in], out_specs=[*out_specs, *[ANY] * c_out],
                  out_shape=[*out_shapes, *comm.out_shapes], scratch_shapes=[*scratch, *comm.scratch],
                  input_output_aliases={n_in + k: n_out + v for k, v in comm.aliases.items()},
                  compiler_params=_params(len(grid), side_effects=True))(*operands, *comm.operands)


def _gather_level1(shards, part=(0, 1), into=None):
    n = len(shards)
    k_part, m_part = part

    def copies(ins, outs, sems):
        send_sems, recv_sems, local_sems = sems
        x, y, c = _place()
        me, sibling = (x, y, c), (x, y, 1 - c)
        chips = _other_chips(x, y)

        def rows(a):
            r = shards[a].shape[0] // m_part
            return pl.ds(k_part * r, r)

        def copy(a, k, block, to, own=False):
            slot = outs[a].at[_dev_index(block), rows(a)]
            return pltpu.make_async_remote_copy(src_ref=ins[a].at[rows(a)] if own else slot, dst_ref=slot, send_sem=send_sems.at[a, k],
                                                recv_sem=recv_sems.at[a, k], device_id=to, device_id_type=MESH)

        mine = [pltpu.make_async_copy(ins[a].at[rows(a)], outs[a].at[_dev_index(me), rows(a)], local_sems.at[a]) for a in range(n)]
        sends = [copy(a, 1 + j, me, (*chip, c), own=True) for j, chip in enumerate(chips) for a in range(n)]
        sends += [copy(a, 0, me, sibling, own=True) for a in range(n)]
        recvs = [copy(a, 1 + j, (*chip, c), me) for j, chip in enumerate(chips) for a in range(n)]
        recvs += [copy(a, 0, sibling, me) for a in range(n)]
        return mine, sends, recvs

    def start(ins, outs, sems):
        mine, sends, _ = copies(ins, outs, sems)
        for cp in mine + sends:
            cp.start()

    def finish(ins, outs, sems):
        mine, sends, recvs = copies(ins, outs, sems)
        for cp in recvs:
            cp.wait_recv()
        for cp in sends:
            cp.wait_send()
        for cp in mine:
            cp.wait()

    return _Comm([*shards, *(into or [])], [_sds((N_DEV, *a.shape), a.dtype) for a in shards], {n + a: a for a in range(n)} if into else {},
                 [pltpu.SemaphoreType.DMA((n, 4)), pltpu.SemaphoreType.DMA((n, 4)), pltpu.SemaphoreType.DMA((n,))], start, finish)


def _gather_level2(bufs):
    n = len(bufs)

    def copies(outs, sems):
        send_sems, recv_sems = sems
        x, y, c = _place()
        sibling = (x, y, 1 - c)
        sends, recvs = [], []
        for j, chip in enumerate(_other_chips(x, y)):
            for a in range(n):
                have, want = outs[a].at[_dev_index((*chip, c))], outs[a].at[_dev_index((*chip, 1 - c))]
                sends.append(pltpu.make_async_remote_copy(src_ref=have, dst_ref=have, send_sem=send_sems.at[a, j], recv_sem=recv_sems.at[a, j],
                                                          device_id=sibling, device_id_type=MESH))
                recvs.append(pltpu.make_async_remote_copy(src_ref=want, dst_ref=want, send_sem=send_sems.at[a, j], recv_sem=recv_sems.at[a, j],
                                                          device_id=sibling, device_id_type=MESH))
        return sends, recvs

    def start(ins, outs, sems):
        for cp in copies(outs, sems)[0]:
            cp.start()

    def finish(ins, outs, sems):
        sends, recvs = copies(outs, sems)
        for cp in recvs:
            cp.wait_recv()
        for cp in sends:
            cp.wait_send()

    return _Comm(bufs, [_sds(b.shape, b.dtype) for b in bufs], {a: a for a in range(n)},
                 [pltpu.SemaphoreType.DMA((n, 3)), pltpu.SemaphoreType.DMA((n, 3))], start, finish)


def _all_gather(name, arrays):
    n = len(arrays)

    def body(*refs):
        ins = refs[:n]
        outs = refs[n:2 * n]
        send_sems, recv_sems, local_sems = refs[2 * n:]
        x, y, c = _place()
        me, sibling = (x, y, c), (x, y, 1 - c)
        chips = _other_chips(x, y)

        def copy(a, k, block, to, src=None):
            slot = outs[a].at[_dev_index(block)]
            return pltpu.make_async_remote_copy(src_ref=slot if src is None else src, dst_ref=slot, send_sem=send_sems.at[a, k],
                                                recv_sem=recv_sems.at[a, k], device_id=to, device_id_type=MESH)

        mine = [pltpu.make_async_copy(ins[a], outs[a].at[_dev_index(me)], local_sems.at[a]) for a in range(n)]
        for cp in mine:
            cp.start()
        first = []
        for j, chip in enumerate(chips):
            first += [copy(a, 1 + j, me, (*chip, c), src=ins[a]) for a in range(n)]
        first += [copy(a, 0, me, sibling, src=ins[a]) for a in range(n)]
        for cp in first:
            cp.start()
        passed = []
        for j, chip in enumerate(chips):
            for a in range(n):
                copy(a, 1 + j, (*chip, c), me).wait_recv()
                fwd = copy(a, 4 + j, (*chip, c), sibling)
                fwd.start()
                passed.append(fwd)
        for a in range(n):
            copy(a, 0, sibling, me).wait_recv()
            for j, chip in enumerate(chips):
                copy(a, 4 + j, (*chip, 1 - c), me).wait_recv()
        for cp in first + passed:
            cp.wait_send()
        for cp in mine:
            cp.wait()

    return _pcall(
        body, name=name, in_specs=[ANY] * n, out_specs=[ANY] * n,
        out_shape=[_sds((N_DEV, *a.shape), a.dtype) for a in arrays],
        scratch_shapes=[pltpu.SemaphoreType.DMA((n, 7)), pltpu.SemaphoreType.DMA((n, 7)), pltpu.SemaphoreType.DMA((n,))],
        compiler_params=pltpu.CompilerParams(has_side_effects=True),
    )(*arrays)


def _sibling_exchange(grads):
    n = len(grads)

    def start(ins, outs, sems):
        send_sems, recv_sems = sems
        x, y, c = _place()
        for a in range(n):
            for ch in range(N_CHIP):
                pltpu.make_async_remote_copy(src_ref=ins[a].at[2 * ch + 1 - c], dst_ref=outs[a].at[ch], send_sem=send_sems.at[a],
                                             recv_sem=recv_sems.at[a], device_id=(x, y, 1 - c), device_id_type=MESH).start()

    def finish(ins, outs, sems):
        send_sems, recv_sems = sems
        x, y, c = _place()
        for a in range(n):
            pltpu.make_async_remote_copy(src_ref=outs[a], dst_ref=outs[a], send_sem=send_sems.at[a], recv_sem=recv_sems.at[a],
                                         device_id=(x, y, 1 - c), device_id_type=MESH).wait()

    return _Comm(grads, [_sds((N_CHIP, *g.shape[1:]), g.dtype) for g in grads], {},
                 [pltpu.SemaphoreType.DMA((n,)), pltpu.SemaphoreType.DMA((n,))], start, finish)


def _chip_exchange(parts, slots, layers, stacked):
    n = len(parts)
    names = []
    for nm, _ in slots:
        if nm not in names:
            names.append(nm)
    shapes = {nm: _sds((N_CHIP, layers[nm], *parts[a].shape[1:]), parts[a].dtype) for a, (nm, _) in enumerate(slots)}
    kept = [nm for nm in names if stacked.get(nm) is not None]
    aliases = {n + k: names.index(nm) for k, nm in enumerate(kept)}

    def copies(ins, outs, sems):
        send_sems, recv_sems, local_sems = sems
        x, y, c = _place()
        mine = 2 * x + y
        local, sends, recvs = [], [], []
        for a, (nm, l) in enumerate(slots):
            buf = outs[names.index(nm)]
            local.append(pltpu.make_async_copy(ins[a].at[mine], buf.at[mine, l], local_sems.at[a]))
            for j, chip in enumerate(_other_chips(x, y)):
                theirs = buf.at[2 * chip[0] + chip[1], l]
                sends.append(pltpu.make_async_remote_copy(src_ref=ins[a].at[2 * chip[0] + chip[1]], dst_ref=buf.at[mine, l], send_sem=send_sems.at[a, j],
                                                          recv_sem=recv_sems.at[a, j], device_id=(*chip, c), device_id_type=MESH))
                recvs.append(pltpu.make_async_remote_copy(src_ref=theirs, dst_ref=theirs, send_sem=send_sems.at[a, j],
                                                          recv_sem=recv_sems.at[a, j], device_id=(*chip, c), device_id_type=MESH))
        return local, sends, recvs

    def start(ins, outs, sems):
        local, sends, _ = copies(ins, outs, sems)
        for cp in local + sends:
            cp.start()

    def finish(ins, outs, sems):
        local, sends, recvs = copies(ins, outs, sems)
        for cp in recvs:
            cp.wait_recv()
        for cp in sends:
            cp.wait_send()
        for cp in local:
            cp.wait()

    comm = _Comm([*parts, *[stacked[nm] for nm in kept]], [shapes[nm] for nm in names], aliases,
                 [pltpu.SemaphoreType.DMA((n, 3)), pltpu.SemaphoreType.DMA((n, 3)), pltpu.SemaphoreType.DMA((n,))], start, finish)
    return comm, names


def _matmul(name, a, b, extras, *, grid, a_spec, b_spec, extra_specs, out_shapes, out_specs, dims, k_axis=None, nk=1,
            acc_shape=None, epilogue=None, comm=None, n_sum=0, write=None):
    n_extra = len(extras)
    n_out = len(out_shapes)

    def body(*refs):
        a_ref, b_ref = refs[0], refs[1]
        ex = refs[2:2 + n_extra]
        outs = refs[2 + n_extra:2 + n_extra + n_out]
        prod = lax.dot_general(a_ref[...], b_ref[...], dims, preferred_element_type=F32)

        def finish(acc):
            if write is not None:
                write(outs, acc, *[e[...] for e in ex])
                return
            res = epilogue(acc, *[e[...] for e in ex]) if epilogue is not None else (acc,)
            first = None
            for d in range(len(grid)):
                if d != k_axis:
                    here = pl.program_id(d) == 0
                    first = here if first is None else jnp.logical_and(first, here)
            for idx, (o, r) in enumerate(zip(outs, res)):
                if idx < n_out - n_sum:
                    o[...] = r.astype(o.dtype)
                else:
                    @pl.when(first)
                    def _(o=o, r=r):
                        o[...] = r.astype(o.dtype)

                    @pl.when(jnp.logical_not(first))
                    def _(o=o, r=r):
                        o[...] += r.astype(o.dtype)

        if k_axis is None:
            finish(prod)
        else:
            acc_ref = refs[-1]
            k = pl.program_id(k_axis)

            @pl.when(k == 0)
            def _():
                acc_ref[...] = prod

            @pl.when(k > 0)
            def _():
                acc_ref[...] += prod

            @pl.when(k == nk - 1)
            def _():
                finish(acc_ref[...])

    scratch = [] if k_axis is None else [pltpu.VMEM(acc_shape, F32)]
    return _call(name, body, [a, b, *extras], [a_spec, b_spec, *extra_specs], list(out_shapes), list(out_specs), scratch, grid, comm)


def _rowwise(name, fn, operands, *, grid, in_specs, out_shapes, out_specs, n_acc=0, grid_spec_prefetch=None, comm=None):
    n_in = len(operands)
    n_out = len(out_shapes)
    n_pre = 0 if grid_spec_prefetch is None else 1

    def body(*refs):
        refs = refs[n_pre:]
        ins = refs[:n_in]
        outs = refs[n_in:n_in + n_out]
        res = fn(*[r[...] for r in ins])
        if not isinstance(res, (tuple, list)):
            res = (res,)
        first = pl.program_id(0) == 0
        for d in range(1, len(grid)):
            first = jnp.logical_and(first, pl.program_id(d) == 0)
        for idx, (o, r) in enumerate(zip(outs, res)):
            if idx < n_out - n_acc:
                o[...] = r.astype(o.dtype)
            else:
                @pl.when(first)
                def _(o=o, r=r):
                    o[...] = r.astype(o.dtype)

                @pl.when(jnp.logical_not(first))
                def _(o=o, r=r):
                    o[...] += r.astype(o.dtype)

    if comm is not None:
        return _call(name, body, list(operands), list(in_specs), list(out_shapes), list(out_specs), [], grid, comm)
    if grid_spec_prefetch is None:
        return _pcall(body, name=name, grid=grid, in_specs=in_specs, out_specs=out_specs, out_shape=out_shapes,
                      compiler_params=_params(len(grid)))(*operands)
    gs = pltpu.PrefetchScalarGridSpec(num_scalar_prefetch=1, grid=grid, in_specs=in_specs, out_specs=out_specs)
    return _pcall(body, name=name, grid_spec=gs, out_shape=out_shapes,
                  compiler_params=_params(len(grid)))(grid_spec_prefetch, *operands)


def _row_spec(tm, w):
    return pl.BlockSpec((tm, w), lambda i: (i, 0))


def _const_spec(shape):
    nd = len(shape)
    return pl.BlockSpec(tuple(shape), lambda *_: (0,) * nd)


def _rms_fwd(x, g):
    r = lax.rsqrt(jnp.mean(x * x, axis=-1, keepdims=True) + NORM_EPS)
    return x * r * g


def _rms_bwd(x, g, dy):
    r = lax.rsqrt(jnp.mean(x * x, axis=-1, keepdims=True) + NORM_EPS)
    xh = x * r
    u = dy * g
    dx = r * (u - xh * jnp.mean(u * xh, axis=-1, keepdims=True))
    dg = jnp.sum(dy * xh, axis=0, keepdims=True)
    return dx, dg


def _gelu_and_grad(z):
    cdf = 0.5 * (1.0 + lax.erf(z * (2.0 ** -0.5)))
    return cdf + z * jnp.exp(-0.5 * z * z) * ((2.0 * math.pi) ** -0.5), z * cdf


def _rope_fwd(x, cc, sa, sb):
    return x * cc + pltpu.roll(x, 96, 1) * sa + pltpu.roll(x, 32, 1) * sb


def _rope_bwd(d, cc, sa, sb):
    return d * cc + pltpu.roll(d * sa, 32, 1) + pltpu.roll(d * sb, 96, 1)


def _adam(w, g, m, v):
    m = ADAM_B1 * m + (1.0 - ADAM_B1) * g
    v = ADAM_B2 * v + (1.0 - ADAM_B2) * (g * g)
    m_hat = m / (1.0 - ADAM_B1 ** ADAM_STEP)
    v_hat = v / (1.0 - ADAM_B2 ** ADAM_STEP)
    delta = -ADAM_LR * (m_hat / (jnp.sqrt(v_hat) + ADAM_EPS) + ADAM_WD * w)
    return delta, m, v


def _flash_fwd(q, k, vt, tq, comm=None):
    h, t = vt.shape[0], q.shape[0]
    nq = t // tq

    chunk_blocks = [c for c in (4, 2) if c < nq]

    def body(q_ref, k_ref, vt_ref, o_ref, lse_ref, m_ref, l_ref, acc_ref):
        qi = pl.program_id(1)
        m_ref[...] = jnp.full((1, tq), NEG, F32)
        l_ref[...] = jnp.zeros((1, tq), F32)
        acc_ref[...] = jnp.zeros((VDIM, tq), F32)

        def update(kb0, nblk, masked):
            kb = k_ref[pl.ds(pl.multiple_of(kb0 * tq, tq), nblk * tq), :]
            st = lax.dot_general(kb, q_ref[...], NT, preferred_element_type=F32)
            if masked:
                key = lax.broadcasted_iota(jnp.int32, (nblk * tq, tq), 0) - (nblk - 1) * tq
                qry = lax.broadcasted_iota(jnp.int32, (nblk * tq, tq), 1)
                st = jnp.where(key <= qry, st, NEG)
            m_old = m_ref[...]
            m_new = jnp.maximum(m_old, jnp.max(st, axis=0, keepdims=True))
            alpha = jnp.exp2((m_old - m_new) * EXP2_SCALE)
            pt = jnp.exp2((st - m_new) * EXP2_SCALE)
            l_ref[...] = alpha * l_ref[...] + jnp.sum(pt, axis=0, keepdims=True)
            ptb = pt.astype(BF16)
            pv = lax.dot_general(vt_ref[kb0], ptb[:tq], NN, preferred_element_type=F32)
            for j in range(1, nblk):
                pv += lax.dot_general(vt_ref[kb0 + j], ptb[j * tq:(j + 1) * tq], NN, preferred_element_type=F32)
            acc_ref[...] = alpha * acc_ref[...] + pv
            m_ref[...] = m_new

        start = jnp.int32(0)
        for c in chunk_blocks:
            take = (qi & c) != 0

            @pl.when(take)
            def _(start=start, c=c):
                update(start, c, False)

            start = start + jnp.where(take, c, 0)
        if nq > 1:
            @pl.when((qi & 1) != 0)
            def _():
                update(qi - 1, 2, True)

            @pl.when((qi & 1) == 0)
            def _():
                update(qi, 1, True)
        else:
            update(qi, 1, True)
        l = l_ref[...]
        o_ref[...] = (acc_ref[...] / l).T.astype(o_ref.dtype)
        lse_ref[...] = m_ref[...] * EXP2_SCALE + jnp.log2(l)

    return _call(
        "flash_fwd", body, [q, k, vt],
        [pl.BlockSpec((tq, QPAD), lambda hh, i: (i, hh)),
         pl.BlockSpec((t, QPAD), lambda hh, i: (0, hh)),
         pl.BlockSpec((None, nq, VDIM, tq), lambda hh, i: (hh, 0, 0, 0))],
        [_sds((t, h * VDIM), BF16), _sds((h, nq, 1, tq), F32)],
        [pl.BlockSpec((tq, VDIM), lambda hh, i: (i, hh)),
         pl.BlockSpec((None, None, 1, tq), lambda hh, i: (hh, i, 0, 0))],
        [pltpu.VMEM((1, tq), F32), pltpu.VMEM((1, tq), F32), pltpu.VMEM((VDIM, tq), F32)], (h, nq), comm)


def _flash_bwd(q, k, v, o, do, lse, tabs, tq, comm=None):
    t = q.shape[0]
    h = q.shape[1] // QPAD
    nq = t // tq

    def body(q_ref, k_ref, v_ref, o_ref, do_ref, lse_ref, cc_ref, sa_ref, sb_ref, dq_ref, dk_out, dv_out, delta_ref, dqt_ref, dk_ref, dv_ref):
        kj = pl.program_id(1)

        @pl.when(kj == 0)
        def _():
            dqt_ref[...] = jnp.zeros_like(dqt_ref)
            ones = jnp.ones((8, VDIM), BF16)
            for qi in range(nq):
                rows = pl.ds(qi * tq, tq)
                prod = do_ref[rows, :].astype(F32) * o_ref[rows, :].astype(F32)
                hi = prod.astype(BF16)
                lo = (prod - hi.astype(F32)).astype(BF16)
                delta_ref[qi] = (lax.dot_general(ones, hi, NT, preferred_element_type=F32)
                                 + lax.dot_general(ones, lo, NT, preferred_element_type=F32))

        kb = k_ref[...]
        vb = v_ref[...]
        kbt = kb.astype(F32).T.astype(BF16)
        dk_ref[...] = jnp.zeros_like(dk_ref)
        dv_ref[...] = jnp.zeros_like(dv_ref)

        def step(q0, nblk, masked):
            rows = pl.ds(pl.multiple_of(q0 * tq, tq), nblk * tq)
            qb = q_ref[rows, :]
            dob = do_ref[rows, :]
            lse = jnp.concatenate([lse_ref[q0 + j] for j in range(nblk)], axis=1)
            delta = jnp.concatenate([delta_ref[q0 + j, pl.ds(0, 1), :] for j in range(nblk)], axis=1)
            st = lax.dot_general(kb, qb, NT, preferred_element_type=F32)
            pt = jnp.exp2(st * EXP2_SCALE - lse)
            if masked:
                key = lax.broadcasted_iota(jnp.int32, (tq, nblk * tq), 0)
                qry = lax.broadcasted_iota(jnp.int32, (tq, nblk * tq), 1)
                pt = jnp.where(key <= qry, pt, 0.0)
            dv_ref[...] += lax.dot_general(pt.astype(BF16), dob, NN, preferred_element_type=F32)
            dpt = lax.dot_general(vb, dob, NT, preferred_element_type=F32)
            dst = (pt * (dpt - delta) * ATTN_SCALE).astype(BF16)
            dk_ref[...] += lax.dot_general(dst, qb, NN, preferred_element_type=F32)
            dqt = lax.dot_general(kbt, dst, NN, preferred_element_type=F32)
            for j in range(nblk):
                dqt_ref[q0 + j] += dqt[:, j * tq:(j + 1) * tq]

        later = nq - 1 - kj
        if nq > 1:
            @pl.when((later & 1) != 0)
            def _():
                step(kj, 2, True)

            @pl.when((later & 1) == 0)
            def _():
                step(kj, 1, True)
        else:
            step(kj, 1, True)
        start = kj + 1 + (later & 1)
        for c in [c for c in (2, 4) if c < nq]:
            take = (later & c) != 0

            @pl.when(take)
            def _(start=start, c=c):
                step(start, c, False)

            start = start + jnp.where(take, c, 0)
        dk_out[...] = dk_ref[...].astype(BF16)
        dv_out[...] = dv_ref[...].astype(BF16)

        @pl.when(kj == nq - 1)
        def _():
            for qi in range(nq):
                rows = pl.ds(qi * tq, tq)
                d = dqt_ref[qi].T
                roped = _rope_bwd(d[:, NOPE:], cc_ref[rows, :], sa_ref[rows, :], sb_ref[rows, :])
                dq_ref[rows, :] = jnp.concatenate([d[:, :NOPE], roped], axis=1).astype(BF16)

    head_q = pl.BlockSpec((t, QPAD), lambda hh, j: (0, hh))
    head_v = pl.BlockSpec((t, VDIM), lambda hh, j: (0, hh))
    table = pl.BlockSpec((t, 128), lambda hh, j: (0, 0))
    return _call(
        "flash_bwd", body, [q, k, v, o, do, lse, *tabs],
        [head_q, pl.BlockSpec((tq, QPAD), lambda hh, j: (j, hh)), pl.BlockSpec((tq, VDIM), lambda hh, j: (j, hh)), head_v, head_v,
         pl.BlockSpec((None, nq, 1, tq), lambda hh, j: (hh, 0, 0, 0)), table, table, table],
        [_sds((t, h * QPAD), BF16), _sds((t, h * QPAD), BF16), _sds((t, h * VDIM), BF16)],
        [head_q, pl.BlockSpec((tq, QPAD), lambda hh, j: (j, hh)), pl.BlockSpec((tq, VDIM), lambda hh, j: (j, hh))],
        [pltpu.VMEM((nq, 8, tq), F32), pltpu.VMEM((nq, QPAD, tq), F32), pltpu.VMEM((tq, QPAD), F32), pltpu.VMEM((tq, VDIM), F32)], (h, nq), comm)


def _tril_bf16(w):
    row = lax.broadcasted_iota(jnp.int32, w.shape, 0)
    col = lax.broadcasted_iota(jnp.int32, w.shape, 1)
    return jnp.where(col <= row, w, 0.0).astype(BF16)


def _layer_norm_parts(v0):
    mu = jnp.mean(v0, axis=-1, keepdims=True)
    vc = v0 - mu
    rstd = lax.rsqrt(jnp.mean(vc * vc, axis=-1, keepdims=True) + LN_EPS)
    return vc * rstd, rstd


def _sgu_mid_fwd(ge, ln_g, ln_b, w_sp, b_sp, chunks_per_step):
    t, e2 = ge.shape
    e = e2 // 2
    gd = e // SGU_GROUPS
    rows = SGU_CHUNK * chunks_per_step

    def body(u_ref, v_ref, g_ref, b_ref, w_ref, bs_ref, gate_ref):
        for ck in range(chunks_per_step):
            r = pl.ds(ck * SGU_CHUNK, SGU_CHUNK)
            xh, _ = _layer_norm_parts(v_ref[r, :].astype(F32))
            v1 = (xh * g_ref[...] + b_ref[...]).astype(BF16)
            for g in range(SGU_GROUPS):
                cols = pl.ds(g * gd, gd)
                mixed = lax.dot_general(_tril_bf16(w_ref[g]), v1[:, g * gd:(g + 1) * gd], NN, preferred_element_type=F32) + bs_ref[g]
                gate_ref[r, cols] = (u_ref[r, cols].astype(F32) * mixed).astype(BF16)

    return _pcall(
        body, name="sgu_mid_fwd", grid=(t // rows,),
        in_specs=[pl.BlockSpec((rows, e), lambda i: (i, 0)), pl.BlockSpec((rows, e), lambda i: (i, 1)),
                  _const_spec((1, e)), _const_spec((1, e)), _const_spec(w_sp.shape), _const_spec(b_sp.shape)],
        out_specs=pl.BlockSpec((rows, e), lambda i: (i, 0)),
        out_shape=_sds((t, e), BF16), compiler_params=_params(1),
    )(ge, ge, ln_g, ln_b, w_sp, b_sp)


def _sgu_mid_bwd(ge, gp, dgate, ln_g, ln_b, w_sp, b_sp, chunks_per_step):
    t, e2 = ge.shape
    e = e2 // 2
    gd = e // SGU_GROUPS
    rows = SGU_CHUNK * chunks_per_step

    def body(u_ref, v_ref, zu_ref, zv_ref, dg_ref, g_ref, b_ref, w_ref, bs_ref, dz_ref, dw_ref, dbs_ref, dlg_ref, dlb_ref):
        @pl.when(pl.program_id(0) == 0)
        def _():
            dw_ref[...] = jnp.zeros_like(dw_ref)
            dbs_ref[...] = jnp.zeros_like(dbs_ref)
            dlg_ref[...] = jnp.zeros_like(dlg_ref)
            dlb_ref[...] = jnp.zeros_like(dlb_ref)

        for ck in range(chunks_per_step):
            r = pl.ds(ck * SGU_CHUNK, SGU_CHUNK)
            xh, rstd = _layer_norm_parts(v_ref[r, :].astype(F32))
            v1 = (xh * g_ref[...] + b_ref[...]).astype(BF16)
            dv1_parts = []
            for g in range(SGU_GROUPS):
                cols = pl.ds(g * gd, gd)
                wc = _tril_bf16(w_ref[g])
                v1g = v1[:, g * gd:(g + 1) * gd]
                mixed = lax.dot_general(wc, v1g, NN, preferred_element_type=F32) + bs_ref[g]
                dgate = dg_ref[r, cols].astype(F32)
                dmixed = dgate * u_ref[r, cols].astype(F32)
                du = dgate * mixed
                dz_ref[r, cols] = (du * zu_ref[r, cols].astype(F32)).astype(BF16)
                dbs_ref[g] += jnp.sum(dmixed, axis=1, keepdims=True)
                dmb = dmixed.astype(BF16)
                dwg = lax.dot_general(dmb, v1g, NT, preferred_element_type=F32)
                row = lax.broadcasted_iota(jnp.int32, dwg.shape, 0)
                col = lax.broadcasted_iota(jnp.int32, dwg.shape, 1)
                dw_ref[g] += jnp.where(col <= row, dwg, 0.0)
                dv1_parts.append(lax.dot_general(wc, dmb, TN, preferred_element_type=F32))
            dv1 = jnp.concatenate(dv1_parts, axis=1)
            dlg_ref[...] += jnp.sum(dv1 * xh, axis=0, keepdims=True)
            dlb_ref[...] += jnp.sum(dv1, axis=0, keepdims=True)
            dxh = dv1 * g_ref[...]
            dv0 = rstd * (dxh - jnp.mean(dxh, axis=-1, keepdims=True) - xh * jnp.mean(dxh * xh, axis=-1, keepdims=True))
            dz_ref[r, pl.ds(e, e)] = (dv0 * zv_ref[r, :].astype(F32)).astype(BF16)

    half0 = pl.BlockSpec((rows, e), lambda i: (i, 0))
    half1 = pl.BlockSpec((rows, e), lambda i: (i, 1))
    return _pcall(
        body, name="sgu_mid_bwd", grid=(t // rows,),
        in_specs=[half0, half1, half0, half1, half0, _const_spec((1, e)), _const_spec((1, e)), _const_spec(w_sp.shape), _const_spec(b_sp.shape)],
        out_specs=[pl.BlockSpec((rows, e2), lambda i: (i, 0)), _const_spec(w_sp.shape), _const_spec(b_sp.shape), _const_spec((1, e)), _const_spec((1, e))],
        out_shape=[_sds((t, e2), BF16), _sds(w_sp.shape, F32), _sds(b_sp.shape, F32), _sds((1, e), F32), _sds((1, e), F32)],
        compiler_params=_params(1),
    )(ge, ge, gp, gp, dgate, ln_g, ln_b, w_sp, b_sp)


def kernel(x, positions, norm_mix, norm_ffn, final_norm, mla_w_dkv, mla_q_norm, mla_kv_norm, mla_w_uq, mla_w_ukv, mla_w_o, sgu_w_in, sgu_ln_g, sgu_ln_b, sgu_w_spatial, sgu_b_spatial, sgu_w_out, ffn_w_up, ffn_w_down, loss_target, m_norm_mix, m_norm_ffn, m_final_norm, m_mla_w_dkv, m_mla_q_norm, m_mla_kv_norm, m_mla_w_uq, m_mla_w_ukv, m_mla_w_o, m_sgu_w_in, m_sgu_ln_g, m_sgu_ln_b, m_sgu_w_spatial, m_sgu_b_spatial, m_sgu_w_out, m_ffn_w_up, m_ffn_w_down, v_norm_mix, v_norm_ffn, v_final_norm, v_mla_w_dkv, v_mla_q_norm, v_mla_kv_norm, v_mla_w_uq, v_mla_w_ukv, v_mla_w_o, v_sgu_w_in, v_sgu_ln_g, v_sgu_ln_b, v_sgu_w_spatial, v_sgu_b_spatial, v_sgu_w_out, v_ffn_w_up, v_ffn_w_down):
    _, T, D = x.shape
    depth = norm_mix.shape[0]
    n_mla, n_sgu = mla_w_dkv.shape[0], sgu_w_in.shape[0]
    assert depth % 2 == 0
    FF = ffn_w_up.shape[2] * N_DEV
    E = sgu_w_out.shape[1] * N_DEV
    ffc, ec, e2c = FF // N_DEV, E // N_DEV, 2 * E // N_DEV
    dc = D // N_DEV
    OW = HEADS * VDIM
    HW = HEADS * QPAD
    owc = OW // N_DEV
    tm = _tile(T, 1024)
    tb = _tile(T, 4096)
    tk = _tile(T, 512)
    tq = _tile(T, 512)
    ts = _tile(T, 256)
    nt = T // tm
    x2 = x.reshape(T, D)
    tgt = loss_target.reshape(T, D)
    cidx = lax.axis_index("c").astype(jnp.int32).reshape(1)

    ln_local = jnp.concatenate([sgu_ln_g, sgu_ln_b, jnp.zeros((8 - 2 * n_sgu, ec), F32)], axis=0)
    mla_sh = [[w[l].astype(BF16) for w in (mla_w_dkv, mla_w_uq, mla_w_ukv, mla_w_o)] for l in range(n_mla)]

    def mla_layouts(g_dkv, g_uq, g_ukv, g_o):
        w_dkv = jnp.pad(g_dkv.reshape(1, D, LAT), ((0, 0), (0, 0), (0, LAT_PAD - LAT)))
        w_uq = jnp.pad(g_uq, ((0, 0), (0, 0), (0, QPAD - NOPE - ROPE))).transpose(1, 0, 2).reshape(1, Q_RANK, HEADS * QPAD)
        w_ukv = g_ukv.transpose(1, 0, 2).reshape(1, KV_RANK, HEADS * (NOPE + VDIM))
        return w_dkv, w_uq, w_ukv, g_o.reshape(1, HEADS * VDIM, D)

    mla_w = [None] * n_mla
    small_later = [a for l in range(1, n_mla) for a in mla_sh[l]] + [ln_local]
    ln_g_full, ln_b_full = [None] * n_sgu, [None] * n_sgu
    b_sp = sgu_b_spatial.reshape(n_sgu, SGU_GROUPS, SGU_CHUNK, 1)
    up_sh = [ffn_w_up[i].astype(BF16) for i in range(depth)]
    down_sh = [ffn_w_down[i].astype(BF16) for i in range(depth)]
    in_sh = [sgu_w_in[l].astype(BF16) for l in range(n_sgu)]
    out_sh = [sgu_w_out[l].astype(BF16) for l in range(n_sgu)]
    g_up, g_down, g_in, g_out = [None] * depth, [None] * depth, [None] * n_sgu, [None] * n_sgu

    inv_freq = ROPE_THETA ** (-jnp.arange(0, ROPE, 2, dtype=F32) / ROPE)
    zeros32 = jnp.zeros((ROPE // 2,), F32)
    inv128 = jnp.concatenate([inv_freq, inv_freq, zeros32, zeros32]).reshape(1, 128)
    sel_a = jnp.concatenate([-jnp.ones((32,), F32), zeros32, zeros32, zeros32]).reshape(1, 128)
    sel_b = jnp.concatenate([zeros32, jnp.ones((32,), F32), zeros32, zeros32]).reshape(1, 128)
    sel_c = jnp.concatenate([jnp.ones((64,), F32), zeros32, zeros32]).reshape(1, 128)

    def rope_tables(pos, inv, sa, sb, sc):
        ang = pos.astype(F32) * inv
        cs, sn = jnp.cos(ang), jnp.sin(ang)
        return cs * sc, sn * sa, sn * sb

    t_cc, t_sa, t_sb, *first_half = _rowwise(
        "rope_tables", rope_tables, [positions.reshape(T, 1), inv128, sel_a, sel_b, sel_c], grid=(nt,),
        in_specs=[_row_spec(tm, 1)] + [_const_spec((1, 128))] * 4,
        out_shapes=[_sds((T, 128), F32)] * 3, out_specs=[_row_spec(tm, 128)] * 3, comm=_gather_level1(mla_sh[0]))
    tab_specs = [_row_spec(tm, 128)] * 3

    def rmsnorm(xv, g, comm):
        return _rowwise("rmsnorm", lambda a, gg: _rms_fwd(a, gg), [xv, g.reshape(1, D)], grid=(nt,),
                        in_specs=[_row_spec(tm, D), _const_spec((1, D))], out_shapes=[_sds((T, D), BF16)], out_specs=[_row_spec(tm, D)], comm=comm)

    def proj_cols(name, h, gw, nc, epilogue, n_out, comm=None):
        return _matmul(name, h, gw, [], grid=(N_DEV, T // tb),
                       a_spec=pl.BlockSpec((tb, D), lambda j, i: (i, 0)),
                       b_spec=pl.BlockSpec((None, D, nc), lambda j, i: (j, 0, 0)), extra_specs=[],
                       out_shapes=[_sds((T, nc * N_DEV), BF16)] * n_out, out_specs=[pl.BlockSpec((tb, nc), lambda j, i: (i, j))] * n_out,
                       dims=NN, epilogue=epilogue, comm=comm)

    def residual_norm(acc, xr, g):
        xn = acc + xr
        return xn, _rms_fwd(xn, g)

    def proj_rows_residual(name, a, gw, xres, g_next, comm=None):
        kk_ = a.shape[1]
        return _matmul(name, a, gw.reshape(kk_, D), [xres, g_next.reshape(1, D)], grid=(T // tk,),
                       a_spec=_row_spec(tk, kk_), b_spec=_const_spec((kk_, D)), extra_specs=[_row_spec(tk, D), _const_spec((1, D))],
                       out_shapes=[_sds((T, D), F32), _sds((T, D), BF16)], out_specs=[_row_spec(tk, D)] * 2,
                       dims=NN, epilogue=residual_norm, comm=comm)

    def back_rows(name, dy, gw, kc, extras, epilogue, comm=None):
        return _matmul(name, dy, gw, extras, grid=(N_DEV, T // tb),
                       a_spec=pl.BlockSpec((tb, D), lambda j, i: (i, 0)),
                       b_spec=pl.BlockSpec((None, kc, D), lambda j, i: (j, 0, 0)),
                       extra_specs=[pl.BlockSpec((tb, kc), lambda j, i: (i, j))] * len(extras),
                       out_shapes=[_sds((T, kc * N_DEV), BF16)], out_specs=[pl.BlockSpec((tb, kc), lambda j, i: (i, j))],
                       dims=NT, epilogue=epilogue, comm=comm)

    def norm_bwd_epilogue(dh, xv, g, dxi):
        dxn, dg = _rms_bwd(xv, g, dh)
        return dxi + dxn, dxi + dxn, dg

    def transposed(gw):
        return gw.transpose(0, 2, 1).reshape(gw.shape[0] * gw.shape[2], D)

    def back_cols(name, da, gwt, xv, g, dx_in, comm=None):
        n = da.shape[1]
        row = _row_spec(tk, D)
        return _matmul(name, da, gwt, [xv, g.reshape(1, D), dx_in], grid=(T // tk,),
                       a_spec=_row_spec(tk, n), b_spec=_const_spec((n, D)), extra_specs=[row, _const_spec((1, D)), row],
                       out_shapes=[_sds((T, D), F32), _sds((T, D), BF16), _sds((1, D), F32)], out_specs=[row, row, _const_spec((1, D))],
                       dims=NN, epilogue=norm_bwd_epilogue, n_sum=1, comm=comm)

    def token_sum(tt):
        return dict(k_axis=1, nk=T // tt) if T // tt > 1 else dict(k_axis=None)

    def wgrad_cols(name, h, da, nc):
        return _matmul(name, h, da, [], grid=(N_DEV, T // tb),
                       a_spec=pl.BlockSpec((tb, D), lambda j, t: (t, 0)), b_spec=pl.BlockSpec((tb, nc), lambda j, t: (t, j)),
                       extra_specs=[], out_shapes=[_sds((N_DEV, D, nc), BF16)],
                       out_specs=[pl.BlockSpec((None, D, nc), lambda j, t: (j, 0, 0))],
                       dims=TN, acc_shape=(D, nc), **token_sum(tb))[0]

    def wgrad_rows(name, a, dy, kc, ncols, tt, comm=None):
        res = _matmul(name, a, dy, [], grid=(a.shape[1] // kc, T // tt),
                      a_spec=pl.BlockSpec((tt, kc), lambda j, t: (t, j)), b_spec=pl.BlockSpec((tt, ncols), lambda j, t: (t, 0)),
                      extra_specs=[], out_shapes=[_sds((a.shape[1], ncols), BF16)],
                      out_specs=[pl.BlockSpec((kc, ncols), lambda j, t: (j, 0))],
                      dims=TN, acc_shape=(kc, ncols), comm=comm, **token_sum(tt))
        return res[0] if comm is None else res

    saved = []
    xs = x2
    for i in range(depth):
        l = i // 2
        if i == 0:
            h, *first_w = rmsnorm(xs, norm_mix[0], _gather_level2(first_half))
            mla_w[0] = mla_layouts(*first_w)
        if i % 2 == 0:
            w_dkv, w_uq, w_ukv, w_o = mla_w[l]
            lat = _matmul("mla_down", h, w_dkv, [], grid=(nt,), a_spec=_row_spec(tm, D),
                          b_spec=pl.BlockSpec((None, D, LAT_PAD), lambda i_: (0, 0, 0)), extra_specs=[],
                          out_shapes=[_sds((T, LAT_PAD), F32)], out_specs=[_row_spec(tm, LAT_PAD)], dims=NN)[0]

            def latent_post(la, qn, kvn, cc, sa, sb):
                cq = _rms_fwd(la[:, :Q_RANK], qn)
                ckv = _rms_fwd(la[:, Q_RANK:Q_RANK + KV_RANK], kvn)
                kr = _rope_fwd(la[:, Q_RANK + KV_RANK:], cc, sa, sb)
                return cq, ckv, kr

            cq, ckv, kr = _rowwise(
                "mla_latent", latent_post, [lat, mla_q_norm[l].reshape(1, Q_RANK), mla_kv_norm[l].reshape(1, KV_RANK), t_cc, t_sa, t_sb],
                grid=(nt,), in_specs=[_row_spec(tm, LAT_PAD), _const_spec((1, Q_RANK)), _const_spec((1, KV_RANK))] + tab_specs,
                out_shapes=[_sds((T, Q_RANK), BF16), _sds((T, KV_RANK), BF16), _sds((T, 128), BF16)],
                out_specs=[_row_spec(tm, Q_RANK), _row_spec(tm, KV_RANK), _row_spec(tm, 128)])

            def q_epilogue(acc, cc, sa, sb):
                parts = []
                for b in range(HEADS):
                    parts += [acc[:, b * QPAD:b * QPAD + NOPE], _rope_fwd(acc[:, b * QPAD + NOPE:(b + 1) * QPAD], cc, sa, sb)]
                return (jnp.concatenate(parts, axis=1),)

            q = _matmul("mla_q", cq, w_uq, [t_cc, t_sa, t_sb], grid=(nt,), a_spec=_row_spec(tm, Q_RANK),
                        b_spec=pl.BlockSpec((None, Q_RANK, HW), lambda i_: (0, 0, 0)), extra_specs=tab_specs,
                        out_shapes=[_sds((T, HW), BF16)], out_specs=[_row_spec(tm, HW)], dims=NN, epilogue=q_epilogue)[0]

            def kv_write(outs, acc, krb):
                k_ref, v_ref, vt_ref = outs
                for b in range(HEADS):
                    vb = acc[:, b * QPAD + NOPE:(b + 1) * QPAD]
                    k_ref[:, b * QPAD:b * QPAD + NOPE] = acc[:, b * QPAD:b * QPAD + NOPE].astype(BF16)
                    k_ref[:, b * QPAD + NOPE:(b + 1) * QPAD] = krb
                    v_ref[:, b * VDIM:(b + 1) * VDIM] = vb.astype(BF16)
                    vbt = vb.T.astype(BF16)
                    for u in range(tm // tq):
                        vt_ref[b, u] = vbt[:, u * tq:(u + 1) * tq]

            kk, vv, vt = _matmul("mla_kv", ckv, w_ukv, [kr], grid=(nt,), a_spec=_row_spec(tm, KV_RANK),
                                 b_spec=pl.BlockSpec((None, KV_RANK, HW), lambda i_: (0, 0, 0)), extra_specs=[_row_spec(tm, 128)],
                                 out_shapes=[_sds((T, HW), BF16), _sds((T, OW), BF16), _sds((HEADS, T // tq, VDIM, tq), BF16)],
                                 out_specs=[_row_spec(tm, HW), _row_spec(tm, OW), pl.BlockSpec((HEADS, tm // tq, VDIM, tq), lambda i_: (0, i_, 0, 0))],
                                 dims=NN, write=kv_write)
            group = [up_sh[i], down_sh[i], in_sh[l], out_sh[l]] + (small_later if i == 0 else [])
            o, lse, *bufs = _flash_fwd(q, kk, vt, tq, comm=_gather_level1(group))
            xm, h2, g_up[i], g_down[i] = _matmul(
                "mla_out", o, w_o, [xs, norm_ffn[i].reshape(1, D)], grid=(nt,), a_spec=_row_spec(tm, OW),
                b_spec=pl.BlockSpec((None, OW, D), lambda i_: (0, 0, 0)), extra_specs=[_row_spec(tm, D), _const_spec((1, D))],
                out_shapes=[_sds((T, D), F32), _sds((T, D), BF16)], out_specs=[_row_spec(tm, D)] * 2, dims=NN,
                epilogue=residual_norm, comm=_gather_level2(bufs[:2]))
            half_gathered = bufs[2:]
            mix_saved = (h, lat, cq, ckv, q, kk, vv, o, lse)
        else:
            gp, ge, g_down[i], up_half = proj_cols("sgu_in", h, g_in[l], e2c, _gelu_and_grad, 2,
                                                   comm=_merge_comm(_gather_level2([down_half]), _gather_level1([up_sh[i]])))
            gate = _sgu_mid_fwd(ge, ln_g_full[l], ln_b_full[l], sgu_w_spatial[l], b_sp[l], 4)
            xm, h2, g_up[i] = proj_rows_residual("sgu_out", gate, g_out[l], xs, norm_ffn[i], comm=_gather_level2([up_half]))
            mix_saved = (h, gp, ge, gate)
        r, s, *rest = proj_cols("ffn_up", h2, g_up[i], ffc, lambda acc: (jnp.maximum(acc, 0.0), jnp.square(jnp.maximum(acc, 0.0))), 2,
                                comm=_merge_comm(_gather_level2(half_gathered), _gather_level1([down_sh[i + 1]], part=(0, 2))) if i % 2 == 0 else None)
        if i % 2 == 0:
            g_in[l], g_out[l], *small_gathered, down_part = rest
        if i == 0:
            for l_ in range(1, n_mla):
                mla_w[l_] = mla_layouts(*small_gathered[4 * (l_ - 1):4 * l_])
            g_ln = small_gathered[-1]
            ln_g_full = [g_ln[:, l_, :].reshape(1, E) for l_ in range(n_sgu)]
            ln_b_full = [g_ln[:, n_sgu + l_, :].reshape(1, E) for l_ in range(n_sgu)]
        xo, h_next, *rest = proj_rows_residual("ffn_down", s, g_down[i], xm, norm_mix[i + 1] if i + 1 < depth else final_norm,
                                               comm=_gather_level1([down_sh[i + 1]], part=(1, 2), into=[down_part]) if i % 2 == 0 else None)
        if i % 2 == 0:
            (down_half,) = rest
        saved.append((xs, xm, mix_saved, h2, r, s))
        xs, h = xo, h_next

    def loss_head(xv, tg, g):
        y = _rms_fwd(xv, g)
        err = y - tg
        part = 0.5 * jnp.sum(jnp.sum(err * err, axis=-1, keepdims=True), axis=0, keepdims=True) / D
        dx, dg = _rms_bwd(xv, g, err / D)
        return dx, dx, jnp.broadcast_to(part, (1, 128)), dg

    dx, dyb, loss_part, d_final = _rowwise(
        "loss_head", loss_head, [xs, tgt, final_norm.reshape(1, D)], grid=(nt,),
        in_specs=[_row_spec(tm, D), _row_spec(tm, D), _const_spec((1, D))],
        out_shapes=[_sds((T, D), F32), _sds((T, D), BF16), _sds((1, 128), F32), _sds((1, D), F32)],
        out_specs=[_row_spec(tm, D), _row_spec(tm, D), _const_spec((1, 128)), _const_spec((1, D))], n_acc=2)
    loss = lax.psum(loss_part[0, 0], ("x", "y", "c"))

    d_norm_mix, d_norm_ffn = [None] * depth, [None] * depth
    d_qn, d_kvn = [None] * n_mla, [None] * n_mla
    d_wsp, d_bsp, d_lng, d_lnb = [None] * n_sgu, [None] * n_sgu, [None] * n_sgu, [None] * n_sgu
    layers = {"dkv": n_mla, "uq": n_mla, "ukv": n_mla, "o": n_mla, "in": n_sgu, "out": n_sgu, "up": depth, "down": depth}
    stacked = {nm: None for nm in layers}
    pending = []
    summed = []

    def add_pairs(gs, rcvs):
        operands, in_specs, out_shapes, out_specs = [], [], [], []
        for g, rcv in zip(gs, rcvs):
            _, rws, cls = g.shape
            slab = pl.BlockSpec((None, rws, cls), lambda ch, cr: (ch, 0, 0))
            operands += [g.reshape(N_CHIP, 2, rws, cls), rcv]
            in_specs += [pl.BlockSpec((None, None, rws, cls), lambda ch, cr: (ch, cr[0], 0, 0)), slab]
            out_shapes.append(_sds(rcv.shape, BF16))
            out_specs.append(slab)

        def fn(*blocks):
            return tuple(blocks[2 * k].astype(F32) + blocks[2 * k + 1].astype(F32) for k in range(len(gs)))

        return _rowwise("grad_pair_sum", fn, operands, grid=(N_CHIP,), in_specs=in_specs, out_shapes=out_shapes, out_specs=out_specs,
                        grid_spec_prefetch=cidx)

    def sibling_comm():
        return _sibling_exchange([g for _, _, g in pending]) if pending else None

    def absorb(from_sibling):
        if pending:
            parts = add_pairs([g for _, _, g in pending], list(from_sibling))
            summed.extend((nm, l_, p) for (nm, l_, _), p in zip(pending, parts))
            pending.clear()

    def chip_comm():
        if pending:
            absorb(_comm_call("grad_sibling_exchange", sibling_comm()))
        comm, names = _chip_exchange([p for _, _, p in summed], [(nm, l_) for nm, l_, _ in summed], layers, stacked)
        summed.clear()
        return comm, names

    def rows128(a, rows):
        flat = a.reshape(-1, 128)
        return jnp.pad(flat, ((0, rows - flat.shape[0]), (0, 0)))

    def pad_to(n, mult):
        return -(-n // mult) * mult

    def packed(arrs, sizes):
        return jnp.concatenate([rows128(a, sz) for a, sz in zip(arrs, sizes)], axis=0)

    n_wsp, n_bsp, n_ln = sgu_w_spatial.size // 128, pad_to(sgu_b_spatial.size // 128, 8), pad_to(n_sgu * E // 128, 8)
    early_sizes = [n_wsp, pad_to(n_wsp + n_bsp, SMALL_ROWS) - n_wsp, n_ln, n_ln]
    early_rep = early_sizes[0] + early_sizes[1]
    gathered_early = None

    for i in reversed(range(depth)):
        l = i // 2
        xs_i, xm, mix_saved, h2, r, s = saved[i]
        comm = sibling_comm()
        if i == 0:
            comm = _gather_level2([early_half]) if comm is None else _merge_comm(comm, _gather_level2([early_half]))
        da, *rcv = back_rows("ffn_down_bwd", dyb, g_down[i], ffc, [r], lambda acc, rr: (acc * (2.0 * rr.astype(F32)),), comm=comm)
        if i == 0:
            *rcv, gathered_early = rcv
        absorb(rcv)
        if i == 0 and any(nm == "down" for nm, _, _ in summed):
            nm_, l_, part = summed.pop([nm for nm, _, _ in summed].index("down"))
            comm, names = _chip_exchange([part], [(nm_, l_)], layers, stacked)
            g_down_i, *bufs = wgrad_rows("ffn_down_wgrad", s, dyb, ffc, D, tb, comm=comm)
            stacked.update(dict(zip(names, bufs)))
        else:
            g_down_i = wgrad_rows("ffn_down_wgrad", s, dyb, ffc, D, tb)
        pending.append(("down", i, g_down_i.reshape(N_DEV, ffc, D)))
        pending.append(("up", i, wgrad_cols("ffn_up_wgrad", h2, da, ffc)))
        dx, dyb, d_norm_ffn[i], *rcv = back_cols("ffn_up_bwd", da, transposed(g_up[i]), xm, norm_ffn[i], dx, comm=sibling_comm())
        absorb(rcv)
        if i % 2 == 0:
            h, lat, cq, ckv, q, kk, vv, o, lse = mix_saved
            w_dkv, w_uq, w_ukv, w_o = mla_w[l]
            do = _matmul("mla_out_bwd", dyb, w_o, [], grid=(nt,), a_spec=_row_spec(tm, D),
                         b_spec=pl.BlockSpec((None, OW, D), lambda i_: (0, 0, 0)), extra_specs=[],
                         out_shapes=[_sds((T, OW), BF16)], out_specs=[_row_spec(tm, OW)], dims=NT)[0]
            g_o_l = wgrad_rows("mla_out_wgrad", o, dyb, OW, D, tm).reshape(N_DEV, owc, D)
            comm, names = chip_comm()
            dq_pre, dk, dv, *bufs = _flash_bwd(q, kk, vv, o, do, lse, (t_cc, t_sa, t_sb), tq, comm=comm)
            stacked.update(dict(zip(names, bufs)))
            pending.append(("o", l, g_o_l))

            def kv_pre(dkb, dvb, cc, sa, sb):
                parts, dkr = [], None
                for b in range(HEADS):
                    parts += [dkb[:, b * QPAD:b * QPAD + NOPE], dvb[:, b * VDIM:(b + 1) * VDIM]]
                    piece = dkb[:, b * QPAD + NOPE:(b + 1) * QPAD].astype(F32)
                    dkr = piece if dkr is None else dkr + piece
                return jnp.concatenate(parts, axis=1), _rope_bwd(dkr, cc, sa, sb)

            dkv, dkr = _rowwise("mla_dkv_rope", kv_pre, [dk, dv, t_cc, t_sa, t_sb], grid=(T // ts,),
                                in_specs=[_row_spec(ts, HW), _row_spec(ts, OW)] + [_row_spec(ts, 128)] * 3,
                                out_shapes=[_sds((T, HW), BF16), _sds((T, 128), F32)], out_specs=[_row_spec(ts, HW), _row_spec(ts, 128)])
            g_uq_l = wgrad_rows("mla_q_wgrad", cq, dq_pre, Q_RANK, HW, tm)
            g_ukv_l = wgrad_rows("mla_kv_wgrad", ckv, dkv, KV_RANK, HW, tm)
            pending.append(("uq", l, g_uq_l.reshape(Q_RANK, HEADS, QPAD)[:, :, :NOPE + ROPE].transpose(1, 0, 2)))
            pending.append(("ukv", l, g_ukv_l.reshape(KV_RANK, HEADS, NOPE + VDIM).transpose(1, 0, 2)))
            dcq = _matmul("mla_q_bwd", dq_pre, w_uq, [], grid=(nt,), a_spec=_row_spec(tm, HW),
                          b_spec=pl.BlockSpec((None, Q_RANK, HW), lambda i_: (0, 0, 0)), extra_specs=[],
                          out_shapes=[_sds((T, Q_RANK), F32)], out_specs=[_row_spec(tm, Q_RANK)], dims=NT)[0]
            dckv = _matmul("mla_kv_bwd", dkv, w_ukv, [], grid=(nt,), a_spec=_row_spec(tm, HW),
                           b_spec=pl.BlockSpec((None, KV_RANK, HW), lambda i_: (0, 0, 0)), extra_specs=[],
                           out_shapes=[_sds((T, KV_RANK), F32)], out_specs=[_row_spec(tm, KV_RANK)], dims=NT)[0]

            def latent_bwd(la, qn, kvn, dq_, dkv_, dkr_):
                dcq_raw, dqn = _rms_bwd(la[:, :Q_RANK], qn, dq_)
                dckv_raw, dkvn = _rms_bwd(la[:, Q_RANK:Q_RANK + KV_RANK], kvn, dkv_)
                return jnp.concatenate([dcq_raw, dckv_raw, dkr_], axis=1), dqn, dkvn

            dlat, d_qn[l], d_kvn[l] = _rowwise(
                "mla_latent_bwd", latent_bwd, [lat, mla_q_norm[l].reshape(1, Q_RANK), mla_kv_norm[l].reshape(1, KV_RANK), dcq, dckv, dkr],
                grid=(nt,), in_specs=[_row_spec(tm, LAT_PAD), _const_spec((1, Q_RANK)), _const_spec((1, KV_RANK)),
                                      _row_spec(tm, Q_RANK), _row_spec(tm, KV_RANK), _row_spec(tm, 128)],
                out_shapes=[_sds((T, LAT_PAD), BF16), _sds((1, Q_RANK), F32), _sds((1, KV_RANK), F32)],
                out_specs=[_row_spec(tm, LAT_PAD), _const_spec((1, Q_RANK)), _const_spec((1, KV_RANK))], n_acc=2)
            g_dkv_l = wgrad_rows("mla_down_wgrad", h, dlat, D, LAT_PAD, tm)
            pending.append(("dkv", l, g_dkv_l[:, :LAT].reshape(N_DEV, dc, LAT)))
            dx, dyb, d_norm_mix[i] = _matmul(
                "mla_down_bwd", dlat, w_dkv, [xs_i, norm_mix[i].reshape(1, D), dx], grid=(nt,), a_spec=_row_spec(tm, LAT_PAD),
                b_spec=pl.BlockSpec((None, D, LAT_PAD), lambda i_: (0, 0, 0)), extra_specs=[_row_spec(tm, D), _const_spec((1, D)), _row_spec(tm, D)],
                out_shapes=[_sds((T, D), F32), _sds((T, D), BF16), _sds((1, D), F32)],
                out_specs=[_row_spec(tm, D), _row_spec(tm, D), _const_spec((1, D))], dims=NT, epilogue=norm_bwd_epilogue, n_sum=1)
        else:
            h, gp, ge, gate = mix_saved
            (dgate,) = back_rows("sgu_out_bwd", dyb, g_out[l], ec, [], None)
            pending.append(("out", l, wgrad_rows("sgu_out_wgrad", gate, dyb, ec, D, tb).reshape(N_DEV, ec, D)))
            dz, d_wsp[l], d_bsp[l], d_lng[l], d_lnb[l] = _sgu_mid_bwd(ge, gp, dgate, ln_g_full[l], ln_b_full[l], sgu_w_spatial[l], b_sp[l], 2)
            pending.append(("in", l, wgrad_cols("sgu_in_wgrad", h, dz, e2c)))
            comm = sibling_comm()
            if i == 1:
                early = packed([jnp.stack(d_wsp, 0), jnp.stack(d_bsp, 0), jnp.concatenate(d_lng, 0), jnp.concatenate(d_lnb, 0)], early_sizes)
                comm = _merge_comm(comm, _gather_level1([early]))
            dx, dyb, d_norm_mix[i], *rcv = back_cols("sgu_in_bwd", dz, transposed(g_in[l]), xs_i, norm_mix[i], dx, comm=comm)
            if i == 1:
                *rcv, early_half = rcv
            absorb(rcv)
    grad_x = dx.reshape(1, T, D)

    last_comm, last_names = chip_comm()
    late_g = [jnp.concatenate(d_norm_mix, 0), jnp.concatenate(d_norm_ffn, 0), d_final, jnp.concatenate(d_qn, 0), jnp.concatenate(d_kvn, 0)]
    late_w = [norm_mix, norm_ffn, final_norm, mla_q_norm, mla_kv_norm]
    late_m = [m_norm_mix, m_norm_ffn, m_final_norm, m_mla_q_norm, m_mla_kv_norm]
    late_v = [v_norm_mix, v_norm_ffn, v_final_norm, v_mla_q_norm, v_mla_kv_norm]
    late_sizes = [pad_to(g.size // 128, 8) for g in late_g]
    late_rows = sum(late_sizes)

    def adam_big(parts, w, m, v):
        lyr, rws, cls = w.shape
        rt = _tile(rws, 512)

        def fn(p, w_, m_, v_):
            g = (p[0].astype(F32) + p[1].astype(F32)) + (p[2].astype(F32) + p[3].astype(F32))
            return (g, *_adam(w_, g, m_, v_))

        spec = pl.BlockSpec((None, rt, cls), lambda l_, i_: (l_, i_, 0))
        return _rowwise("adam_large", fn, [parts, w, m, v], grid=(lyr, rws // rt),
                        in_specs=[pl.BlockSpec((N_CHIP, None, rt, cls), lambda l_, i_: (0, l_, i_, 0)), spec, spec, spec],
                        out_shapes=[_sds(w.shape, F32)] * 4, out_specs=[spec] * 4)

    stacked.update(dict(zip(last_names, _comm_call("grad_chip_exchange", last_comm))))
    (gathered_late,) = _all_gather("gather_small_grads", [packed(late_g, late_sizes)])
    big = {}
    big["in"] = adam_big(stacked["in"], sgu_w_in, m_sgu_w_in, v_sgu_w_in)
    big["up"] = adam_big(stacked["up"], ffn_w_up, m_ffn_w_up, v_ffn_w_up)
    big["down"] = adam_big(stacked["down"], ffn_w_down, m_ffn_w_down, v_ffn_w_down)
    big["out"] = adam_big(stacked["out"], sgu_w_out, m_sgu_w_out, v_sgu_w_out)
    big["dkv"] = adam_big(stacked["dkv"], mla_w_dkv, m_mla_w_dkv, v_mla_w_dkv)
    big["uq"] = adam_big(stacked["uq"], mla_w_uq, m_mla_w_uq, v_mla_w_uq)
    big["ukv"] = adam_big(stacked["ukv"], mla_w_ukv, m_mla_w_ukv, v_mla_w_ukv)
    big["o"] = adam_big(stacked["o"], mla_w_o, m_mla_w_o, v_mla_w_o)
    big_res = [big[nm][:4] for nm in ("dkv", "uq", "ukv", "o", "in", "out", "up", "down")]

    def sum8(p):
        return ((p[0] + p[1]) + (p[2] + p[3])) + ((p[4] + p[5]) + (p[6] + p[7]))

    def adam_packed(name, gathered, ws, ms, vs, sizes, rows, tile):
        spec = _row_spec(tile, 128)
        return _rowwise(name, lambda p, w_, m_, v_: (sum8(p), *_adam(w_, sum8(p), m_, v_)),
                        [gathered, packed(ws, sizes), packed(ms, sizes), packed(vs, sizes)], grid=(rows // tile,),
                        in_specs=[pl.BlockSpec((N_DEV, tile, 128), lambda i_: (0, i_, 0)), spec, spec, spec],
                        out_shapes=[_sds((rows, 128), F32)] * 4, out_specs=[spec] * 4)

    late_res = adam_packed("adam_small", gathered_late, late_w, late_m, late_v, late_sizes, late_rows, late_rows)
    early_res = adam_packed("adam_spatial", gathered_early, [sgu_w_spatial, sgu_b_spatial], [m_sgu_w_spatial, m_sgu_b_spatial],
                            [v_sgu_w_spatial, v_sgu_b_spatial], early_sizes[:2], early_rep, SMALL_ROWS)

    def unpack(res, sizes, k, like):
        off = sum(sizes[:k])
        return res[off:off + like.size // 128].reshape(like.shape)

    my_b = 4 * lax.axis_index("x") + 2 * lax.axis_index("y") + lax.axis_index("c")
    ln_w = jnp.concatenate([sgu_ln_g, sgu_ln_b], 0)
    ln_m = jnp.concatenate([m_sgu_ln_g, m_sgu_ln_b], 0)
    ln_v = jnp.concatenate([v_sgu_ln_g, v_sgu_ln_b], 0)
    ln_all = jnp.concatenate([gathered_early[:, early_rep:early_rep + n_sgu * E // 128], gathered_early[:, early_rep + n_ln:early_rep + n_ln + n_sgu * E // 128]], axis=1)
    ln_mine = lax.dynamic_slice_in_dim(ln_all.reshape(N_DEV, 2 * n_sgu, N_DEV, ec), my_b, 1, axis=2).reshape(N_DEV, 2 * n_sgu, ec)
    ln_g_, ln_d, ln_m2, ln_v2 = _rowwise(
        "adam_ln", lambda p, w_, m_, v_: (sum8(p), *_adam(w_, sum8(p), m_, v_)), [ln_mine, ln_w, ln_m, ln_v], grid=(1,),
        in_specs=[_const_spec(ln_mine.shape), _const_spec(ln_w.shape), _const_spec(ln_w.shape), _const_spec(ln_w.shape)],
        out_shapes=[_sds(ln_w.shape, F32)] * 4, out_specs=[_const_spec(ln_w.shape)] * 4)

    def family(pos):
        ln = [ln_g_, ln_d, ln_m2, ln_v2][pos]
        late = [unpack(late_res[pos], late_sizes, k, w_) for k, w_ in enumerate(late_w)]
        w_sp_, b_sp_ = unpack(early_res[pos], early_sizes, 0, sgu_w_spatial), unpack(early_res[pos], early_sizes, 1, sgu_b_spatial)
        bigs = [res[pos] for res in big_res]
        return [late[0], late[1], late[2], bigs[0], late[3], late[4], bigs[1], bigs[2], bigs[3],
                bigs[4], ln[:n_sgu], ln[n_sgu:], w_sp_, b_sp_, bigs[5], bigs[6], bigs[7]]

    return (loss, grad_x, *family(0), *family(1), *family(2), *family(3))
```

```python
import math

import jax
import jax.numpy as jnp
from jax import lax
from jax.experimental import pallas as pl
from jax.experimental.pallas import tpu as pltpu

F32 = jnp.float32
BF16 = jnp.bfloat16
MESH = pl.DeviceIdType.MESH

N_DEV = 8
N_CHIP = 4
HEADS = 8
NOPE = 128
ROPE = 64
VDIM = 128
QPAD = 256
Q_RANK = 256
KV_RANK = 128
LAT = Q_RANK + KV_RANK + ROPE
LAT_PAD = 512
ROPE_THETA = 10000.0
SGU_CHUNK = 128
SGU_GROUPS = 8
NORM_EPS = 1e-6
LN_EPS = 1e-5
ADAM_LR = 0.001
ADAM_B1 = 0.9
ADAM_B2 = 0.999
ADAM_EPS = 1e-08
ADAM_WD = 0.01
ADAM_STEP = 10
ATTN_SCALE = (NOPE + ROPE) ** -0.5
NEG = -1e30
EXP2_SCALE = ATTN_SCALE * math.log2(math.e)
VMEM_LIMIT = 56 * 1024 * 1024
SMALL_ROWS = 256

NN = (((1,), (0,)), ((), ()))
NT = (((1,), (1,)), ((), ()))
TN = (((0,), (0,)), ((), ()))
ANY = pl.BlockSpec(memory_space=pl.ANY)


def _pcall(body, **kw):
    return pl.pallas_call(body, **kw)


def _params(n_grid, side_effects=False):
    return pltpu.CompilerParams(dimension_semantics=("arbitrary",) * n_grid, vmem_limit_bytes=VMEM_LIMIT, has_side_effects=side_effects)


def _sds(shape, dtype):
    return jax.ShapeDtypeStruct(tuple(shape), dtype)


def _tile(n, want):
    t = min(n, want)
    assert n % t == 0, (n, want)
    return t


class _Comm:
    def __init__(self, operands, out_shapes, aliases, scratch, start, finish):
        self.operands, self.out_shapes, self.aliases, self.scratch = operands, out_shapes, aliases, scratch
        self.start, self.finish = start, finish


def _merge_comm(first, second):
    n_in, n_out, n_sc = len(first.operands), len(first.out_shapes), len(first.scratch)
    aliases = dict(first.aliases)
    aliases.update({n_in + k: n_out + v for k, v in second.aliases.items()})

    def start(ins, outs, sems):
        first.start(ins[:n_in], outs[:n_out], sems[:n_sc])
        second.start(ins[n_in:], outs[n_out:], sems[n_sc:])

    def finish(ins, outs, sems):
        first.finish(ins[:n_in], outs[:n_out], sems[:n_sc])
        second.finish(ins[n_in:], outs[n_out:], sems[n_sc:])

    return _Comm([*first.operands, *second.operands], [*first.out_shapes, *second.out_shapes], aliases,
                 [*first.scratch, *second.scratch], start, finish)


def _place():
    return lax.axis_index("x"), lax.axis_index("y"), lax.axis_index("c")


def _other_chips(x, y):
    return [(1 - x, y), (x, 1 - y), (1 - x, 1 - y)]


def _dev_index(dev):
    return 4 * dev[0] + 2 * dev[1] + dev[2]


def _comm_call(name, comm):
    c_in, c_out = len(comm.operands), len(comm.out_shapes)

    def body(*refs):
        ins, outs, sems = refs[:c_in], refs[c_in:c_in + c_out], refs[c_in + c_out:]
        comm.start(ins, outs, sems)
        comm.finish(ins, outs, sems)

    return _pcall(body, name=name, in_specs=[ANY] * c_in, out_specs=[ANY] * c_out, out_shape=comm.out_shapes,
                  scratch_shapes=comm.scratch, input_output_aliases=dict(comm.aliases),
                  compiler_params=pltpu.CompilerParams(has_side_effects=True))(*comm.operands)


def _call(name, body, operands, in_specs, out_shapes, out_specs, scratch, grid, comm=None):
    if comm is None:
        return _pcall(body, name=name, grid=grid, in_specs=in_specs, out_specs=out_specs, out_shape=out_shapes,
                      scratch_shapes=scratch, compiler_params=_params(len(grid)))(*operands)
    n_in, n_out, n_sc = len(operands), len(out_shapes), len(scratch)
    c_in, c_out = len(comm.operands), len(comm.out_shapes)

    def hosted(*refs):
        ins, cins = refs[:n_in], refs[n_in:n_in + c_in]
        o0 = n_in + c_in
        outs, couts = refs[o0:o0 + n_out], refs[o0 + n_out:o0 + n_out + c_out]
        rest = refs[o0 + n_out + c_out:]
        sc, csems = rest[:n_sc], rest[n_sc:]
        first = pl.program_id(0) == 0
        last = pl.program_id(0) == grid[0] - 1
        for d in range(1, len(grid)):
            first = jnp.logical_and(first, pl.program_id(d) == 0)
            last = jnp.logical_and(last, pl.program_id(d) == grid[d] - 1)

        @pl.when(first)
        def _():
            comm.start(cins, couts, csems)

        body(*ins, *outs, *sc)

        @pl.when(last)
        def _():
            comm.finish(cins, couts, csems)

    return _pcall(hosted, name=name, grid=grid, in_specs=[*in_specs, *[ANY] * c_in], out_specs=[*out_specs, *[ANY] * c_out],
                  out_shape=[*out_shapes, *comm.out_shapes], scratch_shapes=[*scratch, *comm.scratch],
                  input_output_aliases={n_in + k: n_out + v for k, v in comm.aliases.items()},
                  compiler_params=_params(len(grid), side_effects=True))(*operands, *comm.operands)


def _gather_level1(shards, part=(0, 1), into=None):
    n = len(shards)
    k_part, m_part = part

    def copies(ins, outs, sems):
        send_sems, recv_sems, local_sems = sems
        x, y, c = _place()
        me, sibling = (x, y, c), (x, y, 1 - c)
        chips = _other_chips(x, y)

        def rows(a):
            r = shards[a].shape[0] // m_part
            return pl.ds(k_part * r, r)

        def copy(a, k, block, to, own=False):
            slot = outs[a].at[_dev_index(block), rows(a)]
            return pltpu.make_async_remote_copy(src_ref=ins[a].at[rows(a)] if own else slot, dst_ref=slot, send_sem=send_sems.at[a, k],
                                                recv_sem=recv_sems.at[a, k], device_id=to, device_id_type=MESH)

        mine = [pltpu.make_async_copy(ins[a].at[rows(a)], outs[a].at[_dev_index(me), rows(a)], local_sems.at[a]) for a in range(n)]
        sends = [copy(a, 1 + j, me, (*chip, c), own=True) for j, chip in enumerate(chips) for a in range(n)]
        sends += [copy(a, 0, me, sibling, own=True) for a in range(n)]
        recvs = [copy(a, 1 + j, (*chip, c), me) for j, chip in enumerate(chips) for a in range(n)]
        recvs += [copy(a, 0, sibling, me) for a in range(n)]
        return mine, sends, recvs

    def start(ins, outs, sems):
        mine, sends, _ = copies(ins, outs, sems)
        for cp in mine + sends:
            cp.start()

    def finish(ins, outs, sems):
        mine, sends, recvs = copies(ins, outs, sems)
        for cp in recvs:
            cp.wait_recv()
        for cp in sends:
            cp.wait_send()
        for cp in mine:
            cp.wait()

    return _Comm([*shards, *(into or [])], [_sds((N_DEV, *a.shape), a.dtype) for a in shards], {n + a: a for a in range(n)} if into else {},
                 [pltpu.SemaphoreType.DMA((n, 4)), pltpu.SemaphoreType.DMA((n, 4)), pltpu.SemaphoreType.DMA((n,))], start, finish)


def _gather_level2(bufs):
    n = len(bufs)

    def copies(outs, sems):
        send_sems, recv_sems = sems
        x, y, c = _place()
        sibling = (x, y, 1 - c)
        sends, recvs = [], []
        for j, chip in enumerate(_other_chips(x, y)):
            for a in range(n):
                have, want = outs[a].at[_dev_index((*chip, c))], outs[a].at[_dev_index((*chip, 1 - c))]
                sends.append(pltpu.make_async_remote_copy(src_ref=have, dst_ref=have, send_sem=send_sems.at[a, j], recv_sem=recv_sems.at[a, j],
                                                          device_id=sibling, device_id_type=MESH))
                recvs.append(pltpu.make_async_remote_copy(src_ref=want, dst_ref=want, send_sem=send_sems.at[a, j], recv_sem=recv_sems.at[a, j],
                                                          device_id=sibling, device_id_type=MESH))
        return sends, recvs

    def start(ins, outs, sems):
        for cp in copies(outs, sems)[0]:
            cp.start()

    def finish(ins, outs, sems):
        sends, recvs = copies(outs, sems)
        for cp in recvs:
            cp.wait_recv()
        for cp in sends:
            cp.wait_send()

    return _Comm(bufs, [_sds(b.shape, b.dtype) for b in bufs], {a: a for a in range(n)},
                 [pltpu.SemaphoreType.DMA((n, 3)), pltpu.SemaphoreType.DMA((n, 3))], start, finish)


def _all_gather(name, arrays):
    n = len(arrays)

    def body(*refs):
        ins = refs[:n]
        outs = refs[n:2 * n]
        send_sems, recv_sems, local_sems = refs[2 * n:]
        x, y, c = _place()
        me, sibling = (x, y, c), (x, y, 1 - c)
        chips = _other_chips(x, y)

        def copy(a, k, block, to, src=None):
            slot = outs[a].at[_dev_index(block)]
            return pltpu.make_async_remote_copy(src_ref=slot if src is None else src, dst_ref=slot, send_sem=send_sems.at[a, k],
                                                recv_sem=recv_sems.at[a, k], device_id=to, device_id_type=MESH)

        mine = [pltpu.make_async_copy(ins[a], outs[a].at[_dev_index(me)], local_sems.at[a]) for a in range(n)]
        for cp in mine:
            cp.start()
        first = []
        for j, chip in enumerate(chips):
            first += [copy(a, 1 + j, me, (*chip, c), src=ins[a]) for a in range(n)]
        first += [copy(a, 0, me, sibling, src=ins[a]) for a in range(n)]
        for cp in first:
            cp.start()
        passed = []
        for j, chip in enumerate(chips):
            for a in range(n):
                copy(a, 1 + j, (*chip, c), me).wait_recv()
                fwd = copy(a, 4 + j, (*chip, c), sibling)
                fwd.start()
                passed.append(fwd)
        for a in range(n):
            copy(a, 0, sibling, me).wait_recv()
            for j, chip in enumerate(chips):
                copy(a, 4 + j, (*chip, 1 - c), me).wait_recv()
        for cp in first + passed:
            cp.wait_send()
        for cp in mine:
            cp.wait()

    return _pcall(
        body, name=name, in_specs=[ANY] * n, out_specs=[ANY] * n,
        out_shape=[_sds((N_DEV, *a.shape), a.dtype) for a in arrays],
        scratch_shapes=[pltpu.SemaphoreType.DMA((n, 7)), pltpu.SemaphoreType.DMA((n, 7)), pltpu.SemaphoreType.DMA((n,))],
        compiler_params=pltpu.CompilerParams(has_side_effects=True),
    )(*arrays)


def _sibling_exchange(grads):
    n = len(grads)

    def start(ins, outs, sems):
        send_sems, recv_sems = sems
        x, y, c = _place()
        for a in range(n):
            for ch in range(N_CHIP):
                pltpu.make_async_remote_copy(src_ref=ins[a].at[2 * ch + 1 - c], dst_ref=outs[a].at[ch], send_sem=send_sems.at[a],
                                             recv_sem=recv_sems.at[a], device_id=(x, y, 1 - c), device_id_type=MESH).start()

    def finish(ins, outs, sems):
        send_sems, recv_sems = sems
        x, y, c = _place()
        for a in range(n):
            pltpu.make_async_remote_copy(src_ref=outs[a], dst_ref=outs[a], send_sem=send_sems.at[a], recv_sem=recv_sems.at[a],
                                         device_id=(x, y, 1 - c), device_id_type=MESH).wait()

    return _Comm(grads, [_sds((N_CHIP, *g.shape[1:]), g.dtype) for g in grads], {},
                 [pltpu.SemaphoreType.DMA((n,)), pltpu.SemaphoreType.DMA((n,))], start, finish)


def _chip_exchange(parts, slots, layers, stacked):
    n = len(parts)
    names = []
    for nm, _ in slots:
        if nm not in names:
            names.append(nm)
    shapes = {nm: _sds((N_CHIP, layers[nm], *parts[a].shape[1:]), parts[a].dtype) for a, (nm, _) in enumerate(slots)}
    kept = [nm for nm in names if stacked.get(nm) is not None]
    aliases = {n + k: names.index(nm) for k, nm in enumerate(kept)}

    def copies(ins, outs, sems):
        send_sems, recv_sems, local_sems = sems
        x, y, c = _place()
        mine = 2 * x + y
        local, sends, recvs = [], [], []
        for a, (nm, l) in enumerate(slots):
            buf = outs[names.index(nm)]
            local.append(pltpu.make_async_copy(ins[a].at[mine], buf.at[mine, l], local_sems.at[a]))
            for j, chip in enumerate(_other_chips(x, y)):
                theirs = buf.at[2 * chip[0] + chip[1], l]
                sends.append(pltpu.make_async_remote_copy(src_ref=ins[a].at[2 * chip[0] + chip[1]], dst_ref=buf.at[mine, l], send_sem=send_sems.at[a, j],
                                                          recv_sem=recv_sems.at[a, j], device_id=(*chip, c), device_id_type=MESH))
                recvs.append(pltpu.make_async_remote_copy(src_ref=theirs, dst_ref=theirs, send_sem=send_sems.at[a, j],
                                                          recv_sem=recv_sems.at[a, j], device_id=(*chip, c), device_id_type=MESH))
        return local, sends, recvs

    def start(ins, outs, sems):
        local, sends, _ = copies(ins, outs, sems)
        for cp in local + sends:
            cp.start()

    def finish(ins, outs, sems):
        local, sends, recvs = copies(ins, outs, sems)
        for cp in recvs:
            cp.wait_recv()
        for cp in sends:
            cp.wait_send()
        for cp in local:
            cp.wait()

    comm = _Comm([*parts, *[stacked[nm] for nm in kept]], [shapes[nm] for nm in names], aliases,
                 [pltpu.SemaphoreType.DMA((n, 3)), pltpu.SemaphoreType.DMA((n, 3)), pltpu.SemaphoreType.DMA((n,))], start, finish)
    return comm, names


def _matmul(name, a, b, extras, *, grid, a_spec, b_spec, extra_specs, out_shapes, out_specs, dims, k_axis=None, nk=1,
            acc_shape=None, epilogue=None, comm=None, n_sum=0, write=None):
    n_extra = len(extras)
    n_out = len(out_shapes)

    def body(*refs):
        a_ref, b_ref = refs[0], refs[1]
        ex = refs[2:2 + n_extra]
        outs = refs[2 + n_extra:2 + n_extra + n_out]
        prod = lax.dot_general(a_ref[...], b_ref[...], dims, preferred_element_type=F32)

        def finish(acc):
            if write is not None:
                write(outs, acc, *[e[...] for e in ex])
                return
            res = epilogue(acc, *[e[...] for e in ex]) if epilogue is not None else (acc,)
            first = None
            for d in range(len(grid)):
                if d != k_axis:
                    here = pl.program_id(d) == 0
                    first = here if first is None else jnp.logical_and(first, here)
            for idx, (o, r) in enumerate(zip(outs, res)):
                if idx < n_out - n_sum:
                    o[...] = r.astype(o.dtype)
                else:
                    @pl.when(first)
                    def _(o=o, r=r):
                        o[...] = r.astype(o.dtype)

                    @pl.when(jnp.logical_not(first))
                    def _(o=o, r=r):
                        o[...] += r.astype(o.dtype)

        if k_axis is None:
            finish(prod)
        else:
            acc_ref = refs[-1]
            k = pl.program_id(k_axis)

            @pl.when(k == 0)
            def _():
                acc_ref[...] = prod

            @pl.when(k > 0)
            def _():
                acc_ref[...] += prod

            @pl.when(k == nk - 1)
            def _():
                finish(acc_ref[...])

    scratch = [] if k_axis is None else [pltpu.VMEM(acc_shape, F32)]
    return _call(name, body, [a, b, *extras], [a_spec, b_spec, *extra_specs], list(out_shapes), list(out_specs), scratch, grid, comm)


def _rowwise(name, fn, operands, *, grid, in_specs, out_shapes, out_specs, n_acc=0, grid_spec_prefetch=None, comm=None):
    n_in = len(operands)
    n_out = len(out_shapes)
    n_pre = 0 if grid_spec_prefetch is None else 1

    def body(*refs):
        refs = refs[n_pre:]
        ins = refs[:n_in]
        outs = refs[n_in:n_in + n_out]
        res = fn(*[r[...] for r in ins])
        if not isinstance(res, (tuple, list)):
            res = (res,)
        first = pl.program_id(0) == 0
        for d in range(1, len(grid)):
            first = jnp.logical_and(first, pl.program_id(d) == 0)
        for idx, (o, r) in enumerate(zip(outs, res)):
            if idx < n_out - n_acc:
                o[...] = r.astype(o.dtype)
            else:
                @pl.when(first)
                def _(o=o, r=r):
                    o[...] = r.astype(o.dtype)

                @pl.when(jnp.logical_not(first))
                def _(o=o, r=r):
                    o[...] += r.astype(o.dtype)

    if comm is not None:
        return _call(name, body, list(operands), list(in_specs), list(out_shapes), list(out_specs), [], grid, comm)
    if grid_spec_prefetch is None:
        return _pcall(body, name=name, grid=grid, in_specs=in_specs, out_specs=out_specs, out_shape=out_shapes,
                      compiler_params=_params(len(grid)))(*operands)
    gs = pltpu.PrefetchScalarGridSpec(num_scalar_prefetch=1, grid=grid, in_specs=in_specs, out_specs=out_specs)
    return _pcall(body, name=name, grid_spec=gs, out_shape=out_shapes,
                  compiler_params=_params(len(grid)))(grid_spec_prefetch, *operands)


def _row_spec(tm, w):
    return pl.BlockSpec((tm, w), lambda i: (i, 0))


def _const_spec(shape):
    nd = len(shape)
    return pl.BlockSpec(tuple(shape), lambda *_: (0,) * nd)


def _rms_fwd(x, g):
    r = lax.rsqrt(jnp.mean(x * x, axis=-1, keepdims=True) + NORM_EPS)
    return x * r * g


def _rms_bwd(x, g, dy):
    r = lax.rsqrt(jnp.mean(x * x, axis=-1, keepdims=True) + NORM_EPS)
    xh = x * r
    u = dy * g
    dx = r * (u - xh * jnp.mean(u * xh, axis=-1, keepdims=True))
    dg = jnp.sum(dy * xh, axis=0, keepdims=True)
    return dx, dg


def _gelu_and_grad(z):
    cdf = 0.5 * (1.0 + lax.erf(z * (2.0 ** -0.5)))
    return cdf + z * jnp.exp(-0.5 * z * z) * ((2.0 * math.pi) ** -0.5), z * cdf


def _rope_fwd(x, cc, sa, sb):
    return x * cc + pltpu.roll(x, 96, 1) * sa + pltpu.roll(x, 32, 1) * sb


def _rope_bwd(d, cc, sa, sb):
    return d * cc + pltpu.roll(d * sa, 32, 1) + pltpu.roll(d * sb, 96, 1)


def _adam(w, g, m, v):
    m = ADAM_B1 * m + (1.0 - ADAM_B1) * g
    v = ADAM_B2 * v + (1.0 - ADAM_B2) * (g * g)
    m_hat = m / (1.0 - ADAM_B1 ** ADAM_STEP)
    v_hat = v / (1.0 - ADAM_B2 ** ADAM_STEP)
    delta = -ADAM_LR * (m_hat / (jnp.sqrt(v_hat) + ADAM_EPS) + ADAM_WD * w)
    return delta, m, v


def _flash_fwd(q, k, vt, tq, comm=None):
    h, t = vt.shape[0], q.shape[0]
    nq = t // tq

    chunk_blocks = [c for c in (4, 2) if c < nq]

    def body(q_ref, k_ref, vt_ref, o_ref, lse_ref, m_ref, l_ref, acc_ref):
        qi = pl.program_id(1)
        m_ref[...] = jnp.full((1, tq), NEG, F32)
        l_ref[...] = jnp.zeros((1, tq), F32)
        acc_ref[...] = jnp.zeros((VDIM, tq), F32)

        def update(kb0, nblk, masked):
            kb = k_ref[pl.ds(pl.multiple_of(kb0 * tq, tq), nblk * tq), :]
            st = lax.dot_general(kb, q_ref[...], NT, preferred_element_type=F32)
            if masked:
                key = lax.broadcasted_iota(jnp.int32, (nblk * tq, tq), 0) - (nblk - 1) * tq
                qry = lax.broadcasted_iota(jnp.int32, (nblk * tq, tq), 1)
                st = jnp.where(key <= qry, st, NEG)
            m_old = m_ref[...]
            m_new = jnp.maximum(m_old, jnp.max(st, axis=0, keepdims=True))
            alpha = jnp.exp2((m_old - m_new) * EXP2_SCALE)
            pt = jnp.exp2((st - m_new) * EXP2_SCALE)
            l_ref[...] = alpha * l_ref[...] + jnp.sum(pt, axis=0, keepdims=True)
            ptb = pt.astype(BF16)
            pv = lax.dot_general(vt_ref[kb0], ptb[:tq], NN, preferred_element_type=F32)
            for j in range(1, nblk):
                pv += lax.dot_general(vt_ref[kb0 + j], ptb[j * tq:(j + 1) * tq], NN, preferred_element_type=F32)
            acc_ref[...] = alpha * acc_ref[...] + pv
            m_ref[...] = m_new

        start = jnp.int32(0)
        for c in chunk_blocks:
            take = (qi & c) != 0

            @pl.when(take)
            def _(start=start, c=c):
                update(start, c, False)

            start = start + jnp.where(take, c, 0)
        if nq > 1:
            @pl.when((qi & 1) != 0)
            def _():
                update(qi - 1, 2, True)

            @pl.when((qi & 1) == 0)
            def _():
                update(qi, 1, True)
        else:
            update(qi, 1, True)
        l = l_ref[...]
        o_ref[...] = (acc_ref[...] / l).T.astype(o_ref.dtype)
        lse_ref[...] = m_ref[...] * EXP2_SCALE + jnp.log2(l)

    return _call(
        "flash_fwd", body, [q, k, vt],
        [pl.BlockSpec((tq, QPAD), lambda hh, i: (i, hh)),
         pl.BlockSpec((t, QPAD), lambda hh, i: (0, hh)),
         pl.BlockSpec((None, nq, VDIM, tq), lambda hh, i: (hh, 0, 0, 0))],
        [_sds((t, h * VDIM), BF16), _sds((h, nq, 1, tq), F32)],
        [pl.BlockSpec((tq, VDIM), lambda hh, i: (i, hh)),
         pl.BlockSpec((None, None, 1, tq), lambda hh, i: (hh, i, 0, 0))],
        [pltpu.VMEM((1, tq), F32), pltpu.VMEM((1, tq), F32), pltpu.VMEM((VDIM, tq), F32)], (h, nq), comm)


def _flash_bwd(q, k, v, o, do, lse, tabs, tq, comm=None):
    t = q.shape[0]
    h = q.shape[1] // QPAD
    nq = t // tq

    def body(q_ref, k_ref, v_ref, o_ref, do_ref, lse_ref, cc_ref, sa_ref, sb_ref, dq_ref, dk_out, dv_out, delta_ref, dqt_ref, dk_ref, dv_ref):
        kj = pl.program_id(1)

        @pl.when(kj == 0)
        def _():
            dqt_ref[...] = jnp.zeros_like(dqt_ref)
            ones = jnp.ones((8, VDIM), BF16)
            for qi in range(nq):
                rows = pl.ds(qi * tq, tq)
                prod = do_ref[rows, :].astype(F32) * o_ref[rows, :].astype(F32)
                hi = prod.astype(BF16)
                lo = (prod - hi.astype(F32)).astype(BF16)
                delta_ref[qi] = (lax.dot_general(ones, hi, NT, preferred_element_type=F32)
                                 + lax.dot_general(ones, lo, NT, preferred_element_type=F32))

        kb = k_ref[...]
        vb = v_ref[...]
        kbt = kb.astype(F32).T.astype(BF16)
        dk_ref[...] = jnp.zeros_like(dk_ref)
        dv_ref[...] = jnp.zeros_like(dv_ref)

        def step(q0, nblk, masked):
            rows = pl.ds(pl.multiple_of(q0 * tq, tq), nblk * tq)
            qb = q_ref[rows, :]
            dob = do_ref[rows, :]
            lse = jnp.concatenate([lse_ref[q0 + j] for j in range(nblk)], axis=1)
            delta = jnp.concatenate([delta_ref[q0 + j, pl.ds(0, 1), :] for j in range(nblk)], axis=1)
            st = lax.dot_general(kb, qb, NT, preferred_element_type=F32)
            pt = jnp.exp2(st * EXP2_SCALE - lse)
            if masked:
                key = lax.broadcasted_iota(jnp.int32, (tq, nblk * tq), 0)
                qry = lax.broadcasted_iota(jnp.int32, (tq, nblk * tq), 1)
                pt = jnp.where(key <= qry, pt, 0.0)
            dv_ref[...] += lax.dot_general(pt.astype(BF16), dob, NN, preferred_element_type=F32)
            dpt = lax.dot_general(vb, dob, NT, preferred_element_type=F32)
            dst = (pt * (dpt - delta) * ATTN_SCALE).astype(BF16)
            dk_ref[...] += lax.dot_general(dst, qb, NN, preferred_element_type=F32)
            dqt = lax.dot_general(kbt, dst, NN, preferred_element_type=F32)
            for j in range(nblk):
                dqt_ref[q0 + j] += dqt[:, j * tq:(j + 1) * tq]

        later = nq - 1 - kj
        if nq > 1:
            @pl.when((later & 1) != 0)
            def _():
                step(kj, 2, True)

            @pl.when((later & 1) == 0)
            def _():
                step(kj, 1, True)
        else:
            step(kj, 1, True)
        start = kj + 1 + (later & 1)
        for c in [c for c in (2, 4) if c < nq]:
            take = (later & c) != 0

            @pl.when(take)
            def _(start=start, c=c):
                step(start, c, False)

            start = start + jnp.where(take, c, 0)
        dk_out[...] = dk_ref[...].astype(BF16)
        dv_out[...] = dv_ref[...].astype(BF16)

        @pl.when(kj == nq - 1)
        def _():
            for qi in range(nq):
                rows = pl.ds(qi * tq, tq)
                d = dqt_ref[qi].T
                roped = _rope_bwd(d[:, NOPE:], cc_ref[rows, :], sa_ref[rows, :], sb_ref[rows, :])
                dq_ref[rows, :] = jnp.concatenate([d[:, :NOPE], roped], axis=1).astype(BF16)

    head_q = pl.BlockSpec((t, QPAD), lambda hh, j: (0, hh))
    head_v = pl.BlockSpec((t, VDIM), lambda hh, j: (0, hh))
    table = pl.BlockSpec((t, 128), lambda hh, j: (0, 0))
    return _call(
        "flash_bwd", body, [q, k, v, o, do, lse, *tabs],
        [head_q, pl.BlockSpec((tq, QPAD), lambda hh, j: (j, hh)), pl.BlockSpec((tq, VDIM), lambda hh, j: (j, hh)), head_v, head_v,
         pl.BlockSpec((None, nq, 1, tq), lambda hh, j: (hh, 0, 0, 0)), table, table, table],
        [_sds((t, h * QPAD), BF16), _sds((t, h * QPAD), BF16), _sds((t, h * VDIM), BF16)],
        [head_q, pl.BlockSpec((tq, QPAD), lambda hh, j: (j, hh)), pl.BlockSpec((tq, VDIM), lambda hh, j: (j, hh))],
        [pltpu.VMEM((nq, 8, tq), F32), pltpu.VMEM((nq, QPAD, tq), F32), pltpu.VMEM((tq, QPAD), F32), pltpu.VMEM((tq, VDIM), F32)], (h, nq), comm)


def _tril_bf16(w):
    row = lax.broadcasted_iota(jnp.int32, w.shape, 0)
    col = lax.broadcasted_iota(jnp.int32, w.shape, 1)
    return jnp.where(col <= row, w, 0.0).astype(BF16)


def _layer_norm_parts(v0):
    mu = jnp.mean(v0, axis=-1, keepdims=True)
    vc = v0 - mu
    rstd = lax.rsqrt(jnp.mean(vc * vc, axis=-1, keepdims=True) + LN_EPS)
    return vc * rstd, rstd


def _sgu_mid_fwd(ge, ln_g, ln_b, w_sp, b_sp, chunks_per_step):
    t, e2 = ge.shape
    e = e2 // 2
    gd = e // SGU_GROUPS
    rows = SGU_CHUNK * chunks_per_step

    def body(u_ref, v_ref, g_ref, b_ref, w_ref, bs_ref, gate_ref):
        for ck in range(chunks_per_step):
            r = pl.ds(ck * SGU_CHUNK, SGU_CHUNK)
            xh, _ = _layer_norm_parts(v_ref[r, :].astype(F32))
            v1 = (xh * g_ref[...] + b_ref[...]).astype(BF16)
            for g in range(SGU_GROUPS):
                cols = pl.ds(g * gd, gd)
                mixed = lax.dot_general(_tril_bf16(w_ref[g]), v1[:, g * gd:(g + 1) * gd], NN, preferred_element_type=F32) + bs_ref[g]
                gate_ref[r, cols] = (u_ref[r, cols].astype(F32) * mixed).astype(BF16)

    return _pcall(
        body, name="sgu_mid_fwd", grid=(t // rows,),
        in_specs=[pl.BlockSpec((rows, e), lambda i: (i, 0)), pl.BlockSpec((rows, e), lambda i: (i, 1)),
                  _const_spec((1, e)), _const_spec((1, e)), _const_spec(w_sp.shape), _const_spec(b_sp.shape)],
        out_specs=pl.BlockSpec((rows, e), lambda i: (i, 0)),
        out_shape=_sds((t, e), BF16), compiler_params=_params(1),
    )(ge, ge, ln_g, ln_b, w_sp, b_sp)


def _sgu_mid_bwd(ge, gp, dgate, ln_g, ln_b, w_sp, b_sp, chunks_per_step):
    t, e2 = ge.shape
    e = e2 // 2
    gd = e // SGU_GROUPS
    rows = SGU_CHUNK * chunks_per_step

    def body(u_ref, v_ref, zu_ref, zv_ref, dg_ref, g_ref, b_ref, w_ref, bs_ref, dz_ref, dw_ref, dbs_ref, dlg_ref, dlb_ref):
        @pl.when(pl.program_id(0) == 0)
        def _():
            dw_ref[...] = jnp.zeros_like(dw_ref)
            dbs_ref[...] = jnp.zeros_like(dbs_ref)
            dlg_ref[...] = jnp.zeros_like(dlg_ref)
            dlb_ref[...] = jnp.zeros_like(dlb_ref)

        for ck in range(chunks_per_step):
            r = pl.ds(ck * SGU_CHUNK, SGU_CHUNK)
            xh, rstd = _layer_norm_parts(v_ref[r, :].astype(F32))
            v1 = (xh * g_ref[...] + b_ref[...]).astype(BF16)
            dv1_parts = []
            for g in range(SGU_GROUPS):
                cols = pl.ds(g * gd, gd)
                wc = _tril_bf16(w_ref[g])
                v1g = v1[:, g * gd:(g + 1) * gd]
                mixed = lax.dot_general(wc, v1g, NN, preferred_element_type=F32) + bs_ref[g]
                dgate = dg_ref[r, cols].astype(F32)
                dmixed = dgate * u_ref[r, cols].astype(F32)
                du = dgate * mixed
                dz_ref[r, cols] = (du * zu_ref[r, cols].astype(F32)).astype(BF16)
                dbs_ref[g] += jnp.sum(dmixed, axis=1, keepdims=True)
                dmb = dmixed.astype(BF16)
                dwg = lax.dot_general(dmb, v1g, NT, preferred_element_type=F32)
                row = lax.broadcasted_iota(jnp.int32, dwg.shape, 0)
                col = lax.broadcasted_iota(jnp.int32, dwg.shape, 1)
                dw_ref[g] += jnp.where(col <= row, dwg, 0.0)
                dv1_parts.append(lax.dot_general(wc, dmb, TN, preferred_element_type=F32))
            dv1 = jnp.concatenate(dv1_parts, axis=1)
            dlg_ref[...] += jnp.sum(dv1 * xh, axis=0, keepdims=True)
            dlb_ref[...] += jnp.sum(dv1, axis=0, keepdims=True)
            dxh = dv1 * g_ref[...]
            dv0 = rstd * (dxh - jnp.mean(dxh, axis=-1, keepdims=True) - xh * jnp.mean(dxh * xh, axis=-1, keepdims=True))
            dz_ref[r, pl.ds(e, e)] = (dv0 * zv_ref[r, :].astype(F32)).astype(BF16)

    half0 = pl.BlockSpec((rows, e), lambda i: (i, 0))
    half1 = pl.BlockSpec((rows, e), lambda i: (i, 1))
    return _pcall(
        body, name="sgu_mid_bwd", grid=(t // rows,),
        in_specs=[half0, half1, half0, half1, half0, _const_spec((1, e)), _const_spec((1, e)), _const_spec(w_sp.shape), _const_spec(b_sp.shape)],
        out_specs=[pl.BlockSpec((rows, e2), lambda i: (i, 0)), _const_spec(w_sp.shape), _const_spec(b_sp.shape), _const_spec((1, e)), _const_spec((1, e))],
        out_shape=[_sds((t, e2), BF16), _sds(w_sp.shape, F32), _sds(b_sp.shape, F32), _sds((1, e), F32), _sds((1, e), F32)],
        compiler_params=_params(1),
    )(ge, ge, gp, gp, dgate, ln_g, ln_b, w_sp, b_sp)


def kernel(x, positions, norm_mix, norm_ffn, final_norm, mla_w_dkv, mla_q_norm, mla_kv_norm, mla_w_uq, mla_w_ukv, mla_w_o, sgu_w_in, sgu_ln_g, sgu_ln_b, sgu_w_spatial, sgu_b_spatial, sgu_w_out, ffn_w_up, ffn_w_down, loss_target, m_norm_mix, m_norm_ffn, m_final_norm, m_mla_w_dkv, m_mla_q_norm, m_mla_kv_norm, m_mla_w_uq, m_mla_w_ukv, m_mla_w_o, m_sgu_w_in, m_sgu_ln_g, m_sgu_ln_b, m_sgu_w_spatial, m_sgu_b_spatial, m_sgu_w_out, m_ffn_w_up, m_ffn_w_down, v_norm_mix, v_norm_ffn, v_final_norm, v_mla_w_dkv, v_mla_q_norm, v_mla_kv_norm, v_mla_w_uq, v_mla_w_ukv, v_mla_w_o, v_sgu_w_in, v_sgu_ln_g, v_sgu_ln_b, v_sgu_w_spatial, v_sgu_b_spatial, v_sgu_w_out, v_ffn_w_up, v_ffn_w_down):
    _, T, D = x.shape
    depth = norm_mix.shape[0]
    n_mla, n_sgu = mla_w_dkv.shape[0], sgu_w_in.shape[0]
    assert depth % 2 == 0
    FF = ffn_w_up.shape[2] * N_DEV
    E = sgu_w_out.shape[1] * N_DEV
    ffc, ec, e2c = FF // N_DEV, E // N_DEV, 2 * E // N_DEV
    dc = D // N_DEV
    OW = HEADS * VDIM
    HW = HEADS * QPAD
    owc = OW // N_DEV
    tm = _tile(T, 1024)
    tb = _tile(T, 4096)
    tk = _tile(T, 512)
    tq = _tile(T, 512)
    ts = _tile(T, 256)
    nt = T // tm
    x2 = x.reshape(T, D)
    tgt = loss_target.reshape(T, D)
    cidx = lax.axis_index("c").astype(jnp.int32).reshape(1)

    ln_local = jnp.concatenate([sgu_ln_g, sgu_ln_b, jnp.zeros((8 - 2 * n_sgu, ec), F32)], axis=0)
    mla_sh = [[w[l].astype(BF16) for w in (mla_w_dkv, mla_w_uq, mla_w_ukv, mla_w_o)] for l in range(n_mla)]

    def mla_layouts(g_dkv, g_uq, g_ukv, g_o):
        w_dkv = jnp.pad(g_dkv.reshape(1, D, LAT), ((0, 0), (0, 0), (0, LAT_PAD - LAT)))
        w_uq = jnp.pad(g_uq, ((0, 0), (0, 0), (0, QPAD - NOPE - ROPE))).transpose(1, 0, 2).reshape(1, Q_RANK, HEADS * QPAD)
        w_ukv = g_ukv.transpose(1, 0, 2).reshape(1, KV_RANK, HEADS * (NOPE + VDIM))
        return w_dkv, w_uq, w_ukv, g_o.reshape(1, HEADS * VDIM, D)

    mla_w = [None] * n_mla
    small_later = [a for l in range(1, n_mla) for a in mla_sh[l]] + [ln_local]
    ln_g_full, ln_b_full = [None] * n_sgu, [None] * n_sgu
    b_sp = sgu_b_spatial.reshape(n_sgu, SGU_GROUPS, SGU_CHUNK, 1)
    up_sh = [ffn_w_up[i].astype(BF16) for i in range(depth)]
    down_sh = [ffn_w_down[i].astype(BF16) for i in range(depth)]
    in_sh = [sgu_w_in[l].astype(BF16) for l in range(n_sgu)]
    out_sh = [sgu_w_out[l].astype(BF16) for l in range(n_sgu)]
    g_up, g_down, g_in, g_out = [None] * depth, [None] * depth, [None] * n_sgu, [None] * n_sgu

    inv_freq = ROPE_THETA ** (-jnp.arange(0, ROPE, 2, dtype=F32) / ROPE)
    zeros32 = jnp.zeros((ROPE // 2,), F32)
    inv128 = jnp.concatenate([inv_freq, inv_freq, zeros32, zeros32]).reshape(1, 128)
    sel_a = jnp.concatenate([-jnp.ones((32,), F32), zeros32, zeros32, zeros32]).reshape(1, 128)
    sel_b = jnp.concatenate([zeros32, jnp.ones((32,), F32), zeros32, zeros32]).reshape(1, 128)
    sel_c = jnp.concatenate([jnp.ones((64,), F32), zeros32, zeros32]).reshape(1, 128)

    def rope_tables(pos, inv, sa, sb, sc):
        ang = pos.astype(F32) * inv
        cs, sn = jnp.cos(ang), jnp.sin(ang)
        return cs * sc, sn * sa, sn * sb

    t_cc, t_sa, t_sb, *first_half = _rowwise(
        "rope_tables", rope_tables, [positions.reshape(T, 1), inv128, sel_a, sel_b, sel_c], grid=(nt,),
        in_specs=[_row_spec(tm, 1)] + [_const_spec((1, 128))] * 4,
        out_shapes=[_sds((T, 128), F32)] * 3, out_specs=[_row_spec(tm, 128)] * 3, comm=_gather_level1(mla_sh[0]))
    tab_specs = [_row_spec(tm, 128)] * 3

    def rmsnorm(xv, g, comm):
        return _rowwise("rmsnorm", lambda a, gg: _rms_fwd(a, gg), [xv, g.reshape(1, D)], grid=(nt,),
                        in_specs=[_row_spec(tm, D), _const_spec((1, D))], out_shapes=[_sds((T, D), BF16)], out_specs=[_row_spec(tm, D)], comm=comm)

    def proj_cols(name, h, gw, nc, epilogue, n_out, comm=None):
        return _matmul(name, h, gw, [], grid=(N_DEV, T // tb),
                       a_spec=pl.BlockSpec((tb, D), lambda j, i: (i, 0)),
                       b_spec=pl.BlockSpec((None, D, nc), lambda j, i: (j, 0, 0)), extra_specs=[],
                       out_shapes=[_sds((T, nc * N_DEV), BF16)] * n_out, out_specs=[pl.BlockSpec((tb, nc), lambda j, i: (i, j))] * n_out,
                       dims=NN, epilogue=epilogue, comm=comm)

    def residual_norm(acc, xr, g):
        xn = acc + xr
        return xn, _rms_fwd(xn, g)

    def proj_rows_residual(name, a, gw, xres, g_next, comm=None):
        kk_ = a.shape[1]
        return _matmul(name, a, gw.reshape(kk_, D), [xres, g_next.reshape(1, D)], grid=(T // tk,),
                       a_spec=_row_spec(tk, kk_), b_spec=_const_spec((kk_, D)), extra_specs=[_row_spec(tk, D), _const_spec((1, D))],
                       out_shapes=[_sds((T, D), F32), _sds((T, D), BF16)], out_specs=[_row_spec(tk, D)] * 2,
                       dims=NN, epilogue=residual_norm, comm=comm)

    def back_rows(name, dy, gw, kc, extras, epilogue, comm=None):
        return _matmul(name, dy, gw, extras, grid=(N_DEV, T // tb),
                       a_spec=pl.BlockSpec((tb, D), lambda j, i: (i, 0)),
                       b_spec=pl.BlockSpec((None, kc, D), lambda j, i: (j, 0, 0)),
                       extra_specs=[pl.BlockSpec((tb, kc), lambda j, i: (i, j))] * len(extras),
                       out_shapes=[_sds((T, kc * N_DEV), BF16)], out_specs=[pl.BlockSpec((tb, kc), lambda j, i: (i, j))],
                       dims=NT, epilogue=epilogue, comm=comm)

    def norm_bwd_epilogue(dh, xv, g, dxi):
        dxn, dg = _rms_bwd(xv, g, dh)
        return dxi + dxn, dxi + dxn, dg

    def transposed(gw):
        return gw.transpose(0, 2, 1).reshape(gw.shape[0] * gw.shape[2], D)

    def back_cols(name, da, gwt, xv, g, dx_in, comm=None):
        n = da.shape[1]
        row = _row_spec(tk, D)
        return _matmul(name, da, gwt, [xv, g.reshape(1, D), dx_in], grid=(T // tk,),
                       a_spec=_row_spec(tk, n), b_spec=_const_spec((n, D)), extra_specs=[row, _const_spec((1, D)), row],
                       out_shapes=[_sds((T, D), F32), _sds((T, D), BF16), _sds((1, D), F32)], out_specs=[row, row, _const_spec((1, D))],
                       dims=NN, epilogue=norm_bwd_epilogue, n_sum=1, comm=comm)

    def token_sum(tt):
        return dict(k_axis=1, nk=T // tt) if T // tt > 1 else dict(k_axis=None)

    def wgrad_cols(name, h, da, nc, comm=None):
        res = _matmul(name, h, da, [], grid=(N_DEV, T // tb),
                       a_spec=pl.BlockSpec((tb, D), lambda j, t: (t, 0)), b_spec=pl.BlockSpec((tb, nc), lambda j, t: (t, j)),
                       extra_specs=[], out_shapes=[_sds((N_DEV, D, nc), BF16)],
                       out_specs=[pl.BlockSpec((None, D, nc), lambda j, t: (j, 0, 0))],
                       dims=TN, acc_shape=(D, nc), comm=comm, **token_sum(tb))
        return res[0] if comm is None else res

    def wgrad_rows(name, a, dy, kc, ncols, tt, comm=None):
        res = _matmul(name, a, dy, [], grid=(a.shape[1] // kc, T // tt),
                      a_spec=pl.BlockSpec((tt, kc), lambda j, t: (t, j)), b_spec=pl.BlockSpec((tt, ncols), lambda j, t: (t, 0)),
                      extra_specs=[], out_shapes=[_sds((a.shape[1], ncols), BF16)],
                      out_specs=[pl.BlockSpec((kc, ncols), lambda j, t: (j, 0))],
                      dims=TN, acc_shape=(kc, ncols), comm=comm, **token_sum(tt))
        return res[0] if comm is None else res

    saved = []
    xs = x2
    for i in range(depth):
        l = i // 2
        if i == 0:
            h, *first_w = rmsnorm(xs, norm_mix[0], _gather_level2(first_half))
            mla_w[0] = mla_layouts(*first_w)
        if i % 2 == 0:
            w_dkv, w_uq, w_ukv, w_o = mla_w[l]
            lat = _matmul("mla_down", h, w_dkv, [], grid=(nt,), a_spec=_row_spec(tm, D),
                          b_spec=pl.BlockSpec((None, D, LAT_PAD), lambda i_: (0, 0, 0)), extra_specs=[],
                          out_shapes=[_sds((T, LAT_PAD), F32)], out_specs=[_row_spec(tm, LAT_PAD)], dims=NN)[0]

            def latent_post(la, qn, kvn, cc, sa, sb):
                cq = _rms_fwd(la[:, :Q_RANK], qn)
                ckv = _rms_fwd(la[:, Q_RANK:Q_RANK + KV_RANK], kvn)
                kr = _rope_fwd(la[:, Q_RANK + KV_RANK:], cc, sa, sb)
                return cq, ckv, kr

            cq, ckv, kr = _rowwise(
                "mla_latent", latent_post, [lat, mla_q_norm[l].reshape(1, Q_RANK), mla_kv_norm[l].reshape(1, KV_RANK), t_cc, t_sa, t_sb],
                grid=(nt,), in_specs=[_row_spec(tm, LAT_PAD), _const_spec((1, Q_RANK)), _const_spec((1, KV_RANK))] + tab_specs,
                out_shapes=[_sds((T, Q_RANK), BF16), _sds((T, KV_RANK), BF16), _sds((T, 128), BF16)],
                out_specs=[_row_spec(tm, Q_RANK), _row_spec(tm, KV_RANK), _row_spec(tm, 128)])

            def q_epilogue(acc, cc, sa, sb):
                parts = []
                for b in range(HEADS):
                    parts += [acc[:, b * QPAD:b * QPAD + NOPE], _rope_fwd(acc[:, b * QPAD + NOPE:(b + 1) * QPAD], cc, sa, sb)]
                return (jnp.concatenate(parts, axis=1),)

            q = _matmul("mla_q", cq, w_uq, [t_cc, t_sa, t_sb], grid=(nt,), a_spec=_row_spec(tm, Q_RANK),
                        b_spec=pl.BlockSpec((None, Q_RANK, HW), lambda i_: (0, 0, 0)), extra_specs=tab_specs,
                        out_shapes=[_sds((T, HW), BF16)], out_specs=[_row_spec(tm, HW)], dims=NN, epilogue=q_epilogue)[0]

            def kv_write(outs, acc, krb):
                k_ref, v_ref, vt_ref = outs
                for b in range(HEADS):
                    vb = acc[:, b * QPAD + NOPE:(b + 1) * QPAD]
                    k_ref[:, b * QPAD:b * QPAD + NOPE] = acc[:, b * QPAD:b * QPAD + NOPE].astype(BF16)
                    k_ref[:, b * QPAD + NOPE:(b + 1) * QPAD] = krb
                    v_ref[:, b * VDIM:(b + 1) * VDIM] = vb.astype(BF16)
                    vbt = vb.T.astype(BF16)
                    for u in range(tm // tq):
                        vt_ref[b, u] = vbt[:, u * tq:(u + 1) * tq]

            kk, vv, vt = _matmul("mla_kv", ckv, w_ukv, [kr], grid=(nt,), a_spec=_row_spec(tm, KV_RANK),
                                 b_spec=pl.BlockSpec((None, KV_RANK, HW), lambda i_: (0, 0, 0)), extra_specs=[_row_spec(tm, 128)],
                                 out_shapes=[_sds((T, HW), BF16), _sds((T, OW), BF16), _sds((HEADS, T // tq, VDIM, tq), BF16)],
                                 out_specs=[_row_spec(tm, HW), _row_spec(tm, OW), pl.BlockSpec((HEADS, tm // tq, VDIM, tq), lambda i_: (0, i_, 0, 0))],
                                 dims=NN, write=kv_write)
            group = [up_sh[i], down_sh[i], in_sh[l], out_sh[l]] + (small_later if i == 0 else [])
            o, lse, *bufs = _flash_fwd(q, kk, vt, tq, comm=_gather_level1(group))
            xm, h2, g_up[i], g_down[i] = _matmul(
                "mla_out", o, w_o, [xs, norm_ffn[i].reshape(1, D)], grid=(nt,), a_spec=_row_spec(tm, OW),
                b_spec=pl.BlockSpec((None, OW, D), lambda i_: (0, 0, 0)), extra_specs=[_row_spec(tm, D), _const_spec((1, D))],
                out_shapes=[_sds((T, D), F32), _sds((T, D), BF16)], out_specs=[_row_spec(tm, D)] * 2, dims=NN,
                epilogue=residual_norm, comm=_gather_level2(bufs[:2]))
            half_gathered = bufs[2:]
            mix_saved = (h, lat, cq, ckv, q, kk, vv, o, lse)
        else:
            gp, ge, g_down[i], up_half = proj_cols("sgu_in", h, g_in[l], e2c, _gelu_and_grad, 2,
                                                   comm=_merge_comm(_gather_level2([down_half]), _gather_level1([up_sh[i]])))
            gate = _sgu_mid_fwd(ge, ln_g_full[l], ln_b_full[l], sgu_w_spatial[l], b_sp[l], 4)
            xm, h2, g_up[i] = proj_rows_residual("sgu_out", gate, g_out[l], xs, norm_ffn[i], comm=_gather_level2([up_half]))
            mix_saved = (h, gp, ge, gate)
        r, s, *rest = proj_cols("ffn_up", h2, g_up[i], ffc, lambda acc: (jnp.maximum(acc, 0.0), jnp.square(jnp.maximum(acc, 0.0))), 2,
                                comm=_merge_comm(_gather_level2(half_gathered), _gather_level1([down_sh[i + 1]], part=(0, 2))) if i % 2 == 0 else None)
        if i % 2 == 0:
            g_in[l], g_out[l], *small_gathered, down_part = rest
        if i == 0:
            for l_ in range(1, n_mla):
                mla_w[l_] = mla_layouts(*small_gathered[4 * (l_ - 1):4 * l_])
            g_ln = small_gathered[-1]
            ln_g_full = [g_ln[:, l_, :].reshape(1, E) for l_ in range(n_sgu)]
            ln_b_full = [g_ln[:, n_sgu + l_, :].reshape(1, E) for l_ in range(n_sgu)]
        xo, h_next, *rest = proj_rows_residual("ffn_down", s, g_down[i], xm, norm_mix[i + 1] if i + 1 < depth else final_norm,
                                               comm=_gather_level1([down_sh[i + 1]], part=(1, 2), into=[down_part]) if i % 2 == 0 else None)
        if i % 2 == 0:
            (down_half,) = rest
        saved.append((xs, xm, mix_saved, h2, r, s))
        xs, h = xo, h_next

    def loss_head(xv, tg, g):
        y = _rms_fwd(xv, g)
        err = y - tg
        part = 0.5 * jnp.sum(jnp.sum(err * err, axis=-1, keepdims=True), axis=0, keepdims=True) / D
        dx, dg = _rms_bwd(xv, g, err / D)
        return dx, dx, jnp.broadcast_to(part, (1, 128)), dg

    dx, dyb, loss_part, d_final = _rowwise(
        "loss_head", loss_head, [xs, tgt, final_norm.reshape(1, D)], grid=(nt,),
        in_specs=[_row_spec(tm, D), _row_spec(tm, D), _const_spec((1, D))],
        out_shapes=[_sds((T, D), F32), _sds((T, D), BF16), _sds((1, 128), F32), _sds((1, D), F32)],
        out_specs=[_row_spec(tm, D), _row_spec(tm, D), _const_spec((1, 128)), _const_spec((1, D))], n_acc=2)
    loss = lax.psum(loss_part[0, 0], ("x", "y", "c"))

    d_norm_mix, d_norm_ffn = [None] * depth, [None] * depth
    d_qn, d_kvn = [None] * n_mla, [None] * n_mla
    d_wsp, d_bsp, d_lng, d_lnb = [None] * n_sgu, [None] * n_sgu, [None] * n_sgu, [None] * n_sgu
    layers = {"dkv": n_mla, "uq": n_mla, "ukv": n_mla, "o": n_mla, "in": n_sgu, "out": n_sgu, "up": depth, "down": depth}
    stacked = {nm: None for nm in layers}
    pending = []
    summed = []

    def add_pairs(gs, rcvs):
        operands, in_specs, out_shapes, out_specs = [], [], [], []
        for g, rcv in zip(gs, rcvs):
            _, rws, cls = g.shape
            slab = pl.BlockSpec((None, rws, cls), lambda ch, cr: (ch, 0, 0))
            operands += [g.reshape(N_CHIP, 2, rws, cls), rcv]
            in_specs += [pl.BlockSpec((None, None, rws, cls), lambda ch, cr: (ch, cr[0], 0, 0)), slab]
            out_shapes.append(_sds(rcv.shape, BF16))
            out_specs.append(slab)

        def fn(*blocks):
            return tuple(blocks[2 * k].astype(F32) + blocks[2 * k + 1].astype(F32) for k in range(len(gs)))

        return _rowwise("grad_pair_sum", fn, operands, grid=(N_CHIP,), in_specs=in_specs, out_shapes=out_shapes, out_specs=out_specs,
                        grid_spec_prefetch=cidx)

    def sibling_comm():
        return _sibling_exchange([g for _, _, g in pending]) if pending else None

    def absorb(from_sibling):
        if pending:
            parts = add_pairs([g for _, _, g in pending], list(from_sibling))
            summed.extend((nm, l_, p) for (nm, l_, _), p in zip(pending, parts))
            pending.clear()

    def chip_comm():
        if pending:
            absorb(_comm_call("grad_sibling_exchange", sibling_comm()))
        comm, names = _chip_exchange([p for _, _, p in summed], [(nm, l_) for nm, l_, _ in summed], layers, stacked)
        summed.clear()
        return comm, names

    def rows128(a, rows):
        flat = a.reshape(-1, 128)
        return jnp.pad(flat, ((0, rows - flat.shape[0]), (0, 0)))

    def pad_to(n, mult):
        return -(-n // mult) * mult

    def packed(arrs, sizes):
        return jnp.concatenate([rows128(a, sz) for a, sz in zip(arrs, sizes)], axis=0)

    n_wsp, n_bsp, n_ln = sgu_w_spatial.size // 128, pad_to(sgu_b_spatial.size // 128, 8), pad_to(n_sgu * E // 128, 8)
    early_sizes = [n_wsp, pad_to(n_wsp + n_bsp, SMALL_ROWS) - n_wsp, n_ln, n_ln]
    early_rep = early_sizes[0] + early_sizes[1]
    gathered_early = None

    for i in reversed(range(depth)):
        l = i // 2
        xs_i, xm, mix_saved, h2, r, s = saved[i]
        comm = sibling_comm()
        if i == 0:
            comm = _gather_level2([early_half]) if comm is None else _merge_comm(comm, _gather_level2([early_half]))
        da, *rcv = back_rows("ffn_down_bwd", dyb, g_down[i], ffc, [r], lambda acc, rr: (acc * (2.0 * rr.astype(F32)),), comm=comm)
        if i == 0:
            *rcv, gathered_early = rcv
        absorb(rcv)
        if i == 0 and any(nm == "down" for nm, _, _ in summed):
            nm_, l_, part = summed.pop([nm for nm, _, _ in summed].index("down"))
            comm, names = _chip_exchange([part], [(nm_, l_)], layers, stacked)
            g_down_i, *bufs = wgrad_rows("ffn_down_wgrad", s, dyb, ffc, D, tb, comm=comm)
            stacked.update(dict(zip(names, bufs)))
        else:
            g_down_i = wgrad_rows("ffn_down_wgrad", s, dyb, ffc, D, tb)
        pending.append(("down", i, g_down_i.reshape(N_DEV, ffc, D)))
        pending.append(("up", i, wgrad_cols("ffn_up_wgrad", h2, da, ffc)))
        dx, dyb, d_norm_ffn[i], *rcv = back_cols("ffn_up_bwd", da, transposed(g_up[i]), xm, norm_ffn[i], dx, comm=sibling_comm())
        absorb(rcv)
        if i % 2 == 0:
            h, lat, cq, ckv, q, kk, vv, o, lse = mix_saved
            w_dkv, w_uq, w_ukv, w_o = mla_w[l]
            do = _matmul("mla_out_bwd", dyb, w_o, [], grid=(nt,), a_spec=_row_spec(tm, D),
                         b_spec=pl.BlockSpec((None, OW, D), lambda i_: (0, 0, 0)), extra_specs=[],
                         out_shapes=[_sds((T, OW), BF16)], out_specs=[_row_spec(tm, OW)], dims=NT)[0]
            g_o_l = wgrad_rows("mla_out_wgrad", o, dyb, OW, D, tm).reshape(N_DEV, owc, D)
            comm, names = chip_comm()
            dq_pre, dk, dv, *bufs = _flash_bwd(q, kk, vv, o, do, lse, (t_cc, t_sa, t_sb), tq, comm=comm)
            stacked.update(dict(zip(names, bufs)))
            pending.append(("o", l, g_o_l))

            def kv_pre(dkb, dvb, cc, sa, sb):
                parts, dkr = [], None
                for b in range(HEADS):
                    parts += [dkb[:, b * QPAD:b * QPAD + NOPE], dvb[:, b * VDIM:(b + 1) * VDIM]]
                    piece = dkb[:, b * QPAD + NOPE:(b + 1) * QPAD].astype(F32)
                    dkr = piece if dkr is None else dkr + piece
                return jnp.concatenate(parts, axis=1), _rope_bwd(dkr, cc, sa, sb)

            dkv, dkr = _rowwise("mla_dkv_rope", kv_pre, [dk, dv, t_cc, t_sa, t_sb], grid=(T // ts,),
                                in_specs=[_row_spec(ts, HW), _row_spec(ts, OW)] + [_row_spec(ts, 128)] * 3,
                                out_shapes=[_sds((T, HW), BF16), _sds((T, 128), F32)], out_specs=[_row_spec(ts, HW), _row_spec(ts, 128)])
            g_uq_l = wgrad_rows("mla_q_wgrad", cq, dq_pre, Q_RANK, HW, tm)
            g_ukv_l = wgrad_rows("mla_kv_wgrad", ckv, dkv, KV_RANK, HW, tm)
            pending.append(("uq", l, g_uq_l.reshape(Q_RANK, HEADS, QPAD)[:, :, :NOPE + ROPE].transpose(1, 0, 2)))
            pending.append(("ukv", l, g_ukv_l.reshape(KV_RANK, HEADS, NOPE + VDIM).transpose(1, 0, 2)))
            dcq = _matmul("mla_q_bwd", dq_pre, w_uq, [], grid=(nt,), a_spec=_row_spec(tm, HW),
                          b_spec=pl.BlockSpec((None, Q_RANK, HW), lambda i_: (0, 0, 0)), extra_specs=[],
                          out_shapes=[_sds((T, Q_RANK), F32)], out_specs=[_row_spec(tm, Q_RANK)], dims=NT)[0]
            dckv = _matmul("mla_kv_bwd", dkv, w_ukv, [], grid=(nt,), a_spec=_row_spec(tm, HW),
                           b_spec=pl.BlockSpec((None, KV_RANK, HW), lambda i_: (0, 0, 0)), extra_specs=[],
                           out_shapes=[_sds((T, KV_RANK), F32)], out_specs=[_row_spec(tm, KV_RANK)], dims=NT)[0]

            def latent_bwd(la, qn, kvn, dq_, dkv_, dkr_):
                dcq_raw, dqn = _rms_bwd(la[:, :Q_RANK], qn, dq_)
                dckv_raw, dkvn = _rms_bwd(la[:, Q_RANK:Q_RANK + KV_RANK], kvn, dkv_)
                return jnp.concatenate([dcq_raw, dckv_raw, dkr_], axis=1), dqn, dkvn

            dlat, d_qn[l], d_kvn[l] = _rowwise(
                "mla_latent_bwd", latent_bwd, [lat, mla_q_norm[l].reshape(1, Q_RANK), mla_kv_norm[l].reshape(1, KV_RANK), dcq, dckv, dkr],
                grid=(nt,), in_specs=[_row_spec(tm, LAT_PAD), _const_spec((1, Q_RANK)), _const_spec((1, KV_RANK)),
                                      _row_spec(tm, Q_RANK), _row_spec(tm, KV_RANK), _row_spec(tm, 128)],
                out_shapes=[_sds((T, LAT_PAD), BF16), _sds((1, Q_RANK), F32), _sds((1, KV_RANK), F32)],
                out_specs=[_row_spec(tm, LAT_PAD), _const_spec((1, Q_RANK)), _const_spec((1, KV_RANK))], n_acc=2)
            g_dkv_l = wgrad_rows("mla_down_wgrad", h, dlat, D, LAT_PAD, tm)
            pending.append(("dkv", l, g_dkv_l[:, :LAT].reshape(N_DEV, dc, LAT)))
            dx, dyb, d_norm_mix[i] = _matmul(
                "mla_down_bwd", dlat, w_dkv, [xs_i, norm_mix[i].reshape(1, D), dx], grid=(nt,), a_spec=_row_spec(tm, LAT_PAD),
                b_spec=pl.BlockSpec((None, D, LAT_PAD), lambda i_: (0, 0, 0)), extra_specs=[_row_spec(tm, D), _const_spec((1, D)), _row_spec(tm, D)],
                out_shapes=[_sds((T, D), F32), _sds((T, D), BF16), _sds((1, D), F32)],
                out_specs=[_row_spec(tm, D), _row_spec(tm, D), _const_spec((1, D))], dims=NT, epilogue=norm_bwd_epilogue, n_sum=1)
        else:
            h, gp, ge, gate = mix_saved
            (dgate,) = back_rows("sgu_out_bwd", dyb, g_out[l], ec, [], None)
            pending.append(("out", l, wgrad_rows("sgu_out_wgrad", gate, dyb, ec, D, tb).reshape(N_DEV, ec, D)))
            dz, d_wsp[l], d_bsp[l], d_lng[l], d_lnb[l] = _sgu_mid_bwd(ge, gp, dgate, ln_g_full[l], ln_b_full[l], sgu_w_spatial[l], b_sp[l], 2)
            if i == 1:
                early = packed([jnp.stack(d_wsp, 0), jnp.stack(d_bsp, 0), jnp.concatenate(d_lng, 0), jnp.concatenate(d_lnb, 0)], early_sizes)
                g_in_l, early_part = wgrad_cols("sgu_in_wgrad", h, dz, e2c, comm=_gather_level1([early], part=(0, 2)))
            else:
                g_in_l = wgrad_cols("sgu_in_wgrad", h, dz, e2c)
            pending.append(("in", l, g_in_l))
            comm = sibling_comm()
            if i == 1:
                comm = _merge_comm(comm, _gather_level1([early], part=(1, 2), into=[early_part]))
            dx, dyb, d_norm_mix[i], *rcv = back_cols("sgu_in_bwd", dz, transposed(g_in[l]), xs_i, norm_mix[i], dx, comm=comm)
            if i == 1:
                *rcv, early_half = rcv
            absorb(rcv)
    grad_x = dx.reshape(1, T, D)

    last_comm, last_names = chip_comm()
    late_g = [jnp.concatenate(d_norm_mix, 0), jnp.concatenate(d_norm_ffn, 0), d_final, jnp.concatenate(d_qn, 0), jnp.concatenate(d_kvn, 0)]
    late_w = [norm_mix, norm_ffn, final_norm, mla_q_norm, mla_kv_norm]
    late_m = [m_norm_mix, m_norm_ffn, m_final_norm, m_mla_q_norm, m_mla_kv_norm]
    late_v = [v_norm_mix, v_norm_ffn, v_final_norm, v_mla_q_norm, v_mla_kv_norm]
    late_sizes = [pad_to(g.size // 128, 8) for g in late_g]
    late_rows = sum(late_sizes)

    def adam_big(parts, w, m, v):
        lyr, rws, cls = w.shape
        rt = _tile(rws, 512)

        def fn(p, w_, m_, v_):
            g = (p[0].astype(F32) + p[1].astype(F32)) + (p[2].astype(F32) + p[3].astype(F32))
            return (g, *_adam(w_, g, m_, v_))

        spec = pl.BlockSpec((None, rt, cls), lambda l_, i_: (l_, i_, 0))
        return _rowwise("adam_large", fn, [parts, w, m, v], grid=(lyr, rws // rt),
                        in_specs=[pl.BlockSpec((N_CHIP, None, rt, cls), lambda l_, i_: (0, l_, i_, 0)), spec, spec, spec],
                        out_shapes=[_sds(w.shape, F32)] * 4, out_specs=[spec] * 4)

    stacked.update(dict(zip(last_names, _comm_call("grad_chip_exchange", last_comm))))
    (gathered_late,) = _all_gather("gather_small_grads", [packed(late_g, late_sizes)])
    big = {}
    big["in"] = adam_big(stacked["in"], sgu_w_in, m_sgu_w_in, v_sgu_w_in)
    big["up"] = adam_big(stacked["up"], ffn_w_up, m_ffn_w_up, v_ffn_w_up)
    big["down"] = adam_big(stacked["down"], ffn_w_down, m_ffn_w_down, v_ffn_w_down)
    big["out"] = adam_big(stacked["out"], sgu_w_out, m_sgu_w_out, v_sgu_w_out)
    big["dkv"] = adam_big(stacked["dkv"], mla_w_dkv, m_mla_w_dkv, v_mla_w_dkv)
    big["uq"] = adam_big(stacked["uq"], mla_w_uq, m_mla_w_uq, v_mla_w_uq)
    big["ukv"] = adam_big(stacked["ukv"], mla_w_ukv, m_mla_w_ukv, v_mla_w_ukv)
    big["o"] = adam_big(stacked["o"], mla_w_o, m_mla_w_o, v_mla_w_o)
    big_res = [big[nm][:4] for nm in ("dkv", "uq", "ukv", "o", "in", "out", "up", "down")]

    def sum8(p):
        return ((p[0] + p[1]) + (p[2] + p[3])) + ((p[4] + p[5]) + (p[6] + p[7]))

    def adam_packed(name, gathered, ws, ms, vs, sizes, rows, tile):
        spec = _row_spec(tile, 128)
        return _rowwise(name, lambda p, w_, m_, v_: (sum8(p), *_adam(w_, sum8(p), m_, v_)),
                        [gathered, packed(ws, sizes), packed(ms, sizes), packed(vs, sizes)], grid=(rows // tile,),
                        in_specs=[pl.BlockSpec((N_DEV, tile, 128), lambda i_: (0, i_, 0)), spec, spec, spec],
                        out_shapes=[_sds((rows, 128), F32)] * 4, out_specs=[spec] * 4)

    late_res = adam_packed("adam_small", gathered_late, late_w, late_m, late_v, late_sizes, late_rows, late_rows)
    early_res = adam_packed("adam_spatial", gathered_early, [sgu_w_spatial, sgu_b_spatial], [m_sgu_w_spatial, m_sgu_b_spatial],
                            [v_sgu_w_spatial, v_sgu_b_spatial], early_sizes[:2], early_rep, SMALL_ROWS)

    def unpack(res, sizes, k, like):
        off = sum(sizes[:k])
        return res[off:off + like.size // 128].reshape(like.shape)

    my_b = 4 * lax.axis_index("x") + 2 * lax.axis_index("y") + lax.axis_index("c")
    ln_w = jnp.concatenate([sgu_ln_g, sgu_ln_b], 0)
    ln_m = jnp.concatenate([m_sgu_ln_g, m_sgu_ln_b], 0)
    ln_v = jnp.concatenate([v_sgu_ln_g, v_sgu_ln_b], 0)
    ln_all = jnp.concatenate([gathered_early[:, early_rep:early_rep + n_sgu * E // 128], gathered_early[:, early_rep + n_ln:early_rep + n_ln + n_sgu * E // 128]], axis=1)
    ln_mine = lax.dynamic_slice_in_dim(ln_all.reshape(N_DEV, 2 * n_sgu, N_DEV, ec), my_b, 1, axis=2).reshape(N_DEV, 2 * n_sgu, ec)
    ln_g_, ln_d, ln_m2, ln_v2 = _rowwise(
        "adam_ln", lambda p, w_, m_, v_: (sum8(p), *_adam(w_, sum8(p), m_, v_)), [ln_mine, ln_w, ln_m, ln_v], grid=(1,),
        in_specs=[_const_spec(ln_mine.shape), _const_spec(ln_w.shape), _const_spec(ln_w.shape), _const_spec(ln_w.shape)],
        out_shapes=[_sds(ln_w.shape, F32)] * 4, out_specs=[_const_spec(ln_w.shape)] * 4)

    def family(pos):
        ln = [ln_g_, ln_d, ln_m2, ln_v2][pos]
        late = [unpack(late_res[pos], late_sizes, k, w_) for k, w_ in enumerate(late_w)]
        w_sp_, b_sp_ = unpack(early_res[pos], early_sizes, 0, sgu_w_spatial), unpack(early_res[pos], early_sizes, 1, sgu_b_spatial)
        bigs = [res[pos] for res in big_res]
        return [late[0], late[1], late[2], bigs[0], late[3], late[4], bigs[1], bigs[2], bigs[3],
                bigs[4], ln[:n_sgu], ln[n_sgu:], w_sp_, b_sp_, bigs[5], bigs[6], bigs[7]]

    return (loss, grad_x, *family(0), *family(1), *family(2), *family(3))
```

```python
import math

import jax
import jax.numpy as jnp
from jax import lax
from jax.experimental import pallas as pl
from jax.experimental.pallas import tpu as pltpu

F32 = jnp.float32
BF16 = jnp.bfloat16
MESH = pl.DeviceIdType.MESH

N_DEV = 8
N_CHIP = 4
HEADS = 8
NOPE = 128
ROPE = 64
VDIM = 128
QPAD = 256
Q_RANK = 256
KV_RANK = 128
LAT = Q_RANK + KV_RANK + ROPE
LAT_PAD = 512
ROPE_THETA = 10000.0
SGU_CHUNK = 128
SGU_GROUPS = 8
NORM_EPS = 1e-6
LN_EPS = 1e-5
ADAM_LR = 0.001
ADAM_B1 = 0.9
ADAM_B2 = 0.999
ADAM_EPS = 1e-08
ADAM_WD = 0.01
ADAM_STEP = 10
ATTN_SCALE = (NOPE + ROPE) ** -0.5
NEG = -1e30
EXP2_SCALE = ATTN_SCALE * math.log2(math.e)
VMEM_LIMIT = 56 * 1024 * 1024
SMALL_ROWS = 256

NN = (((1,), (0,)), ((), ()))
NT = (((1,), (1,)), ((), ()))
TN = (((0,), (0,)), ((), ()))
ANY = pl.BlockSpec(memory_space=pl.ANY)


def _pcall(body, **kw):
    return pl.pallas_call(body, **kw)


def _params(n_grid, side_effects=False):
    return pltpu.CompilerParams(dimension_semantics=("arbitrary",) * n_grid, vmem_limit_bytes=VMEM_LIMIT, has_side_effects=side_effects)


def _sds(shape, dtype):
    return jax.ShapeDtypeStruct(tuple(shape), dtype)


def _tile(n, want):
    t = min(n, want)
    assert n % t == 0, (n, want)
    return t


class _Comm:
    def __init__(self, operands, out_shapes, aliases, scratch, start, finish):
        self.operands, self.out_shapes, self.aliases, self.scratch = operands, out_shapes, aliases, scratch
        self.start, self.finish = start, finish


def _merge_comm(first, second):
    n_in, n_out, n_sc = len(first.operands), len(first.out_shapes), len(first.scratch)
    aliases = dict(first.aliases)
    aliases.update({n_in + k: n_out + v for k, v in second.aliases.items()})

    def start(ins, outs, sems):
        first.start(ins[:n_in], outs[:n_out], sems[:n_sc])
        second.start(ins[n_in:], outs[n_out:], sems[n_sc:])

    def finish(ins, outs, sems):
        first.finish(ins[:n_in], outs[:n_out], sems[:n_sc])
        second.finish(ins[n_in:], outs[n_out:], sems[n_sc:])

    return _Comm([*first.operands, *second.operands], [*first.out_shapes, *second.out_shapes], aliases,
                 [*first.scratch, *second.scratch], start, finish)


def _place():
    return lax.axis_index("x"), lax.axis_index("y"), lax.axis_index("c")


def _other_chips(x, y):
    return [(1 - x, y), (x, 1 - y), (1 - x, 1 - y)]


def _dev_index(dev):
    return 4 * dev[0] + 2 * dev[1] + dev[2]


def _comm_call(name, comm):
    c_in, c_out = len(comm.operands), len(comm.out_shapes)

    def body(*refs):
        ins, outs, sems = refs[:c_in], refs[c_in:c_in + c_out], refs[c_in + c_out:]
        comm.start(ins, outs, sems)
        comm.finish(ins, outs, sems)

    return _pcall(body, name=name, in_specs=[ANY] * c_in, out_specs=[ANY] * c_out, out_shape=comm.out_shapes,
                  scratch_shapes=comm.scratch, input_output_aliases=dict(comm.aliases),
                  compiler_params=pltpu.CompilerParams(has_side_effects=True))(*comm.operands)


def _call(name, body, operands, in_specs, out_shapes, out_specs, scratch, grid, comm=None):
    if comm is None:
        return _pcall(body, name=name, grid=grid, in_specs=in_specs, out_specs=out_specs, out_shape=out_shapes,
                      scratch_shapes=scratch, compiler_params=_params(len(grid)))(*operands)
    n_in, n_out, n_sc = len(operands), len(out_shapes), len(scratch)
    c_in, c_out = len(comm.operands), len(comm.out_shapes)

    def hosted(*refs):
        ins, cins = refs[:n_in], refs[n_in:n_in + c_in]
        o0 = n_in + c_in
        outs, couts = refs[o0:o0 + n_out], refs[o0 + n_out:o0 + n_out + c_out]
        rest = refs[o0 + n_out + c_out:]
        sc, csems = rest[:n_sc], rest[n_sc:]
        first = pl.program_id(0) == 0
        last = pl.program_id(0) == grid[0] - 1
        for d in range(1, len(grid)):
            first = jnp.logical_and(first, pl.program_id(d) == 0)
            last = jnp.logical_and(last, pl.program_id(d) == grid[d] - 1)

        @pl.when(first)
        def _():
            comm.start(cins, couts, csems)

        body(*ins, *outs, *sc)

        @pl.when(last)
        def _():
            comm.finish(cins, couts, csems)

    return _pcall(hosted, name=name, grid=grid, in_specs=[*in_specs, *[ANY] * c_in], out_specs=[*out_specs, *[ANY] * c_out],
                  out_shape=[*out_shapes, *comm.out_shapes], scratch_shapes=[*scratch, *comm.scratch],
                  input_output_aliases={n_in + k: n_out + v for k, v in comm.aliases.items()},
                  compiler_params=_params(len(grid), side_effects=True))(*operands, *comm.operands)


def _gather_level1(shards, part=(0, 1), into=None):
    n = len(shards)
    k_part, m_part = part

    def copies(ins, outs, sems):
        send_sems, recv_sems, local_sems = sems
        x, y, c = _place()
        me, sibling = (x, y, c), (x, y, 1 - c)
        chips = _other_chips(x, y)

        def rows(a):
            r = shards[a].shape[0] // m_part
            return pl.ds(k_part * r, r)

        def copy(a, k, block, to, own=False):
            slot = outs[a].at[_dev_index(block), rows(a)]
            return pltpu.make_async_remote_copy(src_ref=ins[a].at[rows(a)] if own else slot, dst_ref=slot, send_sem=send_sems.at[a, k],
                                                recv_sem=recv_sems.at[a, k], device_id=to, device_id_type=MESH)

        mine = [pltpu.make_async_copy(ins[a].at[rows(a)], outs[a].at[_dev_index(me), rows(a)], local_sems.at[a]) for a in range(n)]
        sends = [copy(a, 1 + j, me, (*chip, c), own=True) for j, chip in enumerate(chips) for a in range(n)]
        sends += [copy(a, 0, me, sibling, own=True) for a in range(n)]
        recvs = [copy(a, 1 + j, (*chip, c), me) for j, chip in enumerate(chips) for a in range(n)]
        recvs += [copy(a, 0, sibling, me) for a in range(n)]
        return mine, sends, recvs

    def start(ins, outs, sems):
        mine, sends, _ = copies(ins, outs, sems)
        for cp in mine + sends:
            cp.start()

    def finish(ins, outs, sems):
        mine, sends, recvs = copies(ins, outs, sems)
        for cp in recvs:
            cp.wait_recv()
        for cp in sends:
            cp.wait_send()
        for cp in mine:
            cp.wait()

    return _Comm([*shards, *(into or [])], [_sds((N_DEV, *a.shape), a.dtype) for a in shards], {n + a: a for a in range(n)} if into else {},
                 [pltpu.SemaphoreType.DMA((n, 4)), pltpu.SemaphoreType.DMA((n, 4)), pltpu.SemaphoreType.DMA((n,))], start, finish)


def _gather_level2(bufs):
    n = len(bufs)

    def copies(outs, sems):
        send_sems, recv_sems = sems
        x, y, c = _place()
        sibling = (x, y, 1 - c)
        sends, recvs = [], []
        for j, chip in enumerate(_other_chips(x, y)):
            for a in range(n):
                have, want = outs[a].at[_dev_index((*chip, c))], outs[a].at[_dev_index((*chip, 1 - c))]
                sends.append(pltpu.make_async_remote_copy(src_ref=have, dst_ref=have, send_sem=send_sems.at[a, j], recv_sem=recv_sems.at[a, j],
                                                          device_id=sibling, device_id_type=MESH))
                recvs.append(pltpu.make_async_remote_copy(src_ref=want, dst_ref=want, send_sem=send_sems.at[a, j], recv_sem=recv_sems.at[a, j],
                                                          device_id=sibling, device_id_type=MESH))
        return sends, recvs

    def start(ins, outs, sems):
        for cp in copies(outs, sems)[0]:
            cp.start()

    def finish(ins, outs, sems):
        sends, recvs = copies(outs, sems)
        for cp in recvs:
            cp.wait_recv()
        for cp in sends:
            cp.wait_send()

    return _Comm(bufs, [_sds(b.shape, b.dtype) for b in bufs], {a: a for a in range(n)},
                 [pltpu.SemaphoreType.DMA((n, 3)), pltpu.SemaphoreType.DMA((n, 3))], start, finish)


def _all_gather(name, arrays):
    n = len(arrays)

    def body(*refs):
        ins = refs[:n]
        outs = refs[n:2 * n]
        send_sems, recv_sems, local_sems = refs[2 * n:]
        x, y, c = _place()
        me, sibling = (x, y, c), (x, y, 1 - c)
        chips = _other_chips(x, y)

        def copy(a, k, block, to, src=None):
            slot = outs[a].at[_dev_index(block)]
            return pltpu.make_async_remote_copy(src_ref=slot if src is None else src, dst_ref=slot, send_sem=send_sems.at[a, k],
                                                recv_sem=recv_sems.at[a, k], device_id=to, device_id_type=MESH)

        mine = [pltpu.make_async_copy(ins[a], outs[a].at[_dev_index(me)], local_sems.at[a]) for a in range(n)]
        for cp in mine:
            cp.start()
        first = []
        for j, chip in enumerate(chips):
            first += [copy(a, 1 + j, me, (*chip, c), src=ins[a]) for a in range(n)]
        first += [copy(a, 0, me, sibling, src=ins[a]) for a in range(n)]
        for cp in first:
            cp.start()
        passed = []
        for j, chip in enumerate(chips):
            for a in range(n):
                copy(a, 1 + j, (*chip, c), me).wait_recv()
                fwd = copy(a, 4 + j, (*chip, c), sibling)
                fwd.start()
                passed.append(fwd)
        for a in range(n):
            copy(a, 0, sibling, me).wait_recv()
            for j, chip in enumerate(chips):
                copy(a, 4 + j, (*chip, 1 - c), me).wait_recv()
        for cp in first + passed:
            cp.wait_send()
        for cp in mine:
            cp.wait()

    return _pcall(
        body, name=name, in_specs=[ANY] * n, out_specs=[ANY] * n,
        out_shape=[_sds((N_DEV, *a.shape), a.dtype) for a in arrays],
        scratch_shapes=[pltpu.SemaphoreType.DMA((n, 7)), pltpu.SemaphoreType.DMA((n, 7)), pltpu.SemaphoreType.DMA((n,))],
        compiler_params=pltpu.CompilerParams(has_side_effects=True),
    )(*arrays)


def _sibling_exchange(grads):
    n = len(grads)

    def start(ins, outs, sems):
        send_sems, recv_sems = sems
        x, y, c = _place()
        for a in range(n):
            for ch in range(N_CHIP):
                pltpu.make_async_remote_copy(src_ref=ins[a].at[2 * ch + 1 - c], dst_ref=outs[a].at[ch], send_sem=send_sems.at[a],
                                             recv_sem=recv_sems.at[a], device_id=(x, y, 1 - c), device_id_type=MESH).start()

    def finish(ins, outs, sems):
        send_sems, recv_sems = sems
        x, y, c = _place()
        for a in range(n):
            pltpu.make_async_remote_copy(src_ref=outs[a], dst_ref=outs[a], send_sem=send_sems.at[a], recv_sem=recv_sems.at[a],
                                         device_id=(x, y, 1 - c), device_id_type=MESH).wait()

    return _Comm(grads, [_sds((N_CHIP, *g.shape[1:]), g.dtype) for g in grads], {},
                 [pltpu.SemaphoreType.DMA((n,)), pltpu.SemaphoreType.DMA((n,))], start, finish)


def _chip_exchange(parts, slots, layers, stacked):
    n = len(parts)
    names = []
    for nm, _ in slots:
        if nm not in names:
            names.append(nm)
    shapes = {nm: _sds((N_CHIP, layers[nm], *parts[a].shape[1:]), parts[a].dtype) for a, (nm, _) in enumerate(slots)}
    kept = [nm for nm in names if stacked.get(nm) is not None]
    aliases = {n + k: names.index(nm) for k, nm in enumerate(kept)}

    def copies(ins, outs, sems):
        send_sems, recv_sems, local_sems = sems
        x, y, c = _place()
        mine = 2 * x + y
        local, sends, recvs = [], [], []
        for a, (nm, l) in enumerate(slots):
            buf = outs[names.index(nm)]
            local.append(pltpu.make_async_copy(ins[a].at[mine], buf.at[mine, l], local_sems.at[a]))
            for j, chip in enumerate(_other_chips(x, y)):
                theirs = buf.at[2 * chip[0] + chip[1], l]
                sends.append(pltpu.make_async_remote_copy(src_ref=ins[a].at[2 * chip[0] + chip[1]], dst_ref=buf.at[mine, l], send_sem=send_sems.at[a, j],
                                                          recv_sem=recv_sems.at[a, j], device_id=(*chip, c), device_id_type=MESH))
                recvs.append(pltpu.make_async_remote_copy(src_ref=theirs, dst_ref=theirs, send_sem=send_sems.at[a, j],
                                                          recv_sem=recv_sems.at[a, j], device_id=(*chip, c), device_id_type=MESH))
        return local, sends, recvs

    def start(ins, outs, sems):
        local, sends, _ = copies(ins, outs, sems)
        for cp in local + sends:
            cp.start()

    def finish(ins, outs, sems):
        local, sends, recvs = copies(ins, outs, sems)
        for cp in recvs:
            cp.wait_recv()
        for cp in sends:
            cp.wait_send()
        for cp in local:
            cp.wait()

    comm = _Comm([*parts, *[stacked[nm] for nm in kept]], [shapes[nm] for nm in names], aliases,
                 [pltpu.SemaphoreType.DMA((n, 3)), pltpu.SemaphoreType.DMA((n, 3)), pltpu.SemaphoreType.DMA((n,))], start, finish)
    return comm, names


def _matmul(name, a, b, extras, *, grid, a_spec, b_spec, extra_specs, out_shapes, out_specs, dims, k_axis=None, nk=1,
            acc_shape=None, epilogue=None, comm=None, n_sum=0, write=None):
    n_extra = len(extras)
    n_out = len(out_shapes)

    def body(*refs):
        a_ref, b_ref = refs[0], refs[1]
        ex = refs[2:2 + n_extra]
        outs = refs[2 + n_extra:2 + n_extra + n_out]
        prod = lax.dot_general(a_ref[...], b_ref[...], dims, preferred_element_type=F32)

        def finish(acc):
            if write is not None:
                write(outs, acc, *[e[...] for e in ex])
                return
            res = epilogue(acc, *[e[...] for e in ex]) if epilogue is not None else (acc,)
            first = None
            for d in range(len(grid)):
                if d != k_axis:
                    here = pl.program_id(d) == 0
                    first = here if first is None else jnp.logical_and(first, here)
            for idx, (o, r) in enumerate(zip(outs, res)):
                if idx < n_out - n_sum:
                    o[...] = r.astype(o.dtype)
                else:
                    @pl.when(first)
                    def _(o=o, r=r):
                        o[...] = r.astype(o.dtype)

                    @pl.when(jnp.logical_not(first))
                    def _(o=o, r=r):
                        o[...] += r.astype(o.dtype)

        if k_axis is None:
            finish(prod)
        else:
            acc_ref = refs[-1]
            k = pl.program_id(k_axis)

            @pl.when(k == 0)
            def _():
                acc_ref[...] = prod

            @pl.when(k > 0)
            def _():
                acc_ref[...] += prod

            @pl.when(k == nk - 1)
            def _():
                finish(acc_ref[...])

    scratch = [] if k_axis is None else [pltpu.VMEM(acc_shape, F32)]
    return _call(name, body, [a, b, *extras], [a_spec, b_spec, *extra_specs], list(out_shapes), list(out_specs), scratch, grid, comm)


def _rowwise(name, fn, operands, *, grid, in_specs, out_shapes, out_specs, n_acc=0, grid_spec_prefetch=None, comm=None):
    n_in = len(operands)
    n_out = len(out_shapes)
    n_pre = 0 if grid_spec_prefetch is None else 1

    def body(*refs):
        refs = refs[n_pre:]
        ins = refs[:n_in]
        outs = refs[n_in:n_in + n_out]
        res = fn(*[r[...] for r in ins])
        if not isinstance(res, (tuple, list)):
            res = (res,)
        first = pl.program_id(0) == 0
        for d in range(1, len(grid)):
            first = jnp.logical_and(first, pl.program_id(d) == 0)
        for idx, (o, r) in enumerate(zip(outs, res)):
            if idx < n_out - n_acc:
                o[...] = r.astype(o.dtype)
            else:
                @pl.when(first)
                def _(o=o, r=r):
                    o[...] = r.astype(o.dtype)

                @pl.when(jnp.logical_not(first))
                def _(o=o, r=r):
                    o[...] += r.astype(o.dtype)

    if comm is not None:
        return _call(name, body, list(operands), list(in_specs), list(out_shapes), list(out_specs), [], grid, comm)
    if grid_spec_prefetch is None:
        return _pcall(body, name=name, grid=grid, in_specs=in_specs, out_specs=out_specs, out_shape=out_shapes,
                      compiler_params=_params(len(grid)))(*operands)
    gs = pltpu.PrefetchScalarGridSpec(num_scalar_prefetch=1, grid=grid, in_specs=in_specs, out_specs=out_specs)
    return _pcall(body, name=name, grid_spec=gs, out_shape=out_shapes,
                  compiler_params=_params(len(grid)))(grid_spec_prefetch, *operands)


def _row_spec(tm, w):
    return pl.BlockSpec((tm, w), lambda i: (i, 0))


def _const_spec(shape):
    nd = len(shape)
    return pl.BlockSpec(tuple(shape), lambda *_: (0,) * nd)


def _rms_fwd(x, g):
    r = lax.rsqrt(jnp.mean(x * x, axis=-1, keepdims=True) + NORM_EPS)
    return x * r * g


def _rms_bwd(x, g, dy):
    r = lax.rsqrt(jnp.mean(x * x, axis=-1, keepdims=True) + NORM_EPS)
    xh = x * r
    u = dy * g
    dx = r * (u - xh * jnp.mean(u * xh, axis=-1, keepdims=True))
    dg = jnp.sum(dy * xh, axis=0, keepdims=True)
    return dx, dg


def _gelu_and_grad(z):
    cdf = 0.5 * (1.0 + lax.erf(z * (2.0 ** -0.5)))
    return cdf + z * jnp.exp(-0.5 * z * z) * ((2.0 * math.pi) ** -0.5), z * cdf


def _rope_fwd(x, cc, sa, sb):
    return x * cc + pltpu.roll(x, 96, 1) * sa + pltpu.roll(x, 32, 1) * sb


def _rope_bwd(d, cc, sa, sb):
    return d * cc + pltpu.roll(d * sa, 32, 1) + pltpu.roll(d * sb, 96, 1)


def _adam(w, g, m, v):
    m = ADAM_B1 * m + (1.0 - ADAM_B1) * g
    v = ADAM_B2 * v + (1.0 - ADAM_B2) * (g * g)
    m_hat = m / (1.0 - ADAM_B1 ** ADAM_STEP)
    v_hat = v / (1.0 - ADAM_B2 ** ADAM_STEP)
    delta = -ADAM_LR * (m_hat / (jnp.sqrt(v_hat) + ADAM_EPS) + ADAM_WD * w)
    return delta, m, v


def _flash_fwd(q, k, vt, tq, comm=None):
    h, t = vt.shape[0], q.shape[0]
    nq = t // tq

    chunk_blocks = [c for c in (4, 2) if c < nq]

    def body(q_ref, k_ref, vt_ref, o_ref, lse_ref, m_ref, l_ref, acc_ref):
        qi = pl.program_id(1)
        m_ref[...] = jnp.full((1, tq), NEG, F32)
        l_ref[...] = jnp.zeros((1, tq), F32)
        acc_ref[...] = jnp.zeros((VDIM, tq), F32)

        def update(kb0, nblk, masked):
            kb = k_ref[pl.ds(pl.multiple_of(kb0 * tq, tq), nblk * tq), :]
            st = lax.dot_general(kb, q_ref[...], NT, preferred_element_type=F32)
            if masked:
                key = lax.broadcasted_iota(jnp.int32, (nblk * tq, tq), 0) - (nblk - 1) * tq
                qry = lax.broadcasted_iota(jnp.int32, (nblk * tq, tq), 1)
                st = jnp.where(key <= qry, st, NEG)
            m_old = m_ref[...]
            m_new = jnp.maximum(m_old, jnp.max(st, axis=0, keepdims=True))
            alpha = jnp.exp2((m_old - m_new) * EXP2_SCALE)
            pt = jnp.exp2((st - m_new) * EXP2_SCALE)
            l_ref[...] = alpha * l_ref[...] + jnp.sum(pt, axis=0, keepdims=True)
            ptb = pt.astype(BF16)
            pv = lax.dot_general(vt_ref[kb0], ptb[:tq], NN, preferred_element_type=F32)
            for j in range(1, nblk):
                pv += lax.dot_general(vt_ref[kb0 + j], ptb[j * tq:(j + 1) * tq], NN, preferred_element_type=F32)
            acc_ref[...] = alpha * acc_ref[...] + pv
            m_ref[...] = m_new

        start = jnp.int32(0)
        for c in chunk_blocks:
            take = (qi & c) != 0

            @pl.when(take)
            def _(start=start, c=c):
                update(start, c, False)

            start = start + jnp.where(take, c, 0)
        if nq > 1:
            @pl.when((qi & 1) != 0)
            def _():
                update(qi - 1, 2, True)

            @pl.when((qi & 1) == 0)
            def _():
                update(qi, 1, True)
        else:
            update(qi, 1, True)
        l = l_ref[...]
        o_ref[...] = (acc_ref[...] / l).T.astype(o_ref.dtype)
        lse_ref[...] = m_ref[...] * EXP2_SCALE + jnp.log2(l)

    return _call(
        "flash_fwd", body, [q, k, vt],
        [pl.BlockSpec((tq, QPAD), lambda hh, i: (i, hh)),
         pl.BlockSpec((t, QPAD), lambda hh, i: (0, hh)),
         pl.BlockSpec((None, nq, VDIM, tq), lambda hh, i: (hh, 0, 0, 0))],
        [_sds((t, h * VDIM), BF16), _sds((h, nq, 1, tq), F32)],
        [pl.BlockSpec((tq, VDIM), lambda hh, i: (i, hh)),
         pl.BlockSpec((None, None, 1, tq), lambda hh, i: (hh, i, 0, 0))],
        [pltpu.VMEM((1, tq), F32), pltpu.VMEM((1, tq), F32), pltpu.VMEM((VDIM, tq), F32)], (h, nq), comm)


def _flash_bwd(q, k, v, o, do, lse, tabs, tq, comm=None):
    t = q.shape[0]
    h = q.shape[1] // QPAD
    nq = t // tq

    def body(q_ref, k_ref, v_ref, o_ref, do_ref, lse_ref, cc_ref, sa_ref, sb_ref, dq_ref, dk_out, dv_out, delta_ref, dqt_ref, dk_ref, dv_ref):
        kj = pl.program_id(1)

        @pl.when(kj == 0)
        def _():
            dqt_ref[...] = jnp.zeros_like(dqt_ref)
            ones = jnp.ones((8, VDIM), BF16)
            for qi in range(nq):
                rows = pl.ds(qi * tq, tq)
                prod = do_ref[rows, :].astype(F32) * o_ref[rows, :].astype(F32)
                hi = prod.astype(BF16)
                lo = (prod - hi.astype(F32)).astype(BF16)
                delta_ref[qi] = (lax.dot_general(ones, hi, NT, preferred_element_type=F32)
                                 + lax.dot_general(ones, lo, NT, preferred_element_type=F32))

        kb = k_ref[...]
        vb = v_ref[...]
        kbt = kb.astype(F32).T.astype(BF16)
        dk_ref[...] = jnp.zeros_like(dk_ref)
        dv_ref[...] = jnp.zeros_like(dv_ref)

        def step(q0, nblk, masked):
            rows = pl.ds(pl.multiple_of(q0 * tq, tq), nblk * tq)
            qb = q_ref[rows, :]
            dob = do_ref[rows, :]
            lse = jnp.concatenate([lse_ref[q0 + j] for j in range(nblk)], axis=1)
            delta = jnp.concatenate([delta_ref[q0 + j, pl.ds(0, 1), :] for j in range(nblk)], axis=1)
            st = lax.dot_general(kb, qb, NT, preferred_element_type=F32)
            pt = jnp.exp2(st * EXP2_SCALE - lse)
            if masked:
                key = lax.broadcasted_iota(jnp.int32, (tq, nblk * tq), 0)
                qry = lax.broadcasted_iota(jnp.int32, (tq, nblk * tq), 1)
                pt = jnp.where(key <= qry, pt, 0.0)
            dv_ref[...] += lax.dot_general(pt.astype(BF16), dob, NN, preferred_element_type=F32)
            dpt = lax.dot_general(vb, dob, NT, preferred_element_type=F32)
            dst = (pt * (dpt - delta) * ATTN_SCALE).astype(BF16)
            dk_ref[...] += lax.dot_general(dst, qb, NN, preferred_element_type=F32)
            dqt = lax.dot_general(kbt, dst, NN, preferred_element_type=F32)
            for j in range(nblk):
                dqt_ref[q0 + j] += dqt[:, j * tq:(j + 1) * tq]

        later = nq - 1 - kj
        if nq > 1:
            @pl.when((later & 1) != 0)
            def _():
                step(kj, 2, True)

            @pl.when((later & 1) == 0)
            def _():
                step(kj, 1, True)
        else:
            step(kj, 1, True)
        start = kj + 1 + (later & 1)
        for c in [c for c in (2, 4) if c < nq]:
            take = (later & c) != 0

            @pl.when(take)
            def _(start=start, c=c):
                step(start, c, False)

            start = start + jnp.where(take, c, 0)
        dk_out[...] = dk_ref[...].astype(BF16)
        dv_out[...] = dv_ref[...].astype(BF16)

        @pl.when(kj == nq - 1)
        def _():
            for qi in range(nq):
                rows = pl.ds(qi * tq, tq)
                d = dqt_ref[qi].T
                roped = _rope_bwd(d[:, NOPE:], cc_ref[rows, :], sa_ref[rows, :], sb_ref[rows, :])
                dq_ref[rows, :] = jnp.concatenate([d[:, :NOPE], roped], axis=1).astype(BF16)

    head_q = pl.BlockSpec((t, QPAD), lambda hh, j: (0, hh))
    head_v = pl.BlockSpec((t, VDIM), lambda hh, j: (0, hh))
    table = pl.BlockSpec((t, 128), lambda hh, j: (0, 0))
    return _call(
        "flash_bwd", body, [q, k, v, o, do, lse, *tabs],
        [head_q, pl.BlockSpec((tq, QPAD), lambda hh, j: (j, hh)), pl.BlockSpec((tq, VDIM), lambda hh, j: (j, hh)), head_v, head_v,
         pl.BlockSpec((None, nq, 1, tq), lambda hh, j: (hh, 0, 0, 0)), table, table, table],
        [_sds((t, h * QPAD), BF16), _sds((t, h * QPAD), BF16), _sds((t, h * VDIM), BF16)],
        [head_q, pl.BlockSpec((tq, QPAD), lambda hh, j: (j, hh)), pl.BlockSpec((tq, VDIM), lambda hh, j: (j, hh))],
        [pltpu.VMEM((nq, 8, tq), F32), pltpu.VMEM((nq, QPAD, tq), F32), pltpu.VMEM((tq, QPAD), F32), pltpu.VMEM((tq, VDIM), F32)], (h, nq), comm)


def _tril_bf16(w):
    row = lax.broadcasted_iota(jnp.int32, w.shape, 0)
    col = lax.broadcasted_iota(jnp.int32, w.shape, 1)
    return jnp.where(col <= row, w, 0.0).astype(BF16)


def _layer_norm_parts(v0):
    mu = jnp.mean(v0, axis=-1, keepdims=True)
    vc = v0 - mu
    rstd = lax.rsqrt(jnp.mean(vc * vc, axis=-1, keepdims=True) + LN_EPS)
    return vc * rstd, rstd


def _sgu_mid_fwd(ge, ln_g, ln_b, w_sp, b_sp, chunks_per_step):
    t, e2 = ge.shape
    e = e2 // 2
    gd = e // SGU_GROUPS
    rows = SGU_CHUNK * chunks_per_step

    def body(u_ref, v_ref, g_ref, b_ref, w_ref, bs_ref, gate_ref):
        for ck in range(chunks_per_step):
            r = pl.ds(ck * SGU_CHUNK, SGU_CHUNK)
            xh, _ = _layer_norm_parts(v_ref[r, :].astype(F32))
            v1 = (xh * g_ref[...] + b_ref[...]).astype(BF16)
            for g in range(SGU_GROUPS):
                cols = pl.ds(g * gd, gd)
                mixed = lax.dot_general(_tril_bf16(w_ref[g]), v1[:, g * gd:(g + 1) * gd], NN, preferred_element_type=F32) + bs_ref[g]
                gate_ref[r, cols] = (u_ref[r, cols].astype(F32) * mixed).astype(BF16)

    return _pcall(
        body, name="sgu_mid_fwd", grid=(t // rows,),
        in_specs=[pl.BlockSpec((rows, e), lambda i: (i, 0)), pl.BlockSpec((rows, e), lambda i: (i, 1)),
                  _const_spec((1, e)), _const_spec((1, e)), _const_spec(w_sp.shape), _const_spec(b_sp.shape)],
        out_specs=pl.BlockSpec((rows, e), lambda i: (i, 0)),
        out_shape=_sds((t, e), BF16), compiler_params=_params(1),
    )(ge, ge, ln_g, ln_b, w_sp, b_sp)


def _sgu_mid_bwd(ge, gp, dgate, ln_g, ln_b, w_sp, b_sp, chunks_per_step):
    t, e2 = ge.shape
    e = e2 // 2
    gd = e // SGU_GROUPS
    rows = SGU_CHUNK * chunks_per_step

    def body(u_ref, v_ref, zu_ref, zv_ref, dg_ref, g_ref, b_ref, w_ref, bs_ref, dz_ref, dw_ref, dbs_ref, dlg_ref, dlb_ref):
        @pl.when(pl.program_id(0) == 0)
        def _():
            dw_ref[...] = jnp.zeros_like(dw_ref)
            dbs_ref[...] = jnp.zeros_like(dbs_ref)
            dlg_ref[...] = jnp.zeros_like(dlg_ref)
            dlb_ref[...] = jnp.zeros_like(dlb_ref)

        for ck in range(chunks_per_step):
            r = pl.ds(ck * SGU_CHUNK, SGU_CHUNK)
            xh, rstd = _layer_norm_parts(v_ref[r, :].astype(F32))
            v1 = (xh * g_ref[...] + b_ref[...]).astype(BF16)
            dv1_parts = []
            for g in range(SGU_GROUPS):
                cols = pl.ds(g * gd, gd)
                wc = _tril_bf16(w_ref[g])
                v1g = v1[:, g * gd:(g + 1) * gd]
                mixed = lax.dot_general(wc, v1g, NN, preferred_element_type=F32) + bs_ref[g]
                dgate = dg_ref[r, cols].astype(F32)
                dmixed = dgate * u_ref[r, cols].astype(F32)
                du = dgate * mixed
                dz_ref[r, cols] = (du * zu_ref[r, cols].astype(F32)).astype(BF16)
                dbs_ref[g] += jnp.sum(dmixed, axis=1, keepdims=True)
                dmb = dmixed.astype(BF16)
                dwg = lax.dot_general(dmb, v1g, NT, preferred_element_type=F32)
                row = lax.broadcasted_iota(jnp.int32, dwg.shape, 0)
                col = lax.broadcasted_iota(jnp.int32, dwg.shape, 1)
                dw_ref[g] += jnp.where(col <= row, dwg, 0.0)
                dv1_parts.append(lax.dot_general(wc, dmb, TN, preferred_element_type=F32))
            dv1 = jnp.concatenate(dv1_parts, axis=1)
            dlg_ref[...] += jnp.sum(dv1 * xh, axis=0, keepdims=True)
            dlb_ref[...] += jnp.sum(dv1, axis=0, keepdims=True)
            dxh = dv1 * g_ref[...]
            dv0 = rstd * (dxh - jnp.mean(dxh, axis=-1, keepdims=True) - xh * jnp.mean(dxh * xh, axis=-1, keepdims=True))
            dz_ref[r, pl.ds(e, e)] = (dv0 * zv_ref[r, :].astype(F32)).astype(BF16)

    half0 = pl.BlockSpec((rows, e), lambda i: (i, 0))
    half1 = pl.BlockSpec((rows, e), lambda i: (i, 1))
    return _pcall(
        body, name="sgu_mid_bwd", grid=(t // rows,),
        in_specs=[half0, half1, half0, half1, half0, _const_spec((1, e)), _const_spec((1, e)), _const_spec(w_sp.shape), _const_spec(b_sp.shape)],
        out_specs=[pl.BlockSpec((rows, e2), lambda i: (i, 0)), _const_spec(w_sp.shape), _const_spec(b_sp.shape), _const_spec((1, e)), _const_spec((1, e))],
        out_shape=[_sds((t, e2), BF16), _sds(w_sp.shape, F32), _sds(b_sp.shape, F32), _sds((1, e), F32), _sds((1, e), F32)],
        compiler_params=_params(1),
    )(ge, ge, gp, gp, dgate, ln_g, ln_b, w_sp, b_sp)


def kernel(x, positions, norm_mix, norm_ffn, final_norm, mla_w_dkv, mla_q_norm, mla_kv_norm, mla_w_uq, mla_w_ukv, mla_w_o, sgu_w_in, sgu_ln_g, sgu_ln_b, sgu_w_spatial, sgu_b_spatial, sgu_w_out, ffn_w_up, ffn_w_down, loss_target, m_norm_mix, m_norm_ffn, m_final_norm, m_mla_w_dkv, m_mla_q_norm, m_mla_kv_norm, m_mla_w_uq, m_mla_w_ukv, m_mla_w_o, m_sgu_w_in, m_sgu_ln_g, m_sgu_ln_b, m_sgu_w_spatial, m_sgu_b_spatial, m_sgu_w_out, m_ffn_w_up, m_ffn_w_down, v_norm_mix, v_norm_ffn, v_final_norm, v_mla_w_dkv, v_mla_q_norm, v_mla_kv_norm, v_mla_w_uq, v_mla_w_ukv, v_mla_w_o, v_sgu_w_in, v_sgu_ln_g, v_sgu_ln_b, v_sgu_w_spatial, v_sgu_b_spatial, v_sgu_w_out, v_ffn_w_up, v_ffn_w_down):
    _, T, D = x.shape
    depth = norm_mix.shape[0]
    n_mla, n_sgu = mla_w_dkv.shape[0], sgu_w_in.shape[0]
    assert depth % 2 == 0
    FF = ffn_w_up.shape[2] * N_DEV
    E = sgu_w_out.shape[1] * N_DEV
    ffc, ec, e2c = FF // N_DEV, E // N_DEV, 2 * E // N_DEV
    dc = D // N_DEV
    OW = HEADS * VDIM
    HW = HEADS * QPAD
    owc = OW // N_DEV
    tm = _tile(T, 1024)
    tb = _tile(T, 4096)
    tk = _tile(T, 512)
    tq = _tile(T, 512)
    ts = _tile(T, 256)
    nt = T // tm
    x2 = x.reshape(T, D)
    tgt = loss_target.reshape(T, D)
    cidx = lax.axis_index("c").astype(jnp.int32).reshape(1)

    ln_local = jnp.concatenate([sgu_ln_g, sgu_ln_b, jnp.zeros((8 - 2 * n_sgu, ec), F32)], axis=0)
    mla_sh = [[w[l].astype(BF16) for w in (mla_w_dkv, mla_w_uq, mla_w_ukv, mla_w_o)] for l in range(n_mla)]

    def mla_layouts(g_dkv, g_uq, g_ukv, g_o):
        w_dkv = jnp.pad(g_dkv.reshape(1, D, LAT), ((0, 0), (0, 0), (0, LAT_PAD - LAT)))
        w_uq = jnp.pad(g_uq, ((0, 0), (0, 0), (0, QPAD - NOPE - ROPE))).transpose(1, 0, 2).reshape(1, Q_RANK, HEADS * QPAD)
        w_ukv = g_ukv.transpose(1, 0, 2).reshape(1, KV_RANK, HEADS * (NOPE + VDIM))
        return w_dkv, w_uq, w_ukv, g_o.reshape(1, HEADS * VDIM, D)

    mla_w = [None] * n_mla
    small_later = [a for l in range(1, n_mla) for a in mla_sh[l]] + [ln_local]
    ln_g_full, ln_b_full = [None] * n_sgu, [None] * n_sgu
    b_sp = sgu_b_spatial.reshape(n_sgu, SGU_GROUPS, SGU_CHUNK, 1)
    up_sh = [ffn_w_up[i].astype(BF16) for i in range(depth)]
    down_sh = [ffn_w_down[i].astype(BF16) for i in range(depth)]
    in_sh = [sgu_w_in[l].astype(BF16) for l in range(n_sgu)]
    out_sh = [sgu_w_out[l].astype(BF16) for l in range(n_sgu)]
    g_up, g_down, g_in, g_out = [None] * depth, [None] * depth, [None] * n_sgu, [None] * n_sgu

    inv_freq = ROPE_THETA ** (-jnp.arange(0, ROPE, 2, dtype=F32) / ROPE)
    zeros32 = jnp.zeros((ROPE // 2,), F32)
    inv128 = jnp.concatenate([inv_freq, inv_freq, zeros32, zeros32]).reshape(1, 128)
    sel_a = jnp.concatenate([-jnp.ones((32,), F32), zeros32, zeros32, zeros32]).reshape(1, 128)
    sel_b = jnp.concatenate([zeros32, jnp.ones((32,), F32), zeros32, zeros32]).reshape(1, 128)
    sel_c = jnp.concatenate([jnp.ones((64,), F32), zeros32, zeros32]).reshape(1, 128)

    def rope_tables(pos, inv, sa, sb, sc):
        ang = pos.astype(F32) * inv
        cs, sn = jnp.cos(ang), jnp.sin(ang)
        return cs * sc, sn * sa, sn * sb

    t_cc, t_sa, t_sb, *first_half = _rowwise(
        "rope_tables", rope_tables, [positions.reshape(T, 1), inv128, sel_a, sel_b, sel_c], grid=(nt,),
        in_specs=[_row_spec(tm, 1)] + [_const_spec((1, 128))] * 4,
        out_shapes=[_sds((T, 128), F32)] * 3, out_specs=[_row_spec(tm, 128)] * 3, comm=_gather_level1(mla_sh[0]))
    tab_specs = [_row_spec(tm, 128)] * 3

    def rmsnorm(xv, g, comm):
        return _rowwise("rmsnorm", lambda a, gg: _rms_fwd(a, gg), [xv, g.reshape(1, D)], grid=(nt,),
                        in_specs=[_row_spec(tm, D), _const_spec((1, D))], out_shapes=[_sds((T, D), BF16)], out_specs=[_row_spec(tm, D)], comm=comm)

    def proj_cols(name, h, gw, nc, epilogue, n_out, comm=None):
        return _matmul(name, h, gw, [], grid=(N_DEV, T // tb),
                       a_spec=pl.BlockSpec((tb, D), lambda j, i: (i, 0)),
                       b_spec=pl.BlockSpec((None, D, nc), lambda j, i: (j, 0, 0)), extra_specs=[],
                       out_shapes=[_sds((T, nc * N_DEV), BF16)] * n_out, out_specs=[pl.BlockSpec((tb, nc), lambda j, i: (i, j))] * n_out,
                       dims=NN, epilogue=epilogue, comm=comm)

    def residual_norm(acc, xr, g):
        xn = acc + xr
        return xn, _rms_fwd(xn, g)

    def proj_rows_residual(name, a, gw, xres, g_next, comm=None):
        kk_ = a.shape[1]
        return _matmul(name, a, gw.reshape(kk_, D), [xres, g_next.reshape(1, D)], grid=(T // tk,),
                       a_spec=_row_spec(tk, kk_), b_spec=_const_spec((kk_, D)), extra_specs=[_row_spec(tk, D), _const_spec((1, D))],
                       out_shapes=[_sds((T, D), F32), _sds((T, D), BF16)], out_specs=[_row_spec(tk, D)] * 2,
                       dims=NN, epilogue=residual_norm, comm=comm)

    def back_rows(name, dy, gw, kc, extras, epilogue, comm=None):
        return _matmul(name, dy, gw, extras, grid=(N_DEV, T // tb),
                       a_spec=pl.BlockSpec((tb, D), lambda j, i: (i, 0)),
                       b_spec=pl.BlockSpec((None, kc, D), lambda j, i: (j, 0, 0)),
                       extra_specs=[pl.BlockSpec((tb, kc), lambda j, i: (i, j))] * len(extras),
                       out_shapes=[_sds((T, kc * N_DEV), BF16)], out_specs=[pl.BlockSpec((tb, kc), lambda j, i: (i, j))],
                       dims=NT, epilogue=epilogue, comm=comm)

    def norm_bwd_epilogue(dh, xv, g, dxi):
        dxn, dg = _rms_bwd(xv, g, dh)
        return dxi + dxn, dxi + dxn, dg

    def transposed(gw):
        return gw.transpose(0, 2, 1).reshape(gw.shape[0] * gw.shape[2], D)

    def back_cols(name, da, gwt, xv, g, dx_in, comm=None):
        n = da.shape[1]
        row = _row_spec(tk, D)
        return _matmul(name, da, gwt, [xv, g.reshape(1, D), dx_in], grid=(T // tk,),
                       a_spec=_row_spec(tk, n), b_spec=_const_spec((n, D)), extra_specs=[row, _const_spec((1, D)), row],
                       out_shapes=[_sds((T, D), F32), _sds((T, D), BF16), _sds((1, D), F32)], out_specs=[row, row, _const_spec((1, D))],
                       dims=NN, epilogue=norm_bwd_epilogue, n_sum=1, comm=comm)

    def token_sum(tt):
        return dict(k_axis=1, nk=T // tt) if T // tt > 1 else dict(k_axis=None)

    def wgrad_cols(name, h, da, nc):
        return _matmul(name, h, da, [], grid=(N_DEV, T // tb),
                       a_spec=pl.BlockSpec((tb, D), lambda j, t: (t, 0)), b_spec=pl.BlockSpec((tb, nc), lambda j, t: (t, j)),
                       extra_specs=[], out_shapes=[_sds((N_DEV, D, nc), BF16)],
                       out_specs=[pl.BlockSpec((None, D, nc), lambda j, t: (j, 0, 0))],
                       dims=TN, acc_shape=(D, nc), **token_sum(tb))[0]

    def wgrad_rows(name, a, dy, kc, ncols, tt, comm=None):
        res = _matmul(name, a, dy, [], grid=(a.shape[1] // kc, T // tt),
                      a_spec=pl.BlockSpec((tt, kc), lambda j, t: (t, j)), b_spec=pl.BlockSpec((tt, ncols), lambda j, t: (t, 0)),
                      extra_specs=[], out_shapes=[_sds((a.shape[1], ncols), BF16)],
                      out_specs=[pl.BlockSpec((kc, ncols), lambda j, t: (j, 0))],
                      dims=TN, acc_shape=(kc, ncols), comm=comm, **token_sum(tt))
        return res[0] if comm is None else res

    saved = []
    xs = x2
    for i in range(depth):
        l = i // 2
        if i == 0:
            h, *first_w = rmsnorm(xs, norm_mix[0], _gather_level2(first_half))
            mla_w[0] = mla_layouts(*first_w)
        if i % 2 == 0:
            w_dkv, w_uq, w_ukv, w_o = mla_w[l]
            lat = _matmul("mla_down", h, w_dkv, [], grid=(nt,), a_spec=_row_spec(tm, D),
                          b_spec=pl.BlockSpec((None, D, LAT_PAD), lambda i_: (0, 0, 0)), extra_specs=[],
                          out_shapes=[_sds((T, LAT_PAD), F32)], out_specs=[_row_spec(tm, LAT_PAD)], dims=NN)[0]

            def latent_post(la, qn, kvn, cc, sa, sb):
                cq = _rms_fwd(la[:, :Q_RANK], qn)
                ckv = _rms_fwd(la[:, Q_RANK:Q_RANK + KV_RANK], kvn)
                kr = _rope_fwd(la[:, Q_RANK + KV_RANK:], cc, sa, sb)
                return cq, ckv, kr

            cq, ckv, kr = _rowwise(
                "mla_latent", latent_post, [lat, mla_q_norm[l].reshape(1, Q_RANK), mla_kv_norm[l].reshape(1, KV_RANK), t_cc, t_sa, t_sb],
                grid=(nt,), in_specs=[_row_spec(tm, LAT_PAD), _const_spec((1, Q_RANK)), _const_spec((1, KV_RANK))] + tab_specs,
                out_shapes=[_sds((T, Q_RANK), BF16), _sds((T, KV_RANK), BF16), _sds((T, 128), BF16)],
                out_specs=[_row_spec(tm, Q_RANK), _row_spec(tm, KV_RANK), _row_spec(tm, 128)])

            def q_epilogue(acc, cc, sa, sb):
                parts = []
                for b in range(HEADS):
                    parts += [acc[:, b * QPAD:b * QPAD + NOPE], _rope_fwd(acc[:, b * QPAD + NOPE:(b + 1) * QPAD], cc, sa, sb)]
                return (jnp.concatenate(parts, axis=1),)

            q = _matmul("mla_q", cq, w_uq, [t_cc, t_sa, t_sb], grid=(nt,), a_spec=_row_spec(tm, Q_RANK),
                        b_spec=pl.BlockSpec((None, Q_RANK, HW), lambda i_: (0, 0, 0)), extra_specs=tab_specs,
                        out_shapes=[_sds((T, HW), BF16)], out_specs=[_row_spec(tm, HW)], dims=NN, epilogue=q_epilogue)[0]

            def kv_write(outs, acc, krb):
                k_ref, v_ref, vt_ref = outs
                for b in range(HEADS):
                    vb = acc[:, b * QPAD + NOPE:(b + 1) * QPAD]
                    k_ref[:, b * QPAD:b * QPAD + NOPE] = acc[:, b * QPAD:b * QPAD + NOPE].astype(BF16)
                    k_ref[:, b * QPAD + NOPE:(b + 1) * QPAD] = krb
                    v_ref[:, b * VDIM:(b + 1) * VDIM] = vb.astype(BF16)
                    vbt = vb.T.astype(BF16)
                    for u in range(tm // tq):
                        vt_ref[b, u] = vbt[:, u * tq:(u + 1) * tq]

            kk, vv, vt = _matmul("mla_kv", ckv, w_ukv, [kr], grid=(nt,), a_spec=_row_spec(tm, KV_RANK),
                                 b_spec=pl.BlockSpec((None, KV_RANK, HW), lambda i_: (0, 0, 0)), extra_specs=[_row_spec(tm, 128)],
                                 out_shapes=[_sds((T, HW), BF16), _sds((T, OW), BF16), _sds((HEADS, T // tq, VDIM, tq), BF16)],
                                 out_specs=[_row_spec(tm, HW), _row_spec(tm, OW), pl.BlockSpec((HEADS, tm // tq, VDIM, tq), lambda i_: (0, i_, 0, 0))],
                                 dims=NN, write=kv_write)
            group = [up_sh[i], down_sh[i], in_sh[l], out_sh[l]] + (small_later if i == 0 else [])
            o, lse, *bufs = _flash_fwd(q, kk, vt, tq, comm=_gather_level1(group))
            xm, h2, g_up[i], g_down[i] = _matmul(
                "mla_out", o, w_o, [xs, norm_ffn[i].reshape(1, D)], grid=(nt,), a_spec=_row_spec(tm, OW),
                b_spec=pl.BlockSpec((None, OW, D), lambda i_: (0, 0, 0)), extra_specs=[_row_spec(tm, D), _const_spec((1, D))],
                out_shapes=[_sds((T, D), F32), _sds((T, D), BF16)], out_specs=[_row_spec(tm, D)] * 2, dims=NN,
                epilogue=residual_norm, comm=_gather_level2(bufs[:2]))
            half_gathered = bufs[2:]
            mix_saved = (h, lat, cq, ckv, q, kk, vv, o, lse)
        else:
            gp, ge, g_down[i], up_half = proj_cols("sgu_in", h, g_in[l], e2c, _gelu_and_grad, 2,
                                                   comm=_merge_comm(_gather_level2([down_half]), _gather_level1([up_sh[i]])))
            gate = _sgu_mid_fwd(ge, ln_g_full[l], ln_b_full[l], sgu_w_spatial[l], b_sp[l], 4)
            xm, h2, g_up[i] = proj_rows_residual("sgu_out", gate, g_out[l], xs, norm_ffn[i], comm=_gather_level2([up_half]))
            mix_saved = (h, gp, ge, gate)
        r, s, *rest = proj_cols("ffn_up", h2, g_up[i], ffc, lambda acc: (jnp.maximum(acc, 0.0), jnp.square(jnp.maximum(acc, 0.0))), 2,
                                comm=_merge_comm(_gather_level2(half_gathered), _gather_level1([down_sh[i + 1]], part=(0, 2))) if i % 2 == 0 else None)
        if i % 2 == 0:
            g_in[l], g_out[l], *small_gathered, down_part = rest
        if i == 0:
            for l_ in range(1, n_mla):
                mla_w[l_] = mla_layouts(*small_gathered[4 * (l_ - 1):4 * l_])
            g_ln = small_gathered[-1]
            ln_g_full = [g_ln[:, l_, :].reshape(1, E) for l_ in range(n_sgu)]
            ln_b_full = [g_ln[:, n_sgu + l_, :].reshape(1, E) for l_ in range(n_sgu)]
        xo, h_next, *rest = proj_rows_residual("ffn_down", s, g_down[i], xm, norm_mix[i + 1] if i + 1 < depth else final_norm,
                                               comm=_gather_level1([down_sh[i + 1]], part=(1, 2), into=[down_part]) if i % 2 == 0 else None)
        if i % 2 == 0:
            (down_half,) = rest
        saved.append((xs, xm, mix_saved, h2, r, s))
        xs, h = xo, h_next

    def loss_head(xv, tg, g):
        y = _rms_fwd(xv, g)
        err = y - tg
        part = 0.5 * jnp.sum(jnp.sum(err * err, axis=-1, keepdims=True), axis=0, keepdims=True) / D
        dx, dg = _rms_bwd(xv, g, err / D)
        return dx, dx, jnp.broadcast_to(part, (1, 128)), dg

    dx, dyb, loss_part, d_final = _rowwise(
        "loss_head", loss_head, [xs, tgt, final_norm.reshape(1, D)], grid=(nt,),
        in_specs=[_row_spec(tm, D), _row_spec(tm, D), _const_spec((1, D))],
        out_shapes=[_sds((T, D), F32), _sds((T, D), BF16), _sds((1, 128), F32), _sds((1, D), F32)],
        out_specs=[_row_spec(tm, D), _row_spec(tm, D), _const_spec((1, 128)), _const_spec((1, D))], n_acc=2)
    loss = lax.psum(loss_part[0, 0], ("x", "y", "c"))

    d_norm_mix, d_norm_ffn = [None] * depth, [None] * depth
    d_qn, d_kvn = [None] * n_mla, [None] * n_mla
    d_wsp, d_bsp, d_lng, d_lnb = [None] * n_sgu, [None] * n_sgu, [None] * n_sgu, [None] * n_sgu
    layers = {"dkv": n_mla, "uq": n_mla, "ukv": n_mla, "o": n_mla, "in": n_sgu, "out": n_sgu, "up": depth, "down": depth}
    stacked = {nm: None for nm in layers}
    pending = []
    summed = []

    def add_pairs(gs, rcvs):
        operands, in_specs, out_shapes, out_specs = [], [], [], []
        for g, rcv in zip(gs, rcvs):
            _, rws, cls = g.shape
            slab = pl.BlockSpec((None, rws, cls), lambda ch, cr: (ch, 0, 0))
            operands += [g.reshape(N_CHIP, 2, rws, cls), rcv]
            in_specs += [pl.BlockSpec((None, None, rws, cls), lambda ch, cr: (ch, cr[0], 0, 0)), slab]
            out_shapes.append(_sds(rcv.shape, BF16))
            out_specs.append(slab)

        def fn(*blocks):
            return tuple(blocks[2 * k].astype(F32) + blocks[2 * k + 1].astype(F32) for k in range(len(gs)))

        return _rowwise("grad_pair_sum", fn, operands, grid=(N_CHIP,), in_specs=in_specs, out_shapes=out_shapes, out_specs=out_specs,
                        grid_spec_prefetch=cidx)

    def sibling_comm():
        return _sibling_exchange([g for _, _, g in pending]) if pending else None

    def absorb(from_sibling):
        if pending:
            parts = add_pairs([g for _, _, g in pending], list(from_sibling))
            summed.extend((nm, l_, p) for (nm, l_, _), p in zip(pending, parts))
            pending.clear()

    def chip_comm():
        if pending:
            absorb(_comm_call("grad_sibling_exchange", sibling_comm()))
        comm, names = _chip_exchange([p for _, _, p in summed], [(nm, l_) for nm, l_, _ in summed], layers, stacked)
        summed.clear()
        return comm, names

    def rows128(a, rows):
        flat = a.reshape(-1, 128)
        return jnp.pad(flat, ((0, rows - flat.shape[0]), (0, 0)))

    def pad_to(n, mult):
        return -(-n // mult) * mult

    def packed(arrs, sizes):
        return jnp.concatenate([rows128(a, sz) for a, sz in zip(arrs, sizes)], axis=0)

    n_wsp, n_bsp, n_ln = sgu_w_spatial.size // 128, pad_to(sgu_b_spatial.size // 128, 8), pad_to(n_sgu * E // 128, 8)
    early_sizes = [n_wsp, pad_to(n_wsp + n_bsp, SMALL_ROWS) - n_wsp, n_ln, n_ln]
    early_rep = early_sizes[0] + early_sizes[1]
    gathered_early = None

    for i in reversed(range(depth)):
        l = i // 2
        xs_i, xm, mix_saved, h2, r, s = saved[i]
        comm = sibling_comm()
        if i == 0:
            comm = _gather_level2([early_half]) if comm is None else _merge_comm(comm, _gather_level2([early_half]))
        da, *rcv = back_rows("ffn_down_bwd", dyb, g_down[i], ffc, [r], lambda acc, rr: (acc * (2.0 * rr.astype(F32)),), comm=comm)
        if i == 0:
            *rcv, gathered_early = rcv
        absorb(rcv)
        pending.append(("down", i, wgrad_rows("ffn_down_wgrad", s, dyb, ffc, D, tb).reshape(N_DEV, ffc, D)))
        pending.append(("up", i, wgrad_cols("ffn_up_wgrad", h2, da, ffc)))
        dx, dyb, d_norm_ffn[i], *rcv = back_cols("ffn_up_bwd", da, transposed(g_up[i]), xm, norm_ffn[i], dx, comm=sibling_comm())
        absorb(rcv)
        if i % 2 == 0:
            h, lat, cq, ckv, q, kk, vv, o, lse = mix_saved
            w_dkv, w_uq, w_ukv, w_o = mla_w[l]
            do = _matmul("mla_out_bwd", dyb, w_o, [], grid=(nt,), a_spec=_row_spec(tm, D),
                         b_spec=pl.BlockSpec((None, OW, D), lambda i_: (0, 0, 0)), extra_specs=[],
                         out_shapes=[_sds((T, OW), BF16)], out_specs=[_row_spec(tm, OW)], dims=NT)[0]
            g_o_l = wgrad_rows("mla_out_wgrad", o, dyb, OW, D, tm).reshape(N_DEV, owc, D)
            comm, names = chip_comm()
            dq_pre, dk, dv, *bufs = _flash_bwd(q, kk, vv, o, do, lse, (t_cc, t_sa, t_sb), tq, comm=comm)
            stacked.update(dict(zip(names, bufs)))
            pending.append(("o", l, g_o_l))

            def kv_pre(dkb, dvb, cc, sa, sb):
                parts, dkr = [], None
                for b in range(HEADS):
                    parts += [dkb[:, b * QPAD:b * QPAD + NOPE], dvb[:, b * VDIM:(b + 1) * VDIM]]
                    piece = dkb[:, b * QPAD + NOPE:(b + 1) * QPAD].astype(F32)
                    dkr = piece if dkr is None else dkr + piece
                return jnp.concatenate(parts, axis=1), _rope_bwd(dkr, cc, sa, sb)

            dkv, dkr = _rowwise("mla_dkv_rope", kv_pre, [dk, dv, t_cc, t_sa, t_sb], grid=(T // ts,),
                                in_specs=[_row_spec(ts, HW), _row_spec(ts, OW)] + [_row_spec(ts, 128)] * 3,
                                out_shapes=[_sds((T, HW), BF16), _sds((T, 128), F32)], out_specs=[_row_spec(ts, HW), _row_spec(ts, 128)])
            g_uq_l = wgrad_rows("mla_q_wgrad", cq, dq_pre, Q_RANK, HW, tm)
            g_ukv_l = wgrad_rows("mla_kv_wgrad", ckv, dkv, KV_RANK, HW, tm)
            pending.append(("uq", l, g_uq_l.reshape(Q_RANK, HEADS, QPAD)[:, :, :NOPE + ROPE].transpose(1, 0, 2)))
            pending.append(("ukv", l, g_ukv_l.reshape(KV_RANK, HEADS, NOPE + VDIM).transpose(1, 0, 2)))
            dcq = _matmul("mla_q_bwd", dq_pre, w_uq, [], grid=(nt,), a_spec=_row_spec(tm, HW),
                          b_spec=pl.BlockSpec((None, Q_RANK, HW), lambda i_: (0, 0, 0)), extra_specs=[],
                          out_shapes=[_sds((T, Q_RANK), F32)], out_specs=[_row_spec(tm, Q_RANK)], dims=NT)[0]
            dckv = _matmul("mla_kv_bwd", dkv, w_ukv, [], grid=(nt,), a_spec=_row_spec(tm, HW),
                           b_spec=pl.BlockSpec((None, KV_RANK, HW), lambda i_: (0, 0, 0)), extra_specs=[],
                           out_shapes=[_sds((T, KV_RANK), F32)], out_specs=[_row_spec(tm, KV_RANK)], dims=NT)[0]

            def latent_bwd(la, qn, kvn, dq_, dkv_, dkr_):
                dcq_raw, dqn = _rms_bwd(la[:, :Q_RANK], qn, dq_)
                dckv_raw, dkvn = _rms_bwd(la[:, Q_RANK:Q_RANK + KV_RANK], kvn, dkv_)
                return jnp.concatenate([dcq_raw, dckv_raw, dkr_], axis=1), dqn, dkvn

            dlat, d_qn[l], d_kvn[l] = _rowwise(
                "mla_latent_bwd", latent_bwd, [lat, mla_q_norm[l].reshape(1, Q_RANK), mla_kv_norm[l].reshape(1, KV_RANK), dcq, dckv, dkr],
                grid=(nt,), in_specs=[_row_spec(tm, LAT_PAD), _const_spec((1, Q_RANK)), _const_spec((1, KV_RANK)),
                                      _row_spec(tm, Q_RANK), _row_spec(tm, KV_RANK), _row_spec(tm, 128)],
                out_shapes=[_sds((T, LAT_PAD), BF16), _sds((1, Q_RANK), F32), _sds((1, KV_RANK), F32)],
                out_specs=[_row_spec(tm, LAT_PAD), _const_spec((1, Q_RANK)), _const_spec((1, KV_RANK))], n_acc=2)
            g_dkv_l = wgrad_rows("mla_down_wgrad", h, dlat, D, LAT_PAD, tm)
            pending.append(("dkv", l, g_dkv_l[:, :LAT].reshape(N_DEV, dc, LAT)))
            dx, dyb, d_norm_mix[i] = _matmul(
                "mla_down_bwd", dlat, w_dkv, [xs_i, norm_mix[i].reshape(1, D), dx], grid=(nt,), a_spec=_row_spec(tm, LAT_PAD),
                b_spec=pl.BlockSpec((None, D, LAT_PAD), lambda i_: (0, 0, 0)), extra_specs=[_row_spec(tm, D), _const_spec((1, D)), _row_spec(tm, D)],
                out_shapes=[_sds((T, D), F32), _sds((T, D), BF16), _sds((1, D), F32)],
                out_specs=[_row_spec(tm, D), _row_spec(tm, D), _const_spec((1, D))], dims=NT, epilogue=norm_bwd_epilogue, n_sum=1)
        else:
            h, gp, ge, gate = mix_saved
            (dgate,) = back_rows("sgu_out_bwd", dyb, g_out[l], ec, [], None)
            pending.append(("out", l, wgrad_rows("sgu_out_wgrad", gate, dyb, ec, D, tb).reshape(N_DEV, ec, D)))
            dz, d_wsp[l], d_bsp[l], d_lng[l], d_lnb[l] = _sgu_mid_bwd(ge, gp, dgate, ln_g_full[l], ln_b_full[l], sgu_w_spatial[l], b_sp[l], 2)
            pending.append(("in", l, wgrad_cols("sgu_in_wgrad", h, dz, e2c)))
            comm = sibling_comm()
            if i == 1:
                early = packed([jnp.stack(d_wsp, 0), jnp.stack(d_bsp, 0), jnp.concatenate(d_lng, 0), jnp.concatenate(d_lnb, 0)], early_sizes)
                comm = _merge_comm(comm, _gather_level1([early]))
            dx, dyb, d_norm_mix[i], *rcv = back_cols("sgu_in_bwd", dz, transposed(g_in[l]), xs_i, norm_mix[i], dx, comm=comm)
            if i == 1:
                *rcv, early_half = rcv
            absorb(rcv)
    grad_x = dx.reshape(1, T, D)

    last_comm, last_names = chip_comm()
    late_g = [jnp.concatenate(d_norm_mix, 0), jnp.concatenate(d_norm_ffn, 0), d_final, jnp.concatenate(d_qn, 0), jnp.concatenate(d_kvn, 0)]
    late_w = [norm_mix, norm_ffn, final_norm, mla_q_norm, mla_kv_norm]
    late_m = [m_norm_mix, m_norm_ffn, m_final_norm, m_mla_q_norm, m_mla_kv_norm]
    late_v = [v_norm_mix, v_norm_ffn, v_final_norm, v_mla_q_norm, v_mla_kv_norm]
    late_sizes = [pad_to(g.size // 128, 8) for g in late_g]
    late_rows = sum(late_sizes)

    def adam_big(parts, w, m, v):
        lyr, rws, cls = w.shape
        rt = _tile(rws, 512)

        def fn(p, w_, m_, v_):
            g = (p[0].astype(F32) + p[1].astype(F32)) + (p[2].astype(F32) + p[3].astype(F32))
            return (g, *_adam(w_, g, m_, v_))

        spec = pl.BlockSpec((None, rt, cls), lambda l_, i_: (l_, i_, 0))
        return _rowwise("adam_large", fn, [parts, w, m, v], grid=(lyr, rws // rt),
                        in_specs=[pl.BlockSpec((N_CHIP, None, rt, cls), lambda l_, i_: (0, l_, i_, 0)), spec, spec, spec],
                        out_shapes=[_sds(w.shape, F32)] * 4, out_specs=[spec] * 4)

    stacked.update(dict(zip(last_names, _comm_call("grad_chip_exchange", last_comm))))
    (gathered_late,) = _all_gather("gather_small_grads", [packed(late_g, late_sizes)])
    big = {}
    big["in"] = adam_big(stacked["in"], sgu_w_in, m_sgu_w_in, v_sgu_w_in)
    big["up"] = adam_big(stacked["up"], ffn_w_up, m_ffn_w_up, v_ffn_w_up)
    big["down"] = adam_big(stacked["down"], ffn_w_down, m_ffn_w_down, v_ffn_w_down)
    big["out"] = adam_big(stacked["out"], sgu_w_out, m_sgu_w_out, v_sgu_w_out)
    big["dkv"] = adam_big(stacked["dkv"], mla_w_dkv, m_mla_w_dkv, v_mla_w_dkv)
    big["uq"] = adam_big(stacked["uq"], mla_w_uq, m_mla_w_uq, v_mla_w_uq)
    big["ukv"] = adam_big(stacked["ukv"], mla_w_ukv, m_mla_w_ukv, v_mla_w_ukv)
    big["o"] = adam_big(stacked["o"], mla_w_o, m_mla_w_o, v_mla_w_o)
    big_res = [big[nm][:4] for nm in ("dkv", "uq", "ukv", "o", "in", "out", "up", "down")]

    def sum8(p):
        return ((p[0] + p[1]) + (p[2] + p[3])) + ((p[4] + p[5]) + (p[6] + p[7]))

    def adam_packed(name, gathered, ws, ms, vs, sizes, rows, tile):
        spec = _row_spec(tile, 128)
        return _rowwise(name, lambda p, w_, m_, v_: (sum8(p), *_adam(w_, sum8(p), m_, v_)),
                        [gathered, packed(ws, sizes), packed(ms, sizes), packed(vs, sizes)], grid=(rows // tile,),
                        in_specs=[pl.BlockSpec((N_DEV, tile, 128), lambda i_: (0, i_, 0)), spec, spec, spec],
                        out_shapes=[_sds((rows, 128), F32)] * 4, out_specs=[spec] * 4)

    late_res = adam_packed("adam_small", gathered_late, late_w, late_m, late_v, late_sizes, late_rows, late_rows)
    early_res = adam_packed("adam_spatial", gathered_early, [sgu_w_spatial, sgu_b_spatial], [m_sgu_w_spatial, m_sgu_b_spatial],
                            [v_sgu_w_spatial, v_sgu_b_spatial], early_sizes[:2], early_rep, SMALL_ROWS)

    def unpack(res, sizes, k, like):
        off = sum(sizes[:k])
        return res[off:off + like.size // 128].reshape(like.shape)

    my_b = 4 * lax.axis_index("x") + 2 * lax.axis_index("y") + lax.axis_index("c")
    ln_w = jnp.concatenate([sgu_ln_g, sgu_ln_b], 0)
    ln_m = jnp.concatenate([m_sgu_ln_g, m_sgu_ln_b], 0)
    ln_v = jnp.concatenate([v_sgu_ln_g, v_sgu_ln_b], 0)
    ln_all = jnp.concatenate([gathered_early[:, early_rep:early_rep + n_sgu * E // 128], gathered_early[:, early_rep + n_ln:early_rep + n_ln + n_sgu * E // 128]], axis=1)
    ln_mine = lax.dynamic_slice_in_dim(ln_all.reshape(N_DEV, 2 * n_sgu, N_DEV, ec), my_b, 1, axis=2).reshape(N_DEV, 2 * n_sgu, ec)
    ln_g_, ln_d, ln_m2, ln_v2 = _rowwise(
        "adam_ln", lambda p, w_, m_, v_: (sum8(p), *_adam(w_, sum8(p), m_, v_)), [ln_mine, ln_w, ln_m, ln_v], grid=(1,),
        in_specs=[_const_spec(ln_mine.shape), _const_spec(ln_w.shape), _const_spec(ln_w.shape), _const_spec(ln_w.shape)],
        out_shapes=[_sds(ln_w.shape, F32)] * 4, out_specs=[_const_spec(ln_w.shape)] * 4)

    def family(pos):
        ln = [ln_g_, ln_d, ln_m2, ln_v2][pos]
        late = [unpack(late_res[pos], late_sizes, k, w_) for k, w_ in enumerate(late_w)]
        w_sp_, b_sp_ = unpack(early_res[pos], early_sizes, 0, sgu_w_spatial), unpack(early_res[pos], early_sizes, 1, sgu_b_spatial)
        bigs = [res[pos] for res in big_res]
        return [late[0], late[1], late[2], bigs[0], late[3], late[4], bigs[1], bigs[2], bigs[3],
                bigs[4], ln[:n_sgu], ln[n_sgu:], w_sp_, b_sp_, bigs[5], bigs[6], bigs[7]]

    return (loss, grad_x, *family(0), *family(1), *family(2), *family(3))
```

```python
import math

import jax
import jax.numpy as jnp
from jax import lax
from jax.experimental import pallas as pl
from jax.experimental.pallas import tpu as pltpu

F32 = jnp.float32
BF16 = jnp.bfloat16
MESH = pl.DeviceIdType.MESH

N_DEV = 8
N_CHIP = 4
HEADS = 8
NOPE = 128
ROPE = 64
VDIM = 128
QPAD = 256
Q_RANK = 256
KV_RANK = 128
LAT = Q_RANK + KV_RANK + ROPE
LAT_PAD = 512
ROPE_THETA = 10000.0
SGU_CHUNK = 128
SGU_GROUPS = 8
NORM_EPS = 1e-6
LN_EPS = 1e-5
ADAM_LR = 0.001
ADAM_B1 = 0.9
ADAM_B2 = 0.999
ADAM_EPS = 1e-08
ADAM_WD = 0.01
ADAM_STEP = 10
ATTN_SCALE = (NOPE + ROPE) ** -0.5
NEG = -1e30
EXP2_SCALE = ATTN_SCALE * math.log2(math.e)
VMEM_LIMIT = 56 * 1024 * 1024
SMALL_ROWS = 256

NN = (((1,), (0,)), ((), ()))
NT = (((1,), (1,)), ((), ()))
TN = (((0,), (0,)), ((), ()))
ANY = pl.BlockSpec(memory_space=pl.ANY)


def _pcall(body, **kw):
    return pl.pallas_call(body, **kw)


def _params(n_grid, side_effects=False):
    return pltpu.CompilerParams(dimension_semantics=("arbitrary",) * n_grid, vmem_limit_bytes=VMEM_LIMIT, has_side_effects=side_effects)


def _sds(shape, dtype):
    return jax.ShapeDtypeStruct(tuple(shape), dtype)


def _tile(n, want):
    t = min(n, want)
    assert n % t == 0, (n, want)
    return t


class _Comm:
    def __init__(self, operands, out_shapes, aliases, scratch, start, finish):
        self.operands, self.out_shapes, self.aliases, self.scratch = operands, out_shapes, aliases, scratch
        self.start, self.finish = start, finish


def _merge_comm(first, second):
    n_in, n_out, n_sc = len(first.operands), len(first.out_shapes), len(first.scratch)
    aliases = dict(first.aliases)
    aliases.update({n_in + k: n_out + v for k, v in second.aliases.items()})

    def start(ins, outs, sems):
        first.start(ins[:n_in], outs[:n_out], sems[:n_sc])
        second.start(ins[n_in:], outs[n_out:], sems[n_sc:])

    def finish(ins, outs, sems):
        first.finish(ins[:n_in], outs[:n_out], sems[:n_sc])
        second.finish(ins[n_in:], outs[n_out:], sems[n_sc:])

    return _Comm([*first.operands, *second.operands], [*first.out_shapes, *second.out_shapes], aliases,
                 [*first.scratch, *second.scratch], start, finish)


def _place():
    return lax.axis_index("x"), lax.axis_index("y"), lax.axis_index("c")


def _other_chips(x, y):
    return [(1 - x, y), (x, 1 - y), (1 - x, 1 - y)]


def _dev_index(dev):
    return 4 * dev[0] + 2 * dev[1] + dev[2]


def _comm_call(name, comm):
    c_in, c_out = len(comm.operands), len(comm.out_shapes)

    def body(*refs):
        ins, outs, sems = refs[:c_in], refs[c_in:c_in + c_out], refs[c_in + c_out:]
        comm.start(ins, outs, sems)
        comm.finish(ins, outs, sems)

    return _pcall(body, name=name, in_specs=[ANY] * c_in, out_specs=[ANY] * c_out, out_shape=comm.out_shapes,
                  scratch_shapes=comm.scratch, input_output_aliases=dict(comm.aliases),
                  compiler_params=pltpu.CompilerParams(has_side_effects=True))(*comm.operands)


def _call(name, body, operands, in_specs, out_shapes, out_specs, scratch, grid, comm=None):
    if comm is None:
        return _pcall(body, name=name, grid=grid, in_specs=in_specs, out_specs=out_specs, out_shape=out_shapes,
                      scratch_shapes=scratch, compiler_params=_params(len(grid)))(*operands)
    n_in, n_out, n_sc = len(operands), len(out_shapes), len(scratch)
    c_in, c_out = len(comm.operands), len(comm.out_shapes)

    def hosted(*refs):
        ins, cins = refs[:n_in], refs[n_in:n_in + c_in]
        o0 = n_in + c_in
        outs, couts = refs[o0:o0 + n_out], refs[o0 + n_out:o0 + n_out + c_out]
        rest = refs[o0 + n_out + c_out:]
        sc, csems = rest[:n_sc], rest[n_sc:]
        first = pl.program_id(0) == 0
        last = pl.program_id(0) == grid[0] - 1
        for d in range(1, len(grid)):
            first = jnp.logical_and(first, pl.program_id(d) == 0)
            last = jnp.logical_and(last, pl.program_id(d) == grid[d] - 1)

        @pl.when(first)
        def _():
            comm.start(cins, couts, csems)

        body(*ins, *outs, *sc)

        @pl.when(last)
        def _():
            comm.finish(cins, couts, csems)

    return _pcall(hosted, name=name, grid=grid, in_specs=[*in_specs, *[ANY] * c_in], out_specs=[*out_specs, *[ANY] * c_out],
                  out_shape=[*out_shapes, *comm.out_shapes], scratch_shapes=[*scratch, *comm.scratch],
                  input_output_aliases={n_in + k: n_out + v for k, v in comm.aliases.items()},
                  compiler_params=_params(len(grid), side_effects=True))(*operands, *comm.operands)


def _gather_level1(shards, part=(0, 1), into=None):
    n = len(shards)
    k_part, m_part = part

    def copies(ins, outs, sems):
        send_sems, recv_sems, local_sems = sems
        x, y, c = _place()
        me, sibling = (x, y, c), (x, y, 1 - c)
        chips = _other_chips(x, y)

        def rows(a):
            r = shards[a].shape[0] // m_part
            return pl.ds(k_part * r, r)

        def copy(a, k, block, to, own=False):
            slot = outs[a].at[_dev_index(block), rows(a)]
            return pltpu.make_async_remote_copy(src_ref=ins[a].at[rows(a)] if own else slot, dst_ref=slot, send_sem=send_sems.at[a, k],
                                                recv_sem=recv_sems.at[a, k], device_id=to, device_id_type=MESH)

        mine = [pltpu.make_async_copy(ins[a].at[rows(a)], outs[a].at[_dev_index(me), rows(a)], local_sems.at[a]) for a in range(n)]
        sends = [copy(a, 1 + j, me, (*chip, c), own=True) for j, chip in enumerate(chips) for a in range(n)]
        sends += [copy(a, 0, me, sibling, own=True) for a in range(n)]
        recvs = [copy(a, 1 + j, (*chip, c), me) for j, chip in enumerate(chips) for a in range(n)]
        recvs += [copy(a, 0, sibling, me) for a in range(n)]
        return mine, sends, recvs

    def start(ins, outs, sems):
        mine, sends, _ = copies(ins, outs, sems)
        for cp in mine + sends:
            cp.start()

    def finish(ins, outs, sems):
        mine, sends, recvs = copies(ins, outs, sems)
        for cp in recvs:
            cp.wait_recv()
        for cp in sends:
            cp.wait_send()
        for cp in mine:
            cp.wait()

    return _Comm([*shards, *(into or [])], [_sds((N_DEV, *a.shape), a.dtype) for a in shards], {n + a: a for a in range(n)} if into else {},
                 [pltpu.SemaphoreType.DMA((n, 4)), pltpu.SemaphoreType.DMA((n, 4)), pltpu.SemaphoreType.DMA((n,))], start, finish)


def _gather_level2(bufs):
    n = len(bufs)

    def copies(outs, sems):
        send_sems, recv_sems = sems
        x, y, c = _place()
        sibling = (x, y, 1 - c)
        sends, recvs = [], []
        for j, chip in enumerate(_other_chips(x, y)):
            for a in range(n):
                have, want = outs[a].at[_dev_index((*chip, c))], outs[a].at[_dev_index((*chip, 1 - c))]
                sends.append(pltpu.make_async_remote_copy(src_ref=have, dst_ref=have, send_sem=send_sems.at[a, j], recv_sem=recv_sems.at[a, j],
                                                          device_id=sibling, device_id_type=MESH))
                recvs.append(pltpu.make_async_remote_copy(src_ref=want, dst_ref=want, send_sem=send_sems.at[a, j], recv_sem=recv_sems.at[a, j],
                                                          device_id=sibling, device_id_type=MESH))
        return sends, recvs

    def start(ins, outs, sems):
        for cp in copies(outs, sems)[0]:
            cp.start()

    def finish(ins, outs, sems):
        sends, recvs = copies(outs, sems)
        for cp in recvs:
            cp.wait_recv()
        for cp in sends:
            cp.wait_send()

    return _Comm(bufs, [_sds(b.shape, b.dtype) for b in bufs], {a: a for a in range(n)},
                 [pltpu.SemaphoreType.DMA((n, 3)), pltpu.SemaphoreType.DMA((n, 3))], start, finish)


def _all_gather(name, arrays):
    n = len(arrays)

    def body(*refs):
        ins = refs[:n]
        outs = refs[n:2 * n]
        send_sems, recv_sems, local_sems = refs[2 * n:]
        x, y, c = _place()
        me, sibling = (x, y, c), (x, y, 1 - c)
        chips = _other_chips(x, y)

        def copy(a, k, block, to, src=None):
            slot = outs[a].at[_dev_index(block)]
            return pltpu.make_async_remote_copy(src_ref=slot if src is None else src, dst_ref=slot, send_sem=send_sems.at[a, k],
                                                recv_sem=recv_sems.at[a, k], device_id=to, device_id_type=MESH)

        mine = [pltpu.make_async_copy(ins[a], outs[a].at[_dev_index(me)], local_sems.at[a]) for a in range(n)]
        for cp in mine:
            cp.start()
        first = []
        for j, chip in enumerate(chips):
            first += [copy(a, 1 + j, me, (*chip, c), src=ins[a]) for a in range(n)]
        first += [copy(a, 0, me, sibling, src=ins[a]) for a in range(n)]
        for cp in first:
            cp.start()
        passed = []
        for j, chip in enumerate(chips):
            for a in range(n):
                copy(a, 1 + j, (*chip, c), me).wait_recv()
                fwd = copy(a, 4 + j, (*chip, c), sibling)
                fwd.start()
                passed.append(fwd)
        for a in range(n):
            copy(a, 0, sibling, me).wait_recv()
            for j, chip in enumerate(chips):
                copy(a, 4 + j, (*chip, 1 - c), me).wait_recv()
        for cp in first + passed:
            cp.wait_send()
        for cp in mine:
            cp.wait()

    return _pcall(
        body, name=name, in_specs=[ANY] * n, out_specs=[ANY] * n,
        out_shape=[_sds((N_DEV, *a.shape), a.dtype) for a in arrays],
        scratch_shapes=[pltpu.SemaphoreType.DMA((n, 7)), pltpu.SemaphoreType.DMA((n, 7)), pltpu.SemaphoreType.DMA((n,))],
        compiler_params=pltpu.CompilerParams(has_side_effects=True),
    )(*arrays)


def _sibling_exchange(grads):
    n = len(grads)

    def start(ins, outs, sems):
        send_sems, recv_sems = sems
        x, y, c = _place()
        for a in range(n):
            for ch in range(N_CHIP):
                pltpu.make_async_remote_copy(src_ref=ins[a].at[2 * ch + 1 - c], dst_ref=outs[a].at[ch], send_sem=send_sems.at[a],
                                             recv_sem=recv_sems.at[a], device_id=(x, y, 1 - c), device_id_type=MESH).start()

    def finish(ins, outs, sems):
        send_sems, recv_sems = sems
        x, y, c = _place()
        for a in range(n):
            pltpu.make_async_remote_copy(src_ref=outs[a], dst_ref=outs[a], send_sem=send_sems.at[a], recv_sem=recv_sems.at[a],
                                         device_id=(x, y, 1 - c), device_id_type=MESH).wait()

    return _Comm(grads, [_sds((N_CHIP, *g.shape[1:]), g.dtype) for g in grads], {},
                 [pltpu.SemaphoreType.DMA((n,)), pltpu.SemaphoreType.DMA((n,))], start, finish)


def _chip_exchange(parts, slots, layers, stacked):
    n = len(parts)
    names = []
    for nm, _ in slots:
        if nm not in names:
            names.append(nm)
    shapes = {nm: _sds((N_CHIP, layers[nm], *parts[a].shape[1:]), parts[a].dtype) for a, (nm, _) in enumerate(slots)}
    kept = [nm for nm in names if stacked.get(nm) is not None]
    aliases = {n + k: names.index(nm) for k, nm in enumerate(kept)}

    def copies(ins, outs, sems):
        send_sems, recv_sems, local_sems = sems
        x, y, c = _place()
        mine = 2 * x + y
        local, sends, recvs = [], [], []
        for a, (nm, l) in enumerate(slots):
            buf = outs[names.index(nm)]
            local.append(pltpu.make_async_copy(ins[a].at[mine], buf.at[mine, l], local_sems.at[a]))
            for j, chip in enumerate(_other_chips(x, y)):
                theirs = buf.at[2 * chip[0] + chip[1], l]
                sends.append(pltpu.make_async_remote_copy(src_ref=ins[a].at[2 * chip[0] + chip[1]], dst_ref=buf.at[mine, l], send_sem=send_sems.at[a, j],
                                                          recv_sem=recv_sems.at[a, j], device_id=(*chip, c), device_id_type=MESH))
                recvs.append(pltpu.make_async_remote_copy(src_ref=theirs, dst_ref=theirs, send_sem=send_sems.at[a, j],
                                                          recv_sem=recv_sems.at[a, j], device_id=(*chip, c), device_id_type=MESH))
        return local, sends, recvs

    def start(ins, outs, sems):
        local, sends, _ = copies(ins, outs, sems)
        for cp in local + sends:
            cp.start()

    def finish(ins, outs, sems):
        local, sends, recvs = copies(ins, outs, sems)
        for cp in recvs:
            cp.wait_recv()
        for cp in sends:
            cp.wait_send()
        for cp in local:
            cp.wait()

    comm = _Comm([*parts, *[stacked[nm] for nm in kept]], [shapes[nm] for nm in names], aliases,
                 [pltpu.SemaphoreType.DMA((n, 3)), pltpu.SemaphoreType.DMA((n, 3)), pltpu.SemaphoreType.DMA((n,))], start, finish)
    return comm, names


def _matmul(name, a, b, extras, *, grid, a_spec, b_spec, extra_specs, out_shapes, out_specs, dims, k_axis=None, nk=1,
            acc_shape=None, epilogue=None, comm=None, n_sum=0, write=None):
    n_extra = len(extras)
    n_out = len(out_shapes)

    def body(*refs):
        a_ref, b_ref = refs[0], refs[1]
        ex = refs[2:2 + n_extra]
        outs = refs[2 + n_extra:2 + n_extra + n_out]
        prod = lax.dot_general(a_ref[...], b_ref[...], dims, preferred_element_type=F32)

        def finish(acc):
            if write is not None:
                write(outs, acc, *[e[...] for e in ex])
                return
            res = epilogue(acc, *[e[...] for e in ex]) if epilogue is not None else (acc,)
            first = None
            for d in range(len(grid)):
                if d != k_axis:
                    here = pl.program_id(d) == 0
                    first = here if first is None else jnp.logical_and(first, here)
            for idx, (o, r) in enumerate(zip(outs, res)):
                if idx < n_out - n_sum:
                    o[...] = r.astype(o.dtype)
                else:
                    @pl.when(first)
                    def _(o=o, r=r):
                        o[...] = r.astype(o.dtype)

                    @pl.when(jnp.logical_not(first))
                    def _(o=o, r=r):
                        o[...] += r.astype(o.dtype)

        if k_axis is None:
            finish(prod)
        else:
            acc_ref = refs[-1]
            k = pl.program_id(k_axis)

            @pl.when(k == 0)
            def _():
                acc_ref[...] = prod

            @pl.when(k > 0)
            def _():
                acc_ref[...] += prod

            @pl.when(k == nk - 1)
            def _():
                finish(acc_ref[...])

    scratch = [] if k_axis is None else [pltpu.VMEM(acc_shape, F32)]
    return _call(name, body, [a, b, *extras], [a_spec, b_spec, *extra_specs], list(out_shapes), list(out_specs), scratch, grid, comm)


def _rowwise(name, fn, operands, *, grid, in_specs, out_shapes, out_specs, n_acc=0, grid_spec_prefetch=None, comm=None):
    n_in = len(operands)
    n_out = len(out_shapes)
    n_pre = 0 if grid_spec_prefetch is None else 1

    def body(*refs):
        refs = refs[n_pre:]
        ins = refs[:n_in]
        outs = refs[n_in:n_in + n_out]
        res = fn(*[r[...] for r in ins])
        if not isinstance(res, (tuple, list)):
            res = (res,)
        first = pl.program_id(0) == 0
        for d in range(1, len(grid)):
            first = jnp.logical_and(first, pl.program_id(d) == 0)
        for idx, (o, r) in enumerate(zip(outs, res)):
            if idx < n_out - n_acc:
                o[...] = r.astype(o.dtype)
            else:
                @pl.when(first)
                def _(o=o, r=r):
                    o[...] = r.astype(o.dtype)

                @pl.when(jnp.logical_not(first))
                def _(o=o, r=r):
                    o[...] += r.astype(o.dtype)

    if comm is not None:
        return _call(name, body, list(operands), list(in_specs), list(out_shapes), list(out_specs), [], grid, comm)
    if grid_spec_prefetch is None:
        return _pcall(body, name=name, grid=grid, in_specs=in_specs, out_specs=out_specs, out_shape=out_shapes,
                      compiler_params=_params(len(grid)))(*operands)
    gs = pltpu.PrefetchScalarGridSpec(num_scalar_prefetch=1, grid=grid, in_specs=in_specs, out_specs=out_specs)
    return _pcall(body, name=name, grid_spec=gs, out_shape=out_shapes,
                  compiler_params=_params(len(grid)))(grid_spec_prefetch, *operands)


def _row_spec(tm, w):
    return pl.BlockSpec((tm, w), lambda i: (i, 0))


def _const_spec(shape):
    nd = len(shape)
    return pl.BlockSpec(tuple(shape), lambda *_: (0,) * nd)


def _rms_fwd(x, g):
    r = lax.rsqrt(jnp.mean(x * x, axis=-1, keepdims=True) + NORM_EPS)
    return x * r * g


def _rms_bwd(x, g, dy):
    r = lax.rsqrt(jnp.mean(x * x, axis=-1, keepdims=True) + NORM_EPS)
    xh = x * r
    u = dy * g
    dx = r * (u - xh * jnp.mean(u * xh, axis=-1, keepdims=True))
    dg = jnp.sum(dy * xh, axis=0, keepdims=True)
    return dx, dg


def _gelu_and_grad(z):
    cdf = 0.5 * (1.0 + lax.erf(z * (2.0 ** -0.5)))
    return cdf + z * jnp.exp(-0.5 * z * z) * ((2.0 * math.pi) ** -0.5), z * cdf


def _rope_fwd(x, cc, sa, sb):
    return x * cc + pltpu.roll(x, 96, 1) * sa + pltpu.roll(x, 32, 1) * sb


def _rope_bwd(d, cc, sa, sb):
    return d * cc + pltpu.roll(d * sa, 32, 1) + pltpu.roll(d * sb, 96, 1)


def _adam(w, g, m, v):
    m = ADAM_B1 * m + (1.0 - ADAM_B1) * g
    v = ADAM_B2 * v + (1.0 - ADAM_B2) * (g * g)
    m_hat = m / (1.0 - ADAM_B1 ** ADAM_STEP)
    v_hat = v / (1.0 - ADAM_B2 ** ADAM_STEP)
    delta = -ADAM_LR * (m_hat / (jnp.sqrt(v_hat) + ADAM_EPS) + ADAM_WD * w)
    return delta, m, v


def _flash_fwd(q, k, vt, tq, comm=None):
    h, t = vt.shape[0], q.shape[0]
    nq = t // tq

    chunk_blocks = [c for c in (4, 2) if c < nq]

    def body(q_ref, k_ref, vt_ref, o_ref, lse_ref, m_ref, l_ref, acc_ref):
        qi = pl.program_id(1)
        m_ref[...] = jnp.full((1, tq), NEG, F32)
        l_ref[...] = jnp.zeros((1, tq), F32)
        acc_ref[...] = jnp.zeros((VDIM, tq), F32)

        def update(kb0, nblk, masked):
            kb = k_ref[pl.ds(pl.multiple_of(kb0 * tq, tq), nblk * tq), :]
            st = lax.dot_general(kb, q_ref[...], NT, preferred_element_type=F32)
            if masked:
                key = lax.broadcasted_iota(jnp.int32, (nblk * tq, tq), 0) - (nblk - 1) * tq
                qry = lax.broadcasted_iota(jnp.int32, (nblk * tq, tq), 1)
                st = jnp.where(key <= qry, st, NEG)
            m_old = m_ref[...]
            m_new = jnp.maximum(m_old, jnp.max(st, axis=0, keepdims=True))
            alpha = jnp.exp2((m_old - m_new) * EXP2_SCALE)
            pt = jnp.exp2((st - m_new) * EXP2_SCALE)
            l_ref[...] = alpha * l_ref[...] + jnp.sum(pt, axis=0, keepdims=True)
            ptb = pt.astype(BF16)
            pv = lax.dot_general(vt_ref[kb0], ptb[:tq], NN, preferred_element_type=F32)
            for j in range(1, nblk):
                pv += lax.dot_general(vt_ref[kb0 + j], ptb[j * tq:(j + 1) * tq], NN, preferred_element_type=F32)
            acc_ref[...] = alpha * acc_ref[...] + pv
            m_ref[...] = m_new

        start = jnp.int32(0)
        for c in chunk_blocks:
            take = (qi & c) != 0

            @pl.when(take)
            def _(start=start, c=c):
                update(start, c, False)

            start = start + jnp.where(take, c, 0)
        if nq > 1:
            @pl.when((qi & 1) != 0)
            def _():
                update(qi - 1, 2, True)

            @pl.when((qi & 1) == 0)
            def _():
                update(qi, 1, True)
        else:
            update(qi, 1, True)
        l = l_ref[...]
        o_ref[...] = (acc_ref[...] / l).T.astype(o_ref.dtype)
        lse_ref[...] = m_ref[...] * EXP2_SCALE + jnp.log2(l)

    return _call(
        "flash_fwd", body, [q, k, vt],
        [pl.BlockSpec((tq, QPAD), lambda hh, i: (i, hh)),
         pl.BlockSpec((t, QPAD), lambda hh, i: (0, hh)),
         pl.BlockSpec((None, nq, VDIM, tq), lambda hh, i: (hh, 0, 0, 0))],
        [_sds((t, h * VDIM), BF16), _sds((h, nq, 1, tq), F32)],
        [pl.BlockSpec((tq, VDIM), lambda hh, i: (i, hh)),
         pl.BlockSpec((None, None, 1, tq), lambda hh, i: (hh, i, 0, 0))],
        [pltpu.VMEM((1, tq), F32), pltpu.VMEM((1, tq), F32), pltpu.VMEM((VDIM, tq), F32)], (h, nq), comm)


def _flash_bwd(q, k, v, o, do, lse, tabs, tq, comm=None):
    t = q.shape[0]
    h = q.shape[1] // QPAD
    nq = t // tq

    def body(q_ref, k_ref, v_ref, o_ref, do_ref, lse_ref, cc_ref, sa_ref, sb_ref, dq_ref, dk_out, dv_out, delta_ref, dqt_ref, dk_ref, dv_ref):
        kj = pl.program_id(1)

        @pl.when(kj == 0)
        def _():
            dqt_ref[...] = jnp.zeros_like(dqt_ref)
            ones = jnp.ones((8, VDIM), BF16)
            for qi in range(nq):
                rows = pl.ds(qi * tq, tq)
                prod = do_ref[rows, :].astype(F32) * o_ref[rows, :].astype(F32)
                hi = prod.astype(BF16)
                lo = (prod - hi.astype(F32)).astype(BF16)
                delta_ref[qi] = (lax.dot_general(ones, hi, NT, preferred_element_type=F32)
                                 + lax.dot_general(ones, lo, NT, preferred_element_type=F32))

        kb = k_ref[...]
        vb = v_ref[...]
        kbt = kb.astype(F32).T.astype(BF16)
        dk_ref[...] = jnp.zeros_like(dk_ref)
        dv_ref[...] = jnp.zeros_like(dv_ref)

        def step(q0, nblk, masked):
            rows = pl.ds(pl.multiple_of(q0 * tq, tq), nblk * tq)
            qb = q_ref[rows, :]
            dob = do_ref[rows, :]
            lse = jnp.concatenate([lse_ref[q0 + j] for j in range(nblk)], axis=1)
            delta = jnp.concatenate([delta_ref[q0 + j, pl.ds(0, 1), :] for j in range(nblk)], axis=1)
            st = lax.dot_general(kb, qb, NT, preferred_element_type=F32)
            pt = jnp.exp2(st * EXP2_SCALE - lse)
            if masked:
                key = lax.broadcasted_iota(jnp.int32, (tq, nblk * tq), 0)
                qry = lax.broadcasted_iota(jnp.int32, (tq, nblk * tq), 1)
                pt = jnp.where(key <= qry, pt, 0.0)
            dv_ref[...] += lax.dot_general(pt.astype(BF16), dob, NN, preferred_element_type=F32)
            dpt = lax.dot_general(vb, dob, NT, preferred_element_type=F32)
            dst = (pt * (dpt - delta) * ATTN_SCALE).astype(BF16)
            dk_ref[...] += lax.dot_general(dst, qb, NN, preferred_element_type=F32)
            dqt = lax.dot_general(kbt, dst, NN, preferred_element_type=F32)
            for j in range(nblk):
                dqt_ref[q0 + j] += dqt[:, j * tq:(j + 1) * tq]

        later = nq - 1 - kj
        if nq > 1:
            @pl.when((later & 1) != 0)
            def _():
                step(kj, 2, True)

            @pl.when((later & 1) == 0)
            def _():
                step(kj, 1, True)
        else:
            step(kj, 1, True)
        start = kj + 1 + (later & 1)
        for c in [c for c in (2, 4) if c < nq]:
            take = (later & c) != 0

            @pl.when(take)
            def _(start=start, c=c):
                step(start, c, False)

            start = start + jnp.where(take, c, 0)
        dk_out[...] = dk_ref[...].astype(BF16)
        dv_out[...] = dv_ref[...].astype(BF16)

        @pl.when(kj == nq - 1)
        def _():
            for qi in range(nq):
                rows = pl.ds(qi * tq, tq)
                d = dqt_ref[qi].T
                roped = _rope_bwd(d[:, NOPE:], cc_ref[rows, :], sa_ref[rows, :], sb_ref[rows, :])
                dq_ref[rows, :] = jnp.concatenate([d[:, :NOPE], roped], axis=1).astype(BF16)

    head_q = pl.BlockSpec((t, QPAD), lambda hh, j: (0, hh))
    head_v = pl.BlockSpec((t, VDIM), lambda hh, j: (0, hh))
    table = pl.BlockSpec((t, 128), lambda hh, j: (0, 0))
    return _call(
        "flash_bwd", body, [q, k, v, o, do, lse, *tabs],
        [head_q, pl.BlockSpec((tq, QPAD), lambda hh, j: (j, hh)), pl.BlockSpec((tq, VDIM), lambda hh, j: (j, hh)), head_v, head_v,
         pl.BlockSpec((None, nq, 1, tq), lambda hh, j: (hh, 0, 0, 0)), table, table, table],
        [_sds((t, h * QPAD), BF16), _sds((t, h * QPAD), BF16), _sds((t, h * VDIM), BF16)],
        [head_q, pl.BlockSpec((tq, QPAD), lambda hh, j: (j, hh)), pl.BlockSpec((tq, VDIM), lambda hh, j: (j, hh))],
        [pltpu.VMEM((nq, 8, tq), F32), pltpu.VMEM((nq, QPAD, tq), F32), pltpu.VMEM((tq, QPAD), F32), pltpu.VMEM((tq, VDIM), F32)], (h, nq), comm)


def _tril_bf16(w):
    row = lax.broadcasted_iota(jnp.int32, w.shape, 0)
    col = lax.broadcasted_iota(jnp.int32, w.shape, 1)
    return jnp.where(col <= row, w, 0.0).astype(BF16)


def _layer_norm_parts(v0):
    mu = jnp.mean(v0, axis=-1, keepdims=True)
    vc = v0 - mu
    rstd = lax.rsqrt(jnp.mean(vc * vc, axis=-1, keepdims=True) + LN_EPS)
    return vc * rstd, rstd


def _sgu_mid_fwd(ge, ln_g, ln_b, w_sp, b_sp, chunks_per_step):
    t, e2 = ge.shape
    e = e2 // 2
    gd = e // SGU_GROUPS
    rows = SGU_CHUNK * chunks_per_step

    def body(u_ref, v_ref, g_ref, b_ref, w_ref, bs_ref, gate_ref):
        for ck in range(chunks_per_step):
            r = pl.ds(ck * SGU_CHUNK, SGU_CHUNK)
            xh, _ = _layer_norm_parts(v_ref[r, :].astype(F32))
            v1 = (xh * g_ref[...] + b_ref[...]).astype(BF16)
            for g in range(SGU_GROUPS):
                cols = pl.ds(g * gd, gd)
                mixed = lax.dot_general(_tril_bf16(w_ref[g]), v1[:, g * gd:(g + 1) * gd], NN, preferred_element_type=F32) + bs_ref[g]
                gate_ref[r, cols] = (u_ref[r, cols].astype(F32) * mixed).astype(BF16)

    return _pcall(
        body, name="sgu_mid_fwd", grid=(t // rows,),
        in_specs=[pl.BlockSpec((rows, e), lambda i: (i, 0)), pl.BlockSpec((rows, e), lambda i: (i, 1)),
                  _const_spec((1, e)), _const_spec((1, e)), _const_spec(w_sp.shape), _const_spec(b_sp.shape)],
        out_specs=pl.BlockSpec((rows, e), lambda i: (i, 0)),
        out_shape=_sds((t, e), BF16), compiler_params=_params(1),
    )(ge, ge, ln_g, ln_b, w_sp, b_sp)


def _sgu_mid_bwd(ge, gp, dgate, ln_g, ln_b, w_sp, b_sp, chunks_per_step):
    t, e2 = ge.shape
    e = e2 // 2
    gd = e // SGU_GROUPS
    rows = SGU_CHUNK * chunks_per_step

    def body(u_ref, v_ref, zu_ref, zv_ref, dg_ref, g_ref, b_ref, w_ref, bs_ref, dz_ref, dw_ref, dbs_ref, dlg_ref, dlb_ref):
        @pl.when(pl.program_id(0) == 0)
        def _():
            dw_ref[...] = jnp.zeros_like(dw_ref)
            dbs_ref[...] = jnp.zeros_like(dbs_ref)
            dlg_ref[...] = jnp.zeros_like(dlg_ref)
            dlb_ref[...] = jnp.zeros_like(dlb_ref)

        for ck in range(chunks_per_step):
            r = pl.ds(ck * SGU_CHUNK, SGU_CHUNK)
            xh, rstd = _layer_norm_parts(v_ref[r, :].astype(F32))
            v1 = (xh * g_ref[...] + b_ref[...]).astype(BF16)
            dv1_parts = []
            for g in range(SGU_GROUPS):
                cols = pl.ds(g * gd, gd)
                wc = _tril_bf16(w_ref[g])
                v1g = v1[:, g * gd:(g + 1) * gd]
                mixed = lax.dot_general(wc, v1g, NN, preferred_element_type=F32) + bs_ref[g]
                dgate = dg_ref[r, cols].astype(F32)
                dmixed = dgate * u_ref[r, cols].astype(F32)
                du = dgate * mixed
                dz_ref[r, cols] = (du * zu_ref[r, cols].astype(F32)).astype(BF16)
                dbs_ref[g] += jnp.sum(dmixed, axis=1, keepdims=True)
                dmb = dmixed.astype(BF16)
                dwg = lax.dot_general(dmb, v1g, NT, preferred_element_type=F32)
                row = lax.broadcasted_iota(jnp.int32, dwg.shape, 0)
                col = lax.broadcasted_iota(jnp.int32, dwg.shape, 1)
                dw_ref[g] += jnp.where(col <= row, dwg, 0.0)
                dv1_parts.append(lax.dot_general(wc, dmb, TN, preferred_element_type=F32))
            dv1 = jnp.concatenate(dv1_parts, axis=1)
            dlg_ref[...] += jnp.sum(dv1 * xh, axis=0, keepdims=True)
            dlb_ref[...] += jnp.sum(dv1, axis=0, keepdims=True)
            dxh = dv1 * g_ref[...]
            dv0 = rstd * (dxh - jnp.mean(dxh, axis=-1, keepdims=True) - xh * jnp.mean(dxh * xh, axis=-1, keepdims=True))
            dz_ref[r, pl.ds(e, e)] = (dv0 * zv_ref[r, :].astype(F32)).astype(BF16)

    half0 = pl.BlockSpec((rows, e), lambda i: (i, 0))
    half1 = pl.BlockSpec((rows, e), lambda i: (i, 1))
    return _pcall(
        body, name="sgu_mid_bwd", grid=(t // rows,),
        in_specs=[half0, half1, half0, half1, half0, _const_spec((1, e)), _const_spec((1, e)), _const_spec(w_sp.shape), _const_spec(b_sp.shape)],
        out_specs=[pl.BlockSpec((rows, e2), lambda i: (i, 0)), _const_spec(w_sp.shape), _const_spec(b_sp.shape), _const_spec((1, e)), _const_spec((1, e))],
        out_shape=[_sds((t, e2), BF16), _sds(w_sp.shape, F32), _sds(b_sp.shape, F32), _sds((1, e), F32), _sds((1, e), F32)],
        compiler_params=_params(1),
    )(ge, ge, gp, gp, dgate, ln_g, ln_b, w_sp, b_sp)


def kernel(x, positions, norm_mix, norm_ffn, final_norm, mla_w_dkv, mla_q_norm, mla_kv_norm, mla_w_uq, mla_w_ukv, mla_w_o, sgu_w_in, sgu_ln_g, sgu_ln_b, sgu_w_spatial, sgu_b_spatial, sgu_w_out, ffn_w_up, ffn_w_down, loss_target, m_norm_mix, m_norm_ffn, m_final_norm, m_mla_w_dkv, m_mla_q_norm, m_mla_kv_norm, m_mla_w_uq, m_mla_w_ukv, m_mla_w_o, m_sgu_w_in, m_sgu_ln_g, m_sgu_ln_b, m_sgu_w_spatial, m_sgu_b_spatial, m_sgu_w_out, m_ffn_w_up, m_ffn_w_down, v_norm_mix, v_norm_ffn, v_final_norm, v_mla_w_dkv, v_mla_q_norm, v_mla_kv_norm, v_mla_w_uq, v_mla_w_ukv, v_mla_w_o, v_sgu_w_in, v_sgu_ln_g, v_sgu_ln_b, v_sgu_w_spatial, v_sgu_b_spatial, v_sgu_w_out, v_ffn_w_up, v_ffn_w_down):
    _, T, D = x.shape
    depth = norm_mix.shape[0]
    n_mla, n_sgu = mla_w_dkv.shape[0], sgu_w_in.shape[0]
    assert depth % 2 == 0
    FF = ffn_w_up.shape[2] * N_DEV
    E = sgu_w_out.shape[1] * N_DEV
    ffc, ec, e2c = FF // N_DEV, E // N_DEV, 2 * E // N_DEV
    dc = D // N_DEV
    OW = HEADS * VDIM
    HW = HEADS * QPAD
    owc = OW // N_DEV
    tm = _tile(T, 1024)
    tb = _tile(T, 4096)
    tk = _tile(T, 512)
    tq = _tile(T, 512)
    ts = _tile(T, 256)
    nt = T // tm
    x2 = x.reshape(T, D)
    tgt = loss_target.reshape(T, D)
    cidx = lax.axis_index("c").astype(jnp.int32).reshape(1)

    ln_local = jnp.concatenate([sgu_ln_g, sgu_ln_b, jnp.zeros((8 - 2 * n_sgu, ec), F32)], axis=0)
    mla_sh = [[w[l].astype(BF16) for w in (mla_w_dkv, mla_w_uq, mla_w_ukv, mla_w_o)] for l in range(n_mla)]

    def mla_layouts(g_dkv, g_uq, g_ukv, g_o):
        w_dkv = jnp.pad(g_dkv.reshape(1, D, LAT), ((0, 0), (0, 0), (0, LAT_PAD - LAT)))
        w_uq = jnp.pad(g_uq, ((0, 0), (0, 0), (0, QPAD - NOPE - ROPE))).transpose(1, 0, 2).reshape(1, Q_RANK, HEADS * QPAD)
        w_ukv = g_ukv.transpose(1, 0, 2).reshape(1, KV_RANK, HEADS * (NOPE + VDIM))
        return w_dkv, w_uq, w_ukv, g_o.reshape(1, HEADS * VDIM, D)

    mla_w = [None] * n_mla
    small_later = [a for l in range(1, n_mla) for a in mla_sh[l]] + [ln_local]
    ln_g_full, ln_b_full = [None] * n_sgu, [None] * n_sgu
    b_sp = sgu_b_spatial.reshape(n_sgu, SGU_GROUPS, SGU_CHUNK, 1)
    up_sh = [ffn_w_up[i].astype(BF16) for i in range(depth)]
    down_sh = [ffn_w_down[i].astype(BF16) for i in range(depth)]
    in_sh = [sgu_w_in[l].astype(BF16) for l in range(n_sgu)]
    out_sh = [sgu_w_out[l].astype(BF16) for l in range(n_sgu)]
    g_up, g_down, g_in, g_out = [None] * depth, [None] * depth, [None] * n_sgu, [None] * n_sgu

    inv_freq = ROPE_THETA ** (-jnp.arange(0, ROPE, 2, dtype=F32) / ROPE)
    zeros32 = jnp.zeros((ROPE // 2,), F32)
    inv128 = jnp.concatenate([inv_freq, inv_freq, zeros32, zeros32]).reshape(1, 128)
    sel_a = jnp.concatenate([-jnp.ones((32,), F32), zeros32, zeros32, zeros32]).reshape(1, 128)
    sel_b = jnp.concatenate([zeros32, jnp.ones((32,), F32), zeros32, zeros32]).reshape(1, 128)
    sel_c = jnp.concatenate([jnp.ones((64,), F32), zeros32, zeros32]).reshape(1, 128)

    def rope_tables(pos, inv, sa, sb, sc):
        ang = pos.astype(F32) * inv
        cs, sn = jnp.cos(ang), jnp.sin(ang)
        return cs * sc, sn * sa, sn * sb

    t_cc, t_sa, t_sb, *first_half = _rowwise(
        "rope_tables", rope_tables, [positions.reshape(T, 1), inv128, sel_a, sel_b, sel_c], grid=(nt,),
        in_specs=[_row_spec(tm, 1)] + [_const_spec((1, 128))] * 4,
        out_shapes=[_sds((T, 128), F32)] * 3, out_specs=[_row_spec(tm, 128)] * 3, comm=_gather_level1(mla_sh[0]))
    tab_specs = [_row_spec(tm, 128)] * 3

    def rmsnorm(xv, g, comm):
        return _rowwise("rmsnorm", lambda a, gg: _rms_fwd(a, gg), [xv, g.reshape(1, D)], grid=(nt,),
                        in_specs=[_row_spec(tm, D), _const_spec((1, D))], out_shapes=[_sds((T, D), BF16)], out_specs=[_row_spec(tm, D)], comm=comm)

    def proj_cols(name, h, gw, nc, epilogue, n_out, comm=None):
        return _matmul(name, h, gw, [], grid=(N_DEV, T // tb),
                       a_spec=pl.BlockSpec((tb, D), lambda j, i: (i, 0)),
                       b_spec=pl.BlockSpec((None, D, nc), lambda j, i: (j, 0, 0)), extra_specs=[],
                       out_shapes=[_sds((T, nc * N_DEV), BF16)] * n_out, out_specs=[pl.BlockSpec((tb, nc), lambda j, i: (i, j))] * n_out,
                       dims=NN, epilogue=epilogue, comm=comm)

    def residual_norm(acc, xr, g):
        xn = acc + xr
        return xn, _rms_fwd(xn, g)

    def proj_rows_residual(name, a, gw, xres, g_next, comm=None):
        kk_ = a.shape[1]
        return _matmul(name, a, gw.reshape(kk_, D), [xres, g_next.reshape(1, D)], grid=(T // tk,),
                       a_spec=_row_spec(tk, kk_), b_spec=_const_spec((kk_, D)), extra_specs=[_row_spec(tk, D), _const_spec((1, D))],
                       out_shapes=[_sds((T, D), F32), _sds((T, D), BF16)], out_specs=[_row_spec(tk, D)] * 2,
                       dims=NN, epilogue=residual_norm, comm=comm)

    def back_rows(name, dy, gw, kc, extras, epilogue, comm=None):
        return _matmul(name, dy, gw, extras, grid=(N_DEV, T // tb),
                       a_spec=pl.BlockSpec((tb, D), lambda j, i: (i, 0)),
                       b_spec=pl.BlockSpec((None, kc, D), lambda j, i: (j, 0, 0)),
                       extra_specs=[pl.BlockSpec((tb, kc), lambda j, i: (i, j))] * len(extras),
                       out_shapes=[_sds((T, kc * N_DEV), BF16)], out_specs=[pl.BlockSpec((tb, kc), lambda j, i: (i, j))],
                       dims=NT, epilogue=epilogue, comm=comm)

    def norm_bwd_epilogue(dh, xv, g, dxi):
        dxn, dg = _rms_bwd(xv, g, dh)
        return dxi + dxn, dxi + dxn, dg

    def transposed(gw):
        return gw.transpose(0, 2, 1).reshape(gw.shape[0] * gw.shape[2], D)

    def back_cols(name, da, gwt, xv, g, dx_in, comm=None):
        n = da.shape[1]
        row = _row_spec(tk, D)
        return _matmul(name, da, gwt, [xv, g.reshape(1, D), dx_in], grid=(T // tk,),
                       a_spec=_row_spec(tk, n), b_spec=_const_spec((n, D)), extra_specs=[row, _const_spec((1, D)), row],
                       out_shapes=[_sds((T, D), F32), _sds((T, D), BF16), _sds((1, D), F32)], out_specs=[row, row, _const_spec((1, D))],
                       dims=NN, epilogue=norm_bwd_epilogue, n_sum=1, comm=comm)

    def token_sum(tt):
        return dict(k_axis=1, nk=T // tt) if T // tt > 1 else dict(k_axis=None)

    def wgrad_cols(name, h, da, nc, comm=None):
        res = _matmul(name, h, da, [], grid=(N_DEV, T // tb),
                       a_spec=pl.BlockSpec((tb, D), lambda j, t: (t, 0)), b_spec=pl.BlockSpec((tb, nc), lambda j, t: (t, j)),
                       extra_specs=[], out_shapes=[_sds((N_DEV, D, nc), BF16)],
                       out_specs=[pl.BlockSpec((None, D, nc), lambda j, t: (j, 0, 0))],
                       dims=TN, acc_shape=(D, nc), comm=comm, **token_sum(tb))
        return res[0] if comm is None else res

    def wgrad_rows(name, a, dy, kc, ncols, tt, comm=None):
        res = _matmul(name, a, dy, [], grid=(a.shape[1] // kc, T // tt),
                      a_spec=pl.BlockSpec((tt, kc), lambda j, t: (t, j)), b_spec=pl.BlockSpec((tt, ncols), lambda j, t: (t, 0)),
                      extra_specs=[], out_shapes=[_sds((a.shape[1], ncols), BF16)],
                      out_specs=[pl.BlockSpec((kc, ncols), lambda j, t: (j, 0))],
                      dims=TN, acc_shape=(kc, ncols), comm=comm, **token_sum(tt))
        return res[0] if comm is None else res

    saved = []
    xs = x2
    for i in range(depth):
        l = i // 2
        if i == 0:
            h, *first_w = rmsnorm(xs, norm_mix[0], _gather_level2(first_half))
            mla_w[0] = mla_layouts(*first_w)
        if i % 2 == 0:
            w_dkv, w_uq, w_ukv, w_o = mla_w[l]
            lat = _matmul("mla_down", h, w_dkv, [], grid=(nt,), a_spec=_row_spec(tm, D),
                          b_spec=pl.BlockSpec((None, D, LAT_PAD), lambda i_: (0, 0, 0)), extra_specs=[],
                          out_shapes=[_sds((T, LAT_PAD), F32)], out_specs=[_row_spec(tm, LAT_PAD)], dims=NN)[0]

            def latent_post(la, qn, kvn, cc, sa, sb):
                cq = _rms_fwd(la[:, :Q_RANK], qn)
                ckv = _rms_fwd(la[:, Q_RANK:Q_RANK + KV_RANK], kvn)
                kr = _rope_fwd(la[:, Q_RANK + KV_RANK:], cc, sa, sb)
                return cq, ckv, kr

            cq, ckv, kr = _rowwise(
                "mla_latent", latent_post, [lat, mla_q_norm[l].reshape(1, Q_RANK), mla_kv_norm[l].reshape(1, KV_RANK), t_cc, t_sa, t_sb],
                grid=(nt,), in_specs=[_row_spec(tm, LAT_PAD), _const_spec((1, Q_RANK)), _const_spec((1, KV_RANK))] + tab_specs,
                out_shapes=[_sds((T, Q_RANK), BF16), _sds((T, KV_RANK), BF16), _sds((T, 128), BF16)],
                out_specs=[_row_spec(tm, Q_RANK), _row_spec(tm, KV_RANK), _row_spec(tm, 128)])

            def q_epilogue(acc, cc, sa, sb):
                parts = []
                for b in range(HEADS):
                    parts += [acc[:, b * QPAD:b * QPAD + NOPE], _rope_fwd(acc[:, b * QPAD + NOPE:(b + 1) * QPAD], cc, sa, sb)]
                return (jnp.concatenate(parts, axis=1),)

            q = _matmul("mla_q", cq, w_uq, [t_cc, t_sa, t_sb], grid=(nt,), a_spec=_row_spec(tm, Q_RANK),
                        b_spec=pl.BlockSpec((None, Q_RANK, HW), lambda i_: (0, 0, 0)), extra_specs=tab_specs,
                        out_shapes=[_sds((T, HW), BF16)], out_specs=[_row_spec(tm, HW)], dims=NN, epilogue=q_epilogue)[0]

            def kv_write(outs, acc, krb):
                k_ref, v_ref, vt_ref = outs
                for b in range(HEADS):
                    vb = acc[:, b * QPAD + NOPE:(b + 1) * QPAD]
                    k_ref[:, b * QPAD:b * QPAD + NOPE] = acc[:, b * QPAD:b * QPAD + NOPE].astype(BF16)
                    k_ref[:, b * QPAD + NOPE:(b + 1) * QPAD] = krb
                    v_ref[:, b * VDIM:(b + 1) * VDIM] = vb.astype(BF16)
                    vbt = vb.T.astype(BF16)
                    for u in range(tm // tq):
                        vt_ref[b, u] = vbt[:, u * tq:(u + 1) * tq]

            kk, vv, vt = _matmul("mla_kv", ckv, w_ukv, [kr], grid=(nt,), a_spec=_row_spec(tm, KV_RANK),
                                 b_spec=pl.BlockSpec((None, KV_RANK, HW), lambda i_: (0, 0, 0)), extra_specs=[_row_spec(tm, 128)],
                                 out_shapes=[_sds((T, HW), BF16), _sds((T, OW), BF16), _sds((HEADS, T // tq, VDIM, tq), BF16)],
                                 out_specs=[_row_spec(tm, HW), _row_spec(tm, OW), pl.BlockSpec((HEADS, tm // tq, VDIM, tq), lambda i_: (0, i_, 0, 0))],
                                 dims=NN, write=kv_write)
            group = [up_sh[i], down_sh[i], in_sh[l], out_sh[l]] + (small_later if i == 0 else [])
            o, lse, *bufs = _flash_fwd(q, kk, vt, tq, comm=_gather_level1(group))
            xm, h2, g_up[i], g_down[i] = _matmul(
                "mla_out", o, w_o, [xs, norm_ffn[i].reshape(1, D)], grid=(nt,), a_spec=_row_spec(tm, OW),
                b_spec=pl.BlockSpec((None, OW, D), lambda i_: (0, 0, 0)), extra_specs=[_row_spec(tm, D), _const_spec((1, D))],
                out_shapes=[_sds((T, D), F32), _sds((T, D), BF16)], out_specs=[_row_spec(tm, D)] * 2, dims=NN,
                epilogue=residual_norm, comm=_gather_level2(bufs[:2]))
            half_gathered = bufs[2:]
            mix_saved = (h, lat, cq, ckv, q, kk, vv, o, lse)
        else:
            gp, ge, g_down[i], up_half = proj_cols("sgu_in", h, g_in[l], e2c, _gelu_and_grad, 2,
                                                   comm=_merge_comm(_gather_level2([down_half]), _gather_level1([up_sh[i]])))
            gate = _sgu_mid_fwd(ge, ln_g_full[l], ln_b_full[l], sgu_w_spatial[l], b_sp[l], 4)
            xm, h2, g_up[i] = proj_rows_residual("sgu_out", gate, g_out[l], xs, norm_ffn[i], comm=_gather_level2([up_half]))
            mix_saved = (h, gp, ge, gate)
        r, s, *rest = proj_cols("ffn_up", h2, g_up[i], ffc, lambda acc: (jnp.maximum(acc, 0.0), jnp.square(jnp.maximum(acc, 0.0))), 2,
                                comm=_merge_comm(_gather_level2(half_gathered), _gather_level1([down_sh[i + 1]], part=(0, 2))) if i % 2 == 0 else None)
        if i % 2 == 0:
            g_in[l], g_out[l], *small_gathered, down_part = rest
        if i == 0:
            for l_ in range(1, n_mla):
                mla_w[l_] = mla_layouts(*small_gathered[4 * (l_ - 1):4 * l_])
            g_ln = small_gathered[-1]
            ln_g_full = [g_ln[:, l_, :].reshape(1, E) for l_ in range(n_sgu)]
            ln_b_full = [g_ln[:, n_sgu + l_, :].reshape(1, E) for l_ in range(n_sgu)]
        saved.append((xs, xm, mix_saved, h2, r, s))
        if i + 1 < depth:
            xs, h, *rest = proj_rows_residual("ffn_down", s, g_down[i], xm, norm_mix[i + 1],
                                              comm=_gather_level1([down_sh[i + 1]], part=(1, 2), into=[down_part]) if i % 2 == 0 else None)
            if i % 2 == 0:
                (down_half,) = rest

    def loss_head(acc, xr, tg, g):
        xv = acc + xr
        y = _rms_fwd(xv, g)
        err = y - tg
        part = 0.5 * jnp.sum(jnp.sum(err * err, axis=-1, keepdims=True), axis=0, keepdims=True) / D
        dxv, dg = _rms_bwd(xv, g, err / D)
        return dxv, dxv, jnp.broadcast_to(part, (1, 128)), dg

    dx, dyb, loss_part, d_final = _matmul(
        "ffn_down_loss", s, g_down[depth - 1].reshape(FF, D), [xm, tgt, final_norm.reshape(1, D)], grid=(T // tk,),
        a_spec=_row_spec(tk, FF), b_spec=_const_spec((FF, D)), extra_specs=[_row_spec(tk, D), _row_spec(tk, D), _const_spec((1, D))],
        out_shapes=[_sds((T, D), F32), _sds((T, D), BF16), _sds((1, 128), F32), _sds((1, D), F32)],
        out_specs=[_row_spec(tk, D), _row_spec(tk, D), _const_spec((1, 128)), _const_spec((1, D))], dims=NN, epilogue=loss_head, n_sum=2)
    loss = lax.psum(loss_part[0, 0], ("x", "y", "c"))

    d_norm_mix, d_norm_ffn = [None] * depth, [None] * depth
    d_qn, d_kvn = [None] * n_mla, [None] * n_mla
    d_wsp, d_bsp, d_lng, d_lnb = [None] * n_sgu, [None] * n_sgu, [None] * n_sgu, [None] * n_sgu
    layers = {"dkv": n_mla, "uq": n_mla, "ukv": n_mla, "o": n_mla, "in": n_sgu, "out": n_sgu, "up": depth, "down": depth}
    stacked = {nm: None for nm in layers}
    pending = []
    summed = []

    def add_pairs(gs, rcvs):
        operands, in_specs, out_shapes, out_specs = [], [], [], []
        for g, rcv in zip(gs, rcvs):
            _, rws, cls = g.shape
            slab = pl.BlockSpec((None, rws, cls), lambda ch, cr: (ch, 0, 0))
            operands += [g.reshape(N_CHIP, 2, rws, cls), rcv]
            in_specs += [pl.BlockSpec((None, None, rws, cls), lambda ch, cr: (ch, cr[0], 0, 0)), slab]
            out_shapes.append(_sds(rcv.shape, BF16))
            out_specs.append(slab)

        def fn(*blocks):
            return tuple(blocks[2 * k].astype(F32) + blocks[2 * k + 1].astype(F32) for k in range(len(gs)))

        return _rowwise("grad_pair_sum", fn, operands, grid=(N_CHIP,), in_specs=in_specs, out_shapes=out_shapes, out_specs=out_specs,
                        grid_spec_prefetch=cidx)

    def sibling_comm():
        return _sibling_exchange([g for _, _, g in pending]) if pending else None

    def absorb(from_sibling):
        if pending:
            parts = add_pairs([g for _, _, g in pending], list(from_sibling))
            summed.extend((nm, l_, p) for (nm, l_, _), p in zip(pending, parts))
            pending.clear()

    def chip_comm():
        if pending:
            absorb(_comm_call("grad_sibling_exchange", sibling_comm()))
        comm, names = _chip_exchange([p for _, _, p in summed], [(nm, l_) for nm, l_, _ in summed], layers, stacked)
        summed.clear()
        return comm, names

    def rows128(a, rows):
        flat = a.reshape(-1, 128)
        return jnp.pad(flat, ((0, rows - flat.shape[0]), (0, 0)))

    def pad_to(n, mult):
        return -(-n // mult) * mult

    def packed(arrs, sizes):
        return jnp.concatenate([rows128(a, sz) for a, sz in zip(arrs, sizes)], axis=0)

    n_wsp, n_bsp, n_ln = sgu_w_spatial.size // 128, pad_to(sgu_b_spatial.size // 128, 8), pad_to(n_sgu * E // 128, 8)
    early_sizes = [n_wsp, pad_to(n_wsp + n_bsp, SMALL_ROWS) - n_wsp, n_ln, n_ln]
    early_rep = early_sizes[0] + early_sizes[1]
    gathered_early = None

    for i in reversed(range(depth)):
        l = i // 2
        xs_i, xm, mix_saved, h2, r, s = saved[i]
        comm = sibling_comm()
        if i == 0:
            comm = _gather_level2([early_half]) if comm is None else _merge_comm(comm, _gather_level2([early_half]))
        da, *rcv = back_rows("ffn_down_bwd", dyb, g_down[i], ffc, [r], lambda acc, rr: (acc * (2.0 * rr.astype(F32)),), comm=comm)
        if i == 0:
            *rcv, gathered_early = rcv
        absorb(rcv)
        pending.append(("down", i, wgrad_rows("ffn_down_wgrad", s, dyb, ffc, D, tb).reshape(N_DEV, ffc, D)))
        pending.append(("up", i, wgrad_cols("ffn_up_wgrad", h2, da, ffc)))
        dx, dyb, d_norm_ffn[i], *rcv = back_cols("ffn_up_bwd", da, transposed(g_up[i]), xm, norm_ffn[i], dx, comm=sibling_comm())
        absorb(rcv)
        if i % 2 == 0:
            h, lat, cq, ckv, q, kk, vv, o, lse = mix_saved
            w_dkv, w_uq, w_ukv, w_o = mla_w[l]
            do = _matmul("mla_out_bwd", dyb, w_o, [], grid=(nt,), a_spec=_row_spec(tm, D),
                         b_spec=pl.BlockSpec((None, OW, D), lambda i_: (0, 0, 0)), extra_specs=[],
                         out_shapes=[_sds((T, OW), BF16)], out_specs=[_row_spec(tm, OW)], dims=NT)[0]
            g_o_l = wgrad_rows("mla_out_wgrad", o, dyb, OW, D, tm).reshape(N_DEV, owc, D)
            comm, names = chip_comm()
            dq_pre, dk, dv, *bufs = _flash_bwd(q, kk, vv, o, do, lse, (t_cc, t_sa, t_sb), tq, comm=comm)
            stacked.update(dict(zip(names, bufs)))
            pending.append(("o", l, g_o_l))

            def kv_pre(dkb, dvb, cc, sa, sb):
                parts, dkr = [], None
                for b in range(HEADS):
                    parts += [dkb[:, b * QPAD:b * QPAD + NOPE], dvb[:, b * VDIM:(b + 1) * VDIM]]
                    piece = dkb[:, b * QPAD + NOPE:(b + 1) * QPAD].astype(F32)
                    dkr = piece if dkr is None else dkr + piece
                return jnp.concatenate(parts, axis=1), _rope_bwd(dkr, cc, sa, sb)

            dkv, dkr = _rowwise("mla_dkv_rope", kv_pre, [dk, dv, t_cc, t_sa, t_sb], grid=(T // ts,),
                                in_specs=[_row_spec(ts, HW), _row_spec(ts, OW)] + [_row_spec(ts, 128)] * 3,
                                out_shapes=[_sds((T, HW), BF16), _sds((T, 128), F32)], out_specs=[_row_spec(ts, HW), _row_spec(ts, 128)])
            g_uq_l = wgrad_rows("mla_q_wgrad", cq, dq_pre, Q_RANK, HW, tm)
            g_ukv_l = wgrad_rows("mla_kv_wgrad", ckv, dkv, KV_RANK, HW, tm)
            pending.append(("uq", l, g_uq_l.reshape(Q_RANK, HEADS, QPAD)[:, :, :NOPE + ROPE].transpose(1, 0, 2)))
            pending.append(("ukv", l, g_ukv_l.reshape(KV_RANK, HEADS, NOPE + VDIM).transpose(1, 0, 2)))
            dcq = _matmul("mla_q_bwd", dq_pre, w_uq, [], grid=(nt,), a_spec=_row_spec(tm, HW),
                          b_spec=pl.BlockSpec((None, Q_RANK, HW), lambda i_: (0, 0, 0)), extra_specs=[],
                          out_shapes=[_sds((T, Q_RANK), F32)], out_specs=[_row_spec(tm, Q_RANK)], dims=NT)[0]
            dckv = _matmul("mla_kv_bwd", dkv, w_ukv, [], grid=(nt,), a_spec=_row_spec(tm, HW),
                           b_spec=pl.BlockSpec((None, KV_RANK, HW), lambda i_: (0, 0, 0)), extra_specs=[],
                           out_shapes=[_sds((T, KV_RANK), F32)], out_specs=[_row_spec(tm, KV_RANK)], dims=NT)[0]

            def latent_bwd(la, qn, kvn, dq_, dkv_, dkr_):
                dcq_raw, dqn = _rms_bwd(la[:, :Q_RANK], qn, dq_)
                dckv_raw, dkvn = _rms_bwd(la[:, Q_RANK:Q_RANK + KV_RANK], kvn, dkv_)
                return jnp.concatenate([dcq_raw, dckv_raw, dkr_], axis=1), dqn, dkvn

            dlat, d_qn[l], d_kvn[l] = _rowwise(
                "mla_latent_bwd", latent_bwd, [lat, mla_q_norm[l].reshape(1, Q_RANK), mla_kv_norm[l].reshape(1, KV_RANK), dcq, dckv, dkr],
                grid=(nt,), in_specs=[_row_spec(tm, LAT_PAD), _const_spec((1, Q_RANK)), _const_spec((1, KV_RANK)),
                                      _row_spec(tm, Q_RANK), _row_spec(tm, KV_RANK), _row_spec(tm, 128)],
                out_shapes=[_sds((T, LAT_PAD), BF16), _sds((1, Q_RANK), F32), _sds((1, KV_RANK), F32)],
                out_specs=[_row_spec(tm, LAT_PAD), _const_spec((1, Q_RANK)), _const_spec((1, KV_RANK))], n_acc=2)
            g_dkv_l = wgrad_rows("mla_down_wgrad", h, dlat, D, LAT_PAD, tm)
            pending.append(("dkv", l, g_dkv_l[:, :LAT].reshape(N_DEV, dc, LAT)))
            dx, dyb, d_norm_mix[i] = _matmul(
                "mla_down_bwd", dlat, w_dkv, [xs_i, norm_mix[i].reshape(1, D), dx], grid=(nt,), a_spec=_row_spec(tm, LAT_PAD),
                b_spec=pl.BlockSpec((None, D, LAT_PAD), lambda i_: (0, 0, 0)), extra_specs=[_row_spec(tm, D), _const_spec((1, D)), _row_spec(tm, D)],
                out_shapes=[_sds((T, D), F32), _sds((T, D), BF16), _sds((1, D), F32)],
                out_specs=[_row_spec(tm, D), _row_spec(tm, D), _const_spec((1, D))], dims=NT, epilogue=norm_bwd_epilogue, n_sum=1)
        else:
            h, gp, ge, gate = mix_saved
            (dgate,) = back_rows("sgu_out_bwd", dyb, g_out[l], ec, [], None)
            pending.append(("out", l, wgrad_rows("sgu_out_wgrad", gate, dyb, ec, D, tb).reshape(N_DEV, ec, D)))
            dz, d_wsp[l], d_bsp[l], d_lng[l], d_lnb[l] = _sgu_mid_bwd(ge, gp, dgate, ln_g_full[l], ln_b_full[l], sgu_w_spatial[l], b_sp[l], 2)
            if i == 1:
                early = packed([jnp.stack(d_wsp, 0), jnp.stack(d_bsp, 0), jnp.concatenate(d_lng, 0), jnp.concatenate(d_lnb, 0)], early_sizes)
                g_in_l, early_part = wgrad_cols("sgu_in_wgrad", h, dz, e2c, comm=_gather_level1([early], part=(0, 2)))
            else:
                g_in_l = wgrad_cols("sgu_in_wgrad", h, dz, e2c)
            pending.append(("in", l, g_in_l))
            comm = sibling_comm()
            if i == 1:
                comm = _merge_comm(comm, _gather_level1([early], part=(1, 2), into=[early_part]))
            dx, dyb, d_norm_mix[i], *rcv = back_cols("sgu_in_bwd", dz, transposed(g_in[l]), xs_i, norm_mix[i], dx, comm=comm)
            if i == 1:
                *rcv, early_half = rcv
            absorb(rcv)
    grad_x = dx.reshape(1, T, D)

    last_comm, last_names = chip_comm()
    late_g = [jnp.concatenate(d_norm_mix, 0), jnp.concatenate(d_norm_ffn, 0), d_final, jnp.concatenate(d_qn, 0), jnp.concatenate(d_kvn, 0)]
    late_w = [norm_mix, norm_ffn, final_norm, mla_q_norm, mla_kv_norm]
    late_m = [m_norm_mix, m_norm_ffn, m_final_norm, m_mla_q_norm, m_mla_kv_norm]
    late_v = [v_norm_mix, v_norm_ffn, v_final_norm, v_mla_q_norm, v_mla_kv_norm]
    late_sizes = [pad_to(g.size // 128, 8) for g in late_g]
    late_rows = sum(late_sizes)

    def adam_big(parts, w, m, v):
        lyr, rws, cls = w.shape
        rt = _tile(rws, 512)

        def fn(p, w_, m_, v_):
            g = (p[0].astype(F32) + p[1].astype(F32)) + (p[2].astype(F32) + p[3].astype(F32))
            return (g, *_adam(w_, g, m_, v_))

        spec = pl.BlockSpec((None, rt, cls), lambda l_, i_: (l_, i_, 0))
        return _rowwise("adam_large", fn, [parts, w, m, v], grid=(lyr, rws // rt),
                        in_specs=[pl.BlockSpec((N_CHIP, None, rt, cls), lambda l_, i_: (0, l_, i_, 0)), spec, spec, spec],
                        out_shapes=[_sds(w.shape, F32)] * 4, out_specs=[spec] * 4)

    stacked.update(dict(zip(last_names, _comm_call("grad_chip_exchange", last_comm))))
    (gathered_late,) = _all_gather("gather_small_grads", [packed(late_g, late_sizes)])
    big = {}
    big["in"] = adam_big(stacked["in"], sgu_w_in, m_sgu_w_in, v_sgu_w_in)
    big["up"] = adam_big(stacked["up"], ffn_w_up, m_ffn_w_up, v_ffn_w_up)
    big["down"] = adam_big(stacked["down"], ffn_w_down, m_ffn_w_down, v_ffn_w_down)
    big["out"] = adam_big(stacked["out"], sgu_w_out, m_sgu_w_out, v_sgu_w_out)
    big["dkv"] = adam_big(stacked["dkv"], mla_w_dkv, m_mla_w_dkv, v_mla_w_dkv)
    big["uq"] = adam_big(stacked["uq"], mla_w_uq, m_mla_w_uq, v_mla_w_uq)
    big["ukv"] = adam_big(stacked["ukv"], mla_w_ukv, m_mla_w_ukv, v_mla_w_ukv)
    big["o"] = adam_big(stacked["o"], mla_w_o, m_mla_w_o, v_mla_w_o)
    big_res = [big[nm][:4] for nm in ("dkv", "uq", "ukv", "o", "in", "out", "up", "down")]

    def sum8(p):
        return ((p[0] + p[1]) + (p[2] + p[3])) + ((p[4] + p[5]) + (p[6] + p[7]))

    def adam_packed(name, gathered, ws, ms, vs, sizes, rows, tile):
        spec = _row_spec(tile, 128)
        return _rowwise(name, lambda p, w_, m_, v_: (sum8(p), *_adam(w_, sum8(p), m_, v_)),
                        [gathered, packed(ws, sizes), packed(ms, sizes), packed(vs, sizes)], grid=(rows // tile,),
                        in_specs=[pl.BlockSpec((N_DEV, tile, 128), lambda i_: (0, i_, 0)), spec, spec, spec],
                        out_shapes=[_sds((rows, 128), F32)] * 4, out_specs=[spec] * 4)

    late_res = adam_packed("adam_small", gathered_late, late_w, late_m, late_v, late_sizes, late_rows, late_rows)
    early_res = adam_packed("adam_spatial", gathered_early, [sgu_w_spatial, sgu_b_spatial], [m_sgu_w_spatial, m_sgu_b_spatial],
                            [v_sgu_w_spatial, v_sgu_b_spatial], early_sizes[:2], early_rep, SMALL_ROWS)

    def unpack(res, sizes, k, like):
        off = sum(sizes[:k])
        return res[off:off + like.size // 128].reshape(like.shape)

    my_b = 4 * lax.axis_index("x") + 2 * lax.axis_index("y") + lax.axis_index("c")
    ln_w = jnp.concatenate([sgu_ln_g, sgu_ln_b], 0)
    ln_m = jnp.concatenate([m_sgu_ln_g, m_sgu_ln_b], 0)
    ln_v = jnp.concatenate([v_sgu_ln_g, v_sgu_ln_b], 0)
    ln_all = jnp.concatenate([gathered_early[:, early_rep:early_rep + n_sgu * E // 128], gathered_early[:, early_rep + n_ln:early_rep + n_ln + n_sgu * E // 128]], axis=1)
    ln_mine = lax.dynamic_slice_in_dim(ln_all.reshape(N_DEV, 2 * n_sgu, N_DEV, ec), my_b, 1, axis=2).reshape(N_DEV, 2 * n_sgu, ec)
    ln_g_, ln_d, ln_m2, ln_v2 = _rowwise(
        "adam_ln", lambda p, w_, m_, v_: (sum8(p), *_adam(w_, sum8(p), m_, v_)), [ln_mine, ln_w, ln_m, ln_v], grid=(1,),
        in_specs=[_const_spec(ln_mine.shape), _const_spec(ln_w.shape), _const_spec(ln_w.shape), _const_spec(ln_w.shape)],
        out_shapes=[_sds(ln_w.shape, F32)] * 4, out_specs=[_const_spec(ln_w.shape)] * 4)

    def family(pos):
        ln = [ln_g_, ln_d, ln_m2, ln_v2][pos]
        late = [unpack(late_res[pos], late_sizes, k, w_) for k, w_ in enumerate(late_w)]
        w_sp_, b_sp_ = unpack(early_res[pos], early_sizes, 0, sgu_w_spatial), unpack(early_res[pos], early_sizes, 1, sgu_b_spatial)
        bigs = [res[pos] for res in big_res]
        return [late[0], late[1], late[2], bigs[0], late[3], late[4], bigs[1], bigs[2], bigs[3],
                bigs[4], ln[:n_sgu], ln[n_sgu:], w_sp_, b_sp_, bigs[5], bigs[6], bigs[7]]

    return (loss, grad_x, *family(0), *family(1), *family(2), *family(3))
```

```python
import math

import jax
import jax.numpy as jnp
from jax import lax
from jax.experimental import pallas as pl
from jax.experimental.pallas import tpu as pltpu

F32 = jnp.float32
BF16 = jnp.bfloat16
MESH = pl.DeviceIdType.MESH

N_DEV = 8
N_CHIP = 4
HEADS = 8
NOPE = 128
ROPE = 64
VDIM = 128
QPAD = 256
Q_RANK = 256
KV_RANK = 128
LAT = Q_RANK + KV_RANK + ROPE
LAT_PAD = 512
ROPE_THETA = 10000.0
SGU_CHUNK = 128
SGU_GROUPS = 8
NORM_EPS = 1e-6
LN_EPS = 1e-5
ADAM_LR = 0.001
ADAM_B1 = 0.9
ADAM_B2 = 0.999
ADAM_EPS = 1e-08
ADAM_WD = 0.01
ADAM_STEP = 10
ATTN_SCALE = (NOPE + ROPE) ** -0.5
NEG = -1e30
EXP2_SCALE = ATTN_SCALE * math.log2(math.e)
VMEM_LIMIT = 56 * 1024 * 1024
SMALL_ROWS = 256

NN = (((1,), (0,)), ((), ()))
NT = (((1,), (1,)), ((), ()))
TN = (((0,), (0,)), ((), ()))
ANY = pl.BlockSpec(memory_space=pl.ANY)


def _pcall(body, **kw):
    return pl.pallas_call(body, **kw)


def _params(n_grid, side_effects=False):
    return pltpu.CompilerParams(dimension_semantics=("arbitrary",) * n_grid, vmem_limit_bytes=VMEM_LIMIT, has_side_effects=side_effects)


def _sds(shape, dtype):
    return jax.ShapeDtypeStruct(tuple(shape), dtype)


def _tile(n, want):
    t = min(n, want)
    assert n % t == 0, (n, want)
    return t


class _Comm:
    def __init__(self, operands, out_shapes, aliases, scratch, start, finish):
        self.operands, self.out_shapes, self.aliases, self.scratch = operands, out_shapes, aliases, scratch
        self.start, self.finish = start, finish


def _merge_comm(first, second):
    n_in, n_out, n_sc = len(first.operands), len(first.out_shapes), len(first.scratch)
    aliases = dict(first.aliases)
    aliases.update({n_in + k: n_out + v for k, v in second.aliases.items()})

    def start(ins, outs, sems):
        first.start(ins[:n_in], outs[:n_out], sems[:n_sc])
        second.start(ins[n_in:], outs[n_out:], sems[n_sc:])

    def finish(ins, outs, sems):
        first.finish(ins[:n_in], outs[:n_out], sems[:n_sc])
        second.finish(ins[n_in:], outs[n_out:], sems[n_sc:])

    return _Comm([*first.operands, *second.operands], [*first.out_shapes, *second.out_shapes], aliases,
                 [*first.scratch, *second.scratch], start, finish)


def _place():
    return lax.axis_index("x"), lax.axis_index("y"), lax.axis_index("c")


def _other_chips(x, y):
    return [(1 - x, y), (x, 1 - y), (1 - x, 1 - y)]


def _dev_index(dev):
    return 4 * dev[0] + 2 * dev[1] + dev[2]


def _comm_call(name, comm):
    c_in, c_out = len(comm.operands), len(comm.out_shapes)

    def body(*refs):
        ins, outs, sems = refs[:c_in], refs[c_in:c_in + c_out], refs[c_in + c_out:]
        comm.start(ins, outs, sems)
        comm.finish(ins, outs, sems)

    return _pcall(body, name=name, in_specs=[ANY] * c_in, out_specs=[ANY] * c_out, out_shape=comm.out_shapes,
                  scratch_shapes=comm.scratch, input_output_aliases=dict(comm.aliases),
                  compiler_params=pltpu.CompilerParams(has_side_effects=True))(*comm.operands)


def _call(name, body, operands, in_specs, out_shapes, out_specs, scratch, grid, comm=None):
    if comm is None:
        return _pcall(body, name=name, grid=grid, in_specs=in_specs, out_specs=out_specs, out_shape=out_shapes,
                      scratch_shapes=scratch, compiler_params=_params(len(grid)))(*operands)
    n_in, n_out, n_sc = len(operands), len(out_shapes), len(scratch)
    c_in, c_out = len(comm.operands), len(comm.out_shapes)

    def hosted(*refs):
        ins, cins = refs[:n_in], refs[n_in:n_in + c_in]
        o0 = n_in + c_in
        outs, couts = refs[o0:o0 + n_out], refs[o0 + n_out:o0 + n_out + c_out]
        rest = refs[o0 + n_out + c_out:]
        sc, csems = rest[:n_sc], rest[n_sc:]
        first = pl.program_id(0) == 0
        last = pl.program_id(0) == grid[0] - 1
        for d in range(1, len(grid)):
            first = jnp.logical_and(first, pl.program_id(d) == 0)
            last = jnp.logical_and(last, pl.program_id(d) == grid[d] - 1)

        @pl.when(first)
        def _():
            comm.start(cins, couts, csems)

        body(*ins, *outs, *sc)

        @pl.when(last)
        def _():
            comm.finish(cins, couts, csems)

    return _pcall(hosted, name=name, grid=grid, in_specs=[*in_specs, *[ANY] * c_in], out_specs=[*out_specs, *[ANY] * c_out],
                  out_shape=[*out_shapes, *comm.out_shapes], scratch_shapes=[*scratch, *comm.scratch],
                  input_output_aliases={n_in + k: n_out + v for k, v in comm.aliases.items()},
                  compiler_params=_params(len(grid), side_effects=True))(*operands, *comm.operands)


def _gather_level1(shards, part=(0, 1), into=None):
    n = len(shards)
    k_part, m_part = part

    def copies(ins, outs, sems):
        send_sems, recv_sems, local_sems = sems
        x, y, c = _place()
        me, sibling = (x, y, c), (x, y, 1 - c)
        chips = _other_chips(x, y)

        def rows(a):
            r = shards[a].shape[0] // m_part
            return pl.ds(k_part * r, r)

        def copy(a, k, block, to, own=False):
            slot = outs[a].at[_dev_index(block), rows(a)]
            return pltpu.make_async_remote_copy(src_ref=ins[a].at[rows(a)] if own else slot, dst_ref=slot, send_sem=send_sems.at[a, k],
                                                recv_sem=recv_sems.at[a, k], device_id=to, device_id_type=MESH)

        mine = [pltpu.make_async_copy(ins[a].at[rows(a)], outs[a].at[_dev_index(me), rows(a)], local_sems.at[a]) for a in range(n)]
        sends = [copy(a, 1 + j, me, (*chip, c), own=True) for j, chip in enumerate(chips) for a in range(n)]
        sends += [copy(a, 0, me, sibling, own=True) for a in range(n)]
        recvs = [copy(a, 1 + j, (*chip, c), me) for j, chip in enumerate(chips) for a in range(n)]
        recvs += [copy(a, 0, sibling, me) for a in range(n)]
        return mine, sends, recvs

    def start(ins, outs, sems):
        mine, sends, _ = copies(ins, outs, sems)
        for cp in mine + sends:
            cp.start()

    def finish(ins, outs, sems):
        mine, sends, recvs = copies(ins, outs, sems)
        for cp in recvs:
            cp.wait_recv()
        for cp in sends:
            cp.wait_send()
        for cp in mine:
            cp.wait()

    return _Comm([*shards, *(into or [])], [_sds((N_DEV, *a.shape), a.dtype) for a in shards], {n + a: a for a in range(n)} if into else {},
                 [pltpu.SemaphoreType.DMA((n, 4)), pltpu.SemaphoreType.DMA((n, 4)), pltpu.SemaphoreType.DMA((n,))], start, finish)


def _gather_level2(bufs):
    n = len(bufs)

    def copies(outs, sems):
        send_sems, recv_sems = sems
        x, y, c = _place()
        sibling = (x, y, 1 - c)
        sends, recvs = [], []
        for j, chip in enumerate(_other_chips(x, y)):
            for a in range(n):
                have, want = outs[a].at[_dev_index((*chip, c))], outs[a].at[_dev_index((*chip, 1 - c))]
                sends.append(pltpu.make_async_remote_copy(src_ref=have, dst_ref=have, send_sem=send_sems.at[a, j], recv_sem=recv_sems.at[a, j],
                                                          device_id=sibling, device_id_type=MESH))
                recvs.append(pltpu.make_async_remote_copy(src_ref=want, dst_ref=want, send_sem=send_sems.at[a, j], recv_sem=recv_sems.at[a, j],
                                                          device_id=sibling, device_id_type=MESH))
        return sends, recvs

    def start(ins, outs, sems):
        for cp in copies(outs, sems)[0]:
            cp.start()

    def finish(ins, outs, sems):
        sends, recvs = copies(outs, sems)
        for cp in recvs:
            cp.wait_recv()
        for cp in sends:
            cp.wait_send()

    return _Comm(bufs, [_sds(b.shape, b.dtype) for b in bufs], {a: a for a in range(n)},
                 [pltpu.SemaphoreType.DMA((n, 3)), pltpu.SemaphoreType.DMA((n, 3))], start, finish)


def _all_gather(name, arrays):
    n = len(arrays)

    def body(*refs):
        ins = refs[:n]
        outs = refs[n:2 * n]
        send_sems, recv_sems, local_sems = refs[2 * n:]
        x, y, c = _place()
        me, sibling = (x, y, c), (x, y, 1 - c)
        chips = _other_chips(x, y)

        def copy(a, k, block, to, src=None):
            slot = outs[a].at[_dev_index(block)]
            return pltpu.make_async_remote_copy(src_ref=slot if src is None else src, dst_ref=slot, send_sem=send_sems.at[a, k],
                                                recv_sem=recv_sems.at[a, k], device_id=to, device_id_type=MESH)

        mine = [pltpu.make_async_copy(ins[a], outs[a].at[_dev_index(me)], local_sems.at[a]) for a in range(n)]
        for cp in mine:
            cp.start()
        first = []
        for j, chip in enumerate(chips):
            first += [copy(a, 1 + j, me, (*chip, c), src=ins[a]) for a in range(n)]
        first += [copy(a, 0, me, sibling, src=ins[a]) for a in range(n)]
        for cp in first:
            cp.start()
        passed = []
        for j, chip in enumerate(chips):
            for a in range(n):
                copy(a, 1 + j, (*chip, c), me).wait_recv()
                fwd = copy(a, 4 + j, (*chip, c), sibling)
                fwd.start()
                passed.append(fwd)
        for a in range(n):
            copy(a, 0, sibling, me).wait_recv()
            for j, chip in enumerate(chips):
                copy(a, 4 + j, (*chip, 1 - c), me).wait_recv()
        for cp in first + passed:
            cp.wait_send()
        for cp in mine:
            cp.wait()

    return _pcall(
        body, name=name, in_specs=[ANY] * n, out_specs=[ANY] * n,
        out_shape=[_sds((N_DEV, *a.shape), a.dtype) for a in arrays],
        scratch_shapes=[pltpu.SemaphoreType.DMA((n, 7)), pltpu.SemaphoreType.DMA((n, 7)), pltpu.SemaphoreType.DMA((n,))],
        compiler_params=pltpu.CompilerParams(has_side_effects=True),
    )(*arrays)


def _sibling_exchange(grads):
    n = len(grads)

    def start(ins, outs, sems):
        send_sems, recv_sems = sems
        x, y, c = _place()
        for a in range(n):
            for ch in range(N_CHIP):
                pltpu.make_async_remote_copy(src_ref=ins[a].at[2 * ch + 1 - c], dst_ref=outs[a].at[ch], send_sem=send_sems.at[a],
                                             recv_sem=recv_sems.at[a], device_id=(x, y, 1 - c), device_id_type=MESH).start()

    def finish(ins, outs, sems):
        send_sems, recv_sems = sems
        x, y, c = _place()
        for a in range(n):
            pltpu.make_async_remote_copy(src_ref=outs[a], dst_ref=outs[a], send_sem=send_sems.at[a], recv_sem=recv_sems.at[a],
                                         device_id=(x, y, 1 - c), device_id_type=MESH).wait()

    return _Comm(grads, [_sds((N_CHIP, *g.shape[1:]), g.dtype) for g in grads], {},
                 [pltpu.SemaphoreType.DMA((n,)), pltpu.SemaphoreType.DMA((n,))], start, finish)


def _chip_exchange(parts, slots, layers, stacked):
    n = len(parts)
    names = []
    for nm, _ in slots:
        if nm not in names:
            names.append(nm)
    shapes = {nm: _sds((N_CHIP, layers[nm], *parts[a].shape[1:]), parts[a].dtype) for a, (nm, _) in enumerate(slots)}
    kept = [nm for nm in names if stacked.get(nm) is not None]
    aliases = {n + k: names.index(nm) for k, nm in enumerate(kept)}

    def copies(ins, outs, sems):
        send_sems, recv_sems, local_sems = sems
        x, y, c = _place()
        mine = 2 * x + y
        local, sends, recvs = [], [], []
        for a, (nm, l) in enumerate(slots):
            buf = outs[names.index(nm)]
            local.append(pltpu.make_async_copy(ins[a].at[mine], buf.at[mine, l], local_sems.at[a]))
            for j, chip in enumerate(_other_chips(x, y)):
                theirs = buf.at[2 * chip[0] + chip[1], l]
                sends.append(pltpu.make_async_remote_copy(src_ref=ins[a].at[2 * chip[0] + chip[1]], dst_ref=buf.at[mine, l], send_sem=send_sems.at[a, j],
                                                          recv_sem=recv_sems.at[a, j], device_id=(*chip, c), device_id_type=MESH))
                recvs.append(pltpu.make_async_remote_copy(src_ref=theirs, dst_ref=theirs, send_sem=send_sems.at[a, j],
                                                          recv_sem=recv_sems.at[a, j], device_id=(*chip, c), device_id_type=MESH))
        return local, sends, recvs

    def start(ins, outs, sems):
        local, sends, _ = copies(ins, outs, sems)
        for cp in local + sends:
            cp.start()

    def finish(ins, outs, sems):
        local, sends, recvs = copies(ins, outs, sems)
        for cp in recvs:
            cp.wait_recv()
        for cp in sends:
            cp.wait_send()
        for cp in local:
            cp.wait()

    comm = _Comm([*parts, *[stacked[nm] for nm in kept]], [shapes[nm] for nm in names], aliases,
                 [pltpu.SemaphoreType.DMA((n, 3)), pltpu.SemaphoreType.DMA((n, 3)), pltpu.SemaphoreType.DMA((n,))], start, finish)
    return comm, names


def _matmul(name, a, b, extras, *, grid, a_spec, b_spec, extra_specs, out_shapes, out_specs, dims, k_axis=None, nk=1,
            acc_shape=None, epilogue=None, comm=None, n_sum=0, write=None):
    n_extra = len(extras)
    n_out = len(out_shapes)

    def body(*refs):
        a_ref, b_ref = refs[0], refs[1]
        ex = refs[2:2 + n_extra]
        outs = refs[2 + n_extra:2 + n_extra + n_out]
        prod = lax.dot_general(a_ref[...], b_ref[...], dims, preferred_element_type=F32)

        def finish(acc):
            if write is not None:
                write(outs, acc, *[e[...] for e in ex])
                return
            res = epilogue(acc, *[e[...] for e in ex]) if epilogue is not None else (acc,)
            first = None
            for d in range(len(grid)):
                if d != k_axis:
                    here = pl.program_id(d) == 0
                    first = here if first is None else jnp.logical_and(first, here)
            for idx, (o, r) in enumerate(zip(outs, res)):
                if idx < n_out - n_sum:
                    o[...] = r.astype(o.dtype)
                else:
                    @pl.when(first)
                    def _(o=o, r=r):
                        o[...] = r.astype(o.dtype)

                    @pl.when(jnp.logical_not(first))
                    def _(o=o, r=r):
                        o[...] += r.astype(o.dtype)

        if k_axis is None:
            finish(prod)
        else:
            acc_ref = refs[-1]
            k = pl.program_id(k_axis)

            @pl.when(k == 0)
            def _():
                acc_ref[...] = prod

            @pl.when(k > 0)
            def _():
                acc_ref[...] += prod

            @pl.when(k == nk - 1)
            def _():
                finish(acc_ref[...])

    scratch = [] if k_axis is None else [pltpu.VMEM(acc_shape, F32)]
    return _call(name, body, [a, b, *extras], [a_spec, b_spec, *extra_specs], list(out_shapes), list(out_specs), scratch, grid, comm)


def _rowwise(name, fn, operands, *, grid, in_specs, out_shapes, out_specs, n_acc=0, grid_spec_prefetch=None, comm=None):
    n_in = len(operands)
    n_out = len(out_shapes)
    n_pre = 0 if grid_spec_prefetch is None else 1

    def body(*refs):
        refs = refs[n_pre:]
        ins = refs[:n_in]
        outs = refs[n_in:n_in + n_out]
        res = fn(*[r[...] for r in ins])
        if not isinstance(res, (tuple, list)):
            res = (res,)
        first = pl.program_id(0) == 0
        for d in range(1, len(grid)):
            first = jnp.logical_and(first, pl.program_id(d) == 0)
        for idx, (o, r) in enumerate(zip(outs, res)):
            if idx < n_out - n_acc:
                o[...] = r.astype(o.dtype)
            else:
                @pl.when(first)
                def _(o=o, r=r):
                    o[...] = r.astype(o.dtype)

                @pl.when(jnp.logical_not(first))
                def _(o=o, r=r):
                    o[...] += r.astype(o.dtype)

    if comm is not None:
        return _call(name, body, list(operands), list(in_specs), list(out_shapes), list(out_specs), [], grid, comm)
    if grid_spec_prefetch is None:
        return _pcall(body, name=name, grid=grid, in_specs=in_specs, out_specs=out_specs, out_shape=out_shapes,
                      compiler_params=_params(len(grid)))(*operands)
    gs = pltpu.PrefetchScalarGridSpec(num_scalar_prefetch=1, grid=grid, in_specs=in_specs, out_specs=out_specs)
    return _pcall(body, name=name, grid_spec=gs, out_shape=out_shapes,
                  compiler_params=_params(len(grid)))(grid_spec_prefetch, *operands)


def _row_spec(tm, w):
    return pl.BlockSpec((tm, w), lambda i: (i, 0))


def _const_spec(shape):
    nd = len(shape)
    return pl.BlockSpec(tuple(shape), lambda *_: (0,) * nd)


def _rms_fwd(x, g):
    r = lax.rsqrt(jnp.mean(x * x, axis=-1, keepdims=True) + NORM_EPS)
    return x * r * g


def _rms_bwd(x, g, dy):
    r = lax.rsqrt(jnp.mean(x * x, axis=-1, keepdims=True) + NORM_EPS)
    xh = x * r
    u = dy * g
    dx = r * (u - xh * jnp.mean(u * xh, axis=-1, keepdims=True))
    dg = jnp.sum(dy * xh, axis=0, keepdims=True)
    return dx, dg


def _gelu_and_grad(z):
    cdf = 0.5 * (1.0 + lax.erf(z * (2.0 ** -0.5)))
    return cdf + z * jnp.exp(-0.5 * z * z) * ((2.0 * math.pi) ** -0.5), z * cdf


def _rope_fwd(x, cc, sa, sb):
    return x * cc + pltpu.roll(x, 96, 1) * sa + pltpu.roll(x, 32, 1) * sb


def _rope_bwd(d, cc, sa, sb):
    return d * cc + pltpu.roll(d * sa, 32, 1) + pltpu.roll(d * sb, 96, 1)


def _adam(w, g, m, v):
    m = ADAM_B1 * m + (1.0 - ADAM_B1) * g
    v = ADAM_B2 * v + (1.0 - ADAM_B2) * (g * g)
    m_hat = m / (1.0 - ADAM_B1 ** ADAM_STEP)
    v_hat = v / (1.0 - ADAM_B2 ** ADAM_STEP)
    delta = -ADAM_LR * (m_hat / (jnp.sqrt(v_hat) + ADAM_EPS) + ADAM_WD * w)
    return delta, m, v


def _flash_fwd(q, k, vt, tq, comm=None):
    h, t = vt.shape[0], q.shape[0]
    nq = t // tq

    chunk_blocks = [c for c in (4, 2) if c < nq]

    def body(q_ref, k_ref, vt_ref, o_ref, lse_ref, m_ref, l_ref, acc_ref):
        qi = pl.program_id(1)
        m_ref[...] = jnp.full((1, tq), NEG, F32)
        l_ref[...] = jnp.zeros((1, tq), F32)
        acc_ref[...] = jnp.zeros((VDIM, tq), F32)

        def update(kb0, nblk, masked):
            kb = k_ref[pl.ds(pl.multiple_of(kb0 * tq, tq), nblk * tq), :]
            st = lax.dot_general(kb, q_ref[...], NT, preferred_element_type=F32)
            if masked:
                key = lax.broadcasted_iota(jnp.int32, (nblk * tq, tq), 0) - (nblk - 1) * tq
                qry = lax.broadcasted_iota(jnp.int32, (nblk * tq, tq), 1)
                st = jnp.where(key <= qry, st, NEG)
            m_old = m_ref[...]
            m_new = jnp.maximum(m_old, jnp.max(st, axis=0, keepdims=True))
            alpha = jnp.exp2((m_old - m_new) * EXP2_SCALE)
            pt = jnp.exp2((st - m_new) * EXP2_SCALE)
            l_ref[...] = alpha * l_ref[...] + jnp.sum(pt, axis=0, keepdims=True)
            ptb = pt.astype(BF16)
            pv = lax.dot_general(vt_ref[kb0], ptb[:tq], NN, preferred_element_type=F32)
            for j in range(1, nblk):
                pv += lax.dot_general(vt_ref[kb0 + j], ptb[j * tq:(j + 1) * tq], NN, preferred_element_type=F32)
            acc_ref[...] = alpha * acc_ref[...] + pv
            m_ref[...] = m_new

        start = jnp.int32(0)
        for c in chunk_blocks:
            take = (qi & c) != 0

            @pl.when(take)
            def _(start=start, c=c):
                update(start, c, False)

            start = start + jnp.where(take, c, 0)
        if nq > 1:
            @pl.when((qi & 1) != 0)
            def _():
                update(qi - 1, 2, True)

            @pl.when((qi & 1) == 0)
            def _():
                update(qi, 1, True)
        else:
            update(qi, 1, True)
        l = l_ref[...]
        o_ref[...] = (acc_ref[...] / l).T.astype(o_ref.dtype)
        lse_ref[...] = m_ref[...] * EXP2_SCALE + jnp.log2(l)

    return _call(
        "flash_fwd", body, [q, k, vt],
        [pl.BlockSpec((tq, QPAD), lambda hh, i: (i, hh)),
         pl.BlockSpec((t, QPAD), lambda hh, i: (0, hh)),
         pl.BlockSpec((None, nq, VDIM, tq), lambda hh, i: (hh, 0, 0, 0))],
        [_sds((t, h * VDIM), BF16), _sds((h, nq, 1, tq), F32)],
        [pl.BlockSpec((tq, VDIM), lambda hh, i: (i, hh)),
         pl.BlockSpec((None, None, 1, tq), lambda hh, i: (hh, i, 0, 0))],
        [pltpu.VMEM((1, tq), F32), pltpu.VMEM((1, tq), F32), pltpu.VMEM((VDIM, tq), F32)], (h, nq), comm)


def _flash_bwd(q, k, v, o, do, lse, tabs, tq, comm=None):
    t = q.shape[0]
    h = q.shape[1] // QPAD
    nq = t // tq

    def body(q_ref, k_ref, v_ref, o_ref, do_ref, lse_ref, cc_ref, sa_ref, sb_ref, dq_ref, dk_out, dv_out, delta_ref, dqt_ref, dk_ref, dv_ref):
        kj = pl.program_id(1)

        @pl.when(kj == 0)
        def _():
            dqt_ref[...] = jnp.zeros_like(dqt_ref)
            ones = jnp.ones((8, VDIM), BF16)
            for qi in range(nq):
                rows = pl.ds(qi * tq, tq)
                prod = do_ref[rows, :].astype(F32) * o_ref[rows, :].astype(F32)
                hi = prod.astype(BF16)
                lo = (prod - hi.astype(F32)).astype(BF16)
                delta_ref[qi] = (lax.dot_general(ones, hi, NT, preferred_element_type=F32)
                                 + lax.dot_general(ones, lo, NT, preferred_element_type=F32))

        kb = k_ref[...]
        vb = v_ref[...]
        kbt = kb.astype(F32).T.astype(BF16)
        dk_ref[...] = jnp.zeros_like(dk_ref)
        dv_ref[...] = jnp.zeros_like(dv_ref)

        def step(q0, nblk, masked):
            rows = pl.ds(pl.multiple_of(q0 * tq, tq), nblk * tq)
            qb = q_ref[rows, :]
            dob = do_ref[rows, :]
            lse = jnp.concatenate([lse_ref[q0 + j] for j in range(nblk)], axis=1)
            delta = jnp.concatenate([delta_ref[q0 + j, pl.ds(0, 1), :] for j in range(nblk)], axis=1)
            st = lax.dot_general(kb, qb, NT, preferred_element_type=F32)
            pt = jnp.exp2(st * EXP2_SCALE - lse)
            if masked:
                key = lax.broadcasted_iota(jnp.int32, (tq, nblk * tq), 0)
                qry = lax.broadcasted_iota(jnp.int32, (tq, nblk * tq), 1)
                pt = jnp.where(key <= qry, pt, 0.0)
            dv_ref[...] += lax.dot_general(pt.astype(BF16), dob, NN, preferred_element_type=F32)
            dpt = lax.dot_general(vb, dob, NT, preferred_element_type=F32)
            dst = (pt * (dpt - delta) * ATTN_SCALE).astype(BF16)
            dk_ref[...] += lax.dot_general(dst, qb, NN, preferred_element_type=F32)
            dqt = lax.dot_general(kbt, dst, NN, preferred_element_type=F32)
            for j in range(nblk):
                dqt_ref[q0 + j] += dqt[:, j * tq:(j + 1) * tq]

        later = nq - 1 - kj
        if nq > 1:
            @pl.when((later & 1) != 0)
            def _():
                step(kj, 2, True)

            @pl.when((later & 1) == 0)
            def _():
                step(kj, 1, True)
        else:
            step(kj, 1, True)
        start = kj + 1 + (later & 1)
        for c in [c for c in (2, 4) if c < nq]:
            take = (later & c) != 0

            @pl.when(take)
            def _(start=start, c=c):
                step(start, c, False)

            start = start + jnp.where(take, c, 0)
        dk_out[...] = dk_ref[...].astype(BF16)
        dv_out[...] = dv_ref[...].astype(BF16)

        @pl.when(kj == nq - 1)
        def _():
            for qi in range(nq):
                rows = pl.ds(qi * tq, tq)
                d = dqt_ref[qi].T
                roped = _rope_bwd(d[:, NOPE:], cc_ref[rows, :], sa_ref[rows, :], sb_ref[rows, :])
                dq_ref[rows, :] = jnp.concatenate([d[:, :NOPE], roped], axis=1).astype(BF16)

    head_q = pl.BlockSpec((t, QPAD), lambda hh, j: (0, hh))
    head_v = pl.BlockSpec((t, VDIM), lambda hh, j: (0, hh))
    table = pl.BlockSpec((t, 128), lambda hh, j: (0, 0))
    return _call(
        "flash_bwd", body, [q, k, v, o, do, lse, *tabs],
        [head_q, pl.BlockSpec((tq, QPAD), lambda hh, j: (j, hh)), pl.BlockSpec((tq, VDIM), lambda hh, j: (j, hh)), head_v, head_v,
         pl.BlockSpec((None, nq, 1, tq), lambda hh, j: (hh, 0, 0, 0)), table, table, table],
        [_sds((t, h * QPAD), BF16), _sds((t, h * QPAD), BF16), _sds((t, h * VDIM), BF16)],
        [head_q, pl.BlockSpec((tq, QPAD), lambda hh, j: (j, hh)), pl.BlockSpec((tq, VDIM), lambda hh, j: (j, hh))],
        [pltpu.VMEM((nq, 8, tq), F32), pltpu.VMEM((nq, QPAD, tq), F32), pltpu.VMEM((tq, QPAD), F32), pltpu.VMEM((tq, VDIM), F32)], (h, nq), comm)


def _tril_bf16(w):
    row = lax.broadcasted_iota(jnp.int32, w.shape, 0)
    col = lax.broadcasted_iota(jnp.int32, w.shape, 1)
    return jnp.where(col <= row, w, 0.0).astype(BF16)


def _layer_norm_parts(v0):
    mu = jnp.mean(v0, axis=-1, keepdims=True)
    vc = v0 - mu
    rstd = lax.rsqrt(jnp.mean(vc * vc, axis=-1, keepdims=True) + LN_EPS)
    return vc * rstd, rstd


def _sgu_mid_fwd(ge, ln_g, ln_b, w_sp, b_sp, chunks_per_step):
    t, e2 = ge.shape
    e = e2 // 2
    gd = e // SGU_GROUPS
    rows = SGU_CHUNK * chunks_per_step

    def body(u_ref, v_ref, g_ref, b_ref, w_ref, bs_ref, gate_ref):
        for ck in range(chunks_per_step):
            r = pl.ds(ck * SGU_CHUNK, SGU_CHUNK)
            xh, _ = _layer_norm_parts(v_ref[r, :].astype(F32))
            v1 = (xh * g_ref[...] + b_ref[...]).astype(BF16)
            for g in range(SGU_GROUPS):
                cols = pl.ds(g * gd, gd)
                mixed = lax.dot_general(_tril_bf16(w_ref[g]), v1[:, g * gd:(g + 1) * gd], NN, preferred_element_type=F32) + bs_ref[g]
                gate_ref[r, cols] = (u_ref[r, cols].astype(F32) * mixed).astype(BF16)

    return _pcall(
        body, name="sgu_mid_fwd", grid=(t // rows,),
        in_specs=[pl.BlockSpec((rows, e), lambda i: (i, 0)), pl.BlockSpec((rows, e), lambda i: (i, 1)),
                  _const_spec((1, e)), _const_spec((1, e)), _const_spec(w_sp.shape), _const_spec(b_sp.shape)],
        out_specs=pl.BlockSpec((rows, e), lambda i: (i, 0)),
        out_shape=_sds((t, e), BF16), compiler_params=_params(1),
    )(ge, ge, ln_g, ln_b, w_sp, b_sp)


def _sgu_mid_bwd(ge, gp, dgate, ln_g, ln_b, w_sp, b_sp, chunks_per_step):
    t, e2 = ge.shape
    e = e2 // 2
    gd = e // SGU_GROUPS
    rows = SGU_CHUNK * chunks_per_step

    def body(u_ref, v_ref, zu_ref, zv_ref, dg_ref, g_ref, b_ref, w_ref, bs_ref, dz_ref, dw_ref, dbs_ref, dlg_ref, dlb_ref):
        @pl.when(pl.program_id(0) == 0)
        def _():
            dw_ref[...] = jnp.zeros_like(dw_ref)
            dbs_ref[...] = jnp.zeros_like(dbs_ref)
            dlg_ref[...] = jnp.zeros_like(dlg_ref)
            dlb_ref[...] = jnp.zeros_like(dlb_ref)

        for ck in range(chunks_per_step):
            r = pl.ds(ck * SGU_CHUNK, SGU_CHUNK)
            xh, rstd = _layer_norm_parts(v_ref[r, :].astype(F32))
            v1 = (xh * g_ref[...] + b_ref[...]).astype(BF16)
            dv1_parts = []
            for g in range(SGU_GROUPS):
                cols = pl.ds(g * gd, gd)
                wc = _tril_bf16(w_ref[g])
                v1g = v1[:, g * gd:(g + 1) * gd]
                mixed = lax.dot_general(wc, v1g, NN, preferred_element_type=F32) + bs_ref[g]
                dgate = dg_ref[r, cols].astype(F32)
                dmixed = dgate * u_ref[r, cols].astype(F32)
                du = dgate * mixed
                dz_ref[r, cols] = (du * zu_ref[r, cols].astype(F32)).astype(BF16)
                dbs_ref[g] += jnp.sum(dmixed, axis=1, keepdims=True)
                dmb = dmixed.astype(BF16)
                dwg = lax.dot_general(dmb, v1g, NT, preferred_element_type=F32)
                row = lax.broadcasted_iota(jnp.int32, dwg.shape, 0)
                col = lax.broadcasted_iota(jnp.int32, dwg.shape, 1)
                dw_ref[g] += jnp.where(col <= row, dwg, 0.0)
                dv1_parts.append(lax.dot_general(wc, dmb, TN, preferred_element_type=F32))
            dv1 = jnp.concatenate(dv1_parts, axis=1)
            dlg_ref[...] += jnp.sum(dv1 * xh, axis=0, keepdims=True)
            dlb_ref[...] += jnp.sum(dv1, axis=0, keepdims=True)
            dxh = dv1 * g_ref[...]
            dv0 = rstd * (dxh - jnp.mean(dxh, axis=-1, keepdims=True) - xh * jnp.mean(dxh * xh, axis=-1, keepdims=True))
            dz_ref[r, pl.ds(e, e)] = (dv0 * zv_ref[r, :].astype(F32)).astype(BF16)

    half0 = pl.BlockSpec((rows, e), lambda i: (i, 0))
    half1 = pl.BlockSpec((rows, e), lambda i: (i, 1))
    return _pcall(
        body, name="sgu_mid_bwd", grid=(t // rows,),
        in_specs=[half0, half1, half0, half1, half0, _const_spec((1, e)), _const_spec((1, e)), _const_spec(w_sp.shape), _const_spec(b_sp.shape)],
        out_specs=[pl.BlockSpec((rows, e2), lambda i: (i, 0)), _const_spec(w_sp.shape), _const_spec(b_sp.shape), _const_spec((1, e)), _const_spec((1, e))],
        out_shape=[_sds((t, e2), BF16), _sds(w_sp.shape, F32), _sds(b_sp.shape, F32), _sds((1, e), F32), _sds((1, e), F32)],
        compiler_params=_params(1),
    )(ge, ge, gp, gp, dgate, ln_g, ln_b, w_sp, b_sp)


def kernel(x, positions, norm_mix, norm_ffn, final_norm, mla_w_dkv, mla_q_norm, mla_kv_norm, mla_w_uq, mla_w_ukv, mla_w_o, sgu_w_in, sgu_ln_g, sgu_ln_b, sgu_w_spatial, sgu_b_spatial, sgu_w_out, ffn_w_up, ffn_w_down, loss_target, m_norm_mix, m_norm_ffn, m_final_norm, m_mla_w_dkv, m_mla_q_norm, m_mla_kv_norm, m_mla_w_uq, m_mla_w_ukv, m_mla_w_o, m_sgu_w_in, m_sgu_ln_g, m_sgu_ln_b, m_sgu_w_spatial, m_sgu_b_spatial, m_sgu_w_out, m_ffn_w_up, m_ffn_w_down, v_norm_mix, v_norm_ffn, v_final_norm, v_mla_w_dkv, v_mla_q_norm, v_mla_kv_norm, v_mla_w_uq, v_mla_w_ukv, v_mla_w_o, v_sgu_w_in, v_sgu_ln_g, v_sgu_ln_b, v_sgu_w_spatial, v_sgu_b_spatial, v_sgu_w_out, v_ffn_w_up, v_ffn_w_down):
    _, T, D = x.shape
    depth = norm_mix.shape[0]
    n_mla, n_sgu = mla_w_dkv.shape[0], sgu_w_in.shape[0]
    assert depth % 2 == 0
    FF = ffn_w_up.shape[2] * N_DEV
    E = sgu_w_out.shape[1] * N_DEV
    ffc, ec, e2c = FF // N_DEV, E // N_DEV, 2 * E // N_DEV
    dc = D // N_DEV
    OW = HEADS * VDIM
    HW = HEADS * QPAD
    owc = OW // N_DEV
    tm = _tile(T, 1024)
    tb = _tile(T, 4096)
    tk = _tile(T, 512)
    tq = _tile(T, 512)
    ts = _tile(T, 256)
    nt = T // tm
    x2 = x.reshape(T, D)
    tgt = loss_target.reshape(T, D)
    cidx = lax.axis_index("c").astype(jnp.int32).reshape(1)

    ln_local = jnp.concatenate([sgu_ln_g, sgu_ln_b, jnp.zeros((8 - 2 * n_sgu, ec), F32)], axis=0)
    mla_sh = [[w[l].astype(BF16) for w in (mla_w_dkv, mla_w_uq, mla_w_ukv, mla_w_o)] for l in range(n_mla)]

    def mla_layouts(g_dkv, g_uq, g_ukv, g_o):
        w_dkv = jnp.pad(g_dkv.reshape(1, D, LAT), ((0, 0), (0, 0), (0, LAT_PAD - LAT)))
        w_uq = jnp.pad(g_uq, ((0, 0), (0, 0), (0, QPAD - NOPE - ROPE))).transpose(1, 0, 2).reshape(1, Q_RANK, HEADS * QPAD)
        w_ukv = g_ukv.transpose(1, 0, 2).reshape(1, KV_RANK, HEADS * (NOPE + VDIM))
        return w_dkv, w_uq, w_ukv, g_o.reshape(1, HEADS * VDIM, D)

    mla_w = [None] * n_mla
    small_later = [a for l in range(1, n_mla) for a in mla_sh[l]] + [ln_local]
    ln_g_full, ln_b_full = [None] * n_sgu, [None] * n_sgu
    b_sp = sgu_b_spatial.reshape(n_sgu, SGU_GROUPS, SGU_CHUNK, 1)
    up_sh = [ffn_w_up[i].astype(BF16) for i in range(depth)]
    down_sh = [ffn_w_down[i].astype(BF16) for i in range(depth)]
    in_sh = [sgu_w_in[l].astype(BF16) for l in range(n_sgu)]
    out_sh = [sgu_w_out[l].astype(BF16) for l in range(n_sgu)]
    g_up, g_down, g_in, g_out = [None] * depth, [None] * depth, [None] * n_sgu, [None] * n_sgu

    inv_freq = ROPE_THETA ** (-jnp.arange(0, ROPE, 2, dtype=F32) / ROPE)
    zeros32 = jnp.zeros((ROPE // 2,), F32)
    inv128 = jnp.concatenate([inv_freq, inv_freq, zeros32, zeros32]).reshape(1, 128)
    sel_a = jnp.concatenate([-jnp.ones((32,), F32), zeros32, zeros32, zeros32]).reshape(1, 128)
    sel_b = jnp.concatenate([zeros32, jnp.ones((32,), F32), zeros32, zeros32]).reshape(1, 128)
    sel_c = jnp.concatenate([jnp.ones((64,), F32), zeros32, zeros32]).reshape(1, 128)

    def rope_tables(pos, inv, sa, sb, sc):
        ang = pos.astype(F32) * inv
        cs, sn = jnp.cos(ang), jnp.sin(ang)
        return cs * sc, sn * sa, sn * sb

    t_cc, t_sa, t_sb, *first_half = _rowwise(
        "rope_tables", rope_tables, [positions.reshape(T, 1), inv128, sel_a, sel_b, sel_c], grid=(nt,),
        in_specs=[_row_spec(tm, 1)] + [_const_spec((1, 128))] * 4,
        out_shapes=[_sds((T, 128), F32)] * 3, out_specs=[_row_spec(tm, 128)] * 3, comm=_gather_level1(mla_sh[0]))
    tab_specs = [_row_spec(tm, 128)] * 3

    def rmsnorm(xv, g, comm):
        return _rowwise("rmsnorm", lambda a, gg: _rms_fwd(a, gg), [xv, g.reshape(1, D)], grid=(nt,),
                        in_specs=[_row_spec(tm, D), _const_spec((1, D))], out_shapes=[_sds((T, D), BF16)], out_specs=[_row_spec(tm, D)], comm=comm)

    def proj_cols(name, h, gw, nc, epilogue, n_out, comm=None):
        return _matmul(name, h, gw, [], grid=(N_DEV, T // tb),
                       a_spec=pl.BlockSpec((tb, D), lambda j, i: (i, 0)),
                       b_spec=pl.BlockSpec((None, D, nc), lambda j, i: (j, 0, 0)), extra_specs=[],
                       out_shapes=[_sds((T, nc * N_DEV), BF16)] * n_out, out_specs=[pl.BlockSpec((tb, nc), lambda j, i: (i, j))] * n_out,
                       dims=NN, epilogue=epilogue, comm=comm)

    def residual_norm(acc, xr, g):
        xn = acc + xr
        return xn, _rms_fwd(xn, g)

    def proj_rows_residual(name, a, gw, xres, g_next, comm=None):
        kk_ = a.shape[1]
        return _matmul(name, a, gw.reshape(kk_, D), [xres, g_next.reshape(1, D)], grid=(T // tk,),
                       a_spec=_row_spec(tk, kk_), b_spec=_const_spec((kk_, D)), extra_specs=[_row_spec(tk, D), _const_spec((1, D))],
                       out_shapes=[_sds((T, D), F32), _sds((T, D), BF16)], out_specs=[_row_spec(tk, D)] * 2,
                       dims=NN, epilogue=residual_norm, comm=comm)

    def back_rows(name, dy, gw, kc, extras, epilogue, comm=None):
        return _matmul(name, dy, gw, extras, grid=(N_DEV, T // tb),
                       a_spec=pl.BlockSpec((tb, D), lambda j, i: (i, 0)),
                       b_spec=pl.BlockSpec((None, kc, D), lambda j, i: (j, 0, 0)),
                       extra_specs=[pl.BlockSpec((tb, kc), lambda j, i: (i, j))] * len(extras),
                       out_shapes=[_sds((T, kc * N_DEV), BF16)], out_specs=[pl.BlockSpec((tb, kc), lambda j, i: (i, j))],
                       dims=NT, epilogue=epilogue, comm=comm)

    def norm_bwd_epilogue(dh, xv, g, dxi):
        dxn, dg = _rms_bwd(xv, g, dh)
        return dxi + dxn, dxi + dxn, dg

    def transposed(gw):
        return gw.transpose(0, 2, 1).reshape(gw.shape[0] * gw.shape[2], D)

    def back_cols(name, da, gwt, xv, g, dx_in, comm=None):
        n = da.shape[1]
        row = _row_spec(tk, D)
        return _matmul(name, da, gwt, [xv, g.reshape(1, D), dx_in], grid=(T // tk,),
                       a_spec=_row_spec(tk, n), b_spec=_const_spec((n, D)), extra_specs=[row, _const_spec((1, D)), row],
                       out_shapes=[_sds((T, D), F32), _sds((T, D), BF16), _sds((1, D), F32)], out_specs=[row, row, _const_spec((1, D))],
                       dims=NN, epilogue=norm_bwd_epilogue, n_sum=1, comm=comm)

    def token_sum(tt):
        return dict(k_axis=1, nk=T // tt) if T // tt > 1 else dict(k_axis=None)

    def wgrad_cols(name, h, da, nc, comm=None):
        res = _matmul(name, h, da, [], grid=(N_DEV, T // tb),
                       a_spec=pl.BlockSpec((tb, D), lambda j, t: (t, 0)), b_spec=pl.BlockSpec((tb, nc), lambda j, t: (t, j)),
                       extra_specs=[], out_shapes=[_sds((N_DEV, D, nc), BF16)],
                       out_specs=[pl.BlockSpec((None, D, nc), lambda j, t: (j, 0, 0))],
                       dims=TN, acc_shape=(D, nc), comm=comm, **token_sum(tb))
        return res[0] if comm is None else res

    def wgrad_rows(name, a, dy, kc, ncols, tt, comm=None):
        res = _matmul(name, a, dy, [], grid=(a.shape[1] // kc, T // tt),
                      a_spec=pl.BlockSpec((tt, kc), lambda j, t: (t, j)), b_spec=pl.BlockSpec((tt, ncols), lambda j, t: (t, 0)),
                      extra_specs=[], out_shapes=[_sds((a.shape[1], ncols), BF16)],
                      out_specs=[pl.BlockSpec((kc, ncols), lambda j, t: (j, 0))],
                      dims=TN, acc_shape=(kc, ncols), comm=comm, **token_sum(tt))
        return res[0] if comm is None else res

    saved = []
    xs = x2
    for i in range(depth):
        l = i // 2
        if i == 0:
            h, *first_w = rmsnorm(xs, norm_mix[0], _gather_level2(first_half))
            mla_w[0] = mla_layouts(*first_w)
        if i % 2 == 0:
            w_dkv, w_uq, w_ukv, w_o = mla_w[l]
            lat = _matmul("mla_down", h, w_dkv, [], grid=(nt,), a_spec=_row_spec(tm, D),
                          b_spec=pl.BlockSpec((None, D, LAT_PAD), lambda i_: (0, 0, 0)), extra_specs=[],
                          out_shapes=[_sds((T, LAT_PAD), F32)], out_specs=[_row_spec(tm, LAT_PAD)], dims=NN)[0]

            def latent_post(la, qn, kvn, cc, sa, sb):
                cq = _rms_fwd(la[:, :Q_RANK], qn)
                ckv = _rms_fwd(la[:, Q_RANK:Q_RANK + KV_RANK], kvn)
                kr = _rope_fwd(la[:, Q_RANK + KV_RANK:], cc, sa, sb)
                return cq, ckv, kr

            cq, ckv, kr = _rowwise(
                "mla_latent", latent_post, [lat, mla_q_norm[l].reshape(1, Q_RANK), mla_kv_norm[l].reshape(1, KV_RANK), t_cc, t_sa, t_sb],
                grid=(nt,), in_specs=[_row_spec(tm, LAT_PAD), _const_spec((1, Q_RANK)), _const_spec((1, KV_RANK))] + tab_specs,
                out_shapes=[_sds((T, Q_RANK), BF16), _sds((T, KV_RANK), BF16), _sds((T, 128), BF16)],
                out_specs=[_row_spec(tm, Q_RANK), _row_spec(tm, KV_RANK), _row_spec(tm, 128)])

            def q_epilogue(acc, cc, sa, sb):
                parts = []
                for b in range(HEADS):
                    parts += [acc[:, b * QPAD:b * QPAD + NOPE], _rope_fwd(acc[:, b * QPAD + NOPE:(b + 1) * QPAD], cc, sa, sb)]
                return (jnp.concatenate(parts, axis=1),)

            q = _matmul("mla_q", cq, w_uq, [t_cc, t_sa, t_sb], grid=(nt,), a_spec=_row_spec(tm, Q_RANK),
                        b_spec=pl.BlockSpec((None, Q_RANK, HW), lambda i_: (0, 0, 0)), extra_specs=tab_specs,
                        out_shapes=[_sds((T, HW), BF16)], out_specs=[_row_spec(tm, HW)], dims=NN, epilogue=q_epilogue)[0]

            def kv_write(outs, acc, krb):
                k_ref, v_ref, vt_ref = outs
                for b in range(HEADS):
                    vb = acc[:, b * QPAD + NOPE:(b + 1) * QPAD]
                    k_ref[:, b * QPAD:b * QPAD + NOPE] = acc[:, b * QPAD:b * QPAD + NOPE].astype(BF16)
                    k_ref[:, b * QPAD + NOPE:(b + 1) * QPAD] = krb
                    v_ref[:, b * VDIM:(b + 1) * VDIM] = vb.astype(BF16)
                    vbt = vb.T.astype(BF16)
                    for u in range(tm // tq):
                        vt_ref[b, u] = vbt[:, u * tq:(u + 1) * tq]

            kk, vv, vt = _matmul("mla_kv", ckv, w_ukv, [kr], grid=(nt,), a_spec=_row_spec(tm, KV_RANK),
                                 b_spec=pl.BlockSpec((None, KV_RANK, HW), lambda i_: (0, 0, 0)), extra_specs=[_row_spec(tm, 128)],
                                 out_shapes=[_sds((T, HW), BF16), _sds((T, OW), BF16), _sds((HEADS, T // tq, VDIM, tq), BF16)],
                                 out_specs=[_row_spec(tm, HW), _row_spec(tm, OW), pl.BlockSpec((HEADS, tm // tq, VDIM, tq), lambda i_: (0, i_, 0, 0))],
                                 dims=NN, write=kv_write)
            group = [up_sh[i], down_sh[i], in_sh[l], out_sh[l]] + (small_later if i == 0 else [])
            o, lse, *bufs = _flash_fwd(q, kk, vt, tq, comm=_gather_level1(group))
            xm, h2, g_up[i], g_down[i] = _matmul(
                "mla_out", o, w_o, [xs, norm_ffn[i].reshape(1, D)], grid=(nt,), a_spec=_row_spec(tm, OW),
                b_spec=pl.BlockSpec((None, OW, D), lambda i_: (0, 0, 0)), extra_specs=[_row_spec(tm, D), _const_spec((1, D))],
                out_shapes=[_sds((T, D), F32), _sds((T, D), BF16)], out_specs=[_row_spec(tm, D)] * 2, dims=NN,
                epilogue=residual_norm, comm=_gather_level2(bufs[:2]))
            half_gathered = bufs[2:]
            mix_saved = (h, lat, cq, ckv, q, kk, vv, o, lse)
        else:
            gp, ge, g_down[i], up_half = proj_cols("sgu_in", h, g_in[l], e2c, _gelu_and_grad, 2,
                                                   comm=_merge_comm(_gather_level2([down_half]), _gather_level1([up_sh[i]])))
            gate = _sgu_mid_fwd(ge, ln_g_full[l], ln_b_full[l], sgu_w_spatial[l], b_sp[l], 4)
            xm, h2, g_up[i] = proj_rows_residual("sgu_out", gate, g_out[l], xs, norm_ffn[i], comm=_gather_level2([up_half]))
            mix_saved = (h, gp, ge, gate)
        r, s, *rest = proj_cols("ffn_up", h2, g_up[i], ffc, lambda acc: (jnp.maximum(acc, 0.0), jnp.square(jnp.maximum(acc, 0.0))), 2,
                                comm=_merge_comm(_gather_level2(half_gathered), _gather_level1([down_sh[i + 1]], part=(0, 2))) if i % 2 == 0 else None)
        if i % 2 == 0:
            g_in[l], g_out[l], *small_gathered, down_part = rest
        if i == 0:
            for l_ in range(1, n_mla):
                mla_w[l_] = mla_layouts(*small_gathered[4 * (l_ - 1):4 * l_])
            g_ln = small_gathered[-1]
            ln_g_full = [g_ln[:, l_, :].reshape(1, E) for l_ in range(n_sgu)]
            ln_b_full = [g_ln[:, n_sgu + l_, :].reshape(1, E) for l_ in range(n_sgu)]
        saved.append((xs, xm, mix_saved, h2, r, s))
        if i + 1 < depth:
            xs, h, *rest = proj_rows_residual("ffn_down", s, g_down[i], xm, norm_mix[i + 1],
                                              comm=_gather_level1([down_sh[i + 1]], part=(1, 2), into=[down_part]) if i % 2 == 0 else None)
            if i % 2 == 0:
                (down_half,) = rest

    def loss_head(acc, xr, tg, g):
        xv = acc + xr
        y = _rms_fwd(xv, g)
        err = y - tg
        part = 0.5 * jnp.sum(jnp.sum(err * err, axis=-1, keepdims=True), axis=0, keepdims=True) / D
        dxv, dg = _rms_bwd(xv, g, err / D)
        return dxv, dxv, jnp.broadcast_to(part, (1, 128)), dg

    dx, dyb, loss_part, d_final = _matmul(
        "ffn_down_loss", s, g_down[depth - 1].reshape(FF, D), [xm, tgt, final_norm.reshape(1, D)], grid=(T // tk,),
        a_spec=_row_spec(tk, FF), b_spec=_const_spec((FF, D)), extra_specs=[_row_spec(tk, D), _row_spec(tk, D), _const_spec((1, D))],
        out_shapes=[_sds((T, D), F32), _sds((T, D), BF16), _sds((1, 128), F32), _sds((1, D), F32)],
        out_specs=[_row_spec(tk, D), _row_spec(tk, D), _const_spec((1, 128)), _const_spec((1, D))], dims=NN, epilogue=loss_head, n_sum=2)
    loss = lax.psum(loss_part[0, 0], ("x", "y", "c"))

    d_norm_mix, d_norm_ffn = [None] * depth, [None] * depth
    d_qn, d_kvn = [None] * n_mla, [None] * n_mla
    d_wsp, d_bsp, d_lng, d_lnb = [None] * n_sgu, [None] * n_sgu, [None] * n_sgu, [None] * n_sgu
    layers = {"dkv": n_mla, "uq": n_mla, "ukv": n_mla, "o": n_mla, "in": n_sgu, "out": n_sgu, "up": depth, "down": depth}
    stacked = {nm: None for nm in layers}
    pending = []
    summed = []

    def add_pairs(gs, rcvs):
        operands, in_specs, out_shapes, out_specs = [], [], [], []
        for g, rcv in zip(gs, rcvs):
            _, rws, cls = g.shape
            slab = pl.BlockSpec((None, rws, cls), lambda ch, cr: (ch, 0, 0))
            operands += [g.reshape(N_CHIP, 2, rws, cls), rcv]
            in_specs += [pl.BlockSpec((None, None, rws, cls), lambda ch, cr: (ch, cr[0], 0, 0)), slab]
            out_shapes.append(_sds(rcv.shape, BF16))
            out_specs.append(slab)

        def fn(*blocks):
            return tuple(blocks[2 * k].astype(F32) + blocks[2 * k + 1].astype(F32) for k in range(len(gs)))

        return _rowwise("grad_pair_sum", fn, operands, grid=(N_CHIP,), in_specs=in_specs, out_shapes=out_shapes, out_specs=out_specs,
                        grid_spec_prefetch=cidx)

    def sibling_comm():
        return _sibling_exchange([g for _, _, g in pending]) if pending else None

    def absorb(from_sibling):
        if pending:
            parts = add_pairs([g for _, _, g in pending], list(from_sibling))
            summed.extend((nm, l_, p) for (nm, l_, _), p in zip(pending, parts))
            pending.clear()

    def chip_comm():
        if pending:
            absorb(_comm_call("grad_sibling_exchange", sibling_comm()))
        comm, names = _chip_exchange([p for _, _, p in summed], [(nm, l_) for nm, l_, _ in summed], layers, stacked)
        summed.clear()
        return comm, names

    def rows128(a, rows):
        flat = a.reshape(-1, 128)
        return jnp.pad(flat, ((0, rows - flat.shape[0]), (0, 0)))

    def pad_to(n, mult):
        return -(-n // mult) * mult

    def packed(arrs, sizes):
        return jnp.concatenate([rows128(a, sz) for a, sz in zip(arrs, sizes)], axis=0)

    n_wsp, n_bsp, n_ln = sgu_w_spatial.size // 128, pad_to(sgu_b_spatial.size // 128, 8), pad_to(n_sgu * E // 128, 8)
    early_sizes = [n_wsp, pad_to(n_wsp + n_bsp, SMALL_ROWS) - n_wsp, n_ln, n_ln]
    early_rep = early_sizes[0] + early_sizes[1]
    gathered_early = None

    for i in reversed(range(depth)):
        l = i // 2
        xs_i, xm, mix_saved, h2, r, s = saved[i]
        comm = sibling_comm()
        if i == 0:
            comm = _gather_level2([early_half]) if comm is None else _merge_comm(comm, _gather_level2([early_half]))
        da, *rcv = back_rows("ffn_down_bwd", dyb, g_down[i], ffc, [r], lambda acc, rr: (acc * (2.0 * rr.astype(F32)),), comm=comm)
        if i == 0:
            *rcv, gathered_early = rcv
        absorb(rcv)
        pending.append(("down", i, wgrad_rows("ffn_down_wgrad", s, dyb, ffc, D, tb).reshape(N_DEV, ffc, D)))
        pending.append(("up", i, wgrad_cols("ffn_up_wgrad", h2, da, ffc)))
        dx, dyb, d_norm_ffn[i], *rcv = back_cols("ffn_up_bwd", da, transposed(g_up[i]), xm, norm_ffn[i], dx, comm=sibling_comm())
        absorb(rcv)
        if i % 2 == 0:
            h, lat, cq, ckv, q, kk, vv, o, lse = mix_saved
            w_dkv, w_uq, w_ukv, w_o = mla_w[l]
            do = _matmul("mla_out_bwd", dyb, w_o, [], grid=(nt,), a_spec=_row_spec(tm, D),
                         b_spec=pl.BlockSpec((None, OW, D), lambda i_: (0, 0, 0)), extra_specs=[],
                         out_shapes=[_sds((T, OW), BF16)], out_specs=[_row_spec(tm, OW)], dims=NT)[0]
            g_o_l = wgrad_rows("mla_out_wgrad", o, dyb, OW, D, tm).reshape(N_DEV, owc, D)
            comm, names = chip_comm()
            dq_pre, dk, dv, *bufs = _flash_bwd(q, kk, vv, o, do, lse, (t_cc, t_sa, t_sb), tq, comm=comm)
            stacked.update(dict(zip(names, bufs)))
            pending.append(("o", l, g_o_l))

            def kv_pre(dkb, dvb, cc, sa, sb):
                parts, dkr = [], None
                for b in range(HEADS):
                    parts += [dkb[:, b * QPAD:b * QPAD + NOPE], dvb[:, b * VDIM:(b + 1) * VDIM]]
                    piece = dkb[:, b * QPAD + NOPE:(b + 1) * QPAD].astype(F32)
                    dkr = piece if dkr is None else dkr + piece
                return jnp.concatenate(parts, axis=1), _rope_bwd(dkr, cc, sa, sb)

            dkv, dkr = _rowwise("mla_dkv_rope", kv_pre, [dk, dv, t_cc, t_sa, t_sb], grid=(T // ts,),
                                in_specs=[_row_spec(ts, HW), _row_spec(ts, OW)] + [_row_spec(ts, 128)] * 3,
                                out_shapes=[_sds((T, HW), BF16), _sds((T, 128), F32)], out_specs=[_row_spec(ts, HW), _row_spec(ts, 128)])
            g_uq_l = wgrad_rows("mla_q_wgrad", cq, dq_pre, Q_RANK, HW, tm)
            g_ukv_l = wgrad_rows("mla_kv_wgrad", ckv, dkv, KV_RANK, HW, tm)
            pending.append(("uq", l, g_uq_l.reshape(Q_RANK, HEADS, QPAD)[:, :, :NOPE + ROPE].transpose(1, 0, 2)))
            pending.append(("ukv", l, g_ukv_l.reshape(KV_RANK, HEADS, NOPE + VDIM).transpose(1, 0, 2)))
            dcq = _matmul("mla_q_bwd", dq_pre, w_uq, [], grid=(nt,), a_spec=_row_spec(tm, HW),
                          b_spec=pl.BlockSpec((None, Q_RANK, HW), lambda i_: (0, 0, 0)), extra_specs=[],
                          out_shapes=[_sds((T, Q_RANK), F32)], out_specs=[_row_spec(tm, Q_RANK)], dims=NT)[0]
            def latent_bwd(dkv_, la, qn, kvn, dq_, dkr_):
                dcq_raw, dqn = _rms_bwd(la[:, :Q_RANK], qn, dq_)
                dckv_raw, dkvn = _rms_bwd(la[:, Q_RANK:Q_RANK + KV_RANK], kvn, dkv_)
                return jnp.concatenate([dcq_raw, dckv_raw, dkr_], axis=1), dqn, dkvn

            dlat, d_qn[l], d_kvn[l] = _matmul(
                "mla_kv_bwd", dkv, w_ukv, [lat, mla_q_norm[l].reshape(1, Q_RANK), mla_kv_norm[l].reshape(1, KV_RANK), dcq, dkr],
                grid=(nt,), a_spec=_row_spec(tm, HW), b_spec=pl.BlockSpec((None, KV_RANK, HW), lambda i_: (0, 0, 0)),
                extra_specs=[_row_spec(tm, LAT_PAD), _const_spec((1, Q_RANK)), _const_spec((1, KV_RANK)), _row_spec(tm, Q_RANK), _row_spec(tm, 128)],
                out_shapes=[_sds((T, LAT_PAD), BF16), _sds((1, Q_RANK), F32), _sds((1, KV_RANK), F32)],
                out_specs=[_row_spec(tm, LAT_PAD), _const_spec((1, Q_RANK)), _const_spec((1, KV_RANK))], dims=NT,
                epilogue=latent_bwd, n_sum=2)
            g_dkv_l = wgrad_rows("mla_down_wgrad", h, dlat, D, LAT_PAD, tm)
            pending.append(("dkv", l, g_dkv_l[:, :LAT].reshape(N_DEV, dc, LAT)))
            dx, dyb, d_norm_mix[i] = _matmul(
                "mla_down_bwd", dlat, w_dkv, [xs_i, norm_mix[i].reshape(1, D), dx], grid=(nt,), a_spec=_row_spec(tm, LAT_PAD),
                b_spec=pl.BlockSpec((None, D, LAT_PAD), lambda i_: (0, 0, 0)), extra_specs=[_row_spec(tm, D), _const_spec((1, D)), _row_spec(tm, D)],
                out_shapes=[_sds((T, D), F32), _sds((T, D), BF16), _sds((1, D), F32)],
                out_specs=[_row_spec(tm, D), _row_spec(tm, D), _const_spec((1, D))], dims=NT, epilogue=norm_bwd_epilogue, n_sum=1)
        else:
            h, gp, ge, gate = mix_saved
            (dgate,) = back_rows("sgu_out_bwd", dyb, g_out[l], ec, [], None)
            pending.append(("out", l, wgrad_rows("sgu_out_wgrad", gate, dyb, ec, D, tb).reshape(N_DEV, ec, D)))
            dz, d_wsp[l], d_bsp[l], d_lng[l], d_lnb[l] = _sgu_mid_bwd(ge, gp, dgate, ln_g_full[l], ln_b_full[l], sgu_w_spatial[l], b_sp[l], 2)
            if i == 1:
                early = packed([jnp.stack(d_wsp, 0), jnp.stack(d_bsp, 0), jnp.concatenate(d_lng, 0), jnp.concatenate(d_lnb, 0)], early_sizes)
                g_in_l, early_part = wgrad_cols("sgu_in_wgrad", h, dz, e2c, comm=_gather_level1([early], part=(0, 2)))
            else:
                g_in_l = wgrad_cols("sgu_in_wgrad", h, dz, e2c)
            pending.append(("in", l, g_in_l))
            comm = sibling_comm()
            if i == 1:
                comm = _merge_comm(comm, _gather_level1([early], part=(1, 2), into=[early_part]))
            dx, dyb, d_norm_mix[i], *rcv = back_cols("sgu_in_bwd", dz, transposed(g_in[l]), xs_i, norm_mix[i], dx, comm=comm)
            if i == 1:
                *rcv, early_half = rcv
            absorb(rcv)
    grad_x = dx.reshape(1, T, D)

    last_comm, last_names = chip_comm()
    late_g = [jnp.concatenate(d_norm_mix, 0), jnp.concatenate(d_norm_ffn, 0), d_final, jnp.concatenate(d_qn, 0), jnp.concatenate(d_kvn, 0)]
    late_w = [norm_mix, norm_ffn, final_norm, mla_q_norm, mla_kv_norm]
    late_m = [m_norm_mix, m_norm_ffn, m_final_norm, m_mla_q_norm, m_mla_kv_norm]
    late_v = [v_norm_mix, v_norm_ffn, v_final_norm, v_mla_q_norm, v_mla_kv_norm]
    late_sizes = [pad_to(g.size // 128, 8) for g in late_g]
    late_rows = sum(late_sizes)

    def adam_big(parts, w, m, v):
        lyr, rws, cls = w.shape
        rt = _tile(rws, 512)

        def fn(p, w_, m_, v_):
            g = (p[0].astype(F32) + p[1].astype(F32)) + (p[2].astype(F32) + p[3].astype(F32))
            return (g, *_adam(w_, g, m_, v_))

        spec = pl.BlockSpec((None, rt, cls), lambda l_, i_: (l_, i_, 0))
        return _rowwise("adam_large", fn, [parts, w, m, v], grid=(lyr, rws // rt),
                        in_specs=[pl.BlockSpec((N_CHIP, None, rt, cls), lambda l_, i_: (0, l_, i_, 0)), spec, spec, spec],
                        out_shapes=[_sds(w.shape, F32)] * 4, out_specs=[spec] * 4)

    stacked.update(dict(zip(last_names, _comm_call("grad_chip_exchange", last_comm))))
    (gathered_late,) = _all_gather("gather_small_grads", [packed(late_g, late_sizes)])
    big = {}
    big["in"] = adam_big(stacked["in"], sgu_w_in, m_sgu_w_in, v_sgu_w_in)
    big["up"] = adam_big(stacked["up"], ffn_w_up, m_ffn_w_up, v_ffn_w_up)
    big["down"] = adam_big(stacked["down"], ffn_w_down, m_ffn_w_down, v_ffn_w_down)
    big["out"] = adam_big(stacked["out"], sgu_w_out, m_sgu_w_out, v_sgu_w_out)
    big["dkv"] = adam_big(stacked["dkv"], mla_w_dkv, m_mla_w_dkv, v_mla_w_dkv)
    big["uq"] = adam_big(stacked["uq"], mla_w_uq, m_mla_w_uq, v_mla_w_uq)
    big["ukv"] = adam_big(stacked["ukv"], mla_w_ukv, m_mla_w_ukv, v_mla_w_ukv)
    big["o"] = adam_big(stacked["o"], mla_w_o, m_mla_w_o, v_mla_w_o)
    big_res = [big[nm][:4] for nm in ("dkv", "uq", "ukv", "o", "in", "out", "up", "down")]

    def sum8(p):
        return ((p[0] + p[1]) + (p[2] + p[3])) + ((p[4] + p[5]) + (p[6] + p[7]))

    def adam_packed(name, gathered, ws, ms, vs, sizes, rows, tile):
        spec = _row_spec(tile, 128)
        return _rowwise(name, lambda p, w_, m_, v_: (sum8(p), *_adam(w_, sum8(p), m_, v_)),
                        [gathered, packed(ws, sizes), packed(ms, sizes), packed(vs, sizes)], grid=(rows // tile,),
                        in_specs=[pl.BlockSpec((N_DEV, tile, 128), lambda i_: (0, i_, 0)), spec, spec, spec],
                        out_shapes=[_sds((rows, 128), F32)] * 4, out_specs=[spec] * 4)

    late_res = adam_packed("adam_small", gathered_late, late_w, late_m, late_v, late_sizes, late_rows, late_rows)
    early_res = adam_packed("adam_spatial", gathered_early, [sgu_w_spatial, sgu_b_spatial], [m_sgu_w_spatial, m_sgu_b_spatial],
                            [v_sgu_w_spatial, v_sgu_b_spatial], early_sizes[:2], early_rep, SMALL_ROWS)

    def unpack(res, sizes, k, like):
        off = sum(sizes[:k])
        return res[off:off + like.size // 128].reshape(like.shape)

    my_b = 4 * lax.axis_index("x") + 2 * lax.axis_index("y") + lax.axis_index("c")
    ln_w = jnp.concatenate([sgu_ln_g, sgu_ln_b], 0)
    ln_m = jnp.concatenate([m_sgu_ln_g, m_sgu_ln_b], 0)
    ln_v = jnp.concatenate([v_sgu_ln_g, v_sgu_ln_b], 0)
    ln_all = jnp.concatenate([gathered_early[:, early_rep:early_rep + n_sgu * E // 128], gathered_early[:, early_rep + n_ln:early_rep + n_ln + n_sgu * E // 128]], axis=1)
    ln_mine = lax.dynamic_slice_in_dim(ln_all.reshape(N_DEV, 2 * n_sgu, N_DEV, ec), my_b, 1, axis=2).reshape(N_DEV, 2 * n_sgu, ec)
    ln_g_, ln_d, ln_m2, ln_v2 = _rowwise(
        "adam_ln", lambda p, w_, m_, v_: (sum8(p), *_adam(w_, sum8(p), m_, v_)), [ln_mine, ln_w, ln_m, ln_v], grid=(1,),
        in_specs=[_const_spec(ln_mine.shape), _const_spec(ln_w.shape), _const_spec(ln_w.shape), _const_spec(ln_w.shape)],
        out_shapes=[_sds(ln_w.shape, F32)] * 4, out_specs=[_const_spec(ln_w.shape)] * 4)

    def family(pos):
        ln = [ln_g_, ln_d, ln_m2, ln_v2][pos]
        late = [unpack(late_res[pos], late_sizes, k, w_) for k, w_ in enumerate(late_w)]
        w_sp_, b_sp_ = unpack(early_res[pos], early_sizes, 0, sgu_w_spatial), unpack(early_res[pos], early_sizes, 1, sgu_b_spatial)
        bigs = [res[pos] for res in big_res]
        return [late[0], late[1], late[2], bigs[0], late[3], late[4], bigs[1], bigs[2], bigs[3],
                bigs[4], ln[:n_sgu], ln[n_sgu:], w_sp_, b_sp_, bigs[5], bigs[6], bigs[7]]

    return (loss, grad_x, *family(0), *family(1), *family(2), *family(3))
```

```python
import math

import jax
import jax.numpy as jnp
from jax import lax
from jax.experimental import pallas as pl
from jax.experimental.pallas import tpu as pltpu

F32 = jnp.float32
BF16 = jnp.bfloat16
MESH = pl.DeviceIdType.MESH

N_DEV = 8
N_CHIP = 4
HEADS = 8
NOPE = 128
ROPE = 64
VDIM = 128
QPAD = 256
Q_RANK = 256
KV_RANK = 128
LAT = Q_RANK + KV_RANK + ROPE
LAT_PAD = 512
ROPE_THETA = 10000.0
SGU_CHUNK = 128
SGU_GROUPS = 8
NORM_EPS = 1e-6
LN_EPS = 1e-5
ADAM_LR = 0.001
ADAM_B1 = 0.9
ADAM_B2 = 0.999
ADAM_EPS = 1e-08
ADAM_WD = 0.01
ADAM_STEP = 10
ATTN_SCALE = (NOPE + ROPE) ** -0.5
NEG = -1e30
EXP2_SCALE = ATTN_SCALE * math.log2(math.e)
VMEM_LIMIT = 56 * 1024 * 1024
SMALL_ROWS = 256

NN = (((1,), (0,)), ((), ()))
NT = (((1,), (1,)), ((), ()))
TN = (((0,), (0,)), ((), ()))
ANY = pl.BlockSpec(memory_space=pl.ANY)


def _pcall(body, **kw):
    return pl.pallas_call(body, **kw)


def _params(n_grid, side_effects=False):
    return pltpu.CompilerParams(dimension_semantics=("arbitrary",) * n_grid, vmem_limit_bytes=VMEM_LIMIT, has_side_effects=side_effects)


def _sds(shape, dtype):
    return jax.ShapeDtypeStruct(tuple(shape), dtype)


def _tile(n, want):
    t = min(n, want)
    assert n % t == 0, (n, want)
    return t


class _Comm:
    def __init__(self, operands, out_shapes, aliases, scratch, start, finish):
        self.operands, self.out_shapes, self.aliases, self.scratch = operands, out_shapes, aliases, scratch
        self.start, self.finish = start, finish


def _merge_comm(first, second):
    n_in, n_out, n_sc = len(first.operands), len(first.out_shapes), len(first.scratch)
    aliases = dict(first.aliases)
    aliases.update({n_in + k: n_out + v for k, v in second.aliases.items()})

    def start(ins, outs, sems):
        first.start(ins[:n_in], outs[:n_out], sems[:n_sc])
        second.start(ins[n_in:], outs[n_out:], sems[n_sc:])

    def finish(ins, outs, sems):
        first.finish(ins[:n_in], outs[:n_out], sems[:n_sc])
        second.finish(ins[n_in:], outs[n_out:], sems[n_sc:])

    return _Comm([*first.operands, *second.operands], [*first.out_shapes, *second.out_shapes], aliases,
                 [*first.scratch, *second.scratch], start, finish)


def _place():
    return lax.axis_index("x"), lax.axis_index("y"), lax.axis_index("c")


def _other_chips(x, y):
    return [(1 - x, y), (x, 1 - y), (1 - x, 1 - y)]


def _dev_index(dev):
    return 4 * dev[0] + 2 * dev[1] + dev[2]


def _comm_call(name, comm):
    c_in, c_out = len(comm.operands), len(comm.out_shapes)

    def body(*refs):
        ins, outs, sems = refs[:c_in], refs[c_in:c_in + c_out], refs[c_in + c_out:]
        comm.start(ins, outs, sems)
        comm.finish(ins, outs, sems)

    return _pcall(body, name=name, in_specs=[ANY] * c_in, out_specs=[ANY] * c_out, out_shape=comm.out_shapes,
                  scratch_shapes=comm.scratch, input_output_aliases=dict(comm.aliases),
                  compiler_params=pltpu.CompilerParams(has_side_effects=True))(*comm.operands)


def _call(name, body, operands, in_specs, out_shapes, out_specs, scratch, grid, comm=None):
    if comm is None:
        return _pcall(body, name=name, grid=grid, in_specs=in_specs, out_specs=out_specs, out_shape=out_shapes,
                      scratch_shapes=scratch, compiler_params=_params(len(grid)))(*operands)
    n_in, n_out, n_sc = len(operands), len(out_shapes), len(scratch)
    c_in, c_out = len(comm.operands), len(comm.out_shapes)

    def hosted(*refs):
        ins, cins = refs[:n_in], refs[n_in:n_in + c_in]
        o0 = n_in + c_in
        outs, couts = refs[o0:o0 + n_out], refs[o0 + n_out:o0 + n_out + c_out]
        rest = refs[o0 + n_out + c_out:]
        sc, csems = rest[:n_sc], rest[n_sc:]
        first = pl.program_id(0) == 0
        last = pl.program_id(0) == grid[0] - 1
        for d in range(1, len(grid)):
            first = jnp.logical_and(first, pl.program_id(d) == 0)
            last = jnp.logical_and(last, pl.program_id(d) == grid[d] - 1)

        @pl.when(first)
        def _():
            comm.start(cins, couts, csems)

        body(*ins, *outs, *sc)

        @pl.when(last)
        def _():
            comm.finish(cins, couts, csems)

    return _pcall(hosted, name=name, grid=grid, in_specs=[*in_specs, *[ANY] * c_in], out_specs=[*out_specs, *[ANY] * c_out],
                  out_shape=[*out_shapes, *comm.out_shapes], scratch_shapes=[*scratch, *comm.scratch],
                  input_output_aliases={n_in + k: n_out + v for k, v in comm.aliases.items()},
                  compiler_params=_params(len(grid), side_effects=True))(*operands, *comm.operands)


def _gather_level1(shards, part=(0, 1), into=None):
    n = len(shards)
    k_part, m_part = part

    def copies(ins, outs, sems):
        send_sems, recv_sems, local_sems = sems
        x, y, c = _place()
        me, sibling = (x, y, c), (x, y, 1 - c)
        chips = _other_chips(x, y)

        def rows(a):
            r = shards[a].shape[0] // m_part
            return pl.ds(k_part * r, r)

        def copy(a, k, block, to, own=False):
            slot = outs[a].at[_dev_index(block), rows(a)]
            return pltpu.make_async_remote_copy(src_ref=ins[a].at[rows(a)] if own else slot, dst_ref=slot, send_sem=send_sems.at[a, k],
                                                recv_sem=recv_sems.at[a, k], device_id=to, device_id_type=MESH)

        mine = [pltpu.make_async_copy(ins[a].at[rows(a)], outs[a].at[_dev_index(me), rows(a)], local_sems.at[a]) for a in range(n)]
        sends = [copy(a, 1 + j, me, (*chip, c), own=True) for j, chip in enumerate(chips) for a in range(n)]
        sends += [copy(a, 0, me, sibling, own=True) for a in range(n)]
        recvs = [copy(a, 1 + j, (*chip, c), me) for j, chip in enumerate(chips) for a in range(n)]
        recvs += [copy(a, 0, sibling, me) for a in range(n)]
        return mine, sends, recvs

    def start(ins, outs, sems):
        mine, sends, _ = copies(ins, outs, sems)
        for cp in mine + sends:
            cp.start()

    def finish(ins, outs, sems):
        mine, sends, recvs = copies(ins, outs, sems)
        for cp in recvs:
            cp.wait_recv()
        for cp in sends:
            cp.wait_send()
        for cp in mine:
            cp.wait()

    return _Comm([*shards, *(into or [])], [_sds((N_DEV, *a.shape), a.dtype) for a in shards], {n + a: a for a in range(n)} if into else {},
                 [pltpu.SemaphoreType.DMA((n, 4)), pltpu.SemaphoreType.DMA((n, 4)), pltpu.SemaphoreType.DMA((n,))], start, finish)


def _gather_level2(bufs):
    n = len(bufs)

    def copies(outs, sems):
        send_sems, recv_sems = sems
        x, y, c = _place()
        sibling = (x, y, 1 - c)
        sends, recvs = [], []
        for j, chip in enumerate(_other_chips(x, y)):
            for a in range(n):
                have, want = outs[a].at[_dev_index((*chip, c))], outs[a].at[_dev_index((*chip, 1 - c))]
                sends.append(pltpu.make_async_remote_copy(src_ref=have, dst_ref=have, send_sem=send_sems.at[a, j], recv_sem=recv_sems.at[a, j],
                                                          device_id=sibling, device_id_type=MESH))
                recvs.append(pltpu.make_async_remote_copy(src_ref=want, dst_ref=want, send_sem=send_sems.at[a, j], recv_sem=recv_sems.at[a, j],
                                                          device_id=sibling, device_id_type=MESH))
        return sends, recvs

    def start(ins, outs, sems):
        for cp in copies(outs, sems)[0]:
            cp.start()

    def finish(ins, outs, sems):
        sends, recvs = copies(outs, sems)
        for cp in recvs:
            cp.wait_recv()
        for cp in sends:
            cp.wait_send()

    return _Comm(bufs, [_sds(b.shape, b.dtype) for b in bufs], {a: a for a in range(n)},
                 [pltpu.SemaphoreType.DMA((n, 3)), pltpu.SemaphoreType.DMA((n, 3))], start, finish)


def _all_gather(name, arrays):
    n = len(arrays)

    def body(*refs):
        ins = refs[:n]
        outs = refs[n:2 * n]
        send_sems, recv_sems, local_sems = refs[2 * n:]
        x, y, c = _place()
        me, sibling = (x, y, c), (x, y, 1 - c)
        chips = _other_chips(x, y)

        def copy(a, k, block, to, src=None):
            slot = outs[a].at[_dev_index(block)]
            return pltpu.make_async_remote_copy(src_ref=slot if src is None else src, dst_ref=slot, send_sem=send_sems.at[a, k],
                                                recv_sem=recv_sems.at[a, k], device_id=to, device_id_type=MESH)

        mine = [pltpu.make_async_copy(ins[a], outs[a].at[_dev_index(me)], local_sems.at[a]) for a in range(n)]
        for cp in mine:
            cp.start()
        first = []
        for j, chip in enumerate(chips):
            first += [copy(a, 1 + j, me, (*chip, c), src=ins[a]) for a in range(n)]
        first += [copy(a, 0, me, sibling, src=ins[a]) for a in range(n)]
        for cp in first:
            cp.start()
        passed = []
        for j, chip in enumerate(chips):
            for a in range(n):
                copy(a, 1 + j, (*chip, c), me).wait_recv()
                fwd = copy(a, 4 + j, (*chip, c), sibling)
                fwd.start()
                passed.append(fwd)
        for a in range(n):
            copy(a, 0, sibling, me).wait_recv()
            for j, chip in enumerate(chips):
                copy(a, 4 + j, (*chip, 1 - c), me).wait_recv()
        for cp in first + passed:
            cp.wait_send()
        for cp in mine:
            cp.wait()

    return _pcall(
        body, name=name, in_specs=[ANY] * n, out_specs=[ANY] * n,
        out_shape=[_sds((N_DEV, *a.shape), a.dtype) for a in arrays],
        scratch_shapes=[pltpu.SemaphoreType.DMA((n, 7)), pltpu.SemaphoreType.DMA((n, 7)), pltpu.SemaphoreType.DMA((n,))],
        compiler_params=pltpu.CompilerParams(has_side_effects=True),
    )(*arrays)


def _sibling_exchange(grads):
    n = len(grads)

    def start(ins, outs, sems):
        send_sems, recv_sems = sems
        x, y, c = _place()
        for a in range(n):
            for ch in range(N_CHIP):
                pltpu.make_async_remote_copy(src_ref=ins[a].at[2 * ch + 1 - c], dst_ref=outs[a].at[ch], send_sem=send_sems.at[a],
                                             recv_sem=recv_sems.at[a], device_id=(x, y, 1 - c), device_id_type=MESH).start()

    def finish(ins, outs, sems):
        send_sems, recv_sems = sems
        x, y, c = _place()
        for a in range(n):
            pltpu.make_async_remote_copy(src_ref=outs[a], dst_ref=outs[a], send_sem=send_sems.at[a], recv_sem=recv_sems.at[a],
                                         device_id=(x, y, 1 - c), device_id_type=MESH).wait()

    return _Comm(grads, [_sds((N_CHIP, *g.shape[1:]), g.dtype) for g in grads], {},
                 [pltpu.SemaphoreType.DMA((n,)), pltpu.SemaphoreType.DMA((n,))], start, finish)


def _chip_exchange(parts, slots, layers, stacked):
    n = len(parts)
    names = []
    for nm, _ in slots:
        if nm not in names:
            names.append(nm)
    shapes = {nm: _sds((N_CHIP, layers[nm], *parts[a].shape[1:]), parts[a].dtype) for a, (nm, _) in enumerate(slots)}
    kept = [nm for nm in names if stacked.get(nm) is not None]
    aliases = {n + k: names.index(nm) for k, nm in enumerate(kept)}

    def copies(ins, outs, sems):
        send_sems, recv_sems, local_sems = sems
        x, y, c = _place()
        mine = 2 * x + y
        local, sends, recvs = [], [], []
        for a, (nm, l) in enumerate(slots):
            buf = outs[names.index(nm)]
            local.append(pltpu.make_async_copy(ins[a].at[mine], buf.at[mine, l], local_sems.at[a]))
            for j, chip in enumerate(_other_chips(x, y)):
                theirs = buf.at[2 * chip[0] + chip[1], l]
                sends.append(pltpu.make_async_remote_copy(src_ref=ins[a].at[2 * chip[0] + chip[1]], dst_ref=buf.at[mine, l], send_sem=send_sems.at[a, j],
                                                          recv_sem=recv_sems.at[a, j], device_id=(*chip, c), device_id_type=MESH))
                recvs.append(pltpu.make_async_remote_copy(src_ref=theirs, dst_ref=theirs, send_sem=send_sems.at[a, j],
                                                          recv_sem=recv_sems.at[a, j], device_id=(*chip, c), device_id_type=MESH))
        return local, sends, recvs

    def start(ins, outs, sems):
        local, sends, _ = copies(ins, outs, sems)
        for cp in local + sends:
            cp.start()

    def finish(ins, outs, sems):
        local, sends, recvs = copies(ins, outs, sems)
        for cp in recvs:
            cp.wait_recv()
        for cp in sends:
            cp.wait_send()
        for cp in local:
            cp.wait()

    comm = _Comm([*parts, *[stacked[nm] for nm in kept]], [shapes[nm] for nm in names], aliases,
                 [pltpu.SemaphoreType.DMA((n, 3)), pltpu.SemaphoreType.DMA((n, 3)), pltpu.SemaphoreType.DMA((n,))], start, finish)
    return comm, names


def _matmul(name, a, b, extras, *, grid, a_spec, b_spec, extra_specs, out_shapes, out_specs, dims, k_axis=None, nk=1,
            acc_shape=None, epilogue=None, comm=None, n_sum=0, write=None):
    n_extra = len(extras)
    n_out = len(out_shapes)

    def body(*refs):
        a_ref, b_ref = refs[0], refs[1]
        ex = refs[2:2 + n_extra]
        outs = refs[2 + n_extra:2 + n_extra + n_out]
        prod = lax.dot_general(a_ref[...], b_ref[...], dims, preferred_element_type=F32)

        def finish(acc):
            if write is not None:
                write(outs, acc, *[e[...] for e in ex])
                return
            res = epilogue(acc, *[e[...] for e in ex]) if epilogue is not None else (acc,)
            first = None
            for d in range(len(grid)):
                if d != k_axis:
                    here = pl.program_id(d) == 0
                    first = here if first is None else jnp.logical_and(first, here)
            for idx, (o, r) in enumerate(zip(outs, res)):
                if idx < n_out - n_sum:
                    o[...] = r.astype(o.dtype)
                else:
                    @pl.when(first)
                    def _(o=o, r=r):
                        o[...] = r.astype(o.dtype)

                    @pl.when(jnp.logical_not(first))
                    def _(o=o, r=r):
                        o[...] += r.astype(o.dtype)

        if k_axis is None:
            finish(prod)
        else:
            acc_ref = refs[-1]
            k = pl.program_id(k_axis)

            @pl.when(k == 0)
            def _():
                acc_ref[...] = prod

            @pl.when(k > 0)
            def _():
                acc_ref[...] += prod

            @pl.when(k == nk - 1)
            def _():
                finish(acc_ref[...])

    scratch = [] if k_axis is None else [pltpu.VMEM(acc_shape, F32)]
    return _call(name, body, [a, b, *extras], [a_spec, b_spec, *extra_specs], list(out_shapes), list(out_specs), scratch, grid, comm)


def _rowwise(name, fn, operands, *, grid, in_specs, out_shapes, out_specs, n_acc=0, grid_spec_prefetch=None, comm=None):
    n_in = len(operands)
    n_out = len(out_shapes)
    n_pre = 0 if grid_spec_prefetch is None else 1

    def body(*refs):
        refs = refs[n_pre:]
        ins = refs[:n_in]
        outs = refs[n_in:n_in + n_out]
        res = fn(*[r[...] for r in ins])
        if not isinstance(res, (tuple, list)):
            res = (res,)
        first = pl.program_id(0) == 0
        for d in range(1, len(grid)):
            first = jnp.logical_and(first, pl.program_id(d) == 0)
        for idx, (o, r) in enumerate(zip(outs, res)):
            if idx < n_out - n_acc:
                o[...] = r.astype(o.dtype)
            else:
                @pl.when(first)
                def _(o=o, r=r):
                    o[...] = r.astype(o.dtype)

                @pl.when(jnp.logical_not(first))
                def _(o=o, r=r):
                    o[...] += r.astype(o.dtype)

    if comm is not None:
        return _call(name, body, list(operands), list(in_specs), list(out_shapes), list(out_specs), [], grid, comm)
    if grid_spec_prefetch is None:
        return _pcall(body, name=name, grid=grid, in_specs=in_specs, out_specs=out_specs, out_shape=out_shapes,
                      compiler_params=_params(len(grid)))(*operands)
    gs = pltpu.PrefetchScalarGridSpec(num_scalar_prefetch=1, grid=grid, in_specs=in_specs, out_specs=out_specs)
    return _pcall(body, name=name, grid_spec=gs, out_shape=out_shapes,
                  compiler_params=_params(len(grid)))(grid_spec_prefetch, *operands)


def _row_spec(tm, w):
    return pl.BlockSpec((tm, w), lambda i: (i, 0))


def _const_spec(shape):
    nd = len(shape)
    return pl.BlockSpec(tuple(shape), lambda *_: (0,) * nd)


def _rms_fwd(x, g):
    r = lax.rsqrt(jnp.mean(x * x, axis=-1, keepdims=True) + NORM_EPS)
    return x * r * g


def _rms_bwd(x, g, dy):
    r = lax.rsqrt(jnp.mean(x * x, axis=-1, keepdims=True) + NORM_EPS)
    xh = x * r
    u = dy * g
    dx = r * (u - xh * jnp.mean(u * xh, axis=-1, keepdims=True))
    dg = jnp.sum(dy * xh, axis=0, keepdims=True)
    return dx, dg


def _gelu_and_grad(z):
    cdf = 0.5 * (1.0 + lax.erf(z * (2.0 ** -0.5)))
    return cdf + z * jnp.exp(-0.5 * z * z) * ((2.0 * math.pi) ** -0.5), z * cdf


def _rope_fwd(x, cc, sa, sb):
    return x * cc + pltpu.roll(x, 96, 1) * sa + pltpu.roll(x, 32, 1) * sb


def _rope_bwd(d, cc, sa, sb):
    return d * cc + pltpu.roll(d * sa, 32, 1) + pltpu.roll(d * sb, 96, 1)


def _adam(w, g, m, v):
    m = ADAM_B1 * m + (1.0 - ADAM_B1) * g
    v = ADAM_B2 * v + (1.0 - ADAM_B2) * (g * g)
    m_hat = m / (1.0 - ADAM_B1 ** ADAM_STEP)
    v_hat = v / (1.0 - ADAM_B2 ** ADAM_STEP)
    delta = -ADAM_LR * (m_hat / (jnp.sqrt(v_hat) + ADAM_EPS) + ADAM_WD * w)
    return delta, m, v


def _flash_fwd(q, k, vt, tq, comm=None):
    h, t = vt.shape[0], q.shape[0]
    nq = t // tq

    chunk_blocks = [c for c in (4, 2) if c < nq]

    def body(q_ref, k_ref, vt_ref, o_ref, lse_ref, m_ref, l_ref, acc_ref):
        qi = pl.program_id(1)
        m_ref[...] = jnp.full((1, tq), NEG, F32)
        l_ref[...] = jnp.zeros((1, tq), F32)
        acc_ref[...] = jnp.zeros((VDIM, tq), F32)

        def update(kb0, nblk, masked):
            kb = k_ref[pl.ds(pl.multiple_of(kb0 * tq, tq), nblk * tq), :]
            st = lax.dot_general(kb, q_ref[...], NT, preferred_element_type=F32)
            if masked:
                key = lax.broadcasted_iota(jnp.int32, (nblk * tq, tq), 0) - (nblk - 1) * tq
                qry = lax.broadcasted_iota(jnp.int32, (nblk * tq, tq), 1)
                st = jnp.where(key <= qry, st, NEG)
            m_old = m_ref[...]
            m_new = jnp.maximum(m_old, jnp.max(st, axis=0, keepdims=True))
            alpha = jnp.exp2((m_old - m_new) * EXP2_SCALE)
            pt = jnp.exp2((st - m_new) * EXP2_SCALE)
            l_ref[...] = alpha * l_ref[...] + jnp.sum(pt, axis=0, keepdims=True)
            ptb = pt.astype(BF16)
            pv = lax.dot_general(vt_ref[kb0], ptb[:tq], NN, preferred_element_type=F32)
            for j in range(1, nblk):
                pv += lax.dot_general(vt_ref[kb0 + j], ptb[j * tq:(j + 1) * tq], NN, preferred_element_type=F32)
            acc_ref[...] = alpha * acc_ref[...] + pv
            m_ref[...] = m_new

        start = jnp.int32(0)
        for c in chunk_blocks:
            take = (qi & c) != 0

            @pl.when(take)
            def _(start=start, c=c):
                update(start, c, False)

            start = start + jnp.where(take, c, 0)
        if nq > 1:
            @pl.when((qi & 1) != 0)
            def _():
                update(qi - 1, 2, True)

            @pl.when((qi & 1) == 0)
            def _():
                update(qi, 1, True)
        else:
            update(qi, 1, True)
        l = l_ref[...]
        o_ref[...] = (acc_ref[...] / l).T.astype(o_ref.dtype)
        lse_ref[...] = m_ref[...] * EXP2_SCALE + jnp.log2(l)

    return _call(
        "flash_fwd", body, [q, k, vt],
        [pl.BlockSpec((tq, QPAD), lambda hh, i: (i, hh)),
         pl.BlockSpec((t, QPAD), lambda hh, i: (0, hh)),
         pl.BlockSpec((None, nq, VDIM, tq), lambda hh, i: (hh, 0, 0, 0))],
        [_sds((t, h * VDIM), BF16), _sds((h, nq, 1, tq), F32)],
        [pl.BlockSpec((tq, VDIM), lambda hh, i: (i, hh)),
         pl.BlockSpec((None, None, 1, tq), lambda hh, i: (hh, i, 0, 0))],
        [pltpu.VMEM((1, tq), F32), pltpu.VMEM((1, tq), F32), pltpu.VMEM((VDIM, tq), F32)], (h, nq), comm)


def _flash_bwd(q, k, v, o, do, lse, tabs, tq, comm=None):
    t = q.shape[0]
    h = q.shape[1] // QPAD
    nq = t // tq

    def body(q_ref, k_ref, v_ref, o_ref, do_ref, lse_ref, cc_ref, sa_ref, sb_ref, dq_ref, dk_out, dv_out, delta_ref, dqt_ref, dk_ref, dv_ref):
        kj = pl.program_id(1)

        @pl.when(kj == 0)
        def _():
            dqt_ref[...] = jnp.zeros_like(dqt_ref)
            ones = jnp.ones((8, VDIM), BF16)
            for qi in range(nq):
                rows = pl.ds(qi * tq, tq)
                prod = do_ref[rows, :].astype(F32) * o_ref[rows, :].astype(F32)
                hi = prod.astype(BF16)
                lo = (prod - hi.astype(F32)).astype(BF16)
                delta_ref[qi] = (lax.dot_general(ones, hi, NT, preferred_element_type=F32)
                                 + lax.dot_general(ones, lo, NT, preferred_element_type=F32))

        kb = k_ref[...]
        vb = v_ref[...]
        kbt = kb.astype(F32).T.astype(BF16)
        dk_ref[...] = jnp.zeros_like(dk_ref)
        dv_ref[...] = jnp.zeros_like(dv_ref)

        def step(q0, nblk, masked):
            rows = pl.ds(pl.multiple_of(q0 * tq, tq), nblk * tq)
            qb = q_ref[rows, :]
            dob = do_ref[rows, :]
            lse = jnp.concatenate([lse_ref[q0 + j] for j in range(nblk)], axis=1)
            delta = jnp.concatenate([delta_ref[q0 + j, pl.ds(0, 1), :] for j in range(nblk)], axis=1)
            st = lax.dot_general(kb, qb, NT, preferred_element_type=F32)
            pt = jnp.exp2(st * EXP2_SCALE - lse)
            if masked:
                key = lax.broadcasted_iota(jnp.int32, (tq, nblk * tq), 0)
                qry = lax.broadcasted_iota(jnp.int32, (tq, nblk * tq), 1)
                pt = jnp.where(key <= qry, pt, 0.0)
            dv_ref[...] += lax.dot_general(pt.astype(BF16), dob, NN, preferred_element_type=F32)
            dpt = lax.dot_general(vb, dob, NT, preferred_element_type=F32)
            dst = (pt * (dpt - delta) * ATTN_SCALE).astype(BF16)
            dk_ref[...] += lax.dot_general(dst, qb, NN, preferred_element_type=F32)
            dqt = lax.dot_general(kbt, dst, NN, preferred_element_type=F32)
            for j in range(nblk):
                dqt_ref[q0 + j] += dqt[:, j * tq:(j + 1) * tq]

        later = nq - 1 - kj
        if nq > 1:
            @pl.when((later & 1) != 0)
            def _():
                step(kj, 2, True)

            @pl.when((later & 1) == 0)
            def _():
                step(kj, 1, True)
        else:
            step(kj, 1, True)
        start = kj + 1 + (later & 1)
        for c in [c for c in (2, 4) if c < nq]:
            take = (later & c) != 0

            @pl.when(take)
            def _(start=start, c=c):
                step(start, c, False)

            start = start + jnp.where(take, c, 0)
        dk_out[...] = dk_ref[...].astype(BF16)
        dv_out[...] = dv_ref[...].astype(BF16)

        @pl.when(kj == nq - 1)
        def _():
            for qi in range(nq):
                rows = pl.ds(qi * tq, tq)
                d = dqt_ref[qi].T
                roped = _rope_bwd(d[:, NOPE:], cc_ref[rows, :], sa_ref[rows, :], sb_ref[rows, :])
                dq_ref[rows, :] = jnp.concatenate([d[:, :NOPE], roped], axis=1).astype(BF16)

    head_q = pl.BlockSpec((t, QPAD), lambda hh, j: (0, hh))
    head_v = pl.BlockSpec((t, VDIM), lambda hh, j: (0, hh))
    table = pl.BlockSpec((t, 128), lambda hh, j: (0, 0))
    return _call(
        "flash_bwd", body, [q, k, v, o, do, lse, *tabs],
        [head_q, pl.BlockSpec((tq, QPAD), lambda hh, j: (j, hh)), pl.BlockSpec((tq, VDIM), lambda hh, j: (j, hh)), head_v, head_v,
         pl.BlockSpec((None, nq, 1, tq), lambda hh, j: (hh, 0, 0, 0)), table, table, table],
        [_sds((t, h * QPAD), BF16), _sds((t, h * QPAD), BF16), _sds((t, h * VDIM), BF16)],
        [head_q, pl.BlockSpec((tq, QPAD), lambda hh, j: (j, hh)), pl.BlockSpec((tq, VDIM), lambda hh, j: (j, hh))],
        [pltpu.VMEM((nq, 8, tq), F32), pltpu.VMEM((nq, QPAD, tq), F32), pltpu.VMEM((tq, QPAD), F32), pltpu.VMEM((tq, VDIM), F32)], (h, nq), comm)


def _tril_bf16(w):
    row = lax.broadcasted_iota(jnp.int32, w.shape, 0)
    col = lax.broadcasted_iota(jnp.int32, w.shape, 1)
    return jnp.where(col <= row, w, 0.0).astype(BF16)


def _layer_norm_parts(v0):
    mu = jnp.mean(v0, axis=-1, keepdims=True)
    vc = v0 - mu
    rstd = lax.rsqrt(jnp.mean(vc * vc, axis=-1, keepdims=True) + LN_EPS)
    return vc * rstd, rstd


def _sgu_mid_fwd(ge, ln_g, ln_b, w_sp, b_sp, chunks_per_step):
    t, e2 = ge.shape
    e = e2 // 2
    gd = e // SGU_GROUPS
    rows = SGU_CHUNK * chunks_per_step

    def body(u_ref, v_ref, g_ref, b_ref, w_ref, bs_ref, gate_ref):
        for ck in range(chunks_per_step):
            r = pl.ds(ck * SGU_CHUNK, SGU_CHUNK)
            xh, _ = _layer_norm_parts(v_ref[r, :].astype(F32))
            v1 = (xh * g_ref[...] + b_ref[...]).astype(BF16)
            for g in range(SGU_GROUPS):
                cols = pl.ds(g * gd, gd)
                mixed = lax.dot_general(_tril_bf16(w_ref[g]), v1[:, g * gd:(g + 1) * gd], NN, preferred_element_type=F32) + bs_ref[g]
                gate_ref[r, cols] = (u_ref[r, cols].astype(F32) * mixed).astype(BF16)

    return _pcall(
        body, name="sgu_mid_fwd", grid=(t // rows,),
        in_specs=[pl.BlockSpec((rows, e), lambda i: (i, 0)), pl.BlockSpec((rows, e), lambda i: (i, 1)),
                  _const_spec((1, e)), _const_spec((1, e)), _const_spec(w_sp.shape), _const_spec(b_sp.shape)],
        out_specs=pl.BlockSpec((rows, e), lambda i: (i, 0)),
        out_shape=_sds((t, e), BF16), compiler_params=_params(1),
    )(ge, ge, ln_g, ln_b, w_sp, b_sp)


def _sgu_mid_bwd(ge, gp, dgate, ln_g, ln_b, w_sp, b_sp, chunks_per_step):
    t, e2 = ge.shape
    e = e2 // 2
    gd = e // SGU_GROUPS
    rows = SGU_CHUNK * chunks_per_step

    def body(u_ref, v_ref, zu_ref, zv_ref, dg_ref, g_ref, b_ref, w_ref, bs_ref, dz_ref, dw_ref, dbs_ref, dlg_ref, dlb_ref):
        @pl.when(pl.program_id(0) == 0)
        def _():
            dw_ref[...] = jnp.zeros_like(dw_ref)
            dbs_ref[...] = jnp.zeros_like(dbs_ref)
            dlg_ref[...] = jnp.zeros_like(dlg_ref)
            dlb_ref[...] = jnp.zeros_like(dlb_ref)

        for ck in range(chunks_per_step):
            r = pl.ds(ck * SGU_CHUNK, SGU_CHUNK)
            xh, rstd = _layer_norm_parts(v_ref[r, :].astype(F32))
            v1 = (xh * g_ref[...] + b_ref[...]).astype(BF16)
            dv1_parts = []
            for g in range(SGU_GROUPS):
                cols = pl.ds(g * gd, gd)
                wc = _tril_bf16(w_ref[g])
                v1g = v1[:, g * gd:(g + 1) * gd]
                mixed = lax.dot_general(wc, v1g, NN, preferred_element_type=F32) + bs_ref[g]
                dgate = dg_ref[r, cols].astype(F32)
                dmixed = dgate * u_ref[r, cols].astype(F32)
                du = dgate * mixed
                dz_ref[r, cols] = (du * zu_ref[r, cols].astype(F32)).astype(BF16)
                dbs_ref[g] += jnp.sum(dmixed, axis=1, keepdims=True)
                dmb = dmixed.astype(BF16)
                dwg = lax.dot_general(dmb, v1g, NT, preferred_element_type=F32)
                row = lax.broadcasted_iota(jnp.int32, dwg.shape, 0)
                col = lax.broadcasted_iota(jnp.int32, dwg.shape, 1)
                dw_ref[g] += jnp.where(col <= row, dwg, 0.0)
                dv1_parts.append(lax.dot_general(wc, dmb, TN, preferred_element_type=F32))
            dv1 = jnp.concatenate(dv1_parts, axis=1)
            dlg_ref[...] += jnp.sum(dv1 * xh, axis=0, keepdims=True)
            dlb_ref[...] += jnp.sum(dv1, axis=0, keepdims=True)
            dxh = dv1 * g_ref[...]
            dv0 = rstd * (dxh - jnp.mean(dxh, axis=-1, keepdims=True) - xh * jnp.mean(dxh * xh, axis=-1, keepdims=True))
            dz_ref[r, pl.ds(e, e)] = (dv0 * zv_ref[r, :].astype(F32)).astype(BF16)

    half0 = pl.BlockSpec((rows, e), lambda i: (i, 0))
    half1 = pl.BlockSpec((rows, e), lambda i: (i, 1))
    return _pcall(
        body, name="sgu_mid_bwd", grid=(t // rows,),
        in_specs=[half0, half1, half0, half1, half0, _const_spec((1, e)), _const_spec((1, e)), _const_spec(w_sp.shape), _const_spec(b_sp.shape)],
        out_specs=[pl.BlockSpec((rows, e2), lambda i: (i, 0)), _const_spec(w_sp.shape), _const_spec(b_sp.shape), _const_spec((1, e)), _const_spec((1, e))],
        out_shape=[_sds((t, e2), BF16), _sds(w_sp.shape, F32), _sds(b_sp.shape, F32), _sds((1, e), F32), _sds((1, e), F32)],
        compiler_params=_params(1),
    )(ge, ge, gp, gp, dgate, ln_g, ln_b, w_sp, b_sp)


def kernel(x, positions, norm_mix, norm_ffn, final_norm, mla_w_dkv, mla_q_norm, mla_kv_norm, mla_w_uq, mla_w_ukv, mla_w_o, sgu_w_in, sgu_ln_g, sgu_ln_b, sgu_w_spatial, sgu_b_spatial, sgu_w_out, ffn_w_up, ffn_w_down, loss_target, m_norm_mix, m_norm_ffn, m_final_norm, m_mla_w_dkv, m_mla_q_norm, m_mla_kv_norm, m_mla_w_uq, m_mla_w_ukv, m_mla_w_o, m_sgu_w_in, m_sgu_ln_g, m_sgu_ln_b, m_sgu_w_spatial, m_sgu_b_spatial, m_sgu_w_out, m_ffn_w_up, m_ffn_w_down, v_norm_mix, v_norm_ffn, v_final_norm, v_mla_w_dkv, v_mla_q_norm, v_mla_kv_norm, v_mla_w_uq, v_mla_w_ukv, v_mla_w_o, v_sgu_w_in, v_sgu_ln_g, v_sgu_ln_b, v_sgu_w_spatial, v_sgu_b_spatial, v_sgu_w_out, v_ffn_w_up, v_ffn_w_down):
    _, T, D = x.shape
    depth = norm_mix.shape[0]
    n_mla, n_sgu = mla_w_dkv.shape[0], sgu_w_in.shape[0]
    assert depth % 2 == 0
    FF = ffn_w_up.shape[2] * N_DEV
    E = sgu_w_out.shape[1] * N_DEV
    ffc, ec, e2c = FF // N_DEV, E // N_DEV, 2 * E // N_DEV
    dc = D // N_DEV
    OW = HEADS * VDIM
    HW = HEADS * QPAD
    owc = OW // N_DEV
    tm = _tile(T, 1024)
    tb = _tile(T, 4096)
    tk = _tile(T, 512)
    tq = _tile(T, 512)
    ts = _tile(T, 256)
    nt = T // tm
    x2 = x.reshape(T, D)
    tgt = loss_target.reshape(T, D)
    cidx = lax.axis_index("c").astype(jnp.int32).reshape(1)

    ln_local = jnp.concatenate([sgu_ln_g, sgu_ln_b, jnp.zeros((8 - 2 * n_sgu, ec), F32)], axis=0)
    mla_sh = [[w[l].astype(BF16) for w in (mla_w_dkv, mla_w_uq, mla_w_ukv, mla_w_o)] for l in range(n_mla)]

    def mla_layouts(g_dkv, g_uq, g_ukv, g_o):
        w_dkv = jnp.pad(g_dkv.reshape(1, D, LAT), ((0, 0), (0, 0), (0, LAT_PAD - LAT)))
        w_uq = jnp.pad(g_uq, ((0, 0), (0, 0), (0, QPAD - NOPE - ROPE))).transpose(1, 0, 2).reshape(1, Q_RANK, HEADS * QPAD)
        w_ukv = g_ukv.transpose(1, 0, 2).reshape(1, KV_RANK, HEADS * (NOPE + VDIM))
        return w_dkv, w_uq, w_ukv, g_o.reshape(1, HEADS * VDIM, D)

    mla_w = [None] * n_mla
    small_later = [a for l in range(1, n_mla) for a in mla_sh[l]] + [ln_local]
    ln_g_full, ln_b_full = [None] * n_sgu, [None] * n_sgu
    b_sp = sgu_b_spatial.reshape(n_sgu, SGU_GROUPS, SGU_CHUNK, 1)
    up_sh = [ffn_w_up[i].astype(BF16) for i in range(depth)]
    down_sh = [ffn_w_down[i].astype(BF16) for i in range(depth)]
    in_sh = [sgu_w_in[l].astype(BF16) for l in range(n_sgu)]
    out_sh = [sgu_w_out[l].astype(BF16) for l in range(n_sgu)]
    g_up, g_down, g_in, g_out = [None] * depth, [None] * depth, [None] * n_sgu, [None] * n_sgu

    inv_freq = ROPE_THETA ** (-jnp.arange(0, ROPE, 2, dtype=F32) / ROPE)
    zeros32 = jnp.zeros((ROPE // 2,), F32)
    inv128 = jnp.concatenate([inv_freq, inv_freq, zeros32, zeros32]).reshape(1, 128)
    sel_a = jnp.concatenate([-jnp.ones((32,), F32), zeros32, zeros32, zeros32]).reshape(1, 128)
    sel_b = jnp.concatenate([zeros32, jnp.ones((32,), F32), zeros32, zeros32]).reshape(1, 128)
    sel_c = jnp.concatenate([jnp.ones((64,), F32), zeros32, zeros32]).reshape(1, 128)

    def rope_tables(pos, inv, sa, sb, sc):
        ang = pos.astype(F32) * inv
        cs, sn = jnp.cos(ang), jnp.sin(ang)
        return cs * sc, sn * sa, sn * sb

    t_cc, t_sa, t_sb, *first_half = _rowwise(
        "rope_tables", rope_tables, [positions.reshape(T, 1), inv128, sel_a, sel_b, sel_c], grid=(nt,),
        in_specs=[_row_spec(tm, 1)] + [_const_spec((1, 128))] * 4,
        out_shapes=[_sds((T, 128), F32)] * 3, out_specs=[_row_spec(tm, 128)] * 3, comm=_gather_level1(mla_sh[0]))
    tab_specs = [_row_spec(tm, 128)] * 3

    def rmsnorm(xv, g, comm):
        return _rowwise("rmsnorm", lambda a, gg: _rms_fwd(a, gg), [xv, g.reshape(1, D)], grid=(nt,),
                        in_specs=[_row_spec(tm, D), _const_spec((1, D))], out_shapes=[_sds((T, D), BF16)], out_specs=[_row_spec(tm, D)], comm=comm)

    def proj_cols(name, h, gw, nc, epilogue, n_out, comm=None):
        return _matmul(name, h, gw, [], grid=(N_DEV, T // tb),
                       a_spec=pl.BlockSpec((tb, D), lambda j, i: (i, 0)),
                       b_spec=pl.BlockSpec((None, D, nc), lambda j, i: (j, 0, 0)), extra_specs=[],
                       out_shapes=[_sds((T, nc * N_DEV), BF16)] * n_out, out_specs=[pl.BlockSpec((tb, nc), lambda j, i: (i, j))] * n_out,
                       dims=NN, epilogue=epilogue, comm=comm)

    def residual_norm(acc, xr, g):
        xn = acc + xr
        return xn, _rms_fwd(xn, g)

    def proj_rows_residual(name, a, gw, xres, g_next, comm=None):
        kk_ = a.shape[1]
        return _matmul(name, a, gw.reshape(kk_, D), [xres, g_next.reshape(1, D)], grid=(T // tk,),
                       a_spec=_row_spec(tk, kk_), b_spec=_const_spec((kk_, D)), extra_specs=[_row_spec(tk, D), _const_spec((1, D))],
                       out_shapes=[_sds((T, D), F32), _sds((T, D), BF16)], out_specs=[_row_spec(tk, D)] * 2,
                       dims=NN, epilogue=residual_norm, comm=comm)

    def back_rows(name, dy, gw, kc, extras, epilogue, comm=None):
        return _matmul(name, dy, gw, extras, grid=(N_DEV, T // tb),
                       a_spec=pl.BlockSpec((tb, D), lambda j, i: (i, 0)),
                       b_spec=pl.BlockSpec((None, kc, D), lambda j, i: (j, 0, 0)),
                       extra_specs=[pl.BlockSpec((tb, kc), lambda j, i: (i, j))] * len(extras),
                       out_shapes=[_sds((T, kc * N_DEV), BF16)], out_specs=[pl.BlockSpec((tb, kc), lambda j, i: (i, j))],
                       dims=NT, epilogue=epilogue, comm=comm)

    def norm_bwd_epilogue(dh, xv, g, dxi):
        dxn, dg = _rms_bwd(xv, g, dh)
        return dxi + dxn, dxi + dxn, dg

    def transposed(gw):
        return gw.transpose(0, 2, 1).reshape(gw.shape[0] * gw.shape[2], D)

    def back_cols(name, da, gwt, xv, g, dx_in, comm=None):
        n = da.shape[1]
        row = _row_spec(tk, D)
        return _matmul(name, da, gwt, [xv, g.reshape(1, D), dx_in], grid=(T // tk,),
                       a_spec=_row_spec(tk, n), b_spec=_const_spec((n, D)), extra_specs=[row, _const_spec((1, D)), row],
                       out_shapes=[_sds((T, D), F32), _sds((T, D), BF16), _sds((1, D), F32)], out_specs=[row, row, _const_spec((1, D))],
                       dims=NN, epilogue=norm_bwd_epilogue, n_sum=1, comm=comm)

    def token_sum(tt):
        return dict(k_axis=1, nk=T // tt) if T // tt > 1 else dict(k_axis=None)

    def wgrad_cols(name, h, da, nc, comm=None):
        res = _matmul(name, h, da, [], grid=(N_DEV, T // tb),
                       a_spec=pl.BlockSpec((tb, D), lambda j, t: (t, 0)), b_spec=pl.BlockSpec((tb, nc), lambda j, t: (t, j)),
                       extra_specs=[], out_shapes=[_sds((N_DEV, D, nc), BF16)],
                       out_specs=[pl.BlockSpec((None, D, nc), lambda j, t: (j, 0, 0))],
                       dims=TN, acc_shape=(D, nc), comm=comm, **token_sum(tb))
        return res[0] if comm is None else res

    def wgrad_rows(name, a, dy, kc, ncols, tt, comm=None):
        res = _matmul(name, a, dy, [], grid=(a.shape[1] // kc, T // tt),
                      a_spec=pl.BlockSpec((tt, kc), lambda j, t: (t, j)), b_spec=pl.BlockSpec((tt, ncols), lambda j, t: (t, 0)),
                      extra_specs=[], out_shapes=[_sds((a.shape[1], ncols), BF16)],
                      out_specs=[pl.BlockSpec((kc, ncols), lambda j, t: (j, 0))],
                      dims=TN, acc_shape=(kc, ncols), comm=comm, **token_sum(tt))
        return res[0] if comm is None else res

    saved = []
    xs = x2
    for i in range(depth):
        l = i // 2
        if i == 0:
            h, *first_w = rmsnorm(xs, norm_mix[0], _gather_level2(first_half))
            mla_w[0] = mla_layouts(*first_w)
        if i % 2 == 0:
            w_dkv, w_uq, w_ukv, w_o = mla_w[l]
            def latent_post(la, qn, kvn, cc, sa, sb):
                cq = _rms_fwd(la[:, :Q_RANK], qn)
                ckv = _rms_fwd(la[:, Q_RANK:Q_RANK + KV_RANK], kvn)
                kr = _rope_fwd(la[:, Q_RANK + KV_RANK:], cc, sa, sb)
                return la, cq, ckv, kr

            lat, cq, ckv, kr = _matmul(
                "mla_down", h, w_dkv, [mla_q_norm[l].reshape(1, Q_RANK), mla_kv_norm[l].reshape(1, KV_RANK), t_cc, t_sa, t_sb],
                grid=(nt,), a_spec=_row_spec(tm, D), b_spec=pl.BlockSpec((None, D, LAT_PAD), lambda i_: (0, 0, 0)),
                extra_specs=[_const_spec((1, Q_RANK)), _const_spec((1, KV_RANK))] + tab_specs,
                out_shapes=[_sds((T, LAT_PAD), F32), _sds((T, Q_RANK), BF16), _sds((T, KV_RANK), BF16), _sds((T, 128), BF16)],
                out_specs=[_row_spec(tm, LAT_PAD), _row_spec(tm, Q_RANK), _row_spec(tm, KV_RANK), _row_spec(tm, 128)], dims=NN,
                epilogue=latent_post)

            def q_epilogue(acc, cc, sa, sb):
                parts = []
                for b in range(HEADS):
                    parts += [acc[:, b * QPAD:b * QPAD + NOPE], _rope_fwd(acc[:, b * QPAD + NOPE:(b + 1) * QPAD], cc, sa, sb)]
                return (jnp.concatenate(parts, axis=1),)

            q = _matmul("mla_q", cq, w_uq, [t_cc, t_sa, t_sb], grid=(nt,), a_spec=_row_spec(tm, Q_RANK),
                        b_spec=pl.BlockSpec((None, Q_RANK, HW), lambda i_: (0, 0, 0)), extra_specs=tab_specs,
                        out_shapes=[_sds((T, HW), BF16)], out_specs=[_row_spec(tm, HW)], dims=NN, epilogue=q_epilogue)[0]

            def kv_write(outs, acc, krb):
                k_ref, v_ref, vt_ref = outs
                for b in range(HEADS):
                    vb = acc[:, b * QPAD + NOPE:(b + 1) * QPAD]
                    k_ref[:, b * QPAD:b * QPAD + NOPE] = acc[:, b * QPAD:b * QPAD + NOPE].astype(BF16)
                    k_ref[:, b * QPAD + NOPE:(b + 1) * QPAD] = krb
                    v_ref[:, b * VDIM:(b + 1) * VDIM] = vb.astype(BF16)
                    vbt = vb.T.astype(BF16)
                    for u in range(tm // tq):
                        vt_ref[b, u] = vbt[:, u * tq:(u + 1) * tq]

            kk, vv, vt = _matmul("mla_kv", ckv, w_ukv, [kr], grid=(nt,), a_spec=_row_spec(tm, KV_RANK),
                                 b_spec=pl.BlockSpec((None, KV_RANK, HW), lambda i_: (0, 0, 0)), extra_specs=[_row_spec(tm, 128)],
                                 out_shapes=[_sds((T, HW), BF16), _sds((T, OW), BF16), _sds((HEADS, T // tq, VDIM, tq), BF16)],
                                 out_specs=[_row_spec(tm, HW), _row_spec(tm, OW), pl.BlockSpec((HEADS, tm // tq, VDIM, tq), lambda i_: (0, i_, 0, 0))],
                                 dims=NN, write=kv_write)
            group = [up_sh[i], down_sh[i], in_sh[l], out_sh[l]] + (small_later if i == 0 else [])
            o, lse, *bufs = _flash_fwd(q, kk, vt, tq, comm=_gather_level1(group))
            xm, h2, g_up[i], g_down[i] = _matmul(
                "mla_out", o, w_o, [xs, norm_ffn[i].reshape(1, D)], grid=(nt,), a_spec=_row_spec(tm, OW),
                b_spec=pl.BlockSpec((None, OW, D), lambda i_: (0, 0, 0)), extra_specs=[_row_spec(tm, D), _const_spec((1, D))],
                out_shapes=[_sds((T, D), F32), _sds((T, D), BF16)], out_specs=[_row_spec(tm, D)] * 2, dims=NN,
                epilogue=residual_norm, comm=_gather_level2(bufs[:2]))
            half_gathered = bufs[2:]
            mix_saved = (h, lat, cq, ckv, q, kk, vv, o, lse)
        else:
            gp, ge, g_down[i], up_half = proj_cols("sgu_in", h, g_in[l], e2c, _gelu_and_grad, 2,
                                                   comm=_merge_comm(_gather_level2([down_half]), _gather_level1([up_sh[i]])))
            gate = _sgu_mid_fwd(ge, ln_g_full[l], ln_b_full[l], sgu_w_spatial[l], b_sp[l], 4)
            xm, h2, g_up[i] = proj_rows_residual("sgu_out", gate, g_out[l], xs, norm_ffn[i], comm=_gather_level2([up_half]))
            mix_saved = (h, gp, ge, gate)
        r, s, *rest = proj_cols("ffn_up", h2, g_up[i], ffc, lambda acc: (jnp.maximum(acc, 0.0), jnp.square(jnp.maximum(acc, 0.0))), 2,
                                comm=_merge_comm(_gather_level2(half_gathered), _gather_level1([down_sh[i + 1]], part=(0, 2))) if i % 2 == 0 else None)
        if i % 2 == 0:
            g_in[l], g_out[l], *small_gathered, down_part = rest
        if i == 0:
            for l_ in range(1, n_mla):
                mla_w[l_] = mla_layouts(*small_gathered[4 * (l_ - 1):4 * l_])
            g_ln = small_gathered[-1]
            ln_g_full = [g_ln[:, l_, :].reshape(1, E) for l_ in range(n_sgu)]
            ln_b_full = [g_ln[:, n_sgu + l_, :].reshape(1, E) for l_ in range(n_sgu)]
        saved.append((xs, xm, mix_saved, h2, r, s))
        if i + 1 < depth:
            xs, h, *rest = proj_rows_residual("ffn_down", s, g_down[i], xm, norm_mix[i + 1],
                                              comm=_gather_level1([down_sh[i + 1]], part=(1, 2), into=[down_part]) if i % 2 == 0 else None)
            if i % 2 == 0:
                (down_half,) = rest

    def loss_head(acc, xr, tg, g):
        xv = acc + xr
        y = _rms_fwd(xv, g)
        err = y - tg
        part = 0.5 * jnp.sum(jnp.sum(err * err, axis=-1, keepdims=True), axis=0, keepdims=True) / D
        dxv, dg = _rms_bwd(xv, g, err / D)
        return dxv, dxv, jnp.broadcast_to(part, (1, 128)), dg

    dx, dyb, loss_part, d_final = _matmul(
        "ffn_down_loss", s, g_down[depth - 1].reshape(FF, D), [xm, tgt, final_norm.reshape(1, D)], grid=(T // tk,),
        a_spec=_row_spec(tk, FF), b_spec=_const_spec((FF, D)), extra_specs=[_row_spec(tk, D), _row_spec(tk, D), _const_spec((1, D))],
        out_shapes=[_sds((T, D), F32), _sds((T, D), BF16), _sds((1, 128), F32), _sds((1, D), F32)],
        out_specs=[_row_spec(tk, D), _row_spec(tk, D), _const_spec((1, 128)), _const_spec((1, D))], dims=NN, epilogue=loss_head, n_sum=2)
    loss = lax.psum(loss_part[0, 0], ("x", "y", "c"))

    d_norm_mix, d_norm_ffn = [None] * depth, [None] * depth
    d_qn, d_kvn = [None] * n_mla, [None] * n_mla
    d_wsp, d_bsp, d_lng, d_lnb = [None] * n_sgu, [None] * n_sgu, [None] * n_sgu, [None] * n_sgu
    layers = {"dkv": n_mla, "uq": n_mla, "ukv": n_mla, "o": n_mla, "in": n_sgu, "out": n_sgu, "up": depth, "down": depth}
    stacked = {nm: None for nm in layers}
    pending = []
    summed = []

    def add_pairs(gs, rcvs):
        operands, in_specs, out_shapes, out_specs = [], [], [], []
        for g, rcv in zip(gs, rcvs):
            _, rws, cls = g.shape
            slab = pl.BlockSpec((None, rws, cls), lambda ch, cr: (ch, 0, 0))
            operands += [g.reshape(N_CHIP, 2, rws, cls), rcv]
            in_specs += [pl.BlockSpec((None, None, rws, cls), lambda ch, cr: (ch, cr[0], 0, 0)), slab]
            out_shapes.append(_sds(rcv.shape, BF16))
            out_specs.append(slab)

        def fn(*blocks):
            return tuple(blocks[2 * k].astype(F32) + blocks[2 * k + 1].astype(F32) for k in range(len(gs)))

        return _rowwise("grad_pair_sum", fn, operands, grid=(N_CHIP,), in_specs=in_specs, out_shapes=out_shapes, out_specs=out_specs,
                        grid_spec_prefetch=cidx)

    def sibling_comm():
        return _sibling_exchange([g for _, _, g in pending]) if pending else None

    def absorb(from_sibling):
        if pending:
            parts = add_pairs([g for _, _, g in pending], list(from_sibling))
            summed.extend((nm, l_, p) for (nm, l_, _), p in zip(pending, parts))
            pending.clear()

    def chip_comm():
        if pending:
            absorb(_comm_call("grad_sibling_exchange", sibling_comm()))
        comm, names = _chip_exchange([p for _, _, p in summed], [(nm, l_) for nm, l_, _ in summed], layers, stacked)
        summed.clear()
        return comm, names

    def rows128(a, rows):
        flat = a.reshape(-1, 128)
        return jnp.pad(flat, ((0, rows - flat.shape[0]), (0, 0)))

    def pad_to(n, mult):
        return -(-n // mult) * mult

    def packed(arrs, sizes):
        return jnp.concatenate([rows128(a, sz) for a, sz in zip(arrs, sizes)], axis=0)

    n_wsp, n_bsp, n_ln = sgu_w_spatial.size // 128, pad_to(sgu_b_spatial.size // 128, 8), pad_to(n_sgu * E // 128, 8)
    early_sizes = [n_wsp, pad_to(n_wsp + n_bsp, SMALL_ROWS) - n_wsp, n_ln, n_ln]
    early_rep = early_sizes[0] + early_sizes[1]
    gathered_early = None

    for i in reversed(range(depth)):
        l = i // 2
        xs_i, xm, mix_saved, h2, r, s = saved[i]
        comm = sibling_comm()
        if i == 0:
            comm = _gather_level2([early_half]) if comm is None else _merge_comm(comm, _gather_level2([early_half]))
        da, *rcv = back_rows("ffn_down_bwd", dyb, g_down[i], ffc, [r], lambda acc, rr: (acc * (2.0 * rr.astype(F32)),), comm=comm)
        if i == 0:
            *rcv, gathered_early = rcv
        absorb(rcv)
        pending.append(("down", i, wgrad_rows("ffn_down_wgrad", s, dyb, ffc, D, tb).reshape(N_DEV, ffc, D)))
        pending.append(("up", i, wgrad_cols("ffn_up_wgrad", h2, da, ffc)))
        dx, dyb, d_norm_ffn[i], *rcv = back_cols("ffn_up_bwd", da, transposed(g_up[i]), xm, norm_ffn[i], dx, comm=sibling_comm())
        absorb(rcv)
        if i % 2 == 0:
            h, lat, cq, ckv, q, kk, vv, o, lse = mix_saved
            w_dkv, w_uq, w_ukv, w_o = mla_w[l]
            do = _matmul("mla_out_bwd", dyb, w_o, [], grid=(nt,), a_spec=_row_spec(tm, D),
                         b_spec=pl.BlockSpec((None, OW, D), lambda i_: (0, 0, 0)), extra_specs=[],
                         out_shapes=[_sds((T, OW), BF16)], out_specs=[_row_spec(tm, OW)], dims=NT)[0]
            g_o_l = wgrad_rows("mla_out_wgrad", o, dyb, OW, D, tm).reshape(N_DEV, owc, D)
            comm, names = chip_comm()
            dq_pre, dk, dv, *bufs = _flash_bwd(q, kk, vv, o, do, lse, (t_cc, t_sa, t_sb), tq, comm=comm)
            stacked.update(dict(zip(names, bufs)))
            pending.append(("o", l, g_o_l))

            def kv_pre(dkb, dvb, cc, sa, sb):
                parts, dkr = [], None
                for b in range(HEADS):
                    parts += [dkb[:, b * QPAD:b * QPAD + NOPE], dvb[:, b * VDIM:(b + 1) * VDIM]]
                    piece = dkb[:, b * QPAD + NOPE:(b + 1) * QPAD].astype(F32)
                    dkr = piece if dkr is None else dkr + piece
                return jnp.concatenate(parts, axis=1), _rope_bwd(dkr, cc, sa, sb)

            dkv, dkr = _rowwise("mla_dkv_rope", kv_pre, [dk, dv, t_cc, t_sa, t_sb], grid=(T // ts,),
                                in_specs=[_row_spec(ts, HW), _row_spec(ts, OW)] + [_row_spec(ts, 128)] * 3,
                                out_shapes=[_sds((T, HW), BF16), _sds((T, 128), F32)], out_specs=[_row_spec(ts, HW), _row_spec(ts, 128)])
            g_uq_l = wgrad_rows("mla_q_wgrad", cq, dq_pre, Q_RANK, HW, tm)
            g_ukv_l = wgrad_rows("mla_kv_wgrad", ckv, dkv, KV_RANK, HW, tm)
            pending.append(("uq", l, g_uq_l.reshape(Q_RANK, HEADS, QPAD)[:, :, :NOPE + ROPE].transpose(1, 0, 2)))
            pending.append(("ukv", l, g_ukv_l.reshape(KV_RANK, HEADS, NOPE + VDIM).transpose(1, 0, 2)))
            dcq = _matmul("mla_q_bwd", dq_pre, w_uq, [], grid=(nt,), a_spec=_row_spec(tm, HW),
                          b_spec=pl.BlockSpec((None, Q_RANK, HW), lambda i_: (0, 0, 0)), extra_specs=[],
                          out_shapes=[_sds((T, Q_RANK), F32)], out_specs=[_row_spec(tm, Q_RANK)], dims=NT)[0]
            def latent_bwd(dkv_, la, qn, kvn, dq_, dkr_):
                dcq_raw, dqn = _rms_bwd(la[:, :Q_RANK], qn, dq_)
                dckv_raw, dkvn = _rms_bwd(la[:, Q_RANK:Q_RANK + KV_RANK], kvn, dkv_)
                return jnp.concatenate([dcq_raw, dckv_raw, dkr_], axis=1), dqn, dkvn

            dlat, d_qn[l], d_kvn[l] = _matmul(
                "mla_kv_bwd", dkv, w_ukv, [lat, mla_q_norm[l].reshape(1, Q_RANK), mla_kv_norm[l].reshape(1, KV_RANK), dcq, dkr],
                grid=(nt,), a_spec=_row_spec(tm, HW), b_spec=pl.BlockSpec((None, KV_RANK, HW), lambda i_: (0, 0, 0)),
                extra_specs=[_row_spec(tm, LAT_PAD), _const_spec((1, Q_RANK)), _const_spec((1, KV_RANK)), _row_spec(tm, Q_RANK), _row_spec(tm, 128)],
                out_shapes=[_sds((T, LAT_PAD), BF16), _sds((1, Q_RANK), F32), _sds((1, KV_RANK), F32)],
                out_specs=[_row_spec(tm, LAT_PAD), _const_spec((1, Q_RANK)), _const_spec((1, KV_RANK))], dims=NT,
                epilogue=latent_bwd, n_sum=2)
            g_dkv_l = wgrad_rows("mla_down_wgrad", h, dlat, D, LAT_PAD, tm)
            pending.append(("dkv", l, g_dkv_l[:, :LAT].reshape(N_DEV, dc, LAT)))
            dx, dyb, d_norm_mix[i] = _matmul(
                "mla_down_bwd", dlat, w_dkv, [xs_i, norm_mix[i].reshape(1, D), dx], grid=(nt,), a_spec=_row_spec(tm, LAT_PAD),
                b_spec=pl.BlockSpec((None, D, LAT_PAD), lambda i_: (0, 0, 0)), extra_specs=[_row_spec(tm, D), _const_spec((1, D)), _row_spec(tm, D)],
                out_shapes=[_sds((T, D), F32), _sds((T, D), BF16), _sds((1, D), F32)],
                out_specs=[_row_spec(tm, D), _row_spec(tm, D), _const_spec((1, D))], dims=NT, epilogue=norm_bwd_epilogue, n_sum=1)
        else:
            h, gp, ge, gate = mix_saved
            (dgate,) = back_rows("sgu_out_bwd", dyb, g_out[l], ec, [], None)
            pending.append(("out", l, wgrad_rows("sgu_out_wgrad", gate, dyb, ec, D, tb).reshape(N_DEV, ec, D)))
            dz, d_wsp[l], d_bsp[l], d_lng[l], d_lnb[l] = _sgu_mid_bwd(ge, gp, dgate, ln_g_full[l], ln_b_full[l], sgu_w_spatial[l], b_sp[l], 2)
            if i == 1:
                early = packed([jnp.stack(d_wsp, 0), jnp.stack(d_bsp, 0), jnp.concatenate(d_lng, 0), jnp.concatenate(d_lnb, 0)], early_sizes)
                g_in_l, early_part = wgrad_cols("sgu_in_wgrad", h, dz, e2c, comm=_gather_level1([early], part=(0, 2)))
            else:
                g_in_l = wgrad_cols("sgu_in_wgrad", h, dz, e2c)
            pending.append(("in", l, g_in_l))
            comm = sibling_comm()
            if i == 1:
                comm = _merge_comm(comm, _gather_level1([early], part=(1, 2), into=[early_part]))
            dx, dyb, d_norm_mix[i], *rcv = back_cols("sgu_in_bwd", dz, transposed(g_in[l]), xs_i, norm_mix[i], dx, comm=comm)
            if i == 1:
                *rcv, early_half = rcv
            absorb(rcv)
    grad_x = dx.reshape(1, T, D)

    last_comm, last_names = chip_comm()
    late_g = [jnp.concatenate(d_norm_mix, 0), jnp.concatenate(d_norm_ffn, 0), d_final, jnp.concatenate(d_qn, 0), jnp.concatenate(d_kvn, 0)]
    late_w = [norm_mix, norm_ffn, final_norm, mla_q_norm, mla_kv_norm]
    late_m = [m_norm_mix, m_norm_ffn, m_final_norm, m_mla_q_norm, m_mla_kv_norm]
    late_v = [v_norm_mix, v_norm_ffn, v_final_norm, v_mla_q_norm, v_mla_kv_norm]
    late_sizes = [pad_to(g.size // 128, 8) for g in late_g]
    late_rows = sum(late_sizes)

    def adam_big(parts, w, m, v):
        lyr, rws, cls = w.shape
        rt = _tile(rws, 512)

        def fn(p, w_, m_, v_):
            g = (p[0].astype(F32) + p[1].astype(F32)) + (p[2].astype(F32) + p[3].astype(F32))
            return (g, *_adam(w_, g, m_, v_))

        spec = pl.BlockSpec((None, rt, cls), lambda l_, i_: (l_, i_, 0))
        return _rowwise("adam_large", fn, [parts, w, m, v], grid=(lyr, rws // rt),
                        in_specs=[pl.BlockSpec((N_CHIP, None, rt, cls), lambda l_, i_: (0, l_, i_, 0)), spec, spec, spec],
                        out_shapes=[_sds(w.shape, F32)] * 4, out_specs=[spec] * 4)

    stacked.update(dict(zip(last_names, _comm_call("grad_chip_exchange", last_comm))))
    (gathered_late,) = _all_gather("gather_small_grads", [packed(late_g, late_sizes)])
    big = {}
    big["in"] = adam_big(stacked["in"], sgu_w_in, m_sgu_w_in, v_sgu_w_in)
    big["up"] = adam_big(stacked["up"], ffn_w_up, m_ffn_w_up, v_ffn_w_up)
    big["down"] = adam_big(stacked["down"], ffn_w_down, m_ffn_w_down, v_ffn_w_down)
    big["out"] = adam_big(stacked["out"], sgu_w_out, m_sgu_w_out, v_sgu_w_out)
    big["dkv"] = adam_big(stacked["dkv"], mla_w_dkv, m_mla_w_dkv, v_mla_w_dkv)
    big["uq"] = adam_big(stacked["uq"], mla_w_uq, m_mla_w_uq, v_mla_w_uq)
    big["ukv"] = adam_big(stacked["ukv"], mla_w_ukv, m_mla_w_ukv, v_mla_w_ukv)
    big["o"] = adam_big(stacked["o"], mla_w_o, m_mla_w_o, v_mla_w_o)
    big_res = [big[nm][:4] for nm in ("dkv", "uq", "ukv", "o", "in", "out", "up", "down")]

    def sum8(p):
        return ((p[0] + p[1]) + (p[2] + p[3])) + ((p[4] + p[5]) + (p[6] + p[7]))

    def adam_packed(name, gathered, ws, ms, vs, sizes, rows, tile):
        spec = _row_spec(tile, 128)
        return _rowwise(name, lambda p, w_, m_, v_: (sum8(p), *_adam(w_, sum8(p), m_, v_)),
                        [gathered, packed(ws, sizes), packed(ms, sizes), packed(vs, sizes)], grid=(rows // tile,),
                        in_specs=[pl.BlockSpec((N_DEV, tile, 128), lambda i_: (0, i_, 0)), spec, spec, spec],
                        out_shapes=[_sds((rows, 128), F32)] * 4, out_specs=[spec] * 4)

    late_res = adam_packed("adam_small", gathered_late, late_w, late_m, late_v, late_sizes, late_rows, late_rows)
    early_res = adam_packed("adam_spatial", gathered_early, [sgu_w_spatial, sgu_b_spatial], [m_sgu_w_spatial, m_sgu_b_spatial],
                            [v_sgu_w_spatial, v_sgu_b_spatial], early_sizes[:2], early_rep, SMALL_ROWS)

    def unpack(res, sizes, k, like):
        off = sum(sizes[:k])
        return res[off:off + like.size // 128].reshape(like.shape)

    my_b = 4 * lax.axis_index("x") + 2 * lax.axis_index("y") + lax.axis_index("c")
    ln_w = jnp.concatenate([sgu_ln_g, sgu_ln_b], 0)
    ln_m = jnp.concatenate([m_sgu_ln_g, m_sgu_ln_b], 0)
    ln_v = jnp.concatenate([v_sgu_ln_g, v_sgu_ln_b], 0)
    ln_all = jnp.concatenate([gathered_early[:, early_rep:early_rep + n_sgu * E // 128], gathered_early[:, early_rep + n_ln:early_rep + n_ln + n_sgu * E // 128]], axis=1)
    ln_mine = lax.dynamic_slice_in_dim(ln_all.reshape(N_DEV, 2 * n_sgu, N_DEV, ec), my_b, 1, axis=2).reshape(N_DEV, 2 * n_sgu, ec)
    ln_g_, ln_d, ln_m2, ln_v2 = _rowwise(
        "adam_ln", lambda p, w_, m_, v_: (sum8(p), *_adam(w_, sum8(p), m_, v_)), [ln_mine, ln_w, ln_m, ln_v], grid=(1,),
        in_specs=[_const_spec(ln_mine.shape), _const_spec(ln_w.shape), _const_spec(ln_w.shape), _const_spec(ln_w.shape)],
        out_shapes=[_sds(ln_w.shape, F32)] * 4, out_specs=[_const_spec(ln_w.shape)] * 4)

    def family(pos):
        ln = [ln_g_, ln_d, ln_m2, ln_v2][pos]
        late = [unpack(late_res[pos], late_sizes, k, w_) for k, w_ in enumerate(late_w)]
        w_sp_, b_sp_ = unpack(early_res[pos], early_sizes, 0, sgu_w_spatial), unpack(early_res[pos], early_sizes, 1, sgu_b_spatial)
        bigs = [res[pos] for res in big_res]
        return [late[0], late[1], late[2], bigs[0], late[3], late[4], bigs[1], bigs[2], bigs[3],
                bigs[4], ln[:n_sgu], ln[n_sgu:], w_sp_, b_sp_, bigs[5], bigs[6], bigs[7]]

    return (loss, grad_x, *family(0), *family(1), *family(2), *family(3))
```

```python
import math

import jax
import jax.numpy as jnp
from jax import lax
from jax.experimental import pallas as pl
from jax.experimental.pallas import tpu as pltpu

F32 = jnp.float32
BF16 = jnp.bfloat16
MESH = pl.DeviceIdType.MESH

N_DEV = 8
N_CHIP = 4
HEADS = 8
NOPE = 128
ROPE = 64
VDIM = 128
QPAD = 256
Q_RANK = 256
KV_RANK = 128
LAT = Q_RANK + KV_RANK + ROPE
LAT_PAD = 512
ROPE_THETA = 10000.0
SGU_CHUNK = 128
SGU_GROUPS = 8
NORM_EPS = 1e-6
LN_EPS = 1e-5
ADAM_LR = 0.001
ADAM_B1 = 0.9
ADAM_B2 = 0.999
ADAM_EPS = 1e-08
ADAM_WD = 0.01
ADAM_STEP = 10
ATTN_SCALE = (NOPE + ROPE) ** -0.5
NEG = -1e30
EXP2_SCALE = ATTN_SCALE * math.log2(math.e)
VMEM_LIMIT = 56 * 1024 * 1024
SMALL_ROWS = 256

NN = (((1,), (0,)), ((), ()))
NT = (((1,), (1,)), ((), ()))
TN = (((0,), (0,)), ((), ()))
ANY = pl.BlockSpec(memory_space=pl.ANY)


def _pcall(body, **kw):
    return pl.pallas_call(body, **kw)


def _params(n_grid, side_effects=False):
    return pltpu.CompilerParams(dimension_semantics=("arbitrary",) * n_grid, vmem_limit_bytes=VMEM_LIMIT, has_side_effects=side_effects)


def _sds(shape, dtype):
    return jax.ShapeDtypeStruct(tuple(shape), dtype)


def _tile(n, want):
    t = min(n, want)
    assert n % t == 0, (n, want)
    return t


class _Comm:
    def __init__(self, operands, out_shapes, aliases, scratch, start, finish):
        self.operands, self.out_shapes, self.aliases, self.scratch = operands, out_shapes, aliases, scratch
        self.start, self.finish = start, finish


def _merge_comm(first, second):
    n_in, n_out, n_sc = len(first.operands), len(first.out_shapes), len(first.scratch)
    aliases = dict(first.aliases)
    aliases.update({n_in + k: n_out + v for k, v in second.aliases.items()})

    def start(ins, outs, sems):
        first.start(ins[:n_in], outs[:n_out], sems[:n_sc])
        second.start(ins[n_in:], outs[n_out:], sems[n_sc:])

    def finish(ins, outs, sems):
        first.finish(ins[:n_in], outs[:n_out], sems[:n_sc])
        second.finish(ins[n_in:], outs[n_out:], sems[n_sc:])

    return _Comm([*first.operands, *second.operands], [*first.out_shapes, *second.out_shapes], aliases,
                 [*first.scratch, *second.scratch], start, finish)


def _place():
    return lax.axis_index("x"), lax.axis_index("y"), lax.axis_index("c")


def _other_chips(x, y):
    return [(1 - x, y), (x, 1 - y), (1 - x, 1 - y)]


def _dev_index(dev):
    return 4 * dev[0] + 2 * dev[1] + dev[2]


def _comm_call(name, comm):
    c_in, c_out = len(comm.operands), len(comm.out_shapes)

    def body(*refs):
        ins, outs, sems = refs[:c_in], refs[c_in:c_in + c_out], refs[c_in + c_out:]
        comm.start(ins, outs, sems)
        comm.finish(ins, outs, sems)

    return _pcall(body, name=name, in_specs=[ANY] * c_in, out_specs=[ANY] * c_out, out_shape=comm.out_shapes,
                  scratch_shapes=comm.scratch, input_output_aliases=dict(comm.aliases),
                  compiler_params=pltpu.CompilerParams(has_side_effects=True))(*comm.operands)


def _call(name, body, operands, in_specs, out_shapes, out_specs, scratch, grid, comm=None):
    if comm is None:
        return _pcall(body, name=name, grid=grid, in_specs=in_specs, out_specs=out_specs, out_shape=out_shapes,
                      scratch_shapes=scratch, compiler_params=_params(len(grid)))(*operands)
    n_in, n_out, n_sc = len(operands), len(out_shapes), len(scratch)
    c_in, c_out = len(comm.operands), len(comm.out_shapes)

    def hosted(*refs):
        ins, cins = refs[:n_in], refs[n_in:n_in + c_in]
        o0 = n_in + c_in
        outs, couts = refs[o0:o0 + n_out], refs[o0 + n_out:o0 + n_out + c_out]
        rest = refs[o0 + n_out + c_out:]
        sc, csems = rest[:n_sc], rest[n_sc:]
        first = pl.program_id(0) == 0
        last = pl.program_id(0) == grid[0] - 1
        for d in range(1, len(grid)):
            first = jnp.logical_and(first, pl.program_id(d) == 0)
            last = jnp.logical_and(last, pl.program_id(d) == grid[d] - 1)

        @pl.when(first)
        def _():
            comm.start(cins, couts, csems)

        body(*ins, *outs, *sc)

        @pl.when(last)
        def _():
            comm.finish(cins, couts, csems)

    return _pcall(hosted, name=name, grid=grid, in_specs=[*in_specs, *[ANY] * c_in], out_specs=[*out_specs, *[ANY] * c_out],
                  out_shape=[*out_shapes, *comm.out_shapes], scratch_shapes=[*scratch, *comm.scratch],
                  input_output_aliases={n_in + k: n_out + v for k, v in comm.aliases.items()},
                  compiler_params=_params(len(grid), side_effects=True))(*operands, *comm.operands)


def _gather_level1(shards, part=(0, 1), into=None):
    n = len(shards)
    k_part, m_part = part

    def copies(ins, outs, sems):
        send_sems, recv_sems, local_sems = sems
        x, y, c = _place()
        me, sibling = (x, y, c), (x, y, 1 - c)
        chips = _other_chips(x, y)

        def rows(a):
            r = shards[a].shape[0] // m_part
            return pl.ds(k_part * r, r)

        def copy(a, k, block, to, own=False):
            slot = outs[a].at[_dev_index(block), rows(a)]
            return pltpu.make_async_remote_copy(src_ref=ins[a].at[rows(a)] if own else slot, dst_ref=slot, send_sem=send_sems.at[a, k],
                                                recv_sem=recv_sems.at[a, k], device_id=to, device_id_type=MESH)

        mine = [pltpu.make_async_copy(ins[a].at[rows(a)], outs[a].at[_dev_index(me), rows(a)], local_sems.at[a]) for a in range(n)]
        sends = [copy(a, 1 + j, me, (*chip, c), own=True) for j, chip in enumerate(chips) for a in range(n)]
        sends += [copy(a, 0, me, sibling, own=True) for a in range(n)]
        recvs = [copy(a, 1 + j, (*chip, c), me) for j, chip in enumerate(chips) for a in range(n)]
        recvs += [copy(a, 0, sibling, me) for a in range(n)]
        return mine, sends, recvs

    def start(ins, outs, sems):
        mine, sends, _ = copies(ins, outs, sems)
        for cp in mine + sends:
            cp.start()

    def finish(ins, outs, sems):
        mine, sends, recvs = copies(ins, outs, sems)
        for cp in recvs:
            cp.wait_recv()
        for cp in sends:
            cp.wait_send()
        for cp in mine:
            cp.wait()

    return _Comm([*shards, *(into or [])], [_sds((N_DEV, *a.shape), a.dtype) for a in shards], {n + a: a for a in range(n)} if into else {},
                 [pltpu.SemaphoreType.DMA((n, 4)), pltpu.SemaphoreType.DMA((n, 4)), pltpu.SemaphoreType.DMA((n,))], start, finish)


def _gather_level2(bufs):
    n = len(bufs)

    def copies(outs, sems):
        send_sems, recv_sems = sems
        x, y, c = _place()
        sibling = (x, y, 1 - c)
        sends, recvs = [], []
        for j, chip in enumerate(_other_chips(x, y)):
            for a in range(n):
                have, want = outs[a].at[_dev_index((*chip, c))], outs[a].at[_dev_index((*chip, 1 - c))]
                sends.append(pltpu.make_async_remote_copy(src_ref=have, dst_ref=have, send_sem=send_sems.at[a, j], recv_sem=recv_sems.at[a, j],
                                                          device_id=sibling, device_id_type=MESH))
                recvs.append(pltpu.make_async_remote_copy(src_ref=want, dst_ref=want, send_sem=send_sems.at[a, j], recv_sem=recv_sems.at[a, j],
                                                          device_id=sibling, device_id_type=MESH))
        return sends, recvs

    def start(ins, outs, sems):
        for cp in copies(outs, sems)[0]:
            cp.start()

    def finish(ins, outs, sems):
        sends, recvs = copies(outs, sems)
        for cp in recvs:
            cp.wait_recv()
        for cp in sends:
            cp.wait_send()

    return _Comm(bufs, [_sds(b.shape, b.dtype) for b in bufs], {a: a for a in range(n)},
                 [pltpu.SemaphoreType.DMA((n, 3)), pltpu.SemaphoreType.DMA((n, 3))], start, finish)


def _all_gather(name, arrays):
    n = len(arrays)

    def body(*refs):
        ins = refs[:n]
        outs = refs[n:2 * n]
        send_sems, recv_sems, local_sems = refs[2 * n:]
        x, y, c = _place()
        me, sibling = (x, y, c), (x, y, 1 - c)
        chips = _other_chips(x, y)

        def copy(a, k, block, to, src=None):
            slot = outs[a].at[_dev_index(block)]
            return pltpu.make_async_remote_copy(src_ref=slot if src is None else src, dst_ref=slot, send_sem=send_sems.at[a, k],
                                                recv_sem=recv_sems.at[a, k], device_id=to, device_id_type=MESH)

        mine = [pltpu.make_async_copy(ins[a], outs[a].at[_dev_index(me)], local_sems.at[a]) for a in range(n)]
        for cp in mine:
            cp.start()
        first = []
        for j, chip in enumerate(chips):
            first += [copy(a, 1 + j, me, (*chip, c), src=ins[a]) for a in range(n)]
        first += [copy(a, 0, me, sibling, src=ins[a]) for a in range(n)]
        for cp in first:
            cp.start()
        passed = []
        for j, chip in enumerate(chips):
            for a in range(n):
                copy(a, 1 + j, (*chip, c), me).wait_recv()
                fwd = copy(a, 4 + j, (*chip, c), sibling)
                fwd.start()
                passed.append(fwd)
        for a in range(n):
            copy(a, 0, sibling, me).wait_recv()
            for j, chip in enumerate(chips):
                copy(a, 4 + j, (*chip, 1 - c), me).wait_recv()
        for cp in first + passed:
            cp.wait_send()
        for cp in mine:
            cp.wait()

    return _pcall(
        body, name=name, in_specs=[ANY] * n, out_specs=[ANY] * n,
        out_shape=[_sds((N_DEV, *a.shape), a.dtype) for a in arrays],
        scratch_shapes=[pltpu.SemaphoreType.DMA((n, 7)), pltpu.SemaphoreType.DMA((n, 7)), pltpu.SemaphoreType.DMA((n,))],
        compiler_params=pltpu.CompilerParams(has_side_effects=True),
    )(*arrays)


def _sibling_exchange(grads):
    n = len(grads)

    def start(ins, outs, sems):
        send_sems, recv_sems = sems
        x, y, c = _place()
        for a in range(n):
            for ch in range(N_CHIP):
                pltpu.make_async_remote_copy(src_ref=ins[a].at[2 * ch + 1 - c], dst_ref=outs[a].at[ch], send_sem=send_sems.at[a],
                                             recv_sem=recv_sems.at[a], device_id=(x, y, 1 - c), device_id_type=MESH).start()

    def finish(ins, outs, sems):
        send_sems, recv_sems = sems
        x, y, c = _place()
        for a in range(n):
            pltpu.make_async_remote_copy(src_ref=outs[a], dst_ref=outs[a], send_sem=send_sems.at[a], recv_sem=recv_sems.at[a],
                                         device_id=(x, y, 1 - c), device_id_type=MESH).wait()

    return _Comm(grads, [_sds((N_CHIP, *g.shape[1:]), g.dtype) for g in grads], {},
                 [pltpu.SemaphoreType.DMA((n,)), pltpu.SemaphoreType.DMA((n,))], start, finish)


def _chip_exchange(parts, slots, layers, stacked):
    n = len(parts)
    names = []
    for nm, _ in slots:
        if nm not in names:
            names.append(nm)
    shapes = {nm: _sds((N_CHIP, layers[nm], *parts[a].shape[1:]), parts[a].dtype) for a, (nm, _) in enumerate(slots)}
    kept = [nm for nm in names if stacked.get(nm) is not None]
    aliases = {n + k: names.index(nm) for k, nm in enumerate(kept)}

    def copies(ins, outs, sems):
        send_sems, recv_sems, local_sems = sems
        x, y, c = _place()
        mine = 2 * x + y
        local, sends, recvs = [], [], []
        for a, (nm, l) in enumerate(slots):
            buf = outs[names.index(nm)]
            local.append(pltpu.make_async_copy(ins[a].at[mine], buf.at[mine, l], local_sems.at[a]))
            for j, chip in enumerate(_other_chips(x, y)):
                theirs = buf.at[2 * chip[0] + chip[1], l]
                sends.append(pltpu.make_async_remote_copy(src_ref=ins[a].at[2 * chip[0] + chip[1]], dst_ref=buf.at[mine, l], send_sem=send_sems.at[a, j],
                                                          recv_sem=recv_sems.at[a, j], device_id=(*chip, c), device_id_type=MESH))
                recvs.append(pltpu.make_async_remote_copy(src_ref=theirs, dst_ref=theirs, send_sem=send_sems.at[a, j],
                                                          recv_sem=recv_sems.at[a, j], device_id=(*chip, c), device_id_type=MESH))
        return local, sends, recvs

    def start(ins, outs, sems):
        local, sends, _ = copies(ins, outs, sems)
        for cp in local + sends:
            cp.start()

    def finish(ins, outs, sems):
        local, sends, recvs = copies(ins, outs, sems)
        for cp in recvs:
            cp.wait_recv()
        for cp in sends:
            cp.wait_send()
        for cp in local:
            cp.wait()

    comm = _Comm([*parts, *[stacked[nm] for nm in kept]], [shapes[nm] for nm in names], aliases,
                 [pltpu.SemaphoreType.DMA((n, 3)), pltpu.SemaphoreType.DMA((n, 3)), pltpu.SemaphoreType.DMA((n,))], start, finish)
    return comm, names


def _matmul(name, a, b, extras, *, grid, a_spec, b_spec, extra_specs, out_shapes, out_specs, dims, k_axis=None, nk=1,
            acc_shape=None, epilogue=None, comm=None, n_sum=0, write=None):
    n_extra = len(extras)
    n_out = len(out_shapes)

    def body(*refs):
        a_ref, b_ref = refs[0], refs[1]
        ex = refs[2:2 + n_extra]
        outs = refs[2 + n_extra:2 + n_extra + n_out]
        prod = lax.dot_general(a_ref[...], b_ref[...], dims, preferred_element_type=F32)

        def finish(acc):
            if write is not None:
                write(outs, acc, *[e[...] for e in ex])
                return
            res = epilogue(acc, *[e[...] for e in ex]) if epilogue is not None else (acc,)
            first = None
            for d in range(len(grid)):
                if d != k_axis:
                    here = pl.program_id(d) == 0
                    first = here if first is None else jnp.logical_and(first, here)
            for idx, (o, r) in enumerate(zip(outs, res)):
                if idx < n_out - n_sum:
                    o[...] = r.astype(o.dtype)
                else:
                    @pl.when(first)
                    def _(o=o, r=r):
                        o[...] = r.astype(o.dtype)

                    @pl.when(jnp.logical_not(first))
                    def _(o=o, r=r):
                        o[...] += r.astype(o.dtype)

        if k_axis is None:
            finish(prod)
        else:
            acc_ref = refs[-1]
            k = pl.program_id(k_axis)

            @pl.when(k == 0)
            def _():
                acc_ref[...] = prod

            @pl.when(k > 0)
            def _():
                acc_ref[...] += prod

            @pl.when(k == nk - 1)
            def _():
                finish(acc_ref[...])

    scratch = [] if k_axis is None else [pltpu.VMEM(acc_shape, F32)]
    return _call(name, body, [a, b, *extras], [a_spec, b_spec, *extra_specs], list(out_shapes), list(out_specs), scratch, grid, comm)


def _rowwise(name, fn, operands, *, grid, in_specs, out_shapes, out_specs, n_acc=0, grid_spec_prefetch=None, comm=None):
    n_in = len(operands)
    n_out = len(out_shapes)
    n_pre = 0 if grid_spec_prefetch is None else 1

    def body(*refs):
        refs = refs[n_pre:]
        ins = refs[:n_in]
        outs = refs[n_in:n_in + n_out]
        res = fn(*[r[...] for r in ins])
        if not isinstance(res, (tuple, list)):
            res = (res,)
        first = pl.program_id(0) == 0
        for d in range(1, len(grid)):
            first = jnp.logical_and(first, pl.program_id(d) == 0)
        for idx, (o, r) in enumerate(zip(outs, res)):
            if idx < n_out - n_acc:
                o[...] = r.astype(o.dtype)
            else:
                @pl.when(first)
                def _(o=o, r=r):
                    o[...] = r.astype(o.dtype)

                @pl.when(jnp.logical_not(first))
                def _(o=o, r=r):
                    o[...] += r.astype(o.dtype)

    if comm is not None:
        return _call(name, body, list(operands), list(in_specs), list(out_shapes), list(out_specs), [], grid, comm)
    if grid_spec_prefetch is None:
        return _pcall(body, name=name, grid=grid, in_specs=in_specs, out_specs=out_specs, out_shape=out_shapes,
                      compiler_params=_params(len(grid)))(*operands)
    gs = pltpu.PrefetchScalarGridSpec(num_scalar_prefetch=1, grid=grid, in_specs=in_specs, out_specs=out_specs)
    return _pcall(body, name=name, grid_spec=gs, out_shape=out_shapes,
                  compiler_params=_params(len(grid)))(grid_spec_prefetch, *operands)


def _row_spec(tm, w):
    return pl.BlockSpec((tm, w), lambda i: (i, 0))


def _const_spec(shape):
    nd = len(shape)
    return pl.BlockSpec(tuple(shape), lambda *_: (0,) * nd)


def _rms_fwd(x, g):
    r = lax.rsqrt(jnp.mean(x * x, axis=-1, keepdims=True) + NORM_EPS)
    return x * r * g


def _rms_bwd(x, g, dy):
    r = lax.rsqrt(jnp.mean(x * x, axis=-1, keepdims=True) + NORM_EPS)
    xh = x * r
    u = dy * g
    dx = r * (u - xh * jnp.mean(u * xh, axis=-1, keepdims=True))
    dg = jnp.sum(dy * xh, axis=0, keepdims=True)
    return dx, dg


def _gelu_and_grad(z):
    cdf = 0.5 * (1.0 + lax.erf(z * (2.0 ** -0.5)))
    return cdf + z * jnp.exp(-0.5 * z * z) * ((2.0 * math.pi) ** -0.5), z * cdf


def _rope_fwd(x, cc, sa, sb):
    return x * cc + pltpu.roll(x, 96, 1) * sa + pltpu.roll(x, 32, 1) * sb


def _rope_bwd(d, cc, sa, sb):
    return d * cc + pltpu.roll(d * sa, 32, 1) + pltpu.roll(d * sb, 96, 1)


def _adam(w, g, m, v):
    m = ADAM_B1 * m + (1.0 - ADAM_B1) * g
    v = ADAM_B2 * v + (1.0 - ADAM_B2) * (g * g)
    m_hat = m / (1.0 - ADAM_B1 ** ADAM_STEP)
    v_hat = v / (1.0 - ADAM_B2 ** ADAM_STEP)
    delta = -ADAM_LR * (m_hat / (jnp.sqrt(v_hat) + ADAM_EPS) + ADAM_WD * w)
    return delta, m, v


def _flash_fwd(q, k, vt, tq, comm=None):
    h, t = vt.shape[0], q.shape[0]
    nq = t // tq

    chunk_blocks = [c for c in (4, 2) if c < nq]

    def body(q_ref, k_ref, vt_ref, o_ref, lse_ref, m_ref, l_ref, acc_ref):
        qi = pl.program_id(1)
        m_ref[...] = jnp.full((1, tq), NEG, F32)
        l_ref[...] = jnp.zeros((1, tq), F32)
        acc_ref[...] = jnp.zeros((VDIM, tq), F32)

        def update(kb0, nblk, masked):
            kb = k_ref[pl.ds(pl.multiple_of(kb0 * tq, tq), nblk * tq), :]
            st = lax.dot_general(kb, q_ref[...], NT, preferred_element_type=F32)
            if masked:
                key = lax.broadcasted_iota(jnp.int32, (nblk * tq, tq), 0) - (nblk - 1) * tq
                qry = lax.broadcasted_iota(jnp.int32, (nblk * tq, tq), 1)
                st = jnp.where(key <= qry, st, NEG)
            m_old = m_ref[...]
            m_new = jnp.maximum(m_old, jnp.max(st, axis=0, keepdims=True))
            alpha = jnp.exp2((m_old - m_new) * EXP2_SCALE)
            pt = jnp.exp2((st - m_new) * EXP2_SCALE)
            l_ref[...] = alpha * l_ref[...] + jnp.sum(pt, axis=0, keepdims=True)
            ptb = pt.astype(BF16)
            pv = lax.dot_general(vt_ref[kb0], ptb[:tq], NN, preferred_element_type=F32)
            for j in range(1, nblk):
                pv += lax.dot_general(vt_ref[kb0 + j], ptb[j * tq:(j + 1) * tq], NN, preferred_element_type=F32)
            acc_ref[...] = alpha * acc_ref[...] + pv
            m_ref[...] = m_new

        start = jnp.int32(0)
        for c in chunk_blocks:
            take = (qi & c) != 0

            @pl.when(take)
            def _(start=start, c=c):
                update(start, c, False)

            start = start + jnp.where(take, c, 0)
        if nq > 1:
            @pl.when((qi & 1) != 0)
            def _():
                update(qi - 1, 2, True)

            @pl.when((qi & 1) == 0)
            def _():
                update(qi, 1, True)
        else:
            update(qi, 1, True)
        l = l_ref[...]
        o_ref[...] = (acc_ref[...] / l).T.astype(o_ref.dtype)
        lse_ref[...] = m_ref[...] * EXP2_SCALE + jnp.log2(l)

    return _call(
        "flash_fwd", body, [q, k, vt],
        [pl.BlockSpec((tq, QPAD), lambda hh, i: (i, hh)),
         pl.BlockSpec((t, QPAD), lambda hh, i: (0, hh)),
         pl.BlockSpec((None, nq, VDIM, tq), lambda hh, i: (hh, 0, 0, 0))],
        [_sds((t, h * VDIM), BF16), _sds((h, nq, 1, tq), F32)],
        [pl.BlockSpec((tq, VDIM), lambda hh, i: (i, hh)),
         pl.BlockSpec((None, None, 1, tq), lambda hh, i: (hh, i, 0, 0))],
        [pltpu.VMEM((1, tq), F32), pltpu.VMEM((1, tq), F32), pltpu.VMEM((VDIM, tq), F32)], (h, nq), comm)


def _flash_bwd(q, k, v, o, do, lse, tabs, tq, comm=None):
    t = q.shape[0]
    h = q.shape[1] // QPAD
    nq = t // tq

    def body(q_ref, k_ref, v_ref, o_ref, do_ref, lse_ref, cc_ref, sa_ref, sb_ref, dq_ref, dk_out, dv_out, delta_ref, dqt_ref, dk_ref, dv_ref):
        kj = pl.program_id(1)

        @pl.when(kj == 0)
        def _():
            dqt_ref[...] = jnp.zeros_like(dqt_ref)
            ones = jnp.ones((8, VDIM), BF16)
            for qi in range(nq):
                rows = pl.ds(qi * tq, tq)
                prod = do_ref[rows, :].astype(F32) * o_ref[rows, :].astype(F32)
                hi = prod.astype(BF16)
                lo = (prod - hi.astype(F32)).astype(BF16)
                delta_ref[qi] = (lax.dot_general(ones, hi, NT, preferred_element_type=F32)
                                 + lax.dot_general(ones, lo, NT, preferred_element_type=F32))

        kb = k_ref[...]
        vb = v_ref[...]
        kbt = kb.astype(F32).T.astype(BF16)
        dk_ref[...] = jnp.zeros_like(dk_ref)
        dv_ref[...] = jnp.zeros_like(dv_ref)

        def step(q0, nblk, masked):
            rows = pl.ds(pl.multiple_of(q0 * tq, tq), nblk * tq)
            qb = q_ref[rows, :]
            dob = do_ref[rows, :]
            lse = jnp.concatenate([lse_ref[q0 + j] for j in range(nblk)], axis=1)
            delta = jnp.concatenate([delta_ref[q0 + j, pl.ds(0, 1), :] for j in range(nblk)], axis=1)
            st = lax.dot_general(kb, qb, NT, preferred_element_type=F32)
            pt = jnp.exp2(st * EXP2_SCALE - lse)
            if masked:
                key = lax.broadcasted_iota(jnp.int32, (tq, nblk * tq), 0)
                qry = lax.broadcasted_iota(jnp.int32, (tq, nblk * tq), 1)
                pt = jnp.where(key <= qry, pt, 0.0)
            dv_ref[...] += lax.dot_general(pt.astype(BF16), dob, NN, preferred_element_type=F32)
            dpt = lax.dot_general(vb, dob, NT, preferred_element_type=F32)
            dst = (pt * (dpt - delta) * ATTN_SCALE).astype(BF16)
            dk_ref[...] += lax.dot_general(dst, qb, NN, preferred_element_type=F32)
            dqt = lax.dot_general(kbt, dst, NN, preferred_element_type=F32)
            for j in range(nblk):
                dqt_ref[q0 + j] += dqt[:, j * tq:(j + 1) * tq]

        later = nq - 1 - kj
        if nq > 1:
            @pl.when((later & 1) != 0)
            def _():
                step(kj, 2, True)

            @pl.when((later & 1) == 0)
            def _():
                step(kj, 1, True)
        else:
            step(kj, 1, True)
        start = kj + 1 + (later & 1)
        for c in [c for c in (2, 4) if c < nq]:
            take = (later & c) != 0

            @pl.when(take)
            def _(start=start, c=c):
                step(start, c, False)

            start = start + jnp.where(take, c, 0)
        dk_out[...] = dk_ref[...].astype(BF16)
        dv_out[...] = dv_ref[...].astype(BF16)

        @pl.when(kj == nq - 1)
        def _():
            for qi in range(nq):
                rows = pl.ds(qi * tq, tq)
                d = dqt_ref[qi].T
                roped = _rope_bwd(d[:, NOPE:], cc_ref[rows, :], sa_ref[rows, :], sb_ref[rows, :])
                dq_ref[rows, :] = jnp.concatenate([d[:, :NOPE], roped], axis=1).astype(BF16)

    head_q = pl.BlockSpec((t, QPAD), lambda hh, j: (0, hh))
    head_v = pl.BlockSpec((t, VDIM), lambda hh, j: (0, hh))
    table = pl.BlockSpec((t, 128), lambda hh, j: (0, 0))
    return _call(
        "flash_bwd", body, [q, k, v, o, do, lse, *tabs],
        [head_q, pl.BlockSpec((tq, QPAD), lambda hh, j: (j, hh)), pl.BlockSpec((tq, VDIM), lambda hh, j: (j, hh)), head_v, head_v,
         pl.BlockSpec((None, nq, 1, tq), lambda hh, j: (hh, 0, 0, 0)), table, table, table],
        [_sds((t, h * QPAD), BF16), _sds((t, h * QPAD), BF16), _sds((t, h * VDIM), BF16)],
        [head_q, pl.BlockSpec((tq, QPAD), lambda hh, j: (j, hh)), pl.BlockSpec((tq, VDIM), lambda hh, j: (j, hh))],
        [pltpu.VMEM((nq, 8, tq), F32), pltpu.VMEM((nq, QPAD, tq), F32), pltpu.VMEM((tq, QPAD), F32), pltpu.VMEM((tq, VDIM), F32)], (h, nq), comm)


def _tril_bf16(w):
    row = lax.broadcasted_iota(jnp.int32, w.shape, 0)
    col = lax.broadcasted_iota(jnp.int32, w.shape, 1)
    return jnp.where(col <= row, w, 0.0).astype(BF16)


def _layer_norm_parts(v0):
    mu = jnp.mean(v0, axis=-1, keepdims=True)
    vc = v0 - mu
    rstd = lax.rsqrt(jnp.mean(vc * vc, axis=-1, keepdims=True) + LN_EPS)
    return vc * rstd, rstd


def _sgu_mid_fwd(ge, ln_g, ln_b, w_sp, b_sp, chunks_per_step):
    t, e2 = ge.shape
    e = e2 // 2
    gd = e // SGU_GROUPS
    rows = SGU_CHUNK * chunks_per_step

    def body(u_ref, v_ref, g_ref, b_ref, w_ref, bs_ref, gate_ref):
        for ck in range(chunks_per_step):
            r = pl.ds(ck * SGU_CHUNK, SGU_CHUNK)
            xh, _ = _layer_norm_parts(v_ref[r, :].astype(F32))
            v1 = (xh * g_ref[...] + b_ref[...]).astype(BF16)
            for g in range(SGU_GROUPS):
                cols = pl.ds(g * gd, gd)
                mixed = lax.dot_general(_tril_bf16(w_ref[g]), v1[:, g * gd:(g + 1) * gd], NN, preferred_element_type=F32) + bs_ref[g]
                gate_ref[r, cols] = (u_ref[r, cols].astype(F32) * mixed).astype(BF16)

    return _pcall(
        body, name="sgu_mid_fwd", grid=(t // rows,),
        in_specs=[pl.BlockSpec((rows, e), lambda i: (i, 0)), pl.BlockSpec((rows, e), lambda i: (i, 1)),
                  _const_spec((1, e)), _const_spec((1, e)), _const_spec(w_sp.shape), _const_spec(b_sp.shape)],
        out_specs=pl.BlockSpec((rows, e), lambda i: (i, 0)),
        out_shape=_sds((t, e), BF16), compiler_params=_params(1),
    )(ge, ge, ln_g, ln_b, w_sp, b_sp)


def _sgu_mid_bwd(ge, gp, dgate, ln_g, ln_b, w_sp, b_sp, chunks_per_step):
    t, e2 = ge.shape
    e = e2 // 2
    gd = e // SGU_GROUPS
    rows = SGU_CHUNK * chunks_per_step

    def body(u_ref, v_ref, zu_ref, zv_ref, dg_ref, g_ref, b_ref, w_ref, bs_ref, dz_ref, dw_ref, dbs_ref, dlg_ref, dlb_ref):
        @pl.when(pl.program_id(0) == 0)
        def _():
            dw_ref[...] = jnp.zeros_like(dw_ref)
            dbs_ref[...] = jnp.zeros_like(dbs_ref)
            dlg_ref[...] = jnp.zeros_like(dlg_ref)
            dlb_ref[...] = jnp.zeros_like(dlb_ref)

        for ck in range(chunks_per_step):
            r = pl.ds(ck * SGU_CHUNK, SGU_CHUNK)
            xh, rstd = _layer_norm_parts(v_ref[r, :].astype(F32))
            v1 = (xh * g_ref[...] + b_ref[...]).astype(BF16)
            dv1_parts = []
            for g in range(SGU_GROUPS):
                cols = pl.ds(g * gd, gd)
                wc = _tril_bf16(w_ref[g])
                v1g = v1[:, g * gd:(g + 1) * gd]
                mixed = lax.dot_general(wc, v1g, NN, preferred_element_type=F32) + bs_ref[g]
                dgate = dg_ref[r, cols].astype(F32)
                dmixed = dgate * u_ref[r, cols].astype(F32)
                du = dgate * mixed
                dz_ref[r, cols] = (du * zu_ref[r, cols].astype(F32)).astype(BF16)
                dbs_ref[g] += jnp.sum(dmixed, axis=1, keepdims=True)
                dmb = dmixed.astype(BF16)
                dwg = lax.dot_general(dmb, v1g, NT, preferred_element_type=F32)
                row = lax.broadcasted_iota(jnp.int32, dwg.shape, 0)
                col = lax.broadcasted_iota(jnp.int32, dwg.shape, 1)
                dw_ref[g] += jnp.where(col <= row, dwg, 0.0)
                dv1_parts.append(lax.dot_general(wc, dmb, TN, preferred_element_type=F32))
            dv1 = jnp.concatenate(dv1_parts, axis=1)
            dlg_ref[...] += jnp.sum(dv1 * xh, axis=0, keepdims=True)
            dlb_ref[...] += jnp.sum(dv1, axis=0, keepdims=True)
            dxh = dv1 * g_ref[...]
            dv0 = rstd * (dxh - jnp.mean(dxh, axis=-1, keepdims=True) - xh * jnp.mean(dxh * xh, axis=-1, keepdims=True))
            dz_ref[r, pl.ds(e, e)] = (dv0 * zv_ref[r, :].astype(F32)).astype(BF16)

    half0 = pl.BlockSpec((rows, e), lambda i: (i, 0))
    half1 = pl.BlockSpec((rows, e), lambda i: (i, 1))
    return _pcall(
        body, name="sgu_mid_bwd", grid=(t // rows,),
        in_specs=[half0, half1, half0, half1, half0, _const_spec((1, e)), _const_spec((1, e)), _const_spec(w_sp.shape), _const_spec(b_sp.shape)],
        out_specs=[pl.BlockSpec((rows, e2), lambda i: (i, 0)), _const_spec(w_sp.shape), _const_spec(b_sp.shape), _const_spec((1, e)), _const_spec((1, e))],
        out_shape=[_sds((t, e2), BF16), _sds(w_sp.shape, F32), _sds(b_sp.shape, F32), _sds((1, e), F32), _sds((1, e), F32)],
        compiler_params=_params(1),
    )(ge, ge, gp, gp, dgate, ln_g, ln_b, w_sp, b_sp)


def kernel(x, positions, norm_mix, norm_ffn, final_norm, mla_w_dkv, mla_q_norm, mla_kv_norm, mla_w_uq, mla_w_ukv, mla_w_o, sgu_w_in, sgu_ln_g, sgu_ln_b, sgu_w_spatial, sgu_b_spatial, sgu_w_out, ffn_w_up, ffn_w_down, loss_target, m_norm_mix, m_norm_ffn, m_final_norm, m_mla_w_dkv, m_mla_q_norm, m_mla_kv_norm, m_mla_w_uq, m_mla_w_ukv, m_mla_w_o, m_sgu_w_in, m_sgu_ln_g, m_sgu_ln_b, m_sgu_w_spatial, m_sgu_b_spatial, m_sgu_w_out, m_ffn_w_up, m_ffn_w_down, v_norm_mix, v_norm_ffn, v_final_norm, v_mla_w_dkv, v_mla_q_norm, v_mla_kv_norm, v_mla_w_uq, v_mla_w_ukv, v_mla_w_o, v_sgu_w_in, v_sgu_ln_g, v_sgu_ln_b, v_sgu_w_spatial, v_sgu_b_spatial, v_sgu_w_out, v_ffn_w_up, v_ffn_w_down):
    _, T, D = x.shape
    depth = norm_mix.shape[0]
    n_mla, n_sgu = mla_w_dkv.shape[0], sgu_w_in.shape[0]
    assert depth % 2 == 0
    FF = ffn_w_up.shape[2] * N_DEV
    E = sgu_w_out.shape[1] * N_DEV
    ffc, ec, e2c = FF // N_DEV, E // N_DEV, 2 * E // N_DEV
    dc = D // N_DEV
    OW = HEADS * VDIM
    HW = HEADS * QPAD
    owc = OW // N_DEV
    tm = _tile(T, 1024)
    tb = _tile(T, 4096)
    tk = _tile(T, 512)
    tq = _tile(T, 512)
    ts = _tile(T, 256)
    nt = T // tm
    x2 = x.reshape(T, D)
    tgt = loss_target.reshape(T, D)
    cidx = lax.axis_index("c").astype(jnp.int32).reshape(1)

    ln_local = jnp.concatenate([sgu_ln_g, sgu_ln_b, jnp.zeros((8 - 2 * n_sgu, ec), F32)], axis=0)
    mla_sh = [[w[l].astype(BF16) for w in (mla_w_dkv, mla_w_uq, mla_w_ukv, mla_w_o)] for l in range(n_mla)]

    def mla_layouts(g_dkv, g_uq, g_ukv, g_o):
        w_dkv = jnp.pad(g_dkv.reshape(1, D, LAT), ((0, 0), (0, 0), (0, LAT_PAD - LAT)))
        w_uq = jnp.pad(g_uq, ((0, 0), (0, 0), (0, QPAD - NOPE - ROPE))).transpose(1, 0, 2).reshape(1, Q_RANK, HEADS * QPAD)
        w_ukv = g_ukv.transpose(1, 0, 2).reshape(1, KV_RANK, HEADS * (NOPE + VDIM))
        return w_dkv, w_uq, w_ukv, g_o.reshape(1, HEADS * VDIM, D)

    mla_w = [None] * n_mla
    small_later = [a for l in range(1, n_mla) for a in mla_sh[l]] + [ln_local]
    ln_g_full, ln_b_full = [None] * n_sgu, [None] * n_sgu
    b_sp = sgu_b_spatial.reshape(n_sgu, SGU_GROUPS, SGU_CHUNK, 1)
    up_sh = [ffn_w_up[i].astype(BF16) for i in range(depth)]
    down_sh = [ffn_w_down[i].astype(BF16) for i in range(depth)]
    in_sh = [sgu_w_in[l].astype(BF16) for l in range(n_sgu)]
    out_sh = [sgu_w_out[l].astype(BF16) for l in range(n_sgu)]
    g_up, g_down, g_in, g_out = [None] * depth, [None] * depth, [None] * n_sgu, [None] * n_sgu

    inv_freq = ROPE_THETA ** (-jnp.arange(0, ROPE, 2, dtype=F32) / ROPE)
    zeros32 = jnp.zeros((ROPE // 2,), F32)
    inv128 = jnp.concatenate([inv_freq, inv_freq, zeros32, zeros32]).reshape(1, 128)
    sel_a = jnp.concatenate([-jnp.ones((32,), F32), zeros32, zeros32, zeros32]).reshape(1, 128)
    sel_b = jnp.concatenate([zeros32, jnp.ones((32,), F32), zeros32, zeros32]).reshape(1, 128)
    sel_c = jnp.concatenate([jnp.ones((64,), F32), zeros32, zeros32]).reshape(1, 128)

    def rope_tables(pos, inv, sa, sb, sc):
        ang = pos.astype(F32) * inv
        cs, sn = jnp.cos(ang), jnp.sin(ang)
        return cs * sc, sn * sa, sn * sb

    t_cc, t_sa, t_sb, *first_half = _rowwise(
        "rope_tables", rope_tables, [positions.reshape(T, 1), inv128, sel_a, sel_b, sel_c], grid=(nt,),
        in_specs=[_row_spec(tm, 1)] + [_const_spec((1, 128))] * 4,
        out_shapes=[_sds((T, 128), F32)] * 3, out_specs=[_row_spec(tm, 128)] * 3, comm=_gather_level1(mla_sh[0]))
    tab_specs = [_row_spec(tm, 128)] * 3

    def rmsnorm(xv, g, comm):
        return _rowwise("rmsnorm", lambda a, gg: _rms_fwd(a, gg), [xv, g.reshape(1, D)], grid=(nt,),
                        in_specs=[_row_spec(tm, D), _const_spec((1, D))], out_shapes=[_sds((T, D), BF16)], out_specs=[_row_spec(tm, D)], comm=comm)

    def proj_cols(name, h, gw, nc, epilogue, n_out, comm=None):
        return _matmul(name, h, gw, [], grid=(N_DEV, T // tb),
                       a_spec=pl.BlockSpec((tb, D), lambda j, i: (i, 0)),
                       b_spec=pl.BlockSpec((None, D, nc), lambda j, i: (j, 0, 0)), extra_specs=[],
                       out_shapes=[_sds((T, nc * N_DEV), BF16)] * n_out, out_specs=[pl.BlockSpec((tb, nc), lambda j, i: (i, j))] * n_out,
                       dims=NN, epilogue=epilogue, comm=comm)

    def residual_norm(acc, xr, g):
        xn = acc + xr
        return xn, _rms_fwd(xn, g)

    def proj_rows_residual(name, a, gw, xres, g_next, comm=None):
        kk_ = a.shape[1]
        return _matmul(name, a, gw.reshape(kk_, D), [xres, g_next.reshape(1, D)], grid=(T // tk,),
                       a_spec=_row_spec(tk, kk_), b_spec=_const_spec((kk_, D)), extra_specs=[_row_spec(tk, D), _const_spec((1, D))],
                       out_shapes=[_sds((T, D), F32), _sds((T, D), BF16)], out_specs=[_row_spec(tk, D)] * 2,
                       dims=NN, epilogue=residual_norm, comm=comm)

    def back_rows(name, dy, gw, kc, extras, epilogue, comm=None):
        return _matmul(name, dy, gw, extras, grid=(N_DEV, T // tb),
                       a_spec=pl.BlockSpec((tb, D), lambda j, i: (i, 0)),
                       b_spec=pl.BlockSpec((None, kc, D), lambda j, i: (j, 0, 0)),
                       extra_specs=[pl.BlockSpec((tb, kc), lambda j, i: (i, j))] * len(extras),
                       out_shapes=[_sds((T, kc * N_DEV), BF16)], out_specs=[pl.BlockSpec((tb, kc), lambda j, i: (i, j))],
                       dims=NT, epilogue=epilogue, comm=comm)

    def norm_bwd_epilogue(dh, xv, g, dxi):
        dxn, dg = _rms_bwd(xv, g, dh)
        return dxi + dxn, dxi + dxn, dg

    def transposed(gw):
        return gw.transpose(0, 2, 1).reshape(gw.shape[0] * gw.shape[2], D)

    def back_cols(name, da, gwt, xv, g, dx_in, comm=None):
        n = da.shape[1]
        row = _row_spec(tk, D)
        return _matmul(name, da, gwt, [xv, g.reshape(1, D), dx_in], grid=(T // tk,),
                       a_spec=_row_spec(tk, n), b_spec=_const_spec((n, D)), extra_specs=[row, _const_spec((1, D)), row],
                       out_shapes=[_sds((T, D), F32), _sds((T, D), BF16), _sds((1, D), F32)], out_specs=[row, row, _const_spec((1, D))],
                       dims=NN, epilogue=norm_bwd_epilogue, n_sum=1, comm=comm)

    def token_sum(tt):
        return dict(k_axis=1, nk=T // tt) if T // tt > 1 else dict(k_axis=None)

    def wgrad_cols(name, h, da, nc, comm=None):
        res = _matmul(name, h, da, [], grid=(N_DEV, T // tb),
                       a_spec=pl.BlockSpec((tb, D), lambda j, t: (t, 0)), b_spec=pl.BlockSpec((tb, nc), lambda j, t: (t, j)),
                       extra_specs=[], out_shapes=[_sds((N_DEV, D, nc), BF16)],
                       out_specs=[pl.BlockSpec((None, D, nc), lambda j, t: (j, 0, 0))],
                       dims=TN, acc_shape=(D, nc), comm=comm, **token_sum(tb))
        return res[0] if comm is None else res

    def wgrad_rows(name, a, dy, kc, ncols, tt, comm=None):
        res = _matmul(name, a, dy, [], grid=(a.shape[1] // kc, T // tt),
                      a_spec=pl.BlockSpec((tt, kc), lambda j, t: (t, j)), b_spec=pl.BlockSpec((tt, ncols), lambda j, t: (t, 0)),
                      extra_specs=[], out_shapes=[_sds((a.shape[1], ncols), BF16)],
                      out_specs=[pl.BlockSpec((kc, ncols), lambda j, t: (j, 0))],
                      dims=TN, acc_shape=(kc, ncols), comm=comm, **token_sum(tt))
        return res[0] if comm is None else res

    saved = []
    xs = x2
    for i in range(depth):
        l = i // 2
        if i == 0:
            h, *first_w = rmsnorm(xs, norm_mix[0], _gather_level2(first_half))
            mla_w[0] = mla_layouts(*first_w)
        if i % 2 == 0:
            w_dkv, w_uq, w_ukv, w_o = mla_w[l]
            def latent_post(la, qn, kvn, cc, sa, sb):
                cq = _rms_fwd(la[:, :Q_RANK], qn)
                ckv = _rms_fwd(la[:, Q_RANK:Q_RANK + KV_RANK], kvn)
                kr = _rope_fwd(la[:, Q_RANK + KV_RANK:], cc, sa, sb)
                return la, cq, ckv, kr

            lat, cq, ckv, kr = _matmul(
                "mla_down", h, w_dkv, [mla_q_norm[l].reshape(1, Q_RANK), mla_kv_norm[l].reshape(1, KV_RANK), t_cc, t_sa, t_sb],
                grid=(nt,), a_spec=_row_spec(tm, D), b_spec=pl.BlockSpec((None, D, LAT_PAD), lambda i_: (0, 0, 0)),
                extra_specs=[_const_spec((1, Q_RANK)), _const_spec((1, KV_RANK))] + tab_specs,
                out_shapes=[_sds((T, LAT_PAD), F32), _sds((T, Q_RANK), BF16), _sds((T, KV_RANK), BF16), _sds((T, 128), BF16)],
                out_specs=[_row_spec(tm, LAT_PAD), _row_spec(tm, Q_RANK), _row_spec(tm, KV_RANK), _row_spec(tm, 128)], dims=NN,
                epilogue=latent_post)

            def q_epilogue(acc, cc, sa, sb):
                parts = []
                for b in range(HEADS):
                    parts += [acc[:, b * QPAD:b * QPAD + NOPE], _rope_fwd(acc[:, b * QPAD + NOPE:(b + 1) * QPAD], cc, sa, sb)]
                return (jnp.concatenate(parts, axis=1),)

            q = _matmul("mla_q", cq, w_uq, [t_cc, t_sa, t_sb], grid=(nt,), a_spec=_row_spec(tm, Q_RANK),
                        b_spec=pl.BlockSpec((None, Q_RANK, HW), lambda i_: (0, 0, 0)), extra_specs=tab_specs,
                        out_shapes=[_sds((T, HW), BF16)], out_specs=[_row_spec(tm, HW)], dims=NN, epilogue=q_epilogue)[0]

            def kv_write(outs, acc, krb):
                k_ref, v_ref, vt_ref = outs
                for b in range(HEADS):
                    vb = acc[:, b * QPAD + NOPE:(b + 1) * QPAD]
                    k_ref[:, b * QPAD:b * QPAD + NOPE] = acc[:, b * QPAD:b * QPAD + NOPE].astype(BF16)
                    k_ref[:, b * QPAD + NOPE:(b + 1) * QPAD] = krb
                    v_ref[:, b * VDIM:(b + 1) * VDIM] = vb.astype(BF16)
                    vbt = vb.T.astype(BF16)
                    for u in range(tm // tq):
                        vt_ref[b, u] = vbt[:, u * tq:(u + 1) * tq]

            kk, vv, vt = _matmul("mla_kv", ckv, w_ukv, [kr], grid=(nt,), a_spec=_row_spec(tm, KV_RANK),
                                 b_spec=pl.BlockSpec((None, KV_RANK, HW), lambda i_: (0, 0, 0)), extra_specs=[_row_spec(tm, 128)],
                                 out_shapes=[_sds((T, HW), BF16), _sds((T, OW), BF16), _sds((HEADS, T // tq, VDIM, tq), BF16)],
                                 out_specs=[_row_spec(tm, HW), _row_spec(tm, OW), pl.BlockSpec((HEADS, tm // tq, VDIM, tq), lambda i_: (0, i_, 0, 0))],
                                 dims=NN, write=kv_write)
            group = [up_sh[i], down_sh[i], in_sh[l], out_sh[l]] + (small_later if i == 0 else [])
            o, lse, *bufs = _flash_fwd(q, kk, vt, tq, comm=_gather_level1(group))
            xm, h2, g_up[i], g_down[i] = _matmul(
                "mla_out", o, w_o, [xs, norm_ffn[i].reshape(1, D)], grid=(nt,), a_spec=_row_spec(tm, OW),
                b_spec=pl.BlockSpec((None, OW, D), lambda i_: (0, 0, 0)), extra_specs=[_row_spec(tm, D), _const_spec((1, D))],
                out_shapes=[_sds((T, D), F32), _sds((T, D), BF16)], out_specs=[_row_spec(tm, D)] * 2, dims=NN,
                epilogue=residual_norm, comm=_gather_level2(bufs[:2]))
            half_gathered = bufs[2:]
            mix_saved = (h, lat, cq, ckv, q, kk, vv, o, lse)
        else:
            gp, ge, g_down[i], up_half = proj_cols("sgu_in", h, g_in[l], e2c, _gelu_and_grad, 2,
                                                   comm=_merge_comm(_gather_level2([down_half]), _gather_level1([up_sh[i]])))
            gate = _sgu_mid_fwd(ge, ln_g_full[l], ln_b_full[l], sgu_w_spatial[l], b_sp[l], 8)
            xm, h2, g_up[i] = proj_rows_residual("sgu_out", gate, g_out[l], xs, norm_ffn[i], comm=_gather_level2([up_half]))
            mix_saved = (h, gp, ge, gate)
        r, s, *rest = proj_cols("ffn_up", h2, g_up[i], ffc, lambda acc: (jnp.maximum(acc, 0.0), jnp.square(jnp.maximum(acc, 0.0))), 2,
                                comm=_merge_comm(_gather_level2(half_gathered), _gather_level1([down_sh[i + 1]], part=(0, 2))) if i % 2 == 0 else None)
        if i % 2 == 0:
            g_in[l], g_out[l], *small_gathered, down_part = rest
        if i == 0:
            for l_ in range(1, n_mla):
                mla_w[l_] = mla_layouts(*small_gathered[4 * (l_ - 1):4 * l_])
            g_ln = small_gathered[-1]
            ln_g_full = [g_ln[:, l_, :].reshape(1, E) for l_ in range(n_sgu)]
            ln_b_full = [g_ln[:, n_sgu + l_, :].reshape(1, E) for l_ in range(n_sgu)]
        saved.append((xs, xm, mix_saved, h2, r, s))
        if i + 1 < depth:
            xs, h, *rest = proj_rows_residual("ffn_down", s, g_down[i], xm, norm_mix[i + 1],
                                              comm=_gather_level1([down_sh[i + 1]], part=(1, 2), into=[down_part]) if i % 2 == 0 else None)
            if i % 2 == 0:
                (down_half,) = rest

    def loss_head(acc, xr, tg, g):
        xv = acc + xr
        y = _rms_fwd(xv, g)
        err = y - tg
        part = 0.5 * jnp.sum(jnp.sum(err * err, axis=-1, keepdims=True), axis=0, keepdims=True) / D
        dxv, dg = _rms_bwd(xv, g, err / D)
        return dxv, dxv, jnp.broadcast_to(part, (1, 128)), dg

    dx, dyb, loss_part, d_final = _matmul(
        "ffn_down_loss", s, g_down[depth - 1].reshape(FF, D), [xm, tgt, final_norm.reshape(1, D)], grid=(T // tk,),
        a_spec=_row_spec(tk, FF), b_spec=_const_spec((FF, D)), extra_specs=[_row_spec(tk, D), _row_spec(tk, D), _const_spec((1, D))],
        out_shapes=[_sds((T, D), F32), _sds((T, D), BF16), _sds((1, 128), F32), _sds((1, D), F32)],
        out_specs=[_row_spec(tk, D), _row_spec(tk, D), _const_spec((1, 128)), _const_spec((1, D))], dims=NN, epilogue=loss_head, n_sum=2)
    loss = lax.psum(loss_part[0, 0], ("x", "y", "c"))

    d_norm_mix, d_norm_ffn = [None] * depth, [None] * depth
    d_qn, d_kvn = [None] * n_mla, [None] * n_mla
    d_wsp, d_bsp, d_lng, d_lnb = [None] * n_sgu, [None] * n_sgu, [None] * n_sgu, [None] * n_sgu
    layers = {"dkv": n_mla, "uq": n_mla, "ukv": n_mla, "o": n_mla, "in": n_sgu, "out": n_sgu, "up": depth, "down": depth}
    stacked = {nm: None for nm in layers}
    pending = []
    summed = []

    def add_pairs(gs, rcvs):
        operands, in_specs, out_shapes, out_specs = [], [], [], []
        for g, rcv in zip(gs, rcvs):
            _, rws, cls = g.shape
            slab = pl.BlockSpec((None, rws, cls), lambda ch, cr: (ch, 0, 0))
            operands += [g.reshape(N_CHIP, 2, rws, cls), rcv]
            in_specs += [pl.BlockSpec((None, None, rws, cls), lambda ch, cr: (ch, cr[0], 0, 0)), slab]
            out_shapes.append(_sds(rcv.shape, BF16))
            out_specs.append(slab)

        def fn(*blocks):
            return tuple(blocks[2 * k].astype(F32) + blocks[2 * k + 1].astype(F32) for k in range(len(gs)))

        return _rowwise("grad_pair_sum", fn, operands, grid=(N_CHIP,), in_specs=in_specs, out_shapes=out_shapes, out_specs=out_specs,
                        grid_spec_prefetch=cidx)

    def sibling_comm():
        return _sibling_exchange([g for _, _, g in pending]) if pending else None

    def absorb(from_sibling):
        if pending:
            parts = add_pairs([g for _, _, g in pending], list(from_sibling))
            summed.extend((nm, l_, p) for (nm, l_, _), p in zip(pending, parts))
            pending.clear()

    def chip_comm():
        if pending:
            absorb(_comm_call("grad_sibling_exchange", sibling_comm()))
        comm, names = _chip_exchange([p for _, _, p in summed], [(nm, l_) for nm, l_, _ in summed], layers, stacked)
        summed.clear()
        return comm, names

    def rows128(a, rows):
        flat = a.reshape(-1, 128)
        return jnp.pad(flat, ((0, rows - flat.shape[0]), (0, 0)))

    def pad_to(n, mult):
        return -(-n // mult) * mult

    def packed(arrs, sizes):
        return jnp.concatenate([rows128(a, sz) for a, sz in zip(arrs, sizes)], axis=0)

    n_wsp, n_bsp, n_ln = sgu_w_spatial.size // 128, pad_to(sgu_b_spatial.size // 128, 8), pad_to(n_sgu * E // 128, 8)
    early_sizes = [n_wsp, pad_to(n_wsp + n_bsp, SMALL_ROWS) - n_wsp, n_ln, n_ln]
    early_rep = early_sizes[0] + early_sizes[1]
    gathered_early = None

    for i in reversed(range(depth)):
        l = i // 2
        xs_i, xm, mix_saved, h2, r, s = saved[i]
        comm = sibling_comm()
        if i == 0:
            comm = _gather_level2([early_half]) if comm is None else _merge_comm(comm, _gather_level2([early_half]))
        da, *rcv = back_rows("ffn_down_bwd", dyb, g_down[i], ffc, [r], lambda acc, rr: (acc * (2.0 * rr.astype(F32)),), comm=comm)
        if i == 0:
            *rcv, gathered_early = rcv
        absorb(rcv)
        pending.append(("down", i, wgrad_rows("ffn_down_wgrad", s, dyb, ffc, D, tb).reshape(N_DEV, ffc, D)))
        pending.append(("up", i, wgrad_cols("ffn_up_wgrad", h2, da, ffc)))
        dx, dyb, d_norm_ffn[i], *rcv = back_cols("ffn_up_bwd", da, transposed(g_up[i]), xm, norm_ffn[i], dx, comm=sibling_comm())
        absorb(rcv)
        if i % 2 == 0:
            h, lat, cq, ckv, q, kk, vv, o, lse = mix_saved
            w_dkv, w_uq, w_ukv, w_o = mla_w[l]
            do = _matmul("mla_out_bwd", dyb, w_o, [], grid=(nt,), a_spec=_row_spec(tm, D),
                         b_spec=pl.BlockSpec((None, OW, D), lambda i_: (0, 0, 0)), extra_specs=[],
                         out_shapes=[_sds((T, OW), BF16)], out_specs=[_row_spec(tm, OW)], dims=NT)[0]
            g_o_l = wgrad_rows("mla_out_wgrad", o, dyb, OW, D, tm).reshape(N_DEV, owc, D)
            comm, names = chip_comm()
            dq_pre, dk, dv, *bufs = _flash_bwd(q, kk, vv, o, do, lse, (t_cc, t_sa, t_sb), tq, comm=comm)
            stacked.update(dict(zip(names, bufs)))
            pending.append(("o", l, g_o_l))

            def kv_pre(dkb, dvb, cc, sa, sb):
                parts, dkr = [], None
                for b in range(HEADS):
                    parts += [dkb[:, b * QPAD:b * QPAD + NOPE], dvb[:, b * VDIM:(b + 1) * VDIM]]
                    piece = dkb[:, b * QPAD + NOPE:(b + 1) * QPAD].astype(F32)
                    dkr = piece if dkr is None else dkr + piece
                return jnp.concatenate(parts, axis=1), _rope_bwd(dkr, cc, sa, sb)

            dkv, dkr = _rowwise("mla_dkv_rope", kv_pre, [dk, dv, t_cc, t_sa, t_sb], grid=(T // ts,),
                                in_specs=[_row_spec(ts, HW), _row_spec(ts, OW)] + [_row_spec(ts, 128)] * 3,
                                out_shapes=[_sds((T, HW), BF16), _sds((T, 128), F32)], out_specs=[_row_spec(ts, HW), _row_spec(ts, 128)])
            g_uq_l = wgrad_rows("mla_q_wgrad", cq, dq_pre, Q_RANK, HW, tm)
            g_ukv_l = wgrad_rows("mla_kv_wgrad", ckv, dkv, KV_RANK, HW, tm)
            pending.append(("uq", l, g_uq_l.reshape(Q_RANK, HEADS, QPAD)[:, :, :NOPE + ROPE].transpose(1, 0, 2)))
            pending.append(("ukv", l, g_ukv_l.reshape(KV_RANK, HEADS, NOPE + VDIM).transpose(1, 0, 2)))
            dcq = _matmul("mla_q_bwd", dq_pre, w_uq, [], grid=(nt,), a_spec=_row_spec(tm, HW),
                          b_spec=pl.BlockSpec((None, Q_RANK, HW), lambda i_: (0, 0, 0)), extra_specs=[],
                          out_shapes=[_sds((T, Q_RANK), F32)], out_specs=[_row_spec(tm, Q_RANK)], dims=NT)[0]
            def latent_bwd(dkv_, la, qn, kvn, dq_, dkr_):
                dcq_raw, dqn = _rms_bwd(la[:, :Q_RANK], qn, dq_)
                dckv_raw, dkvn = _rms_bwd(la[:, Q_RANK:Q_RANK + KV_RANK], kvn, dkv_)
                return jnp.concatenate([dcq_raw, dckv_raw, dkr_], axis=1), dqn, dkvn

            dlat, d_qn[l], d_kvn[l] = _matmul(
                "mla_kv_bwd", dkv, w_ukv, [lat, mla_q_norm[l].reshape(1, Q_RANK), mla_kv_norm[l].reshape(1, KV_RANK), dcq, dkr],
                grid=(nt,), a_spec=_row_spec(tm, HW), b_spec=pl.BlockSpec((None, KV_RANK, HW), lambda i_: (0, 0, 0)),
                extra_specs=[_row_spec(tm, LAT_PAD), _const_spec((1, Q_RANK)), _const_spec((1, KV_RANK)), _row_spec(tm, Q_RANK), _row_spec(tm, 128)],
                out_shapes=[_sds((T, LAT_PAD), BF16), _sds((1, Q_RANK), F32), _sds((1, KV_RANK), F32)],
                out_specs=[_row_spec(tm, LAT_PAD), _const_spec((1, Q_RANK)), _const_spec((1, KV_RANK))], dims=NT,
                epilogue=latent_bwd, n_sum=2)
            g_dkv_l = wgrad_rows("mla_down_wgrad", h, dlat, D, LAT_PAD, tm)
            pending.append(("dkv", l, g_dkv_l[:, :LAT].reshape(N_DEV, dc, LAT)))
            dx, dyb, d_norm_mix[i] = _matmul(
                "mla_down_bwd", dlat, w_dkv, [xs_i, norm_mix[i].reshape(1, D), dx], grid=(nt,), a_spec=_row_spec(tm, LAT_PAD),
                b_spec=pl.BlockSpec((None, D, LAT_PAD), lambda i_: (0, 0, 0)), extra_specs=[_row_spec(tm, D), _const_spec((1, D)), _row_spec(tm, D)],
                out_shapes=[_sds((T, D), F32), _sds((T, D), BF16), _sds((1, D), F32)],
                out_specs=[_row_spec(tm, D), _row_spec(tm, D), _const_spec((1, D))], dims=NT, epilogue=norm_bwd_epilogue, n_sum=1)
        else:
            h, gp, ge, gate = mix_saved
            (dgate,) = back_rows("sgu_out_bwd", dyb, g_out[l], ec, [], None)
            pending.append(("out", l, wgrad_rows("sgu_out_wgrad", gate, dyb, ec, D, tb).reshape(N_DEV, ec, D)))
            dz, d_wsp[l], d_bsp[l], d_lng[l], d_lnb[l] = _sgu_mid_bwd(ge, gp, dgate, ln_g_full[l], ln_b_full[l], sgu_w_spatial[l], b_sp[l], 2)
            if i == 1:
                early = packed([jnp.stack(d_wsp, 0), jnp.stack(d_bsp, 0), jnp.concatenate(d_lng, 0), jnp.concatenate(d_lnb, 0)], early_sizes)
                g_in_l, early_part = wgrad_cols("sgu_in_wgrad", h, dz, e2c, comm=_gather_level1([early], part=(0, 2)))
            else:
                g_in_l = wgrad_cols("sgu_in_wgrad", h, dz, e2c)
            pending.append(("in", l, g_in_l))
            comm = sibling_comm()
            if i == 1:
                comm = _merge_comm(comm, _gather_level1([early], part=(1, 2), into=[early_part]))
            dx, dyb, d_norm_mix[i], *rcv = back_cols("sgu_in_bwd", dz, transposed(g_in[l]), xs_i, norm_mix[i], dx, comm=comm)
            if i == 1:
                *rcv, early_half = rcv
            absorb(rcv)
    grad_x = dx.reshape(1, T, D)

    last_comm, last_names = chip_comm()
    late_g = [jnp.concatenate(d_norm_mix, 0), jnp.concatenate(d_norm_ffn, 0), d_final, jnp.concatenate(d_qn, 0), jnp.concatenate(d_kvn, 0)]
    late_w = [norm_mix, norm_ffn, final_norm, mla_q_norm, mla_kv_norm]
    late_m = [m_norm_mix, m_norm_ffn, m_final_norm, m_mla_q_norm, m_mla_kv_norm]
    late_v = [v_norm_mix, v_norm_ffn, v_final_norm, v_mla_q_norm, v_mla_kv_norm]
    late_sizes = [pad_to(g.size // 128, 8) for g in late_g]
    late_rows = sum(late_sizes)

    def adam_big(parts, w, m, v):
        lyr, rws, cls = w.shape
        rt = _tile(rws, 512)

        def fn(p, w_, m_, v_):
            g = (p[0].astype(F32) + p[1].astype(F32)) + (p[2].astype(F32) + p[3].astype(F32))
            return (g, *_adam(w_, g, m_, v_))

        spec = pl.BlockSpec((None, rt, cls), lambda l_, i_: (l_, i_, 0))
        return _rowwise("adam_large", fn, [parts, w, m, v], grid=(lyr, rws // rt),
                        in_specs=[pl.BlockSpec((N_CHIP, None, rt, cls), lambda l_, i_: (0, l_, i_, 0)), spec, spec, spec],
                        out_shapes=[_sds(w.shape, F32)] * 4, out_specs=[spec] * 4)

    stacked.update(dict(zip(last_names, _comm_call("grad_chip_exchange", last_comm))))
    (gathered_late,) = _all_gather("gather_small_grads", [packed(late_g, late_sizes)])
    big = {}
    big["in"] = adam_big(stacked["in"], sgu_w_in, m_sgu_w_in, v_sgu_w_in)
    big["up"] = adam_big(stacked["up"], ffn_w_up, m_ffn_w_up, v_ffn_w_up)
    big["down"] = adam_big(stacked["down"], ffn_w_down, m_ffn_w_down, v_ffn_w_down)
    big["out"] = adam_big(stacked["out"], sgu_w_out, m_sgu_w_out, v_sgu_w_out)
    big["dkv"] = adam_big(stacked["dkv"], mla_w_dkv, m_mla_w_dkv, v_mla_w_dkv)
    big["uq"] = adam_big(stacked["uq"], mla_w_uq, m_mla_w_uq, v_mla_w_uq)
    big["ukv"] = adam_big(stacked["ukv"], mla_w_ukv, m_mla_w_ukv, v_mla_w_ukv)
    big["o"] = adam_big(stacked["o"], mla_w_o, m_mla_w_o, v_mla_w_o)
    big_res = [big[nm][:4] for nm in ("dkv", "uq", "ukv", "o", "in", "out", "up", "down")]

    def sum8(p):
        return ((p[0] + p[1]) + (p[2] + p[3])) + ((p[4] + p[5]) + (p[6] + p[7]))

    def adam_packed(name, gathered, ws, ms, vs, sizes, rows, tile):
        spec = _row_spec(tile, 128)
        return _rowwise(name, lambda p, w_, m_, v_: (sum8(p), *_adam(w_, sum8(p), m_, v_)),
                        [gathered, packed(ws, sizes), packed(ms, sizes), packed(vs, sizes)], grid=(rows // tile,),
                        in_specs=[pl.BlockSpec((N_DEV, tile, 128), lambda i_: (0, i_, 0)), spec, spec, spec],
                        out_shapes=[_sds((rows, 128), F32)] * 4, out_specs=[spec] * 4)

    late_res = adam_packed("adam_small", gathered_late, late_w, late_m, late_v, late_sizes, late_rows, late_rows)
    early_res = adam_packed("adam_spatial", gathered_early, [sgu_w_spatial, sgu_b_spatial], [m_sgu_w_spatial, m_sgu_b_spatial],
                            [v_sgu_w_spatial, v_sgu_b_spatial], early_sizes[:2], early_rep, SMALL_ROWS)

    def unpack(res, sizes, k, like):
        off = sum(sizes[:k])
        return res[off:off + like.size // 128].reshape(like.shape)

    my_b = 4 * lax.axis_index("x") + 2 * lax.axis_index("y") + lax.axis_index("c")
    ln_w = jnp.concatenate([sgu_ln_g, sgu_ln_b], 0)
    ln_m = jnp.concatenate([m_sgu_ln_g, m_sgu_ln_b], 0)
    ln_v = jnp.concatenate([v_sgu_ln_g, v_sgu_ln_b], 0)
    ln_all = jnp.concatenate([gathered_early[:, early_rep:early_rep + n_sgu * E // 128], gathered_early[:, early_rep + n_ln:early_rep + n_ln + n_sgu * E // 128]], axis=1)
    ln_mine = lax.dynamic_slice_in_dim(ln_all.reshape(N_DEV, 2 * n_sgu, N_DEV, ec), my_b, 1, axis=2).reshape(N_DEV, 2 * n_sgu, ec)
    ln_g_, ln_d, ln_m2, ln_v2 = _rowwise(
        "adam_ln", lambda p, w_, m_, v_: (sum8(p), *_adam(w_, sum8(p), m_, v_)), [ln_mine, ln_w, ln_m, ln_v], grid=(1,),
        in_specs=[_const_spec(ln_mine.shape), _const_spec(ln_w.shape), _const_spec(ln_w.shape), _const_spec(ln_w.shape)],
        out_shapes=[_sds(ln_w.shape, F32)] * 4, out_specs=[_const_spec(ln_w.shape)] * 4)

    def family(pos):
        ln = [ln_g_, ln_d, ln_m2, ln_v2][pos]
        late = [unpack(late_res[pos], late_sizes, k, w_) for k, w_ in enumerate(late_w)]
        w_sp_, b_sp_ = unpack(early_res[pos], early_sizes, 0, sgu_w_spatial), unpack(early_res[pos], early_sizes, 1, sgu_b_spatial)
        bigs = [res[pos] for res in big_res]
        return [late[0], late[1], late[2], bigs[0], late[3], late[4], bigs[1], bigs[2], bigs[3],
                bigs[4], ln[:n_sgu], ln[n_sgu:], w_sp_, b_sp_, bigs[5], bigs[6], bigs[7]]

    return (loss, grad_x, *family(0), *family(1), *family(2), *family(3))
```
